```python
import math
import jax, jax.numpy as jnp
from jax import lax
import numpy as np

D_MODEL = 1024
BATCH = 8
SEQ = 4096
DEPTH = 4

HEAD_DIM = 64
SWA_Q_HEADS = 8
SWA_KV_HEADS = 2
SWA_GROUP = SWA_Q_HEADS // SWA_KV_HEADS
WINDOW = 128
FOX_HEADS = 4
MLA_HEADS = 4
MLA_Q_RANK = 256
MLA_KV_RANK = 128
MLA_NOPE_DIM = 64
MLA_ROPE_DIM = 32
MLA_V_DIM = 64
ROPE_THETA = 10000.0
REL_BUCKETS = 32
REL_MAX_DIST = 128
D_FF = 2816
CONV_WIDTH = 3
Q_BLOCK = 128
EPS = 1e-6
NEG_INF = -1e30

SWA_WIDTH = SWA_Q_HEADS * HEAD_DIM
FOX_WIDTH = FOX_HEADS * HEAD_DIM
MLA_WIDTH = MLA_HEADS * MLA_V_DIM
MIX_WIDTH = SWA_WIDTH + FOX_WIDTH + MLA_WIDTH
SWA_COLS = (SWA_Q_HEADS + 2 * SWA_KV_HEADS) * HEAD_DIM
FOX_COLS = 3 * FOX_HEADS * HEAD_DIM + FOX_HEADS
MLA_COLS = MLA_Q_RANK + MLA_KV_RANK + MLA_ROPE_DIM
IN_COLS = SWA_COLS + FOX_COLS + MLA_COLS
MLA_QK_DIM = MLA_NOPE_DIM + MLA_ROPE_DIM

kernel_name = "hymba_swa_fox_mla_convffn_trunk"


def rmsnorm(x, g):
    xf = x.astype(jnp.float32)
    y = xf * lax.rsqrt(jnp.mean(xf * xf, axis=-1, keepdims=True) + EPS) * g.astype(jnp.float32)
    return y.astype(x.dtype)


def t5_causal_bucket(dist):
    max_exact = REL_BUCKETS // 2
    d = jnp.maximum(dist, 0)
    log_ratio = jnp.log(jnp.maximum(d, 1).astype(jnp.float32) / max_exact) / math.log(REL_MAX_DIST / max_exact)
    large = max_exact + (log_ratio * (REL_BUCKETS - max_exact)).astype(jnp.int32)
    large = jnp.minimum(large, REL_BUCKETS - 1)
    return jnp.where(d < max_exact, d, large)


def apply_rope(t, cos, sin):
    t1, t2 = jnp.split(t, 2, axis=-1)
    return jnp.concatenate([t1 * cos - t2 * sin, t1 * sin + t2 * cos], axis=-1)


def swa_sink_attention(q, k, v, sinks, rel_bias):
    B, S = q.shape[0], q.shape[1]
    nb = S // WINDOW
    qb = q.reshape(B, nb, WINDOW, SWA_KV_HEADS, SWA_GROUP, HEAD_DIM)

    def band(t):
        tb = t.reshape(B, nb, WINDOW, SWA_KV_HEADS, HEAD_DIM)
        prev = jnp.pad(tb, ((0, 0), (1, 0), (0, 0), (0, 0), (0, 0)))[:, :-1]
        return jnp.concatenate([prev, tb], axis=2)

    kb, vb = band(k), band(v)
    qi = jnp.arange(WINDOW, dtype=jnp.int32)[:, None] + WINDOW
    kj = jnp.arange(2 * WINDOW, dtype=jnp.int32)[None, :]
    dist = qi - kj
    in_band = (dist >= 0) & (dist < WINDOW)
    valid_key = (jnp.arange(nb)[:, None, None] > 0) | (kj >= WINDOW)[None]
    mask = in_band[None] & valid_key
    bias = rel_bias.astype(jnp.float32)[t5_causal_bucket(dist)]
    bias = bias.transpose(2, 0, 1).reshape(SWA_KV_HEADS, SWA_GROUP, WINDOW, 2 * WINDOW)
    s = jnp.einsum('bnqhgd,bnkhd->bnhgqk', qb, kb, preferred_element_type=jnp.float32)
    s = s * (HEAD_DIM ** -0.5) + bias
    s = jnp.where(mask[None, :, None, None], s, NEG_INF)
    sink = sinks.astype(jnp.float32).reshape(1, 1, SWA_KV_HEADS, SWA_GROUP, 1, 1)
    sink = jnp.broadcast_to(sink, s.shape[:-1] + (1,))
    p = jax.nn.softmax(jnp.concatenate([s, sink], axis=-1), axis=-1)[..., :-1]
    o = jnp.einsum('bnhgqk,bnkhd->bnqhgd', p.astype(v.dtype), vb)
    return o.reshape(B, S, SWA_Q_HEADS * HEAD_DIM)


def blocked_causal_attention(q, k, v, scale, log_forget_cum=None):
    B, S, H = q.shape[0], q.shape[1], q.shape[2]
    nb = S // Q_BLOCK
    q_blocks = q.reshape(B, nb, Q_BLOCK, H, q.shape[-1]).swapaxes(0, 1)
    k_pos = jnp.arange(S, dtype=jnp.int32)
    if log_forget_cum is None:
        f_blocks, f_keys = None, None
    else:
        f_t = log_forget_cum.transpose(0, 2, 1)
        f_blocks = f_t.reshape(B, H, nb, Q_BLOCK).transpose(2, 0, 1, 3)
        f_keys = f_t

    def one_block(args):
        qb, fb, i = args
        s = jnp.einsum('bqhd,bkhd->bhqk', qb, k, preferred_element_type=jnp.float32) * scale
        if fb is not None:
            s = s + (fb[..., :, None] - f_keys[..., None, :])
        q_pos = i * Q_BLOCK + jnp.arange(Q_BLOCK, dtype=jnp.int32)
        s = jnp.where(k_pos[None, :] <= q_pos[:, None], s, NEG_INF)
        p = jax.nn.softmax(s, axis=-1)
        return jnp.einsum('bhqk,bkhd->bqhd', p.astype(v.dtype), v)

    out = lax.map(one_block, (q_blocks, f_blocks, jnp.arange(nb, dtype=jnp.int32)))
    return out.swapaxes(0, 1).reshape(B, S, H * v.shape[-1])


def causal_depthwise_conv(u, w, b):
    S = u.shape[1]
    up = jnp.pad(u, ((0, 0), (CONV_WIDTH - 1, 0), (0, 0)))
    y = b.astype(u.dtype)
    for tap in range(CONV_WIDTH):
        y = y + w[tap].astype(u.dtype) * up[:, tap:tap + S]
    return y


def _fwd_setup_inputs(seed: int = 0) -> dict:
    key = jax.random.key(seed)
    ks = jax.random.split(key, 20)
    L, D = DEPTH, D_MODEL
    f32 = jnp.float32

    def nrm(k, shape, scale):
        return jax.random.normal(k, shape, f32) * scale

    def gain(k, shape):
        return 1.0 + 0.05 * jax.random.normal(k, shape, f32)

    return {
        "x": jax.random.normal(ks[0], (BATCH, SEQ, D), f32),
        "attn_pre_norm": gain(ks[1], (L, D)),
        "w_in": nrm(ks[2], (L, D, IN_COLS), D ** -0.5),
        "forget_bias": 2.0 + 0.5 * jax.random.normal(ks[3], (L, FOX_HEADS), f32),
        "swa_sinks": nrm(ks[4], (L, SWA_Q_HEADS), 0.5),
        "rel_bias": nrm(ks[5], (REL_BUCKETS, SWA_Q_HEADS), 0.5),
        "q_latent_norm": gain(ks[6], (L, MLA_Q_RANK)),
        "w_uq": nrm(ks[7], (L, MLA_Q_RANK, MLA_HEADS * MLA_QK_DIM), MLA_Q_RANK ** -0.5),
        "kv_latent_norm": gain(ks[8], (L, MLA_KV_RANK)),
        "w_ukv": nrm(ks[9], (L, MLA_KV_RANK, MLA_HEADS * (MLA_NOPE_DIM + MLA_V_DIM)), MLA_KV_RANK ** -0.5),
        "group_norm": gain(ks[10], (L, MIX_WIDTH)),
        "w_out": nrm(ks[11], (L, MIX_WIDTH, D), MIX_WIDTH ** -0.5),
        "attn_post_norm": gain(ks[12], (L, D)),
        "ffn_pre_norm": gain(ks[13], (L, D)),
        "w_up": nrm(ks[14], (L, D, 2 * D_FF), D ** -0.5),
        "conv_w": nrm(ks[15], (L, CONV_WIDTH, 2 * D_FF), CONV_WIDTH ** -0.5),
        "conv_b": nrm(ks[16], (L, 2 * D_FF), 0.02),
        "w_down": nrm(ks[17], (L, D_FF, D), D_FF ** -0.5),
        "ffn_post_norm": gain(ks[18], (L, D)),
    }


def _fwd_reference(x, attn_pre_norm, w_in, forget_bias, swa_sinks, rel_bias, q_latent_norm, w_uq,
              kv_latent_norm, w_ukv, group_norm, w_out, attn_post_norm, ffn_pre_norm, w_up,
              conv_w, conv_b, w_down, ffn_post_norm):
    B, S, _ = x.shape
    pos = jnp.arange(S, dtype=jnp.float32)
    inv_freq = ROPE_THETA ** (-(jnp.arange(MLA_ROPE_DIM // 2, dtype=jnp.float32) * 2.0 / MLA_ROPE_DIM))
    ang = pos[:, None] * inv_freq[None, :]
    cos = jnp.cos(ang)[:, None, :].astype(x.dtype)
    sin = jnp.sin(ang)[:, None, :].astype(x.dtype)

    for l in range(DEPTH):
        h = rmsnorm(x, attn_pre_norm[l])
        proj = h @ w_in[l]
        a_cols, f_cols, m_cols = jnp.split(proj, [SWA_COLS, SWA_COLS + FOX_COLS], axis=-1)

        qa, ka, va = jnp.split(a_cols, [SWA_WIDTH, SWA_WIDTH + SWA_KV_HEADS * HEAD_DIM], axis=-1)
        out_a = swa_sink_attention(qa.reshape(B, S, SWA_Q_HEADS, HEAD_DIM),
                                   ka.reshape(B, S, SWA_KV_HEADS, HEAD_DIM),
                                   va.reshape(B, S, SWA_KV_HEADS, HEAD_DIM),
                                   swa_sinks[l], rel_bias)

        qf, kf, vf, f_logit = jnp.split(f_cols, [FOX_WIDTH, 2 * FOX_WIDTH, 3 * FOX_WIDTH], axis=-1)
        log_f = jax.nn.log_sigmoid(f_logit.astype(jnp.float32) + forget_bias[l].astype(jnp.float32))
        F = jnp.cumsum(log_f, axis=1)
        out_b = blocked_causal_attention(qf.reshape(B, S, FOX_HEADS, HEAD_DIM),
                                         kf.reshape(B, S, FOX_HEADS, HEAD_DIM),
                                         vf.reshape(B, S, FOX_HEADS, HEAD_DIM),
                                         HEAD_DIM ** -0.5, log_forget_cum=F)

        c_q, c_kv, k_rope = jnp.split(m_cols, [MLA_Q_RANK, MLA_Q_RANK + MLA_KV_RANK], axis=-1)
        qm = (rmsnorm(c_q, q_latent_norm[l]) @ w_uq[l]).reshape(B, S, MLA_HEADS, MLA_QK_DIM)
        q_nope, q_rope = jnp.split(qm, [MLA_NOPE_DIM], axis=-1)
        kv = (rmsnorm(c_kv, kv_latent_norm[l]) @ w_ukv[l]).reshape(B, S, MLA_HEADS, MLA_NOPE_DIM + MLA_V_DIM)
        k_nope, vm = jnp.split(kv, [MLA_NOPE_DIM], axis=-1)
        q_rope = apply_rope(q_rope, cos, sin)
        k_rope = jnp.broadcast_to(apply_rope(k_rope[:, :, None, :], cos, sin), (B, S, MLA_HEADS, MLA_ROPE_DIM))
        out_c = blocked_causal_attention(jnp.concatenate([q_nope, q_rope], axis=-1),
                                         jnp.concatenate([k_nope, k_rope], axis=-1),
                                         vm, MLA_QK_DIM ** -0.5)

        g_a, g_b, g_c = jnp.split(group_norm[l], [SWA_WIDTH, SWA_WIDTH + FOX_WIDTH])
        mixed = jnp.concatenate([rmsnorm(out_a, g_a), rmsnorm(out_b, g_b), rmsnorm(out_c, g_c)], axis=-1)
        x = x + rmsnorm(mixed @ w_out[l], attn_post_norm[l])

        h = rmsnorm(x, ffn_pre_norm[l])
        u = causal_depthwise_conv(h @ w_up[l], conv_w[l], conv_b[l])
        gate, up = jnp.split(u, 2, axis=-1)
        y = (jax.nn.gelu(gate, approximate=True) * up) @ w_down[l]
        x = x + rmsnorm(y, ffn_post_norm[l])
    return x


import jax as _jax
import jax.numpy as _jnp

TWIN_FORMAT = 'train_step'
FWD_PARAMS = ['x', 'attn_pre_norm', 'w_in', 'forget_bias', 'swa_sinks', 'rel_bias', 'q_latent_norm', 'w_uq', 'kv_latent_norm', 'w_ukv', 'group_norm', 'w_out', 'attn_post_norm', 'ffn_pre_norm', 'w_up', 'conv_w', 'conv_b', 'w_down', 'ffn_post_norm']
TWIN_WEIGHTS = ['attn_pre_norm', 'w_in', 'forget_bias', 'swa_sinks', 'rel_bias', 'q_latent_norm', 'w_uq', 'kv_latent_norm', 'w_ukv', 'group_norm', 'w_out', 'attn_post_norm', 'ffn_pre_norm', 'w_up', 'conv_w', 'conv_b', 'w_down', 'ffn_post_norm']
TWIN_DIFF_INPUT = 'x'
TWIN_INPUTS = ['x', 'attn_pre_norm', 'w_in', 'forget_bias', 'swa_sinks', 'rel_bias', 'q_latent_norm', 'w_uq', 'kv_latent_norm', 'w_ukv', 'group_norm', 'w_out', 'attn_post_norm', 'ffn_pre_norm', 'w_up', 'conv_w', 'conv_b', 'w_down', 'ffn_post_norm', 'loss_target', 'm_attn_pre_norm', 'm_w_in', 'm_forget_bias', 'm_swa_sinks', 'm_rel_bias', 'm_q_latent_norm', 'm_w_uq', 'm_kv_latent_norm', 'm_w_ukv', 'm_group_norm', 'm_w_out', 'm_attn_post_norm', 'm_ffn_pre_norm', 'm_w_up', 'm_conv_w', 'm_conv_b', 'm_w_down', 'm_ffn_post_norm', 'v_attn_pre_norm', 'v_w_in', 'v_forget_bias', 'v_swa_sinks', 'v_rel_bias', 'v_q_latent_norm', 'v_w_uq', 'v_kv_latent_norm', 'v_w_ukv', 'v_group_norm', 'v_w_out', 'v_attn_post_norm', 'v_ffn_pre_norm', 'v_w_up', 'v_conv_w', 'v_conv_b', 'v_w_down', 'v_ffn_post_norm']
TWIN_OUTPUTS = ['loss', 'grad_x', 'grad_attn_pre_norm', 'grad_w_in', 'grad_forget_bias', 'grad_swa_sinks', 'grad_rel_bias', 'grad_q_latent_norm', 'grad_w_uq', 'grad_kv_latent_norm', 'grad_w_ukv', 'grad_group_norm', 'grad_w_out', 'grad_attn_post_norm', 'grad_ffn_pre_norm', 'grad_w_up', 'grad_conv_w', 'grad_conv_b', 'grad_w_down', 'grad_ffn_post_norm', 'delta_attn_pre_norm', 'delta_w_in', 'delta_forget_bias', 'delta_swa_sinks', 'delta_rel_bias', 'delta_q_latent_norm', 'delta_w_uq', 'delta_kv_latent_norm', 'delta_w_ukv', 'delta_group_norm', 'delta_w_out', 'delta_attn_post_norm', 'delta_ffn_pre_norm', 'delta_w_up', 'delta_conv_w', 'delta_conv_b', 'delta_w_down', 'delta_ffn_post_norm', 'new_m_attn_pre_norm', 'new_m_w_in', 'new_m_forget_bias', 'new_m_swa_sinks', 'new_m_rel_bias', 'new_m_q_latent_norm', 'new_m_w_uq', 'new_m_kv_latent_norm', 'new_m_w_ukv', 'new_m_group_norm', 'new_m_w_out', 'new_m_attn_post_norm', 'new_m_ffn_pre_norm', 'new_m_w_up', 'new_m_conv_w', 'new_m_conv_b', 'new_m_w_down', 'new_m_ffn_post_norm', 'new_v_attn_pre_norm', 'new_v_w_in', 'new_v_forget_bias', 'new_v_swa_sinks', 'new_v_rel_bias', 'new_v_q_latent_norm', 'new_v_w_uq', 'new_v_kv_latent_norm', 'new_v_w_ukv', 'new_v_group_norm', 'new_v_w_out', 'new_v_attn_post_norm', 'new_v_ffn_pre_norm', 'new_v_w_up', 'new_v_conv_w', 'new_v_conv_b', 'new_v_w_down', 'new_v_ffn_post_norm']
TWIN_LEAF_KINDS = {'loss': 'loss', 'grad_x': 'grad_x', 'grad_attn_pre_norm': 'grad_w', 'grad_w_in': 'grad_w', 'grad_forget_bias': 'grad_w', 'grad_swa_sinks': 'grad_w', 'grad_rel_bias': 'grad_w', 'grad_q_latent_norm': 'grad_w', 'grad_w_uq': 'grad_w', 'grad_kv_latent_norm': 'grad_w', 'grad_w_ukv': 'grad_w', 'grad_group_norm': 'grad_w', 'grad_w_out': 'grad_w', 'grad_attn_post_norm': 'grad_w', 'grad_ffn_pre_norm': 'grad_w', 'grad_w_up': 'grad_w', 'grad_conv_w': 'grad_w', 'grad_conv_b': 'grad_w', 'grad_w_down': 'grad_w', 'grad_ffn_post_norm': 'grad_w', 'delta_attn_pre_norm': 'delta_w', 'delta_w_in': 'delta_w', 'delta_forget_bias': 'delta_w', 'delta_swa_sinks': 'delta_w', 'delta_rel_bias': 'delta_w', 'delta_q_latent_norm': 'delta_w', 'delta_w_uq': 'delta_w', 'delta_kv_latent_norm': 'delta_w', 'delta_w_ukv': 'delta_w', 'delta_group_norm': 'delta_w', 'delta_w_out': 'delta_w', 'delta_attn_post_norm': 'delta_w', 'delta_ffn_pre_norm': 'delta_w', 'delta_w_up': 'delta_w', 'delta_conv_w': 'delta_w', 'delta_conv_b': 'delta_w', 'delta_w_down': 'delta_w', 'delta_ffn_post_norm': 'delta_w', 'new_m_attn_pre_norm': 'new_m', 'new_m_w_in': 'new_m', 'new_m_forget_bias': 'new_m', 'new_m_swa_sinks': 'new_m', 'new_m_rel_bias': 'new_m', 'new_m_q_latent_norm': 'new_m', 'new_m_w_uq': 'new_m', 'new_m_kv_latent_norm': 'new_m', 'new_m_w_ukv': 'new_m', 'new_m_group_norm': 'new_m', 'new_m_w_out': 'new_m', 'new_m_attn_post_norm': 'new_m', 'new_m_ffn_pre_norm': 'new_m', 'new_m_w_up': 'new_m', 'new_m_conv_w': 'new_m', 'new_m_conv_b': 'new_m', 'new_m_w_down': 'new_m', 'new_m_ffn_post_norm': 'new_m', 'new_v_attn_pre_norm': 'new_v', 'new_v_w_in': 'new_v', 'new_v_forget_bias': 'new_v', 'new_v_swa_sinks': 'new_v', 'new_v_rel_bias': 'new_v', 'new_v_q_latent_norm': 'new_v', 'new_v_w_uq': 'new_v', 'new_v_kv_latent_norm': 'new_v', 'new_v_w_ukv': 'new_v', 'new_v_group_norm': 'new_v', 'new_v_w_out': 'new_v', 'new_v_attn_post_norm': 'new_v', 'new_v_ffn_pre_norm': 'new_v', 'new_v_w_up': 'new_v', 'new_v_conv_w': 'new_v', 'new_v_conv_b': 'new_v', 'new_v_w_down': 'new_v', 'new_v_ffn_post_norm': 'new_v'}


def _forward(args):
    return _fwd_reference(*[args[k] for k in FWD_PARAMS])


def _output_shape():
    out = _jax.eval_shape(lambda: _forward(_fwd_setup_inputs(0)))
    return out.shape, out.dtype

N_MICROBATCH = 1
ADAM_LR = 0.001
ADAM_B1 = 0.9
ADAM_B2 = 0.999
ADAM_EPS = 1e-08
ADAM_WD = 0.01
ADAM_STEP = 10
PER_EXAMPLE_BATCH_AXIS = {'x': 0, 'loss_target': 0}
SHARED_INPUTS = []
_WEIGHT_DTYPES = {'attn_pre_norm': _jnp.float32, 'w_in': _jnp.float32, 'forget_bias': _jnp.float32, 'swa_sinks': _jnp.float32, 'rel_bias': _jnp.float32, 'q_latent_norm': _jnp.float32, 'w_uq': _jnp.float32, 'kv_latent_norm': _jnp.float32, 'w_ukv': _jnp.float32, 'group_norm': _jnp.float32, 'w_out': _jnp.float32, 'attn_post_norm': _jnp.float32, 'ffn_pre_norm': _jnp.float32, 'w_up': _jnp.float32, 'conv_w': _jnp.float32, 'conv_b': _jnp.float32, 'w_down': _jnp.float32, 'ffn_post_norm': _jnp.float32}
MOMENT_SCALE = {'attn_pre_norm': 1.095540e+01, 'w_in': 7.556884e+00, 'forget_bias': 6.177516e+00, 'swa_sinks': 1.407223e+00, 'rel_bias': 3.092591e+00, 'q_latent_norm': 1.553896e+00, 'w_uq': 1.452902e+00, 'kv_latent_norm': 2.225485e+01, 'w_ukv': 1.137336e+01, 'group_norm': 1.103480e+01, 'w_out': 1.065297e+01, 'attn_post_norm': 3.076471e+01, 'ffn_pre_norm': 4.032317e+00, 'w_up': 1.653127e+00, 'conv_w': 1.966603e+00, 'conv_b': 8.482903e+00, 'w_down': 3.394440e+00, 'ffn_post_norm': 3.179719e+01}


def _to_microbatches(a, axis):
    t = _jnp.moveaxis(a, axis, 0)
    t = t.reshape((N_MICROBATCH, t.shape[0] // N_MICROBATCH) + t.shape[1:])
    return _jnp.moveaxis(t, 1, axis + 1)


def setup_inputs(seed: int = 0) -> dict:
    inp = _fwd_setup_inputs(seed)
    key = _jax.random.fold_in(_jax.random.key(seed), 7919)
    shape, _ = _output_shape()
    out = dict(inp)
    out["loss_target"] = _jax.random.normal(_jax.random.fold_in(key, 0), shape, _jnp.float32)
    for i, name in enumerate(TWIN_WEIGHTS):
        w = inp[name].astype(_jnp.float32)
        if MOMENT_SCALE is None:
            s = _jnp.sqrt(_jnp.mean(_jnp.square(w)) + 1e-30)
        else:
            s = MOMENT_SCALE[name]
        km, kv = _jax.random.split(_jax.random.fold_in(key, i + 1))
        out[name] = w
        out["m_" + name] = s * _jax.random.normal(km, w.shape, _jnp.float32)
        out["v_" + name] = (s * s) * _jax.random.uniform(kv, w.shape, _jnp.float32, 0.5, 1.5)
    if N_MICROBATCH > 1:
        for name, axis in PER_EXAMPLE_BATCH_AXIS.items():
            out[name] = _to_microbatches(out[name], axis)
    return {'x': out['x'], 'attn_pre_norm': out['attn_pre_norm'], 'w_in': out['w_in'], 'forget_bias': out['forget_bias'], 'swa_sinks': out['swa_sinks'], 'rel_bias': out['rel_bias'], 'q_latent_norm': out['q_latent_norm'], 'w_uq': out['w_uq'], 'kv_latent_norm': out['kv_latent_norm'], 'w_ukv': out['w_ukv'], 'group_norm': out['group_norm'], 'w_out': out['w_out'], 'attn_post_norm': out['attn_post_norm'], 'ffn_pre_norm': out['ffn_pre_norm'], 'w_up': out['w_up'], 'conv_w': out['conv_w'], 'conv_b': out['conv_b'], 'w_down': out['w_down'], 'ffn_post_norm': out['ffn_post_norm'], 'loss_target': out['loss_target'], 'm_attn_pre_norm': out['m_attn_pre_norm'], 'm_w_in': out['m_w_in'], 'm_forget_bias': out['m_forget_bias'], 'm_swa_sinks': out['m_swa_sinks'], 'm_rel_bias': out['m_rel_bias'], 'm_q_latent_norm': out['m_q_latent_norm'], 'm_w_uq': out['m_w_uq'], 'm_kv_latent_norm': out['m_kv_latent_norm'], 'm_w_ukv': out['m_w_ukv'], 'm_group_norm': out['m_group_norm'], 'm_w_out': out['m_w_out'], 'm_attn_post_norm': out['m_attn_post_norm'], 'm_ffn_pre_norm': out['m_ffn_pre_norm'], 'm_w_up': out['m_w_up'], 'm_conv_w': out['m_conv_w'], 'm_conv_b': out['m_conv_b'], 'm_w_down': out['m_w_down'], 'm_ffn_post_norm': out['m_ffn_post_norm'], 'v_attn_pre_norm': out['v_attn_pre_norm'], 'v_w_in': out['v_w_in'], 'v_forget_bias': out['v_forget_bias'], 'v_swa_sinks': out['v_swa_sinks'], 'v_rel_bias': out['v_rel_bias'], 'v_q_latent_norm': out['v_q_latent_norm'], 'v_w_uq': out['v_w_uq'], 'v_kv_latent_norm': out['v_kv_latent_norm'], 'v_w_ukv': out['v_w_ukv'], 'v_group_norm': out['v_group_norm'], 'v_w_out': out['v_w_out'], 'v_attn_post_norm': out['v_attn_post_norm'], 'v_ffn_pre_norm': out['v_ffn_pre_norm'], 'v_w_up': out['v_w_up'], 'v_conv_w': out['v_conv_w'], 'v_conv_b': out['v_conv_b'], 'v_w_down': out['v_w_down'], 'v_ffn_post_norm': out['v_ffn_post_norm']}


def _loss(weights, diff, rest, loss_target):
    with _jax.named_scope("forward"):
        args = {**rest, TWIN_DIFF_INPUT: diff, **{k: w.astype(_WEIGHT_DTYPES[k]) for k, w in weights.items()}}
        y = _forward(args)
    with _jax.named_scope("loss_head"):
        err = _jnp.square(y.astype(_jnp.float32) - loss_target)
        return 0.5 * _jnp.sum(_jnp.mean(err, axis=-1)) if err.ndim else 0.5 * err


def _adamw(w, g, m, v):
    m = ADAM_B1 * m + (1.0 - ADAM_B1) * g
    v = ADAM_B2 * v + (1.0 - ADAM_B2) * _jnp.square(g)
    m_hat = m / (1.0 - ADAM_B1 ** ADAM_STEP)
    v_hat = v / (1.0 - ADAM_B2 ** ADAM_STEP)
    delta = -ADAM_LR * (m_hat / (_jnp.sqrt(v_hat) + ADAM_EPS) + ADAM_WD * w)
    return delta, m, v


def reference(x, attn_pre_norm, w_in, forget_bias, swa_sinks, rel_bias, q_latent_norm, w_uq, kv_latent_norm, w_ukv, group_norm, w_out, attn_post_norm, ffn_pre_norm, w_up, conv_w, conv_b, w_down, ffn_post_norm, loss_target, m_attn_pre_norm, m_w_in, m_forget_bias, m_swa_sinks, m_rel_bias, m_q_latent_norm, m_w_uq, m_kv_latent_norm, m_w_ukv, m_group_norm, m_w_out, m_attn_post_norm, m_ffn_pre_norm, m_w_up, m_conv_w, m_conv_b, m_w_down, m_ffn_post_norm, v_attn_pre_norm, v_w_in, v_forget_bias, v_swa_sinks, v_rel_bias, v_q_latent_norm, v_w_uq, v_kv_latent_norm, v_w_ukv, v_group_norm, v_w_out, v_attn_post_norm, v_ffn_pre_norm, v_w_up, v_conv_w, v_conv_b, v_w_down, v_ffn_post_norm):
    given = dict(x=x, attn_pre_norm=attn_pre_norm, w_in=w_in, forget_bias=forget_bias, swa_sinks=swa_sinks, rel_bias=rel_bias, q_latent_norm=q_latent_norm, w_uq=w_uq, kv_latent_norm=kv_latent_norm, w_ukv=w_ukv, group_norm=group_norm, w_out=w_out, attn_post_norm=attn_post_norm, ffn_pre_norm=ffn_pre_norm, w_up=w_up, conv_w=conv_w, conv_b=conv_b, w_down=w_down, ffn_post_norm=ffn_post_norm, loss_target=loss_target, m_attn_pre_norm=m_attn_pre_norm, m_w_in=m_w_in, m_forget_bias=m_forget_bias, m_swa_sinks=m_swa_sinks, m_rel_bias=m_rel_bias, m_q_latent_norm=m_q_latent_norm, m_w_uq=m_w_uq, m_kv_latent_norm=m_kv_latent_norm, m_w_ukv=m_w_ukv, m_group_norm=m_group_norm, m_w_out=m_w_out, m_attn_post_norm=m_attn_post_norm, m_ffn_pre_norm=m_ffn_pre_norm, m_w_up=m_w_up, m_conv_w=m_conv_w, m_conv_b=m_conv_b, m_w_down=m_w_down, m_ffn_post_norm=m_ffn_post_norm, v_attn_pre_norm=v_attn_pre_norm, v_w_in=v_w_in, v_forget_bias=v_forget_bias, v_swa_sinks=v_swa_sinks, v_rel_bias=v_rel_bias, v_q_latent_norm=v_q_latent_norm, v_w_uq=v_w_uq, v_kv_latent_norm=v_kv_latent_norm, v_w_ukv=v_w_ukv, v_group_norm=v_group_norm, v_w_out=v_w_out, v_attn_post_norm=v_attn_post_norm, v_ffn_pre_norm=v_ffn_pre_norm, v_w_up=v_w_up, v_conv_w=v_conv_w, v_conv_b=v_conv_b, v_w_down=v_w_down, v_ffn_post_norm=v_ffn_post_norm)
    weights = {n: given[n] for n in TWIN_WEIGHTS}
    shared = {n: given[n] for n in SHARED_INPUTS}
    per_example = {n: given[n] for n in ['x']}
    grad_fn = _jax.value_and_grad(_loss, argnums=(0, 1))

    def one_microbatch(ex, loss_target):
        ex = dict(ex)
        diff = ex.pop(TWIN_DIFF_INPUT)
        return grad_fn(weights, diff, {**shared, **ex}, loss_target)

    if N_MICROBATCH == 1:
        loss, (grad_w, grad_x) = one_microbatch(per_example, given["loss_target"])
    else:
        def body(carry, xs):
            loss_sum, grad_sum = carry
            l_k, (gw_k, gx_k) = one_microbatch(xs[0], xs[1])
            with _jax.named_scope("update"):
                return (loss_sum + l_k, _jax.tree.map(_jnp.add, grad_sum, gw_k)), gx_k

        init = (_jnp.zeros((), _jnp.float32), _jax.tree.map(_jnp.zeros_like, weights))
        (loss, grad_w), grad_x = _jax.lax.scan(body, init, (per_example, given["loss_target"]))
    with _jax.named_scope("update"):
        delta_w, new_m, new_v = {}, {}, {}
        for n in TWIN_WEIGHTS:
            delta_w[n], new_m[n], new_v[n] = _adamw(weights[n], grad_w[n], given["m_" + n], given["v_" + n])
    return (loss, grad_x, *[grad_w[n] for n in TWIN_WEIGHTS], *[delta_w[n] for n in TWIN_WEIGHTS],
            *[new_m[n] for n in TWIN_WEIGHTS], *[new_v[n] for n in TWIN_WEIGHTS])
```

```python
import functools
import math

import numpy as np
import jax
import jax.numpy as jnp
from jax import lax
from jax.experimental import pallas as pl
from jax.experimental.pallas import tpu as pltpu

F32 = jnp.float32
BF16 = jnp.bfloat16

D_MODEL = 1024
DEPTH = 4
HEAD_DIM = 64
SWA_Q_HEADS = 8
SWA_KV_HEADS = 2
SWA_GROUP = SWA_Q_HEADS // SWA_KV_HEADS
WINDOW = 128
FOX_HEADS = 4
MLA_HEADS = 4
MLA_Q_RANK = 256
MLA_KV_RANK = 128
MLA_NOPE = 64
MLA_ROPE = 32
MLA_QK = MLA_NOPE + MLA_ROPE
ROPE_THETA = 10000.0
REL_BUCKETS = 32
REL_MAX_DIST = 128
D_FF = 2816
EPS = 1e-6
NEG_INF = -1e30
IN_COLS = 1956
IN_PAD = 2048
MISC_COL = 1920
KR_LANE = 64
FL_LANE = 96

ADAM_LR = 0.001
ADAM_B1 = 0.9
ADAM_B2 = 0.999
ADAM_EPS = 1e-08
ADAM_WD = 0.01
ADAM_STEP = 10

LANES = 128
VMEM_LIMIT_BYTES = 48 * 1024 * 1024
ATT_TILE = 256
PACK_COLS = 1024
PACK_ROW_TILE = 256
MESH = pl.DeviceIdType.MESH

NT = (((1,), (1,)), ((), ()))
TN = (((0,), (0,)), ((), ()))
NN = (((1,), (0,)), ((), ()))


def _params(*sem):
    return pltpu.CompilerParams(dimension_semantics=sem, vmem_limit_bytes=VMEM_LIMIT_BYTES)


def _tile(dim, cap):
    for t in (2048, 1408, 1024, 512, 256, 128, 64, 32, 16, 8):
        if t <= cap and dim % t == 0:
            return t
    return dim


def _dot(a, b, dims=NN):
    return lax.dot_general(a, b, dims, preferred_element_type=F32)


def _split3(a):
    a1 = a.astype(BF16)
    r1 = a - a1.astype(F32)
    a2 = r1.astype(BF16)
    a3 = (r1 - a2.astype(F32)).astype(BF16)
    return a1, a2, a3


def _matmul(a, b, *, ta=False, tb=False, out_dtype=F32, name):
    if ta:
        K, M = a.shape
    else:
        M, K = a.shape
    if tb:
        N, K2 = b.shape
    else:
        K2, N = b.shape
    assert K == K2, (a.shape, b.shape)
    tm, tn, tk = _tile(M, 1024), _tile(N, 1408), _tile(K, 1408)
    nk = K // tk
    dims = (((0 if ta else 1,), (1 if tb else 0,)), ((), ()))

    def body(a_ref, b_ref, o_ref, acc_ref):
        k = pl.program_id(2)

        @pl.when(k == 0)
        def _():
            acc_ref[...] = jnp.zeros_like(acc_ref)

        acc_ref[...] += lax.dot_general(a_ref[...], b_ref[...], dims, preferred_element_type=F32)

        @pl.when(k == nk - 1)
        def _():
            o_ref[...] = acc_ref[...].astype(o_ref.dtype)

    a_spec = pl.BlockSpec((tk, tm), lambda i, j, k: (k, i)) if ta else pl.BlockSpec((tm, tk), lambda i, j, k: (i, k))
    b_spec = pl.BlockSpec((tn, tk), lambda i, j, k: (j, k)) if tb else pl.BlockSpec((tk, tn), lambda i, j, k: (k, j))
    return pl.pallas_call(
        body, name=name, grid=(M // tm, N // tn, nk),
        in_specs=[a_spec, b_spec],
        out_specs=pl.BlockSpec((tm, tn), lambda i, j, k: (i, j)),
        out_shape=jax.ShapeDtypeStruct((M, N), out_dtype),
        scratch_shapes=[pltpu.VMEM((tm, tn), F32)],
        compiler_params=_params("parallel", "parallel", "arbitrary"),
    )(a, b)


def _seg_rms(xs, g):
    r = lax.rsqrt(jnp.mean(xs * xs, axis=-1, keepdims=True) + EPS)
    return xs * r * g


def _seg_rms_bwd(xs, g, dy):
    r = lax.rsqrt(jnp.mean(xs * xs, axis=-1, keepdims=True) + EPS)
    gd = dy * g
    c = jnp.mean(gd * xs, axis=-1, keepdims=True)
    dx = r * gd - xs * (r * r * r * c)
    dg = jnp.sum(dy * (xs * r), axis=0, keepdims=True)
    return dx, dg


def _rms_fwd(x, g, segs, *, name):
    S, W = x.shape
    tm = _tile(S, 512)

    def body(x_ref, g_ref, o_ref):
        for s0, n in segs:
            o_ref[:, s0:s0 + n] = _seg_rms(x_ref[:, s0:s0 + n], g_ref[:, s0:s0 + n]).astype(o_ref.dtype)

    return pl.pallas_call(
        body, name=name, grid=(S // tm,),
        in_specs=[pl.BlockSpec((tm, W), lambda i: (i, 0)), pl.BlockSpec((1, W), lambda i: (0, 0))],
        out_specs=pl.BlockSpec((tm, W), lambda i: (i, 0)),
        out_shape=jax.ShapeDtypeStruct((S, W), BF16),
        compiler_params=_params("parallel"),
    )(x, g)


def _rms_bwd(x, g, dy, segs, *, resid=None, out_dtype, name):
    S, W = x.shape
    tm = _tile(S, 512)
    has_resid = resid is not None

    def body(*refs):
        if has_resid:
            x_ref, g_ref, dy_ref, r_ref, dx_ref, dg_ref = refs
        else:
            x_ref, g_ref, dy_ref, dx_ref, dg_ref = refs

        @pl.when(pl.program_id(0) == 0)
        def _():
            dg_ref[...] = jnp.zeros_like(dg_ref)

        for s0, n in segs:
            dx, dg = _seg_rms_bwd(x_ref[:, s0:s0 + n], g_ref[:, s0:s0 + n], dy_ref[:, s0:s0 + n])
            if has_resid:
                dx = dx + r_ref[:, s0:s0 + n]
            dx_ref[:, s0:s0 + n] = dx.astype(dx_ref.dtype)
            dg_ref[:, s0:s0 + n] += dg

    row = pl.BlockSpec((tm, W), lambda i: (i, 0))
    vec = pl.BlockSpec((1, W), lambda i: (0, 0))
    ins = [x, g, dy] + ([resid] if has_resid else [])
    return pl.pallas_call(
        body, name=name, grid=(S // tm,),
        in_specs=[row, vec, row] + ([row] if has_resid else []),
        out_specs=[row, vec],
        out_shape=[jax.ShapeDtypeStruct((S, W), out_dtype), jax.ShapeDtypeStruct((1, W), F32)],
        compiler_params=_params("arbitrary"),
    )(*ins)


def _resid_rms(x, y, g_post, g_next, *, name):
    S, W = x.shape
    tm = _tile(S, 512)
    with_next = g_next is not None

    def body(*refs):
        if with_next:
            x_ref, y_ref, gp_ref, gn_ref, xo_ref, h_ref = refs
        else:
            x_ref, y_ref, gp_ref, xo_ref = refs
        xn = x_ref[...] + _seg_rms(y_ref[...], gp_ref[...])
        xo_ref[...] = xn
        if with_next:
            h_ref[...] = _seg_rms(xn, gn_ref[...]).astype(BF16)

    row = pl.BlockSpec((tm, W), lambda i: (i, 0))
    vec = pl.BlockSpec((1, W), lambda i: (0, 0))
    outs = [jax.ShapeDtypeStruct((S, W), F32)] + ([jax.ShapeDtypeStruct((S, W), BF16)] if with_next else [])
    res = pl.pallas_call(
        body, name=name, grid=(S // tm,),
        in_specs=[row, row, vec] + ([vec] if with_next else []),
        out_specs=[row] + ([row] if with_next else []),
        out_shape=outs,
        compiler_params=_params("parallel"),
    )(*([x, y, g_post] + ([g_next] if with_next else [])))
    return (res[0], res[1]) if with_next else (res[0], None)


def _loss_head(y, target):
    S, W = y.shape
    tm = _tile(S, 512)

    def body(y_ref, t_ref, d_ref, l_ref):
        @pl.when(pl.program_id(0) == 0)
        def _():
            l_ref[...] = jnp.zeros_like(l_ref)

        err = y_ref[...] - t_ref[...]
        d_ref[...] = err * (1.0 / W)
        l_ref[...] += 0.5 * jnp.sum(jnp.mean(err * err, axis=-1, keepdims=True), axis=0, keepdims=True)

    row = pl.BlockSpec((tm, W), lambda i: (i, 0))
    d, l = pl.pallas_call(
        body, name="loss_head", grid=(S // tm,),
        in_specs=[row, row],
        out_specs=[row, pl.BlockSpec((1, 1), lambda i: (0, 0))],
        out_shape=[jax.ShapeDtypeStruct((S, W), F32), jax.ShapeDtypeStruct((1, 1), F32)],
        compiler_params=_params("arbitrary"),
    )(y, target)
    return l[0, 0], d


def _attn_fwd(q, k, v, scale, fq_col=None, fk_row=None, *, name):
    H, S, Dk = q.shape
    Dv = v.shape[-1]
    T = _tile(S, ATT_TILE)
    nq = S // T
    forget = fq_col is not None

    def body(*refs):
        if forget:
            q_ref, k_ref, v_ref, fq_ref, fk_ref, o_ref, lse_ref, m_s, l_s, acc_s = refs
        else:
            q_ref, k_ref, v_ref, o_ref, lse_ref, m_s, l_s, acc_s = refs
        i = pl.program_id(1)
        m_s[...] = jnp.full_like(m_s, NEG_INF)
        l_s[...] = jnp.zeros_like(l_s)
        acc_s[...] = jnp.zeros_like(acc_s)
        qi = q_ref[0]

        def tile(j, masked):
            off = pl.multiple_of(j * T, T)
            kj = k_ref[0, pl.ds(off, T), :]
            vj = v_ref[0, pl.ds(off, T), :]
            s = _dot(qi, kj, NT) * scale
            if forget:
                s = s + (fq_ref[0] - fk_ref[0, j])
            if masked:
                r = lax.broadcasted_iota(jnp.int32, (T, T), 0)
                c = lax.broadcasted_iota(jnp.int32, (T, T), 1)
                s = jnp.where(c <= r, s, NEG_INF)
            m_prev = m_s[...]
            m_new = jnp.maximum(m_prev, jnp.max(s, axis=-1, keepdims=True))
            alpha = jnp.exp(m_prev - m_new)
            p = jnp.exp(s - m_new)
            l_s[...] = alpha * l_s[...] + jnp.sum(p, axis=-1, keepdims=True)
            p_hi = p.astype(BF16)
            pv = _dot(p_hi, vj)
            if forget:
                pv = pv + _dot((p - p_hi.astype(F32)).astype(BF16), vj)
            acc_s[...] = alpha * acc_s[...] + pv
            m_s[...] = m_new

        def loop_body(j, carry):
            tile(j, False)
            return carry

        lax.fori_loop(0, i, loop_body, 0)
        tile(i, True)
        o_ref[0] = acc_s[...] / l_s[...]
        lse_ref[0] = m_s[...] + jnp.log(l_s[...])

    in_specs = [pl.BlockSpec((1, T, Dk), lambda h, i: (h, i, 0)),
                pl.BlockSpec((1, S, Dk), lambda h, i: (h, 0, 0)),
                pl.BlockSpec((1, S, Dv), lambda h, i: (h, 0, 0))]
    ins = [q, k, v]
    if forget:
        in_specs += [pl.BlockSpec((1, T, 1), lambda h, i: (h, i, 0)),
                     pl.BlockSpec((1, nq, 1, T), lambda h, i: (h, 0, 0, 0))]
        ins += [fq_col, fk_row]
    return pl.pallas_call(
        body, name=name, grid=(H, nq),
        in_specs=in_specs,
        out_specs=[pl.BlockSpec((1, T, Dv), lambda h, i: (h, i, 0)), pl.BlockSpec((1, T, 1), lambda h, i: (h, i, 0))],
        out_shape=[jax.ShapeDtypeStruct((H, S, Dv), F32), jax.ShapeDtypeStruct((H, S, 1), F32)],
        scratch_shapes=[pltpu.VMEM((T, 1), F32), pltpu.VMEM((T, 1), F32), pltpu.VMEM((T, Dv), F32)],
        compiler_params=_params("parallel", "arbitrary"),
    )(*ins)


def _attn_bwd(q, k, v, do, lse_row, delta_row, scale, fq_row=None, fk_col=None, *, name):
    H, S, Dk = q.shape
    Dv = v.shape[-1]
    T = _tile(S, ATT_TILE)
    nq = S // T
    forget = fq_row is not None

    def body(*refs):
        if forget:
            (q_ref, k_ref, v_ref, do_ref, lse_ref, dl_ref, fq_ref, fk_ref,
             dq_ref, dk_ref, dv_ref, df_ref, dk_s, dv_s, df_s) = refs
        else:
            q_ref, k_ref, v_ref, do_ref, lse_ref, dl_ref, dq_ref, dk_ref, dv_ref, dk_s, dv_s = refs
        j = pl.program_id(1)

        @pl.when(j == 0)
        def _():
            dq_ref[...] = jnp.zeros_like(dq_ref)

        dk_s[...] = jnp.zeros_like(dk_s)
        dv_s[...] = jnp.zeros_like(dv_s)
        if forget:
            df_s[...] = jnp.zeros_like(df_s)
        kj = k_ref[0]
        vj = v_ref[0]

        def tile(i, masked):
            off = pl.multiple_of(i * T, T)
            qi = q_ref[0, pl.ds(off, T), :]
            doi = do_ref[0, pl.ds(off, T), :]
            st = _dot(kj, qi, NT) * scale
            if forget:
                st = st + (fq_ref[0, i] - fk_ref[0])
            if masked:
                r = lax.broadcasted_iota(jnp.int32, (T, T), 0)
                c = lax.broadcasted_iota(jnp.int32, (T, T), 1)
                st = jnp.where(r <= c, st, NEG_INF)
            pt = jnp.exp(st - lse_ref[0, i])
            dv_s[...] += _dot(pt.astype(BF16), doi)
            dpt = _dot(vj, doi, NT)
            dst = pt * (dpt - dl_ref[0, i])
            dsb = dst.astype(BF16)
            dk_s[...] += _dot(dsb, qi)
            dq_ref[0, pl.ds(off, T), :] += _dot(dsb, kj, TN) * scale
            if forget:
                part = dst[:, 0:LANES]
                for c0 in range(LANES, T, LANES):
                    part = part + dst[:, c0:c0 + LANES]
                df_s[...] += part

        tile(j, True)

        def loop_body(i, carry):
            tile(i, False)
            return carry

        lax.fori_loop(j + 1, nq, loop_body, 0)
        dk_ref[0] = dk_s[...] * scale
        dv_ref[0] = dv_s[...]
        if forget:
            df_ref[0] = -jnp.sum(df_s[...], axis=-1, keepdims=True)

    res_q = pl.BlockSpec((1, S, Dk), lambda h, j: (h, 0, 0))
    res_do = pl.BlockSpec((1, S, Dv), lambda h, j: (h, 0, 0))
    row4 = pl.BlockSpec((1, nq, 1, T), lambda h, j: (h, 0, 0, 0))
    in_specs = [res_q, pl.BlockSpec((1, T, Dk), lambda h, j: (h, j, 0)),
                pl.BlockSpec((1, T, Dv), lambda h, j: (h, j, 0)), res_do, row4, row4]
    ins = [q, k, v, do, lse_row, delta_row]
    out_specs = [res_q, pl.BlockSpec((1, T, Dk), lambda h, j: (h, j, 0)), pl.BlockSpec((1, T, Dv), lambda h, j: (h, j, 0))]
    out_shape = [jax.ShapeDtypeStruct((H, S, Dk), F32), jax.ShapeDtypeStruct((H, S, Dk), F32),
                 jax.ShapeDtypeStruct((H, S, Dv), F32)]
    scratch = [pltpu.VMEM((T, Dk), F32), pltpu.VMEM((T, Dv), F32)]
    if forget:
        in_specs += [row4, pl.BlockSpec((1, T, 1), lambda h, j: (h, j, 0))]
        ins += [fq_row, fk_col]
        out_specs.append(pl.BlockSpec((1, T, 1), lambda h, j: (h, j, 0)))
        out_shape.append(jax.ShapeDtypeStruct((H, S, 1), F32))
        scratch.append(pltpu.VMEM((T, min(T, LANES)), F32))
    return pl.pallas_call(
        body, name=name, grid=(H, nq),
        in_specs=in_specs, out_specs=out_specs, out_shape=out_shape, scratch_shapes=scratch,
        compiler_params=_params("parallel", "arbitrary"),
    )(*ins)


def _attn_delta(o, do, *, name):
    H, S, Dv = o.shape
    T = _tile(S, 512)

    def body(o_ref, do_ref, d_ref):
        d_ref[0] = jnp.sum(o_ref[0] * do_ref[0].astype(F32), axis=-1, keepdims=True)

    blk = pl.BlockSpec((1, T, Dv), lambda h, i: (h, i, 0))
    return pl.pallas_call(
        body, name=name, grid=(H, S // T),
        in_specs=[blk, blk], out_specs=pl.BlockSpec((1, T, 1), lambda h, i: (h, i, 0)),
        out_shape=jax.ShapeDtypeStruct((H, S, 1), F32),
        compiler_params=_params("parallel", "parallel"),
    )(o, do)


def _swa_masks(i):
    r = lax.broadcasted_iota(jnp.int32, (WINDOW, WINDOW), 0)
    c = lax.broadcasted_iota(jnp.int32, (WINDOW, WINDOW), 1)
    return (c > r) & (i > 0), c <= r


def _swa_specs(S):
    nb = S // WINDOW
    W = WINDOW
    qs = pl.BlockSpec((SWA_Q_HEADS, W, HEAD_DIM), lambda i: (0, i, 0))
    kp = pl.BlockSpec((SWA_KV_HEADS, W, HEAD_DIM), lambda i: (0, jnp.maximum(i - 1, 0), 0))
    kc = pl.BlockSpec((SWA_KV_HEADS, W, HEAD_DIM), lambda i: (0, i, 0))
    bias = pl.BlockSpec((SWA_Q_HEADS, W, 2 * W), lambda i: (0, 0, 0))
    col = pl.BlockSpec((SWA_Q_HEADS, W, 1), lambda i: (0, i, 0))
    sink = pl.BlockSpec(memory_space=pltpu.SMEM)
    return nb, qs, kp, kc, bias, col, sink


def _swa_fwd(q, k, v, bias, sinks, *, name):
    S = q.shape[1]
    W = WINDOW
    scale = HEAD_DIM ** -0.5
    nb, qs, kp, kc, bs, col, sk = _swa_specs(S)

    def body(sink_ref, q_ref, kp_ref, kc_ref, vp_ref, vc_ref, b_ref, o_ref, lse_ref):
        mask_p, mask_c = _swa_masks(pl.program_id(0))
        for h in range(SWA_Q_HEADS):
            g = h // SWA_GROUP
            qh = q_ref[h]
            s_p = jnp.where(mask_p, _dot(qh, kp_ref[g], NT) * scale + b_ref[h, :, 0:W], NEG_INF)
            s_c = jnp.where(mask_c, _dot(qh, kc_ref[g], NT) * scale + b_ref[h, :, W:2 * W], NEG_INF)
            sink = sink_ref[h]
            m = jnp.maximum(jnp.maximum(jnp.max(s_p, axis=-1, keepdims=True), jnp.max(s_c, axis=-1, keepdims=True)), sink)
            p_p = jnp.exp(s_p - m)
            p_c = jnp.exp(s_c - m)
            l = jnp.sum(p_p, axis=-1, keepdims=True) + jnp.sum(p_c, axis=-1, keepdims=True) + jnp.exp(sink - m)
            o = _dot(p_p.astype(BF16), vp_ref[g]) + _dot(p_c.astype(BF16), vc_ref[g])
            o_ref[h] = o / l
            lse_ref[h] = m + jnp.log(l)

    return pl.pallas_call(
        body, name=name, grid=(nb,),
        in_specs=[sk, qs, kp, kc, kp, kc, bs],
        out_specs=[qs, col],
        out_shape=[jax.ShapeDtypeStruct((SWA_Q_HEADS, S, HEAD_DIM), F32), jax.ShapeDtypeStruct((SWA_Q_HEADS, S, 1), F32)],
        compiler_params=_params("parallel"),
    )(sinks, q, k, k, v, v, bias)


def _swa_bwd(q, k, v, bias, sinks, do, lse, delta, *, name):
    S = q.shape[1]
    W = WINDOW
    scale = HEAD_DIM ** -0.5
    nb, qs, kp, kc, bs, col, sk = _swa_specs(S)

    def body(sink_ref, q_ref, kp_ref, kc_ref, vp_ref, vc_ref, b_ref, do_ref, lse_ref, dl_ref,
             dq_ref, dk_ref, dv_ref, db_ref, dsk_ref):
        i = pl.program_id(0)

        @pl.when(i == 0)
        def _():
            dk_ref[...] = jnp.zeros_like(dk_ref)
            dv_ref[...] = jnp.zeros_like(dv_ref)
            db_ref[...] = jnp.zeros_like(db_ref)
            dsk_ref[...] = jnp.zeros_like(dsk_ref)

        mask_p, mask_c = _swa_masks(i)
        prev = pl.ds(pl.multiple_of(jnp.maximum(i - 1, 0) * W, W), W)
        cur = pl.ds(pl.multiple_of(i * W, W), W)
        for h in range(SWA_Q_HEADS):
            g = h // SWA_GROUP
            qh = q_ref[h]
            doh = do_ref[h]
            lse_h = lse_ref[h]
            dl_h = dl_ref[h]
            s_p = jnp.where(mask_p, _dot(qh, kp_ref[g], NT) * scale + b_ref[h, :, 0:W], NEG_INF)
            s_c = jnp.where(mask_c, _dot(qh, kc_ref[g], NT) * scale + b_ref[h, :, W:2 * W], NEG_INF)
            p_p = jnp.exp(s_p - lse_h)
            p_c = jnp.exp(s_c - lse_h)
            ds_p = p_p * (_dot(doh, vp_ref[g], NT) - dl_h)
            ds_c = p_c * (_dot(doh, vc_ref[g], NT) - dl_h)
            db_ref[h, :, 0:W] += ds_p
            db_ref[h, :, W:2 * W] += ds_c
            dsk = -jnp.sum(jnp.exp(sink_ref[h] - lse_h) * dl_h, axis=0, keepdims=True)
            dsk_ref[h:h + 1, :] += jnp.broadcast_to(dsk, (1, LANES))
            dsb_p = ds_p.astype(BF16)
            dsb_c = ds_c.astype(BF16)
            dq_ref[h] = (_dot(dsb_p, kp_ref[g]) + _dot(dsb_c, kc_ref[g])) * scale
            dk_ref[g, prev, :] += _dot(dsb_p, qh, TN) * scale
            dk_ref[g, cur, :] += _dot(dsb_c, qh, TN) * scale
            dv_ref[g, prev, :] += _dot(p_p.astype(BF16), doh, TN)
            dv_ref[g, cur, :] += _dot(p_c.astype(BF16), doh, TN)

    full_kv = pl.BlockSpec((SWA_KV_HEADS, S, HEAD_DIM), lambda i: (0, 0, 0))
    return pl.pallas_call(
        body, name=name, grid=(nb,),
        in_specs=[sk, qs, kp, kc, kp, kc, bs, qs, col, col],
        out_specs=[qs, full_kv, full_kv, bs, pl.BlockSpec((SWA_Q_HEADS, LANES), lambda i: (0, 0))],
        out_shape=[jax.ShapeDtypeStruct((SWA_Q_HEADS, S, HEAD_DIM), F32),
                   jax.ShapeDtypeStruct((SWA_KV_HEADS, S, HEAD_DIM), F32),
                   jax.ShapeDtypeStruct((SWA_KV_HEADS, S, HEAD_DIM), F32),
                   jax.ShapeDtypeStruct((SWA_Q_HEADS, WINDOW, 2 * WINDOW), F32),
                   jax.ShapeDtypeStruct((SWA_Q_HEADS, LANES), F32)],
        compiler_params=_params("arbitrary"),
    )(sinks, q, k, k, v, v, bias, do, lse, delta)


def _rel_onehot_t():
    qi = jnp.arange(WINDOW, dtype=jnp.int32)[:, None] + WINDOW
    kj = jnp.arange(2 * WINDOW, dtype=jnp.int32)[None, :]
    dist = qi - kj
    max_exact = REL_BUCKETS // 2
    d = jnp.maximum(dist, 0)
    log_ratio = jnp.log(jnp.maximum(d, 1).astype(F32) / max_exact) / math.log(REL_MAX_DIST / max_exact)
    large = jnp.minimum(max_exact + (log_ratio * (REL_BUCKETS - max_exact)).astype(jnp.int32), REL_BUCKETS - 1)
    bucket = jnp.where(d < max_exact, d, large).reshape(-1)
    return (bucket[None, :] == jnp.arange(REL_BUCKETS, dtype=jnp.int32)[:, None]).astype(BF16)


def _bias_table(rel_bias_t, onehot_t):
    Hq, NB = rel_bias_t.shape
    N = onehot_t.shape[1]
    tn = _tile(N, 4096)

    def body(r_ref, oh_ref, o_ref):
        oh = oh_ref[...]
        a1, a2, a3 = _split3(r_ref[...])
        o_ref[...] = _dot(a1, oh) + _dot(a2, oh) + _dot(a3, oh)

    return pl.pallas_call(
        body, name="rel_bias_table", grid=(N // tn,),
        in_specs=[pl.BlockSpec((Hq, NB), lambda j: (0, 0)), pl.BlockSpec((NB, tn), lambda j: (0, j))],
        out_specs=pl.BlockSpec((Hq, tn), lambda j: (0, j)),
        out_shape=jax.ShapeDtypeStruct((Hq, N), F32),
        compiler_params=_params("parallel"),
    )(rel_bias_t, onehot_t)


def _bias_table_bwd(dbias, onehot_t):
    L, Hq, N = dbias.shape
    NB = onehot_t.shape[0]
    tn = _tile(N, 4096)

    def body(d_ref, oh_ref, o_ref):
        @pl.when(pl.program_id(0) == 0)
        def _():
            o_ref[...] = jnp.zeros_like(o_ref)

        d = d_ref[0]
        for l in range(1, L):
            d = d + d_ref[l]
        oh = oh_ref[...]
        a1, a2, a3 = _split3(d)
        o_ref[...] += _dot(a1, oh, NT) + _dot(a2, oh, NT) + _dot(a3, oh, NT)

    return pl.pallas_call(
        body, name="rel_bias_bwd", grid=(N // tn,),
        in_specs=[pl.BlockSpec((L, Hq, tn), lambda j: (0, 0, j)), pl.BlockSpec((NB, tn), lambda j: (0, j))],
        out_specs=pl.BlockSpec((Hq, NB), lambda j: (0, 0)),
        out_shape=jax.ShapeDtypeStruct((Hq, NB), F32),
        compiler_params=_params("arbitrary"),
    )(dbias, onehot_t)


def _gate_fwd(proj, fb_slab, *, name):
    S = proj.shape[0]
    tm = _tile(S, 256)
    mc = MISC_COL // LANES

    def body(z_ref, fb_ref, o_ref, carry):
        @pl.when(pl.program_id(0) == 0)
        def _():
            carry[...] = jnp.zeros_like(carry)

        z = z_ref[...] + fb_ref[...]
        lf = jnp.minimum(z, 0.0) - jnp.log1p(jnp.exp(-jnp.abs(z)))
        r = lax.broadcasted_iota(jnp.int32, (tm, tm), 0)
        c = lax.broadcasted_iota(jnp.int32, (tm, tm), 1)
        tri = (r >= c).astype(BF16)
        a1, a2, a3 = _split3(lf)
        cum = _dot(tri, a1) + _dot(tri, a2) + _dot(tri, a3) + carry[0:1, :]
        o_ref[...] = cum
        carry[...] = jnp.broadcast_to(cum[tm - 1:tm, :], carry.shape)

    return pl.pallas_call(
        body, name=name, grid=(S // tm,),
        in_specs=[pl.BlockSpec((tm, LANES), lambda i: (i, mc)), pl.BlockSpec((1, LANES), lambda i: (0, 0))],
        out_specs=pl.BlockSpec((tm, LANES), lambda i: (i, 0)),
        out_shape=jax.ShapeDtypeStruct((S, LANES), F32),
        scratch_shapes=[pltpu.VMEM((8, LANES), F32)],
        compiler_params=_params("arbitrary"),
    )(proj, fb_slab)


def _gate_bwd(proj, fb_slab, dF, *, name):
    S = proj.shape[0]
    tm = _tile(S, 256)
    nt = S // tm
    mc = MISC_COL // LANES

    def body(z_ref, fb_ref, df_ref, dz_ref, dfb_ref, carry):
        @pl.when(pl.program_id(0) == 0)
        def _():
            carry[...] = jnp.zeros_like(carry)
            dfb_ref[...] = jnp.zeros_like(dfb_ref)

        r = lax.broadcasted_iota(jnp.int32, (tm, tm), 0)
        c = lax.broadcasted_iota(jnp.int32, (tm, tm), 1)
        tri = (c >= r).astype(BF16)
        a1, a2, a3 = _split3(df_ref[...])
        dlf = _dot(tri, a1) + _dot(tri, a2) + _dot(tri, a3) + carry[0:1, :]
        carry[...] = jnp.broadcast_to(dlf[0:1, :], carry.shape)
        z = z_ref[...] + fb_ref[...]
        lane = lax.broadcasted_iota(jnp.int32, (tm, LANES), 1)
        keep = (lane >= FL_LANE) & (lane < FL_LANE + FOX_HEADS)
        dz = jnp.where(keep, dlf / (1.0 + jnp.exp(z)), 0.0)
        dz_ref[...] = dz
        dfb_ref[...] += jnp.sum(dz, axis=0, keepdims=True)

    return pl.pallas_call(
        body, name=name, grid=(nt,),
        in_specs=[pl.BlockSpec((tm, LANES), lambda i: (nt - 1 - i, mc)), pl.BlockSpec((1, LANES), lambda i: (0, 0)),
                  pl.BlockSpec((tm, LANES), lambda i: (nt - 1 - i, 0))],
        out_specs=[pl.BlockSpec((tm, LANES), lambda i: (nt - 1 - i, 0)), pl.BlockSpec((1, LANES), lambda i: (0, 0))],
        out_shape=[jax.ShapeDtypeStruct((S, LANES), F32), jax.ShapeDtypeStruct((1, LANES), F32)],
        scratch_shapes=[pltpu.VMEM((8, LANES), F32)],
        compiler_params=_params("arbitrary"),
    )(proj, fb_slab, dF)


def _rope_tables(S):
    pos = jnp.arange(S, dtype=F32)
    inv_freq = ROPE_THETA ** (-(jnp.arange(MLA_ROPE // 2, dtype=F32) * 2.0 / MLA_ROPE))
    ang = pos[:, None] * inv_freq[None, :]
    cos, sin = jnp.cos(ang), jnp.sin(ang)
    z16 = jnp.zeros_like(cos)

    def slab(lo, width):
        def put(first, second, fill):
            parts = [jnp.full((S, lo), fill, F32), first, second, jnp.full((S, width - lo - MLA_ROPE), fill, F32)]
            return jnp.concatenate(parts, axis=1)
        return put(cos, cos, 1.0), put(-sin, z16, 0.0), put(z16, sin, 0.0)

    cq, s1q, s2q = slab(MLA_NOPE, LANES)
    tq = tuple(jnp.tile(t, (1, MLA_HEADS)) for t in (cq, s1q, s2q))
    cm, s1m, s2m = slab(KR_LANE, LANES)
    lane = jnp.arange(LANES)[None, :]
    cm = jnp.where((lane >= KR_LANE) & (lane < KR_LANE + MLA_ROPE), cm, 0.0)
    return tq, (cm, s1m, s2m)


def _rope(x, c, s1, s2):
    n = x.shape[-1]
    half = MLA_ROPE // 2
    return x * c + pltpu.roll(x, n - half, 1) * s1 + pltpu.roll(x, half, 1) * s2


def _rope_t(dy, c, s1, s2):
    n = dy.shape[-1]
    half = MLA_ROPE // 2
    return dy * c + pltpu.roll(dy * s1, half, 1) + pltpu.roll(dy * s2, n - half, 1)


def _mla_prep_fwd(proj, g_q, g_kv, w_uq_p, w_ukv, tq, tm_tabs, *, name):
    S = proj.shape[0]
    tm = _tile(S, 512)
    QW = MLA_HEADS * LANES

    def body(cq_ref, ckv_ref, mi_ref, gq_ref, gkv_ref, wq_ref, wkv_ref, c_ref, s1_ref, s2_ref, cm_ref, s1m_ref, s2m_ref,
             nq_ref, nkv_ref, q_ref, kv_ref, kr_ref):
        nq = _seg_rms(cq_ref[...], gq_ref[...]).astype(BF16)
        nkv = _seg_rms(ckv_ref[...], gkv_ref[...]).astype(BF16)
        nq_ref[...] = nq
        nkv_ref[...] = nkv
        q_ref[...] = _rope(_dot(nq, wq_ref[...]), c_ref[...], s1_ref[...], s2_ref[...]).astype(BF16)
        kv_ref[...] = _dot(nkv, wkv_ref[...]).astype(BF16)
        kr_ref[...] = _rope(mi_ref[...], cm_ref[...], s1m_ref[...], s2m_ref[...]).astype(BF16)

    def row(w, cblk=0):
        return pl.BlockSpec((tm, w), lambda i: (i, cblk))

    def full(a):
        return pl.BlockSpec(a.shape, lambda i: (0, 0))

    return pl.pallas_call(
        body, name=name, grid=(S // tm,),
        in_specs=[row(MLA_Q_RANK, 1536 // MLA_Q_RANK), row(MLA_KV_RANK, 1792 // MLA_KV_RANK), row(LANES, MISC_COL // LANES),
                  full(g_q), full(g_kv), full(w_uq_p), full(w_ukv),
                  row(QW), row(QW), row(QW), row(LANES), row(LANES), row(LANES)],
        out_specs=[row(MLA_Q_RANK), row(MLA_KV_RANK), row(QW), row(QW), row(LANES)],
        out_shape=[jax.ShapeDtypeStruct((S, MLA_Q_RANK), BF16), jax.ShapeDtypeStruct((S, MLA_KV_RANK), BF16),
                   jax.ShapeDtypeStruct((S, QW), BF16), jax.ShapeDtypeStruct((S, QW), BF16),
                   jax.ShapeDtypeStruct((S, LANES), BF16)],
        compiler_params=_params("parallel"),
    )(proj, proj, proj, g_q, g_kv, w_uq_p, w_ukv, *tq, *tm_tabs)


def _mla_prep_bwd(proj, nq, nkv, g_q, g_kv, w_uq_p, w_ukv, tq, tm_tabs, dq, dkv, dkr4, *, name):
    S = proj.shape[0]
    tm = _tile(S, 512)
    QW = MLA_HEADS * LANES

    def body(cq_ref, ckv_ref, nq_ref, nkv_ref, gq_ref, gkv_ref, wq_ref, wkv_ref, c_ref, s1_ref, s2_ref,
             cm_ref, s1m_ref, s2m_ref, dq_ref, dkv_ref, dkr_ref,
             dcq_ref, dckv_ref, dmi_ref, dwq_ref, dwkv_ref, dgq_ref, dgkv_ref):
        @pl.when(pl.program_id(0) == 0)
        def _():
            dwq_ref[...] = jnp.zeros_like(dwq_ref)
            dwkv_ref[...] = jnp.zeros_like(dwkv_ref)
            dgq_ref[...] = jnp.zeros_like(dgq_ref)
            dgkv_ref[...] = jnp.zeros_like(dgkv_ref)

        dqm = _rope_t(dq_ref[...], c_ref[...], s1_ref[...], s2_ref[...]).astype(BF16)
        dwq_ref[...] += _dot(nq_ref[...], dqm, TN)
        dx, dg = _seg_rms_bwd(cq_ref[...], gq_ref[...], _dot(dqm, wq_ref[...], NT))
        dcq_ref[...] = dx
        dgq_ref[...] += dg
        dkv = dkv_ref[...].astype(BF16)
        dwkv_ref[...] += _dot(nkv_ref[...], dkv, TN)
        dx, dg = _seg_rms_bwd(ckv_ref[...], gkv_ref[...], _dot(dkv, wkv_ref[...], NT))
        dckv_ref[...] = dx
        dgkv_ref[...] += dg
        dkr = dkr_ref[0]
        for h in range(1, MLA_HEADS):
            dkr = dkr + dkr_ref[h]
        dmi_ref[...] = _rope_t(dkr, cm_ref[...], s1m_ref[...], s2m_ref[...])

    def row(w, cblk=0):
        return pl.BlockSpec((tm, w), lambda i: (i, cblk))

    def full(a):
        return pl.BlockSpec(a.shape, lambda i: (0, 0))

    def acc(r, c):
        return pl.BlockSpec((r, c), lambda i: (0, 0))

    return pl.pallas_call(
        body, name=name, grid=(S // tm,),
        in_specs=[row(MLA_Q_RANK, 1536 // MLA_Q_RANK), row(MLA_KV_RANK, 1792 // MLA_KV_RANK),
                  row(MLA_Q_RANK), row(MLA_KV_RANK), full(g_q), full(g_kv), full(w_uq_p), full(w_ukv),
                  row(QW), row(QW), row(QW), row(LANES), row(LANES), row(LANES),
                  row(QW), row(QW), pl.BlockSpec((MLA_HEADS, tm, LANES), lambda i: (0, i, 0))],
        out_specs=[row(MLA_Q_RANK), row(MLA_KV_RANK), row(LANES), acc(MLA_Q_RANK, QW), acc(MLA_KV_RANK, QW),
                   acc(1, MLA_Q_RANK), acc(1, MLA_KV_RANK)],
        out_shape=[jax.ShapeDtypeStruct((S, MLA_Q_RANK), F32), jax.ShapeDtypeStruct((S, MLA_KV_RANK), F32),
                   jax.ShapeDtypeStruct((S, LANES), F32), jax.ShapeDtypeStruct((MLA_Q_RANK, QW), F32),
                   jax.ShapeDtypeStruct((MLA_KV_RANK, QW), F32), jax.ShapeDtypeStruct((1, MLA_Q_RANK), F32),
                   jax.ShapeDtypeStruct((1, MLA_KV_RANK), F32)],
        compiler_params=_params("arbitrary"),
    )(proj, proj, nq, nkv, g_q, g_kv, w_uq_p, w_ukv, *tq, *tm_tabs, dq, dkv, dkr4)


GELU_C = math.sqrt(2.0 / math.pi)
GELU_A = 0.044715


def _conv_taps(a, halo, w_ref, b_ref, first):
    row = lax.broadcasted_iota(jnp.int32, a.shape, 0)
    h7 = jnp.where(first, 0.0, halo[7:8, :])
    h6 = jnp.where(first, 0.0, halo[6:7, :])
    a1 = jnp.where(row == 0, h7, pltpu.roll(a, 1, 0))
    a2 = jnp.where(row == 0, h6, jnp.where(row == 1, h7, pltpu.roll(a, 2, 0)))
    u = ((b_ref[...] + w_ref[0:1, :] * a2) + w_ref[1:2, :] * a1) + w_ref[2:3, :] * a
    return u, a1, a2


def _conv_specs(S, tm, tc, nc):
    hb = tm // 8
    main = lambda off: pl.BlockSpec((tm, tc), lambda j, i: (i, j + off))
    halo = lambda off: pl.BlockSpec((8, tc), lambda j, i: (jnp.maximum(i * hb - 1, 0), j + off))
    wspec = lambda off: pl.BlockSpec((3, tc), lambda j, i: (0, j + off))
    bspec = lambda off: pl.BlockSpec((1, tc), lambda j, i: (0, j + off))
    return main, halo, wspec, bspec


def _conv_geglu_fwd(a, conv_w, conv_b, *, name):
    S = a.shape[0]
    tm, tc = _tile(S, 512), _tile(D_FF, 1408)
    nc = D_FF // tc
    main, halo, wspec, bspec = _conv_specs(S, tm, tc, nc)

    def body(ag_ref, au_ref, hg_ref, hu_ref, wg_ref, wu_ref, bg_ref, bu_ref, z_ref):
        first = pl.program_id(1) == 0
        gate, _, _ = _conv_taps(ag_ref[...], hg_ref[...], wg_ref, bg_ref, first)
        up, _, _ = _conv_taps(au_ref[...], hu_ref[...], wu_ref, bu_ref, first)
        cdf = 0.5 * (1.0 + jnp.tanh(GELU_C * (gate + GELU_A * (gate * gate * gate))))
        z_ref[...] = (gate * cdf * up).astype(BF16)

    return pl.pallas_call(
        body, name=name, grid=(nc, S // tm),
        in_specs=[main(0), main(nc), halo(0), halo(nc), wspec(0), wspec(nc), bspec(0), bspec(nc)],
        out_specs=pl.BlockSpec((tm, tc), lambda j, i: (i, j)),
        out_shape=jax.ShapeDtypeStruct((S, D_FF), BF16),
        compiler_params=_params("parallel", "arbitrary"),
    )(a, a, a, a, conv_w, conv_w, conv_b, conv_b)


def _conv_geglu_bwd(a, conv_w, conv_b, dz, *, name):
    S = a.shape[0]
    tm, tc = _tile(S, 512), _tile(D_FF, 1408)
    nc = D_FF // tc
    main, halo, wspec, bspec = _conv_specs(S, tm, tc, nc)

    def body(ag_ref, au_ref, hg_ref, hu_ref, wg_ref, wu_ref, bg_ref, bu_ref, dz_ref, du_ref, dw_ref, db_ref):
        first = pl.program_id(1) == 0

        @pl.when(first)
        def _():
            dw_ref[...] = jnp.zeros_like(dw_ref)
            db_ref[...] = jnp.zeros_like(db_ref)

        gate, g1, g2 = _conv_taps(ag_ref[...], hg_ref[...], wg_ref, bg_ref, first)
        up, u1, u2 = _conv_taps(au_ref[...], hu_ref[...], wu_ref, bu_ref, first)
        dz = dz_ref[...]
        g2x = gate * gate
        th = jnp.tanh(GELU_C * (gate + GELU_A * (g2x * gate)))
        cdf = 0.5 * (1.0 + th)
        dgelu = cdf + gate * (0.5 * (1.0 - th * th) * (GELU_C * (1.0 + 3.0 * GELU_A * g2x)))
        dug = dz * up * dgelu
        duu = dz * (gate * cdf)
        du_ref[0] = dug
        du_ref[1] = duu
        for half, du, taps in ((0, dug, (g2, g1, ag_ref[...])), (1, duu, (u2, u1, au_ref[...]))):
            for tap in range(3):
                dw_ref[half, tap:tap + 1, :] += jnp.sum(du * taps[tap], axis=0, keepdims=True)
            db_ref[half] += jnp.sum(du, axis=0, keepdims=True)

    return pl.pallas_call(
        body, name=name, grid=(nc, S // tm),
        in_specs=[main(0), main(nc), halo(0), halo(nc), wspec(0), wspec(nc), bspec(0), bspec(nc),
                  pl.BlockSpec((tm, tc), lambda j, i: (i, j))],
        out_specs=[pl.BlockSpec((2, tm, tc), lambda j, i: (0, i, j)), pl.BlockSpec((2, 3, tc), lambda j, i: (0, 0, j)),
                   pl.BlockSpec((2, 1, tc), lambda j, i: (0, 0, j))],
        out_shape=[jax.ShapeDtypeStruct((2, S, D_FF), F32), jax.ShapeDtypeStruct((2, 3, D_FF), F32),
                   jax.ShapeDtypeStruct((2, 1, D_FF), F32)],
        compiler_params=_params("parallel", "arbitrary"),
    )(a, a, a, a, conv_w, conv_w, conv_b, conv_b, dz)


def _conv_bwd_input(du, conv_w, *, name):
    S = du.shape[1]
    tm, tc = _tile(S, 512), _tile(D_FF, 1408)
    nc = D_FF // tc
    nr = S // tm
    hb = tm // 8

    def body(du_ref, nx_ref, w_ref, da_ref):
        last = pl.program_id(2) == nr - 1
        d = du_ref[0]
        row = lax.broadcasted_iota(jnp.int32, d.shape, 0)
        n0 = jnp.where(last, 0.0, nx_ref[0, 0:1, :])
        n1 = jnp.where(last, 0.0, nx_ref[0, 1:2, :])
        d1 = jnp.where(row == tm - 1, n0, pltpu.roll(d, tm - 1, 0))
        d2 = jnp.where(row == tm - 1, n1, jnp.where(row == tm - 2, n0, pltpu.roll(d, tm - 2, 0)))
        da_ref[...] = (w_ref[2:3, :] * d + w_ref[1:2, :] * d1 + w_ref[0:1, :] * d2).astype(BF16)

    return pl.pallas_call(
        body, name=name, grid=(2, nc, nr),
        in_specs=[pl.BlockSpec((1, tm, tc), lambda h, j, i: (h, i, j)),
                  pl.BlockSpec((1, 8, tc), lambda h, j, i: (h, jnp.minimum((i + 1) * hb, S // 8 - 1), j)),
                  pl.BlockSpec((3, tc), lambda h, j, i: (0, h * nc + j))],
        out_specs=pl.BlockSpec((tm, tc), lambda h, j, i: (i, h * nc + j)),
        out_shape=jax.ShapeDtypeStruct((S, 2 * D_FF), BF16),
        compiler_params=_params("parallel", "parallel", "arbitrary"),
    )(du, du, conv_w)


def _adamw(w, g, m, v, *, name):
    R, C = w.shape
    tr = _tile(R, PACK_ROW_TILE)

    def body(w_ref, g_ref, m_ref, v_ref, d_ref, mo_ref, vo_ref):
        g = g_ref[...]
        m = ADAM_B1 * m_ref[...] + (1.0 - ADAM_B1) * g
        v = ADAM_B2 * v_ref[...] + (1.0 - ADAM_B2) * jnp.square(g)
        m_hat = m / (1.0 - ADAM_B1 ** ADAM_STEP)
        v_hat = v / (1.0 - ADAM_B2 ** ADAM_STEP)
        d_ref[...] = -ADAM_LR * (m_hat / (jnp.sqrt(v_hat) + ADAM_EPS) + ADAM_WD * w_ref[...])
        mo_ref[...] = m
        vo_ref[...] = v

    blk = pl.BlockSpec((tr, C), lambda i: (i, 0))
    shp = jax.ShapeDtypeStruct((R, C), F32)
    return pl.pallas_call(
        body, name=name, grid=(R // tr,),
        in_specs=[blk] * 4, out_specs=[blk] * 3, out_shape=[shp] * 3,
        compiler_params=_params("parallel"),
    )(w, g, m, v)


HBM_SPEC = pl.BlockSpec(memory_space=pl.ANY)


def _mesh_pos():
    return lax.axis_index("x"), lax.axis_index("y"), lax.axis_index("c")


def _other_chips(x, y):
    return [(1 - x, y), (x, 1 - y), (1 - x, 1 - y)]


def _gather_shards(flat):
    R, C = flat.shape
    Rh = R // 2

    def body(in_ref, out_ref, send_sems, recv_sems, local_sem):
        x, y, c = _mesh_pos()
        me = 2 * x + y
        chips = _other_chips(x, y)
        sibling = (x, y, 1 - c)
        mine_rows = pl.ds(pl.multiple_of(c * Rh, 8), Rh)
        other_rows = pl.ds(pl.multiple_of((1 - c) * Rh, 8), Rh)

        def copy(k, src, dst, to):
            return pltpu.make_async_remote_copy(src_ref=src, dst_ref=dst, send_sem=send_sems.at[k], recv_sem=recv_sems.at[k],
                                                device_id=to, device_id_type=MESH)

        local = pltpu.make_async_copy(in_ref, out_ref.at[me], local_sem)
        local.start()
        first = [copy(j, in_ref.at[mine_rows], out_ref.at[me, mine_rows], (px, py, c)) for j, (px, py) in enumerate(chips)]
        for cp in first:
            cp.start()
        passed = []
        for j, (px, py) in enumerate(chips):
            landed = out_ref.at[2 * px + py, mine_rows]
            copy(j, landed, landed, (px, py, c)).wait_recv()
            cp = copy(3 + j, landed, landed, sibling)
            cp.start()
            passed.append(cp)
        for j, (px, py) in enumerate(chips):
            landed = out_ref.at[2 * px + py, other_rows]
            copy(3 + j, landed, landed, sibling).wait_recv()
        for cp in first + passed:
            cp.wait_send()
        local.wait()

    return pl.pallas_call(
        body, name="gather_weight_shards",
        in_specs=[HBM_SPEC], out_specs=HBM_SPEC,
        out_shape=jax.ShapeDtypeStruct((4, R, C), flat.dtype),
        scratch_shapes=[pltpu.SemaphoreType.DMA((6,)), pltpu.SemaphoreType.DMA((6,)), pltpu.SemaphoreType.DMA],
        compiler_params=pltpu.CompilerParams(has_side_effects=True),
    )(flat)


def _sibling_exchange(g):
    n, _, Rh, C = g.shape

    def body(g_ref, out_ref, send_sems, recv_sems):
        x, y, c = _mesh_pos()
        sibling = (x, y, 1 - c)
        copies = [pltpu.make_async_remote_copy(src_ref=g_ref.at[s, 1 - c], dst_ref=out_ref.at[s], send_sem=send_sems.at[s],
                                               recv_sem=recv_sems.at[s], device_id=sibling, device_id_type=MESH)
                  for s in range(n)]
        for cp in copies:
            cp.start()
        for cp in copies:
            cp.wait()

    return pl.pallas_call(
        body, name="grad_sibling_exchange",
        in_specs=[HBM_SPEC], out_specs=HBM_SPEC,
        out_shape=jax.ShapeDtypeStruct((n, Rh, C), g.dtype),
        scratch_shapes=[pltpu.SemaphoreType.DMA((n,)), pltpu.SemaphoreType.DMA((n,))],
        compiler_params=pltpu.CompilerParams(has_side_effects=True),
    )(g)


def _core_index():
    return jnp.reshape(lax.axis_index("c"), (1,)).astype(jnp.int32)


def _pair_sum(g, recv):
    n, _, Rh, C = g.shape
    tr = _tile(Rh, PACK_ROW_TILE)

    def body(c_ref, g_ref, r_ref, o_ref):
        o_ref[...] = g_ref[0] + r_ref[...]

    return pl.pallas_call(
        body, name="grad_pair_sum",
        grid_spec=pltpu.PrefetchScalarGridSpec(
            num_scalar_prefetch=1, grid=(n, Rh // tr),
            in_specs=[pl.BlockSpec((1, 1, tr, C), lambda s, r, c_ref: (s, c_ref[0], r, 0)),
                      pl.BlockSpec((1, tr, C), lambda s, r, c_ref: (s, r, 0))],
            out_specs=pl.BlockSpec((1, tr, C), lambda s, r, c_ref: (s, r, 0))),
        out_shape=jax.ShapeDtypeStruct((n, Rh, C), F32),
        compiler_params=_params("parallel", "parallel"),
    )(_core_index(), g, recv)


def _chip_scatter(gc):
    n, Rh, C = gc.shape

    def body(g_ref, out_ref, send_sems, recv_sems, local_sem):
        x, y, c = _mesh_pos()
        me = 2 * x + y
        chips = _other_chips(x, y)
        local = pltpu.make_async_copy(g_ref.at[me], out_ref.at[me], local_sem)
        local.start()
        copies = [pltpu.make_async_remote_copy(src_ref=g_ref.at[2 * px + py], dst_ref=out_ref.at[me], send_sem=send_sems.at[j],
                                               recv_sem=recv_sems.at[j], device_id=(px, py, c), device_id_type=MESH)
                  for j, (px, py) in enumerate(chips)]
        for cp in copies:
            cp.start()
        for j, (px, py) in enumerate(chips):
            landed = out_ref.at[2 * px + py]
            pltpu.make_async_remote_copy(src_ref=landed, dst_ref=landed, send_sem=send_sems.at[j], recv_sem=recv_sems.at[j],
                                         device_id=(px, py, c), device_id_type=MESH).wait_recv()
        for cp in copies:
            cp.wait_send()
        local.wait()

    return pl.pallas_call(
        body, name="grad_chip_scatter",
        in_specs=[HBM_SPEC], out_specs=HBM_SPEC,
        out_shape=jax.ShapeDtypeStruct((n, Rh, C), gc.dtype),
        scratch_shapes=[pltpu.SemaphoreType.DMA((3,)), pltpu.SemaphoreType.DMA((3,)), pltpu.SemaphoreType.DMA],
        compiler_params=pltpu.CompilerParams(has_side_effects=True),
    )(gc)


def _chip_sum(parts):
    n, Rh, C = parts.shape
    tr = _tile(Rh, PACK_ROW_TILE)

    def body(p_ref, o_ref):
        o_ref[...] = ((p_ref[0] + p_ref[1]) + p_ref[2]) + p_ref[3]

    return pl.pallas_call(
        body, name="grad_chip_sum", grid=(Rh // tr,),
        in_specs=[pl.BlockSpec((n, tr, C), lambda r: (0, r, 0))],
        out_specs=pl.BlockSpec((tr, C), lambda r: (r, 0)),
        out_shape=jax.ShapeDtypeStruct((Rh, C), F32),
        compiler_params=_params("parallel"),
    )(parts)


def _sibling_share(half):
    Rh, C = half.shape

    def body(h_ref, out_ref, send_sem, recv_sem, local_sem):
        x, y, c = _mesh_pos()
        local = pltpu.make_async_copy(h_ref, out_ref.at[c], local_sem)
        local.start()
        cp = pltpu.make_async_remote_copy(src_ref=h_ref, dst_ref=out_ref.at[c], send_sem=send_sem, recv_sem=recv_sem,
                                          device_id=(x, y, 1 - c), device_id_type=MESH)
        cp.start()
        landed = out_ref.at[1 - c]
        pltpu.make_async_remote_copy(src_ref=landed, dst_ref=landed, send_sem=send_sem, recv_sem=recv_sem,
                                     device_id=(x, y, 1 - c), device_id_type=MESH).wait_recv()
        cp.wait_send()
        local.wait()

    return pl.pallas_call(
        body, name="grad_sibling_share",
        in_specs=[HBM_SPEC], out_specs=HBM_SPEC,
        out_shape=jax.ShapeDtypeStruct((2, Rh, C), half.dtype),
        scratch_shapes=[pltpu.SemaphoreType.DMA, pltpu.SemaphoreType.DMA, pltpu.SemaphoreType.DMA],
        compiler_params=pltpu.CompilerParams(has_side_effects=True),
    )(half)


def _allreduce_small(part):
    rows, C = part.shape

    def body(p_ref, o_ref, slots, send_sems, recv_sems):
        x, y, c = _mesh_pos()
        me = 4 * x + 2 * y + c
        slots[me] = p_ref[...]
        copies = []
        for k in range(1, 8):
            kx, ky, kc = (k >> 2) & 1, (k >> 1) & 1, k & 1
            peer = (x ^ kx if kx else x, y ^ ky if ky else y, c ^ kc if kc else c)
            cp = pltpu.make_async_remote_copy(src_ref=p_ref, dst_ref=slots.at[me], send_sem=send_sems.at[k - 1],
                                              recv_sem=recv_sems.at[k - 1], device_id=peer, device_id_type=MESH)
            cp.start()
            copies.append((cp, peer))
        for k, (cp, peer) in enumerate(copies):
            src = 4 * peer[0] + 2 * peer[1] + peer[2]
            pltpu.make_async_remote_copy(src_ref=p_ref, dst_ref=slots.at[src], send_sem=send_sems.at[k],
                                         recv_sem=recv_sems.at[k], device_id=peer, device_id_type=MESH).wait_recv()
        for cp, _ in copies:
            cp.wait_send()
        total = slots[0]
        for d in range(1, 8):
            total = total + slots[d]
        o_ref[...] = total

    return pl.pallas_call(
        body, name="small_grad_allreduce",
        in_specs=[pl.BlockSpec(memory_space=pltpu.VMEM)], out_specs=pl.BlockSpec(memory_space=pltpu.VMEM),
        out_shape=jax.ShapeDtypeStruct((rows, C), F32),
        scratch_shapes=[pltpu.VMEM((8, rows, C), F32), pltpu.SemaphoreType.DMA((7,)), pltpu.SemaphoreType.DMA((7,))],
        compiler_params=pltpu.CompilerParams(has_side_effects=True, vmem_limit_bytes=VMEM_LIMIT_BYTES),
    )(part)


def _heads(x2d, H):
    S = x2d.shape[0]
    return x2d.reshape(S, H, x2d.shape[1] // H).transpose(1, 0, 2)


def _unheads(x3d):
    H, S, Dh = x3d.shape
    return x3d.transpose(1, 0, 2).reshape(S, H * Dh)


def _rows(col, T):
    H, S, _ = col.shape
    return col.reshape(H, S // T, 1, T)


def _permute_w_in(w):
    z = lambda n: jnp.zeros(w.shape[:-1] + (n,), w.dtype)
    return jnp.concatenate([w[..., :1536], w[..., 1540:1924], z(KR_LANE), w[..., 1924:1956], w[..., 1536:1540],
                            z(LANES - FL_LANE - FOX_HEADS)], axis=-1)


def _unpermute_w_in(g):
    return jnp.concatenate([g[..., :1536], g[..., MISC_COL + FL_LANE:MISC_COL + FL_LANE + FOX_HEADS], g[..., 1536:MISC_COL],
                            g[..., MISC_COL + KR_LANE:MISC_COL + KR_LANE + MLA_ROPE]], axis=-1)


def _pad_w_uq(w):
    lead = w.shape[:-1]
    w = w.reshape(lead + (MLA_HEADS, MLA_QK))
    w = jnp.concatenate([w, jnp.zeros(lead + (MLA_HEADS, LANES - MLA_QK), w.dtype)], axis=-1)
    return w.reshape(lead + (MLA_HEADS * LANES,))


def _unpad_w_uq(g):
    lead = g.shape[:-1]
    return g.reshape(lead + (MLA_HEADS, LANES))[..., :MLA_QK].reshape(lead + (MLA_HEADS * MLA_QK,))


BIG = ("w_in", "w_uq", "w_ukv", "w_out", "w_up", "conv_w", "w_down")
BIG_AXIS = {"w_in": 2, "w_uq": 2, "w_ukv": 2, "w_out": 1, "w_up": 2, "conv_w": 2, "w_down": 1}
SMALL = ("attn_pre_norm", "forget_bias", "swa_sinks", "rel_bias", "q_latent_norm", "kv_latent_norm", "group_norm",
         "attn_post_norm", "ffn_pre_norm", "conv_b", "ffn_post_norm")
WEIGHTS = ("attn_pre_norm", "w_in", "forget_bias", "swa_sinks", "rel_bias", "q_latent_norm", "w_uq", "kv_latent_norm",
           "w_ukv", "group_norm", "w_out", "attn_post_norm", "ffn_pre_norm", "w_up", "conv_w", "conv_b", "w_down",
           "ffn_post_norm")


def _pack(arrs, cols, row_mult):
    flat = jnp.concatenate([a.reshape(-1) for a in arrs])
    n = flat.shape[0]
    per = cols * row_mult
    total = -(-n // per) * per
    return jnp.pad(flat, (0, total - n)).reshape(total // cols, cols)


def _unpack(packed, shapes):
    flat = packed.reshape(-1)
    out, off = [], 0
    for shp in shapes:
        n = int(np.prod(shp))
        out.append(flat[off:off + n].reshape(shp))
        off += n
    return out


def _local_step(x, target, W):
    S = x.shape[0]
    T = _tile(S, ATT_TILE)
    tq_tabs, tm_tabs = _rope_tables(S)
    onehot_t = _rel_onehot_t()
    bias = _bias_table(W["rel_bias"].T, onehot_t).reshape(SWA_Q_HEADS, WINDOW, 2 * WINDOW)
    row = lambda a: a.reshape(1, -1)
    full_seg = ((0, D_MODEL),)
    group_segs = ((0, 512), (512, 256), (768, 256))
    lane_pad = lambda v4, lo: jnp.pad(v4, ((0, 0), (lo, LANES - lo - v4.shape[1])))

    saved = []
    h = _rms_fwd(x, row(W["attn_pre_norm"][0]), full_seg, name="rms_in")
    for l in range(DEPTH):
        sv = {"x0": x, "h1": h}
        proj = _matmul(h, W["w_in"][l], name="proj_in")
        sv["proj"] = proj
        qa, ka, va = (_heads(proj[:, 0:512].astype(BF16), 8), _heads(proj[:, 512:640].astype(BF16), 2),
                      _heads(proj[:, 640:768].astype(BF16), 2))
        oa, lse_a = _swa_fwd(qa, ka, va, bias, W["swa_sinks"][l], name="swa_fwd")
        qf, kf, vf = (_heads(proj[:, 768:1024].astype(BF16), 4), _heads(proj[:, 1024:1280].astype(BF16), 4),
                      _heads(proj[:, 1280:1536].astype(BF16), 4))
        fb_slab = lane_pad(row(W["forget_bias"][l]), FL_LANE)
        f_slab = _gate_fwd(proj, fb_slab, name="fox_gate_fwd")
        f_col = f_slab[:, FL_LANE:FL_LANE + FOX_HEADS].T[:, :, None]
        f_row = _rows(f_col, T)
        of, lse_f = _attn_fwd(qf, kf, vf, HEAD_DIM ** -0.5, f_col, f_row, name="fox_fwd")
        nq, nkv, q_s, kv_s, kr_s = _mla_prep_fwd(proj, row(W["q_latent_norm"][l]), row(W["kv_latent_norm"][l]),
                                                 W["w_uq"][l], W["w_ukv"][l], tq_tabs, tm_tabs, name="mla_prep_fwd")
        qm = _heads(q_s, 4)
        kv4 = _heads(kv_s, 4)
        km = jnp.concatenate([kv4[:, :, :MLA_NOPE], jnp.broadcast_to(kr_s[None, :, KR_LANE:], (4, S, LANES - KR_LANE))], axis=-1)
        vm = kv4[:, :, MLA_NOPE:]
        oc, lse_c = _attn_fwd(qm, km, vm, MLA_QK ** -0.5, name="mla_fwd")
        out_cat = jnp.concatenate([_unheads(oa), _unheads(of), _unheads(oc)], axis=-1)
        mixed = _rms_fwd(out_cat, row(W["group_norm"][l]), group_segs, name="group_norm_fwd")
        y = _matmul(mixed, W["w_out"][l], name="proj_out")
        x1, h2 = _resid_rms(x, y, row(W["attn_post_norm"][l]), row(W["ffn_pre_norm"][l]), name="attn_resid")
        a = _matmul(h2, W["w_up"][l], name="ffn_up")
        z = _conv_geglu_fwd(a, W["conv_w"][l], row(W["conv_b"][l]), name="conv_geglu_fwd")
        y2 = _matmul(z, W["w_down"][l], name="ffn_down")
        g_next = row(W["attn_pre_norm"][l + 1]) if l + 1 < DEPTH else None
        x2, h_next = _resid_rms(x1, y2, row(W["ffn_post_norm"][l]), g_next, name="ffn_resid")
        sv.update(qa=qa, ka=ka, va=va, oa=oa, lse_a=lse_a, qf=qf, kf=kf, vf=vf, f_row=f_row, f_col=f_col, of=of, lse_f=lse_f,
                  fb_slab=fb_slab, nq=nq, nkv=nkv, qm=qm, km=km, vm=vm, oc=oc, lse_c=lse_c, out_cat=out_cat, mixed=mixed,
                  y=y, x1=x1, h2=h2, a=a, z=z, y2=y2)
        saved.append(sv)
        x, h = x2, h_next

    loss, dx = _loss_head(x, target)

    G = {k: [None] * DEPTH for k in WEIGHTS if k != "rel_bias"}
    dbias_layers = [None] * DEPTH
    for l in reversed(range(DEPTH)):
        sv = saved[l]
        dy2, dg = _rms_bwd(sv["y2"], row(W["ffn_post_norm"][l]), dx, full_seg, out_dtype=BF16, name="ffn_post_bwd")
        G["ffn_post_norm"][l] = dg[0]
        dz = _matmul(dy2, W["w_down"][l], tb=True, name="ffn_down_dx")
        G["w_down"][l] = _matmul(sv["z"], dy2, ta=True, name="ffn_down_dw")
        du, dcw, dcb = _conv_geglu_bwd(sv["a"], W["conv_w"][l], row(W["conv_b"][l]), dz, name="conv_geglu_bwd")
        G["conv_w"][l] = dcw.transpose(1, 0, 2).reshape(3, 2 * D_FF)
        G["conv_b"][l] = dcb.reshape(2 * D_FF)
        da = _conv_bwd_input(du, W["conv_w"][l], name="conv_bwd_input")
        dh2 = _matmul(da, W["w_up"][l], tb=True, name="ffn_up_dx")
        G["w_up"][l] = _matmul(sv["h2"], da, ta=True, name="ffn_up_dw")
        dx1, dg = _rms_bwd(sv["x1"], row(W["ffn_pre_norm"][l]), dh2, full_seg, resid=dx, out_dtype=F32, name="ffn_pre_bwd")
        G["ffn_pre_norm"][l] = dg[0]
        dy, dg = _rms_bwd(sv["y"], row(W["attn_post_norm"][l]), dx1, full_seg, out_dtype=BF16, name="attn_post_bwd")
        G["attn_post_norm"][l] = dg[0]
        dmixed = _matmul(dy, W["w_out"][l], tb=True, name="proj_out_dx")
        G["w_out"][l] = _matmul(sv["mixed"], dy, ta=True, name="proj_out_dw")
        dout, dg = _rms_bwd(sv["out_cat"], row(W["group_norm"][l]), dmixed, group_segs, out_dtype=BF16, name="group_norm_bwd")
        G["group_norm"][l] = dg[0]
        doa, dof, doc = _heads(dout[:, 0:512], 8), _heads(dout[:, 512:768], 4), _heads(dout[:, 768:1024], 4)
        dl_a = _attn_delta(sv["oa"], doa, name="swa_delta")
        dqa, dka, dva, dbias_l, dsink = _swa_bwd(sv["qa"], sv["ka"], sv["va"], bias, W["swa_sinks"][l], doa,
                                                 sv["lse_a"], dl_a, name="swa_bwd")
        dbias_layers[l] = dbias_l.reshape(SWA_Q_HEADS, -1)
        G["swa_sinks"][l] = dsink[:, 0]
        dl_f = _attn_delta(sv["of"], dof, name="fox_delta")
        dqf, dkf, dvf, dfk = _attn_bwd(sv["qf"], sv["kf"], sv["vf"], dof, _rows(sv["lse_f"], T), _rows(dl_f, T),
                                       HEAD_DIM ** -0.5, sv["f_row"], sv["f_col"], name="fox_bwd")
        dF = lane_pad(dfk[:, :, 0].T, FL_LANE)
        dmisc_f, dfb = _gate_bwd(sv["proj"], sv["fb_slab"], dF, name="fox_gate_bwd")
        G["forget_bias"][l] = dfb[0, FL_LANE:FL_LANE + FOX_HEADS]
        dl_c = _attn_delta(sv["oc"], doc, name="mla_delta")
        dqm, dkm, dvm = _attn_bwd(sv["qm"], sv["km"], sv["vm"], doc, _rows(sv["lse_c"], T), _rows(dl_c, T),
                                  MLA_QK ** -0.5, name="mla_bwd")
        dkv_s = _unheads(jnp.concatenate([dkm[:, :, :MLA_NOPE], dvm], axis=-1))
        dcq, dckv, dmisc_k, dwq, dwkv, dgq, dgkv = _mla_prep_bwd(
            sv["proj"], sv["nq"], sv["nkv"], row(W["q_latent_norm"][l]), row(W["kv_latent_norm"][l]), W["w_uq"][l],
            W["w_ukv"][l], tq_tabs, tm_tabs, _unheads(dqm), dkv_s, dkm, name="mla_prep_bwd")
        G["w_uq"][l], G["w_ukv"][l] = dwq, dwkv
        G["q_latent_norm"][l], G["kv_latent_norm"][l] = dgq[0], dgkv[0]
        dproj = jnp.concatenate([_unheads(dqa), _unheads(dka), _unheads(dva), _unheads(dqf), _unheads(dkf), _unheads(dvf),
                                 dcq, dckv, dmisc_k + dmisc_f], axis=-1).astype(BF16)
        dh1 = _matmul(dproj, W["w_in"][l], tb=True, name="proj_in_dx")
        G["w_in"][l] = _matmul(sv["h1"], dproj, ta=True, name="proj_in_dw")
        dx, dg = _rms_bwd(sv["x0"], row(W["attn_pre_norm"][l]), dh1, full_seg, resid=dx1, out_dtype=F32, name="attn_pre_bwd")
        G["attn_pre_norm"][l] = dg[0]

    grads = {k: jnp.stack(v) for k, v in G.items()}
    grads["rel_bias"] = _bias_table_bwd(jnp.stack(dbias_layers), onehot_t).T
    return loss, dx, grads


def _shard_blocks(full, axis):
    return jnp.split(full, 4, axis=axis)


def kernel(x, attn_pre_norm, w_in, forget_bias, swa_sinks, rel_bias, q_latent_norm, w_uq, kv_latent_norm, w_ukv, group_norm, w_out, attn_post_norm, ffn_pre_norm, w_up, conv_w, conv_b, w_down, ffn_post_norm, loss_target, m_attn_pre_norm, m_w_in, m_forget_bias, m_swa_sinks, m_rel_bias, m_q_latent_norm, m_w_uq, m_kv_latent_norm, m_w_ukv, m_group_norm, m_w_out, m_attn_post_norm, m_ffn_pre_norm, m_w_up, m_conv_w, m_conv_b, m_w_down, m_ffn_post_norm, v_attn_pre_norm, v_w_in, v_forget_bias, v_swa_sinks, v_rel_bias, v_q_latent_norm, v_w_uq, v_kv_latent_norm, v_w_ukv, v_group_norm, v_w_out, v_attn_post_norm, v_ffn_pre_norm, v_w_up, v_conv_w, v_conv_b, v_w_down, v_ffn_post_norm):
    args = dict(locals())
    w = {k: args[k] for k in WEIGHTS}
    m = {k: args["m_" + k] for k in WEIGHTS}
    v = {k: args["v_" + k] for k in WEIGHTS}
    big_shapes = [w[k].shape for k in BIG]
    row_mult = 2 * PACK_ROW_TILE

    as_sent = lambda k: lax.bitcast_convert_type(w[k], BF16) if k == "conv_w" else w[k].astype(BF16)
    sent_shapes = [s + (2,) if k == "conv_w" else s for k, s in zip(BIG, big_shapes)]
    gathered = _gather_shards(_pack([as_sent(k) for k in BIG], PACK_COLS, row_mult))
    shards = [_unpack(gathered[s], sent_shapes) for s in range(4)]
    full = {k: jnp.concatenate([shards[s][i] for s in range(4)], axis=BIG_AXIS[k]) for i, k in enumerate(BIG)}
    W = {k: w[k] for k in SMALL}
    W.update(w_in=_permute_w_in(full["w_in"]), w_uq=_pad_w_uq(full["w_uq"]), w_ukv=full["w_ukv"], w_out=full["w_out"],
             w_up=full["w_up"], conv_w=lax.bitcast_convert_type(full["conv_w"], F32), w_down=full["w_down"])

    loss_part, dx, g = _local_step(x[0], loss_target[0], W)
    loss = lax.psum(loss_part, ("x", "y", "c"))

    gfull = dict(g)
    gfull["w_in"] = _unpermute_w_in(g["w_in"])
    gfull["w_uq"] = _unpad_w_uq(g["w_uq"])
    per_shard = [_pack([_shard_blocks(gfull[k], BIG_AXIS[k])[s] for k in BIG], PACK_COLS, row_mult) for s in range(4)]
    gp = jnp.stack(per_shard)
    R = gp.shape[1]
    gp = gp.reshape(4, 2, R // 2, PACK_COLS)
    pair = _pair_sum(gp, _sibling_exchange(gp))
    mine = _chip_sum(_chip_scatter(pair))
    g_shard = _sibling_share(mine).reshape(R, PACK_COLS)
    d_b, m_b, v_b = _adamw(_pack([w[k] for k in BIG], PACK_COLS, row_mult), g_shard,
                           _pack([m[k] for k in BIG], PACK_COLS, row_mult), _pack([v[k] for k in BIG], PACK_COLS, row_mult),
                           name="adamw_big")
    out_g = dict(zip(BIG, _unpack(g_shard, big_shapes)))
    out_d = dict(zip(BIG, _unpack(d_b, big_shapes)))
    out_m = dict(zip(BIG, _unpack(m_b, big_shapes)))
    out_v = dict(zip(BIG, _unpack(v_b, big_shapes)))

    small_shapes = [w[k].shape for k in SMALL]
    g_small = _allreduce_small(_pack([g[k] for k in SMALL], LANES, 8))
    d_s, m_s, v_s = _adamw(_pack([w[k] for k in SMALL], LANES, 8), g_small, _pack([m[k] for k in SMALL], LANES, 8),
                           _pack([v[k] for k in SMALL], LANES, 8), name="adamw_small")
    out_g.update(zip(SMALL, _unpack(g_small, small_shapes)))
    out_d.update(zip(SMALL, _unpack(d_s, small_shapes)))
    out_m.update(zip(SMALL, _unpack(m_s, small_shapes)))
    out_v.update(zip(SMALL, _unpack(v_s, small_shapes)))

    return (loss, dx[None], *[out_g[k] for k in WEIGHTS], *[out_d[k] for k in WEIGHTS],
            *[out_m[k] for k in WEIGHTS], *[out_v[k] for k in WEIGHTS])
```

```python
import math

import numpy as np
import jax
import jax.numpy as jnp
from jax import lax
from jax.experimental import pallas as pl
from jax.experimental.pallas import tpu as pltpu

F32 = jnp.float32
BF16 = jnp.bfloat16

D_MODEL = 1024
DEPTH = 4
HEAD_DIM = 64
SWA_Q_HEADS = 8
SWA_KV_HEADS = 2
SWA_GROUP = SWA_Q_HEADS // SWA_KV_HEADS
WINDOW = 128
FOX_HEADS = 4
MLA_HEADS = 4
MLA_Q_RANK = 256
MLA_KV_RANK = 128
MLA_NOPE = 64
MLA_ROPE = 32
MLA_QK = MLA_NOPE + MLA_ROPE
ROPE_THETA = 10000.0
REL_BUCKETS = 32
REL_MAX_DIST = 128
D_FF = 2816
EPS = 1e-6
NEG_INF = -1e30
LANES = 128

QKV_ROWS = 1536
LAT_ROWS = 512
FOX_ROW0 = 768
KR_ROW = 64
FL_ROW = 96
MLA_PAD = LANES

ADAM_LR = 0.001
ADAM_B1 = 0.9
ADAM_B2 = 0.999
ADAM_EPS = 1e-08
ADAM_WD = 0.01
ADAM_STEP = 10

VMEM_LIMIT_BYTES = 48 * 1024 * 1024
ATT_TILE = 256
PACK_COLS = 1024
PACK_ROW_TILE = 256
MESH = pl.DeviceIdType.MESH

NT = (((1,), (1,)), ((), ()))
TN = (((0,), (0,)), ((), ()))
NN = (((1,), (0,)), ((), ()))


def _params(*sem):
    return pltpu.CompilerParams(dimension_semantics=sem, vmem_limit_bytes=VMEM_LIMIT_BYTES)


def _tile(dim, cap):
    for t in (2048, 1408, 1024, 512, 256, 128, 64, 32, 16, 8):
        if t <= cap and dim % t == 0:
            return t
    return dim


def _dot(a, b, dims=NN):
    return lax.dot_general(a, b, dims, preferred_element_type=F32)


def _split3(a):
    a1 = a.astype(BF16)
    r1 = a - a1.astype(F32)
    a2 = r1.astype(BF16)
    a3 = (r1 - a2.astype(F32)).astype(BF16)
    return a1, a2, a3


def _matmul(a, b, *, ta=False, tb=False, out_dtype=F32, name):
    if ta:
        K, M = a.shape
    else:
        M, K = a.shape
    if tb:
        N, K2 = b.shape
    else:
        K2, N = b.shape
    assert K == K2, (a.shape, b.shape)
    tm, tn, tk = _tile(M, 1408), _tile(N, 1408), _tile(K, 1408)
    nk = K // tk
    dims = (((0 if ta else 1,), (1 if tb else 0,)), ((), ()))

    def body(a_ref, b_ref, o_ref, acc_ref):
        k = pl.program_id(2)

        @pl.when(k == 0)
        def _():
            acc_ref[...] = jnp.zeros_like(acc_ref)

        acc_ref[...] += lax.dot_general(a_ref[...], b_ref[...], dims, preferred_element_type=F32)

        @pl.when(k == nk - 1)
        def _():
            o_ref[...] = acc_ref[...].astype(o_ref.dtype)

    a_spec = pl.BlockSpec((tk, tm), lambda i, j, k: (k, i)) if ta else pl.BlockSpec((tm, tk), lambda i, j, k: (i, k))
    b_spec = pl.BlockSpec((tn, tk), lambda i, j, k: (j, k)) if tb else pl.BlockSpec((tk, tn), lambda i, j, k: (k, j))
    return pl.pallas_call(
        body, name=name, grid=(M // tm, N // tn, nk),
        in_specs=[a_spec, b_spec],
        out_specs=pl.BlockSpec((tm, tn), lambda i, j, k: (i, j)),
        out_shape=jax.ShapeDtypeStruct((M, N), out_dtype),
        scratch_shapes=[pltpu.VMEM((tm, tn), F32)],
        compiler_params=_params("parallel", "parallel", "arbitrary"),
    )(a, b)


def _seg_rms(xs, g):
    r = lax.rsqrt(jnp.mean(xs * xs, axis=-1, keepdims=True) + EPS)
    return xs * r * g


def _seg_rms_bwd(xs, g, dy):
    r = lax.rsqrt(jnp.mean(xs * xs, axis=-1, keepdims=True) + EPS)
    gd = dy * g
    c = jnp.mean(gd * xs, axis=-1, keepdims=True)
    dx = r * gd - xs * (r * r * r * c)
    dg = jnp.sum(dy * (xs * r), axis=0, keepdims=True)
    return dx, dg


def _rms_fwd(x, g, *, name):
    S, W = x.shape
    tm = _tile(S, 512)

    def body(x_ref, g_ref, o_ref):
        o_ref[...] = _seg_rms(x_ref[...], g_ref[...]).astype(o_ref.dtype)

    return pl.pallas_call(
        body, name=name, grid=(S // tm,),
        in_specs=[pl.BlockSpec((tm, W), lambda i: (i, 0)), pl.BlockSpec((1, W), lambda i: (0, 0))],
        out_specs=pl.BlockSpec((tm, W), lambda i: (i, 0)),
        out_shape=jax.ShapeDtypeStruct((S, W), BF16),
        compiler_params=_params("parallel"),
    )(x, g)


def _rms_bwd(x, g, dy, *, resid=None, out_dtype, name):
    S, W = x.shape
    tm = _tile(S, 512)
    has_resid = resid is not None

    def body(*refs):
        if has_resid:
            x_ref, g_ref, dy_ref, r_ref, dx_ref, dg_ref = refs
        else:
            x_ref, g_ref, dy_ref, dx_ref, dg_ref = refs

        @pl.when(pl.program_id(0) == 0)
        def _():
            dg_ref[...] = jnp.zeros_like(dg_ref)

        dx, dg = _seg_rms_bwd(x_ref[...], g_ref[...], dy_ref[...])
        if has_resid:
            dx = dx + r_ref[...]
        dx_ref[...] = dx.astype(dx_ref.dtype)
        dg_ref[...] += dg

    row = pl.BlockSpec((tm, W), lambda i: (i, 0))
    vec = pl.BlockSpec((1, W), lambda i: (0, 0))
    ins = [x, g, dy] + ([resid] if has_resid else [])
    return pl.pallas_call(
        body, name=name, grid=(S // tm,),
        in_specs=[row, vec, row] + ([row] if has_resid else []),
        out_specs=[row, vec],
        out_shape=[jax.ShapeDtypeStruct((S, W), out_dtype), jax.ShapeDtypeStruct((1, W), F32)],
        compiler_params=_params("arbitrary"),
    )(*ins)


def _resid_rms(x, y, g_post, g_next, *, name):
    S, W = x.shape
    tm = _tile(S, 512)
    with_next = g_next is not None

    def body(*refs):
        if with_next:
            x_ref, y_ref, gp_ref, gn_ref, xo_ref, h_ref = refs
        else:
            x_ref, y_ref, gp_ref, xo_ref = refs
        xn = x_ref[...] + _seg_rms(y_ref[...], gp_ref[...])
        xo_ref[...] = xn
        if with_next:
            h_ref[...] = _seg_rms(xn, gn_ref[...]).astype(BF16)

    row = pl.BlockSpec((tm, W), lambda i: (i, 0))
    vec = pl.BlockSpec((1, W), lambda i: (0, 0))
    outs = [jax.ShapeDtypeStruct((S, W), F32)] + ([jax.ShapeDtypeStruct((S, W), BF16)] if with_next else [])
    res = pl.pallas_call(
        body, name=name, grid=(S // tm,),
        in_specs=[row, row, vec] + ([vec] if with_next else []),
        out_specs=[row] + ([row] if with_next else []),
        out_shape=outs,
        compiler_params=_params("parallel"),
    )(*([x, y, g_post] + ([g_next] if with_next else [])))
    return (res[0], res[1]) if with_next else (res[0], None)


def _col_rms(xs, g):
    r = lax.rsqrt(jnp.mean(xs * xs, axis=0, keepdims=True) + EPS)
    return xs * r * g


def _col_rms_bwd(xs, g, dy):
    r = lax.rsqrt(jnp.mean(xs * xs, axis=0, keepdims=True) + EPS)
    gd = dy * g
    c = jnp.mean(gd * xs, axis=0, keepdims=True)
    dx = r * gd - xs * (r * r * r * c)
    dg = jnp.sum(dy * (xs * r), axis=1, keepdims=True)
    return dx, dg


GROUP_ROWS = (SWA_Q_HEADS * HEAD_DIM, FOX_HEADS * HEAD_DIM, MLA_HEADS * HEAD_DIM)


def _group_specs(S, tn):
    outs = [pl.BlockSpec((n, tn), lambda i: (0, i)) for n in GROUP_ROWS]
    g = pl.BlockSpec((D_MODEL, 1), lambda i: (0, 0))
    mixed = pl.BlockSpec((D_MODEL, tn), lambda i: (0, i))
    return outs, g, mixed


def _group_norm_fwd(oa, of, oc, g, *, name):
    S = oa.shape[1]
    tn = _tile(S, 512)
    outs, gs, mixed = _group_specs(S, tn)

    def body(a_ref, f_ref, c_ref, g_ref, o_ref):
        r0 = 0
        for ref, n in zip((a_ref, f_ref, c_ref), GROUP_ROWS):
            o_ref[r0:r0 + n, :] = _col_rms(ref[...], g_ref[r0:r0 + n, :]).astype(BF16)
            r0 += n

    return pl.pallas_call(
        body, name=name, grid=(S // tn,),
        in_specs=outs + [gs], out_specs=mixed,
        out_shape=jax.ShapeDtypeStruct((D_MODEL, S), BF16),
        compiler_params=_params("parallel"),
    )(oa, of, oc, g)


def _group_norm_bwd(oa, of, oc, g, dmixed, *, name):
    S = oa.shape[1]
    tn = _tile(S, 512)
    outs, gs, mixed = _group_specs(S, tn)

    def body(a_ref, f_ref, c_ref, g_ref, dm_ref, da_ref, df_ref, dc_ref, dg_ref):
        @pl.when(pl.program_id(0) == 0)
        def _():
            dg_ref[...] = jnp.zeros_like(dg_ref)

        r0 = 0
        for ref, dref, n in zip((a_ref, f_ref, c_ref), (da_ref, df_ref, dc_ref), GROUP_ROWS):
            dx, dg = _col_rms_bwd(ref[...], g_ref[r0:r0 + n, :], dm_ref[r0:r0 + n, :])
            dref[...] = dx.astype(BF16)
            dg_ref[r0:r0 + n, :] += dg
            r0 += n

    return pl.pallas_call(
        body, name=name, grid=(S // tn,),
        in_specs=outs + [gs, mixed], out_specs=outs + [gs],
        out_shape=[jax.ShapeDtypeStruct((n, S), BF16) for n in GROUP_ROWS] + [jax.ShapeDtypeStruct((D_MODEL, 1), F32)],
        compiler_params=_params("arbitrary"),
    )(oa, of, oc, g, dmixed)


def _loss_head(y, target):
    S, W = y.shape
    tm = _tile(S, 512)

    def body(y_ref, t_ref, d_ref, l_ref):
        @pl.when(pl.program_id(0) == 0)
        def _():
            l_ref[...] = jnp.zeros_like(l_ref)

        err = y_ref[...] - t_ref[...]
        d_ref[...] = err * (1.0 / W)
        l_ref[...] += 0.5 * jnp.sum(jnp.mean(err * err, axis=-1, keepdims=True), axis=0, keepdims=True)

    row = pl.BlockSpec((tm, W), lambda i: (i, 0))
    d, l = pl.pallas_call(
        body, name="loss_head", grid=(S // tm,),
        in_specs=[row, row],
        out_specs=[row, pl.BlockSpec((1, 1), lambda i: (0, 0))],
        out_shape=[jax.ShapeDtypeStruct((S, W), F32), jax.ShapeDtypeStruct((1, 1), F32)],
        compiler_params=_params("arbitrary"),
    )(y, target)
    return l[0, 0], d


def _attn_fwd(q_src, k_src, v_src, rows, H, Dk, Dv, scale, f_row=None, f_col=None, *, name):
    S = q_src.shape[1]
    T = _tile(S, ATT_TILE)
    nq = S // T
    forget = f_row is not None
    qb, kb, vb = rows[0] // Dk, rows[1] // Dk, rows[2] // Dv

    def body(*refs):
        if forget:
            q_ref, k_ref, v_ref, fq_ref, fk_ref, o_ref, lse_ref, m_s, l_s, acc_s = refs
        else:
            q_ref, k_ref, v_ref, o_ref, lse_ref, m_s, l_s, acc_s = refs
        i = pl.program_id(1)
        m_s[...] = jnp.full_like(m_s, NEG_INF)
        l_s[...] = jnp.zeros_like(l_s)
        acc_s[...] = jnp.zeros_like(acc_s)
        qi = q_ref[...]

        def tile(j, masked):
            off = pl.multiple_of(j * T, T)
            kj = k_ref[:, pl.ds(off, T)]
            vj = v_ref[:, pl.ds(off, T)]
            s = _dot(kj, qi, TN) * scale
            if forget:
                s = s + (fq_ref[0] - fk_ref[0, pl.ds(off, T), :])
            if masked:
                r = lax.broadcasted_iota(jnp.int32, (T, T), 0)
                c = lax.broadcasted_iota(jnp.int32, (T, T), 1)
                s = jnp.where(r <= c, s, NEG_INF)
            m_prev = m_s[...]
            m_new = jnp.maximum(m_prev, jnp.max(s, axis=0, keepdims=True))
            alpha = jnp.exp(m_prev - m_new)
            p = jnp.exp(s - m_new)
            l_s[...] = alpha * l_s[...] + jnp.sum(p, axis=0, keepdims=True)
            p_hi = p.astype(BF16)
            pv = _dot(vj, p_hi)
            if forget:
                pv = pv + _dot(vj, (p - p_hi.astype(F32)).astype(BF16))
            acc_s[...] = alpha * acc_s[...] + pv
            m_s[...] = m_new

        def loop_body(j, carry):
            tile(j, False)
            return carry

        lax.fori_loop(0, i, loop_body, 0)
        tile(i, True)
        o_ref[...] = acc_s[...] / l_s[...]
        lse_ref[0] = m_s[...] + jnp.log(l_s[...])

    in_specs = [pl.BlockSpec((Dk, T), lambda h, i: (qb + h, i)),
                pl.BlockSpec((Dk, S), lambda h, i: (kb + h, 0)),
                pl.BlockSpec((Dv, S), lambda h, i: (vb + h, 0))]
    ins = [q_src, k_src, v_src]
    if forget:
        in_specs += [pl.BlockSpec((1, 1, T), lambda h, i: (h, 0, i)), pl.BlockSpec((1, S, 1), lambda h, i: (h, 0, 0))]
        ins += [f_row, f_col]
    return pl.pallas_call(
        body, name=name, grid=(H, nq),
        in_specs=in_specs,
        out_specs=[pl.BlockSpec((Dv, T), lambda h, i: (h, i)), pl.BlockSpec((1, 1, T), lambda h, i: (h, 0, i))],
        out_shape=[jax.ShapeDtypeStruct((H * Dv, S), F32), jax.ShapeDtypeStruct((H, 1, S), F32)],
        scratch_shapes=[pltpu.VMEM((1, T), F32), pltpu.VMEM((1, T), F32), pltpu.VMEM((Dv, T), F32)],
        compiler_params=_params("parallel", "arbitrary"),
    )(*ins)


def _attn_bwd(q_src, k_src, v_src, rows, H, Dk, Dv, scale, o, do, lse, f_row=None, f_col=None, *, name):
    S = q_src.shape[1]
    T = _tile(S, ATT_TILE)
    nq = S // T
    forget = f_row is not None
    qb, kb, vb = rows[0] // Dk, rows[1] // Dk, rows[2] // Dv

    def body(*refs):
        if forget:
            (q_ref, k_ref, v_ref, o_ref, do_ref, lse_ref, fq_ref, fk_ref,
             dq_ref, dk_ref, dv_ref, df_ref, dk_s, dv_s, df_s) = refs
        else:
            q_ref, k_ref, v_ref, o_ref, do_ref, lse_ref, dq_ref, dk_ref, dv_ref, dk_s, dv_s = refs
        j = pl.program_id(1)

        @pl.when(j == 0)
        def _():
            dq_ref[...] = jnp.zeros_like(dq_ref)

        dk_s[...] = jnp.zeros_like(dk_s)
        dv_s[...] = jnp.zeros_like(dv_s)
        if forget:
            df_s[...] = jnp.zeros_like(df_s)
        kt = k_ref[...]
        kj = kt.T
        vj = v_ref[...].T

        def tile(i, masked):
            cols = pl.ds(pl.multiple_of(i * T, T), T)
            qi = q_ref[:, cols]
            doi = do_ref[:, cols]
            delta = jnp.sum(o_ref[:, cols] * doi.astype(F32), axis=0, keepdims=True)
            st = _dot(kj, qi) * scale
            if forget:
                st = st + (fq_ref[0, :, cols] - fk_ref[0])
            if masked:
                r = lax.broadcasted_iota(jnp.int32, (T, T), 0)
                c = lax.broadcasted_iota(jnp.int32, (T, T), 1)
                st = jnp.where(r <= c, st, NEG_INF)
            pt = jnp.exp(st - lse_ref[0, :, cols])
            dv_s[...] += _dot(doi, pt.astype(BF16), NT)
            dst = pt * (_dot(vj, doi) - delta)
            dsb = dst.astype(BF16)
            dk_s[...] += _dot(qi, dsb, NT)
            dq_ref[:, cols] += _dot(kt, dsb) * scale
            if forget:
                part = dst[:, 0:LANES]
                for c0 in range(LANES, T, LANES):
                    part = part + dst[:, c0:c0 + LANES]
                df_s[...] += part

        tile(j, True)

        def loop_body(i, carry):
            tile(i, False)
            return carry

        lax.fori_loop(j + 1, nq, loop_body, 0)
        dk_ref[...] = dk_s[...] * scale
        dv_ref[...] = dv_s[...]
        if forget:
            df_ref[0] = -jnp.sum(df_s[...], axis=-1, keepdims=True)

    res = lambda D, b0: pl.BlockSpec((D, S), lambda h, j: (b0 + h, 0))
    blk = lambda D, b0: pl.BlockSpec((D, T), lambda h, j: (b0 + h, j))
    row3 = pl.BlockSpec((1, 1, S), lambda h, j: (h, 0, 0))
    in_specs = [res(Dk, qb), blk(Dk, kb), blk(Dv, vb), res(Dv, 0), res(Dv, 0), row3]
    ins = [q_src, k_src, v_src, o, do, lse]
    out_specs = [res(Dk, 0), blk(Dk, 0), blk(Dv, 0)]
    out_shape = [jax.ShapeDtypeStruct((H * Dk, S), F32), jax.ShapeDtypeStruct((H * Dk, S), F32),
                 jax.ShapeDtypeStruct((H * Dv, S), F32)]
    scratch = [pltpu.VMEM((Dk, T), F32), pltpu.VMEM((Dv, T), F32)]
    if forget:
        in_specs += [row3, pl.BlockSpec((1, T, 1), lambda h, j: (h, j, 0))]
        ins += [f_row, f_col]
        out_specs.append(pl.BlockSpec((1, T, 1), lambda h, j: (h, j, 0)))
        out_shape.append(jax.ShapeDtypeStruct((H, S, 1), F32))
        scratch.append(pltpu.VMEM((T, min(T, LANES)), F32))
    return pl.pallas_call(
        body, name=name, grid=(H, nq),
        in_specs=in_specs, out_specs=out_specs, out_shape=out_shape, scratch_shapes=scratch,
        compiler_params=_params("parallel", "arbitrary"),
    )(*ins)


def _swa_masks(i):
    r = lax.broadcasted_iota(jnp.int32, (WINDOW, WINDOW), 0)
    c = lax.broadcasted_iota(jnp.int32, (WINDOW, WINDOW), 1)
    return (r > c) & (i > 0), r <= c


def _swa_specs():
    W = WINDOW
    kv_rows = SWA_KV_HEADS * HEAD_DIM
    q = pl.BlockSpec((SWA_Q_HEADS * HEAD_DIM, W), lambda i: (0, i))
    prev = lambda b: pl.BlockSpec((kv_rows, W), lambda i: (b, jnp.maximum(i - 1, 0)))
    cur = lambda b: pl.BlockSpec((kv_rows, W), lambda i: (b, i))
    bias = pl.BlockSpec((SWA_Q_HEADS, 2 * W, W), lambda i: (0, 0, 0))
    stat = pl.BlockSpec((SWA_Q_HEADS, W), lambda i: (0, i))
    sink = pl.BlockSpec(memory_space=pltpu.SMEM)
    return q, prev(4), cur(4), prev(5), cur(5), bias, stat, sink


def _swa_scores(h, q_ref, kp_ref, kc_ref, b_ref, masks):
    g = h // SWA_GROUP
    rows = slice(g * HEAD_DIM, (g + 1) * HEAD_DIM)
    qh = q_ref[h * HEAD_DIM:(h + 1) * HEAD_DIM, :]
    scale = HEAD_DIM ** -0.5
    s_p = jnp.where(masks[0], _dot(kp_ref[rows, :], qh, TN) * scale + b_ref[h, 0:WINDOW, :], NEG_INF)
    s_c = jnp.where(masks[1], _dot(kc_ref[rows, :], qh, TN) * scale + b_ref[h, WINDOW:2 * WINDOW, :], NEG_INF)
    return qh, rows, s_p, s_c


def _swa_fwd(qkv, bias_t, sinks, *, name):
    S = qkv.shape[1]
    qs, kp, kc, vp, vc, bs, stat, sk = _swa_specs()

    def body(sink_ref, q_ref, kp_ref, kc_ref, vp_ref, vc_ref, b_ref, o_ref, lse_ref):
        masks = _swa_masks(pl.program_id(0))
        for h in range(SWA_Q_HEADS):
            qh, rows, s_p, s_c = _swa_scores(h, q_ref, kp_ref, kc_ref, b_ref, masks)
            sink = sink_ref[h]
            m = jnp.maximum(jnp.maximum(jnp.max(s_p, axis=0, keepdims=True), jnp.max(s_c, axis=0, keepdims=True)), sink)
            p_p = jnp.exp(s_p - m)
            p_c = jnp.exp(s_c - m)
            l = jnp.sum(p_p, axis=0, keepdims=True) + jnp.sum(p_c, axis=0, keepdims=True) + jnp.exp(sink - m)
            o = _dot(vp_ref[rows, :], p_p.astype(BF16)) + _dot(vc_ref[rows, :], p_c.astype(BF16))
            o_ref[h * HEAD_DIM:(h + 1) * HEAD_DIM, :] = o / l
            lse_ref[h:h + 1, :] = m + jnp.log(l)

    return pl.pallas_call(
        body, name=name, grid=(S // WINDOW,),
        in_specs=[sk, qs, kp, kc, vp, vc, bs],
        out_specs=[qs, stat],
        out_shape=[jax.ShapeDtypeStruct((SWA_Q_HEADS * HEAD_DIM, S), F32), jax.ShapeDtypeStruct((SWA_Q_HEADS, S), F32)],
        compiler_params=_params("parallel"),
    )(sinks, qkv, qkv, qkv, qkv, qkv, bias_t)


def _swa_bwd(qkv, bias_t, sinks, o, do, lse, *, name):
    S = qkv.shape[1]
    W = WINDOW
    qs, kp, kc, vp, vc, bs, stat, sk = _swa_specs()
    scale = HEAD_DIM ** -0.5
    kv_rows = SWA_KV_HEADS * HEAD_DIM

    def body(sink_ref, q_ref, kp_ref, kc_ref, vp_ref, vc_ref, b_ref, o_ref, do_ref, lse_ref,
             dq_ref, dkv_ref, db_ref, dsk_ref):
        i = pl.program_id(0)

        @pl.when(i == 0)
        def _():
            dkv_ref[...] = jnp.zeros_like(dkv_ref)
            db_ref[...] = jnp.zeros_like(db_ref)
            dsk_ref[...] = jnp.zeros_like(dsk_ref)

        masks = _swa_masks(i)
        prev = pl.ds(pl.multiple_of(jnp.maximum(i - 1, 0) * W, W), W)
        cur = pl.ds(pl.multiple_of(i * W, W), W)
        for h in range(SWA_Q_HEADS):
            qh, rows, s_p, s_c = _swa_scores(h, q_ref, kp_ref, kc_ref, b_ref, masks)
            vrows = slice(kv_rows + rows.start, kv_rows + rows.stop)
            hrows = slice(h * HEAD_DIM, (h + 1) * HEAD_DIM)
            doh = do_ref[hrows, :]
            lse_h = lse_ref[h:h + 1, :]
            delta = jnp.sum(o_ref[hrows, :] * doh.astype(F32), axis=0, keepdims=True)
            p_p = jnp.exp(s_p - lse_h)
            p_c = jnp.exp(s_c - lse_h)
            ds_p = p_p * (_dot(vp_ref[rows, :], doh, TN) - delta)
            ds_c = p_c * (_dot(vc_ref[rows, :], doh, TN) - delta)
            db_ref[h, 0:W, :] += ds_p
            db_ref[h, W:2 * W, :] += ds_c
            dsk = -jnp.sum(jnp.exp(sink_ref[h] - lse_h) * delta, axis=1, keepdims=True)
            dsk_ref[h:h + 1, :] += jnp.broadcast_to(dsk, (1, LANES))
            dsb_p = ds_p.astype(BF16)
            dsb_c = ds_c.astype(BF16)
            dq_ref[hrows, :] = (_dot(kp_ref[rows, :], dsb_p) + _dot(kc_ref[rows, :], dsb_c)) * scale
            dkv_ref[rows, prev] += _dot(qh, dsb_p, NT) * scale
            dkv_ref[rows, cur] += _dot(qh, dsb_c, NT) * scale
            dkv_ref[vrows, prev] += _dot(doh, p_p.astype(BF16), NT)
            dkv_ref[vrows, cur] += _dot(doh, p_c.astype(BF16), NT)

    return pl.pallas_call(
        body, name=name, grid=(S // W,),
        in_specs=[sk, qs, kp, kc, vp, vc, bs, qs, qs, stat],
        out_specs=[qs, pl.BlockSpec((2 * kv_rows, S), lambda i: (0, 0)), bs, pl.BlockSpec((SWA_Q_HEADS, LANES), lambda i: (0, 0))],
        out_shape=[jax.ShapeDtypeStruct((SWA_Q_HEADS * HEAD_DIM, S), F32), jax.ShapeDtypeStruct((2 * kv_rows, S), F32),
                   jax.ShapeDtypeStruct((SWA_Q_HEADS, 2 * W, W), F32), jax.ShapeDtypeStruct((SWA_Q_HEADS, LANES), F32)],
        compiler_params=_params("arbitrary"),
    )(sinks, qkv, qkv, qkv, qkv, qkv, bias_t, o, do, lse)


def _rel_onehot_t():
    qi = jnp.arange(WINDOW, dtype=jnp.int32)[None, :] + WINDOW
    kj = jnp.arange(2 * WINDOW, dtype=jnp.int32)[:, None]
    dist = qi - kj
    max_exact = REL_BUCKETS // 2
    d = jnp.maximum(dist, 0)
    log_ratio = jnp.log(jnp.maximum(d, 1).astype(F32) / max_exact) / math.log(REL_MAX_DIST / max_exact)
    large = jnp.minimum(max_exact + (log_ratio * (REL_BUCKETS - max_exact)).astype(jnp.int32), REL_BUCKETS - 1)
    bucket = jnp.where(d < max_exact, d, large).reshape(-1)
    return (bucket[None, :] == jnp.arange(REL_BUCKETS, dtype=jnp.int32)[:, None]).astype(BF16)


def _bias_table(rel_bias_t, onehot_t):
    Hq, NB = rel_bias_t.shape
    N = onehot_t.shape[1]
    tn = _tile(N, 4096)

    def body(r_ref, oh_ref, o_ref):
        oh = oh_ref[...]
        a1, a2, a3 = _split3(r_ref[...])
        o_ref[...] = _dot(a1, oh) + _dot(a2, oh) + _dot(a3, oh)

    return pl.pallas_call(
        body, name="rel_bias_table", grid=(N // tn,),
        in_specs=[pl.BlockSpec((Hq, NB), lambda j: (0, 0)), pl.BlockSpec((NB, tn), lambda j: (0, j))],
        out_specs=pl.BlockSpec((Hq, tn), lambda j: (0, j)),
        out_shape=jax.ShapeDtypeStruct((Hq, N), F32),
        compiler_params=_params("parallel"),
    )(rel_bias_t, onehot_t)


def _bias_table_bwd(dbias, onehot_t):
    L, Hq, N = dbias.shape
    NB = onehot_t.shape[0]
    tn = _tile(N, 4096)

    def body(d_ref, oh_ref, o_ref):
        @pl.when(pl.program_id(0) == 0)
        def _():
            o_ref[...] = jnp.zeros_like(o_ref)

        d = d_ref[0]
        for l in range(1, L):
            d = d + d_ref[l]
        oh = oh_ref[...]
        a1, a2, a3 = _split3(d)
        o_ref[...] += _dot(a1, oh, NT) + _dot(a2, oh, NT) + _dot(a3, oh, NT)

    return pl.pallas_call(
        body, name="rel_bias_bwd", grid=(N // tn,),
        in_specs=[pl.BlockSpec((L, Hq, tn), lambda j: (0, 0, j)), pl.BlockSpec((NB, tn), lambda j: (0, j))],
        out_specs=pl.BlockSpec((Hq, NB), lambda j: (0, 0)),
        out_shape=jax.ShapeDtypeStruct((Hq, NB), F32),
        compiler_params=_params("arbitrary"),
    )(dbias, onehot_t)


MISC_BLOCK = (LAT_ROWS - LANES) // LANES


def _gate_fwd(lat, fb_col, *, name):
    S = lat.shape[1]
    tn = _tile(S, 256)

    def body(z_ref, fb_ref, o_ref, carry):
        @pl.when(pl.program_id(0) == 0)
        def _():
            carry[...] = jnp.zeros_like(carry)

        z = z_ref[...] + fb_ref[...]
        lf = jnp.minimum(z, 0.0) - jnp.log1p(jnp.exp(-jnp.abs(z)))
        r = lax.broadcasted_iota(jnp.int32, (tn, tn), 0)
        c = lax.broadcasted_iota(jnp.int32, (tn, tn), 1)
        tri = (r <= c).astype(BF16)
        a1, a2, a3 = _split3(lf)
        cum = _dot(a1, tri) + _dot(a2, tri) + _dot(a3, tri) + carry[:, 0:1]
        o_ref[...] = cum
        carry[...] = jnp.broadcast_to(cum[:, tn - 1:tn], carry.shape)

    return pl.pallas_call(
        body, name=name, grid=(S // tn,),
        in_specs=[pl.BlockSpec((LANES, tn), lambda i: (MISC_BLOCK, i)), pl.BlockSpec((LANES, 1), lambda i: (0, 0))],
        out_specs=pl.BlockSpec((LANES, tn), lambda i: (0, i)),
        out_shape=jax.ShapeDtypeStruct((LANES, S), F32),
        scratch_shapes=[pltpu.VMEM((LANES, LANES), F32)],
        compiler_params=_params("arbitrary"),
    )(lat, fb_col)


def _gate_bwd(lat, fb_col, dF, *, name):
    S = lat.shape[1]
    tn = _tile(S, 256)
    nt = S // tn

    def body(z_ref, fb_ref, df_ref, dz_ref, dfb_ref, carry):
        @pl.when(pl.program_id(0) == 0)
        def _():
            carry[...] = jnp.zeros_like(carry)
            dfb_ref[...] = jnp.zeros_like(dfb_ref)

        r = lax.broadcasted_iota(jnp.int32, (tn, tn), 0)
        c = lax.broadcasted_iota(jnp.int32, (tn, tn), 1)
        tri = (r >= c).astype(BF16)
        a1, a2, a3 = _split3(df_ref[...])
        dlf = _dot(a1, tri) + _dot(a2, tri) + _dot(a3, tri) + carry[:, 0:1]
        carry[...] = jnp.broadcast_to(dlf[:, 0:1], carry.shape)
        z = z_ref[...] + fb_ref[...]
        row = lax.broadcasted_iota(jnp.int32, (LANES, tn), 0)
        keep = (row >= FL_ROW) & (row < FL_ROW + FOX_HEADS)
        dz = jnp.where(keep, dlf / (1.0 + jnp.exp(z)), 0.0)
        dz_ref[...] = dz
        dfb_ref[...] += jnp.sum(dz, axis=1, keepdims=True)

    return pl.pallas_call(
        body, name=name, grid=(nt,),
        in_specs=[pl.BlockSpec((LANES, tn), lambda i: (MISC_BLOCK, nt - 1 - i)), pl.BlockSpec((LANES, 1), lambda i: (0, 0)),
                  pl.BlockSpec((LANES, tn), lambda i: (0, nt - 1 - i))],
        out_specs=[pl.BlockSpec((LANES, tn), lambda i: (0, nt - 1 - i)), pl.BlockSpec((LANES, 1), lambda i: (0, 0))],
        out_shape=[jax.ShapeDtypeStruct((LANES, S), F32), jax.ShapeDtypeStruct((LANES, 1), F32)],
        scratch_shapes=[pltpu.VMEM((LANES, LANES), F32)],
        compiler_params=_params("arbitrary"),
    )(lat, fb_col, dF)


def _rope_tables(S):
    pos = jnp.arange(S, dtype=F32)
    inv_freq = ROPE_THETA ** (-(jnp.arange(MLA_ROPE // 2, dtype=F32) * 2.0 / MLA_ROPE))
    ang = pos[:, None] * inv_freq[None, :]
    cos, sin = jnp.cos(ang).T, jnp.sin(ang).T
    z16 = jnp.zeros_like(cos)

    def slab(lo, fill):
        def put(first, second, f):
            return jnp.concatenate([jnp.full((lo, S), f, F32), first, second, jnp.full((LANES - lo - MLA_ROPE, S), f, F32)], axis=0)
        return put(cos, cos, fill), put(-sin, z16, 0.0), put(z16, sin, 0.0)

    tq = tuple(jnp.tile(t, (MLA_HEADS, 1)) for t in slab(MLA_NOPE, 1.0))
    tm = slab(KR_ROW, 0.0)
    return tq, tm


def _rope(x, c, s1, s2):
    n = x.shape[0]
    half = MLA_ROPE // 2
    return x * c + pltpu.roll(x, n - half, 0) * s1 + pltpu.roll(x, half, 0) * s2


def _rope_t(dy, c, s1, s2):
    n = dy.shape[0]
    half = MLA_ROPE // 2
    return dy * c + pltpu.roll(dy * s1, half, 0) + pltpu.roll(dy * s2, n - half, 0)


def _mla_prep_fwd(lat, g_q, g_kv, w_uq_t, w_ukv_t, tq, tmisc, *, name):
    S = lat.shape[1]
    tn = _tile(S, 512)
    QW = MLA_HEADS * MLA_PAD

    def body(cq_ref, ckv_ref, mi_ref, gq_ref, gkv_ref, wq_ref, wkv_ref, c_ref, s1_ref, s2_ref, cm_ref, s1m_ref, s2m_ref,
             nq_ref, nkv_ref, q_ref, k_ref, v_ref):
        nq = _col_rms(cq_ref[...], gq_ref[...]).astype(BF16)
        nkv = _col_rms(ckv_ref[...], gkv_ref[...]).astype(BF16)
        nq_ref[...] = nq
        nkv_ref[...] = nkv
        q_ref[...] = _rope(_dot(wq_ref[...], nq), c_ref[...], s1_ref[...], s2_ref[...]).astype(BF16)
        kv = _dot(wkv_ref[...], nkv).astype(BF16)
        kr = _rope(mi_ref[...], cm_ref[...], s1m_ref[...], s2m_ref[...]).astype(BF16)
        for h in range(MLA_HEADS):
            k_ref[h * MLA_PAD:h * MLA_PAD + MLA_NOPE, :] = kv[h * LANES:h * LANES + MLA_NOPE, :]
            k_ref[h * MLA_PAD + MLA_NOPE:(h + 1) * MLA_PAD, :] = kr[KR_ROW:LANES, :]
            v_ref[h * HEAD_DIM:(h + 1) * HEAD_DIM, :] = kv[h * LANES + MLA_NOPE:(h + 1) * LANES, :]

    def col(rows, rblk=0):
        return pl.BlockSpec((rows, tn), lambda i: (rblk, i))

    def full(a):
        return pl.BlockSpec(a.shape, lambda i: (0, 0))

    return pl.pallas_call(
        body, name=name, grid=(S // tn,),
        in_specs=[col(MLA_Q_RANK, 0), col(MLA_KV_RANK, 2), col(LANES, 3), full(g_q), full(g_kv), full(w_uq_t), full(w_ukv_t),
                  col(QW), col(QW), col(QW), col(LANES), col(LANES), col(LANES)],
        out_specs=[col(MLA_Q_RANK), col(MLA_KV_RANK), col(QW), col(QW), col(MLA_HEADS * HEAD_DIM)],
        out_shape=[jax.ShapeDtypeStruct((MLA_Q_RANK, S), BF16), jax.ShapeDtypeStruct((MLA_KV_RANK, S), BF16),
                   jax.ShapeDtypeStruct((QW, S), BF16), jax.ShapeDtypeStruct((QW, S), BF16),
                   jax.ShapeDtypeStruct((MLA_HEADS * HEAD_DIM, S), BF16)],
        compiler_params=_params("parallel"),
    )(lat, lat, lat, g_q, g_kv, w_uq_t, w_ukv_t, *tq, *tmisc)


def _mla_prep_bwd(lat, nq, nkv, g_q, g_kv, w_uq_p, w_ukv, tq, tmisc, dq, dk, dv, dmisc_f, *, name):
    S = lat.shape[1]
    tn = _tile(S, 512)
    QW = MLA_HEADS * MLA_PAD

    def body(cq_ref, ckv_ref, nq_ref, nkv_ref, gq_ref, gkv_ref, wq_ref, wkv_ref, c_ref, s1_ref, s2_ref,
             cm_ref, s1m_ref, s2m_ref, dq_ref, dk_ref, dv_ref, dmf_ref,
             dlat_ref, dwq_ref, dwkv_ref, dgq_ref, dgkv_ref):
        @pl.when(pl.program_id(0) == 0)
        def _():
            dwq_ref[...] = jnp.zeros_like(dwq_ref)
            dwkv_ref[...] = jnp.zeros_like(dwkv_ref)
            dgq_ref[...] = jnp.zeros_like(dgq_ref)
            dgkv_ref[...] = jnp.zeros_like(dgkv_ref)

        dqm = _rope_t(dq_ref[...], c_ref[...], s1_ref[...], s2_ref[...]).astype(BF16)
        dwq_ref[...] += _dot(dqm, nq_ref[...], NT)
        dx, dg = _col_rms_bwd(cq_ref[...], gq_ref[...], _dot(wq_ref[...], dqm))
        dlat_ref[0:MLA_Q_RANK, :] = dx
        dgq_ref[...] += dg
        dkv = jnp.concatenate(
            [part for h in range(MLA_HEADS)
             for part in (dk_ref[h * MLA_PAD:h * MLA_PAD + MLA_NOPE, :], dv_ref[h * HEAD_DIM:(h + 1) * HEAD_DIM, :])],
            axis=0).astype(BF16)
        dwkv_ref[...] += _dot(dkv, nkv_ref[...], NT)
        dx, dg = _col_rms_bwd(ckv_ref[...], gkv_ref[...], _dot(wkv_ref[...], dkv))
        dlat_ref[MLA_Q_RANK:MLA_Q_RANK + MLA_KV_RANK, :] = dx
        dgkv_ref[...] += dg
        dkr = dk_ref[MLA_NOPE:MLA_PAD, :]
        for h in range(1, MLA_HEADS):
            dkr = dkr + dk_ref[h * MLA_PAD + MLA_NOPE:(h + 1) * MLA_PAD, :]
        dkr = jnp.concatenate([jnp.zeros((KR_ROW, tn), F32), dkr], axis=0)
        dlat_ref[MLA_Q_RANK + MLA_KV_RANK:LAT_ROWS, :] = _rope_t(dkr, cm_ref[...], s1m_ref[...], s2m_ref[...]) + dmf_ref[...]

    def col(rows, rblk=0):
        return pl.BlockSpec((rows, tn), lambda i: (rblk, i))

    def full(a):
        return pl.BlockSpec(a.shape, lambda i: (0, 0))

    def acc(r, c):
        return pl.BlockSpec((r, c), lambda i: (0, 0))

    return pl.pallas_call(
        body, name=name, grid=(S // tn,),
        in_specs=[col(MLA_Q_RANK, 0), col(MLA_KV_RANK, 2), col(MLA_Q_RANK), col(MLA_KV_RANK), full(g_q), full(g_kv),
                  full(w_uq_p), full(w_ukv), col(QW), col(QW), col(QW), col(LANES), col(LANES), col(LANES),
                  col(QW), col(QW), col(MLA_HEADS * HEAD_DIM), col(LANES)],
        out_specs=[col(LAT_ROWS), acc(QW, MLA_Q_RANK), acc(QW, MLA_KV_RANK), acc(MLA_Q_RANK, 1), acc(MLA_KV_RANK, 1)],
        out_shape=[jax.ShapeDtypeStruct((LAT_ROWS, S), F32), jax.ShapeDtypeStruct((QW, MLA_Q_RANK), F32),
                   jax.ShapeDtypeStruct((QW, MLA_KV_RANK), F32), jax.ShapeDtypeStruct((MLA_Q_RANK, 1), F32),
                   jax.ShapeDtypeStruct((MLA_KV_RANK, 1), F32)],
        compiler_params=_params("arbitrary"),
    )(lat, lat, nq, nkv, g_q, g_kv, w_uq_p, w_ukv, *tq, *tmisc, dq, dk, dv, dmisc_f)


GELU_C = math.sqrt(2.0 / math.pi)
GELU_A = 0.044715


def _conv_taps(a, halo, w_ref, b_ref, first):
    row = lax.broadcasted_iota(jnp.int32, a.shape, 0)
    h7 = jnp.where(first, 0.0, halo[7:8, :])
    h6 = jnp.where(first, 0.0, halo[6:7, :])
    a1 = jnp.where(row == 0, h7, pltpu.roll(a, 1, 0))
    a2 = jnp.where(row == 0, h6, jnp.where(row == 1, h7, pltpu.roll(a, 2, 0)))
    u = ((b_ref[...] + w_ref[0:1, :] * a2) + w_ref[1:2, :] * a1) + w_ref[2:3, :] * a
    return u, a1, a2


def _conv_specs(S, tm, tc, nc):
    hb = tm // 8
    main = lambda off: pl.BlockSpec((tm, tc), lambda j, i: (i, j + off))
    halo = lambda off: pl.BlockSpec((8, tc), lambda j, i: (jnp.maximum(i * hb - 1, 0), j + off))
    wspec = lambda off: pl.BlockSpec((3, tc), lambda j, i: (0, j + off))
    bspec = lambda off: pl.BlockSpec((1, tc), lambda j, i: (0, j + off))
    return main, halo, wspec, bspec


def _conv_geglu_fwd(a, conv_w, conv_b, *, name):
    S = a.shape[0]
    tm, tc = _tile(S, 512), _tile(D_FF, 1408)
    nc = D_FF // tc
    main, halo, wspec, bspec = _conv_specs(S, tm, tc, nc)

    def body(ag_ref, au_ref, hg_ref, hu_ref, wg_ref, wu_ref, bg_ref, bu_ref, z_ref):
        first = pl.program_id(1) == 0
        gate, _, _ = _conv_taps(ag_ref[...], hg_ref[...], wg_ref, bg_ref, first)
        up, _, _ = _conv_taps(au_ref[...], hu_ref[...], wu_ref, bu_ref, first)
        cdf = 0.5 * (1.0 + jnp.tanh(GELU_C * (gate + GELU_A * (gate * gate * gate))))
        z_ref[...] = (gate * cdf * up).astype(BF16)

    return pl.pallas_call(
        body, name=name, grid=(nc, S // tm),
        in_specs=[main(0), main(nc), halo(0), halo(nc), wspec(0), wspec(nc), bspec(0), bspec(nc)],
        out_specs=pl.BlockSpec((tm, tc), lambda j, i: (i, j)),
        out_shape=jax.ShapeDtypeStruct((S, D_FF), BF16),
        compiler_params=_params("parallel", "arbitrary"),
    )(a, a, a, a, conv_w, conv_w, conv_b, conv_b)


def _conv_geglu_bwd(a, conv_w, conv_b, dz, *, name):
    S = a.shape[0]
    tm, tc = _tile(S, 512), _tile(D_FF, 1408)
    nc = D_FF // tc
    main, halo, wspec, bspec = _conv_specs(S, tm, tc, nc)

    def body(ag_ref, au_ref, hg_ref, hu_ref, wg_ref, wu_ref, bg_ref, bu_ref, dz_ref, du_ref, dw_ref, db_ref):
        first = pl.program_id(1) == 0

        @pl.when(first)
        def _():
            dw_ref[...] = jnp.zeros_like(dw_ref)
            db_ref[...] = jnp.zeros_like(db_ref)

        gate, g1, g2 = _conv_taps(ag_ref[...], hg_ref[...], wg_ref, bg_ref, first)
        up, u1, u2 = _conv_taps(au_ref[...], hu_ref[...], wu_ref, bu_ref, first)
        dz = dz_ref[...]
        g2x = gate * gate
        th = jnp.tanh(GELU_C * (gate + GELU_A * (g2x * gate)))
        cdf = 0.5 * (1.0 + th)
        dgelu = cdf + gate * (0.5 * (1.0 - th * th) * (GELU_C * (1.0 + 3.0 * GELU_A * g2x)))
        dug = dz * up * dgelu
        duu = dz * (gate * cdf)
        du_ref[0] = dug
        du_ref[1] = duu
        for half, du, taps in ((0, dug, (g2, g1, ag_ref[...])), (1, duu, (u2, u1, au_ref[...]))):
            for tap in range(3):
                dw_ref[half, tap:tap + 1, :] += jnp.sum(du * taps[tap], axis=0, keepdims=True)
            db_ref[half] += jnp.sum(du, axis=0, keepdims=True)

    return pl.pallas_call(
        body, name=name, grid=(nc, S // tm),
        in_specs=[main(0), main(nc), halo(0), halo(nc), wspec(0), wspec(nc), bspec(0), bspec(nc),
                  pl.BlockSpec((tm, tc), lambda j, i: (i, j))],
        out_specs=[pl.BlockSpec((2, tm, tc), lambda j, i: (0, i, j)), pl.BlockSpec((2, 3, tc), lambda j, i: (0, 0, j)),
                   pl.BlockSpec((2, 1, tc), lambda j, i: (0, 0, j))],
        out_shape=[jax.ShapeDtypeStruct((2, S, D_FF), F32), jax.ShapeDtypeStruct((2, 3, D_FF), F32),
                   jax.ShapeDtypeStruct((2, 1, D_FF), F32)],
        compiler_params=_params("parallel", "arbitrary"),
    )(a, a, a, a, conv_w, conv_w, conv_b, conv_b, dz)


def _conv_bwd_input(du, conv_w, *, name):
    S = du.shape[1]
    tm, tc = _tile(S, 512), _tile(D_FF, 1408)
    nc = D_FF // tc
    nr = S // tm
    hb = tm // 8

    def body(du_ref, nx_ref, w_ref, da_ref):
        last = pl.program_id(2) == nr - 1
        d = du_ref[0]
        row = lax.broadcasted_iota(jnp.int32, d.shape, 0)
        n0 = jnp.where(last, 0.0, nx_ref[0, 0:1, :])
        n1 = jnp.where(last, 0.0, nx_ref[0, 1:2, :])
        d1 = jnp.where(row == tm - 1, n0, pltpu.roll(d, tm - 1, 0))
        d2 = jnp.where(row == tm - 1, n1, jnp.where(row == tm - 2, n0, pltpu.roll(d, tm - 2, 0)))
        da_ref[...] = (w_ref[2:3, :] * d + w_ref[1:2, :] * d1 + w_ref[0:1, :] * d2).astype(BF16)

    return pl.pallas_call(
        body, name=name, grid=(2, nc, nr),
        in_specs=[pl.BlockSpec((1, tm, tc), lambda h, j, i: (h, i, j)),
                  pl.BlockSpec((1, 8, tc), lambda h, j, i: (h, jnp.minimum((i + 1) * hb, S // 8 - 1), j)),
                  pl.BlockSpec((3, tc), lambda h, j, i: (0, h * nc + j))],
        out_specs=pl.BlockSpec((tm, tc), lambda h, j, i: (i, h * nc + j)),
        out_shape=jax.ShapeDtypeStruct((S, 2 * D_FF), BF16),
        compiler_params=_params("parallel", "parallel", "arbitrary"),
    )(du, du, conv_w)


def _adamw(w, g, m, v, *, name):
    R, C = w.shape
    tr = _tile(R, PACK_ROW_TILE)

    def body(w_ref, g_ref, m_ref, v_ref, d_ref, mo_ref, vo_ref):
        g = g_ref[...]
        m = ADAM_B1 * m_ref[...] + (1.0 - ADAM_B1) * g
        v = ADAM_B2 * v_ref[...] + (1.0 - ADAM_B2) * jnp.square(g)
        m_hat = m / (1.0 - ADAM_B1 ** ADAM_STEP)
        v_hat = v / (1.0 - ADAM_B2 ** ADAM_STEP)
        d_ref[...] = -ADAM_LR * (m_hat / (jnp.sqrt(v_hat) + ADAM_EPS) + ADAM_WD * w_ref[...])
        mo_ref[...] = m
        vo_ref[...] = v

    blk = pl.BlockSpec((tr, C), lambda i: (i, 0))
    shp = jax.ShapeDtypeStruct((R, C), F32)
    return pl.pallas_call(
        body, name=name, grid=(R // tr,),
        in_specs=[blk] * 4, out_specs=[blk] * 3, out_shape=[shp] * 3,
        compiler_params=_params("parallel"),
    )(w, g, m, v)


HBM_SPEC = pl.BlockSpec(memory_space=pl.ANY)


def _mesh_pos():
    return lax.axis_index("x"), lax.axis_index("y"), lax.axis_index("c")


def _other_chips(x, y):
    return [(1 - x, y), (x, 1 - y), (1 - x, 1 - y)]


def _gather_shards(flat):
    R, C = flat.shape
    Rh = R // 2

    def body(in_ref, out_ref, send_sems, recv_sems, local_sem):
        x, y, c = _mesh_pos()
        me = 2 * x + y
        chips = _other_chips(x, y)
        sibling = (x, y, 1 - c)
        mine_rows = pl.ds(pl.multiple_of(c * Rh, 8), Rh)
        other_rows = pl.ds(pl.multiple_of((1 - c) * Rh, 8), Rh)

        def copy(k, src, dst, to):
            return pltpu.make_async_remote_copy(src_ref=src, dst_ref=dst, send_sem=send_sems.at[k], recv_sem=recv_sems.at[k],
                                                device_id=to, device_id_type=MESH)

        local = pltpu.make_async_copy(in_ref, out_ref.at[me], local_sem)
        local.start()
        first = [copy(j, in_ref.at[mine_rows], out_ref.at[me, mine_rows], (px, py, c)) for j, (px, py) in enumerate(chips)]
        for cp in first:
            cp.start()
        passed = []
        for j, (px, py) in enumerate(chips):
            landed = out_ref.at[2 * px + py, mine_rows]
            copy(j, landed, landed, (px, py, c)).wait_recv()
            cp = copy(3 + j, landed, landed, sibling)
            cp.start()
            passed.append(cp)
        for j, (px, py) in enumerate(chips):
            landed = out_ref.at[2 * px + py, other_rows]
            copy(3 + j, landed, landed, sibling).wait_recv()
        for cp in first + passed:
            cp.wait_send()
        local.wait()

    return pl.pallas_call(
        body, name="gather_weight_shards",
        in_specs=[HBM_SPEC], out_specs=HBM_SPEC,
        out_shape=jax.ShapeDtypeStruct((4, R, C), flat.dtype),
        scratch_shapes=[pltpu.SemaphoreType.DMA((6,)), pltpu.SemaphoreType.DMA((6,)), pltpu.SemaphoreType.DMA],
        compiler_params=pltpu.CompilerParams(has_side_effects=True),
    )(flat)


def _sibling_exchange(g):
    n, _, Rh, C = g.shape

    def body(g_ref, out_ref, send_sems, recv_sems):
        x, y, c = _mesh_pos()
        sibling = (x, y, 1 - c)
        copies = [pltpu.make_async_remote_copy(src_ref=g_ref.at[s, 1 - c], dst_ref=out_ref.at[s], send_sem=send_sems.at[s],
                                               recv_sem=recv_sems.at[s], device_id=sibling, device_id_type=MESH)
                  for s in range(n)]
        for cp in copies:
            cp.start()
        for cp in copies:
            cp.wait()

    return pl.pallas_call(
        body, name="grad_sibling_exchange",
        in_specs=[HBM_SPEC], out_specs=HBM_SPEC,
        out_shape=jax.ShapeDtypeStruct((n, Rh, C), g.dtype),
        scratch_shapes=[pltpu.SemaphoreType.DMA((n,)), pltpu.SemaphoreType.DMA((n,))],
        compiler_params=pltpu.CompilerParams(has_side_effects=True),
    )(g)


def _core_index():
    return jnp.reshape(lax.axis_index("c"), (1,)).astype(jnp.int32)


def _pair_sum(g, recv):
    n, _, Rh, C = g.shape
    tr = _tile(Rh, PACK_ROW_TILE)

    def body(c_ref, g_ref, r_ref, o_ref):
        o_ref[...] = g_ref[0] + r_ref[...]

    return pl.pallas_call(
        body, name="grad_pair_sum",
        grid_spec=pltpu.PrefetchScalarGridSpec(
            num_scalar_prefetch=1, grid=(n, Rh // tr),
            in_specs=[pl.BlockSpec((1, 1, tr, C), lambda s, r, c_ref: (s, c_ref[0], r, 0)),
                      pl.BlockSpec((1, tr, C), lambda s, r, c_ref: (s, r, 0))],
            out_specs=pl.BlockSpec((1, tr, C), lambda s, r, c_ref: (s, r, 0))),
        out_shape=jax.ShapeDtypeStruct((n, Rh, C), F32),
        compiler_params=_params("parallel", "parallel"),
    )(_core_index(), g, recv)


def _chip_scatter(gc):
    n, Rh, C = gc.shape

    def body(g_ref, out_ref, send_sems, recv_sems, local_sem):
        x, y, c = _mesh_pos()
        me = 2 * x + y
        chips = _other_chips(x, y)
        local = pltpu.make_async_copy(g_ref.at[me], out_ref.at[me], local_sem)
        local.start()
        copies = [pltpu.make_async_remote_copy(src_ref=g_ref.at[2 * px + py], dst_ref=out_ref.at[me], send_sem=send_sems.at[j],
                                               recv_sem=recv_sems.at[j], device_id=(px, py, c), device_id_type=MESH)
                  for j, (px, py) in enumerate(chips)]
        for cp in copies:
            cp.start()
        for j, (px, py) in enumerate(chips):
            landed = out_ref.at[2 * px + py]
            pltpu.make_async_remote_copy(src_ref=landed, dst_ref=landed, send_sem=send_sems.at[j], recv_sem=recv_sems.at[j],
                                         device_id=(px, py, c), device_id_type=MESH).wait_recv()
        for cp in copies:
            cp.wait_send()
        local.wait()

    return pl.pallas_call(
        body, name="grad_chip_scatter",
        in_specs=[HBM_SPEC], out_specs=HBM_SPEC,
        out_shape=jax.ShapeDtypeStruct((n, Rh, C), gc.dtype),
        scratch_shapes=[pltpu.SemaphoreType.DMA((3,)), pltpu.SemaphoreType.DMA((3,)), pltpu.SemaphoreType.DMA],
        compiler_params=pltpu.CompilerParams(has_side_effects=True),
    )(gc)


def _chip_sum(parts):
    n, Rh, C = parts.shape
    tr = _tile(Rh, PACK_ROW_TILE)

    def body(p_ref, o_ref):
        o_ref[...] = ((p_ref[0] + p_ref[1]) + p_ref[2]) + p_ref[3]

    return pl.pallas_call(
        body, name="grad_chip_sum", grid=(Rh // tr,),
        in_specs=[pl.BlockSpec((n, tr, C), lambda r: (0, r, 0))],
        out_specs=pl.BlockSpec((tr, C), lambda r: (r, 0)),
        out_shape=jax.ShapeDtypeStruct((Rh, C), F32),
        compiler_params=_params("parallel"),
    )(parts)


def _sibling_share(half):
    Rh, C = half.shape

    def body(h_ref, out_ref, send_sem, recv_sem, local_sem):
        x, y, c = _mesh_pos()
        local = pltpu.make_async_copy(h_ref, out_ref.at[c], local_sem)
        local.start()
        cp = pltpu.make_async_remote_copy(src_ref=h_ref, dst_ref=out_ref.at[c], send_sem=send_sem, recv_sem=recv_sem,
                                          device_id=(x, y, 1 - c), device_id_type=MESH)
        cp.start()
        landed = out_ref.at[1 - c]
        pltpu.make_async_remote_copy(src_ref=landed, dst_ref=landed, send_sem=send_sem, recv_sem=recv_sem,
                                     device_id=(x, y, 1 - c), device_id_type=MESH).wait_recv()
        cp.wait_send()
        local.wait()

    return pl.pallas_call(
        body, name="grad_sibling_share",
        in_specs=[HBM_SPEC], out_specs=HBM_SPEC,
        out_shape=jax.ShapeDtypeStruct((2, Rh, C), half.dtype),
        scratch_shapes=[pltpu.SemaphoreType.DMA, pltpu.SemaphoreType.DMA, pltpu.SemaphoreType.DMA],
        compiler_params=pltpu.CompilerParams(has_side_effects=True),
    )(half)


def _allreduce_small(part):
    rows, C = part.shape

    def body(p_ref, o_ref, slots, send_sems, recv_sems):
        x, y, c = _mesh_pos()
        me = 4 * x + 2 * y + c
        slots[me] = p_ref[...]
        copies = []
        for k in range(1, 8):
            kx, ky, kc = (k >> 2) & 1, (k >> 1) & 1, k & 1
            peer = (x ^ kx if kx else x, y ^ ky if ky else y, c ^ kc if kc else c)
            cp = pltpu.make_async_remote_copy(src_ref=p_ref, dst_ref=slots.at[me], send_sem=send_sems.at[k - 1],
                                              recv_sem=recv_sems.at[k - 1], device_id=peer, device_id_type=MESH)
            cp.start()
            copies.append((cp, peer))
        for k, (cp, peer) in enumerate(copies):
            src = 4 * peer[0] + 2 * peer[1] + peer[2]
            pltpu.make_async_remote_copy(src_ref=p_ref, dst_ref=slots.at[src], send_sem=send_sems.at[k],
                                         recv_sem=recv_sems.at[k], device_id=peer, device_id_type=MESH).wait_recv()
        for cp, _ in copies:
            cp.wait_send()
        total = slots[0]
        for d in range(1, 8):
            total = total + slots[d]
        o_ref[...] = total

    return pl.pallas_call(
        body, name="small_grad_allreduce",
        in_specs=[pl.BlockSpec(memory_space=pltpu.VMEM)], out_specs=pl.BlockSpec(memory_space=pltpu.VMEM),
        out_shape=jax.ShapeDtypeStruct((rows, C), F32),
        scratch_shapes=[pltpu.VMEM((8, rows, C), F32), pltpu.SemaphoreType.DMA((7,)), pltpu.SemaphoreType.DMA((7,))],
        compiler_params=pltpu.CompilerParams(has_side_effects=True, vmem_limit_bytes=VMEM_LIMIT_BYTES),
    )(part)


def _permute_w_in(w):
    z = lambda n: jnp.zeros(w.shape[:-1] + (n,), w.dtype)
    return jnp.concatenate([w[..., :QKV_ROWS], w[..., 1540:1924], z(KR_ROW), w[..., 1924:1956], w[..., 1536:1540],
                            z(LANES - FL_ROW - FOX_HEADS)], axis=-1)


def _unpermute_w_in(g):
    misc = QKV_ROWS + LAT_ROWS - LANES
    return jnp.concatenate([g[..., :QKV_ROWS], g[..., misc + FL_ROW:misc + FL_ROW + FOX_HEADS], g[..., QKV_ROWS:misc],
                            g[..., misc + KR_ROW:misc + KR_ROW + MLA_ROPE]], axis=-1)


def _pad_w_uq(w):
    lead = w.shape[:-1]
    w = w.reshape(lead + (MLA_HEADS, MLA_QK))
    w = jnp.concatenate([w, jnp.zeros(lead + (MLA_HEADS, MLA_PAD - MLA_QK), w.dtype)], axis=-1)
    return w.reshape(lead + (MLA_HEADS * MLA_PAD,))


def _unpad_w_uq(g):
    lead = g.shape[:-1]
    return g.reshape(lead + (MLA_HEADS, MLA_PAD))[..., :MLA_QK].reshape(lead + (MLA_HEADS * MLA_QK,))


def _t(a):
    return jnp.swapaxes(a, -1, -2)


BIG = ("w_in", "w_uq", "w_ukv", "w_out", "w_up", "conv_w", "w_down")
BIG_AXIS = {"w_in": 2, "w_uq": 2, "w_ukv": 2, "w_out": 1, "w_up": 2, "conv_w": 2, "w_down": 1}
SMALL = ("attn_pre_norm", "forget_bias", "swa_sinks", "rel_bias", "q_latent_norm", "kv_latent_norm", "group_norm",
         "attn_post_norm", "ffn_pre_norm", "conv_b", "ffn_post_norm")
WEIGHTS = ("attn_pre_norm", "w_in", "forget_bias", "swa_sinks", "rel_bias", "q_latent_norm", "w_uq", "kv_latent_norm",
           "w_ukv", "group_norm", "w_out", "attn_post_norm", "ffn_pre_norm", "w_up", "conv_w", "conv_b", "w_down",
           "ffn_post_norm")


def _pack(arrs, cols, row_mult):
    flat = jnp.concatenate([a.reshape(-1) for a in arrs])
    n = flat.shape[0]
    per = cols * row_mult
    total = -(-n // per) * per
    return jnp.pad(flat, (0, total - n)).reshape(total // cols, cols)


def _unpack(packed, shapes):
    flat = packed.reshape(-1)
    out, off = [], 0
    for shp in shapes:
        n = int(np.prod(shp))
        out.append(flat[off:off + n].reshape(shp))
        off += n
    return out


def _kernel_weights(full, small):
    w_in_p = _permute_w_in(full["w_in"])
    w_in_t = _t(w_in_p)
    w_uq_p = _pad_w_uq(full["w_uq"])
    W = dict(small)
    W.update(w_qkv_t=w_in_t[:, :QKV_ROWS], w_lat_t=w_in_t[:, QKV_ROWS:], w_in_t=w_in_t, w_uq_p=w_uq_p, w_uq_t=_t(w_uq_p),
             w_ukv=full["w_ukv"], w_ukv_t=_t(full["w_ukv"]), w_out=full["w_out"], w_up=full["w_up"], conv_w=full["conv_w"],
             w_down=full["w_down"])
    return W


def _local_step(x, target, W):
    S = x.shape[0]
    tq_tabs, tm_tabs = _rope_tables(S)
    onehot_t = _rel_onehot_t()
    bias_t = _bias_table(W["rel_bias"].T, onehot_t).reshape(SWA_Q_HEADS, 2 * WINDOW, WINDOW)
    row = lambda a: a.reshape(1, -1)
    col = lambda a: a.reshape(-1, 1)
    fox_rows = (FOX_ROW0, FOX_ROW0 + FOX_HEADS * HEAD_DIM, FOX_ROW0 + 2 * FOX_HEADS * HEAD_DIM)
    fox = dict(rows=fox_rows, H=FOX_HEADS, Dk=HEAD_DIM, Dv=HEAD_DIM, scale=HEAD_DIM ** -0.5)
    mla = dict(rows=(0, 0, 0), H=MLA_HEADS, Dk=MLA_PAD, Dv=HEAD_DIM, scale=MLA_QK ** -0.5)

    saved = []
    h = _rms_fwd(x, row(W["attn_pre_norm"][0]), name="rms_in")
    for l in range(DEPTH):
        sv = {"x0": x, "h1": h}
        qkv = _matmul(W["w_qkv_t"][l], h, tb=True, out_dtype=BF16, name="proj_qkv")
        lat = _matmul(W["w_lat_t"][l], h, tb=True, name="proj_lat")
        oa, lse_a = _swa_fwd(qkv, bias_t, W["swa_sinks"][l], name="swa_fwd")
        fb_col = jnp.pad(col(W["forget_bias"][l]), ((FL_ROW, LANES - FL_ROW - FOX_HEADS), (0, 0)))
        f_slab = _gate_fwd(lat, fb_col, name="fox_gate_fwd")
        f4 = f_slab[FL_ROW:FL_ROW + FOX_HEADS]
        f_row, f_col = f4[:, None, :], f4[:, :, None]
        of, lse_f = _attn_fwd(qkv, qkv, qkv, f_row=f_row, f_col=f_col, name="fox_fwd", **fox)
        nq, nkv, qm, km, vm = _mla_prep_fwd(lat, col(W["q_latent_norm"][l]), col(W["kv_latent_norm"][l]), W["w_uq_t"][l],
                                            W["w_ukv_t"][l], tq_tabs, tm_tabs, name="mla_prep_fwd")
        oc, lse_c = _attn_fwd(qm, km, vm, name="mla_fwd", **mla)
        mixed = _group_norm_fwd(oa, of, oc, col(W["group_norm"][l]), name="group_norm_fwd")
        y = _matmul(mixed, W["w_out"][l], ta=True, name="proj_out")
        x1, h2 = _resid_rms(x, y, row(W["attn_post_norm"][l]), row(W["ffn_pre_norm"][l]), name="attn_resid")
        a = _matmul(h2, W["w_up"][l], name="ffn_up")
        z = _conv_geglu_fwd(a, W["conv_w"][l], row(W["conv_b"][l]), name="conv_geglu_fwd")
        y2 = _matmul(z, W["w_down"][l], name="ffn_down")
        g_next = row(W["attn_pre_norm"][l + 1]) if l + 1 < DEPTH else None
        x2, h_next = _resid_rms(x1, y2, row(W["ffn_post_norm"][l]), g_next, name="ffn_resid")
        sv.update(qkv=qkv, lat=lat, oa=oa, lse_a=lse_a, fb_col=fb_col, f_row=f_row, f_col=f_col, of=of, lse_f=lse_f,
                  nq=nq, nkv=nkv, qm=qm, km=km, vm=vm, oc=oc, lse_c=lse_c, mixed=mixed, y=y, x1=x1, h2=h2, a=a, z=z, y2=y2)
        saved.append(sv)
        x, h = x2, h_next

    loss, dx = _loss_head(x, target)

    G = {k: [None] * DEPTH for k in WEIGHTS if k != "rel_bias"}
    dbias_layers = [None] * DEPTH
    for l in reversed(range(DEPTH)):
        sv = saved[l]
        dy2, dg = _rms_bwd(sv["y2"], row(W["ffn_post_norm"][l]), dx, out_dtype=BF16, name="ffn_post_bwd")
        G["ffn_post_norm"][l] = dg[0]
        dz = _matmul(dy2, W["w_down"][l], tb=True, name="ffn_down_dx")
        G["w_down"][l] = _matmul(sv["z"], dy2, ta=True, name="ffn_down_dw")
        du, dcw, dcb = _conv_geglu_bwd(sv["a"], W["conv_w"][l], row(W["conv_b"][l]), dz, name="conv_geglu_bwd")
        G["conv_w"][l] = dcw.transpose(1, 0, 2).reshape(3, 2 * D_FF)
        G["conv_b"][l] = dcb.reshape(2 * D_FF)
        da = _conv_bwd_input(du, W["conv_w"][l], name="conv_bwd_input")
        dh2 = _matmul(da, W["w_up"][l], tb=True, name="ffn_up_dx")
        G["w_up"][l] = _matmul(sv["h2"], da, ta=True, name="ffn_up_dw")
        dx1, dg = _rms_bwd(sv["x1"], row(W["ffn_pre_norm"][l]), dh2, resid=dx, out_dtype=F32, name="ffn_pre_bwd")
        G["ffn_pre_norm"][l] = dg[0]
        dy, dg = _rms_bwd(sv["y"], row(W["attn_post_norm"][l]), dx1, out_dtype=BF16, name="attn_post_bwd")
        G["attn_post_norm"][l] = dg[0]
        dmixed = _matmul(W["w_out"][l], dy, tb=True, name="proj_out_dx")
        G["w_out"][l] = _matmul(sv["mixed"], dy, name="proj_out_dw")
        doa, dof, doc, dg = _group_norm_bwd(sv["oa"], sv["of"], sv["oc"], col(W["group_norm"][l]), dmixed,
                                            name="group_norm_bwd")
        G["group_norm"][l] = dg[:, 0]
        dqa, dkva, dbias_l, dsink = _swa_bwd(sv["qkv"], bias_t, W["swa_sinks"][l], sv["oa"], doa, sv["lse_a"], name="swa_bwd")
        dbias_layers[l] = dbias_l.reshape(SWA_Q_HEADS, -1)
        G["swa_sinks"][l] = dsink[:, 0]
        dqf, dkf, dvf, dfk = _attn_bwd(sv["qkv"], sv["qkv"], sv["qkv"], o=sv["of"], do=dof, lse=sv["lse_f"],
                                       f_row=sv["f_row"], f_col=sv["f_col"], name="fox_bwd", **fox)
        dF = jnp.pad(dfk[:, :, 0], ((FL_ROW, LANES - FL_ROW - FOX_HEADS), (0, 0)))
        dmisc_f, dfb = _gate_bwd(sv["lat"], sv["fb_col"], dF, name="fox_gate_bwd")
        G["forget_bias"][l] = dfb[FL_ROW:FL_ROW + FOX_HEADS, 0]
        dqm, dkm, dvm = _attn_bwd(sv["qm"], sv["km"], sv["vm"], o=sv["oc"], do=doc, lse=sv["lse_c"], name="mla_bwd", **mla)
        dlat, dwq_t, dwkv_t, dgq, dgkv = _mla_prep_bwd(
            sv["lat"], sv["nq"], sv["nkv"], col(W["q_latent_norm"][l]), col(W["kv_latent_norm"][l]), W["w_uq_p"][l],
            W["w_ukv"][l], tq_tabs, tm_tabs, dqm, dkm, dvm, dmisc_f, name="mla_prep_bwd")
        G["w_uq"][l], G["w_ukv"][l] = _unpad_w_uq(dwq_t.T), dwkv_t.T
        G["q_latent_norm"][l], G["kv_latent_norm"][l] = dgq[:, 0], dgkv[:, 0]
        dproj = jnp.concatenate([dqa, dkva, dqf, dkf, dvf, dlat], axis=0).astype(BF16)
        dh1 = _matmul(dproj, W["w_in_t"][l], ta=True, name="proj_in_dx")
        G["w_in"][l] = _unpermute_w_in(_matmul(dproj, sv["h1"], name="proj_in_dw").T)
        dx, dg = _rms_bwd(sv["x0"], row(W["attn_pre_norm"][l]), dh1, resid=dx1, out_dtype=F32, name="attn_pre_bwd")
        G["attn_pre_norm"][l] = dg[0]

    grads = {k: jnp.stack(v) for k, v in G.items()}
    grads["rel_bias"] = _bias_table_bwd(jnp.stack(dbias_layers), onehot_t).T
    return loss, dx, grads


def _shard_blocks(full, axis):
    return jnp.split(full, 4, axis=axis)


def kernel(x, attn_pre_norm, w_in, forget_bias, swa_sinks, rel_bias, q_latent_norm, w_uq, kv_latent_norm, w_ukv, group_norm, w_out, attn_post_norm, ffn_pre_norm, w_up, conv_w, conv_b, w_down, ffn_post_norm, loss_target, m_attn_pre_norm, m_w_in, m_forget_bias, m_swa_sinks, m_rel_bias, m_q_latent_norm, m_w_uq, m_kv_latent_norm, m_w_ukv, m_group_norm, m_w_out, m_attn_post_norm, m_ffn_pre_norm, m_w_up, m_conv_w, m_conv_b, m_w_down, m_ffn_post_norm, v_attn_pre_norm, v_w_in, v_forget_bias, v_swa_sinks, v_rel_bias, v_q_latent_norm, v_w_uq, v_kv_latent_norm, v_w_ukv, v_group_norm, v_w_out, v_attn_post_norm, v_ffn_pre_norm, v_w_up, v_conv_w, v_conv_b, v_w_down, v_ffn_post_norm):
    args = dict(locals())
    w = {k: args[k] for k in WEIGHTS}
    m = {k: args["m_" + k] for k in WEIGHTS}
    v = {k: args["v_" + k] for k in WEIGHTS}
    big_shapes = [w[k].shape for k in BIG]
    row_mult = 2 * PACK_ROW_TILE

    as_sent = lambda k: lax.bitcast_convert_type(w[k], BF16) if k == "conv_w" else w[k].astype(BF16)
    sent_shapes = [s + (2,) if k == "conv_w" else s for k, s in zip(BIG, big_shapes)]
    gathered = _gather_shards(_pack([as_sent(k) for k in BIG], PACK_COLS, row_mult))
    shards = [_unpack(gathered[s], sent_shapes) for s in range(4)]
    full = {k: jnp.concatenate([shards[s][i] for s in range(4)], axis=BIG_AXIS[k]) for i, k in enumerate(BIG)}
    full["conv_w"] = lax.bitcast_convert_type(full["conv_w"], F32)
    W = _kernel_weights(full, {k: w[k] for k in SMALL})

    loss_part, dx, g = _local_step(x[0], loss_target[0], W)
    loss = lax.psum(loss_part, ("x", "y", "c"))

    per_shard = [_pack([_shard_blocks(g[k], BIG_AXIS[k])[s] for k in BIG], PACK_COLS, row_mult) for s in range(4)]
    gp = jnp.stack(per_shard)
    R = gp.shape[1]
    gp = gp.reshape(4, 2, R // 2, PACK_COLS)
    pair = _pair_sum(gp, _sibling_exchange(gp))
    mine = _chip_sum(_chip_scatter(pair))
    g_shard = _sibling_share(mine).reshape(R, PACK_COLS)
    d_b, m_b, v_b = _adamw(_pack([w[k] for k in BIG], PACK_COLS, row_mult), g_shard,
                           _pack([m[k] for k in BIG], PACK_COLS, row_mult), _pack([v[k] for k in BIG], PACK_COLS, row_mult),
                           name="adamw_big")
    out_g = dict(zip(BIG, _unpack(g_shard, big_shapes)))
    out_d = dict(zip(BIG, _unpack(d_b, big_shapes)))
    out_m = dict(zip(BIG, _unpack(m_b, big_shapes)))
    out_v = dict(zip(BIG, _unpack(v_b, big_shapes)))

    small_shapes = [w[k].shape for k in SMALL]
    g_small = _allreduce_small(_pack([g[k] for k in SMALL], LANES, 8))
    d_s, m_s, v_s = _adamw(_pack([w[k] for k in SMALL], LANES, 8), g_small, _pack([m[k] for k in SMALL], LANES, 8),
                           _pack([v[k] for k in SMALL], LANES, 8), name="adamw_small")
    out_g.update(zip(SMALL, _unpack(g_small, small_shapes)))
    out_d.update(zip(SMALL, _unpack(d_s, small_shapes)))
    out_m.update(zip(SMALL, _unpack(m_s, small_shapes)))
    out_v.update(zip(SMALL, _unpack(v_s, small_shapes)))

    return (loss, dx[None], *[out_g[k] for k in WEIGHTS], *[out_d[k] for k in WEIGHTS],
            *[out_m[k] for k in WEIGHTS], *[out_v[k] for k in WEIGHTS])
```

```python
import math

import numpy as np
import jax
import jax.numpy as jnp
from jax import lax
from jax.experimental import pallas as pl
from jax.experimental.pallas import tpu as pltpu

F32 = jnp.float32
BF16 = jnp.bfloat16

D_MODEL = 1024
DEPTH = 4
HEAD_DIM = 64
SWA_Q_HEADS = 8
SWA_KV_HEADS = 2
SWA_GROUP = SWA_Q_HEADS // SWA_KV_HEADS
WINDOW = 128
FOX_HEADS = 4
MLA_HEADS = 4
MLA_Q_RANK = 256
MLA_KV_RANK = 128
MLA_NOPE = 64
MLA_ROPE = 32
MLA_QK = MLA_NOPE + MLA_ROPE
ROPE_THETA = 10000.0
REL_BUCKETS = 32
REL_MAX_DIST = 128
D_FF = 2816
EPS = 1e-6
NEG_INF = -1e30
LANES = 128
N_CHIPS = 4
HALF_DEPTH = DEPTH // 2

IN_COLS = 1956
IN_ROWS = 2048
QKV_ROWS = 1536
LAT_ROWS = IN_ROWS - QKV_ROWS
LAT_SHIFT = FOX_HEADS
FOX_ROW0 = 768
MLA_PAD = LANES
GATE_ROWS = 8

ADAM_LR = 0.001
ADAM_B1 = 0.9
ADAM_B2 = 0.999
ADAM_EPS = 1e-08
ADAM_WD = 0.01
ADAM_STEP = 10

VMEM_LIMIT_BYTES = 48 * 1024 * 1024
ATT_TILE = 256
ROW_TILE = 256
MESH = pl.DeviceIdType.MESH

NT = (((1,), (1,)), ((), ()))
TN = (((0,), (0,)), ((), ()))
NN = (((1,), (0,)), ((), ()))


def _params(*sem):
    return pltpu.CompilerParams(dimension_semantics=sem, vmem_limit_bytes=VMEM_LIMIT_BYTES)


def _tile(dim, cap):
    for t in (2048, 1408, 1024, 512, 256, 128, 64, 32, 16, 8):
        if t <= cap and dim % t == 0:
            return t
    return dim


def _dot(a, b, dims=NN):
    return lax.dot_general(a, b, dims, preferred_element_type=F32)


def _split3(a):
    a1 = a.astype(BF16)
    r1 = a - a1.astype(F32)
    a2 = r1.astype(BF16)
    a3 = (r1 - a2.astype(F32)).astype(BF16)
    return a1, a2, a3


FF_SHARD = 2 * D_FF // N_CHIPS


def _matmul(a, b, *, ta=False, tb=False, out_dtype=F32, name, b_shards=None, out_into=None):
    if ta:
        K, M = a.shape
    else:
        M, K = a.shape
    if b_shards is not None:
        K2, N = (2 * D_FF, D_MODEL) if tb else (D_MODEL, 2 * D_FF)
    elif tb:
        N, K2 = b.shape
    else:
        K2, N = b.shape
    assert K == K2, (a.shape, b.shape)
    tm, tn, tk = _tile(M, 1408), _tile(N, 1408), _tile(K, 1408)
    nk = K // tk
    dims = (((0 if ta else 1,), (1 if tb else 0,)), ((), ()))

    def body(*refs):
        a_ref, b_ref = refs[0], refs[1]
        o_ref, acc_ref = refs[-2], refs[-1]
        k = pl.program_id(2)

        @pl.when(k == 0)
        def _():
            acc_ref[...] = jnp.zeros_like(acc_ref)

        acc_ref[...] += lax.dot_general(a_ref[...], b_ref[...], dims, preferred_element_type=F32)

        @pl.when(k == nk - 1)
        def _():
            o_ref[...] = acc_ref[...].astype(o_ref.dtype)

    a_spec = pl.BlockSpec((tk, tm), lambda i, j, k: (k, i)) if ta else pl.BlockSpec((tm, tk), lambda i, j, k: (i, k))
    if b_shards is not None:
        l = b_shards
        if tb:
            assert tk == FF_SHARD
            b_spec = pl.BlockSpec((None, None, tn, tk), lambda i, j, k: (l, k, j, 0))
        else:
            assert tn == FF_SHARD
            b_spec = pl.BlockSpec((None, None, tk, tn), lambda i, j, k: (l, j, k, 0))
    else:
        b_spec = pl.BlockSpec((tn, tk), lambda i, j, k: (j, k)) if tb else pl.BlockSpec((tk, tn), lambda i, j, k: (k, j))
    ins, in_specs, aliases = [a, b], [a_spec, b_spec], {}
    if out_into is None:
        out_spec = pl.BlockSpec((tm, tn), lambda i, j, k: (i, j))
        out_shape = jax.ShapeDtypeStruct((M, N), out_dtype)
    else:
        buf, lo, sharded = out_into
        if sharded:
            assert tn == FF_SHARD
            out_spec = pl.BlockSpec((None, None, tm, tn), lambda i, j, k: (lo, j, i, 0))
        else:
            out_spec = pl.BlockSpec((None, tm, tn), lambda i, j, k: (lo, i, j))
        out_shape = jax.ShapeDtypeStruct(buf.shape, buf.dtype)
        ins.append(buf)
        in_specs.append(pl.BlockSpec(memory_space=pl.ANY))
        aliases = {2: 0}
    return pl.pallas_call(
        body, name=name, grid=(M // tm, N // tn, nk),
        in_specs=in_specs, out_specs=out_spec, out_shape=out_shape,
        scratch_shapes=[pltpu.VMEM((tm, tn), F32)],
        input_output_aliases=aliases,
        compiler_params=_params("parallel", "parallel", "arbitrary"),
    )(*ins)


def _seg_rms(xs, g):
    r = lax.rsqrt(jnp.mean(xs * xs, axis=-1, keepdims=True) + EPS)
    return xs * r * g


def _seg_rms_bwd(xs, g, dy):
    r = lax.rsqrt(jnp.mean(xs * xs, axis=-1, keepdims=True) + EPS)
    gd = dy * g
    c = jnp.mean(gd * xs, axis=-1, keepdims=True)
    dx = r * gd - xs * (r * r * r * c)
    dg = jnp.sum(dy * (xs * r), axis=0, keepdims=True)
    return dx, dg


def _rms_fwd(x, g, *, name):
    S, W = x.shape
    tm = _tile(S, 512)

    def body(x_ref, g_ref, o_ref):
        o_ref[...] = _seg_rms(x_ref[...], g_ref[...]).astype(o_ref.dtype)

    return pl.pallas_call(
        body, name=name, grid=(S // tm,),
        in_specs=[pl.BlockSpec((tm, W), lambda i: (i, 0)), pl.BlockSpec((1, W), lambda i: (0, 0))],
        out_specs=pl.BlockSpec((tm, W), lambda i: (i, 0)),
        out_shape=jax.ShapeDtypeStruct((S, W), BF16),
        compiler_params=_params("parallel"),
    )(x, g)


def _rms_bwd(x, g, dy, *, resid=None, out_dtype, name):
    S, W = x.shape
    tm = _tile(S, 512)
    has_resid = resid is not None

    def body(*refs):
        if has_resid:
            x_ref, g_ref, dy_ref, r_ref, dx_ref, dg_ref = refs
        else:
            x_ref, g_ref, dy_ref, dx_ref, dg_ref = refs

        @pl.when(pl.program_id(0) == 0)
        def _():
            dg_ref[...] = jnp.zeros_like(dg_ref)

        dx, dg = _seg_rms_bwd(x_ref[...], g_ref[...], dy_ref[...])
        if has_resid:
            dx = dx + r_ref[...]
        dx_ref[...] = dx.astype(dx_ref.dtype)
        dg_ref[...] += dg

    row = pl.BlockSpec((tm, W), lambda i: (i, 0))
    vec = pl.BlockSpec((1, W), lambda i: (0, 0))
    ins = [x, g, dy] + ([resid] if has_resid else [])
    return pl.pallas_call(
        body, name=name, grid=(S // tm,),
        in_specs=[row, vec, row] + ([row] if has_resid else []),
        out_specs=[row, vec],
        out_shape=[jax.ShapeDtypeStruct((S, W), out_dtype), jax.ShapeDtypeStruct((1, W), F32)],
        compiler_params=_params("arbitrary"),
    )(*ins)


def _resid_rms(x, y, g_post, g_next, *, name):
    S, W = x.shape
    tm = _tile(S, 512)
    with_next = g_next is not None

    def body(*refs):
        if with_next:
            x_ref, y_ref, gp_ref, gn_ref, xo_ref, h_ref = refs
        else:
            x_ref, y_ref, gp_ref, xo_ref = refs
        xn = x_ref[...] + _seg_rms(y_ref[...], gp_ref[...])
        xo_ref[...] = xn
        if with_next:
            h_ref[...] = _seg_rms(xn, gn_ref[...]).astype(BF16)

    row = pl.BlockSpec((tm, W), lambda i: (i, 0))
    vec = pl.BlockSpec((1, W), lambda i: (0, 0))
    outs = [jax.ShapeDtypeStruct((S, W), F32)] + ([jax.ShapeDtypeStruct((S, W), BF16)] if with_next else [])
    res = pl.pallas_call(
        body, name=name, grid=(S // tm,),
        in_specs=[row, row, vec] + ([vec] if with_next else []),
        out_specs=[row] + ([row] if with_next else []),
        out_shape=outs,
        compiler_params=_params("parallel"),
    )(*([x, y, g_post] + ([g_next] if with_next else [])))
    return (res[0], res[1]) if with_next else (res[0], None)


def _col_rms(xs, g):
    r = lax.rsqrt(jnp.mean(xs * xs, axis=0, keepdims=True) + EPS)
    return xs * r * g


def _col_rms_bwd(xs, g, dy):
    r = lax.rsqrt(jnp.mean(xs * xs, axis=0, keepdims=True) + EPS)
    gd = dy * g
    c = jnp.mean(gd * xs, axis=0, keepdims=True)
    dx = r * gd - xs * (r * r * r * c)
    dg = jnp.sum(dy * (xs * r), axis=1, keepdims=True)
    return dx, dg


GROUP_ROWS = (SWA_Q_HEADS * HEAD_DIM, FOX_HEADS * HEAD_DIM, MLA_HEADS * HEAD_DIM)


def _group_specs(S, tn):
    outs = [pl.BlockSpec((n, tn), lambda i: (0, i)) for n in GROUP_ROWS]
    g = pl.BlockSpec((D_MODEL, 1), lambda i: (0, 0))
    mixed = pl.BlockSpec((D_MODEL, tn), lambda i: (0, i))
    return outs, g, mixed


def _group_norm_fwd(oa, of, oc, g, *, name):
    S = oa.shape[1]
    tn = _tile(S, 512)
    outs, gs, mixed = _group_specs(S, tn)

    def body(a_ref, f_ref, c_ref, g_ref, o_ref):
        r0 = 0
        for ref, n in zip((a_ref, f_ref, c_ref), GROUP_ROWS):
            o_ref[r0:r0 + n, :] = _col_rms(ref[...], g_ref[r0:r0 + n, :]).astype(BF16)
            r0 += n

    return pl.pallas_call(
        body, name=name, grid=(S // tn,),
        in_specs=outs + [gs], out_specs=mixed,
        out_shape=jax.ShapeDtypeStruct((D_MODEL, S), BF16),
        compiler_params=_params("parallel"),
    )(oa, of, oc, g)


def _group_norm_bwd(oa, of, oc, g, dmixed, *, name):
    S = oa.shape[1]
    tn = _tile(S, 512)
    outs, gs, mixed = _group_specs(S, tn)

    def body(a_ref, f_ref, c_ref, g_ref, dm_ref, da_ref, df_ref, dc_ref, dg_ref):
        @pl.when(pl.program_id(0) == 0)
        def _():
            dg_ref[...] = jnp.zeros_like(dg_ref)

        r0 = 0
        for ref, dref, n in zip((a_ref, f_ref, c_ref), (da_ref, df_ref, dc_ref), GROUP_ROWS):
            dx, dg = _col_rms_bwd(ref[...], g_ref[r0:r0 + n, :], dm_ref[r0:r0 + n, :])
            dref[...] = dx.astype(BF16)
            dg_ref[r0:r0 + n, :] += dg
            r0 += n

    return pl.pallas_call(
        body, name=name, grid=(S // tn,),
        in_specs=outs + [gs, mixed], out_specs=outs + [gs],
        out_shape=[jax.ShapeDtypeStruct((n, S), BF16) for n in GROUP_ROWS] + [jax.ShapeDtypeStruct((D_MODEL, 1), F32)],
        compiler_params=_params("arbitrary"),
    )(oa, of, oc, g, dmixed)


def _loss_head(y, target):
    S, W = y.shape
    tm = _tile(S, 512)

    def body(y_ref, t_ref, d_ref, l_ref):
        @pl.when(pl.program_id(0) == 0)
        def _():
            l_ref[...] = jnp.zeros_like(l_ref)

        err = y_ref[...] - t_ref[...]
        d_ref[...] = err * (1.0 / W)
        l_ref[...] += 0.5 * jnp.sum(jnp.mean(err * err, axis=-1, keepdims=True), axis=0, keepdims=True)

    row = pl.BlockSpec((tm, W), lambda i: (i, 0))
    d, l = pl.pallas_call(
        body, name="loss_head", grid=(S // tm,),
        in_specs=[row, row],
        out_specs=[row, pl.BlockSpec((1, 1), lambda i: (0, 0))],
        out_shape=[jax.ShapeDtypeStruct((S, W), F32), jax.ShapeDtypeStruct((1, 1), F32)],
        compiler_params=_params("arbitrary"),
    )(y, target)
    return l[0, 0], d


def _attn_fwd(q_src, k_src, v_src, rows, H, Dk, Dv, scale, f_row=None, f_col=None, *, name):
    S = q_src.shape[1]
    T = _tile(S, ATT_TILE)
    nq = S // T
    forget = f_row is not None
    qb, kb, vb = rows[0] // Dk, rows[1] // Dk, rows[2] // Dv

    def body(*refs):
        if forget:
            q_ref, k_ref, v_ref, fq_ref, fk_ref, o_ref, lse_ref, m_s, l_s, acc_s = refs
        else:
            q_ref, k_ref, v_ref, o_ref, lse_ref, m_s, l_s, acc_s = refs
        i = pl.program_id(1)
        m_s[...] = jnp.full_like(m_s, NEG_INF)
        l_s[...] = jnp.zeros_like(l_s)
        acc_s[...] = jnp.zeros_like(acc_s)
        qi = q_ref[...]

        def tile(j, masked):
            off = pl.multiple_of(j * T, T)
            kj = k_ref[:, pl.ds(off, T)]
            vj = v_ref[:, pl.ds(off, T)]
            s = _dot(kj, qi, TN) * scale
            if forget:
                s = s + (fq_ref[0] - fk_ref[0, pl.ds(off, T), :])
            if masked:
                r = lax.broadcasted_iota(jnp.int32, (T, T), 0)
                c = lax.broadcasted_iota(jnp.int32, (T, T), 1)
                s = jnp.where(r <= c, s, NEG_INF)
            m_prev = m_s[...]
            m_new = jnp.maximum(m_prev, jnp.max(s, axis=0, keepdims=True))
            alpha = jnp.exp(m_prev - m_new)
            p = jnp.exp(s - m_new)
            l_s[...] = alpha * l_s[...] + jnp.sum(p, axis=0, keepdims=True)
            p_hi = p.astype(BF16)
            pv = _dot(vj, p_hi)
            if forget:
                pv = pv + _dot(vj, (p - p_hi.astype(F32)).astype(BF16))
            acc_s[...] = alpha * acc_s[...] + pv
            m_s[...] = m_new

        def loop_body(j, carry):
            tile(j, False)
            return carry

        lax.fori_loop(0, i, loop_body, 0)
        tile(i, True)
        o_ref[...] = acc_s[...] / l_s[...]
        lse_ref[0] = m_s[...] + jnp.log(l_s[...])

    in_specs = [pl.BlockSpec((Dk, T), lambda h, i: (qb + h, i)),
                pl.BlockSpec((Dk, S), lambda h, i: (kb + h, 0)),
                pl.BlockSpec((Dv, S), lambda h, i: (vb + h, 0))]
    ins = [q_src, k_src, v_src]
    if forget:
        in_specs += [pl.BlockSpec((1, 1, T), lambda h, i: (h, 0, i)), pl.BlockSpec((1, S, 1), lambda h, i: (h, 0, 0))]
        ins += [f_row, f_col]
    return pl.pallas_call(
        body, name=name, grid=(H, nq),
        in_specs=in_specs,
        out_specs=[pl.BlockSpec((Dv, T), lambda h, i: (h, i)), pl.BlockSpec((1, 1, T), lambda h, i: (h, 0, i))],
        out_shape=[jax.ShapeDtypeStruct((H * Dv, S), F32), jax.ShapeDtypeStruct((H, 1, S), F32)],
        scratch_shapes=[pltpu.VMEM((1, T), F32), pltpu.VMEM((1, T), F32), pltpu.VMEM((Dv, T), F32)],
        compiler_params=_params("parallel", "arbitrary"),
    )(*ins)


def _attn_bwd(q_src, k_src, v_src, rows, H, Dk, Dv, scale, o, do, lse, f_row=None, f_col=None, *, name):
    S = q_src.shape[1]
    T = _tile(S, ATT_TILE)
    nq = S // T
    forget = f_row is not None
    qb, kb, vb = rows[0] // Dk, rows[1] // Dk, rows[2] // Dv

    def body(*refs):
        if forget:
            (q_ref, k_ref, v_ref, o_ref, do_ref, lse_ref, fq_ref, fk_ref,
             dq_ref, dk_ref, dv_ref, df_ref, dk_s, dv_s, df_s) = refs
        else:
            q_ref, k_ref, v_ref, o_ref, do_ref, lse_ref, dq_ref, dk_ref, dv_ref, dk_s, dv_s = refs
        j = pl.program_id(1)

        @pl.when(j == 0)
        def _():
            dq_ref[...] = jnp.zeros_like(dq_ref)

        dk_s[...] = jnp.zeros_like(dk_s)
        dv_s[...] = jnp.zeros_like(dv_s)
        if forget:
            df_s[...] = jnp.zeros_like(df_s)
        kt = k_ref[...]
        kj = kt.T
        vj = v_ref[...].T

        def tile(i, masked):
            cols = pl.ds(pl.multiple_of(i * T, T), T)
            qi = q_ref[:, cols]
            doi = do_ref[:, cols]
            delta = jnp.sum(o_ref[:, cols] * doi.astype(F32), axis=0, keepdims=True)
            st = _dot(kj, qi) * scale
            if forget:
                st = st + (fq_ref[0, :, cols] - fk_ref[0])
            if masked:
                r = lax.broadcasted_iota(jnp.int32, (T, T), 0)
                c = lax.broadcasted_iota(jnp.int32, (T, T), 1)
                st = jnp.where(r <= c, st, NEG_INF)
            pt = jnp.exp(st - lse_ref[0, :, cols])
            dv_s[...] += _dot(doi, pt.astype(BF16), NT)
            dst = pt * (_dot(vj, doi) - delta)
            dsb = dst.astype(BF16)
            dk_s[...] += _dot(qi, dsb, NT)
            dq_ref[:, cols] += _dot(kt, dsb) * scale
            if forget:
                part = dst[:, 0:LANES]
                for c0 in range(LANES, T, LANES):
                    part = part + dst[:, c0:c0 + LANES]
                df_s[...] += part

        tile(j, True)

        def loop_body(i, carry):
            tile(i, False)
            return carry

        lax.fori_loop(j + 1, nq, loop_body, 0)
        dk_ref[...] = dk_s[...] * scale
        dv_ref[...] = dv_s[...]
        if forget:
            df_ref[0] = -jnp.sum(df_s[...], axis=-1, keepdims=True)

    res = lambda D, b0: pl.BlockSpec((D, S), lambda h, j: (b0 + h, 0))
    blk = lambda D, b0: pl.BlockSpec((D, T), lambda h, j: (b0 + h, j))
    row3 = pl.BlockSpec((1, 1, S), lambda h, j: (h, 0, 0))
    in_specs = [res(Dk, qb), blk(Dk, kb), blk(Dv, vb), res(Dv, 0), res(Dv, 0), row3]
    ins = [q_src, k_src, v_src, o, do, lse]
    out_specs = [res(Dk, 0), blk(Dk, 0), blk(Dv, 0)]
    out_shape = [jax.ShapeDtypeStruct((H * Dk, S), F32), jax.ShapeDtypeStruct((H * Dk, S), F32),
                 jax.ShapeDtypeStruct((H * Dv, S), F32)]
    scratch = [pltpu.VMEM((Dk, T), F32), pltpu.VMEM((Dv, T), F32)]
    if forget:
        in_specs += [row3, pl.BlockSpec((1, T, 1), lambda h, j: (h, j, 0))]
        ins += [f_row, f_col]
        out_specs.append(pl.BlockSpec((1, T, 1), lambda h, j: (h, j, 0)))
        out_shape.append(jax.ShapeDtypeStruct((H, S, 1), F32))
        scratch.append(pltpu.VMEM((T, min(T, LANES)), F32))
    return pl.pallas_call(
        body, name=name, grid=(H, nq),
        in_specs=in_specs, out_specs=out_specs, out_shape=out_shape, scratch_shapes=scratch,
        compiler_params=_params("parallel", "arbitrary"),
    )(*ins)


def _swa_masks(i):
    r = lax.broadcasted_iota(jnp.int32, (WINDOW, WINDOW), 0)
    c = lax.broadcasted_iota(jnp.int32, (WINDOW, WINDOW), 1)
    return (r > c) & (i > 0), r <= c


def _swa_specs():
    W = WINDOW
    kv_rows = SWA_KV_HEADS * HEAD_DIM
    q = pl.BlockSpec((SWA_Q_HEADS * HEAD_DIM, W), lambda i: (0, i))
    prev = lambda b: pl.BlockSpec((kv_rows, W), lambda i: (b, jnp.maximum(i - 1, 0)))
    cur = lambda b: pl.BlockSpec((kv_rows, W), lambda i: (b, i))
    bias = pl.BlockSpec((SWA_Q_HEADS, 2 * W, W), lambda i: (0, 0, 0))
    stat = pl.BlockSpec((SWA_Q_HEADS, W), lambda i: (0, i))
    sink = pl.BlockSpec(memory_space=pltpu.SMEM)
    return q, prev(4), cur(4), prev(5), cur(5), bias, stat, sink


def _swa_scores(h, q_ref, kp_ref, kc_ref, b_ref, masks):
    g = h // SWA_GROUP
    rows = slice(g * HEAD_DIM, (g + 1) * HEAD_DIM)
    qh = q_ref[h * HEAD_DIM:(h + 1) * HEAD_DIM, :]
    scale = HEAD_DIM ** -0.5
    s_p = jnp.where(masks[0], _dot(kp_ref[rows, :], qh, TN) * scale + b_ref[h, 0:WINDOW, :], NEG_INF)
    s_c = jnp.where(masks[1], _dot(kc_ref[rows, :], qh, TN) * scale + b_ref[h, WINDOW:2 * WINDOW, :], NEG_INF)
    return qh, rows, s_p, s_c


def _swa_fwd(qkv, bias_t, sinks, *, name):
    S = qkv.shape[1]
    qs, kp, kc, vp, vc, bs, stat, sk = _swa_specs()

    def body(sink_ref, q_ref, kp_ref, kc_ref, vp_ref, vc_ref, b_ref, o_ref, lse_ref):
        masks = _swa_masks(pl.program_id(0))
        for h in range(SWA_Q_HEADS):
            qh, rows, s_p, s_c = _swa_scores(h, q_ref, kp_ref, kc_ref, b_ref, masks)
            sink = sink_ref[h]
            m = jnp.maximum(jnp.maximum(jnp.max(s_p, axis=0, keepdims=True), jnp.max(s_c, axis=0, keepdims=True)), sink)
            p_p = jnp.exp(s_p - m)
            p_c = jnp.exp(s_c - m)
            l = jnp.sum(p_p, axis=0, keepdims=True) + jnp.sum(p_c, axis=0, keepdims=True) + jnp.exp(sink - m)
            o = _dot(vp_ref[rows, :], p_p.astype(BF16)) + _dot(vc_ref[rows, :], p_c.astype(BF16))
            o_ref[h * HEAD_DIM:(h + 1) * HEAD_DIM, :] = o / l
            lse_ref[h:h + 1, :] = m + jnp.log(l)

    return pl.pallas_call(
        body, name=name, grid=(S // WINDOW,),
        in_specs=[sk, qs, kp, kc, vp, vc, bs],
        out_specs=[qs, stat],
        out_shape=[jax.ShapeDtypeStruct((SWA_Q_HEADS * HEAD_DIM, S), F32), jax.ShapeDtypeStruct((SWA_Q_HEADS, S), F32)],
        compiler_params=_params("parallel"),
    )(sinks, qkv, qkv, qkv, qkv, qkv, bias_t)


def _swa_bwd(qkv, bias_t, sinks, o, do, lse, *, name):
    S = qkv.shape[1]
    W = WINDOW
    qs, kp, kc, vp, vc, bs, stat, sk = _swa_specs()
    scale = HEAD_DIM ** -0.5
    kv_rows = SWA_KV_HEADS * HEAD_DIM

    def body(sink_ref, q_ref, kp_ref, kc_ref, vp_ref, vc_ref, b_ref, o_ref, do_ref, lse_ref,
             dq_ref, dkv_ref, db_ref, dsk_ref):
        i = pl.program_id(0)

        @pl.when(i == 0)
        def _():
            dkv_ref[...] = jnp.zeros_like(dkv_ref)
            db_ref[...] = jnp.zeros_like(db_ref)
            dsk_ref[...] = jnp.zeros_like(dsk_ref)

        masks = _swa_masks(i)
        prev = pl.ds(pl.multiple_of(jnp.maximum(i - 1, 0) * W, W), W)
        cur = pl.ds(pl.multiple_of(i * W, W), W)
        for h in range(SWA_Q_HEADS):
            qh, rows, s_p, s_c = _swa_scores(h, q_ref, kp_ref, kc_ref, b_ref, masks)
            vrows = slice(kv_rows + rows.start, kv_rows + rows.stop)
            hrows = slice(h * HEAD_DIM, (h + 1) * HEAD_DIM)
            doh = do_ref[hrows, :]
            lse_h = lse_ref[h:h + 1, :]
            delta = jnp.sum(o_ref[hrows, :] * doh.astype(F32), axis=0, keepdims=True)
            p_p = jnp.exp(s_p - lse_h)
            p_c = jnp.exp(s_c - lse_h)
            ds_p = p_p * (_dot(vp_ref[rows, :], doh, TN) - delta)
            ds_c = p_c * (_dot(vc_ref[rows, :], doh, TN) - delta)
            db_ref[h, 0:W, :] += ds_p
            db_ref[h, W:2 * W, :] += ds_c
            dsk = -jnp.sum(jnp.exp(sink_ref[h] - lse_h) * delta, axis=1, keepdims=True)
            dsk_ref[h:h + 1, :] += jnp.broadcast_to(dsk, (1, LANES))
            dsb_p = ds_p.astype(BF16)
            dsb_c = ds_c.astype(BF16)
            dq_ref[hrows, :] = (_dot(kp_ref[rows, :], dsb_p) + _dot(kc_ref[rows, :], dsb_c)) * scale
            dkv_ref[rows, prev] += _dot(qh, dsb_p, NT) * scale
            dkv_ref[rows, cur] += _dot(qh, dsb_c, NT) * scale
            dkv_ref[vrows, prev] += _dot(doh, p_p.astype(BF16), NT)
            dkv_ref[vrows, cur] += _dot(doh, p_c.astype(BF16), NT)

    return pl.pallas_call(
        body, name=name, grid=(S // W,),
        in_specs=[sk, qs, kp, kc, vp, vc, bs, qs, qs, stat],
        out_specs=[qs, pl.BlockSpec((2 * kv_rows, S), lambda i: (0, 0)), bs, pl.BlockSpec((SWA_Q_HEADS, LANES), lambda i: (0, 0))],
        out_shape=[jax.ShapeDtypeStruct((SWA_Q_HEADS * HEAD_DIM, S), F32), jax.ShapeDtypeStruct((2 * kv_rows, S), F32),
                   jax.ShapeDtypeStruct((SWA_Q_HEADS, 2 * W, W), F32), jax.ShapeDtypeStruct((SWA_Q_HEADS, LANES), F32)],
        compiler_params=_params("arbitrary"),
    )(sinks, qkv, qkv, qkv, qkv, qkv, bias_t, o, do, lse)


def _rel_onehot_t():
    qi = jnp.arange(WINDOW, dtype=jnp.int32)[None, :] + WINDOW
    kj = jnp.arange(2 * WINDOW, dtype=jnp.int32)[:, None]
    dist = qi - kj
    max_exact = REL_BUCKETS // 2
    d = jnp.maximum(dist, 0)
    log_ratio = jnp.log(jnp.maximum(d, 1).astype(F32) / max_exact) / math.log(REL_MAX_DIST / max_exact)
    large = jnp.minimum(max_exact + (log_ratio * (REL_BUCKETS - max_exact)).astype(jnp.int32), REL_BUCKETS - 1)
    bucket = jnp.where(d < max_exact, d, large).reshape(-1)
    return (bucket[None, :] == jnp.arange(REL_BUCKETS, dtype=jnp.int32)[:, None]).astype(BF16)


def _bias_table(rel_bias_t, onehot_t):
    Hq, NB = rel_bias_t.shape
    N = onehot_t.shape[1]
    tn = _tile(N, 4096)

    def body(r_ref, oh_ref, o_ref):
        oh = oh_ref[...]
        a1, a2, a3 = _split3(r_ref[...])
        o_ref[...] = _dot(a1, oh) + _dot(a2, oh) + _dot(a3, oh)

    return pl.pallas_call(
        body, name="rel_bias_table", grid=(N // tn,),
        in_specs=[pl.BlockSpec((Hq, NB), lambda j: (0, 0)), pl.BlockSpec((NB, tn), lambda j: (0, j))],
        out_specs=pl.BlockSpec((Hq, tn), lambda j: (0, j)),
        out_shape=jax.ShapeDtypeStruct((Hq, N), F32),
        compiler_params=_params("parallel"),
    )(rel_bias_t, onehot_t)


def _bias_table_bwd(dbias, onehot_t):
    L, Hq, N = dbias.shape
    NB = onehot_t.shape[0]
    tn = _tile(N, 4096)

    def body(d_ref, oh_ref, o_ref):
        @pl.when(pl.program_id(0) == 0)
        def _():
            o_ref[...] = jnp.zeros_like(o_ref)

        d = d_ref[0]
        for l in range(1, L):
            d = d + d_ref[l]
        oh = oh_ref[...]
        a1, a2, a3 = _split3(d)
        o_ref[...] += _dot(a1, oh, NT) + _dot(a2, oh, NT) + _dot(a3, oh, NT)

    return pl.pallas_call(
        body, name="rel_bias_bwd", grid=(N // tn,),
        in_specs=[pl.BlockSpec((L, Hq, tn), lambda j: (0, 0, j)), pl.BlockSpec((NB, tn), lambda j: (0, j))],
        out_specs=pl.BlockSpec((Hq, NB), lambda j: (0, 0)),
        out_shape=jax.ShapeDtypeStruct((Hq, NB), F32),
        compiler_params=_params("arbitrary"),
    )(dbias, onehot_t)


def _gate_fwd(lat, fb_col, *, name):
    S = lat.shape[1]
    tn = _tile(S, 256)

    def body(z_ref, fb_ref, o_ref, carry):
        @pl.when(pl.program_id(0) == 0)
        def _():
            carry[...] = jnp.zeros_like(carry)

        z = z_ref[...] + fb_ref[...]
        lf = jnp.minimum(z, 0.0) - jnp.log1p(jnp.exp(-jnp.abs(z)))
        r = lax.broadcasted_iota(jnp.int32, (tn, tn), 0)
        c = lax.broadcasted_iota(jnp.int32, (tn, tn), 1)
        tri = (r <= c).astype(BF16)
        a1, a2, a3 = _split3(lf)
        cum = _dot(a1, tri) + _dot(a2, tri) + _dot(a3, tri) + carry[:, 0:1]
        o_ref[...] = cum
        carry[...] = jnp.broadcast_to(cum[:, tn - 1:tn], carry.shape)

    return pl.pallas_call(
        body, name=name, grid=(S // tn,),
        in_specs=[pl.BlockSpec((GATE_ROWS, tn), lambda i: (0, i)), pl.BlockSpec((GATE_ROWS, 1), lambda i: (0, 0))],
        out_specs=pl.BlockSpec((GATE_ROWS, tn), lambda i: (0, i)),
        out_shape=jax.ShapeDtypeStruct((GATE_ROWS, S), F32),
        scratch_shapes=[pltpu.VMEM((GATE_ROWS, LANES), F32)],
        compiler_params=_params("arbitrary"),
    )(lat, fb_col)


def _gate_bwd(lat, fb_col, dF, *, name):
    S = lat.shape[1]
    tn = _tile(S, 256)
    nt = S // tn

    def body(z_ref, fb_ref, df_ref, dz_ref, dfb_ref, carry):
        @pl.when(pl.program_id(0) == 0)
        def _():
            carry[...] = jnp.zeros_like(carry)
            dfb_ref[...] = jnp.zeros_like(dfb_ref)

        r = lax.broadcasted_iota(jnp.int32, (tn, tn), 0)
        c = lax.broadcasted_iota(jnp.int32, (tn, tn), 1)
        tri = (r >= c).astype(BF16)
        a1, a2, a3 = _split3(df_ref[...])
        dlf = _dot(a1, tri) + _dot(a2, tri) + _dot(a3, tri) + carry[:, 0:1]
        carry[...] = jnp.broadcast_to(dlf[:, 0:1], carry.shape)
        z = z_ref[...] + fb_ref[...]
        row = lax.broadcasted_iota(jnp.int32, (GATE_ROWS, tn), 0)
        dz = jnp.where(row < FOX_HEADS, dlf / (1.0 + jnp.exp(z)), 0.0)
        dz_ref[...] = dz
        dfb_ref[...] += jnp.sum(dz, axis=1, keepdims=True)

    blk = pl.BlockSpec((GATE_ROWS, tn), lambda i: (0, nt - 1 - i))
    vec = pl.BlockSpec((GATE_ROWS, 1), lambda i: (0, 0))
    return pl.pallas_call(
        body, name=name, grid=(nt,),
        in_specs=[blk, vec, blk], out_specs=[blk, vec],
        out_shape=[jax.ShapeDtypeStruct((GATE_ROWS, S), F32), jax.ShapeDtypeStruct((GATE_ROWS, 1), F32)],
        scratch_shapes=[pltpu.VMEM((GATE_ROWS, LANES), F32)],
        compiler_params=_params("arbitrary"),
    )(lat, fb_col, dF)


def _rope_tables(S):
    pos = jnp.arange(S, dtype=F32)
    inv_freq = ROPE_THETA ** (-(jnp.arange(MLA_ROPE // 2, dtype=F32) * 2.0 / MLA_ROPE))
    ang = pos[:, None] * inv_freq[None, :]
    cos, sin = jnp.cos(ang).T, jnp.sin(ang).T
    z16 = jnp.zeros_like(cos)

    def slab(lo, fill):
        def put(first, second, f):
            return jnp.concatenate([jnp.full((lo, S), f, F32), first, second, jnp.full((LANES - lo - MLA_ROPE, S), f, F32)], axis=0)
        return put(cos, cos, fill), put(-sin, z16, 0.0), put(z16, sin, 0.0)

    tq = tuple(jnp.tile(t, (MLA_HEADS, 1)) for t in slab(MLA_NOPE, 1.0))
    return tq, slab(0, 0.0)


def _rope(x, c, s1, s2):
    n = x.shape[0]
    half = MLA_ROPE // 2
    return x * c + pltpu.roll(x, n - half, 0) * s1 + pltpu.roll(x, half, 0) * s2


def _rope_t(dy, c, s1, s2):
    n = dy.shape[0]
    half = MLA_ROPE // 2
    return dy * c + pltpu.roll(dy * s1, half, 0) + pltpu.roll(dy * s2, n - half, 0)


KR_SLAB0 = MLA_Q_RANK + MLA_KV_RANK


def _mla_prep_fwd(lat, g_q, g_kv, w_uq_t, w_ukv_t, tq, tmisc, *, name):
    S = lat.shape[1]
    tn = _tile(S, 512)
    QW = MLA_HEADS * MLA_PAD

    def body(lat_ref, gq_ref, gkv_ref, wq_ref, wkv_ref, c_ref, s1_ref, s2_ref, cm_ref, s1m_ref, s2m_ref,
             nq_ref, nkv_ref, q_ref, k_ref, v_ref):
        x = pltpu.roll(lat_ref[...], LAT_ROWS - LAT_SHIFT, 0)
        nq = _col_rms(x[0:MLA_Q_RANK, :], gq_ref[...]).astype(BF16)
        nkv = _col_rms(x[MLA_Q_RANK:KR_SLAB0, :], gkv_ref[...]).astype(BF16)
        nq_ref[...] = nq
        nkv_ref[...] = nkv
        q_ref[...] = _rope(_dot(wq_ref[...], nq), c_ref[...], s1_ref[...], s2_ref[...]).astype(BF16)
        kv = _dot(wkv_ref[...], nkv).astype(BF16)
        kr = _rope(x[KR_SLAB0:LAT_ROWS, :], cm_ref[...], s1m_ref[...], s2m_ref[...]).astype(BF16)
        for h in range(MLA_HEADS):
            k_ref[h * MLA_PAD:h * MLA_PAD + MLA_NOPE, :] = kv[h * LANES:h * LANES + MLA_NOPE, :]
            k_ref[h * MLA_PAD + MLA_NOPE:(h + 1) * MLA_PAD, :] = kr[0:MLA_PAD - MLA_NOPE, :]
            v_ref[h * HEAD_DIM:(h + 1) * HEAD_DIM, :] = kv[h * LANES + MLA_NOPE:(h + 1) * LANES, :]

    def col(rows):
        return pl.BlockSpec((rows, tn), lambda i: (0, i))

    def full(a):
        return pl.BlockSpec(a.shape, lambda i: (0, 0))

    return pl.pallas_call(
        body, name=name, grid=(S // tn,),
        in_specs=[col(LAT_ROWS), full(g_q), full(g_kv), full(w_uq_t), full(w_ukv_t),
                  col(QW), col(QW), col(QW), col(LANES), col(LANES), col(LANES)],
        out_specs=[col(MLA_Q_RANK), col(MLA_KV_RANK), col(QW), col(QW), col(MLA_HEADS * HEAD_DIM)],
        out_shape=[jax.ShapeDtypeStruct((MLA_Q_RANK, S), BF16), jax.ShapeDtypeStruct((MLA_KV_RANK, S), BF16),
                   jax.ShapeDtypeStruct((QW, S), BF16), jax.ShapeDtypeStruct((QW, S), BF16),
                   jax.ShapeDtypeStruct((MLA_HEADS * HEAD_DIM, S), BF16)],
        compiler_params=_params("parallel"),
    )(lat, g_q, g_kv, w_uq_t, w_ukv_t, *tq, *tmisc)


def _mla_prep_bwd(lat, nq, nkv, g_q, g_kv, w_uq_p, w_ukv, tq, tmisc, dq, dk, dv, dflog, *, name):
    S = lat.shape[1]
    tn = _tile(S, 512)
    QW = MLA_HEADS * MLA_PAD

    def body(lat_ref, nq_ref, nkv_ref, gq_ref, gkv_ref, wq_ref, wkv_ref, c_ref, s1_ref, s2_ref,
             cm_ref, s1m_ref, s2m_ref, dq_ref, dk_ref, dv_ref, dfl_ref,
             dlat_ref, dwq_ref, dwkv_ref, dgq_ref, dgkv_ref, y_s):
        @pl.when(pl.program_id(0) == 0)
        def _():
            dwq_ref[...] = jnp.zeros_like(dwq_ref)
            dwkv_ref[...] = jnp.zeros_like(dwkv_ref)
            dgq_ref[...] = jnp.zeros_like(dgq_ref)
            dgkv_ref[...] = jnp.zeros_like(dgkv_ref)

        x = pltpu.roll(lat_ref[...], LAT_ROWS - LAT_SHIFT, 0)
        dqm = _rope_t(dq_ref[...], c_ref[...], s1_ref[...], s2_ref[...]).astype(BF16)
        dwq_ref[...] += _dot(dqm, nq_ref[...], NT)
        dx, dg = _col_rms_bwd(x[0:MLA_Q_RANK, :], gq_ref[...], _dot(wq_ref[...], dqm))
        y_s[0:MLA_Q_RANK, :] = dx
        dgq_ref[...] += dg
        dkv = jnp.concatenate(
            [part for h in range(MLA_HEADS)
             for part in (dk_ref[h * MLA_PAD:h * MLA_PAD + MLA_NOPE, :], dv_ref[h * HEAD_DIM:(h + 1) * HEAD_DIM, :])],
            axis=0).astype(BF16)
        dwkv_ref[...] += _dot(dkv, nkv_ref[...], NT)
        dx, dg = _col_rms_bwd(x[MLA_Q_RANK:KR_SLAB0, :], gkv_ref[...], _dot(wkv_ref[...], dkv))
        y_s[MLA_Q_RANK:KR_SLAB0, :] = dx
        dgkv_ref[...] += dg
        dkr = dk_ref[MLA_NOPE:MLA_PAD, :]
        for h in range(1, MLA_HEADS):
            dkr = dkr + dk_ref[h * MLA_PAD + MLA_NOPE:(h + 1) * MLA_PAD, :]
        dkr = jnp.concatenate([dkr, jnp.zeros((MLA_NOPE, tn), F32)], axis=0)
        y_s[KR_SLAB0:LAT_ROWS, :] = _rope_t(dkr, cm_ref[...], s1m_ref[...], s2m_ref[...])
        y = pltpu.roll(y_s[...], LAT_SHIFT, 0)
        row = lax.broadcasted_iota(jnp.int32, (LAT_ROWS, tn), 0)
        dfl = jnp.concatenate([dfl_ref[...], jnp.zeros((LAT_ROWS - GATE_ROWS, tn), F32)], axis=0)
        dlat_ref[...] = jnp.where(row < LAT_SHIFT, dfl, y).astype(BF16)

    def col(rows):
        return pl.BlockSpec((rows, tn), lambda i: (0, i))

    def full(a):
        return pl.BlockSpec(a.shape, lambda i: (0, 0))

    def acc(r, c):
        return pl.BlockSpec((r, c), lambda i: (0, 0))

    return pl.pallas_call(
        body, name=name, grid=(S // tn,),
        in_specs=[col(LAT_ROWS), col(MLA_Q_RANK), col(MLA_KV_RANK), full(g_q), full(g_kv),
                  full(w_uq_p), full(w_ukv), col(QW), col(QW), col(QW), col(LANES), col(LANES), col(LANES),
                  col(QW), col(QW), col(MLA_HEADS * HEAD_DIM), col(GATE_ROWS)],
        out_specs=[col(LAT_ROWS), acc(QW, MLA_Q_RANK), acc(QW, MLA_KV_RANK), acc(MLA_Q_RANK, 1), acc(MLA_KV_RANK, 1)],
        out_shape=[jax.ShapeDtypeStruct((LAT_ROWS, S), BF16), jax.ShapeDtypeStruct((QW, MLA_Q_RANK), F32),
                   jax.ShapeDtypeStruct((QW, MLA_KV_RANK), F32), jax.ShapeDtypeStruct((MLA_Q_RANK, 1), F32),
                   jax.ShapeDtypeStruct((MLA_KV_RANK, 1), F32)],
        scratch_shapes=[pltpu.VMEM((LAT_ROWS, tn), F32)],
        compiler_params=_params("arbitrary"),
    )(lat, nq, nkv, g_q, g_kv, w_uq_p, w_ukv, *tq, *tmisc, dq, dk, dv, dflog)


def _dproj_cast(dqa, dkva, dqf, dkf, dvf, dlat, *, name):
    S = dqa.shape[1]
    tn = _tile(S, 512)
    parts = (dqa, dkva, dqf, dkf, dvf, dlat)

    def body(*refs):
        o_ref = refs[-1]
        r0 = 0
        for ref in refs[:-1]:
            n = ref.shape[0]
            o_ref[r0:r0 + n, :] = ref[...].astype(BF16)
            r0 += n

    return pl.pallas_call(
        body, name=name, grid=(S // tn,),
        in_specs=[pl.BlockSpec((p.shape[0], tn), lambda i: (0, i)) for p in parts],
        out_specs=pl.BlockSpec((IN_ROWS, tn), lambda i: (0, i)),
        out_shape=jax.ShapeDtypeStruct((IN_ROWS, S), BF16),
        compiler_params=_params("parallel"),
    )(*parts)


GELU_C = math.sqrt(2.0 / math.pi)
GELU_A = 0.044715


def _conv_taps(a, halo, w_ref, b_ref, first):
    row = lax.broadcasted_iota(jnp.int32, a.shape, 0)
    h7 = jnp.where(first, 0.0, halo[7:8, :])
    h6 = jnp.where(first, 0.0, halo[6:7, :])
    a1 = jnp.where(row == 0, h7, pltpu.roll(a, 1, 0))
    a2 = jnp.where(row == 0, h6, jnp.where(row == 1, h7, pltpu.roll(a, 2, 0)))
    u = ((b_ref[...] + w_ref[0:1, :] * a2) + w_ref[1:2, :] * a1) + w_ref[2:3, :] * a
    return u, a1, a2


def _conv_specs(S, tm, tc, nc):
    hb = tm // 8
    main = lambda off: pl.BlockSpec((tm, tc), lambda j, i: (i, j + off))
    halo = lambda off: pl.BlockSpec((8, tc), lambda j, i: (jnp.maximum(i * hb - 1, 0), j + off))
    wspec = lambda off: pl.BlockSpec((3, tc), lambda j, i: (0, j + off))
    bspec = lambda off: pl.BlockSpec((1, tc), lambda j, i: (0, j + off))
    return main, halo, wspec, bspec


def _conv_geglu_fwd(a, conv_w, conv_b, *, name):
    S = a.shape[0]
    tm, tc = _tile(S, 512), _tile(D_FF, 1408)
    nc = D_FF // tc
    main, halo, wspec, bspec = _conv_specs(S, tm, tc, nc)

    def body(ag_ref, au_ref, hg_ref, hu_ref, wg_ref, wu_ref, bg_ref, bu_ref, z_ref):
        first = pl.program_id(1) == 0
        gate, _, _ = _conv_taps(ag_ref[...], hg_ref[...], wg_ref, bg_ref, first)
        up, _, _ = _conv_taps(au_ref[...], hu_ref[...], wu_ref, bu_ref, first)
        cdf = 0.5 * (1.0 + jnp.tanh(GELU_C * (gate + GELU_A * (gate * gate * gate))))
        z_ref[...] = (gate * cdf * up).astype(BF16)

    return pl.pallas_call(
        body, name=name, grid=(nc, S // tm),
        in_specs=[main(0), main(nc), halo(0), halo(nc), wspec(0), wspec(nc), bspec(0), bspec(nc)],
        out_specs=pl.BlockSpec((tm, tc), lambda j, i: (i, j)),
        out_shape=jax.ShapeDtypeStruct((S, D_FF), BF16),
        compiler_params=_params("parallel", "arbitrary"),
    )(a, a, a, a, conv_w, conv_w, conv_b, conv_b)


def _conv_geglu_bwd(a, conv_w, conv_b, dz, *, name):
    S = a.shape[0]
    tm, tc = _tile(S, 512), _tile(D_FF, 1408)
    nc = D_FF // tc
    main, halo, wspec, bspec = _conv_specs(S, tm, tc, nc)

    def body(ag_ref, au_ref, hg_ref, hu_ref, wg_ref, wu_ref, bg_ref, bu_ref, dz_ref, du_ref, dw_ref, db_ref):
        first = pl.program_id(1) == 0

        @pl.when(first)
        def _():
            dw_ref[...] = jnp.zeros_like(dw_ref)
            db_ref[...] = jnp.zeros_like(db_ref)

        gate, g1, g2 = _conv_taps(ag_ref[...], hg_ref[...], wg_ref, bg_ref, first)
        up, u1, u2 = _conv_taps(au_ref[...], hu_ref[...], wu_ref, bu_ref, first)
        dz = dz_ref[...]
        g2x = gate * gate
        th = jnp.tanh(GELU_C * (gate + GELU_A * (g2x * gate)))
        cdf = 0.5 * (1.0 + th)
        dgelu = cdf + gate * (0.5 * (1.0 - th * th) * (GELU_C * (1.0 + 3.0 * GELU_A * g2x)))
        dug = dz * up * dgelu
        duu = dz * (gate * cdf)
        du_ref[0] = dug
        du_ref[1] = duu
        for half, du, taps in ((0, dug, (g2, g1, ag_ref[...])), (1, duu, (u2, u1, au_ref[...]))):
            for tap in range(3):
                dw_ref[half, tap:tap + 1, :] += jnp.sum(du * taps[tap], axis=0, keepdims=True)
            db_ref[half] += jnp.sum(du, axis=0, keepdims=True)

    return pl.pallas_call(
        body, name=name, grid=(nc, S // tm),
        in_specs=[main(0), main(nc), halo(0), halo(nc), wspec(0), wspec(nc), bspec(0), bspec(nc),
                  pl.BlockSpec((tm, tc), lambda j, i: (i, j))],
        out_specs=[pl.BlockSpec((2, tm, tc), lambda j, i: (0, i, j)), pl.BlockSpec((2, 3, tc), lambda j, i: (0, 0, j)),
                   pl.BlockSpec((2, 1, tc), lambda j, i: (0, 0, j))],
        out_shape=[jax.ShapeDtypeStruct((2, S, D_FF), F32), jax.ShapeDtypeStruct((2, 3, D_FF), F32),
                   jax.ShapeDtypeStruct((2, 1, D_FF), F32)],
        compiler_params=_params("parallel", "arbitrary"),
    )(a, a, a, a, conv_w, conv_w, conv_b, conv_b, dz)


def _conv_bwd_input(du, conv_w, *, name):
    S = du.shape[1]
    tm, tc = _tile(S, 512), _tile(D_FF, 1408)
    nc = D_FF // tc
    nr = S // tm
    hb = tm // 8

    def body(du_ref, nx_ref, w_ref, da_ref):
        last = pl.program_id(2) == nr - 1
        d = du_ref[0]
        row = lax.broadcasted_iota(jnp.int32, d.shape, 0)
        n0 = jnp.where(last, 0.0, nx_ref[0, 0:1, :])
        n1 = jnp.where(last, 0.0, nx_ref[0, 1:2, :])
        d1 = jnp.where(row == tm - 1, n0, pltpu.roll(d, tm - 1, 0))
        d2 = jnp.where(row == tm - 1, n1, jnp.where(row == tm - 2, n0, pltpu.roll(d, tm - 2, 0)))
        da_ref[...] = (w_ref[2:3, :] * d + w_ref[1:2, :] * d1 + w_ref[0:1, :] * d2).astype(BF16)

    return pl.pallas_call(
        body, name=name, grid=(2, nc, nr),
        in_specs=[pl.BlockSpec((1, tm, tc), lambda h, j, i: (h, i, j)),
                  pl.BlockSpec((1, 8, tc), lambda h, j, i: (h, jnp.minimum((i + 1) * hb, S // 8 - 1), j)),
                  pl.BlockSpec((3, tc), lambda h, j, i: (0, h * nc + j))],
        out_specs=pl.BlockSpec((tm, tc), lambda h, j, i: (i, h * nc + j)),
        out_shape=jax.ShapeDtypeStruct((S, 2 * D_FF), BF16),
        compiler_params=_params("parallel", "parallel", "arbitrary"),
    )(du, du, conv_w)


def _adamw(w, g, m, v, *, name):
    L, A, B = w.shape
    ta = _tile(A, ROW_TILE)

    def body(w_ref, g_ref, m_ref, v_ref, d_ref, mo_ref, vo_ref):
        g = g_ref[...]
        m = ADAM_B1 * m_ref[...] + (1.0 - ADAM_B1) * g
        v = ADAM_B2 * v_ref[...] + (1.0 - ADAM_B2) * jnp.square(g)
        m_hat = m / (1.0 - ADAM_B1 ** ADAM_STEP)
        v_hat = v / (1.0 - ADAM_B2 ** ADAM_STEP)
        d_ref[...] = -ADAM_LR * (m_hat / (jnp.sqrt(v_hat) + ADAM_EPS) + ADAM_WD * w_ref[...])
        mo_ref[...] = m
        vo_ref[...] = v

    blk = pl.BlockSpec((None, ta, B), lambda l, i: (l, i, 0))
    shp = jax.ShapeDtypeStruct((L, A, B), F32)
    return pl.pallas_call(
        body, name=name, grid=(L, A // ta),
        in_specs=[blk] * 4, out_specs=[blk] * 3, out_shape=[shp] * 3,
        compiler_params=_params("parallel", "parallel"),
    )(w, g, m, v)


def _core_index():
    return jnp.reshape(lax.axis_index("c"), (1,)).astype(jnp.int32)


def _pair_sum(g, recv, *, name):
    L, n, A, B = g.shape
    ta = _tile(A, ROW_TILE)

    def body(c_ref, g_ref, r_ref, o_ref):
        o_ref[...] = g_ref[...] + r_ref[...]

    return pl.pallas_call(
        body, name=name,
        grid_spec=pltpu.PrefetchScalarGridSpec(
            num_scalar_prefetch=1, grid=(HALF_DEPTH, n, A // ta),
            in_specs=[pl.BlockSpec((None, None, ta, B), lambda l, s, r, c_ref: (HALF_DEPTH * c_ref[0] + l, s, r, 0)),
                      pl.BlockSpec((None, None, ta, B), lambda l, s, r, c_ref: (l, s, r, 0))],
            out_specs=pl.BlockSpec((None, None, ta, B), lambda l, s, r, c_ref: (l, s, r, 0))),
        out_shape=jax.ShapeDtypeStruct((HALF_DEPTH, n, A, B), F32),
        compiler_params=_params("parallel", "parallel", "parallel"),
    )(_core_index(), g, recv)


def _chip_sum(parts, *, name):
    n, L2, A, B = parts.shape
    ta = _tile(A, ROW_TILE)

    def body(p_ref, o_ref):
        o_ref[...] = ((p_ref[0] + p_ref[1]) + p_ref[2]) + p_ref[3]

    return pl.pallas_call(
        body, name=name, grid=(L2, A // ta),
        in_specs=[pl.BlockSpec((n, None, ta, B), lambda l, r: (0, l, r, 0))],
        out_specs=pl.BlockSpec((None, ta, B), lambda l, r: (l, r, 0)),
        out_shape=jax.ShapeDtypeStruct((L2, A, B), F32),
        compiler_params=_params("parallel", "parallel"),
    )(parts)


HBM_SPEC = pl.BlockSpec(memory_space=pl.ANY)
COMM_PARAMS = pltpu.CompilerParams(has_side_effects=True)


def _mesh_pos():
    return lax.axis_index("x"), lax.axis_index("y"), lax.axis_index("c")


def _other_chips(x, y):
    return [(1 - x, y), (x, 1 - y), (1 - x, 1 - y)]


def _my_layers(c):
    return pl.ds(HALF_DEPTH * c, HALF_DEPTH)


def _remote(src, dst, send_sems, recv_sems, k, to):
    return pltpu.make_async_remote_copy(src_ref=src, dst_ref=dst, send_sem=send_sems.at[k], recv_sem=recv_sems.at[k],
                                        device_id=to, device_id_type=MESH)


def _gather_shards(shards):
    n = len(shards)

    def body(*refs):
        ins, outs = refs[:n], refs[n:2 * n]
        send_sems, recv_sems, local_sems = refs[2 * n:]
        x, y, c = _mesh_pos()
        me = 2 * x + y
        chips = _other_chips(x, y)
        sibling = (x, y, 1 - c)
        mine, other = _my_layers(c), _my_layers(1 - c)
        local = [pltpu.make_async_copy(ins[k], outs[k].at[:, me], local_sems.at[k]) for k in range(n)]
        for cp in local:
            cp.start()
        first = [_remote(ins[k].at[mine], outs[k].at[mine, me], send_sems, recv_sems, 6 * k + j, (px, py, c))
                 for j, (px, py) in enumerate(chips) for k in range(n)]
        for cp in first:
            cp.start()
        passed = []
        for j, (px, py) in enumerate(chips):
            for k in range(n):
                landed = outs[k].at[mine, 2 * px + py]
                _remote(landed, landed, send_sems, recv_sems, 6 * k + j, (px, py, c)).wait_recv()
                cp = _remote(landed, landed, send_sems, recv_sems, 6 * k + 3 + j, sibling)
                cp.start()
                passed.append(cp)
        for j, (px, py) in enumerate(chips):
            for k in range(n):
                landed = outs[k].at[other, 2 * px + py]
                _remote(landed, landed, send_sems, recv_sems, 6 * k + 3 + j, sibling).wait_recv()
        for cp in first + passed:
            cp.wait_send()
        for cp in local:
            cp.wait()

    return pl.pallas_call(
        body, name="gather_weight_shards",
        in_specs=[HBM_SPEC] * n, out_specs=[HBM_SPEC] * n,
        out_shape=[jax.ShapeDtypeStruct((s.shape[0], N_CHIPS) + s.shape[1:], s.dtype) for s in shards],
        scratch_shapes=[pltpu.SemaphoreType.DMA((6 * n,)), pltpu.SemaphoreType.DMA((6 * n,)), pltpu.SemaphoreType.DMA((n,))],
        compiler_params=COMM_PARAMS,
    )(*shards)


def _sibling_exchange(gs):
    n = len(gs)

    def body(*refs):
        ins, outs = refs[:n], refs[n:2 * n]
        send_sems, recv_sems = refs[2 * n:]
        x, y, c = _mesh_pos()
        copies = [_remote(ins[k].at[_my_layers(1 - c)], outs[k], send_sems, recv_sems, k, (x, y, 1 - c)) for k in range(n)]
        for cp in copies:
            cp.start()
        for cp in copies:
            cp.wait()

    return pl.pallas_call(
        body, name="grad_sibling_exchange",
        in_specs=[HBM_SPEC] * n, out_specs=[HBM_SPEC] * n,
        out_shape=[jax.ShapeDtypeStruct((HALF_DEPTH,) + g.shape[1:], g.dtype) for g in gs],
        scratch_shapes=[pltpu.SemaphoreType.DMA((n,)), pltpu.SemaphoreType.DMA((n,))],
        compiler_params=COMM_PARAMS,
    )(*gs)


def _chip_scatter(ps):
    n = len(ps)

    def body(*refs):
        ins, outs = refs[:n], refs[n:2 * n]
        send_sems, recv_sems, local_sems = refs[2 * n:]
        x, y, c = _mesh_pos()
        me = 2 * x + y
        chips = _other_chips(x, y)
        local = [pltpu.make_async_copy(ins[k].at[:, me], outs[k].at[me], local_sems.at[k]) for k in range(n)]
        for cp in local:
            cp.start()
        copies = [_remote(ins[k].at[:, 2 * px + py], outs[k].at[me], send_sems, recv_sems, 3 * k + j, (px, py, c))
                  for j, (px, py) in enumerate(chips) for k in range(n)]
        for cp in copies:
            cp.start()
        for j, (px, py) in enumerate(chips):
            for k in range(n):
                landed = outs[k].at[2 * px + py]
                _remote(landed, landed, send_sems, recv_sems, 3 * k + j, (px, py, c)).wait_recv()
        for cp in copies:
            cp.wait_send()
        for cp in local:
            cp.wait()

    return pl.pallas_call(
        body, name="grad_chip_scatter",
        in_specs=[HBM_SPEC] * n, out_specs=[HBM_SPEC] * n,
        out_shape=[jax.ShapeDtypeStruct((N_CHIPS, p.shape[0]) + p.shape[2:], p.dtype) for p in ps],
        scratch_shapes=[pltpu.SemaphoreType.DMA((3 * n,)), pltpu.SemaphoreType.DMA((3 * n,)), pltpu.SemaphoreType.DMA((n,))],
        compiler_params=COMM_PARAMS,
    )(*ps)


def _sibling_share(hs):
    n = len(hs)

    def body(*refs):
        ins, outs = refs[:n], refs[n:2 * n]
        send_sems, recv_sems, local_sems = refs[2 * n:]
        x, y, c = _mesh_pos()
        mine, other = _my_layers(c), _my_layers(1 - c)
        local = [pltpu.make_async_copy(ins[k], outs[k].at[mine], local_sems.at[k]) for k in range(n)]
        for cp in local:
            cp.start()
        copies = [_remote(ins[k], outs[k].at[mine], send_sems, recv_sems, k, (x, y, 1 - c)) for k in range(n)]
        for cp in copies:
            cp.start()
        for k in range(n):
            landed = outs[k].at[other]
            _remote(landed, landed, send_sems, recv_sems, k, (x, y, 1 - c)).wait_recv()
        for cp in copies:
            cp.wait_send()
        for cp in local:
            cp.wait()

    return pl.pallas_call(
        body, name="grad_sibling_share",
        in_specs=[HBM_SPEC] * n, out_specs=[HBM_SPEC] * n,
        out_shape=[jax.ShapeDtypeStruct((DEPTH,) + h.shape[1:], h.dtype) for h in hs],
        scratch_shapes=[pltpu.SemaphoreType.DMA((n,)), pltpu.SemaphoreType.DMA((n,)), pltpu.SemaphoreType.DMA((n,))],
        compiler_params=COMM_PARAMS,
    )(*hs)


def _allreduce_small(part):
    rows, C = part.shape

    def body(p_ref, o_ref, slots, send_sems, recv_sems):
        x, y, c = _mesh_pos()
        me = 4 * x + 2 * y + c
        slots[me] = p_ref[...]
        copies = []
        for k in range(1, 8):
            kx, ky, kc = (k >> 2) & 1, (k >> 1) & 1, k & 1
            peer = (x ^ kx if kx else x, y ^ ky if ky else y, c ^ kc if kc else c)
            cp = _remote(p_ref, slots.at[me], send_sems, recv_sems, k - 1, peer)
            cp.start()
            copies.append((cp, peer))
        for k, (cp, peer) in enumerate(copies):
            src = 4 * peer[0] + 2 * peer[1] + peer[2]
            _remote(p_ref, slots.at[src], send_sems, recv_sems, k, peer).wait_recv()
        for cp, _ in copies:
            cp.wait_send()
        total = slots[0]
        for d in range(1, 8):
            total = total + slots[d]
        o_ref[...] = total

    return pl.pallas_call(
        body, name="small_grad_allreduce",
        in_specs=[pl.BlockSpec(memory_space=pltpu.VMEM)], out_specs=pl.BlockSpec(memory_space=pltpu.VMEM),
        out_shape=jax.ShapeDtypeStruct((rows, C), F32),
        scratch_shapes=[pltpu.VMEM((8, rows, C), F32), pltpu.SemaphoreType.DMA((7,)), pltpu.SemaphoreType.DMA((7,))],
        compiler_params=pltpu.CompilerParams(has_side_effects=True, vmem_limit_bytes=VMEM_LIMIT_BYTES),
    )(part)


def _pad_w_uq(w):
    lead = w.shape[:-1]
    w = w.reshape(lead + (MLA_HEADS, MLA_QK))
    w = jnp.concatenate([w, jnp.zeros(lead + (MLA_HEADS, MLA_PAD - MLA_QK), w.dtype)], axis=-1)
    return w.reshape(lead + (MLA_HEADS * MLA_PAD,))


def _unpad_w_uq(g):
    lead = g.shape[:-1]
    return g.reshape(lead + (MLA_HEADS, MLA_PAD))[..., :MLA_QK].reshape(lead + (MLA_HEADS * MLA_QK,))


def _t(a):
    return jnp.swapaxes(a, -1, -2)


def _cols_of_shards(g):
    L, n, A, B = g.shape
    return g.transpose(0, 2, 1, 3).reshape(L, A, n * B)


def _shards_of_cols(w):
    A, NB = w.shape
    return w.reshape(A, N_CHIPS, NB // N_CHIPS).transpose(1, 0, 2)


BIG = ("w_in", "w_uq", "w_ukv", "w_out", "w_up", "w_down")
SMALL = ("attn_pre_norm", "forget_bias", "swa_sinks", "rel_bias", "q_latent_norm", "kv_latent_norm", "group_norm",
         "attn_post_norm", "ffn_pre_norm", "conv_b", "ffn_post_norm")
WEIGHTS = ("attn_pre_norm", "w_in", "forget_bias", "swa_sinks", "rel_bias", "q_latent_norm", "w_uq", "kv_latent_norm",
           "w_ukv", "group_norm", "w_out", "attn_post_norm", "ffn_pre_norm", "w_up", "conv_w", "conv_b", "w_down",
           "ffn_post_norm")


def _pack(arrs, cols, row_mult):
    flat = jnp.concatenate([a.reshape(-1) for a in arrs])
    n = flat.shape[0]
    per = cols * row_mult
    total = -(-n // per) * per
    return jnp.pad(flat, (0, total - n)).reshape(total // cols, cols)


def _unpack(packed, shapes):
    flat = packed.reshape(-1)
    out, off = [], 0
    for shp in shapes:
        n = int(np.prod(shp))
        out.append(flat[off:off + n].reshape(shp))
        off += n
    return out


def _kernel_weights(gathered, small):
    L = gathered["w_in"].shape[0]
    w_in_t = _t(gathered["w_in"]).reshape(L, IN_COLS, D_MODEL)
    w_in_t = jnp.pad(w_in_t, ((0, 0), (0, IN_ROWS - IN_COLS), (0, 0)))
    w_uq_p = _pad_w_uq(_cols_of_shards(gathered["w_uq"]))
    w_ukv = _cols_of_shards(gathered["w_ukv"])
    W = dict(small)
    W.update(w_qkv_t=w_in_t[:, :QKV_ROWS], w_lat_t=w_in_t[:, QKV_ROWS:], w_in_t=w_in_t, w_uq_p=w_uq_p, w_uq_t=_t(w_uq_p),
             w_ukv=w_ukv, w_ukv_t=_t(w_ukv), w_out=gathered["w_out"].reshape(L, D_MODEL, D_MODEL), w_up=gathered["w_up"],
             conv_w=_cols_of_shards(gathered["conv_w"]), w_down=gathered["w_down"].reshape(L, D_FF, D_MODEL))
    return W


def _local_step(x, target, W):
    S = x.shape[0]
    tq_tabs, tm_tabs = _rope_tables(S)
    onehot_t = _rel_onehot_t()
    bias_t = _bias_table(W["rel_bias"].T, onehot_t).reshape(SWA_Q_HEADS, 2 * WINDOW, WINDOW)
    row = lambda a: a.reshape(1, -1)
    col = lambda a: a.reshape(-1, 1)
    fox_rows = (FOX_ROW0, FOX_ROW0 + FOX_HEADS * HEAD_DIM, FOX_ROW0 + 2 * FOX_HEADS * HEAD_DIM)
    fox = dict(rows=fox_rows, H=FOX_HEADS, Dk=HEAD_DIM, Dv=HEAD_DIM, scale=HEAD_DIM ** -0.5)
    mla = dict(rows=(0, 0, 0), H=MLA_HEADS, Dk=MLA_PAD, Dv=HEAD_DIM, scale=MLA_QK ** -0.5)

    saved = []
    h = _rms_fwd(x, row(W["attn_pre_norm"][0]), name="rms_in")
    for l in range(DEPTH):
        sv = {"x0": x, "h1": h}
        qkv = _matmul(W["w_qkv_t"][l], h, tb=True, out_dtype=BF16, name="proj_qkv")
        lat = _matmul(W["w_lat_t"][l], h, tb=True, name="proj_lat")
        oa, lse_a = _swa_fwd(qkv, bias_t, W["swa_sinks"][l], name="swa_fwd")
        fb_col = jnp.pad(col(W["forget_bias"][l]), ((0, GATE_ROWS - FOX_HEADS), (0, 0)))
        f4 = _gate_fwd(lat, fb_col, name="fox_gate_fwd")[:FOX_HEADS]
        f_row, f_col = f4[:, None, :], f4[:, :, None]
        of, lse_f = _attn_fwd(qkv, qkv, qkv, f_row=f_row, f_col=f_col, name="fox_fwd", **fox)
        nq, nkv, qm, km, vm = _mla_prep_fwd(lat, col(W["q_latent_norm"][l]), col(W["kv_latent_norm"][l]), W["w_uq_t"][l],
                                            W["w_ukv_t"][l], tq_tabs, tm_tabs, name="mla_prep_fwd")
        oc, lse_c = _attn_fwd(qm, km, vm, name="mla_fwd", **mla)
        mixed = _group_norm_fwd(oa, of, oc, col(W["group_norm"][l]), name="group_norm_fwd")
        y = _matmul(mixed, W["w_out"][l], ta=True, name="proj_out")
        x1, h2 = _resid_rms(x, y, row(W["attn_post_norm"][l]), row(W["ffn_pre_norm"][l]), name="attn_resid")
        a = _matmul(h2, W["w_up"], b_shards=l, name="ffn_up")
        z = _conv_geglu_fwd(a, W["conv_w"][l], row(W["conv_b"][l]), name="conv_geglu_fwd")
        y2 = _matmul(z, W["w_down"][l], name="ffn_down")
        g_next = row(W["attn_pre_norm"][l + 1]) if l + 1 < DEPTH else None
        x2, h_next = _resid_rms(x1, y2, row(W["ffn_post_norm"][l]), g_next, name="ffn_resid")
        sv.update(qkv=qkv, lat=lat, oa=oa, lse_a=lse_a, fb_col=fb_col, f_row=f_row, f_col=f_col, of=of, lse_f=lse_f,
                  nq=nq, nkv=nkv, qm=qm, km=km, vm=vm, oc=oc, lse_c=lse_c, mixed=mixed, y=y, x1=x1, h2=h2, a=a, z=z, y2=y2)
        saved.append(sv)
        x, h = x2, h_next

    loss, dx = _loss_head(x, target)

    G = {k: [None] * DEPTH for k in WEIGHTS if k not in ("rel_bias", "w_up", "w_down", "w_out")}
    g_up = lax.empty((DEPTH, N_CHIPS, D_MODEL, FF_SHARD), F32)
    g_down = lax.empty((DEPTH, D_FF, D_MODEL), F32)
    g_out = lax.empty((DEPTH, D_MODEL, D_MODEL), F32)
    dbias_layers = [None] * DEPTH
    for l in reversed(range(DEPTH)):
        sv = saved[l]
        dy2, dg = _rms_bwd(sv["y2"], row(W["ffn_post_norm"][l]), dx, out_dtype=BF16, name="ffn_post_bwd")
        G["ffn_post_norm"][l] = dg[0]
        dz = _matmul(dy2, W["w_down"][l], tb=True, name="ffn_down_dx")
        g_down = _matmul(sv["z"], dy2, ta=True, out_into=(g_down, l, False), name="ffn_down_dw")
        du, dcw, dcb = _conv_geglu_bwd(sv["a"], W["conv_w"][l], row(W["conv_b"][l]), dz, name="conv_geglu_bwd")
        G["conv_w"][l] = dcw.transpose(1, 0, 2).reshape(3, 2 * D_FF)
        G["conv_b"][l] = dcb.reshape(2 * D_FF)
        da = _conv_bwd_input(du, W["conv_w"][l], name="conv_bwd_input")
        dh2 = _matmul(da, W["w_up"], tb=True, b_shards=l, name="ffn_up_dx")
        g_up = _matmul(sv["h2"], da, ta=True, out_into=(g_up, l, True), name="ffn_up_dw")
        dx1, dg = _rms_bwd(sv["x1"], row(W["ffn_pre_norm"][l]), dh2, resid=dx, out_dtype=F32, name="ffn_pre_bwd")
        G["ffn_pre_norm"][l] = dg[0]
        dy, dg = _rms_bwd(sv["y"], row(W["attn_post_norm"][l]), dx1, out_dtype=BF16, name="attn_post_bwd")
        G["attn_post_norm"][l] = dg[0]
        dmixed = _matmul(W["w_out"][l], dy, tb=True, name="proj_out_dx")
        g_out = _matmul(sv["mixed"], dy, out_into=(g_out, l, False), name="proj_out_dw")
        doa, dof, doc, dg = _group_norm_bwd(sv["oa"], sv["of"], sv["oc"], col(W["group_norm"][l]), dmixed,
                                            name="group_norm_bwd")
        G["group_norm"][l] = dg[:, 0]
        dqa, dkva, dbias_l, dsink = _swa_bwd(sv["qkv"], bias_t, W["swa_sinks"][l], sv["oa"], doa, sv["lse_a"], name="swa_bwd")
        dbias_layers[l] = dbias_l.reshape(SWA_Q_HEADS, -1)
        G["swa_sinks"][l] = dsink[:, 0]
        dqf, dkf, dvf, dfk = _attn_bwd(sv["qkv"], sv["qkv"], sv["qkv"], o=sv["of"], do=dof, lse=sv["lse_f"],
                                       f_row=sv["f_row"], f_col=sv["f_col"], name="fox_bwd", **fox)
        dF = jnp.pad(dfk[:, :, 0], ((0, GATE_ROWS - FOX_HEADS), (0, 0)))
        dflog, dfb = _gate_bwd(sv["lat"], sv["fb_col"], dF, name="fox_gate_bwd")
        G["forget_bias"][l] = dfb[:FOX_HEADS, 0]
        dqm, dkm, dvm = _attn_bwd(sv["qm"], sv["km"], sv["vm"], o=sv["oc"], do=doc, lse=sv["lse_c"], name="mla_bwd", **mla)
        dlat, dwq_t, dwkv_t, dgq, dgkv = _mla_prep_bwd(
            sv["lat"], sv["nq"], sv["nkv"], col(W["q_latent_norm"][l]), col(W["kv_latent_norm"][l]), W["w_uq_p"][l],
            W["w_ukv"][l], tq_tabs, tm_tabs, dqm, dkm, dvm, dflog, name="mla_prep_bwd")
        G["w_uq"][l], G["w_ukv"][l] = _shards_of_cols(_unpad_w_uq(dwq_t.T)), _shards_of_cols(dwkv_t.T)
        G["q_latent_norm"][l], G["kv_latent_norm"][l] = dgq[:, 0], dgkv[:, 0]
        dproj = _dproj_cast(dqa, dkva, dqf, dkf, dvf, dlat, name="dproj_cast")
        dh1 = _matmul(dproj, W["w_in_t"][l], ta=True, name="proj_in_dx")
        dw_in_t = _matmul(dproj, sv["h1"], name="proj_in_dw")
        G["w_in"][l] = _t(dw_in_t[:IN_COLS].reshape(N_CHIPS, IN_COLS // N_CHIPS, D_MODEL))
        dx, dg = _rms_bwd(sv["x0"], row(W["attn_pre_norm"][l]), dh1, resid=dx1, out_dtype=F32, name="attn_pre_bwd")
        G["attn_pre_norm"][l] = dg[0]

    grads = {k: jnp.stack(v) for k, v in G.items()}
    grads["rel_bias"] = _bias_table_bwd(jnp.stack(dbias_layers), onehot_t).T
    grads["w_up"] = g_up
    grads["w_down"] = g_down.reshape(DEPTH, N_CHIPS, D_FF // N_CHIPS, D_MODEL)
    grads["w_out"] = g_out.reshape(DEPTH, N_CHIPS, D_MODEL // N_CHIPS, D_MODEL)
    return loss, dx, grads


def kernel(x, attn_pre_norm, w_in, forget_bias, swa_sinks, rel_bias, q_latent_norm, w_uq, kv_latent_norm, w_ukv, group_norm, w_out, attn_post_norm, ffn_pre_norm, w_up, conv_w, conv_b, w_down, ffn_post_norm, loss_target, m_attn_pre_norm, m_w_in, m_forget_bias, m_swa_sinks, m_rel_bias, m_q_latent_norm, m_w_uq, m_kv_latent_norm, m_w_ukv, m_group_norm, m_w_out, m_attn_post_norm, m_ffn_pre_norm, m_w_up, m_conv_w, m_conv_b, m_w_down, m_ffn_post_norm, v_attn_pre_norm, v_w_in, v_forget_bias, v_swa_sinks, v_rel_bias, v_q_latent_norm, v_w_uq, v_kv_latent_norm, v_w_ukv, v_group_norm, v_w_out, v_attn_post_norm, v_ffn_pre_norm, v_w_up, v_conv_w, v_conv_b, v_w_down, v_ffn_post_norm):
    args = dict(locals())
    w = {k: args[k] for k in WEIGHTS}
    m = {k: args["m_" + k] for k in WEIGHTS}
    v = {k: args["v_" + k] for k in WEIGHTS}

    sent = BIG + ("conv_w",)
    gathered = dict(zip(sent, _gather_shards([w[k] if k == "conv_w" else w[k].astype(BF16) for k in sent])))
    W = _kernel_weights(gathered, {k: w[k] for k in SMALL})

    loss_part, dx, g = _local_step(x[0], loss_target[0], W)
    loss = lax.psum(loss_part, ("x", "y", "c"))

    gs = [g[k] for k in BIG]
    recv = _sibling_exchange(gs)
    pair = [_pair_sum(gk, rk, name="grad_pair_sum") for gk, rk in zip(gs, recv)]
    landed = _chip_scatter(pair)
    mine = [_chip_sum(p, name="grad_chip_sum") for p in landed]
    out_g = dict(zip(BIG, _sibling_share(mine)))
    out_d, out_m, out_v = {}, {}, {}
    for k in BIG:
        out_d[k], out_m[k], out_v[k] = _adamw(w[k], out_g[k], m[k], v[k], name="adamw_" + k)

    small_shapes = [w[k].shape for k in SMALL]
    reduced = _allreduce_small(_pack([g[k] for k in SMALL] + [g["conv_w"]], LANES, 8))
    *g_small, g_cw = _unpack(reduced, small_shapes + [g["conv_w"].shape])
    chip = 2 * lax.axis_index("x") + lax.axis_index("y")
    g_small.append(lax.dynamic_slice_in_dim(g_cw, chip * FF_SHARD, FF_SHARD, axis=2))
    names = SMALL + ("conv_w",)
    shapes = small_shapes + [w["conv_w"].shape]
    packed = lambda arrs: _pack(arrs, LANES, ROW_TILE)[None]
    d_s, m_s, v_s = _adamw(packed([w[k] for k in names]), packed(g_small), packed([m[k] for k in names]),
                           packed([v[k] for k in names]), name="adamw_small")
    out_g.update(zip(names, g_small))
    out_d.update(zip(names, _unpack(d_s, shapes)))
    out_m.update(zip(names, _unpack(m_s, shapes)))
    out_v.update(zip(names, _unpack(v_s, shapes)))

    return (loss, dx[None], *[out_g[k] for k in WEIGHTS], *[out_d[k] for k in WEIGHTS],
            *[out_m[k] for k in WEIGHTS], *[out_v[k] for k in WEIGHTS])
```

```python
import math

import numpy as np
import jax
import jax.numpy as jnp
from jax import lax
from jax.experimental import pallas as pl
from jax.experimental.pallas import tpu as pltpu

F32 = jnp.float32
BF16 = jnp.bfloat16

D_MODEL = 1024
DEPTH = 4
HEAD_DIM = 64
SWA_Q_HEADS = 8
SWA_KV_HEADS = 2
SWA_GROUP = SWA_Q_HEADS // SWA_KV_HEADS
WINDOW = 128
FOX_HEADS = 4
MLA_HEADS = 4
MLA_Q_RANK = 256
MLA_KV_RANK = 128
MLA_NOPE = 64
MLA_ROPE = 32
MLA_QK = MLA_NOPE + MLA_ROPE
ROPE_THETA = 10000.0
REL_BUCKETS = 32
REL_MAX_DIST = 128
D_FF = 2816
EPS = 1e-6
NEG_INF = -1e30
LANES = 128
N_CHIPS = 4
HALF_DEPTH = DEPTH // 2

IN_COLS = 1956
IN_ROWS = 2048
QKV_ROWS = 1536
LAT_ROWS = IN_ROWS - QKV_ROWS
LAT_SHIFT = FOX_HEADS
FOX_ROW0 = 768
MLA_PAD = LANES
GATE_ROWS = 8

ADAM_LR = 0.001
ADAM_B1 = 0.9
ADAM_B2 = 0.999
ADAM_EPS = 1e-08
ADAM_WD = 0.01
ADAM_STEP = 10

VMEM_LIMIT_BYTES = 48 * 1024 * 1024
ATT_TILE = 256
ROW_TILE = 256
MESH = pl.DeviceIdType.MESH

NT = (((1,), (1,)), ((), ()))
TN = (((0,), (0,)), ((), ()))
NN = (((1,), (0,)), ((), ()))


def _params(*sem):
    return pltpu.CompilerParams(dimension_semantics=sem, vmem_limit_bytes=VMEM_LIMIT_BYTES)


def _tile(dim, cap):
    for t in (2048, 1408, 1024, 512, 256, 128, 64, 32, 16, 8):
        if t <= cap and dim % t == 0:
            return t
    return dim


def _dot(a, b, dims=NN):
    return lax.dot_general(a, b, dims, preferred_element_type=F32)


def _split3(a):
    a1 = a.astype(BF16)
    r1 = a - a1.astype(F32)
    a2 = r1.astype(BF16)
    a3 = (r1 - a2.astype(F32)).astype(BF16)
    return a1, a2, a3


FF_SHARD = 2 * D_FF // N_CHIPS


def _matmul(a, b, *, ta=False, tb=False, out_dtype=F32, name, b_shards=None, out_into=None):
    if ta:
        K, M = a.shape
    else:
        M, K = a.shape
    if b_shards is not None:
        K2, N = (2 * D_FF, D_MODEL) if tb else (D_MODEL, 2 * D_FF)
    elif tb:
        N, K2 = b.shape
    else:
        K2, N = b.shape
    assert K == K2, (a.shape, b.shape)
    tm, tn, tk = _tile(M, 1408), _tile(N, 1408), _tile(K, 1408)
    nk = K // tk
    dims = (((0 if ta else 1,), (1 if tb else 0,)), ((), ()))

    def body(*refs):
        a_ref, b_ref = refs[0], refs[1]
        o_ref, acc_ref = refs[-2], refs[-1]
        k = pl.program_id(2)

        @pl.when(k == 0)
        def _():
            acc_ref[...] = jnp.zeros_like(acc_ref)

        acc_ref[...] += lax.dot_general(a_ref[...], b_ref[...], dims, preferred_element_type=F32)

        @pl.when(k == nk - 1)
        def _():
            o_ref[...] = acc_ref[...].astype(o_ref.dtype)

    a_spec = pl.BlockSpec((tk, tm), lambda i, j, k: (k, i)) if ta else pl.BlockSpec((tm, tk), lambda i, j, k: (i, k))
    if b_shards is not None:
        l = b_shards
        if tb:
            assert tk == FF_SHARD
            b_spec = pl.BlockSpec((None, None, tn, tk), lambda i, j, k: (l, k, j, 0))
        else:
            assert tn == FF_SHARD
            b_spec = pl.BlockSpec((None, None, tk, tn), lambda i, j, k: (l, j, k, 0))
    else:
        b_spec = pl.BlockSpec((tn, tk), lambda i, j, k: (j, k)) if tb else pl.BlockSpec((tk, tn), lambda i, j, k: (k, j))
    ins, in_specs, aliases = [a, b], [a_spec, b_spec], {}
    if out_into is None:
        out_spec = pl.BlockSpec((tm, tn), lambda i, j, k: (i, j))
        out_shape = jax.ShapeDtypeStruct((M, N), out_dtype)
    else:
        buf, lo, sharded = out_into
        if sharded:
            assert tn == FF_SHARD
            out_spec = pl.BlockSpec((None, None, tm, tn), lambda i, j, k: (lo, j, i, 0))
        else:
            out_spec = pl.BlockSpec((None, tm, tn), lambda i, j, k: (lo, i, j))
        out_shape = jax.ShapeDtypeStruct(buf.shape, buf.dtype)
        ins.append(buf)
        in_specs.append(pl.BlockSpec(memory_space=pl.ANY))
        aliases = {2: 0}
    return pl.pallas_call(
        body, name=name, grid=(M // tm, N // tn, nk),
        in_specs=in_specs, out_specs=out_spec, out_shape=out_shape,
        scratch_shapes=[pltpu.VMEM((tm, tn), F32)],
        input_output_aliases=aliases,
        compiler_params=_params("parallel", "parallel", "arbitrary"),
    )(*ins)


def _seg_rms(xs, g):
    r = lax.rsqrt(jnp.mean(xs * xs, axis=-1, keepdims=True) + EPS)
    return xs * r * g


def _seg_rms_bwd(xs, g, dy):
    r = lax.rsqrt(jnp.mean(xs * xs, axis=-1, keepdims=True) + EPS)
    gd = dy * g
    c = jnp.mean(gd * xs, axis=-1, keepdims=True)
    dx = r * gd - xs * (r * r * r * c)
    dg = jnp.sum(dy * (xs * r), axis=0, keepdims=True)
    return dx, dg


def _rms_fwd(x, g, *, name):
    S, W = x.shape
    tm = _tile(S, 512)

    def body(x_ref, g_ref, o_ref):
        o_ref[...] = _seg_rms(x_ref[...], g_ref[...]).astype(o_ref.dtype)

    return pl.pallas_call(
        body, name=name, grid=(S // tm,),
        in_specs=[pl.BlockSpec((tm, W), lambda i: (i, 0)), pl.BlockSpec((1, W), lambda i: (0, 0))],
        out_specs=pl.BlockSpec((tm, W), lambda i: (i, 0)),
        out_shape=jax.ShapeDtypeStruct((S, W), BF16),
        compiler_params=_params("parallel"),
    )(x, g)


def _rms_bwd(x, g, dy, *, resid=None, out_dtype, name):
    S, W = x.shape
    tm = _tile(S, 512)
    has_resid = resid is not None

    def body(*refs):
        if has_resid:
            x_ref, g_ref, dy_ref, r_ref, dx_ref, dg_ref = refs
        else:
            x_ref, g_ref, dy_ref, dx_ref, dg_ref = refs

        @pl.when(pl.program_id(0) == 0)
        def _():
            dg_ref[...] = jnp.zeros_like(dg_ref)

        dx, dg = _seg_rms_bwd(x_ref[...], g_ref[...], dy_ref[...])
        if has_resid:
            dx = dx + r_ref[...]
        dx_ref[...] = dx.astype(dx_ref.dtype)
        dg_ref[...] += dg

    row = pl.BlockSpec((tm, W), lambda i: (i, 0))
    vec = pl.BlockSpec((1, W), lambda i: (0, 0))
    ins = [x, g, dy] + ([resid] if has_resid else [])
    return pl.pallas_call(
        body, name=name, grid=(S // tm,),
        in_specs=[row, vec, row] + ([row] if has_resid else []),
        out_specs=[row, vec],
        out_shape=[jax.ShapeDtypeStruct((S, W), out_dtype), jax.ShapeDtypeStruct((1, W), F32)],
        compiler_params=_params("arbitrary"),
    )(*ins)


def _resid_rms(x, y, g_post, g_next, *, name):
    S, W = x.shape
    tm = _tile(S, 512)
    with_next = g_next is not None

    def body(*refs):
        if with_next:
            x_ref, y_ref, gp_ref, gn_ref, xo_ref, h_ref = refs
        else:
            x_ref, y_ref, gp_ref, xo_ref = refs
        xn = x_ref[...] + _seg_rms(y_ref[...], gp_ref[...])
        xo_ref[...] = xn
        if with_next:
            h_ref[...] = _seg_rms(xn, gn_ref[...]).astype(BF16)

    row = pl.BlockSpec((tm, W), lambda i: (i, 0))
    vec = pl.BlockSpec((1, W), lambda i: (0, 0))
    outs = [jax.ShapeDtypeStruct((S, W), F32)] + ([jax.ShapeDtypeStruct((S, W), BF16)] if with_next else [])
    res = pl.pallas_call(
        body, name=name, grid=(S // tm,),
        in_specs=[row, row, vec] + ([vec] if with_next else []),
        out_specs=[row] + ([row] if with_next else []),
        out_shape=outs,
        compiler_params=_params("parallel"),
    )(*([x, y, g_post] + ([g_next] if with_next else [])))
    return (res[0], res[1]) if with_next else (res[0], None)


def _col_rms(xs, g):
    r = lax.rsqrt(jnp.mean(xs * xs, axis=0, keepdims=True) + EPS)
    return xs * r * g


def _col_rms_bwd(xs, g, dy):
    r = lax.rsqrt(jnp.mean(xs * xs, axis=0, keepdims=True) + EPS)
    gd = dy * g
    c = jnp.mean(gd * xs, axis=0, keepdims=True)
    dx = r * gd - xs * (r * r * r * c)
    dg = jnp.sum(dy * (xs * r), axis=1, keepdims=True)
    return dx, dg


GROUP_ROWS = (SWA_Q_HEADS * HEAD_DIM, FOX_HEADS * HEAD_DIM, MLA_HEADS * HEAD_DIM)


def _group_specs(S, tn):
    outs = [pl.BlockSpec((n, tn), lambda i: (0, i)) for n in GROUP_ROWS]
    g = pl.BlockSpec((D_MODEL, 1), lambda i: (0, 0))
    mixed = pl.BlockSpec((D_MODEL, tn), lambda i: (0, i))
    return outs, g, mixed


def _group_norm_fwd(oa, of, oc, g, *, name):
    S = oa.shape[1]
    tn = _tile(S, 512)
    outs, gs, mixed = _group_specs(S, tn)

    def body(a_ref, f_ref, c_ref, g_ref, o_ref):
        r0 = 0
        for ref, n in zip((a_ref, f_ref, c_ref), GROUP_ROWS):
            o_ref[r0:r0 + n, :] = _col_rms(ref[...], g_ref[r0:r0 + n, :]).astype(BF16)
            r0 += n

    return pl.pallas_call(
        body, name=name, grid=(S // tn,),
        in_specs=outs + [gs], out_specs=mixed,
        out_shape=jax.ShapeDtypeStruct((D_MODEL, S), BF16),
        compiler_params=_params("parallel"),
    )(oa, of, oc, g)


def _group_norm_bwd(oa, of, oc, g, dmixed, *, name):
    S = oa.shape[1]
    tn = _tile(S, 512)
    outs, gs, mixed = _group_specs(S, tn)
    n_heads = D_MODEL // HEAD_DIM

    def body(a_ref, f_ref, c_ref, g_ref, dm_ref, da_ref, df_ref, dc_ref, dg_ref, dl_ref):
        @pl.when(pl.program_id(0) == 0)
        def _():
            dg_ref[...] = jnp.zeros_like(dg_ref)

        r0 = 0
        for ref, dref, n in zip((a_ref, f_ref, c_ref), (da_ref, df_ref, dc_ref), GROUP_ROWS):
            o = ref[...]
            dx, dg = _col_rms_bwd(o, g_ref[r0:r0 + n, :], dm_ref[r0:r0 + n, :])
            dxb = dx.astype(BF16)
            dref[...] = dxb
            dg_ref[r0:r0 + n, :] += dg
            od = o * dxb.astype(F32)
            for h in range(n // HEAD_DIM):
                dl_ref[r0 // HEAD_DIM + h] = jnp.sum(od[h * HEAD_DIM:(h + 1) * HEAD_DIM, :], axis=0, keepdims=True)
            r0 += n

    return pl.pallas_call(
        body, name=name, grid=(S // tn,),
        in_specs=outs + [gs, mixed], out_specs=outs + [gs, pl.BlockSpec((n_heads, 1, tn), lambda i: (0, 0, i))],
        out_shape=[jax.ShapeDtypeStruct((n, S), BF16) for n in GROUP_ROWS] + [jax.ShapeDtypeStruct((D_MODEL, 1), F32),
                                                                              jax.ShapeDtypeStruct((n_heads, 1, S), F32)],
        compiler_params=_params("arbitrary"),
    )(oa, of, oc, g, dmixed)


def _loss_head(y, target):
    S, W = y.shape
    tm = _tile(S, 512)

    def body(y_ref, t_ref, d_ref, l_ref):
        @pl.when(pl.program_id(0) == 0)
        def _():
            l_ref[...] = jnp.zeros_like(l_ref)

        err = y_ref[...] - t_ref[...]
        d_ref[...] = err * (1.0 / W)
        l_ref[...] += 0.5 * jnp.sum(jnp.mean(err * err, axis=-1, keepdims=True), axis=0, keepdims=True)

    row = pl.BlockSpec((tm, W), lambda i: (i, 0))
    d, l = pl.pallas_call(
        body, name="loss_head", grid=(S // tm,),
        in_specs=[row, row],
        out_specs=[row, pl.BlockSpec((1, 1), lambda i: (0, 0))],
        out_shape=[jax.ShapeDtypeStruct((S, W), F32), jax.ShapeDtypeStruct((1, 1), F32)],
        compiler_params=_params("arbitrary"),
    )(y, target)
    return l[0, 0], d


def _attn_fwd(q_src, k_src, v_src, rows, H, Dk, Dv, scale, f_row=None, f_col=None, *, name):
    S = q_src.shape[1]
    T = _tile(S, ATT_TILE)
    nq = S // T
    forget = f_row is not None
    qb, kb, vb = rows[0] // (H * Dk), rows[1] // (H * Dk), rows[2] // (H * Dv)
    hs = range(H)

    def body(*refs):
        if forget:
            q_ref, k_ref, v_ref, fq_ref, fk_ref, o_ref, lse_ref = refs
        else:
            q_ref, k_ref, v_ref, o_ref, lse_ref = refs
        i = pl.program_id(0)

        def tile(j, masked, state):
            off = pl.multiple_of(j * T, T)
            ss = [_dot(k_ref[h * Dk:(h + 1) * Dk, pl.ds(off, T)], q_ref[h * Dk:(h + 1) * Dk, :], TN) * scale for h in hs]
            if forget:
                ss = [ss[h] + (fq_ref[h] - fk_ref[pl.ds(off, T), h:h + 1]) for h in hs]
            if masked:
                r = lax.broadcasted_iota(jnp.int32, (T, T), 0)
                c = lax.broadcasted_iota(jnp.int32, (T, T), 1)
                ss = [jnp.where(r <= c, s, NEG_INF) for s in ss]
            m_new = [jnp.maximum(state[h][0], jnp.max(ss[h], axis=0, keepdims=True)) for h in hs]
            alpha = [jnp.exp(state[h][0] - m_new[h]) for h in hs]
            ps = [jnp.exp(ss[h] - m_new[h]) for h in hs]
            l_new = [alpha[h] * state[h][1] + jnp.sum(ps[h], axis=0, keepdims=True) for h in hs]
            p_hi = [p.astype(BF16) for p in ps]
            vs = [v_ref[h * Dv:(h + 1) * Dv, pl.ds(off, T)] for h in hs]
            pv = [_dot(vs[h], p_hi[h]) for h in hs]
            if forget:
                pv = [pv[h] + _dot(vs[h], (ps[h] - p_hi[h].astype(F32)).astype(BF16)) for h in hs]
            return tuple((m_new[h], l_new[h], alpha[h] * state[h][2] + pv[h]) for h in hs)

        init = tuple((jnp.full((1, T), NEG_INF, F32), jnp.zeros((1, T), F32), jnp.zeros((Dv, T), F32)) for _ in hs)
        state = lax.fori_loop(0, i, lambda j, st: tile(j, False, st), init)
        state = tile(i, True, state)
        for h in hs:
            m, l, acc = state[h]
            o_ref[h * Dv:(h + 1) * Dv, :] = acc / l
            lse_ref[h] = m + jnp.log(l)

    in_specs = [pl.BlockSpec((H * Dk, T), lambda i: (qb, i)),
                pl.BlockSpec((H * Dk, S), lambda i: (kb, 0)),
                pl.BlockSpec((H * Dv, S), lambda i: (vb, 0))]
    ins = [q_src, k_src, v_src]
    if forget:
        in_specs += [pl.BlockSpec((H, 1, T), lambda i: (0, 0, i)), pl.BlockSpec((S, H), lambda i: (0, 0))]
        ins += [f_row, f_col]
    return pl.pallas_call(
        body, name=name, grid=(nq,),
        in_specs=in_specs,
        out_specs=[pl.BlockSpec((H * Dv, T), lambda i: (0, i)), pl.BlockSpec((H, 1, T), lambda i: (0, 0, i))],
        out_shape=[jax.ShapeDtypeStruct((H * Dv, S), F32), jax.ShapeDtypeStruct((H, 1, S), F32)],
        compiler_params=_params("parallel"),
    )(*ins)


def _attn_bwd(q_src, k_src, v_src, rows, H, Dk, Dv, scale, do, lse, delta, f_row=None, f_col=None, *, name):
    S = q_src.shape[1]
    T = _tile(S, ATT_TILE)
    nq = S // T
    forget = f_row is not None
    qb, kb, vb, db = rows[0] // (H * Dk), rows[1] // (H * Dk), rows[2] // (H * Dv), rows[3] // H
    hs = range(H)

    def body(*refs):
        if forget:
            (q_ref, k_ref, v_ref, do_ref, lse_ref, dl_ref, fq_ref, fk_ref,
             dq_ref, dk_ref, dv_ref, df_ref, dk_s, dv_s, df_s) = refs
        else:
            q_ref, k_ref, v_ref, do_ref, lse_ref, dl_ref, dq_ref, dk_ref, dv_ref, dk_s, dv_s = refs
        j = pl.program_id(0)

        @pl.when(j == 0)
        def _():
            dq_ref[...] = jnp.zeros_like(dq_ref)

        dk_s[...] = jnp.zeros_like(dk_s)
        dv_s[...] = jnp.zeros_like(dv_s)
        if forget:
            df_s[...] = jnp.zeros_like(df_s)
        kt = [k_ref[h * Dk:(h + 1) * Dk, :] for h in hs]
        kj = [k.T for k in kt]
        vj = [v_ref[h * Dv:(h + 1) * Dv, :].T for h in hs]
        koff = pl.multiple_of(j * T, T)

        def tile(i, masked):
            cols = pl.ds(pl.multiple_of(i * T, T), T)
            qi = [q_ref[h * Dk:(h + 1) * Dk, cols] for h in hs]
            doi = [do_ref[h * Dv:(h + 1) * Dv, cols] for h in hs]
            st = [_dot(kj[h], qi[h]) * scale for h in hs]
            if forget:
                st = [st[h] + (fq_ref[h, :, cols] - fk_ref[pl.ds(koff, T), h:h + 1]) for h in hs]
            if masked:
                r = lax.broadcasted_iota(jnp.int32, (T, T), 0)
                c = lax.broadcasted_iota(jnp.int32, (T, T), 1)
                st = [jnp.where(r <= c, x, NEG_INF) for x in st]
            pt = [jnp.exp(st[h] - lse_ref[h, :, cols]) for h in hs]
            dpt = [_dot(vj[h], doi[h]) for h in hs]
            dst = [pt[h] * (dpt[h] - dl_ref[h, :, cols]) for h in hs]
            ptb = [p.astype(BF16) for p in pt]
            dsb = [d.astype(BF16) for d in dst]
            for h in hs:
                dv_s[h * Dv:(h + 1) * Dv, :] += _dot(doi[h], ptb[h], NT)
            for h in hs:
                dk_s[h * Dk:(h + 1) * Dk, :] += _dot(qi[h], dsb[h], NT)
            for h in hs:
                dq_ref[h * Dk:(h + 1) * Dk, cols] += _dot(kt[h], dsb[h]) * scale
            if forget:
                for h in hs:
                    part = dst[h][:, 0:LANES]
                    for c0 in range(LANES, T, LANES):
                        part = part + dst[h][:, c0:c0 + LANES]
                    df_s[h] += part

        tile(j, True)

        def loop_body(i, carry):
            tile(i, False)
            return carry

        lax.fori_loop(j + 1, nq, loop_body, 0)
        dk_ref[...] = dk_s[...] * scale
        dv_ref[...] = dv_s[...]
        if forget:
            df_ref[...] = jnp.concatenate([-jnp.sum(df_s[h], axis=-1, keepdims=True) for h in hs], axis=1)

    res = lambda D, b0: pl.BlockSpec((H * D, S), lambda j: (b0, 0))
    blk = lambda D, b0: pl.BlockSpec((H * D, T), lambda j: (b0, j))
    row3 = lambda b0: pl.BlockSpec((H, 1, S), lambda j: (b0, 0, 0))
    in_specs = [res(Dk, qb), blk(Dk, kb), blk(Dv, vb), res(Dv, 0), row3(0), row3(db)]
    ins = [q_src, k_src, v_src, do, lse, delta]
    out_specs = [res(Dk, 0), blk(Dk, 0), blk(Dv, 0)]
    out_shape = [jax.ShapeDtypeStruct((H * Dk, S), F32), jax.ShapeDtypeStruct((H * Dk, S), F32),
                 jax.ShapeDtypeStruct((H * Dv, S), F32)]
    scratch = [pltpu.VMEM((H * Dk, T), F32), pltpu.VMEM((H * Dv, T), F32)]
    if forget:
        in_specs += [row3(0), pl.BlockSpec((S, H), lambda j: (0, 0))]
        ins += [f_row, f_col]
        out_specs.append(pl.BlockSpec((T, H), lambda j: (j, 0)))
        out_shape.append(jax.ShapeDtypeStruct((S, H), F32))
        scratch.append(pltpu.VMEM((H, T, min(T, LANES)), F32))
    return pl.pallas_call(
        body, name=name, grid=(nq,),
        in_specs=in_specs, out_specs=out_specs, out_shape=out_shape, scratch_shapes=scratch,
        compiler_params=_params("arbitrary"),
    )(*ins)


def _swa_masks(i):
    r = lax.broadcasted_iota(jnp.int32, (WINDOW, WINDOW), 0)
    c = lax.broadcasted_iota(jnp.int32, (WINDOW, WINDOW), 1)
    return (r > c) & (i > 0), r <= c


def _swa_specs():
    W = WINDOW
    kv_rows = SWA_KV_HEADS * HEAD_DIM
    q = pl.BlockSpec((SWA_Q_HEADS * HEAD_DIM, W), lambda i: (0, i))
    prev = lambda b: pl.BlockSpec((kv_rows, W), lambda i: (b, jnp.maximum(i - 1, 0)))
    cur = lambda b: pl.BlockSpec((kv_rows, W), lambda i: (b, i))
    bias = pl.BlockSpec((SWA_Q_HEADS, 2 * W, W), lambda i: (0, 0, 0))
    stat = pl.BlockSpec((SWA_Q_HEADS, W), lambda i: (0, i))
    sink = pl.BlockSpec(memory_space=pltpu.SMEM)
    return q, prev(4), cur(4), prev(5), cur(5), bias, stat, sink


def _swa_scores(h, q_ref, kp_ref, kc_ref, b_ref, masks):
    g = h // SWA_GROUP
    rows = slice(g * HEAD_DIM, (g + 1) * HEAD_DIM)
    qh = q_ref[h * HEAD_DIM:(h + 1) * HEAD_DIM, :]
    scale = HEAD_DIM ** -0.5
    s_p = jnp.where(masks[0], _dot(kp_ref[rows, :], qh, TN) * scale + b_ref[h, 0:WINDOW, :], NEG_INF)
    s_c = jnp.where(masks[1], _dot(kc_ref[rows, :], qh, TN) * scale + b_ref[h, WINDOW:2 * WINDOW, :], NEG_INF)
    return qh, rows, s_p, s_c


def _swa_fwd(qkv, bias_t, sinks, *, name):
    S = qkv.shape[1]
    qs, kp, kc, vp, vc, bs, stat, sk = _swa_specs()

    def body(sink_ref, q_ref, kp_ref, kc_ref, vp_ref, vc_ref, b_ref, o_ref, lse_ref):
        masks = _swa_masks(pl.program_id(0))
        for h in range(SWA_Q_HEADS):
            qh, rows, s_p, s_c = _swa_scores(h, q_ref, kp_ref, kc_ref, b_ref, masks)
            sink = sink_ref[h]
            m = jnp.maximum(jnp.maximum(jnp.max(s_p, axis=0, keepdims=True), jnp.max(s_c, axis=0, keepdims=True)), sink)
            p_p = jnp.exp(s_p - m)
            p_c = jnp.exp(s_c - m)
            l = jnp.sum(p_p, axis=0, keepdims=True) + jnp.sum(p_c, axis=0, keepdims=True) + jnp.exp(sink - m)
            o = _dot(vp_ref[rows, :], p_p.astype(BF16)) + _dot(vc_ref[rows, :], p_c.astype(BF16))
            o_ref[h * HEAD_DIM:(h + 1) * HEAD_DIM, :] = o / l
            lse_ref[h:h + 1, :] = m + jnp.log(l)

    return pl.pallas_call(
        body, name=name, grid=(S // WINDOW,),
        in_specs=[sk, qs, kp, kc, vp, vc, bs],
        out_specs=[qs, stat],
        out_shape=[jax.ShapeDtypeStruct((SWA_Q_HEADS * HEAD_DIM, S), F32), jax.ShapeDtypeStruct((SWA_Q_HEADS, S), F32)],
        compiler_params=_params("parallel"),
    )(sinks, qkv, qkv, qkv, qkv, qkv, bias_t)


def _swa_bwd(qkv, bias_t, sinks, do, lse, delta, *, name):
    S = qkv.shape[1]
    W = WINDOW
    qs, kp, kc, vp, vc, bs, stat, sk = _swa_specs()
    scale = HEAD_DIM ** -0.5
    kv_rows = SWA_KV_HEADS * HEAD_DIM

    def body(sink_ref, q_ref, kp_ref, kc_ref, vp_ref, vc_ref, b_ref, do_ref, lse_ref, dl_ref,
             dq_ref, dkv_ref, db_ref, dsk_ref):
        i = pl.program_id(0)

        @pl.when(i == 0)
        def _():
            dkv_ref[...] = jnp.zeros_like(dkv_ref)
            db_ref[...] = jnp.zeros_like(db_ref)
            dsk_ref[...] = jnp.zeros_like(dsk_ref)

        masks = _swa_masks(i)
        prev = pl.ds(pl.multiple_of(jnp.maximum(i - 1, 0) * W, W), W)
        cur = pl.ds(pl.multiple_of(i * W, W), W)
        for h in range(SWA_Q_HEADS):
            qh, rows, s_p, s_c = _swa_scores(h, q_ref, kp_ref, kc_ref, b_ref, masks)
            vrows = slice(kv_rows + rows.start, kv_rows + rows.stop)
            hrows = slice(h * HEAD_DIM, (h + 1) * HEAD_DIM)
            doh = do_ref[hrows, :]
            lse_h = lse_ref[h:h + 1, :]
            delta = dl_ref[h:h + 1, :]
            p_p = jnp.exp(s_p - lse_h)
            p_c = jnp.exp(s_c - lse_h)
            ds_p = p_p * (_dot(vp_ref[rows, :], doh, TN) - delta)
            ds_c = p_c * (_dot(vc_ref[rows, :], doh, TN) - delta)
            db_ref[h, 0:W, :] += ds_p
            db_ref[h, W:2 * W, :] += ds_c
            dsk = -jnp.sum(jnp.exp(sink_ref[h] - lse_h) * delta, axis=1, keepdims=True)
            dsk_ref[h:h + 1, :] += jnp.broadcast_to(dsk, (1, LANES))
            dsb_p = ds_p.astype(BF16)
            dsb_c = ds_c.astype(BF16)
            dq_ref[hrows, :] = (_dot(kp_ref[rows, :], dsb_p) + _dot(kc_ref[rows, :], dsb_c)) * scale
            dkv_ref[rows, prev] += _dot(qh, dsb_p, NT) * scale
            dkv_ref[rows, cur] += _dot(qh, dsb_c, NT) * scale
            dkv_ref[vrows, prev] += _dot(doh, p_p.astype(BF16), NT)
            dkv_ref[vrows, cur] += _dot(doh, p_c.astype(BF16), NT)

    return pl.pallas_call(
        body, name=name, grid=(S // W,),
        in_specs=[sk, qs, kp, kc, vp, vc, bs, qs, stat, stat],
        out_specs=[qs, pl.BlockSpec((2 * kv_rows, S), lambda i: (0, 0)), bs, pl.BlockSpec((SWA_Q_HEADS, LANES), lambda i: (0, 0))],
        out_shape=[jax.ShapeDtypeStruct((SWA_Q_HEADS * HEAD_DIM, S), F32), jax.ShapeDtypeStruct((2 * kv_rows, S), F32),
                   jax.ShapeDtypeStruct((SWA_Q_HEADS, 2 * W, W), F32), jax.ShapeDtypeStruct((SWA_Q_HEADS, LANES), F32)],
        compiler_params=_params("arbitrary"),
    )(sinks, qkv, qkv, qkv, qkv, qkv, bias_t, do, lse, delta)


def _rel_onehot_t():
    qi = jnp.arange(WINDOW, dtype=jnp.int32)[None, :] + WINDOW
    kj = jnp.arange(2 * WINDOW, dtype=jnp.int32)[:, None]
    dist = qi - kj
    max_exact = REL_BUCKETS // 2
    d = jnp.maximum(dist, 0)
    log_ratio = jnp.log(jnp.maximum(d, 1).astype(F32) / max_exact) / math.log(REL_MAX_DIST / max_exact)
    large = jnp.minimum(max_exact + (log_ratio * (REL_BUCKETS - max_exact)).astype(jnp.int32), REL_BUCKETS - 1)
    bucket = jnp.where(d < max_exact, d, large).reshape(-1)
    return (bucket[None, :] == jnp.arange(REL_BUCKETS, dtype=jnp.int32)[:, None]).astype(BF16)


def _bias_table(rel_bias_t, onehot_t):
    Hq, NB = rel_bias_t.shape
    N = onehot_t.shape[1]
    tn = _tile(N, 4096)

    def body(r_ref, oh_ref, o_ref):
        oh = oh_ref[...]
        a1, a2, a3 = _split3(r_ref[...])
        o_ref[...] = _dot(a1, oh) + _dot(a2, oh) + _dot(a3, oh)

    return pl.pallas_call(
        body, name="rel_bias_table", grid=(N // tn,),
        in_specs=[pl.BlockSpec((Hq, NB), lambda j: (0, 0)), pl.BlockSpec((NB, tn), lambda j: (0, j))],
        out_specs=pl.BlockSpec((Hq, tn), lambda j: (0, j)),
        out_shape=jax.ShapeDtypeStruct((Hq, N), F32),
        compiler_params=_params("parallel"),
    )(rel_bias_t, onehot_t)


def _bias_table_bwd(dbias, onehot_t):
    L, Hq, N = dbias.shape
    NB = onehot_t.shape[0]
    tn = _tile(N, 4096)

    def body(d_ref, oh_ref, o_ref):
        @pl.when(pl.program_id(0) == 0)
        def _():
            o_ref[...] = jnp.zeros_like(o_ref)

        d = d_ref[0]
        for l in range(1, L):
            d = d + d_ref[l]
        oh = oh_ref[...]
        a1, a2, a3 = _split3(d)
        o_ref[...] += _dot(a1, oh, NT) + _dot(a2, oh, NT) + _dot(a3, oh, NT)

    return pl.pallas_call(
        body, name="rel_bias_bwd", grid=(N // tn,),
        in_specs=[pl.BlockSpec((L, Hq, tn), lambda j: (0, 0, j)), pl.BlockSpec((NB, tn), lambda j: (0, j))],
        out_specs=pl.BlockSpec((Hq, NB), lambda j: (0, 0)),
        out_shape=jax.ShapeDtypeStruct((Hq, NB), F32),
        compiler_params=_params("arbitrary"),
    )(dbias, onehot_t)


def _gate_fwd(lat, fb_col, *, name):
    S = lat.shape[1]
    tn = _tile(S, 256)

    def body(z_ref, fb_ref, o_ref, carry):
        @pl.when(pl.program_id(0) == 0)
        def _():
            carry[...] = jnp.zeros_like(carry)

        z = z_ref[...] + fb_ref[...]
        lf = jnp.minimum(z, 0.0) - jnp.log1p(jnp.exp(-jnp.abs(z)))
        r = lax.broadcasted_iota(jnp.int32, (tn, tn), 0)
        c = lax.broadcasted_iota(jnp.int32, (tn, tn), 1)
        tri = (r <= c).astype(BF16)
        a1, a2, a3 = _split3(lf)
        cum = _dot(a1, tri) + _dot(a2, tri) + _dot(a3, tri) + carry[:, 0:1]
        o_ref[...] = cum
        carry[...] = jnp.broadcast_to(cum[:, tn - 1:tn], carry.shape)

    return pl.pallas_call(
        body, name=name, grid=(S // tn,),
        in_specs=[pl.BlockSpec((GATE_ROWS, tn), lambda i: (0, i)), pl.BlockSpec((GATE_ROWS, 1), lambda i: (0, 0))],
        out_specs=pl.BlockSpec((GATE_ROWS, tn), lambda i: (0, i)),
        out_shape=jax.ShapeDtypeStruct((GATE_ROWS, S), F32),
        scratch_shapes=[pltpu.VMEM((GATE_ROWS, LANES), F32)],
        compiler_params=_params("arbitrary"),
    )(lat, fb_col)


def _gate_bwd(lat, fb_col, dF, *, name):
    S = lat.shape[1]
    tn = _tile(S, 256)
    nt = S // tn

    def body(z_ref, fb_ref, df_ref, dz_ref, dfb_ref, carry):
        @pl.when(pl.program_id(0) == 0)
        def _():
            carry[...] = jnp.zeros_like(carry)
            dfb_ref[...] = jnp.zeros_like(dfb_ref)

        r = lax.broadcasted_iota(jnp.int32, (tn, tn), 0)
        c = lax.broadcasted_iota(jnp.int32, (tn, tn), 1)
        tri = (r >= c).astype(BF16)
        a1, a2, a3 = _split3(df_ref[...])
        dlf = _dot(a1, tri) + _dot(a2, tri) + _dot(a3, tri) + carry[:, 0:1]
        carry[...] = jnp.broadcast_to(dlf[:, 0:1], carry.shape)
        z = z_ref[...] + fb_ref[...]
        row = lax.broadcasted_iota(jnp.int32, (GATE_ROWS, tn), 0)
        dz = jnp.where(row < FOX_HEADS, dlf / (1.0 + jnp.exp(z)), 0.0)
        dz_ref[...] = dz
        dfb_ref[...] += jnp.sum(dz, axis=1, keepdims=True)

    blk = pl.BlockSpec((GATE_ROWS, tn), lambda i: (0, nt - 1 - i))
    vec = pl.BlockSpec((GATE_ROWS, 1), lambda i: (0, 0))
    return pl.pallas_call(
        body, name=name, grid=(nt,),
        in_specs=[blk, vec, blk], out_specs=[blk, vec],
        out_shape=[jax.ShapeDtypeStruct((GATE_ROWS, S), F32), jax.ShapeDtypeStruct((GATE_ROWS, 1), F32)],
        scratch_shapes=[pltpu.VMEM((GATE_ROWS, LANES), F32)],
        compiler_params=_params("arbitrary"),
    )(lat, fb_col, dF)


def _rope_tables(S):
    pos = jnp.arange(S, dtype=F32)
    inv_freq = ROPE_THETA ** (-(jnp.arange(MLA_ROPE // 2, dtype=F32) * 2.0 / MLA_ROPE))
    ang = pos[:, None] * inv_freq[None, :]
    cos, sin = jnp.cos(ang).T, jnp.sin(ang).T
    z16 = jnp.zeros_like(cos)

    def slab(lo, fill):
        def put(first, second, f):
            return jnp.concatenate([jnp.full((lo, S), f, F32), first, second, jnp.full((LANES - lo - MLA_ROPE, S), f, F32)], axis=0)
        return put(cos, cos, fill), put(-sin, z16, 0.0), put(z16, sin, 0.0)

    tq = tuple(jnp.tile(t, (MLA_HEADS, 1)) for t in slab(MLA_NOPE, 1.0))
    return tq, slab(0, 0.0)


def _rope(x, c, s1, s2):
    n = x.shape[0]
    half = MLA_ROPE // 2
    return x * c + pltpu.roll(x, n - half, 0) * s1 + pltpu.roll(x, half, 0) * s2


def _rope_t(dy, c, s1, s2):
    n = dy.shape[0]
    half = MLA_ROPE // 2
    return dy * c + pltpu.roll(dy * s1, half, 0) + pltpu.roll(dy * s2, n - half, 0)


KR_SLAB0 = MLA_Q_RANK + MLA_KV_RANK


def _mla_prep_fwd(lat, g_q, g_kv, w_uq_t, w_ukv_t, tq, tmisc, *, name):
    S = lat.shape[1]
    tn = _tile(S, 512)
    QW = MLA_HEADS * MLA_PAD

    def body(lat_ref, gq_ref, gkv_ref, wq_ref, wkv_ref, c_ref, s1_ref, s2_ref, cm_ref, s1m_ref, s2m_ref,
             nq_ref, nkv_ref, q_ref, k_ref, v_ref):
        x = pltpu.roll(lat_ref[...], LAT_ROWS - LAT_SHIFT, 0)
        nq = _col_rms(x[0:MLA_Q_RANK, :], gq_ref[...]).astype(BF16)
        nkv = _col_rms(x[MLA_Q_RANK:KR_SLAB0, :], gkv_ref[...]).astype(BF16)
        nq_ref[...] = nq
        nkv_ref[...] = nkv
        q_ref[...] = _rope(_dot(wq_ref[...], nq), c_ref[...], s1_ref[...], s2_ref[...]).astype(BF16)
        kv = _dot(wkv_ref[...], nkv).astype(BF16)
        kr = _rope(x[KR_SLAB0:LAT_ROWS, :], cm_ref[...], s1m_ref[...], s2m_ref[...]).astype(BF16)
        for h in range(MLA_HEADS):
            k_ref[h * MLA_PAD:h * MLA_PAD + MLA_NOPE, :] = kv[h * LANES:h * LANES + MLA_NOPE, :]
            k_ref[h * MLA_PAD + MLA_NOPE:(h + 1) * MLA_PAD, :] = kr[0:MLA_PAD - MLA_NOPE, :]
            v_ref[h * HEAD_DIM:(h + 1) * HEAD_DIM, :] = kv[h * LANES + MLA_NOPE:(h + 1) * LANES, :]

    def col(rows):
        return pl.BlockSpec((rows, tn), lambda i: (0, i))

    def full(a):
        return pl.BlockSpec(a.shape, lambda i: (0, 0))

    return pl.pallas_call(
        body, name=name, grid=(S // tn,),
        in_specs=[col(LAT_ROWS), full(g_q), full(g_kv), full(w_uq_t), full(w_ukv_t),
                  col(QW), col(QW), col(QW), col(LANES), col(LANES), col(LANES)],
        out_specs=[col(MLA_Q_RANK), col(MLA_KV_RANK), col(QW), col(QW), col(MLA_HEADS * HEAD_DIM)],
        out_shape=[jax.ShapeDtypeStruct((MLA_Q_RANK, S), BF16), jax.ShapeDtypeStruct((MLA_KV_RANK, S), BF16),
                   jax.ShapeDtypeStruct((QW, S), BF16), jax.ShapeDtypeStruct((QW, S), BF16),
                   jax.ShapeDtypeStruct((MLA_HEADS * HEAD_DIM, S), BF16)],
        compiler_params=_params("parallel"),
    )(lat, g_q, g_kv, w_uq_t, w_ukv_t, *tq, *tmisc)


def _mla_prep_bwd(lat, nq, nkv, g_q, g_kv, w_uq_p, w_ukv, tq, tmisc, dq, dk, dv, dflog, *, name):
    S = lat.shape[1]
    tn = _tile(S, 512)
    QW = MLA_HEADS * MLA_PAD

    def body(lat_ref, nq_ref, nkv_ref, gq_ref, gkv_ref, wq_ref, wkv_ref, c_ref, s1_ref, s2_ref,
             cm_ref, s1m_ref, s2m_ref, dq_ref, dk_ref, dv_ref, dfl_ref,
             dlat_ref, dwq_ref, dwkv_ref, dgq_ref, dgkv_ref, y_s):
        @pl.when(pl.program_id(0) == 0)
        def _():
            dwq_ref[...] = jnp.zeros_like(dwq_ref)
            dwkv_ref[...] = jnp.zeros_like(dwkv_ref)
            dgq_ref[...] = jnp.zeros_like(dgq_ref)
            dgkv_ref[...] = jnp.zeros_like(dgkv_ref)

        x = pltpu.roll(lat_ref[...], LAT_ROWS - LAT_SHIFT, 0)
        dqm = _rope_t(dq_ref[...], c_ref[...], s1_ref[...], s2_ref[...]).astype(BF16)
        dwq_ref[...] += _dot(dqm, nq_ref[...], NT)
        dx, dg = _col_rms_bwd(x[0:MLA_Q_RANK, :], gq_ref[...], _dot(wq_ref[...], dqm))
        y_s[0:MLA_Q_RANK, :] = dx
        dgq_ref[...] += dg
        dkv = jnp.concatenate(
            [part for h in range(MLA_HEADS)
             for part in (dk_ref[h * MLA_PAD:h * MLA_PAD + MLA_NOPE, :], dv_ref[h * HEAD_DIM:(h + 1) * HEAD_DIM, :])],
            axis=0).astype(BF16)
        dwkv_ref[...] += _dot(dkv, nkv_ref[...], NT)
        dx, dg = _col_rms_bwd(x[MLA_Q_RANK:KR_SLAB0, :], gkv_ref[...], _dot(wkv_ref[...], dkv))
        y_s[MLA_Q_RANK:KR_SLAB0, :] = dx
        dgkv_ref[...] += dg
        dkr = dk_ref[MLA_NOPE:MLA_PAD, :]
        for h in range(1, MLA_HEADS):
            dkr = dkr + dk_ref[h * MLA_PAD + MLA_NOPE:(h + 1) * MLA_PAD, :]
        dkr = jnp.concatenate([dkr, jnp.zeros((MLA_NOPE, tn), F32)], axis=0)
        y_s[KR_SLAB0:LAT_ROWS, :] = _rope_t(dkr, cm_ref[...], s1m_ref[...], s2m_ref[...])
        y = pltpu.roll(y_s[...], LAT_SHIFT, 0)
        row = lax.broadcasted_iota(jnp.int32, (LAT_ROWS, tn), 0)
        dfl = jnp.concatenate([dfl_ref[...], jnp.zeros((LAT_ROWS - GATE_ROWS, tn), F32)], axis=0)
        dlat_ref[...] = jnp.where(row < LAT_SHIFT, dfl, y).astype(BF16)

    def col(rows):
        return pl.BlockSpec((rows, tn), lambda i: (0, i))

    def full(a):
        return pl.BlockSpec(a.shape, lambda i: (0, 0))

    def acc(r, c):
        return pl.BlockSpec((r, c), lambda i: (0, 0))

    return pl.pallas_call(
        body, name=name, grid=(S // tn,),
        in_specs=[col(LAT_ROWS), col(MLA_Q_RANK), col(MLA_KV_RANK), full(g_q), full(g_kv),
                  full(w_uq_p), full(w_ukv), col(QW), col(QW), col(QW), col(LANES), col(LANES), col(LANES),
                  col(QW), col(QW), col(MLA_HEADS * HEAD_DIM), col(GATE_ROWS)],
        out_specs=[col(LAT_ROWS), acc(QW, MLA_Q_RANK), acc(QW, MLA_KV_RANK), acc(MLA_Q_RANK, 1), acc(MLA_KV_RANK, 1)],
        out_shape=[jax.ShapeDtypeStruct((LAT_ROWS, S), BF16), jax.ShapeDtypeStruct((QW, MLA_Q_RANK), F32),
                   jax.ShapeDtypeStruct((QW, MLA_KV_RANK), F32), jax.ShapeDtypeStruct((MLA_Q_RANK, 1), F32),
                   jax.ShapeDtypeStruct((MLA_KV_RANK, 1), F32)],
        scratch_shapes=[pltpu.VMEM((LAT_ROWS, tn), F32)],
        compiler_params=_params("arbitrary"),
    )(lat, nq, nkv, g_q, g_kv, w_uq_p, w_ukv, *tq, *tmisc, dq, dk, dv, dflog)


def _dproj_cast(dqa, dkva, dqf, dkf, dvf, dlat, *, name):
    S = dqa.shape[1]
    tn = _tile(S, 512)
    parts = (dqa, dkva, dqf, dkf, dvf, dlat)

    def body(*refs):
        o_ref = refs[-1]
        r0 = 0
        for ref in refs[:-1]:
            n = ref.shape[0]
            o_ref[r0:r0 + n, :] = ref[...].astype(BF16)
            r0 += n

    return pl.pallas_call(
        body, name=name, grid=(S // tn,),
        in_specs=[pl.BlockSpec((p.shape[0], tn), lambda i: (0, i)) for p in parts],
        out_specs=pl.BlockSpec((IN_ROWS, tn), lambda i: (0, i)),
        out_shape=jax.ShapeDtypeStruct((IN_ROWS, S), BF16),
        compiler_params=_params("parallel"),
    )(*parts)


GELU_C = math.sqrt(2.0 / math.pi)
GELU_A = 0.044715


def _conv_taps(a, halo, w_ref, b_ref, first):
    row = lax.broadcasted_iota(jnp.int32, a.shape, 0)
    h7 = jnp.where(first, 0.0, halo[7:8, :])
    h6 = jnp.where(first, 0.0, halo[6:7, :])
    a1 = jnp.where(row == 0, h7, pltpu.roll(a, 1, 0))
    a2 = jnp.where(row == 0, h6, jnp.where(row == 1, h7, pltpu.roll(a, 2, 0)))
    u = ((b_ref[...] + w_ref[0:1, :] * a2) + w_ref[1:2, :] * a1) + w_ref[2:3, :] * a
    return u, a1, a2


def _conv_specs(S, tm, tc, nc):
    hb = tm // 8
    main = lambda off: pl.BlockSpec((tm, tc), lambda j, i: (i, j + off))
    halo = lambda off: pl.BlockSpec((8, tc), lambda j, i: (jnp.maximum(i * hb - 1, 0), j + off))
    wspec = lambda off: pl.BlockSpec((3, tc), lambda j, i: (0, j + off))
    bspec = lambda off: pl.BlockSpec((1, tc), lambda j, i: (0, j + off))
    return main, halo, wspec, bspec


def _conv_geglu_fwd(a, conv_w, conv_b, *, name):
    S = a.shape[0]
    tm, tc = _tile(S, 512), _tile(D_FF, 1408)
    nc = D_FF // tc
    main, halo, wspec, bspec = _conv_specs(S, tm, tc, nc)

    def body(ag_ref, au_ref, hg_ref, hu_ref, wg_ref, wu_ref, bg_ref, bu_ref, z_ref):
        first = pl.program_id(1) == 0
        gate, _, _ = _conv_taps(ag_ref[...], hg_ref[...], wg_ref, bg_ref, first)
        up, _, _ = _conv_taps(au_ref[...], hu_ref[...], wu_ref, bu_ref, first)
        cdf = 0.5 * (1.0 + jnp.tanh(GELU_C * (gate + GELU_A * (gate * gate * gate))))
        z_ref[...] = (gate * cdf * up).astype(BF16)

    return pl.pallas_call(
        body, name=name, grid=(nc, S // tm),
        in_specs=[main(0), main(nc), halo(0), halo(nc), wspec(0), wspec(nc), bspec(0), bspec(nc)],
        out_specs=pl.BlockSpec((tm, tc), lambda j, i: (i, j)),
        out_shape=jax.ShapeDtypeStruct((S, D_FF), BF16),
        compiler_params=_params("parallel", "arbitrary"),
    )(a, a, a, a, conv_w, conv_w, conv_b, conv_b)


def _conv_geglu_bwd(a, conv_w, conv_b, dz, *, name):
    S = a.shape[0]
    tm, tc = _tile(S, 512), _tile(D_FF, 1408)
    nc = D_FF // tc
    main, halo, wspec, bspec = _conv_specs(S, tm, tc, nc)

    def body(ag_ref, au_ref, hg_ref, hu_ref, wg_ref, wu_ref, bg_ref, bu_ref, dz_ref, du_ref, dw_ref, db_ref):
        first = pl.program_id(1) == 0

        @pl.when(first)
        def _():
            dw_ref[...] = jnp.zeros_like(dw_ref)
            db_ref[...] = jnp.zeros_like(db_ref)

        gate, g1, g2 = _conv_taps(ag_ref[...], hg_ref[...], wg_ref, bg_ref, first)
        up, u1, u2 = _conv_taps(au_ref[...], hu_ref[...], wu_ref, bu_ref, first)
        dz = dz_ref[...]
        g2x = gate * gate
        th = jnp.tanh(GELU_C * (gate + GELU_A * (g2x * gate)))
        cdf = 0.5 * (1.0 + th)
        dgelu = cdf + gate * (0.5 * (1.0 - th * th) * (GELU_C * (1.0 + 3.0 * GELU_A * g2x)))
        dug = dz * up * dgelu
        duu = dz * (gate * cdf)
        du_ref[0] = dug
        du_ref[1] = duu
        for half, du, taps in ((0, dug, (g2, g1, ag_ref[...])), (1, duu, (u2, u1, au_ref[...]))):
            for tap in range(3):
                dw_ref[half, tap:tap + 1, :] += jnp.sum(du * taps[tap], axis=0, keepdims=True)
            db_ref[half] += jnp.sum(du, axis=0, keepdims=True)

    return pl.pallas_call(
        body, name=name, grid=(nc, S // tm),
        in_specs=[main(0), main(nc), halo(0), halo(nc), wspec(0), wspec(nc), bspec(0), bspec(nc),
                  pl.BlockSpec((tm, tc), lambda j, i: (i, j))],
        out_specs=[pl.BlockSpec((2, tm, tc), lambda j, i: (0, i, j)), pl.BlockSpec((2, 3, tc), lambda j, i: (0, 0, j)),
                   pl.BlockSpec((2, 1, tc), lambda j, i: (0, 0, j))],
        out_shape=[jax.ShapeDtypeStruct((2, S, D_FF), F32), jax.ShapeDtypeStruct((2, 3, D_FF), F32),
                   jax.ShapeDtypeStruct((2, 1, D_FF), F32)],
        compiler_params=_params("parallel", "arbitrary"),
    )(a, a, a, a, conv_w, conv_w, conv_b, conv_b, dz)


def _conv_bwd_input(du, conv_w, *, name):
    S = du.shape[1]
    tm, tc = _tile(S, 512), _tile(D_FF, 1408)
    nc = D_FF // tc
    nr = S // tm
    hb = tm // 8

    def body(du_ref, nx_ref, w_ref, da_ref):
        last = pl.program_id(2) == nr - 1
        d = du_ref[0]
        row = lax.broadcasted_iota(jnp.int32, d.shape, 0)
        n0 = jnp.where(last, 0.0, nx_ref[0, 0:1, :])
        n1 = jnp.where(last, 0.0, nx_ref[0, 1:2, :])
        d1 = jnp.where(row == tm - 1, n0, pltpu.roll(d, tm - 1, 0))
        d2 = jnp.where(row == tm - 1, n1, jnp.where(row == tm - 2, n0, pltpu.roll(d, tm - 2, 0)))
        da_ref[...] = (w_ref[2:3, :] * d + w_ref[1:2, :] * d1 + w_ref[0:1, :] * d2).astype(BF16)

    return pl.pallas_call(
        body, name=name, grid=(2, nc, nr),
        in_specs=[pl.BlockSpec((1, tm, tc), lambda h, j, i: (h, i, j)),
                  pl.BlockSpec((1, 8, tc), lambda h, j, i: (h, jnp.minimum((i + 1) * hb, S // 8 - 1), j)),
                  pl.BlockSpec((3, tc), lambda h, j, i: (0, h * nc + j))],
        out_specs=pl.BlockSpec((tm, tc), lambda h, j, i: (i, h * nc + j)),
        out_shape=jax.ShapeDtypeStruct((S, 2 * D_FF), BF16),
        compiler_params=_params("parallel", "parallel", "arbitrary"),
    )(du, du, conv_w)


def _adamw(w, g, m, v, *, name):
    L, A, B = w.shape
    ta = _tile(A, ROW_TILE)

    def body(w_ref, g_ref, m_ref, v_ref, d_ref, mo_ref, vo_ref):
        g = g_ref[...]
        m = ADAM_B1 * m_ref[...] + (1.0 - ADAM_B1) * g
        v = ADAM_B2 * v_ref[...] + (1.0 - ADAM_B2) * jnp.square(g)
        m_hat = m / (1.0 - ADAM_B1 ** ADAM_STEP)
        v_hat = v / (1.0 - ADAM_B2 ** ADAM_STEP)
        d_ref[...] = -ADAM_LR * (m_hat / (jnp.sqrt(v_hat) + ADAM_EPS) + ADAM_WD * w_ref[...])
        mo_ref[...] = m
        vo_ref[...] = v

    blk = pl.BlockSpec((None, ta, B), lambda l, i: (l, i, 0))
    shp = jax.ShapeDtypeStruct((L, A, B), F32)
    return pl.pallas_call(
        body, name=name, grid=(L, A // ta),
        in_specs=[blk] * 4, out_specs=[blk] * 3, out_shape=[shp] * 3,
        compiler_params=_params("parallel", "parallel"),
    )(w, g, m, v)


def _core_index():
    return jnp.reshape(lax.axis_index("c"), (1,)).astype(jnp.int32)


def _pair_sum(g, recv, *, name):
    L, n, A, B = g.shape
    ta = _tile(A, ROW_TILE)

    def body(c_ref, g_ref, r_ref, o_ref):
        o_ref[...] = g_ref[...] + r_ref[...]

    return pl.pallas_call(
        body, name=name,
        grid_spec=pltpu.PrefetchScalarGridSpec(
            num_scalar_prefetch=1, grid=(HALF_DEPTH, n, A // ta),
            in_specs=[pl.BlockSpec((None, None, ta, B), lambda l, s, r, c_ref: (HALF_DEPTH * c_ref[0] + l, s, r, 0)),
                      pl.BlockSpec((None, None, ta, B), lambda l, s, r, c_ref: (l, s, r, 0))],
            out_specs=pl.BlockSpec((None, None, ta, B), lambda l, s, r, c_ref: (l, s, r, 0))),
        out_shape=jax.ShapeDtypeStruct((HALF_DEPTH, n, A, B), F32),
        compiler_params=_params("parallel", "parallel", "parallel"),
    )(_core_index(), g, recv)


def _chip_sum(parts, *, name):
    n, L2, A, B = parts.shape
    ta = _tile(A, ROW_TILE)

    def body(p_ref, o_ref):
        o_ref[...] = ((p_ref[0] + p_ref[1]) + p_ref[2]) + p_ref[3]

    return pl.pallas_call(
        body, name=name, grid=(L2, A // ta),
        in_specs=[pl.BlockSpec((n, None, ta, B), lambda l, r: (0, l, r, 0))],
        out_specs=pl.BlockSpec((None, ta, B), lambda l, r: (l, r, 0)),
        out_shape=jax.ShapeDtypeStruct((L2, A, B), F32),
        compiler_params=_params("parallel", "parallel"),
    )(parts)


HBM_SPEC = pl.BlockSpec(memory_space=pl.ANY)
COMM_PARAMS = pltpu.CompilerParams(has_side_effects=True)


def _mesh_pos():
    return lax.axis_index("x"), lax.axis_index("y"), lax.axis_index("c")


def _other_chips(x, y):
    return [(1 - x, y), (x, 1 - y), (1 - x, 1 - y)]


def _my_layers(c):
    return pl.ds(HALF_DEPTH * c, HALF_DEPTH)


def _remote(src, dst, send_sems, recv_sems, k, to):
    return pltpu.make_async_remote_copy(src_ref=src, dst_ref=dst, send_sem=send_sems.at[k], recv_sem=recv_sems.at[k],
                                        device_id=to, device_id_type=MESH)


def _gather_shards(shards):
    n = len(shards)

    def body(*refs):
        ins, outs = refs[:n], refs[n:2 * n]
        send_sems, recv_sems, local_sems = refs[2 * n:]
        x, y, c = _mesh_pos()
        me = 2 * x + y
        chips = _other_chips(x, y)
        sibling = (x, y, 1 - c)
        mine, other = _my_layers(c), _my_layers(1 - c)
        local = [pltpu.make_async_copy(ins[k], outs[k].at[:, me], local_sems.at[k]) for k in range(n)]
        for cp in local:
            cp.start()
        first = [_remote(ins[k].at[mine], outs[k].at[mine, me], send_sems, recv_sems, 6 * k + j, (px, py, c))
                 for j, (px, py) in enumerate(chips) for k in range(n)]
        for cp in first:
            cp.start()
        passed = []
        for j, (px, py) in enumerate(chips):
            for k in range(n):
                landed = outs[k].at[mine, 2 * px + py]
                _remote(landed, landed, send_sems, recv_sems, 6 * k + j, (px, py, c)).wait_recv()
                cp = _remote(landed, landed, send_sems, recv_sems, 6 * k + 3 + j, sibling)
                cp.start()
                passed.append(cp)
        for j, (px, py) in enumerate(chips):
            for k in range(n):
                landed = outs[k].at[other, 2 * px + py]
                _remote(landed, landed, send_sems, recv_sems, 6 * k + 3 + j, sibling).wait_recv()
        for cp in first + passed:
            cp.wait_send()
        for cp in local:
            cp.wait()

    return pl.pallas_call(
        body, name="gather_weight_shards",
        in_specs=[HBM_SPEC] * n, out_specs=[HBM_SPEC] * n,
        out_shape=[jax.ShapeDtypeStruct((s.shape[0], N_CHIPS) + s.shape[1:], s.dtype) for s in shards],
        scratch_shapes=[pltpu.SemaphoreType.DMA((6 * n,)), pltpu.SemaphoreType.DMA((6 * n,)), pltpu.SemaphoreType.DMA((n,))],
        compiler_params=COMM_PARAMS,
    )(*shards)


def _sibling_exchange(gs):
    n = len(gs)

    def body(*refs):
        ins, outs = refs[:n], refs[n:2 * n]
        send_sems, recv_sems = refs[2 * n:]
        x, y, c = _mesh_pos()
        copies = [_remote(ins[k].at[_my_layers(1 - c)], outs[k], send_sems, recv_sems, k, (x, y, 1 - c)) for k in range(n)]
        for cp in copies:
            cp.start()
        for cp in copies:
            cp.wait()

    return pl.pallas_call(
        body, name="grad_sibling_exchange",
        in_specs=[HBM_SPEC] * n, out_specs=[HBM_SPEC] * n,
        out_shape=[jax.ShapeDtypeStruct((HALF_DEPTH,) + g.shape[1:], g.dtype) for g in gs],
        scratch_shapes=[pltpu.SemaphoreType.DMA((n,)), pltpu.SemaphoreType.DMA((n,))],
        compiler_params=COMM_PARAMS,
    )(*gs)


def _chip_scatter(ps):
    n = len(ps)

    def body(*refs):
        ins, outs = refs[:n], refs[n:2 * n]
        send_sems, recv_sems, local_sems = refs[2 * n:]
        x, y, c = _mesh_pos()
        me = 2 * x + y
        chips = _other_chips(x, y)
        local = [pltpu.make_async_copy(ins[k].at[:, me], outs[k].at[me], local_sems.at[k]) for k in range(n)]
        for cp in local:
            cp.start()
        copies = [_remote(ins[k].at[:, 2 * px + py], outs[k].at[me], send_sems, recv_sems, 3 * k + j, (px, py, c))
                  for j, (px, py) in enumerate(chips) for k in range(n)]
        for cp in copies:
            cp.start()
        for j, (px, py) in enumerate(chips):
            for k in range(n):
                landed = outs[k].at[2 * px + py]
                _remote(landed, landed, send_sems, recv_sems, 3 * k + j, (px, py, c)).wait_recv()
        for cp in copies:
            cp.wait_send()
        for cp in local:
            cp.wait()

    return pl.pallas_call(
        body, name="grad_chip_scatter",
        in_specs=[HBM_SPEC] * n, out_specs=[HBM_SPEC] * n,
        out_shape=[jax.ShapeDtypeStruct((N_CHIPS, p.shape[0]) + p.shape[2:], p.dtype) for p in ps],
        scratch_shapes=[pltpu.SemaphoreType.DMA((3 * n,)), pltpu.SemaphoreType.DMA((3 * n,)), pltpu.SemaphoreType.DMA((n,))],
        compiler_params=COMM_PARAMS,
    )(*ps)


def _sibling_share(hs):
    n = len(hs)

    def body(*refs):
        ins, outs = refs[:n], refs[n:2 * n]
        send_sems, recv_sems, local_sems = refs[2 * n:]
        x, y, c = _mesh_pos()
        mine, other = _my_layers(c), _my_layers(1 - c)
        local = [pltpu.make_async_copy(ins[k], outs[k].at[mine], local_sems.at[k]) for k in range(n)]
        for cp in local:
            cp.start()
        copies = [_remote(ins[k], outs[k].at[mine], send_sems, recv_sems, k, (x, y, 1 - c)) for k in range(n)]
        for cp in copies:
            cp.start()
        for k in range(n):
            landed = outs[k].at[other]
            _remote(landed, landed, send_sems, recv_sems, k, (x, y, 1 - c)).wait_recv()
        for cp in copies:
            cp.wait_send()
        for cp in local:
            cp.wait()

    return pl.pallas_call(
        body, name="grad_sibling_share",
        in_specs=[HBM_SPEC] * n, out_specs=[HBM_SPEC] * n,
        out_shape=[jax.ShapeDtypeStruct((DEPTH,) + h.shape[1:], h.dtype) for h in hs],
        scratch_shapes=[pltpu.SemaphoreType.DMA((n,)), pltpu.SemaphoreType.DMA((n,)), pltpu.SemaphoreType.DMA((n,))],
        compiler_params=COMM_PARAMS,
    )(*hs)


def _allreduce_small(part):
    rows, C = part.shape

    def body(p_ref, o_ref, slots, send_sems, recv_sems):
        x, y, c = _mesh_pos()
        me = 4 * x + 2 * y + c
        slots[me] = p_ref[...]
        copies = []
        for k in range(1, 8):
            kx, ky, kc = (k >> 2) & 1, (k >> 1) & 1, k & 1
            peer = (x ^ kx if kx else x, y ^ ky if ky else y, c ^ kc if kc else c)
            cp = _remote(p_ref, slots.at[me], send_sems, recv_sems, k - 1, peer)
            cp.start()
            copies.append((cp, peer))
        for k, (cp, peer) in enumerate(copies):
            src = 4 * peer[0] + 2 * peer[1] + peer[2]
            _remote(p_ref, slots.at[src], send_sems, recv_sems, k, peer).wait_recv()
        for cp, _ in copies:
            cp.wait_send()
        total = slots[0]
        for d in range(1, 8):
            total = total + slots[d]
        o_ref[...] = total

    return pl.pallas_call(
        body, name="small_grad_allreduce",
        in_specs=[pl.BlockSpec(memory_space=pltpu.VMEM)], out_specs=pl.BlockSpec(memory_space=pltpu.VMEM),
        out_shape=jax.ShapeDtypeStruct((rows, C), F32),
        scratch_shapes=[pltpu.VMEM((8, rows, C), F32), pltpu.SemaphoreType.DMA((7,)), pltpu.SemaphoreType.DMA((7,))],
        compiler_params=pltpu.CompilerParams(has_side_effects=True, vmem_limit_bytes=VMEM_LIMIT_BYTES),
    )(part)


def _pad_w_uq(w):
    lead = w.shape[:-1]
    w = w.reshape(lead + (MLA_HEADS, MLA_QK))
    w = jnp.concatenate([w, jnp.zeros(lead + (MLA_HEADS, MLA_PAD - MLA_QK), w.dtype)], axis=-1)
    return w.reshape(lead + (MLA_HEADS * MLA_PAD,))


def _unpad_w_uq(g):
    lead = g.shape[:-1]
    return g.reshape(lead + (MLA_HEADS, MLA_PAD))[..., :MLA_QK].reshape(lead + (MLA_HEADS * MLA_QK,))


def _t(a):
    return jnp.swapaxes(a, -1, -2)


def _cols_of_shards(g):
    L, n, A, B = g.shape
    return g.transpose(0, 2, 1, 3).reshape(L, A, n * B)


def _shards_of_cols(w):
    A, NB = w.shape
    return w.reshape(A, N_CHIPS, NB // N_CHIPS).transpose(1, 0, 2)


BIG = ("w_in", "w_uq", "w_ukv", "w_out", "w_up", "w_down")
SMALL = ("attn_pre_norm", "forget_bias", "swa_sinks", "rel_bias", "q_latent_norm", "kv_latent_norm", "group_norm",
         "attn_post_norm", "ffn_pre_norm", "conv_b", "ffn_post_norm")
WEIGHTS = ("attn_pre_norm", "w_in", "forget_bias", "swa_sinks", "rel_bias", "q_latent_norm", "w_uq", "kv_latent_norm",
           "w_ukv", "group_norm", "w_out", "attn_post_norm", "ffn_pre_norm", "w_up", "conv_w", "conv_b", "w_down",
           "ffn_post_norm")


def _pack(arrs, cols, row_mult):
    flat = jnp.concatenate([a.reshape(-1) for a in arrs])
    n = flat.shape[0]
    per = cols * row_mult
    total = -(-n // per) * per
    return jnp.pad(flat, (0, total - n)).reshape(total // cols, cols)


def _unpack(packed, shapes):
    flat = packed.reshape(-1)
    out, off = [], 0
    for shp in shapes:
        n = int(np.prod(shp))
        out.append(flat[off:off + n].reshape(shp))
        off += n
    return out


def _kernel_weights(gathered, small):
    L = gathered["w_in"].shape[0]
    w_in_t = _t(gathered["w_in"]).reshape(L, IN_COLS, D_MODEL)
    w_in_t = jnp.pad(w_in_t, ((0, 0), (0, IN_ROWS - IN_COLS), (0, 0)))
    w_uq_p = _pad_w_uq(_cols_of_shards(gathered["w_uq"]))
    w_ukv = _cols_of_shards(gathered["w_ukv"])
    W = dict(small)
    W.update(w_qkv_t=w_in_t[:, :QKV_ROWS], w_lat_t=w_in_t[:, QKV_ROWS:], w_in_t=w_in_t, w_uq_p=w_uq_p, w_uq_t=_t(w_uq_p),
             w_ukv=w_ukv, w_ukv_t=_t(w_ukv), w_out=gathered["w_out"].reshape(L, D_MODEL, D_MODEL), w_up=gathered["w_up"],
             conv_w=_cols_of_shards(gathered["conv_w"]), w_down=gathered["w_down"].reshape(L, D_FF, D_MODEL))
    return W


def _local_step(x, target, W):
    S = x.shape[0]
    tq_tabs, tm_tabs = _rope_tables(S)
    onehot_t = _rel_onehot_t()
    bias_t = _bias_table(W["rel_bias"].T, onehot_t).reshape(SWA_Q_HEADS, 2 * WINDOW, WINDOW)
    row = lambda a: a.reshape(1, -1)
    col = lambda a: a.reshape(-1, 1)
    fox_rows = (FOX_ROW0, FOX_ROW0 + FOX_HEADS * HEAD_DIM, FOX_ROW0 + 2 * FOX_HEADS * HEAD_DIM, SWA_Q_HEADS)
    fox = dict(rows=fox_rows, H=FOX_HEADS, Dk=HEAD_DIM, Dv=HEAD_DIM, scale=HEAD_DIM ** -0.5)
    mla = dict(rows=(0, 0, 0, SWA_Q_HEADS + FOX_HEADS), H=MLA_HEADS, Dk=MLA_PAD, Dv=HEAD_DIM, scale=MLA_QK ** -0.5)

    saved = []
    h = _rms_fwd(x, row(W["attn_pre_norm"][0]), name="rms_in")
    for l in range(DEPTH):
        sv = {"x0": x, "h1": h}
        qkv = _matmul(W["w_qkv_t"][l], h, tb=True, out_dtype=BF16, name="proj_qkv")
        lat = _matmul(W["w_lat_t"][l], h, tb=True, name="proj_lat")
        oa, lse_a = _swa_fwd(qkv, bias_t, W["swa_sinks"][l], name="swa_fwd")
        fb_col = jnp.pad(col(W["forget_bias"][l]), ((0, GATE_ROWS - FOX_HEADS), (0, 0)))
        f4 = _gate_fwd(lat, fb_col, name="fox_gate_fwd")[:FOX_HEADS]
        f_row, f_col = f4[:, None, :], f4.T
        of, lse_f = _attn_fwd(qkv, qkv, qkv, f_row=f_row, f_col=f_col, name="fox_fwd", **fox)
        nq, nkv, qm, km, vm = _mla_prep_fwd(lat, col(W["q_latent_norm"][l]), col(W["kv_latent_norm"][l]), W["w_uq_t"][l],
                                            W["w_ukv_t"][l], tq_tabs, tm_tabs, name="mla_prep_fwd")
        oc, lse_c = _attn_fwd(qm, km, vm, name="mla_fwd", **mla)
        mixed = _group_norm_fwd(oa, of, oc, col(W["group_norm"][l]), name="group_norm_fwd")
        y = _matmul(mixed, W["w_out"][l], ta=True, name="proj_out")
        x1, h2 = _resid_rms(x, y, row(W["attn_post_norm"][l]), row(W["ffn_pre_norm"][l]), name="attn_resid")
        a = _matmul(h2, W["w_up"], b_shards=l, name="ffn_up")
        z = _conv_geglu_fwd(a, W["conv_w"][l], row(W["conv_b"][l]), name="conv_geglu_fwd")
        y2 = _matmul(z, W["w_down"][l], name="ffn_down")
        g_next = row(W["attn_pre_norm"][l + 1]) if l + 1 < DEPTH else None
        x2, h_next = _resid_rms(x1, y2, row(W["ffn_post_norm"][l]), g_next, name="ffn_resid")
        sv.update(qkv=qkv, lat=lat, oa=oa, lse_a=lse_a, fb_col=fb_col, f_row=f_row, f_col=f_col, of=of, lse_f=lse_f,
                  nq=nq, nkv=nkv, qm=qm, km=km, vm=vm, oc=oc, lse_c=lse_c, mixed=mixed, y=y, x1=x1, h2=h2, a=a, z=z, y2=y2)
        saved.append(sv)
        x, h = x2, h_next

    loss, dx = _loss_head(x, target)

    G = {k: [None] * DEPTH for k in WEIGHTS if k not in ("rel_bias", "w_up", "w_down", "w_out")}
    g_up = lax.empty((DEPTH, N_CHIPS, D_MODEL, FF_SHARD), F32)
    g_down = lax.empty((DEPTH, D_FF, D_MODEL), F32)
    g_out = lax.empty((DEPTH, D_MODEL, D_MODEL), F32)
    dbias_layers = [None] * DEPTH
    for l in reversed(range(DEPTH)):
        sv = saved[l]
        dy2, dg = _rms_bwd(sv["y2"], row(W["ffn_post_norm"][l]), dx, out_dtype=BF16, name="ffn_post_bwd")
        G["ffn_post_norm"][l] = dg[0]
        dz = _matmul(dy2, W["w_down"][l], tb=True, name="ffn_down_dx")
        g_down = _matmul(sv["z"], dy2, ta=True, out_into=(g_down, l, False), name="ffn_down_dw")
        du, dcw, dcb = _conv_geglu_bwd(sv["a"], W["conv_w"][l], row(W["conv_b"][l]), dz, name="conv_geglu_bwd")
        G["conv_w"][l] = dcw.transpose(1, 0, 2).reshape(3, 2 * D_FF)
        G["conv_b"][l] = dcb.reshape(2 * D_FF)
        da = _conv_bwd_input(du, W["conv_w"][l], name="conv_bwd_input")
        dh2 = _matmul(da, W["w_up"], tb=True, b_shards=l, name="ffn_up_dx")
        g_up = _matmul(sv["h2"], da, ta=True, out_into=(g_up, l, True), name="ffn_up_dw")
        dx1, dg = _rms_bwd(sv["x1"], row(W["ffn_pre_norm"][l]), dh2, resid=dx, out_dtype=F32, name="ffn_pre_bwd")
        G["ffn_pre_norm"][l] = dg[0]
        dy, dg = _rms_bwd(sv["y"], row(W["attn_post_norm"][l]), dx1, out_dtype=BF16, name="attn_post_bwd")
        G["attn_post_norm"][l] = dg[0]
        dmixed = _matmul(W["w_out"][l], dy, tb=True, name="proj_out_dx")
        g_out = _matmul(sv["mixed"], dy, out_into=(g_out, l, False), name="proj_out_dw")
        doa, dof, doc, dg, delta = _group_norm_bwd(sv["oa"], sv["of"], sv["oc"], col(W["group_norm"][l]), dmixed,
                                                   name="group_norm_bwd")
        G["group_norm"][l] = dg[:, 0]
        dqa, dkva, dbias_l, dsink = _swa_bwd(sv["qkv"], bias_t, W["swa_sinks"][l], doa, sv["lse_a"],
                                             delta.reshape(-1, S), name="swa_bwd")
        dbias_layers[l] = dbias_l.reshape(SWA_Q_HEADS, -1)
        G["swa_sinks"][l] = dsink[:, 0]
        dqf, dkf, dvf, dfk = _attn_bwd(sv["qkv"], sv["qkv"], sv["qkv"], do=dof, lse=sv["lse_f"], delta=delta,
                                       f_row=sv["f_row"], f_col=sv["f_col"], name="fox_bwd", **fox)
        dF = jnp.pad(dfk.T, ((0, GATE_ROWS - FOX_HEADS), (0, 0)))
        dflog, dfb = _gate_bwd(sv["lat"], sv["fb_col"], dF, name="fox_gate_bwd")
        G["forget_bias"][l] = dfb[:FOX_HEADS, 0]
        dqm, dkm, dvm = _attn_bwd(sv["qm"], sv["km"], sv["vm"], do=doc, lse=sv["lse_c"], delta=delta, name="mla_bwd", **mla)
        dlat, dwq_t, dwkv_t, dgq, dgkv = _mla_prep_bwd(
            sv["lat"], sv["nq"], sv["nkv"], col(W["q_latent_norm"][l]), col(W["kv_latent_norm"][l]), W["w_uq_p"][l],
            W["w_ukv"][l], tq_tabs, tm_tabs, dqm, dkm, dvm, dflog, name="mla_prep_bwd")
        G["w_uq"][l], G["w_ukv"][l] = _shards_of_cols(_unpad_w_uq(dwq_t.T)), _shards_of_cols(dwkv_t.T)
        G["q_latent_norm"][l], G["kv_latent_norm"][l] = dgq[:, 0], dgkv[:, 0]
        dproj = _dproj_cast(dqa, dkva, dqf, dkf, dvf, dlat, name="dproj_cast")
        dh1 = _matmul(dproj, W["w_in_t"][l], ta=True, name="proj_in_dx")
        dw_in_t = _matmul(dproj, sv["h1"], name="proj_in_dw")
        G["w_in"][l] = _t(dw_in_t[:IN_COLS].reshape(N_CHIPS, IN_COLS // N_CHIPS, D_MODEL))
        dx, dg = _rms_bwd(sv["x0"], row(W["attn_pre_norm"][l]), dh1, resid=dx1, out_dtype=F32, name="attn_pre_bwd")
        G["attn_pre_norm"][l] = dg[0]

    grads = {k: jnp.stack(v) for k, v in G.items()}
    grads["rel_bias"] = _bias_table_bwd(jnp.stack(dbias_layers), onehot_t).T
    grads["w_up"] = g_up
    grads["w_down"] = g_down.reshape(DEPTH, N_CHIPS, D_FF // N_CHIPS, D_MODEL)
    grads["w_out"] = g_out.reshape(DEPTH, N_CHIPS, D_MODEL // N_CHIPS, D_MODEL)
    return loss, dx, grads


def kernel(x, attn_pre_norm, w_in, forget_bias, swa_sinks, rel_bias, q_latent_norm, w_uq, kv_latent_norm, w_ukv, group_norm, w_out, attn_post_norm, ffn_pre_norm, w_up, conv_w, conv_b, w_down, ffn_post_norm, loss_target, m_attn_pre_norm, m_w_in, m_forget_bias, m_swa_sinks, m_rel_bias, m_q_latent_norm, m_w_uq, m_kv_latent_norm, m_w_ukv, m_group_norm, m_w_out, m_attn_post_norm, m_ffn_pre_norm, m_w_up, m_conv_w, m_conv_b, m_w_down, m_ffn_post_norm, v_attn_pre_norm, v_w_in, v_forget_bias, v_swa_sinks, v_rel_bias, v_q_latent_norm, v_w_uq, v_kv_latent_norm, v_w_ukv, v_group_norm, v_w_out, v_attn_post_norm, v_ffn_pre_norm, v_w_up, v_conv_w, v_conv_b, v_w_down, v_ffn_post_norm):
    args = dict(locals())
    w = {k: args[k] for k in WEIGHTS}
    m = {k: args["m_" + k] for k in WEIGHTS}
    v = {k: args["v_" + k] for k in WEIGHTS}

    sent = BIG + ("conv_w",)
    gathered = dict(zip(sent, _gather_shards([w[k] if k == "conv_w" else w[k].astype(BF16) for k in sent])))
    W = _kernel_weights(gathered, {k: w[k] for k in SMALL})

    loss_part, dx, g = _local_step(x[0], loss_target[0], W)
    loss = lax.psum(loss_part, ("x", "y", "c"))

    gs = [g[k] for k in BIG]
    recv = _sibling_exchange(gs)
    pair = [_pair_sum(gk, rk, name="grad_pair_sum") for gk, rk in zip(gs, recv)]
    landed = _chip_scatter(pair)
    mine = [_chip_sum(p, name="grad_chip_sum") for p in landed]
    out_g = dict(zip(BIG, _sibling_share(mine)))
    out_d, out_m, out_v = {}, {}, {}
    for k in BIG:
        out_d[k], out_m[k], out_v[k] = _adamw(w[k], out_g[k], m[k], v[k], name="adamw_" + k)

    small_shapes = [w[k].shape for k in SMALL]
    reduced = _allreduce_small(_pack([g[k] for k in SMALL] + [g["conv_w"]], LANES, 8))
    *g_small, g_cw = _unpack(reduced, small_shapes + [g["conv_w"].shape])
    chip = 2 * lax.axis_index("x") + lax.axis_index("y")
    g_small.append(lax.dynamic_slice_in_dim(g_cw, chip * FF_SHARD, FF_SHARD, axis=2))
    names = SMALL + ("conv_w",)
    shapes = small_shapes + [w["conv_w"].shape]
    packed = lambda arrs: _pack(arrs, LANES, ROW_TILE)[None]
    d_s, m_s, v_s = _adamw(packed([w[k] for k in names]), packed(g_small), packed([m[k] for k in names]),
                           packed([v[k] for k in names]), name="adamw_small")
    out_g.update(zip(names, g_small))
    out_d.update(zip(names, _unpack(d_s, shapes)))
    out_m.update(zip(names, _unpack(m_s, shapes)))
    out_v.update(zip(names, _unpack(v_s, shapes)))

    return (loss, dx[None], *[out_g[k] for k in WEIGHTS], *[out_d[k] for k in WEIGHTS],
            *[out_m[k] for k in WEIGHTS], *[out_v[k] for k in WEIGHTS])
```

```python
import math

import numpy as np
import jax
import jax.numpy as jnp
from jax import lax
from jax.experimental import pallas as pl
from jax.experimental.pallas import tpu as pltpu

F32 = jnp.float32
BF16 = jnp.bfloat16

D_MODEL = 1024
DEPTH = 4
HEAD_DIM = 64
SWA_Q_HEADS = 8
SWA_KV_HEADS = 2
SWA_GROUP = SWA_Q_HEADS // SWA_KV_HEADS
WINDOW = 128
FOX_HEADS = 4
MLA_HEADS = 4
MLA_Q_RANK = 256
MLA_KV_RANK = 128
MLA_NOPE = 64
MLA_ROPE = 32
MLA_QK = MLA_NOPE + MLA_ROPE
ROPE_THETA = 10000.0
REL_BUCKETS = 32
REL_MAX_DIST = 128
D_FF = 2816
EPS = 1e-6
NEG_INF = -1e30
LANES = 128
N_CHIPS = 4
HALF_DEPTH = DEPTH // 2

IN_COLS = 1956
IN_ROWS = 2048
QKV_ROWS = 1536
LAT_ROWS = IN_ROWS - QKV_ROWS
LAT_SHIFT = FOX_HEADS
FOX_ROW0 = 768
MLA_PAD = LANES
GATE_ROWS = 8

ADAM_LR = 0.001
ADAM_B1 = 0.9
ADAM_B2 = 0.999
ADAM_EPS = 1e-08
ADAM_WD = 0.01
ADAM_STEP = 10

VMEM_LIMIT_BYTES = 48 * 1024 * 1024
ATT_TILE = 256
ROW_TILE = 256
MESH = pl.DeviceIdType.MESH

NT = (((1,), (1,)), ((), ()))
TN = (((0,), (0,)), ((), ()))
NN = (((1,), (0,)), ((), ()))


def _params(*sem):
    return pltpu.CompilerParams(dimension_semantics=sem, vmem_limit_bytes=VMEM_LIMIT_BYTES)


def _tile(dim, cap):
    for t in (2048, 1408, 1024, 512, 256, 128, 64, 32, 16, 8):
        if t <= cap and dim % t == 0:
            return t
    return dim


def _dot(a, b, dims=NN):
    return lax.dot_general(a, b, dims, preferred_element_type=F32)


def _split3(a):
    a1 = a.astype(BF16)
    r1 = a - a1.astype(F32)
    a2 = r1.astype(BF16)
    a3 = (r1 - a2.astype(F32)).astype(BF16)
    return a1, a2, a3


FF_SHARD = 2 * D_FF // N_CHIPS


def _matmul(a, b, *, ta=False, tb=False, out_dtype=F32, name, b_shards=False, out_shards=False):
    if ta:
        K, M = a.shape
    else:
        M, K = a.shape
    if b_shards:
        K2, N = (2 * D_FF, D_MODEL) if tb else (D_MODEL, 2 * D_FF)
    elif tb:
        N, K2 = b.shape
    else:
        K2, N = b.shape
    assert K == K2, (a.shape, b.shape)
    tm, tn, tk = _tile(M, 1408), _tile(N, 1408), _tile(K, 1408)
    nk = K // tk
    dims = (((0 if ta else 1,), (1 if tb else 0,)), ((), ()))

    def body(a_ref, b_ref, o_ref, acc_ref):
        k = pl.program_id(2)

        @pl.when(k == 0)
        def _():
            acc_ref[...] = jnp.zeros_like(acc_ref)

        acc_ref[...] += lax.dot_general(a_ref[...], b_ref[...], dims, preferred_element_type=F32)

        @pl.when(k == nk - 1)
        def _():
            o_ref[...] = acc_ref[...].astype(o_ref.dtype)

    a_spec = pl.BlockSpec((tk, tm), lambda i, j, k: (k, i)) if ta else pl.BlockSpec((tm, tk), lambda i, j, k: (i, k))
    if b_shards and tb:
        assert tk == FF_SHARD
        b_spec = pl.BlockSpec((None, tn, tk), lambda i, j, k: (k, j, 0))
    elif b_shards:
        assert tn == FF_SHARD
        b_spec = pl.BlockSpec((None, tk, tn), lambda i, j, k: (j, k, 0))
    else:
        b_spec = pl.BlockSpec((tn, tk), lambda i, j, k: (j, k)) if tb else pl.BlockSpec((tk, tn), lambda i, j, k: (k, j))
    if out_shards:
        assert tn == FF_SHARD
        out_spec = pl.BlockSpec((None, tm, tn), lambda i, j, k: (j, i, 0))
        out_shape = jax.ShapeDtypeStruct((N // tn, M, tn), out_dtype)
    else:
        out_spec = pl.BlockSpec((tm, tn), lambda i, j, k: (i, j))
        out_shape = jax.ShapeDtypeStruct((M, N), out_dtype)
    return pl.pallas_call(
        body, name=name, grid=(M // tm, N // tn, nk),
        in_specs=[a_spec, b_spec], out_specs=out_spec, out_shape=out_shape,
        scratch_shapes=[pltpu.VMEM((tm, tn), F32)],
        compiler_params=_params("parallel", "parallel", "arbitrary"),
    )(a, b)


def _seg_rms(xs, g):
    r = lax.rsqrt(jnp.mean(xs * xs, axis=-1, keepdims=True) + EPS)
    return xs * r * g


def _seg_rms_bwd(xs, g, dy):
    r = lax.rsqrt(jnp.mean(xs * xs, axis=-1, keepdims=True) + EPS)
    gd = dy * g
    c = jnp.mean(gd * xs, axis=-1, keepdims=True)
    dx = r * gd - xs * (r * r * r * c)
    dg = jnp.sum(dy * (xs * r), axis=0, keepdims=True)
    return dx, dg


def _rms_fwd(x, g, *, name):
    S, W = x.shape
    tm = _tile(S, 512)

    def body(x_ref, g_ref, o_ref):
        o_ref[...] = _seg_rms(x_ref[...], g_ref[...]).astype(o_ref.dtype)

    return pl.pallas_call(
        body, name=name, grid=(S // tm,),
        in_specs=[pl.BlockSpec((tm, W), lambda i: (i, 0)), pl.BlockSpec((1, W), lambda i: (0, 0))],
        out_specs=pl.BlockSpec((tm, W), lambda i: (i, 0)),
        out_shape=jax.ShapeDtypeStruct((S, W), BF16),
        compiler_params=_params("parallel"),
    )(x, g)


def _rms_bwd(x, g, dy, *, resid=None, out_dtype, name):
    S, W = x.shape
    tm = _tile(S, 512)
    has_resid = resid is not None

    def body(*refs):
        if has_resid:
            x_ref, g_ref, dy_ref, r_ref, dx_ref, dg_ref = refs
        else:
            x_ref, g_ref, dy_ref, dx_ref, dg_ref = refs

        @pl.when(pl.program_id(0) == 0)
        def _():
            dg_ref[...] = jnp.zeros_like(dg_ref)

        dx, dg = _seg_rms_bwd(x_ref[...], g_ref[...], dy_ref[...])
        if has_resid:
            dx = dx + r_ref[...]
        dx_ref[...] = dx.astype(dx_ref.dtype)
        dg_ref[...] += dg

    row = pl.BlockSpec((tm, W), lambda i: (i, 0))
    vec = pl.BlockSpec((1, W), lambda i: (0, 0))
    ins = [x, g, dy] + ([resid] if has_resid else [])
    return pl.pallas_call(
        body, name=name, grid=(S // tm,),
        in_specs=[row, vec, row] + ([row] if has_resid else []),
        out_specs=[row, vec],
        out_shape=[jax.ShapeDtypeStruct((S, W), out_dtype), jax.ShapeDtypeStruct((1, W), F32)],
        compiler_params=_params("arbitrary"),
    )(*ins)


def _resid_rms(x, y, g_post, g_next, *, name):
    S, W = x.shape
    tm = _tile(S, 512)
    with_next = g_next is not None

    def body(*refs):
        if with_next:
            x_ref, y_ref, gp_ref, gn_ref, xo_ref, h_ref = refs
        else:
            x_ref, y_ref, gp_ref, xo_ref = refs
        xn = x_ref[...] + _seg_rms(y_ref[...], gp_ref[...])
        xo_ref[...] = xn
        if with_next:
            h_ref[...] = _seg_rms(xn, gn_ref[...]).astype(BF16)

    row = pl.BlockSpec((tm, W), lambda i: (i, 0))
    vec = pl.BlockSpec((1, W), lambda i: (0, 0))
    outs = [jax.ShapeDtypeStruct((S, W), F32)] + ([jax.ShapeDtypeStruct((S, W), BF16)] if with_next else [])
    res = pl.pallas_call(
        body, name=name, grid=(S // tm,),
        in_specs=[row, row, vec] + ([vec] if with_next else []),
        out_specs=[row] + ([row] if with_next else []),
        out_shape=outs,
        compiler_params=_params("parallel"),
    )(*([x, y, g_post] + ([g_next] if with_next else [])))
    return (res[0], res[1]) if with_next else (res[0], None)


def _col_rms(xs, g):
    r = lax.rsqrt(jnp.mean(xs * xs, axis=0, keepdims=True) + EPS)
    return xs * r * g


def _col_rms_bwd(xs, g, dy):
    r = lax.rsqrt(jnp.mean(xs * xs, axis=0, keepdims=True) + EPS)
    gd = dy * g
    c = jnp.mean(gd * xs, axis=0, keepdims=True)
    dx = r * gd - xs * (r * r * r * c)
    dg = jnp.sum(dy * (xs * r), axis=1, keepdims=True)
    return dx, dg


GROUP_ROWS = (SWA_Q_HEADS * HEAD_DIM, FOX_HEADS * HEAD_DIM, MLA_HEADS * HEAD_DIM)


def _group_specs(S, tn):
    outs = [pl.BlockSpec((n, tn), lambda i: (0, i)) for n in GROUP_ROWS]
    g = pl.BlockSpec((D_MODEL, 1), lambda i: (0, 0))
    mixed = pl.BlockSpec((D_MODEL, tn), lambda i: (0, i))
    return outs, g, mixed


def _group_norm_fwd(oa, of, oc, g, *, name):
    S = oa.shape[1]
    tn = _tile(S, 512)
    outs, gs, mixed = _group_specs(S, tn)

    def body(a_ref, f_ref, c_ref, g_ref, o_ref):
        r0 = 0
        for ref, n in zip((a_ref, f_ref, c_ref), GROUP_ROWS):
            o_ref[r0:r0 + n, :] = _col_rms(ref[...], g_ref[r0:r0 + n, :]).astype(BF16)
            r0 += n

    return pl.pallas_call(
        body, name=name, grid=(S // tn,),
        in_specs=outs + [gs], out_specs=mixed,
        out_shape=jax.ShapeDtypeStruct((D_MODEL, S), BF16),
        compiler_params=_params("parallel"),
    )(oa, of, oc, g)


def _group_norm_bwd(oa, of, oc, g, dmixed, *, name):
    S = oa.shape[1]
    tn = _tile(S, 512)
    outs, gs, mixed = _group_specs(S, tn)
    n_heads = D_MODEL // HEAD_DIM

    def body(a_ref, f_ref, c_ref, g_ref, dm_ref, da_ref, df_ref, dc_ref, dg_ref, dl_ref):
        @pl.when(pl.program_id(0) == 0)
        def _():
            dg_ref[...] = jnp.zeros_like(dg_ref)

        r0 = 0
        for ref, dref, n in zip((a_ref, f_ref, c_ref), (da_ref, df_ref, dc_ref), GROUP_ROWS):
            o = ref[...]
            dx, dg = _col_rms_bwd(o, g_ref[r0:r0 + n, :], dm_ref[r0:r0 + n, :])
            dxb = dx.astype(BF16)
            dref[...] = dxb
            dg_ref[r0:r0 + n, :] += dg
            od = o * dxb.astype(F32)
            for h in range(n // HEAD_DIM):
                dl_ref[r0 // HEAD_DIM + h] = jnp.sum(od[h * HEAD_DIM:(h + 1) * HEAD_DIM, :], axis=0, keepdims=True)
            r0 += n

    return pl.pallas_call(
        body, name=name, grid=(S // tn,),
        in_specs=outs + [gs, mixed], out_specs=outs + [gs, pl.BlockSpec((n_heads, 1, tn), lambda i: (0, 0, i))],
        out_shape=[jax.ShapeDtypeStruct((n, S), BF16) for n in GROUP_ROWS] + [jax.ShapeDtypeStruct((D_MODEL, 1), F32),
                                                                              jax.ShapeDtypeStruct((n_heads, 1, S), F32)],
        compiler_params=_params("arbitrary"),
    )(oa, of, oc, g, dmixed)


def _loss_head(y, target):
    S, W = y.shape
    tm = _tile(S, 512)

    def body(y_ref, t_ref, d_ref, l_ref):
        @pl.when(pl.program_id(0) == 0)
        def _():
            l_ref[...] = jnp.zeros_like(l_ref)

        err = y_ref[...] - t_ref[...]
        d_ref[...] = err * (1.0 / W)
        l_ref[...] += 0.5 * jnp.sum(jnp.mean(err * err, axis=-1, keepdims=True), axis=0, keepdims=True)

    row = pl.BlockSpec((tm, W), lambda i: (i, 0))
    d, l = pl.pallas_call(
        body, name="loss_head", grid=(S // tm,),
        in_specs=[row, row],
        out_specs=[row, pl.BlockSpec((1, 1), lambda i: (0, 0))],
        out_shape=[jax.ShapeDtypeStruct((S, W), F32), jax.ShapeDtypeStruct((1, 1), F32)],
        compiler_params=_params("arbitrary"),
    )(y, target)
    return l[0, 0], d


def _attn_fwd(q_src, k_src, v_src, rows, H, Dk, Dv, scale, f_row=None, f_col=None, *, name):
    S = q_src.shape[1]
    T = _tile(S, ATT_TILE)
    nq = S // T
    forget = f_row is not None
    qb, kb, vb = rows[0] // (H * Dk), rows[1] // (H * Dk), rows[2] // (H * Dv)
    hs = range(H)

    def body(*refs):
        if forget:
            q_ref, k_ref, v_ref, fq_ref, fk_ref, o_ref, lse_ref = refs
        else:
            q_ref, k_ref, v_ref, o_ref, lse_ref = refs
        i = pl.program_id(0)

        def tile(j, masked, state):
            off = pl.multiple_of(j * T, T)
            ss = [_dot(k_ref[h * Dk:(h + 1) * Dk, pl.ds(off, T)], q_ref[h * Dk:(h + 1) * Dk, :], TN) * scale for h in hs]
            if forget:
                ss = [ss[h] + (fq_ref[h] - fk_ref[pl.ds(off, T), h:h + 1]) for h in hs]
            if masked:
                r = lax.broadcasted_iota(jnp.int32, (T, T), 0)
                c = lax.broadcasted_iota(jnp.int32, (T, T), 1)
                ss = [jnp.where(r <= c, s, NEG_INF) for s in ss]
            m_new = [jnp.maximum(state[h][0], jnp.max(ss[h], axis=0, keepdims=True)) for h in hs]
            alpha = [jnp.exp(state[h][0] - m_new[h]) for h in hs]
            ps = [jnp.exp(ss[h] - m_new[h]) for h in hs]
            l_new = [alpha[h] * state[h][1] + jnp.sum(ps[h], axis=0, keepdims=True) for h in hs]
            p_hi = [p.astype(BF16) for p in ps]
            vs = [v_ref[h * Dv:(h + 1) * Dv, pl.ds(off, T)] for h in hs]
            pv = [_dot(vs[h], p_hi[h]) for h in hs]
            if forget:
                pv = [pv[h] + _dot(vs[h], (ps[h] - p_hi[h].astype(F32)).astype(BF16)) for h in hs]
            return tuple((m_new[h], l_new[h], alpha[h] * state[h][2] + pv[h]) for h in hs)

        init = tuple((jnp.full((1, T), NEG_INF, F32), jnp.zeros((1, T), F32), jnp.zeros((Dv, T), F32)) for _ in hs)
        state = lax.fori_loop(0, i, lambda j, st: tile(j, False, st), init)
        state = tile(i, True, state)
        for h in hs:
            m, l, acc = state[h]
            o_ref[h * Dv:(h + 1) * Dv, :] = acc / l
            lse_ref[h] = m + jnp.log(l)

    in_specs = [pl.BlockSpec((H * Dk, T), lambda i: (qb, i)),
                pl.BlockSpec((H * Dk, S), lambda i: (kb, 0)),
                pl.BlockSpec((H * Dv, S), lambda i: (vb, 0))]
    ins = [q_src, k_src, v_src]
    if forget:
        in_specs += [pl.BlockSpec((H, 1, T), lambda i: (0, 0, i)), pl.BlockSpec((S, H), lambda i: (0, 0))]
        ins += [f_row, f_col]
    return pl.pallas_call(
        body, name=name, grid=(nq,),
        in_specs=in_specs,
        out_specs=[pl.BlockSpec((H * Dv, T), lambda i: (0, i)), pl.BlockSpec((H, 1, T), lambda i: (0, 0, i))],
        out_shape=[jax.ShapeDtypeStruct((H * Dv, S), F32), jax.ShapeDtypeStruct((H, 1, S), F32)],
        compiler_params=_params("parallel"),
    )(*ins)


def _attn_bwd(q_src, k_src, v_src, rows, H, Dk, Dv, scale, do, lse, delta, f_row=None, f_col=None, *, name):
    S = q_src.shape[1]
    T = _tile(S, ATT_TILE)
    nq = S // T
    forget = f_row is not None
    qb, kb, vb, db = rows[0] // (H * Dk), rows[1] // (H * Dk), rows[2] // (H * Dv), rows[3] // H
    hs = range(H)

    def body(*refs):
        if forget:
            (q_ref, k_ref, v_ref, do_ref, lse_ref, dl_ref, fq_ref, fk_ref,
             dq_ref, dk_ref, dv_ref, df_ref, dk_s, dv_s, df_s) = refs
        else:
            q_ref, k_ref, v_ref, do_ref, lse_ref, dl_ref, dq_ref, dk_ref, dv_ref, dk_s, dv_s = refs
        j = pl.program_id(0)

        @pl.when(j == 0)
        def _():
            dq_ref[...] = jnp.zeros_like(dq_ref)

        dk_s[...] = jnp.zeros_like(dk_s)
        dv_s[...] = jnp.zeros_like(dv_s)
        if forget:
            df_s[...] = jnp.zeros_like(df_s)
        kt = [k_ref[h * Dk:(h + 1) * Dk, :] for h in hs]
        kj = [k.T for k in kt]
        vj = [v_ref[h * Dv:(h + 1) * Dv, :].T for h in hs]
        koff = pl.multiple_of(j * T, T)

        def tile(i, masked):
            cols = pl.ds(pl.multiple_of(i * T, T), T)
            qi = [q_ref[h * Dk:(h + 1) * Dk, cols] for h in hs]
            doi = [do_ref[h * Dv:(h + 1) * Dv, cols] for h in hs]
            st = [_dot(kj[h], qi[h]) * scale for h in hs]
            if forget:
                st = [st[h] + (fq_ref[h, :, cols] - fk_ref[pl.ds(koff, T), h:h + 1]) for h in hs]
            if masked:
                r = lax.broadcasted_iota(jnp.int32, (T, T), 0)
                c = lax.broadcasted_iota(jnp.int32, (T, T), 1)
                st = [jnp.where(r <= c, x, NEG_INF) for x in st]
            pt = [jnp.exp(st[h] - lse_ref[h, :, cols]) for h in hs]
            dpt = [_dot(vj[h], doi[h]) for h in hs]
            dst = [pt[h] * (dpt[h] - dl_ref[h, :, cols]) for h in hs]
            ptb = [p.astype(BF16) for p in pt]
            dsb = [d.astype(BF16) for d in dst]
            for h in hs:
                dv_s[h * Dv:(h + 1) * Dv, :] += _dot(doi[h], ptb[h], NT)
            for h in hs:
                dk_s[h * Dk:(h + 1) * Dk, :] += _dot(qi[h], dsb[h], NT)
            for h in hs:
                dq_ref[h * Dk:(h + 1) * Dk, cols] += _dot(kt[h], dsb[h]) * scale
            if forget:
                for h in hs:
                    part = dst[h][:, 0:LANES]
                    for c0 in range(LANES, T, LANES):
                        part = part + dst[h][:, c0:c0 + LANES]
                    df_s[h] += part

        tile(j, True)

        def loop_body(i, carry):
            tile(i, False)
            return carry

        lax.fori_loop(j + 1, nq, loop_body, 0)
        dk_ref[...] = dk_s[...] * scale
        dv_ref[...] = dv_s[...]
        if forget:
            df_ref[...] = jnp.concatenate([-jnp.sum(df_s[h], axis=-1, keepdims=True) for h in hs], axis=1)

    res = lambda D, b0: pl.BlockSpec((H * D, S), lambda j: (b0, 0))
    blk = lambda D, b0: pl.BlockSpec((H * D, T), lambda j: (b0, j))
    row3 = lambda b0: pl.BlockSpec((H, 1, S), lambda j: (b0, 0, 0))
    in_specs = [res(Dk, qb), blk(Dk, kb), blk(Dv, vb), res(Dv, 0), row3(0), row3(db)]
    ins = [q_src, k_src, v_src, do, lse, delta]
    out_specs = [res(Dk, 0), blk(Dk, 0), blk(Dv, 0)]
    out_shape = [jax.ShapeDtypeStruct((H * Dk, S), F32), jax.ShapeDtypeStruct((H * Dk, S), F32),
                 jax.ShapeDtypeStruct((H * Dv, S), F32)]
    scratch = [pltpu.VMEM((H * Dk, T), F32), pltpu.VMEM((H * Dv, T), F32)]
    if forget:
        in_specs += [row3(0), pl.BlockSpec((S, H), lambda j: (0, 0))]
        ins += [f_row, f_col]
        out_specs.append(pl.BlockSpec((T, H), lambda j: (j, 0)))
        out_shape.append(jax.ShapeDtypeStruct((S, H), F32))
        scratch.append(pltpu.VMEM((H, T, min(T, LANES)), F32))
    return pl.pallas_call(
        body, name=name, grid=(nq,),
        in_specs=in_specs, out_specs=out_specs, out_shape=out_shape, scratch_shapes=scratch,
        compiler_params=_params("arbitrary"),
    )(*ins)


def _swa_masks(i):
    r = lax.broadcasted_iota(jnp.int32, (WINDOW, WINDOW), 0)
    c = lax.broadcasted_iota(jnp.int32, (WINDOW, WINDOW), 1)
    return (r > c) & (i > 0), r <= c


def _swa_specs():
    W = WINDOW
    kv_rows = SWA_KV_HEADS * HEAD_DIM
    q = pl.BlockSpec((SWA_Q_HEADS * HEAD_DIM, W), lambda i: (0, i))
    prev = lambda b: pl.BlockSpec((kv_rows, W), lambda i: (b, jnp.maximum(i - 1, 0)))
    cur = lambda b: pl.BlockSpec((kv_rows, W), lambda i: (b, i))
    bias = pl.BlockSpec((SWA_Q_HEADS, 2 * W, W), lambda i: (0, 0, 0))
    stat = pl.BlockSpec((SWA_Q_HEADS, W), lambda i: (0, i))
    sink = pl.BlockSpec(memory_space=pltpu.SMEM)
    return q, prev(4), cur(4), prev(5), cur(5), bias, stat, sink


def _swa_scores(h, q_ref, kp_ref, kc_ref, b_ref, masks):
    g = h // SWA_GROUP
    rows = slice(g * HEAD_DIM, (g + 1) * HEAD_DIM)
    qh = q_ref[h * HEAD_DIM:(h + 1) * HEAD_DIM, :]
    scale = HEAD_DIM ** -0.5
    s_p = jnp.where(masks[0], _dot(kp_ref[rows, :], qh, TN) * scale + b_ref[h, 0:WINDOW, :], NEG_INF)
    s_c = jnp.where(masks[1], _dot(kc_ref[rows, :], qh, TN) * scale + b_ref[h, WINDOW:2 * WINDOW, :], NEG_INF)
    return qh, rows, s_p, s_c


def _swa_fwd(qkv, bias_t, sinks, *, name):
    S = qkv.shape[1]
    qs, kp, kc, vp, vc, bs, stat, sk = _swa_specs()

    def body(sink_ref, q_ref, kp_ref, kc_ref, vp_ref, vc_ref, b_ref, o_ref, lse_ref):
        masks = _swa_masks(pl.program_id(0))
        for h in range(SWA_Q_HEADS):
            qh, rows, s_p, s_c = _swa_scores(h, q_ref, kp_ref, kc_ref, b_ref, masks)
            sink = sink_ref[h]
            m = jnp.maximum(jnp.maximum(jnp.max(s_p, axis=0, keepdims=True), jnp.max(s_c, axis=0, keepdims=True)), sink)
            p_p = jnp.exp(s_p - m)
            p_c = jnp.exp(s_c - m)
            l = jnp.sum(p_p, axis=0, keepdims=True) + jnp.sum(p_c, axis=0, keepdims=True) + jnp.exp(sink - m)
            o = _dot(vp_ref[rows, :], p_p.astype(BF16)) + _dot(vc_ref[rows, :], p_c.astype(BF16))
            o_ref[h * HEAD_DIM:(h + 1) * HEAD_DIM, :] = o / l
            lse_ref[h:h + 1, :] = m + jnp.log(l)

    return pl.pallas_call(
        body, name=name, grid=(S // WINDOW,),
        in_specs=[sk, qs, kp, kc, vp, vc, bs],
        out_specs=[qs, stat],
        out_shape=[jax.ShapeDtypeStruct((SWA_Q_HEADS * HEAD_DIM, S), F32), jax.ShapeDtypeStruct((SWA_Q_HEADS, S), F32)],
        compiler_params=_params("parallel"),
    )(sinks, qkv, qkv, qkv, qkv, qkv, bias_t)


def _swa_bwd(qkv, bias_t, sinks, do, lse, delta, *, name):
    S = qkv.shape[1]
    W = WINDOW
    qs, kp, kc, vp, vc, bs, stat, sk = _swa_specs()
    scale = HEAD_DIM ** -0.5
    kv_rows = SWA_KV_HEADS * HEAD_DIM

    def body(sink_ref, q_ref, kp_ref, kc_ref, vp_ref, vc_ref, b_ref, do_ref, lse_ref, dl_ref,
             dq_ref, dkv_ref, db_ref, dsk_ref):
        i = pl.program_id(0)

        @pl.when(i == 0)
        def _():
            dkv_ref[...] = jnp.zeros_like(dkv_ref)
            db_ref[...] = jnp.zeros_like(db_ref)
            dsk_ref[...] = jnp.zeros_like(dsk_ref)

        masks = _swa_masks(i)
        prev = pl.ds(pl.multiple_of(jnp.maximum(i - 1, 0) * W, W), W)
        cur = pl.ds(pl.multiple_of(i * W, W), W)
        for h in range(SWA_Q_HEADS):
            qh, rows, s_p, s_c = _swa_scores(h, q_ref, kp_ref, kc_ref, b_ref, masks)
            vrows = slice(kv_rows + rows.start, kv_rows + rows.stop)
            hrows = slice(h * HEAD_DIM, (h + 1) * HEAD_DIM)
            doh = do_ref[hrows, :]
            lse_h = lse_ref[h:h + 1, :]
            delta = dl_ref[h:h + 1, :]
            p_p = jnp.exp(s_p - lse_h)
            p_c = jnp.exp(s_c - lse_h)
            ds_p = p_p * (_dot(vp_ref[rows, :], doh, TN) - delta)
            ds_c = p_c * (_dot(vc_ref[rows, :], doh, TN) - delta)
            db_ref[h, 0:W, :] += ds_p
            db_ref[h, W:2 * W, :] += ds_c
            dsk = -jnp.sum(jnp.exp(sink_ref[h] - lse_h) * delta, axis=1, keepdims=True)
            dsk_ref[h:h + 1, :] += jnp.broadcast_to(dsk, (1, LANES))
            dsb_p = ds_p.astype(BF16)
            dsb_c = ds_c.astype(BF16)
            dq_ref[hrows, :] = (_dot(kp_ref[rows, :], dsb_p) + _dot(kc_ref[rows, :], dsb_c)) * scale
            dkv_ref[rows, prev] += _dot(qh, dsb_p, NT) * scale
            dkv_ref[rows, cur] += _dot(qh, dsb_c, NT) * scale
            dkv_ref[vrows, prev] += _dot(doh, p_p.astype(BF16), NT)
            dkv_ref[vrows, cur] += _dot(doh, p_c.astype(BF16), NT)

    return pl.pallas_call(
        body, name=name, grid=(S // W,),
        in_specs=[sk, qs, kp, kc, vp, vc, bs, qs, stat, stat],
        out_specs=[qs, pl.BlockSpec((2 * kv_rows, S), lambda i: (0, 0)), bs, pl.BlockSpec((SWA_Q_HEADS, LANES), lambda i: (0, 0))],
        out_shape=[jax.ShapeDtypeStruct((SWA_Q_HEADS * HEAD_DIM, S), F32), jax.ShapeDtypeStruct((2 * kv_rows, S), F32),
                   jax.ShapeDtypeStruct((SWA_Q_HEADS, 2 * W, W), F32), jax.ShapeDtypeStruct((SWA_Q_HEADS, LANES), F32)],
        compiler_params=_params("arbitrary"),
    )(sinks, qkv, qkv, qkv, qkv, qkv, bias_t, do, lse, delta)


def _rel_onehot_t():
    qi = jnp.arange(WINDOW, dtype=jnp.int32)[None, :] + WINDOW
    kj = jnp.arange(2 * WINDOW, dtype=jnp.int32)[:, None]
    dist = qi - kj
    max_exact = REL_BUCKETS // 2
    d = jnp.maximum(dist, 0)
    log_ratio = jnp.log(jnp.maximum(d, 1).astype(F32) / max_exact) / math.log(REL_MAX_DIST / max_exact)
    large = jnp.minimum(max_exact + (log_ratio * (REL_BUCKETS - max_exact)).astype(jnp.int32), REL_BUCKETS - 1)
    bucket = jnp.where(d < max_exact, d, large).reshape(-1)
    return (bucket[None, :] == jnp.arange(REL_BUCKETS, dtype=jnp.int32)[:, None]).astype(BF16)


def _bias_table(rel_bias_t, onehot_t):
    Hq, NB = rel_bias_t.shape
    N = onehot_t.shape[1]
    tn = _tile(N, 4096)

    def body(r_ref, oh_ref, o_ref):
        oh = oh_ref[...]
        a1, a2, a3 = _split3(r_ref[...])
        o_ref[...] = _dot(a1, oh) + _dot(a2, oh) + _dot(a3, oh)

    return pl.pallas_call(
        body, name="rel_bias_table", grid=(N // tn,),
        in_specs=[pl.BlockSpec((Hq, NB), lambda j: (0, 0)), pl.BlockSpec((NB, tn), lambda j: (0, j))],
        out_specs=pl.BlockSpec((Hq, tn), lambda j: (0, j)),
        out_shape=jax.ShapeDtypeStruct((Hq, N), F32),
        compiler_params=_params("parallel"),
    )(rel_bias_t, onehot_t)


def _bias_table_bwd(dbias, onehot_t):
    L, Hq, N = dbias.shape
    NB = onehot_t.shape[0]
    tn = _tile(N, 4096)

    def body(d_ref, oh_ref, o_ref):
        @pl.when(pl.program_id(0) == 0)
        def _():
            o_ref[...] = jnp.zeros_like(o_ref)

        d = d_ref[0]
        for l in range(1, L):
            d = d + d_ref[l]
        oh = oh_ref[...]
        a1, a2, a3 = _split3(d)
        o_ref[...] += _dot(a1, oh, NT) + _dot(a2, oh, NT) + _dot(a3, oh, NT)

    return pl.pallas_call(
        body, name="rel_bias_bwd", grid=(N // tn,),
        in_specs=[pl.BlockSpec((L, Hq, tn), lambda j: (0, 0, j)), pl.BlockSpec((NB, tn), lambda j: (0, j))],
        out_specs=pl.BlockSpec((Hq, NB), lambda j: (0, 0)),
        out_shape=jax.ShapeDtypeStruct((Hq, NB), F32),
        compiler_params=_params("arbitrary"),
    )(dbias, onehot_t)


def _gate_fwd(lat, fb_col, *, name):
    S = lat.shape[1]
    tn = _tile(S, 256)

    def body(z_ref, fb_ref, o_ref, carry):
        @pl.when(pl.program_id(0) == 0)
        def _():
            carry[...] = jnp.zeros_like(carry)

        z = z_ref[...] + fb_ref[...]
        lf = jnp.minimum(z, 0.0) - jnp.log1p(jnp.exp(-jnp.abs(z)))
        r = lax.broadcasted_iota(jnp.int32, (tn, tn), 0)
        c = lax.broadcasted_iota(jnp.int32, (tn, tn), 1)
        tri = (r <= c).astype(BF16)
        a1, a2, a3 = _split3(lf)
        cum = _dot(a1, tri) + _dot(a2, tri) + _dot(a3, tri) + carry[:, 0:1]
        o_ref[...] = cum
        carry[...] = jnp.broadcast_to(cum[:, tn - 1:tn], carry.shape)

    return pl.pallas_call(
        body, name=name, grid=(S // tn,),
        in_specs=[pl.BlockSpec((GATE_ROWS, tn), lambda i: (0, i)), pl.BlockSpec((GATE_ROWS, 1), lambda i: (0, 0))],
        out_specs=pl.BlockSpec((GATE_ROWS, tn), lambda i: (0, i)),
        out_shape=jax.ShapeDtypeStruct((GATE_ROWS, S), F32),
        scratch_shapes=[pltpu.VMEM((GATE_ROWS, LANES), F32)],
        compiler_params=_params("arbitrary"),
    )(lat, fb_col)


def _gate_bwd(lat, fb_col, dF, *, name):
    S = lat.shape[1]
    tn = _tile(S, 256)
    nt = S // tn

    def body(z_ref, fb_ref, df_ref, dz_ref, dfb_ref, carry):
        @pl.when(pl.program_id(0) == 0)
        def _():
            carry[...] = jnp.zeros_like(carry)
            dfb_ref[...] = jnp.zeros_like(dfb_ref)

        r = lax.broadcasted_iota(jnp.int32, (tn, tn), 0)
        c = lax.broadcasted_iota(jnp.int32, (tn, tn), 1)
        tri = (r >= c).astype(BF16)
        a1, a2, a3 = _split3(df_ref[...])
        dlf = _dot(a1, tri) + _dot(a2, tri) + _dot(a3, tri) + carry[:, 0:1]
        carry[...] = jnp.broadcast_to(dlf[:, 0:1], carry.shape)
        z = z_ref[...] + fb_ref[...]
        row = lax.broadcasted_iota(jnp.int32, (GATE_ROWS, tn), 0)
        dz = jnp.where(row < FOX_HEADS, dlf / (1.0 + jnp.exp(z)), 0.0)
        dz_ref[...] = dz
        dfb_ref[...] += jnp.sum(dz, axis=1, keepdims=True)

    blk = pl.BlockSpec((GATE_ROWS, tn), lambda i: (0, nt - 1 - i))
    vec = pl.BlockSpec((GATE_ROWS, 1), lambda i: (0, 0))
    return pl.pallas_call(
        body, name=name, grid=(nt,),
        in_specs=[blk, vec, blk], out_specs=[blk, vec],
        out_shape=[jax.ShapeDtypeStruct((GATE_ROWS, S), F32), jax.ShapeDtypeStruct((GATE_ROWS, 1), F32)],
        scratch_shapes=[pltpu.VMEM((GATE_ROWS, LANES), F32)],
        compiler_params=_params("arbitrary"),
    )(lat, fb_col, dF)


def _rope_tables(S):
    pos = jnp.arange(S, dtype=F32)
    inv_freq = ROPE_THETA ** (-(jnp.arange(MLA_ROPE // 2, dtype=F32) * 2.0 / MLA_ROPE))
    ang = pos[:, None] * inv_freq[None, :]
    cos, sin = jnp.cos(ang).T, jnp.sin(ang).T
    z16 = jnp.zeros_like(cos)

    def slab(lo, fill):
        def put(first, second, f):
            return jnp.concatenate([jnp.full((lo, S), f, F32), first, second, jnp.full((LANES - lo - MLA_ROPE, S), f, F32)], axis=0)
        return put(cos, cos, fill), put(-sin, z16, 0.0), put(z16, sin, 0.0)

    tq = tuple(jnp.tile(t, (MLA_HEADS, 1)) for t in slab(MLA_NOPE, 1.0))
    return tq, slab(0, 0.0)


def _rope(x, c, s1, s2):
    n = x.shape[0]
    half = MLA_ROPE // 2
    return x * c + pltpu.roll(x, n - half, 0) * s1 + pltpu.roll(x, half, 0) * s2


def _rope_t(dy, c, s1, s2):
    n = dy.shape[0]
    half = MLA_ROPE // 2
    return dy * c + pltpu.roll(dy * s1, half, 0) + pltpu.roll(dy * s2, n - half, 0)


KR_SLAB0 = MLA_Q_RANK + MLA_KV_RANK


def _mla_prep_fwd(lat, g_q, g_kv, w_uq_t, w_ukv_t, tq, tmisc, *, name):
    S = lat.shape[1]
    tn = _tile(S, 512)
    QW = MLA_HEADS * MLA_PAD

    def body(lat_ref, gq_ref, gkv_ref, wq_ref, wkv_ref, c_ref, s1_ref, s2_ref, cm_ref, s1m_ref, s2m_ref,
             nq_ref, nkv_ref, q_ref, k_ref, v_ref):
        x = pltpu.roll(lat_ref[...], LAT_ROWS - LAT_SHIFT, 0)
        nq = _col_rms(x[0:MLA_Q_RANK, :], gq_ref[...]).astype(BF16)
        nkv = _col_rms(x[MLA_Q_RANK:KR_SLAB0, :], gkv_ref[...]).astype(BF16)
        nq_ref[...] = nq
        nkv_ref[...] = nkv
        q_ref[...] = _rope(_dot(wq_ref[...], nq), c_ref[...], s1_ref[...], s2_ref[...]).astype(BF16)
        kv = _dot(wkv_ref[...], nkv).astype(BF16)
        kr = _rope(x[KR_SLAB0:LAT_ROWS, :], cm_ref[...], s1m_ref[...], s2m_ref[...]).astype(BF16)
        for h in range(MLA_HEADS):
            k_ref[h * MLA_PAD:h * MLA_PAD + MLA_NOPE, :] = kv[h * LANES:h * LANES + MLA_NOPE, :]
            k_ref[h * MLA_PAD + MLA_NOPE:(h + 1) * MLA_PAD, :] = kr[0:MLA_PAD - MLA_NOPE, :]
            v_ref[h * HEAD_DIM:(h + 1) * HEAD_DIM, :] = kv[h * LANES + MLA_NOPE:(h + 1) * LANES, :]

    def col(rows):
        return pl.BlockSpec((rows, tn), lambda i: (0, i))

    def full(a):
        return pl.BlockSpec(a.shape, lambda i: (0, 0))

    return pl.pallas_call(
        body, name=name, grid=(S // tn,),
        in_specs=[col(LAT_ROWS), full(g_q), full(g_kv), full(w_uq_t), full(w_ukv_t),
                  col(QW), col(QW), col(QW), col(LANES), col(LANES), col(LANES)],
        out_specs=[col(MLA_Q_RANK), col(MLA_KV_RANK), col(QW), col(QW), col(MLA_HEADS * HEAD_DIM)],
        out_shape=[jax.ShapeDtypeStruct((MLA_Q_RANK, S), BF16), jax.ShapeDtypeStruct((MLA_KV_RANK, S), BF16),
                   jax.ShapeDtypeStruct((QW, S), BF16), jax.ShapeDtypeStruct((QW, S), BF16),
                   jax.ShapeDtypeStruct((MLA_HEADS * HEAD_DIM, S), BF16)],
        compiler_params=_params("parallel"),
    )(lat, g_q, g_kv, w_uq_t, w_ukv_t, *tq, *tmisc)


def _mla_prep_bwd(lat, nq, nkv, g_q, g_kv, w_uq_p, w_ukv, tq, tmisc, dq, dk, dv, dflog, *, name):
    S = lat.shape[1]
    tn = _tile(S, 512)
    QW = MLA_HEADS * MLA_PAD

    def body(lat_ref, nq_ref, nkv_ref, gq_ref, gkv_ref, wq_ref, wkv_ref, c_ref, s1_ref, s2_ref,
             cm_ref, s1m_ref, s2m_ref, dq_ref, dk_ref, dv_ref, dfl_ref,
             dlat_ref, dwq_ref, dwkv_ref, dgq_ref, dgkv_ref, y_s):
        @pl.when(pl.program_id(0) == 0)
        def _():
            dwq_ref[...] = jnp.zeros_like(dwq_ref)
            dwkv_ref[...] = jnp.zeros_like(dwkv_ref)
            dgq_ref[...] = jnp.zeros_like(dgq_ref)
            dgkv_ref[...] = jnp.zeros_like(dgkv_ref)

        x = pltpu.roll(lat_ref[...], LAT_ROWS - LAT_SHIFT, 0)
        dqm = _rope_t(dq_ref[...], c_ref[...], s1_ref[...], s2_ref[...]).astype(BF16)
        dwq_ref[...] += _dot(dqm, nq_ref[...], NT)
        dx, dg = _col_rms_bwd(x[0:MLA_Q_RANK, :], gq_ref[...], _dot(wq_ref[...], dqm))
        y_s[0:MLA_Q_RANK, :] = dx
        dgq_ref[...] += dg
        dkv = jnp.concatenate(
            [part for h in range(MLA_HEADS)
             for part in (dk_ref[h * MLA_PAD:h * MLA_PAD + MLA_NOPE, :], dv_ref[h * HEAD_DIM:(h + 1) * HEAD_DIM, :])],
            axis=0).astype(BF16)
        dwkv_ref[...] += _dot(dkv, nkv_ref[...], NT)
        dx, dg = _col_rms_bwd(x[MLA_Q_RANK:KR_SLAB0, :], gkv_ref[...], _dot(wkv_ref[...], dkv))
        y_s[MLA_Q_RANK:KR_SLAB0, :] = dx
        dgkv_ref[...] += dg
        dkr = dk_ref[MLA_NOPE:MLA_PAD, :]
        for h in range(1, MLA_HEADS):
            dkr = dkr + dk_ref[h * MLA_PAD + MLA_NOPE:(h + 1) * MLA_PAD, :]
        dkr = jnp.concatenate([dkr, jnp.zeros((MLA_NOPE, tn), F32)], axis=0)
        y_s[KR_SLAB0:LAT_ROWS, :] = _rope_t(dkr, cm_ref[...], s1m_ref[...], s2m_ref[...])
        y = pltpu.roll(y_s[...], LAT_SHIFT, 0)
        row = lax.broadcasted_iota(jnp.int32, (LAT_ROWS, tn), 0)
        dfl = jnp.concatenate([dfl_ref[...], jnp.zeros((LAT_ROWS - GATE_ROWS, tn), F32)], axis=0)
        dlat_ref[...] = jnp.where(row < LAT_SHIFT, dfl, y).astype(BF16)

    def col(rows):
        return pl.BlockSpec((rows, tn), lambda i: (0, i))

    def full(a):
        return pl.BlockSpec(a.shape, lambda i: (0, 0))

    def acc(r, c):
        return pl.BlockSpec((r, c), lambda i: (0, 0))

    return pl.pallas_call(
        body, name=name, grid=(S // tn,),
        in_specs=[col(LAT_ROWS), col(MLA_Q_RANK), col(MLA_KV_RANK), full(g_q), full(g_kv),
                  full(w_uq_p), full(w_ukv), col(QW), col(QW), col(QW), col(LANES), col(LANES), col(LANES),
                  col(QW), col(QW), col(MLA_HEADS * HEAD_DIM), col(GATE_ROWS)],
        out_specs=[col(LAT_ROWS), acc(QW, MLA_Q_RANK), acc(QW, MLA_KV_RANK), acc(MLA_Q_RANK, 1), acc(MLA_KV_RANK, 1)],
        out_shape=[jax.ShapeDtypeStruct((LAT_ROWS, S), BF16), jax.ShapeDtypeStruct((QW, MLA_Q_RANK), F32),
                   jax.ShapeDtypeStruct((QW, MLA_KV_RANK), F32), jax.ShapeDtypeStruct((MLA_Q_RANK, 1), F32),
                   jax.ShapeDtypeStruct((MLA_KV_RANK, 1), F32)],
        scratch_shapes=[pltpu.VMEM((LAT_ROWS, tn), F32)],
        compiler_params=_params("arbitrary"),
    )(lat, nq, nkv, g_q, g_kv, w_uq_p, w_ukv, *tq, *tmisc, dq, dk, dv, dflog)


def _dproj_cast(dqa, dkva, dqf, dkf, dvf, dlat, *, name):
    S = dqa.shape[1]
    tn = _tile(S, 512)
    parts = (dqa, dkva, dqf, dkf, dvf, dlat)

    def body(*refs):
        o_ref = refs[-1]
        r0 = 0
        for ref in refs[:-1]:
            n = ref.shape[0]
            o_ref[r0:r0 + n, :] = ref[...].astype(BF16)
            r0 += n

    return pl.pallas_call(
        body, name=name, grid=(S // tn,),
        in_specs=[pl.BlockSpec((p.shape[0], tn), lambda i: (0, i)) for p in parts],
        out_specs=pl.BlockSpec((IN_ROWS, tn), lambda i: (0, i)),
        out_shape=jax.ShapeDtypeStruct((IN_ROWS, S), BF16),
        compiler_params=_params("parallel"),
    )(*parts)


GELU_C = math.sqrt(2.0 / math.pi)
GELU_A = 0.044715


def _conv_taps(a, halo, w_ref, b_ref, first):
    row = lax.broadcasted_iota(jnp.int32, a.shape, 0)
    h7 = jnp.where(first, 0.0, halo[7:8, :])
    h6 = jnp.where(first, 0.0, halo[6:7, :])
    a1 = jnp.where(row == 0, h7, pltpu.roll(a, 1, 0))
    a2 = jnp.where(row == 0, h6, jnp.where(row == 1, h7, pltpu.roll(a, 2, 0)))
    u = ((b_ref[...] + w_ref[0:1, :] * a2) + w_ref[1:2, :] * a1) + w_ref[2:3, :] * a
    return u, a1, a2


def _conv_specs(S, tm, tc, nc):
    hb = tm // 8
    main = lambda off: pl.BlockSpec((tm, tc), lambda j, i: (i, j + off))
    halo = lambda off: pl.BlockSpec((8, tc), lambda j, i: (jnp.maximum(i * hb - 1, 0), j + off))
    wspec = lambda off: pl.BlockSpec((3, tc), lambda j, i: (0, j + off))
    bspec = lambda off: pl.BlockSpec((1, tc), lambda j, i: (0, j + off))
    return main, halo, wspec, bspec


def _conv_geglu_fwd(a, conv_w, conv_b, *, name):
    S = a.shape[0]
    tm, tc = _tile(S, 512), _tile(D_FF, 1408)
    nc = D_FF // tc
    main, halo, wspec, bspec = _conv_specs(S, tm, tc, nc)

    def body(ag_ref, au_ref, hg_ref, hu_ref, wg_ref, wu_ref, bg_ref, bu_ref, z_ref):
        first = pl.program_id(1) == 0
        gate, _, _ = _conv_taps(ag_ref[...], hg_ref[...], wg_ref, bg_ref, first)
        up, _, _ = _conv_taps(au_ref[...], hu_ref[...], wu_ref, bu_ref, first)
        cdf = 0.5 * (1.0 + jnp.tanh(GELU_C * (gate + GELU_A * (gate * gate * gate))))
        z_ref[...] = (gate * cdf * up).astype(BF16)

    return pl.pallas_call(
        body, name=name, grid=(nc, S // tm),
        in_specs=[main(0), main(nc), halo(0), halo(nc), wspec(0), wspec(nc), bspec(0), bspec(nc)],
        out_specs=pl.BlockSpec((tm, tc), lambda j, i: (i, j)),
        out_shape=jax.ShapeDtypeStruct((S, D_FF), BF16),
        compiler_params=_params("parallel", "arbitrary"),
    )(a, a, a, a, conv_w, conv_w, conv_b, conv_b)


def _conv_geglu_bwd(a, conv_w, conv_b, dz, *, name):
    S = a.shape[0]
    tm, tc = _tile(S, 512), _tile(D_FF, 1408)
    nc = D_FF // tc
    main, halo, wspec, bspec = _conv_specs(S, tm, tc, nc)

    def body(ag_ref, au_ref, hg_ref, hu_ref, wg_ref, wu_ref, bg_ref, bu_ref, dz_ref, du_ref, dw_ref, db_ref):
        first = pl.program_id(1) == 0

        @pl.when(first)
        def _():
            dw_ref[...] = jnp.zeros_like(dw_ref)
            db_ref[...] = jnp.zeros_like(db_ref)

        gate, g1, g2 = _conv_taps(ag_ref[...], hg_ref[...], wg_ref, bg_ref, first)
        up, u1, u2 = _conv_taps(au_ref[...], hu_ref[...], wu_ref, bu_ref, first)
        dz = dz_ref[...]
        g2x = gate * gate
        th = jnp.tanh(GELU_C * (gate + GELU_A * (g2x * gate)))
        cdf = 0.5 * (1.0 + th)
        dgelu = cdf + gate * (0.5 * (1.0 - th * th) * (GELU_C * (1.0 + 3.0 * GELU_A * g2x)))
        dug = dz * up * dgelu
        duu = dz * (gate * cdf)
        du_ref[0] = dug
        du_ref[1] = duu
        for half, du, taps in ((0, dug, (g2, g1, ag_ref[...])), (1, duu, (u2, u1, au_ref[...]))):
            for tap in range(3):
                dw_ref[half, tap:tap + 1, :] += jnp.sum(du * taps[tap], axis=0, keepdims=True)
            db_ref[half] += jnp.sum(du, axis=0, keepdims=True)

    return pl.pallas_call(
        body, name=name, grid=(nc, S // tm),
        in_specs=[main(0), main(nc), halo(0), halo(nc), wspec(0), wspec(nc), bspec(0), bspec(nc),
                  pl.BlockSpec((tm, tc), lambda j, i: (i, j))],
        out_specs=[pl.BlockSpec((2, tm, tc), lambda j, i: (0, i, j)), pl.BlockSpec((2, 3, tc), lambda j, i: (0, 0, j)),
                   pl.BlockSpec((2, 1, tc), lambda j, i: (0, 0, j))],
        out_shape=[jax.ShapeDtypeStruct((2, S, D_FF), F32), jax.ShapeDtypeStruct((2, 3, D_FF), F32),
                   jax.ShapeDtypeStruct((2, 1, D_FF), F32)],
        compiler_params=_params("parallel", "arbitrary"),
    )(a, a, a, a, conv_w, conv_w, conv_b, conv_b, dz)


def _conv_bwd_input(du, conv_w, *, name):
    S = du.shape[1]
    tm, tc = _tile(S, 512), _tile(D_FF, 1408)
    nc = D_FF // tc
    nr = S // tm
    hb = tm // 8

    def body(du_ref, nx_ref, w_ref, da_ref):
        last = pl.program_id(2) == nr - 1
        d = du_ref[0]
        row = lax.broadcasted_iota(jnp.int32, d.shape, 0)
        n0 = jnp.where(last, 0.0, nx_ref[0, 0:1, :])
        n1 = jnp.where(last, 0.0, nx_ref[0, 1:2, :])
        d1 = jnp.where(row == tm - 1, n0, pltpu.roll(d, tm - 1, 0))
        d2 = jnp.where(row == tm - 1, n1, jnp.where(row == tm - 2, n0, pltpu.roll(d, tm - 2, 0)))
        da_ref[...] = (w_ref[2:3, :] * d + w_ref[1:2, :] * d1 + w_ref[0:1, :] * d2).astype(BF16)

    return pl.pallas_call(
        body, name=name, grid=(2, nc, nr),
        in_specs=[pl.BlockSpec((1, tm, tc), lambda h, j, i: (h, i, j)),
                  pl.BlockSpec((1, 8, tc), lambda h, j, i: (h, jnp.minimum((i + 1) * hb, S // 8 - 1), j)),
                  pl.BlockSpec((3, tc), lambda h, j, i: (0, h * nc + j))],
        out_specs=pl.BlockSpec((tm, tc), lambda h, j, i: (i, h * nc + j)),
        out_shape=jax.ShapeDtypeStruct((S, 2 * D_FF), BF16),
        compiler_params=_params("parallel", "parallel", "arbitrary"),
    )(du, du, conv_w)


def _adamw(w, g, m, v, *, name):
    L, A, B = w.shape
    ta = _tile(A, ROW_TILE)

    def body(w_ref, g_ref, m_ref, v_ref, d_ref, mo_ref, vo_ref):
        g = g_ref[...]
        m = ADAM_B1 * m_ref[...] + (1.0 - ADAM_B1) * g
        v = ADAM_B2 * v_ref[...] + (1.0 - ADAM_B2) * jnp.square(g)
        m_hat = m / (1.0 - ADAM_B1 ** ADAM_STEP)
        v_hat = v / (1.0 - ADAM_B2 ** ADAM_STEP)
        d_ref[...] = -ADAM_LR * (m_hat / (jnp.sqrt(v_hat) + ADAM_EPS) + ADAM_WD * w_ref[...])
        mo_ref[...] = m
        vo_ref[...] = v

    blk = pl.BlockSpec((None, ta, B), lambda l, i: (l, i, 0))
    shp = jax.ShapeDtypeStruct((L, A, B), F32)
    return pl.pallas_call(
        body, name=name, grid=(L, A // ta),
        in_specs=[blk] * 4, out_specs=[blk] * 3, out_shape=[shp] * 3,
        compiler_params=_params("parallel", "parallel"),
    )(w, g, m, v)


def _scalar(v):
    return jnp.reshape(v, (1,)).astype(jnp.int32)


def _chip_index():
    return 2 * lax.axis_index("x") + lax.axis_index("y")


def _pair_sum(g, recv, *, name):
    n, A, B = g.shape
    ta = _tile(A // 2, ROW_TILE)
    nb = A // 2 // ta

    def body(c_ref, g_ref, r_ref, o_ref):
        o_ref[...] = g_ref[...] + r_ref[...]

    return pl.pallas_call(
        body, name=name,
        grid_spec=pltpu.PrefetchScalarGridSpec(
            num_scalar_prefetch=1, grid=(n, nb),
            in_specs=[pl.BlockSpec((None, ta, B), lambda s, r, c_ref: (s, c_ref[0] * nb + r, 0)),
                      pl.BlockSpec((None, ta, B), lambda s, r, c_ref: (s, r, 0))],
            out_specs=pl.BlockSpec((None, ta, B), lambda s, r, c_ref: (s, r, 0))),
        out_shape=jax.ShapeDtypeStruct((n, A // 2, B), F32),
        compiler_params=_params("parallel", "parallel"),
    )(_scalar(lax.axis_index("c")), g, recv)


def _chip_sum(landed, own, *, name):
    n, A2, B = landed.shape
    ta = _tile(A2, ROW_TILE)

    def body(me_ref, *refs):
        slots, own_ref, o_ref = refs[:n], refs[n], refs[n + 1]
        parts = [jnp.where(me_ref[0] == s, own_ref[...], slots[s][...]) for s in range(n)]
        o_ref[...] = ((parts[0] + parts[1]) + parts[2]) + parts[3]

    def slot(s):
        return pl.BlockSpec((None, ta, B), lambda r, me_ref: (jnp.where(me_ref[0] == s, (s + 1) % n, s), r, 0))

    return pl.pallas_call(
        body, name=name,
        grid_spec=pltpu.PrefetchScalarGridSpec(
            num_scalar_prefetch=1, grid=(A2 // ta,),
            in_specs=[slot(s) for s in range(n)] + [pl.BlockSpec((None, ta, B), lambda r, me_ref: (me_ref[0], r, 0))],
            out_specs=pl.BlockSpec((ta, B), lambda r, me_ref: (r, 0))),
        out_shape=jax.ShapeDtypeStruct((A2, B), F32),
        compiler_params=_params("parallel"),
    )(_scalar(_chip_index()), *([landed] * n), own)


HBM_SPEC = pl.BlockSpec(memory_space=pl.ANY)
COMM_PARAMS = pltpu.CompilerParams(has_side_effects=True)


def _mesh_pos():
    return lax.axis_index("x"), lax.axis_index("y"), lax.axis_index("c")


def _other_chips(x, y):
    return [(1 - x, y), (x, 1 - y), (1 - x, 1 - y)]


def _my_layers(c):
    return pl.ds(HALF_DEPTH * c, HALF_DEPTH)


def _remote(src, dst, send_sems, recv_sems, k, to):
    return pltpu.make_async_remote_copy(src_ref=src, dst_ref=dst, send_sem=send_sems.at[k], recv_sem=recv_sems.at[k],
                                        device_id=to, device_id_type=MESH)


def _gather_shards(shards):
    n = len(shards)

    def body(*refs):
        ins, outs = refs[:n], refs[n:2 * n]
        send_sems, recv_sems, local_sems = refs[2 * n:]
        x, y, c = _mesh_pos()
        me = 2 * x + y
        chips = _other_chips(x, y)
        sibling = (x, y, 1 - c)
        mine, other = _my_layers(c), _my_layers(1 - c)
        local = [pltpu.make_async_copy(ins[k], outs[k].at[:, me], local_sems.at[k]) for k in range(n)]
        for cp in local:
            cp.start()
        first = [_remote(ins[k].at[mine], outs[k].at[mine, me], send_sems, recv_sems, 6 * k + j, (px, py, c))
                 for j, (px, py) in enumerate(chips) for k in range(n)]
        for cp in first:
            cp.start()
        passed = []
        for j, (px, py) in enumerate(chips):
            for k in range(n):
                landed = outs[k].at[mine, 2 * px + py]
                _remote(landed, landed, send_sems, recv_sems, 6 * k + j, (px, py, c)).wait_recv()
                cp = _remote(landed, landed, send_sems, recv_sems, 6 * k + 3 + j, sibling)
                cp.start()
                passed.append(cp)
        for j, (px, py) in enumerate(chips):
            for k in range(n):
                landed = outs[k].at[other, 2 * px + py]
                _remote(landed, landed, send_sems, recv_sems, 6 * k + 3 + j, sibling).wait_recv()
        for cp in first + passed:
            cp.wait_send()
        for cp in local:
            cp.wait()

    return pl.pallas_call(
        body, name="gather_weight_shards",
        in_specs=[HBM_SPEC] * n, out_specs=[HBM_SPEC] * n,
        out_shape=[jax.ShapeDtypeStruct((s.shape[0], N_CHIPS) + s.shape[1:], s.dtype) for s in shards],
        scratch_shapes=[pltpu.SemaphoreType.DMA((6 * n,)), pltpu.SemaphoreType.DMA((6 * n,)), pltpu.SemaphoreType.DMA((n,))],
        compiler_params=COMM_PARAMS,
    )(*shards)


def _half_rows(rows, c):
    return pl.ds(pl.multiple_of(c * (rows // 2), 8), rows // 2)


def _sibling_exchange(gs, *, name):
    n = len(gs)

    def body(*refs):
        ins, outs = refs[:n], refs[n:2 * n]
        send_sems, recv_sems = refs[2 * n:]
        x, y, c = _mesh_pos()
        copies = [_remote(ins[k].at[:, _half_rows(gs[k].shape[1], 1 - c)], outs[k], send_sems, recv_sems, k, (x, y, 1 - c))
                  for k in range(n)]
        for cp in copies:
            cp.start()
        for cp in copies:
            cp.wait()

    return pl.pallas_call(
        body, name=name,
        in_specs=[HBM_SPEC] * n, out_specs=[HBM_SPEC] * n,
        out_shape=[jax.ShapeDtypeStruct((g.shape[0], g.shape[1] // 2, g.shape[2]), g.dtype) for g in gs],
        scratch_shapes=[pltpu.SemaphoreType.DMA((n,)), pltpu.SemaphoreType.DMA((n,))],
        compiler_params=COMM_PARAMS,
    )(*gs)


HBM_ONLY = pl.BlockSpec(memory_space=pltpu.HBM)
SEM_SPEC = pl.BlockSpec(memory_space=pltpu.SEMAPHORE)
SPLIT_PARAMS = pltpu.CompilerParams(has_side_effects=pltpu.SideEffectType.DATAFLOW_SIDE_EFFECTING)


def _scatter_copies(srcs, lands, send_sems, recv_sems):
    x, y, c = _mesh_pos()
    me = 2 * x + y
    out = []
    for k in range(len(srcs)):
        for j, (px, py) in enumerate(_other_chips(x, y)):
            s = 2 * px + py
            send = _remote(srcs[k].at[s], lands[k].at[me], send_sems, recv_sems, 3 * k + j, (px, py, c))
            recv = _remote(srcs[k].at[s], lands[k].at[s], send_sems, recv_sems, 3 * k + j, (px, py, c))
            out.append((send, recv))
    return out


def _scatter_start(ps, *, name):
    n = len(ps)
    lands = [lax.empty(p.shape, p.dtype) for p in ps]

    def body(*refs):
        srcs, zones = refs[:n], refs[n:2 * n]
        send_sems, recv_sems = refs[2 * n], refs[2 * n + 1]
        for send, _ in _scatter_copies(srcs, zones, send_sems, recv_sems):
            send.start()

    hbm = lambda a: pltpu.HBM(a.shape, a.dtype)
    res = pl.pallas_call(
        body, name=name,
        in_specs=[HBM_ONLY] * (2 * n),
        out_specs=[SEM_SPEC, SEM_SPEC] + [HBM_ONLY] * (2 * n),
        out_shape=[pltpu.SemaphoreType.DMA((3 * n,)), pltpu.SemaphoreType.DMA((3 * n,))] + [hbm(a) for a in ps + lands],
        input_output_aliases={i: 2 + i for i in range(2 * n)},
        compiler_params=SPLIT_PARAMS,
    )(*[pltpu.with_memory_space_constraint(a, pltpu.HBM) for a in ps + lands])
    return res[0], res[1], list(res[2:2 + n]), list(res[2 + n:])


def _scatter_wait(started, after, *, name):
    nl = len(started)
    n = len(started[0][2])
    flat = [a for (_, _, ps, lands) in started for a in ps + lands]

    def body(*refs):
        bufs = refs[:2 * n * nl]
        sems = refs[2 * n * nl:2 * n * nl + 2 * nl]
        for l in range(nl):
            srcs = bufs[2 * n * l:2 * n * l + n]
            zones = bufs[2 * n * l + n:2 * n * (l + 1)]
            for send, recv in _scatter_copies(srcs, zones, sems[2 * l], sems[2 * l + 1]):
                send.wait_send()
                recv.wait_recv()

    res = pl.pallas_call(
        body, name=name,
        in_specs=[HBM_ONLY] * len(flat) + [SEM_SPEC] * (2 * nl) + [HBM_SPEC],
        out_specs=[HBM_ONLY] * len(flat),
        out_shape=[pltpu.HBM(a.shape, a.dtype) for a in flat],
        input_output_aliases={i: i for i in range(len(flat))},
        compiler_params=SPLIT_PARAMS,
    )(*flat, *[s for (ss, rs, _, _) in started for s in (ss, rs)], after)
    return [(list(res[2 * n * l:2 * n * l + n]), list(res[2 * n * l + n:2 * n * (l + 1)])) for l in range(nl)]


def _sibling_share(hs):
    n, nl = len(hs), len(hs[0])

    def body(*refs):
        ins, outs = refs[:n * nl], refs[n * nl:n * nl + n]
        send_sems, recv_sems, local_sems = refs[n * nl + n:]
        x, y, c = _mesh_pos()
        local, copies, waits = [], [], []
        for k in range(n):
            rows = 2 * hs[k][0].shape[0]
            for l in range(nl):
                i = k * nl + l
                dst = outs[k].at[l, _half_rows(rows, c)]
                local.append(pltpu.make_async_copy(ins[i], dst, local_sems.at[i]))
                copies.append(_remote(ins[i], dst, send_sems, recv_sems, i, (x, y, 1 - c)))
                landed = outs[k].at[l, _half_rows(rows, 1 - c)]
                waits.append(_remote(landed, landed, send_sems, recv_sems, i, (x, y, 1 - c)))
        for cp in local + copies:
            cp.start()
        for cp in waits:
            cp.wait_recv()
        for cp in copies:
            cp.wait_send()
        for cp in local:
            cp.wait()

    return pl.pallas_call(
        body, name="grad_sibling_share",
        in_specs=[HBM_SPEC] * (n * nl), out_specs=[HBM_SPEC] * n,
        out_shape=[jax.ShapeDtypeStruct((nl, 2 * h[0].shape[0], h[0].shape[1]), h[0].dtype) for h in hs],
        scratch_shapes=[pltpu.SemaphoreType.DMA((n * nl,))] * 3,
        compiler_params=COMM_PARAMS,
    )(*[h for hk in hs for h in hk])


def _allreduce_small(part):
    rows, C = part.shape

    def body(p_ref, o_ref, slots, send_sems, recv_sems):
        x, y, c = _mesh_pos()
        me = 4 * x + 2 * y + c
        slots[me] = p_ref[...]
        copies = []
        for k in range(1, 8):
            kx, ky, kc = (k >> 2) & 1, (k >> 1) & 1, k & 1
            peer = (x ^ kx if kx else x, y ^ ky if ky else y, c ^ kc if kc else c)
            cp = _remote(p_ref, slots.at[me], send_sems, recv_sems, k - 1, peer)
            cp.start()
            copies.append((cp, peer))
        for k, (cp, peer) in enumerate(copies):
            src = 4 * peer[0] + 2 * peer[1] + peer[2]
            _remote(p_ref, slots.at[src], send_sems, recv_sems, k, peer).wait_recv()
        for cp, _ in copies:
            cp.wait_send()
        total = slots[0]
        for d in range(1, 8):
            total = total + slots[d]
        o_ref[...] = total

    return pl.pallas_call(
        body, name="small_grad_allreduce",
        in_specs=[pl.BlockSpec(memory_space=pltpu.VMEM)], out_specs=pl.BlockSpec(memory_space=pltpu.VMEM),
        out_shape=jax.ShapeDtypeStruct((rows, C), F32),
        scratch_shapes=[pltpu.VMEM((8, rows, C), F32), pltpu.SemaphoreType.DMA((7,)), pltpu.SemaphoreType.DMA((7,))],
        compiler_params=pltpu.CompilerParams(has_side_effects=True, vmem_limit_bytes=VMEM_LIMIT_BYTES),
    )(part)


def _pad_w_uq(w):
    lead = w.shape[:-1]
    w = w.reshape(lead + (MLA_HEADS, MLA_QK))
    w = jnp.concatenate([w, jnp.zeros(lead + (MLA_HEADS, MLA_PAD - MLA_QK), w.dtype)], axis=-1)
    return w.reshape(lead + (MLA_HEADS * MLA_PAD,))


def _unpad_w_uq(g):
    lead = g.shape[:-1]
    return g.reshape(lead + (MLA_HEADS, MLA_PAD))[..., :MLA_QK].reshape(lead + (MLA_HEADS * MLA_QK,))


def _t(a):
    return jnp.swapaxes(a, -1, -2)


def _cols_of_shards(g):
    L, n, A, B = g.shape
    return g.transpose(0, 2, 1, 3).reshape(L, A, n * B)


def _shards_of_cols(w):
    A, NB = w.shape
    return w.reshape(A, N_CHIPS, NB // N_CHIPS).transpose(1, 0, 2)


BIG = ("w_in", "w_uq", "w_ukv", "w_out", "w_up", "w_down")
SMALL = ("attn_pre_norm", "forget_bias", "swa_sinks", "rel_bias", "q_latent_norm", "kv_latent_norm", "group_norm",
         "attn_post_norm", "ffn_pre_norm", "conv_b", "ffn_post_norm")
WEIGHTS = ("attn_pre_norm", "w_in", "forget_bias", "swa_sinks", "rel_bias", "q_latent_norm", "w_uq", "kv_latent_norm",
           "w_ukv", "group_norm", "w_out", "attn_post_norm", "ffn_pre_norm", "w_up", "conv_w", "conv_b", "w_down",
           "ffn_post_norm")


def _pack(arrs, cols, row_mult):
    flat = jnp.concatenate([a.reshape(-1) for a in arrs])
    n = flat.shape[0]
    per = cols * row_mult
    total = -(-n // per) * per
    return jnp.pad(flat, (0, total - n)).reshape(total // cols, cols)


def _unpack(packed, shapes):
    flat = packed.reshape(-1)
    out, off = [], 0
    for shp in shapes:
        n = int(np.prod(shp))
        out.append(flat[off:off + n].reshape(shp))
        off += n
    return out


def _kernel_weights(gathered, small):
    L = gathered["w_in"].shape[0]
    w_in_t = _t(gathered["w_in"]).reshape(L, IN_COLS, D_MODEL)
    w_in_t = jnp.pad(w_in_t, ((0, 0), (0, IN_ROWS - IN_COLS), (0, 0)))
    w_uq_p = _pad_w_uq(_cols_of_shards(gathered["w_uq"]))
    w_ukv = _cols_of_shards(gathered["w_ukv"])
    W = dict(small)
    W.update(w_qkv_t=w_in_t[:, :QKV_ROWS], w_lat_t=w_in_t[:, QKV_ROWS:], w_in_t=w_in_t, w_uq_p=w_uq_p, w_uq_t=_t(w_uq_p),
             w_ukv=w_ukv, w_ukv_t=_t(w_ukv), w_out=gathered["w_out"].reshape(L, D_MODEL, D_MODEL), w_up=gathered["w_up"],
             conv_w=_cols_of_shards(gathered["conv_w"]), w_down=gathered["w_down"].reshape(L, D_FF, D_MODEL))
    return W


def _local_step(x, target, W, layer_done):
    S = x.shape[0]
    tq_tabs, tm_tabs = _rope_tables(S)
    onehot_t = _rel_onehot_t()
    bias_t = _bias_table(W["rel_bias"].T, onehot_t).reshape(SWA_Q_HEADS, 2 * WINDOW, WINDOW)
    row = lambda a: a.reshape(1, -1)
    col = lambda a: a.reshape(-1, 1)
    fox_rows = (FOX_ROW0, FOX_ROW0 + FOX_HEADS * HEAD_DIM, FOX_ROW0 + 2 * FOX_HEADS * HEAD_DIM, SWA_Q_HEADS)
    fox = dict(rows=fox_rows, H=FOX_HEADS, Dk=HEAD_DIM, Dv=HEAD_DIM, scale=HEAD_DIM ** -0.5)
    mla = dict(rows=(0, 0, 0, SWA_Q_HEADS + FOX_HEADS), H=MLA_HEADS, Dk=MLA_PAD, Dv=HEAD_DIM, scale=MLA_QK ** -0.5)

    saved = []
    h = _rms_fwd(x, row(W["attn_pre_norm"][0]), name="rms_in")
    for l in range(DEPTH):
        sv = {"x0": x, "h1": h}
        qkv = _matmul(W["w_qkv_t"][l], h, tb=True, out_dtype=BF16, name="proj_qkv")
        lat = _matmul(W["w_lat_t"][l], h, tb=True, name="proj_lat")
        oa, lse_a = _swa_fwd(qkv, bias_t, W["swa_sinks"][l], name="swa_fwd")
        fb_col = jnp.pad(col(W["forget_bias"][l]), ((0, GATE_ROWS - FOX_HEADS), (0, 0)))
        f4 = _gate_fwd(lat, fb_col, name="fox_gate_fwd")[:FOX_HEADS]
        f_row, f_col = f4[:, None, :], f4.T
        of, lse_f = _attn_fwd(qkv, qkv, qkv, f_row=f_row, f_col=f_col, name="fox_fwd", **fox)
        nq, nkv, qm, km, vm = _mla_prep_fwd(lat, col(W["q_latent_norm"][l]), col(W["kv_latent_norm"][l]), W["w_uq_t"][l],
                                            W["w_ukv_t"][l], tq_tabs, tm_tabs, name="mla_prep_fwd")
        oc, lse_c = _attn_fwd(qm, km, vm, name="mla_fwd", **mla)
        mixed = _group_norm_fwd(oa, of, oc, col(W["group_norm"][l]), name="group_norm_fwd")
        y = _matmul(mixed, W["w_out"][l], ta=True, name="proj_out")
        x1, h2 = _resid_rms(x, y, row(W["attn_post_norm"][l]), row(W["ffn_pre_norm"][l]), name="attn_resid")
        a = _matmul(h2, W["w_up"][l], b_shards=True, name="ffn_up")
        z = _conv_geglu_fwd(a, W["conv_w"][l], row(W["conv_b"][l]), name="conv_geglu_fwd")
        y2 = _matmul(z, W["w_down"][l], name="ffn_down")
        g_next = row(W["attn_pre_norm"][l + 1]) if l + 1 < DEPTH else None
        x2, h_next = _resid_rms(x1, y2, row(W["ffn_post_norm"][l]), g_next, name="ffn_resid")
        sv.update(qkv=qkv, lat=lat, oa=oa, lse_a=lse_a, fb_col=fb_col, f_row=f_row, f_col=f_col, of=of, lse_f=lse_f,
                  nq=nq, nkv=nkv, qm=qm, km=km, vm=vm, oc=oc, lse_c=lse_c, mixed=mixed, y=y, x1=x1, h2=h2, a=a, z=z, y2=y2)
        saved.append(sv)
        x, h = x2, h_next

    loss, dx = _loss_head(x, target)

    G = {k: [None] * DEPTH for k in WEIGHTS if k != "rel_bias" and k not in BIG}
    dbias_layers = [None] * DEPTH
    for l in reversed(range(DEPTH)):
        sv = saved[l]
        gb = {}
        dy2, dg = _rms_bwd(sv["y2"], row(W["ffn_post_norm"][l]), dx, out_dtype=BF16, name="ffn_post_bwd")
        G["ffn_post_norm"][l] = dg[0]
        dz = _matmul(dy2, W["w_down"][l], tb=True, name="ffn_down_dx")
        gb["w_down"] = _matmul(sv["z"], dy2, ta=True, name="ffn_down_dw").reshape(N_CHIPS, D_FF // N_CHIPS, D_MODEL)
        du, dcw, dcb = _conv_geglu_bwd(sv["a"], W["conv_w"][l], row(W["conv_b"][l]), dz, name="conv_geglu_bwd")
        G["conv_w"][l] = dcw.transpose(1, 0, 2).reshape(3, 2 * D_FF)
        G["conv_b"][l] = dcb.reshape(2 * D_FF)
        da = _conv_bwd_input(du, W["conv_w"][l], name="conv_bwd_input")
        dh2 = _matmul(da, W["w_up"][l], tb=True, b_shards=True, name="ffn_up_dx")
        gb["w_up"] = _matmul(sv["h2"], da, ta=True, out_shards=True, name="ffn_up_dw")
        dx1, dg = _rms_bwd(sv["x1"], row(W["ffn_pre_norm"][l]), dh2, resid=dx, out_dtype=F32, name="ffn_pre_bwd")
        G["ffn_pre_norm"][l] = dg[0]
        dy, dg = _rms_bwd(sv["y"], row(W["attn_post_norm"][l]), dx1, out_dtype=BF16, name="attn_post_bwd")
        G["attn_post_norm"][l] = dg[0]
        dmixed = _matmul(W["w_out"][l], dy, tb=True, name="proj_out_dx")
        gb["w_out"] = _matmul(sv["mixed"], dy, name="proj_out_dw").reshape(N_CHIPS, D_MODEL // N_CHIPS, D_MODEL)
        doa, dof, doc, dg, delta = _group_norm_bwd(sv["oa"], sv["of"], sv["oc"], col(W["group_norm"][l]), dmixed,
                                                   name="group_norm_bwd")
        G["group_norm"][l] = dg[:, 0]
        dqa, dkva, dbias_l, dsink = _swa_bwd(sv["qkv"], bias_t, W["swa_sinks"][l], doa, sv["lse_a"],
                                             delta.reshape(-1, S), name="swa_bwd")
        dbias_layers[l] = dbias_l.reshape(SWA_Q_HEADS, -1)
        G["swa_sinks"][l] = dsink[:, 0]
        dqf, dkf, dvf, dfk = _attn_bwd(sv["qkv"], sv["qkv"], sv["qkv"], do=dof, lse=sv["lse_f"], delta=delta,
                                       f_row=sv["f_row"], f_col=sv["f_col"], name="fox_bwd", **fox)
        dF = jnp.pad(dfk.T, ((0, GATE_ROWS - FOX_HEADS), (0, 0)))
        dflog, dfb = _gate_bwd(sv["lat"], sv["fb_col"], dF, name="fox_gate_bwd")
        G["forget_bias"][l] = dfb[:FOX_HEADS, 0]
        dqm, dkm, dvm = _attn_bwd(sv["qm"], sv["km"], sv["vm"], do=doc, lse=sv["lse_c"], delta=delta, name="mla_bwd", **mla)
        dlat, dwq_t, dwkv_t, dgq, dgkv = _mla_prep_bwd(
            sv["lat"], sv["nq"], sv["nkv"], col(W["q_latent_norm"][l]), col(W["kv_latent_norm"][l]), W["w_uq_p"][l],
            W["w_ukv"][l], tq_tabs, tm_tabs, dqm, dkm, dvm, dflog, name="mla_prep_bwd")
        gb["w_uq"], gb["w_ukv"] = _shards_of_cols(_unpad_w_uq(dwq_t.T)), _shards_of_cols(dwkv_t.T)
        G["q_latent_norm"][l], G["kv_latent_norm"][l] = dgq[:, 0], dgkv[:, 0]
        dproj = _dproj_cast(dqa, dkva, dqf, dkf, dvf, dlat, name="dproj_cast")
        dh1 = _matmul(dproj, W["w_in_t"][l], ta=True, name="proj_in_dx")
        dw_in_t = _matmul(dproj, sv["h1"], name="proj_in_dw")
        gb["w_in"] = _t(dw_in_t[:IN_COLS].reshape(N_CHIPS, IN_COLS // N_CHIPS, D_MODEL))
        layer_done(l, gb)
        dx, dg = _rms_bwd(sv["x0"], row(W["attn_pre_norm"][l]), dh1, resid=dx1, out_dtype=F32, name="attn_pre_bwd")
        G["attn_pre_norm"][l] = dg[0]

    grads = {k: jnp.stack(v) for k, v in G.items()}
    grads["rel_bias"] = _bias_table_bwd(jnp.stack(dbias_layers), onehot_t).T
    return loss, dx, grads


def kernel(x, attn_pre_norm, w_in, forget_bias, swa_sinks, rel_bias, q_latent_norm, w_uq, kv_latent_norm, w_ukv, group_norm, w_out, attn_post_norm, ffn_pre_norm, w_up, conv_w, conv_b, w_down, ffn_post_norm, loss_target, m_attn_pre_norm, m_w_in, m_forget_bias, m_swa_sinks, m_rel_bias, m_q_latent_norm, m_w_uq, m_kv_latent_norm, m_w_ukv, m_group_norm, m_w_out, m_attn_post_norm, m_ffn_pre_norm, m_w_up, m_conv_w, m_conv_b, m_w_down, m_ffn_post_norm, v_attn_pre_norm, v_w_in, v_forget_bias, v_swa_sinks, v_rel_bias, v_q_latent_norm, v_w_uq, v_kv_latent_norm, v_w_ukv, v_group_norm, v_w_out, v_attn_post_norm, v_ffn_pre_norm, v_w_up, v_conv_w, v_conv_b, v_w_down, v_ffn_post_norm):
    args = dict(locals())
    w = {k: args[k] for k in WEIGHTS}
    m = {k: args["m_" + k] for k in WEIGHTS}
    v = {k: args["v_" + k] for k in WEIGHTS}

    sent = BIG + ("conv_w",)
    gathered = dict(zip(sent, _gather_shards([w[k] if k == "conv_w" else w[k].astype(BF16) for k in sent])))
    W = _kernel_weights(gathered, {k: w[k] for k in SMALL})

    started = [None] * DEPTH

    def layer_done(l, gb):
        gs = [gb[k] for k in BIG]
        recv = _sibling_exchange(gs, name=f"grad_sibling_exchange_{l}")
        pair = [_pair_sum(gk, rk, name="grad_pair_sum") for gk, rk in zip(gs, recv)]
        started[l] = _scatter_start(pair, name=f"grad_scatter_start_{l}")

    loss_part, dx, g = _local_step(x[0], loss_target[0], W, layer_done)
    loss = lax.psum(loss_part, ("x", "y", "c"))

    landed = _scatter_wait(started, dx, name="grad_scatter_wait")
    mine = [[_chip_sum(zones[i], pair[i], name="grad_chip_sum") for pair, zones in landed] for i in range(len(BIG))]
    out_g = dict(zip(BIG, _sibling_share(mine)))
    out_d, out_m, out_v = {}, {}, {}
    for k in BIG:
        out_d[k], out_m[k], out_v[k] = _adamw(w[k], out_g[k], m[k], v[k], name="adamw_" + k)

    small_shapes = [w[k].shape for k in SMALL]
    reduced = _allreduce_small(_pack([g[k] for k in SMALL] + [g["conv_w"]], LANES, 8))
    *g_small, g_cw = _unpack(reduced, small_shapes + [g["conv_w"].shape])
    chip = 2 * lax.axis_index("x") + lax.axis_index("y")
    g_small.append(lax.dynamic_slice_in_dim(g_cw, chip * FF_SHARD, FF_SHARD, axis=2))
    names = SMALL + ("conv_w",)
    shapes = small_shapes + [w["conv_w"].shape]
    packed = lambda arrs: _pack(arrs, LANES, ROW_TILE)[None]
    d_s, m_s, v_s = _adamw(packed([w[k] for k in names]), packed(g_small), packed([m[k] for k in names]),
                           packed([v[k] for k in names]), name="adamw_small")
    out_g.update(zip(names, g_small))
    out_d.update(zip(names, _unpack(d_s, shapes)))
    out_m.update(zip(names, _unpack(m_s, shapes)))
    out_v.update(zip(names, _unpack(v_s, shapes)))

    return (loss, dx[None], *[out_g[k] for k in WEIGHTS], *[out_d[k] for k in WEIGHTS],
            *[out_m[k] for k in WEIGHTS], *[out_v[k] for k in WEIGHTS])
```

```python
import math

import numpy as np
import jax
import jax.numpy as jnp
from jax import lax
from jax.experimental import pallas as pl
from jax.experimental.pallas import tpu as pltpu

F32 = jnp.float32
BF16 = jnp.bfloat16

D_MODEL = 1024
DEPTH = 4
HEAD_DIM = 64
SWA_Q_HEADS = 8
SWA_KV_HEADS = 2
SWA_GROUP = SWA_Q_HEADS // SWA_KV_HEADS
WINDOW = 128
FOX_HEADS = 4
MLA_HEADS = 4
MLA_Q_RANK = 256
MLA_KV_RANK = 128
MLA_NOPE = 64
MLA_ROPE = 32
MLA_QK = MLA_NOPE + MLA_ROPE
ROPE_THETA = 10000.0
REL_BUCKETS = 32
REL_MAX_DIST = 128
D_FF = 2816
EPS = 1e-6
NEG_INF = -1e30
LANES = 128
N_CHIPS = 4
HALF_DEPTH = DEPTH // 2

IN_COLS = 1956
IN_ROWS = 2048
QKV_ROWS = 1536
LAT_ROWS = IN_ROWS - QKV_ROWS
LAT_SHIFT = FOX_HEADS
FOX_ROW0 = 768
MLA_PAD = LANES
GATE_ROWS = 8

ADAM_LR = 0.001
ADAM_B1 = 0.9
ADAM_B2 = 0.999
ADAM_EPS = 1e-08
ADAM_WD = 0.01
ADAM_STEP = 10

VMEM_LIMIT_BYTES = 48 * 1024 * 1024
ATT_TILE = 256
ROW_TILE = 256
MESH = pl.DeviceIdType.MESH

NT = (((1,), (1,)), ((), ()))
TN = (((0,), (0,)), ((), ()))
NN = (((1,), (0,)), ((), ()))


def _params(*sem):
    return pltpu.CompilerParams(dimension_semantics=sem, vmem_limit_bytes=VMEM_LIMIT_BYTES)


def _tile(dim, cap):
    for t in (2048, 1408, 1024, 512, 256, 128, 64, 32, 16, 8):
        if t <= cap and dim % t == 0:
            return t
    return dim


def _dot(a, b, dims=NN):
    return lax.dot_general(a, b, dims, preferred_element_type=F32)


def _split3(a):
    a1 = a.astype(BF16)
    r1 = a - a1.astype(F32)
    a2 = r1.astype(BF16)
    a3 = (r1 - a2.astype(F32)).astype(BF16)
    return a1, a2, a3


FF_SHARD = 2 * D_FF // N_CHIPS


def _matmul(a, b, *, ta=False, tb=False, out_dtype=F32, name, b_shards=False, out_shards=False):
    if ta:
        K, M = a.shape
    else:
        M, K = a.shape
    if b_shards:
        K2, N = (2 * D_FF, D_MODEL) if tb else (D_MODEL, 2 * D_FF)
    elif tb:
        N, K2 = b.shape
    else:
        K2, N = b.shape
    assert K == K2, (a.shape, b.shape)
    tm, tn, tk = _tile(M, 1408), _tile(N, 1408), _tile(K, 1408)
    nk = K // tk
    dims = (((0 if ta else 1,), (1 if tb else 0,)), ((), ()))

    def body(a_ref, b_ref, o_ref, acc_ref):
        k = pl.program_id(2)

        @pl.when(k == 0)
        def _():
            acc_ref[...] = jnp.zeros_like(acc_ref)

        acc_ref[...] += lax.dot_general(a_ref[...], b_ref[...], dims, preferred_element_type=F32)

        @pl.when(k == nk - 1)
        def _():
            o_ref[...] = acc_ref[...].astype(o_ref.dtype)

    a_spec = pl.BlockSpec((tk, tm), lambda i, j, k: (k, i)) if ta else pl.BlockSpec((tm, tk), lambda i, j, k: (i, k))
    if b_shards and tb:
        assert tk == FF_SHARD
        b_spec = pl.BlockSpec((None, tn, tk), lambda i, j, k: (k, j, 0))
    elif b_shards:
        assert tn == FF_SHARD
        b_spec = pl.BlockSpec((None, tk, tn), lambda i, j, k: (j, k, 0))
    else:
        b_spec = pl.BlockSpec((tn, tk), lambda i, j, k: (j, k)) if tb else pl.BlockSpec((tk, tn), lambda i, j, k: (k, j))
    if out_shards:
        assert tn == FF_SHARD
        out_spec = pl.BlockSpec((None, tm, tn), lambda i, j, k: (j, i, 0))
        out_shape = jax.ShapeDtypeStruct((N // tn, M, tn), out_dtype)
    else:
        out_spec = pl.BlockSpec((tm, tn), lambda i, j, k: (i, j))
        out_shape = jax.ShapeDtypeStruct((M, N), out_dtype)
    return pl.pallas_call(
        body, name=name, grid=(M // tm, N // tn, nk),
        in_specs=[a_spec, b_spec], out_specs=out_spec, out_shape=out_shape,
        scratch_shapes=[pltpu.VMEM((tm, tn), F32)],
        compiler_params=_params("parallel", "parallel", "arbitrary"),
    )(a, b)


def _seg_rms(xs, g):
    r = lax.rsqrt(jnp.mean(xs * xs, axis=-1, keepdims=True) + EPS)
    return xs * r * g


def _seg_rms_bwd(xs, g, dy):
    r = lax.rsqrt(jnp.mean(xs * xs, axis=-1, keepdims=True) + EPS)
    gd = dy * g
    c = jnp.mean(gd * xs, axis=-1, keepdims=True)
    dx = r * gd - xs * (r * r * r * c)
    dg = jnp.sum(dy * (xs * r), axis=0, keepdims=True)
    return dx, dg


def _rms_fwd(x, g, *, name):
    S, W = x.shape
    tm = _tile(S, 512)

    def body(x_ref, g_ref, o_ref):
        o_ref[...] = _seg_rms(x_ref[...], g_ref[...]).astype(o_ref.dtype)

    return pl.pallas_call(
        body, name=name, grid=(S // tm,),
        in_specs=[pl.BlockSpec((tm, W), lambda i: (i, 0)), pl.BlockSpec((1, W), lambda i: (0, 0))],
        out_specs=pl.BlockSpec((tm, W), lambda i: (i, 0)),
        out_shape=jax.ShapeDtypeStruct((S, W), BF16),
        compiler_params=_params("parallel"),
    )(x, g)


def _rms_bwd(x, g, dy, *, resid=None, out_dtype, name):
    S, W = x.shape
    tm = _tile(S, 512)
    has_resid = resid is not None

    def body(*refs):
        if has_resid:
            x_ref, g_ref, dy_ref, r_ref, dx_ref, dg_ref = refs
        else:
            x_ref, g_ref, dy_ref, dx_ref, dg_ref = refs

        @pl.when(pl.program_id(0) == 0)
        def _():
            dg_ref[...] = jnp.zeros_like(dg_ref)

        dx, dg = _seg_rms_bwd(x_ref[...], g_ref[...], dy_ref[...])
        if has_resid:
            dx = dx + r_ref[...]
        dx_ref[...] = dx.astype(dx_ref.dtype)
        dg_ref[...] += dg

    row = pl.BlockSpec((tm, W), lambda i: (i, 0))
    vec = pl.BlockSpec((1, W), lambda i: (0, 0))
    ins = [x, g, dy] + ([resid] if has_resid else [])
    return pl.pallas_call(
        body, name=name, grid=(S // tm,),
        in_specs=[row, vec, row] + ([row] if has_resid else []),
        out_specs=[row, vec],
        out_shape=[jax.ShapeDtypeStruct((S, W), out_dtype), jax.ShapeDtypeStruct((1, W), F32)],
        compiler_params=_params("arbitrary"),
    )(*ins)


def _resid_rms(x, y, g_post, g_next, *, name):
    S, W = x.shape
    tm = _tile(S, 512)
    with_next = g_next is not None

    def body(*refs):
        if with_next:
            x_ref, y_ref, gp_ref, gn_ref, xo_ref, h_ref = refs
        else:
            x_ref, y_ref, gp_ref, xo_ref = refs
        xn = x_ref[...] + _seg_rms(y_ref[...], gp_ref[...])
        xo_ref[...] = xn
        if with_next:
            h_ref[...] = _seg_rms(xn, gn_ref[...]).astype(BF16)

    row = pl.BlockSpec((tm, W), lambda i: (i, 0))
    vec = pl.BlockSpec((1, W), lambda i: (0, 0))
    outs = [jax.ShapeDtypeStruct((S, W), F32)] + ([jax.ShapeDtypeStruct((S, W), BF16)] if with_next else [])
    res = pl.pallas_call(
        body, name=name, grid=(S // tm,),
        in_specs=[row, row, vec] + ([vec] if with_next else []),
        out_specs=[row] + ([row] if with_next else []),
        out_shape=outs,
        compiler_params=_params("parallel"),
    )(*([x, y, g_post] + ([g_next] if with_next else [])))
    return (res[0], res[1]) if with_next else (res[0], None)


def _col_rms(xs, g):
    r = lax.rsqrt(jnp.mean(xs * xs, axis=0, keepdims=True) + EPS)
    return xs * r * g


def _col_rms_bwd(xs, g, dy):
    r = lax.rsqrt(jnp.mean(xs * xs, axis=0, keepdims=True) + EPS)
    gd = dy * g
    c = jnp.mean(gd * xs, axis=0, keepdims=True)
    dx = r * gd - xs * (r * r * r * c)
    dg = jnp.sum(dy * (xs * r), axis=1, keepdims=True)
    return dx, dg


GROUP_ROWS = (SWA_Q_HEADS * HEAD_DIM, FOX_HEADS * HEAD_DIM, MLA_HEADS * HEAD_DIM)


def _group_specs(S, tn):
    outs = [pl.BlockSpec((n, tn), lambda i: (0, i)) for n in GROUP_ROWS]
    g = pl.BlockSpec((D_MODEL, 1), lambda i: (0, 0))
    mixed = pl.BlockSpec((D_MODEL, tn), lambda i: (0, i))
    return outs, g, mixed


def _group_norm_fwd(oa, of, oc, g, *, name):
    S = oa.shape[1]
    tn = _tile(S, 512)
    outs, gs, mixed = _group_specs(S, tn)

    def body(a_ref, f_ref, c_ref, g_ref, o_ref):
        r0 = 0
        for ref, n in zip((a_ref, f_ref, c_ref), GROUP_ROWS):
            o_ref[r0:r0 + n, :] = _col_rms(ref[...], g_ref[r0:r0 + n, :]).astype(BF16)
            r0 += n

    return pl.pallas_call(
        body, name=name, grid=(S // tn,),
        in_specs=outs + [gs], out_specs=mixed,
        out_shape=jax.ShapeDtypeStruct((D_MODEL, S), BF16),
        compiler_params=_params("parallel"),
    )(oa, of, oc, g)


def _group_norm_bwd(oa, of, oc, g, dmixed, *, name):
    S = oa.shape[1]
    tn = _tile(S, 512)
    outs, gs, mixed = _group_specs(S, tn)
    n_heads = D_MODEL // HEAD_DIM

    def body(a_ref, f_ref, c_ref, g_ref, dm_ref, da_ref, df_ref, dc_ref, dg_ref, dl_ref):
        @pl.when(pl.program_id(0) == 0)
        def _():
            dg_ref[...] = jnp.zeros_like(dg_ref)

        r0 = 0
        for ref, dref, n in zip((a_ref, f_ref, c_ref), (da_ref, df_ref, dc_ref), GROUP_ROWS):
            o = ref[...]
            dx, dg = _col_rms_bwd(o, g_ref[r0:r0 + n, :], dm_ref[r0:r0 + n, :])
            dxb = dx.astype(BF16)
            dref[...] = dxb
            dg_ref[r0:r0 + n, :] += dg
            od = o * dxb.astype(F32)
            for h in range(n // HEAD_DIM):
                dl_ref[r0 // HEAD_DIM + h] = jnp.sum(od[h * HEAD_DIM:(h + 1) * HEAD_DIM, :], axis=0, keepdims=True)
            r0 += n

    return pl.pallas_call(
        body, name=name, grid=(S // tn,),
        in_specs=outs + [gs, mixed], out_specs=outs + [gs, pl.BlockSpec((n_heads, 1, tn), lambda i: (0, 0, i))],
        out_shape=[jax.ShapeDtypeStruct((n, S), BF16) for n in GROUP_ROWS] + [jax.ShapeDtypeStruct((D_MODEL, 1), F32),
                                                                              jax.ShapeDtypeStruct((n_heads, 1, S), F32)],
        compiler_params=_params("arbitrary"),
    )(oa, of, oc, g, dmixed)


def _loss_head(y, target):
    S, W = y.shape
    tm = _tile(S, 512)

    def body(y_ref, t_ref, d_ref, l_ref):
        @pl.when(pl.program_id(0) == 0)
        def _():
            l_ref[...] = jnp.zeros_like(l_ref)

        err = y_ref[...] - t_ref[...]
        d_ref[...] = err * (1.0 / W)
        l_ref[...] += 0.5 * jnp.sum(jnp.mean(err * err, axis=-1, keepdims=True), axis=0, keepdims=True)

    row = pl.BlockSpec((tm, W), lambda i: (i, 0))
    d, l = pl.pallas_call(
        body, name="loss_head", grid=(S // tm,),
        in_specs=[row, row],
        out_specs=[row, pl.BlockSpec((1, 1), lambda i: (0, 0))],
        out_shape=[jax.ShapeDtypeStruct((S, W), F32), jax.ShapeDtypeStruct((1, 1), F32)],
        compiler_params=_params("arbitrary"),
    )(y, target)
    return l[0, 0], d


def _attn_fwd(q_src, k_src, v_src, rows, H, Dk, Dv, scale, f_row=None, f_col=None, *, name):
    S = q_src.shape[1]
    T = _tile(S, ATT_TILE)
    nq = S // T
    forget = f_row is not None
    qb, kb, vb = rows[0] // (H * Dk), rows[1] // (H * Dk), rows[2] // (H * Dv)
    hs = range(H)

    def body(*refs):
        if forget:
            q_ref, k_ref, v_ref, fq_ref, fk_ref, o_ref, lse_ref = refs
        else:
            q_ref, k_ref, v_ref, o_ref, lse_ref = refs
        i = pl.program_id(0)

        def tile(j, masked, state):
            off = pl.multiple_of(j * T, T)
            ss = [_dot(k_ref[h * Dk:(h + 1) * Dk, pl.ds(off, T)], q_ref[h * Dk:(h + 1) * Dk, :], TN) * scale for h in hs]
            if forget:
                ss = [ss[h] + (fq_ref[h] - fk_ref[pl.ds(off, T), h:h + 1]) for h in hs]
            if masked:
                r = lax.broadcasted_iota(jnp.int32, (T, T), 0)
                c = lax.broadcasted_iota(jnp.int32, (T, T), 1)
                ss = [jnp.where(r <= c, s, NEG_INF) for s in ss]
            m_new = [jnp.maximum(state[h][0], jnp.max(ss[h], axis=0, keepdims=True)) for h in hs]
            alpha = [jnp.exp(state[h][0] - m_new[h]) for h in hs]
            ps = [jnp.exp(ss[h] - m_new[h]) for h in hs]
            l_new = [alpha[h] * state[h][1] + jnp.sum(ps[h], axis=0, keepdims=True) for h in hs]
            p_hi = [p.astype(BF16) for p in ps]
            vs = [v_ref[h * Dv:(h + 1) * Dv, pl.ds(off, T)] for h in hs]
            pv = [_dot(vs[h], p_hi[h]) for h in hs]
            if forget:
                pv = [pv[h] + _dot(vs[h], (ps[h] - p_hi[h].astype(F32)).astype(BF16)) for h in hs]
            return tuple((m_new[h], l_new[h], alpha[h] * state[h][2] + pv[h]) for h in hs)

        init = tuple((jnp.full((1, T), NEG_INF, F32), jnp.zeros((1, T), F32), jnp.zeros((Dv, T), F32)) for _ in hs)
        state = lax.fori_loop(0, i, lambda j, st: tile(j, False, st), init)
        state = tile(i, True, state)
        for h in hs:
            m, l, acc = state[h]
            o_ref[h * Dv:(h + 1) * Dv, :] = acc / l
            lse_ref[h] = m + jnp.log(l)

    in_specs = [pl.BlockSpec((H * Dk, T), lambda i: (qb, i)),
                pl.BlockSpec((H * Dk, S), lambda i: (kb, 0)),
                pl.BlockSpec((H * Dv, S), lambda i: (vb, 0))]
    ins = [q_src, k_src, v_src]
    if forget:
        in_specs += [pl.BlockSpec((H, 1, T), lambda i: (0, 0, i)), pl.BlockSpec((S, H), lambda i: (0, 0))]
        ins += [f_row, f_col]
    return pl.pallas_call(
        body, name=name, grid=(nq,),
        in_specs=in_specs,
        out_specs=[pl.BlockSpec((H * Dv, T), lambda i: (0, i)), pl.BlockSpec((H, 1, T), lambda i: (0, 0, i))],
        out_shape=[jax.ShapeDtypeStruct((H * Dv, S), F32), jax.ShapeDtypeStruct((H, 1, S), F32)],
        compiler_params=_params("parallel"),
    )(*ins)


def _attn_bwd(q_src, k_src, v_src, rows, H, Dk, Dv, scale, do, lse, delta, f_row=None, f_col=None, *, name):
    S = q_src.shape[1]
    T = _tile(S, ATT_TILE)
    nq = S // T
    forget = f_row is not None
    qb, kb, vb, db = rows[0] // (H * Dk), rows[1] // (H * Dk), rows[2] // (H * Dv), rows[3] // H
    hs = range(H)

    def body(*refs):
        if forget:
            (q_ref, k_ref, v_ref, do_ref, lse_ref, dl_ref, fq_ref, fk_ref,
             dq_ref, dk_ref, dv_ref, df_ref, dk_s, dv_s, df_s) = refs
        else:
            q_ref, k_ref, v_ref, do_ref, lse_ref, dl_ref, dq_ref, dk_ref, dv_ref, dk_s, dv_s = refs
        j = pl.program_id(0)

        @pl.when(j == 0)
        def _():
            dq_ref[...] = jnp.zeros_like(dq_ref)

        dk_s[...] = jnp.zeros_like(dk_s)
        dv_s[...] = jnp.zeros_like(dv_s)
        if forget:
            df_s[...] = jnp.zeros_like(df_s)
        kt = [k_ref[h * Dk:(h + 1) * Dk, :] for h in hs]
        kj = [k.T for k in kt]
        vj = [v_ref[h * Dv:(h + 1) * Dv, :].T for h in hs]
        koff = pl.multiple_of(j * T, T)

        def tile(i, masked):
            cols = pl.ds(pl.multiple_of(i * T, T), T)
            qi = [q_ref[h * Dk:(h + 1) * Dk, cols] for h in hs]
            doi = [do_ref[h * Dv:(h + 1) * Dv, cols] for h in hs]
            st = [_dot(kj[h], qi[h]) * scale for h in hs]
            if forget:
                st = [st[h] + (fq_ref[h, :, cols] - fk_ref[pl.ds(koff, T), h:h + 1]) for h in hs]
            if masked:
                r = lax.broadcasted_iota(jnp.int32, (T, T), 0)
                c = lax.broadcasted_iota(jnp.int32, (T, T), 1)
                st = [jnp.where(r <= c, x, NEG_INF) for x in st]
            pt = [jnp.exp(st[h] - lse_ref[h, :, cols]) for h in hs]
            dpt = [_dot(vj[h], doi[h]) for h in hs]
            dst = [pt[h] * (dpt[h] - dl_ref[h, :, cols]) for h in hs]
            ptb = [p.astype(BF16) for p in pt]
            dsb = [d.astype(BF16) for d in dst]
            for h in hs:
                dv_s[h * Dv:(h + 1) * Dv, :] += _dot(doi[h], ptb[h], NT)
            for h in hs:
                dk_s[h * Dk:(h + 1) * Dk, :] += _dot(qi[h], dsb[h], NT)
            for h in hs:
                dq_ref[h * Dk:(h + 1) * Dk, cols] += _dot(kt[h], dsb[h]) * scale
            if forget:
                for h in hs:
                    part = dst[h][:, 0:LANES]
                    for c0 in range(LANES, T, LANES):
                        part = part + dst[h][:, c0:c0 + LANES]
                    df_s[h] += part

        tile(j, True)

        def loop_body(i, carry):
            tile(i, False)
            return carry

        lax.fori_loop(j + 1, nq, loop_body, 0)
        dk_ref[...] = dk_s[...] * scale
        dv_ref[...] = dv_s[...]
        if forget:
            df_ref[...] = jnp.concatenate([-jnp.sum(df_s[h], axis=-1, keepdims=True) for h in hs], axis=1)

    res = lambda D, b0: pl.BlockSpec((H * D, S), lambda j: (b0, 0))
    blk = lambda D, b0: pl.BlockSpec((H * D, T), lambda j: (b0, j))
    row3 = lambda b0: pl.BlockSpec((H, 1, S), lambda j: (b0, 0, 0))
    in_specs = [res(Dk, qb), blk(Dk, kb), blk(Dv, vb), res(Dv, 0), row3(0), row3(db)]
    ins = [q_src, k_src, v_src, do, lse, delta]
    out_specs = [res(Dk, 0), blk(Dk, 0), blk(Dv, 0)]
    out_shape = [jax.ShapeDtypeStruct((H * Dk, S), F32), jax.ShapeDtypeStruct((H * Dk, S), F32),
                 jax.ShapeDtypeStruct((H * Dv, S), F32)]
    scratch = [pltpu.VMEM((H * Dk, T), F32), pltpu.VMEM((H * Dv, T), F32)]
    if forget:
        in_specs += [row3(0), pl.BlockSpec((S, H), lambda j: (0, 0))]
        ins += [f_row, f_col]
        out_specs.append(pl.BlockSpec((T, H), lambda j: (j, 0)))
        out_shape.append(jax.ShapeDtypeStruct((S, H), F32))
        scratch.append(pltpu.VMEM((H, T, min(T, LANES)), F32))
    return pl.pallas_call(
        body, name=name, grid=(nq,),
        in_specs=in_specs, out_specs=out_specs, out_shape=out_shape, scratch_shapes=scratch,
        compiler_params=_params("arbitrary"),
    )(*ins)


def _swa_masks(i):
    r = lax.broadcasted_iota(jnp.int32, (WINDOW, WINDOW), 0)
    c = lax.broadcasted_iota(jnp.int32, (WINDOW, WINDOW), 1)
    return (r > c) & (i > 0), r <= c


def _swa_specs():
    W = WINDOW
    kv_rows = SWA_KV_HEADS * HEAD_DIM
    q = pl.BlockSpec((SWA_Q_HEADS * HEAD_DIM, W), lambda i: (0, i))
    prev = lambda b: pl.BlockSpec((kv_rows, W), lambda i: (b, jnp.maximum(i - 1, 0)))
    cur = lambda b: pl.BlockSpec((kv_rows, W), lambda i: (b, i))
    bias = pl.BlockSpec((SWA_Q_HEADS, 2 * W, W), lambda i: (0, 0, 0))
    stat = pl.BlockSpec((SWA_Q_HEADS, W), lambda i: (0, i))
    sink = pl.BlockSpec(memory_space=pltpu.SMEM)
    return q, prev(4), cur(4), prev(5), cur(5), bias, stat, sink


def _swa_scores(h, q_ref, kp_ref, kc_ref, b_ref, masks):
    g = h // SWA_GROUP
    rows = slice(g * HEAD_DIM, (g + 1) * HEAD_DIM)
    qh = q_ref[h * HEAD_DIM:(h + 1) * HEAD_DIM, :]
    scale = HEAD_DIM ** -0.5
    s_p = jnp.where(masks[0], _dot(kp_ref[rows, :], qh, TN) * scale + b_ref[h, 0:WINDOW, :], NEG_INF)
    s_c = jnp.where(masks[1], _dot(kc_ref[rows, :], qh, TN) * scale + b_ref[h, WINDOW:2 * WINDOW, :], NEG_INF)
    return qh, rows, s_p, s_c


def _swa_fwd(qkv, bias_t, sinks, *, name):
    S = qkv.shape[1]
    qs, kp, kc, vp, vc, bs, stat, sk = _swa_specs()

    def body(sink_ref, q_ref, kp_ref, kc_ref, vp_ref, vc_ref, b_ref, o_ref, lse_ref):
        masks = _swa_masks(pl.program_id(0))
        for h in range(SWA_Q_HEADS):
            qh, rows, s_p, s_c = _swa_scores(h, q_ref, kp_ref, kc_ref, b_ref, masks)
            sink = sink_ref[h]
            m = jnp.maximum(jnp.maximum(jnp.max(s_p, axis=0, keepdims=True), jnp.max(s_c, axis=0, keepdims=True)), sink)
            p_p = jnp.exp(s_p - m)
            p_c = jnp.exp(s_c - m)
            l = jnp.sum(p_p, axis=0, keepdims=True) + jnp.sum(p_c, axis=0, keepdims=True) + jnp.exp(sink - m)
            o = _dot(vp_ref[rows, :], p_p.astype(BF16)) + _dot(vc_ref[rows, :], p_c.astype(BF16))
            o_ref[h * HEAD_DIM:(h + 1) * HEAD_DIM, :] = o / l
            lse_ref[h:h + 1, :] = m + jnp.log(l)

    return pl.pallas_call(
        body, name=name, grid=(S // WINDOW,),
        in_specs=[sk, qs, kp, kc, vp, vc, bs],
        out_specs=[qs, stat],
        out_shape=[jax.ShapeDtypeStruct((SWA_Q_HEADS * HEAD_DIM, S), F32), jax.ShapeDtypeStruct((SWA_Q_HEADS, S), F32)],
        compiler_params=_params("parallel"),
    )(sinks, qkv, qkv, qkv, qkv, qkv, bias_t)


def _swa_bwd(qkv, bias_t, sinks, do, lse, delta, *, name):
    S = qkv.shape[1]
    W = WINDOW
    qs, kp, kc, vp, vc, bs, stat, sk = _swa_specs()
    scale = HEAD_DIM ** -0.5
    kv_rows = SWA_KV_HEADS * HEAD_DIM

    def body(sink_ref, q_ref, kp_ref, kc_ref, vp_ref, vc_ref, b_ref, do_ref, lse_ref, dl_ref,
             dq_ref, dkv_ref, db_ref, dsk_ref):
        i = pl.program_id(0)

        @pl.when(i == 0)
        def _():
            dkv_ref[...] = jnp.zeros_like(dkv_ref)
            db_ref[...] = jnp.zeros_like(db_ref)
            dsk_ref[...] = jnp.zeros_like(dsk_ref)

        masks = _swa_masks(i)
        prev = pl.ds(pl.multiple_of(jnp.maximum(i - 1, 0) * W, W), W)
        cur = pl.ds(pl.multiple_of(i * W, W), W)
        for h in range(SWA_Q_HEADS):
            qh, rows, s_p, s_c = _swa_scores(h, q_ref, kp_ref, kc_ref, b_ref, masks)
            vrows = slice(kv_rows + rows.start, kv_rows + rows.stop)
            hrows = slice(h * HEAD_DIM, (h + 1) * HEAD_DIM)
            doh = do_ref[hrows, :]
            lse_h = lse_ref[h:h + 1, :]
            delta = dl_ref[h:h + 1, :]
            p_p = jnp.exp(s_p - lse_h)
            p_c = jnp.exp(s_c - lse_h)
            ds_p = p_p * (_dot(vp_ref[rows, :], doh, TN) - delta)
            ds_c = p_c * (_dot(vc_ref[rows, :], doh, TN) - delta)
            db_ref[h, 0:W, :] += ds_p
            db_ref[h, W:2 * W, :] += ds_c
            dsk = -jnp.sum(jnp.exp(sink_ref[h] - lse_h) * delta, axis=1, keepdims=True)
            dsk_ref[h:h + 1, :] += jnp.broadcast_to(dsk, (1, LANES))
            dsb_p = ds_p.astype(BF16)
            dsb_c = ds_c.astype(BF16)
            dq_ref[hrows, :] = (_dot(kp_ref[rows, :], dsb_p) + _dot(kc_ref[rows, :], dsb_c)) * scale
            dkv_ref[rows, prev] += _dot(qh, dsb_p, NT) * scale
            dkv_ref[rows, cur] += _dot(qh, dsb_c, NT) * scale
            dkv_ref[vrows, prev] += _dot(doh, p_p.astype(BF16), NT)
            dkv_ref[vrows, cur] += _dot(doh, p_c.astype(BF16), NT)

    return pl.pallas_call(
        body, name=name, grid=(S // W,),
        in_specs=[sk, qs, kp, kc, vp, vc, bs, qs, stat, stat],
        out_specs=[qs, pl.BlockSpec((2 * kv_rows, S), lambda i: (0, 0)), bs, pl.BlockSpec((SWA_Q_HEADS, LANES), lambda i: (0, 0))],
        out_shape=[jax.ShapeDtypeStruct((SWA_Q_HEADS * HEAD_DIM, S), F32), jax.ShapeDtypeStruct((2 * kv_rows, S), F32),
                   jax.ShapeDtypeStruct((SWA_Q_HEADS, 2 * W, W), F32), jax.ShapeDtypeStruct((SWA_Q_HEADS, LANES), F32)],
        compiler_params=_params("arbitrary"),
    )(sinks, qkv, qkv, qkv, qkv, qkv, bias_t, do, lse, delta)


def _rel_onehot_t():
    qi = jnp.arange(WINDOW, dtype=jnp.int32)[None, :] + WINDOW
    kj = jnp.arange(2 * WINDOW, dtype=jnp.int32)[:, None]
    dist = qi - kj
    max_exact = REL_BUCKETS // 2
    d = jnp.maximum(dist, 0)
    log_ratio = jnp.log(jnp.maximum(d, 1).astype(F32) / max_exact) / math.log(REL_MAX_DIST / max_exact)
    large = jnp.minimum(max_exact + (log_ratio * (REL_BUCKETS - max_exact)).astype(jnp.int32), REL_BUCKETS - 1)
    bucket = jnp.where(d < max_exact, d, large).reshape(-1)
    return (bucket[None, :] == jnp.arange(REL_BUCKETS, dtype=jnp.int32)[:, None]).astype(BF16)


def _bias_table(rel_bias_t, onehot_t):
    Hq, NB = rel_bias_t.shape
    N = onehot_t.shape[1]
    tn = _tile(N, 4096)

    def body(r_ref, oh_ref, o_ref):
        oh = oh_ref[...]
        a1, a2, a3 = _split3(r_ref[...])
        o_ref[...] = _dot(a1, oh) + _dot(a2, oh) + _dot(a3, oh)

    return pl.pallas_call(
        body, name="rel_bias_table", grid=(N // tn,),
        in_specs=[pl.BlockSpec((Hq, NB), lambda j: (0, 0)), pl.BlockSpec((NB, tn), lambda j: (0, j))],
        out_specs=pl.BlockSpec((Hq, tn), lambda j: (0, j)),
        out_shape=jax.ShapeDtypeStruct((Hq, N), F32),
        compiler_params=_params("parallel"),
    )(rel_bias_t, onehot_t)


def _bias_table_bwd(dbias, onehot_t):
    L, Hq, N = dbias.shape
    NB = onehot_t.shape[0]
    tn = _tile(N, 4096)

    def body(d_ref, oh_ref, o_ref):
        @pl.when(pl.program_id(0) == 0)
        def _():
            o_ref[...] = jnp.zeros_like(o_ref)

        d = d_ref[0]
        for l in range(1, L):
            d = d + d_ref[l]
        oh = oh_ref[...]
        a1, a2, a3 = _split3(d)
        o_ref[...] += _dot(a1, oh, NT) + _dot(a2, oh, NT) + _dot(a3, oh, NT)

    return pl.pallas_call(
        body, name="rel_bias_bwd", grid=(N // tn,),
        in_specs=[pl.BlockSpec((L, Hq, tn), lambda j: (0, 0, j)), pl.BlockSpec((NB, tn), lambda j: (0, j))],
        out_specs=pl.BlockSpec((Hq, NB), lambda j: (0, 0)),
        out_shape=jax.ShapeDtypeStruct((Hq, NB), F32),
        compiler_params=_params("arbitrary"),
    )(dbias, onehot_t)


def _gate_fwd(lat, fb_col, *, name):
    S = lat.shape[1]
    tn = _tile(S, 256)

    def body(z_ref, fb_ref, o_ref, carry):
        @pl.when(pl.program_id(0) == 0)
        def _():
            carry[...] = jnp.zeros_like(carry)

        z = z_ref[...] + fb_ref[...]
        lf = jnp.minimum(z, 0.0) - jnp.log1p(jnp.exp(-jnp.abs(z)))
        r = lax.broadcasted_iota(jnp.int32, (tn, tn), 0)
        c = lax.broadcasted_iota(jnp.int32, (tn, tn), 1)
        tri = (r <= c).astype(BF16)
        a1, a2, a3 = _split3(lf)
        cum = _dot(a1, tri) + _dot(a2, tri) + _dot(a3, tri) + carry[:, 0:1]
        o_ref[...] = cum
        carry[...] = jnp.broadcast_to(cum[:, tn - 1:tn], carry.shape)

    return pl.pallas_call(
        body, name=name, grid=(S // tn,),
        in_specs=[pl.BlockSpec((GATE_ROWS, tn), lambda i: (0, i)), pl.BlockSpec((GATE_ROWS, 1), lambda i: (0, 0))],
        out_specs=pl.BlockSpec((GATE_ROWS, tn), lambda i: (0, i)),
        out_shape=jax.ShapeDtypeStruct((GATE_ROWS, S), F32),
        scratch_shapes=[pltpu.VMEM((GATE_ROWS, LANES), F32)],
        compiler_params=_params("arbitrary"),
    )(lat, fb_col)


def _gate_bwd(lat, fb_col, dF, *, name):
    S = lat.shape[1]
    tn = _tile(S, 256)
    nt = S // tn

    def body(z_ref, fb_ref, df_ref, dz_ref, dfb_ref, carry):
        @pl.when(pl.program_id(0) == 0)
        def _():
            carry[...] = jnp.zeros_like(carry)
            dfb_ref[...] = jnp.zeros_like(dfb_ref)

        r = lax.broadcasted_iota(jnp.int32, (tn, tn), 0)
        c = lax.broadcasted_iota(jnp.int32, (tn, tn), 1)
        tri = (r >= c).astype(BF16)
        a1, a2, a3 = _split3(df_ref[...])
        dlf = _dot(a1, tri) + _dot(a2, tri) + _dot(a3, tri) + carry[:, 0:1]
        carry[...] = jnp.broadcast_to(dlf[:, 0:1], carry.shape)
        z = z_ref[...] + fb_ref[...]
        row = lax.broadcasted_iota(jnp.int32, (GATE_ROWS, tn), 0)
        dz = jnp.where(row < FOX_HEADS, dlf / (1.0 + jnp.exp(z)), 0.0)
        dz_ref[...] = dz
        dfb_ref[...] += jnp.sum(dz, axis=1, keepdims=True)

    blk = pl.BlockSpec((GATE_ROWS, tn), lambda i: (0, nt - 1 - i))
    vec = pl.BlockSpec((GATE_ROWS, 1), lambda i: (0, 0))
    return pl.pallas_call(
        body, name=name, grid=(nt,),
        in_specs=[blk, vec, blk], out_specs=[blk, vec],
        out_shape=[jax.ShapeDtypeStruct((GATE_ROWS, S), F32), jax.ShapeDtypeStruct((GATE_ROWS, 1), F32)],
        scratch_shapes=[pltpu.VMEM((GATE_ROWS, LANES), F32)],
        compiler_params=_params("arbitrary"),
    )(lat, fb_col, dF)


def _rope_tables(S):
    pos = jnp.arange(S, dtype=F32)
    inv_freq = ROPE_THETA ** (-(jnp.arange(MLA_ROPE // 2, dtype=F32) * 2.0 / MLA_ROPE))
    ang = pos[:, None] * inv_freq[None, :]
    cos, sin = jnp.cos(ang).T, jnp.sin(ang).T
    z16 = jnp.zeros_like(cos)

    def slab(lo, fill):
        def put(first, second, f):
            return jnp.concatenate([jnp.full((lo, S), f, F32), first, second, jnp.full((LANES - lo - MLA_ROPE, S), f, F32)], axis=0)
        return put(cos, cos, fill), put(-sin, z16, 0.0), put(z16, sin, 0.0)

    tq = tuple(jnp.tile(t, (MLA_HEADS, 1)) for t in slab(MLA_NOPE, 1.0))
    return tq, slab(0, 0.0)


def _rope(x, c, s1, s2):
    n = x.shape[0]
    half = MLA_ROPE // 2
    return x * c + pltpu.roll(x, n - half, 0) * s1 + pltpu.roll(x, half, 0) * s2


def _rope_t(dy, c, s1, s2):
    n = dy.shape[0]
    half = MLA_ROPE // 2
    return dy * c + pltpu.roll(dy * s1, half, 0) + pltpu.roll(dy * s2, n - half, 0)


KR_SLAB0 = MLA_Q_RANK + MLA_KV_RANK


def _mla_prep_fwd(lat, g_q, g_kv, w_uq_t, w_ukv_t, tq, tmisc, *, name):
    S = lat.shape[1]
    tn = _tile(S, 512)
    QW = MLA_HEADS * MLA_PAD

    def body(lat_ref, gq_ref, gkv_ref, wq_ref, wkv_ref, c_ref, s1_ref, s2_ref, cm_ref, s1m_ref, s2m_ref,
             nq_ref, nkv_ref, q_ref, k_ref, v_ref):
        x = pltpu.roll(lat_ref[...], LAT_ROWS - LAT_SHIFT, 0)
        nq = _col_rms(x[0:MLA_Q_RANK, :], gq_ref[...]).astype(BF16)
        nkv = _col_rms(x[MLA_Q_RANK:KR_SLAB0, :], gkv_ref[...]).astype(BF16)
        nq_ref[...] = nq
        nkv_ref[...] = nkv
        q_ref[...] = _rope(_dot(wq_ref[...], nq), c_ref[...], s1_ref[...], s2_ref[...]).astype(BF16)
        kv = _dot(wkv_ref[...], nkv).astype(BF16)
        kr = _rope(x[KR_SLAB0:LAT_ROWS, :], cm_ref[...], s1m_ref[...], s2m_ref[...]).astype(BF16)
        for h in range(MLA_HEADS):
            k_ref[h * MLA_PAD:h * MLA_PAD + MLA_NOPE, :] = kv[h * LANES:h * LANES + MLA_NOPE, :]
            k_ref[h * MLA_PAD + MLA_NOPE:(h + 1) * MLA_PAD, :] = kr[0:MLA_PAD - MLA_NOPE, :]
            v_ref[h * HEAD_DIM:(h + 1) * HEAD_DIM, :] = kv[h * LANES + MLA_NOPE:(h + 1) * LANES, :]

    def col(rows):
        return pl.BlockSpec((rows, tn), lambda i: (0, i))

    def full(a):
        return pl.BlockSpec(a.shape, lambda i: (0, 0))

    return pl.pallas_call(
        body, name=name, grid=(S // tn,),
        in_specs=[col(LAT_ROWS), full(g_q), full(g_kv), full(w_uq_t), full(w_ukv_t),
                  col(QW), col(QW), col(QW), col(LANES), col(LANES), col(LANES)],
        out_specs=[col(MLA_Q_RANK), col(MLA_KV_RANK), col(QW), col(QW), col(MLA_HEADS * HEAD_DIM)],
        out_shape=[jax.ShapeDtypeStruct((MLA_Q_RANK, S), BF16), jax.ShapeDtypeStruct((MLA_KV_RANK, S), BF16),
                   jax.ShapeDtypeStruct((QW, S), BF16), jax.ShapeDtypeStruct((QW, S), BF16),
                   jax.ShapeDtypeStruct((MLA_HEADS * HEAD_DIM, S), BF16)],
        compiler_params=_params("parallel"),
    )(lat, g_q, g_kv, w_uq_t, w_ukv_t, *tq, *tmisc)


def _mla_prep_bwd(lat, nq, nkv, g_q, g_kv, w_uq_p, w_ukv, tq, tmisc, dq, dk, dv, dflog, *, name):
    S = lat.shape[1]
    tn = _tile(S, 512)
    QW = MLA_HEADS * MLA_PAD

    def body(lat_ref, nq_ref, nkv_ref, gq_ref, gkv_ref, wq_ref, wkv_ref, c_ref, s1_ref, s2_ref,
             cm_ref, s1m_ref, s2m_ref, dq_ref, dk_ref, dv_ref, dfl_ref,
             dlat_ref, dwq_ref, dwkv_ref, dgq_ref, dgkv_ref, y_s):
        @pl.when(pl.program_id(0) == 0)
        def _():
            dwq_ref[...] = jnp.zeros_like(dwq_ref)
            dwkv_ref[...] = jnp.zeros_like(dwkv_ref)
            dgq_ref[...] = jnp.zeros_like(dgq_ref)
            dgkv_ref[...] = jnp.zeros_like(dgkv_ref)

        x = pltpu.roll(lat_ref[...], LAT_ROWS - LAT_SHIFT, 0)
        dqm = _rope_t(dq_ref[...], c_ref[...], s1_ref[...], s2_ref[...]).astype(BF16)
        dwq_ref[...] += _dot(dqm, nq_ref[...], NT)
        dx, dg = _col_rms_bwd(x[0:MLA_Q_RANK, :], gq_ref[...], _dot(wq_ref[...], dqm))
        y_s[0:MLA_Q_RANK, :] = dx
        dgq_ref[...] += dg
        dkv = jnp.concatenate(
            [part for h in range(MLA_HEADS)
             for part in (dk_ref[h * MLA_PAD:h * MLA_PAD + MLA_NOPE, :], dv_ref[h * HEAD_DIM:(h + 1) * HEAD_DIM, :])],
            axis=0).astype(BF16)
        dwkv_ref[...] += _dot(dkv, nkv_ref[...], NT)
        dx, dg = _col_rms_bwd(x[MLA_Q_RANK:KR_SLAB0, :], gkv_ref[...], _dot(wkv_ref[...], dkv))
        y_s[MLA_Q_RANK:KR_SLAB0, :] = dx
        dgkv_ref[...] += dg
        dkr = dk_ref[MLA_NOPE:MLA_PAD, :]
        for h in range(1, MLA_HEADS):
            dkr = dkr + dk_ref[h * MLA_PAD + MLA_NOPE:(h + 1) * MLA_PAD, :]
        dkr = jnp.concatenate([dkr, jnp.zeros((MLA_NOPE, tn), F32)], axis=0)
        y_s[KR_SLAB0:LAT_ROWS, :] = _rope_t(dkr, cm_ref[...], s1m_ref[...], s2m_ref[...])
        y = pltpu.roll(y_s[...], LAT_SHIFT, 0)
        row = lax.broadcasted_iota(jnp.int32, (LAT_ROWS, tn), 0)
        dfl = jnp.concatenate([dfl_ref[...], jnp.zeros((LAT_ROWS - GATE_ROWS, tn), F32)], axis=0)
        dlat_ref[...] = jnp.where(row < LAT_SHIFT, dfl, y).astype(BF16)

    def col(rows):
        return pl.BlockSpec((rows, tn), lambda i: (0, i))

    def full(a):
        return pl.BlockSpec(a.shape, lambda i: (0, 0))

    def acc(r, c):
        return pl.BlockSpec((r, c), lambda i: (0, 0))

    return pl.pallas_call(
        body, name=name, grid=(S // tn,),
        in_specs=[col(LAT_ROWS), col(MLA_Q_RANK), col(MLA_KV_RANK), full(g_q), full(g_kv),
                  full(w_uq_p), full(w_ukv), col(QW), col(QW), col(QW), col(LANES), col(LANES), col(LANES),
                  col(QW), col(QW), col(MLA_HEADS * HEAD_DIM), col(GATE_ROWS)],
        out_specs=[col(LAT_ROWS), acc(QW, MLA_Q_RANK), acc(QW, MLA_KV_RANK), acc(MLA_Q_RANK, 1), acc(MLA_KV_RANK, 1)],
        out_shape=[jax.ShapeDtypeStruct((LAT_ROWS, S), BF16), jax.ShapeDtypeStruct((QW, MLA_Q_RANK), F32),
                   jax.ShapeDtypeStruct((QW, MLA_KV_RANK), F32), jax.ShapeDtypeStruct((MLA_Q_RANK, 1), F32),
                   jax.ShapeDtypeStruct((MLA_KV_RANK, 1), F32)],
        scratch_shapes=[pltpu.VMEM((LAT_ROWS, tn), F32)],
        compiler_params=_params("arbitrary"),
    )(lat, nq, nkv, g_q, g_kv, w_uq_p, w_ukv, *tq, *tmisc, dq, dk, dv, dflog)


def _dproj_cast(dqa, dkva, dqf, dkf, dvf, dlat, *, name):
    S = dqa.shape[1]
    tn = _tile(S, 512)
    parts = (dqa, dkva, dqf, dkf, dvf, dlat)

    def body(*refs):
        o_ref = refs[-1]
        r0 = 0
        for ref in refs[:-1]:
            n = ref.shape[0]
            o_ref[r0:r0 + n, :] = ref[...].astype(BF16)
            r0 += n

    return pl.pallas_call(
        body, name=name, grid=(S // tn,),
        in_specs=[pl.BlockSpec((p.shape[0], tn), lambda i: (0, i)) for p in parts],
        out_specs=pl.BlockSpec((IN_ROWS, tn), lambda i: (0, i)),
        out_shape=jax.ShapeDtypeStruct((IN_ROWS, S), BF16),
        compiler_params=_params("parallel"),
    )(*parts)


GELU_C = math.sqrt(2.0 / math.pi)
GELU_A = 0.044715


def _conv_taps(a, halo, w_ref, b_ref, first):
    row = lax.broadcasted_iota(jnp.int32, a.shape, 0)
    h7 = jnp.where(first, 0.0, halo[7:8, :])
    h6 = jnp.where(first, 0.0, halo[6:7, :])
    a1 = jnp.where(row == 0, h7, pltpu.roll(a, 1, 0))
    a2 = jnp.where(row == 0, h6, jnp.where(row == 1, h7, pltpu.roll(a, 2, 0)))
    u = ((b_ref[...] + w_ref[0:1, :] * a2) + w_ref[1:2, :] * a1) + w_ref[2:3, :] * a
    return u, a1, a2


def _conv_specs(S, tm, tc, nc):
    hb = tm // 8
    main = lambda off: pl.BlockSpec((tm, tc), lambda j, i: (i, j + off))
    halo = lambda off: pl.BlockSpec((8, tc), lambda j, i: (jnp.maximum(i * hb - 1, 0), j + off))
    wspec = lambda off: pl.BlockSpec((3, tc), lambda j, i: (0, j + off))
    bspec = lambda off: pl.BlockSpec((1, tc), lambda j, i: (0, j + off))
    return main, halo, wspec, bspec


def _conv_geglu_fwd(a, conv_w, conv_b, *, name):
    S = a.shape[0]
    tm, tc = _tile(S, 512), _tile(D_FF, 1408)
    nc = D_FF // tc
    main, halo, wspec, bspec = _conv_specs(S, tm, tc, nc)

    def body(ag_ref, au_ref, hg_ref, hu_ref, wg_ref, wu_ref, bg_ref, bu_ref, z_ref):
        first = pl.program_id(1) == 0
        gate, _, _ = _conv_taps(ag_ref[...], hg_ref[...], wg_ref, bg_ref, first)
        up, _, _ = _conv_taps(au_ref[...], hu_ref[...], wu_ref, bu_ref, first)
        cdf = 0.5 * (1.0 + jnp.tanh(GELU_C * (gate + GELU_A * (gate * gate * gate))))
        z_ref[...] = (gate * cdf * up).astype(BF16)

    return pl.pallas_call(
        body, name=name, grid=(nc, S // tm),
        in_specs=[main(0), main(nc), halo(0), halo(nc), wspec(0), wspec(nc), bspec(0), bspec(nc)],
        out_specs=pl.BlockSpec((tm, tc), lambda j, i: (i, j)),
        out_shape=jax.ShapeDtypeStruct((S, D_FF), BF16),
        compiler_params=_params("parallel", "arbitrary"),
    )(a, a, a, a, conv_w, conv_w, conv_b, conv_b)


def _conv_geglu_bwd(a, conv_w, conv_b, dz, *, name):
    S = a.shape[0]
    tm, tc = _tile(S, 512), _tile(D_FF, 1408)
    nc = D_FF // tc
    main, halo, wspec, bspec = _conv_specs(S, tm, tc, nc)

    def body(ag_ref, au_ref, hg_ref, hu_ref, wg_ref, wu_ref, bg_ref, bu_ref, dz_ref, du_ref, dw_ref, db_ref):
        first = pl.program_id(1) == 0

        @pl.when(first)
        def _():
            dw_ref[...] = jnp.zeros_like(dw_ref)
            db_ref[...] = jnp.zeros_like(db_ref)

        gate, g1, g2 = _conv_taps(ag_ref[...], hg_ref[...], wg_ref, bg_ref, first)
        up, u1, u2 = _conv_taps(au_ref[...], hu_ref[...], wu_ref, bu_ref, first)
        dz = dz_ref[...]
        g2x = gate * gate
        th = jnp.tanh(GELU_C * (gate + GELU_A * (g2x * gate)))
        cdf = 0.5 * (1.0 + th)
        dgelu = cdf + gate * (0.5 * (1.0 - th * th) * (GELU_C * (1.0 + 3.0 * GELU_A * g2x)))
        dug = dz * up * dgelu
        duu = dz * (gate * cdf)
        du_ref[0] = dug
        du_ref[1] = duu
        for half, du, taps in ((0, dug, (g2, g1, ag_ref[...])), (1, duu, (u2, u1, au_ref[...]))):
            for tap in range(3):
                dw_ref[half, tap:tap + 1, :] += jnp.sum(du * taps[tap], axis=0, keepdims=True)
            db_ref[half] += jnp.sum(du, axis=0, keepdims=True)

    return pl.pallas_call(
        body, name=name, grid=(nc, S // tm),
        in_specs=[main(0), main(nc), halo(0), halo(nc), wspec(0), wspec(nc), bspec(0), bspec(nc),
                  pl.BlockSpec((tm, tc), lambda j, i: (i, j))],
        out_specs=[pl.BlockSpec((2, tm, tc), lambda j, i: (0, i, j)), pl.BlockSpec((2, 3, tc), lambda j, i: (0, 0, j)),
                   pl.BlockSpec((2, 1, tc), lambda j, i: (0, 0, j))],
        out_shape=[jax.ShapeDtypeStruct((2, S, D_FF), F32), jax.ShapeDtypeStruct((2, 3, D_FF), F32),
                   jax.ShapeDtypeStruct((2, 1, D_FF), F32)],
        compiler_params=_params("parallel", "arbitrary"),
    )(a, a, a, a, conv_w, conv_w, conv_b, conv_b, dz)


def _conv_bwd_input(du, conv_w, *, name):
    S = du.shape[1]
    tm, tc = _tile(S, 512), _tile(D_FF, 1408)
    nc = D_FF // tc
    nr = S // tm
    hb = tm // 8

    def body(du_ref, nx_ref, w_ref, da_ref):
        last = pl.program_id(2) == nr - 1
        d = du_ref[0]
        row = lax.broadcasted_iota(jnp.int32, d.shape, 0)
        n0 = jnp.where(last, 0.0, nx_ref[0, 0:1, :])
        n1 = jnp.where(last, 0.0, nx_ref[0, 1:2, :])
        d1 = jnp.where(row == tm - 1, n0, pltpu.roll(d, tm - 1, 0))
        d2 = jnp.where(row == tm - 1, n1, jnp.where(row == tm - 2, n0, pltpu.roll(d, tm - 2, 0)))
        da_ref[...] = (w_ref[2:3, :] * d + w_ref[1:2, :] * d1 + w_ref[0:1, :] * d2).astype(BF16)

    return pl.pallas_call(
        body, name=name, grid=(2, nc, nr),
        in_specs=[pl.BlockSpec((1, tm, tc), lambda h, j, i: (h, i, j)),
                  pl.BlockSpec((1, 8, tc), lambda h, j, i: (h, jnp.minimum((i + 1) * hb, S // 8 - 1), j)),
                  pl.BlockSpec((3, tc), lambda h, j, i: (0, h * nc + j))],
        out_specs=pl.BlockSpec((tm, tc), lambda h, j, i: (i, h * nc + j)),
        out_shape=jax.ShapeDtypeStruct((S, 2 * D_FF), BF16),
        compiler_params=_params("parallel", "parallel", "arbitrary"),
    )(du, du, conv_w)


def _adamw(w, g, m, v, *, name):
    L, A, B = w.shape
    ta = _tile(A, ROW_TILE)

    def body(w_ref, g_ref, m_ref, v_ref, d_ref, mo_ref, vo_ref):
        g = g_ref[...]
        m = ADAM_B1 * m_ref[...] + (1.0 - ADAM_B1) * g
        v = ADAM_B2 * v_ref[...] + (1.0 - ADAM_B2) * jnp.square(g)
        m_hat = m / (1.0 - ADAM_B1 ** ADAM_STEP)
        v_hat = v / (1.0 - ADAM_B2 ** ADAM_STEP)
        d_ref[...] = -ADAM_LR * (m_hat / (jnp.sqrt(v_hat) + ADAM_EPS) + ADAM_WD * w_ref[...])
        mo_ref[...] = m
        vo_ref[...] = v

    blk = pl.BlockSpec((None, ta, B), lambda l, i: (l, i, 0))
    shp = jax.ShapeDtypeStruct((L, A, B), F32)
    return pl.pallas_call(
        body, name=name, grid=(L, A // ta),
        in_specs=[blk] * 4, out_specs=[blk] * 3, out_shape=[shp] * 3,
        compiler_params=_params("parallel", "parallel"),
    )(w, g, m, v)


def _scalar(v):
    return jnp.reshape(v, (1,)).astype(jnp.int32)


def _chip_index():
    return 2 * lax.axis_index("x") + lax.axis_index("y")


def _pair_sum(g, recv, *, name):
    n, A, B = g.shape
    ta = _tile(A // 2, ROW_TILE)
    nb = A // 2 // ta

    def body(c_ref, g_ref, r_ref, o_ref):
        o_ref[...] = g_ref[...] + r_ref[...]

    return pl.pallas_call(
        body, name=name,
        grid_spec=pltpu.PrefetchScalarGridSpec(
            num_scalar_prefetch=1, grid=(n, nb),
            in_specs=[pl.BlockSpec((None, ta, B), lambda s, r, c_ref: (s, c_ref[0] * nb + r, 0)),
                      pl.BlockSpec((None, ta, B), lambda s, r, c_ref: (s, r, 0))],
            out_specs=pl.BlockSpec((None, ta, B), lambda s, r, c_ref: (s, r, 0))),
        out_shape=jax.ShapeDtypeStruct((n, A // 2, B), F32),
        compiler_params=_params("parallel", "parallel"),
    )(_scalar(lax.axis_index("c")), g, recv)


def _chip_sum(landed, own, *, name):
    n, A2, B = landed.shape
    ta = _tile(A2, ROW_TILE)

    def body(me_ref, *refs):
        slots, own_ref, o_ref = refs[:n], refs[n], refs[n + 1]
        parts = [jnp.where(me_ref[0] == s, own_ref[...], slots[s][...]) for s in range(n)]
        o_ref[...] = ((parts[0] + parts[1]) + parts[2]) + parts[3]

    def slot(s):
        return pl.BlockSpec((None, ta, B), lambda r, me_ref: (jnp.where(me_ref[0] == s, (s + 1) % n, s), r, 0))

    return pl.pallas_call(
        body, name=name,
        grid_spec=pltpu.PrefetchScalarGridSpec(
            num_scalar_prefetch=1, grid=(A2 // ta,),
            in_specs=[slot(s) for s in range(n)] + [pl.BlockSpec((None, ta, B), lambda r, me_ref: (me_ref[0], r, 0))],
            out_specs=pl.BlockSpec((ta, B), lambda r, me_ref: (r, 0))),
        out_shape=jax.ShapeDtypeStruct((A2, B), F32),
        compiler_params=_params("parallel"),
    )(_scalar(_chip_index()), *([landed] * n), own)


HBM_SPEC = pl.BlockSpec(memory_space=pl.ANY)
COMM_PARAMS = pltpu.CompilerParams(has_side_effects=True)


def _mesh_pos():
    return lax.axis_index("x"), lax.axis_index("y"), lax.axis_index("c")


def _other_chips(x, y):
    return [(1 - x, y), (x, 1 - y), (1 - x, 1 - y)]


def _my_layers(c):
    return pl.ds(HALF_DEPTH * c, HALF_DEPTH)


def _remote(src, dst, send_sems, recv_sems, k, to):
    return pltpu.make_async_remote_copy(src_ref=src, dst_ref=dst, send_sem=send_sems.at[k], recv_sem=recv_sems.at[k],
                                        device_id=to, device_id_type=MESH)


def _gather_shards(shards):
    n = len(shards)

    def body(*refs):
        ins, outs = refs[:n], refs[n:2 * n]
        send_sems, recv_sems, local_sems = refs[2 * n:]
        x, y, c = _mesh_pos()
        me = 2 * x + y
        chips = _other_chips(x, y)
        sibling = (x, y, 1 - c)
        mine, other = _my_layers(c), _my_layers(1 - c)
        local = [pltpu.make_async_copy(ins[k], outs[k].at[:, me], local_sems.at[k]) for k in range(n)]
        for cp in local:
            cp.start()
        first = [_remote(ins[k].at[mine], outs[k].at[mine, me], send_sems, recv_sems, 6 * k + j, (px, py, c))
                 for j, (px, py) in enumerate(chips) for k in range(n)]
        for cp in first:
            cp.start()
        passed = []
        for j, (px, py) in enumerate(chips):
            for k in range(n):
                landed = outs[k].at[mine, 2 * px + py]
                _remote(landed, landed, send_sems, recv_sems, 6 * k + j, (px, py, c)).wait_recv()
                cp = _remote(landed, landed, send_sems, recv_sems, 6 * k + 3 + j, sibling)
                cp.start()
                passed.append(cp)
        for j, (px, py) in enumerate(chips):
            for k in range(n):
                landed = outs[k].at[other, 2 * px + py]
                _remote(landed, landed, send_sems, recv_sems, 6 * k + 3 + j, sibling).wait_recv()
        for cp in first + passed:
            cp.wait_send()
        for cp in local:
            cp.wait()

    return pl.pallas_call(
        body, name="gather_weight_shards",
        in_specs=[HBM_SPEC] * n, out_specs=[HBM_SPEC] * n,
        out_shape=[jax.ShapeDtypeStruct((s.shape[0], N_CHIPS) + s.shape[1:], s.dtype) for s in shards],
        scratch_shapes=[pltpu.SemaphoreType.DMA((6 * n,)), pltpu.SemaphoreType.DMA((6 * n,)), pltpu.SemaphoreType.DMA((n,))],
        compiler_params=COMM_PARAMS,
    )(*shards)


def _half_rows(rows, c):
    return pl.ds(pl.multiple_of(c * (rows // 2), 8), rows // 2)


def _sibling_exchange(gs, *, name):
    n = len(gs)

    def body(*refs):
        ins, outs = refs[:n], refs[n:2 * n]
        send_sems, recv_sems = refs[2 * n:]
        x, y, c = _mesh_pos()
        copies = [_remote(ins[k].at[:, _half_rows(gs[k].shape[1], 1 - c)], outs[k], send_sems, recv_sems, k, (x, y, 1 - c))
                  for k in range(n)]
        for cp in copies:
            cp.start()
        for cp in copies:
            cp.wait()

    return pl.pallas_call(
        body, name=name,
        in_specs=[HBM_SPEC] * n, out_specs=[HBM_SPEC] * n,
        out_shape=[jax.ShapeDtypeStruct((g.shape[0], g.shape[1] // 2, g.shape[2]), g.dtype) for g in gs],
        scratch_shapes=[pltpu.SemaphoreType.DMA((n,)), pltpu.SemaphoreType.DMA((n,))],
        compiler_params=COMM_PARAMS,
    )(*gs)


HBM_ONLY = pl.BlockSpec(memory_space=pltpu.HBM)
SEM_SPEC = pl.BlockSpec(memory_space=pltpu.SEMAPHORE)
SPLIT_PARAMS = pltpu.CompilerParams(has_side_effects=pltpu.SideEffectType.DATAFLOW_SIDE_EFFECTING)


def _scatter_copies(srcs, lands, send_sems, recv_sems):
    x, y, c = _mesh_pos()
    me = 2 * x + y
    out = []
    for k in range(len(srcs)):
        for j, (px, py) in enumerate(_other_chips(x, y)):
            s = 2 * px + py
            send = _remote(srcs[k].at[s], lands[k].at[me], send_sems, recv_sems, 3 * k + j, (px, py, c))
            recv = _remote(srcs[k].at[s], lands[k].at[s], send_sems, recv_sems, 3 * k + j, (px, py, c))
            out.append((send, recv))
    return out


def _scatter_start(ps, *, name):
    n = len(ps)
    lands = [lax.empty(p.shape, p.dtype) for p in ps]

    def body(*refs):
        srcs, zones = refs[:n], refs[n:2 * n]
        send_sems, recv_sems, token = refs[2 * n], refs[2 * n + 1], refs[-1]
        for send, _ in _scatter_copies(srcs, zones, send_sems, recv_sems):
            send.start()
        token[...] = jnp.zeros_like(token)

    hbm = lambda a: pltpu.HBM(a.shape, a.dtype)
    res = pl.pallas_call(
        body, name=name,
        in_specs=[HBM_ONLY] * (2 * n),
        out_specs=[SEM_SPEC, SEM_SPEC] + [HBM_ONLY] * (2 * n) + [pl.BlockSpec(memory_space=pltpu.VMEM)],
        out_shape=[pltpu.SemaphoreType.DMA((3 * n,)), pltpu.SemaphoreType.DMA((3 * n,))] + [hbm(a) for a in ps + lands]
        + [jax.ShapeDtypeStruct((8, LANES), F32)],
        input_output_aliases={i: 2 + i for i in range(2 * n)},
        compiler_params=SPLIT_PARAMS,
    )(*[pltpu.with_memory_space_constraint(a, pltpu.HBM) for a in ps + lands])
    return (res[0], res[1], list(res[2:2 + n]), list(res[2 + n:2 + 2 * n])), res[-1][0:1, 0:1]


def _scatter_wait(started, after, *, name):
    nl = len(started)
    n = len(started[0][2])
    flat = [a for (_, _, ps, lands) in started for a in ps + lands]

    def body(*refs):
        bufs = refs[:2 * n * nl]
        sems = refs[2 * n * nl:2 * n * nl + 2 * nl]
        for l in range(nl):
            srcs = bufs[2 * n * l:2 * n * l + n]
            zones = bufs[2 * n * l + n:2 * n * (l + 1)]
            for send, recv in _scatter_copies(srcs, zones, sems[2 * l], sems[2 * l + 1]):
                send.wait_send()
                recv.wait_recv()

    res = pl.pallas_call(
        body, name=name,
        in_specs=[HBM_ONLY] * len(flat) + [SEM_SPEC] * (2 * nl) + [HBM_SPEC],
        out_specs=[HBM_ONLY] * len(flat),
        out_shape=[pltpu.HBM(a.shape, a.dtype) for a in flat],
        input_output_aliases={i: i for i in range(len(flat))},
        compiler_params=SPLIT_PARAMS,
    )(*flat, *[s for (ss, rs, _, _) in started for s in (ss, rs)], after)
    return [(list(res[2 * n * l:2 * n * l + n]), list(res[2 * n * l + n:2 * n * (l + 1)])) for l in range(nl)]


def _sibling_share(hs):
    n, nl = len(hs), len(hs[0])

    def body(*refs):
        ins, outs = refs[:n * nl], refs[n * nl:n * nl + n]
        send_sems, recv_sems, local_sems = refs[n * nl + n:]
        x, y, c = _mesh_pos()
        local, copies, waits = [], [], []
        for k in range(n):
            rows = 2 * hs[k][0].shape[0]
            for l in range(nl):
                i = k * nl + l
                dst = outs[k].at[l, _half_rows(rows, c)]
                local.append(pltpu.make_async_copy(ins[i], dst, local_sems.at[i]))
                copies.append(_remote(ins[i], dst, send_sems, recv_sems, i, (x, y, 1 - c)))
                landed = outs[k].at[l, _half_rows(rows, 1 - c)]
                waits.append(_remote(landed, landed, send_sems, recv_sems, i, (x, y, 1 - c)))
        for cp in local + copies:
            cp.start()
        for cp in waits:
            cp.wait_recv()
        for cp in copies:
            cp.wait_send()
        for cp in local:
            cp.wait()

    return pl.pallas_call(
        body, name="grad_sibling_share",
        in_specs=[HBM_SPEC] * (n * nl), out_specs=[HBM_SPEC] * n,
        out_shape=[jax.ShapeDtypeStruct((nl, 2 * h[0].shape[0], h[0].shape[1]), h[0].dtype) for h in hs],
        scratch_shapes=[pltpu.SemaphoreType.DMA((n * nl,))] * 3,
        compiler_params=COMM_PARAMS,
    )(*[h for hk in hs for h in hk])


def _allreduce_small(part):
    rows, C = part.shape

    def body(p_ref, o_ref, slots, send_sems, recv_sems):
        x, y, c = _mesh_pos()
        me = 4 * x + 2 * y + c
        slots[me] = p_ref[...]
        copies = []
        for k in range(1, 8):
            kx, ky, kc = (k >> 2) & 1, (k >> 1) & 1, k & 1
            peer = (x ^ kx if kx else x, y ^ ky if ky else y, c ^ kc if kc else c)
            cp = _remote(p_ref, slots.at[me], send_sems, recv_sems, k - 1, peer)
            cp.start()
            copies.append((cp, peer))
        for k, (cp, peer) in enumerate(copies):
            src = 4 * peer[0] + 2 * peer[1] + peer[2]
            _remote(p_ref, slots.at[src], send_sems, recv_sems, k, peer).wait_recv()
        for cp, _ in copies:
            cp.wait_send()
        total = slots[0]
        for d in range(1, 8):
            total = total + slots[d]
        o_ref[...] = total

    return pl.pallas_call(
        body, name="small_grad_allreduce",
        in_specs=[pl.BlockSpec(memory_space=pltpu.VMEM)], out_specs=pl.BlockSpec(memory_space=pltpu.VMEM),
        out_shape=jax.ShapeDtypeStruct((rows, C), F32),
        scratch_shapes=[pltpu.VMEM((8, rows, C), F32), pltpu.SemaphoreType.DMA((7,)), pltpu.SemaphoreType.DMA((7,))],
        compiler_params=pltpu.CompilerParams(has_side_effects=True, vmem_limit_bytes=VMEM_LIMIT_BYTES),
    )(part)


def _pad_w_uq(w):
    lead = w.shape[:-1]
    w = w.reshape(lead + (MLA_HEADS, MLA_QK))
    w = jnp.concatenate([w, jnp.zeros(lead + (MLA_HEADS, MLA_PAD - MLA_QK), w.dtype)], axis=-1)
    return w.reshape(lead + (MLA_HEADS * MLA_PAD,))


def _unpad_w_uq(g):
    lead = g.shape[:-1]
    return g.reshape(lead + (MLA_HEADS, MLA_PAD))[..., :MLA_QK].reshape(lead + (MLA_HEADS * MLA_QK,))


def _t(a):
    return jnp.swapaxes(a, -1, -2)


def _cols_of_shards(g):
    L, n, A, B = g.shape
    return g.transpose(0, 2, 1, 3).reshape(L, A, n * B)


def _shards_of_cols(w):
    A, NB = w.shape
    return w.reshape(A, N_CHIPS, NB // N_CHIPS).transpose(1, 0, 2)


BIG = ("w_in", "w_uq", "w_ukv", "w_out", "w_up", "w_down")
SMALL = ("attn_pre_norm", "forget_bias", "swa_sinks", "rel_bias", "q_latent_norm", "kv_latent_norm", "group_norm",
         "attn_post_norm", "ffn_pre_norm", "conv_b", "ffn_post_norm")
WEIGHTS = ("attn_pre_norm", "w_in", "forget_bias", "swa_sinks", "rel_bias", "q_latent_norm", "w_uq", "kv_latent_norm",
           "w_ukv", "group_norm", "w_out", "attn_post_norm", "ffn_pre_norm", "w_up", "conv_w", "conv_b", "w_down",
           "ffn_post_norm")


def _pack(arrs, cols, row_mult):
    flat = jnp.concatenate([a.reshape(-1) for a in arrs])
    n = flat.shape[0]
    per = cols * row_mult
    total = -(-n // per) * per
    return jnp.pad(flat, (0, total - n)).reshape(total // cols, cols)


def _unpack(packed, shapes):
    flat = packed.reshape(-1)
    out, off = [], 0
    for shp in shapes:
        n = int(np.prod(shp))
        out.append(flat[off:off + n].reshape(shp))
        off += n
    return out


def _kernel_weights(gathered, small):
    L = gathered["w_in"].shape[0]
    w_in_t = _t(gathered["w_in"]).reshape(L, IN_COLS, D_MODEL)
    w_in_t = jnp.pad(w_in_t, ((0, 0), (0, IN_ROWS - IN_COLS), (0, 0)))
    w_uq_p = _pad_w_uq(_cols_of_shards(gathered["w_uq"]))
    w_ukv = _cols_of_shards(gathered["w_ukv"])
    W = dict(small)
    W.update(w_qkv_t=w_in_t[:, :QKV_ROWS], w_lat_t=w_in_t[:, QKV_ROWS:], w_in_t=w_in_t, w_uq_p=w_uq_p, w_uq_t=_t(w_uq_p),
             w_ukv=w_ukv, w_ukv_t=_t(w_ukv), w_out=gathered["w_out"].reshape(L, D_MODEL, D_MODEL), w_up=gathered["w_up"],
             conv_w=_cols_of_shards(gathered["conv_w"]), w_down=gathered["w_down"].reshape(L, D_FF, D_MODEL))
    return W


def _local_step(x, target, W, layer_done):
    S = x.shape[0]
    tq_tabs, tm_tabs = _rope_tables(S)
    onehot_t = _rel_onehot_t()
    bias_t = _bias_table(W["rel_bias"].T, onehot_t).reshape(SWA_Q_HEADS, 2 * WINDOW, WINDOW)
    row = lambda a: a.reshape(1, -1)
    col = lambda a: a.reshape(-1, 1)
    fox_rows = (FOX_ROW0, FOX_ROW0 + FOX_HEADS * HEAD_DIM, FOX_ROW0 + 2 * FOX_HEADS * HEAD_DIM, SWA_Q_HEADS)
    fox = dict(rows=fox_rows, H=FOX_HEADS, Dk=HEAD_DIM, Dv=HEAD_DIM, scale=HEAD_DIM ** -0.5)
    mla = dict(rows=(0, 0, 0, SWA_Q_HEADS + FOX_HEADS), H=MLA_HEADS, Dk=MLA_PAD, Dv=HEAD_DIM, scale=MLA_QK ** -0.5)

    saved = []
    h = _rms_fwd(x, row(W["attn_pre_norm"][0]), name="rms_in")
    for l in range(DEPTH):
        sv = {"x0": x, "h1": h}
        qkv = _matmul(W["w_qkv_t"][l], h, tb=True, out_dtype=BF16, name="proj_qkv")
        lat = _matmul(W["w_lat_t"][l], h, tb=True, name="proj_lat")
        oa, lse_a = _swa_fwd(qkv, bias_t, W["swa_sinks"][l], name="swa_fwd")
        fb_col = jnp.pad(col(W["forget_bias"][l]), ((0, GATE_ROWS - FOX_HEADS), (0, 0)))
        f4 = _gate_fwd(lat, fb_col, name="fox_gate_fwd")[:FOX_HEADS]
        f_row, f_col = f4[:, None, :], f4.T
        of, lse_f = _attn_fwd(qkv, qkv, qkv, f_row=f_row, f_col=f_col, name="fox_fwd", **fox)
        nq, nkv, qm, km, vm = _mla_prep_fwd(lat, col(W["q_latent_norm"][l]), col(W["kv_latent_norm"][l]), W["w_uq_t"][l],
                                            W["w_ukv_t"][l], tq_tabs, tm_tabs, name="mla_prep_fwd")
        oc, lse_c = _attn_fwd(qm, km, vm, name="mla_fwd", **mla)
        mixed = _group_norm_fwd(oa, of, oc, col(W["group_norm"][l]), name="group_norm_fwd")
        y = _matmul(mixed, W["w_out"][l], ta=True, name="proj_out")
        x1, h2 = _resid_rms(x, y, row(W["attn_post_norm"][l]), row(W["ffn_pre_norm"][l]), name="attn_resid")
        a = _matmul(h2, W["w_up"][l], b_shards=True, name="ffn_up")
        z = _conv_geglu_fwd(a, W["conv_w"][l], row(W["conv_b"][l]), name="conv_geglu_fwd")
        y2 = _matmul(z, W["w_down"][l], name="ffn_down")
        g_next = row(W["attn_pre_norm"][l + 1]) if l + 1 < DEPTH else None
        x2, h_next = _resid_rms(x1, y2, row(W["ffn_post_norm"][l]), g_next, name="ffn_resid")
        sv.update(qkv=qkv, lat=lat, oa=oa, lse_a=lse_a, fb_col=fb_col, f_row=f_row, f_col=f_col, of=of, lse_f=lse_f,
                  nq=nq, nkv=nkv, qm=qm, km=km, vm=vm, oc=oc, lse_c=lse_c, mixed=mixed, y=y, x1=x1, h2=h2, a=a, z=z, y2=y2)
        saved.append(sv)
        x, h = x2, h_next

    loss, dx = _loss_head(x, target)

    G = {k: [None] * DEPTH for k in WEIGHTS if k != "rel_bias" and k not in BIG}
    dbias_layers = [None] * DEPTH
    for l in reversed(range(DEPTH)):
        sv = saved[l]
        gb = {}
        dy2, dg = _rms_bwd(sv["y2"], row(W["ffn_post_norm"][l]), dx, out_dtype=BF16, name="ffn_post_bwd")
        G["ffn_post_norm"][l] = dg[0]
        dz = _matmul(dy2, W["w_down"][l], tb=True, name="ffn_down_dx")
        gb["w_down"] = _matmul(sv["z"], dy2, ta=True, name="ffn_down_dw").reshape(N_CHIPS, D_FF // N_CHIPS, D_MODEL)
        du, dcw, dcb = _conv_geglu_bwd(sv["a"], W["conv_w"][l], row(W["conv_b"][l]), dz, name="conv_geglu_bwd")
        G["conv_w"][l] = dcw.transpose(1, 0, 2).reshape(3, 2 * D_FF)
        G["conv_b"][l] = dcb.reshape(2 * D_FF)
        da = _conv_bwd_input(du, W["conv_w"][l], name="conv_bwd_input")
        dh2 = _matmul(da, W["w_up"][l], tb=True, b_shards=True, name="ffn_up_dx")
        gb["w_up"] = _matmul(sv["h2"], da, ta=True, out_shards=True, name="ffn_up_dw")
        dx1, dg = _rms_bwd(sv["x1"], row(W["ffn_pre_norm"][l]), dh2, resid=dx, out_dtype=F32, name="ffn_pre_bwd")
        G["ffn_pre_norm"][l] = dg[0]
        dy, dg = _rms_bwd(sv["y"], row(W["attn_post_norm"][l]), dx1, out_dtype=BF16, name="attn_post_bwd")
        G["attn_post_norm"][l] = dg[0]
        dmixed = _matmul(W["w_out"][l], dy, tb=True, name="proj_out_dx")
        gb["w_out"] = _matmul(sv["mixed"], dy, name="proj_out_dw").reshape(N_CHIPS, D_MODEL // N_CHIPS, D_MODEL)
        doa, dof, doc, dg, delta = _group_norm_bwd(sv["oa"], sv["of"], sv["oc"], col(W["group_norm"][l]), dmixed,
                                                   name="group_norm_bwd")
        G["group_norm"][l] = dg[:, 0]
        dqa, dkva, dbias_l, dsink = _swa_bwd(sv["qkv"], bias_t, W["swa_sinks"][l], doa, sv["lse_a"],
                                             delta.reshape(-1, S), name="swa_bwd")
        dbias_layers[l] = dbias_l.reshape(SWA_Q_HEADS, -1)
        G["swa_sinks"][l] = dsink[:, 0]
        dqf, dkf, dvf, dfk = _attn_bwd(sv["qkv"], sv["qkv"], sv["qkv"], do=dof, lse=sv["lse_f"], delta=delta,
                                       f_row=sv["f_row"], f_col=sv["f_col"], name="fox_bwd", **fox)
        dF = jnp.pad(dfk.T, ((0, GATE_ROWS - FOX_HEADS), (0, 0)))
        dflog, dfb = _gate_bwd(sv["lat"], sv["fb_col"], dF, name="fox_gate_bwd")
        G["forget_bias"][l] = dfb[:FOX_HEADS, 0]
        dqm, dkm, dvm = _attn_bwd(sv["qm"], sv["km"], sv["vm"], do=doc, lse=sv["lse_c"], delta=delta, name="mla_bwd", **mla)
        dlat, dwq_t, dwkv_t, dgq, dgkv = _mla_prep_bwd(
            sv["lat"], sv["nq"], sv["nkv"], col(W["q_latent_norm"][l]), col(W["kv_latent_norm"][l]), W["w_uq_p"][l],
            W["w_ukv"][l], tq_tabs, tm_tabs, dqm, dkm, dvm, dflog, name="mla_prep_bwd")
        gb["w_uq"], gb["w_ukv"] = _shards_of_cols(_unpad_w_uq(dwq_t.T)), _shards_of_cols(dwkv_t.T)
        G["q_latent_norm"][l], G["kv_latent_norm"][l] = dgq[:, 0], dgkv[:, 0]
        dproj = _dproj_cast(dqa, dkva, dqf, dkf, dvf, dlat, name="dproj_cast")
        dh1 = _matmul(dproj, W["w_in_t"][l], ta=True, name="proj_in_dx")
        dw_in_t = _matmul(dproj, sv["h1"], name="proj_in_dw")
        gb["w_in"] = _t(dw_in_t[:IN_COLS].reshape(N_CHIPS, IN_COLS // N_CHIPS, D_MODEL))
        token = layer_done(l, gb)
        dx, dg = _rms_bwd(sv["x0"], row(W["attn_pre_norm"][l]) + token, dh1, resid=dx1, out_dtype=F32, name="attn_pre_bwd")
        G["attn_pre_norm"][l] = dg[0]

    grads = {k: jnp.stack(v) for k, v in G.items()}
    grads["rel_bias"] = _bias_table_bwd(jnp.stack(dbias_layers), onehot_t).T
    return loss, dx, grads


def kernel(x, attn_pre_norm, w_in, forget_bias, swa_sinks, rel_bias, q_latent_norm, w_uq, kv_latent_norm, w_ukv, group_norm, w_out, attn_post_norm, ffn_pre_norm, w_up, conv_w, conv_b, w_down, ffn_post_norm, loss_target, m_attn_pre_norm, m_w_in, m_forget_bias, m_swa_sinks, m_rel_bias, m_q_latent_norm, m_w_uq, m_kv_latent_norm, m_w_ukv, m_group_norm, m_w_out, m_attn_post_norm, m_ffn_pre_norm, m_w_up, m_conv_w, m_conv_b, m_w_down, m_ffn_post_norm, v_attn_pre_norm, v_w_in, v_forget_bias, v_swa_sinks, v_rel_bias, v_q_latent_norm, v_w_uq, v_kv_latent_norm, v_w_ukv, v_group_norm, v_w_out, v_attn_post_norm, v_ffn_pre_norm, v_w_up, v_conv_w, v_conv_b, v_w_down, v_ffn_post_norm):
    args = dict(locals())
    w = {k: args[k] for k in WEIGHTS}
    m = {k: args["m_" + k] for k in WEIGHTS}
    v = {k: args["v_" + k] for k in WEIGHTS}

    sent = BIG + ("conv_w",)
    gathered = dict(zip(sent, _gather_shards([w[k] if k == "conv_w" else w[k].astype(BF16) for k in sent])))
    W = _kernel_weights(gathered, {k: w[k] for k in SMALL})

    started = [None] * DEPTH

    def layer_done(l, gb):
        gs = [gb[k] for k in BIG]
        recv = _sibling_exchange(gs, name=f"grad_sibling_exchange_{l}")
        pair = [_pair_sum(gk, rk, name="grad_pair_sum") for gk, rk in zip(gs, recv)]
        started[l], token = _scatter_start(pair, name=f"grad_scatter_start_{l}")
        return token

    loss_part, dx, g = _local_step(x[0], loss_target[0], W, layer_done)
    loss = lax.psum(loss_part, ("x", "y", "c"))

    landed = _scatter_wait(started, dx, name="grad_scatter_wait")
    mine = [[_chip_sum(zones[i], pair[i], name="grad_chip_sum") for pair, zones in landed] for i in range(len(BIG))]
    out_g = dict(zip(BIG, _sibling_share(mine)))
    out_d, out_m, out_v = {}, {}, {}
    for k in BIG:
        out_d[k], out_m[k], out_v[k] = _adamw(w[k], out_g[k], m[k], v[k], name="adamw_" + k)

    small_shapes = [w[k].shape for k in SMALL]
    reduced = _allreduce_small(_pack([g[k] for k in SMALL] + [g["conv_w"]], LANES, 8))
    *g_small, g_cw = _unpack(reduced, small_shapes + [g["conv_w"].shape])
    chip = 2 * lax.axis_index("x") + lax.axis_index("y")
    g_small.append(lax.dynamic_slice_in_dim(g_cw, chip * FF_SHARD, FF_SHARD, axis=2))
    names = SMALL + ("conv_w",)
    shapes = small_shapes + [w["conv_w"].shape]
    packed = lambda arrs: _pack(arrs, LANES, ROW_TILE)[None]
    d_s, m_s, v_s = _adamw(packed([w[k] for k in names]), packed(g_small), packed([m[k] for k in names]),
                           packed([v[k] for k in names]), name="adamw_small")
    out_g.update(zip(names, g_small))
    out_d.update(zip(names, _unpack(d_s, shapes)))
    out_m.update(zip(names, _unpack(m_s, shapes)))
    out_v.update(zip(names, _unpack(v_s, shapes)))

    return (loss, dx[None], *[out_g[k] for k in WEIGHTS], *[out_d[k] for k in WEIGHTS],
            *[out_m[k] for k in WEIGHTS], *[out_v[k] for k in WEIGHTS])
```

```python
import math

import numpy as np
import jax
import jax.numpy as jnp
from jax import lax
from jax.experimental import pallas as pl
from jax.experimental.pallas import tpu as pltpu

F32 = jnp.float32
BF16 = jnp.bfloat16

D_MODEL = 1024
DEPTH = 4
HEAD_DIM = 64
SWA_Q_HEADS = 8
SWA_KV_HEADS = 2
SWA_GROUP = SWA_Q_HEADS // SWA_KV_HEADS
WINDOW = 128
FOX_HEADS = 4
MLA_HEADS = 4
MLA_Q_RANK = 256
MLA_KV_RANK = 128
MLA_NOPE = 64
MLA_ROPE = 32
MLA_QK = MLA_NOPE + MLA_ROPE
ROPE_THETA = 10000.0
REL_BUCKETS = 32
REL_MAX_DIST = 128
D_FF = 2816
EPS = 1e-6
NEG_INF = -1e30
LANES = 128
N_CHIPS = 4
HALF_DEPTH = DEPTH // 2

IN_COLS = 1956
IN_ROWS = 2048
QKV_ROWS = 1536
LAT_ROWS = IN_ROWS - QKV_ROWS
LAT_SHIFT = FOX_HEADS
FOX_ROW0 = 768
MLA_PAD = LANES
GATE_ROWS = 8

ADAM_LR = 0.001
ADAM_B1 = 0.9
ADAM_B2 = 0.999
ADAM_EPS = 1e-08
ADAM_WD = 0.01
ADAM_STEP = 10

VMEM_LIMIT_BYTES = 48 * 1024 * 1024
ATT_TILE = 256
ROW_TILE = 256
MESH = pl.DeviceIdType.MESH

NT = (((1,), (1,)), ((), ()))
TN = (((0,), (0,)), ((), ()))
NN = (((1,), (0,)), ((), ()))


def _params(*sem):
    return pltpu.CompilerParams(dimension_semantics=sem, vmem_limit_bytes=VMEM_LIMIT_BYTES)


def _tile(dim, cap):
    for t in (2048, 1408, 1024, 512, 256, 128, 64, 32, 16, 8):
        if t <= cap and dim % t == 0:
            return t
    return dim


def _dot(a, b, dims=NN):
    return lax.dot_general(a, b, dims, preferred_element_type=F32)


def _split3(a):
    a1 = a.astype(BF16)
    r1 = a - a1.astype(F32)
    a2 = r1.astype(BF16)
    a3 = (r1 - a2.astype(F32)).astype(BF16)
    return a1, a2, a3


FF_SHARD = 2 * D_FF // N_CHIPS


def _matmul(a, b, *, ta=False, tb=False, out_dtype=F32, name, b_shards=False, out_shards=False):
    if ta:
        K, M = a.shape
    else:
        M, K = a.shape
    if b_shards:
        K2, N = (2 * D_FF, D_MODEL) if tb else (D_MODEL, 2 * D_FF)
    elif tb:
        N, K2 = b.shape
    else:
        K2, N = b.shape
    assert K == K2, (a.shape, b.shape)
    tm, tn, tk = _tile(M, 1408), _tile(N, 1408), _tile(K, 1408)
    nk = K // tk
    dims = (((0 if ta else 1,), (1 if tb else 0,)), ((), ()))

    def body(a_ref, b_ref, o_ref, acc_ref):
        k = pl.program_id(2)

        @pl.when(k == 0)
        def _():
            acc_ref[...] = jnp.zeros_like(acc_ref)

        acc_ref[...] += lax.dot_general(a_ref[...], b_ref[...], dims, preferred_element_type=F32)

        @pl.when(k == nk - 1)
        def _():
            o_ref[...] = acc_ref[...].astype(o_ref.dtype)

    a_spec = pl.BlockSpec((tk, tm), lambda i, j, k: (k, i)) if ta else pl.BlockSpec((tm, tk), lambda i, j, k: (i, k))
    if b_shards and tb:
        assert tk == FF_SHARD
        b_spec = pl.BlockSpec((None, tn, tk), lambda i, j, k: (k, j, 0))
    elif b_shards:
        assert tn == FF_SHARD
        b_spec = pl.BlockSpec((None, tk, tn), lambda i, j, k: (j, k, 0))
    else:
        b_spec = pl.BlockSpec((tn, tk), lambda i, j, k: (j, k)) if tb else pl.BlockSpec((tk, tn), lambda i, j, k: (k, j))
    if out_shards:
        assert tn == FF_SHARD
        out_spec = pl.BlockSpec((None, tm, tn), lambda i, j, k: (j, i, 0))
        out_shape = jax.ShapeDtypeStruct((N // tn, M, tn), out_dtype)
    else:
        out_spec = pl.BlockSpec((tm, tn), lambda i, j, k: (i, j))
        out_shape = jax.ShapeDtypeStruct((M, N), out_dtype)
    return pl.pallas_call(
        body, name=name, grid=(M // tm, N // tn, nk),
        in_specs=[a_spec, b_spec], out_specs=out_spec, out_shape=out_shape,
        scratch_shapes=[pltpu.VMEM((tm, tn), F32)],
        compiler_params=_params("parallel", "parallel", "arbitrary"),
    )(a, b)


def _seg_rms(xs, g):
    r = lax.rsqrt(jnp.mean(xs * xs, axis=-1, keepdims=True) + EPS)
    return xs * r * g


def _seg_rms_bwd(xs, g, dy):
    r = lax.rsqrt(jnp.mean(xs * xs, axis=-1, keepdims=True) + EPS)
    gd = dy * g
    c = jnp.mean(gd * xs, axis=-1, keepdims=True)
    dx = r * gd - xs * (r * r * r * c)
    dg = jnp.sum(dy * (xs * r), axis=0, keepdims=True)
    return dx, dg


def _rms_fwd(x, g, *, name):
    S, W = x.shape
    tm = _tile(S, 512)

    def body(x_ref, g_ref, o_ref):
        o_ref[...] = _seg_rms(x_ref[...], g_ref[...]).astype(o_ref.dtype)

    return pl.pallas_call(
        body, name=name, grid=(S // tm,),
        in_specs=[pl.BlockSpec((tm, W), lambda i: (i, 0)), pl.BlockSpec((1, W), lambda i: (0, 0))],
        out_specs=pl.BlockSpec((tm, W), lambda i: (i, 0)),
        out_shape=jax.ShapeDtypeStruct((S, W), BF16),
        compiler_params=_params("parallel"),
    )(x, g)


def _rms_bwd(x, g, dy, *, resid=None, out_dtype, name):
    S, W = x.shape
    tm = _tile(S, 512)
    has_resid = resid is not None

    def body(*refs):
        if has_resid:
            x_ref, g_ref, dy_ref, r_ref, dx_ref, dg_ref = refs
        else:
            x_ref, g_ref, dy_ref, dx_ref, dg_ref = refs

        @pl.when(pl.program_id(0) == 0)
        def _():
            dg_ref[...] = jnp.zeros_like(dg_ref)

        dx, dg = _seg_rms_bwd(x_ref[...], g_ref[...], dy_ref[...])
        if has_resid:
            dx = dx + r_ref[...]
        dx_ref[...] = dx.astype(dx_ref.dtype)
        dg_ref[...] += dg

    row = pl.BlockSpec((tm, W), lambda i: (i, 0))
    vec = pl.BlockSpec((1, W), lambda i: (0, 0))
    ins = [x, g, dy] + ([resid] if has_resid else [])
    return pl.pallas_call(
        body, name=name, grid=(S // tm,),
        in_specs=[row, vec, row] + ([row] if has_resid else []),
        out_specs=[row, vec],
        out_shape=[jax.ShapeDtypeStruct((S, W), out_dtype), jax.ShapeDtypeStruct((1, W), F32)],
        compiler_params=_params("arbitrary"),
    )(*ins)


def _resid_rms(x, y, g_post, g_next, *, name):
    S, W = x.shape
    tm = _tile(S, 512)
    with_next = g_next is not None

    def body(*refs):
        if with_next:
            x_ref, y_ref, gp_ref, gn_ref, xo_ref, h_ref = refs
        else:
            x_ref, y_ref, gp_ref, xo_ref = refs
        xn = x_ref[...] + _seg_rms(y_ref[...], gp_ref[...])
        xo_ref[...] = xn
        if with_next:
            h_ref[...] = _seg_rms(xn, gn_ref[...]).astype(BF16)

    row = pl.BlockSpec((tm, W), lambda i: (i, 0))
    vec = pl.BlockSpec((1, W), lambda i: (0, 0))
    outs = [jax.ShapeDtypeStruct((S, W), F32)] + ([jax.ShapeDtypeStruct((S, W), BF16)] if with_next else [])
    res = pl.pallas_call(
        body, name=name, grid=(S // tm,),
        in_specs=[row, row, vec] + ([vec] if with_next else []),
        out_specs=[row] + ([row] if with_next else []),
        out_shape=outs,
        compiler_params=_params("parallel"),
    )(*([x, y, g_post] + ([g_next] if with_next else [])))
    return (res[0], res[1]) if with_next else (res[0], None)


def _col_rms(xs, g):
    r = lax.rsqrt(jnp.mean(xs * xs, axis=0, keepdims=True) + EPS)
    return xs * r * g


def _col_rms_bwd(xs, g, dy):
    r = lax.rsqrt(jnp.mean(xs * xs, axis=0, keepdims=True) + EPS)
    gd = dy * g
    c = jnp.mean(gd * xs, axis=0, keepdims=True)
    dx = r * gd - xs * (r * r * r * c)
    dg = jnp.sum(dy * (xs * r), axis=1, keepdims=True)
    return dx, dg


GROUP_ROWS = (SWA_Q_HEADS * HEAD_DIM, FOX_HEADS * HEAD_DIM, MLA_HEADS * HEAD_DIM)


def _group_specs(S, tn):
    outs = [pl.BlockSpec((n, tn), lambda i: (0, i)) for n in GROUP_ROWS]
    g = pl.BlockSpec((D_MODEL, 1), lambda i: (0, 0))
    mixed = pl.BlockSpec((D_MODEL, tn), lambda i: (0, i))
    return outs, g, mixed


def _group_norm_fwd(oa, of, oc, g, *, name):
    S = oa.shape[1]
    tn = _tile(S, 512)
    outs, gs, mixed = _group_specs(S, tn)

    def body(a_ref, f_ref, c_ref, g_ref, o_ref):
        r0 = 0
        for ref, n in zip((a_ref, f_ref, c_ref), GROUP_ROWS):
            o_ref[r0:r0 + n, :] = _col_rms(ref[...], g_ref[r0:r0 + n, :]).astype(BF16)
            r0 += n

    return pl.pallas_call(
        body, name=name, grid=(S // tn,),
        in_specs=outs + [gs], out_specs=mixed,
        out_shape=jax.ShapeDtypeStruct((D_MODEL, S), BF16),
        compiler_params=_params("parallel"),
    )(oa, of, oc, g)


def _group_norm_bwd(oa, of, oc, g, dmixed, *, name):
    S = oa.shape[1]
    tn = _tile(S, 512)
    outs, gs, mixed = _group_specs(S, tn)
    n_heads = D_MODEL // HEAD_DIM

    def body(a_ref, f_ref, c_ref, g_ref, dm_ref, da_ref, df_ref, dc_ref, dg_ref, dl_ref):
        @pl.when(pl.program_id(0) == 0)
        def _():
            dg_ref[...] = jnp.zeros_like(dg_ref)

        r0 = 0
        for ref, dref, n in zip((a_ref, f_ref, c_ref), (da_ref, df_ref, dc_ref), GROUP_ROWS):
            o = ref[...]
            dx, dg = _col_rms_bwd(o, g_ref[r0:r0 + n, :], dm_ref[r0:r0 + n, :])
            dxb = dx.astype(BF16)
            dref[...] = dxb
            dg_ref[r0:r0 + n, :] += dg
            od = o * dxb.astype(F32)
            for h in range(n // HEAD_DIM):
                dl_ref[r0 // HEAD_DIM + h] = jnp.sum(od[h * HEAD_DIM:(h + 1) * HEAD_DIM, :], axis=0, keepdims=True)
            r0 += n

    return pl.pallas_call(
        body, name=name, grid=(S // tn,),
        in_specs=outs + [gs, mixed], out_specs=outs + [gs, pl.BlockSpec((n_heads, 1, tn), lambda i: (0, 0, i))],
        out_shape=[jax.ShapeDtypeStruct((n, S), BF16) for n in GROUP_ROWS] + [jax.ShapeDtypeStruct((D_MODEL, 1), F32),
                                                                              jax.ShapeDtypeStruct((n_heads, 1, S), F32)],
        compiler_params=_params("arbitrary"),
    )(oa, of, oc, g, dmixed)


def _loss_head(y, target):
    S, W = y.shape
    tm = _tile(S, 512)

    def body(y_ref, t_ref, d_ref, l_ref):
        @pl.when(pl.program_id(0) == 0)
        def _():
            l_ref[...] = jnp.zeros_like(l_ref)

        err = y_ref[...] - t_ref[...]
        d_ref[...] = err * (1.0 / W)
        l_ref[...] += 0.5 * jnp.sum(jnp.mean(err * err, axis=-1, keepdims=True), axis=0, keepdims=True)

    row = pl.BlockSpec((tm, W), lambda i: (i, 0))
    d, l = pl.pallas_call(
        body, name="loss_head", grid=(S // tm,),
        in_specs=[row, row],
        out_specs=[row, pl.BlockSpec((1, 1), lambda i: (0, 0))],
        out_shape=[jax.ShapeDtypeStruct((S, W), F32), jax.ShapeDtypeStruct((1, 1), F32)],
        compiler_params=_params("arbitrary"),
    )(y, target)
    return l[0, 0], d


def _attn_fwd(q_src, k_src, v_src, rows, H, Dk, Dv, scale, f_row=None, f_col=None, *, name):
    S = q_src.shape[1]
    T = _tile(S, ATT_TILE)
    nq = S // T
    forget = f_row is not None
    qb, kb, vb = rows[0] // (H * Dk), rows[1] // (H * Dk), rows[2] // (H * Dv)
    hs = range(H)

    def body(*refs):
        if forget:
            q_ref, k_ref, v_ref, fq_ref, fk_ref, o_ref, lse_ref = refs
        else:
            q_ref, k_ref, v_ref, o_ref, lse_ref = refs
        i = pl.program_id(0)

        def tile(j, masked, state):
            off = pl.multiple_of(j * T, T)
            ss = [_dot(k_ref[h * Dk:(h + 1) * Dk, pl.ds(off, T)], q_ref[h * Dk:(h + 1) * Dk, :], TN) * scale for h in hs]
            if forget:
                ss = [ss[h] + (fq_ref[h] - fk_ref[pl.ds(off, T), h:h + 1]) for h in hs]
            if masked:
                r = lax.broadcasted_iota(jnp.int32, (T, T), 0)
                c = lax.broadcasted_iota(jnp.int32, (T, T), 1)
                ss = [jnp.where(r <= c, s, NEG_INF) for s in ss]
            m_new = [jnp.maximum(state[h][0], jnp.max(ss[h], axis=0, keepdims=True)) for h in hs]
            alpha = [jnp.exp(state[h][0] - m_new[h]) for h in hs]
            ps = [jnp.exp(ss[h] - m_new[h]) for h in hs]
            l_new = [alpha[h] * state[h][1] + jnp.sum(ps[h], axis=0, keepdims=True) for h in hs]
            p_hi = [p.astype(BF16) for p in ps]
            vs = [v_ref[h * Dv:(h + 1) * Dv, pl.ds(off, T)] for h in hs]
            pv = [_dot(vs[h], p_hi[h]) for h in hs]
            if forget:
                pv = [pv[h] + _dot(vs[h], (ps[h] - p_hi[h].astype(F32)).astype(BF16)) for h in hs]
            return tuple((m_new[h], l_new[h], alpha[h] * state[h][2] + pv[h]) for h in hs)

        init = tuple((jnp.full((1, T), NEG_INF, F32), jnp.zeros((1, T), F32), jnp.zeros((Dv, T), F32)) for _ in hs)
        state = lax.fori_loop(0, i, lambda j, st: tile(j, False, st), init)
        state = tile(i, True, state)
        for h in hs:
            m, l, acc = state[h]
            o_ref[h * Dv:(h + 1) * Dv, :] = acc / l
            lse_ref[h] = m + jnp.log(l)

    in_specs = [pl.BlockSpec((H * Dk, T), lambda i: (qb, i)),
                pl.BlockSpec((H * Dk, S), lambda i: (kb, 0)),
                pl.BlockSpec((H * Dv, S), lambda i: (vb, 0))]
    ins = [q_src, k_src, v_src]
    if forget:
        in_specs += [pl.BlockSpec((H, 1, T), lambda i: (0, 0, i)), pl.BlockSpec((S, H), lambda i: (0, 0))]
        ins += [f_row, f_col]
    return pl.pallas_call(
        body, name=name, grid=(nq,),
        in_specs=in_specs,
        out_specs=[pl.BlockSpec((H * Dv, T), lambda i: (0, i)), pl.BlockSpec((H, 1, T), lambda i: (0, 0, i))],
        out_shape=[jax.ShapeDtypeStruct((H * Dv, S), F32), jax.ShapeDtypeStruct((H, 1, S), F32)],
        compiler_params=_params("parallel"),
    )(*ins)


def _attn_bwd(q_src, k_src, v_src, rows, H, Dk, Dv, scale, do, lse, delta, f_row=None, f_col=None, *, name):
    S = q_src.shape[1]
    T = _tile(S, ATT_TILE)
    nq = S // T
    forget = f_row is not None
    qb, kb, vb, db = rows[0] // (H * Dk), rows[1] // (H * Dk), rows[2] // (H * Dv), rows[3] // H
    hs = range(H)

    def body(*refs):
        if forget:
            (q_ref, k_ref, v_ref, do_ref, lse_ref, dl_ref, fq_ref, fk_ref,
             dq_ref, dk_ref, dv_ref, df_ref, dk_s, dv_s, df_s) = refs
        else:
            q_ref, k_ref, v_ref, do_ref, lse_ref, dl_ref, dq_ref, dk_ref, dv_ref, dk_s, dv_s = refs
        j = pl.program_id(0)

        @pl.when(j == 0)
        def _():
            dq_ref[...] = jnp.zeros_like(dq_ref)

        dk_s[...] = jnp.zeros_like(dk_s)
        dv_s[...] = jnp.zeros_like(dv_s)
        if forget:
            df_s[...] = jnp.zeros_like(df_s)
        kt = [k_ref[h * Dk:(h + 1) * Dk, :] for h in hs]
        kj = [k.T for k in kt]
        vj = [v_ref[h * Dv:(h + 1) * Dv, :].T for h in hs]
        koff = pl.multiple_of(j * T, T)

        def tile(i, masked):
            cols = pl.ds(pl.multiple_of(i * T, T), T)
            qi = [q_ref[h * Dk:(h + 1) * Dk, cols] for h in hs]
            doi = [do_ref[h * Dv:(h + 1) * Dv, cols] for h in hs]
            st = [_dot(kj[h], qi[h]) * scale for h in hs]
            if forget:
                st = [st[h] + (fq_ref[h, :, cols] - fk_ref[pl.ds(koff, T), h:h + 1]) for h in hs]
            if masked:
                r = lax.broadcasted_iota(jnp.int32, (T, T), 0)
                c = lax.broadcasted_iota(jnp.int32, (T, T), 1)
                st = [jnp.where(r <= c, x, NEG_INF) for x in st]
            pt = [jnp.exp(st[h] - lse_ref[h, :, cols]) for h in hs]
            dpt = [_dot(vj[h], doi[h]) for h in hs]
            dst = [pt[h] * (dpt[h] - dl_ref[h, :, cols]) for h in hs]
            ptb = [p.astype(BF16) for p in pt]
            dsb = [d.astype(BF16) for d in dst]
            for h in hs:
                dv_s[h * Dv:(h + 1) * Dv, :] += _dot(doi[h], ptb[h], NT)
            for h in hs:
                dk_s[h * Dk:(h + 1) * Dk, :] += _dot(qi[h], dsb[h], NT)
            for h in hs:
                dq_ref[h * Dk:(h + 1) * Dk, cols] += _dot(kt[h], dsb[h]) * scale
            if forget:
                for h in hs:
                    part = dst[h][:, 0:LANES]
                    for c0 in range(LANES, T, LANES):
                        part = part + dst[h][:, c0:c0 + LANES]
                    df_s[h] += part

        tile(j, True)

        def loop_body(i, carry):
            tile(i, False)
            return carry

        lax.fori_loop(j + 1, nq, loop_body, 0)
        dk_ref[...] = dk_s[...] * scale
        dv_ref[...] = dv_s[...]
        if forget:
            df_ref[...] = jnp.concatenate([-jnp.sum(df_s[h], axis=-1, keepdims=True) for h in hs], axis=1)

    res = lambda D, b0: pl.BlockSpec((H * D, S), lambda j: (b0, 0))
    blk = lambda D, b0: pl.BlockSpec((H * D, T), lambda j: (b0, j))
    row3 = lambda b0: pl.BlockSpec((H, 1, S), lambda j: (b0, 0, 0))
    in_specs = [res(Dk, qb), blk(Dk, kb), blk(Dv, vb), res(Dv, 0), row3(0), row3(db)]
    ins = [q_src, k_src, v_src, do, lse, delta]
    out_specs = [res(Dk, 0), blk(Dk, 0), blk(Dv, 0)]
    out_shape = [jax.ShapeDtypeStruct((H * Dk, S), F32), jax.ShapeDtypeStruct((H * Dk, S), F32),
                 jax.ShapeDtypeStruct((H * Dv, S), F32)]
    scratch = [pltpu.VMEM((H * Dk, T), F32), pltpu.VMEM((H * Dv, T), F32)]
    if forget:
        in_specs += [row3(0), pl.BlockSpec((S, H), lambda j: (0, 0))]
        ins += [f_row, f_col]
        out_specs.append(pl.BlockSpec((T, H), lambda j: (j, 0)))
        out_shape.append(jax.ShapeDtypeStruct((S, H), F32))
        scratch.append(pltpu.VMEM((H, T, min(T, LANES)), F32))
    return pl.pallas_call(
        body, name=name, grid=(nq,),
        in_specs=in_specs, out_specs=out_specs, out_shape=out_shape, scratch_shapes=scratch,
        compiler_params=_params("arbitrary"),
    )(*ins)


def _swa_masks(i):
    r = lax.broadcasted_iota(jnp.int32, (WINDOW, WINDOW), 0)
    c = lax.broadcasted_iota(jnp.int32, (WINDOW, WINDOW), 1)
    return (r > c) & (i > 0), r <= c


def _swa_specs():
    W = WINDOW
    kv_rows = SWA_KV_HEADS * HEAD_DIM
    q = pl.BlockSpec((SWA_Q_HEADS * HEAD_DIM, W), lambda i: (0, i))
    prev = lambda b: pl.BlockSpec((kv_rows, W), lambda i: (b, jnp.maximum(i - 1, 0)))
    cur = lambda b: pl.BlockSpec((kv_rows, W), lambda i: (b, i))
    bias = pl.BlockSpec((SWA_Q_HEADS, 2 * W, W), lambda i: (0, 0, 0))
    stat = pl.BlockSpec((SWA_Q_HEADS, W), lambda i: (0, i))
    sink = pl.BlockSpec(memory_space=pltpu.SMEM)
    return q, prev(4), cur(4), prev(5), cur(5), bias, stat, sink


def _swa_scores(h, q_ref, kp_ref, kc_ref, b_ref, masks):
    g = h // SWA_GROUP
    rows = slice(g * HEAD_DIM, (g + 1) * HEAD_DIM)
    qh = q_ref[h * HEAD_DIM:(h + 1) * HEAD_DIM, :]
    scale = HEAD_DIM ** -0.5
    s_p = jnp.where(masks[0], _dot(kp_ref[rows, :], qh, TN) * scale + b_ref[h, 0:WINDOW, :], NEG_INF)
    s_c = jnp.where(masks[1], _dot(kc_ref[rows, :], qh, TN) * scale + b_ref[h, WINDOW:2 * WINDOW, :], NEG_INF)
    return qh, rows, s_p, s_c


def _swa_fwd(qkv, bias_t, sinks, *, name):
    S = qkv.shape[1]
    qs, kp, kc, vp, vc, bs, stat, sk = _swa_specs()

    def body(sink_ref, q_ref, kp_ref, kc_ref, vp_ref, vc_ref, b_ref, o_ref, lse_ref):
        masks = _swa_masks(pl.program_id(0))
        for h in range(SWA_Q_HEADS):
            qh, rows, s_p, s_c = _swa_scores(h, q_ref, kp_ref, kc_ref, b_ref, masks)
            sink = sink_ref[h]
            m = jnp.maximum(jnp.maximum(jnp.max(s_p, axis=0, keepdims=True), jnp.max(s_c, axis=0, keepdims=True)), sink)
            p_p = jnp.exp(s_p - m)
            p_c = jnp.exp(s_c - m)
            l = jnp.sum(p_p, axis=0, keepdims=True) + jnp.sum(p_c, axis=0, keepdims=True) + jnp.exp(sink - m)
            o = _dot(vp_ref[rows, :], p_p.astype(BF16)) + _dot(vc_ref[rows, :], p_c.astype(BF16))
            o_ref[h * HEAD_DIM:(h + 1) * HEAD_DIM, :] = o / l
            lse_ref[h:h + 1, :] = m + jnp.log(l)

    return pl.pallas_call(
        body, name=name, grid=(S // WINDOW,),
        in_specs=[sk, qs, kp, kc, vp, vc, bs],
        out_specs=[qs, stat],
        out_shape=[jax.ShapeDtypeStruct((SWA_Q_HEADS * HEAD_DIM, S), F32), jax.ShapeDtypeStruct((SWA_Q_HEADS, S), F32)],
        compiler_params=_params("parallel"),
    )(sinks, qkv, qkv, qkv, qkv, qkv, bias_t)


def _swa_bwd(qkv, bias_t, sinks, do, lse, delta, *, name):
    S = qkv.shape[1]
    W = WINDOW
    qs, kp, kc, vp, vc, bs, stat, sk = _swa_specs()
    scale = HEAD_DIM ** -0.5
    kv_rows = SWA_KV_HEADS * HEAD_DIM

    def body(sink_ref, q_ref, kp_ref, kc_ref, vp_ref, vc_ref, b_ref, do_ref, lse_ref, dl_ref,
             dq_ref, dkv_ref, db_ref, dsk_ref):
        i = pl.program_id(0)

        @pl.when(i == 0)
        def _():
            dkv_ref[...] = jnp.zeros_like(dkv_ref)
            db_ref[...] = jnp.zeros_like(db_ref)
            dsk_ref[...] = jnp.zeros_like(dsk_ref)

        masks = _swa_masks(i)
        prev = pl.ds(pl.multiple_of(jnp.maximum(i - 1, 0) * W, W), W)
        cur = pl.ds(pl.multiple_of(i * W, W), W)
        for h in range(SWA_Q_HEADS):
            qh, rows, s_p, s_c = _swa_scores(h, q_ref, kp_ref, kc_ref, b_ref, masks)
            vrows = slice(kv_rows + rows.start, kv_rows + rows.stop)
            hrows = slice(h * HEAD_DIM, (h + 1) * HEAD_DIM)
            doh = do_ref[hrows, :]
            lse_h = lse_ref[h:h + 1, :]
            delta = dl_ref[h:h + 1, :]
            p_p = jnp.exp(s_p - lse_h)
            p_c = jnp.exp(s_c - lse_h)
            ds_p = p_p * (_dot(vp_ref[rows, :], doh, TN) - delta)
            ds_c = p_c * (_dot(vc_ref[rows, :], doh, TN) - delta)
            db_ref[h, 0:W, :] += ds_p
            db_ref[h, W:2 * W, :] += ds_c
            dsk = -jnp.sum(jnp.exp(sink_ref[h] - lse_h) * delta, axis=1, keepdims=True)
            dsk_ref[h:h + 1, :] += jnp.broadcast_to(dsk, (1, LANES))
            dsb_p = ds_p.astype(BF16)
            dsb_c = ds_c.astype(BF16)
            dq_ref[hrows, :] = (_dot(kp_ref[rows, :], dsb_p) + _dot(kc_ref[rows, :], dsb_c)) * scale
            dkv_ref[rows, prev] += _dot(qh, dsb_p, NT) * scale
            dkv_ref[rows, cur] += _dot(qh, dsb_c, NT) * scale
            dkv_ref[vrows, prev] += _dot(doh, p_p.astype(BF16), NT)
            dkv_ref[vrows, cur] += _dot(doh, p_c.astype(BF16), NT)

    return pl.pallas_call(
        body, name=name, grid=(S // W,),
        in_specs=[sk, qs, kp, kc, vp, vc, bs, qs, stat, stat],
        out_specs=[qs, pl.BlockSpec((2 * kv_rows, S), lambda i: (0, 0)), bs, pl.BlockSpec((SWA_Q_HEADS, LANES), lambda i: (0, 0))],
        out_shape=[jax.ShapeDtypeStruct((SWA_Q_HEADS * HEAD_DIM, S), F32), jax.ShapeDtypeStruct((2 * kv_rows, S), F32),
                   jax.ShapeDtypeStruct((SWA_Q_HEADS, 2 * W, W), F32), jax.ShapeDtypeStruct((SWA_Q_HEADS, LANES), F32)],
        compiler_params=_params("arbitrary"),
    )(sinks, qkv, qkv, qkv, qkv, qkv, bias_t, do, lse, delta)


def _rel_onehot_t():
    qi = jnp.arange(WINDOW, dtype=jnp.int32)[None, :] + WINDOW
    kj = jnp.arange(2 * WINDOW, dtype=jnp.int32)[:, None]
    dist = qi - kj
    max_exact = REL_BUCKETS // 2
    d = jnp.maximum(dist, 0)
    log_ratio = jnp.log(jnp.maximum(d, 1).astype(F32) / max_exact) / math.log(REL_MAX_DIST / max_exact)
    large = jnp.minimum(max_exact + (log_ratio * (REL_BUCKETS - max_exact)).astype(jnp.int32), REL_BUCKETS - 1)
    bucket = jnp.where(d < max_exact, d, large).reshape(-1)
    return (bucket[None, :] == jnp.arange(REL_BUCKETS, dtype=jnp.int32)[:, None]).astype(BF16)


def _bias_table(rel_bias_t, onehot_t):
    Hq, NB = rel_bias_t.shape
    N = onehot_t.shape[1]
    tn = _tile(N, 4096)

    def body(r_ref, oh_ref, o_ref):
        oh = oh_ref[...]
        a1, a2, a3 = _split3(r_ref[...])
        o_ref[...] = _dot(a1, oh) + _dot(a2, oh) + _dot(a3, oh)

    return pl.pallas_call(
        body, name="rel_bias_table", grid=(N // tn,),
        in_specs=[pl.BlockSpec((Hq, NB), lambda j: (0, 0)), pl.BlockSpec((NB, tn), lambda j: (0, j))],
        out_specs=pl.BlockSpec((Hq, tn), lambda j: (0, j)),
        out_shape=jax.ShapeDtypeStruct((Hq, N), F32),
        compiler_params=_params("parallel"),
    )(rel_bias_t, onehot_t)


def _bias_table_bwd(dbias, onehot_t):
    L, Hq, N = dbias.shape
    NB = onehot_t.shape[0]
    tn = _tile(N, 4096)

    def body(d_ref, oh_ref, o_ref):
        @pl.when(pl.program_id(0) == 0)
        def _():
            o_ref[...] = jnp.zeros_like(o_ref)

        d = d_ref[0]
        for l in range(1, L):
            d = d + d_ref[l]
        oh = oh_ref[...]
        a1, a2, a3 = _split3(d)
        o_ref[...] += _dot(a1, oh, NT) + _dot(a2, oh, NT) + _dot(a3, oh, NT)

    return pl.pallas_call(
        body, name="rel_bias_bwd", grid=(N // tn,),
        in_specs=[pl.BlockSpec((L, Hq, tn), lambda j: (0, 0, j)), pl.BlockSpec((NB, tn), lambda j: (0, j))],
        out_specs=pl.BlockSpec((Hq, NB), lambda j: (0, 0)),
        out_shape=jax.ShapeDtypeStruct((Hq, NB), F32),
        compiler_params=_params("arbitrary"),
    )(dbias, onehot_t)


def _gate_fwd(lat, fb_col, *, name):
    S = lat.shape[1]
    tn = _tile(S, 256)

    def body(z_ref, fb_ref, o_ref, carry):
        @pl.when(pl.program_id(0) == 0)
        def _():
            carry[...] = jnp.zeros_like(carry)

        z = z_ref[...] + fb_ref[...]
        lf = jnp.minimum(z, 0.0) - jnp.log1p(jnp.exp(-jnp.abs(z)))
        r = lax.broadcasted_iota(jnp.int32, (tn, tn), 0)
        c = lax.broadcasted_iota(jnp.int32, (tn, tn), 1)
        tri = (r <= c).astype(BF16)
        a1, a2, a3 = _split3(lf)
        cum = _dot(a1, tri) + _dot(a2, tri) + _dot(a3, tri) + carry[:, 0:1]
        o_ref[...] = cum
        carry[...] = jnp.broadcast_to(cum[:, tn - 1:tn], carry.shape)

    return pl.pallas_call(
        body, name=name, grid=(S // tn,),
        in_specs=[pl.BlockSpec((GATE_ROWS, tn), lambda i: (0, i)), pl.BlockSpec((GATE_ROWS, 1), lambda i: (0, 0))],
        out_specs=pl.BlockSpec((GATE_ROWS, tn), lambda i: (0, i)),
        out_shape=jax.ShapeDtypeStruct((GATE_ROWS, S), F32),
        scratch_shapes=[pltpu.VMEM((GATE_ROWS, LANES), F32)],
        compiler_params=_params("arbitrary"),
    )(lat, fb_col)


def _gate_bwd(lat, fb_col, dF, *, name):
    S = lat.shape[1]
    tn = _tile(S, 256)
    nt = S // tn

    def body(z_ref, fb_ref, df_ref, dz_ref, dfb_ref, carry):
        @pl.when(pl.program_id(0) == 0)
        def _():
            carry[...] = jnp.zeros_like(carry)
            dfb_ref[...] = jnp.zeros_like(dfb_ref)

        r = lax.broadcasted_iota(jnp.int32, (tn, tn), 0)
        c = lax.broadcasted_iota(jnp.int32, (tn, tn), 1)
        tri = (r >= c).astype(BF16)
        a1, a2, a3 = _split3(df_ref[...])
        dlf = _dot(a1, tri) + _dot(a2, tri) + _dot(a3, tri) + carry[:, 0:1]
        carry[...] = jnp.broadcast_to(dlf[:, 0:1], carry.shape)
        z = z_ref[...] + fb_ref[...]
        row = lax.broadcasted_iota(jnp.int32, (GATE_ROWS, tn), 0)
        dz = jnp.where(row < FOX_HEADS, dlf / (1.0 + jnp.exp(z)), 0.0)
        dz_ref[...] = dz
        dfb_ref[...] += jnp.sum(dz, axis=1, keepdims=True)

    blk = pl.BlockSpec((GATE_ROWS, tn), lambda i: (0, nt - 1 - i))
    vec = pl.BlockSpec((GATE_ROWS, 1), lambda i: (0, 0))
    return pl.pallas_call(
        body, name=name, grid=(nt,),
        in_specs=[blk, vec, blk], out_specs=[blk, vec],
        out_shape=[jax.ShapeDtypeStruct((GATE_ROWS, S), F32), jax.ShapeDtypeStruct((GATE_ROWS, 1), F32)],
        scratch_shapes=[pltpu.VMEM((GATE_ROWS, LANES), F32)],
        compiler_params=_params("arbitrary"),
    )(lat, fb_col, dF)


def _rope_tables(S):
    pos = jnp.arange(S, dtype=F32)
    inv_freq = ROPE_THETA ** (-(jnp.arange(MLA_ROPE // 2, dtype=F32) * 2.0 / MLA_ROPE))
    ang = pos[:, None] * inv_freq[None, :]
    cos, sin = jnp.cos(ang).T, jnp.sin(ang).T
    z16 = jnp.zeros_like(cos)

    def slab(lo, fill):
        def put(first, second, f):
            return jnp.concatenate([jnp.full((lo, S), f, F32), first, second, jnp.full((LANES - lo - MLA_ROPE, S), f, F32)], axis=0)
        return put(cos, cos, fill), put(-sin, z16, 0.0), put(z16, sin, 0.0)

    tq = tuple(jnp.tile(t, (MLA_HEADS, 1)) for t in slab(MLA_NOPE, 1.0))
    return tq, slab(0, 0.0)


def _rope(x, c, s1, s2):
    n = x.shape[0]
    half = MLA_ROPE // 2
    return x * c + pltpu.roll(x, n - half, 0) * s1 + pltpu.roll(x, half, 0) * s2


def _rope_t(dy, c, s1, s2):
    n = dy.shape[0]
    half = MLA_ROPE // 2
    return dy * c + pltpu.roll(dy * s1, half, 0) + pltpu.roll(dy * s2, n - half, 0)


KR_SLAB0 = MLA_Q_RANK + MLA_KV_RANK


def _mla_prep_fwd(lat, g_q, g_kv, w_uq_t, w_ukv_t, tq, tmisc, *, name):
    S = lat.shape[1]
    tn = _tile(S, 512)
    QW = MLA_HEADS * MLA_PAD

    def body(lat_ref, gq_ref, gkv_ref, wq_ref, wkv_ref, c_ref, s1_ref, s2_ref, cm_ref, s1m_ref, s2m_ref,
             nq_ref, nkv_ref, q_ref, k_ref, v_ref):
        x = pltpu.roll(lat_ref[...], LAT_ROWS - LAT_SHIFT, 0)
        nq = _col_rms(x[0:MLA_Q_RANK, :], gq_ref[...]).astype(BF16)
        nkv = _col_rms(x[MLA_Q_RANK:KR_SLAB0, :], gkv_ref[...]).astype(BF16)
        nq_ref[...] = nq
        nkv_ref[...] = nkv
        q_ref[...] = _rope(_dot(wq_ref[...], nq), c_ref[...], s1_ref[...], s2_ref[...]).astype(BF16)
        kv = _dot(wkv_ref[...], nkv).astype(BF16)
        kr = _rope(x[KR_SLAB0:LAT_ROWS, :], cm_ref[...], s1m_ref[...], s2m_ref[...]).astype(BF16)
        for h in range(MLA_HEADS):
            k_ref[h * MLA_PAD:h * MLA_PAD + MLA_NOPE, :] = kv[h * LANES:h * LANES + MLA_NOPE, :]
            k_ref[h * MLA_PAD + MLA_NOPE:(h + 1) * MLA_PAD, :] = kr[0:MLA_PAD - MLA_NOPE, :]
            v_ref[h * HEAD_DIM:(h + 1) * HEAD_DIM, :] = kv[h * LANES + MLA_NOPE:(h + 1) * LANES, :]

    def col(rows):
        return pl.BlockSpec((rows, tn), lambda i: (0, i))

    def full(a):
        return pl.BlockSpec(a.shape, lambda i: (0, 0))

    return pl.pallas_call(
        body, name=name, grid=(S // tn,),
        in_specs=[col(LAT_ROWS), full(g_q), full(g_kv), full(w_uq_t), full(w_ukv_t),
                  col(QW), col(QW), col(QW), col(LANES), col(LANES), col(LANES)],
        out_specs=[col(MLA_Q_RANK), col(MLA_KV_RANK), col(QW), col(QW), col(MLA_HEADS * HEAD_DIM)],
        out_shape=[jax.ShapeDtypeStruct((MLA_Q_RANK, S), BF16), jax.ShapeDtypeStruct((MLA_KV_RANK, S), BF16),
                   jax.ShapeDtypeStruct((QW, S), BF16), jax.ShapeDtypeStruct((QW, S), BF16),
                   jax.ShapeDtypeStruct((MLA_HEADS * HEAD_DIM, S), BF16)],
        compiler_params=_params("parallel"),
    )(lat, g_q, g_kv, w_uq_t, w_ukv_t, *tq, *tmisc)


def _mla_prep_bwd(lat, nq, nkv, g_q, g_kv, w_uq_p, w_ukv, tq, tmisc, dq, dk, dv, dflog, *, name):
    S = lat.shape[1]
    tn = _tile(S, 512)
    QW = MLA_HEADS * MLA_PAD

    def body(lat_ref, nq_ref, nkv_ref, gq_ref, gkv_ref, wq_ref, wkv_ref, c_ref, s1_ref, s2_ref,
             cm_ref, s1m_ref, s2m_ref, dq_ref, dk_ref, dv_ref, dfl_ref,
             dlat_ref, dwq_ref, dwkv_ref, dgq_ref, dgkv_ref, y_s):
        @pl.when(pl.program_id(0) == 0)
        def _():
            dwq_ref[...] = jnp.zeros_like(dwq_ref)
            dwkv_ref[...] = jnp.zeros_like(dwkv_ref)
            dgq_ref[...] = jnp.zeros_like(dgq_ref)
            dgkv_ref[...] = jnp.zeros_like(dgkv_ref)

        x = pltpu.roll(lat_ref[...], LAT_ROWS - LAT_SHIFT, 0)
        dqm = _rope_t(dq_ref[...], c_ref[...], s1_ref[...], s2_ref[...]).astype(BF16)
        dwq_ref[...] += _dot(dqm, nq_ref[...], NT)
        dx, dg = _col_rms_bwd(x[0:MLA_Q_RANK, :], gq_ref[...], _dot(wq_ref[...], dqm))
        y_s[0:MLA_Q_RANK, :] = dx
        dgq_ref[...] += dg
        dkv = jnp.concatenate(
            [part for h in range(MLA_HEADS)
             for part in (dk_ref[h * MLA_PAD:h * MLA_PAD + MLA_NOPE, :], dv_ref[h * HEAD_DIM:(h + 1) * HEAD_DIM, :])],
            axis=0).astype(BF16)
        dwkv_ref[...] += _dot(dkv, nkv_ref[...], NT)
        dx, dg = _col_rms_bwd(x[MLA_Q_RANK:KR_SLAB0, :], gkv_ref[...], _dot(wkv_ref[...], dkv))
        y_s[MLA_Q_RANK:KR_SLAB0, :] = dx
        dgkv_ref[...] += dg
        dkr = dk_ref[MLA_NOPE:MLA_PAD, :]
        for h in range(1, MLA_HEADS):
            dkr = dkr + dk_ref[h * MLA_PAD + MLA_NOPE:(h + 1) * MLA_PAD, :]
        dkr = jnp.concatenate([dkr, jnp.zeros((MLA_NOPE, tn), F32)], axis=0)
        y_s[KR_SLAB0:LAT_ROWS, :] = _rope_t(dkr, cm_ref[...], s1m_ref[...], s2m_ref[...])
        y = pltpu.roll(y_s[...], LAT_SHIFT, 0)
        row = lax.broadcasted_iota(jnp.int32, (LAT_ROWS, tn), 0)
        dfl = jnp.concatenate([dfl_ref[...], jnp.zeros((LAT_ROWS - GATE_ROWS, tn), F32)], axis=0)
        dlat_ref[...] = jnp.where(row < LAT_SHIFT, dfl, y).astype(BF16)

    def col(rows):
        return pl.BlockSpec((rows, tn), lambda i: (0, i))

    def full(a):
        return pl.BlockSpec(a.shape, lambda i: (0, 0))

    def acc(r, c):
        return pl.BlockSpec((r, c), lambda i: (0, 0))

    return pl.pallas_call(
        body, name=name, grid=(S // tn,),
        in_specs=[col(LAT_ROWS), col(MLA_Q_RANK), col(MLA_KV_RANK), full(g_q), full(g_kv),
                  full(w_uq_p), full(w_ukv), col(QW), col(QW), col(QW), col(LANES), col(LANES), col(LANES),
                  col(QW), col(QW), col(MLA_HEADS * HEAD_DIM), col(GATE_ROWS)],
        out_specs=[col(LAT_ROWS), acc(QW, MLA_Q_RANK), acc(QW, MLA_KV_RANK), acc(MLA_Q_RANK, 1), acc(MLA_KV_RANK, 1)],
        out_shape=[jax.ShapeDtypeStruct((LAT_ROWS, S), BF16), jax.ShapeDtypeStruct((QW, MLA_Q_RANK), F32),
                   jax.ShapeDtypeStruct((QW, MLA_KV_RANK), F32), jax.ShapeDtypeStruct((MLA_Q_RANK, 1), F32),
                   jax.ShapeDtypeStruct((MLA_KV_RANK, 1), F32)],
        scratch_shapes=[pltpu.VMEM((LAT_ROWS, tn), F32)],
        compiler_params=_params("arbitrary"),
    )(lat, nq, nkv, g_q, g_kv, w_uq_p, w_ukv, *tq, *tmisc, dq, dk, dv, dflog)


def _dproj_cast(dqa, dkva, dqf, dkf, dvf, dlat, *, name):
    S = dqa.shape[1]
    tn = _tile(S, 512)
    parts = (dqa, dkva, dqf, dkf, dvf, dlat)

    def body(*refs):
        o_ref = refs[-1]
        r0 = 0
        for ref in refs[:-1]:
            n = ref.shape[0]
            o_ref[r0:r0 + n, :] = ref[...].astype(BF16)
            r0 += n

    return pl.pallas_call(
        body, name=name, grid=(S // tn,),
        in_specs=[pl.BlockSpec((p.shape[0], tn), lambda i: (0, i)) for p in parts],
        out_specs=pl.BlockSpec((IN_ROWS, tn), lambda i: (0, i)),
        out_shape=jax.ShapeDtypeStruct((IN_ROWS, S), BF16),
        compiler_params=_params("parallel"),
    )(*parts)


GELU_C = math.sqrt(2.0 / math.pi)
GELU_A = 0.044715


def _conv_taps(a, halo, w_ref, b_ref, first):
    row = lax.broadcasted_iota(jnp.int32, a.shape, 0)
    h7 = jnp.where(first, 0.0, halo[7:8, :])
    h6 = jnp.where(first, 0.0, halo[6:7, :])
    a1 = jnp.where(row == 0, h7, pltpu.roll(a, 1, 0))
    a2 = jnp.where(row == 0, h6, jnp.where(row == 1, h7, pltpu.roll(a, 2, 0)))
    u = ((b_ref[...] + w_ref[0:1, :] * a2) + w_ref[1:2, :] * a1) + w_ref[2:3, :] * a
    return u, a1, a2


def _conv_specs(S, tm, tc, nc):
    hb = tm // 8
    main = lambda off: pl.BlockSpec((tm, tc), lambda j, i: (i, j + off))
    halo = lambda off: pl.BlockSpec((8, tc), lambda j, i: (jnp.maximum(i * hb - 1, 0), j + off))
    wspec = lambda off: pl.BlockSpec((3, tc), lambda j, i: (0, j + off))
    bspec = lambda off: pl.BlockSpec((1, tc), lambda j, i: (0, j + off))
    return main, halo, wspec, bspec


def _conv_geglu_fwd(a, conv_w, conv_b, *, name):
    S = a.shape[0]
    tm, tc = _tile(S, 512), _tile(D_FF, 1408)
    nc = D_FF // tc
    main, halo, wspec, bspec = _conv_specs(S, tm, tc, nc)

    def body(ag_ref, au_ref, hg_ref, hu_ref, wg_ref, wu_ref, bg_ref, bu_ref, z_ref):
        first = pl.program_id(1) == 0
        gate, _, _ = _conv_taps(ag_ref[...], hg_ref[...], wg_ref, bg_ref, first)
        up, _, _ = _conv_taps(au_ref[...], hu_ref[...], wu_ref, bu_ref, first)
        cdf = 0.5 * (1.0 + jnp.tanh(GELU_C * (gate + GELU_A * (gate * gate * gate))))
        z_ref[...] = (gate * cdf * up).astype(BF16)

    return pl.pallas_call(
        body, name=name, grid=(nc, S // tm),
        in_specs=[main(0), main(nc), halo(0), halo(nc), wspec(0), wspec(nc), bspec(0), bspec(nc)],
        out_specs=pl.BlockSpec((tm, tc), lambda j, i: (i, j)),
        out_shape=jax.ShapeDtypeStruct((S, D_FF), BF16),
        compiler_params=_params("parallel", "arbitrary"),
    )(a, a, a, a, conv_w, conv_w, conv_b, conv_b)


def _conv_geglu_bwd(a, conv_w, conv_b, dz, *, name):
    S = a.shape[0]
    tm, tc = _tile(S, 512), _tile(D_FF, 1408)
    nc = D_FF // tc
    main, halo, wspec, bspec = _conv_specs(S, tm, tc, nc)

    def body(ag_ref, au_ref, hg_ref, hu_ref, wg_ref, wu_ref, bg_ref, bu_ref, dz_ref, du_ref, dw_ref, db_ref):
        first = pl.program_id(1) == 0

        @pl.when(first)
        def _():
            dw_ref[...] = jnp.zeros_like(dw_ref)
            db_ref[...] = jnp.zeros_like(db_ref)

        gate, g1, g2 = _conv_taps(ag_ref[...], hg_ref[...], wg_ref, bg_ref, first)
        up, u1, u2 = _conv_taps(au_ref[...], hu_ref[...], wu_ref, bu_ref, first)
        dz = dz_ref[...]
        g2x = gate * gate
        th = jnp.tanh(GELU_C * (gate + GELU_A * (g2x * gate)))
        cdf = 0.5 * (1.0 + th)
        dgelu = cdf + gate * (0.5 * (1.0 - th * th) * (GELU_C * (1.0 + 3.0 * GELU_A * g2x)))
        dug = dz * up * dgelu
        duu = dz * (gate * cdf)
        du_ref[0] = dug
        du_ref[1] = duu
        for half, du, taps in ((0, dug, (g2, g1, ag_ref[...])), (1, duu, (u2, u1, au_ref[...]))):
            for tap in range(3):
                dw_ref[half, tap:tap + 1, :] += jnp.sum(du * taps[tap], axis=0, keepdims=True)
            db_ref[half] += jnp.sum(du, axis=0, keepdims=True)

    return pl.pallas_call(
        body, name=name, grid=(nc, S // tm),
        in_specs=[main(0), main(nc), halo(0), halo(nc), wspec(0), wspec(nc), bspec(0), bspec(nc),
                  pl.BlockSpec((tm, tc), lambda j, i: (i, j))],
        out_specs=[pl.BlockSpec((2, tm, tc), lambda j, i: (0, i, j)), pl.BlockSpec((2, 3, tc), lambda j, i: (0, 0, j)),
                   pl.BlockSpec((2, 1, tc), lambda j, i: (0, 0, j))],
        out_shape=[jax.ShapeDtypeStruct((2, S, D_FF), F32), jax.ShapeDtypeStruct((2, 3, D_FF), F32),
                   jax.ShapeDtypeStruct((2, 1, D_FF), F32)],
        compiler_params=_params("parallel", "arbitrary"),
    )(a, a, a, a, conv_w, conv_w, conv_b, conv_b, dz)


def _conv_bwd_input(du, conv_w, *, name):
    S = du.shape[1]
    tm, tc = _tile(S, 512), _tile(D_FF, 1408)
    nc = D_FF // tc
    nr = S // tm
    hb = tm // 8

    def body(du_ref, nx_ref, w_ref, da_ref):
        last = pl.program_id(2) == nr - 1
        d = du_ref[0]
        row = lax.broadcasted_iota(jnp.int32, d.shape, 0)
        n0 = jnp.where(last, 0.0, nx_ref[0, 0:1, :])
        n1 = jnp.where(last, 0.0, nx_ref[0, 1:2, :])
        d1 = jnp.where(row == tm - 1, n0, pltpu.roll(d, tm - 1, 0))
        d2 = jnp.where(row == tm - 1, n1, jnp.where(row == tm - 2, n0, pltpu.roll(d, tm - 2, 0)))
        da_ref[...] = (w_ref[2:3, :] * d + w_ref[1:2, :] * d1 + w_ref[0:1, :] * d2).astype(BF16)

    return pl.pallas_call(
        body, name=name, grid=(2, nc, nr),
        in_specs=[pl.BlockSpec((1, tm, tc), lambda h, j, i: (h, i, j)),
                  pl.BlockSpec((1, 8, tc), lambda h, j, i: (h, jnp.minimum((i + 1) * hb, S // 8 - 1), j)),
                  pl.BlockSpec((3, tc), lambda h, j, i: (0, h * nc + j))],
        out_specs=pl.BlockSpec((tm, tc), lambda h, j, i: (i, h * nc + j)),
        out_shape=jax.ShapeDtypeStruct((S, 2 * D_FF), BF16),
        compiler_params=_params("parallel", "parallel", "arbitrary"),
    )(du, du, conv_w)


def _adamw_update(w, g, m, v):
    m = ADAM_B1 * m + (1.0 - ADAM_B1) * g
    v = ADAM_B2 * v + (1.0 - ADAM_B2) * jnp.square(g)
    m_hat = m / (1.0 - ADAM_B1 ** ADAM_STEP)
    v_hat = v / (1.0 - ADAM_B2 ** ADAM_STEP)
    return -ADAM_LR * (m_hat / (jnp.sqrt(v_hat) + ADAM_EPS) + ADAM_WD * w), m, v


def _adamw(w, g, m, v, *, name):
    L, A, B = w.shape
    ta = _tile(A, ROW_TILE)

    def body(w_ref, g_ref, m_ref, v_ref, d_ref, mo_ref, vo_ref):
        d_ref[...], mo_ref[...], vo_ref[...] = _adamw_update(w_ref[...], g_ref[...], m_ref[...], v_ref[...])

    blk = pl.BlockSpec((None, ta, B), lambda l, i: (l, i, 0))
    shp = jax.ShapeDtypeStruct((L, A, B), F32)
    return pl.pallas_call(
        body, name=name, grid=(L, A // ta),
        in_specs=[blk] * 4, out_specs=[blk] * 3, out_shape=[shp] * 3,
        compiler_params=_params("parallel", "parallel"),
    )(w, g, m, v)


def _scalar(v):
    return jnp.reshape(v, (1,)).astype(jnp.int32)


def _adamw_halves(w, g_mine, g_other, m, v, *, name):
    L, A, B = w.shape
    ta = _tile(A // 2, ROW_TILE)
    nb = A // 2 // ta

    def body(c_ref, w_ref, gm_ref, go_ref, m_ref, v_ref, g_ref, d_ref, mo_ref, vo_ref):
        g = jnp.where(pl.program_id(1) // nb == c_ref[0], gm_ref[...], go_ref[...])
        g_ref[...] = g
        d_ref[...], mo_ref[...], vo_ref[...] = _adamw_update(w_ref[...], g, m_ref[...], v_ref[...])

    blk = pl.BlockSpec((None, ta, B), lambda l, i, c_ref: (l, i, 0))
    half = pl.BlockSpec((None, ta, B), lambda l, i, c_ref: (l, i % nb, 0))
    shp = jax.ShapeDtypeStruct((L, A, B), F32)
    return pl.pallas_call(
        body, name=name,
        grid_spec=pltpu.PrefetchScalarGridSpec(num_scalar_prefetch=1, grid=(L, A // ta),
                                               in_specs=[blk, half, half, blk, blk], out_specs=[blk] * 4),
        out_shape=[shp] * 4,
        compiler_params=_params("parallel", "parallel"),
    )(_scalar(lax.axis_index("c")), w, g_mine, g_other, m, v)


def _chip_index():
    return 2 * lax.axis_index("x") + lax.axis_index("y")


def _pair_sum(g, recv, *, name):
    n, A, B = g.shape
    ta = _tile(A // 2, ROW_TILE)
    nb = A // 2 // ta

    def body(c_ref, g_ref, r_ref, o_ref):
        o_ref[...] = g_ref[...] + r_ref[...]

    return pl.pallas_call(
        body, name=name,
        grid_spec=pltpu.PrefetchScalarGridSpec(
            num_scalar_prefetch=1, grid=(n, nb),
            in_specs=[pl.BlockSpec((None, ta, B), lambda s, r, c_ref: (s, c_ref[0] * nb + r, 0)),
                      pl.BlockSpec((None, ta, B), lambda s, r, c_ref: (s, r, 0))],
            out_specs=pl.BlockSpec((None, ta, B), lambda s, r, c_ref: (s, r, 0))),
        out_shape=jax.ShapeDtypeStruct((n, A // 2, B), F32),
        compiler_params=_params("parallel", "parallel"),
    )(_scalar(lax.axis_index("c")), g, recv)


def _chip_sum(landed, own, *, name):
    n, A2, B = landed.shape
    ta = _tile(A2, ROW_TILE)

    def body(me_ref, *refs):
        slots, own_ref, o_ref = refs[:n], refs[n], refs[n + 1]
        parts = [jnp.where(me_ref[0] == s, own_ref[...], slots[s][...]) for s in range(n)]
        o_ref[...] = ((parts[0] + parts[1]) + parts[2]) + parts[3]

    def slot(s):
        return pl.BlockSpec((None, ta, B), lambda r, me_ref: (jnp.where(me_ref[0] == s, (s + 1) % n, s), r, 0))

    return pl.pallas_call(
        body, name=name,
        grid_spec=pltpu.PrefetchScalarGridSpec(
            num_scalar_prefetch=1, grid=(A2 // ta,),
            in_specs=[slot(s) for s in range(n)] + [pl.BlockSpec((None, ta, B), lambda r, me_ref: (me_ref[0], r, 0))],
            out_specs=pl.BlockSpec((ta, B), lambda r, me_ref: (r, 0))),
        out_shape=jax.ShapeDtypeStruct((A2, B), F32),
        compiler_params=_params("parallel"),
    )(_scalar(_chip_index()), *([landed] * n), own)


HBM_SPEC = pl.BlockSpec(memory_space=pl.ANY)
COMM_PARAMS = pltpu.CompilerParams(has_side_effects=True)


def _mesh_pos():
    return lax.axis_index("x"), lax.axis_index("y"), lax.axis_index("c")


def _other_chips(x, y):
    return [(1 - x, y), (x, 1 - y), (1 - x, 1 - y)]


def _my_layers(c):
    return pl.ds(HALF_DEPTH * c, HALF_DEPTH)


def _remote(src, dst, send_sems, recv_sems, k, to):
    return pltpu.make_async_remote_copy(src_ref=src, dst_ref=dst, send_sem=send_sems.at[k], recv_sem=recv_sems.at[k],
                                        device_id=to, device_id_type=MESH)


def _gather_shards(shards):
    n = len(shards)

    def body(*refs):
        ins, outs = refs[:n], refs[n:2 * n]
        send_sems, recv_sems = refs[2 * n:]
        x, y, c = _mesh_pos()
        me = 2 * x + y
        chips = _other_chips(x, y)
        sibling = (x, y, 1 - c)
        mine, other = _my_layers(c), _my_layers(1 - c)
        first = [_remote(ins[k].at[mine], outs[k].at[mine, me], send_sems, recv_sems, 6 * k + j, (px, py, c))
                 for j, (px, py) in enumerate(chips) for k in range(n)]
        for cp in first:
            cp.start()
        passed = []
        for j, (px, py) in enumerate(chips):
            for k in range(n):
                landed = outs[k].at[mine, 2 * px + py]
                _remote(landed, landed, send_sems, recv_sems, 6 * k + j, (px, py, c)).wait_recv()
                cp = _remote(landed, landed, send_sems, recv_sems, 6 * k + 3 + j, sibling)
                cp.start()
                passed.append(cp)
        for j, (px, py) in enumerate(chips):
            for k in range(n):
                landed = outs[k].at[other, 2 * px + py]
                _remote(landed, landed, send_sems, recv_sems, 6 * k + 3 + j, sibling).wait_recv()
        for cp in first + passed:
            cp.wait_send()

    gathered = pl.pallas_call(
        body, name="gather_weight_shards",
        in_specs=[HBM_SPEC] * n, out_specs=[HBM_SPEC] * n,
        out_shape=[jax.ShapeDtypeStruct((s.shape[0], N_CHIPS) + s.shape[1:], s.dtype) for s in shards],
        scratch_shapes=[pltpu.SemaphoreType.DMA((6 * n,)), pltpu.SemaphoreType.DMA((6 * n,))],
        compiler_params=COMM_PARAMS,
    )(*shards)
    return [_place_own(g, s, name="place_own_shard") for g, s in zip(gathered, shards)]


def _place_own(gathered, shard, *, name):
    L, A, B = shard.shape
    ta = _tile(A, ROW_TILE)

    def body(me_ref, s_ref, g_ref, o_ref):
        o_ref[...] = s_ref[...]

    return pl.pallas_call(
        body, name=name,
        grid_spec=pltpu.PrefetchScalarGridSpec(
            num_scalar_prefetch=1, grid=(L, A // ta),
            in_specs=[pl.BlockSpec((None, ta, B), lambda l, r, me_ref: (l, r, 0)), HBM_SPEC],
            out_specs=pl.BlockSpec((None, None, ta, B), lambda l, r, me_ref: (l, me_ref[0], r, 0))),
        out_shape=jax.ShapeDtypeStruct(gathered.shape, gathered.dtype),
        input_output_aliases={2: 0},
        compiler_params=_params("parallel", "parallel"),
    )(_scalar(_chip_index()), shard, gathered)


def _half_rows(rows, c):
    return pl.ds(pl.multiple_of(c * (rows // 2), 8), rows // 2)


def _sibling_exchange(gs, *, name):
    n = len(gs)

    def body(*refs):
        ins, outs = refs[:n], refs[n:2 * n]
        send_sems, recv_sems = refs[2 * n:]
        x, y, c = _mesh_pos()
        copies = [_remote(ins[k].at[:, _half_rows(gs[k].shape[1], 1 - c)], outs[k], send_sems, recv_sems, k, (x, y, 1 - c))
                  for k in range(n)]
        for cp in copies:
            cp.start()
        for cp in copies:
            cp.wait()

    return pl.pallas_call(
        body, name=name,
        in_specs=[HBM_SPEC] * n, out_specs=[HBM_SPEC] * n,
        out_shape=[jax.ShapeDtypeStruct((g.shape[0], g.shape[1] // 2, g.shape[2]), g.dtype) for g in gs],
        scratch_shapes=[pltpu.SemaphoreType.DMA((n,)), pltpu.SemaphoreType.DMA((n,))],
        compiler_params=COMM_PARAMS,
    )(*gs)


HBM_ONLY = pl.BlockSpec(memory_space=pltpu.HBM)
SEM_SPEC = pl.BlockSpec(memory_space=pltpu.SEMAPHORE)
SPLIT_PARAMS = pltpu.CompilerParams(has_side_effects=pltpu.SideEffectType.DATAFLOW_SIDE_EFFECTING)


def _scatter_copies(srcs, lands, send_sems, recv_sems):
    x, y, c = _mesh_pos()
    me = 2 * x + y
    out = []
    for k in range(len(srcs)):
        for j, (px, py) in enumerate(_other_chips(x, y)):
            s = 2 * px + py
            send = _remote(srcs[k].at[s], lands[k].at[me], send_sems, recv_sems, 3 * k + j, (px, py, c))
            recv = _remote(srcs[k].at[s], lands[k].at[s], send_sems, recv_sems, 3 * k + j, (px, py, c))
            out.append((send, recv))
    return out


def _scatter_start(ps, *, name):
    n = len(ps)
    lands = [lax.empty(p.shape, p.dtype) for p in ps]

    def body(*refs):
        srcs, zones = refs[:n], refs[n:2 * n]
        send_sems, recv_sems, token = refs[2 * n], refs[2 * n + 1], refs[-1]
        for send, _ in _scatter_copies(srcs, zones, send_sems, recv_sems):
            send.start()
        token[...] = jnp.zeros_like(token)

    hbm = lambda a: pltpu.HBM(a.shape, a.dtype)
    res = pl.pallas_call(
        body, name=name,
        in_specs=[HBM_ONLY] * (2 * n),
        out_specs=[SEM_SPEC, SEM_SPEC] + [HBM_ONLY] * (2 * n) + [pl.BlockSpec(memory_space=pltpu.VMEM)],
        out_shape=[pltpu.SemaphoreType.DMA((3 * n,)), pltpu.SemaphoreType.DMA((3 * n,))] + [hbm(a) for a in ps + lands]
        + [jax.ShapeDtypeStruct((8, LANES), F32)],
        input_output_aliases={i: 2 + i for i in range(2 * n)},
        compiler_params=SPLIT_PARAMS,
    )(*[pltpu.with_memory_space_constraint(a, pltpu.HBM) for a in ps + lands])
    return (res[0], res[1], list(res[2:2 + n]), list(res[2 + n:2 + 2 * n])), res[-1][0:1, 0:1]


def _scatter_wait(started, after, *, name):
    nl = len(started)
    n = len(started[0][2])
    flat = [a for (_, _, ps, lands) in started for a in ps + lands]

    def body(*refs):
        bufs = refs[:2 * n * nl]
        sems = refs[2 * n * nl:2 * n * nl + 2 * nl]
        for l in range(nl):
            srcs = bufs[2 * n * l:2 * n * l + n]
            zones = bufs[2 * n * l + n:2 * n * (l + 1)]
            for send, recv in _scatter_copies(srcs, zones, sems[2 * l], sems[2 * l + 1]):
                send.wait_send()
                recv.wait_recv()

    res = pl.pallas_call(
        body, name=name,
        in_specs=[HBM_ONLY] * len(flat) + [SEM_SPEC] * (2 * nl) + [HBM_SPEC],
        out_specs=[HBM_ONLY] * len(flat),
        out_shape=[pltpu.HBM(a.shape, a.dtype) for a in flat],
        input_output_aliases={i: i for i in range(len(flat))},
        compiler_params=SPLIT_PARAMS,
    )(*flat, *[s for (ss, rs, _, _) in started for s in (ss, rs)], after)
    return [(list(res[2 * n * l:2 * n * l + n]), list(res[2 * n * l + n:2 * n * (l + 1)])) for l in range(nl)]


def _sibling_share(hs):
    n = len(hs)

    def body(*refs):
        ins, outs = refs[:n], refs[n:2 * n]
        send_sems, recv_sems = refs[2 * n:]
        x, y, c = _mesh_pos()
        copies = [_remote(ins[k], outs[k], send_sems, recv_sems, k, (x, y, 1 - c)) for k in range(n)]
        for cp in copies:
            cp.start()
        for cp in copies:
            cp.wait()

    return pl.pallas_call(
        body, name="grad_sibling_share",
        in_specs=[HBM_SPEC] * n, out_specs=[HBM_SPEC] * n,
        out_shape=[jax.ShapeDtypeStruct(h.shape, h.dtype) for h in hs],
        scratch_shapes=[pltpu.SemaphoreType.DMA((n,)), pltpu.SemaphoreType.DMA((n,))],
        compiler_params=COMM_PARAMS,
    )(*hs)


def _allreduce_small(part):
    rows, C = part.shape

    def body(p_ref, o_ref, slots, send_sems, recv_sems):
        x, y, c = _mesh_pos()
        me = 4 * x + 2 * y + c
        slots[me] = p_ref[...]
        copies = []
        for k in range(1, 8):
            kx, ky, kc = (k >> 2) & 1, (k >> 1) & 1, k & 1
            peer = (x ^ kx if kx else x, y ^ ky if ky else y, c ^ kc if kc else c)
            cp = _remote(p_ref, slots.at[me], send_sems, recv_sems, k - 1, peer)
            cp.start()
            copies.append((cp, peer))
        for k, (cp, peer) in enumerate(copies):
            src = 4 * peer[0] + 2 * peer[1] + peer[2]
            _remote(p_ref, slots.at[src], send_sems, recv_sems, k, peer).wait_recv()
        for cp, _ in copies:
            cp.wait_send()
        total = slots[0]
        for d in range(1, 8):
            total = total + slots[d]
        o_ref[...] = total

    return pl.pallas_call(
        body, name="small_grad_allreduce",
        in_specs=[pl.BlockSpec(memory_space=pltpu.VMEM)], out_specs=pl.BlockSpec(memory_space=pltpu.VMEM),
        out_shape=jax.ShapeDtypeStruct((rows, C), F32),
        scratch_shapes=[pltpu.VMEM((8, rows, C), F32), pltpu.SemaphoreType.DMA((7,)), pltpu.SemaphoreType.DMA((7,))],
        compiler_params=pltpu.CompilerParams(has_side_effects=True, vmem_limit_bytes=VMEM_LIMIT_BYTES),
    )(part)


def _pad_w_uq(w):
    lead = w.shape[:-1]
    w = w.reshape(lead + (MLA_HEADS, MLA_QK))
    w = jnp.concatenate([w, jnp.zeros(lead + (MLA_HEADS, MLA_PAD - MLA_QK), w.dtype)], axis=-1)
    return w.reshape(lead + (MLA_HEADS * MLA_PAD,))


def _unpad_w_uq(g):
    lead = g.shape[:-1]
    return g.reshape(lead + (MLA_HEADS, MLA_PAD))[..., :MLA_QK].reshape(lead + (MLA_HEADS * MLA_QK,))


def _t(a):
    return jnp.swapaxes(a, -1, -2)


def _cols_of_shards(g):
    L, n, A, B = g.shape
    return g.transpose(0, 2, 1, 3).reshape(L, A, n * B)


def _shards_of_cols(w):
    A, NB = w.shape
    return w.reshape(A, N_CHIPS, NB // N_CHIPS).transpose(1, 0, 2)


BIG = ("w_in", "w_uq", "w_ukv", "w_out", "w_up", "w_down")
SMALL = ("attn_pre_norm", "forget_bias", "swa_sinks", "rel_bias", "q_latent_norm", "kv_latent_norm", "group_norm",
         "attn_post_norm", "ffn_pre_norm", "conv_b", "ffn_post_norm")
WEIGHTS = ("attn_pre_norm", "w_in", "forget_bias", "swa_sinks", "rel_bias", "q_latent_norm", "w_uq", "kv_latent_norm",
           "w_ukv", "group_norm", "w_out", "attn_post_norm", "ffn_pre_norm", "w_up", "conv_w", "conv_b", "w_down",
           "ffn_post_norm")


def _pack(arrs, cols, row_mult):
    flat = jnp.concatenate([a.reshape(-1) for a in arrs])
    n = flat.shape[0]
    per = cols * row_mult
    total = -(-n // per) * per
    return jnp.pad(flat, (0, total - n)).reshape(total // cols, cols)


def _unpack(packed, shapes):
    flat = packed.reshape(-1)
    out, off = [], 0
    for shp in shapes:
        n = int(np.prod(shp))
        out.append(flat[off:off + n].reshape(shp))
        off += n
    return out


def _kernel_weights(gathered, small):
    L = gathered["w_in"].shape[0]
    w_in_t = _t(gathered["w_in"]).reshape(L, IN_COLS, D_MODEL)
    w_in_t = jnp.pad(w_in_t, ((0, 0), (0, IN_ROWS - IN_COLS), (0, 0)))
    w_uq_p = _pad_w_uq(_cols_of_shards(gathered["w_uq"]))
    w_ukv = _cols_of_shards(gathered["w_ukv"])
    W = dict(small)
    W.update(w_qkv_t=w_in_t[:, :QKV_ROWS], w_lat_t=w_in_t[:, QKV_ROWS:], w_in_t=w_in_t, w_uq_p=w_uq_p, w_uq_t=_t(w_uq_p),
             w_ukv=w_ukv, w_ukv_t=_t(w_ukv), w_out=gathered["w_out"].reshape(L, D_MODEL, D_MODEL), w_up=gathered["w_up"],
             conv_w=_cols_of_shards(gathered["conv_w"]), w_down=gathered["w_down"].reshape(L, D_FF, D_MODEL))
    return W


def _local_step(x, target, W, layer_done):
    S = x.shape[0]
    tq_tabs, tm_tabs = _rope_tables(S)
    onehot_t = _rel_onehot_t()
    bias_t = _bias_table(W["rel_bias"].T, onehot_t).reshape(SWA_Q_HEADS, 2 * WINDOW, WINDOW)
    row = lambda a: a.reshape(1, -1)
    col = lambda a: a.reshape(-1, 1)
    fox_rows = (FOX_ROW0, FOX_ROW0 + FOX_HEADS * HEAD_DIM, FOX_ROW0 + 2 * FOX_HEADS * HEAD_DIM, SWA_Q_HEADS)
    fox = dict(rows=fox_rows, H=FOX_HEADS, Dk=HEAD_DIM, Dv=HEAD_DIM, scale=HEAD_DIM ** -0.5)
    mla = dict(rows=(0, 0, 0, SWA_Q_HEADS + FOX_HEADS), H=MLA_HEADS, Dk=MLA_PAD, Dv=HEAD_DIM, scale=MLA_QK ** -0.5)

    saved = []
    h = _rms_fwd(x, row(W["attn_pre_norm"][0]), name="rms_in")
    for l in range(DEPTH):
        sv = {"x0": x, "h1": h}
        qkv = _matmul(W["w_qkv_t"][l], h, tb=True, out_dtype=BF16, name="proj_qkv")
        lat = _matmul(W["w_lat_t"][l], h, tb=True, name="proj_lat")
        oa, lse_a = _swa_fwd(qkv, bias_t, W["swa_sinks"][l], name="swa_fwd")
        fb_col = jnp.pad(col(W["forget_bias"][l]), ((0, GATE_ROWS - FOX_HEADS), (0, 0)))
        f4 = _gate_fwd(lat, fb_col, name="fox_gate_fwd")[:FOX_HEADS]
        f_row, f_col = f4[:, None, :], f4.T
        of, lse_f = _attn_fwd(qkv, qkv, qkv, f_row=f_row, f_col=f_col, name="fox_fwd", **fox)
        nq, nkv, qm, km, vm = _mla_prep_fwd(lat, col(W["q_latent_norm"][l]), col(W["kv_latent_norm"][l]), W["w_uq_t"][l],
                                            W["w_ukv_t"][l], tq_tabs, tm_tabs, name="mla_prep_fwd")
        oc, lse_c = _attn_fwd(qm, km, vm, name="mla_fwd", **mla)
        mixed = _group_norm_fwd(oa, of, oc, col(W["group_norm"][l]), name="group_norm_fwd")
        y = _matmul(mixed, W["w_out"][l], ta=True, name="proj_out")
        x1, h2 = _resid_rms(x, y, row(W["attn_post_norm"][l]), row(W["ffn_pre_norm"][l]), name="attn_resid")
        a = _matmul(h2, W["w_up"][l], b_shards=True, name="ffn_up")
        z = _conv_geglu_fwd(a, W["conv_w"][l], row(W["conv_b"][l]), name="conv_geglu_fwd")
        y2 = _matmul(z, W["w_down"][l], name="ffn_down")
        g_next = row(W["attn_pre_norm"][l + 1]) if l + 1 < DEPTH else None
        x2, h_next = _resid_rms(x1, y2, row(W["ffn_post_norm"][l]), g_next, name="ffn_resid")
        sv.update(qkv=qkv, lat=lat, oa=oa, lse_a=lse_a, fb_col=fb_col, f_row=f_row, f_col=f_col, of=of, lse_f=lse_f,
                  nq=nq, nkv=nkv, qm=qm, km=km, vm=vm, oc=oc, lse_c=lse_c, mixed=mixed, y=y, x1=x1, h2=h2, a=a, z=z, y2=y2)
        saved.append(sv)
        x, h = x2, h_next

    loss, dx = _loss_head(x, target)

    G = {k: [None] * DEPTH for k in WEIGHTS if k != "rel_bias" and k not in BIG}
    dbias_layers = [None] * DEPTH
    for l in reversed(range(DEPTH)):
        sv = saved[l]
        gb = {}
        dy2, dg = _rms_bwd(sv["y2"], row(W["ffn_post_norm"][l]), dx, out_dtype=BF16, name="ffn_post_bwd")
        G["ffn_post_norm"][l] = dg[0]
        dz = _matmul(dy2, W["w_down"][l], tb=True, name="ffn_down_dx")
        gb["w_down"] = _matmul(sv["z"], dy2, ta=True, name="ffn_down_dw").reshape(N_CHIPS, D_FF // N_CHIPS, D_MODEL)
        du, dcw, dcb = _conv_geglu_bwd(sv["a"], W["conv_w"][l], row(W["conv_b"][l]), dz, name="conv_geglu_bwd")
        G["conv_w"][l] = dcw.transpose(1, 0, 2).reshape(3, 2 * D_FF)
        G["conv_b"][l] = dcb.reshape(2 * D_FF)
        da = _conv_bwd_input(du, W["conv_w"][l], name="conv_bwd_input")
        dh2 = _matmul(da, W["w_up"][l], tb=True, b_shards=True, name="ffn_up_dx")
        gb["w_up"] = _matmul(sv["h2"], da, ta=True, out_shards=True, name="ffn_up_dw")
        dx1, dg = _rms_bwd(sv["x1"], row(W["ffn_pre_norm"][l]), dh2, resid=dx, out_dtype=F32, name="ffn_pre_bwd")
        G["ffn_pre_norm"][l] = dg[0]
        dy, dg = _rms_bwd(sv["y"], row(W["attn_post_norm"][l]), dx1, out_dtype=BF16, name="attn_post_bwd")
        G["attn_post_norm"][l] = dg[0]
        dmixed = _matmul(W["w_out"][l], dy, tb=True, name="proj_out_dx")
        gb["w_out"] = _matmul(sv["mixed"], dy, name="proj_out_dw").reshape(N_CHIPS, D_MODEL // N_CHIPS, D_MODEL)
        doa, dof, doc, dg, delta = _group_norm_bwd(sv["oa"], sv["of"], sv["oc"], col(W["group_norm"][l]), dmixed,
                                                   name="group_norm_bwd")
        G["group_norm"][l] = dg[:, 0]
        dqa, dkva, dbias_l, dsink = _swa_bwd(sv["qkv"], bias_t, W["swa_sinks"][l], doa, sv["lse_a"],
                                             delta.reshape(-1, S), name="swa_bwd")
        dbias_layers[l] = dbias_l.reshape(SWA_Q_HEADS, -1)
        G["swa_sinks"][l] = dsink[:, 0]
        dqf, dkf, dvf, dfk = _attn_bwd(sv["qkv"], sv["qkv"], sv["qkv"], do=dof, lse=sv["lse_f"], delta=delta,
                                       f_row=sv["f_row"], f_col=sv["f_col"], name="fox_bwd", **fox)
        dF = jnp.pad(dfk.T, ((0, GATE_ROWS - FOX_HEADS), (0, 0)))
        dflog, dfb = _gate_bwd(sv["lat"], sv["fb_col"], dF, name="fox_gate_bwd")
        G["forget_bias"][l] = dfb[:FOX_HEADS, 0]
        dqm, dkm, dvm = _attn_bwd(sv["qm"], sv["km"], sv["vm"], do=doc, lse=sv["lse_c"], delta=delta, name="mla_bwd", **mla)
        dlat, dwq_t, dwkv_t, dgq, dgkv = _mla_prep_bwd(
            sv["lat"], sv["nq"], sv["nkv"], col(W["q_latent_norm"][l]), col(W["kv_latent_norm"][l]), W["w_uq_p"][l],
            W["w_ukv"][l], tq_tabs, tm_tabs, dqm, dkm, dvm, dflog, name="mla_prep_bwd")
        gb["w_uq"], gb["w_ukv"] = _shards_of_cols(_unpad_w_uq(dwq_t.T)), _shards_of_cols(dwkv_t.T)
        G["q_latent_norm"][l], G["kv_latent_norm"][l] = dgq[:, 0], dgkv[:, 0]
        dproj = _dproj_cast(dqa, dkva, dqf, dkf, dvf, dlat, name="dproj_cast")
        dh1 = _matmul(dproj, W["w_in_t"][l], ta=True, name="proj_in_dx")
        dw_in_t = _matmul(dproj, sv["h1"], name="proj_in_dw")
        gb["w_in"] = _t(dw_in_t[:IN_COLS].reshape(N_CHIPS, IN_COLS // N_CHIPS, D_MODEL))
        token = layer_done(l, gb)
        dx, dg = _rms_bwd(sv["x0"], row(W["attn_pre_norm"][l]) + token, dh1, resid=dx1, out_dtype=F32, name="attn_pre_bwd")
        G["attn_pre_norm"][l] = dg[0]

    grads = {k: jnp.stack(v) for k, v in G.items()}
    grads["rel_bias"] = _bias_table_bwd(jnp.stack(dbias_layers), onehot_t).T
    return loss, dx, grads


def kernel(x, attn_pre_norm, w_in, forget_bias, swa_sinks, rel_bias, q_latent_norm, w_uq, kv_latent_norm, w_ukv, group_norm, w_out, attn_post_norm, ffn_pre_norm, w_up, conv_w, conv_b, w_down, ffn_post_norm, loss_target, m_attn_pre_norm, m_w_in, m_forget_bias, m_swa_sinks, m_rel_bias, m_q_latent_norm, m_w_uq, m_kv_latent_norm, m_w_ukv, m_group_norm, m_w_out, m_attn_post_norm, m_ffn_pre_norm, m_w_up, m_conv_w, m_conv_b, m_w_down, m_ffn_post_norm, v_attn_pre_norm, v_w_in, v_forget_bias, v_swa_sinks, v_rel_bias, v_q_latent_norm, v_w_uq, v_kv_latent_norm, v_w_ukv, v_group_norm, v_w_out, v_attn_post_norm, v_ffn_pre_norm, v_w_up, v_conv_w, v_conv_b, v_w_down, v_ffn_post_norm):
    args = dict(locals())
    w = {k: args[k] for k in WEIGHTS}
    m = {k: args["m_" + k] for k in WEIGHTS}
    v = {k: args["v_" + k] for k in WEIGHTS}

    sent = BIG + ("conv_w",)
    gathered = dict(zip(sent, _gather_shards([w[k] if k == "conv_w" else w[k].astype(BF16) for k in sent])))
    W = _kernel_weights(gathered, {k: w[k] for k in SMALL})

    started = [None] * DEPTH

    def layer_done(l, gb):
        gs = [gb[k] for k in BIG]
        recv = _sibling_exchange(gs, name=f"grad_sibling_exchange_{l}")
        pair = [_pair_sum(gk, rk, name="grad_pair_sum") for gk, rk in zip(gs, recv)]
        started[l], token = _scatter_start(pair, name=f"grad_scatter_start_{l}")
        return token

    loss_part, dx, g = _local_step(x[0], loss_target[0], W, layer_done)
    loss = lax.psum(loss_part, ("x", "y", "c"))

    landed = _scatter_wait(started, dx, name="grad_scatter_wait")
    mine = [jnp.stack([_chip_sum(zones[i], pair[i], name="grad_chip_sum") for pair, zones in landed]) for i in range(len(BIG))]
    other = _sibling_share(mine)
    out_g, out_d, out_m, out_v = {}, {}, {}, {}
    for k, g_mine, g_other in zip(BIG, mine, other):
        out_g[k], out_d[k], out_m[k], out_v[k] = _adamw_halves(w[k], g_mine, g_other, m[k], v[k], name="adamw_" + k)

    small_shapes = [w[k].shape for k in SMALL]
    reduced = _allreduce_small(_pack([g[k] for k in SMALL] + [g["conv_w"]], LANES, 8))
    *g_small, g_cw = _unpack(reduced, small_shapes + [g["conv_w"].shape])
    chip = 2 * lax.axis_index("x") + lax.axis_index("y")
    g_small.append(lax.dynamic_slice_in_dim(g_cw, chip * FF_SHARD, FF_SHARD, axis=2))
    names = SMALL + ("conv_w",)
    shapes = small_shapes + [w["conv_w"].shape]
    packed = lambda arrs: _pack(arrs, LANES, ROW_TILE)[None]
    d_s, m_s, v_s = _adamw(packed([w[k] for k in names]), packed(g_small), packed([m[k] for k in names]),
                           packed([v[k] for k in names]), name="adamw_small")
    out_g.update(zip(names, g_small))
    out_d.update(zip(names, _unpack(d_s, shapes)))
    out_m.update(zip(names, _unpack(m_s, shapes)))
    out_v.update(zip(names, _unpack(v_s, shapes)))

    return (loss, dx[None], *[out_g[k] for k in WEIGHTS], *[out_d[k] for k in WEIGHTS],
            *[out_m[k] for k in WEIGHTS], *[out_v[k] for k in WEIGHTS])
```

```python
import math

import numpy as np
import jax
import jax.numpy as jnp
from jax import lax
from jax.experimental import pallas as pl
from jax.experimental.pallas import tpu as pltpu

F32 = jnp.float32
BF16 = jnp.bfloat16

D_MODEL = 1024
DEPTH = 4
HEAD_DIM = 64
SWA_Q_HEADS = 8
SWA_KV_HEADS = 2
SWA_GROUP = SWA_Q_HEADS // SWA_KV_HEADS
WINDOW = 128
FOX_HEADS = 4
MLA_HEADS = 4
MLA_Q_RANK = 256
MLA_KV_RANK = 128
MLA_NOPE = 64
MLA_ROPE = 32
MLA_QK = MLA_NOPE + MLA_ROPE
ROPE_THETA = 10000.0
REL_BUCKETS = 32
REL_MAX_DIST = 128
D_FF = 2816
EPS = 1e-6
NEG_INF = -1e30
LANES = 128
N_CHIPS = 4
HALF_DEPTH = DEPTH // 2

IN_COLS = 1956
IN_ROWS = 2048
QKV_ROWS = 1536
LAT_ROWS = IN_ROWS - QKV_ROWS
LAT_SHIFT = FOX_HEADS
FOX_ROW0 = 768
MLA_PAD = LANES
GATE_ROWS = 8

ADAM_LR = 0.001
ADAM_B1 = 0.9
ADAM_B2 = 0.999
ADAM_EPS = 1e-08
ADAM_WD = 0.01
ADAM_STEP = 10

VMEM_LIMIT_BYTES = 48 * 1024 * 1024
ATT_TILE = 256
ROW_TILE = 256
MESH = pl.DeviceIdType.MESH

NT = (((1,), (1,)), ((), ()))
TN = (((0,), (0,)), ((), ()))
NN = (((1,), (0,)), ((), ()))


def _params(*sem):
    return pltpu.CompilerParams(dimension_semantics=sem, vmem_limit_bytes=VMEM_LIMIT_BYTES)


def _tile(dim, cap):
    for t in (2048, 1408, 1024, 512, 256, 128, 64, 32, 16, 8):
        if t <= cap and dim % t == 0:
            return t
    return dim


def _dot(a, b, dims=NN):
    return lax.dot_general(a, b, dims, preferred_element_type=F32)


def _split3(a):
    a1 = a.astype(BF16)
    r1 = a - a1.astype(F32)
    a2 = r1.astype(BF16)
    a3 = (r1 - a2.astype(F32)).astype(BF16)
    return a1, a2, a3


FF_SHARD = 2 * D_FF // N_CHIPS


def _matmul(a, b, *, ta=False, tb=False, out_dtype=F32, name, b_shards=False, out_shards=False):
    if ta:
        K, M = a.shape
    else:
        M, K = a.shape
    if b_shards:
        K2, N = (2 * D_FF, D_MODEL) if tb else (D_MODEL, 2 * D_FF)
    elif tb:
        N, K2 = b.shape
    else:
        K2, N = b.shape
    assert K == K2, (a.shape, b.shape)
    tm, tn, tk = _tile(M, 1408), _tile(N, 1408), _tile(K, 1408)
    nk = K // tk
    dims = (((0 if ta else 1,), (1 if tb else 0,)), ((), ()))

    def body(a_ref, b_ref, o_ref, acc_ref):
        k = pl.program_id(2)

        @pl.when(k == 0)
        def _():
            acc_ref[...] = jnp.zeros_like(acc_ref)

        acc_ref[...] += lax.dot_general(a_ref[...], b_ref[...], dims, preferred_element_type=F32)

        @pl.when(k == nk - 1)
        def _():
            o_ref[...] = acc_ref[...].astype(o_ref.dtype)

    a_spec = pl.BlockSpec((tk, tm), lambda i, j, k: (k, i)) if ta else pl.BlockSpec((tm, tk), lambda i, j, k: (i, k))
    if b_shards and tb:
        assert tk == FF_SHARD
        b_spec = pl.BlockSpec((None, tn, tk), lambda i, j, k: (k, j, 0))
    elif b_shards:
        assert tn == FF_SHARD
        b_spec = pl.BlockSpec((None, tk, tn), lambda i, j, k: (j, k, 0))
    else:
        b_spec = pl.BlockSpec((tn, tk), lambda i, j, k: (j, k)) if tb else pl.BlockSpec((tk, tn), lambda i, j, k: (k, j))
    if out_shards:
        assert tn == FF_SHARD
        out_spec = pl.BlockSpec((None, tm, tn), lambda i, j, k: (j, i, 0))
        out_shape = jax.ShapeDtypeStruct((N // tn, M, tn), out_dtype)
    else:
        out_spec = pl.BlockSpec((tm, tn), lambda i, j, k: (i, j))
        out_shape = jax.ShapeDtypeStruct((M, N), out_dtype)
    return pl.pallas_call(
        body, name=name, grid=(M // tm, N // tn, nk),
        in_specs=[a_spec, b_spec], out_specs=out_spec, out_shape=out_shape,
        scratch_shapes=[pltpu.VMEM((tm, tn), F32)],
        compiler_params=_params("parallel", "parallel", "arbitrary"),
    )(a, b)


def _seg_rms(xs, g):
    r = lax.rsqrt(jnp.mean(xs * xs, axis=-1, keepdims=True) + EPS)
    return xs * r * g


def _seg_rms_bwd(xs, g, dy):
    r = lax.rsqrt(jnp.mean(xs * xs, axis=-1, keepdims=True) + EPS)
    gd = dy * g
    c = jnp.mean(gd * xs, axis=-1, keepdims=True)
    dx = r * gd - xs * (r * r * r * c)
    dg = jnp.sum(dy * (xs * r), axis=0, keepdims=True)
    return dx, dg


def _rms_fwd(x, g, *, name):
    S, W = x.shape
    tm = _tile(S, 512)

    def body(x_ref, g_ref, o_ref):
        o_ref[...] = _seg_rms(x_ref[...], g_ref[...]).astype(o_ref.dtype)

    return pl.pallas_call(
        body, name=name, grid=(S // tm,),
        in_specs=[pl.BlockSpec((tm, W), lambda i: (i, 0)), pl.BlockSpec((1, W), lambda i: (0, 0))],
        out_specs=pl.BlockSpec((tm, W), lambda i: (i, 0)),
        out_shape=jax.ShapeDtypeStruct((S, W), BF16),
        compiler_params=_params("parallel"),
    )(x, g)


def _rms_bwd(x, g, dy, *, resid=None, out_dtype, name):
    S, W = x.shape
    tm = _tile(S, 512)
    has_resid = resid is not None

    def body(*refs):
        if has_resid:
            x_ref, g_ref, dy_ref, r_ref, dx_ref, dg_ref = refs
        else:
            x_ref, g_ref, dy_ref, dx_ref, dg_ref = refs

        @pl.when(pl.program_id(0) == 0)
        def _():
            dg_ref[...] = jnp.zeros_like(dg_ref)

        dx, dg = _seg_rms_bwd(x_ref[...], g_ref[...], dy_ref[...])
        if has_resid:
            dx = dx + r_ref[...]
        dx_ref[...] = dx.astype(dx_ref.dtype)
        dg_ref[...] += dg

    row = pl.BlockSpec((tm, W), lambda i: (i, 0))
    vec = pl.BlockSpec((1, W), lambda i: (0, 0))
    ins = [x, g, dy] + ([resid] if has_resid else [])
    return pl.pallas_call(
        body, name=name, grid=(S // tm,),
        in_specs=[row, vec, row] + ([row] if has_resid else []),
        out_specs=[row, vec],
        out_shape=[jax.ShapeDtypeStruct((S, W), out_dtype), jax.ShapeDtypeStruct((1, W), F32)],
        compiler_params=_params("arbitrary"),
    )(*ins)


def _resid_rms(x, y, g_post, g_next, *, name):
    S, W = x.shape
    tm = _tile(S, 512)
    with_next = g_next is not None

    def body(*refs):
        if with_next:
            x_ref, y_ref, gp_ref, gn_ref, xo_ref, h_ref = refs
        else:
            x_ref, y_ref, gp_ref, xo_ref = refs
        xn = x_ref[...] + _seg_rms(y_ref[...], gp_ref[...])
        xo_ref[...] = xn
        if with_next:
            h_ref[...] = _seg_rms(xn, gn_ref[...]).astype(BF16)

    row = pl.BlockSpec((tm, W), lambda i: (i, 0))
    vec = pl.BlockSpec((1, W), lambda i: (0, 0))
    outs = [jax.ShapeDtypeStruct((S, W), F32)] + ([jax.ShapeDtypeStruct((S, W), BF16)] if with_next else [])
    res = pl.pallas_call(
        body, name=name, grid=(S // tm,),
        in_specs=[row, row, vec] + ([vec] if with_next else []),
        out_specs=[row] + ([row] if with_next else []),
        out_shape=outs,
        compiler_params=_params("parallel"),
    )(*([x, y, g_post] + ([g_next] if with_next else [])))
    return (res[0], res[1]) if with_next else (res[0], None)


def _col_rms(xs, g):
    r = lax.rsqrt(jnp.mean(xs * xs, axis=0, keepdims=True) + EPS)
    return xs * r * g


def _col_rms_bwd(xs, g, dy):
    r = lax.rsqrt(jnp.mean(xs * xs, axis=0, keepdims=True) + EPS)
    gd = dy * g
    c = jnp.mean(gd * xs, axis=0, keepdims=True)
    dx = r * gd - xs * (r * r * r * c)
    dg = jnp.sum(dy * (xs * r), axis=1, keepdims=True)
    return dx, dg


GROUP_ROWS = (SWA_Q_HEADS * HEAD_DIM, FOX_HEADS * HEAD_DIM, MLA_HEADS * HEAD_DIM)


def _group_specs(S, tn):
    outs = [pl.BlockSpec((n, tn), lambda i: (0, i)) for n in GROUP_ROWS]
    g = pl.BlockSpec((D_MODEL, 1), lambda i: (0, 0))
    mixed = pl.BlockSpec((D_MODEL, tn), lambda i: (0, i))
    return outs, g, mixed


def _group_norm_fwd(oa, of, oc, g, *, name):
    S = oa.shape[1]
    tn = _tile(S, 512)
    outs, gs, mixed = _group_specs(S, tn)

    def body(a_ref, f_ref, c_ref, g_ref, o_ref):
        r0 = 0
        for ref, n in zip((a_ref, f_ref, c_ref), GROUP_ROWS):
            o_ref[r0:r0 + n, :] = _col_rms(ref[...], g_ref[r0:r0 + n, :]).astype(BF16)
            r0 += n

    return pl.pallas_call(
        body, name=name, grid=(S // tn,),
        in_specs=outs + [gs], out_specs=mixed,
        out_shape=jax.ShapeDtypeStruct((D_MODEL, S), BF16),
        compiler_params=_params("parallel"),
    )(oa, of, oc, g)


def _group_norm_bwd(oa, of, oc, g, dmixed, *, name):
    S = oa.shape[1]
    tn = _tile(S, 512)
    outs, gs, mixed = _group_specs(S, tn)
    n_heads = D_MODEL // HEAD_DIM

    def body(a_ref, f_ref, c_ref, g_ref, dm_ref, da_ref, df_ref, dc_ref, dg_ref, dl_ref):
        @pl.when(pl.program_id(0) == 0)
        def _():
            dg_ref[...] = jnp.zeros_like(dg_ref)

        r0 = 0
        for ref, dref, n in zip((a_ref, f_ref, c_ref), (da_ref, df_ref, dc_ref), GROUP_ROWS):
            o = ref[...]
            dx, dg = _col_rms_bwd(o, g_ref[r0:r0 + n, :], dm_ref[r0:r0 + n, :])
            dxb = dx.astype(BF16)
            dref[...] = dxb
            dg_ref[r0:r0 + n, :] += dg
            od = o * dxb.astype(F32)
            for h in range(n // HEAD_DIM):
                dl_ref[r0 // HEAD_DIM + h] = jnp.sum(od[h * HEAD_DIM:(h + 1) * HEAD_DIM, :], axis=0, keepdims=True)
            r0 += n

    return pl.pallas_call(
        body, name=name, grid=(S // tn,),
        in_specs=outs + [gs, mixed], out_specs=outs + [gs, pl.BlockSpec((n_heads, 1, tn), lambda i: (0, 0, i))],
        out_shape=[jax.ShapeDtypeStruct((n, S), BF16) for n in GROUP_ROWS] + [jax.ShapeDtypeStruct((D_MODEL, 1), F32),
                                                                              jax.ShapeDtypeStruct((n_heads, 1, S), F32)],
        compiler_params=_params("arbitrary"),
    )(oa, of, oc, g, dmixed)


def _loss_head(y, target):
    S, W = y.shape
    tm = _tile(S, 512)

    def body(y_ref, t_ref, d_ref, l_ref):
        @pl.when(pl.program_id(0) == 0)
        def _():
            l_ref[...] = jnp.zeros_like(l_ref)

        err = y_ref[...] - t_ref[...]
        d_ref[...] = err * (1.0 / W)
        l_ref[...] += 0.5 * jnp.sum(jnp.mean(err * err, axis=-1, keepdims=True), axis=0, keepdims=True)

    row = pl.BlockSpec((tm, W), lambda i: (i, 0))
    d, l = pl.pallas_call(
        body, name="loss_head", grid=(S // tm,),
        in_specs=[row, row],
        out_specs=[row, pl.BlockSpec((1, 1), lambda i: (0, 0))],
        out_shape=[jax.ShapeDtypeStruct((S, W), F32), jax.ShapeDtypeStruct((1, 1), F32)],
        compiler_params=_params("arbitrary"),
    )(y, target)
    return l[0, 0], d


def _attn_fwd(q_src, k_src, v_src, rows, H, Dk, Dv, scale, f_row=None, f_col=None, *, name):
    S = q_src.shape[1]
    T = _tile(S, ATT_TILE)
    nq = S // T
    forget = f_row is not None
    qb, kb, vb = rows[0] // (H * Dk), rows[1] // (H * Dk), rows[2] // (H * Dv)
    hs = range(H)

    def body(*refs):
        if forget:
            q_ref, k_ref, v_ref, fq_ref, fk_ref, o_ref, lse_ref = refs
        else:
            q_ref, k_ref, v_ref, o_ref, lse_ref = refs
        i = pl.program_id(0)

        def tile(j, masked, state):
            off = pl.multiple_of(j * T, T)
            ss = [_dot(k_ref[h * Dk:(h + 1) * Dk, pl.ds(off, T)], q_ref[h * Dk:(h + 1) * Dk, :], TN) * scale for h in hs]
            if forget:
                ss = [ss[h] + (fq_ref[h] - fk_ref[pl.ds(off, T), h:h + 1]) for h in hs]
            if masked:
                r = lax.broadcasted_iota(jnp.int32, (T, T), 0)
                c = lax.broadcasted_iota(jnp.int32, (T, T), 1)
                ss = [jnp.where(r <= c, s, NEG_INF) for s in ss]
            m_new = [jnp.maximum(state[h][0], jnp.max(ss[h], axis=0, keepdims=True)) for h in hs]
            alpha = [jnp.exp(state[h][0] - m_new[h]) for h in hs]
            ps = [jnp.exp(ss[h] - m_new[h]) for h in hs]
            l_new = [alpha[h] * state[h][1] + jnp.sum(ps[h], axis=0, keepdims=True) for h in hs]
            p_hi = [p.astype(BF16) for p in ps]
            vs = [v_ref[h * Dv:(h + 1) * Dv, pl.ds(off, T)] for h in hs]
            pv = [_dot(vs[h], p_hi[h]) for h in hs]
            if forget:
                pv = [pv[h] + _dot(vs[h], (ps[h] - p_hi[h].astype(F32)).astype(BF16)) for h in hs]
            return tuple((m_new[h], l_new[h], alpha[h] * state[h][2] + pv[h]) for h in hs)

        init = tuple((jnp.full((1, T), NEG_INF, F32), jnp.zeros((1, T), F32), jnp.zeros((Dv, T), F32)) for _ in hs)
        state = lax.fori_loop(0, i, lambda j, st: tile(j, False, st), init)
        state = tile(i, True, state)
        for h in hs:
            m, l, acc = state[h]
            o_ref[h * Dv:(h + 1) * Dv, :] = acc / l
            lse_ref[h] = m + jnp.log(l)

    in_specs = [pl.BlockSpec((H * Dk, T), lambda i: (qb, i)),
                pl.BlockSpec((H * Dk, S), lambda i: (kb, 0)),
                pl.BlockSpec((H * Dv, S), lambda i: (vb, 0))]
    ins = [q_src, k_src, v_src]
    if forget:
        in_specs += [pl.BlockSpec((H, 1, T), lambda i: (0, 0, i)), pl.BlockSpec((S, H), lambda i: (0, 0))]
        ins += [f_row, f_col]
    return pl.pallas_call(
        body, name=name, grid=(nq,),
        in_specs=in_specs,
        out_specs=[pl.BlockSpec((H * Dv, T), lambda i: (0, i)), pl.BlockSpec((H, 1, T), lambda i: (0, 0, i))],
        out_shape=[jax.ShapeDtypeStruct((H * Dv, S), F32), jax.ShapeDtypeStruct((H, 1, S), F32)],
        compiler_params=_params("parallel"),
    )(*ins)


def _attn_bwd(q_src, k_src, v_src, rows, H, Dk, Dv, scale, do, lse, delta, f_row=None, f_col=None, *, name):
    S = q_src.shape[1]
    T = _tile(S, ATT_TILE)
    nq = S // T
    forget = f_row is not None
    qb, kb, vb, db = rows[0] // (H * Dk), rows[1] // (H * Dk), rows[2] // (H * Dv), rows[3] // H
    hs = range(H)

    def body(*refs):
        if forget:
            (q_ref, k_ref, v_ref, do_ref, lse_ref, dl_ref, fq_ref, fk_ref,
             dq_ref, dk_ref, dv_ref, df_ref, dk_s, dv_s, df_s) = refs
        else:
            q_ref, k_ref, v_ref, do_ref, lse_ref, dl_ref, dq_ref, dk_ref, dv_ref, dk_s, dv_s = refs
        j = pl.program_id(0)

        @pl.when(j == 0)
        def _():
            dq_ref[...] = jnp.zeros_like(dq_ref)

        dk_s[...] = jnp.zeros_like(dk_s)
        dv_s[...] = jnp.zeros_like(dv_s)
        if forget:
            df_s[...] = jnp.zeros_like(df_s)
        kt = [k_ref[h * Dk:(h + 1) * Dk, :] for h in hs]
        kj = [k.T for k in kt]
        vj = [v_ref[h * Dv:(h + 1) * Dv, :].T for h in hs]
        koff = pl.multiple_of(j * T, T)

        def tile(i, masked):
            cols = pl.ds(pl.multiple_of(i * T, T), T)
            qi = [q_ref[h * Dk:(h + 1) * Dk, cols] for h in hs]
            doi = [do_ref[h * Dv:(h + 1) * Dv, cols] for h in hs]
            st = [_dot(kj[h], qi[h]) * scale for h in hs]
            if forget:
                st = [st[h] + (fq_ref[h, :, cols] - fk_ref[pl.ds(koff, T), h:h + 1]) for h in hs]
            if masked:
                r = lax.broadcasted_iota(jnp.int32, (T, T), 0)
                c = lax.broadcasted_iota(jnp.int32, (T, T), 1)
                st = [jnp.where(r <= c, x, NEG_INF) for x in st]
            pt = [jnp.exp(st[h] - lse_ref[h, :, cols]) for h in hs]
            dpt = [_dot(vj[h], doi[h]) for h in hs]
            dst = [pt[h] * (dpt[h] - dl_ref[h, :, cols]) for h in hs]
            ptb = [p.astype(BF16) for p in pt]
            dsb = [d.astype(BF16) for d in dst]
            for h in hs:
                dv_s[h * Dv:(h + 1) * Dv, :] += _dot(doi[h], ptb[h], NT)
            for h in hs:
                dk_s[h * Dk:(h + 1) * Dk, :] += _dot(qi[h], dsb[h], NT)
            for h in hs:
                dq_ref[h * Dk:(h + 1) * Dk, cols] += _dot(kt[h], dsb[h]) * scale
            if forget:
                for h in hs:
                    part = dst[h][:, 0:LANES]
                    for c0 in range(LANES, T, LANES):
                        part = part + dst[h][:, c0:c0 + LANES]
                    df_s[h] += part

        tile(j, True)

        def loop_body(i, carry):
            tile(i, False)
            return carry

        lax.fori_loop(j + 1, nq, loop_body, 0)
        dk_ref[...] = dk_s[...] * scale
        dv_ref[...] = dv_s[...]
        if forget:
            df_ref[...] = jnp.concatenate([-jnp.sum(df_s[h], axis=-1, keepdims=True) for h in hs], axis=1)

    res = lambda D, b0: pl.BlockSpec((H * D, S), lambda j: (b0, 0))
    blk = lambda D, b0: pl.BlockSpec((H * D, T), lambda j: (b0, j))
    row3 = lambda b0: pl.BlockSpec((H, 1, S), lambda j: (b0, 0, 0))
    in_specs = [res(Dk, qb), blk(Dk, kb), blk(Dv, vb), res(Dv, 0), row3(0), row3(db)]
    ins = [q_src, k_src, v_src, do, lse, delta]
    out_specs = [res(Dk, 0), blk(Dk, 0), blk(Dv, 0)]
    out_shape = [jax.ShapeDtypeStruct((H * Dk, S), F32), jax.ShapeDtypeStruct((H * Dk, S), F32),
                 jax.ShapeDtypeStruct((H * Dv, S), F32)]
    scratch = [pltpu.VMEM((H * Dk, T), F32), pltpu.VMEM((H * Dv, T), F32)]
    if forget:
        in_specs += [row3(0), pl.BlockSpec((S, H), lambda j: (0, 0))]
        ins += [f_row, f_col]
        out_specs.append(pl.BlockSpec((T, H), lambda j: (j, 0)))
        out_shape.append(jax.ShapeDtypeStruct((S, H), F32))
        scratch.append(pltpu.VMEM((H, T, min(T, LANES)), F32))
    return pl.pallas_call(
        body, name=name, grid=(nq,),
        in_specs=in_specs, out_specs=out_specs, out_shape=out_shape, scratch_shapes=scratch,
        compiler_params=_params("arbitrary"),
    )(*ins)


def _swa_masks(i):
    r = lax.broadcasted_iota(jnp.int32, (WINDOW, WINDOW), 0)
    c = lax.broadcasted_iota(jnp.int32, (WINDOW, WINDOW), 1)
    return (r > c) & (i > 0), r <= c


def _swa_specs():
    W = WINDOW
    kv_rows = SWA_KV_HEADS * HEAD_DIM
    q = pl.BlockSpec((SWA_Q_HEADS * HEAD_DIM, W), lambda i: (0, i))
    prev = lambda b: pl.BlockSpec((kv_rows, W), lambda i: (b, jnp.maximum(i - 1, 0)))
    cur = lambda b: pl.BlockSpec((kv_rows, W), lambda i: (b, i))
    bias = pl.BlockSpec((SWA_Q_HEADS, 2 * W, W), lambda i: (0, 0, 0))
    stat = pl.BlockSpec((SWA_Q_HEADS, W), lambda i: (0, i))
    sink = pl.BlockSpec(memory_space=pltpu.SMEM)
    return q, prev(4), cur(4), prev(5), cur(5), bias, stat, sink


def _swa_scores(h, q_ref, kp_ref, kc_ref, b_ref, masks):
    g = h // SWA_GROUP
    rows = slice(g * HEAD_DIM, (g + 1) * HEAD_DIM)
    qh = q_ref[h * HEAD_DIM:(h + 1) * HEAD_DIM, :]
    scale = HEAD_DIM ** -0.5
    s_p = jnp.where(masks[0], _dot(kp_ref[rows, :], qh, TN) * scale + b_ref[h, 0:WINDOW, :], NEG_INF)
    s_c = jnp.where(masks[1], _dot(kc_ref[rows, :], qh, TN) * scale + b_ref[h, WINDOW:2 * WINDOW, :], NEG_INF)
    return qh, rows, s_p, s_c


def _swa_fwd(qkv, bias_t, sinks, *, name):
    S = qkv.shape[1]
    qs, kp, kc, vp, vc, bs, stat, sk = _swa_specs()

    def body(sink_ref, q_ref, kp_ref, kc_ref, vp_ref, vc_ref, b_ref, o_ref, lse_ref):
        masks = _swa_masks(pl.program_id(0))
        for h in range(SWA_Q_HEADS):
            qh, rows, s_p, s_c = _swa_scores(h, q_ref, kp_ref, kc_ref, b_ref, masks)
            sink = sink_ref[h]
            m = jnp.maximum(jnp.maximum(jnp.max(s_p, axis=0, keepdims=True), jnp.max(s_c, axis=0, keepdims=True)), sink)
            p_p = jnp.exp(s_p - m)
            p_c = jnp.exp(s_c - m)
            l = jnp.sum(p_p, axis=0, keepdims=True) + jnp.sum(p_c, axis=0, keepdims=True) + jnp.exp(sink - m)
            o = _dot(vp_ref[rows, :], p_p.astype(BF16)) + _dot(vc_ref[rows, :], p_c.astype(BF16))
            o_ref[h * HEAD_DIM:(h + 1) * HEAD_DIM, :] = o / l
            lse_ref[h:h + 1, :] = m + jnp.log(l)

    return pl.pallas_call(
        body, name=name, grid=(S // WINDOW,),
        in_specs=[sk, qs, kp, kc, vp, vc, bs],
        out_specs=[qs, stat],
        out_shape=[jax.ShapeDtypeStruct((SWA_Q_HEADS * HEAD_DIM, S), F32), jax.ShapeDtypeStruct((SWA_Q_HEADS, S), F32)],
        compiler_params=_params("parallel"),
    )(sinks, qkv, qkv, qkv, qkv, qkv, bias_t)


def _swa_bwd(qkv, bias_t, sinks, do, lse, delta, *, name):
    S = qkv.shape[1]
    W = WINDOW
    qs, kp, kc, vp, vc, bs, stat, sk = _swa_specs()
    scale = HEAD_DIM ** -0.5
    kv_rows = SWA_KV_HEADS * HEAD_DIM

    def body(sink_ref, q_ref, kp_ref, kc_ref, vp_ref, vc_ref, b_ref, do_ref, lse_ref, dl_ref,
             dq_ref, dkv_ref, db_ref, dsk_ref):
        i = pl.program_id(0)

        @pl.when(i == 0)
        def _():
            dkv_ref[...] = jnp.zeros_like(dkv_ref)
            db_ref[...] = jnp.zeros_like(db_ref)
            dsk_ref[...] = jnp.zeros_like(dsk_ref)

        masks = _swa_masks(i)
        prev = pl.ds(pl.multiple_of(jnp.maximum(i - 1, 0) * W, W), W)
        cur = pl.ds(pl.multiple_of(i * W, W), W)
        for h in range(SWA_Q_HEADS):
            qh, rows, s_p, s_c = _swa_scores(h, q_ref, kp_ref, kc_ref, b_ref, masks)
            vrows = slice(kv_rows + rows.start, kv_rows + rows.stop)
            hrows = slice(h * HEAD_DIM, (h + 1) * HEAD_DIM)
            doh = do_ref[hrows, :]
            lse_h = lse_ref[h:h + 1, :]
            delta = dl_ref[h:h + 1, :]
            p_p = jnp.exp(s_p - lse_h)
            p_c = jnp.exp(s_c - lse_h)
            ds_p = p_p * (_dot(vp_ref[rows, :], doh, TN) - delta)
            ds_c = p_c * (_dot(vc_ref[rows, :], doh, TN) - delta)
            db_ref[h, 0:W, :] += ds_p
            db_ref[h, W:2 * W, :] += ds_c
            dsk = -jnp.sum(jnp.exp(sink_ref[h] - lse_h) * delta, axis=1, keepdims=True)
            dsk_ref[h:h + 1, :] += jnp.broadcast_to(dsk, (1, LANES))
            dsb_p = ds_p.astype(BF16)
            dsb_c = ds_c.astype(BF16)
            dq_ref[hrows, :] = (_dot(kp_ref[rows, :], dsb_p) + _dot(kc_ref[rows, :], dsb_c)) * scale
            dkv_ref[rows, prev] += _dot(qh, dsb_p, NT) * scale
            dkv_ref[rows, cur] += _dot(qh, dsb_c, NT) * scale
            dkv_ref[vrows, prev] += _dot(doh, p_p.astype(BF16), NT)
            dkv_ref[vrows, cur] += _dot(doh, p_c.astype(BF16), NT)

    return pl.pallas_call(
        body, name=name, grid=(S // W,),
        in_specs=[sk, qs, kp, kc, vp, vc, bs, qs, stat, stat],
        out_specs=[qs, pl.BlockSpec((2 * kv_rows, S), lambda i: (0, 0)), bs, pl.BlockSpec((SWA_Q_HEADS, LANES), lambda i: (0, 0))],
        out_shape=[jax.ShapeDtypeStruct((SWA_Q_HEADS * HEAD_DIM, S), F32), jax.ShapeDtypeStruct((2 * kv_rows, S), F32),
                   jax.ShapeDtypeStruct((SWA_Q_HEADS, 2 * W, W), F32), jax.ShapeDtypeStruct((SWA_Q_HEADS, LANES), F32)],
        compiler_params=_params("arbitrary"),
    )(sinks, qkv, qkv, qkv, qkv, qkv, bias_t, do, lse, delta)


def _rel_onehot_t():
    qi = jnp.arange(WINDOW, dtype=jnp.int32)[None, :] + WINDOW
    kj = jnp.arange(2 * WINDOW, dtype=jnp.int32)[:, None]
    dist = qi - kj
    max_exact = REL_BUCKETS // 2
    d = jnp.maximum(dist, 0)
    log_ratio = jnp.log(jnp.maximum(d, 1).astype(F32) / max_exact) / math.log(REL_MAX_DIST / max_exact)
    large = jnp.minimum(max_exact + (log_ratio * (REL_BUCKETS - max_exact)).astype(jnp.int32), REL_BUCKETS - 1)
    bucket = jnp.where(d < max_exact, d, large).reshape(-1)
    return (bucket[None, :] == jnp.arange(REL_BUCKETS, dtype=jnp.int32)[:, None]).astype(BF16)


def _bias_table(rel_bias_t, onehot_t):
    Hq, NB = rel_bias_t.shape
    N = onehot_t.shape[1]
    tn = _tile(N, 4096)

    def body(r_ref, oh_ref, o_ref):
        oh = oh_ref[...]
        a1, a2, a3 = _split3(r_ref[...])
        o_ref[...] = _dot(a1, oh) + _dot(a2, oh) + _dot(a3, oh)

    return pl.pallas_call(
        body, name="rel_bias_table", grid=(N // tn,),
        in_specs=[pl.BlockSpec((Hq, NB), lambda j: (0, 0)), pl.BlockSpec((NB, tn), lambda j: (0, j))],
        out_specs=pl.BlockSpec((Hq, tn), lambda j: (0, j)),
        out_shape=jax.ShapeDtypeStruct((Hq, N), F32),
        compiler_params=_params("parallel"),
    )(rel_bias_t, onehot_t)


def _bias_table_bwd(dbias, onehot_t):
    L, Hq, N = dbias.shape
    NB = onehot_t.shape[0]
    tn = _tile(N, 4096)

    def body(d_ref, oh_ref, o_ref):
        @pl.when(pl.program_id(0) == 0)
        def _():
            o_ref[...] = jnp.zeros_like(o_ref)

        d = d_ref[0]
        for l in range(1, L):
            d = d + d_ref[l]
        oh = oh_ref[...]
        a1, a2, a3 = _split3(d)
        o_ref[...] += _dot(a1, oh, NT) + _dot(a2, oh, NT) + _dot(a3, oh, NT)

    return pl.pallas_call(
        body, name="rel_bias_bwd", grid=(N // tn,),
        in_specs=[pl.BlockSpec((L, Hq, tn), lambda j: (0, 0, j)), pl.BlockSpec((NB, tn), lambda j: (0, j))],
        out_specs=pl.BlockSpec((Hq, NB), lambda j: (0, 0)),
        out_shape=jax.ShapeDtypeStruct((Hq, NB), F32),
        compiler_params=_params("arbitrary"),
    )(dbias, onehot_t)


def _gate_fwd(lat, fb_col, *, name):
    S = lat.shape[1]
    tn = _tile(S, 256)

    def body(z_ref, fb_ref, o_ref, carry):
        @pl.when(pl.program_id(0) == 0)
        def _():
            carry[...] = jnp.zeros_like(carry)

        z = z_ref[...] + fb_ref[...]
        lf = jnp.minimum(z, 0.0) - jnp.log1p(jnp.exp(-jnp.abs(z)))
        r = lax.broadcasted_iota(jnp.int32, (tn, tn), 0)
        c = lax.broadcasted_iota(jnp.int32, (tn, tn), 1)
        tri = (r <= c).astype(BF16)
        a1, a2, a3 = _split3(lf)
        cum = _dot(a1, tri) + _dot(a2, tri) + _dot(a3, tri) + carry[:, 0:1]
        o_ref[...] = cum
        carry[...] = jnp.broadcast_to(cum[:, tn - 1:tn], carry.shape)

    return pl.pallas_call(
        body, name=name, grid=(S // tn,),
        in_specs=[pl.BlockSpec((GATE_ROWS, tn), lambda i: (0, i)), pl.BlockSpec((GATE_ROWS, 1), lambda i: (0, 0))],
        out_specs=pl.BlockSpec((GATE_ROWS, tn), lambda i: (0, i)),
        out_shape=jax.ShapeDtypeStruct((GATE_ROWS, S), F32),
        scratch_shapes=[pltpu.VMEM((GATE_ROWS, LANES), F32)],
        compiler_params=_params("arbitrary"),
    )(lat, fb_col)


def _gate_bwd(lat, fb_col, dF, *, name):
    S = lat.shape[1]
    tn = _tile(S, 256)
    nt = S // tn

    def body(z_ref, fb_ref, df_ref, dz_ref, dfb_ref, carry):
        @pl.when(pl.program_id(0) == 0)
        def _():
            carry[...] = jnp.zeros_like(carry)
            dfb_ref[...] = jnp.zeros_like(dfb_ref)

        r = lax.broadcasted_iota(jnp.int32, (tn, tn), 0)
        c = lax.broadcasted_iota(jnp.int32, (tn, tn), 1)
        tri = (r >= c).astype(BF16)
        a1, a2, a3 = _split3(df_ref[...])
        dlf = _dot(a1, tri) + _dot(a2, tri) + _dot(a3, tri) + carry[:, 0:1]
        carry[...] = jnp.broadcast_to(dlf[:, 0:1], carry.shape)
        z = z_ref[...] + fb_ref[...]
        row = lax.broadcasted_iota(jnp.int32, (GATE_ROWS, tn), 0)
        dz = jnp.where(row < FOX_HEADS, dlf / (1.0 + jnp.exp(z)), 0.0)
        dz_ref[...] = dz
        dfb_ref[...] += jnp.sum(dz, axis=1, keepdims=True)

    blk = pl.BlockSpec((GATE_ROWS, tn), lambda i: (0, nt - 1 - i))
    vec = pl.BlockSpec((GATE_ROWS, 1), lambda i: (0, 0))
    return pl.pallas_call(
        body, name=name, grid=(nt,),
        in_specs=[blk, vec, blk], out_specs=[blk, vec],
        out_shape=[jax.ShapeDtypeStruct((GATE_ROWS, S), F32), jax.ShapeDtypeStruct((GATE_ROWS, 1), F32)],
        scratch_shapes=[pltpu.VMEM((GATE_ROWS, LANES), F32)],
        compiler_params=_params("arbitrary"),
    )(lat, fb_col, dF)


def _rope_tables(S):
    pos = jnp.arange(S, dtype=F32)
    inv_freq = ROPE_THETA ** (-(jnp.arange(MLA_ROPE // 2, dtype=F32) * 2.0 / MLA_ROPE))
    ang = pos[:, None] * inv_freq[None, :]
    cos, sin = jnp.cos(ang).T, jnp.sin(ang).T
    z16 = jnp.zeros_like(cos)

    def slab(lo, fill):
        def put(first, second, f):
            return jnp.concatenate([jnp.full((lo, S), f, F32), first, second, jnp.full((LANES - lo - MLA_ROPE, S), f, F32)], axis=0)
        return put(cos, cos, fill), put(-sin, z16, 0.0), put(z16, sin, 0.0)

    tq = tuple(jnp.tile(t, (MLA_HEADS, 1)) for t in slab(MLA_NOPE, 1.0))
    return tq, slab(0, 0.0)


def _rope(x, c, s1, s2):
    n = x.shape[0]
    half = MLA_ROPE // 2
    return x * c + pltpu.roll(x, n - half, 0) * s1 + pltpu.roll(x, half, 0) * s2


def _rope_t(dy, c, s1, s2):
    n = dy.shape[0]
    half = MLA_ROPE // 2
    return dy * c + pltpu.roll(dy * s1, half, 0) + pltpu.roll(dy * s2, n - half, 0)


KR_SLAB0 = MLA_Q_RANK + MLA_KV_RANK


def _mla_prep_fwd(lat, g_q, g_kv, w_uq_t, w_ukv_t, tq, tmisc, *, name):
    S = lat.shape[1]
    tn = _tile(S, 512)
    QW = MLA_HEADS * MLA_PAD

    def body(lat_ref, gq_ref, gkv_ref, wq_ref, wkv_ref, c_ref, s1_ref, s2_ref, cm_ref, s1m_ref, s2m_ref,
             nq_ref, nkv_ref, q_ref, k_ref, v_ref):
        x = pltpu.roll(lat_ref[...], LAT_ROWS - LAT_SHIFT, 0)
        nq = _col_rms(x[0:MLA_Q_RANK, :], gq_ref[...]).astype(BF16)
        nkv = _col_rms(x[MLA_Q_RANK:KR_SLAB0, :], gkv_ref[...]).astype(BF16)
        nq_ref[...] = nq
        nkv_ref[...] = nkv
        q_ref[...] = _rope(_dot(wq_ref[...], nq), c_ref[...], s1_ref[...], s2_ref[...]).astype(BF16)
        kv = _dot(wkv_ref[...], nkv).astype(BF16)
        kr = _rope(x[KR_SLAB0:LAT_ROWS, :], cm_ref[...], s1m_ref[...], s2m_ref[...]).astype(BF16)
        for h in range(MLA_HEADS):
            k_ref[h * MLA_PAD:h * MLA_PAD + MLA_NOPE, :] = kv[h * LANES:h * LANES + MLA_NOPE, :]
            k_ref[h * MLA_PAD + MLA_NOPE:(h + 1) * MLA_PAD, :] = kr[0:MLA_PAD - MLA_NOPE, :]
            v_ref[h * HEAD_DIM:(h + 1) * HEAD_DIM, :] = kv[h * LANES + MLA_NOPE:(h + 1) * LANES, :]

    def col(rows):
        return pl.BlockSpec((rows, tn), lambda i: (0, i))

    def full(a):
        return pl.BlockSpec(a.shape, lambda i: (0, 0))

    return pl.pallas_call(
        body, name=name, grid=(S // tn,),
        in_specs=[col(LAT_ROWS), full(g_q), full(g_kv), full(w_uq_t), full(w_ukv_t),
                  col(QW), col(QW), col(QW), col(LANES), col(LANES), col(LANES)],
        out_specs=[col(MLA_Q_RANK), col(MLA_KV_RANK), col(QW), col(QW), col(MLA_HEADS * HEAD_DIM)],
        out_shape=[jax.ShapeDtypeStruct((MLA_Q_RANK, S), BF16), jax.ShapeDtypeStruct((MLA_KV_RANK, S), BF16),
                   jax.ShapeDtypeStruct((QW, S), BF16), jax.ShapeDtypeStruct((QW, S), BF16),
                   jax.ShapeDtypeStruct((MLA_HEADS * HEAD_DIM, S), BF16)],
        compiler_params=_params("parallel"),
    )(lat, g_q, g_kv, w_uq_t, w_ukv_t, *tq, *tmisc)


def _mla_prep_bwd(lat, nq, nkv, g_q, g_kv, w_uq_p, w_ukv, tq, tmisc, dq, dk, dv, dflog, *, name):
    S = lat.shape[1]
    tn = _tile(S, 512)
    QW = MLA_HEADS * MLA_PAD

    def body(lat_ref, nq_ref, nkv_ref, gq_ref, gkv_ref, wq_ref, wkv_ref, c_ref, s1_ref, s2_ref,
             cm_ref, s1m_ref, s2m_ref, dq_ref, dk_ref, dv_ref, dfl_ref,
             dlat_ref, dwq_ref, dwkv_ref, dgq_ref, dgkv_ref, y_s):
        @pl.when(pl.program_id(0) == 0)
        def _():
            dwq_ref[...] = jnp.zeros_like(dwq_ref)
            dwkv_ref[...] = jnp.zeros_like(dwkv_ref)
            dgq_ref[...] = jnp.zeros_like(dgq_ref)
            dgkv_ref[...] = jnp.zeros_like(dgkv_ref)

        x = pltpu.roll(lat_ref[...], LAT_ROWS - LAT_SHIFT, 0)
        dqm = _rope_t(dq_ref[...], c_ref[...], s1_ref[...], s2_ref[...]).astype(BF16)
        dwq_ref[...] += _dot(dqm, nq_ref[...], NT)
        dx, dg = _col_rms_bwd(x[0:MLA_Q_RANK, :], gq_ref[...], _dot(wq_ref[...], dqm))
        y_s[0:MLA_Q_RANK, :] = dx
        dgq_ref[...] += dg
        dkv = jnp.concatenate(
            [part for h in range(MLA_HEADS)
             for part in (dk_ref[h * MLA_PAD:h * MLA_PAD + MLA_NOPE, :], dv_ref[h * HEAD_DIM:(h + 1) * HEAD_DIM, :])],
            axis=0).astype(BF16)
        dwkv_ref[...] += _dot(dkv, nkv_ref[...], NT)
        dx, dg = _col_rms_bwd(x[MLA_Q_RANK:KR_SLAB0, :], gkv_ref[...], _dot(wkv_ref[...], dkv))
        y_s[MLA_Q_RANK:KR_SLAB0, :] = dx
        dgkv_ref[...] += dg
        dkr = dk_ref[MLA_NOPE:MLA_PAD, :]
        for h in range(1, MLA_HEADS):
            dkr = dkr + dk_ref[h * MLA_PAD + MLA_NOPE:(h + 1) * MLA_PAD, :]
        dkr = jnp.concatenate([dkr, jnp.zeros((MLA_NOPE, tn), F32)], axis=0)
        y_s[KR_SLAB0:LAT_ROWS, :] = _rope_t(dkr, cm_ref[...], s1m_ref[...], s2m_ref[...])
        y = pltpu.roll(y_s[...], LAT_SHIFT, 0)
        row = lax.broadcasted_iota(jnp.int32, (LAT_ROWS, tn), 0)
        dfl = jnp.concatenate([dfl_ref[...], jnp.zeros((LAT_ROWS - GATE_ROWS, tn), F32)], axis=0)
        dlat_ref[...] = jnp.where(row < LAT_SHIFT, dfl, y).astype(BF16)

    def col(rows):
        return pl.BlockSpec((rows, tn), lambda i: (0, i))

    def full(a):
        return pl.BlockSpec(a.shape, lambda i: (0, 0))

    def acc(r, c):
        return pl.BlockSpec((r, c), lambda i: (0, 0))

    return pl.pallas_call(
        body, name=name, grid=(S // tn,),
        in_specs=[col(LAT_ROWS), col(MLA_Q_RANK), col(MLA_KV_RANK), full(g_q), full(g_kv),
                  full(w_uq_p), full(w_ukv), col(QW), col(QW), col(QW), col(LANES), col(LANES), col(LANES),
                  col(QW), col(QW), col(MLA_HEADS * HEAD_DIM), col(GATE_ROWS)],
        out_specs=[col(LAT_ROWS), acc(QW, MLA_Q_RANK), acc(QW, MLA_KV_RANK), acc(MLA_Q_RANK, 1), acc(MLA_KV_RANK, 1)],
        out_shape=[jax.ShapeDtypeStruct((LAT_ROWS, S), BF16), jax.ShapeDtypeStruct((QW, MLA_Q_RANK), F32),
                   jax.ShapeDtypeStruct((QW, MLA_KV_RANK), F32), jax.ShapeDtypeStruct((MLA_Q_RANK, 1), F32),
                   jax.ShapeDtypeStruct((MLA_KV_RANK, 1), F32)],
        scratch_shapes=[pltpu.VMEM((LAT_ROWS, tn), F32)],
        compiler_params=_params("arbitrary"),
    )(lat, nq, nkv, g_q, g_kv, w_uq_p, w_ukv, *tq, *tmisc, dq, dk, dv, dflog)


def _dproj_cast(dqa, dkva, dqf, dkf, dvf, dlat, *, name):
    S = dqa.shape[1]
    tn = _tile(S, 512)
    parts = (dqa, dkva, dqf, dkf, dvf, dlat)

    def body(*refs):
        o_ref = refs[-1]
        r0 = 0
        for ref in refs[:-1]:
            n = ref.shape[0]
            o_ref[r0:r0 + n, :] = ref[...].astype(BF16)
            r0 += n

    return pl.pallas_call(
        body, name=name, grid=(S // tn,),
        in_specs=[pl.BlockSpec((p.shape[0], tn), lambda i: (0, i)) for p in parts],
        out_specs=pl.BlockSpec((IN_ROWS, tn), lambda i: (0, i)),
        out_shape=jax.ShapeDtypeStruct((IN_ROWS, S), BF16),
        compiler_params=_params("parallel"),
    )(*parts)


GELU_C = math.sqrt(2.0 / math.pi)
GELU_A = 0.044715


def _conv_taps(a, halo, w_ref, b_ref, first):
    row = lax.broadcasted_iota(jnp.int32, a.shape, 0)
    h7 = jnp.where(first, 0.0, halo[7:8, :])
    h6 = jnp.where(first, 0.0, halo[6:7, :])
    a1 = jnp.where(row == 0, h7, pltpu.roll(a, 1, 0))
    a2 = jnp.where(row == 0, h6, jnp.where(row == 1, h7, pltpu.roll(a, 2, 0)))
    u = ((b_ref[...] + w_ref[0:1, :] * a2) + w_ref[1:2, :] * a1) + w_ref[2:3, :] * a
    return u, a1, a2


def _conv_specs(S, tm, tc, nc):
    hb = tm // 8
    main = lambda off: pl.BlockSpec((tm, tc), lambda j, i: (i, j + off))
    halo = lambda off: pl.BlockSpec((8, tc), lambda j, i: (jnp.maximum(i * hb - 1, 0), j + off))
    wspec = lambda off: pl.BlockSpec((3, tc), lambda j, i: (0, j + off))
    bspec = lambda off: pl.BlockSpec((1, tc), lambda j, i: (0, j + off))
    return main, halo, wspec, bspec


def _conv_geglu_fwd(a, conv_w, conv_b, *, name):
    S = a.shape[0]
    tm, tc = _tile(S, 512), _tile(D_FF, 1408)
    nc = D_FF // tc
    main, halo, wspec, bspec = _conv_specs(S, tm, tc, nc)

    def body(ag_ref, au_ref, hg_ref, hu_ref, wg_ref, wu_ref, bg_ref, bu_ref, z_ref):
        first = pl.program_id(1) == 0
        gate, _, _ = _conv_taps(ag_ref[...], hg_ref[...], wg_ref, bg_ref, first)
        up, _, _ = _conv_taps(au_ref[...], hu_ref[...], wu_ref, bu_ref, first)
        cdf = 0.5 * (1.0 + jnp.tanh(GELU_C * (gate + GELU_A * (gate * gate * gate))))
        z_ref[...] = (gate * cdf * up).astype(BF16)

    return pl.pallas_call(
        body, name=name, grid=(nc, S // tm),
        in_specs=[main(0), main(nc), halo(0), halo(nc), wspec(0), wspec(nc), bspec(0), bspec(nc)],
        out_specs=pl.BlockSpec((tm, tc), lambda j, i: (i, j)),
        out_shape=jax.ShapeDtypeStruct((S, D_FF), BF16),
        compiler_params=_params("parallel", "arbitrary"),
    )(a, a, a, a, conv_w, conv_w, conv_b, conv_b)


def _conv_geglu_bwd(a, conv_w, conv_b, dz, *, name):
    S = a.shape[0]
    tm, tc = _tile(S, 512), _tile(D_FF, 1408)
    nc = D_FF // tc
    main, halo, wspec, bspec = _conv_specs(S, tm, tc, nc)

    def body(ag_ref, au_ref, hg_ref, hu_ref, wg_ref, wu_ref, bg_ref, bu_ref, dz_ref, du_ref, dw_ref, db_ref):
        first = pl.program_id(1) == 0

        @pl.when(first)
        def _():
            dw_ref[...] = jnp.zeros_like(dw_ref)
            db_ref[...] = jnp.zeros_like(db_ref)

        gate, g1, g2 = _conv_taps(ag_ref[...], hg_ref[...], wg_ref, bg_ref, first)
        up, u1, u2 = _conv_taps(au_ref[...], hu_ref[...], wu_ref, bu_ref, first)
        dz = dz_ref[...]
        g2x = gate * gate
        th = jnp.tanh(GELU_C * (gate + GELU_A * (g2x * gate)))
        cdf = 0.5 * (1.0 + th)
        dgelu = cdf + gate * (0.5 * (1.0 - th * th) * (GELU_C * (1.0 + 3.0 * GELU_A * g2x)))
        dug = dz * up * dgelu
        duu = dz * (gate * cdf)
        du_ref[0] = dug
        du_ref[1] = duu
        for half, du, taps in ((0, dug, (g2, g1, ag_ref[...])), (1, duu, (u2, u1, au_ref[...]))):
            for tap in range(3):
                dw_ref[half, tap:tap + 1, :] += jnp.sum(du * taps[tap], axis=0, keepdims=True)
            db_ref[half] += jnp.sum(du, axis=0, keepdims=True)

    return pl.pallas_call(
        body, name=name, grid=(nc, S // tm),
        in_specs=[main(0), main(nc), halo(0), halo(nc), wspec(0), wspec(nc), bspec(0), bspec(nc),
                  pl.BlockSpec((tm, tc), lambda j, i: (i, j))],
        out_specs=[pl.BlockSpec((2, tm, tc), lambda j, i: (0, i, j)), pl.BlockSpec((2, 3, tc), lambda j, i: (0, 0, j)),
                   pl.BlockSpec((2, 1, tc), lambda j, i: (0, 0, j))],
        out_shape=[jax.ShapeDtypeStruct((2, S, D_FF), F32), jax.ShapeDtypeStruct((2, 3, D_FF), F32),
                   jax.ShapeDtypeStruct((2, 1, D_FF), F32)],
        compiler_params=_params("parallel", "arbitrary"),
    )(a, a, a, a, conv_w, conv_w, conv_b, conv_b, dz)


def _conv_bwd_input(du, conv_w, *, name):
    S = du.shape[1]
    tm, tc = _tile(S, 512), _tile(D_FF, 1408)
    nc = D_FF // tc
    nr = S // tm
    hb = tm // 8

    def body(du_ref, nx_ref, w_ref, da_ref):
        last = pl.program_id(2) == nr - 1
        d = du_ref[0]
        row = lax.broadcasted_iota(jnp.int32, d.shape, 0)
        n0 = jnp.where(last, 0.0, nx_ref[0, 0:1, :])
        n1 = jnp.where(last, 0.0, nx_ref[0, 1:2, :])
        d1 = jnp.where(row == tm - 1, n0, pltpu.roll(d, tm - 1, 0))
        d2 = jnp.where(row == tm - 1, n1, jnp.where(row == tm - 2, n0, pltpu.roll(d, tm - 2, 0)))
        da_ref[...] = (w_ref[2:3, :] * d + w_ref[1:2, :] * d1 + w_ref[0:1, :] * d2).astype(BF16)

    return pl.pallas_call(
        body, name=name, grid=(2, nc, nr),
        in_specs=[pl.BlockSpec((1, tm, tc), lambda h, j, i: (h, i, j)),
                  pl.BlockSpec((1, 8, tc), lambda h, j, i: (h, jnp.minimum((i + 1) * hb, S // 8 - 1), j)),
                  pl.BlockSpec((3, tc), lambda h, j, i: (0, h * nc + j))],
        out_specs=pl.BlockSpec((tm, tc), lambda h, j, i: (i, h * nc + j)),
        out_shape=jax.ShapeDtypeStruct((S, 2 * D_FF), BF16),
        compiler_params=_params("parallel", "parallel", "arbitrary"),
    )(du, du, conv_w)


def _adamw_update(w, g, m, v):
    m = ADAM_B1 * m + (1.0 - ADAM_B1) * g
    v = ADAM_B2 * v + (1.0 - ADAM_B2) * jnp.square(g)
    m_hat = m / (1.0 - ADAM_B1 ** ADAM_STEP)
    v_hat = v / (1.0 - ADAM_B2 ** ADAM_STEP)
    return -ADAM_LR * (m_hat / (jnp.sqrt(v_hat) + ADAM_EPS) + ADAM_WD * w), m, v


def _adamw(w, g, m, v, *, name):
    L, A, B = w.shape
    ta = _tile(A, ROW_TILE)

    def body(w_ref, g_ref, m_ref, v_ref, d_ref, mo_ref, vo_ref):
        d_ref[...], mo_ref[...], vo_ref[...] = _adamw_update(w_ref[...], g_ref[...], m_ref[...], v_ref[...])

    blk = pl.BlockSpec((None, ta, B), lambda l, i: (l, i, 0))
    shp = jax.ShapeDtypeStruct((L, A, B), F32)
    return pl.pallas_call(
        body, name=name, grid=(L, A // ta),
        in_specs=[blk] * 4, out_specs=[blk] * 3, out_shape=[shp] * 3,
        compiler_params=_params("parallel", "parallel"),
    )(w, g, m, v)


def _scalar(v):
    return jnp.reshape(v, (1,)).astype(jnp.int32)


def _adamw_halves(w, g_mine, g_other, m, v, *, name):
    L, A, B = w.shape
    ta = _tile(A // 2, ROW_TILE)
    nb = A // 2 // ta

    def body(c_ref, w_ref, gm_ref, go_ref, m_ref, v_ref, g_ref, d_ref, mo_ref, vo_ref):
        g = jnp.where(pl.program_id(1) // nb == c_ref[0], gm_ref[...], go_ref[...])
        g_ref[...] = g
        d_ref[...], mo_ref[...], vo_ref[...] = _adamw_update(w_ref[...], g, m_ref[...], v_ref[...])

    blk = pl.BlockSpec((None, ta, B), lambda l, i, c_ref: (l, i, 0))
    half = pl.BlockSpec((None, ta, B), lambda l, i, c_ref: (l, i % nb, 0))
    shp = jax.ShapeDtypeStruct((L, A, B), F32)
    return pl.pallas_call(
        body, name=name,
        grid_spec=pltpu.PrefetchScalarGridSpec(num_scalar_prefetch=1, grid=(L, A // ta),
                                               in_specs=[blk, half, half, blk, blk], out_specs=[blk] * 4),
        out_shape=[shp] * 4,
        compiler_params=_params("parallel", "parallel"),
    )(_scalar(lax.axis_index("c")), w, g_mine, g_other, m, v)


def _chip_index():
    return 2 * lax.axis_index("x") + lax.axis_index("y")


def _pair_sum(g, recv, *, name):
    n, A, B = g.shape
    ta = _tile(A // 2, ROW_TILE)
    nb = A // 2 // ta

    def body(c_ref, g_ref, r_ref, o_ref):
        o_ref[...] = g_ref[...] + r_ref[...]

    return pl.pallas_call(
        body, name=name,
        grid_spec=pltpu.PrefetchScalarGridSpec(
            num_scalar_prefetch=1, grid=(n, nb),
            in_specs=[pl.BlockSpec((None, ta, B), lambda s, r, c_ref: (s, c_ref[0] * nb + r, 0)),
                      pl.BlockSpec((None, ta, B), lambda s, r, c_ref: (s, r, 0))],
            out_specs=pl.BlockSpec((None, ta, B), lambda s, r, c_ref: (s, r, 0))),
        out_shape=jax.ShapeDtypeStruct((n, A // 2, B), F32),
        compiler_params=_params("parallel", "parallel"),
    )(_scalar(lax.axis_index("c")), g, recv)


def _chip_sum(landed, own, *, name):
    n, A2, B = landed.shape
    ta = _tile(A2, ROW_TILE)

    def body(me_ref, *refs):
        slots, own_ref, o_ref = refs[:n], refs[n], refs[n + 1]
        parts = [jnp.where(me_ref[0] == s, own_ref[...], slots[s][...]) for s in range(n)]
        o_ref[...] = ((parts[0] + parts[1]) + parts[2]) + parts[3]

    def slot(s):
        return pl.BlockSpec((None, ta, B), lambda r, me_ref: (jnp.where(me_ref[0] == s, (s + 1) % n, s), r, 0))

    return pl.pallas_call(
        body, name=name,
        grid_spec=pltpu.PrefetchScalarGridSpec(
            num_scalar_prefetch=1, grid=(A2 // ta,),
            in_specs=[slot(s) for s in range(n)] + [pl.BlockSpec((None, ta, B), lambda r, me_ref: (me_ref[0], r, 0))],
            out_specs=pl.BlockSpec((ta, B), lambda r, me_ref: (r, 0))),
        out_shape=jax.ShapeDtypeStruct((A2, B), F32),
        compiler_params=_params("parallel"),
    )(_scalar(_chip_index()), *([landed] * n), own)


HBM_SPEC = pl.BlockSpec(memory_space=pl.ANY)
COMM_PARAMS = pltpu.CompilerParams(has_side_effects=True)


def _mesh_pos():
    return lax.axis_index("x"), lax.axis_index("y"), lax.axis_index("c")


def _other_chips(x, y):
    return [(1 - x, y), (x, 1 - y), (1 - x, 1 - y)]


def _my_layers(c):
    return pl.ds(HALF_DEPTH * c, HALF_DEPTH)


def _remote(src, dst, send_sems, recv_sems, k, to):
    return pltpu.make_async_remote_copy(src_ref=src, dst_ref=dst, send_sem=send_sems.at[k], recv_sem=recv_sems.at[k],
                                        device_id=to, device_id_type=MESH)


def _gather_shards(shards):
    n = len(shards)

    def body(*refs):
        ins, outs = refs[:n], refs[n:2 * n]
        send_sems, recv_sems = refs[2 * n:]
        x, y, c = _mesh_pos()
        me = 2 * x + y
        chips = _other_chips(x, y)
        sibling = (x, y, 1 - c)
        mine, other = _my_layers(c), _my_layers(1 - c)
        first = [_remote(ins[k].at[mine], outs[k].at[mine, me], send_sems, recv_sems, 6 * k + j, (px, py, c))
                 for j, (px, py) in enumerate(chips) for k in range(n)]
        for cp in first:
            cp.start()
        passed = []
        for j, (px, py) in enumerate(chips):
            for k in range(n):
                landed = outs[k].at[mine, 2 * px + py]
                _remote(landed, landed, send_sems, recv_sems, 6 * k + j, (px, py, c)).wait_recv()
                cp = _remote(landed, landed, send_sems, recv_sems, 6 * k + 3 + j, sibling)
                cp.start()
                passed.append(cp)
        for j, (px, py) in enumerate(chips):
            for k in range(n):
                landed = outs[k].at[other, 2 * px + py]
                _remote(landed, landed, send_sems, recv_sems, 6 * k + 3 + j, sibling).wait_recv()
        for cp in first + passed:
            cp.wait_send()

    gathered = pl.pallas_call(
        body, name="gather_weight_shards",
        in_specs=[HBM_SPEC] * n, out_specs=[HBM_SPEC] * n,
        out_shape=[jax.ShapeDtypeStruct((s.shape[0], N_CHIPS) + s.shape[1:], s.dtype) for s in shards],
        scratch_shapes=[pltpu.SemaphoreType.DMA((6 * n,)), pltpu.SemaphoreType.DMA((6 * n,))],
        compiler_params=COMM_PARAMS,
    )(*shards)
    return [_place_own(g, s, name="place_own_shard") for g, s in zip(gathered, shards)]


def _place_own(gathered, shard, *, name):
    L, A, B = shard.shape
    ta = _tile(A, ROW_TILE)

    def body(me_ref, s_ref, g_ref, o_ref):
        o_ref[...] = s_ref[...]

    return pl.pallas_call(
        body, name=name,
        grid_spec=pltpu.PrefetchScalarGridSpec(
            num_scalar_prefetch=1, grid=(L, A // ta),
            in_specs=[pl.BlockSpec((None, ta, B), lambda l, r, me_ref: (l, r, 0)), HBM_SPEC],
            out_specs=pl.BlockSpec((None, None, ta, B), lambda l, r, me_ref: (l, me_ref[0], r, 0))),
        out_shape=jax.ShapeDtypeStruct(gathered.shape, gathered.dtype),
        input_output_aliases={2: 0},
        compiler_params=_params("parallel", "parallel"),
    )(_scalar(_chip_index()), shard, gathered)


def _place_own_layer(gathered, shard, *, name):
    return _place_own(gathered[None], shard[None], name=name)[0]


def _half_rows(rows, c, align=8):
    assert (rows // 2) % align == 0
    return pl.ds(pl.multiple_of(c * (rows // 2), align), rows // 2)


BF16_ROWS = 16


def _gather_copies(srcs, lands, send_sems, recv_sems):
    x, y, c = _mesh_pos()
    me = 2 * x + y
    out = []
    for k in range(len(srcs)):
        rows = _half_rows(srcs[k].shape[0], c, BF16_ROWS)
        for j, (px, py) in enumerate(_other_chips(x, y)):
            send = _remote(srcs[k].at[rows], lands[k].at[me, rows], send_sems, recv_sems, 3 * k + j, (px, py, c))
            recv = _remote(srcs[k].at[rows], lands[k].at[2 * px + py, rows], send_sems, recv_sems, 3 * k + j, (px, py, c))
            out.append((send, recv))
    return out


def _gather_start(srcs):
    nl, n = len(srcs), len(srcs[0])
    lands = [[lax.empty((N_CHIPS,) + s.shape, s.dtype) for s in sl] for sl in srcs]
    flat = [a for l in range(nl) for a in srcs[l] + lands[l]]

    def body(*refs):
        bufs, sems, token = refs[:len(flat)], refs[len(flat):len(flat) + 2 * nl], refs[-1]
        for l in range(nl):
            mine = bufs[2 * n * l:2 * n * (l + 1)]
            for send, _ in _gather_copies(mine[:n], mine[n:], sems[2 * l], sems[2 * l + 1]):
                send.start()
        token[...] = jnp.zeros_like(token)

    res = pl.pallas_call(
        body, name="weight_gather_start",
        in_specs=[HBM_ONLY] * len(flat),
        out_specs=[SEM_SPEC] * (2 * nl) + [HBM_ONLY] * len(flat) + [pl.BlockSpec(memory_space=pltpu.VMEM)],
        out_shape=[pltpu.SemaphoreType.DMA((3 * n,))] * (2 * nl) + [pltpu.HBM(a.shape, a.dtype) for a in flat]
        + [jax.ShapeDtypeStruct((8, LANES), F32)],
        input_output_aliases={i: 2 * nl + i for i in range(len(flat))},
        compiler_params=SPLIT_PARAMS,
    )(*[pltpu.with_memory_space_constraint(a, pltpu.HBM) for a in flat])
    bufs = res[2 * nl:2 * nl + len(flat)]
    state = [(res[2 * l], res[2 * l + 1], list(bufs[2 * n * l:2 * n * l + n]), list(bufs[2 * n * l + n:2 * n * (l + 1)]))
             for l in range(nl)]
    return state, res[-1][0:1, 0:1]


def _gather_wait(state, after, *, name):
    send_sems, recv_sems, srcs, lands = state
    n = len(srcs)

    def body(*refs):
        for send, recv in _gather_copies(refs[:n], refs[n:2 * n], refs[2 * n], refs[2 * n + 1]):
            send.wait_send()
            recv.wait_recv()

    res = pl.pallas_call(
        body, name=name,
        in_specs=[HBM_ONLY] * (2 * n) + [SEM_SPEC, SEM_SPEC, HBM_SPEC],
        out_specs=[HBM_ONLY] * (2 * n),
        out_shape=[pltpu.HBM(a.shape, a.dtype) for a in srcs + lands],
        input_output_aliases={i: i for i in range(2 * n)},
        compiler_params=SPLIT_PARAMS,
    )(*srcs, *lands, send_sems, recv_sems, after)
    return list(res[:n]), list(res[n:])


def _gather_forward(lands, *, name):
    n = len(lands)

    def body(*refs):
        bufs, outs = refs[:n], refs[n:2 * n]
        send_sems, recv_sems = refs[2 * n:]
        x, y, c = _mesh_pos()
        copies, waits = [], []
        for k in range(n):
            a = lands[k].shape[1]
            for j, (px, py) in enumerate(_other_chips(x, y)):
                mine = 2 * px + py, _half_rows(a, c, BF16_ROWS)
                copies.append(_remote(bufs[k].at[mine], outs[k].at[mine], send_sems, recv_sems, 3 * k + j, (x, y, 1 - c)))
                lands_here = outs[k].at[2 * px + py, _half_rows(a, 1 - c, BF16_ROWS)]
                waits.append(_remote(lands_here, lands_here, send_sems, recv_sems, 3 * k + j, (x, y, 1 - c)))
        for cp in copies:
            cp.start()
        for cp in waits:
            cp.wait_recv()
        for cp in copies:
            cp.wait_send()

    return pl.pallas_call(
        body, name=name,
        in_specs=[HBM_SPEC] * n, out_specs=[HBM_SPEC] * n,
        out_shape=[jax.ShapeDtypeStruct(a.shape, a.dtype) for a in lands],
        scratch_shapes=[pltpu.SemaphoreType.DMA((3 * n,)), pltpu.SemaphoreType.DMA((3 * n,))],
        input_output_aliases={i: i for i in range(n)},
        compiler_params=COMM_PARAMS,
    )(*lands)


def _sibling_exchange(gs, *, name):
    n = len(gs)

    def body(*refs):
        ins, outs = refs[:n], refs[n:2 * n]
        send_sems, recv_sems = refs[2 * n:]
        x, y, c = _mesh_pos()
        copies = [_remote(ins[k].at[:, _half_rows(gs[k].shape[1], 1 - c)], outs[k], send_sems, recv_sems, k, (x, y, 1 - c))
                  for k in range(n)]
        for cp in copies:
            cp.start()
        for cp in copies:
            cp.wait()

    return pl.pallas_call(
        body, name=name,
        in_specs=[HBM_SPEC] * n, out_specs=[HBM_SPEC] * n,
        out_shape=[jax.ShapeDtypeStruct((g.shape[0], g.shape[1] // 2, g.shape[2]), g.dtype) for g in gs],
        scratch_shapes=[pltpu.SemaphoreType.DMA((n,)), pltpu.SemaphoreType.DMA((n,))],
        compiler_params=COMM_PARAMS,
    )(*gs)


HBM_ONLY = pl.BlockSpec(memory_space=pltpu.HBM)
SEM_SPEC = pl.BlockSpec(memory_space=pltpu.SEMAPHORE)
SPLIT_PARAMS = pltpu.CompilerParams(has_side_effects=pltpu.SideEffectType.DATAFLOW_SIDE_EFFECTING)


def _scatter_copies(srcs, lands, send_sems, recv_sems):
    x, y, c = _mesh_pos()
    me = 2 * x + y
    out = []
    for k in range(len(srcs)):
        for j, (px, py) in enumerate(_other_chips(x, y)):
            s = 2 * px + py
            send = _remote(srcs[k].at[s], lands[k].at[me], send_sems, recv_sems, 3 * k + j, (px, py, c))
            recv = _remote(srcs[k].at[s], lands[k].at[s], send_sems, recv_sems, 3 * k + j, (px, py, c))
            out.append((send, recv))
    return out


def _scatter_start(ps, *, name):
    n = len(ps)
    lands = [lax.empty(p.shape, p.dtype) for p in ps]

    def body(*refs):
        srcs, zones = refs[:n], refs[n:2 * n]
        send_sems, recv_sems, token = refs[2 * n], refs[2 * n + 1], refs[-1]
        for send, _ in _scatter_copies(srcs, zones, send_sems, recv_sems):
            send.start()
        token[...] = jnp.zeros_like(token)

    hbm = lambda a: pltpu.HBM(a.shape, a.dtype)
    res = pl.pallas_call(
        body, name=name,
        in_specs=[HBM_ONLY] * (2 * n),
        out_specs=[SEM_SPEC, SEM_SPEC] + [HBM_ONLY] * (2 * n) + [pl.BlockSpec(memory_space=pltpu.VMEM)],
        out_shape=[pltpu.SemaphoreType.DMA((3 * n,)), pltpu.SemaphoreType.DMA((3 * n,))] + [hbm(a) for a in ps + lands]
        + [jax.ShapeDtypeStruct((8, LANES), F32)],
        input_output_aliases={i: 2 + i for i in range(2 * n)},
        compiler_params=SPLIT_PARAMS,
    )(*[pltpu.with_memory_space_constraint(a, pltpu.HBM) for a in ps + lands])
    return (res[0], res[1], list(res[2:2 + n]), list(res[2 + n:2 + 2 * n])), res[-1][0:1, 0:1]


def _scatter_wait(started, after, *, name):
    nl = len(started)
    n = len(started[0][2])
    flat = [a for (_, _, ps, lands) in started for a in ps + lands]

    def body(*refs):
        bufs = refs[:2 * n * nl]
        sems = refs[2 * n * nl:2 * n * nl + 2 * nl]
        for l in range(nl):
            srcs = bufs[2 * n * l:2 * n * l + n]
            zones = bufs[2 * n * l + n:2 * n * (l + 1)]
            for send, recv in _scatter_copies(srcs, zones, sems[2 * l], sems[2 * l + 1]):
                send.wait_send()
                recv.wait_recv()

    res = pl.pallas_call(
        body, name=name,
        in_specs=[HBM_ONLY] * len(flat) + [SEM_SPEC] * (2 * nl) + [HBM_SPEC],
        out_specs=[HBM_ONLY] * len(flat),
        out_shape=[pltpu.HBM(a.shape, a.dtype) for a in flat],
        input_output_aliases={i: i for i in range(len(flat))},
        compiler_params=SPLIT_PARAMS,
    )(*flat, *[s for (ss, rs, _, _) in started for s in (ss, rs)], after)
    return [(list(res[2 * n * l:2 * n * l + n]), list(res[2 * n * l + n:2 * n * (l + 1)])) for l in range(nl)]


def _sibling_share(hs):
    n = len(hs)

    def body(*refs):
        ins, outs = refs[:n], refs[n:2 * n]
        send_sems, recv_sems = refs[2 * n:]
        x, y, c = _mesh_pos()
        copies = [_remote(ins[k], outs[k], send_sems, recv_sems, k, (x, y, 1 - c)) for k in range(n)]
        for cp in copies:
            cp.start()
        for cp in copies:
            cp.wait()

    return pl.pallas_call(
        body, name="grad_sibling_share",
        in_specs=[HBM_SPEC] * n, out_specs=[HBM_SPEC] * n,
        out_shape=[jax.ShapeDtypeStruct(h.shape, h.dtype) for h in hs],
        scratch_shapes=[pltpu.SemaphoreType.DMA((n,)), pltpu.SemaphoreType.DMA((n,))],
        compiler_params=COMM_PARAMS,
    )(*hs)


def _allreduce_small(part):
    rows, C = part.shape

    def body(p_ref, o_ref, slots, send_sems, recv_sems):
        x, y, c = _mesh_pos()
        me = 4 * x + 2 * y + c
        slots[me] = p_ref[...]
        copies = []
        for k in range(1, 8):
            kx, ky, kc = (k >> 2) & 1, (k >> 1) & 1, k & 1
            peer = (x ^ kx if kx else x, y ^ ky if ky else y, c ^ kc if kc else c)
            cp = _remote(p_ref, slots.at[me], send_sems, recv_sems, k - 1, peer)
            cp.start()
            copies.append((cp, peer))
        for k, (cp, peer) in enumerate(copies):
            src = 4 * peer[0] + 2 * peer[1] + peer[2]
            _remote(p_ref, slots.at[src], send_sems, recv_sems, k, peer).wait_recv()
        for cp, _ in copies:
            cp.wait_send()
        total = slots[0]
        for d in range(1, 8):
            total = total + slots[d]
        o_ref[...] = total

    return pl.pallas_call(
        body, name="small_grad_allreduce",
        in_specs=[pl.BlockSpec(memory_space=pltpu.VMEM)], out_specs=pl.BlockSpec(memory_space=pltpu.VMEM),
        out_shape=jax.ShapeDtypeStruct((rows, C), F32),
        scratch_shapes=[pltpu.VMEM((8, rows, C), F32), pltpu.SemaphoreType.DMA((7,)), pltpu.SemaphoreType.DMA((7,))],
        compiler_params=pltpu.CompilerParams(has_side_effects=True, vmem_limit_bytes=VMEM_LIMIT_BYTES),
    )(part)


def _pad_w_uq(w):
    lead = w.shape[:-1]
    w = w.reshape(lead + (MLA_HEADS, MLA_QK))
    w = jnp.concatenate([w, jnp.zeros(lead + (MLA_HEADS, MLA_PAD - MLA_QK), w.dtype)], axis=-1)
    return w.reshape(lead + (MLA_HEADS * MLA_PAD,))


def _unpad_w_uq(g):
    lead = g.shape[:-1]
    return g.reshape(lead + (MLA_HEADS, MLA_PAD))[..., :MLA_QK].reshape(lead + (MLA_HEADS * MLA_QK,))


def _t(a):
    return jnp.swapaxes(a, -1, -2)


def _cols_of_shards(g):
    L, n, A, B = g.shape
    return g.transpose(0, 2, 1, 3).reshape(L, A, n * B)


def _shards_of_cols(w):
    A, NB = w.shape
    return w.reshape(A, N_CHIPS, NB // N_CHIPS).transpose(1, 0, 2)


BIG = ("w_in", "w_uq", "w_ukv", "w_out", "w_up", "w_down")
SMALL = ("attn_pre_norm", "forget_bias", "swa_sinks", "rel_bias", "q_latent_norm", "kv_latent_norm", "group_norm",
         "attn_post_norm", "ffn_pre_norm", "conv_b", "ffn_post_norm")
WEIGHTS = ("attn_pre_norm", "w_in", "forget_bias", "swa_sinks", "rel_bias", "q_latent_norm", "w_uq", "kv_latent_norm",
           "w_ukv", "group_norm", "w_out", "attn_post_norm", "ffn_pre_norm", "w_up", "conv_w", "conv_b", "w_down",
           "ffn_post_norm")


def _pack(arrs, cols, row_mult):
    flat = jnp.concatenate([a.reshape(-1) for a in arrs])
    n = flat.shape[0]
    per = cols * row_mult
    total = -(-n // per) * per
    return jnp.pad(flat, (0, total - n)).reshape(total // cols, cols)


def _unpack(packed, shapes):
    flat = packed.reshape(-1)
    out, off = [], 0
    for shp in shapes:
        n = int(np.prod(shp))
        out.append(flat[off:off + n].reshape(shp))
        off += n
    return out


LAYER_KEYS = ("w_qkv_t", "w_lat_t", "w_in_t", "w_uq_p", "w_uq_t", "w_ukv", "w_ukv_t", "w_out", "w_up", "w_down")


def _layer_weights(gathered):
    cols = lambda g: g.transpose(1, 0, 2).reshape(g.shape[1], N_CHIPS * g.shape[2])
    w_in_t = _t(gathered["w_in"]).reshape(IN_COLS, D_MODEL)
    w_in_t = jnp.pad(w_in_t, ((0, IN_ROWS - IN_COLS), (0, 0)))
    w_uq_p = _pad_w_uq(cols(gathered["w_uq"]))
    w_ukv = cols(gathered["w_ukv"])
    return dict(w_qkv_t=w_in_t[:QKV_ROWS], w_lat_t=w_in_t[QKV_ROWS:], w_in_t=w_in_t, w_uq_p=w_uq_p, w_uq_t=_t(w_uq_p),
                w_ukv=w_ukv, w_ukv_t=_t(w_ukv), w_out=gathered["w_out"].reshape(D_MODEL, D_MODEL), w_up=gathered["w_up"],
                w_down=gathered["w_down"].reshape(D_FF, D_MODEL))


def _local_step(x, target, W, layer_weights, layer_done):
    W = dict(W, **{key: [None] * DEPTH for key in LAYER_KEYS})
    S = x.shape[0]
    tq_tabs, tm_tabs = _rope_tables(S)
    onehot_t = _rel_onehot_t()
    bias_t = _bias_table(W["rel_bias"].T, onehot_t).reshape(SWA_Q_HEADS, 2 * WINDOW, WINDOW)
    row = lambda a: a.reshape(1, -1)
    col = lambda a: a.reshape(-1, 1)
    fox_rows = (FOX_ROW0, FOX_ROW0 + FOX_HEADS * HEAD_DIM, FOX_ROW0 + 2 * FOX_HEADS * HEAD_DIM, SWA_Q_HEADS)
    fox = dict(rows=fox_rows, H=FOX_HEADS, Dk=HEAD_DIM, Dv=HEAD_DIM, scale=HEAD_DIM ** -0.5)
    mla = dict(rows=(0, 0, 0, SWA_Q_HEADS + FOX_HEADS), H=MLA_HEADS, Dk=MLA_PAD, Dv=HEAD_DIM, scale=MLA_QK ** -0.5)

    saved = []
    h = _rms_fwd(x, row(W["attn_pre_norm"][0]), name="rms_in")
    for l in range(DEPTH):
        sv = {"x0": x, "h1": h}
        for key, val in layer_weights(l, h).items():
            W[key][l] = val
        qkv = _matmul(W["w_qkv_t"][l], h, tb=True, out_dtype=BF16, name="proj_qkv")
        lat = _matmul(W["w_lat_t"][l], h, tb=True, name="proj_lat")
        oa, lse_a = _swa_fwd(qkv, bias_t, W["swa_sinks"][l], name="swa_fwd")
        fb_col = jnp.pad(col(W["forget_bias"][l]), ((0, GATE_ROWS - FOX_HEADS), (0, 0)))
        f4 = _gate_fwd(lat, fb_col, name="fox_gate_fwd")[:FOX_HEADS]
        f_row, f_col = f4[:, None, :], f4.T
        of, lse_f = _attn_fwd(qkv, qkv, qkv, f_row=f_row, f_col=f_col, name="fox_fwd", **fox)
        nq, nkv, qm, km, vm = _mla_prep_fwd(lat, col(W["q_latent_norm"][l]), col(W["kv_latent_norm"][l]), W["w_uq_t"][l],
                                            W["w_ukv_t"][l], tq_tabs, tm_tabs, name="mla_prep_fwd")
        oc, lse_c = _attn_fwd(qm, km, vm, name="mla_fwd", **mla)
        mixed = _group_norm_fwd(oa, of, oc, col(W["group_norm"][l]), name="group_norm_fwd")
        y = _matmul(mixed, W["w_out"][l], ta=True, name="proj_out")
        x1, h2 = _resid_rms(x, y, row(W["attn_post_norm"][l]), row(W["ffn_pre_norm"][l]), name="attn_resid")
        a = _matmul(h2, W["w_up"][l], b_shards=True, name="ffn_up")
        z = _conv_geglu_fwd(a, W["conv_w"][l], row(W["conv_b"][l]), name="conv_geglu_fwd")
        y2 = _matmul(z, W["w_down"][l], name="ffn_down")
        g_next = row(W["attn_pre_norm"][l + 1]) if l + 1 < DEPTH else None
        x2, h_next = _resid_rms(x1, y2, row(W["ffn_post_norm"][l]), g_next, name="ffn_resid")
        sv.update(qkv=qkv, lat=lat, oa=oa, lse_a=lse_a, fb_col=fb_col, f_row=f_row, f_col=f_col, of=of, lse_f=lse_f,
                  nq=nq, nkv=nkv, qm=qm, km=km, vm=vm, oc=oc, lse_c=lse_c, mixed=mixed, y=y, x1=x1, h2=h2, a=a, z=z, y2=y2)
        saved.append(sv)
        x, h = x2, h_next

    loss, dx = _loss_head(x, target)

    G = {k: [None] * DEPTH for k in WEIGHTS if k != "rel_bias" and k not in BIG}
    dbias_layers = [None] * DEPTH
    for l in reversed(range(DEPTH)):
        sv = saved[l]
        gb = {}
        dy2, dg = _rms_bwd(sv["y2"], row(W["ffn_post_norm"][l]), dx, out_dtype=BF16, name="ffn_post_bwd")
        G["ffn_post_norm"][l] = dg[0]
        dz = _matmul(dy2, W["w_down"][l], tb=True, name="ffn_down_dx")
        gb["w_down"] = _matmul(sv["z"], dy2, ta=True, name="ffn_down_dw").reshape(N_CHIPS, D_FF // N_CHIPS, D_MODEL)
        du, dcw, dcb = _conv_geglu_bwd(sv["a"], W["conv_w"][l], row(W["conv_b"][l]), dz, name="conv_geglu_bwd")
        G["conv_w"][l] = dcw.transpose(1, 0, 2).reshape(3, 2 * D_FF)
        G["conv_b"][l] = dcb.reshape(2 * D_FF)
        da = _conv_bwd_input(du, W["conv_w"][l], name="conv_bwd_input")
        dh2 = _matmul(da, W["w_up"][l], tb=True, b_shards=True, name="ffn_up_dx")
        gb["w_up"] = _matmul(sv["h2"], da, ta=True, out_shards=True, name="ffn_up_dw")
        dx1, dg = _rms_bwd(sv["x1"], row(W["ffn_pre_norm"][l]), dh2, resid=dx, out_dtype=F32, name="ffn_pre_bwd")
        G["ffn_pre_norm"][l] = dg[0]
        dy, dg = _rms_bwd(sv["y"], row(W["attn_post_norm"][l]), dx1, out_dtype=BF16, name="attn_post_bwd")
        G["attn_post_norm"][l] = dg[0]
        dmixed = _matmul(W["w_out"][l], dy, tb=True, name="proj_out_dx")
        gb["w_out"] = _matmul(sv["mixed"], dy, name="proj_out_dw").reshape(N_CHIPS, D_MODEL // N_CHIPS, D_MODEL)
        doa, dof, doc, dg, delta = _group_norm_bwd(sv["oa"], sv["of"], sv["oc"], col(W["group_norm"][l]), dmixed,
                                                   name="group_norm_bwd")
        G["group_norm"][l] = dg[:, 0]
        dqa, dkva, dbias_l, dsink = _swa_bwd(sv["qkv"], bias_t, W["swa_sinks"][l], doa, sv["lse_a"],
                                             delta.reshape(-1, S), name="swa_bwd")
        dbias_layers[l] = dbias_l.reshape(SWA_Q_HEADS, -1)
        G["swa_sinks"][l] = dsink[:, 0]
        dqf, dkf, dvf, dfk = _attn_bwd(sv["qkv"], sv["qkv"], sv["qkv"], do=dof, lse=sv["lse_f"], delta=delta,
                                       f_row=sv["f_row"], f_col=sv["f_col"], name="fox_bwd", **fox)
        dF = jnp.pad(dfk.T, ((0, GATE_ROWS - FOX_HEADS), (0, 0)))
        dflog, dfb = _gate_bwd(sv["lat"], sv["fb_col"], dF, name="fox_gate_bwd")
        G["forget_bias"][l] = dfb[:FOX_HEADS, 0]
        dqm, dkm, dvm = _attn_bwd(sv["qm"], sv["km"], sv["vm"], do=doc, lse=sv["lse_c"], delta=delta, name="mla_bwd", **mla)
        dlat, dwq_t, dwkv_t, dgq, dgkv = _mla_prep_bwd(
            sv["lat"], sv["nq"], sv["nkv"], col(W["q_latent_norm"][l]), col(W["kv_latent_norm"][l]), W["w_uq_p"][l],
            W["w_ukv"][l], tq_tabs, tm_tabs, dqm, dkm, dvm, dflog, name="mla_prep_bwd")
        gb["w_uq"], gb["w_ukv"] = _shards_of_cols(_unpad_w_uq(dwq_t.T)), _shards_of_cols(dwkv_t.T)
        G["q_latent_norm"][l], G["kv_latent_norm"][l] = dgq[:, 0], dgkv[:, 0]
        dproj = _dproj_cast(dqa, dkva, dqf, dkf, dvf, dlat, name="dproj_cast")
        dh1 = _matmul(dproj, W["w_in_t"][l], ta=True, name="proj_in_dx")
        dw_in_t = _matmul(dproj, sv["h1"], name="proj_in_dw")
        gb["w_in"] = _t(dw_in_t[:IN_COLS].reshape(N_CHIPS, IN_COLS // N_CHIPS, D_MODEL))
        token = layer_done(l, gb)
        dx, dg = _rms_bwd(sv["x0"], row(W["attn_pre_norm"][l]) + token, dh1, resid=dx1, out_dtype=F32, name="attn_pre_bwd")
        G["attn_pre_norm"][l] = dg[0]

    grads = {k: jnp.stack(v) for k, v in G.items()}
    grads["rel_bias"] = _bias_table_bwd(jnp.stack(dbias_layers), onehot_t).T
    return loss, dx, grads


def kernel(x, attn_pre_norm, w_in, forget_bias, swa_sinks, rel_bias, q_latent_norm, w_uq, kv_latent_norm, w_ukv, group_norm, w_out, attn_post_norm, ffn_pre_norm, w_up, conv_w, conv_b, w_down, ffn_post_norm, loss_target, m_attn_pre_norm, m_w_in, m_forget_bias, m_swa_sinks, m_rel_bias, m_q_latent_norm, m_w_uq, m_kv_latent_norm, m_w_ukv, m_group_norm, m_w_out, m_attn_post_norm, m_ffn_pre_norm, m_w_up, m_conv_w, m_conv_b, m_w_down, m_ffn_post_norm, v_attn_pre_norm, v_w_in, v_forget_bias, v_swa_sinks, v_rel_bias, v_q_latent_norm, v_w_uq, v_kv_latent_norm, v_w_ukv, v_group_norm, v_w_out, v_attn_post_norm, v_ffn_pre_norm, v_w_up, v_conv_w, v_conv_b, v_w_down, v_ffn_post_norm):
    args = dict(locals())
    w = {k: args[k] for k in WEIGHTS}
    m = {k: args["m_" + k] for k in WEIGHTS}
    v = {k: args["v_" + k] for k in WEIGHTS}

    conv_w_full = _cols_of_shards(_gather_shards([w["conv_w"]])[0])
    gather_state, token = _gather_start([[w[k][l].astype(BF16) for k in BIG] for l in range(DEPTH)])
    W = {k: w[k] for k in SMALL}
    W["attn_pre_norm"] = W["attn_pre_norm"] + token
    W["conv_w"] = conv_w_full

    def layer_weights(l, after):
        srcs, lands = _gather_wait(gather_state[l], after, name=f"weight_gather_wait_{l}")
        lands = _gather_forward(lands, name=f"weight_gather_forward_{l}")
        lands = [_place_own_layer(g, s, name="place_own_shard") for g, s in zip(lands, srcs)]
        return _layer_weights(dict(zip(BIG, lands)))

    started = [None] * DEPTH

    def layer_done(l, gb):
        gs = [gb[k] for k in BIG]
        recv = _sibling_exchange(gs, name=f"grad_sibling_exchange_{l}")
        pair = [_pair_sum(gk, rk, name="grad_pair_sum") for gk, rk in zip(gs, recv)]
        started[l], token = _scatter_start(pair, name=f"grad_scatter_start_{l}")
        return token

    loss_part, dx, g = _local_step(x[0], loss_target[0], W, layer_weights, layer_done)
    loss = lax.psum(loss_part, ("x", "y", "c"))

    landed = _scatter_wait(started, dx, name="grad_scatter_wait")
    mine = [jnp.stack([_chip_sum(zones[i], pair[i], name="grad_chip_sum") for pair, zones in landed]) for i in range(len(BIG))]
    other = _sibling_share(mine)
    out_g, out_d, out_m, out_v = {}, {}, {}, {}
    for k, g_mine, g_other in zip(BIG, mine, other):
        out_g[k], out_d[k], out_m[k], out_v[k] = _adamw_halves(w[k], g_mine, g_other, m[k], v[k], name="adamw_" + k)

    small_shapes = [w[k].shape for k in SMALL]
    reduced = _allreduce_small(_pack([g[k] for k in SMALL] + [g["conv_w"]], LANES, 8))
    *g_small, g_cw = _unpack(reduced, small_shapes + [g["conv_w"].shape])
    chip = 2 * lax.axis_index("x") + lax.axis_index("y")
    g_small.append(lax.dynamic_slice_in_dim(g_cw, chip * FF_SHARD, FF_SHARD, axis=2))
    names = SMALL + ("conv_w",)
    shapes = small_shapes + [w["conv_w"].shape]
    packed = lambda arrs: _pack(arrs, LANES, ROW_TILE)[None]
    d_s, m_s, v_s = _adamw(packed([w[k] for k in names]), packed(g_small), packed([m[k] for k in names]),
                           packed([v[k] for k in names]), name="adamw_small")
    out_g.update(zip(names, g_small))
    out_d.update(zip(names, _unpack(d_s, shapes)))
    out_m.update(zip(names, _unpack(m_s, shapes)))
    out_v.update(zip(names, _unpack(v_s, shapes)))

    return (loss, dx[None], *[out_g[k] for k in WEIGHTS], *[out_d[k] for k in WEIGHTS],
            *[out_m[k] for k in WEIGHTS], *[out_v[k] for k in WEIGHTS])
```

```python
import math

import numpy as np
import jax
import jax.numpy as jnp
from jax import lax
from jax.experimental import pallas as pl
from jax.experimental.pallas import tpu as pltpu

F32 = jnp.float32
BF16 = jnp.bfloat16

D_MODEL = 1024
DEPTH = 4
HEAD_DIM = 64
SWA_Q_HEADS = 8
SWA_KV_HEADS = 2
SWA_GROUP = SWA_Q_HEADS // SWA_KV_HEADS
WINDOW = 128
FOX_HEADS = 4
MLA_HEADS = 4
MLA_Q_RANK = 256
MLA_KV_RANK = 128
MLA_NOPE = 64
MLA_ROPE = 32
MLA_QK = MLA_NOPE + MLA_ROPE
ROPE_THETA = 10000.0
REL_BUCKETS = 32
REL_MAX_DIST = 128
D_FF = 2816
EPS = 1e-6
NEG_INF = -1e30
LANES = 128
N_CHIPS = 4

IN_COLS = 1956
IN_ROWS = 2048
QKV_ROWS = 1536
LAT_ROWS = IN_ROWS - QKV_ROWS
LAT_SHIFT = FOX_HEADS
FOX_ROW0 = 768
MLA_PAD = LANES
GATE_ROWS = 8

ADAM_LR = 0.001
ADAM_B1 = 0.9
ADAM_B2 = 0.999
ADAM_EPS = 1e-08
ADAM_WD = 0.01
ADAM_STEP = 10

VMEM_LIMIT_BYTES = 48 * 1024 * 1024
ATT_TILE = 256
ROW_TILE = 256
MESH = pl.DeviceIdType.MESH

NT = (((1,), (1,)), ((), ()))
TN = (((0,), (0,)), ((), ()))
NN = (((1,), (0,)), ((), ()))


def _params(*sem):
    return pltpu.CompilerParams(dimension_semantics=sem, vmem_limit_bytes=VMEM_LIMIT_BYTES)


def _tile(dim, cap):
    for t in (2048, 1408, 1024, 512, 256, 128, 64, 32, 16, 8):
        if t <= cap and dim % t == 0:
            return t
    return dim


def _dot(a, b, dims=NN):
    return lax.dot_general(a, b, dims, preferred_element_type=F32)


def _split3(a):
    a1 = a.astype(BF16)
    r1 = a - a1.astype(F32)
    a2 = r1.astype(BF16)
    a3 = (r1 - a2.astype(F32)).astype(BF16)
    return a1, a2, a3


FF_SHARD = 2 * D_FF // N_CHIPS


def _matmul(a, b, *, ta=False, tb=False, out_dtype=F32, name, b_shards=False, out_shards=False):
    if ta:
        K, M = a.shape
    else:
        M, K = a.shape
    if b_shards:
        K2, N = (2 * D_FF, D_MODEL) if tb else (D_MODEL, 2 * D_FF)
    elif tb:
        N, K2 = b.shape
    else:
        K2, N = b.shape
    assert K == K2, (a.shape, b.shape)
    tm, tn, tk = _tile(M, 1408), _tile(N, 1408), _tile(K, 1408)
    nk = K // tk
    dims = (((0 if ta else 1,), (1 if tb else 0,)), ((), ()))

    def body(a_ref, b_ref, o_ref, acc_ref):
        k = pl.program_id(2)

        @pl.when(k == 0)
        def _():
            acc_ref[...] = jnp.zeros_like(acc_ref)

        acc_ref[...] += lax.dot_general(a_ref[...], b_ref[...], dims, preferred_element_type=F32)

        @pl.when(k == nk - 1)
        def _():
            o_ref[...] = acc_ref[...].astype(o_ref.dtype)

    a_spec = pl.BlockSpec((tk, tm), lambda i, j, k: (k, i)) if ta else pl.BlockSpec((tm, tk), lambda i, j, k: (i, k))
    if b_shards and tb:
        assert tk == FF_SHARD
        b_spec = pl.BlockSpec((None, tn, tk), lambda i, j, k: (k, j, 0))
    elif b_shards:
        assert tn == FF_SHARD
        b_spec = pl.BlockSpec((None, tk, tn), lambda i, j, k: (j, k, 0))
    else:
        b_spec = pl.BlockSpec((tn, tk), lambda i, j, k: (j, k)) if tb else pl.BlockSpec((tk, tn), lambda i, j, k: (k, j))
    if out_shards:
        assert tn == FF_SHARD
        out_spec = pl.BlockSpec((None, tm, tn), lambda i, j, k: (j, i, 0))
        out_shape = jax.ShapeDtypeStruct((N // tn, M, tn), out_dtype)
    else:
        out_spec = pl.BlockSpec((tm, tn), lambda i, j, k: (i, j))
        out_shape = jax.ShapeDtypeStruct((M, N), out_dtype)
    return pl.pallas_call(
        body, name=name, grid=(M // tm, N // tn, nk),
        in_specs=[a_spec, b_spec], out_specs=out_spec, out_shape=out_shape,
        scratch_shapes=[pltpu.VMEM((tm, tn), F32)],
        compiler_params=_params("parallel", "parallel", "arbitrary"),
    )(a, b)


def _seg_rms(xs, g):
    r = lax.rsqrt(jnp.mean(xs * xs, axis=-1, keepdims=True) + EPS)
    return xs * r * g


def _seg_rms_bwd(xs, g, dy):
    r = lax.rsqrt(jnp.mean(xs * xs, axis=-1, keepdims=True) + EPS)
    gd = dy * g
    c = jnp.mean(gd * xs, axis=-1, keepdims=True)
    dx = r * gd - xs * (r * r * r * c)
    dg = jnp.sum(dy * (xs * r), axis=0, keepdims=True)
    return dx, dg


def _rms_fwd(x, g, *, name):
    S, W = x.shape
    tm = _tile(S, 512)

    def body(x_ref, g_ref, o_ref):
        o_ref[...] = _seg_rms(x_ref[...], g_ref[...]).astype(o_ref.dtype)

    return pl.pallas_call(
        body, name=name, grid=(S // tm,),
        in_specs=[pl.BlockSpec((tm, W), lambda i: (i, 0)), pl.BlockSpec((1, W), lambda i: (0, 0))],
        out_specs=pl.BlockSpec((tm, W), lambda i: (i, 0)),
        out_shape=jax.ShapeDtypeStruct((S, W), BF16),
        compiler_params=_params("parallel"),
    )(x, g)


def _rms_bwd(x, g, dy, *, resid=None, out_dtype, name):
    S, W = x.shape
    tm = _tile(S, 512)
    has_resid = resid is not None

    def body(*refs):
        if has_resid:
            x_ref, g_ref, dy_ref, r_ref, dx_ref, dg_ref = refs
        else:
            x_ref, g_ref, dy_ref, dx_ref, dg_ref = refs

        @pl.when(pl.program_id(0) == 0)
        def _():
            dg_ref[...] = jnp.zeros_like(dg_ref)

        dx, dg = _seg_rms_bwd(x_ref[...], g_ref[...], dy_ref[...])
        if has_resid:
            dx = dx + r_ref[...]
        dx_ref[...] = dx.astype(dx_ref.dtype)
        dg_ref[...] += dg

    row = pl.BlockSpec((tm, W), lambda i: (i, 0))
    vec = pl.BlockSpec((1, W), lambda i: (0, 0))
    ins = [x, g, dy] + ([resid] if has_resid else [])
    return pl.pallas_call(
        body, name=name, grid=(S // tm,),
        in_specs=[row, vec, row] + ([row] if has_resid else []),
        out_specs=[row, vec],
        out_shape=[jax.ShapeDtypeStruct((S, W), out_dtype), jax.ShapeDtypeStruct((1, W), F32)],
        compiler_params=_params("arbitrary"),
    )(*ins)


def _resid_rms(x, y, g_post, g_next, *, name):
    S, W = x.shape
    tm = _tile(S, 512)
    with_next = g_next is not None

    def body(*refs):
        if with_next:
            x_ref, y_ref, gp_ref, gn_ref, xo_ref, h_ref = refs
        else:
            x_ref, y_ref, gp_ref, xo_ref = refs
        xn = x_ref[...] + _seg_rms(y_ref[...], gp_ref[...])
        xo_ref[...] = xn
        if with_next:
            h_ref[...] = _seg_rms(xn, gn_ref[...]).astype(BF16)

    row = pl.BlockSpec((tm, W), lambda i: (i, 0))
    vec = pl.BlockSpec((1, W), lambda i: (0, 0))
    outs = [jax.ShapeDtypeStruct((S, W), F32)] + ([jax.ShapeDtypeStruct((S, W), BF16)] if with_next else [])
    res = pl.pallas_call(
        body, name=name, grid=(S // tm,),
        in_specs=[row, row, vec] + ([vec] if with_next else []),
        out_specs=[row] + ([row] if with_next else []),
        out_shape=outs,
        compiler_params=_params("parallel"),
    )(*([x, y, g_post] + ([g_next] if with_next else [])))
    return (res[0], res[1]) if with_next else (res[0], None)


def _col_rms(xs, g):
    r = lax.rsqrt(jnp.mean(xs * xs, axis=0, keepdims=True) + EPS)
    return xs * r * g


def _col_rms_bwd(xs, g, dy):
    r = lax.rsqrt(jnp.mean(xs * xs, axis=0, keepdims=True) + EPS)
    gd = dy * g
    c = jnp.mean(gd * xs, axis=0, keepdims=True)
    dx = r * gd - xs * (r * r * r * c)
    dg = jnp.sum(dy * (xs * r), axis=1, keepdims=True)
    return dx, dg


GROUP_ROWS = (SWA_Q_HEADS * HEAD_DIM, FOX_HEADS * HEAD_DIM, MLA_HEADS * HEAD_DIM)


def _group_specs(S, tn):
    outs = [pl.BlockSpec((n, tn), lambda i: (0, i)) for n in GROUP_ROWS]
    g = pl.BlockSpec((D_MODEL, 1), lambda i: (0, 0))
    mixed = pl.BlockSpec((D_MODEL, tn), lambda i: (0, i))
    return outs, g, mixed


def _group_norm_fwd(oa, of, oc, g, *, name):
    S = oa.shape[1]
    tn = _tile(S, 512)
    outs, gs, mixed = _group_specs(S, tn)

    def body(a_ref, f_ref, c_ref, g_ref, o_ref):
        r0 = 0
        for ref, n in zip((a_ref, f_ref, c_ref), GROUP_ROWS):
            o_ref[r0:r0 + n, :] = _col_rms(ref[...], g_ref[r0:r0 + n, :]).astype(BF16)
            r0 += n

    return pl.pallas_call(
        body, name=name, grid=(S // tn,),
        in_specs=outs + [gs], out_specs=mixed,
        out_shape=jax.ShapeDtypeStruct((D_MODEL, S), BF16),
        compiler_params=_params("parallel"),
    )(oa, of, oc, g)


def _group_norm_bwd(oa, of, oc, g, dmixed, *, name):
    S = oa.shape[1]
    tn = _tile(S, 512)
    outs, gs, mixed = _group_specs(S, tn)
    n_heads = D_MODEL // HEAD_DIM

    def body(a_ref, f_ref, c_ref, g_ref, dm_ref, da_ref, df_ref, dc_ref, dg_ref, dl_ref):
        @pl.when(pl.program_id(0) == 0)
        def _():
            dg_ref[...] = jnp.zeros_like(dg_ref)

        r0 = 0
        for ref, dref, n in zip((a_ref, f_ref, c_ref), (da_ref, df_ref, dc_ref), GROUP_ROWS):
            o = ref[...]
            dx, dg = _col_rms_bwd(o, g_ref[r0:r0 + n, :], dm_ref[r0:r0 + n, :])
            dxb = dx.astype(BF16)
            dref[...] = dxb
            dg_ref[r0:r0 + n, :] += dg
            od = o * dxb.astype(F32)
            for h in range(n // HEAD_DIM):
                dl_ref[r0 // HEAD_DIM + h] = jnp.sum(od[h * HEAD_DIM:(h + 1) * HEAD_DIM, :], axis=0, keepdims=True)
            r0 += n

    return pl.pallas_call(
        body, name=name, grid=(S // tn,),
        in_specs=outs + [gs, mixed], out_specs=outs + [gs, pl.BlockSpec((n_heads, 1, tn), lambda i: (0, 0, i))],
        out_shape=[jax.ShapeDtypeStruct((n, S), BF16) for n in GROUP_ROWS] + [jax.ShapeDtypeStruct((D_MODEL, 1), F32),
                                                                              jax.ShapeDtypeStruct((n_heads, 1, S), F32)],
        compiler_params=_params("arbitrary"),
    )(oa, of, oc, g, dmixed)


def _loss_head(y, target):
    S, W = y.shape
    tm = _tile(S, 512)

    def body(y_ref, t_ref, d_ref, l_ref):
        @pl.when(pl.program_id(0) == 0)
        def _():
            l_ref[...] = jnp.zeros_like(l_ref)

        err = y_ref[...] - t_ref[...]
        d_ref[...] = err * (1.0 / W)
        l_ref[...] += 0.5 * jnp.sum(jnp.mean(err * err, axis=-1, keepdims=True), axis=0, keepdims=True)

    row = pl.BlockSpec((tm, W), lambda i: (i, 0))
    d, l = pl.pallas_call(
        body, name="loss_head", grid=(S // tm,),
        in_specs=[row, row],
        out_specs=[row, pl.BlockSpec((1, 1), lambda i: (0, 0))],
        out_shape=[jax.ShapeDtypeStruct((S, W), F32), jax.ShapeDtypeStruct((1, 1), F32)],
        compiler_params=_params("arbitrary"),
    )(y, target)
    return l[0, 0], d


def _attn_fwd(q_src, k_src, v_src, rows, H, Dk, Dv, scale, f_row=None, f_col=None, *, name):
    S = q_src.shape[1]
    T = _tile(S, ATT_TILE)
    nq = S // T
    forget = f_row is not None
    qb, kb, vb = rows[0] // (H * Dk), rows[1] // (H * Dk), rows[2] // (H * Dv)
    hs = range(H)

    def body(*refs):
        if forget:
            q_ref, k_ref, v_ref, fq_ref, fk_ref, o_ref, lse_ref = refs
        else:
            q_ref, k_ref, v_ref, o_ref, lse_ref = refs
        i = pl.program_id(0)

        def tile(j, masked, state):
            off = pl.multiple_of(j * T, T)
            ss = [_dot(k_ref[h * Dk:(h + 1) * Dk, pl.ds(off, T)], q_ref[h * Dk:(h + 1) * Dk, :], TN) * scale for h in hs]
            if forget:
                ss = [ss[h] + (fq_ref[h] - fk_ref[pl.ds(off, T), h:h + 1]) for h in hs]
            if masked:
                r = lax.broadcasted_iota(jnp.int32, (T, T), 0)
                c = lax.broadcasted_iota(jnp.int32, (T, T), 1)
                ss = [jnp.where(r <= c, s, NEG_INF) for s in ss]
            m_new = [jnp.maximum(state[h][0], jnp.max(ss[h], axis=0, keepdims=True)) for h in hs]
            alpha = [jnp.exp(state[h][0] - m_new[h]) for h in hs]
            ps = [jnp.exp(ss[h] - m_new[h]) for h in hs]
            l_new = [alpha[h] * state[h][1] + jnp.sum(ps[h], axis=0, keepdims=True) for h in hs]
            p_hi = [p.astype(BF16) for p in ps]
            vs = [v_ref[h * Dv:(h + 1) * Dv, pl.ds(off, T)] for h in hs]
            pv = [_dot(vs[h], p_hi[h]) for h in hs]
            if forget:
                pv = [pv[h] + _dot(vs[h], (ps[h] - p_hi[h].astype(F32)).astype(BF16)) for h in hs]
            return tuple((m_new[h], l_new[h], alpha[h] * state[h][2] + pv[h]) for h in hs)

        init = tuple((jnp.full((1, T), NEG_INF, F32), jnp.zeros((1, T), F32), jnp.zeros((Dv, T), F32)) for _ in hs)
        state = lax.fori_loop(0, i, lambda j, st: tile(j, False, st), init)
        state = tile(i, True, state)
        for h in hs:
            m, l, acc = state[h]
            o_ref[h * Dv:(h + 1) * Dv, :] = acc / l
            lse_ref[h] = m + jnp.log(l)

    in_specs = [pl.BlockSpec((H * Dk, T), lambda i: (qb, i)),
                pl.BlockSpec((H * Dk, S), lambda i: (kb, 0)),
                pl.BlockSpec((H * Dv, S), lambda i: (vb, 0))]
    ins = [q_src, k_src, v_src]
    if forget:
        in_specs += [pl.BlockSpec((H, 1, T), lambda i: (0, 0, i)), pl.BlockSpec((S, H), lambda i: (0, 0))]
        ins += [f_row, f_col]
    return pl.pallas_call(
        body, name=name, grid=(nq,),
        in_specs=in_specs,
        out_specs=[pl.BlockSpec((H * Dv, T), lambda i: (0, i)), pl.BlockSpec((H, 1, T), lambda i: (0, 0, i))],
        out_shape=[jax.ShapeDtypeStruct((H * Dv, S), F32), jax.ShapeDtypeStruct((H, 1, S), F32)],
        compiler_params=_params("parallel"),
    )(*ins)


def _attn_bwd(q_src, k_src, v_src, rows, H, Dk, Dv, scale, do, lse, delta, f_row=None, f_col=None, *, name):
    S = q_src.shape[1]
    T = _tile(S, ATT_TILE)
    nq = S // T
    forget = f_row is not None
    qb, kb, vb, db = rows[0] // (H * Dk), rows[1] // (H * Dk), rows[2] // (H * Dv), rows[3] // H
    hs = range(H)

    def body(*refs):
        if forget:
            (q_ref, k_ref, v_ref, do_ref, lse_ref, dl_ref, fq_ref, fk_ref,
             dq_ref, dk_ref, dv_ref, df_ref, dk_s, dv_s, df_s) = refs
        else:
            q_ref, k_ref, v_ref, do_ref, lse_ref, dl_ref, dq_ref, dk_ref, dv_ref, dk_s, dv_s = refs
        j = pl.program_id(0)

        @pl.when(j == 0)
        def _():
            dq_ref[...] = jnp.zeros_like(dq_ref)

        dk_s[...] = jnp.zeros_like(dk_s)
        dv_s[...] = jnp.zeros_like(dv_s)
        if forget:
            df_s[...] = jnp.zeros_like(df_s)
        kt = [k_ref[h * Dk:(h + 1) * Dk, :] for h in hs]
        kj = [k.T for k in kt]
        vj = [v_ref[h * Dv:(h + 1) * Dv, :].T for h in hs]
        koff = pl.multiple_of(j * T, T)

        def tile(i, masked):
            cols = pl.ds(pl.multiple_of(i * T, T), T)
            qi = [q_ref[h * Dk:(h + 1) * Dk, cols] for h in hs]
            doi = [do_ref[h * Dv:(h + 1) * Dv, cols] for h in hs]
            st = [_dot(kj[h], qi[h]) * scale for h in hs]
            if forget:
                st = [st[h] + (fq_ref[h, :, cols] - fk_ref[pl.ds(koff, T), h:h + 1]) for h in hs]
            if masked:
                r = lax.broadcasted_iota(jnp.int32, (T, T), 0)
                c = lax.broadcasted_iota(jnp.int32, (T, T), 1)
                st = [jnp.where(r <= c, x, NEG_INF) for x in st]
            pt = [jnp.exp(st[h] - lse_ref[h, :, cols]) for h in hs]
            dpt = [_dot(vj[h], doi[h]) for h in hs]
            dst = [pt[h] * (dpt[h] - dl_ref[h, :, cols]) for h in hs]
            ptb = [p.astype(BF16) for p in pt]
            dsb = [d.astype(BF16) for d in dst]
            for h in hs:
                dv_s[h * Dv:(h + 1) * Dv, :] += _dot(doi[h], ptb[h], NT)
            for h in hs:
                dk_s[h * Dk:(h + 1) * Dk, :] += _dot(qi[h], dsb[h], NT)
            for h in hs:
                dq_ref[h * Dk:(h + 1) * Dk, cols] += _dot(kt[h], dsb[h]) * scale
            if forget:
                for h in hs:
                    part = dst[h][:, 0:LANES]
                    for c0 in range(LANES, T, LANES):
                        part = part + dst[h][:, c0:c0 + LANES]
                    df_s[h] += part

        tile(j, True)

        def loop_body(i, carry):
            tile(i, False)
            return carry

        lax.fori_loop(j + 1, nq, loop_body, 0)
        dk_ref[...] = dk_s[...] * scale
        dv_ref[...] = dv_s[...]
        if forget:
            df_ref[...] = jnp.concatenate([-jnp.sum(df_s[h], axis=-1, keepdims=True) for h in hs], axis=1)

    res = lambda D, b0: pl.BlockSpec((H * D, S), lambda j: (b0, 0))
    blk = lambda D, b0: pl.BlockSpec((H * D, T), lambda j: (b0, j))
    row3 = lambda b0: pl.BlockSpec((H, 1, S), lambda j: (b0, 0, 0))
    in_specs = [res(Dk, qb), blk(Dk, kb), blk(Dv, vb), res(Dv, 0), row3(0), row3(db)]
    ins = [q_src, k_src, v_src, do, lse, delta]
    out_specs = [res(Dk, 0), blk(Dk, 0), blk(Dv, 0)]
    out_shape = [jax.ShapeDtypeStruct((H * Dk, S), F32), jax.ShapeDtypeStruct((H * Dk, S), F32),
                 jax.ShapeDtypeStruct((H * Dv, S), F32)]
    scratch = [pltpu.VMEM((H * Dk, T), F32), pltpu.VMEM((H * Dv, T), F32)]
    if forget:
        in_specs += [row3(0), pl.BlockSpec((S, H), lambda j: (0, 0))]
        ins += [f_row, f_col]
        out_specs.append(pl.BlockSpec((T, H), lambda j: (j, 0)))
        out_shape.append(jax.ShapeDtypeStruct((S, H), F32))
        scratch.append(pltpu.VMEM((H, T, min(T, LANES)), F32))
    return pl.pallas_call(
        body, name=name, grid=(nq,),
        in_specs=in_specs, out_specs=out_specs, out_shape=out_shape, scratch_shapes=scratch,
        compiler_params=_params("arbitrary"),
    )(*ins)


def _swa_masks(i):
    r = lax.broadcasted_iota(jnp.int32, (WINDOW, WINDOW), 0)
    c = lax.broadcasted_iota(jnp.int32, (WINDOW, WINDOW), 1)
    return (r > c) & (i > 0), r <= c


def _swa_specs():
    W = WINDOW
    kv_rows = SWA_KV_HEADS * HEAD_DIM
    q = pl.BlockSpec((SWA_Q_HEADS * HEAD_DIM, W), lambda i: (0, i))
    prev = lambda b: pl.BlockSpec((kv_rows, W), lambda i: (b, jnp.maximum(i - 1, 0)))
    cur = lambda b: pl.BlockSpec((kv_rows, W), lambda i: (b, i))
    bias = pl.BlockSpec((SWA_Q_HEADS, 2 * W, W), lambda i: (0, 0, 0))
    stat = pl.BlockSpec((SWA_Q_HEADS, W), lambda i: (0, i))
    sink = pl.BlockSpec(memory_space=pltpu.SMEM)
    return q, prev(4), cur(4), prev(5), cur(5), bias, stat, sink


def _swa_scores(h, q_ref, kp_ref, kc_ref, b_ref, masks):
    g = h // SWA_GROUP
    rows = slice(g * HEAD_DIM, (g + 1) * HEAD_DIM)
    qh = q_ref[h * HEAD_DIM:(h + 1) * HEAD_DIM, :]
    scale = HEAD_DIM ** -0.5
    s_p = jnp.where(masks[0], _dot(kp_ref[rows, :], qh, TN) * scale + b_ref[h, 0:WINDOW, :], NEG_INF)
    s_c = jnp.where(masks[1], _dot(kc_ref[rows, :], qh, TN) * scale + b_ref[h, WINDOW:2 * WINDOW, :], NEG_INF)
    return qh, rows, s_p, s_c


def _swa_fwd(qkv, bias_t, sinks, *, name):
    S = qkv.shape[1]
    qs, kp, kc, vp, vc, bs, stat, sk = _swa_specs()

    def body(sink_ref, q_ref, kp_ref, kc_ref, vp_ref, vc_ref, b_ref, o_ref, lse_ref):
        masks = _swa_masks(pl.program_id(0))
        for h in range(SWA_Q_HEADS):
            qh, rows, s_p, s_c = _swa_scores(h, q_ref, kp_ref, kc_ref, b_ref, masks)
            sink = sink_ref[h]
            m = jnp.maximum(jnp.maximum(jnp.max(s_p, axis=0, keepdims=True), jnp.max(s_c, axis=0, keepdims=True)), sink)
            p_p = jnp.exp(s_p - m)
            p_c = jnp.exp(s_c - m)
            l = jnp.sum(p_p, axis=0, keepdims=True) + jnp.sum(p_c, axis=0, keepdims=True) + jnp.exp(sink - m)
            o = _dot(vp_ref[rows, :], p_p.astype(BF16)) + _dot(vc_ref[rows, :], p_c.astype(BF16))
            o_ref[h * HEAD_DIM:(h + 1) * HEAD_DIM, :] = o / l
            lse_ref[h:h + 1, :] = m + jnp.log(l)

    return pl.pallas_call(
        body, name=name, grid=(S // WINDOW,),
        in_specs=[sk, qs, kp, kc, vp, vc, bs],
        out_specs=[qs, stat],
        out_shape=[jax.ShapeDtypeStruct((SWA_Q_HEADS * HEAD_DIM, S), F32), jax.ShapeDtypeStruct((SWA_Q_HEADS, S), F32)],
        compiler_params=_params("parallel"),
    )(sinks, qkv, qkv, qkv, qkv, qkv, bias_t)


def _swa_bwd(qkv, bias_t, sinks, do, lse, delta, *, name):
    S = qkv.shape[1]
    W = WINDOW
    qs, kp, kc, vp, vc, bs, stat, sk = _swa_specs()
    scale = HEAD_DIM ** -0.5
    kv_rows = SWA_KV_HEADS * HEAD_DIM

    def body(sink_ref, q_ref, kp_ref, kc_ref, vp_ref, vc_ref, b_ref, do_ref, lse_ref, dl_ref,
             dq_ref, dkv_ref, db_ref, dsk_ref):
        i = pl.program_id(0)

        @pl.when(i == 0)
        def _():
            dkv_ref[...] = jnp.zeros_like(dkv_ref)
            db_ref[...] = jnp.zeros_like(db_ref)
            dsk_ref[...] = jnp.zeros_like(dsk_ref)

        masks = _swa_masks(i)
        prev = pl.ds(pl.multiple_of(jnp.maximum(i - 1, 0) * W, W), W)
        cur = pl.ds(pl.multiple_of(i * W, W), W)
        for h in range(SWA_Q_HEADS):
            qh, rows, s_p, s_c = _swa_scores(h, q_ref, kp_ref, kc_ref, b_ref, masks)
            vrows = slice(kv_rows + rows.start, kv_rows + rows.stop)
            hrows = slice(h * HEAD_DIM, (h + 1) * HEAD_DIM)
            doh = do_ref[hrows, :]
            lse_h = lse_ref[h:h + 1, :]
            delta = dl_ref[h:h + 1, :]
            p_p = jnp.exp(s_p - lse_h)
            p_c = jnp.exp(s_c - lse_h)
            ds_p = p_p * (_dot(vp_ref[rows, :], doh, TN) - delta)
            ds_c = p_c * (_dot(vc_ref[rows, :], doh, TN) - delta)
            db_ref[h, 0:W, :] += ds_p
            db_ref[h, W:2 * W, :] += ds_c
            dsk = -jnp.sum(jnp.exp(sink_ref[h] - lse_h) * delta, axis=1, keepdims=True)
            dsk_ref[h:h + 1, :] += jnp.broadcast_to(dsk, (1, LANES))
            dsb_p = ds_p.astype(BF16)
            dsb_c = ds_c.astype(BF16)
            dq_ref[hrows, :] = (_dot(kp_ref[rows, :], dsb_p) + _dot(kc_ref[rows, :], dsb_c)) * scale
            dkv_ref[rows, prev] += _dot(qh, dsb_p, NT) * scale
            dkv_ref[rows, cur] += _dot(qh, dsb_c, NT) * scale
            dkv_ref[vrows, prev] += _dot(doh, p_p.astype(BF16), NT)
            dkv_ref[vrows, cur] += _dot(doh, p_c.astype(BF16), NT)

    return pl.pallas_call(
        body, name=name, grid=(S // W,),
        in_specs=[sk, qs, kp, kc, vp, vc, bs, qs, stat, stat],
        out_specs=[qs, pl.BlockSpec((2 * kv_rows, S), lambda i: (0, 0)), bs, pl.BlockSpec((SWA_Q_HEADS, LANES), lambda i: (0, 0))],
        out_shape=[jax.ShapeDtypeStruct((SWA_Q_HEADS * HEAD_DIM, S), F32), jax.ShapeDtypeStruct((2 * kv_rows, S), F32),
                   jax.ShapeDtypeStruct((SWA_Q_HEADS, 2 * W, W), F32), jax.ShapeDtypeStruct((SWA_Q_HEADS, LANES), F32)],
        compiler_params=_params("arbitrary"),
    )(sinks, qkv, qkv, qkv, qkv, qkv, bias_t, do, lse, delta)


def _rel_onehot_t():
    qi = jnp.arange(WINDOW, dtype=jnp.int32)[None, :] + WINDOW
    kj = jnp.arange(2 * WINDOW, dtype=jnp.int32)[:, None]
    dist = qi - kj
    max_exact = REL_BUCKETS // 2
    d = jnp.maximum(dist, 0)
    log_ratio = jnp.log(jnp.maximum(d, 1).astype(F32) / max_exact) / math.log(REL_MAX_DIST / max_exact)
    large = jnp.minimum(max_exact + (log_ratio * (REL_BUCKETS - max_exact)).astype(jnp.int32), REL_BUCKETS - 1)
    bucket = jnp.where(d < max_exact, d, large).reshape(-1)
    return (bucket[None, :] == jnp.arange(REL_BUCKETS, dtype=jnp.int32)[:, None]).astype(BF16)


def _bias_table(rel_bias_t, onehot_t):
    Hq, NB = rel_bias_t.shape
    N = onehot_t.shape[1]
    tn = _tile(N, 4096)

    def body(r_ref, oh_ref, o_ref):
        oh = oh_ref[...]
        a1, a2, a3 = _split3(r_ref[...])
        o_ref[...] = _dot(a1, oh) + _dot(a2, oh) + _dot(a3, oh)

    return pl.pallas_call(
        body, name="rel_bias_table", grid=(N // tn,),
        in_specs=[pl.BlockSpec((Hq, NB), lambda j: (0, 0)), pl.BlockSpec((NB, tn), lambda j: (0, j))],
        out_specs=pl.BlockSpec((Hq, tn), lambda j: (0, j)),
        out_shape=jax.ShapeDtypeStruct((Hq, N), F32),
        compiler_params=_params("parallel"),
    )(rel_bias_t, onehot_t)


def _bias_table_bwd(dbias, onehot_t):
    L, Hq, N = dbias.shape
    NB = onehot_t.shape[0]
    tn = _tile(N, 4096)

    def body(d_ref, oh_ref, o_ref):
        @pl.when(pl.program_id(0) == 0)
        def _():
            o_ref[...] = jnp.zeros_like(o_ref)

        d = d_ref[0]
        for l in range(1, L):
            d = d + d_ref[l]
        oh = oh_ref[...]
        a1, a2, a3 = _split3(d)
        o_ref[...] += _dot(a1, oh, NT) + _dot(a2, oh, NT) + _dot(a3, oh, NT)

    return pl.pallas_call(
        body, name="rel_bias_bwd", grid=(N // tn,),
        in_specs=[pl.BlockSpec((L, Hq, tn), lambda j: (0, 0, j)), pl.BlockSpec((NB, tn), lambda j: (0, j))],
        out_specs=pl.BlockSpec((Hq, NB), lambda j: (0, 0)),
        out_shape=jax.ShapeDtypeStruct((Hq, NB), F32),
        compiler_params=_params("arbitrary"),
    )(dbias, onehot_t)


def _gate_fwd(lat, fb_col, *, name):
    S = lat.shape[1]
    tn = _tile(S, 256)

    def body(z_ref, fb_ref, o_ref, carry):
        @pl.when(pl.program_id(0) == 0)
        def _():
            carry[...] = jnp.zeros_like(carry)

        z = z_ref[...] + fb_ref[...]
        lf = jnp.minimum(z, 0.0) - jnp.log1p(jnp.exp(-jnp.abs(z)))
        r = lax.broadcasted_iota(jnp.int32, (tn, tn), 0)
        c = lax.broadcasted_iota(jnp.int32, (tn, tn), 1)
        tri = (r <= c).astype(BF16)
        a1, a2, a3 = _split3(lf)
        cum = _dot(a1, tri) + _dot(a2, tri) + _dot(a3, tri) + carry[:, 0:1]
        o_ref[...] = cum
        carry[...] = jnp.broadcast_to(cum[:, tn - 1:tn], carry.shape)

    return pl.pallas_call(
        body, name=name, grid=(S // tn,),
        in_specs=[pl.BlockSpec((GATE_ROWS, tn), lambda i: (0, i)), pl.BlockSpec((GATE_ROWS, 1), lambda i: (0, 0))],
        out_specs=pl.BlockSpec((GATE_ROWS, tn), lambda i: (0, i)),
        out_shape=jax.ShapeDtypeStruct((GATE_ROWS, S), F32),
        scratch_shapes=[pltpu.VMEM((GATE_ROWS, LANES), F32)],
        compiler_params=_params("arbitrary"),
    )(lat, fb_col)


def _gate_bwd(lat, fb_col, dF, *, name):
    S = lat.shape[1]
    tn = _tile(S, 256)
    nt = S // tn

    def body(z_ref, fb_ref, df_ref, dz_ref, dfb_ref, carry):
        @pl.when(pl.program_id(0) == 0)
        def _():
            carry[...] = jnp.zeros_like(carry)
            dfb_ref[...] = jnp.zeros_like(dfb_ref)

        r = lax.broadcasted_iota(jnp.int32, (tn, tn), 0)
        c = lax.broadcasted_iota(jnp.int32, (tn, tn), 1)
        tri = (r >= c).astype(BF16)
        a1, a2, a3 = _split3(df_ref[...])
        dlf = _dot(a1, tri) + _dot(a2, tri) + _dot(a3, tri) + carry[:, 0:1]
        carry[...] = jnp.broadcast_to(dlf[:, 0:1], carry.shape)
        z = z_ref[...] + fb_ref[...]
        row = lax.broadcasted_iota(jnp.int32, (GATE_ROWS, tn), 0)
        dz = jnp.where(row < FOX_HEADS, dlf / (1.0 + jnp.exp(z)), 0.0)
        dz_ref[...] = dz
        dfb_ref[...] += jnp.sum(dz, axis=1, keepdims=True)

    blk = pl.BlockSpec((GATE_ROWS, tn), lambda i: (0, nt - 1 - i))
    vec = pl.BlockSpec((GATE_ROWS, 1), lambda i: (0, 0))
    return pl.pallas_call(
        body, name=name, grid=(nt,),
        in_specs=[blk, vec, blk], out_specs=[blk, vec],
        out_shape=[jax.ShapeDtypeStruct((GATE_ROWS, S), F32), jax.ShapeDtypeStruct((GATE_ROWS, 1), F32)],
        scratch_shapes=[pltpu.VMEM((GATE_ROWS, LANES), F32)],
        compiler_params=_params("arbitrary"),
    )(lat, fb_col, dF)


def _rope_tables(S):
    pos = jnp.arange(S, dtype=F32)
    inv_freq = ROPE_THETA ** (-(jnp.arange(MLA_ROPE // 2, dtype=F32) * 2.0 / MLA_ROPE))
    ang = pos[:, None] * inv_freq[None, :]
    cos, sin = jnp.cos(ang).T, jnp.sin(ang).T
    z16 = jnp.zeros_like(cos)

    def slab(lo, fill):
        def put(first, second, f):
            return jnp.concatenate([jnp.full((lo, S), f, F32), first, second, jnp.full((LANES - lo - MLA_ROPE, S), f, F32)], axis=0)
        return put(cos, cos, fill), put(-sin, z16, 0.0), put(z16, sin, 0.0)

    tq = tuple(jnp.tile(t, (MLA_HEADS, 1)) for t in slab(MLA_NOPE, 1.0))
    return tq, slab(0, 0.0)


def _rope(x, c, s1, s2):
    n = x.shape[0]
    half = MLA_ROPE // 2
    return x * c + pltpu.roll(x, n - half, 0) * s1 + pltpu.roll(x, half, 0) * s2


def _rope_t(dy, c, s1, s2):
    n = dy.shape[0]
    half = MLA_ROPE // 2
    return dy * c + pltpu.roll(dy * s1, half, 0) + pltpu.roll(dy * s2, n - half, 0)


KR_SLAB0 = MLA_Q_RANK + MLA_KV_RANK


def _mla_prep_fwd(lat, g_q, g_kv, w_uq_t, w_ukv_t, tq, tmisc, *, name):
    S = lat.shape[1]
    tn = _tile(S, 512)
    QW = MLA_HEADS * MLA_PAD

    def body(lat_ref, gq_ref, gkv_ref, wq_ref, wkv_ref, c_ref, s1_ref, s2_ref, cm_ref, s1m_ref, s2m_ref,
             nq_ref, nkv_ref, q_ref, k_ref, v_ref):
        x = pltpu.roll(lat_ref[...], LAT_ROWS - LAT_SHIFT, 0)
        nq = _col_rms(x[0:MLA_Q_RANK, :], gq_ref[...]).astype(BF16)
        nkv = _col_rms(x[MLA_Q_RANK:KR_SLAB0, :], gkv_ref[...]).astype(BF16)
        nq_ref[...] = nq
        nkv_ref[...] = nkv
        q_ref[...] = _rope(_dot(wq_ref[...], nq), c_ref[...], s1_ref[...], s2_ref[...]).astype(BF16)
        kv = _dot(wkv_ref[...], nkv).astype(BF16)
        kr = _rope(x[KR_SLAB0:LAT_ROWS, :], cm_ref[...], s1m_ref[...], s2m_ref[...]).astype(BF16)
        for h in range(MLA_HEADS):
            k_ref[h * MLA_PAD:h * MLA_PAD + MLA_NOPE, :] = kv[h * LANES:h * LANES + MLA_NOPE, :]
            k_ref[h * MLA_PAD + MLA_NOPE:(h + 1) * MLA_PAD, :] = kr[0:MLA_PAD - MLA_NOPE, :]
            v_ref[h * HEAD_DIM:(h + 1) * HEAD_DIM, :] = kv[h * LANES + MLA_NOPE:(h + 1) * LANES, :]

    def col(rows):
        return pl.BlockSpec((rows, tn), lambda i: (0, i))

    def full(a):
        return pl.BlockSpec(a.shape, lambda i: (0, 0))

    return pl.pallas_call(
        body, name=name, grid=(S // tn,),
        in_specs=[col(LAT_ROWS), full(g_q), full(g_kv), full(w_uq_t), full(w_ukv_t),
                  col(QW), col(QW), col(QW), col(LANES), col(LANES), col(LANES)],
        out_specs=[col(MLA_Q_RANK), col(MLA_KV_RANK), col(QW), col(QW), col(MLA_HEADS * HEAD_DIM)],
        out_shape=[jax.ShapeDtypeStruct((MLA_Q_RANK, S), BF16), jax.ShapeDtypeStruct((MLA_KV_RANK, S), BF16),
                   jax.ShapeDtypeStruct((QW, S), BF16), jax.ShapeDtypeStruct((QW, S), BF16),
                   jax.ShapeDtypeStruct((MLA_HEADS * HEAD_DIM, S), BF16)],
        compiler_params=_params("parallel"),
    )(lat, g_q, g_kv, w_uq_t, w_ukv_t, *tq, *tmisc)


def _mla_prep_bwd(lat, nq, nkv, g_q, g_kv, w_uq_p, w_ukv, tq, tmisc, dq, dk, dv, dflog, *, name):
    S = lat.shape[1]
    tn = _tile(S, 512)
    QW = MLA_HEADS * MLA_PAD

    def body(lat_ref, nq_ref, nkv_ref, gq_ref, gkv_ref, wq_ref, wkv_ref, c_ref, s1_ref, s2_ref,
             cm_ref, s1m_ref, s2m_ref, dq_ref, dk_ref, dv_ref, dfl_ref,
             dlat_ref, dwq_ref, dwkv_ref, dgq_ref, dgkv_ref, y_s):
        @pl.when(pl.program_id(0) == 0)
        def _():
            dwq_ref[...] = jnp.zeros_like(dwq_ref)
            dwkv_ref[...] = jnp.zeros_like(dwkv_ref)
            dgq_ref[...] = jnp.zeros_like(dgq_ref)
            dgkv_ref[...] = jnp.zeros_like(dgkv_ref)

        x = pltpu.roll(lat_ref[...], LAT_ROWS - LAT_SHIFT, 0)
        dqm = _rope_t(dq_ref[...], c_ref[...], s1_ref[...], s2_ref[...]).astype(BF16)
        dwq_ref[...] += _dot(dqm, nq_ref[...], NT)
        dx, dg = _col_rms_bwd(x[0:MLA_Q_RANK, :], gq_ref[...], _dot(wq_ref[...], dqm))
        y_s[0:MLA_Q_RANK, :] = dx
        dgq_ref[...] += dg
        dkv = jnp.concatenate(
            [part for h in range(MLA_HEADS)
             for part in (dk_ref[h * MLA_PAD:h * MLA_PAD + MLA_NOPE, :], dv_ref[h * HEAD_DIM:(h + 1) * HEAD_DIM, :])],
            axis=0).astype(BF16)
        dwkv_ref[...] += _dot(dkv, nkv_ref[...], NT)
        dx, dg = _col_rms_bwd(x[MLA_Q_RANK:KR_SLAB0, :], gkv_ref[...], _dot(wkv_ref[...], dkv))
        y_s[MLA_Q_RANK:KR_SLAB0, :] = dx
        dgkv_ref[...] += dg
        dkr = dk_ref[MLA_NOPE:MLA_PAD, :]
        for h in range(1, MLA_HEADS):
            dkr = dkr + dk_ref[h * MLA_PAD + MLA_NOPE:(h + 1) * MLA_PAD, :]
        dkr = jnp.concatenate([dkr, jnp.zeros((MLA_NOPE, tn), F32)], axis=0)
        y_s[KR_SLAB0:LAT_ROWS, :] = _rope_t(dkr, cm_ref[...], s1m_ref[...], s2m_ref[...])
        y = pltpu.roll(y_s[...], LAT_SHIFT, 0)
        row = lax.broadcasted_iota(jnp.int32, (LAT_ROWS, tn), 0)
        dfl = jnp.concatenate([dfl_ref[...], jnp.zeros((LAT_ROWS - GATE_ROWS, tn), F32)], axis=0)
        dlat_ref[...] = jnp.where(row < LAT_SHIFT, dfl, y).astype(BF16)

    def col(rows):
        return pl.BlockSpec((rows, tn), lambda i: (0, i))

    def full(a):
        return pl.BlockSpec(a.shape, lambda i: (0, 0))

    def acc(r, c):
        return pl.BlockSpec((r, c), lambda i: (0, 0))

    return pl.pallas_call(
        body, name=name, grid=(S // tn,),
        in_specs=[col(LAT_ROWS), col(MLA_Q_RANK), col(MLA_KV_RANK), full(g_q), full(g_kv),
                  full(w_uq_p), full(w_ukv), col(QW), col(QW), col(QW), col(LANES), col(LANES), col(LANES),
                  col(QW), col(QW), col(MLA_HEADS * HEAD_DIM), col(GATE_ROWS)],
        out_specs=[col(LAT_ROWS), acc(QW, MLA_Q_RANK), acc(QW, MLA_KV_RANK), acc(MLA_Q_RANK, 1), acc(MLA_KV_RANK, 1)],
        out_shape=[jax.ShapeDtypeStruct((LAT_ROWS, S), BF16), jax.ShapeDtypeStruct((QW, MLA_Q_RANK), F32),
                   jax.ShapeDtypeStruct((QW, MLA_KV_RANK), F32), jax.ShapeDtypeStruct((MLA_Q_RANK, 1), F32),
                   jax.ShapeDtypeStruct((MLA_KV_RANK, 1), F32)],
        scratch_shapes=[pltpu.VMEM((LAT_ROWS, tn), F32)],
        compiler_params=_params("arbitrary"),
    )(lat, nq, nkv, g_q, g_kv, w_uq_p, w_ukv, *tq, *tmisc, dq, dk, dv, dflog)


def _dproj_cast(dqa, dkva, dqf, dkf, dvf, dlat, *, name):
    S = dqa.shape[1]
    tn = _tile(S, 512)
    parts = (dqa, dkva, dqf, dkf, dvf, dlat)

    def body(*refs):
        o_ref = refs[-1]
        r0 = 0
        for ref in refs[:-1]:
            n = ref.shape[0]
            o_ref[r0:r0 + n, :] = ref[...].astype(BF16)
            r0 += n

    return pl.pallas_call(
        body, name=name, grid=(S // tn,),
        in_specs=[pl.BlockSpec((p.shape[0], tn), lambda i: (0, i)) for p in parts],
        out_specs=pl.BlockSpec((IN_ROWS, tn), lambda i: (0, i)),
        out_shape=jax.ShapeDtypeStruct((IN_ROWS, S), BF16),
        compiler_params=_params("parallel"),
    )(*parts)


GELU_C = math.sqrt(2.0 / math.pi)
GELU_A = 0.044715


def _conv_taps(a, halo, w_ref, b_ref, first):
    row = lax.broadcasted_iota(jnp.int32, a.shape, 0)
    h7 = jnp.where(first, 0.0, halo[7:8, :])
    h6 = jnp.where(first, 0.0, halo[6:7, :])
    a1 = jnp.where(row == 0, h7, pltpu.roll(a, 1, 0))
    a2 = jnp.where(row == 0, h6, jnp.where(row == 1, h7, pltpu.roll(a, 2, 0)))
    u = ((b_ref[...] + w_ref[0:1, :] * a2) + w_ref[1:2, :] * a1) + w_ref[2:3, :] * a
    return u, a1, a2


def _conv_specs(S, tm, tc, nc):
    hb = tm // 8
    main = lambda off: pl.BlockSpec((tm, tc), lambda j, i: (i, j + off))
    halo = lambda off: pl.BlockSpec((8, tc), lambda j, i: (jnp.maximum(i * hb - 1, 0), j + off))
    wspec = lambda off: pl.BlockSpec((3, tc), lambda j, i: (0, j + off))
    bspec = lambda off: pl.BlockSpec((1, tc), lambda j, i: (0, j + off))
    return main, halo, wspec, bspec


def _conv_geglu_fwd(a, conv_w, conv_b, *, name):
    S = a.shape[0]
    tm, tc = _tile(S, 512), _tile(D_FF, 1408)
    nc = D_FF // tc
    main, halo, wspec, bspec = _conv_specs(S, tm, tc, nc)

    def body(ag_ref, au_ref, hg_ref, hu_ref, wg_ref, wu_ref, bg_ref, bu_ref, z_ref):
        first = pl.program_id(1) == 0
        gate, _, _ = _conv_taps(ag_ref[...], hg_ref[...], wg_ref, bg_ref, first)
        up, _, _ = _conv_taps(au_ref[...], hu_ref[...], wu_ref, bu_ref, first)
        cdf = 0.5 * (1.0 + jnp.tanh(GELU_C * (gate + GELU_A * (gate * gate * gate))))
        z_ref[...] = (gate * cdf * up).astype(BF16)

    return pl.pallas_call(
        body, name=name, grid=(nc, S // tm),
        in_specs=[main(0), main(nc), halo(0), halo(nc), wspec(0), wspec(nc), bspec(0), bspec(nc)],
        out_specs=pl.BlockSpec((tm, tc), lambda j, i: (i, j)),
        out_shape=jax.ShapeDtypeStruct((S, D_FF), BF16),
        compiler_params=_params("parallel", "arbitrary"),
    )(a, a, a, a, conv_w, conv_w, conv_b, conv_b)


def _conv_geglu_bwd(a, conv_w, conv_b, dz, *, name):
    S = a.shape[0]
    tm, tc = _tile(S, 512), _tile(D_FF, 1408)
    nc = D_FF // tc
    main, halo, wspec, bspec = _conv_specs(S, tm, tc, nc)

    def body(ag_ref, au_ref, hg_ref, hu_ref, wg_ref, wu_ref, bg_ref, bu_ref, dz_ref, du_ref, dw_ref, db_ref):
        first = pl.program_id(1) == 0

        @pl.when(first)
        def _():
            dw_ref[...] = jnp.zeros_like(dw_ref)
            db_ref[...] = jnp.zeros_like(db_ref)

        gate, g1, g2 = _conv_taps(ag_ref[...], hg_ref[...], wg_ref, bg_ref, first)
        up, u1, u2 = _conv_taps(au_ref[...], hu_ref[...], wu_ref, bu_ref, first)
        dz = dz_ref[...]
        g2x = gate * gate
        th = jnp.tanh(GELU_C * (gate + GELU_A * (g2x * gate)))
        cdf = 0.5 * (1.0 + th)
        dgelu = cdf + gate * (0.5 * (1.0 - th * th) * (GELU_C * (1.0 + 3.0 * GELU_A * g2x)))
        dug = dz * up * dgelu
        duu = dz * (gate * cdf)
        du_ref[0] = dug
        du_ref[1] = duu
        for half, du, taps in ((0, dug, (g2, g1, ag_ref[...])), (1, duu, (u2, u1, au_ref[...]))):
            for tap in range(3):
                dw_ref[half, tap:tap + 1, :] += jnp.sum(du * taps[tap], axis=0, keepdims=True)
            db_ref[half] += jnp.sum(du, axis=0, keepdims=True)

    return pl.pallas_call(
        body, name=name, grid=(nc, S // tm),
        in_specs=[main(0), main(nc), halo(0), halo(nc), wspec(0), wspec(nc), bspec(0), bspec(nc),
                  pl.BlockSpec((tm, tc), lambda j, i: (i, j))],
        out_specs=[pl.BlockSpec((2, tm, tc), lambda j, i: (0, i, j)), pl.BlockSpec((2, 3, tc), lambda j, i: (0, 0, j)),
                   pl.BlockSpec((2, 1, tc), lambda j, i: (0, 0, j))],
        out_shape=[jax.ShapeDtypeStruct((2, S, D_FF), F32), jax.ShapeDtypeStruct((2, 3, D_FF), F32),
                   jax.ShapeDtypeStruct((2, 1, D_FF), F32)],
        compiler_params=_params("parallel", "arbitrary"),
    )(a, a, a, a, conv_w, conv_w, conv_b, conv_b, dz)


def _conv_bwd_input(du, conv_w, *, name):
    S = du.shape[1]
    tm, tc = _tile(S, 512), _tile(D_FF, 1408)
    nc = D_FF // tc
    nr = S // tm
    hb = tm // 8

    def body(du_ref, nx_ref, w_ref, da_ref):
        last = pl.program_id(2) == nr - 1
        d = du_ref[0]
        row = lax.broadcasted_iota(jnp.int32, d.shape, 0)
        n0 = jnp.where(last, 0.0, nx_ref[0, 0:1, :])
        n1 = jnp.where(last, 0.0, nx_ref[0, 1:2, :])
        d1 = jnp.where(row == tm - 1, n0, pltpu.roll(d, tm - 1, 0))
        d2 = jnp.where(row == tm - 1, n1, jnp.where(row == tm - 2, n0, pltpu.roll(d, tm - 2, 0)))
        da_ref[...] = (w_ref[2:3, :] * d + w_ref[1:2, :] * d1 + w_ref[0:1, :] * d2).astype(BF16)

    return pl.pallas_call(
        body, name=name, grid=(2, nc, nr),
        in_specs=[pl.BlockSpec((1, tm, tc), lambda h, j, i: (h, i, j)),
                  pl.BlockSpec((1, 8, tc), lambda h, j, i: (h, jnp.minimum((i + 1) * hb, S // 8 - 1), j)),
                  pl.BlockSpec((3, tc), lambda h, j, i: (0, h * nc + j))],
        out_specs=pl.BlockSpec((tm, tc), lambda h, j, i: (i, h * nc + j)),
        out_shape=jax.ShapeDtypeStruct((S, 2 * D_FF), BF16),
        compiler_params=_params("parallel", "parallel", "arbitrary"),
    )(du, du, conv_w)


def _adamw_update(w, g, m, v):
    m = ADAM_B1 * m + (1.0 - ADAM_B1) * g
    v = ADAM_B2 * v + (1.0 - ADAM_B2) * jnp.square(g)
    m_hat = m / (1.0 - ADAM_B1 ** ADAM_STEP)
    v_hat = v / (1.0 - ADAM_B2 ** ADAM_STEP)
    return -ADAM_LR * (m_hat / (jnp.sqrt(v_hat) + ADAM_EPS) + ADAM_WD * w), m, v


def _adamw(w, g, m, v, *, name):
    L, A, B = w.shape
    ta = _tile(A, ROW_TILE)

    def body(w_ref, g_ref, m_ref, v_ref, d_ref, mo_ref, vo_ref):
        d_ref[...], mo_ref[...], vo_ref[...] = _adamw_update(w_ref[...], g_ref[...], m_ref[...], v_ref[...])

    blk = pl.BlockSpec((None, ta, B), lambda l, i: (l, i, 0))
    shp = jax.ShapeDtypeStruct((L, A, B), F32)
    return pl.pallas_call(
        body, name=name, grid=(L, A // ta),
        in_specs=[blk] * 4, out_specs=[blk] * 3, out_shape=[shp] * 3,
        compiler_params=_params("parallel", "parallel"),
    )(w, g, m, v)


def _scalar(v):
    return jnp.reshape(v, (1,)).astype(jnp.int32)


def _adamw_halves(w, g_mine, g_other, m, v, *, name):
    L, A, B = w.shape
    ta = _tile(A // 2, ROW_TILE)
    nb = A // 2 // ta

    def body(c_ref, w_ref, gm_ref, go_ref, m_ref, v_ref, g_ref, d_ref, mo_ref, vo_ref):
        g = jnp.where(pl.program_id(1) // nb == c_ref[0], gm_ref[...], go_ref[...])
        g_ref[...] = g
        d_ref[...], mo_ref[...], vo_ref[...] = _adamw_update(w_ref[...], g, m_ref[...], v_ref[...])

    blk = pl.BlockSpec((None, ta, B), lambda l, i, c_ref: (l, i, 0))
    half = pl.BlockSpec((None, ta, B), lambda l, i, c_ref: (l, i % nb, 0))
    shp = jax.ShapeDtypeStruct((L, A, B), F32)
    return pl.pallas_call(
        body, name=name,
        grid_spec=pltpu.PrefetchScalarGridSpec(num_scalar_prefetch=1, grid=(L, A // ta),
                                               in_specs=[blk, half, half, blk, blk], out_specs=[blk] * 4),
        out_shape=[shp] * 4,
        compiler_params=_params("parallel", "parallel"),
    )(_scalar(lax.axis_index("c")), w, g_mine, g_other, m, v)


def _chip_index():
    return 2 * lax.axis_index("x") + lax.axis_index("y")


def _pair_sum(g, recv, *, name):
    n, A, B = g.shape
    ta = _tile(A // 2, ROW_TILE)
    nb = A // 2 // ta

    def body(c_ref, g_ref, r_ref, o_ref):
        o_ref[...] = g_ref[...] + r_ref[...]

    return pl.pallas_call(
        body, name=name,
        grid_spec=pltpu.PrefetchScalarGridSpec(
            num_scalar_prefetch=1, grid=(n, nb),
            in_specs=[pl.BlockSpec((None, ta, B), lambda s, r, c_ref: (s, c_ref[0] * nb + r, 0)),
                      pl.BlockSpec((None, ta, B), lambda s, r, c_ref: (s, r, 0))],
            out_specs=pl.BlockSpec((None, ta, B), lambda s, r, c_ref: (s, r, 0))),
        out_shape=jax.ShapeDtypeStruct((n, A // 2, B), F32),
        compiler_params=_params("parallel", "parallel"),
    )(_scalar(lax.axis_index("c")), g, recv)


def _chip_sum(landed, own, *, name):
    n, A2, B = landed.shape
    ta = _tile(A2, ROW_TILE)

    def body(me_ref, *refs):
        slots, own_ref, o_ref = refs[:n], refs[n], refs[n + 1]
        parts = [jnp.where(me_ref[0] == s, own_ref[...], slots[s][...]) for s in range(n)]
        o_ref[...] = ((parts[0] + parts[1]) + parts[2]) + parts[3]

    def slot(s):
        return pl.BlockSpec((None, ta, B), lambda r, me_ref: (jnp.where(me_ref[0] == s, (s + 1) % n, s), r, 0))

    return pl.pallas_call(
        body, name=name,
        grid_spec=pltpu.PrefetchScalarGridSpec(
            num_scalar_prefetch=1, grid=(A2 // ta,),
            in_specs=[slot(s) for s in range(n)] + [pl.BlockSpec((None, ta, B), lambda r, me_ref: (me_ref[0], r, 0))],
            out_specs=pl.BlockSpec((ta, B), lambda r, me_ref: (r, 0))),
        out_shape=jax.ShapeDtypeStruct((A2, B), F32),
        compiler_params=_params("parallel"),
    )(_scalar(_chip_index()), *([landed] * n), own)


HBM_SPEC = pl.BlockSpec(memory_space=pl.ANY)
COMM_PARAMS = pltpu.CompilerParams(has_side_effects=True)


def _mesh_pos():
    return lax.axis_index("x"), lax.axis_index("y"), lax.axis_index("c")


def _other_chips(x, y):
    return [(1 - x, y), (x, 1 - y), (1 - x, 1 - y)]


def _remote(src, dst, send_sems, recv_sems, k, to):
    return pltpu.make_async_remote_copy(src_ref=src, dst_ref=dst, send_sem=send_sems.at[k], recv_sem=recv_sems.at[k],
                                        device_id=to, device_id_type=MESH)


def _place_own(gathered, shard, *, name):
    A, B = shard.shape
    ta = _tile(A, ROW_TILE)

    def body(me_ref, s_ref, g_ref, o_ref):
        o_ref[...] = s_ref[...]

    return pl.pallas_call(
        body, name=name,
        grid_spec=pltpu.PrefetchScalarGridSpec(
            num_scalar_prefetch=1, grid=(A // ta,),
            in_specs=[pl.BlockSpec((ta, B), lambda r, me_ref: (r, 0)), HBM_SPEC],
            out_specs=pl.BlockSpec((None, ta, B), lambda r, me_ref: (me_ref[0], r, 0))),
        out_shape=jax.ShapeDtypeStruct(gathered.shape, gathered.dtype),
        input_output_aliases={2: 0},
        compiler_params=_params("parallel"),
    )(_scalar(_chip_index()), shard, gathered)


def _half_rows(rows, c, align=8):
    assert (rows // 2) % align == 0
    return pl.ds(pl.multiple_of(c * (rows // 2), align), rows // 2)


BF16_ROWS = 16


def _halved(rows):
    return rows % (2 * BF16_ROWS) == 0


def _gather_copies(srcs, lands, send_sems, recv_sems):
    x, y, c = _mesh_pos()
    me = 2 * x + y
    out = []
    for k in range(len(srcs)):
        a = srcs[k].shape[0]
        rows = _half_rows(a, c, BF16_ROWS) if _halved(a) else pl.ds(0, a)
        for j, (px, py) in enumerate(_other_chips(x, y)):
            send = _remote(srcs[k].at[rows], lands[k].at[me, rows], send_sems, recv_sems, 3 * k + j, (px, py, c))
            recv = _remote(srcs[k].at[rows], lands[k].at[2 * px + py, rows], send_sems, recv_sems, 3 * k + j, (px, py, c))
            out.append((send, recv))
    return out


def _gather_start(srcs):
    nl, n = len(srcs), len(srcs[0])
    lands = [[lax.empty((N_CHIPS,) + s.shape, s.dtype) for s in sl] for sl in srcs]
    flat = [a for l in range(nl) for a in srcs[l] + lands[l]]

    def body(*refs):
        bufs, sems, token = refs[:len(flat)], refs[len(flat):len(flat) + 2 * nl], refs[-1]
        for l in range(nl):
            mine = bufs[2 * n * l:2 * n * (l + 1)]
            for send, _ in _gather_copies(mine[:n], mine[n:], sems[2 * l], sems[2 * l + 1]):
                send.start()
        token[...] = jnp.zeros_like(token)

    res = pl.pallas_call(
        body, name="weight_gather_start",
        in_specs=[HBM_ONLY] * len(flat),
        out_specs=[SEM_SPEC] * (2 * nl) + [HBM_ONLY] * len(flat) + [pl.BlockSpec(memory_space=pltpu.VMEM)],
        out_shape=[pltpu.SemaphoreType.DMA((3 * n,))] * (2 * nl) + [pltpu.HBM(a.shape, a.dtype) for a in flat]
        + [jax.ShapeDtypeStruct((8, LANES), F32)],
        input_output_aliases={i: 2 * nl + i for i in range(len(flat))},
        compiler_params=SPLIT_PARAMS,
    )(*[pltpu.with_memory_space_constraint(a, pltpu.HBM) for a in flat])
    bufs = res[2 * nl:2 * nl + len(flat)]
    state = [(res[2 * l], res[2 * l + 1], list(bufs[2 * n * l:2 * n * l + n]), list(bufs[2 * n * l + n:2 * n * (l + 1)]))
             for l in range(nl)]
    return state, res[-1][0:1, 0:1]


def _gather_wait(state, after, *, name):
    send_sems, recv_sems, srcs, lands = state
    n = len(srcs)

    def body(*refs):
        for send, recv in _gather_copies(refs[:n], refs[n:2 * n], refs[2 * n], refs[2 * n + 1]):
            send.wait_send()
            recv.wait_recv()

    res = pl.pallas_call(
        body, name=name,
        in_specs=[HBM_ONLY] * (2 * n) + [SEM_SPEC, SEM_SPEC, HBM_SPEC],
        out_specs=[HBM_ONLY] * (2 * n),
        out_shape=[pltpu.HBM(a.shape, a.dtype) for a in srcs + lands],
        input_output_aliases={i: i for i in range(2 * n)},
        compiler_params=SPLIT_PARAMS,
    )(*srcs, *lands, send_sems, recv_sems, after)
    return list(res[:n]), list(res[n:])


def _gather_forward(lands, *, name):
    n = len(lands)

    def body(*refs):
        bufs, outs = refs[:n], refs[n:2 * n]
        send_sems, recv_sems = refs[2 * n:]
        x, y, c = _mesh_pos()
        copies, waits = [], []
        for k in range(n):
            a = lands[k].shape[1]
            if not _halved(a):
                continue
            for j, (px, py) in enumerate(_other_chips(x, y)):
                mine = 2 * px + py, _half_rows(a, c, BF16_ROWS)
                copies.append(_remote(bufs[k].at[mine], outs[k].at[mine], send_sems, recv_sems, 3 * k + j, (x, y, 1 - c)))
                lands_here = outs[k].at[2 * px + py, _half_rows(a, 1 - c, BF16_ROWS)]
                waits.append(_remote(lands_here, lands_here, send_sems, recv_sems, 3 * k + j, (x, y, 1 - c)))
        for cp in copies:
            cp.start()
        for cp in waits:
            cp.wait_recv()
        for cp in copies:
            cp.wait_send()

    return pl.pallas_call(
        body, name=name,
        in_specs=[HBM_SPEC] * n, out_specs=[HBM_SPEC] * n,
        out_shape=[jax.ShapeDtypeStruct(a.shape, a.dtype) for a in lands],
        scratch_shapes=[pltpu.SemaphoreType.DMA((3 * n,)), pltpu.SemaphoreType.DMA((3 * n,))],
        input_output_aliases={i: i for i in range(n)},
        compiler_params=COMM_PARAMS,
    )(*lands)


def _sibling_exchange(gs, *, name):
    n = len(gs)

    def body(*refs):
        ins, outs = refs[:n], refs[n:2 * n]
        send_sems, recv_sems = refs[2 * n:]
        x, y, c = _mesh_pos()
        copies = [_remote(ins[k].at[:, _half_rows(gs[k].shape[1], 1 - c)], outs[k], send_sems, recv_sems, k, (x, y, 1 - c))
                  for k in range(n)]
        for cp in copies:
            cp.start()
        for cp in copies:
            cp.wait()

    return pl.pallas_call(
        body, name=name,
        in_specs=[HBM_SPEC] * n, out_specs=[HBM_SPEC] * n,
        out_shape=[jax.ShapeDtypeStruct((g.shape[0], g.shape[1] // 2, g.shape[2]), g.dtype) for g in gs],
        scratch_shapes=[pltpu.SemaphoreType.DMA((n,)), pltpu.SemaphoreType.DMA((n,))],
        compiler_params=COMM_PARAMS,
    )(*gs)


HBM_ONLY = pl.BlockSpec(memory_space=pltpu.HBM)
SEM_SPEC = pl.BlockSpec(memory_space=pltpu.SEMAPHORE)
SPLIT_PARAMS = pltpu.CompilerParams(has_side_effects=pltpu.SideEffectType.DATAFLOW_SIDE_EFFECTING)


def _scatter_copies(srcs, lands, send_sems, recv_sems):
    x, y, c = _mesh_pos()
    me = 2 * x + y
    out = []
    for k in range(len(srcs)):
        for j, (px, py) in enumerate(_other_chips(x, y)):
            s = 2 * px + py
            send = _remote(srcs[k].at[s], lands[k].at[me], send_sems, recv_sems, 3 * k + j, (px, py, c))
            recv = _remote(srcs[k].at[s], lands[k].at[s], send_sems, recv_sems, 3 * k + j, (px, py, c))
            out.append((send, recv))
    return out


def _scatter_start(ps, *, name):
    n = len(ps)
    lands = [lax.empty(p.shape, p.dtype) for p in ps]

    def body(*refs):
        srcs, zones = refs[:n], refs[n:2 * n]
        send_sems, recv_sems, token = refs[2 * n], refs[2 * n + 1], refs[-1]
        for send, _ in _scatter_copies(srcs, zones, send_sems, recv_sems):
            send.start()
        token[...] = jnp.zeros_like(token)

    hbm = lambda a: pltpu.HBM(a.shape, a.dtype)
    res = pl.pallas_call(
        body, name=name,
        in_specs=[HBM_ONLY] * (2 * n),
        out_specs=[SEM_SPEC, SEM_SPEC] + [HBM_ONLY] * (2 * n) + [pl.BlockSpec(memory_space=pltpu.VMEM)],
        out_shape=[pltpu.SemaphoreType.DMA((3 * n,)), pltpu.SemaphoreType.DMA((3 * n,))] + [hbm(a) for a in ps + lands]
        + [jax.ShapeDtypeStruct((8, LANES), F32)],
        input_output_aliases={i: 2 + i for i in range(2 * n)},
        compiler_params=SPLIT_PARAMS,
    )(*[pltpu.with_memory_space_constraint(a, pltpu.HBM) for a in ps + lands])
    return (res[0], res[1], list(res[2:2 + n]), list(res[2 + n:2 + 2 * n])), res[-1][0:1, 0:1]


def _scatter_wait(started, after, *, name):
    ng = len(started)
    sizes = [len(st[2]) for st in started]
    offs = [2 * sum(sizes[:i]) for i in range(ng + 1)]
    flat = [a for (_, _, ps, lands) in started for a in ps + lands]

    def body(*refs):
        bufs, sems = refs[:len(flat)], refs[len(flat):len(flat) + 2 * ng]
        for i, n in enumerate(sizes):
            srcs, zones = bufs[offs[i]:offs[i] + n], bufs[offs[i] + n:offs[i + 1]]
            for send, recv in _scatter_copies(srcs, zones, sems[2 * i], sems[2 * i + 1]):
                send.wait_send()
                recv.wait_recv()

    res = pl.pallas_call(
        body, name=name,
        in_specs=[HBM_ONLY] * len(flat) + [SEM_SPEC] * (2 * ng) + [HBM_SPEC],
        out_specs=[HBM_ONLY] * len(flat),
        out_shape=[pltpu.HBM(a.shape, a.dtype) for a in flat],
        input_output_aliases={i: i for i in range(len(flat))},
        compiler_params=SPLIT_PARAMS,
    )(*flat, *[s for (ss, rs, _, _) in started for s in (ss, rs)], after)
    return [(list(res[offs[i]:offs[i] + n]), list(res[offs[i] + n:offs[i + 1]])) for i, n in enumerate(sizes)]


def _sibling_share(hs):
    n = len(hs)

    def body(*refs):
        ins, outs = refs[:n], refs[n:2 * n]
        send_sems, recv_sems = refs[2 * n:]
        x, y, c = _mesh_pos()
        copies = [_remote(ins[k], outs[k], send_sems, recv_sems, k, (x, y, 1 - c)) for k in range(n)]
        for cp in copies:
            cp.start()
        for cp in copies:
            cp.wait()

    return pl.pallas_call(
        body, name="grad_sibling_share",
        in_specs=[HBM_SPEC] * n, out_specs=[HBM_SPEC] * n,
        out_shape=[jax.ShapeDtypeStruct(h.shape, h.dtype) for h in hs],
        scratch_shapes=[pltpu.SemaphoreType.DMA((n,)), pltpu.SemaphoreType.DMA((n,))],
        compiler_params=COMM_PARAMS,
    )(*hs)


def _allreduce_small(part):
    rows, C = part.shape

    def body(p_ref, o_ref, slots, send_sems, recv_sems):
        x, y, c = _mesh_pos()
        me = 4 * x + 2 * y + c
        slots[me] = p_ref[...]
        copies = []
        for k in range(1, 8):
            kx, ky, kc = (k >> 2) & 1, (k >> 1) & 1, k & 1
            peer = (x ^ kx if kx else x, y ^ ky if ky else y, c ^ kc if kc else c)
            cp = _remote(p_ref, slots.at[me], send_sems, recv_sems, k - 1, peer)
            cp.start()
            copies.append((cp, peer))
        for k, (cp, peer) in enumerate(copies):
            src = 4 * peer[0] + 2 * peer[1] + peer[2]
            _remote(p_ref, slots.at[src], send_sems, recv_sems, k, peer).wait_recv()
        for cp, _ in copies:
            cp.wait_send()
        total = slots[0]
        for d in range(1, 8):
            total = total + slots[d]
        o_ref[...] = total

    return pl.pallas_call(
        body, name="small_grad_allreduce",
        in_specs=[pl.BlockSpec(memory_space=pltpu.VMEM)], out_specs=pl.BlockSpec(memory_space=pltpu.VMEM),
        out_shape=jax.ShapeDtypeStruct((rows, C), F32),
        scratch_shapes=[pltpu.VMEM((8, rows, C), F32), pltpu.SemaphoreType.DMA((7,)), pltpu.SemaphoreType.DMA((7,))],
        compiler_params=pltpu.CompilerParams(has_side_effects=True, vmem_limit_bytes=VMEM_LIMIT_BYTES),
    )(part)


def _pad_w_uq(w):
    lead = w.shape[:-1]
    w = w.reshape(lead + (MLA_HEADS, MLA_QK))
    w = jnp.concatenate([w, jnp.zeros(lead + (MLA_HEADS, MLA_PAD - MLA_QK), w.dtype)], axis=-1)
    return w.reshape(lead + (MLA_HEADS * MLA_PAD,))


def _unpad_w_uq(g):
    lead = g.shape[:-1]
    return g.reshape(lead + (MLA_HEADS, MLA_PAD))[..., :MLA_QK].reshape(lead + (MLA_HEADS * MLA_QK,))


def _t(a):
    return jnp.swapaxes(a, -1, -2)


def _shards_of_cols(w):
    A, NB = w.shape
    return w.reshape(A, N_CHIPS, NB // N_CHIPS).transpose(1, 0, 2)


BIG = ("w_in", "w_uq", "w_ukv", "w_out", "w_up", "w_down")
SMALL = ("attn_pre_norm", "forget_bias", "swa_sinks", "rel_bias", "q_latent_norm", "kv_latent_norm", "group_norm",
         "attn_post_norm", "ffn_pre_norm", "conv_b", "ffn_post_norm")
WEIGHTS = ("attn_pre_norm", "w_in", "forget_bias", "swa_sinks", "rel_bias", "q_latent_norm", "w_uq", "kv_latent_norm",
           "w_ukv", "group_norm", "w_out", "attn_post_norm", "ffn_pre_norm", "w_up", "conv_w", "conv_b", "w_down",
           "ffn_post_norm")


def _pack(arrs, cols, row_mult):
    flat = jnp.concatenate([a.reshape(-1) for a in arrs])
    n = flat.shape[0]
    per = cols * row_mult
    total = -(-n // per) * per
    return jnp.pad(flat, (0, total - n)).reshape(total // cols, cols)


def _unpack(packed, shapes):
    flat = packed.reshape(-1)
    out, off = [], 0
    for shp in shapes:
        n = int(np.prod(shp))
        out.append(flat[off:off + n].reshape(shp))
        off += n
    return out


LAYER_KEYS = ("w_qkv_t", "w_lat_t", "w_in_t", "w_uq_p", "w_uq_t", "w_ukv", "w_ukv_t", "w_out", "w_up", "w_down", "conv_w")


def _layer_weights(gathered):
    cols = lambda g: g.transpose(1, 0, 2).reshape(g.shape[1], N_CHIPS * g.shape[2])
    w_in_t = _t(gathered["w_in"]).reshape(IN_COLS, D_MODEL)
    w_in_t = jnp.pad(w_in_t, ((0, IN_ROWS - IN_COLS), (0, 0)))
    w_uq_p = _pad_w_uq(cols(gathered["w_uq"]))
    w_ukv = cols(gathered["w_ukv"])
    return dict(w_qkv_t=w_in_t[:QKV_ROWS], w_lat_t=w_in_t[QKV_ROWS:], w_in_t=w_in_t, w_uq_p=w_uq_p, w_uq_t=_t(w_uq_p),
                w_ukv=w_ukv, w_ukv_t=_t(w_ukv), w_out=gathered["w_out"].reshape(D_MODEL, D_MODEL), w_up=gathered["w_up"],
                w_down=gathered["w_down"].reshape(D_FF, D_MODEL), conv_w=cols(gathered["conv_w"]))


def _local_step(x, target, W, layer_weights, layer_done):
    W = dict(W, **{key: [None] * DEPTH for key in LAYER_KEYS})
    S = x.shape[0]
    tq_tabs, tm_tabs = _rope_tables(S)
    onehot_t = _rel_onehot_t()
    bias_t = _bias_table(W["rel_bias"].T, onehot_t).reshape(SWA_Q_HEADS, 2 * WINDOW, WINDOW)
    row = lambda a: a.reshape(1, -1)
    col = lambda a: a.reshape(-1, 1)
    fox_rows = (FOX_ROW0, FOX_ROW0 + FOX_HEADS * HEAD_DIM, FOX_ROW0 + 2 * FOX_HEADS * HEAD_DIM, SWA_Q_HEADS)
    fox = dict(rows=fox_rows, H=FOX_HEADS, Dk=HEAD_DIM, Dv=HEAD_DIM, scale=HEAD_DIM ** -0.5)
    mla = dict(rows=(0, 0, 0, SWA_Q_HEADS + FOX_HEADS), H=MLA_HEADS, Dk=MLA_PAD, Dv=HEAD_DIM, scale=MLA_QK ** -0.5)

    saved = []
    h = _rms_fwd(x, row(W["attn_pre_norm"][0]), name="rms_in")
    for l in range(DEPTH):
        sv = {"x0": x, "h1": h}
        for key, val in layer_weights(l, h).items():
            W[key][l] = val
        qkv = _matmul(W["w_qkv_t"][l], h, tb=True, out_dtype=BF16, name="proj_qkv")
        lat = _matmul(W["w_lat_t"][l], h, tb=True, name="proj_lat")
        oa, lse_a = _swa_fwd(qkv, bias_t, W["swa_sinks"][l], name="swa_fwd")
        fb_col = jnp.pad(col(W["forget_bias"][l]), ((0, GATE_ROWS - FOX_HEADS), (0, 0)))
        f4 = _gate_fwd(lat, fb_col, name="fox_gate_fwd")[:FOX_HEADS]
        f_row, f_col = f4[:, None, :], f4.T
        of, lse_f = _attn_fwd(qkv, qkv, qkv, f_row=f_row, f_col=f_col, name="fox_fwd", **fox)
        nq, nkv, qm, km, vm = _mla_prep_fwd(lat, col(W["q_latent_norm"][l]), col(W["kv_latent_norm"][l]), W["w_uq_t"][l],
                                            W["w_ukv_t"][l], tq_tabs, tm_tabs, name="mla_prep_fwd")
        oc, lse_c = _attn_fwd(qm, km, vm, name="mla_fwd", **mla)
        mixed = _group_norm_fwd(oa, of, oc, col(W["group_norm"][l]), name="group_norm_fwd")
        y = _matmul(mixed, W["w_out"][l], ta=True, name="proj_out")
        x1, h2 = _resid_rms(x, y, row(W["attn_post_norm"][l]), row(W["ffn_pre_norm"][l]), name="attn_resid")
        a = _matmul(h2, W["w_up"][l], b_shards=True, name="ffn_up")
        z = _conv_geglu_fwd(a, W["conv_w"][l], row(W["conv_b"][l]), name="conv_geglu_fwd")
        y2 = _matmul(z, W["w_down"][l], name="ffn_down")
        g_next = row(W["attn_pre_norm"][l + 1]) if l + 1 < DEPTH else None
        x2, h_next = _resid_rms(x1, y2, row(W["ffn_post_norm"][l]), g_next, name="ffn_resid")
        sv.update(qkv=qkv, lat=lat, oa=oa, lse_a=lse_a, fb_col=fb_col, f_row=f_row, f_col=f_col, of=of, lse_f=lse_f,
                  nq=nq, nkv=nkv, qm=qm, km=km, vm=vm, oc=oc, lse_c=lse_c, mixed=mixed, y=y, x1=x1, h2=h2, a=a, z=z, y2=y2)
        saved.append(sv)
        x, h = x2, h_next

    loss, dx = _loss_head(x, target)

    G = {k: [None] * DEPTH for k in WEIGHTS if k != "rel_bias" and k not in BIG}
    dbias_layers = [None] * DEPTH
    for l in reversed(range(DEPTH)):
        sv = saved[l]
        gb = {}
        dy2, dg = _rms_bwd(sv["y2"], row(W["ffn_post_norm"][l]), dx, out_dtype=BF16, name="ffn_post_bwd")
        G["ffn_post_norm"][l] = dg[0]
        dz = _matmul(dy2, W["w_down"][l], tb=True, name="ffn_down_dx")
        gb["w_down"] = _matmul(sv["z"], dy2, ta=True, name="ffn_down_dw").reshape(N_CHIPS, D_FF // N_CHIPS, D_MODEL)
        du, dcw, dcb = _conv_geglu_bwd(sv["a"], W["conv_w"][l], row(W["conv_b"][l]), dz, name="conv_geglu_bwd")
        G["conv_w"][l] = dcw.transpose(1, 0, 2).reshape(3, 2 * D_FF)
        G["conv_b"][l] = dcb.reshape(2 * D_FF)
        da = _conv_bwd_input(du, W["conv_w"][l], name="conv_bwd_input")
        dh2 = _matmul(da, W["w_up"][l], tb=True, b_shards=True, name="ffn_up_dx")
        gb["w_up"] = _matmul(sv["h2"], da, ta=True, out_shards=True, name="ffn_up_dw")
        token = layer_done(l, gb)
        gb = {}
        dx1, dg = _rms_bwd(sv["x1"], row(W["ffn_pre_norm"][l]) + token, dh2, resid=dx, out_dtype=F32, name="ffn_pre_bwd")
        G["ffn_pre_norm"][l] = dg[0]
        dy, dg = _rms_bwd(sv["y"], row(W["attn_post_norm"][l]), dx1, out_dtype=BF16, name="attn_post_bwd")
        G["attn_post_norm"][l] = dg[0]
        dmixed = _matmul(W["w_out"][l], dy, tb=True, name="proj_out_dx")
        gb["w_out"] = _matmul(sv["mixed"], dy, name="proj_out_dw").reshape(N_CHIPS, D_MODEL // N_CHIPS, D_MODEL)
        doa, dof, doc, dg, delta = _group_norm_bwd(sv["oa"], sv["of"], sv["oc"], col(W["group_norm"][l]), dmixed,
                                                   name="group_norm_bwd")
        G["group_norm"][l] = dg[:, 0]
        dqa, dkva, dbias_l, dsink = _swa_bwd(sv["qkv"], bias_t, W["swa_sinks"][l], doa, sv["lse_a"],
                                             delta.reshape(-1, S), name="swa_bwd")
        dbias_layers[l] = dbias_l.reshape(SWA_Q_HEADS, -1)
        G["swa_sinks"][l] = dsink[:, 0]
        dqf, dkf, dvf, dfk = _attn_bwd(sv["qkv"], sv["qkv"], sv["qkv"], do=dof, lse=sv["lse_f"], delta=delta,
                                       f_row=sv["f_row"], f_col=sv["f_col"], name="fox_bwd", **fox)
        dF = jnp.pad(dfk.T, ((0, GATE_ROWS - FOX_HEADS), (0, 0)))
        dflog, dfb = _gate_bwd(sv["lat"], sv["fb_col"], dF, name="fox_gate_bwd")
        G["forget_bias"][l] = dfb[:FOX_HEADS, 0]
        dqm, dkm, dvm = _attn_bwd(sv["qm"], sv["km"], sv["vm"], do=doc, lse=sv["lse_c"], delta=delta, name="mla_bwd", **mla)
        dlat, dwq_t, dwkv_t, dgq, dgkv = _mla_prep_bwd(
            sv["lat"], sv["nq"], sv["nkv"], col(W["q_latent_norm"][l]), col(W["kv_latent_norm"][l]), W["w_uq_p"][l],
            W["w_ukv"][l], tq_tabs, tm_tabs, dqm, dkm, dvm, dflog, name="mla_prep_bwd")
        gb["w_uq"], gb["w_ukv"] = _shards_of_cols(_unpad_w_uq(dwq_t.T)), _shards_of_cols(dwkv_t.T)
        G["q_latent_norm"][l], G["kv_latent_norm"][l] = dgq[:, 0], dgkv[:, 0]
        dproj = _dproj_cast(dqa, dkva, dqf, dkf, dvf, dlat, name="dproj_cast")
        dh1 = _matmul(dproj, W["w_in_t"][l], ta=True, name="proj_in_dx")
        dw_in_t = _matmul(dproj, sv["h1"], name="proj_in_dw")
        gb["w_in"] = _t(dw_in_t[:IN_COLS].reshape(N_CHIPS, IN_COLS // N_CHIPS, D_MODEL))
        token = layer_done(l, gb)
        dx, dg = _rms_bwd(sv["x0"], row(W["attn_pre_norm"][l]) + token, dh1, resid=dx1, out_dtype=F32, name="attn_pre_bwd")
        G["attn_pre_norm"][l] = dg[0]

    grads = {k: jnp.stack(v) for k, v in G.items()}
    grads["rel_bias"] = _bias_table_bwd(jnp.stack(dbias_layers), onehot_t).T
    return loss, dx, grads


def kernel(x, attn_pre_norm, w_in, forget_bias, swa_sinks, rel_bias, q_latent_norm, w_uq, kv_latent_norm, w_ukv, group_norm, w_out, attn_post_norm, ffn_pre_norm, w_up, conv_w, conv_b, w_down, ffn_post_norm, loss_target, m_attn_pre_norm, m_w_in, m_forget_bias, m_swa_sinks, m_rel_bias, m_q_latent_norm, m_w_uq, m_kv_latent_norm, m_w_ukv, m_group_norm, m_w_out, m_attn_post_norm, m_ffn_pre_norm, m_w_up, m_conv_w, m_conv_b, m_w_down, m_ffn_post_norm, v_attn_pre_norm, v_w_in, v_forget_bias, v_swa_sinks, v_rel_bias, v_q_latent_norm, v_w_uq, v_kv_latent_norm, v_w_ukv, v_group_norm, v_w_out, v_attn_post_norm, v_ffn_pre_norm, v_w_up, v_conv_w, v_conv_b, v_w_down, v_ffn_post_norm):
    args = dict(locals())
    w = {k: args[k] for k in WEIGHTS}
    m = {k: args["m_" + k] for k in WEIGHTS}
    v = {k: args["v_" + k] for k in WEIGHTS}

    sent = BIG + ("conv_w",)
    gather_state, token = _gather_start([[w[k][l] if k == "conv_w" else w[k][l].astype(BF16) for k in sent]
                                         for l in range(DEPTH)])
    W = {k: w[k] for k in SMALL}
    W["attn_pre_norm"] = W["attn_pre_norm"] + token

    def layer_weights(l, after):
        srcs, lands = _gather_wait(gather_state[l], after, name=f"weight_gather_wait_{l}")
        lands = _gather_forward(lands, name=f"weight_gather_forward_{l}")
        lands = [_place_own(g, s, name="place_own_shard") for g, s in zip(lands, srcs)]
        return _layer_weights(dict(zip(sent, lands)))

    started, groups = [], []

    def layer_done(l, gb):
        keys = [k for k in BIG if k in gb]
        gs = [gb[k] for k in keys]
        tag = f"{l}_{keys[0]}"
        recv = _sibling_exchange(gs, name="grad_sibling_exchange_" + tag)
        pair = [_pair_sum(gk, rk, name="grad_pair_sum") for gk, rk in zip(gs, recv)]
        state, token = _scatter_start(pair, name="grad_scatter_start_" + tag)
        started.append(state)
        groups.append((l, keys))
        return token

    loss_part, dx, g = _local_step(x[0], loss_target[0], W, layer_weights, layer_done)
    loss = lax.psum(loss_part, ("x", "y", "c"))

    reduced = {}
    for (l, keys), (pair, zones) in zip(groups, _scatter_wait(started, dx, name="grad_scatter_wait")):
        for k, p, z in zip(keys, pair, zones):
            reduced[k, l] = _chip_sum(z, p, name="grad_chip_sum")
    mine = [jnp.stack([reduced[k, l] for l in range(DEPTH)]) for k in BIG]
    other = _sibling_share(mine)
    out_g, out_d, out_m, out_v = {}, {}, {}, {}
    for k, g_mine, g_other in zip(BIG, mine, other):
        out_g[k], out_d[k], out_m[k], out_v[k] = _adamw_halves(w[k], g_mine, g_other, m[k], v[k], name="adamw_" + k)

    small_shapes = [w[k].shape for k in SMALL]
    reduced = _allreduce_small(_pack([g[k] for k in SMALL] + [g["conv_w"]], LANES, 8))
    *g_small, g_cw = _unpack(reduced, small_shapes + [g["conv_w"].shape])
    chip = 2 * lax.axis_index("x") + lax.axis_index("y")
    g_small.append(lax.dynamic_slice_in_dim(g_cw, chip * FF_SHARD, FF_SHARD, axis=2))
    names = SMALL + ("conv_w",)
    shapes = small_shapes + [w["conv_w"].shape]
    packed = lambda arrs: _pack(arrs, LANES, ROW_TILE)[None]
    d_s, m_s, v_s = _adamw(packed([w[k] for k in names]), packed(g_small), packed([m[k] for k in names]),
                           packed([v[k] for k in names]), name="adamw_small")
    out_g.update(zip(names, g_small))
    out_d.update(zip(names, _unpack(d_s, shapes)))
    out_m.update(zip(names, _unpack(m_s, shapes)))
    out_v.update(zip(names, _unpack(v_s, shapes)))

    return (loss, dx[None], *[out_g[k] for k in WEIGHTS], *[out_d[k] for k in WEIGHTS],
            *[out_m[k] for k in WEIGHTS], *[out_v[k] for k in WEIGHTS])
```

```python
import math

import numpy as np
import jax
import jax.numpy as jnp
from jax import lax
from jax.experimental import pallas as pl
from jax.experimental.pallas import tpu as pltpu

F32 = jnp.float32
BF16 = jnp.bfloat16

D_MODEL = 1024
DEPTH = 4
HEAD_DIM = 64
SWA_Q_HEADS = 8
SWA_KV_HEADS = 2
SWA_GROUP = SWA_Q_HEADS // SWA_KV_HEADS
WINDOW = 128
FOX_HEADS = 4
MLA_HEADS = 4
MLA_Q_RANK = 256
MLA_KV_RANK = 128
MLA_NOPE = 64
MLA_ROPE = 32
MLA_QK = MLA_NOPE + MLA_ROPE
ROPE_THETA = 10000.0
REL_BUCKETS = 32
REL_MAX_DIST = 128
D_FF = 2816
EPS = 1e-6
NEG_INF = -1e30
LANES = 128
N_CHIPS = 4

IN_COLS = 1956
IN_ROWS = 2048
QKV_ROWS = 1536
LAT_ROWS = IN_ROWS - QKV_ROWS
LAT_SHIFT = FOX_HEADS
FOX_ROW0 = 768
MLA_PAD = LANES
GATE_ROWS = 8

ADAM_LR = 0.001
ADAM_B1 = 0.9
ADAM_B2 = 0.999
ADAM_EPS = 1e-08
ADAM_WD = 0.01
ADAM_STEP = 10

VMEM_LIMIT_BYTES = 48 * 1024 * 1024
ATT_TILE = 512
ROW_TILE = 256
MESH = pl.DeviceIdType.MESH

NT = (((1,), (1,)), ((), ()))
TN = (((0,), (0,)), ((), ()))
NN = (((1,), (0,)), ((), ()))


def _params(*sem):
    return pltpu.CompilerParams(dimension_semantics=sem, vmem_limit_bytes=VMEM_LIMIT_BYTES)


def _tile(dim, cap):
    for t in (2048, 1408, 1024, 512, 256, 128, 64, 32, 16, 8):
        if t <= cap and dim % t == 0:
            return t
    return dim


def _dot(a, b, dims=NN):
    return lax.dot_general(a, b, dims, preferred_element_type=F32)


def _split3(a):
    a1 = a.astype(BF16)
    r1 = a - a1.astype(F32)
    a2 = r1.astype(BF16)
    a3 = (r1 - a2.astype(F32)).astype(BF16)
    return a1, a2, a3


FF_SHARD = 2 * D_FF // N_CHIPS


def _matmul(a, b, *, ta=False, tb=False, out_dtype=F32, name, b_shards=False, out_shards=False):
    if ta:
        K, M = a.shape
    else:
        M, K = a.shape
    if b_shards:
        K2, N = (2 * D_FF, D_MODEL) if tb else (D_MODEL, 2 * D_FF)
    elif tb:
        N, K2 = b.shape
    else:
        K2, N = b.shape
    assert K == K2, (a.shape, b.shape)
    tm, tn, tk = _tile(M, 1408), _tile(N, 1408), _tile(K, 1408)
    nk = K // tk
    dims = (((0 if ta else 1,), (1 if tb else 0,)), ((), ()))

    def body(a_ref, b_ref, o_ref, acc_ref):
        k = pl.program_id(2)

        @pl.when(k == 0)
        def _():
            acc_ref[...] = jnp.zeros_like(acc_ref)

        acc_ref[...] += lax.dot_general(a_ref[...], b_ref[...], dims, preferred_element_type=F32)

        @pl.when(k == nk - 1)
        def _():
            o_ref[...] = acc_ref[...].astype(o_ref.dtype)

    a_spec = pl.BlockSpec((tk, tm), lambda i, j, k: (k, i)) if ta else pl.BlockSpec((tm, tk), lambda i, j, k: (i, k))
    if b_shards and tb:
        assert tk == FF_SHARD
        b_spec = pl.BlockSpec((None, tn, tk), lambda i, j, k: (k, j, 0))
    elif b_shards:
        assert tn == FF_SHARD
        b_spec = pl.BlockSpec((None, tk, tn), lambda i, j, k: (j, k, 0))
    else:
        b_spec = pl.BlockSpec((tn, tk), lambda i, j, k: (j, k)) if tb else pl.BlockSpec((tk, tn), lambda i, j, k: (k, j))
    if out_shards:
        assert tn == FF_SHARD
        out_spec = pl.BlockSpec((None, tm, tn), lambda i, j, k: (j, i, 0))
        out_shape = jax.ShapeDtypeStruct((N // tn, M, tn), out_dtype)
    else:
        out_spec = pl.BlockSpec((tm, tn), lambda i, j, k: (i, j))
        out_shape = jax.ShapeDtypeStruct((M, N), out_dtype)
    return pl.pallas_call(
        body, name=name, grid=(M // tm, N // tn, nk),
        in_specs=[a_spec, b_spec], out_specs=out_spec, out_shape=out_shape,
        scratch_shapes=[pltpu.VMEM((tm, tn), F32)],
        compiler_params=_params("parallel", "parallel", "arbitrary"),
    )(a, b)


def _seg_rms(xs, g):
    r = lax.rsqrt(jnp.mean(xs * xs, axis=-1, keepdims=True) + EPS)
    return xs * r * g


def _seg_rms_bwd(xs, g, dy):
    r = lax.rsqrt(jnp.mean(xs * xs, axis=-1, keepdims=True) + EPS)
    gd = dy * g
    c = jnp.mean(gd * xs, axis=-1, keepdims=True)
    dx = r * gd - xs * (r * r * r * c)
    dg = jnp.sum(dy * (xs * r), axis=0, keepdims=True)
    return dx, dg


def _rms_fwd(x, g, *, name):
    S, W = x.shape
    tm = _tile(S, 512)

    def body(x_ref, g_ref, o_ref):
        o_ref[...] = _seg_rms(x_ref[...], g_ref[...]).astype(o_ref.dtype)

    return pl.pallas_call(
        body, name=name, grid=(S // tm,),
        in_specs=[pl.BlockSpec((tm, W), lambda i: (i, 0)), pl.BlockSpec((1, W), lambda i: (0, 0))],
        out_specs=pl.BlockSpec((tm, W), lambda i: (i, 0)),
        out_shape=jax.ShapeDtypeStruct((S, W), BF16),
        compiler_params=_params("parallel"),
    )(x, g)


def _rms_bwd(x, g, dy, *, resid=None, out_dtype, name):
    S, W = x.shape
    tm = _tile(S, 512)
    has_resid = resid is not None

    def body(*refs):
        if has_resid:
            x_ref, g_ref, dy_ref, r_ref, dx_ref, dg_ref = refs
        else:
            x_ref, g_ref, dy_ref, dx_ref, dg_ref = refs

        @pl.when(pl.program_id(0) == 0)
        def _():
            dg_ref[...] = jnp.zeros_like(dg_ref)

        dx, dg = _seg_rms_bwd(x_ref[...], g_ref[...], dy_ref[...])
        if has_resid:
            dx = dx + r_ref[...]
        dx_ref[...] = dx.astype(dx_ref.dtype)
        dg_ref[...] += dg

    row = pl.BlockSpec((tm, W), lambda i: (i, 0))
    vec = pl.BlockSpec((1, W), lambda i: (0, 0))
    ins = [x, g, dy] + ([resid] if has_resid else [])
    return pl.pallas_call(
        body, name=name, grid=(S // tm,),
        in_specs=[row, vec, row] + ([row] if has_resid else []),
        out_specs=[row, vec],
        out_shape=[jax.ShapeDtypeStruct((S, W), out_dtype), jax.ShapeDtypeStruct((1, W), F32)],
        compiler_params=_params("arbitrary"),
    )(*ins)


def _resid_rms(x, y, g_post, g_next, *, name):
    S, W = x.shape
    tm = _tile(S, 512)
    with_next = g_next is not None

    def body(*refs):
        if with_next:
            x_ref, y_ref, gp_ref, gn_ref, xo_ref, h_ref = refs
        else:
            x_ref, y_ref, gp_ref, xo_ref = refs
        xn = x_ref[...] + _seg_rms(y_ref[...], gp_ref[...])
        xo_ref[...] = xn
        if with_next:
            h_ref[...] = _seg_rms(xn, gn_ref[...]).astype(BF16)

    row = pl.BlockSpec((tm, W), lambda i: (i, 0))
    vec = pl.BlockSpec((1, W), lambda i: (0, 0))
    outs = [jax.ShapeDtypeStruct((S, W), F32)] + ([jax.ShapeDtypeStruct((S, W), BF16)] if with_next else [])
    res = pl.pallas_call(
        body, name=name, grid=(S // tm,),
        in_specs=[row, row, vec] + ([vec] if with_next else []),
        out_specs=[row] + ([row] if with_next else []),
        out_shape=outs,
        compiler_params=_params("parallel"),
    )(*([x, y, g_post] + ([g_next] if with_next else [])))
    return (res[0], res[1]) if with_next else (res[0], None)


def _col_rms(xs, g):
    r = lax.rsqrt(jnp.mean(xs * xs, axis=0, keepdims=True) + EPS)
    return xs * r * g


def _col_rms_bwd(xs, g, dy):
    r = lax.rsqrt(jnp.mean(xs * xs, axis=0, keepdims=True) + EPS)
    gd = dy * g
    c = jnp.mean(gd * xs, axis=0, keepdims=True)
    dx = r * gd - xs * (r * r * r * c)
    dg = jnp.sum(dy * (xs * r), axis=1, keepdims=True)
    return dx, dg


GROUP_ROWS = (SWA_Q_HEADS * HEAD_DIM, FOX_HEADS * HEAD_DIM, MLA_HEADS * HEAD_DIM)


def _group_specs(S, tn):
    outs = [pl.BlockSpec((n, tn), lambda i: (0, i)) for n in GROUP_ROWS]
    g = pl.BlockSpec((D_MODEL, 1), lambda i: (0, 0))
    mixed = pl.BlockSpec((D_MODEL, tn), lambda i: (0, i))
    return outs, g, mixed


def _group_norm_fwd(oa, of, oc, g, *, name):
    S = oa.shape[1]
    tn = _tile(S, 512)
    outs, gs, mixed = _group_specs(S, tn)

    def body(a_ref, f_ref, c_ref, g_ref, o_ref):
        r0 = 0
        for ref, n in zip((a_ref, f_ref, c_ref), GROUP_ROWS):
            o_ref[r0:r0 + n, :] = _col_rms(ref[...], g_ref[r0:r0 + n, :]).astype(BF16)
            r0 += n

    return pl.pallas_call(
        body, name=name, grid=(S // tn,),
        in_specs=outs + [gs], out_specs=mixed,
        out_shape=jax.ShapeDtypeStruct((D_MODEL, S), BF16),
        compiler_params=_params("parallel"),
    )(oa, of, oc, g)


def _group_norm_bwd(oa, of, oc, g, dmixed, *, name):
    S = oa.shape[1]
    tn = _tile(S, 512)
    outs, gs, mixed = _group_specs(S, tn)
    n_heads = D_MODEL // HEAD_DIM

    def body(a_ref, f_ref, c_ref, g_ref, dm_ref, da_ref, df_ref, dc_ref, dg_ref, dl_ref):
        @pl.when(pl.program_id(0) == 0)
        def _():
            dg_ref[...] = jnp.zeros_like(dg_ref)

        r0 = 0
        for ref, dref, n in zip((a_ref, f_ref, c_ref), (da_ref, df_ref, dc_ref), GROUP_ROWS):
            o = ref[...]
            dx, dg = _col_rms_bwd(o, g_ref[r0:r0 + n, :], dm_ref[r0:r0 + n, :])
            dxb = dx.astype(BF16)
            dref[...] = dxb
            dg_ref[r0:r0 + n, :] += dg
            od = o * dxb.astype(F32)
            for h in range(n // HEAD_DIM):
                dl_ref[r0 // HEAD_DIM + h] = jnp.sum(od[h * HEAD_DIM:(h + 1) * HEAD_DIM, :], axis=0, keepdims=True)
            r0 += n

    return pl.pallas_call(
        body, name=name, grid=(S // tn,),
        in_specs=outs + [gs, mixed], out_specs=outs + [gs, pl.BlockSpec((n_heads, 1, tn), lambda i: (0, 0, i))],
        out_shape=[jax.ShapeDtypeStruct((n, S), BF16) for n in GROUP_ROWS] + [jax.ShapeDtypeStruct((D_MODEL, 1), F32),
                                                                              jax.ShapeDtypeStruct((n_heads, 1, S), F32)],
        compiler_params=_params("arbitrary"),
    )(oa, of, oc, g, dmixed)


def _loss_head(y, target):
    S, W = y.shape
    tm = _tile(S, 512)

    def body(y_ref, t_ref, d_ref, l_ref):
        @pl.when(pl.program_id(0) == 0)
        def _():
            l_ref[...] = jnp.zeros_like(l_ref)

        err = y_ref[...] - t_ref[...]
        d_ref[...] = err * (1.0 / W)
        l_ref[...] += 0.5 * jnp.sum(jnp.mean(err * err, axis=-1, keepdims=True), axis=0, keepdims=True)

    row = pl.BlockSpec((tm, W), lambda i: (i, 0))
    d, l = pl.pallas_call(
        body, name="loss_head", grid=(S // tm,),
        in_specs=[row, row],
        out_specs=[row, pl.BlockSpec((1, 1), lambda i: (0, 0))],
        out_shape=[jax.ShapeDtypeStruct((S, W), F32), jax.ShapeDtypeStruct((1, 1), F32)],
        compiler_params=_params("arbitrary"),
    )(y, target)
    return l[0, 0], d


def _attn_fwd(q_src, k_src, v_src, rows, H, Dk, Dv, scale, f_row=None, f_col=None, *, name):
    S = q_src.shape[1]
    T = _tile(S, ATT_TILE)
    nq = S // T
    forget = f_row is not None
    qb, kb, vb = rows[0] // (H * Dk), rows[1] // (H * Dk), rows[2] // (H * Dv)
    hs = range(H)

    def body(*refs):
        if forget:
            q_ref, k_ref, v_ref, fq_ref, fk_ref, o_ref, lse_ref = refs
        else:
            q_ref, k_ref, v_ref, o_ref, lse_ref = refs
        i = pl.program_id(0)

        def tile(j, masked, state):
            off = pl.multiple_of(j * T, T)
            ss = [_dot(k_ref[h * Dk:(h + 1) * Dk, pl.ds(off, T)], q_ref[h * Dk:(h + 1) * Dk, :], TN) * scale for h in hs]
            if forget:
                ss = [ss[h] + (fq_ref[h] - fk_ref[pl.ds(off, T), h:h + 1]) for h in hs]
            if masked:
                r = lax.broadcasted_iota(jnp.int32, (T, T), 0)
                c = lax.broadcasted_iota(jnp.int32, (T, T), 1)
                ss = [jnp.where(r <= c, s, NEG_INF) for s in ss]
            m_new = [jnp.maximum(state[h][0], jnp.max(ss[h], axis=0, keepdims=True)) for h in hs]
            alpha = [jnp.exp(state[h][0] - m_new[h]) for h in hs]
            ps = [jnp.exp(ss[h] - m_new[h]) for h in hs]
            l_new = [alpha[h] * state[h][1] + jnp.sum(ps[h], axis=0, keepdims=True) for h in hs]
            p_hi = [p.astype(BF16) for p in ps]
            vs = [v_ref[h * Dv:(h + 1) * Dv, pl.ds(off, T)] for h in hs]
            pv = [_dot(vs[h], p_hi[h]) for h in hs]
            if forget:
                pv = [pv[h] + _dot(vs[h], (ps[h] - p_hi[h].astype(F32)).astype(BF16)) for h in hs]
            return tuple((m_new[h], l_new[h], alpha[h] * state[h][2] + pv[h]) for h in hs)

        init = tuple((jnp.full((1, T), NEG_INF, F32), jnp.zeros((1, T), F32), jnp.zeros((Dv, T), F32)) for _ in hs)
        state = lax.fori_loop(0, i, lambda j, st: tile(j, False, st), init)
        state = tile(i, True, state)
        for h in hs:
            m, l, acc = state[h]
            o_ref[h * Dv:(h + 1) * Dv, :] = acc / l
            lse_ref[h] = m + jnp.log(l)

    in_specs = [pl.BlockSpec((H * Dk, T), lambda i: (qb, i)),
                pl.BlockSpec((H * Dk, S), lambda i: (kb, 0)),
                pl.BlockSpec((H * Dv, S), lambda i: (vb, 0))]
    ins = [q_src, k_src, v_src]
    if forget:
        in_specs += [pl.BlockSpec((H, 1, T), lambda i: (0, 0, i)), pl.BlockSpec((S, H), lambda i: (0, 0))]
        ins += [f_row, f_col]
    return pl.pallas_call(
        body, name=name, grid=(nq,),
        in_specs=in_specs,
        out_specs=[pl.BlockSpec((H * Dv, T), lambda i: (0, i)), pl.BlockSpec((H, 1, T), lambda i: (0, 0, i))],
        out_shape=[jax.ShapeDtypeStruct((H * Dv, S), F32), jax.ShapeDtypeStruct((H, 1, S), F32)],
        compiler_params=_params("parallel"),
    )(*ins)


def _attn_bwd(q_src, k_src, v_src, rows, H, Dk, Dv, scale, do, lse, delta, f_row=None, f_col=None, *, name):
    S = q_src.shape[1]
    T = _tile(S, ATT_TILE)
    nq = S // T
    forget = f_row is not None
    qb, kb, vb, db = rows[0] // (H * Dk), rows[1] // (H * Dk), rows[2] // (H * Dv), rows[3] // H
    hs = range(H)

    def body(*refs):
        if forget:
            (q_ref, k_ref, v_ref, do_ref, lse_ref, dl_ref, fq_ref, fk_ref,
             dq_ref, dk_ref, dv_ref, df_ref, dk_s, dv_s, df_s) = refs
        else:
            q_ref, k_ref, v_ref, do_ref, lse_ref, dl_ref, dq_ref, dk_ref, dv_ref, dk_s, dv_s = refs
        j = pl.program_id(0)

        @pl.when(j == 0)
        def _():
            dq_ref[...] = jnp.zeros_like(dq_ref)

        dk_s[...] = jnp.zeros_like(dk_s)
        dv_s[...] = jnp.zeros_like(dv_s)
        if forget:
            df_s[...] = jnp.zeros_like(df_s)
        kt = [k_ref[h * Dk:(h + 1) * Dk, :] for h in hs]
        kj = [k.T for k in kt]
        vj = [v_ref[h * Dv:(h + 1) * Dv, :].T for h in hs]
        koff = pl.multiple_of(j * T, T)

        def tile(i, masked):
            cols = pl.ds(pl.multiple_of(i * T, T), T)
            qi = [q_ref[h * Dk:(h + 1) * Dk, cols] for h in hs]
            doi = [do_ref[h * Dv:(h + 1) * Dv, cols] for h in hs]
            st = [_dot(kj[h], qi[h]) * scale for h in hs]
            if forget:
                st = [st[h] + (fq_ref[h, :, cols] - fk_ref[pl.ds(koff, T), h:h + 1]) for h in hs]
            if masked:
                r = lax.broadcasted_iota(jnp.int32, (T, T), 0)
                c = lax.broadcasted_iota(jnp.int32, (T, T), 1)
                st = [jnp.where(r <= c, x, NEG_INF) for x in st]
            pt = [jnp.exp(st[h] - lse_ref[h, :, cols]) for h in hs]
            dpt = [_dot(vj[h], doi[h]) for h in hs]
            dst = [pt[h] * (dpt[h] - dl_ref[h, :, cols]) for h in hs]
            ptb = [p.astype(BF16) for p in pt]
            dsb = [d.astype(BF16) for d in dst]
            for h in hs:
                dv_s[h * Dv:(h + 1) * Dv, :] += _dot(doi[h], ptb[h], NT)
            for h in hs:
                dk_s[h * Dk:(h + 1) * Dk, :] += _dot(qi[h], dsb[h], NT)
            for h in hs:
                dq_ref[h * Dk:(h + 1) * Dk, cols] += _dot(kt[h], dsb[h]) * scale
            if forget:
                for h in hs:
                    part = dst[h][:, 0:LANES]
                    for c0 in range(LANES, T, LANES):
                        part = part + dst[h][:, c0:c0 + LANES]
                    df_s[h] += part

        tile(j, True)

        def loop_body(i, carry):
            tile(i, False)
            return carry

        lax.fori_loop(j + 1, nq, loop_body, 0)
        dk_ref[...] = dk_s[...] * scale
        dv_ref[...] = dv_s[...]
        if forget:
            df_ref[...] = jnp.concatenate([-jnp.sum(df_s[h], axis=-1, keepdims=True) for h in hs], axis=1)

    res = lambda D, b0: pl.BlockSpec((H * D, S), lambda j: (b0, 0))
    blk = lambda D, b0: pl.BlockSpec((H * D, T), lambda j: (b0, j))
    row3 = lambda b0: pl.BlockSpec((H, 1, S), lambda j: (b0, 0, 0))
    in_specs = [res(Dk, qb), blk(Dk, kb), blk(Dv, vb), res(Dv, 0), row3(0), row3(db)]
    ins = [q_src, k_src, v_src, do, lse, delta]
    out_specs = [res(Dk, 0), blk(Dk, 0), blk(Dv, 0)]
    out_shape = [jax.ShapeDtypeStruct((H * Dk, S), F32), jax.ShapeDtypeStruct((H * Dk, S), F32),
                 jax.ShapeDtypeStruct((H * Dv, S), F32)]
    scratch = [pltpu.VMEM((H * Dk, T), F32), pltpu.VMEM((H * Dv, T), F32)]
    if forget:
        in_specs += [row3(0), pl.BlockSpec((S, H), lambda j: (0, 0))]
        ins += [f_row, f_col]
        out_specs.append(pl.BlockSpec((T, H), lambda j: (j, 0)))
        out_shape.append(jax.ShapeDtypeStruct((S, H), F32))
        scratch.append(pltpu.VMEM((H, T, min(T, LANES)), F32))
    return pl.pallas_call(
        body, name=name, grid=(nq,),
        in_specs=in_specs, out_specs=out_specs, out_shape=out_shape, scratch_shapes=scratch,
        compiler_params=_params("arbitrary"),
    )(*ins)


GW = SWA_GROUP * WINDOW


def _swa_masks(i):
    r = lax.broadcasted_iota(jnp.int32, (WINDOW, GW), 0)
    c = lax.broadcasted_iota(jnp.int32, (WINDOW, GW), 1) % WINDOW
    return (r > c) & (i > 0), r <= c


def _swa_specs():
    W = WINDOW
    kv_rows = SWA_KV_HEADS * HEAD_DIM
    q = pl.BlockSpec((SWA_Q_HEADS * HEAD_DIM, W), lambda i: (0, i))
    prev = lambda b: pl.BlockSpec((kv_rows, W), lambda i: (b, jnp.maximum(i - 1, 0)))
    cur = lambda b: pl.BlockSpec((kv_rows, W), lambda i: (b, i))
    bias = pl.BlockSpec((SWA_KV_HEADS, 2 * W, GW), lambda i: (0, 0, 0))
    stat = pl.BlockSpec((SWA_Q_HEADS, W), lambda i: (0, i))
    sink = pl.BlockSpec(memory_space=pltpu.SMEM)
    return q, prev(4), cur(4), prev(5), cur(5), bias, stat, sink


def _group_lanes(ref, g, rows_per_head):
    h0 = g * SWA_GROUP
    return jnp.concatenate([ref[(h0 + j) * rows_per_head:(h0 + j + 1) * rows_per_head, :] for j in range(SWA_GROUP)], axis=1)


def _swa_scores(g, q_ref, kp_ref, kc_ref, b_ref, masks):
    rows = slice(g * HEAD_DIM, (g + 1) * HEAD_DIM)
    qg = _group_lanes(q_ref, g, HEAD_DIM)
    scale = HEAD_DIM ** -0.5
    s_p = jnp.where(masks[0], _dot(kp_ref[rows, :], qg, TN) * scale + b_ref[g, 0:WINDOW, :], NEG_INF)
    s_c = jnp.where(masks[1], _dot(kc_ref[rows, :], qg, TN) * scale + b_ref[g, WINDOW:2 * WINDOW, :], NEG_INF)
    return qg, rows, s_p, s_c


def _sink_row(sink_ref, g):
    return jnp.concatenate([jnp.full((1, WINDOW), sink_ref[g * SWA_GROUP + j], F32) for j in range(SWA_GROUP)], axis=1)


def _swa_fwd(qkv, bias_g, sinks, *, name):
    S = qkv.shape[1]
    qs, kp, kc, vp, vc, bs, stat, sk = _swa_specs()
    gs = range(SWA_KV_HEADS)

    def body(sink_ref, q_ref, kp_ref, kc_ref, vp_ref, vc_ref, b_ref, o_ref, lse_ref):
        masks = _swa_masks(pl.program_id(0))
        sc = [_swa_scores(g, q_ref, kp_ref, kc_ref, b_ref, masks) for g in gs]
        sinks_g = [_sink_row(sink_ref, g) for g in gs]
        m = [jnp.maximum(jnp.maximum(jnp.max(sc[g][2], axis=0, keepdims=True), jnp.max(sc[g][3], axis=0, keepdims=True)),
                         sinks_g[g]) for g in gs]
        p_p = [jnp.exp(sc[g][2] - m[g]) for g in gs]
        p_c = [jnp.exp(sc[g][3] - m[g]) for g in gs]
        l = [jnp.sum(p_p[g], axis=0, keepdims=True) + jnp.sum(p_c[g], axis=0, keepdims=True) + jnp.exp(sinks_g[g] - m[g])
             for g in gs]
        o = [_dot(vp_ref[sc[g][1], :], p_p[g].astype(BF16)) + _dot(vc_ref[sc[g][1], :], p_c[g].astype(BF16)) for g in gs]
        for g in gs:
            og = o[g] / l[g]
            lse = m[g] + jnp.log(l[g])
            for j in range(SWA_GROUP):
                h = g * SWA_GROUP + j
                o_ref[h * HEAD_DIM:(h + 1) * HEAD_DIM, :] = og[:, j * WINDOW:(j + 1) * WINDOW]
                lse_ref[h:h + 1, :] = lse[:, j * WINDOW:(j + 1) * WINDOW]

    return pl.pallas_call(
        body, name=name, grid=(S // WINDOW,),
        in_specs=[sk, qs, kp, kc, vp, vc, bs],
        out_specs=[qs, stat],
        out_shape=[jax.ShapeDtypeStruct((SWA_Q_HEADS * HEAD_DIM, S), F32), jax.ShapeDtypeStruct((SWA_Q_HEADS, S), F32)],
        compiler_params=_params("parallel"),
    )(sinks, qkv, qkv, qkv, qkv, qkv, bias_g)


def _swa_bwd(qkv, bias_g, sinks, do, lse, delta, *, name):
    S = qkv.shape[1]
    W = WINDOW
    qs, kp, kc, vp, vc, bs, stat, sk = _swa_specs()
    scale = HEAD_DIM ** -0.5
    kv_rows = SWA_KV_HEADS * HEAD_DIM
    gs = range(SWA_KV_HEADS)

    def body(sink_ref, q_ref, kp_ref, kc_ref, vp_ref, vc_ref, b_ref, do_ref, lse_ref, dl_ref,
             dq_ref, dkv_ref, db_ref, dsk_ref):
        i = pl.program_id(0)

        @pl.when(i == 0)
        def _():
            dkv_ref[...] = jnp.zeros_like(dkv_ref)
            db_ref[...] = jnp.zeros_like(db_ref)
            dsk_ref[...] = jnp.zeros_like(dsk_ref)

        masks = _swa_masks(i)
        prev = pl.ds(pl.multiple_of(jnp.maximum(i - 1, 0) * W, W), W)
        cur = pl.ds(pl.multiple_of(i * W, W), W)
        sc = [_swa_scores(g, q_ref, kp_ref, kc_ref, b_ref, masks) for g in gs]
        dog = [_group_lanes(do_ref, g, HEAD_DIM) for g in gs]
        lse = [_group_lanes(lse_ref, g, 1) for g in gs]
        dl = [_group_lanes(dl_ref, g, 1) for g in gs]
        p_p = [jnp.exp(sc[g][2] - lse[g]) for g in gs]
        p_c = [jnp.exp(sc[g][3] - lse[g]) for g in gs]
        ds_p = [p_p[g] * (_dot(vp_ref[sc[g][1], :], dog[g], TN) - dl[g]) for g in gs]
        ds_c = [p_c[g] * (_dot(vc_ref[sc[g][1], :], dog[g], TN) - dl[g]) for g in gs]
        for g in gs:
            db_ref[g, 0:W, :] += ds_p[g]
            db_ref[g, W:2 * W, :] += ds_c[g]
            dsk = jnp.exp(_sink_row(sink_ref, g) - lse[g]) * dl[g]
            for j in range(SWA_GROUP):
                h = g * SWA_GROUP + j
                dsk_ref[h:h + 1, :] -= jnp.broadcast_to(jnp.sum(dsk[:, j * W:(j + 1) * W], axis=1, keepdims=True), (1, LANES))
        dsb_p = [d.astype(BF16) for d in ds_p]
        dsb_c = [d.astype(BF16) for d in ds_c]
        for g in gs:
            rows = sc[g][1]
            dq = (_dot(kp_ref[rows, :], dsb_p[g]) + _dot(kc_ref[rows, :], dsb_c[g])) * scale
            for j in range(SWA_GROUP):
                h = g * SWA_GROUP + j
                dq_ref[h * HEAD_DIM:(h + 1) * HEAD_DIM, :] = dq[:, j * W:(j + 1) * W]
        for g in gs:
            rows = sc[g][1]
            vrows = slice(kv_rows + rows.start, kv_rows + rows.stop)
            dkv_ref[rows, prev] += _dot(sc[g][0], dsb_p[g], NT) * scale
            dkv_ref[rows, cur] += _dot(sc[g][0], dsb_c[g], NT) * scale
            dkv_ref[vrows, prev] += _dot(dog[g], p_p[g].astype(BF16), NT)
            dkv_ref[vrows, cur] += _dot(dog[g], p_c[g].astype(BF16), NT)

    return pl.pallas_call(
        body, name=name, grid=(S // W,),
        in_specs=[sk, qs, kp, kc, vp, vc, bs, qs, stat, stat],
        out_specs=[qs, pl.BlockSpec((2 * kv_rows, S), lambda i: (0, 0)), bs, pl.BlockSpec((SWA_Q_HEADS, LANES), lambda i: (0, 0))],
        out_shape=[jax.ShapeDtypeStruct((SWA_Q_HEADS * HEAD_DIM, S), F32), jax.ShapeDtypeStruct((2 * kv_rows, S), F32),
                   jax.ShapeDtypeStruct((SWA_KV_HEADS, 2 * W, GW), F32), jax.ShapeDtypeStruct((SWA_Q_HEADS, LANES), F32)],
        compiler_params=_params("arbitrary"),
    )(sinks, qkv, qkv, qkv, qkv, qkv, bias_g, do, lse, delta)


def _rel_onehot_t():
    qi = jnp.arange(WINDOW, dtype=jnp.int32)[None, :] + WINDOW
    kj = jnp.arange(2 * WINDOW, dtype=jnp.int32)[:, None]
    dist = qi - kj
    max_exact = REL_BUCKETS // 2
    d = jnp.maximum(dist, 0)
    log_ratio = jnp.log(jnp.maximum(d, 1).astype(F32) / max_exact) / math.log(REL_MAX_DIST / max_exact)
    large = jnp.minimum(max_exact + (log_ratio * (REL_BUCKETS - max_exact)).astype(jnp.int32), REL_BUCKETS - 1)
    bucket = jnp.where(d < max_exact, d, large).reshape(-1)
    return (bucket[None, :] == jnp.arange(REL_BUCKETS, dtype=jnp.int32)[:, None]).astype(BF16)


def _bias_table(rel_bias_t, onehot_t):
    Hq, NB = rel_bias_t.shape
    N = onehot_t.shape[1]
    tn = _tile(N, 4096)

    def body(r_ref, oh_ref, o_ref):
        oh = oh_ref[...]
        a1, a2, a3 = _split3(r_ref[...])
        o_ref[...] = _dot(a1, oh) + _dot(a2, oh) + _dot(a3, oh)

    return pl.pallas_call(
        body, name="rel_bias_table", grid=(N // tn,),
        in_specs=[pl.BlockSpec((Hq, NB), lambda j: (0, 0)), pl.BlockSpec((NB, tn), lambda j: (0, j))],
        out_specs=pl.BlockSpec((Hq, tn), lambda j: (0, j)),
        out_shape=jax.ShapeDtypeStruct((Hq, N), F32),
        compiler_params=_params("parallel"),
    )(rel_bias_t, onehot_t)


def _bias_table_bwd(dbias, onehot_t):
    L, Hq, N = dbias.shape
    NB = onehot_t.shape[0]
    tn = _tile(N, 4096)

    def body(d_ref, oh_ref, o_ref):
        @pl.when(pl.program_id(0) == 0)
        def _():
            o_ref[...] = jnp.zeros_like(o_ref)

        d = d_ref[0]
        for l in range(1, L):
            d = d + d_ref[l]
        oh = oh_ref[...]
        a1, a2, a3 = _split3(d)
        o_ref[...] += _dot(a1, oh, NT) + _dot(a2, oh, NT) + _dot(a3, oh, NT)

    return pl.pallas_call(
        body, name="rel_bias_bwd", grid=(N // tn,),
        in_specs=[pl.BlockSpec((L, Hq, tn), lambda j: (0, 0, j)), pl.BlockSpec((NB, tn), lambda j: (0, j))],
        out_specs=pl.BlockSpec((Hq, NB), lambda j: (0, 0)),
        out_shape=jax.ShapeDtypeStruct((Hq, NB), F32),
        compiler_params=_params("arbitrary"),
    )(dbias, onehot_t)


def _gate_fwd(lat, fb_col, *, name):
    S = lat.shape[1]
    tn = _tile(S, 256)

    def body(z_ref, fb_ref, o_ref, carry):
        @pl.when(pl.program_id(0) == 0)
        def _():
            carry[...] = jnp.zeros_like(carry)

        z = z_ref[...] + fb_ref[...]
        lf = jnp.minimum(z, 0.0) - jnp.log1p(jnp.exp(-jnp.abs(z)))
        r = lax.broadcasted_iota(jnp.int32, (tn, tn), 0)
        c = lax.broadcasted_iota(jnp.int32, (tn, tn), 1)
        tri = (r <= c).astype(BF16)
        a1, a2, a3 = _split3(lf)
        cum = _dot(a1, tri) + _dot(a2, tri) + _dot(a3, tri) + carry[:, 0:1]
        o_ref[...] = cum
        carry[...] = jnp.broadcast_to(cum[:, tn - 1:tn], carry.shape)

    return pl.pallas_call(
        body, name=name, grid=(S // tn,),
        in_specs=[pl.BlockSpec((GATE_ROWS, tn), lambda i: (0, i)), pl.BlockSpec((GATE_ROWS, 1), lambda i: (0, 0))],
        out_specs=pl.BlockSpec((GATE_ROWS, tn), lambda i: (0, i)),
        out_shape=jax.ShapeDtypeStruct((GATE_ROWS, S), F32),
        scratch_shapes=[pltpu.VMEM((GATE_ROWS, LANES), F32)],
        compiler_params=_params("arbitrary"),
    )(lat, fb_col)


def _gate_bwd(lat, fb_col, dF, *, name):
    S = lat.shape[1]
    tn = _tile(S, 256)
    nt = S // tn

    def body(z_ref, fb_ref, df_ref, dz_ref, dfb_ref, carry):
        @pl.when(pl.program_id(0) == 0)
        def _():
            carry[...] = jnp.zeros_like(carry)
            dfb_ref[...] = jnp.zeros_like(dfb_ref)

        r = lax.broadcasted_iota(jnp.int32, (tn, tn), 0)
        c = lax.broadcasted_iota(jnp.int32, (tn, tn), 1)
        tri = (r >= c).astype(BF16)
        a1, a2, a3 = _split3(df_ref[...])
        dlf = _dot(a1, tri) + _dot(a2, tri) + _dot(a3, tri) + carry[:, 0:1]
        carry[...] = jnp.broadcast_to(dlf[:, 0:1], carry.shape)
        z = z_ref[...] + fb_ref[...]
        row = lax.broadcasted_iota(jnp.int32, (GATE_ROWS, tn), 0)
        dz = jnp.where(row < FOX_HEADS, dlf / (1.0 + jnp.exp(z)), 0.0)
        dz_ref[...] = dz
        dfb_ref[...] += jnp.sum(dz, axis=1, keepdims=True)

    blk = pl.BlockSpec((GATE_ROWS, tn), lambda i: (0, nt - 1 - i))
    vec = pl.BlockSpec((GATE_ROWS, 1), lambda i: (0, 0))
    return pl.pallas_call(
        body, name=name, grid=(nt,),
        in_specs=[blk, vec, blk], out_specs=[blk, vec],
        out_shape=[jax.ShapeDtypeStruct((GATE_ROWS, S), F32), jax.ShapeDtypeStruct((GATE_ROWS, 1), F32)],
        scratch_shapes=[pltpu.VMEM((GATE_ROWS, LANES), F32)],
        compiler_params=_params("arbitrary"),
    )(lat, fb_col, dF)


def _rope_tables(S):
    pos = jnp.arange(S, dtype=F32)
    inv_freq = ROPE_THETA ** (-(jnp.arange(MLA_ROPE // 2, dtype=F32) * 2.0 / MLA_ROPE))
    ang = pos[:, None] * inv_freq[None, :]
    cos, sin = jnp.cos(ang).T, jnp.sin(ang).T
    z16 = jnp.zeros_like(cos)

    def slab(lo, fill):
        def put(first, second, f):
            return jnp.concatenate([jnp.full((lo, S), f, F32), first, second, jnp.full((LANES - lo - MLA_ROPE, S), f, F32)], axis=0)
        return put(cos, cos, fill), put(-sin, z16, 0.0), put(z16, sin, 0.0)

    tq = tuple(jnp.tile(t, (MLA_HEADS, 1)) for t in slab(MLA_NOPE, 1.0))
    return tq, slab(0, 0.0)


def _rope(x, c, s1, s2):
    n = x.shape[0]
    half = MLA_ROPE // 2
    return x * c + pltpu.roll(x, n - half, 0) * s1 + pltpu.roll(x, half, 0) * s2


def _rope_t(dy, c, s1, s2):
    n = dy.shape[0]
    half = MLA_ROPE // 2
    return dy * c + pltpu.roll(dy * s1, half, 0) + pltpu.roll(dy * s2, n - half, 0)


KR_SLAB0 = MLA_Q_RANK + MLA_KV_RANK


def _mla_prep_fwd(lat, g_q, g_kv, w_uq_t, w_ukv_t, tq, tmisc, *, name):
    S = lat.shape[1]
    tn = _tile(S, 512)
    QW = MLA_HEADS * MLA_PAD

    def body(lat_ref, gq_ref, gkv_ref, wq_ref, wkv_ref, c_ref, s1_ref, s2_ref, cm_ref, s1m_ref, s2m_ref,
             nq_ref, nkv_ref, q_ref, k_ref, v_ref):
        x = pltpu.roll(lat_ref[...], LAT_ROWS - LAT_SHIFT, 0)
        nq = _col_rms(x[0:MLA_Q_RANK, :], gq_ref[...]).astype(BF16)
        nkv = _col_rms(x[MLA_Q_RANK:KR_SLAB0, :], gkv_ref[...]).astype(BF16)
        nq_ref[...] = nq
        nkv_ref[...] = nkv
        q_ref[...] = _rope(_dot(wq_ref[...], nq), c_ref[...], s1_ref[...], s2_ref[...]).astype(BF16)
        kv = _dot(wkv_ref[...], nkv).astype(BF16)
        kr = _rope(x[KR_SLAB0:LAT_ROWS, :], cm_ref[...], s1m_ref[...], s2m_ref[...]).astype(BF16)
        for h in range(MLA_HEADS):
            k_ref[h * MLA_PAD:h * MLA_PAD + MLA_NOPE, :] = kv[h * LANES:h * LANES + MLA_NOPE, :]
            k_ref[h * MLA_PAD + MLA_NOPE:(h + 1) * MLA_PAD, :] = kr[0:MLA_PAD - MLA_NOPE, :]
            v_ref[h * HEAD_DIM:(h + 1) * HEAD_DIM, :] = kv[h * LANES + MLA_NOPE:(h + 1) * LANES, :]

    def col(rows):
        return pl.BlockSpec((rows, tn), lambda i: (0, i))

    def full(a):
        return pl.BlockSpec(a.shape, lambda i: (0, 0))

    return pl.pallas_call(
        body, name=name, grid=(S // tn,),
        in_specs=[col(LAT_ROWS), full(g_q), full(g_kv), full(w_uq_t), full(w_ukv_t),
                  col(QW), col(QW), col(QW), col(LANES), col(LANES), col(LANES)],
        out_specs=[col(MLA_Q_RANK), col(MLA_KV_RANK), col(QW), col(QW), col(MLA_HEADS * HEAD_DIM)],
        out_shape=[jax.ShapeDtypeStruct((MLA_Q_RANK, S), BF16), jax.ShapeDtypeStruct((MLA_KV_RANK, S), BF16),
                   jax.ShapeDtypeStruct((QW, S), BF16), jax.ShapeDtypeStruct((QW, S), BF16),
                   jax.ShapeDtypeStruct((MLA_HEADS * HEAD_DIM, S), BF16)],
        compiler_params=_params("parallel"),
    )(lat, g_q, g_kv, w_uq_t, w_ukv_t, *tq, *tmisc)


def _mla_prep_bwd(lat, nq, nkv, g_q, g_kv, w_uq_p, w_ukv, tq, tmisc, dq, dk, dv, dflog, *, name):
    S = lat.shape[1]
    tn = _tile(S, 512)
    QW = MLA_HEADS * MLA_PAD

    def body(lat_ref, nq_ref, nkv_ref, gq_ref, gkv_ref, wq_ref, wkv_ref, c_ref, s1_ref, s2_ref,
             cm_ref, s1m_ref, s2m_ref, dq_ref, dk_ref, dv_ref, dfl_ref,
             dlat_ref, dwq_ref, dwkv_ref, dgq_ref, dgkv_ref, y_s):
        @pl.when(pl.program_id(0) == 0)
        def _():
            dwq_ref[...] = jnp.zeros_like(dwq_ref)
            dwkv_ref[...] = jnp.zeros_like(dwkv_ref)
            dgq_ref[...] = jnp.zeros_like(dgq_ref)
            dgkv_ref[...] = jnp.zeros_like(dgkv_ref)

        x = pltpu.roll(lat_ref[...], LAT_ROWS - LAT_SHIFT, 0)
        dqm = _rope_t(dq_ref[...], c_ref[...], s1_ref[...], s2_ref[...]).astype(BF16)
        dwq_ref[...] += _dot(dqm, nq_ref[...], NT)
        dx, dg = _col_rms_bwd(x[0:MLA_Q_RANK, :], gq_ref[...], _dot(wq_ref[...], dqm))
        y_s[0:MLA_Q_RANK, :] = dx
        dgq_ref[...] += dg
        dkv = jnp.concatenate(
            [part for h in range(MLA_HEADS)
             for part in (dk_ref[h * MLA_PAD:h * MLA_PAD + MLA_NOPE, :], dv_ref[h * HEAD_DIM:(h + 1) * HEAD_DIM, :])],
            axis=0).astype(BF16)
        dwkv_ref[...] += _dot(dkv, nkv_ref[...], NT)
        dx, dg = _col_rms_bwd(x[MLA_Q_RANK:KR_SLAB0, :], gkv_ref[...], _dot(wkv_ref[...], dkv))
        y_s[MLA_Q_RANK:KR_SLAB0, :] = dx
        dgkv_ref[...] += dg
        dkr = dk_ref[MLA_NOPE:MLA_PAD, :]
        for h in range(1, MLA_HEADS):
            dkr = dkr + dk_ref[h * MLA_PAD + MLA_NOPE:(h + 1) * MLA_PAD, :]
        dkr = jnp.concatenate([dkr, jnp.zeros((MLA_NOPE, tn), F32)], axis=0)
        y_s[KR_SLAB0:LAT_ROWS, :] = _rope_t(dkr, cm_ref[...], s1m_ref[...], s2m_ref[...])
        y = pltpu.roll(y_s[...], LAT_SHIFT, 0)
        row = lax.broadcasted_iota(jnp.int32, (LAT_ROWS, tn), 0)
        dfl = jnp.concatenate([dfl_ref[...], jnp.zeros((LAT_ROWS - GATE_ROWS, tn), F32)], axis=0)
        dlat_ref[...] = jnp.where(row < LAT_SHIFT, dfl, y).astype(BF16)

    def col(rows):
        return pl.BlockSpec((rows, tn), lambda i: (0, i))

    def full(a):
        return pl.BlockSpec(a.shape, lambda i: (0, 0))

    def acc(r, c):
        return pl.BlockSpec((r, c), lambda i: (0, 0))

    return pl.pallas_call(
        body, name=name, grid=(S // tn,),
        in_specs=[col(LAT_ROWS), col(MLA_Q_RANK), col(MLA_KV_RANK), full(g_q), full(g_kv),
                  full(w_uq_p), full(w_ukv), col(QW), col(QW), col(QW), col(LANES), col(LANES), col(LANES),
                  col(QW), col(QW), col(MLA_HEADS * HEAD_DIM), col(GATE_ROWS)],
        out_specs=[col(LAT_ROWS), acc(QW, MLA_Q_RANK), acc(QW, MLA_KV_RANK), acc(MLA_Q_RANK, 1), acc(MLA_KV_RANK, 1)],
        out_shape=[jax.ShapeDtypeStruct((LAT_ROWS, S), BF16), jax.ShapeDtypeStruct((QW, MLA_Q_RANK), F32),
                   jax.ShapeDtypeStruct((QW, MLA_KV_RANK), F32), jax.ShapeDtypeStruct((MLA_Q_RANK, 1), F32),
                   jax.ShapeDtypeStruct((MLA_KV_RANK, 1), F32)],
        scratch_shapes=[pltpu.VMEM((LAT_ROWS, tn), F32)],
        compiler_params=_params("arbitrary"),
    )(lat, nq, nkv, g_q, g_kv, w_uq_p, w_ukv, *tq, *tmisc, dq, dk, dv, dflog)


def _dproj_cast(dqa, dkva, dqf, dkf, dvf, dlat, *, name):
    S = dqa.shape[1]
    tn = _tile(S, 512)
    parts = (dqa, dkva, dqf, dkf, dvf, dlat)

    def body(*refs):
        o_ref = refs[-1]
        r0 = 0
        for ref in refs[:-1]:
            n = ref.shape[0]
            o_ref[r0:r0 + n, :] = ref[...].astype(BF16)
            r0 += n

    return pl.pallas_call(
        body, name=name, grid=(S // tn,),
        in_specs=[pl.BlockSpec((p.shape[0], tn), lambda i: (0, i)) for p in parts],
        out_specs=pl.BlockSpec((IN_ROWS, tn), lambda i: (0, i)),
        out_shape=jax.ShapeDtypeStruct((IN_ROWS, S), BF16),
        compiler_params=_params("parallel"),
    )(*parts)


GELU_C = math.sqrt(2.0 / math.pi)
GELU_A = 0.044715


def _conv_taps(a, halo, w_ref, b_ref, first):
    row = lax.broadcasted_iota(jnp.int32, a.shape, 0)
    h7 = jnp.where(first, 0.0, halo[7:8, :])
    h6 = jnp.where(first, 0.0, halo[6:7, :])
    a1 = jnp.where(row == 0, h7, pltpu.roll(a, 1, 0))
    a2 = jnp.where(row == 0, h6, jnp.where(row == 1, h7, pltpu.roll(a, 2, 0)))
    u = ((b_ref[...] + w_ref[0:1, :] * a2) + w_ref[1:2, :] * a1) + w_ref[2:3, :] * a
    return u, a1, a2


def _conv_specs(S, tm, tc, nc):
    hb = tm // 8
    main = lambda off: pl.BlockSpec((tm, tc), lambda j, i: (i, j + off))
    halo = lambda off: pl.BlockSpec((8, tc), lambda j, i: (jnp.maximum(i * hb - 1, 0), j + off))
    wspec = lambda off: pl.BlockSpec((3, tc), lambda j, i: (0, j + off))
    bspec = lambda off: pl.BlockSpec((1, tc), lambda j, i: (0, j + off))
    return main, halo, wspec, bspec


def _conv_geglu_fwd(a, conv_w, conv_b, *, name):
    S = a.shape[0]
    tm, tc = _tile(S, 512), _tile(D_FF, 1408)
    nc = D_FF // tc
    main, halo, wspec, bspec = _conv_specs(S, tm, tc, nc)

    def body(ag_ref, au_ref, hg_ref, hu_ref, wg_ref, wu_ref, bg_ref, bu_ref, z_ref):
        first = pl.program_id(1) == 0
        gate, _, _ = _conv_taps(ag_ref[...], hg_ref[...], wg_ref, bg_ref, first)
        up, _, _ = _conv_taps(au_ref[...], hu_ref[...], wu_ref, bu_ref, first)
        cdf = 0.5 * (1.0 + jnp.tanh(GELU_C * (gate + GELU_A * (gate * gate * gate))))
        z_ref[...] = (gate * cdf * up).astype(BF16)

    return pl.pallas_call(
        body, name=name, grid=(nc, S // tm),
        in_specs=[main(0), main(nc), halo(0), halo(nc), wspec(0), wspec(nc), bspec(0), bspec(nc)],
        out_specs=pl.BlockSpec((tm, tc), lambda j, i: (i, j)),
        out_shape=jax.ShapeDtypeStruct((S, D_FF), BF16),
        compiler_params=_params("parallel", "arbitrary"),
    )(a, a, a, a, conv_w, conv_w, conv_b, conv_b)


def _conv_geglu_bwd(a, conv_w, conv_b, dz, *, name):
    S = a.shape[0]
    tm, tc = _tile(S, 512), _tile(D_FF, 1408)
    nc = D_FF // tc
    main, halo, wspec, bspec = _conv_specs(S, tm, tc, nc)

    def body(ag_ref, au_ref, hg_ref, hu_ref, wg_ref, wu_ref, bg_ref, bu_ref, dz_ref, du_ref, dw_ref, db_ref):
        first = pl.program_id(1) == 0

        @pl.when(first)
        def _():
            dw_ref[...] = jnp.zeros_like(dw_ref)
            db_ref[...] = jnp.zeros_like(db_ref)

        gate, g1, g2 = _conv_taps(ag_ref[...], hg_ref[...], wg_ref, bg_ref, first)
        up, u1, u2 = _conv_taps(au_ref[...], hu_ref[...], wu_ref, bu_ref, first)
        dz = dz_ref[...]
        g2x = gate * gate
        th = jnp.tanh(GELU_C * (gate + GELU_A * (g2x * gate)))
        cdf = 0.5 * (1.0 + th)
        dgelu = cdf + gate * (0.5 * (1.0 - th * th) * (GELU_C * (1.0 + 3.0 * GELU_A * g2x)))
        dug = dz * up * dgelu
        duu = dz * (gate * cdf)
        du_ref[0] = dug
        du_ref[1] = duu
        for half, du, taps in ((0, dug, (g2, g1, ag_ref[...])), (1, duu, (u2, u1, au_ref[...]))):
            for tap in range(3):
                dw_ref[half, tap:tap + 1, :] += jnp.sum(du * taps[tap], axis=0, keepdims=True)
            db_ref[half] += jnp.sum(du, axis=0, keepdims=True)

    return pl.pallas_call(
        body, name=name, grid=(nc, S // tm),
        in_specs=[main(0), main(nc), halo(0), halo(nc), wspec(0), wspec(nc), bspec(0), bspec(nc),
                  pl.BlockSpec((tm, tc), lambda j, i: (i, j))],
        out_specs=[pl.BlockSpec((2, tm, tc), lambda j, i: (0, i, j)), pl.BlockSpec((2, 3, tc), lambda j, i: (0, 0, j)),
                   pl.BlockSpec((2, 1, tc), lambda j, i: (0, 0, j))],
        out_shape=[jax.ShapeDtypeStruct((2, S, D_FF), F32), jax.ShapeDtypeStruct((2, 3, D_FF), F32),
                   jax.ShapeDtypeStruct((2, 1, D_FF), F32)],
        compiler_params=_params("parallel", "arbitrary"),
    )(a, a, a, a, conv_w, conv_w, conv_b, conv_b, dz)


def _conv_bwd_input(du, conv_w, *, name):
    S = du.shape[1]
    tm, tc = _tile(S, 512), _tile(D_FF, 1408)
    nc = D_FF // tc
    nr = S // tm
    hb = tm // 8

    def body(du_ref, nx_ref, w_ref, da_ref):
        last = pl.program_id(2) == nr - 1
        d = du_ref[0]
        row = lax.broadcasted_iota(jnp.int32, d.shape, 0)
        n0 = jnp.where(last, 0.0, nx_ref[0, 0:1, :])
        n1 = jnp.where(last, 0.0, nx_ref[0, 1:2, :])
        d1 = jnp.where(row == tm - 1, n0, pltpu.roll(d, tm - 1, 0))
        d2 = jnp.where(row == tm - 1, n1, jnp.where(row == tm - 2, n0, pltpu.roll(d, tm - 2, 0)))
        da_ref[...] = (w_ref[2:3, :] * d + w_ref[1:2, :] * d1 + w_ref[0:1, :] * d2).astype(BF16)

    return pl.pallas_call(
        body, name=name, grid=(2, nc, nr),
        in_specs=[pl.BlockSpec((1, tm, tc), lambda h, j, i: (h, i, j)),
                  pl.BlockSpec((1, 8, tc), lambda h, j, i: (h, jnp.minimum((i + 1) * hb, S // 8 - 1), j)),
                  pl.BlockSpec((3, tc), lambda h, j, i: (0, h * nc + j))],
        out_specs=pl.BlockSpec((tm, tc), lambda h, j, i: (i, h * nc + j)),
        out_shape=jax.ShapeDtypeStruct((S, 2 * D_FF), BF16),
        compiler_params=_params("parallel", "parallel", "arbitrary"),
    )(du, du, conv_w)


def _adamw_update(w, g, m, v):
    m = ADAM_B1 * m + (1.0 - ADAM_B1) * g
    v = ADAM_B2 * v + (1.0 - ADAM_B2) * jnp.square(g)
    m_hat = m / (1.0 - ADAM_B1 ** ADAM_STEP)
    v_hat = v / (1.0 - ADAM_B2 ** ADAM_STEP)
    return -ADAM_LR * (m_hat / (jnp.sqrt(v_hat) + ADAM_EPS) + ADAM_WD * w), m, v


def _adamw(w, g, m, v, *, name):
    L, A, B = w.shape
    ta = _tile(A, ROW_TILE)

    def body(w_ref, g_ref, m_ref, v_ref, d_ref, mo_ref, vo_ref):
        d_ref[...], mo_ref[...], vo_ref[...] = _adamw_update(w_ref[...], g_ref[...], m_ref[...], v_ref[...])

    blk = pl.BlockSpec((None, ta, B), lambda l, i: (l, i, 0))
    shp = jax.ShapeDtypeStruct((L, A, B), F32)
    return pl.pallas_call(
        body, name=name, grid=(L, A // ta),
        in_specs=[blk] * 4, out_specs=[blk] * 3, out_shape=[shp] * 3,
        compiler_params=_params("parallel", "parallel"),
    )(w, g, m, v)


def _scalar(v):
    return jnp.reshape(v, (1,)).astype(jnp.int32)


def _adamw_halves(w, g_mine, g_other, m, v, *, name):
    L, A, B = w.shape
    ta = _tile(A // 2, ROW_TILE)
    nb = A // 2 // ta

    def body(c_ref, w_ref, gm_ref, go_ref, m_ref, v_ref, g_ref, d_ref, mo_ref, vo_ref):
        g = jnp.where(pl.program_id(1) // nb == c_ref[0], gm_ref[...], go_ref[...])
        g_ref[...] = g
        d_ref[...], mo_ref[...], vo_ref[...] = _adamw_update(w_ref[...], g, m_ref[...], v_ref[...])

    blk = pl.BlockSpec((None, ta, B), lambda l, i, c_ref: (l, i, 0))
    half = pl.BlockSpec((None, ta, B), lambda l, i, c_ref: (l, i % nb, 0))
    shp = jax.ShapeDtypeStruct((L, A, B), F32)
    return pl.pallas_call(
        body, name=name,
        grid_spec=pltpu.PrefetchScalarGridSpec(num_scalar_prefetch=1, grid=(L, A // ta),
                                               in_specs=[blk, half, half, blk, blk], out_specs=[blk] * 4),
        out_shape=[shp] * 4,
        compiler_params=_params("parallel", "parallel"),
    )(_scalar(lax.axis_index("c")), w, g_mine, g_other, m, v)


def _chip_index():
    return 2 * lax.axis_index("x") + lax.axis_index("y")


def _pair_sum(g, recv, *, name):
    n, A, B = g.shape
    ta = _tile(A // 2, ROW_TILE)
    nb = A // 2 // ta

    def body(c_ref, g_ref, r_ref, o_ref):
        o_ref[...] = g_ref[...] + r_ref[...]

    return pl.pallas_call(
        body, name=name,
        grid_spec=pltpu.PrefetchScalarGridSpec(
            num_scalar_prefetch=1, grid=(n, nb),
            in_specs=[pl.BlockSpec((None, ta, B), lambda s, r, c_ref: (s, c_ref[0] * nb + r, 0)),
                      pl.BlockSpec((None, ta, B), lambda s, r, c_ref: (s, r, 0))],
            out_specs=pl.BlockSpec((None, ta, B), lambda s, r, c_ref: (s, r, 0))),
        out_shape=jax.ShapeDtypeStruct((n, A // 2, B), F32),
        compiler_params=_params("parallel", "parallel"),
    )(_scalar(lax.axis_index("c")), g, recv)


def _chip_sum(landed, own, *, name):
    n, A2, B = landed.shape
    ta = _tile(A2, ROW_TILE)

    def body(me_ref, *refs):
        slots, own_ref, o_ref = refs[:n], refs[n], refs[n + 1]
        parts = [jnp.where(me_ref[0] == s, own_ref[...], slots[s][...]) for s in range(n)]
        o_ref[...] = ((parts[0] + parts[1]) + parts[2]) + parts[3]

    def slot(s):
        return pl.BlockSpec((None, ta, B), lambda r, me_ref: (jnp.where(me_ref[0] == s, (s + 1) % n, s), r, 0))

    return pl.pallas_call(
        body, name=name,
        grid_spec=pltpu.PrefetchScalarGridSpec(
            num_scalar_prefetch=1, grid=(A2 // ta,),
            in_specs=[slot(s) for s in range(n)] + [pl.BlockSpec((None, ta, B), lambda r, me_ref: (me_ref[0], r, 0))],
            out_specs=pl.BlockSpec((ta, B), lambda r, me_ref: (r, 0))),
        out_shape=jax.ShapeDtypeStruct((A2, B), F32),
        compiler_params=_params("parallel"),
    )(_scalar(_chip_index()), *([landed] * n), own)


HBM_SPEC = pl.BlockSpec(memory_space=pl.ANY)
COMM_PARAMS = pltpu.CompilerParams(has_side_effects=True)


def _mesh_pos():
    return lax.axis_index("x"), lax.axis_index("y"), lax.axis_index("c")


def _other_chips(x, y):
    return [(1 - x, y), (x, 1 - y), (1 - x, 1 - y)]


def _remote(src, dst, send_sems, recv_sems, k, to):
    return pltpu.make_async_remote_copy(src_ref=src, dst_ref=dst, send_sem=send_sems.at[k], recv_sem=recv_sems.at[k],
                                        device_id=to, device_id_type=MESH)


def _place_own(gathered, shard, *, name):
    A, B = shard.shape
    ta = _tile(A, ROW_TILE)

    def body(me_ref, s_ref, g_ref, o_ref):
        o_ref[...] = s_ref[...]

    return pl.pallas_call(
        body, name=name,
        grid_spec=pltpu.PrefetchScalarGridSpec(
            num_scalar_prefetch=1, grid=(A // ta,),
            in_specs=[pl.BlockSpec((ta, B), lambda r, me_ref: (r, 0)), HBM_SPEC],
            out_specs=pl.BlockSpec((None, ta, B), lambda r, me_ref: (me_ref[0], r, 0))),
        out_shape=jax.ShapeDtypeStruct(gathered.shape, gathered.dtype),
        input_output_aliases={2: 0},
        compiler_params=_params("parallel"),
    )(_scalar(_chip_index()), shard, gathered)


def _half_rows(rows, c, align=8):
    assert (rows // 2) % align == 0
    return pl.ds(pl.multiple_of(c * (rows // 2), align), rows // 2)


BF16_ROWS = 16


def _halved(rows):
    return rows % (2 * BF16_ROWS) == 0


def _gather_copies(srcs, lands, send_sems, recv_sems):
    x, y, c = _mesh_pos()
    me = 2 * x + y
    out = []
    for k in range(len(srcs)):
        a = srcs[k].shape[0]
        rows = _half_rows(a, c, BF16_ROWS) if _halved(a) else pl.ds(0, a)
        for j, (px, py) in enumerate(_other_chips(x, y)):
            send = _remote(srcs[k].at[rows], lands[k].at[me, rows], send_sems, recv_sems, 3 * k + j, (px, py, c))
            recv = _remote(srcs[k].at[rows], lands[k].at[2 * px + py, rows], send_sems, recv_sems, 3 * k + j, (px, py, c))
            out.append((send, recv))
    return out


def _gather_start(srcs):
    nl, n = len(srcs), len(srcs[0])
    lands = [[lax.empty((N_CHIPS,) + s.shape, s.dtype) for s in sl] for sl in srcs]
    flat = [a for l in range(nl) for a in srcs[l] + lands[l]]

    def body(*refs):
        bufs, sems, token = refs[:len(flat)], refs[len(flat):len(flat) + 2 * nl], refs[-1]
        for l in range(nl):
            mine = bufs[2 * n * l:2 * n * (l + 1)]
            for send, _ in _gather_copies(mine[:n], mine[n:], sems[2 * l], sems[2 * l + 1]):
                send.start()
        token[...] = jnp.zeros_like(token)

    res = pl.pallas_call(
        body, name="weight_gather_start",
        in_specs=[HBM_ONLY] * len(flat),
        out_specs=[SEM_SPEC] * (2 * nl) + [HBM_ONLY] * len(flat) + [pl.BlockSpec(memory_space=pltpu.VMEM)],
        out_shape=[pltpu.SemaphoreType.DMA((3 * n,))] * (2 * nl) + [pltpu.HBM(a.shape, a.dtype) for a in flat]
        + [jax.ShapeDtypeStruct((8, LANES), F32)],
        input_output_aliases={i: 2 * nl + i for i in range(len(flat))},
        compiler_params=SPLIT_PARAMS,
    )(*[pltpu.with_memory_space_constraint(a, pltpu.HBM) for a in flat])
    bufs = res[2 * nl:2 * nl + len(flat)]
    state = [(res[2 * l], res[2 * l + 1], list(bufs[2 * n * l:2 * n * l + n]), list(bufs[2 * n * l + n:2 * n * (l + 1)]))
             for l in range(nl)]
    return state, res[-1][0:1, 0:1]


def _gather_wait(state, after, *, name):
    send_sems, recv_sems, srcs, lands = state
    n = len(srcs)

    def body(*refs):
        for send, recv in _gather_copies(refs[:n], refs[n:2 * n], refs[2 * n], refs[2 * n + 1]):
            send.wait_send()
            recv.wait_recv()

    res = pl.pallas_call(
        body, name=name,
        in_specs=[HBM_ONLY] * (2 * n) + [SEM_SPEC, SEM_SPEC, HBM_SPEC],
        out_specs=[HBM_ONLY] * (2 * n),
        out_shape=[pltpu.HBM(a.shape, a.dtype) for a in srcs + lands],
        input_output_aliases={i: i for i in range(2 * n)},
        compiler_params=SPLIT_PARAMS,
    )(*srcs, *lands, send_sems, recv_sems, after)
    return list(res[:n]), list(res[n:])


def _gather_forward(lands, *, name):
    n = len(lands)

    def body(*refs):
        bufs, outs = refs[:n], refs[n:2 * n]
        send_sems, recv_sems = refs[2 * n:]
        x, y, c = _mesh_pos()
        copies, waits = [], []
        for k in range(n):
            a = lands[k].shape[1]
            if not _halved(a):
                continue
            for j, (px, py) in enumerate(_other_chips(x, y)):
                mine = 2 * px + py, _half_rows(a, c, BF16_ROWS)
                copies.append(_remote(bufs[k].at[mine], outs[k].at[mine], send_sems, recv_sems, 3 * k + j, (x, y, 1 - c)))
                lands_here = outs[k].at[2 * px + py, _half_rows(a, 1 - c, BF16_ROWS)]
                waits.append(_remote(lands_here, lands_here, send_sems, recv_sems, 3 * k + j, (x, y, 1 - c)))
        for cp in copies:
            cp.start()
        for cp in waits:
            cp.wait_recv()
        for cp in copies:
            cp.wait_send()

    return pl.pallas_call(
        body, name=name,
        in_specs=[HBM_SPEC] * n, out_specs=[HBM_SPEC] * n,
        out_shape=[jax.ShapeDtypeStruct(a.shape, a.dtype) for a in lands],
        scratch_shapes=[pltpu.SemaphoreType.DMA((3 * n,)), pltpu.SemaphoreType.DMA((3 * n,))],
        input_output_aliases={i: i for i in range(n)},
        compiler_params=COMM_PARAMS,
    )(*lands)


def _sibling_exchange(gs, *, name):
    n = len(gs)

    def body(*refs):
        ins, outs = refs[:n], refs[n:2 * n]
        send_sems, recv_sems = refs[2 * n:]
        x, y, c = _mesh_pos()
        copies = [_remote(ins[k].at[:, _half_rows(gs[k].shape[1], 1 - c)], outs[k], send_sems, recv_sems, k, (x, y, 1 - c))
                  for k in range(n)]
        for cp in copies:
            cp.start()
        for cp in copies:
            cp.wait()

    return pl.pallas_call(
        body, name=name,
        in_specs=[HBM_SPEC] * n, out_specs=[HBM_SPEC] * n,
        out_shape=[jax.ShapeDtypeStruct((g.shape[0], g.shape[1] // 2, g.shape[2]), g.dtype) for g in gs],
        scratch_shapes=[pltpu.SemaphoreType.DMA((n,)), pltpu.SemaphoreType.DMA((n,))],
        compiler_params=COMM_PARAMS,
    )(*gs)


HBM_ONLY = pl.BlockSpec(memory_space=pltpu.HBM)
SEM_SPEC = pl.BlockSpec(memory_space=pltpu.SEMAPHORE)
SPLIT_PARAMS = pltpu.CompilerParams(has_side_effects=pltpu.SideEffectType.DATAFLOW_SIDE_EFFECTING)


def _scatter_copies(srcs, lands, send_sems, recv_sems):
    x, y, c = _mesh_pos()
    me = 2 * x + y
    out = []
    for k in range(len(srcs)):
        for j, (px, py) in enumerate(_other_chips(x, y)):
            s = 2 * px + py
            send = _remote(srcs[k].at[s], lands[k].at[me], send_sems, recv_sems, 3 * k + j, (px, py, c))
            recv = _remote(srcs[k].at[s], lands[k].at[s], send_sems, recv_sems, 3 * k + j, (px, py, c))
            out.append((send, recv))
    return out


def _scatter_start(ps, *, name):
    n = len(ps)
    lands = [lax.empty(p.shape, p.dtype) for p in ps]

    def body(*refs):
        srcs, zones = refs[:n], refs[n:2 * n]
        send_sems, recv_sems, token = refs[2 * n], refs[2 * n + 1], refs[-1]
        for send, _ in _scatter_copies(srcs, zones, send_sems, recv_sems):
            send.start()
        token[...] = jnp.zeros_like(token)

    hbm = lambda a: pltpu.HBM(a.shape, a.dtype)
    res = pl.pallas_call(
        body, name=name,
        in_specs=[HBM_ONLY] * (2 * n),
        out_specs=[SEM_SPEC, SEM_SPEC] + [HBM_ONLY] * (2 * n) + [pl.BlockSpec(memory_space=pltpu.VMEM)],
        out_shape=[pltpu.SemaphoreType.DMA((3 * n,)), pltpu.SemaphoreType.DMA((3 * n,))] + [hbm(a) for a in ps + lands]
        + [jax.ShapeDtypeStruct((8, LANES), F32)],
        input_output_aliases={i: 2 + i for i in range(2 * n)},
        compiler_params=SPLIT_PARAMS,
    )(*[pltpu.with_memory_space_constraint(a, pltpu.HBM) for a in ps + lands])
    return (res[0], res[1], list(res[2:2 + n]), list(res[2 + n:2 + 2 * n])), res[-1][0:1, 0:1]


def _scatter_wait(started, after, *, name):
    ng = len(started)
    sizes = [len(st[2]) for st in started]
    offs = [2 * sum(sizes[:i]) for i in range(ng + 1)]
    flat = [a for (_, _, ps, lands) in started for a in ps + lands]

    def body(*refs):
        bufs, sems = refs[:len(flat)], refs[len(flat):len(flat) + 2 * ng]
        for i, n in enumerate(sizes):
            srcs, zones = bufs[offs[i]:offs[i] + n], bufs[offs[i] + n:offs[i + 1]]
            for send, recv in _scatter_copies(srcs, zones, sems[2 * i], sems[2 * i + 1]):
                send.wait_send()
                recv.wait_recv()

    res = pl.pallas_call(
        body, name=name,
        in_specs=[HBM_ONLY] * len(flat) + [SEM_SPEC] * (2 * ng) + [HBM_SPEC],
        out_specs=[HBM_ONLY] * len(flat),
        out_shape=[pltpu.HBM(a.shape, a.dtype) for a in flat],
        input_output_aliases={i: i for i in range(len(flat))},
        compiler_params=SPLIT_PARAMS,
    )(*flat, *[s for (ss, rs, _, _) in started for s in (ss, rs)], after)
    return [(list(res[offs[i]:offs[i] + n]), list(res[offs[i] + n:offs[i + 1]])) for i, n in enumerate(sizes)]


def _sibling_share(hs):
    n = len(hs)

    def body(*refs):
        ins, outs = refs[:n], refs[n:2 * n]
        send_sems, recv_sems = refs[2 * n:]
        x, y, c = _mesh_pos()
        copies = [_remote(ins[k], outs[k], send_sems, recv_sems, k, (x, y, 1 - c)) for k in range(n)]
        for cp in copies:
            cp.start()
        for cp in copies:
            cp.wait()

    return pl.pallas_call(
        body, name="grad_sibling_share",
        in_specs=[HBM_SPEC] * n, out_specs=[HBM_SPEC] * n,
        out_shape=[jax.ShapeDtypeStruct(h.shape, h.dtype) for h in hs],
        scratch_shapes=[pltpu.SemaphoreType.DMA((n,)), pltpu.SemaphoreType.DMA((n,))],
        compiler_params=COMM_PARAMS,
    )(*hs)


def _allreduce_small(part):
    rows, C = part.shape

    def body(p_ref, o_ref, slots, send_sems, recv_sems):
        x, y, c = _mesh_pos()
        me = 4 * x + 2 * y + c
        slots[me] = p_ref[...]
        copies = []
        for k in range(1, 8):
            kx, ky, kc = (k >> 2) & 1, (k >> 1) & 1, k & 1
            peer = (x ^ kx if kx else x, y ^ ky if ky else y, c ^ kc if kc else c)
            cp = _remote(p_ref, slots.at[me], send_sems, recv_sems, k - 1, peer)
            cp.start()
            copies.append((cp, peer))
        for k, (cp, peer) in enumerate(copies):
            src = 4 * peer[0] + 2 * peer[1] + peer[2]
            _remote(p_ref, slots.at[src], send_sems, recv_sems, k, peer).wait_recv()
        for cp, _ in copies:
            cp.wait_send()
        total = slots[0]
        for d in range(1, 8):
            total = total + slots[d]
        o_ref[...] = total

    return pl.pallas_call(
        body, name="small_grad_allreduce",
        in_specs=[pl.BlockSpec(memory_space=pltpu.VMEM)], out_specs=pl.BlockSpec(memory_space=pltpu.VMEM),
        out_shape=jax.ShapeDtypeStruct((rows, C), F32),
        scratch_shapes=[pltpu.VMEM((8, rows, C), F32), pltpu.SemaphoreType.DMA((7,)), pltpu.SemaphoreType.DMA((7,))],
        compiler_params=pltpu.CompilerParams(has_side_effects=True, vmem_limit_bytes=VMEM_LIMIT_BYTES),
    )(part)


def _pad_w_uq(w):
    lead = w.shape[:-1]
    w = w.reshape(lead + (MLA_HEADS, MLA_QK))
    w = jnp.concatenate([w, jnp.zeros(lead + (MLA_HEADS, MLA_PAD - MLA_QK), w.dtype)], axis=-1)
    return w.reshape(lead + (MLA_HEADS * MLA_PAD,))


def _unpad_w_uq(g):
    lead = g.shape[:-1]
    return g.reshape(lead + (MLA_HEADS, MLA_PAD))[..., :MLA_QK].reshape(lead + (MLA_HEADS * MLA_QK,))


def _t(a):
    return jnp.swapaxes(a, -1, -2)


def _shards_of_cols(w):
    A, NB = w.shape
    return w.reshape(A, N_CHIPS, NB // N_CHIPS).transpose(1, 0, 2)


BIG = ("w_in", "w_uq", "w_ukv", "w_out", "w_up", "w_down")
SMALL = ("attn_pre_norm", "forget_bias", "swa_sinks", "rel_bias", "q_latent_norm", "kv_latent_norm", "group_norm",
         "attn_post_norm", "ffn_pre_norm", "conv_b", "ffn_post_norm")
WEIGHTS = ("attn_pre_norm", "w_in", "forget_bias", "swa_sinks", "rel_bias", "q_latent_norm", "w_uq", "kv_latent_norm",
           "w_ukv", "group_norm", "w_out", "attn_post_norm", "ffn_pre_norm", "w_up", "conv_w", "conv_b", "w_down",
           "ffn_post_norm")


def _pack(arrs, cols, row_mult):
    flat = jnp.concatenate([a.reshape(-1) for a in arrs])
    n = flat.shape[0]
    per = cols * row_mult
    total = -(-n // per) * per
    return jnp.pad(flat, (0, total - n)).reshape(total // cols, cols)


def _unpack(packed, shapes):
    flat = packed.reshape(-1)
    out, off = [], 0
    for shp in shapes:
        n = int(np.prod(shp))
        out.append(flat[off:off + n].reshape(shp))
        off += n
    return out


LAYER_KEYS = ("w_qkv_t", "w_lat_t", "w_in_t", "w_uq_p", "w_uq_t", "w_ukv", "w_ukv_t", "w_out", "w_up", "w_down", "conv_w")


def _layer_weights(gathered):
    cols = lambda g: g.transpose(1, 0, 2).reshape(g.shape[1], N_CHIPS * g.shape[2])
    w_in_t = _t(gathered["w_in"]).reshape(IN_COLS, D_MODEL)
    w_in_t = jnp.pad(w_in_t, ((0, IN_ROWS - IN_COLS), (0, 0)))
    w_uq_p = _pad_w_uq(cols(gathered["w_uq"]))
    w_ukv = cols(gathered["w_ukv"])
    return dict(w_qkv_t=w_in_t[:QKV_ROWS], w_lat_t=w_in_t[QKV_ROWS:], w_in_t=w_in_t, w_uq_p=w_uq_p, w_uq_t=_t(w_uq_p),
                w_ukv=w_ukv, w_ukv_t=_t(w_ukv), w_out=gathered["w_out"].reshape(D_MODEL, D_MODEL), w_up=gathered["w_up"],
                w_down=gathered["w_down"].reshape(D_FF, D_MODEL), conv_w=cols(gathered["conv_w"]))


def _local_step(x, target, W, layer_weights, layer_done):
    W = dict(W, **{key: [None] * DEPTH for key in LAYER_KEYS})
    S = x.shape[0]
    tq_tabs, tm_tabs = _rope_tables(S)
    onehot_t = _rel_onehot_t()
    bias_t = _bias_table(W["rel_bias"].T, onehot_t).reshape(SWA_KV_HEADS, SWA_GROUP, 2 * WINDOW, WINDOW)
    bias_t = bias_t.transpose(0, 2, 1, 3).reshape(SWA_KV_HEADS, 2 * WINDOW, GW)
    row = lambda a: a.reshape(1, -1)
    col = lambda a: a.reshape(-1, 1)
    fox_rows = (FOX_ROW0, FOX_ROW0 + FOX_HEADS * HEAD_DIM, FOX_ROW0 + 2 * FOX_HEADS * HEAD_DIM, SWA_Q_HEADS)
    fox = dict(rows=fox_rows, H=FOX_HEADS, Dk=HEAD_DIM, Dv=HEAD_DIM, scale=HEAD_DIM ** -0.5)
    mla = dict(rows=(0, 0, 0, SWA_Q_HEADS + FOX_HEADS), H=MLA_HEADS, Dk=MLA_PAD, Dv=HEAD_DIM, scale=MLA_QK ** -0.5)

    saved = []
    h = _rms_fwd(x, row(W["attn_pre_norm"][0]), name="rms_in")
    for l in range(DEPTH):
        sv = {"x0": x, "h1": h}
        for key, val in layer_weights(l, h).items():
            W[key][l] = val
        qkv = _matmul(W["w_qkv_t"][l], h, tb=True, out_dtype=BF16, name="proj_qkv")
        lat = _matmul(W["w_lat_t"][l], h, tb=True, name="proj_lat")
        oa, lse_a = _swa_fwd(qkv, bias_t, W["swa_sinks"][l], name="swa_fwd")
        fb_col = jnp.pad(col(W["forget_bias"][l]), ((0, GATE_ROWS - FOX_HEADS), (0, 0)))
        f4 = _gate_fwd(lat, fb_col, name="fox_gate_fwd")[:FOX_HEADS]
        f_row, f_col = f4[:, None, :], f4.T
        of, lse_f = _attn_fwd(qkv, qkv, qkv, f_row=f_row, f_col=f_col, name="fox_fwd", **fox)
        nq, nkv, qm, km, vm = _mla_prep_fwd(lat, col(W["q_latent_norm"][l]), col(W["kv_latent_norm"][l]), W["w_uq_t"][l],
                                            W["w_ukv_t"][l], tq_tabs, tm_tabs, name="mla_prep_fwd")
        oc, lse_c = _attn_fwd(qm, km, vm, name="mla_fwd", **mla)
        mixed = _group_norm_fwd(oa, of, oc, col(W["group_norm"][l]), name="group_norm_fwd")
        y = _matmul(mixed, W["w_out"][l], ta=True, name="proj_out")
        x1, h2 = _resid_rms(x, y, row(W["attn_post_norm"][l]), row(W["ffn_pre_norm"][l]), name="attn_resid")
        a = _matmul(h2, W["w_up"][l], b_shards=True, name="ffn_up")
        z = _conv_geglu_fwd(a, W["conv_w"][l], row(W["conv_b"][l]), name="conv_geglu_fwd")
        y2 = _matmul(z, W["w_down"][l], name="ffn_down")
        g_next = row(W["attn_pre_norm"][l + 1]) if l + 1 < DEPTH else None
        x2, h_next = _resid_rms(x1, y2, row(W["ffn_post_norm"][l]), g_next, name="ffn_resid")
        sv.update(qkv=qkv, lat=lat, oa=oa, lse_a=lse_a, fb_col=fb_col, f_row=f_row, f_col=f_col, of=of, lse_f=lse_f,
                  nq=nq, nkv=nkv, qm=qm, km=km, vm=vm, oc=oc, lse_c=lse_c, mixed=mixed, y=y, x1=x1, h2=h2, a=a, z=z, y2=y2)
        saved.append(sv)
        x, h = x2, h_next

    loss, dx = _loss_head(x, target)

    G = {k: [None] * DEPTH for k in WEIGHTS if k != "rel_bias" and k not in BIG}
    dbias_layers = [None] * DEPTH
    for l in reversed(range(DEPTH)):
        sv = saved[l]
        gb = {}
        dy2, dg = _rms_bwd(sv["y2"], row(W["ffn_post_norm"][l]), dx, out_dtype=BF16, name="ffn_post_bwd")
        G["ffn_post_norm"][l] = dg[0]
        dz = _matmul(dy2, W["w_down"][l], tb=True, name="ffn_down_dx")
        gb["w_down"] = _matmul(sv["z"], dy2, ta=True, name="ffn_down_dw").reshape(N_CHIPS, D_FF // N_CHIPS, D_MODEL)
        du, dcw, dcb = _conv_geglu_bwd(sv["a"], W["conv_w"][l], row(W["conv_b"][l]), dz, name="conv_geglu_bwd")
        G["conv_w"][l] = dcw.transpose(1, 0, 2).reshape(3, 2 * D_FF)
        G["conv_b"][l] = dcb.reshape(2 * D_FF)
        da = _conv_bwd_input(du, W["conv_w"][l], name="conv_bwd_input")
        dh2 = _matmul(da, W["w_up"][l], tb=True, b_shards=True, name="ffn_up_dx")
        gb["w_up"] = _matmul(sv["h2"], da, ta=True, out_shards=True, name="ffn_up_dw")
        token = layer_done(l, gb)
        gb = {}
        dx1, dg = _rms_bwd(sv["x1"], row(W["ffn_pre_norm"][l]) + token, dh2, resid=dx, out_dtype=F32, name="ffn_pre_bwd")
        G["ffn_pre_norm"][l] = dg[0]
        dy, dg = _rms_bwd(sv["y"], row(W["attn_post_norm"][l]), dx1, out_dtype=BF16, name="attn_post_bwd")
        G["attn_post_norm"][l] = dg[0]
        dmixed = _matmul(W["w_out"][l], dy, tb=True, name="proj_out_dx")
        gb["w_out"] = _matmul(sv["mixed"], dy, name="proj_out_dw").reshape(N_CHIPS, D_MODEL // N_CHIPS, D_MODEL)
        doa, dof, doc, dg, delta = _group_norm_bwd(sv["oa"], sv["of"], sv["oc"], col(W["group_norm"][l]), dmixed,
                                                   name="group_norm_bwd")
        G["group_norm"][l] = dg[:, 0]
        dqa, dkva, dbias_l, dsink = _swa_bwd(sv["qkv"], bias_t, W["swa_sinks"][l], doa, sv["lse_a"],
                                             delta.reshape(-1, S), name="swa_bwd")
        dbias_layers[l] = (dbias_l.reshape(SWA_KV_HEADS, 2 * WINDOW, SWA_GROUP, WINDOW).transpose(0, 2, 1, 3)
                           .reshape(SWA_Q_HEADS, -1))
        G["swa_sinks"][l] = dsink[:, 0]
        dqf, dkf, dvf, dfk = _attn_bwd(sv["qkv"], sv["qkv"], sv["qkv"], do=dof, lse=sv["lse_f"], delta=delta,
                                       f_row=sv["f_row"], f_col=sv["f_col"], name="fox_bwd", **fox)
        dF = jnp.pad(dfk.T, ((0, GATE_ROWS - FOX_HEADS), (0, 0)))
        dflog, dfb = _gate_bwd(sv["lat"], sv["fb_col"], dF, name="fox_gate_bwd")
        G["forget_bias"][l] = dfb[:FOX_HEADS, 0]
        dqm, dkm, dvm = _attn_bwd(sv["qm"], sv["km"], sv["vm"], do=doc, lse=sv["lse_c"], delta=delta, name="mla_bwd", **mla)
        dlat, dwq_t, dwkv_t, dgq, dgkv = _mla_prep_bwd(
            sv["lat"], sv["nq"], sv["nkv"], col(W["q_latent_norm"][l]), col(W["kv_latent_norm"][l]), W["w_uq_p"][l],
            W["w_ukv"][l], tq_tabs, tm_tabs, dqm, dkm, dvm, dflog, name="mla_prep_bwd")
        gb["w_uq"], gb["w_ukv"] = _shards_of_cols(_unpad_w_uq(dwq_t.T)), _shards_of_cols(dwkv_t.T)
        G["q_latent_norm"][l], G["kv_latent_norm"][l] = dgq[:, 0], dgkv[:, 0]
        dproj = _dproj_cast(dqa, dkva, dqf, dkf, dvf, dlat, name="dproj_cast")
        dh1 = _matmul(dproj, W["w_in_t"][l], ta=True, name="proj_in_dx")
        dw_in_t = _matmul(dproj, sv["h1"], name="proj_in_dw")
        gb["w_in"] = _t(dw_in_t[:IN_COLS].reshape(N_CHIPS, IN_COLS // N_CHIPS, D_MODEL))
        token = layer_done(l, gb)
        dx, dg = _rms_bwd(sv["x0"], row(W["attn_pre_norm"][l]) + token, dh1, resid=dx1, out_dtype=F32, name="attn_pre_bwd")
        G["attn_pre_norm"][l] = dg[0]

    grads = {k: jnp.stack(v) for k, v in G.items()}
    grads["rel_bias"] = _bias_table_bwd(jnp.stack(dbias_layers), onehot_t).T
    return loss, dx, grads


def kernel(x, attn_pre_norm, w_in, forget_bias, swa_sinks, rel_bias, q_latent_norm, w_uq, kv_latent_norm, w_ukv, group_norm, w_out, attn_post_norm, ffn_pre_norm, w_up, conv_w, conv_b, w_down, ffn_post_norm, loss_target, m_attn_pre_norm, m_w_in, m_forget_bias, m_swa_sinks, m_rel_bias, m_q_latent_norm, m_w_uq, m_kv_latent_norm, m_w_ukv, m_group_norm, m_w_out, m_attn_post_norm, m_ffn_pre_norm, m_w_up, m_conv_w, m_conv_b, m_w_down, m_ffn_post_norm, v_attn_pre_norm, v_w_in, v_forget_bias, v_swa_sinks, v_rel_bias, v_q_latent_norm, v_w_uq, v_kv_latent_norm, v_w_ukv, v_group_norm, v_w_out, v_attn_post_norm, v_ffn_pre_norm, v_w_up, v_conv_w, v_conv_b, v_w_down, v_ffn_post_norm):
    args = dict(locals())
    w = {k: args[k] for k in WEIGHTS}
    m = {k: args["m_" + k] for k in WEIGHTS}
    v = {k: args["v_" + k] for k in WEIGHTS}

    sent = BIG + ("conv_w",)
    gather_state, token = _gather_start([[w[k][l] if k == "conv_w" else w[k][l].astype(BF16) for k in sent]
                                         for l in range(DEPTH)])
    W = {k: w[k] for k in SMALL}
    W["attn_pre_norm"] = W["attn_pre_norm"] + token

    def layer_weights(l, after):
        srcs, lands = _gather_wait(gather_state[l], after, name=f"weight_gather_wait_{l}")
        lands = _gather_forward(lands, name=f"weight_gather_forward_{l}")
        lands = [_place_own(g, s, name="place_own_shard") for g, s in zip(lands, srcs)]
        return _layer_weights(dict(zip(sent, lands)))

    started, groups = [], []

    def layer_done(l, gb):
        keys = [k for k in BIG if k in gb]
        gs = [gb[k] for k in keys]
        tag = f"{l}_{keys[0]}"
        recv = _sibling_exchange(gs, name="grad_sibling_exchange_" + tag)
        pair = [_pair_sum(gk, rk, name="grad_pair_sum") for gk, rk in zip(gs, recv)]
        state, token = _scatter_start(pair, name="grad_scatter_start_" + tag)
        started.append(state)
        groups.append((l, keys))
        return token

    loss_part, dx, g = _local_step(x[0], loss_target[0], W, layer_weights, layer_done)
    loss = lax.psum(loss_part, ("x", "y", "c"))

    reduced = {}
    for (l, keys), (pair, zones) in zip(groups, _scatter_wait(started, dx, name="grad_scatter_wait")):
        for k, p, z in zip(keys, pair, zones):
            reduced[k, l] = _chip_sum(z, p, name="grad_chip_sum")
    mine = [jnp.stack([reduced[k, l] for l in range(DEPTH)]) for k in BIG]
    other = _sibling_share(mine)
    out_g, out_d, out_m, out_v = {}, {}, {}, {}
    for k, g_mine, g_other in zip(BIG, mine, other):
        out_g[k], out_d[k], out_m[k], out_v[k] = _adamw_halves(w[k], g_mine, g_other, m[k], v[k], name="adamw_" + k)

    small_shapes = [w[k].shape for k in SMALL]
    reduced = _allreduce_small(_pack([g[k] for k in SMALL] + [g["conv_w"]], LANES, 8))
    *g_small, g_cw = _unpack(reduced, small_shapes + [g["conv_w"].shape])
    chip = 2 * lax.axis_index("x") + lax.axis_index("y")
    g_small.append(lax.dynamic_slice_in_dim(g_cw, chip * FF_SHARD, FF_SHARD, axis=2))
    names = SMALL + ("conv_w",)
    shapes = small_shapes + [w["conv_w"].shape]
    packed = lambda arrs: _pack(arrs, LANES, ROW_TILE)[None]
    d_s, m_s, v_s = _adamw(packed([w[k] for k in names]), packed(g_small), packed([m[k] for k in names]),
                           packed([v[k] for k in names]), name="adamw_small")
    out_g.update(zip(names, g_small))
    out_d.update(zip(names, _unpack(d_s, shapes)))
    out_m.update(zip(names, _unpack(m_s, shapes)))
    out_v.update(zip(names, _unpack(v_s, shapes)))

    return (loss, dx[None], *[out_g[k] for k in WEIGHTS], *[out_d[k] for k in WEIGHTS],
            *[out_m[k] for k in WEIGHTS], *[out_v[k] for k in WEIGHTS])
```

```python
import math

import numpy as np
import jax
import jax.numpy as jnp
from jax import lax
from jax.experimental import pallas as pl
from jax.experimental.pallas import tpu as pltpu

F32 = jnp.float32
BF16 = jnp.bfloat16

D_MODEL = 1024
DEPTH = 4
HEAD_DIM = 64
SWA_Q_HEADS = 8
SWA_KV_HEADS = 2
SWA_GROUP = SWA_Q_HEADS // SWA_KV_HEADS
WINDOW = 128
FOX_HEADS = 4
MLA_HEADS = 4
MLA_Q_RANK = 256
MLA_KV_RANK = 128
MLA_NOPE = 64
MLA_ROPE = 32
MLA_QK = MLA_NOPE + MLA_ROPE
ROPE_THETA = 10000.0
REL_BUCKETS = 32
REL_MAX_DIST = 128
D_FF = 2816
EPS = 1e-6
NEG_INF = -1e30
LANES = 128
N_CHIPS = 4

IN_COLS = 1956
IN_ROWS = 2048
QKV_ROWS = 1536
LAT_ROWS = IN_ROWS - QKV_ROWS
LAT_SHIFT = FOX_HEADS
FOX_ROW0 = 768
MLA_PAD = LANES
GATE_ROWS = 8

ADAM_LR = 0.001
ADAM_B1 = 0.9
ADAM_B2 = 0.999
ADAM_EPS = 1e-08
ADAM_WD = 0.01
ADAM_STEP = 10

VMEM_LIMIT_BYTES = 48 * 1024 * 1024
ATT_TILE = 512
ROW_TILE = 256
MESH = pl.DeviceIdType.MESH

NT = (((1,), (1,)), ((), ()))
TN = (((0,), (0,)), ((), ()))
NN = (((1,), (0,)), ((), ()))


def _params(*sem):
    return pltpu.CompilerParams(dimension_semantics=sem, vmem_limit_bytes=VMEM_LIMIT_BYTES)


def _tile(dim, cap):
    for t in (2048, 1408, 1024, 512, 256, 128, 64, 32, 16, 8):
        if t <= cap and dim % t == 0:
            return t
    return dim


def _dot(a, b, dims=NN):
    return lax.dot_general(a, b, dims, preferred_element_type=F32)


def _split3(a):
    a1 = a.astype(BF16)
    r1 = a - a1.astype(F32)
    a2 = r1.astype(BF16)
    a3 = (r1 - a2.astype(F32)).astype(BF16)
    return a1, a2, a3


FF_SHARD = 2 * D_FF // N_CHIPS


def _matmul(a, b, *, ta=False, tb=False, out_dtype=F32, name, b_shards=False, out_shards=False, a_halves=False,
            b_halves=False):
    if a_halves:
        M, K = a.shape[1], 2 * a.shape[2]
    elif ta:
        K, M = a.shape
    else:
        M, K = a.shape
    if b_halves:
        K2, N = b.shape[1], 2 * b.shape[2]
    elif b_shards:
        K2, N = (2 * D_FF, D_MODEL) if tb else (D_MODEL, 2 * D_FF)
    elif tb:
        N, K2 = b.shape
    else:
        K2, N = b.shape
    assert K == K2, (a.shape, b.shape)
    tm, tn, tk = (M if M <= 2048 else _tile(M, 1408)), _tile(N, 1408), _tile(K, 1408)
    nk = K // tk
    dims = (((0 if ta else 1,), (1 if tb else 0,)), ((), ()))

    def body(a_ref, b_ref, o_ref, acc_ref):
        k = pl.program_id(2)

        @pl.when(k == 0)
        def _():
            acc_ref[...] = jnp.zeros_like(acc_ref)

        acc_ref[...] += lax.dot_general(a_ref[...], b_ref[...], dims, preferred_element_type=F32)

        @pl.when(k == nk - 1)
        def _():
            o_ref[...] = acc_ref[...].astype(o_ref.dtype)

    if a_halves:
        nh = K // 2 // tk
        a_spec = pl.BlockSpec((None, tm, tk), lambda i, j, k: (k // nh, i, k % nh))
    else:
        a_spec = pl.BlockSpec((tk, tm), lambda i, j, k: (k, i)) if ta else pl.BlockSpec((tm, tk), lambda i, j, k: (i, k))
    if b_halves:
        nh = N // 2 // tn
        b_spec = pl.BlockSpec((None, tk, tn), lambda i, j, k: (j // nh, k, j % nh))
    elif b_shards and tb:
        assert tk == FF_SHARD
        b_spec = pl.BlockSpec((None, tn, tk), lambda i, j, k: (k, j, 0))
    elif b_shards:
        assert tn == FF_SHARD
        b_spec = pl.BlockSpec((None, tk, tn), lambda i, j, k: (j, k, 0))
    else:
        b_spec = pl.BlockSpec((tn, tk), lambda i, j, k: (j, k)) if tb else pl.BlockSpec((tk, tn), lambda i, j, k: (k, j))
    if out_shards:
        assert tn == FF_SHARD
        out_spec = pl.BlockSpec((None, tm, tn), lambda i, j, k: (j, i, 0))
        out_shape = jax.ShapeDtypeStruct((N // tn, M, tn), out_dtype)
    else:
        out_spec = pl.BlockSpec((tm, tn), lambda i, j, k: (i, j))
        out_shape = jax.ShapeDtypeStruct((M, N), out_dtype)
    return pl.pallas_call(
        body, name=name, grid=(M // tm, N // tn, nk),
        in_specs=[a_spec, b_spec], out_specs=out_spec, out_shape=out_shape,
        scratch_shapes=[pltpu.VMEM((tm, tn), F32)],
        compiler_params=_params("parallel", "parallel", "arbitrary"),
    )(a, b)


def _seg_rms(xs, g):
    r = lax.rsqrt(jnp.mean(xs * xs, axis=-1, keepdims=True) + EPS)
    return xs * r * g


def _seg_rms_bwd(xs, g, dy):
    r = lax.rsqrt(jnp.mean(xs * xs, axis=-1, keepdims=True) + EPS)
    gd = dy * g
    c = jnp.mean(gd * xs, axis=-1, keepdims=True)
    dx = r * gd - xs * (r * r * r * c)
    dg = jnp.sum(dy * (xs * r), axis=0, keepdims=True)
    return dx, dg


def _rms_fwd(x, g, *, name):
    S, W = x.shape
    tm = _tile(S, 512)

    def body(x_ref, g_ref, o_ref):
        o_ref[...] = _seg_rms(x_ref[...], g_ref[...]).astype(o_ref.dtype)

    return pl.pallas_call(
        body, name=name, grid=(S // tm,),
        in_specs=[pl.BlockSpec((tm, W), lambda i: (i, 0)), pl.BlockSpec((1, W), lambda i: (0, 0))],
        out_specs=pl.BlockSpec((tm, W), lambda i: (i, 0)),
        out_shape=jax.ShapeDtypeStruct((S, W), BF16),
        compiler_params=_params("parallel"),
    )(x, g)


def _rms_bwd(x, g, dy, *, resid=None, out_dtype, name):
    S, W = x.shape
    tm = _tile(S, 512)
    has_resid = resid is not None

    def body(*refs):
        if has_resid:
            x_ref, g_ref, dy_ref, r_ref, dx_ref, dg_ref = refs
        else:
            x_ref, g_ref, dy_ref, dx_ref, dg_ref = refs

        @pl.when(pl.program_id(0) == 0)
        def _():
            dg_ref[...] = jnp.zeros_like(dg_ref)

        dx, dg = _seg_rms_bwd(x_ref[...], g_ref[...], dy_ref[...])
        if has_resid:
            dx = dx + r_ref[...]
        dx_ref[...] = dx.astype(dx_ref.dtype)
        dg_ref[...] += dg

    row = pl.BlockSpec((tm, W), lambda i: (i, 0))
    vec = pl.BlockSpec((1, W), lambda i: (0, 0))
    ins = [x, g, dy] + ([resid] if has_resid else [])
    return pl.pallas_call(
        body, name=name, grid=(S // tm,),
        in_specs=[row, vec, row] + ([row] if has_resid else []),
        out_specs=[row, vec],
        out_shape=[jax.ShapeDtypeStruct((S, W), out_dtype), jax.ShapeDtypeStruct((1, W), F32)],
        compiler_params=_params("arbitrary"),
    )(*ins)


def _resid_rms(x, y, g_post, g_next, *, name):
    S, W = x.shape
    tm = _tile(S, 512)
    with_next = g_next is not None

    def body(*refs):
        if with_next:
            x_ref, y_ref, gp_ref, gn_ref, xo_ref, h_ref = refs
        else:
            x_ref, y_ref, gp_ref, xo_ref = refs
        xn = x_ref[...] + _seg_rms(y_ref[...], gp_ref[...])
        xo_ref[...] = xn
        if with_next:
            h_ref[...] = _seg_rms(xn, gn_ref[...]).astype(BF16)

    row = pl.BlockSpec((tm, W), lambda i: (i, 0))
    vec = pl.BlockSpec((1, W), lambda i: (0, 0))
    outs = [jax.ShapeDtypeStruct((S, W), F32)] + ([jax.ShapeDtypeStruct((S, W), BF16)] if with_next else [])
    res = pl.pallas_call(
        body, name=name, grid=(S // tm,),
        in_specs=[row, row, vec] + ([vec] if with_next else []),
        out_specs=[row] + ([row] if with_next else []),
        out_shape=outs,
        compiler_params=_params("parallel"),
    )(*([x, y, g_post] + ([g_next] if with_next else [])))
    return (res[0], res[1]) if with_next else (res[0], None)


def _col_rms(xs, g):
    r = lax.rsqrt(jnp.mean(xs * xs, axis=0, keepdims=True) + EPS)
    return xs * r * g


def _col_rms_bwd(xs, g, dy):
    r = lax.rsqrt(jnp.mean(xs * xs, axis=0, keepdims=True) + EPS)
    gd = dy * g
    c = jnp.mean(gd * xs, axis=0, keepdims=True)
    dx = r * gd - xs * (r * r * r * c)
    dg = jnp.sum(dy * (xs * r), axis=1, keepdims=True)
    return dx, dg


GROUP_ROWS = (SWA_Q_HEADS * HEAD_DIM, FOX_HEADS * HEAD_DIM, MLA_HEADS * HEAD_DIM)


def _group_specs(S, tn):
    outs = [pl.BlockSpec((n, tn), lambda i: (0, i)) for n in GROUP_ROWS]
    g = pl.BlockSpec((D_MODEL, 1), lambda i: (0, 0))
    mixed = pl.BlockSpec((D_MODEL, tn), lambda i: (0, i))
    return outs, g, mixed


def _group_norm_fwd(oa, of, oc, g, *, name):
    S = oa.shape[1]
    tn = _tile(S, 512)
    outs, gs, mixed = _group_specs(S, tn)

    def body(a_ref, f_ref, c_ref, g_ref, o_ref):
        r0 = 0
        for ref, n in zip((a_ref, f_ref, c_ref), GROUP_ROWS):
            o_ref[r0:r0 + n, :] = _col_rms(ref[...], g_ref[r0:r0 + n, :]).astype(BF16)
            r0 += n

    return pl.pallas_call(
        body, name=name, grid=(S // tn,),
        in_specs=outs + [gs], out_specs=mixed,
        out_shape=jax.ShapeDtypeStruct((D_MODEL, S), BF16),
        compiler_params=_params("parallel"),
    )(oa, of, oc, g)


def _group_norm_bwd(oa, of, oc, g, dmixed, *, name):
    S = oa.shape[1]
    tn = _tile(S, 512)
    outs, gs, mixed = _group_specs(S, tn)
    n_heads = D_MODEL // HEAD_DIM

    def body(a_ref, f_ref, c_ref, g_ref, dm_ref, da_ref, df_ref, dc_ref, dg_ref, dl_ref):
        @pl.when(pl.program_id(0) == 0)
        def _():
            dg_ref[...] = jnp.zeros_like(dg_ref)

        r0 = 0
        for ref, dref, n in zip((a_ref, f_ref, c_ref), (da_ref, df_ref, dc_ref), GROUP_ROWS):
            o = ref[...]
            dx, dg = _col_rms_bwd(o, g_ref[r0:r0 + n, :], dm_ref[r0:r0 + n, :])
            dxb = dx.astype(BF16)
            dref[...] = dxb
            dg_ref[r0:r0 + n, :] += dg
            od = o * dxb.astype(F32)
            for h in range(n // HEAD_DIM):
                dl_ref[r0 // HEAD_DIM + h] = jnp.sum(od[h * HEAD_DIM:(h + 1) * HEAD_DIM, :], axis=0, keepdims=True)
            r0 += n

    return pl.pallas_call(
        body, name=name, grid=(S // tn,),
        in_specs=outs + [gs, mixed], out_specs=outs + [gs, pl.BlockSpec((n_heads, 1, tn), lambda i: (0, 0, i))],
        out_shape=[jax.ShapeDtypeStruct((n, S), BF16) for n in GROUP_ROWS] + [jax.ShapeDtypeStruct((D_MODEL, 1), F32),
                                                                              jax.ShapeDtypeStruct((n_heads, 1, S), F32)],
        compiler_params=_params("arbitrary"),
    )(oa, of, oc, g, dmixed)


def _loss_head(y, target):
    S, W = y.shape
    tm = _tile(S, 512)

    def body(y_ref, t_ref, d_ref, l_ref):
        @pl.when(pl.program_id(0) == 0)
        def _():
            l_ref[...] = jnp.zeros_like(l_ref)

        err = y_ref[...] - t_ref[...]
        d_ref[...] = err * (1.0 / W)
        l_ref[...] += 0.5 * jnp.sum(jnp.mean(err * err, axis=-1, keepdims=True), axis=0, keepdims=True)

    row = pl.BlockSpec((tm, W), lambda i: (i, 0))
    d, l = pl.pallas_call(
        body, name="loss_head", grid=(S // tm,),
        in_specs=[row, row],
        out_specs=[row, pl.BlockSpec((1, 1), lambda i: (0, 0))],
        out_shape=[jax.ShapeDtypeStruct((S, W), F32), jax.ShapeDtypeStruct((1, 1), F32)],
        compiler_params=_params("arbitrary"),
    )(y, target)
    return l[0, 0], d


def _attn_fwd(q_src, k_src, v_src, rows, H, Dk, Dv, scale, f_row=None, f_col=None, *, name):
    S = q_src.shape[1]
    T = _tile(S, ATT_TILE)
    nq = S // T
    forget = f_row is not None
    qb, kb, vb = rows[0] // (H * Dk), rows[1] // (H * Dk), rows[2] // (H * Dv)
    hs = range(H)

    def body(*refs):
        if forget:
            q_ref, k_ref, v_ref, fq_ref, fk_ref, o_ref, lse_ref = refs
        else:
            q_ref, k_ref, v_ref, o_ref, lse_ref = refs
        i = pl.program_id(0)

        def tile(j, masked, state):
            off = pl.multiple_of(j * T, T)
            ss = [_dot(k_ref[h * Dk:(h + 1) * Dk, pl.ds(off, T)], q_ref[h * Dk:(h + 1) * Dk, :], TN) * scale for h in hs]
            if forget:
                ss = [ss[h] + (fq_ref[h] - fk_ref[pl.ds(off, T), h:h + 1]) for h in hs]
            if masked:
                r = lax.broadcasted_iota(jnp.int32, (T, T), 0)
                c = lax.broadcasted_iota(jnp.int32, (T, T), 1)
                ss = [jnp.where(r <= c, s, NEG_INF) for s in ss]
            m_new = [jnp.maximum(state[h][0], jnp.max(ss[h], axis=0, keepdims=True)) for h in hs]
            alpha = [jnp.exp(state[h][0] - m_new[h]) for h in hs]
            ps = [jnp.exp(ss[h] - m_new[h]) for h in hs]
            l_new = [alpha[h] * state[h][1] + jnp.sum(ps[h], axis=0, keepdims=True) for h in hs]
            p_hi = [p.astype(BF16) for p in ps]
            vs = [v_ref[h * Dv:(h + 1) * Dv, pl.ds(off, T)] for h in hs]
            pv = [_dot(vs[h], p_hi[h]) for h in hs]
            if forget:
                pv = [pv[h] + _dot(vs[h], (ps[h] - p_hi[h].astype(F32)).astype(BF16)) for h in hs]
            return tuple((m_new[h], l_new[h], alpha[h] * state[h][2] + pv[h]) for h in hs)

        init = tuple((jnp.full((1, T), NEG_INF, F32), jnp.zeros((1, T), F32), jnp.zeros((Dv, T), F32)) for _ in hs)
        state = lax.fori_loop(0, i, lambda j, st: tile(j, False, st), init)
        state = tile(i, True, state)
        for h in hs:
            m, l, acc = state[h]
            o_ref[h * Dv:(h + 1) * Dv, :] = acc / l
            lse_ref[h] = m + jnp.log(l)

    in_specs = [pl.BlockSpec((H * Dk, T), lambda i: (qb, i)),
                pl.BlockSpec((H * Dk, S), lambda i: (kb, 0)),
                pl.BlockSpec((H * Dv, S), lambda i: (vb, 0))]
    ins = [q_src, k_src, v_src]
    if forget:
        in_specs += [pl.BlockSpec((H, 1, T), lambda i: (0, 0, i)), pl.BlockSpec((S, H), lambda i: (0, 0))]
        ins += [f_row, f_col]
    return pl.pallas_call(
        body, name=name, grid=(nq,),
        in_specs=in_specs,
        out_specs=[pl.BlockSpec((H * Dv, T), lambda i: (0, i)), pl.BlockSpec((H, 1, T), lambda i: (0, 0, i))],
        out_shape=[jax.ShapeDtypeStruct((H * Dv, S), F32), jax.ShapeDtypeStruct((H, 1, S), F32)],
        compiler_params=_params("parallel"),
    )(*ins)


def _attn_bwd(q_src, k_src, v_src, rows, H, Dk, Dv, scale, do, lse, delta, f_row=None, f_col=None, *, name):
    S = q_src.shape[1]
    T = _tile(S, ATT_TILE)
    nq = S // T
    forget = f_row is not None
    qb, kb, vb, db = rows[0] // (H * Dk), rows[1] // (H * Dk), rows[2] // (H * Dv), rows[3] // H
    hs = range(H)

    def body(*refs):
        if forget:
            (q_ref, k_ref, v_ref, do_ref, lse_ref, dl_ref, fq_ref, fk_ref,
             dq_ref, dk_ref, dv_ref, df_ref, dk_s, dv_s, df_s) = refs
        else:
            q_ref, k_ref, v_ref, do_ref, lse_ref, dl_ref, dq_ref, dk_ref, dv_ref, dk_s, dv_s = refs
        j = pl.program_id(0)

        @pl.when(j == 0)
        def _():
            dq_ref[...] = jnp.zeros_like(dq_ref)

        dk_s[...] = jnp.zeros_like(dk_s)
        dv_s[...] = jnp.zeros_like(dv_s)
        if forget:
            df_s[...] = jnp.zeros_like(df_s)
        kt = [k_ref[h * Dk:(h + 1) * Dk, :] for h in hs]
        kj = [k.T for k in kt]
        vj = [v_ref[h * Dv:(h + 1) * Dv, :].T for h in hs]
        koff = pl.multiple_of(j * T, T)

        def tile(i, masked):
            cols = pl.ds(pl.multiple_of(i * T, T), T)
            qi = [q_ref[h * Dk:(h + 1) * Dk, cols] for h in hs]
            doi = [do_ref[h * Dv:(h + 1) * Dv, cols] for h in hs]
            st = [_dot(kj[h], qi[h]) * scale for h in hs]
            if forget:
                st = [st[h] + (fq_ref[h, :, cols] - fk_ref[pl.ds(koff, T), h:h + 1]) for h in hs]
            if masked:
                r = lax.broadcasted_iota(jnp.int32, (T, T), 0)
                c = lax.broadcasted_iota(jnp.int32, (T, T), 1)
                st = [jnp.where(r <= c, x, NEG_INF) for x in st]
            pt = [jnp.exp(st[h] - lse_ref[h, :, cols]) for h in hs]
            dpt = [_dot(vj[h], doi[h]) for h in hs]
            dst = [pt[h] * (dpt[h] - dl_ref[h, :, cols]) for h in hs]
            ptb = [p.astype(BF16) for p in pt]
            dsb = [d.astype(BF16) for d in dst]
            for h in hs:
                dv_s[h * Dv:(h + 1) * Dv, :] += _dot(doi[h], ptb[h], NT)
            for h in hs:
                dk_s[h * Dk:(h + 1) * Dk, :] += _dot(qi[h], dsb[h], NT)
            for h in hs:
                dq_ref[h * Dk:(h + 1) * Dk, cols] += _dot(kt[h], dsb[h]) * scale
            if forget:
                for h in hs:
                    part = dst[h][:, 0:LANES]
                    for c0 in range(LANES, T, LANES):
                        part = part + dst[h][:, c0:c0 + LANES]
                    df_s[h] += part

        tile(j, True)

        def loop_body(i, carry):
            tile(i, False)
            return carry

        lax.fori_loop(j + 1, nq, loop_body, 0)
        dk_ref[...] = dk_s[...] * scale
        dv_ref[...] = dv_s[...]
        if forget:
            df_ref[...] = jnp.concatenate([-jnp.sum(df_s[h], axis=-1, keepdims=True) for h in hs], axis=1)

    res = lambda D, b0: pl.BlockSpec((H * D, S), lambda j: (b0, 0))
    blk = lambda D, b0: pl.BlockSpec((H * D, T), lambda j: (b0, j))
    row3 = lambda b0: pl.BlockSpec((H, 1, S), lambda j: (b0, 0, 0))
    in_specs = [res(Dk, qb), blk(Dk, kb), blk(Dv, vb), res(Dv, 0), row3(0), row3(db)]
    ins = [q_src, k_src, v_src, do, lse, delta]
    out_specs = [res(Dk, 0), blk(Dk, 0), blk(Dv, 0)]
    out_shape = [jax.ShapeDtypeStruct((H * Dk, S), F32), jax.ShapeDtypeStruct((H * Dk, S), F32),
                 jax.ShapeDtypeStruct((H * Dv, S), F32)]
    scratch = [pltpu.VMEM((H * Dk, T), F32), pltpu.VMEM((H * Dv, T), F32)]
    if forget:
        in_specs += [row3(0), pl.BlockSpec((S, H), lambda j: (0, 0))]
        ins += [f_row, f_col]
        out_specs.append(pl.BlockSpec((T, H), lambda j: (j, 0)))
        out_shape.append(jax.ShapeDtypeStruct((S, H), F32))
        scratch.append(pltpu.VMEM((H, T, min(T, LANES)), F32))
    return pl.pallas_call(
        body, name=name, grid=(nq,),
        in_specs=in_specs, out_specs=out_specs, out_shape=out_shape, scratch_shapes=scratch,
        compiler_params=_params("arbitrary"),
    )(*ins)


GW = SWA_GROUP * WINDOW


def _swa_masks(i):
    r = lax.broadcasted_iota(jnp.int32, (WINDOW, GW), 0)
    c = lax.broadcasted_iota(jnp.int32, (WINDOW, GW), 1) % WINDOW
    return (r > c) & (i > 0), r <= c


def _swa_specs():
    W = WINDOW
    kv_rows = SWA_KV_HEADS * HEAD_DIM
    q = pl.BlockSpec((SWA_Q_HEADS * HEAD_DIM, W), lambda i: (0, i))
    prev = lambda b: pl.BlockSpec((kv_rows, W), lambda i: (b, jnp.maximum(i - 1, 0)))
    cur = lambda b: pl.BlockSpec((kv_rows, W), lambda i: (b, i))
    bias = pl.BlockSpec((SWA_KV_HEADS, 2 * W, GW), lambda i: (0, 0, 0))
    stat = pl.BlockSpec((SWA_Q_HEADS, W), lambda i: (0, i))
    sink = pl.BlockSpec(memory_space=pltpu.SMEM)
    return q, prev(4), cur(4), prev(5), cur(5), bias, stat, sink


def _group_lanes(ref, g, rows_per_head):
    h0 = g * SWA_GROUP
    return jnp.concatenate([ref[(h0 + j) * rows_per_head:(h0 + j + 1) * rows_per_head, :] for j in range(SWA_GROUP)], axis=1)


def _swa_scores(g, q_ref, kp_ref, kc_ref, b_ref, masks):
    rows = slice(g * HEAD_DIM, (g + 1) * HEAD_DIM)
    qg = _group_lanes(q_ref, g, HEAD_DIM)
    scale = HEAD_DIM ** -0.5
    s_p = jnp.where(masks[0], _dot(kp_ref[rows, :], qg, TN) * scale + b_ref[g, 0:WINDOW, :], NEG_INF)
    s_c = jnp.where(masks[1], _dot(kc_ref[rows, :], qg, TN) * scale + b_ref[g, WINDOW:2 * WINDOW, :], NEG_INF)
    return qg, rows, s_p, s_c


def _sink_row(sink_ref, g):
    return jnp.concatenate([jnp.full((1, WINDOW), sink_ref[g * SWA_GROUP + j], F32) for j in range(SWA_GROUP)], axis=1)


def _swa_fwd(qkv, bias_g, sinks, *, name):
    S = qkv.shape[1]
    qs, kp, kc, vp, vc, bs, stat, sk = _swa_specs()
    gs = range(SWA_KV_HEADS)

    def body(sink_ref, q_ref, kp_ref, kc_ref, vp_ref, vc_ref, b_ref, o_ref, lse_ref):
        masks = _swa_masks(pl.program_id(0))
        sc = [_swa_scores(g, q_ref, kp_ref, kc_ref, b_ref, masks) for g in gs]
        sinks_g = [_sink_row(sink_ref, g) for g in gs]
        m = [jnp.maximum(jnp.maximum(jnp.max(sc[g][2], axis=0, keepdims=True), jnp.max(sc[g][3], axis=0, keepdims=True)),
                         sinks_g[g]) for g in gs]
        p_p = [jnp.exp(sc[g][2] - m[g]) for g in gs]
        p_c = [jnp.exp(sc[g][3] - m[g]) for g in gs]
        l = [jnp.sum(p_p[g], axis=0, keepdims=True) + jnp.sum(p_c[g], axis=0, keepdims=True) + jnp.exp(sinks_g[g] - m[g])
             for g in gs]
        o = [_dot(vp_ref[sc[g][1], :], p_p[g].astype(BF16)) + _dot(vc_ref[sc[g][1], :], p_c[g].astype(BF16)) for g in gs]
        for g in gs:
            og = o[g] / l[g]
            lse = m[g] + jnp.log(l[g])
            for j in range(SWA_GROUP):
                h = g * SWA_GROUP + j
                o_ref[h * HEAD_DIM:(h + 1) * HEAD_DIM, :] = og[:, j * WINDOW:(j + 1) * WINDOW]
                lse_ref[h:h + 1, :] = lse[:, j * WINDOW:(j + 1) * WINDOW]

    return pl.pallas_call(
        body, name=name, grid=(S // WINDOW,),
        in_specs=[sk, qs, kp, kc, vp, vc, bs],
        out_specs=[qs, stat],
        out_shape=[jax.ShapeDtypeStruct((SWA_Q_HEADS * HEAD_DIM, S), F32), jax.ShapeDtypeStruct((SWA_Q_HEADS, S), F32)],
        compiler_params=_params("parallel"),
    )(sinks, qkv, qkv, qkv, qkv, qkv, bias_g)


def _swa_bwd(qkv, bias_g, sinks, do, lse, delta, *, name):
    S = qkv.shape[1]
    W = WINDOW
    qs, kp, kc, vp, vc, bs, stat, sk = _swa_specs()
    scale = HEAD_DIM ** -0.5
    kv_rows = SWA_KV_HEADS * HEAD_DIM
    gs = range(SWA_KV_HEADS)

    def body(sink_ref, q_ref, kp_ref, kc_ref, vp_ref, vc_ref, b_ref, do_ref, lse_ref, dl_ref,
             dq_ref, dkv_ref, db_ref, dsk_ref):
        i = pl.program_id(0)

        @pl.when(i == 0)
        def _():
            dkv_ref[...] = jnp.zeros_like(dkv_ref)
            db_ref[...] = jnp.zeros_like(db_ref)
            dsk_ref[...] = jnp.zeros_like(dsk_ref)

        masks = _swa_masks(i)
        prev = pl.ds(pl.multiple_of(jnp.maximum(i - 1, 0) * W, W), W)
        cur = pl.ds(pl.multiple_of(i * W, W), W)
        sc = [_swa_scores(g, q_ref, kp_ref, kc_ref, b_ref, masks) for g in gs]
        dog = [_group_lanes(do_ref, g, HEAD_DIM) for g in gs]
        lse = [_group_lanes(lse_ref, g, 1) for g in gs]
        dl = [_group_lanes(dl_ref, g, 1) for g in gs]
        p_p = [jnp.exp(sc[g][2] - lse[g]) for g in gs]
        p_c = [jnp.exp(sc[g][3] - lse[g]) for g in gs]
        ds_p = [p_p[g] * (_dot(vp_ref[sc[g][1], :], dog[g], TN) - dl[g]) for g in gs]
        ds_c = [p_c[g] * (_dot(vc_ref[sc[g][1], :], dog[g], TN) - dl[g]) for g in gs]
        for g in gs:
            db_ref[g, 0:W, :] += ds_p[g]
            db_ref[g, W:2 * W, :] += ds_c[g]
            dsk = jnp.exp(_sink_row(sink_ref, g) - lse[g]) * dl[g]
            for j in range(SWA_GROUP):
                h = g * SWA_GROUP + j
                dsk_ref[h:h + 1, :] -= jnp.broadcast_to(jnp.sum(dsk[:, j * W:(j + 1) * W], axis=1, keepdims=True), (1, LANES))
        dsb_p = [d.astype(BF16) for d in ds_p]
        dsb_c = [d.astype(BF16) for d in ds_c]
        for g in gs:
            rows = sc[g][1]
            dq = (_dot(kp_ref[rows, :], dsb_p[g]) + _dot(kc_ref[rows, :], dsb_c[g])) * scale
            for j in range(SWA_GROUP):
                h = g * SWA_GROUP + j
                dq_ref[h * HEAD_DIM:(h + 1) * HEAD_DIM, :] = dq[:, j * W:(j + 1) * W]
        for g in gs:
            rows = sc[g][1]
            vrows = slice(kv_rows + rows.start, kv_rows + rows.stop)
            dkv_ref[rows, prev] += _dot(sc[g][0], dsb_p[g], NT) * scale
            dkv_ref[rows, cur] += _dot(sc[g][0], dsb_c[g], NT) * scale
            dkv_ref[vrows, prev] += _dot(dog[g], p_p[g].astype(BF16), NT)
            dkv_ref[vrows, cur] += _dot(dog[g], p_c[g].astype(BF16), NT)

    return pl.pallas_call(
        body, name=name, grid=(S // W,),
        in_specs=[sk, qs, kp, kc, vp, vc, bs, qs, stat, stat],
        out_specs=[qs, pl.BlockSpec((2 * kv_rows, S), lambda i: (0, 0)), bs, pl.BlockSpec((SWA_Q_HEADS, LANES), lambda i: (0, 0))],
        out_shape=[jax.ShapeDtypeStruct((SWA_Q_HEADS * HEAD_DIM, S), F32), jax.ShapeDtypeStruct((2 * kv_rows, S), F32),
                   jax.ShapeDtypeStruct((SWA_KV_HEADS, 2 * W, GW), F32), jax.ShapeDtypeStruct((SWA_Q_HEADS, LANES), F32)],
        compiler_params=_params("arbitrary"),
    )(sinks, qkv, qkv, qkv, qkv, qkv, bias_g, do, lse, delta)


def _rel_onehot_t():
    qi = jnp.arange(WINDOW, dtype=jnp.int32)[None, :] + WINDOW
    kj = jnp.arange(2 * WINDOW, dtype=jnp.int32)[:, None]
    dist = qi - kj
    max_exact = REL_BUCKETS // 2
    d = jnp.maximum(dist, 0)
    log_ratio = jnp.log(jnp.maximum(d, 1).astype(F32) / max_exact) / math.log(REL_MAX_DIST / max_exact)
    large = jnp.minimum(max_exact + (log_ratio * (REL_BUCKETS - max_exact)).astype(jnp.int32), REL_BUCKETS - 1)
    bucket = jnp.where(d < max_exact, d, large).reshape(-1)
    return (bucket[None, :] == jnp.arange(REL_BUCKETS, dtype=jnp.int32)[:, None]).astype(BF16)


def _bias_table(rel_bias_t, onehot_t):
    Hq, NB = rel_bias_t.shape
    N = onehot_t.shape[1]
    tn = _tile(N, 4096)

    def body(r_ref, oh_ref, o_ref):
        oh = oh_ref[...]
        a1, a2, a3 = _split3(r_ref[...])
        o_ref[...] = _dot(a1, oh) + _dot(a2, oh) + _dot(a3, oh)

    return pl.pallas_call(
        body, name="rel_bias_table", grid=(N // tn,),
        in_specs=[pl.BlockSpec((Hq, NB), lambda j: (0, 0)), pl.BlockSpec((NB, tn), lambda j: (0, j))],
        out_specs=pl.BlockSpec((Hq, tn), lambda j: (0, j)),
        out_shape=jax.ShapeDtypeStruct((Hq, N), F32),
        compiler_params=_params("parallel"),
    )(rel_bias_t, onehot_t)


def _bias_table_bwd(dbias, onehot_t):
    L, Hq, N = dbias.shape
    NB = onehot_t.shape[0]
    tn = _tile(N, 4096)

    def body(d_ref, oh_ref, o_ref):
        @pl.when(pl.program_id(0) == 0)
        def _():
            o_ref[...] = jnp.zeros_like(o_ref)

        d = d_ref[0]
        for l in range(1, L):
            d = d + d_ref[l]
        oh = oh_ref[...]
        a1, a2, a3 = _split3(d)
        o_ref[...] += _dot(a1, oh, NT) + _dot(a2, oh, NT) + _dot(a3, oh, NT)

    return pl.pallas_call(
        body, name="rel_bias_bwd", grid=(N // tn,),
        in_specs=[pl.BlockSpec((L, Hq, tn), lambda j: (0, 0, j)), pl.BlockSpec((NB, tn), lambda j: (0, j))],
        out_specs=pl.BlockSpec((Hq, NB), lambda j: (0, 0)),
        out_shape=jax.ShapeDtypeStruct((Hq, NB), F32),
        compiler_params=_params("arbitrary"),
    )(dbias, onehot_t)


def _gate_fwd(lat, fb_col, *, name):
    S = lat.shape[1]
    tn = _tile(S, 256)

    def body(z_ref, fb_ref, o_ref, carry):
        @pl.when(pl.program_id(0) == 0)
        def _():
            carry[...] = jnp.zeros_like(carry)

        z = z_ref[...] + fb_ref[...]
        lf = jnp.minimum(z, 0.0) - jnp.log1p(jnp.exp(-jnp.abs(z)))
        r = lax.broadcasted_iota(jnp.int32, (tn, tn), 0)
        c = lax.broadcasted_iota(jnp.int32, (tn, tn), 1)
        tri = (r <= c).astype(BF16)
        a1, a2, a3 = _split3(lf)
        cum = _dot(a1, tri) + _dot(a2, tri) + _dot(a3, tri) + carry[:, 0:1]
        o_ref[...] = cum
        carry[...] = jnp.broadcast_to(cum[:, tn - 1:tn], carry.shape)

    return pl.pallas_call(
        body, name=name, grid=(S // tn,),
        in_specs=[pl.BlockSpec((GATE_ROWS, tn), lambda i: (0, i)), pl.BlockSpec((GATE_ROWS, 1), lambda i: (0, 0))],
        out_specs=pl.BlockSpec((GATE_ROWS, tn), lambda i: (0, i)),
        out_shape=jax.ShapeDtypeStruct((GATE_ROWS, S), F32),
        scratch_shapes=[pltpu.VMEM((GATE_ROWS, LANES), F32)],
        compiler_params=_params("arbitrary"),
    )(lat, fb_col)


def _gate_bwd(lat, fb_col, dF, *, name):
    S = lat.shape[1]
    tn = _tile(S, 256)
    nt = S // tn

    def body(z_ref, fb_ref, df_ref, dz_ref, dfb_ref, carry):
        @pl.when(pl.program_id(0) == 0)
        def _():
            carry[...] = jnp.zeros_like(carry)
            dfb_ref[...] = jnp.zeros_like(dfb_ref)

        r = lax.broadcasted_iota(jnp.int32, (tn, tn), 0)
        c = lax.broadcasted_iota(jnp.int32, (tn, tn), 1)
        tri = (r >= c).astype(BF16)
        a1, a2, a3 = _split3(df_ref[...])
        dlf = _dot(a1, tri) + _dot(a2, tri) + _dot(a3, tri) + carry[:, 0:1]
        carry[...] = jnp.broadcast_to(dlf[:, 0:1], carry.shape)
        z = z_ref[...] + fb_ref[...]
        row = lax.broadcasted_iota(jnp.int32, (GATE_ROWS, tn), 0)
        dz = jnp.where(row < FOX_HEADS, dlf / (1.0 + jnp.exp(z)), 0.0)
        dz_ref[...] = dz
        dfb_ref[...] += jnp.sum(dz, axis=1, keepdims=True)

    blk = pl.BlockSpec((GATE_ROWS, tn), lambda i: (0, nt - 1 - i))
    vec = pl.BlockSpec((GATE_ROWS, 1), lambda i: (0, 0))
    return pl.pallas_call(
        body, name=name, grid=(nt,),
        in_specs=[blk, vec, blk], out_specs=[blk, vec],
        out_shape=[jax.ShapeDtypeStruct((GATE_ROWS, S), F32), jax.ShapeDtypeStruct((GATE_ROWS, 1), F32)],
        scratch_shapes=[pltpu.VMEM((GATE_ROWS, LANES), F32)],
        compiler_params=_params("arbitrary"),
    )(lat, fb_col, dF)


def _rope_tables(S):
    pos = jnp.arange(S, dtype=F32)
    inv_freq = ROPE_THETA ** (-(jnp.arange(MLA_ROPE // 2, dtype=F32) * 2.0 / MLA_ROPE))
    ang = pos[:, None] * inv_freq[None, :]
    cos, sin = jnp.cos(ang).T, jnp.sin(ang).T
    z16 = jnp.zeros_like(cos)

    def slab(lo, fill):
        def put(first, second, f):
            return jnp.concatenate([jnp.full((lo, S), f, F32), first, second, jnp.full((LANES - lo - MLA_ROPE, S), f, F32)], axis=0)
        return put(cos, cos, fill), put(-sin, z16, 0.0), put(z16, sin, 0.0)

    tq = tuple(jnp.tile(t, (MLA_HEADS, 1)) for t in slab(MLA_NOPE, 1.0))
    return tq, slab(0, 0.0)


def _rope(x, c, s1, s2):
    n = x.shape[0]
    half = MLA_ROPE // 2
    return x * c + pltpu.roll(x, n - half, 0) * s1 + pltpu.roll(x, half, 0) * s2


def _rope_t(dy, c, s1, s2):
    n = dy.shape[0]
    half = MLA_ROPE // 2
    return dy * c + pltpu.roll(dy * s1, half, 0) + pltpu.roll(dy * s2, n - half, 0)


KR_SLAB0 = MLA_Q_RANK + MLA_KV_RANK


def _mla_prep_fwd(lat, g_q, g_kv, w_uq_t, w_ukv_t, tq, tmisc, *, name):
    S = lat.shape[1]
    tn = _tile(S, 512)
    QW = MLA_HEADS * MLA_PAD

    def body(lat_ref, gq_ref, gkv_ref, wq_ref, wkv_ref, c_ref, s1_ref, s2_ref, cm_ref, s1m_ref, s2m_ref,
             nq_ref, nkv_ref, q_ref, k_ref, v_ref):
        x = pltpu.roll(lat_ref[...], LAT_ROWS - LAT_SHIFT, 0)
        nq = _col_rms(x[0:MLA_Q_RANK, :], gq_ref[...]).astype(BF16)
        nkv = _col_rms(x[MLA_Q_RANK:KR_SLAB0, :], gkv_ref[...]).astype(BF16)
        nq_ref[...] = nq
        nkv_ref[...] = nkv
        q_ref[...] = _rope(_dot(wq_ref[...], nq), c_ref[...], s1_ref[...], s2_ref[...]).astype(BF16)
        kv = _dot(wkv_ref[...], nkv).astype(BF16)
        kr = _rope(x[KR_SLAB0:LAT_ROWS, :], cm_ref[...], s1m_ref[...], s2m_ref[...]).astype(BF16)
        for h in range(MLA_HEADS):
            k_ref[h * MLA_PAD:h * MLA_PAD + MLA_NOPE, :] = kv[h * LANES:h * LANES + MLA_NOPE, :]
            k_ref[h * MLA_PAD + MLA_NOPE:(h + 1) * MLA_PAD, :] = kr[0:MLA_PAD - MLA_NOPE, :]
            v_ref[h * HEAD_DIM:(h + 1) * HEAD_DIM, :] = kv[h * LANES + MLA_NOPE:(h + 1) * LANES, :]

    def col(rows):
        return pl.BlockSpec((rows, tn), lambda i: (0, i))

    def full(a):
        return pl.BlockSpec(a.shape, lambda i: (0, 0))

    return pl.pallas_call(
        body, name=name, grid=(S // tn,),
        in_specs=[col(LAT_ROWS), full(g_q), full(g_kv), full(w_uq_t), full(w_ukv_t),
                  col(QW), col(QW), col(QW), col(LANES), col(LANES), col(LANES)],
        out_specs=[col(MLA_Q_RANK), col(MLA_KV_RANK), col(QW), col(QW), col(MLA_HEADS * HEAD_DIM)],
        out_shape=[jax.ShapeDtypeStruct((MLA_Q_RANK, S), BF16), jax.ShapeDtypeStruct((MLA_KV_RANK, S), BF16),
                   jax.ShapeDtypeStruct((QW, S), BF16), jax.ShapeDtypeStruct((QW, S), BF16),
                   jax.ShapeDtypeStruct((MLA_HEADS * HEAD_DIM, S), BF16)],
        compiler_params=_params("parallel"),
    )(lat, g_q, g_kv, w_uq_t, w_ukv_t, *tq, *tmisc)


def _mla_prep_bwd(lat, nq, nkv, g_q, g_kv, w_uq_p, w_ukv, tq, tmisc, dq, dk, dv, dflog, *, name):
    S = lat.shape[1]
    tn = _tile(S, 512)
    QW = MLA_HEADS * MLA_PAD

    def body(lat_ref, nq_ref, nkv_ref, gq_ref, gkv_ref, wq_ref, wkv_ref, c_ref, s1_ref, s2_ref,
             cm_ref, s1m_ref, s2m_ref, dq_ref, dk_ref, dv_ref, dfl_ref,
             dlat_ref, dwq_ref, dwkv_ref, dgq_ref, dgkv_ref, y_s):
        @pl.when(pl.program_id(0) == 0)
        def _():
            dwq_ref[...] = jnp.zeros_like(dwq_ref)
            dwkv_ref[...] = jnp.zeros_like(dwkv_ref)
            dgq_ref[...] = jnp.zeros_like(dgq_ref)
            dgkv_ref[...] = jnp.zeros_like(dgkv_ref)

        x = pltpu.roll(lat_ref[...], LAT_ROWS - LAT_SHIFT, 0)
        dqm = _rope_t(dq_ref[...], c_ref[...], s1_ref[...], s2_ref[...]).astype(BF16)
        dwq_ref[...] += _dot(dqm, nq_ref[...], NT)
        dx, dg = _col_rms_bwd(x[0:MLA_Q_RANK, :], gq_ref[...], _dot(wq_ref[...], dqm))
        y_s[0:MLA_Q_RANK, :] = dx
        dgq_ref[...] += dg
        dkv = jnp.concatenate(
            [part for h in range(MLA_HEADS)
             for part in (dk_ref[h * MLA_PAD:h * MLA_PAD + MLA_NOPE, :], dv_ref[h * HEAD_DIM:(h + 1) * HEAD_DIM, :])],
            axis=0).astype(BF16)
        dwkv_ref[...] += _dot(dkv, nkv_ref[...], NT)
        dx, dg = _col_rms_bwd(x[MLA_Q_RANK:KR_SLAB0, :], gkv_ref[...], _dot(wkv_ref[...], dkv))
        y_s[MLA_Q_RANK:KR_SLAB0, :] = dx
        dgkv_ref[...] += dg
        dkr = dk_ref[MLA_NOPE:MLA_PAD, :]
        for h in range(1, MLA_HEADS):
            dkr = dkr + dk_ref[h * MLA_PAD + MLA_NOPE:(h + 1) * MLA_PAD, :]
        dkr = jnp.concatenate([dkr, jnp.zeros((MLA_NOPE, tn), F32)], axis=0)
        y_s[KR_SLAB0:LAT_ROWS, :] = _rope_t(dkr, cm_ref[...], s1m_ref[...], s2m_ref[...])
        y = pltpu.roll(y_s[...], LAT_SHIFT, 0)
        row = lax.broadcasted_iota(jnp.int32, (LAT_ROWS, tn), 0)
        dfl = jnp.concatenate([dfl_ref[...], jnp.zeros((LAT_ROWS - GATE_ROWS, tn), F32)], axis=0)
        dlat_ref[...] = jnp.where(row < LAT_SHIFT, dfl, y).astype(BF16)

    def col(rows):
        return pl.BlockSpec((rows, tn), lambda i: (0, i))

    def full(a):
        return pl.BlockSpec(a.shape, lambda i: (0, 0))

    def acc(r, c):
        return pl.BlockSpec((r, c), lambda i: (0, 0))

    return pl.pallas_call(
        body, name=name, grid=(S // tn,),
        in_specs=[col(LAT_ROWS), col(MLA_Q_RANK), col(MLA_KV_RANK), full(g_q), full(g_kv),
                  full(w_uq_p), full(w_ukv), col(QW), col(QW), col(QW), col(LANES), col(LANES), col(LANES),
                  col(QW), col(QW), col(MLA_HEADS * HEAD_DIM), col(GATE_ROWS)],
        out_specs=[col(LAT_ROWS), acc(QW, MLA_Q_RANK), acc(QW, MLA_KV_RANK), acc(MLA_Q_RANK, 1), acc(MLA_KV_RANK, 1)],
        out_shape=[jax.ShapeDtypeStruct((LAT_ROWS, S), BF16), jax.ShapeDtypeStruct((QW, MLA_Q_RANK), F32),
                   jax.ShapeDtypeStruct((QW, MLA_KV_RANK), F32), jax.ShapeDtypeStruct((MLA_Q_RANK, 1), F32),
                   jax.ShapeDtypeStruct((MLA_KV_RANK, 1), F32)],
        scratch_shapes=[pltpu.VMEM((LAT_ROWS, tn), F32)],
        compiler_params=_params("arbitrary"),
    )(lat, nq, nkv, g_q, g_kv, w_uq_p, w_ukv, *tq, *tmisc, dq, dk, dv, dflog)


def _dproj_cast(dqa, dkva, dqf, dkf, dvf, dlat, *, name):
    S = dqa.shape[1]
    tn = _tile(S, 512)
    parts = (dqa, dkva, dqf, dkf, dvf, dlat)

    def body(*refs):
        o_ref = refs[-1]
        r0 = 0
        for ref in refs[:-1]:
            n = ref.shape[0]
            o_ref[r0:r0 + n, :] = ref[...].astype(BF16)
            r0 += n

    return pl.pallas_call(
        body, name=name, grid=(S // tn,),
        in_specs=[pl.BlockSpec((p.shape[0], tn), lambda i: (0, i)) for p in parts],
        out_specs=pl.BlockSpec((IN_ROWS, tn), lambda i: (0, i)),
        out_shape=jax.ShapeDtypeStruct((IN_ROWS, S), BF16),
        compiler_params=_params("parallel"),
    )(*parts)


GELU_C = math.sqrt(2.0 / math.pi)
GELU_A = 0.044715


HALO = 16


def _shift_down(a, k, fill):
    r = pltpu.roll(a, k, 0)
    row = lax.broadcasted_iota(jnp.int32, (8, a.shape[1]), 0)
    head = r[0:8, :]
    for i in range(k):
        head = jnp.where(row == i, fill[len(fill) - k + i], head)
    return jnp.concatenate([head, r[8:, :]], axis=0)


def _shift_up(d, k, fill):
    n = d.shape[0]
    r = pltpu.roll(d, n - k, 0)
    row = lax.broadcasted_iota(jnp.int32, (8, d.shape[1]), 0)
    tail = r[n - 8:n, :]
    for i in range(k):
        tail = jnp.where(row == 8 - k + i, fill[i], tail)
    return jnp.concatenate([r[0:n - 8, :], tail], axis=0)


def _conv_taps(a, before, w_ref, b_ref):
    a1 = _shift_down(a, 1, before)
    a2 = _shift_down(a, 2, before)
    u = ((b_ref[...] + w_ref[0:1, :] * a2) + w_ref[1:2, :] * a1) + w_ref[2:3, :] * a
    return u, a1, a2


def _rows_before(halo_ref, first):
    h = halo_ref[HALO - 2:HALO, :].astype(F32)
    return jnp.where(first, 0.0, h[0:1, :]), jnp.where(first, 0.0, h[1:2, :])


def _conv_specs(S, tm, tc, nc):
    hb = tm // HALO
    main = lambda off: pl.BlockSpec((tm, tc), lambda j, i: (i, j + off))
    prev = lambda off: pl.BlockSpec((HALO, tc), lambda j, i: (jnp.maximum(i * hb - 1, 0), j + off))
    nxt = lambda off: pl.BlockSpec((HALO, tc), lambda j, i: (jnp.minimum((i + 1) * hb, S // HALO - 1), j + off))
    wspec = lambda off: pl.BlockSpec((3, tc), lambda j, i: (0, j + off))
    bspec = lambda off: pl.BlockSpec((1, tc), lambda j, i: (0, j + off))
    return main, prev, nxt, wspec, bspec


def _conv_geglu_fwd(a, conv_w, conv_b, *, name):
    S = a.shape[0]
    tm, tc = _tile(S, 512), _tile(D_FF, 1408)
    nc = D_FF // tc
    main, prev, _, wspec, bspec = _conv_specs(S, tm, tc, nc)

    def body(ag_ref, au_ref, hg_ref, hu_ref, wg_ref, wu_ref, bg_ref, bu_ref, z_ref):
        first = pl.program_id(1) == 0
        gate, _, _ = _conv_taps(ag_ref[...].astype(F32), _rows_before(hg_ref, first), wg_ref, bg_ref)
        up, _, _ = _conv_taps(au_ref[...].astype(F32), _rows_before(hu_ref, first), wu_ref, bu_ref)
        cdf = 0.5 * (1.0 + jnp.tanh(GELU_C * (gate + GELU_A * (gate * gate * gate))))
        z_ref[...] = (gate * cdf * up).astype(BF16)

    return pl.pallas_call(
        body, name=name, grid=(nc, S // tm),
        in_specs=[main(0), main(nc), prev(0), prev(nc), wspec(0), wspec(nc), bspec(0), bspec(nc)],
        out_specs=pl.BlockSpec((tm, tc), lambda j, i: (i, j)),
        out_shape=jax.ShapeDtypeStruct((S, D_FF), BF16),
        compiler_params=_params("parallel", "arbitrary"),
    )(a, a, a, a, conv_w, conv_w, conv_b, conv_b)


def _geglu_bwd(gate, up, dz):
    g2x = gate * gate
    th = jnp.tanh(GELU_C * (gate + GELU_A * (g2x * gate)))
    cdf = 0.5 * (1.0 + th)
    dgelu = cdf + gate * (0.5 * (1.0 - th * th) * (GELU_C * (1.0 + 3.0 * GELU_A * g2x)))
    return dz * up * dgelu, dz * (gate * cdf)


def _conv_geglu_bwd(a, conv_w, conv_b, dz, *, name):
    S = a.shape[0]
    tm, tc = _tile(S, 512), _tile(D_FF, 1408)
    nc = D_FF // tc
    nr = S // tm
    main, prev, nxt, wspec, bspec = _conv_specs(S, tm, tc, nc)

    def body(ag_ref, au_ref, pg_ref, pu_ref, ng_ref, nu_ref, wg_ref, wu_ref, bg_ref, bu_ref, dz_ref, dzn_ref,
             da_ref, dw_ref, db_ref):
        i = pl.program_id(1)
        first, last = i == 0, i == nr - 1

        @pl.when(first)
        def _():
            dw_ref[...] = jnp.zeros_like(dw_ref)
            db_ref[...] = jnp.zeros_like(db_ref)

        ag, au = ag_ref[...].astype(F32), au_ref[...].astype(F32)
        gate, g1, g2 = _conv_taps(ag, _rows_before(pg_ref, first), wg_ref, bg_ref)
        up, u1, u2 = _conv_taps(au, _rows_before(pu_ref, first), wu_ref, bu_ref)
        dug, duu = _geglu_bwd(gate, up, dz_ref[...])
        gate_n, _, _ = _conv_taps(ng_ref[...].astype(F32), (ag[tm - 2:tm - 1, :], ag[tm - 1:tm, :]), wg_ref, bg_ref)
        up_n, _, _ = _conv_taps(nu_ref[...].astype(F32), (au[tm - 2:tm - 1, :], au[tm - 1:tm, :]), wu_ref, bu_ref)
        dug_n, duu_n = _geglu_bwd(gate_n, up_n, dzn_ref[...])
        for half, du, du_n, w_ref, taps in ((0, dug, dug_n, wg_ref, (g2, g1, ag)), (1, duu, duu_n, wu_ref, (u2, u1, au))):
            after = (jnp.where(last, 0.0, du_n[0:1, :]), jnp.where(last, 0.0, du_n[1:2, :]))
            d1, d2 = _shift_up(du, 1, after), _shift_up(du, 2, after)
            da_ref[half] = (w_ref[2:3, :] * du + w_ref[1:2, :] * d1 + w_ref[0:1, :] * d2).astype(BF16)
            for tap in range(3):
                dw_ref[half, tap:tap + 1, :] += jnp.sum(du * taps[tap], axis=0, keepdims=True)
            db_ref[half] += jnp.sum(du, axis=0, keepdims=True)

    hb = tm // HALO
    return pl.pallas_call(
        body, name=name, grid=(nc, nr),
        in_specs=[main(0), main(nc), prev(0), prev(nc), nxt(0), nxt(nc), wspec(0), wspec(nc), bspec(0), bspec(nc),
                  pl.BlockSpec((tm, tc), lambda j, i: (i, j)),
                  pl.BlockSpec((HALO, tc), lambda j, i: (jnp.minimum((i + 1) * hb, S // HALO - 1), j))],
        out_specs=[pl.BlockSpec((2, tm, tc), lambda j, i: (0, i, j)), pl.BlockSpec((2, 3, tc), lambda j, i: (0, 0, j)),
                   pl.BlockSpec((2, 1, tc), lambda j, i: (0, 0, j))],
        out_shape=[jax.ShapeDtypeStruct((2, S, D_FF), BF16), jax.ShapeDtypeStruct((2, 3, D_FF), F32),
                   jax.ShapeDtypeStruct((2, 1, D_FF), F32)],
        compiler_params=_params("parallel", "arbitrary"),
    )(a, a, a, a, a, a, conv_w, conv_w, conv_b, conv_b, dz, dz)


def _adamw_update(w, g, m, v):
    m = ADAM_B1 * m + (1.0 - ADAM_B1) * g
    v = ADAM_B2 * v + (1.0 - ADAM_B2) * jnp.square(g)
    m_hat = m / (1.0 - ADAM_B1 ** ADAM_STEP)
    v_hat = v / (1.0 - ADAM_B2 ** ADAM_STEP)
    return -ADAM_LR * (m_hat / (jnp.sqrt(v_hat) + ADAM_EPS) + ADAM_WD * w), m, v


def _adamw(w, g, m, v, *, name):
    L, A, B = w.shape
    ta = _tile(A, ROW_TILE)

    def body(w_ref, g_ref, m_ref, v_ref, d_ref, mo_ref, vo_ref):
        d_ref[...], mo_ref[...], vo_ref[...] = _adamw_update(w_ref[...], g_ref[...], m_ref[...], v_ref[...])

    blk = pl.BlockSpec((None, ta, B), lambda l, i: (l, i, 0))
    shp = jax.ShapeDtypeStruct((L, A, B), F32)
    return pl.pallas_call(
        body, name=name, grid=(L, A // ta),
        in_specs=[blk] * 4, out_specs=[blk] * 3, out_shape=[shp] * 3,
        compiler_params=_params("parallel", "parallel"),
    )(w, g, m, v)


def _scalar(v):
    return jnp.reshape(v, (1,)).astype(jnp.int32)


def _adamw_halves(w, g_mine, g_other, m, v, *, name):
    L, A, B = w.shape
    ta = _tile(A // 2, ROW_TILE)
    nb = A // 2 // ta

    def body(c_ref, w_ref, gm_ref, go_ref, m_ref, v_ref, g_ref, d_ref, mo_ref, vo_ref):
        g = jnp.where(pl.program_id(1) // nb == c_ref[0], gm_ref[...], go_ref[...])
        g_ref[...] = g
        d_ref[...], mo_ref[...], vo_ref[...] = _adamw_update(w_ref[...], g, m_ref[...], v_ref[...])

    blk = pl.BlockSpec((None, ta, B), lambda l, i, c_ref: (l, i, 0))
    half = pl.BlockSpec((None, ta, B), lambda l, i, c_ref: (l, i % nb, 0))
    shp = jax.ShapeDtypeStruct((L, A, B), F32)
    return pl.pallas_call(
        body, name=name,
        grid_spec=pltpu.PrefetchScalarGridSpec(num_scalar_prefetch=1, grid=(L, A // ta),
                                               in_specs=[blk, half, half, blk, blk], out_specs=[blk] * 4),
        out_shape=[shp] * 4,
        compiler_params=_params("parallel", "parallel"),
    )(_scalar(lax.axis_index("c")), w, g_mine, g_other, m, v)


def _chip_index():
    return 2 * lax.axis_index("x") + lax.axis_index("y")


def _pair_sum(g, recv, *, name):
    n, A, B = g.shape
    ta = _tile(A // 2, ROW_TILE)
    nb = A // 2 // ta

    def body(c_ref, g_ref, r_ref, o_ref):
        o_ref[...] = g_ref[...] + r_ref[...]

    return pl.pallas_call(
        body, name=name,
        grid_spec=pltpu.PrefetchScalarGridSpec(
            num_scalar_prefetch=1, grid=(n, nb),
            in_specs=[pl.BlockSpec((None, ta, B), lambda s, r, c_ref: (s, c_ref[0] * nb + r, 0)),
                      pl.BlockSpec((None, ta, B), lambda s, r, c_ref: (s, r, 0))],
            out_specs=pl.BlockSpec((None, ta, B), lambda s, r, c_ref: (s, r, 0))),
        out_shape=jax.ShapeDtypeStruct((n, A // 2, B), F32),
        compiler_params=_params("parallel", "parallel"),
    )(_scalar(lax.axis_index("c")), g, recv)


def _chip_sum(landed, own, *, name):
    n, A2, B = landed.shape
    ta = _tile(A2, ROW_TILE)

    def body(me_ref, *refs):
        slots, own_ref, o_ref = refs[:n], refs[n], refs[n + 1]
        parts = [jnp.where(me_ref[0] == s, own_ref[...], slots[s][...]) for s in range(n)]
        o_ref[...] = ((parts[0] + parts[1]) + parts[2]) + parts[3]

    def slot(s):
        return pl.BlockSpec((None, ta, B), lambda r, me_ref: (jnp.where(me_ref[0] == s, (s + 1) % n, s), r, 0))

    return pl.pallas_call(
        body, name=name,
        grid_spec=pltpu.PrefetchScalarGridSpec(
            num_scalar_prefetch=1, grid=(A2 // ta,),
            in_specs=[slot(s) for s in range(n)] + [pl.BlockSpec((None, ta, B), lambda r, me_ref: (me_ref[0], r, 0))],
            out_specs=pl.BlockSpec((ta, B), lambda r, me_ref: (r, 0))),
        out_shape=jax.ShapeDtypeStruct((A2, B), F32),
        compiler_params=_params("parallel"),
    )(_scalar(_chip_index()), *([landed] * n), own)


HBM_SPEC = pl.BlockSpec(memory_space=pl.ANY)
COMM_PARAMS = pltpu.CompilerParams(has_side_effects=True)


def _mesh_pos():
    return lax.axis_index("x"), lax.axis_index("y"), lax.axis_index("c")


def _other_chips(x, y):
    return [(1 - x, y), (x, 1 - y), (1 - x, 1 - y)]


def _remote(src, dst, send_sems, recv_sems, k, to):
    return pltpu.make_async_remote_copy(src_ref=src, dst_ref=dst, send_sem=send_sems.at[k], recv_sem=recv_sems.at[k],
                                        device_id=to, device_id_type=MESH)


def _place_own(gathered, shard, *, name):
    A, B = shard.shape
    ta = _tile(A, ROW_TILE)

    def body(me_ref, s_ref, g_ref, o_ref):
        o_ref[...] = s_ref[...]

    return pl.pallas_call(
        body, name=name,
        grid_spec=pltpu.PrefetchScalarGridSpec(
            num_scalar_prefetch=1, grid=(A // ta,),
            in_specs=[pl.BlockSpec((ta, B), lambda r, me_ref: (r, 0)), HBM_SPEC],
            out_specs=pl.BlockSpec((None, ta, B), lambda r, me_ref: (me_ref[0], r, 0))),
        out_shape=jax.ShapeDtypeStruct(gathered.shape, gathered.dtype),
        input_output_aliases={2: 0},
        compiler_params=_params("parallel"),
    )(_scalar(_chip_index()), shard, gathered)


def _half_rows(rows, c, align=8):
    assert (rows // 2) % align == 0
    return pl.ds(pl.multiple_of(c * (rows // 2), align), rows // 2)


BF16_ROWS = 16


def _halved(rows):
    return rows % (2 * BF16_ROWS) == 0


def _gather_copies(srcs, lands, send_sems, recv_sems):
    x, y, c = _mesh_pos()
    me = 2 * x + y
    out = []
    for k in range(len(srcs)):
        a = srcs[k].shape[0]
        rows = _half_rows(a, c, BF16_ROWS) if _halved(a) else pl.ds(0, a)
        for j, (px, py) in enumerate(_other_chips(x, y)):
            send = _remote(srcs[k].at[rows], lands[k].at[me, rows], send_sems, recv_sems, 3 * k + j, (px, py, c))
            recv = _remote(srcs[k].at[rows], lands[k].at[2 * px + py, rows], send_sems, recv_sems, 3 * k + j, (px, py, c))
            out.append((send, recv))
    return out


def _gather_start(srcs):
    nl, n = len(srcs), len(srcs[0])
    lands = [[lax.empty((N_CHIPS,) + s.shape, s.dtype) for s in sl] for sl in srcs]
    flat = [a for l in range(nl) for a in srcs[l] + lands[l]]

    def body(*refs):
        bufs, sems, token = refs[:len(flat)], refs[len(flat):len(flat) + 2 * nl], refs[-1]
        for l in range(nl):
            mine = bufs[2 * n * l:2 * n * (l + 1)]
            for send, _ in _gather_copies(mine[:n], mine[n:], sems[2 * l], sems[2 * l + 1]):
                send.start()
        token[...] = jnp.zeros_like(token)

    res = pl.pallas_call(
        body, name="weight_gather_start",
        in_specs=[HBM_ONLY] * len(flat),
        out_specs=[SEM_SPEC] * (2 * nl) + [HBM_ONLY] * len(flat) + [pl.BlockSpec(memory_space=pltpu.VMEM)],
        out_shape=[pltpu.SemaphoreType.DMA((3 * n,))] * (2 * nl) + [pltpu.HBM(a.shape, a.dtype) for a in flat]
        + [jax.ShapeDtypeStruct((8, LANES), F32)],
        input_output_aliases={i: 2 * nl + i for i in range(len(flat))},
        compiler_params=SPLIT_PARAMS,
    )(*[pltpu.with_memory_space_constraint(a, pltpu.HBM) for a in flat])
    bufs = res[2 * nl:2 * nl + len(flat)]
    state = [(res[2 * l], res[2 * l + 1], list(bufs[2 * n * l:2 * n * l + n]), list(bufs[2 * n * l + n:2 * n * (l + 1)]))
             for l in range(nl)]
    return state, res[-1][0:1, 0:1]


def _gather_wait(state, after, *, name):
    send_sems, recv_sems, srcs, lands = state
    n = len(srcs)

    def body(*refs):
        for send, recv in _gather_copies(refs[:n], refs[n:2 * n], refs[2 * n], refs[2 * n + 1]):
            send.wait_send()
            recv.wait_recv()

    res = pl.pallas_call(
        body, name=name,
        in_specs=[HBM_ONLY] * (2 * n) + [SEM_SPEC, SEM_SPEC, HBM_SPEC],
        out_specs=[HBM_ONLY] * (2 * n),
        out_shape=[pltpu.HBM(a.shape, a.dtype) for a in srcs + lands],
        input_output_aliases={i: i for i in range(2 * n)},
        compiler_params=SPLIT_PARAMS,
    )(*srcs, *lands, send_sems, recv_sems, after)
    return list(res[:n]), list(res[n:])


def _gather_forward(lands, *, name):
    n = len(lands)

    def body(*refs):
        bufs, outs = refs[:n], refs[n:2 * n]
        send_sems, recv_sems = refs[2 * n:]
        x, y, c = _mesh_pos()
        copies, waits = [], []
        for k in range(n):
            a = lands[k].shape[1]
            if not _halved(a):
                continue
            for j, (px, py) in enumerate(_other_chips(x, y)):
                mine = 2 * px + py, _half_rows(a, c, BF16_ROWS)
                copies.append(_remote(bufs[k].at[mine], outs[k].at[mine], send_sems, recv_sems, 3 * k + j, (x, y, 1 - c)))
                lands_here = outs[k].at[2 * px + py, _half_rows(a, 1 - c, BF16_ROWS)]
                waits.append(_remote(lands_here, lands_here, send_sems, recv_sems, 3 * k + j, (x, y, 1 - c)))
        for cp in copies:
            cp.start()
        for cp in waits:
            cp.wait_recv()
        for cp in copies:
            cp.wait_send()

    return pl.pallas_call(
        body, name=name,
        in_specs=[HBM_SPEC] * n, out_specs=[HBM_SPEC] * n,
        out_shape=[jax.ShapeDtypeStruct(a.shape, a.dtype) for a in lands],
        scratch_shapes=[pltpu.SemaphoreType.DMA((3 * n,)), pltpu.SemaphoreType.DMA((3 * n,))],
        input_output_aliases={i: i for i in range(n)},
        compiler_params=COMM_PARAMS,
    )(*lands)


def _sibling_exchange(gs, *, name):
    n = len(gs)

    def body(*refs):
        ins, outs = refs[:n], refs[n:2 * n]
        send_sems, recv_sems = refs[2 * n:]
        x, y, c = _mesh_pos()
        copies = [_remote(ins[k].at[:, _half_rows(gs[k].shape[1], 1 - c)], outs[k], send_sems, recv_sems, k, (x, y, 1 - c))
                  for k in range(n)]
        for cp in copies:
            cp.start()
        for cp in copies:
            cp.wait()

    return pl.pallas_call(
        body, name=name,
        in_specs=[HBM_SPEC] * n, out_specs=[HBM_SPEC] * n,
        out_shape=[jax.ShapeDtypeStruct((g.shape[0], g.shape[1] // 2, g.shape[2]), g.dtype) for g in gs],
        scratch_shapes=[pltpu.SemaphoreType.DMA((n,)), pltpu.SemaphoreType.DMA((n,))],
        compiler_params=COMM_PARAMS,
    )(*gs)


HBM_ONLY = pl.BlockSpec(memory_space=pltpu.HBM)
SEM_SPEC = pl.BlockSpec(memory_space=pltpu.SEMAPHORE)
SPLIT_PARAMS = pltpu.CompilerParams(has_side_effects=pltpu.SideEffectType.DATAFLOW_SIDE_EFFECTING)


def _scatter_copies(srcs, lands, send_sems, recv_sems):
    x, y, c = _mesh_pos()
    me = 2 * x + y
    out = []
    for k in range(len(srcs)):
        for j, (px, py) in enumerate(_other_chips(x, y)):
            s = 2 * px + py
            send = _remote(srcs[k].at[s], lands[k].at[me], send_sems, recv_sems, 3 * k + j, (px, py, c))
            recv = _remote(srcs[k].at[s], lands[k].at[s], send_sems, recv_sems, 3 * k + j, (px, py, c))
            out.append((send, recv))
    return out


def _scatter_start(ps, *, name):
    n = len(ps)
    lands = [lax.empty(p.shape, p.dtype) for p in ps]

    def body(*refs):
        srcs, zones = refs[:n], refs[n:2 * n]
        send_sems, recv_sems, token = refs[2 * n], refs[2 * n + 1], refs[-1]
        for send, _ in _scatter_copies(srcs, zones, send_sems, recv_sems):
            send.start()
        token[...] = jnp.zeros_like(token)

    hbm = lambda a: pltpu.HBM(a.shape, a.dtype)
    res = pl.pallas_call(
        body, name=name,
        in_specs=[HBM_ONLY] * (2 * n),
        out_specs=[SEM_SPEC, SEM_SPEC] + [HBM_ONLY] * (2 * n) + [pl.BlockSpec(memory_space=pltpu.VMEM)],
        out_shape=[pltpu.SemaphoreType.DMA((3 * n,)), pltpu.SemaphoreType.DMA((3 * n,))] + [hbm(a) for a in ps + lands]
        + [jax.ShapeDtypeStruct((8, LANES), F32)],
        input_output_aliases={i: 2 + i for i in range(2 * n)},
        compiler_params=SPLIT_PARAMS,
    )(*[pltpu.with_memory_space_constraint(a, pltpu.HBM) for a in ps + lands])
    return (res[0], res[1], list(res[2:2 + n]), list(res[2 + n:2 + 2 * n])), res[-1][0:1, 0:1]


def _scatter_wait(started, after, *, name):
    ng = len(started)
    sizes = [len(st[2]) for st in started]
    offs = [2 * sum(sizes[:i]) for i in range(ng + 1)]
    flat = [a for (_, _, ps, lands) in started for a in ps + lands]

    def body(*refs):
        bufs, sems = refs[:len(flat)], refs[len(flat):len(flat) + 2 * ng]
        for i, n in enumerate(sizes):
            srcs, zones = bufs[offs[i]:offs[i] + n], bufs[offs[i] + n:offs[i + 1]]
            for send, recv in _scatter_copies(srcs, zones, sems[2 * i], sems[2 * i + 1]):
                send.wait_send()
                recv.wait_recv()

    res = pl.pallas_call(
        body, name=name,
        in_specs=[HBM_ONLY] * len(flat) + [SEM_SPEC] * (2 * ng) + [HBM_SPEC],
        out_specs=[HBM_ONLY] * len(flat),
        out_shape=[pltpu.HBM(a.shape, a.dtype) for a in flat],
        input_output_aliases={i: i for i in range(len(flat))},
        compiler_params=SPLIT_PARAMS,
    )(*flat, *[s for (ss, rs, _, _) in started for s in (ss, rs)], after)
    return [(list(res[offs[i]:offs[i] + n]), list(res[offs[i] + n:offs[i + 1]])) for i, n in enumerate(sizes)]


def _sibling_share(hs):
    n = len(hs)

    def body(*refs):
        ins, outs = refs[:n], refs[n:2 * n]
        send_sems, recv_sems = refs[2 * n:]
        x, y, c = _mesh_pos()
        copies = [_remote(ins[k], outs[k], send_sems, recv_sems, k, (x, y, 1 - c)) for k in range(n)]
        for cp in copies:
            cp.start()
        for cp in copies:
            cp.wait()

    return pl.pallas_call(
        body, name="grad_sibling_share",
        in_specs=[HBM_SPEC] * n, out_specs=[HBM_SPEC] * n,
        out_shape=[jax.ShapeDtypeStruct(h.shape, h.dtype) for h in hs],
        scratch_shapes=[pltpu.SemaphoreType.DMA((n,)), pltpu.SemaphoreType.DMA((n,))],
        compiler_params=COMM_PARAMS,
    )(*hs)


def _allreduce_small(part):
    rows, C = part.shape

    def body(p_ref, o_ref, slots, send_sems, recv_sems):
        x, y, c = _mesh_pos()
        me = 4 * x + 2 * y + c
        slots[me] = p_ref[...]
        copies = []
        for k in range(1, 8):
            kx, ky, kc = (k >> 2) & 1, (k >> 1) & 1, k & 1
            peer = (x ^ kx if kx else x, y ^ ky if ky else y, c ^ kc if kc else c)
            cp = _remote(p_ref, slots.at[me], send_sems, recv_sems, k - 1, peer)
            cp.start()
            copies.append((cp, peer))
        for k, (cp, peer) in enumerate(copies):
            src = 4 * peer[0] + 2 * peer[1] + peer[2]
            _remote(p_ref, slots.at[src], send_sems, recv_sems, k, peer).wait_recv()
        for cp, _ in copies:
            cp.wait_send()
        total = slots[0]
        for d in range(1, 8):
            total = total + slots[d]
        o_ref[...] = total

    return pl.pallas_call(
        body, name="small_grad_allreduce",
        in_specs=[pl.BlockSpec(memory_space=pltpu.VMEM)], out_specs=pl.BlockSpec(memory_space=pltpu.VMEM),
        out_shape=jax.ShapeDtypeStruct((rows, C), F32),
        scratch_shapes=[pltpu.VMEM((8, rows, C), F32), pltpu.SemaphoreType.DMA((7,)), pltpu.SemaphoreType.DMA((7,))],
        compiler_params=pltpu.CompilerParams(has_side_effects=True, vmem_limit_bytes=VMEM_LIMIT_BYTES),
    )(part)


def _pad_w_uq(w):
    lead = w.shape[:-1]
    w = w.reshape(lead + (MLA_HEADS, MLA_QK))
    w = jnp.concatenate([w, jnp.zeros(lead + (MLA_HEADS, MLA_PAD - MLA_QK), w.dtype)], axis=-1)
    return w.reshape(lead + (MLA_HEADS * MLA_PAD,))


def _unpad_w_uq(g):
    lead = g.shape[:-1]
    return g.reshape(lead + (MLA_HEADS, MLA_PAD))[..., :MLA_QK].reshape(lead + (MLA_HEADS * MLA_QK,))


def _t(a):
    return jnp.swapaxes(a, -1, -2)


def _shards_of_cols(w):
    A, NB = w.shape
    return w.reshape(A, N_CHIPS, NB // N_CHIPS).transpose(1, 0, 2)


BIG = ("w_in", "w_uq", "w_ukv", "w_out", "w_up", "w_down")
SMALL = ("attn_pre_norm", "forget_bias", "swa_sinks", "rel_bias", "q_latent_norm", "kv_latent_norm", "group_norm",
         "attn_post_norm", "ffn_pre_norm", "conv_b", "ffn_post_norm")
WEIGHTS = ("attn_pre_norm", "w_in", "forget_bias", "swa_sinks", "rel_bias", "q_latent_norm", "w_uq", "kv_latent_norm",
           "w_ukv", "group_norm", "w_out", "attn_post_norm", "ffn_pre_norm", "w_up", "conv_w", "conv_b", "w_down",
           "ffn_post_norm")


def _pack(arrs, cols, row_mult):
    flat = jnp.concatenate([a.reshape(-1) for a in arrs])
    n = flat.shape[0]
    per = cols * row_mult
    total = -(-n // per) * per
    return jnp.pad(flat, (0, total - n)).reshape(total // cols, cols)


def _unpack(packed, shapes):
    flat = packed.reshape(-1)
    out, off = [], 0
    for shp in shapes:
        n = int(np.prod(shp))
        out.append(flat[off:off + n].reshape(shp))
        off += n
    return out


LAYER_KEYS = ("w_qkv_t", "w_lat_t", "w_in_t", "w_uq_p", "w_uq_t", "w_ukv", "w_ukv_t", "w_out", "w_up", "w_down", "conv_w")


def _layer_weights(gathered):
    cols = lambda g: g.transpose(1, 0, 2).reshape(g.shape[1], N_CHIPS * g.shape[2])
    w_in_t = _t(gathered["w_in"]).reshape(IN_COLS, D_MODEL)
    w_in_t = jnp.pad(w_in_t, ((0, IN_ROWS - IN_COLS), (0, 0)))
    w_uq_p = _pad_w_uq(cols(gathered["w_uq"]))
    w_ukv = cols(gathered["w_ukv"])
    return dict(w_qkv_t=w_in_t[:QKV_ROWS], w_lat_t=w_in_t[QKV_ROWS:], w_in_t=w_in_t, w_uq_p=w_uq_p, w_uq_t=_t(w_uq_p),
                w_ukv=w_ukv, w_ukv_t=_t(w_ukv), w_out=gathered["w_out"].reshape(D_MODEL, D_MODEL), w_up=gathered["w_up"],
                w_down=gathered["w_down"].reshape(D_FF, D_MODEL), conv_w=cols(gathered["conv_w"]))


def _local_step(x, target, W, layer_weights, layer_done):
    W = dict(W, **{key: [None] * DEPTH for key in LAYER_KEYS})
    S = x.shape[0]
    tq_tabs, tm_tabs = _rope_tables(S)
    onehot_t = _rel_onehot_t()
    bias_t = _bias_table(W["rel_bias"].T, onehot_t).reshape(SWA_KV_HEADS, SWA_GROUP, 2 * WINDOW, WINDOW)
    bias_t = bias_t.transpose(0, 2, 1, 3).reshape(SWA_KV_HEADS, 2 * WINDOW, GW)
    row = lambda a: a.reshape(1, -1)
    col = lambda a: a.reshape(-1, 1)
    fox_rows = (FOX_ROW0, FOX_ROW0 + FOX_HEADS * HEAD_DIM, FOX_ROW0 + 2 * FOX_HEADS * HEAD_DIM, SWA_Q_HEADS)
    fox = dict(rows=fox_rows, H=FOX_HEADS, Dk=HEAD_DIM, Dv=HEAD_DIM, scale=HEAD_DIM ** -0.5)
    mla = dict(rows=(0, 0, 0, SWA_Q_HEADS + FOX_HEADS), H=MLA_HEADS, Dk=MLA_PAD, Dv=HEAD_DIM, scale=MLA_QK ** -0.5)

    saved = []
    h = _rms_fwd(x, row(W["attn_pre_norm"][0]), name="rms_in")
    for l in range(DEPTH):
        sv = {"x0": x, "h1": h}
        for key, val in layer_weights(l, h).items():
            W[key][l] = val
        qkv = _matmul(W["w_qkv_t"][l], h, tb=True, out_dtype=BF16, name="proj_qkv")
        lat = _matmul(W["w_lat_t"][l], h, tb=True, name="proj_lat")
        oa, lse_a = _swa_fwd(qkv, bias_t, W["swa_sinks"][l], name="swa_fwd")
        fb_col = jnp.pad(col(W["forget_bias"][l]), ((0, GATE_ROWS - FOX_HEADS), (0, 0)))
        f4 = _gate_fwd(lat, fb_col, name="fox_gate_fwd")[:FOX_HEADS]
        f_row, f_col = f4[:, None, :], f4.T
        of, lse_f = _attn_fwd(qkv, qkv, qkv, f_row=f_row, f_col=f_col, name="fox_fwd", **fox)
        nq, nkv, qm, km, vm = _mla_prep_fwd(lat, col(W["q_latent_norm"][l]), col(W["kv_latent_norm"][l]), W["w_uq_t"][l],
                                            W["w_ukv_t"][l], tq_tabs, tm_tabs, name="mla_prep_fwd")
        oc, lse_c = _attn_fwd(qm, km, vm, name="mla_fwd", **mla)
        mixed = _group_norm_fwd(oa, of, oc, col(W["group_norm"][l]), name="group_norm_fwd")
        y = _matmul(mixed, W["w_out"][l], ta=True, name="proj_out")
        x1, h2 = _resid_rms(x, y, row(W["attn_post_norm"][l]), row(W["ffn_pre_norm"][l]), name="attn_resid")
        a = _matmul(h2, W["w_up"][l], b_shards=True, out_dtype=BF16, name="ffn_up")
        z = _conv_geglu_fwd(a, W["conv_w"][l], row(W["conv_b"][l]), name="conv_geglu_fwd")
        y2 = _matmul(z, W["w_down"][l], name="ffn_down")
        g_next = row(W["attn_pre_norm"][l + 1]) if l + 1 < DEPTH else None
        x2, h_next = _resid_rms(x1, y2, row(W["ffn_post_norm"][l]), g_next, name="ffn_resid")
        sv.update(qkv=qkv, lat=lat, oa=oa, lse_a=lse_a, fb_col=fb_col, f_row=f_row, f_col=f_col, of=of, lse_f=lse_f,
                  nq=nq, nkv=nkv, qm=qm, km=km, vm=vm, oc=oc, lse_c=lse_c, mixed=mixed, y=y, x1=x1, h2=h2, a=a, z=z, y2=y2)
        saved.append(sv)
        x, h = x2, h_next

    loss, dx = _loss_head(x, target)

    G = {k: [None] * DEPTH for k in WEIGHTS if k != "rel_bias" and k not in BIG}
    dbias_layers = [None] * DEPTH
    for l in reversed(range(DEPTH)):
        sv = saved[l]
        gb = {}
        dy2, dg = _rms_bwd(sv["y2"], row(W["ffn_post_norm"][l]), dx, out_dtype=BF16, name="ffn_post_bwd")
        G["ffn_post_norm"][l] = dg[0]
        dz = _matmul(dy2, W["w_down"][l], tb=True, name="ffn_down_dx")
        gb["w_down"] = _matmul(sv["z"], dy2, ta=True, name="ffn_down_dw").reshape(N_CHIPS, D_FF // N_CHIPS, D_MODEL)
        da, dcw, dcb = _conv_geglu_bwd(sv["a"], W["conv_w"][l], row(W["conv_b"][l]), dz, name="conv_geglu_bwd")
        G["conv_w"][l] = dcw.transpose(1, 0, 2).reshape(3, 2 * D_FF)
        G["conv_b"][l] = dcb.reshape(2 * D_FF)
        dh2 = _matmul(da, W["w_up"][l], tb=True, b_shards=True, a_halves=True, name="ffn_up_dx")
        gb["w_up"] = _matmul(sv["h2"], da, ta=True, out_shards=True, b_halves=True, name="ffn_up_dw")
        token = layer_done(l, gb)
        gb = {}
        dx1, dg = _rms_bwd(sv["x1"], row(W["ffn_pre_norm"][l]) + token, dh2, resid=dx, out_dtype=F32, name="ffn_pre_bwd")
        G["ffn_pre_norm"][l] = dg[0]
        dy, dg = _rms_bwd(sv["y"], row(W["attn_post_norm"][l]), dx1, out_dtype=BF16, name="attn_post_bwd")
        G["attn_post_norm"][l] = dg[0]
        dmixed = _matmul(W["w_out"][l], dy, tb=True, name="proj_out_dx")
        gb["w_out"] = _matmul(sv["mixed"], dy, name="proj_out_dw").reshape(N_CHIPS, D_MODEL // N_CHIPS, D_MODEL)
        doa, dof, doc, dg, delta = _group_norm_bwd(sv["oa"], sv["of"], sv["oc"], col(W["group_norm"][l]), dmixed,
                                                   name="group_norm_bwd")
        G["group_norm"][l] = dg[:, 0]
        dqa, dkva, dbias_l, dsink = _swa_bwd(sv["qkv"], bias_t, W["swa_sinks"][l], doa, sv["lse_a"],
                                             delta.reshape(-1, S), name="swa_bwd")
        dbias_layers[l] = (dbias_l.reshape(SWA_KV_HEADS, 2 * WINDOW, SWA_GROUP, WINDOW).transpose(0, 2, 1, 3)
                           .reshape(SWA_Q_HEADS, -1))
        G["swa_sinks"][l] = dsink[:, 0]
        dqf, dkf, dvf, dfk = _attn_bwd(sv["qkv"], sv["qkv"], sv["qkv"], do=dof, lse=sv["lse_f"], delta=delta,
                                       f_row=sv["f_row"], f_col=sv["f_col"], name="fox_bwd", **fox)
        dF = jnp.pad(dfk.T, ((0, GATE_ROWS - FOX_HEADS), (0, 0)))
        dflog, dfb = _gate_bwd(sv["lat"], sv["fb_col"], dF, name="fox_gate_bwd")
        G["forget_bias"][l] = dfb[:FOX_HEADS, 0]
        dqm, dkm, dvm = _attn_bwd(sv["qm"], sv["km"], sv["vm"], do=doc, lse=sv["lse_c"], delta=delta, name="mla_bwd", **mla)
        dlat, dwq_t, dwkv_t, dgq, dgkv = _mla_prep_bwd(
            sv["lat"], sv["nq"], sv["nkv"], col(W["q_latent_norm"][l]), col(W["kv_latent_norm"][l]), W["w_uq_p"][l],
            W["w_ukv"][l], tq_tabs, tm_tabs, dqm, dkm, dvm, dflog, name="mla_prep_bwd")
        gb["w_uq"], gb["w_ukv"] = _shards_of_cols(_unpad_w_uq(dwq_t.T)), _shards_of_cols(dwkv_t.T)
        G["q_latent_norm"][l], G["kv_latent_norm"][l] = dgq[:, 0], dgkv[:, 0]
        dproj = _dproj_cast(dqa, dkva, dqf, dkf, dvf, dlat, name="dproj_cast")
        dh1 = _matmul(dproj, W["w_in_t"][l], ta=True, name="proj_in_dx")
        dw_in_t = _matmul(dproj, sv["h1"], name="proj_in_dw")
        gb["w_in"] = _t(dw_in_t[:IN_COLS].reshape(N_CHIPS, IN_COLS // N_CHIPS, D_MODEL))
        token = layer_done(l, gb)
        dx, dg = _rms_bwd(sv["x0"], row(W["attn_pre_norm"][l]) + token, dh1, resid=dx1, out_dtype=F32, name="attn_pre_bwd")
        G["attn_pre_norm"][l] = dg[0]

    grads = {k: jnp.stack(v) for k, v in G.items()}
    grads["rel_bias"] = _bias_table_bwd(jnp.stack(dbias_layers), onehot_t).T
    return loss, dx, grads


def kernel(x, attn_pre_norm, w_in, forget_bias, swa_sinks, rel_bias, q_latent_norm, w_uq, kv_latent_norm, w_ukv, group_norm, w_out, attn_post_norm, ffn_pre_norm, w_up, conv_w, conv_b, w_down, ffn_post_norm, loss_target, m_attn_pre_norm, m_w_in, m_forget_bias, m_swa_sinks, m_rel_bias, m_q_latent_norm, m_w_uq, m_kv_latent_norm, m_w_ukv, m_group_norm, m_w_out, m_attn_post_norm, m_ffn_pre_norm, m_w_up, m_conv_w, m_conv_b, m_w_down, m_ffn_post_norm, v_attn_pre_norm, v_w_in, v_forget_bias, v_swa_sinks, v_rel_bias, v_q_latent_norm, v_w_uq, v_kv_latent_norm, v_w_ukv, v_group_norm, v_w_out, v_attn_post_norm, v_ffn_pre_norm, v_w_up, v_conv_w, v_conv_b, v_w_down, v_ffn_post_norm):
    args = dict(locals())
    w = {k: args[k] for k in WEIGHTS}
    m = {k: args["m_" + k] for k in WEIGHTS}
    v = {k: args["v_" + k] for k in WEIGHTS}

    sent = BIG + ("conv_w",)
    gather_state, token = _gather_start([[w[k][l] if k == "conv_w" else w[k][l].astype(BF16) for k in sent]
                                         for l in range(DEPTH)])
    W = {k: w[k] for k in SMALL}
    W["attn_pre_norm"] = W["attn_pre_norm"] + token

    def layer_weights(l, after):
        srcs, lands = _gather_wait(gather_state[l], after, name=f"weight_gather_wait_{l}")
        lands = _gather_forward(lands, name=f"weight_gather_forward_{l}")
        lands = [_place_own(g, s, name="place_own_shard") for g, s in zip(lands, srcs)]
        return _layer_weights(dict(zip(sent, lands)))

    started, groups = [], []

    def layer_done(l, gb):
        keys = [k for k in BIG if k in gb]
        gs = [gb[k] for k in keys]
        tag = f"{l}_{keys[0]}"
        recv = _sibling_exchange(gs, name="grad_sibling_exchange_" + tag)
        pair = [_pair_sum(gk, rk, name="grad_pair_sum") for gk, rk in zip(gs, recv)]
        state, token = _scatter_start(pair, name="grad_scatter_start_" + tag)
        started.append(state)
        groups.append((l, keys))
        return token

    loss_part, dx, g = _local_step(x[0], loss_target[0], W, layer_weights, layer_done)
    loss = lax.psum(loss_part, ("x", "y", "c"))

    reduced = {}
    for (l, keys), (pair, zones) in zip(groups, _scatter_wait(started, dx, name="grad_scatter_wait")):
        for k, p, z in zip(keys, pair, zones):
            reduced[k, l] = _chip_sum(z, p, name="grad_chip_sum")
    mine = [jnp.stack([reduced[k, l] for l in range(DEPTH)]) for k in BIG]
    other = _sibling_share(mine)
    out_g, out_d, out_m, out_v = {}, {}, {}, {}
    for k, g_mine, g_other in zip(BIG, mine, other):
        out_g[k], out_d[k], out_m[k], out_v[k] = _adamw_halves(w[k], g_mine, g_other, m[k], v[k], name="adamw_" + k)

    small_shapes = [w[k].shape for k in SMALL]
    reduced = _allreduce_small(_pack([g[k] for k in SMALL] + [g["conv_w"]], LANES, 8))
    *g_small, g_cw = _unpack(reduced, small_shapes + [g["conv_w"].shape])
    chip = 2 * lax.axis_index("x") + lax.axis_index("y")
    g_small.append(lax.dynamic_slice_in_dim(g_cw, chip * FF_SHARD, FF_SHARD, axis=2))
    names = SMALL + ("conv_w",)
    shapes = small_shapes + [w["conv_w"].shape]
    packed = lambda arrs: _pack(arrs, LANES, ROW_TILE)[None]
    d_s, m_s, v_s = _adamw(packed([w[k] for k in names]), packed(g_small), packed([m[k] for k in names]),
                           packed([v[k] for k in names]), name="adamw_small")
    out_g.update(zip(names, g_small))
    out_d.update(zip(names, _unpack(d_s, shapes)))
    out_m.update(zip(names, _unpack(m_s, shapes)))
    out_v.update(zip(names, _unpack(v_s, shapes)))

    return (loss, dx[None], *[out_g[k] for k in WEIGHTS], *[out_d[k] for k in WEIGHTS],
            *[out_m[k] for k in WEIGHTS], *[out_v[k] for k in WEIGHTS])
```

```python
import math

import numpy as np
import jax
import jax.numpy as jnp
from jax import lax
from jax.experimental import pallas as pl
from jax.experimental.pallas import tpu as pltpu

F32 = jnp.float32
BF16 = jnp.bfloat16

D_MODEL = 1024
DEPTH = 4
HEAD_DIM = 64
SWA_Q_HEADS = 8
SWA_KV_HEADS = 2
SWA_GROUP = SWA_Q_HEADS // SWA_KV_HEADS
WINDOW = 128
FOX_HEADS = 4
MLA_HEADS = 4
MLA_Q_RANK = 256
MLA_KV_RANK = 128
MLA_NOPE = 64
MLA_ROPE = 32
MLA_QK = MLA_NOPE + MLA_ROPE
ROPE_THETA = 10000.0
REL_BUCKETS = 32
REL_MAX_DIST = 128
D_FF = 2816
EPS = 1e-6
NEG_INF = -1e30
LANES = 128
N_CHIPS = 4

IN_COLS = 1956
IN_ROWS = 2048
QKV_ROWS = 1536
LAT_ROWS = IN_ROWS - QKV_ROWS
LAT_SHIFT = FOX_HEADS
FOX_ROW0 = 768
MLA_PAD = LANES
GATE_ROWS = 8

ADAM_LR = 0.001
ADAM_B1 = 0.9
ADAM_B2 = 0.999
ADAM_EPS = 1e-08
ADAM_WD = 0.01
ADAM_STEP = 10

VMEM_LIMIT_BYTES = 48 * 1024 * 1024
ATT_TILE = 512
LOG2E = math.log2(math.e)
ROW_TILE = 256
MESH = pl.DeviceIdType.MESH

NT = (((1,), (1,)), ((), ()))
TN = (((0,), (0,)), ((), ()))
NN = (((1,), (0,)), ((), ()))


def _params(*sem):
    return pltpu.CompilerParams(dimension_semantics=sem, vmem_limit_bytes=VMEM_LIMIT_BYTES)


def _tile(dim, cap):
    for t in (2048, 1408, 1024, 512, 256, 128, 64, 32, 16, 8):
        if t <= cap and dim % t == 0:
            return t
    return dim


def _dot(a, b, dims=NN):
    return lax.dot_general(a, b, dims, preferred_element_type=F32)


def _split3(a):
    a1 = a.astype(BF16)
    r1 = a - a1.astype(F32)
    a2 = r1.astype(BF16)
    a3 = (r1 - a2.astype(F32)).astype(BF16)
    return a1, a2, a3


FF_SHARD = 2 * D_FF // N_CHIPS


def _matmul(a, b, *, ta=False, tb=False, out_dtype=F32, name, b_shards=False, out_shards=False, a_halves=False,
            b_halves=False):
    if a_halves:
        M, K = a.shape[1], 2 * a.shape[2]
    elif ta:
        K, M = a.shape
    else:
        M, K = a.shape
    if b_halves:
        K2, N = b.shape[1], 2 * b.shape[2]
    elif b_shards:
        K2, N = (2 * D_FF, D_MODEL) if tb else (D_MODEL, 2 * D_FF)
    elif tb:
        N, K2 = b.shape
    else:
        K2, N = b.shape
    assert K == K2, (a.shape, b.shape)
    tm, tn, tk = (M if M <= 2048 else _tile(M, 1408)), _tile(N, 1408), _tile(K, 1408)
    nk = K // tk
    dims = (((0 if ta else 1,), (1 if tb else 0,)), ((), ()))

    def body(a_ref, b_ref, o_ref, *acc):
        part = lax.dot_general(a_ref[...], b_ref[...], dims, preferred_element_type=F32)
        if nk == 1:
            o_ref[...] = part.astype(o_ref.dtype)
            return
        acc_ref, k = acc[0], pl.program_id(2)

        @pl.when(k == 0)
        def _():
            acc_ref[...] = part

        @pl.when((k > 0) & (k < nk - 1))
        def _():
            acc_ref[...] += part

        @pl.when(k == nk - 1)
        def _():
            o_ref[...] = (acc_ref[...] + part).astype(o_ref.dtype)

    if a_halves:
        nh = K // 2 // tk
        a_spec = pl.BlockSpec((None, tm, tk), lambda i, j, k: (k // nh, i, k % nh))
    else:
        a_spec = pl.BlockSpec((tk, tm), lambda i, j, k: (k, i)) if ta else pl.BlockSpec((tm, tk), lambda i, j, k: (i, k))
    if b_halves:
        nh = N // 2 // tn
        b_spec = pl.BlockSpec((None, tk, tn), lambda i, j, k: (j // nh, k, j % nh))
    elif b_shards and tb:
        assert tk == FF_SHARD
        b_spec = pl.BlockSpec((None, tn, tk), lambda i, j, k: (k, j, 0))
    elif b_shards:
        assert tn == FF_SHARD
        b_spec = pl.BlockSpec((None, tk, tn), lambda i, j, k: (j, k, 0))
    else:
        b_spec = pl.BlockSpec((tn, tk), lambda i, j, k: (j, k)) if tb else pl.BlockSpec((tk, tn), lambda i, j, k: (k, j))
    if out_shards:
        assert tn == FF_SHARD
        out_spec = pl.BlockSpec((None, tm, tn), lambda i, j, k: (j, i, 0))
        out_shape = jax.ShapeDtypeStruct((N // tn, M, tn), out_dtype)
    else:
        out_spec = pl.BlockSpec((tm, tn), lambda i, j, k: (i, j))
        out_shape = jax.ShapeDtypeStruct((M, N), out_dtype)
    return pl.pallas_call(
        body, name=name, grid=(M // tm, N // tn, nk),
        in_specs=[a_spec, b_spec], out_specs=out_spec, out_shape=out_shape,
        scratch_shapes=[pltpu.VMEM((tm, tn), F32)] if nk > 1 else [],
        compiler_params=_params("parallel", "parallel", "arbitrary"),
    )(a, b)


def _seg_rms(xs, g):
    r = lax.rsqrt(jnp.mean(xs * xs, axis=-1, keepdims=True) + EPS)
    return xs * r * g


def _seg_rms_bwd(xs, g, dy):
    r = lax.rsqrt(jnp.mean(xs * xs, axis=-1, keepdims=True) + EPS)
    gd = dy * g
    c = jnp.mean(gd * xs, axis=-1, keepdims=True)
    dx = r * gd - xs * (r * r * r * c)
    dg = jnp.sum(dy * (xs * r), axis=0, keepdims=True)
    return dx, dg


def _rms_fwd(x, g, *, name):
    S, W = x.shape
    tm = _tile(S, 512)

    def body(x_ref, g_ref, o_ref):
        o_ref[...] = _seg_rms(x_ref[...], g_ref[...]).astype(o_ref.dtype)

    return pl.pallas_call(
        body, name=name, grid=(S // tm,),
        in_specs=[pl.BlockSpec((tm, W), lambda i: (i, 0)), pl.BlockSpec((1, W), lambda i: (0, 0))],
        out_specs=pl.BlockSpec((tm, W), lambda i: (i, 0)),
        out_shape=jax.ShapeDtypeStruct((S, W), BF16),
        compiler_params=_params("parallel"),
    )(x, g)


def _rms_bwd(x, g, dy, *, resid=None, out_dtype, name):
    S, W = x.shape
    tm = _tile(S, 512)
    has_resid = resid is not None

    def body(*refs):
        if has_resid:
            x_ref, g_ref, dy_ref, r_ref, dx_ref, dg_ref = refs
        else:
            x_ref, g_ref, dy_ref, dx_ref, dg_ref = refs

        @pl.when(pl.program_id(0) == 0)
        def _():
            dg_ref[...] = jnp.zeros_like(dg_ref)

        dx, dg = _seg_rms_bwd(x_ref[...], g_ref[...], dy_ref[...])
        if has_resid:
            dx = dx + r_ref[...]
        dx_ref[...] = dx.astype(dx_ref.dtype)
        dg_ref[...] += dg

    row = pl.BlockSpec((tm, W), lambda i: (i, 0))
    vec = pl.BlockSpec((1, W), lambda i: (0, 0))
    ins = [x, g, dy] + ([resid] if has_resid else [])
    return pl.pallas_call(
        body, name=name, grid=(S // tm,),
        in_specs=[row, vec, row] + ([row] if has_resid else []),
        out_specs=[row, vec],
        out_shape=[jax.ShapeDtypeStruct((S, W), out_dtype), jax.ShapeDtypeStruct((1, W), F32)],
        compiler_params=_params("arbitrary"),
    )(*ins)


def _resid_rms(x, y, g_post, g_next, *, name):
    S, W = x.shape
    tm = _tile(S, 512)
    with_next = g_next is not None

    def body(*refs):
        if with_next:
            x_ref, y_ref, gp_ref, gn_ref, xo_ref, h_ref = refs
        else:
            x_ref, y_ref, gp_ref, xo_ref = refs
        xn = x_ref[...] + _seg_rms(y_ref[...], gp_ref[...])
        xo_ref[...] = xn
        if with_next:
            h_ref[...] = _seg_rms(xn, gn_ref[...]).astype(BF16)

    row = pl.BlockSpec((tm, W), lambda i: (i, 0))
    vec = pl.BlockSpec((1, W), lambda i: (0, 0))
    outs = [jax.ShapeDtypeStruct((S, W), F32)] + ([jax.ShapeDtypeStruct((S, W), BF16)] if with_next else [])
    res = pl.pallas_call(
        body, name=name, grid=(S // tm,),
        in_specs=[row, row, vec] + ([vec] if with_next else []),
        out_specs=[row] + ([row] if with_next else []),
        out_shape=outs,
        compiler_params=_params("parallel"),
    )(*([x, y, g_post] + ([g_next] if with_next else [])))
    return (res[0], res[1]) if with_next else (res[0], None)


def _col_rms(xs, g):
    r = lax.rsqrt(jnp.mean(xs * xs, axis=0, keepdims=True) + EPS)
    return xs * r * g


def _col_rms_bwd(xs, g, dy):
    r = lax.rsqrt(jnp.mean(xs * xs, axis=0, keepdims=True) + EPS)
    gd = dy * g
    c = jnp.mean(gd * xs, axis=0, keepdims=True)
    dx = r * gd - xs * (r * r * r * c)
    dg = jnp.sum(dy * (xs * r), axis=1, keepdims=True)
    return dx, dg


GROUP_ROWS = (SWA_Q_HEADS * HEAD_DIM, FOX_HEADS * HEAD_DIM, MLA_HEADS * HEAD_DIM)


def _group_specs(S, tn):
    outs = [pl.BlockSpec((n, tn), lambda i: (0, i)) for n in GROUP_ROWS]
    g = pl.BlockSpec((D_MODEL, 1), lambda i: (0, 0))
    mixed = pl.BlockSpec((D_MODEL, tn), lambda i: (0, i))
    return outs, g, mixed


def _group_norm_fwd(oa, of, oc, g, *, name):
    S = oa.shape[1]
    tn = _tile(S, 512)
    outs, gs, mixed = _group_specs(S, tn)

    def body(a_ref, f_ref, c_ref, g_ref, o_ref):
        r0 = 0
        for ref, n in zip((a_ref, f_ref, c_ref), GROUP_ROWS):
            o_ref[r0:r0 + n, :] = _col_rms(ref[...], g_ref[r0:r0 + n, :]).astype(BF16)
            r0 += n

    return pl.pallas_call(
        body, name=name, grid=(S // tn,),
        in_specs=outs + [gs], out_specs=mixed,
        out_shape=jax.ShapeDtypeStruct((D_MODEL, S), BF16),
        compiler_params=_params("parallel"),
    )(oa, of, oc, g)


def _group_norm_bwd(oa, of, oc, g, dmixed, *, name):
    S = oa.shape[1]
    tn = _tile(S, 512)
    outs, gs, mixed = _group_specs(S, tn)
    n_heads = D_MODEL // HEAD_DIM

    def body(a_ref, f_ref, c_ref, g_ref, dm_ref, da_ref, df_ref, dc_ref, dg_ref, dl_ref):
        @pl.when(pl.program_id(0) == 0)
        def _():
            dg_ref[...] = jnp.zeros_like(dg_ref)

        r0 = 0
        for ref, dref, n in zip((a_ref, f_ref, c_ref), (da_ref, df_ref, dc_ref), GROUP_ROWS):
            o = ref[...]
            dx, dg = _col_rms_bwd(o, g_ref[r0:r0 + n, :], dm_ref[r0:r0 + n, :])
            dxb = dx.astype(BF16)
            dref[...] = dxb
            dg_ref[r0:r0 + n, :] += dg
            od = o * dxb.astype(F32)
            for h in range(n // HEAD_DIM):
                dl_ref[r0 // HEAD_DIM + h] = jnp.sum(od[h * HEAD_DIM:(h + 1) * HEAD_DIM, :], axis=0, keepdims=True)
            r0 += n

    return pl.pallas_call(
        body, name=name, grid=(S // tn,),
        in_specs=outs + [gs, mixed], out_specs=outs + [gs, pl.BlockSpec((n_heads, 1, tn), lambda i: (0, 0, i))],
        out_shape=[jax.ShapeDtypeStruct((n, S), BF16) for n in GROUP_ROWS] + [jax.ShapeDtypeStruct((D_MODEL, 1), F32),
                                                                              jax.ShapeDtypeStruct((n_heads, 1, S), F32)],
        compiler_params=_params("arbitrary"),
    )(oa, of, oc, g, dmixed)


def _loss_head(y, target):
    S, W = y.shape
    tm = _tile(S, 512)

    def body(y_ref, t_ref, d_ref, l_ref):
        @pl.when(pl.program_id(0) == 0)
        def _():
            l_ref[...] = jnp.zeros_like(l_ref)

        err = y_ref[...] - t_ref[...]
        d_ref[...] = err * (1.0 / W)
        l_ref[...] += 0.5 * jnp.sum(jnp.mean(err * err, axis=-1, keepdims=True), axis=0, keepdims=True)

    row = pl.BlockSpec((tm, W), lambda i: (i, 0))
    d, l = pl.pallas_call(
        body, name="loss_head", grid=(S // tm,),
        in_specs=[row, row],
        out_specs=[row, pl.BlockSpec((1, 1), lambda i: (0, 0))],
        out_shape=[jax.ShapeDtypeStruct((S, W), F32), jax.ShapeDtypeStruct((1, 1), F32)],
        compiler_params=_params("arbitrary"),
    )(y, target)
    return l[0, 0], d


def _attn_fwd(q_src, k_src, v_src, rows, H, Dk, Dv, scale, f_row=None, f_col=None, *, name):
    S = q_src.shape[1]
    T = _tile(S, ATT_TILE)
    nq = S // T
    forget = f_row is not None
    qb, kb, vb = rows[0] // (H * Dk), rows[1] // (H * Dk), rows[2] // (H * Dv)
    hs = range(H)

    def body(*refs):
        if forget:
            q_ref, k_ref, v_ref, fq_ref, fk_ref, o_ref, lse_ref = refs
        else:
            q_ref, k_ref, v_ref, o_ref, lse_ref = refs
        i = pl.program_id(0)

        def tile(j, masked, state):
            off = pl.multiple_of(j * T, T)
            ss = [_dot(k_ref[h * Dk:(h + 1) * Dk, pl.ds(off, T)], q_ref[h * Dk:(h + 1) * Dk, :], TN) * (scale * LOG2E)
                  for h in hs]
            if forget:
                ss = [ss[h] + (fq_ref[h] - fk_ref[pl.ds(off, T), h:h + 1]) for h in hs]
            if masked:
                r = lax.broadcasted_iota(jnp.int32, (T, T), 0)
                c = lax.broadcasted_iota(jnp.int32, (T, T), 1)
                ss = [jnp.where(r <= c, s, NEG_INF) for s in ss]
            m_new = [jnp.maximum(state[h][0], jnp.max(ss[h], axis=0, keepdims=True)) for h in hs]
            alpha = [jnp.exp2(state[h][0] - m_new[h]) for h in hs]
            ps = [jnp.exp2(ss[h] - m_new[h]) for h in hs]
            l_new = [alpha[h] * state[h][1] + jnp.sum(ps[h], axis=0, keepdims=True) for h in hs]
            p_hi = [p.astype(BF16) for p in ps]
            vs = [v_ref[h * Dv:(h + 1) * Dv, pl.ds(off, T)] for h in hs]
            pv = [_dot(vs[h], p_hi[h]) for h in hs]
            if forget:
                pv = [pv[h] + _dot(vs[h], (ps[h] - p_hi[h].astype(F32)).astype(BF16)) for h in hs]
            return tuple((m_new[h], l_new[h], alpha[h] * state[h][2] + pv[h]) for h in hs)

        init = tuple((jnp.full((1, T), NEG_INF, F32), jnp.zeros((1, T), F32), jnp.zeros((Dv, T), F32)) for _ in hs)
        state = lax.fori_loop(0, i, lambda j, st: tile(j, False, st), init)
        state = tile(i, True, state)
        for h in hs:
            m, l, acc = state[h]
            o_ref[h * Dv:(h + 1) * Dv, :] = acc / l
            lse_ref[h] = m + jnp.log2(l)

    in_specs = [pl.BlockSpec((H * Dk, T), lambda i: (qb, i)),
                pl.BlockSpec((H * Dk, S), lambda i: (kb, 0)),
                pl.BlockSpec((H * Dv, S), lambda i: (vb, 0))]
    ins = [q_src, k_src, v_src]
    if forget:
        in_specs += [pl.BlockSpec((H, 1, T), lambda i: (0, 0, i)), pl.BlockSpec((S, H), lambda i: (0, 0))]
        ins += [f_row, f_col]
    return pl.pallas_call(
        body, name=name, grid=(nq,),
        in_specs=in_specs,
        out_specs=[pl.BlockSpec((H * Dv, T), lambda i: (0, i)), pl.BlockSpec((H, 1, T), lambda i: (0, 0, i))],
        out_shape=[jax.ShapeDtypeStruct((H * Dv, S), F32), jax.ShapeDtypeStruct((H, 1, S), F32)],
        compiler_params=_params("parallel"),
    )(*ins)


def _attn_bwd(q_src, k_src, v_src, rows, H, Dk, Dv, scale, do, lse, delta, f_row=None, f_col=None, *, name):
    S = q_src.shape[1]
    T = _tile(S, ATT_TILE)
    nq = S // T
    forget = f_row is not None
    qb, kb, vb, db = rows[0] // (H * Dk), rows[1] // (H * Dk), rows[2] // (H * Dv), rows[3] // H
    hs = range(H)

    def body(*refs):
        if forget:
            (q_ref, k_ref, v_ref, do_ref, lse_ref, dl_ref, fq_ref, fk_ref,
             dq_ref, dk_ref, dv_ref, df_ref, dk_s, dv_s, df_s) = refs
        else:
            q_ref, k_ref, v_ref, do_ref, lse_ref, dl_ref, dq_ref, dk_ref, dv_ref, dk_s, dv_s = refs
        j = pl.program_id(0)

        @pl.when(j == 0)
        def _():
            dq_ref[...] = jnp.zeros_like(dq_ref)

        dk_s[...] = jnp.zeros_like(dk_s)
        dv_s[...] = jnp.zeros_like(dv_s)
        if forget:
            df_s[...] = jnp.zeros_like(df_s)
        kt = [k_ref[h * Dk:(h + 1) * Dk, :] for h in hs]
        kj = [k.T for k in kt]
        vj = [v_ref[h * Dv:(h + 1) * Dv, :].T for h in hs]
        koff = pl.multiple_of(j * T, T)

        def tile(i, masked):
            cols = pl.ds(pl.multiple_of(i * T, T), T)
            qi = [q_ref[h * Dk:(h + 1) * Dk, cols] for h in hs]
            doi = [do_ref[h * Dv:(h + 1) * Dv, cols] for h in hs]
            st = [_dot(kj[h], qi[h]) * (scale * LOG2E) for h in hs]
            if forget:
                st = [st[h] + (fq_ref[h, :, cols] - fk_ref[pl.ds(koff, T), h:h + 1]) for h in hs]
            if masked:
                r = lax.broadcasted_iota(jnp.int32, (T, T), 0)
                c = lax.broadcasted_iota(jnp.int32, (T, T), 1)
                st = [jnp.where(r <= c, x, NEG_INF) for x in st]
            pt = [jnp.exp2(st[h] - lse_ref[h, :, cols]) for h in hs]
            dpt = [_dot(vj[h], doi[h]) for h in hs]
            dst = [pt[h] * (dpt[h] - dl_ref[h, :, cols]) for h in hs]
            ptb = [p.astype(BF16) for p in pt]
            dsb = [d.astype(BF16) for d in dst]
            for h in hs:
                dv_s[h * Dv:(h + 1) * Dv, :] += _dot(doi[h], ptb[h], NT)
            for h in hs:
                dk_s[h * Dk:(h + 1) * Dk, :] += _dot(qi[h], dsb[h], NT)
            for h in hs:
                dq_ref[h * Dk:(h + 1) * Dk, cols] += _dot(kt[h], dsb[h]) * scale
            if forget:
                for h in hs:
                    part = dst[h][:, 0:LANES]
                    for c0 in range(LANES, T, LANES):
                        part = part + dst[h][:, c0:c0 + LANES]
                    df_s[h] += part

        tile(j, True)

        def loop_body(i, carry):
            tile(i, False)
            return carry

        lax.fori_loop(j + 1, nq, loop_body, 0)
        dk_ref[...] = dk_s[...] * scale
        dv_ref[...] = dv_s[...]
        if forget:
            df_ref[...] = jnp.concatenate([-jnp.sum(df_s[h], axis=-1, keepdims=True) for h in hs], axis=1)

    res = lambda D, b0: pl.BlockSpec((H * D, S), lambda j: (b0, 0))
    blk = lambda D, b0: pl.BlockSpec((H * D, T), lambda j: (b0, j))
    row3 = lambda b0: pl.BlockSpec((H, 1, S), lambda j: (b0, 0, 0))
    in_specs = [res(Dk, qb), blk(Dk, kb), blk(Dv, vb), res(Dv, 0), row3(0), row3(db)]
    ins = [q_src, k_src, v_src, do, lse, delta]
    out_specs = [res(Dk, 0), blk(Dk, 0), blk(Dv, 0)]
    out_shape = [jax.ShapeDtypeStruct((H * Dk, S), F32), jax.ShapeDtypeStruct((H * Dk, S), F32),
                 jax.ShapeDtypeStruct((H * Dv, S), F32)]
    scratch = [pltpu.VMEM((H * Dk, T), F32), pltpu.VMEM((H * Dv, T), F32)]
    if forget:
        in_specs += [row3(0), pl.BlockSpec((S, H), lambda j: (0, 0))]
        ins += [f_row, f_col]
        out_specs.append(pl.BlockSpec((T, H), lambda j: (j, 0)))
        out_shape.append(jax.ShapeDtypeStruct((S, H), F32))
        scratch.append(pltpu.VMEM((H, T, min(T, LANES)), F32))
    return pl.pallas_call(
        body, name=name, grid=(nq,),
        in_specs=in_specs, out_specs=out_specs, out_shape=out_shape, scratch_shapes=scratch,
        compiler_params=_params("arbitrary"),
    )(*ins)


GW = SWA_GROUP * WINDOW


def _swa_masks(i):
    r = lax.broadcasted_iota(jnp.int32, (WINDOW, GW), 0)
    c = lax.broadcasted_iota(jnp.int32, (WINDOW, GW), 1) % WINDOW
    return (r > c) & (i > 0), r <= c


def _swa_specs():
    W = WINDOW
    kv_rows = SWA_KV_HEADS * HEAD_DIM
    q = pl.BlockSpec((SWA_Q_HEADS * HEAD_DIM, W), lambda i: (0, i))
    prev = lambda b: pl.BlockSpec((kv_rows, W), lambda i: (b, jnp.maximum(i - 1, 0)))
    cur = lambda b: pl.BlockSpec((kv_rows, W), lambda i: (b, i))
    bias = pl.BlockSpec((SWA_KV_HEADS, 2 * W, GW), lambda i: (0, 0, 0))
    stat = pl.BlockSpec((SWA_Q_HEADS, W), lambda i: (0, i))
    sink = pl.BlockSpec(memory_space=pltpu.SMEM)
    return q, prev(4), cur(4), prev(5), cur(5), bias, stat, sink


def _group_lanes(ref, g, rows_per_head):
    h0 = g * SWA_GROUP
    return jnp.concatenate([ref[(h0 + j) * rows_per_head:(h0 + j + 1) * rows_per_head, :] for j in range(SWA_GROUP)], axis=1)


def _swa_scores(g, q_ref, kp_ref, kc_ref, b_ref, masks):
    rows = slice(g * HEAD_DIM, (g + 1) * HEAD_DIM)
    qg = _group_lanes(q_ref, g, HEAD_DIM)
    scale = HEAD_DIM ** -0.5
    s_p = jnp.where(masks[0], _dot(kp_ref[rows, :], qg, TN) * scale + b_ref[g, 0:WINDOW, :], NEG_INF)
    s_c = jnp.where(masks[1], _dot(kc_ref[rows, :], qg, TN) * scale + b_ref[g, WINDOW:2 * WINDOW, :], NEG_INF)
    return qg, rows, s_p, s_c


def _sink_row(sink_ref, g):
    return jnp.concatenate([jnp.full((1, WINDOW), sink_ref[g * SWA_GROUP + j], F32) for j in range(SWA_GROUP)], axis=1)


def _swa_fwd(qkv, bias_g, sinks, *, name):
    S = qkv.shape[1]
    qs, kp, kc, vp, vc, bs, stat, sk = _swa_specs()
    gs = range(SWA_KV_HEADS)

    def body(sink_ref, q_ref, kp_ref, kc_ref, vp_ref, vc_ref, b_ref, o_ref, lse_ref):
        masks = _swa_masks(pl.program_id(0))
        sc = [_swa_scores(g, q_ref, kp_ref, kc_ref, b_ref, masks) for g in gs]
        sinks_g = [_sink_row(sink_ref, g) for g in gs]
        m = [jnp.maximum(jnp.maximum(jnp.max(sc[g][2], axis=0, keepdims=True), jnp.max(sc[g][3], axis=0, keepdims=True)),
                         sinks_g[g]) for g in gs]
        p_p = [jnp.exp(sc[g][2] - m[g]) for g in gs]
        p_c = [jnp.exp(sc[g][3] - m[g]) for g in gs]
        l = [jnp.sum(p_p[g], axis=0, keepdims=True) + jnp.sum(p_c[g], axis=0, keepdims=True) + jnp.exp(sinks_g[g] - m[g])
             for g in gs]
        o = [_dot(vp_ref[sc[g][1], :], p_p[g].astype(BF16)) + _dot(vc_ref[sc[g][1], :], p_c[g].astype(BF16)) for g in gs]
        for g in gs:
            og = o[g] / l[g]
            lse = m[g] + jnp.log(l[g])
            for j in range(SWA_GROUP):
                h = g * SWA_GROUP + j
                o_ref[h * HEAD_DIM:(h + 1) * HEAD_DIM, :] = og[:, j * WINDOW:(j + 1) * WINDOW]
                lse_ref[h:h + 1, :] = lse[:, j * WINDOW:(j + 1) * WINDOW]

    return pl.pallas_call(
        body, name=name, grid=(S // WINDOW,),
        in_specs=[sk, qs, kp, kc, vp, vc, bs],
        out_specs=[qs, stat],
        out_shape=[jax.ShapeDtypeStruct((SWA_Q_HEADS * HEAD_DIM, S), F32), jax.ShapeDtypeStruct((SWA_Q_HEADS, S), F32)],
        compiler_params=_params("parallel"),
    )(sinks, qkv, qkv, qkv, qkv, qkv, bias_g)


def _swa_bwd(qkv, bias_g, sinks, do, lse, delta, *, name):
    S = qkv.shape[1]
    W = WINDOW
    qs, kp, kc, vp, vc, bs, stat, sk = _swa_specs()
    scale = HEAD_DIM ** -0.5
    kv_rows = SWA_KV_HEADS * HEAD_DIM
    gs = range(SWA_KV_HEADS)

    def body(sink_ref, q_ref, kp_ref, kc_ref, vp_ref, vc_ref, b_ref, do_ref, lse_ref, dl_ref,
             dq_ref, dkv_ref, db_ref, dsk_ref):
        i = pl.program_id(0)

        @pl.when(i == 0)
        def _():
            dkv_ref[...] = jnp.zeros_like(dkv_ref)
            db_ref[...] = jnp.zeros_like(db_ref)
            dsk_ref[...] = jnp.zeros_like(dsk_ref)

        masks = _swa_masks(i)
        prev = pl.ds(pl.multiple_of(jnp.maximum(i - 1, 0) * W, W), W)
        cur = pl.ds(pl.multiple_of(i * W, W), W)
        sc = [_swa_scores(g, q_ref, kp_ref, kc_ref, b_ref, masks) for g in gs]
        dog = [_group_lanes(do_ref, g, HEAD_DIM) for g in gs]
        lse = [_group_lanes(lse_ref, g, 1) for g in gs]
        dl = [_group_lanes(dl_ref, g, 1) for g in gs]
        p_p = [jnp.exp(sc[g][2] - lse[g]) for g in gs]
        p_c = [jnp.exp(sc[g][3] - lse[g]) for g in gs]
        ds_p = [p_p[g] * (_dot(vp_ref[sc[g][1], :], dog[g], TN) - dl[g]) for g in gs]
        ds_c = [p_c[g] * (_dot(vc_ref[sc[g][1], :], dog[g], TN) - dl[g]) for g in gs]
        for g in gs:
            db_ref[g, 0:W, :] += ds_p[g]
            db_ref[g, W:2 * W, :] += ds_c[g]
            dsk = jnp.exp(_sink_row(sink_ref, g) - lse[g]) * dl[g]
            for j in range(SWA_GROUP):
                h = g * SWA_GROUP + j
                dsk_ref[h:h + 1, :] -= jnp.broadcast_to(jnp.sum(dsk[:, j * W:(j + 1) * W], axis=1, keepdims=True), (1, LANES))
        dsb_p = [d.astype(BF16) for d in ds_p]
        dsb_c = [d.astype(BF16) for d in ds_c]
        for g in gs:
            rows = sc[g][1]
            dq = (_dot(kp_ref[rows, :], dsb_p[g]) + _dot(kc_ref[rows, :], dsb_c[g])) * scale
            for j in range(SWA_GROUP):
                h = g * SWA_GROUP + j
                dq_ref[h * HEAD_DIM:(h + 1) * HEAD_DIM, :] = dq[:, j * W:(j + 1) * W]
        for g in gs:
            rows = sc[g][1]
            vrows = slice(kv_rows + rows.start, kv_rows + rows.stop)
            dkv_ref[rows, prev] += _dot(sc[g][0], dsb_p[g], NT) * scale
            dkv_ref[rows, cur] += _dot(sc[g][0], dsb_c[g], NT) * scale
            dkv_ref[vrows, prev] += _dot(dog[g], p_p[g].astype(BF16), NT)
            dkv_ref[vrows, cur] += _dot(dog[g], p_c[g].astype(BF16), NT)

    return pl.pallas_call(
        body, name=name, grid=(S // W,),
        in_specs=[sk, qs, kp, kc, vp, vc, bs, qs, stat, stat],
        out_specs=[qs, pl.BlockSpec((2 * kv_rows, S), lambda i: (0, 0)), bs, pl.BlockSpec((SWA_Q_HEADS, LANES), lambda i: (0, 0))],
        out_shape=[jax.ShapeDtypeStruct((SWA_Q_HEADS * HEAD_DIM, S), F32), jax.ShapeDtypeStruct((2 * kv_rows, S), F32),
                   jax.ShapeDtypeStruct((SWA_KV_HEADS, 2 * W, GW), F32), jax.ShapeDtypeStruct((SWA_Q_HEADS, LANES), F32)],
        compiler_params=_params("arbitrary"),
    )(sinks, qkv, qkv, qkv, qkv, qkv, bias_g, do, lse, delta)


def _rel_onehot_t():
    qi = jnp.arange(WINDOW, dtype=jnp.int32)[None, :] + WINDOW
    kj = jnp.arange(2 * WINDOW, dtype=jnp.int32)[:, None]
    dist = qi - kj
    max_exact = REL_BUCKETS // 2
    d = jnp.maximum(dist, 0)
    log_ratio = jnp.log(jnp.maximum(d, 1).astype(F32) / max_exact) / math.log(REL_MAX_DIST / max_exact)
    large = jnp.minimum(max_exact + (log_ratio * (REL_BUCKETS - max_exact)).astype(jnp.int32), REL_BUCKETS - 1)
    bucket = jnp.where(d < max_exact, d, large).reshape(-1)
    return (bucket[None, :] == jnp.arange(REL_BUCKETS, dtype=jnp.int32)[:, None]).astype(BF16)


def _bias_table(rel_bias_t, onehot_t):
    Hq, NB = rel_bias_t.shape
    N = onehot_t.shape[1]
    tn = _tile(N, 4096)

    def body(r_ref, oh_ref, o_ref):
        oh = oh_ref[...]
        a1, a2, a3 = _split3(r_ref[...])
        o_ref[...] = _dot(a1, oh) + _dot(a2, oh) + _dot(a3, oh)

    return pl.pallas_call(
        body, name="rel_bias_table", grid=(N // tn,),
        in_specs=[pl.BlockSpec((Hq, NB), lambda j: (0, 0)), pl.BlockSpec((NB, tn), lambda j: (0, j))],
        out_specs=pl.BlockSpec((Hq, tn), lambda j: (0, j)),
        out_shape=jax.ShapeDtypeStruct((Hq, N), F32),
        compiler_params=_params("parallel"),
    )(rel_bias_t, onehot_t)


def _bias_table_bwd(dbias, onehot_t):
    L, Hq, N = dbias.shape
    NB = onehot_t.shape[0]
    tn = _tile(N, 4096)

    def body(d_ref, oh_ref, o_ref):
        @pl.when(pl.program_id(0) == 0)
        def _():
            o_ref[...] = jnp.zeros_like(o_ref)

        d = d_ref[0]
        for l in range(1, L):
            d = d + d_ref[l]
        oh = oh_ref[...]
        a1, a2, a3 = _split3(d)
        o_ref[...] += _dot(a1, oh, NT) + _dot(a2, oh, NT) + _dot(a3, oh, NT)

    return pl.pallas_call(
        body, name="rel_bias_bwd", grid=(N // tn,),
        in_specs=[pl.BlockSpec((L, Hq, tn), lambda j: (0, 0, j)), pl.BlockSpec((NB, tn), lambda j: (0, j))],
        out_specs=pl.BlockSpec((Hq, NB), lambda j: (0, 0)),
        out_shape=jax.ShapeDtypeStruct((Hq, NB), F32),
        compiler_params=_params("arbitrary"),
    )(dbias, onehot_t)


def _gate_fwd(lat, fb_col, *, name):
    S = lat.shape[1]
    tn = _tile(S, 256)

    def body(z_ref, fb_ref, o_ref, carry):
        @pl.when(pl.program_id(0) == 0)
        def _():
            carry[...] = jnp.zeros_like(carry)

        z = z_ref[...] + fb_ref[...]
        lf = jnp.minimum(z, 0.0) - jnp.log1p(jnp.exp(-jnp.abs(z)))
        r = lax.broadcasted_iota(jnp.int32, (tn, tn), 0)
        c = lax.broadcasted_iota(jnp.int32, (tn, tn), 1)
        tri = (r <= c).astype(BF16)
        a1, a2, a3 = _split3(lf)
        cum = _dot(a1, tri) + _dot(a2, tri) + _dot(a3, tri) + carry[:, 0:1]
        o_ref[...] = cum
        carry[...] = jnp.broadcast_to(cum[:, tn - 1:tn], carry.shape)

    return pl.pallas_call(
        body, name=name, grid=(S // tn,),
        in_specs=[pl.BlockSpec((GATE_ROWS, tn), lambda i: (0, i)), pl.BlockSpec((GATE_ROWS, 1), lambda i: (0, 0))],
        out_specs=pl.BlockSpec((GATE_ROWS, tn), lambda i: (0, i)),
        out_shape=jax.ShapeDtypeStruct((GATE_ROWS, S), F32),
        scratch_shapes=[pltpu.VMEM((GATE_ROWS, LANES), F32)],
        compiler_params=_params("arbitrary"),
    )(lat, fb_col)


def _gate_bwd(lat, fb_col, dF, *, name):
    S = lat.shape[1]
    tn = _tile(S, 256)
    nt = S // tn

    def body(z_ref, fb_ref, df_ref, dz_ref, dfb_ref, carry):
        @pl.when(pl.program_id(0) == 0)
        def _():
            carry[...] = jnp.zeros_like(carry)
            dfb_ref[...] = jnp.zeros_like(dfb_ref)

        r = lax.broadcasted_iota(jnp.int32, (tn, tn), 0)
        c = lax.broadcasted_iota(jnp.int32, (tn, tn), 1)
        tri = (r >= c).astype(BF16)
        a1, a2, a3 = _split3(df_ref[...])
        dlf = _dot(a1, tri) + _dot(a2, tri) + _dot(a3, tri) + carry[:, 0:1]
        carry[...] = jnp.broadcast_to(dlf[:, 0:1], carry.shape)
        z = z_ref[...] + fb_ref[...]
        row = lax.broadcasted_iota(jnp.int32, (GATE_ROWS, tn), 0)
        dz = jnp.where(row < FOX_HEADS, dlf / (1.0 + jnp.exp(z)), 0.0)
        dz_ref[...] = dz
        dfb_ref[...] += jnp.sum(dz, axis=1, keepdims=True)

    blk = pl.BlockSpec((GATE_ROWS, tn), lambda i: (0, nt - 1 - i))
    vec = pl.BlockSpec((GATE_ROWS, 1), lambda i: (0, 0))
    return pl.pallas_call(
        body, name=name, grid=(nt,),
        in_specs=[blk, vec, blk], out_specs=[blk, vec],
        out_shape=[jax.ShapeDtypeStruct((GATE_ROWS, S), F32), jax.ShapeDtypeStruct((GATE_ROWS, 1), F32)],
        scratch_shapes=[pltpu.VMEM((GATE_ROWS, LANES), F32)],
        compiler_params=_params("arbitrary"),
    )(lat, fb_col, dF)


def _rope_tables(S):
    pos = jnp.arange(S, dtype=F32)
    inv_freq = ROPE_THETA ** (-(jnp.arange(MLA_ROPE // 2, dtype=F32) * 2.0 / MLA_ROPE))
    ang = pos[:, None] * inv_freq[None, :]
    cos, sin = jnp.cos(ang).T, jnp.sin(ang).T
    z16 = jnp.zeros_like(cos)

    def slab(lo, fill):
        def put(first, second, f):
            return jnp.concatenate([jnp.full((lo, S), f, F32), first, second, jnp.full((LANES - lo - MLA_ROPE, S), f, F32)], axis=0)
        return put(cos, cos, fill), put(-sin, z16, 0.0), put(z16, sin, 0.0)

    tq = tuple(jnp.tile(t, (MLA_HEADS, 1)) for t in slab(MLA_NOPE, 1.0))
    return tq, slab(0, 0.0)


def _rope(x, c, s1, s2):
    n = x.shape[0]
    half = MLA_ROPE // 2
    return x * c + pltpu.roll(x, n - half, 0) * s1 + pltpu.roll(x, half, 0) * s2


def _rope_t(dy, c, s1, s2):
    n = dy.shape[0]
    half = MLA_ROPE // 2
    return dy * c + pltpu.roll(dy * s1, half, 0) + pltpu.roll(dy * s2, n - half, 0)


KR_SLAB0 = MLA_Q_RANK + MLA_KV_RANK


def _mla_prep_fwd(lat, g_q, g_kv, w_uq_t, w_ukv_t, tq, tmisc, *, name):
    S = lat.shape[1]
    tn = _tile(S, 512)
    QW = MLA_HEADS * MLA_PAD

    def body(lat_ref, gq_ref, gkv_ref, wq_ref, wkv_ref, c_ref, s1_ref, s2_ref, cm_ref, s1m_ref, s2m_ref,
             nq_ref, nkv_ref, q_ref, k_ref, v_ref):
        x = pltpu.roll(lat_ref[...], LAT_ROWS - LAT_SHIFT, 0)
        nq = _col_rms(x[0:MLA_Q_RANK, :], gq_ref[...]).astype(BF16)
        nkv = _col_rms(x[MLA_Q_RANK:KR_SLAB0, :], gkv_ref[...]).astype(BF16)
        nq_ref[...] = nq
        nkv_ref[...] = nkv
        q_ref[...] = _rope(_dot(wq_ref[...], nq), c_ref[...], s1_ref[...], s2_ref[...]).astype(BF16)
        kv = _dot(wkv_ref[...], nkv).astype(BF16)
        kr = _rope(x[KR_SLAB0:LAT_ROWS, :], cm_ref[...], s1m_ref[...], s2m_ref[...]).astype(BF16)
        for h in range(MLA_HEADS):
            k_ref[h * MLA_PAD:h * MLA_PAD + MLA_NOPE, :] = kv[h * LANES:h * LANES + MLA_NOPE, :]
            k_ref[h * MLA_PAD + MLA_NOPE:(h + 1) * MLA_PAD, :] = kr[0:MLA_PAD - MLA_NOPE, :]
            v_ref[h * HEAD_DIM:(h + 1) * HEAD_DIM, :] = kv[h * LANES + MLA_NOPE:(h + 1) * LANES, :]

    def col(rows):
        return pl.BlockSpec((rows, tn), lambda i: (0, i))

    def full(a):
        return pl.BlockSpec(a.shape, lambda i: (0, 0))

    return pl.pallas_call(
        body, name=name, grid=(S // tn,),
        in_specs=[col(LAT_ROWS), full(g_q), full(g_kv), full(w_uq_t), full(w_ukv_t),
                  col(QW), col(QW), col(QW), col(LANES), col(LANES), col(LANES)],
        out_specs=[col(MLA_Q_RANK), col(MLA_KV_RANK), col(QW), col(QW), col(MLA_HEADS * HEAD_DIM)],
        out_shape=[jax.ShapeDtypeStruct((MLA_Q_RANK, S), BF16), jax.ShapeDtypeStruct((MLA_KV_RANK, S), BF16),
                   jax.ShapeDtypeStruct((QW, S), BF16), jax.ShapeDtypeStruct((QW, S), BF16),
                   jax.ShapeDtypeStruct((MLA_HEADS * HEAD_DIM, S), BF16)],
        compiler_params=_params("parallel"),
    )(lat, g_q, g_kv, w_uq_t, w_ukv_t, *tq, *tmisc)


def _mla_prep_bwd(lat, nq, nkv, g_q, g_kv, w_uq_p, w_ukv, tq, tmisc, dq, dk, dv, dflog, *, name):
    S = lat.shape[1]
    tn = _tile(S, 512)
    QW = MLA_HEADS * MLA_PAD

    def body(lat_ref, nq_ref, nkv_ref, gq_ref, gkv_ref, wq_ref, wkv_ref, c_ref, s1_ref, s2_ref,
             cm_ref, s1m_ref, s2m_ref, dq_ref, dk_ref, dv_ref, dfl_ref,
             dlat_ref, dwq_ref, dwkv_ref, dgq_ref, dgkv_ref, y_s):
        @pl.when(pl.program_id(0) == 0)
        def _():
            dwq_ref[...] = jnp.zeros_like(dwq_ref)
            dwkv_ref[...] = jnp.zeros_like(dwkv_ref)
            dgq_ref[...] = jnp.zeros_like(dgq_ref)
            dgkv_ref[...] = jnp.zeros_like(dgkv_ref)

        x = pltpu.roll(lat_ref[...], LAT_ROWS - LAT_SHIFT, 0)
        dqm = _rope_t(dq_ref[...], c_ref[...], s1_ref[...], s2_ref[...]).astype(BF16)
        dwq_ref[...] += _dot(dqm, nq_ref[...], NT)
        dx, dg = _col_rms_bwd(x[0:MLA_Q_RANK, :], gq_ref[...], _dot(wq_ref[...], dqm))
        y_s[0:MLA_Q_RANK, :] = dx
        dgq_ref[...] += dg
        dkv = jnp.concatenate(
            [part for h in range(MLA_HEADS)
             for part in (dk_ref[h * MLA_PAD:h * MLA_PAD + MLA_NOPE, :], dv_ref[h * HEAD_DIM:(h + 1) * HEAD_DIM, :])],
            axis=0).astype(BF16)
        dwkv_ref[...] += _dot(dkv, nkv_ref[...], NT)
        dx, dg = _col_rms_bwd(x[MLA_Q_RANK:KR_SLAB0, :], gkv_ref[...], _dot(wkv_ref[...], dkv))
        y_s[MLA_Q_RANK:KR_SLAB0, :] = dx
        dgkv_ref[...] += dg
        dkr = dk_ref[MLA_NOPE:MLA_PAD, :]
        for h in range(1, MLA_HEADS):
            dkr = dkr + dk_ref[h * MLA_PAD + MLA_NOPE:(h + 1) * MLA_PAD, :]
        dkr = jnp.concatenate([dkr, jnp.zeros((MLA_NOPE, tn), F32)], axis=0)
        y_s[KR_SLAB0:LAT_ROWS, :] = _rope_t(dkr, cm_ref[...], s1m_ref[...], s2m_ref[...])
        y = pltpu.roll(y_s[...], LAT_SHIFT, 0)
        row = lax.broadcasted_iota(jnp.int32, (LAT_ROWS, tn), 0)
        dfl = jnp.concatenate([dfl_ref[...], jnp.zeros((LAT_ROWS - GATE_ROWS, tn), F32)], axis=0)
        dlat_ref[...] = jnp.where(row < LAT_SHIFT, dfl, y).astype(BF16)

    def col(rows):
        return pl.BlockSpec((rows, tn), lambda i: (0, i))

    def full(a):
        return pl.BlockSpec(a.shape, lambda i: (0, 0))

    def acc(r, c):
        return pl.BlockSpec((r, c), lambda i: (0, 0))

    return pl.pallas_call(
        body, name=name, grid=(S // tn,),
        in_specs=[col(LAT_ROWS), col(MLA_Q_RANK), col(MLA_KV_RANK), full(g_q), full(g_kv),
                  full(w_uq_p), full(w_ukv), col(QW), col(QW), col(QW), col(LANES), col(LANES), col(LANES),
                  col(QW), col(QW), col(MLA_HEADS * HEAD_DIM), col(GATE_ROWS)],
        out_specs=[col(LAT_ROWS), acc(QW, MLA_Q_RANK), acc(QW, MLA_KV_RANK), acc(MLA_Q_RANK, 1), acc(MLA_KV_RANK, 1)],
        out_shape=[jax.ShapeDtypeStruct((LAT_ROWS, S), BF16), jax.ShapeDtypeStruct((QW, MLA_Q_RANK), F32),
                   jax.ShapeDtypeStruct((QW, MLA_KV_RANK), F32), jax.ShapeDtypeStruct((MLA_Q_RANK, 1), F32),
                   jax.ShapeDtypeStruct((MLA_KV_RANK, 1), F32)],
        scratch_shapes=[pltpu.VMEM((LAT_ROWS, tn), F32)],
        compiler_params=_params("arbitrary"),
    )(lat, nq, nkv, g_q, g_kv, w_uq_p, w_ukv, *tq, *tmisc, dq, dk, dv, dflog)


def _dproj_cast(dqa, dkva, dqf, dkf, dvf, dlat, *, name):
    S = dqa.shape[1]
    tn = _tile(S, 512)
    parts = (dqa, dkva, dqf, dkf, dvf, dlat)

    def body(*refs):
        o_ref = refs[-1]
        r0 = 0
        for ref in refs[:-1]:
            n = ref.shape[0]
            o_ref[r0:r0 + n, :] = ref[...].astype(BF16)
            r0 += n

    return pl.pallas_call(
        body, name=name, grid=(S // tn,),
        in_specs=[pl.BlockSpec((p.shape[0], tn), lambda i: (0, i)) for p in parts],
        out_specs=pl.BlockSpec((IN_ROWS, tn), lambda i: (0, i)),
        out_shape=jax.ShapeDtypeStruct((IN_ROWS, S), BF16),
        compiler_params=_params("parallel"),
    )(*parts)


GELU_C = math.sqrt(2.0 / math.pi)
GELU_A = 0.044715


HALO = 16


def _shift_down(a, k, fill):
    r = pltpu.roll(a, k, 0)
    row = lax.broadcasted_iota(jnp.int32, (8, a.shape[1]), 0)
    head = r[0:8, :]
    for i in range(k):
        head = jnp.where(row == i, fill[len(fill) - k + i], head)
    return jnp.concatenate([head, r[8:, :]], axis=0)


def _shift_up(d, k, fill):
    n = d.shape[0]
    r = pltpu.roll(d, n - k, 0)
    row = lax.broadcasted_iota(jnp.int32, (8, d.shape[1]), 0)
    tail = r[n - 8:n, :]
    for i in range(k):
        tail = jnp.where(row == 8 - k + i, fill[i], tail)
    return jnp.concatenate([r[0:n - 8, :], tail], axis=0)


def _conv_taps(a, before, w_ref, b_ref):
    a1 = _shift_down(a, 1, before)
    a2 = _shift_down(a, 2, before)
    u = ((b_ref[...] + w_ref[0:1, :] * a2) + w_ref[1:2, :] * a1) + w_ref[2:3, :] * a
    return u, a1, a2


def _rows_before(halo_ref, first):
    h = halo_ref[HALO - 2:HALO, :].astype(F32)
    return jnp.where(first, 0.0, h[0:1, :]), jnp.where(first, 0.0, h[1:2, :])


def _conv_specs(S, tm, tc, nc):
    hb = tm // HALO
    main = lambda off: pl.BlockSpec((tm, tc), lambda j, i: (i, j + off))
    prev = lambda off: pl.BlockSpec((HALO, tc), lambda j, i: (jnp.maximum(i * hb - 1, 0), j + off))
    nxt = lambda off: pl.BlockSpec((HALO, tc), lambda j, i: (jnp.minimum((i + 1) * hb, S // HALO - 1), j + off))
    wspec = lambda off: pl.BlockSpec((3, tc), lambda j, i: (0, j + off))
    bspec = lambda off: pl.BlockSpec((1, tc), lambda j, i: (0, j + off))
    return main, prev, nxt, wspec, bspec


def _conv_geglu_fwd(a, conv_w, conv_b, *, name):
    S = a.shape[0]
    tm, tc = _tile(S, 512), _tile(D_FF, 1408)
    nc = D_FF // tc
    main, prev, _, wspec, bspec = _conv_specs(S, tm, tc, nc)

    def body(ag_ref, au_ref, hg_ref, hu_ref, wg_ref, wu_ref, bg_ref, bu_ref, z_ref):
        first = pl.program_id(1) == 0
        gate, _, _ = _conv_taps(ag_ref[...].astype(F32), _rows_before(hg_ref, first), wg_ref, bg_ref)
        up, _, _ = _conv_taps(au_ref[...].astype(F32), _rows_before(hu_ref, first), wu_ref, bu_ref)
        cdf = 0.5 * (1.0 + jnp.tanh(GELU_C * (gate + GELU_A * (gate * gate * gate))))
        z_ref[...] = (gate * cdf * up).astype(BF16)

    return pl.pallas_call(
        body, name=name, grid=(nc, S // tm),
        in_specs=[main(0), main(nc), prev(0), prev(nc), wspec(0), wspec(nc), bspec(0), bspec(nc)],
        out_specs=pl.BlockSpec((tm, tc), lambda j, i: (i, j)),
        out_shape=jax.ShapeDtypeStruct((S, D_FF), BF16),
        compiler_params=_params("parallel", "arbitrary"),
    )(a, a, a, a, conv_w, conv_w, conv_b, conv_b)


def _geglu_bwd(gate, up, dz):
    g2x = gate * gate
    th = jnp.tanh(GELU_C * (gate + GELU_A * (g2x * gate)))
    cdf = 0.5 * (1.0 + th)
    dgelu = cdf + gate * (0.5 * (1.0 - th * th) * (GELU_C * (1.0 + 3.0 * GELU_A * g2x)))
    return dz * up * dgelu, dz * (gate * cdf)


def _conv_geglu_bwd(a, conv_w, conv_b, dz, *, name):
    S = a.shape[0]
    tm, tc = _tile(S, 512), _tile(D_FF, 1408)
    nc = D_FF // tc
    nr = S // tm
    main, prev, nxt, wspec, bspec = _conv_specs(S, tm, tc, nc)

    def body(ag_ref, au_ref, pg_ref, pu_ref, ng_ref, nu_ref, wg_ref, wu_ref, bg_ref, bu_ref, dz_ref, dzn_ref,
             da_ref, dw_ref, db_ref):
        i = pl.program_id(1)
        first, last = i == 0, i == nr - 1

        @pl.when(first)
        def _():
            dw_ref[...] = jnp.zeros_like(dw_ref)
            db_ref[...] = jnp.zeros_like(db_ref)

        ag, au = ag_ref[...].astype(F32), au_ref[...].astype(F32)
        gate, g1, g2 = _conv_taps(ag, _rows_before(pg_ref, first), wg_ref, bg_ref)
        up, u1, u2 = _conv_taps(au, _rows_before(pu_ref, first), wu_ref, bu_ref)
        dug, duu = _geglu_bwd(gate, up, dz_ref[...])
        gate_n, _, _ = _conv_taps(ng_ref[...].astype(F32), (ag[tm - 2:tm - 1, :], ag[tm - 1:tm, :]), wg_ref, bg_ref)
        up_n, _, _ = _conv_taps(nu_ref[...].astype(F32), (au[tm - 2:tm - 1, :], au[tm - 1:tm, :]), wu_ref, bu_ref)
        dug_n, duu_n = _geglu_bwd(gate_n, up_n, dzn_ref[...])
        for half, du, du_n, w_ref, taps in ((0, dug, dug_n, wg_ref, (g2, g1, ag)), (1, duu, duu_n, wu_ref, (u2, u1, au))):
            after = (jnp.where(last, 0.0, du_n[0:1, :]), jnp.where(last, 0.0, du_n[1:2, :]))
            d1, d2 = _shift_up(du, 1, after), _shift_up(du, 2, after)
            da_ref[half] = (w_ref[2:3, :] * du + w_ref[1:2, :] * d1 + w_ref[0:1, :] * d2).astype(BF16)
            for tap in range(3):
                dw_ref[half, tap:tap + 1, :] += jnp.sum(du * taps[tap], axis=0, keepdims=True)
            db_ref[half] += jnp.sum(du, axis=0, keepdims=True)

    hb = tm // HALO
    return pl.pallas_call(
        body, name=name, grid=(nc, nr),
        in_specs=[main(0), main(nc), prev(0), prev(nc), nxt(0), nxt(nc), wspec(0), wspec(nc), bspec(0), bspec(nc),
                  pl.BlockSpec((tm, tc), lambda j, i: (i, j)),
                  pl.BlockSpec((HALO, tc), lambda j, i: (jnp.minimum((i + 1) * hb, S // HALO - 1), j))],
        out_specs=[pl.BlockSpec((2, tm, tc), lambda j, i: (0, i, j)), pl.BlockSpec((2, 3, tc), lambda j, i: (0, 0, j)),
                   pl.BlockSpec((2, 1, tc), lambda j, i: (0, 0, j))],
        out_shape=[jax.ShapeDtypeStruct((2, S, D_FF), BF16), jax.ShapeDtypeStruct((2, 3, D_FF), F32),
                   jax.ShapeDtypeStruct((2, 1, D_FF), F32)],
        compiler_params=_params("parallel", "arbitrary"),
    )(a, a, a, a, a, a, conv_w, conv_w, conv_b, conv_b, dz, dz)


def _adamw_update(w, g, m, v):
    m = ADAM_B1 * m + (1.0 - ADAM_B1) * g
    v = ADAM_B2 * v + (1.0 - ADAM_B2) * jnp.square(g)
    m_hat = m / (1.0 - ADAM_B1 ** ADAM_STEP)
    v_hat = v / (1.0 - ADAM_B2 ** ADAM_STEP)
    return -ADAM_LR * (m_hat / (jnp.sqrt(v_hat) + ADAM_EPS) + ADAM_WD * w), m, v


def _adamw(w, g, m, v, *, name):
    L, A, B = w.shape
    ta = _tile(A, ROW_TILE)

    def body(w_ref, g_ref, m_ref, v_ref, d_ref, mo_ref, vo_ref):
        d_ref[...], mo_ref[...], vo_ref[...] = _adamw_update(w_ref[...], g_ref[...], m_ref[...], v_ref[...])

    blk = pl.BlockSpec((None, ta, B), lambda l, i: (l, i, 0))
    shp = jax.ShapeDtypeStruct((L, A, B), F32)
    return pl.pallas_call(
        body, name=name, grid=(L, A // ta),
        in_specs=[blk] * 4, out_specs=[blk] * 3, out_shape=[shp] * 3,
        compiler_params=_params("parallel", "parallel"),
    )(w, g, m, v)


def _scalar(v):
    return jnp.reshape(v, (1,)).astype(jnp.int32)


def _adamw_halves(w, g_mine, g_other, m, v, *, name):
    L, A, B = w.shape
    ta = _tile(A // 2, ROW_TILE)
    nb = A // 2 // ta

    def body(c_ref, w_ref, gm_ref, go_ref, m_ref, v_ref, g_ref, d_ref, mo_ref, vo_ref):
        g = jnp.where(pl.program_id(1) // nb == c_ref[0], gm_ref[...], go_ref[...])
        g_ref[...] = g
        d_ref[...], mo_ref[...], vo_ref[...] = _adamw_update(w_ref[...], g, m_ref[...], v_ref[...])

    blk = pl.BlockSpec((None, ta, B), lambda l, i, c_ref: (l, i, 0))
    half = pl.BlockSpec((None, ta, B), lambda l, i, c_ref: (l, i % nb, 0))
    shp = jax.ShapeDtypeStruct((L, A, B), F32)
    return pl.pallas_call(
        body, name=name,
        grid_spec=pltpu.PrefetchScalarGridSpec(num_scalar_prefetch=1, grid=(L, A // ta),
                                               in_specs=[blk, half, half, blk, blk], out_specs=[blk] * 4),
        out_shape=[shp] * 4,
        compiler_params=_params("parallel", "parallel"),
    )(_scalar(lax.axis_index("c")), w, g_mine, g_other, m, v)


def _chip_index():
    return 2 * lax.axis_index("x") + lax.axis_index("y")


def _pair_sum(g, recv, *, name):
    n, A, B = g.shape
    ta = _tile(A // 2, ROW_TILE)
    nb = A // 2 // ta

    def body(c_ref, g_ref, r_ref, o_ref):
        o_ref[...] = g_ref[...] + r_ref[...]

    return pl.pallas_call(
        body, name=name,
        grid_spec=pltpu.PrefetchScalarGridSpec(
            num_scalar_prefetch=1, grid=(n, nb),
            in_specs=[pl.BlockSpec((None, ta, B), lambda s, r, c_ref: (s, c_ref[0] * nb + r, 0)),
                      pl.BlockSpec((None, ta, B), lambda s, r, c_ref: (s, r, 0))],
            out_specs=pl.BlockSpec((None, ta, B), lambda s, r, c_ref: (s, r, 0))),
        out_shape=jax.ShapeDtypeStruct((n, A // 2, B), F32),
        compiler_params=_params("parallel", "parallel"),
    )(_scalar(lax.axis_index("c")), g, recv)


def _chip_sum(landed, own, *, name):
    n, A2, B = landed.shape
    ta = _tile(A2, ROW_TILE)

    def body(me_ref, *refs):
        slots, own_ref, o_ref = refs[:n], refs[n], refs[n + 1]
        parts = [jnp.where(me_ref[0] == s, own_ref[...], slots[s][...]) for s in range(n)]
        o_ref[...] = ((parts[0] + parts[1]) + parts[2]) + parts[3]

    def slot(s):
        return pl.BlockSpec((None, ta, B), lambda r, me_ref: (jnp.where(me_ref[0] == s, (s + 1) % n, s), r, 0))

    return pl.pallas_call(
        body, name=name,
        grid_spec=pltpu.PrefetchScalarGridSpec(
            num_scalar_prefetch=1, grid=(A2 // ta,),
            in_specs=[slot(s) for s in range(n)] + [pl.BlockSpec((None, ta, B), lambda r, me_ref: (me_ref[0], r, 0))],
            out_specs=pl.BlockSpec((ta, B), lambda r, me_ref: (r, 0))),
        out_shape=jax.ShapeDtypeStruct((A2, B), F32),
        compiler_params=_params("parallel"),
    )(_scalar(_chip_index()), *([landed] * n), own)


HBM_SPEC = pl.BlockSpec(memory_space=pl.ANY)
COMM_PARAMS = pltpu.CompilerParams(has_side_effects=True)


def _mesh_pos():
    return lax.axis_index("x"), lax.axis_index("y"), lax.axis_index("c")


def _other_chips(x, y):
    return [(1 - x, y), (x, 1 - y), (1 - x, 1 - y)]


def _remote(src, dst, send_sems, recv_sems, k, to):
    return pltpu.make_async_remote_copy(src_ref=src, dst_ref=dst, send_sem=send_sems.at[k], recv_sem=recv_sems.at[k],
                                        device_id=to, device_id_type=MESH)


def _place_own(gathered, shard, *, name):
    A, B = shard.shape
    ta = _tile(A, ROW_TILE)

    def body(me_ref, s_ref, g_ref, o_ref):
        o_ref[...] = s_ref[...]

    return pl.pallas_call(
        body, name=name,
        grid_spec=pltpu.PrefetchScalarGridSpec(
            num_scalar_prefetch=1, grid=(A // ta,),
            in_specs=[pl.BlockSpec((ta, B), lambda r, me_ref: (r, 0)), HBM_SPEC],
            out_specs=pl.BlockSpec((None, ta, B), lambda r, me_ref: (me_ref[0], r, 0))),
        out_shape=jax.ShapeDtypeStruct(gathered.shape, gathered.dtype),
        input_output_aliases={2: 0},
        compiler_params=_params("parallel"),
    )(_scalar(_chip_index()), shard, gathered)


def _half_rows(rows, c, align=8):
    assert (rows // 2) % align == 0
    return pl.ds(pl.multiple_of(c * (rows // 2), align), rows // 2)


BF16_ROWS = 16


def _halved(rows):
    return rows % (2 * BF16_ROWS) == 0


def _gather_copies(srcs, lands, send_sems, recv_sems):
    x, y, c = _mesh_pos()
    me = 2 * x + y
    out = []
    for k in range(len(srcs)):
        a = srcs[k].shape[0]
        rows = _half_rows(a, c, BF16_ROWS) if _halved(a) else pl.ds(0, a)
        for j, (px, py) in enumerate(_other_chips(x, y)):
            send = _remote(srcs[k].at[rows], lands[k].at[me, rows], send_sems, recv_sems, 3 * k + j, (px, py, c))
            recv = _remote(srcs[k].at[rows], lands[k].at[2 * px + py, rows], send_sems, recv_sems, 3 * k + j, (px, py, c))
            out.append((send, recv))
    return out


def _gather_start(srcs):
    nl, n = len(srcs), len(srcs[0])
    lands = [[lax.empty((N_CHIPS,) + s.shape, s.dtype) for s in sl] for sl in srcs]
    flat = [a for l in range(nl) for a in srcs[l] + lands[l]]

    def body(*refs):
        bufs, sems, token = refs[:len(flat)], refs[len(flat):len(flat) + 2 * nl], refs[-1]
        for l in range(nl):
            mine = bufs[2 * n * l:2 * n * (l + 1)]
            for send, _ in _gather_copies(mine[:n], mine[n:], sems[2 * l], sems[2 * l + 1]):
                send.start()
        token[...] = jnp.zeros_like(token)

    res = pl.pallas_call(
        body, name="weight_gather_start",
        in_specs=[HBM_ONLY] * len(flat),
        out_specs=[SEM_SPEC] * (2 * nl) + [HBM_ONLY] * len(flat) + [pl.BlockSpec(memory_space=pltpu.VMEM)],
        out_shape=[pltpu.SemaphoreType.DMA((3 * n,))] * (2 * nl) + [pltpu.HBM(a.shape, a.dtype) for a in flat]
        + [jax.ShapeDtypeStruct((8, LANES), F32)],
        input_output_aliases={i: 2 * nl + i for i in range(len(flat))},
        compiler_params=SPLIT_PARAMS,
    )(*[pltpu.with_memory_space_constraint(a, pltpu.HBM) for a in flat])
    bufs = res[2 * nl:2 * nl + len(flat)]
    state = [(res[2 * l], res[2 * l + 1], list(bufs[2 * n * l:2 * n * l + n]), list(bufs[2 * n * l + n:2 * n * (l + 1)]))
             for l in range(nl)]
    return state, res[-1][0:1, 0:1]


def _gather_wait(state, after, *, name):
    send_sems, recv_sems, srcs, lands = state
    n = len(srcs)

    def body(*refs):
        for send, recv in _gather_copies(refs[:n], refs[n:2 * n], refs[2 * n], refs[2 * n + 1]):
            send.wait_send()
            recv.wait_recv()

    res = pl.pallas_call(
        body, name=name,
        in_specs=[HBM_ONLY] * (2 * n) + [SEM_SPEC, SEM_SPEC, HBM_SPEC],
        out_specs=[HBM_ONLY] * (2 * n),
        out_shape=[pltpu.HBM(a.shape, a.dtype) for a in srcs + lands],
        input_output_aliases={i: i for i in range(2 * n)},
        compiler_params=SPLIT_PARAMS,
    )(*srcs, *lands, send_sems, recv_sems, after)
    return list(res[:n]), list(res[n:])


def _gather_forward(lands, *, name):
    n = len(lands)

    def body(*refs):
        bufs, outs = refs[:n], refs[n:2 * n]
        send_sems, recv_sems = refs[2 * n:]
        x, y, c = _mesh_pos()
        copies, waits = [], []
        for k in range(n):
            a = lands[k].shape[1]
            if not _halved(a):
                continue
            for j, (px, py) in enumerate(_other_chips(x, y)):
                mine = 2 * px + py, _half_rows(a, c, BF16_ROWS)
                copies.append(_remote(bufs[k].at[mine], outs[k].at[mine], send_sems, recv_sems, 3 * k + j, (x, y, 1 - c)))
                lands_here = outs[k].at[2 * px + py, _half_rows(a, 1 - c, BF16_ROWS)]
                waits.append(_remote(lands_here, lands_here, send_sems, recv_sems, 3 * k + j, (x, y, 1 - c)))
        for cp in copies:
            cp.start()
        for cp in waits:
            cp.wait_recv()
        for cp in copies:
            cp.wait_send()

    return pl.pallas_call(
        body, name=name,
        in_specs=[HBM_SPEC] * n, out_specs=[HBM_SPEC] * n,
        out_shape=[jax.ShapeDtypeStruct(a.shape, a.dtype) for a in lands],
        scratch_shapes=[pltpu.SemaphoreType.DMA((3 * n,)), pltpu.SemaphoreType.DMA((3 * n,))],
        input_output_aliases={i: i for i in range(n)},
        compiler_params=COMM_PARAMS,
    )(*lands)


def _sibling_exchange(gs, *, name):
    n = len(gs)

    def body(*refs):
        ins, outs = refs[:n], refs[n:2 * n]
        send_sems, recv_sems = refs[2 * n:]
        x, y, c = _mesh_pos()
        copies = [_remote(ins[k].at[:, _half_rows(gs[k].shape[1], 1 - c)], outs[k], send_sems, recv_sems, k, (x, y, 1 - c))
                  for k in range(n)]
        for cp in copies:
            cp.start()
        for cp in copies:
            cp.wait()

    return pl.pallas_call(
        body, name=name,
        in_specs=[HBM_SPEC] * n, out_specs=[HBM_SPEC] * n,
        out_shape=[jax.ShapeDtypeStruct((g.shape[0], g.shape[1] // 2, g.shape[2]), g.dtype) for g in gs],
        scratch_shapes=[pltpu.SemaphoreType.DMA((n,)), pltpu.SemaphoreType.DMA((n,))],
        compiler_params=COMM_PARAMS,
    )(*gs)


HBM_ONLY = pl.BlockSpec(memory_space=pltpu.HBM)
SEM_SPEC = pl.BlockSpec(memory_space=pltpu.SEMAPHORE)
SPLIT_PARAMS = pltpu.CompilerParams(has_side_effects=pltpu.SideEffectType.DATAFLOW_SIDE_EFFECTING)


def _scatter_copies(srcs, lands, send_sems, recv_sems):
    x, y, c = _mesh_pos()
    me = 2 * x + y
    out = []
    for k in range(len(srcs)):
        for j, (px, py) in enumerate(_other_chips(x, y)):
            s = 2 * px + py
            send = _remote(srcs[k].at[s], lands[k].at[me], send_sems, recv_sems, 3 * k + j, (px, py, c))
            recv = _remote(srcs[k].at[s], lands[k].at[s], send_sems, recv_sems, 3 * k + j, (px, py, c))
            out.append((send, recv))
    return out


def _scatter_start(ps, *, name):
    n = len(ps)
    lands = [lax.empty(p.shape, p.dtype) for p in ps]

    def body(*refs):
        srcs, zones = refs[:n], refs[n:2 * n]
        send_sems, recv_sems, token = refs[2 * n], refs[2 * n + 1], refs[-1]
        for send, _ in _scatter_copies(srcs, zones, send_sems, recv_sems):
            send.start()
        token[...] = jnp.zeros_like(token)

    hbm = lambda a: pltpu.HBM(a.shape, a.dtype)
    res = pl.pallas_call(
        body, name=name,
        in_specs=[HBM_ONLY] * (2 * n),
        out_specs=[SEM_SPEC, SEM_SPEC] + [HBM_ONLY] * (2 * n) + [pl.BlockSpec(memory_space=pltpu.VMEM)],
        out_shape=[pltpu.SemaphoreType.DMA((3 * n,)), pltpu.SemaphoreType.DMA((3 * n,))] + [hbm(a) for a in ps + lands]
        + [jax.ShapeDtypeStruct((8, LANES), F32)],
        input_output_aliases={i: 2 + i for i in range(2 * n)},
        compiler_params=SPLIT_PARAMS,
    )(*[pltpu.with_memory_space_constraint(a, pltpu.HBM) for a in ps + lands])
    return (res[0], res[1], list(res[2:2 + n]), list(res[2 + n:2 + 2 * n])), res[-1][0:1, 0:1]


def _scatter_wait(started, after, *, name):
    ng = len(started)
    sizes = [len(st[2]) for st in started]
    offs = [2 * sum(sizes[:i]) for i in range(ng + 1)]
    flat = [a for (_, _, ps, lands) in started for a in ps + lands]

    def body(*refs):
        bufs, sems = refs[:len(flat)], refs[len(flat):len(flat) + 2 * ng]
        for i, n in enumerate(sizes):
            srcs, zones = bufs[offs[i]:offs[i] + n], bufs[offs[i] + n:offs[i + 1]]
            for send, recv in _scatter_copies(srcs, zones, sems[2 * i], sems[2 * i + 1]):
                send.wait_send()
                recv.wait_recv()

    res = pl.pallas_call(
        body, name=name,
        in_specs=[HBM_ONLY] * len(flat) + [SEM_SPEC] * (2 * ng) + [HBM_SPEC],
        out_specs=[HBM_ONLY] * len(flat),
        out_shape=[pltpu.HBM(a.shape, a.dtype) for a in flat],
        input_output_aliases={i: i for i in range(len(flat))},
        compiler_params=SPLIT_PARAMS,
    )(*flat, *[s for (ss, rs, _, _) in started for s in (ss, rs)], after)
    return [(list(res[offs[i]:offs[i] + n]), list(res[offs[i] + n:offs[i + 1]])) for i, n in enumerate(sizes)]


def _sibling_share(hs):
    n = len(hs)

    def body(*refs):
        ins, outs = refs[:n], refs[n:2 * n]
        send_sems, recv_sems = refs[2 * n:]
        x, y, c = _mesh_pos()
        copies = [_remote(ins[k], outs[k], send_sems, recv_sems, k, (x, y, 1 - c)) for k in range(n)]
        for cp in copies:
            cp.start()
        for cp in copies:
            cp.wait()

    return pl.pallas_call(
        body, name="grad_sibling_share",
        in_specs=[HBM_SPEC] * n, out_specs=[HBM_SPEC] * n,
        out_shape=[jax.ShapeDtypeStruct(h.shape, h.dtype) for h in hs],
        scratch_shapes=[pltpu.SemaphoreType.DMA((n,)), pltpu.SemaphoreType.DMA((n,))],
        compiler_params=COMM_PARAMS,
    )(*hs)


def _allreduce_small(part):
    rows, C = part.shape

    def body(p_ref, o_ref, slots, send_sems, recv_sems):
        x, y, c = _mesh_pos()
        me = 4 * x + 2 * y + c
        slots[me] = p_ref[...]
        copies = []
        for k in range(1, 8):
            kx, ky, kc = (k >> 2) & 1, (k >> 1) & 1, k & 1
            peer = (x ^ kx if kx else x, y ^ ky if ky else y, c ^ kc if kc else c)
            cp = _remote(p_ref, slots.at[me], send_sems, recv_sems, k - 1, peer)
            cp.start()
            copies.append((cp, peer))
        for k, (cp, peer) in enumerate(copies):
            src = 4 * peer[0] + 2 * peer[1] + peer[2]
            _remote(p_ref, slots.at[src], send_sems, recv_sems, k, peer).wait_recv()
        for cp, _ in copies:
            cp.wait_send()
        total = slots[0]
        for d in range(1, 8):
            total = total + slots[d]
        o_ref[...] = total

    return pl.pallas_call(
        body, name="small_grad_allreduce",
        in_specs=[pl.BlockSpec(memory_space=pltpu.VMEM)], out_specs=pl.BlockSpec(memory_space=pltpu.VMEM),
        out_shape=jax.ShapeDtypeStruct((rows, C), F32),
        scratch_shapes=[pltpu.VMEM((8, rows, C), F32), pltpu.SemaphoreType.DMA((7,)), pltpu.SemaphoreType.DMA((7,))],
        compiler_params=pltpu.CompilerParams(has_side_effects=True, vmem_limit_bytes=VMEM_LIMIT_BYTES),
    )(part)


def _pad_w_uq(w):
    lead = w.shape[:-1]
    w = w.reshape(lead + (MLA_HEADS, MLA_QK))
    w = jnp.concatenate([w, jnp.zeros(lead + (MLA_HEADS, MLA_PAD - MLA_QK), w.dtype)], axis=-1)
    return w.reshape(lead + (MLA_HEADS * MLA_PAD,))


def _unpad_w_uq(g):
    lead = g.shape[:-1]
    return g.reshape(lead + (MLA_HEADS, MLA_PAD))[..., :MLA_QK].reshape(lead + (MLA_HEADS * MLA_QK,))


def _t(a):
    return jnp.swapaxes(a, -1, -2)


def _shards_of_cols(w):
    A, NB = w.shape
    return w.reshape(A, N_CHIPS, NB // N_CHIPS).transpose(1, 0, 2)


BIG = ("w_in", "w_uq", "w_ukv", "w_out", "w_up", "w_down")
SMALL = ("attn_pre_norm", "forget_bias", "swa_sinks", "rel_bias", "q_latent_norm", "kv_latent_norm", "group_norm",
         "attn_post_norm", "ffn_pre_norm", "conv_b", "ffn_post_norm")
WEIGHTS = ("attn_pre_norm", "w_in", "forget_bias", "swa_sinks", "rel_bias", "q_latent_norm", "w_uq", "kv_latent_norm",
           "w_ukv", "group_norm", "w_out", "attn_post_norm", "ffn_pre_norm", "w_up", "conv_w", "conv_b", "w_down",
           "ffn_post_norm")


def _pack(arrs, cols, row_mult):
    flat = jnp.concatenate([a.reshape(-1) for a in arrs])
    n = flat.shape[0]
    per = cols * row_mult
    total = -(-n // per) * per
    return jnp.pad(flat, (0, total - n)).reshape(total // cols, cols)


def _unpack(packed, shapes):
    flat = packed.reshape(-1)
    out, off = [], 0
    for shp in shapes:
        n = int(np.prod(shp))
        out.append(flat[off:off + n].reshape(shp))
        off += n
    return out


LAYER_KEYS = ("w_qkv_t", "w_lat_t", "w_in_t", "w_uq_p", "w_uq_t", "w_ukv", "w_ukv_t", "w_out", "w_up", "w_down", "conv_w")


def _layer_weights(gathered):
    cols = lambda g: g.transpose(1, 0, 2).reshape(g.shape[1], N_CHIPS * g.shape[2])
    w_in_t = _t(gathered["w_in"]).reshape(IN_COLS, D_MODEL)
    w_in_t = jnp.pad(w_in_t, ((0, IN_ROWS - IN_COLS), (0, 0)))
    w_uq_p = _pad_w_uq(cols(gathered["w_uq"]))
    w_ukv = cols(gathered["w_ukv"])
    return dict(w_qkv_t=w_in_t[:QKV_ROWS], w_lat_t=w_in_t[QKV_ROWS:], w_in_t=w_in_t, w_uq_p=w_uq_p, w_uq_t=_t(w_uq_p),
                w_ukv=w_ukv, w_ukv_t=_t(w_ukv), w_out=gathered["w_out"].reshape(D_MODEL, D_MODEL), w_up=gathered["w_up"],
                w_down=gathered["w_down"].reshape(D_FF, D_MODEL), conv_w=cols(gathered["conv_w"]))


def _local_step(x, target, W, layer_weights, layer_done):
    W = dict(W, **{key: [None] * DEPTH for key in LAYER_KEYS})
    S = x.shape[0]
    tq_tabs, tm_tabs = _rope_tables(S)
    onehot_t = _rel_onehot_t()
    bias_t = _bias_table(W["rel_bias"].T, onehot_t).reshape(SWA_KV_HEADS, SWA_GROUP, 2 * WINDOW, WINDOW)
    bias_t = bias_t.transpose(0, 2, 1, 3).reshape(SWA_KV_HEADS, 2 * WINDOW, GW)
    row = lambda a: a.reshape(1, -1)
    col = lambda a: a.reshape(-1, 1)
    fox_rows = (FOX_ROW0, FOX_ROW0 + FOX_HEADS * HEAD_DIM, FOX_ROW0 + 2 * FOX_HEADS * HEAD_DIM, SWA_Q_HEADS)
    fox = dict(rows=fox_rows, H=FOX_HEADS, Dk=HEAD_DIM, Dv=HEAD_DIM, scale=HEAD_DIM ** -0.5)
    mla = dict(rows=(0, 0, 0, SWA_Q_HEADS + FOX_HEADS), H=MLA_HEADS, Dk=MLA_PAD, Dv=HEAD_DIM, scale=MLA_QK ** -0.5)

    saved = []
    h = _rms_fwd(x, row(W["attn_pre_norm"][0]), name="rms_in")
    for l in range(DEPTH):
        sv = {"x0": x, "h1": h}
        for key, val in layer_weights(l, h).items():
            W[key][l] = val
        qkv = _matmul(W["w_qkv_t"][l], h, tb=True, out_dtype=BF16, name="proj_qkv")
        lat = _matmul(W["w_lat_t"][l], h, tb=True, name="proj_lat")
        oa, lse_a = _swa_fwd(qkv, bias_t, W["swa_sinks"][l], name="swa_fwd")
        fb_col = jnp.pad(col(W["forget_bias"][l]), ((0, GATE_ROWS - FOX_HEADS), (0, 0)))
        f4 = _gate_fwd(lat, fb_col, name="fox_gate_fwd")[:FOX_HEADS]
        f2 = f4 * LOG2E
        f_row, f_col = f2[:, None, :], f2.T
        of, lse_f = _attn_fwd(qkv, qkv, qkv, f_row=f_row, f_col=f_col, name="fox_fwd", **fox)
        nq, nkv, qm, km, vm = _mla_prep_fwd(lat, col(W["q_latent_norm"][l]), col(W["kv_latent_norm"][l]), W["w_uq_t"][l],
                                            W["w_ukv_t"][l], tq_tabs, tm_tabs, name="mla_prep_fwd")
        oc, lse_c = _attn_fwd(qm, km, vm, name="mla_fwd", **mla)
        mixed = _group_norm_fwd(oa, of, oc, col(W["group_norm"][l]), name="group_norm_fwd")
        y = _matmul(mixed, W["w_out"][l], ta=True, name="proj_out")
        x1, h2 = _resid_rms(x, y, row(W["attn_post_norm"][l]), row(W["ffn_pre_norm"][l]), name="attn_resid")
        a = _matmul(h2, W["w_up"][l], b_shards=True, out_dtype=BF16, name="ffn_up")
        z = _conv_geglu_fwd(a, W["conv_w"][l], row(W["conv_b"][l]), name="conv_geglu_fwd")
        y2 = _matmul(z, W["w_down"][l], name="ffn_down")
        g_next = row(W["attn_pre_norm"][l + 1]) if l + 1 < DEPTH else None
        x2, h_next = _resid_rms(x1, y2, row(W["ffn_post_norm"][l]), g_next, name="ffn_resid")
        sv.update(qkv=qkv, lat=lat, oa=oa, lse_a=lse_a, fb_col=fb_col, f_row=f_row, f_col=f_col, of=of, lse_f=lse_f,
                  nq=nq, nkv=nkv, qm=qm, km=km, vm=vm, oc=oc, lse_c=lse_c, mixed=mixed, y=y, x1=x1, h2=h2, a=a, z=z, y2=y2)
        saved.append(sv)
        x, h = x2, h_next

    loss, dx = _loss_head(x, target)

    G = {k: [None] * DEPTH for k in WEIGHTS if k != "rel_bias" and k not in BIG}
    dbias_layers = [None] * DEPTH
    for l in reversed(range(DEPTH)):
        sv = saved[l]
        gb = {}
        dy2, dg = _rms_bwd(sv["y2"], row(W["ffn_post_norm"][l]), dx, out_dtype=BF16, name="ffn_post_bwd")
        G["ffn_post_norm"][l] = dg[0]
        dz = _matmul(dy2, W["w_down"][l], tb=True, name="ffn_down_dx")
        gb["w_down"] = _matmul(sv["z"], dy2, ta=True, name="ffn_down_dw").reshape(N_CHIPS, D_FF // N_CHIPS, D_MODEL)
        da, dcw, dcb = _conv_geglu_bwd(sv["a"], W["conv_w"][l], row(W["conv_b"][l]), dz, name="conv_geglu_bwd")
        G["conv_w"][l] = dcw.transpose(1, 0, 2).reshape(3, 2 * D_FF)
        G["conv_b"][l] = dcb.reshape(2 * D_FF)
        dh2 = _matmul(da, W["w_up"][l], tb=True, b_shards=True, a_halves=True, name="ffn_up_dx")
        gb["w_up"] = _matmul(sv["h2"], da, ta=True, out_shards=True, b_halves=True, name="ffn_up_dw")
        token = layer_done(l, gb)
        gb = {}
        dx1, dg = _rms_bwd(sv["x1"], row(W["ffn_pre_norm"][l]) + token, dh2, resid=dx, out_dtype=F32, name="ffn_pre_bwd")
        G["ffn_pre_norm"][l] = dg[0]
        dy, dg = _rms_bwd(sv["y"], row(W["attn_post_norm"][l]), dx1, out_dtype=BF16, name="attn_post_bwd")
        G["attn_post_norm"][l] = dg[0]
        dmixed = _matmul(W["w_out"][l], dy, tb=True, name="proj_out_dx")
        gb["w_out"] = _matmul(sv["mixed"], dy, name="proj_out_dw").reshape(N_CHIPS, D_MODEL // N_CHIPS, D_MODEL)
        doa, dof, doc, dg, delta = _group_norm_bwd(sv["oa"], sv["of"], sv["oc"], col(W["group_norm"][l]), dmixed,
                                                   name="group_norm_bwd")
        G["group_norm"][l] = dg[:, 0]
        dqa, dkva, dbias_l, dsink = _swa_bwd(sv["qkv"], bias_t, W["swa_sinks"][l], doa, sv["lse_a"],
                                             delta.reshape(-1, S), name="swa_bwd")
        dbias_layers[l] = (dbias_l.reshape(SWA_KV_HEADS, 2 * WINDOW, SWA_GROUP, WINDOW).transpose(0, 2, 1, 3)
                           .reshape(SWA_Q_HEADS, -1))
        G["swa_sinks"][l] = dsink[:, 0]
        dqf, dkf, dvf, dfk = _attn_bwd(sv["qkv"], sv["qkv"], sv["qkv"], do=dof, lse=sv["lse_f"], delta=delta,
                                       f_row=sv["f_row"], f_col=sv["f_col"], name="fox_bwd", **fox)
        dF = jnp.pad(dfk.T, ((0, GATE_ROWS - FOX_HEADS), (0, 0)))
        dflog, dfb = _gate_bwd(sv["lat"], sv["fb_col"], dF, name="fox_gate_bwd")
        G["forget_bias"][l] = dfb[:FOX_HEADS, 0]
        dqm, dkm, dvm = _attn_bwd(sv["qm"], sv["km"], sv["vm"], do=doc, lse=sv["lse_c"], delta=delta, name="mla_bwd", **mla)
        dlat, dwq_t, dwkv_t, dgq, dgkv = _mla_prep_bwd(
            sv["lat"], sv["nq"], sv["nkv"], col(W["q_latent_norm"][l]), col(W["kv_latent_norm"][l]), W["w_uq_p"][l],
            W["w_ukv"][l], tq_tabs, tm_tabs, dqm, dkm, dvm, dflog, name="mla_prep_bwd")
        gb["w_uq"], gb["w_ukv"] = _shards_of_cols(_unpad_w_uq(dwq_t.T)), _shards_of_cols(dwkv_t.T)
        G["q_latent_norm"][l], G["kv_latent_norm"][l] = dgq[:, 0], dgkv[:, 0]
        dproj = _dproj_cast(dqa, dkva, dqf, dkf, dvf, dlat, name="dproj_cast")
        dh1 = _matmul(dproj, W["w_in_t"][l], ta=True, name="proj_in_dx")
        dw_in_t = _matmul(dproj, sv["h1"], name="proj_in_dw")
        gb["w_in"] = _t(dw_in_t[:IN_COLS].reshape(N_CHIPS, IN_COLS // N_CHIPS, D_MODEL))
        token = layer_done(l, gb)
        dx, dg = _rms_bwd(sv["x0"], row(W["attn_pre_norm"][l]) + token, dh1, resid=dx1, out_dtype=F32, name="attn_pre_bwd")
        G["attn_pre_norm"][l] = dg[0]

    grads = {k: jnp.stack(v) for k, v in G.items()}
    grads["rel_bias"] = _bias_table_bwd(jnp.stack(dbias_layers), onehot_t).T
    return loss, dx, grads


def kernel(x, attn_pre_norm, w_in, forget_bias, swa_sinks, rel_bias, q_latent_norm, w_uq, kv_latent_norm, w_ukv, group_norm, w_out, attn_post_norm, ffn_pre_norm, w_up, conv_w, conv_b, w_down, ffn_post_norm, loss_target, m_attn_pre_norm, m_w_in, m_forget_bias, m_swa_sinks, m_rel_bias, m_q_latent_norm, m_w_uq, m_kv_latent_norm, m_w_ukv, m_group_norm, m_w_out, m_attn_post_norm, m_ffn_pre_norm, m_w_up, m_conv_w, m_conv_b, m_w_down, m_ffn_post_norm, v_attn_pre_norm, v_w_in, v_forget_bias, v_swa_sinks, v_rel_bias, v_q_latent_norm, v_w_uq, v_kv_latent_norm, v_w_ukv, v_group_norm, v_w_out, v_attn_post_norm, v_ffn_pre_norm, v_w_up, v_conv_w, v_conv_b, v_w_down, v_ffn_post_norm):
    args = dict(locals())
    w = {k: args[k] for k in WEIGHTS}
    m = {k: args["m_" + k] for k in WEIGHTS}
    v = {k: args["v_" + k] for k in WEIGHTS}

    sent = BIG + ("conv_w",)
    gather_state, token = _gather_start([[w[k][l] if k == "conv_w" else w[k][l].astype(BF16) for k in sent]
                                         for l in range(DEPTH)])
    W = {k: w[k] for k in SMALL}
    W["attn_pre_norm"] = W["attn_pre_norm"] + token

    def layer_weights(l, after):
        srcs, lands = _gather_wait(gather_state[l], after, name=f"weight_gather_wait_{l}")
        lands = _gather_forward(lands, name=f"weight_gather_forward_{l}")
        lands = [_place_own(g, s, name="place_own_shard") for g, s in zip(lands, srcs)]
        return _layer_weights(dict(zip(sent, lands)))

    started, groups = [], []

    def layer_done(l, gb):
        keys = [k for k in BIG if k in gb]
        gs = [gb[k] for k in keys]
        tag = f"{l}_{keys[0]}"
        recv = _sibling_exchange(gs, name="grad_sibling_exchange_" + tag)
        pair = [_pair_sum(gk, rk, name="grad_pair_sum") for gk, rk in zip(gs, recv)]
        state, token = _scatter_start(pair, name="grad_scatter_start_" + tag)
        started.append(state)
        groups.append((l, keys))
        return token

    loss_part, dx, g = _local_step(x[0], loss_target[0], W, layer_weights, layer_done)
    loss = lax.psum(loss_part, ("x", "y", "c"))

    reduced = {}
    for (l, keys), (pair, zones) in zip(groups, _scatter_wait(started, dx, name="grad_scatter_wait")):
        for k, p, z in zip(keys, pair, zones):
            reduced[k, l] = _chip_sum(z, p, name="grad_chip_sum")
    mine = [jnp.stack([reduced[k, l] for l in range(DEPTH)]) for k in BIG]
    other = _sibling_share(mine)
    out_g, out_d, out_m, out_v = {}, {}, {}, {}
    for k, g_mine, g_other in zip(BIG, mine, other):
        out_g[k], out_d[k], out_m[k], out_v[k] = _adamw_halves(w[k], g_mine, g_other, m[k], v[k], name="adamw_" + k)

    small_shapes = [w[k].shape for k in SMALL]
    reduced = _allreduce_small(_pack([g[k] for k in SMALL] + [g["conv_w"]], LANES, 8))
    *g_small, g_cw = _unpack(reduced, small_shapes + [g["conv_w"].shape])
    chip = 2 * lax.axis_index("x") + lax.axis_index("y")
    g_small.append(lax.dynamic_slice_in_dim(g_cw, chip * FF_SHARD, FF_SHARD, axis=2))
    names = SMALL + ("conv_w",)
    shapes = small_shapes + [w["conv_w"].shape]
    packed = lambda arrs: _pack(arrs, LANES, ROW_TILE)[None]
    d_s, m_s, v_s = _adamw(packed([w[k] for k in names]), packed(g_small), packed([m[k] for k in names]),
                           packed([v[k] for k in names]), name="adamw_small")
    out_g.update(zip(names, g_small))
    out_d.update(zip(names, _unpack(d_s, shapes)))
    out_m.update(zip(names, _unpack(m_s, shapes)))
    out_v.update(zip(names, _unpack(v_s, shapes)))

    return (loss, dx[None], *[out_g[k] for k in WEIGHTS], *[out_d[k] for k in WEIGHTS],
            *[out_m[k] for k in WEIGHTS], *[out_v[k] for k in WEIGHTS])
```

```python
import math

import numpy as np
import jax
import jax.numpy as jnp
from jax import lax
from jax.experimental import pallas as pl
from jax.experimental.pallas import tpu as pltpu

F32 = jnp.float32
BF16 = jnp.bfloat16

D_MODEL = 1024
DEPTH = 4
HEAD_DIM = 64
SWA_Q_HEADS = 8
SWA_KV_HEADS = 2
SWA_GROUP = SWA_Q_HEADS // SWA_KV_HEADS
WINDOW = 128
FOX_HEADS = 4
MLA_HEADS = 4
MLA_Q_RANK = 256
MLA_KV_RANK = 128
MLA_NOPE = 64
MLA_ROPE = 32
MLA_QK = MLA_NOPE + MLA_ROPE
ROPE_THETA = 10000.0
REL_BUCKETS = 32
REL_MAX_DIST = 128
D_FF = 2816
EPS = 1e-6
NEG_INF = -1e30
LANES = 128
N_CHIPS = 4

IN_COLS = 1956
IN_ROWS = 2048
QKV_ROWS = 1536
LAT_ROWS = IN_ROWS - QKV_ROWS
LAT_SHIFT = FOX_HEADS
FOX_ROW0 = 768
MLA_PAD = LANES
GATE_ROWS = 8

ADAM_LR = 0.001
ADAM_B1 = 0.9
ADAM_B2 = 0.999
ADAM_EPS = 1e-08
ADAM_WD = 0.01
ADAM_STEP = 10

VMEM_LIMIT_BYTES = 48 * 1024 * 1024
ATT_TILE = 512
LOG2E = math.log2(math.e)
ROW_TILE = 256
MESH = pl.DeviceIdType.MESH

NT = (((1,), (1,)), ((), ()))
TN = (((0,), (0,)), ((), ()))
NN = (((1,), (0,)), ((), ()))


def _params(*sem):
    return pltpu.CompilerParams(dimension_semantics=sem, vmem_limit_bytes=VMEM_LIMIT_BYTES)


def _tile(dim, cap):
    for t in (2816, 2048, 1408, 1024, 512, 256, 128, 64, 32, 16, 8):
        if t <= cap and dim % t == 0:
            return t
    return dim


def _dot(a, b, dims=NN):
    return lax.dot_general(a, b, dims, preferred_element_type=F32)


def _split3(a):
    a1 = a.astype(BF16)
    r1 = a - a1.astype(F32)
    a2 = r1.astype(BF16)
    a3 = (r1 - a2.astype(F32)).astype(BF16)
    return a1, a2, a3


FF_SHARD = 2 * D_FF // N_CHIPS
MATMUL_VMEM_BYTES = 40 * 1024 * 1024


def _matmul(a, b, *, ta=False, tb=False, out_dtype=F32, name, b_shards=False, out_shards=False, a_halves=False,
            b_halves=False):
    if a_halves:
        M, K = a.shape[1], 2 * a.shape[2]
    elif ta:
        K, M = a.shape
    else:
        M, K = a.shape
    if b_halves:
        K2, N = b.shape[1], 2 * b.shape[2]
    elif b_shards:
        K2, N = (2 * D_FF, D_MODEL) if tb else (D_MODEL, 2 * D_FF)
    elif tb:
        N, K2 = b.shape
    else:
        K2, N = b.shape
    assert K == K2, (a.shape, b.shape)
    tm, tn = (M if M <= 2048 else _tile(M, 1408)), _tile(N, 1408)
    tk = FF_SHARD if (b_shards and tb) else _tile(K, 2816)
    out_bytes = jnp.dtype(out_dtype).itemsize
    while 2 * 2 * tk * (tm + tn) + (4 + 2 * out_bytes) * tm * tn > MATMUL_VMEM_BYTES and tk % 256 == 0:
        tk //= 2
    nk = K // tk
    dims = (((0 if ta else 1,), (1 if tb else 0,)), ((), ()))

    def body(a_ref, b_ref, o_ref, acc_ref):
        k = pl.program_id(2)

        @pl.when(k == 0)
        def _():
            acc_ref[...] = jnp.zeros_like(acc_ref)

        acc_ref[...] += lax.dot_general(a_ref[...], b_ref[...], dims, preferred_element_type=F32)

        @pl.when(k == nk - 1)
        def _():
            o_ref[...] = acc_ref[...].astype(o_ref.dtype)

    if a_halves:
        nh = K // 2 // tk
        a_spec = pl.BlockSpec((None, tm, tk), lambda i, j, k: (k // nh, i, k % nh))
    else:
        a_spec = pl.BlockSpec((tk, tm), lambda i, j, k: (k, i)) if ta else pl.BlockSpec((tm, tk), lambda i, j, k: (i, k))
    if b_halves:
        nh = N // 2 // tn
        b_spec = pl.BlockSpec((None, tk, tn), lambda i, j, k: (j // nh, k, j % nh))
    elif b_shards and tb:
        assert tk == FF_SHARD
        b_spec = pl.BlockSpec((None, tn, tk), lambda i, j, k: (k, j, 0))
    elif b_shards:
        assert tn == FF_SHARD
        b_spec = pl.BlockSpec((None, tk, tn), lambda i, j, k: (j, k, 0))
    else:
        b_spec = pl.BlockSpec((tn, tk), lambda i, j, k: (j, k)) if tb else pl.BlockSpec((tk, tn), lambda i, j, k: (k, j))
    if out_shards:
        assert tn == FF_SHARD
        out_spec = pl.BlockSpec((None, tm, tn), lambda i, j, k: (j, i, 0))
        out_shape = jax.ShapeDtypeStruct((N // tn, M, tn), out_dtype)
    else:
        out_spec = pl.BlockSpec((tm, tn), lambda i, j, k: (i, j))
        out_shape = jax.ShapeDtypeStruct((M, N), out_dtype)
    return pl.pallas_call(
        body, name=name, grid=(M // tm, N // tn, nk),
        in_specs=[a_spec, b_spec], out_specs=out_spec, out_shape=out_shape,
        scratch_shapes=[pltpu.VMEM((tm, tn), F32)],
        compiler_params=_params("parallel", "parallel", "arbitrary"),
    )(a, b)


def _seg_rms(xs, g):
    r = lax.rsqrt(jnp.mean(xs * xs, axis=-1, keepdims=True) + EPS)
    return xs * r * g


def _seg_rms_bwd(xs, g, dy):
    r = lax.rsqrt(jnp.mean(xs * xs, axis=-1, keepdims=True) + EPS)
    gd = dy * g
    c = jnp.mean(gd * xs, axis=-1, keepdims=True)
    dx = r * gd - xs * (r * r * r * c)
    dg = jnp.sum(dy * (xs * r), axis=0, keepdims=True)
    return dx, dg


def _rms_fwd(x, g, *, name):
    S, W = x.shape
    tm = _tile(S, 512)

    def body(x_ref, g_ref, o_ref):
        o_ref[...] = _seg_rms(x_ref[...], g_ref[...]).astype(o_ref.dtype)

    return pl.pallas_call(
        body, name=name, grid=(S // tm,),
        in_specs=[pl.BlockSpec((tm, W), lambda i: (i, 0)), pl.BlockSpec((1, W), lambda i: (0, 0))],
        out_specs=pl.BlockSpec((tm, W), lambda i: (i, 0)),
        out_shape=jax.ShapeDtypeStruct((S, W), BF16),
        compiler_params=_params("parallel"),
    )(x, g)


def _rms_bwd(x, g, dy, *, resid=None, out_dtype, name):
    S, W = x.shape
    tm = _tile(S, 512)
    has_resid = resid is not None

    def body(*refs):
        if has_resid:
            x_ref, g_ref, dy_ref, r_ref, dx_ref, dg_ref = refs
        else:
            x_ref, g_ref, dy_ref, dx_ref, dg_ref = refs

        @pl.when(pl.program_id(0) == 0)
        def _():
            dg_ref[...] = jnp.zeros_like(dg_ref)

        dx, dg = _seg_rms_bwd(x_ref[...], g_ref[...], dy_ref[...])
        if has_resid:
            dx = dx + r_ref[...]
        dx_ref[...] = dx.astype(dx_ref.dtype)
        dg_ref[...] += dg

    row = pl.BlockSpec((tm, W), lambda i: (i, 0))
    vec = pl.BlockSpec((1, W), lambda i: (0, 0))
    ins = [x, g, dy] + ([resid] if has_resid else [])
    return pl.pallas_call(
        body, name=name, grid=(S // tm,),
        in_specs=[row, vec, row] + ([row] if has_resid else []),
        out_specs=[row, vec],
        out_shape=[jax.ShapeDtypeStruct((S, W), out_dtype), jax.ShapeDtypeStruct((1, W), F32)],
        compiler_params=_params("arbitrary"),
    )(*ins)


def _resid_rms(x, y, g_post, g_next, *, name):
    S, W = x.shape
    tm = _tile(S, 512)
    with_next = g_next is not None

    def body(*refs):
        if with_next:
            x_ref, y_ref, gp_ref, gn_ref, xo_ref, h_ref = refs
        else:
            x_ref, y_ref, gp_ref, xo_ref = refs
        xn = x_ref[...] + _seg_rms(y_ref[...], gp_ref[...])
        xo_ref[...] = xn
        if with_next:
            h_ref[...] = _seg_rms(xn, gn_ref[...]).astype(BF16)

    row = pl.BlockSpec((tm, W), lambda i: (i, 0))
    vec = pl.BlockSpec((1, W), lambda i: (0, 0))
    outs = [jax.ShapeDtypeStruct((S, W), F32)] + ([jax.ShapeDtypeStruct((S, W), BF16)] if with_next else [])
    res = pl.pallas_call(
        body, name=name, grid=(S // tm,),
        in_specs=[row, row, vec] + ([vec] if with_next else []),
        out_specs=[row] + ([row] if with_next else []),
        out_shape=outs,
        compiler_params=_params("parallel"),
    )(*([x, y, g_post] + ([g_next] if with_next else [])))
    return (res[0], res[1]) if with_next else (res[0], None)


def _col_rms(xs, g):
    r = lax.rsqrt(jnp.mean(xs * xs, axis=0, keepdims=True) + EPS)
    return xs * r * g


def _col_rms_bwd(xs, g, dy):
    r = lax.rsqrt(jnp.mean(xs * xs, axis=0, keepdims=True) + EPS)
    gd = dy * g
    c = jnp.mean(gd * xs, axis=0, keepdims=True)
    dx = r * gd - xs * (r * r * r * c)
    dg = jnp.sum(dy * (xs * r), axis=1, keepdims=True)
    return dx, dg


GROUP_ROWS = (SWA_Q_HEADS * HEAD_DIM, FOX_HEADS * HEAD_DIM, MLA_HEADS * HEAD_DIM)


def _group_specs(S, tn):
    outs = [pl.BlockSpec((n, tn), lambda i: (0, i)) for n in GROUP_ROWS]
    g = pl.BlockSpec((D_MODEL, 1), lambda i: (0, 0))
    mixed = pl.BlockSpec((D_MODEL, tn), lambda i: (0, i))
    return outs, g, mixed


def _group_norm_fwd(oa, of, oc, g, *, name):
    S = oa.shape[1]
    tn = _tile(S, 512)
    outs, gs, mixed = _group_specs(S, tn)

    def body(a_ref, f_ref, c_ref, g_ref, o_ref):
        r0 = 0
        for ref, n in zip((a_ref, f_ref, c_ref), GROUP_ROWS):
            o_ref[r0:r0 + n, :] = _col_rms(ref[...], g_ref[r0:r0 + n, :]).astype(BF16)
            r0 += n

    return pl.pallas_call(
        body, name=name, grid=(S // tn,),
        in_specs=outs + [gs], out_specs=mixed,
        out_shape=jax.ShapeDtypeStruct((D_MODEL, S), BF16),
        compiler_params=_params("parallel"),
    )(oa, of, oc, g)


def _group_norm_bwd(oa, of, oc, g, dmixed, *, name):
    S = oa.shape[1]
    tn = _tile(S, 512)
    outs, gs, mixed = _group_specs(S, tn)
    n_heads = D_MODEL // HEAD_DIM

    def body(a_ref, f_ref, c_ref, g_ref, dm_ref, da_ref, df_ref, dc_ref, dg_ref, dl_ref):
        @pl.when(pl.program_id(0) == 0)
        def _():
            dg_ref[...] = jnp.zeros_like(dg_ref)

        r0 = 0
        for ref, dref, n in zip((a_ref, f_ref, c_ref), (da_ref, df_ref, dc_ref), GROUP_ROWS):
            o = ref[...]
            dx, dg = _col_rms_bwd(o, g_ref[r0:r0 + n, :], dm_ref[r0:r0 + n, :])
            dxb = dx.astype(BF16)
            dref[...] = dxb
            dg_ref[r0:r0 + n, :] += dg
            od = o * dxb.astype(F32)
            for h in range(n // HEAD_DIM):
                dl_ref[r0 // HEAD_DIM + h] = jnp.sum(od[h * HEAD_DIM:(h + 1) * HEAD_DIM, :], axis=0, keepdims=True)
            r0 += n

    return pl.pallas_call(
        body, name=name, grid=(S // tn,),
        in_specs=outs + [gs, mixed], out_specs=outs + [gs, pl.BlockSpec((n_heads, 1, tn), lambda i: (0, 0, i))],
        out_shape=[jax.ShapeDtypeStruct((n, S), BF16) for n in GROUP_ROWS] + [jax.ShapeDtypeStruct((D_MODEL, 1), F32),
                                                                              jax.ShapeDtypeStruct((n_heads, 1, S), F32)],
        compiler_params=_params("arbitrary"),
    )(oa, of, oc, g, dmixed)


def _loss_head(y, target):
    S, W = y.shape
    tm = _tile(S, 512)

    def body(y_ref, t_ref, d_ref, l_ref):
        @pl.when(pl.program_id(0) == 0)
        def _():
            l_ref[...] = jnp.zeros_like(l_ref)

        err = y_ref[...] - t_ref[...]
        d_ref[...] = err * (1.0 / W)
        l_ref[...] += 0.5 * jnp.sum(jnp.mean(err * err, axis=-1, keepdims=True), axis=0, keepdims=True)

    row = pl.BlockSpec((tm, W), lambda i: (i, 0))
    d, l = pl.pallas_call(
        body, name="loss_head", grid=(S // tm,),
        in_specs=[row, row],
        out_specs=[row, pl.BlockSpec((1, 1), lambda i: (0, 0))],
        out_shape=[jax.ShapeDtypeStruct((S, W), F32), jax.ShapeDtypeStruct((1, 1), F32)],
        compiler_params=_params("arbitrary"),
    )(y, target)
    return l[0, 0], d


def _attn_fwd(q_src, k_src, v_src, rows, H, Dk, Dv, scale, f_row=None, f_col=None, *, name):
    S = q_src.shape[1]
    T = _tile(S, ATT_TILE)
    nq = S // T
    forget = f_row is not None
    qb, kb, vb = rows[0] // (H * Dk), rows[1] // (H * Dk), rows[2] // (H * Dv)
    hs = range(H)

    def body(*refs):
        if forget:
            q_ref, k_ref, v_ref, fq_ref, fk_ref, o_ref, lse_ref = refs
        else:
            q_ref, k_ref, v_ref, o_ref, lse_ref = refs
        i = pl.program_id(0)

        def tile(j, masked, state):
            off = pl.multiple_of(j * T, T)
            ss = [_dot(k_ref[h * Dk:(h + 1) * Dk, pl.ds(off, T)], q_ref[h * Dk:(h + 1) * Dk, :], TN) * (scale * LOG2E)
                  for h in hs]
            if forget:
                ss = [ss[h] + (fq_ref[h] - fk_ref[pl.ds(off, T), h:h + 1]) for h in hs]
            if masked:
                r = lax.broadcasted_iota(jnp.int32, (T, T), 0)
                c = lax.broadcasted_iota(jnp.int32, (T, T), 1)
                ss = [jnp.where(r <= c, s, NEG_INF) for s in ss]
            m_new = [jnp.maximum(state[h][0], jnp.max(ss[h], axis=0, keepdims=True)) for h in hs]
            alpha = [jnp.exp2(state[h][0] - m_new[h]) for h in hs]
            ps = [jnp.exp2(ss[h] - m_new[h]) for h in hs]
            l_new = [alpha[h] * state[h][1] + jnp.sum(ps[h], axis=0, keepdims=True) for h in hs]
            p_hi = [p.astype(BF16) for p in ps]
            vs = [v_ref[h * Dv:(h + 1) * Dv, pl.ds(off, T)] for h in hs]
            pv = [_dot(vs[h], p_hi[h]) for h in hs]
            if forget:
                pv = [pv[h] + _dot(vs[h], (ps[h] - p_hi[h].astype(F32)).astype(BF16)) for h in hs]
            return tuple((m_new[h], l_new[h], alpha[h] * state[h][2] + pv[h]) for h in hs)

        init = tuple((jnp.full((1, T), NEG_INF, F32), jnp.zeros((1, T), F32), jnp.zeros((Dv, T), F32)) for _ in hs)
        state = lax.fori_loop(0, i, lambda j, st: tile(j, False, st), init)
        state = tile(i, True, state)
        for h in hs:
            m, l, acc = state[h]
            o_ref[h * Dv:(h + 1) * Dv, :] = acc / l
            lse_ref[h] = m + jnp.log2(l)

    in_specs = [pl.BlockSpec((H * Dk, T), lambda i: (qb, i)),
                pl.BlockSpec((H * Dk, S), lambda i: (kb, 0)),
                pl.BlockSpec((H * Dv, S), lambda i: (vb, 0))]
    ins = [q_src, k_src, v_src]
    if forget:
        in_specs += [pl.BlockSpec((H, 1, T), lambda i: (0, 0, i)), pl.BlockSpec((S, H), lambda i: (0, 0))]
        ins += [f_row, f_col]
    return pl.pallas_call(
        body, name=name, grid=(nq,),
        in_specs=in_specs,
        out_specs=[pl.BlockSpec((H * Dv, T), lambda i: (0, i)), pl.BlockSpec((H, 1, T), lambda i: (0, 0, i))],
        out_shape=[jax.ShapeDtypeStruct((H * Dv, S), F32), jax.ShapeDtypeStruct((H, 1, S), F32)],
        compiler_params=_params("parallel"),
    )(*ins)


def _attn_bwd(q_src, k_src, v_src, rows, H, Dk, Dv, scale, do, lse, delta, f_row=None, f_col=None, *, name):
    S = q_src.shape[1]
    T = _tile(S, ATT_TILE)
    nq = S // T
    forget = f_row is not None
    qb, kb, vb, db = rows[0] // (H * Dk), rows[1] // (H * Dk), rows[2] // (H * Dv), rows[3] // H
    hs = range(H)

    def body(*refs):
        if forget:
            (q_ref, k_ref, v_ref, do_ref, lse_ref, dl_ref, fq_ref, fk_ref,
             dq_ref, dk_ref, dv_ref, df_ref, dk_s, dv_s, df_s) = refs
        else:
            q_ref, k_ref, v_ref, do_ref, lse_ref, dl_ref, dq_ref, dk_ref, dv_ref, dk_s, dv_s = refs
        j = pl.program_id(0)

        @pl.when(j == 0)
        def _():
            dq_ref[...] = jnp.zeros_like(dq_ref)

        dk_s[...] = jnp.zeros_like(dk_s)
        dv_s[...] = jnp.zeros_like(dv_s)
        if forget:
            df_s[...] = jnp.zeros_like(df_s)
        kt = [k_ref[h * Dk:(h + 1) * Dk, :] for h in hs]
        kj = [k.T for k in kt]
        vj = [v_ref[h * Dv:(h + 1) * Dv, :].T for h in hs]
        koff = pl.multiple_of(j * T, T)

        def tile(i, masked):
            cols = pl.ds(pl.multiple_of(i * T, T), T)
            qi = [q_ref[h * Dk:(h + 1) * Dk, cols] for h in hs]
            doi = [do_ref[h * Dv:(h + 1) * Dv, cols] for h in hs]
            st = [_dot(kj[h], qi[h]) * (scale * LOG2E) for h in hs]
            if forget:
                st = [st[h] + (fq_ref[h, :, cols] - fk_ref[pl.ds(koff, T), h:h + 1]) for h in hs]
            if masked:
                r = lax.broadcasted_iota(jnp.int32, (T, T), 0)
                c = lax.broadcasted_iota(jnp.int32, (T, T), 1)
                st = [jnp.where(r <= c, x, NEG_INF) for x in st]
            pt = [jnp.exp2(st[h] - lse_ref[h, :, cols]) for h in hs]
            dpt = [_dot(vj[h], doi[h]) for h in hs]
            dst = [pt[h] * (dpt[h] - dl_ref[h, :, cols]) for h in hs]
            ptb = [p.astype(BF16) for p in pt]
            dsb = [d.astype(BF16) for d in dst]
            for h in hs:
                dv_s[h * Dv:(h + 1) * Dv, :] += _dot(doi[h], ptb[h], NT)
            for h in hs:
                dk_s[h * Dk:(h + 1) * Dk, :] += _dot(qi[h], dsb[h], NT)
            for h in hs:
                dq_ref[h * Dk:(h + 1) * Dk, cols] += _dot(kt[h], dsb[h]) * scale
            if forget:
                for h in hs:
                    part = dst[h][:, 0:LANES]
                    for c0 in range(LANES, T, LANES):
                        part = part + dst[h][:, c0:c0 + LANES]
                    df_s[h] += part

        tile(j, True)

        def loop_body(i, carry):
            tile(i, False)
            return carry

        lax.fori_loop(j + 1, nq, loop_body, 0)
        dk_ref[...] = dk_s[...] * scale
        dv_ref[...] = dv_s[...]
        if forget:
            df_ref[...] = jnp.concatenate([-jnp.sum(df_s[h], axis=-1, keepdims=True) for h in hs], axis=1)

    res = lambda D, b0: pl.BlockSpec((H * D, S), lambda j: (b0, 0))
    blk = lambda D, b0: pl.BlockSpec((H * D, T), lambda j: (b0, j))
    row3 = lambda b0: pl.BlockSpec((H, 1, S), lambda j: (b0, 0, 0))
    in_specs = [res(Dk, qb), blk(Dk, kb), blk(Dv, vb), res(Dv, 0), row3(0), row3(db)]
    ins = [q_src, k_src, v_src, do, lse, delta]
    out_specs = [res(Dk, 0), blk(Dk, 0), blk(Dv, 0)]
    out_shape = [jax.ShapeDtypeStruct((H * Dk, S), F32), jax.ShapeDtypeStruct((H * Dk, S), F32),
                 jax.ShapeDtypeStruct((H * Dv, S), F32)]
    scratch = [pltpu.VMEM((H * Dk, T), F32), pltpu.VMEM((H * Dv, T), F32)]
    if forget:
        in_specs += [row3(0), pl.BlockSpec((S, H), lambda j: (0, 0))]
        ins += [f_row, f_col]
        out_specs.append(pl.BlockSpec((T, H), lambda j: (j, 0)))
        out_shape.append(jax.ShapeDtypeStruct((S, H), F32))
        scratch.append(pltpu.VMEM((H, T, min(T, LANES)), F32))
    return pl.pallas_call(
        body, name=name, grid=(nq,),
        in_specs=in_specs, out_specs=out_specs, out_shape=out_shape, scratch_shapes=scratch,
        compiler_params=_params("arbitrary"),
    )(*ins)


GW = SWA_GROUP * WINDOW


def _swa_masks(i):
    r = lax.broadcasted_iota(jnp.int32, (WINDOW, GW), 0)
    c = lax.broadcasted_iota(jnp.int32, (WINDOW, GW), 1) % WINDOW
    return (r > c) & (i > 0), r <= c


def _swa_specs():
    W = WINDOW
    kv_rows = SWA_KV_HEADS * HEAD_DIM
    q = pl.BlockSpec((SWA_Q_HEADS * HEAD_DIM, W), lambda i: (0, i))
    prev = lambda b: pl.BlockSpec((kv_rows, W), lambda i: (b, jnp.maximum(i - 1, 0)))
    cur = lambda b: pl.BlockSpec((kv_rows, W), lambda i: (b, i))
    bias = pl.BlockSpec((SWA_KV_HEADS, 2 * W, GW), lambda i: (0, 0, 0))
    stat = pl.BlockSpec((SWA_Q_HEADS, W), lambda i: (0, i))
    sink = pl.BlockSpec(memory_space=pltpu.SMEM)
    return q, prev(4), cur(4), prev(5), cur(5), bias, stat, sink


def _group_lanes(ref, g, rows_per_head):
    h0 = g * SWA_GROUP
    return jnp.concatenate([ref[(h0 + j) * rows_per_head:(h0 + j + 1) * rows_per_head, :] for j in range(SWA_GROUP)], axis=1)


def _swa_scores(g, q_ref, kp_ref, kc_ref, b_ref, masks):
    rows = slice(g * HEAD_DIM, (g + 1) * HEAD_DIM)
    qg = _group_lanes(q_ref, g, HEAD_DIM)
    scale = HEAD_DIM ** -0.5
    s_p = jnp.where(masks[0], _dot(kp_ref[rows, :], qg, TN) * scale + b_ref[g, 0:WINDOW, :], NEG_INF)
    s_c = jnp.where(masks[1], _dot(kc_ref[rows, :], qg, TN) * scale + b_ref[g, WINDOW:2 * WINDOW, :], NEG_INF)
    return qg, rows, s_p, s_c


def _sink_row(sink_ref, g):
    return jnp.concatenate([jnp.full((1, WINDOW), sink_ref[g * SWA_GROUP + j], F32) for j in range(SWA_GROUP)], axis=1)


def _swa_fwd(qkv, bias_g, sinks, *, name):
    S = qkv.shape[1]
    qs, kp, kc, vp, vc, bs, stat, sk = _swa_specs()
    gs = range(SWA_KV_HEADS)

    def body(sink_ref, q_ref, kp_ref, kc_ref, vp_ref, vc_ref, b_ref, o_ref, lse_ref):
        masks = _swa_masks(pl.program_id(0))
        sc = [_swa_scores(g, q_ref, kp_ref, kc_ref, b_ref, masks) for g in gs]
        sinks_g = [_sink_row(sink_ref, g) for g in gs]
        m = [jnp.maximum(jnp.maximum(jnp.max(sc[g][2], axis=0, keepdims=True), jnp.max(sc[g][3], axis=0, keepdims=True)),
                         sinks_g[g]) for g in gs]
        p_p = [jnp.exp(sc[g][2] - m[g]) for g in gs]
        p_c = [jnp.exp(sc[g][3] - m[g]) for g in gs]
        l = [jnp.sum(p_p[g], axis=0, keepdims=True) + jnp.sum(p_c[g], axis=0, keepdims=True) + jnp.exp(sinks_g[g] - m[g])
             for g in gs]
        o = [_dot(vp_ref[sc[g][1], :], p_p[g].astype(BF16)) + _dot(vc_ref[sc[g][1], :], p_c[g].astype(BF16)) for g in gs]
        for g in gs:
            og = o[g] / l[g]
            lse = m[g] + jnp.log(l[g])
            for j in range(SWA_GROUP):
                h = g * SWA_GROUP + j
                o_ref[h * HEAD_DIM:(h + 1) * HEAD_DIM, :] = og[:, j * WINDOW:(j + 1) * WINDOW]
                lse_ref[h:h + 1, :] = lse[:, j * WINDOW:(j + 1) * WINDOW]

    return pl.pallas_call(
        body, name=name, grid=(S // WINDOW,),
        in_specs=[sk, qs, kp, kc, vp, vc, bs],
        out_specs=[qs, stat],
        out_shape=[jax.ShapeDtypeStruct((SWA_Q_HEADS * HEAD_DIM, S), F32), jax.ShapeDtypeStruct((SWA_Q_HEADS, S), F32)],
        compiler_params=_params("parallel"),
    )(sinks, qkv, qkv, qkv, qkv, qkv, bias_g)


def _swa_bwd(qkv, bias_g, sinks, do, lse, delta, *, name):
    S = qkv.shape[1]
    W = WINDOW
    qs, kp, kc, vp, vc, bs, stat, sk = _swa_specs()
    scale = HEAD_DIM ** -0.5
    kv_rows = SWA_KV_HEADS * HEAD_DIM
    gs = range(SWA_KV_HEADS)

    def body(sink_ref, q_ref, kp_ref, kc_ref, vp_ref, vc_ref, b_ref, do_ref, lse_ref, dl_ref,
             dq_ref, dkv_ref, db_ref, dsk_ref):
        i = pl.program_id(0)

        @pl.when(i == 0)
        def _():
            dkv_ref[...] = jnp.zeros_like(dkv_ref)
            db_ref[...] = jnp.zeros_like(db_ref)
            dsk_ref[...] = jnp.zeros_like(dsk_ref)

        masks = _swa_masks(i)
        prev = pl.ds(pl.multiple_of(jnp.maximum(i - 1, 0) * W, W), W)
        cur = pl.ds(pl.multiple_of(i * W, W), W)
        sc = [_swa_scores(g, q_ref, kp_ref, kc_ref, b_ref, masks) for g in gs]
        dog = [_group_lanes(do_ref, g, HEAD_DIM) for g in gs]
        lse = [_group_lanes(lse_ref, g, 1) for g in gs]
        dl = [_group_lanes(dl_ref, g, 1) for g in gs]
        p_p = [jnp.exp(sc[g][2] - lse[g]) for g in gs]
        p_c = [jnp.exp(sc[g][3] - lse[g]) for g in gs]
        ds_p = [p_p[g] * (_dot(vp_ref[sc[g][1], :], dog[g], TN) - dl[g]) for g in gs]
        ds_c = [p_c[g] * (_dot(vc_ref[sc[g][1], :], dog[g], TN) - dl[g]) for g in gs]
        for g in gs:
            db_ref[g, 0:W, :] += ds_p[g]
            db_ref[g, W:2 * W, :] += ds_c[g]
            dsk = jnp.exp(_sink_row(sink_ref, g) - lse[g]) * dl[g]
            for j in range(SWA_GROUP):
                h = g * SWA_GROUP + j
                dsk_ref[h:h + 1, :] -= jnp.broadcast_to(jnp.sum(dsk[:, j * W:(j + 1) * W], axis=1, keepdims=True), (1, LANES))
        dsb_p = [d.astype(BF16) for d in ds_p]
        dsb_c = [d.astype(BF16) for d in ds_c]
        for g in gs:
            rows = sc[g][1]
            dq = (_dot(kp_ref[rows, :], dsb_p[g]) + _dot(kc_ref[rows, :], dsb_c[g])) * scale
            for j in range(SWA_GROUP):
                h = g * SWA_GROUP + j
                dq_ref[h * HEAD_DIM:(h + 1) * HEAD_DIM, :] = dq[:, j * W:(j + 1) * W]
        for g in gs:
            rows = sc[g][1]
            vrows = slice(kv_rows + rows.start, kv_rows + rows.stop)
            dkv_ref[rows, prev] += _dot(sc[g][0], dsb_p[g], NT) * scale
            dkv_ref[rows, cur] += _dot(sc[g][0], dsb_c[g], NT) * scale
            dkv_ref[vrows, prev] += _dot(dog[g], p_p[g].astype(BF16), NT)
            dkv_ref[vrows, cur] += _dot(dog[g], p_c[g].astype(BF16), NT)

    return pl.pallas_call(
        body, name=name, grid=(S // W,),
        in_specs=[sk, qs, kp, kc, vp, vc, bs, qs, stat, stat],
        out_specs=[qs, pl.BlockSpec((2 * kv_rows, S), lambda i: (0, 0)), bs, pl.BlockSpec((SWA_Q_HEADS, LANES), lambda i: (0, 0))],
        out_shape=[jax.ShapeDtypeStruct((SWA_Q_HEADS * HEAD_DIM, S), F32), jax.ShapeDtypeStruct((2 * kv_rows, S), F32),
                   jax.ShapeDtypeStruct((SWA_KV_HEADS, 2 * W, GW), F32), jax.ShapeDtypeStruct((SWA_Q_HEADS, LANES), F32)],
        compiler_params=_params("arbitrary"),
    )(sinks, qkv, qkv, qkv, qkv, qkv, bias_g, do, lse, delta)


def _rel_onehot_t():
    qi = jnp.arange(WINDOW, dtype=jnp.int32)[None, :] + WINDOW
    kj = jnp.arange(2 * WINDOW, dtype=jnp.int32)[:, None]
    dist = qi - kj
    max_exact = REL_BUCKETS // 2
    d = jnp.maximum(dist, 0)
    log_ratio = jnp.log(jnp.maximum(d, 1).astype(F32) / max_exact) / math.log(REL_MAX_DIST / max_exact)
    large = jnp.minimum(max_exact + (log_ratio * (REL_BUCKETS - max_exact)).astype(jnp.int32), REL_BUCKETS - 1)
    bucket = jnp.where(d < max_exact, d, large).reshape(-1)
    return (bucket[None, :] == jnp.arange(REL_BUCKETS, dtype=jnp.int32)[:, None]).astype(BF16)


def _bias_table(rel_bias_t, onehot_t):
    Hq, NB = rel_bias_t.shape
    N = onehot_t.shape[1]
    tn = _tile(N, 4096)

    def body(r_ref, oh_ref, o_ref):
        oh = oh_ref[...]
        a1, a2, a3 = _split3(r_ref[...])
        o_ref[...] = _dot(a1, oh) + _dot(a2, oh) + _dot(a3, oh)

    return pl.pallas_call(
        body, name="rel_bias_table", grid=(N // tn,),
        in_specs=[pl.BlockSpec((Hq, NB), lambda j: (0, 0)), pl.BlockSpec((NB, tn), lambda j: (0, j))],
        out_specs=pl.BlockSpec((Hq, tn), lambda j: (0, j)),
        out_shape=jax.ShapeDtypeStruct((Hq, N), F32),
        compiler_params=_params("parallel"),
    )(rel_bias_t, onehot_t)


def _bias_table_bwd(dbias, onehot_t):
    L, Hq, N = dbias.shape
    NB = onehot_t.shape[0]
    tn = _tile(N, 4096)

    def body(d_ref, oh_ref, o_ref):
        @pl.when(pl.program_id(0) == 0)
        def _():
            o_ref[...] = jnp.zeros_like(o_ref)

        d = d_ref[0]
        for l in range(1, L):
            d = d + d_ref[l]
        oh = oh_ref[...]
        a1, a2, a3 = _split3(d)
        o_ref[...] += _dot(a1, oh, NT) + _dot(a2, oh, NT) + _dot(a3, oh, NT)

    return pl.pallas_call(
        body, name="rel_bias_bwd", grid=(N // tn,),
        in_specs=[pl.BlockSpec((L, Hq, tn), lambda j: (0, 0, j)), pl.BlockSpec((NB, tn), lambda j: (0, j))],
        out_specs=pl.BlockSpec((Hq, NB), lambda j: (0, 0)),
        out_shape=jax.ShapeDtypeStruct((Hq, NB), F32),
        compiler_params=_params("arbitrary"),
    )(dbias, onehot_t)


def _gate_fwd(lat, fb_col, *, name):
    S = lat.shape[1]
    tn = _tile(S, 256)

    def body(z_ref, fb_ref, o_ref, carry):
        @pl.when(pl.program_id(0) == 0)
        def _():
            carry[...] = jnp.zeros_like(carry)

        z = z_ref[...] + fb_ref[...]
        lf = jnp.minimum(z, 0.0) - jnp.log1p(jnp.exp(-jnp.abs(z)))
        r = lax.broadcasted_iota(jnp.int32, (tn, tn), 0)
        c = lax.broadcasted_iota(jnp.int32, (tn, tn), 1)
        tri = (r <= c).astype(BF16)
        a1, a2, a3 = _split3(lf)
        cum = _dot(a1, tri) + _dot(a2, tri) + _dot(a3, tri) + carry[:, 0:1]
        o_ref[...] = cum
        carry[...] = jnp.broadcast_to(cum[:, tn - 1:tn], carry.shape)

    return pl.pallas_call(
        body, name=name, grid=(S // tn,),
        in_specs=[pl.BlockSpec((GATE_ROWS, tn), lambda i: (0, i)), pl.BlockSpec((GATE_ROWS, 1), lambda i: (0, 0))],
        out_specs=pl.BlockSpec((GATE_ROWS, tn), lambda i: (0, i)),
        out_shape=jax.ShapeDtypeStruct((GATE_ROWS, S), F32),
        scratch_shapes=[pltpu.VMEM((GATE_ROWS, LANES), F32)],
        compiler_params=_params("arbitrary"),
    )(lat, fb_col)


def _gate_bwd(lat, fb_col, dF, *, name):
    S = lat.shape[1]
    tn = _tile(S, 256)
    nt = S // tn

    def body(z_ref, fb_ref, df_ref, dz_ref, dfb_ref, carry):
        @pl.when(pl.program_id(0) == 0)
        def _():
            carry[...] = jnp.zeros_like(carry)
            dfb_ref[...] = jnp.zeros_like(dfb_ref)

        r = lax.broadcasted_iota(jnp.int32, (tn, tn), 0)
        c = lax.broadcasted_iota(jnp.int32, (tn, tn), 1)
        tri = (r >= c).astype(BF16)
        a1, a2, a3 = _split3(df_ref[...])
        dlf = _dot(a1, tri) + _dot(a2, tri) + _dot(a3, tri) + carry[:, 0:1]
        carry[...] = jnp.broadcast_to(dlf[:, 0:1], carry.shape)
        z = z_ref[...] + fb_ref[...]
        row = lax.broadcasted_iota(jnp.int32, (GATE_ROWS, tn), 0)
        dz = jnp.where(row < FOX_HEADS, dlf / (1.0 + jnp.exp(z)), 0.0)
        dz_ref[...] = dz
        dfb_ref[...] += jnp.sum(dz, axis=1, keepdims=True)

    blk = pl.BlockSpec((GATE_ROWS, tn), lambda i: (0, nt - 1 - i))
    vec = pl.BlockSpec((GATE_ROWS, 1), lambda i: (0, 0))
    return pl.pallas_call(
        body, name=name, grid=(nt,),
        in_specs=[blk, vec, blk], out_specs=[blk, vec],
        out_shape=[jax.ShapeDtypeStruct((GATE_ROWS, S), F32), jax.ShapeDtypeStruct((GATE_ROWS, 1), F32)],
        scratch_shapes=[pltpu.VMEM((GATE_ROWS, LANES), F32)],
        compiler_params=_params("arbitrary"),
    )(lat, fb_col, dF)


def _rope_tables(S):
    pos = jnp.arange(S, dtype=F32)
    inv_freq = ROPE_THETA ** (-(jnp.arange(MLA_ROPE // 2, dtype=F32) * 2.0 / MLA_ROPE))
    ang = pos[:, None] * inv_freq[None, :]
    cos, sin = jnp.cos(ang).T, jnp.sin(ang).T
    z16 = jnp.zeros_like(cos)

    def slab(lo, fill):
        def put(first, second, f):
            return jnp.concatenate([jnp.full((lo, S), f, F32), first, second, jnp.full((LANES - lo - MLA_ROPE, S), f, F32)], axis=0)
        return put(cos, cos, fill), put(-sin, z16, 0.0), put(z16, sin, 0.0)

    tq = tuple(jnp.tile(t, (MLA_HEADS, 1)) for t in slab(MLA_NOPE, 1.0))
    return tq, slab(0, 0.0)


def _rope(x, c, s1, s2):
    n = x.shape[0]
    half = MLA_ROPE // 2
    return x * c + pltpu.roll(x, n - half, 0) * s1 + pltpu.roll(x, half, 0) * s2


def _rope_t(dy, c, s1, s2):
    n = dy.shape[0]
    half = MLA_ROPE // 2
    return dy * c + pltpu.roll(dy * s1, half, 0) + pltpu.roll(dy * s2, n - half, 0)


KR_SLAB0 = MLA_Q_RANK + MLA_KV_RANK


def _mla_prep_fwd(lat, g_q, g_kv, w_uq_t, w_ukv_t, tq, tmisc, *, name):
    S = lat.shape[1]
    tn = _tile(S, 512)
    QW = MLA_HEADS * MLA_PAD

    def body(lat_ref, gq_ref, gkv_ref, wq_ref, wkv_ref, c_ref, s1_ref, s2_ref, cm_ref, s1m_ref, s2m_ref,
             nq_ref, nkv_ref, q_ref, k_ref, v_ref):
        x = pltpu.roll(lat_ref[...], LAT_ROWS - LAT_SHIFT, 0)
        nq = _col_rms(x[0:MLA_Q_RANK, :], gq_ref[...]).astype(BF16)
        nkv = _col_rms(x[MLA_Q_RANK:KR_SLAB0, :], gkv_ref[...]).astype(BF16)
        nq_ref[...] = nq
        nkv_ref[...] = nkv
        q_ref[...] = _rope(_dot(wq_ref[...], nq), c_ref[...], s1_ref[...], s2_ref[...]).astype(BF16)
        kv = _dot(wkv_ref[...], nkv).astype(BF16)
        kr = _rope(x[KR_SLAB0:LAT_ROWS, :], cm_ref[...], s1m_ref[...], s2m_ref[...]).astype(BF16)
        for h in range(MLA_HEADS):
            k_ref[h * MLA_PAD:h * MLA_PAD + MLA_NOPE, :] = kv[h * LANES:h * LANES + MLA_NOPE, :]
            k_ref[h * MLA_PAD + MLA_NOPE:(h + 1) * MLA_PAD, :] = kr[0:MLA_PAD - MLA_NOPE, :]
            v_ref[h * HEAD_DIM:(h + 1) * HEAD_DIM, :] = kv[h * LANES + MLA_NOPE:(h + 1) * LANES, :]

    def col(rows):
        return pl.BlockSpec((rows, tn), lambda i: (0, i))

    def full(a):
        return pl.BlockSpec(a.shape, lambda i: (0, 0))

    return pl.pallas_call(
        body, name=name, grid=(S // tn,),
        in_specs=[col(LAT_ROWS), full(g_q), full(g_kv), full(w_uq_t), full(w_ukv_t),
                  col(QW), col(QW), col(QW), col(LANES), col(LANES), col(LANES)],
        out_specs=[col(MLA_Q_RANK), col(MLA_KV_RANK), col(QW), col(QW), col(MLA_HEADS * HEAD_DIM)],
        out_shape=[jax.ShapeDtypeStruct((MLA_Q_RANK, S), BF16), jax.ShapeDtypeStruct((MLA_KV_RANK, S), BF16),
                   jax.ShapeDtypeStruct((QW, S), BF16), jax.ShapeDtypeStruct((QW, S), BF16),
                   jax.ShapeDtypeStruct((MLA_HEADS * HEAD_DIM, S), BF16)],
        compiler_params=_params("parallel"),
    )(lat, g_q, g_kv, w_uq_t, w_ukv_t, *tq, *tmisc)


def _mla_prep_bwd(lat, nq, nkv, g_q, g_kv, w_uq_p, w_ukv, tq, tmisc, dq, dk, dv, dflog, *, name):
    S = lat.shape[1]
    tn = _tile(S, 512)
    QW = MLA_HEADS * MLA_PAD

    def body(lat_ref, nq_ref, nkv_ref, gq_ref, gkv_ref, wq_ref, wkv_ref, c_ref, s1_ref, s2_ref,
             cm_ref, s1m_ref, s2m_ref, dq_ref, dk_ref, dv_ref, dfl_ref,
             dlat_ref, dwq_ref, dwkv_ref, dgq_ref, dgkv_ref, y_s):
        @pl.when(pl.program_id(0) == 0)
        def _():
            dwq_ref[...] = jnp.zeros_like(dwq_ref)
            dwkv_ref[...] = jnp.zeros_like(dwkv_ref)
            dgq_ref[...] = jnp.zeros_like(dgq_ref)
            dgkv_ref[...] = jnp.zeros_like(dgkv_ref)

        x = pltpu.roll(lat_ref[...], LAT_ROWS - LAT_SHIFT, 0)
        dqm = _rope_t(dq_ref[...], c_ref[...], s1_ref[...], s2_ref[...]).astype(BF16)
        dwq_ref[...] += _dot(dqm, nq_ref[...], NT)
        dx, dg = _col_rms_bwd(x[0:MLA_Q_RANK, :], gq_ref[...], _dot(wq_ref[...], dqm))
        y_s[0:MLA_Q_RANK, :] = dx
        dgq_ref[...] += dg
        dkv = jnp.concatenate(
            [part for h in range(MLA_HEADS)
             for part in (dk_ref[h * MLA_PAD:h * MLA_PAD + MLA_NOPE, :], dv_ref[h * HEAD_DIM:(h + 1) * HEAD_DIM, :])],
            axis=0).astype(BF16)
        dwkv_ref[...] += _dot(dkv, nkv_ref[...], NT)
        dx, dg = _col_rms_bwd(x[MLA_Q_RANK:KR_SLAB0, :], gkv_ref[...], _dot(wkv_ref[...], dkv))
        y_s[MLA_Q_RANK:KR_SLAB0, :] = dx
        dgkv_ref[...] += dg
        dkr = dk_ref[MLA_NOPE:MLA_PAD, :]
        for h in range(1, MLA_HEADS):
            dkr = dkr + dk_ref[h * MLA_PAD + MLA_NOPE:(h + 1) * MLA_PAD, :]
        dkr = jnp.concatenate([dkr, jnp.zeros((MLA_NOPE, tn), F32)], axis=0)
        y_s[KR_SLAB0:LAT_ROWS, :] = _rope_t(dkr, cm_ref[...], s1m_ref[...], s2m_ref[...])
        y = pltpu.roll(y_s[...], LAT_SHIFT, 0)
        row = lax.broadcasted_iota(jnp.int32, (LAT_ROWS, tn), 0)
        dfl = jnp.concatenate([dfl_ref[...], jnp.zeros((LAT_ROWS - GATE_ROWS, tn), F32)], axis=0)
        dlat_ref[...] = jnp.where(row < LAT_SHIFT, dfl, y).astype(BF16)

    def col(rows):
        return pl.BlockSpec((rows, tn), lambda i: (0, i))

    def full(a):
        return pl.BlockSpec(a.shape, lambda i: (0, 0))

    def acc(r, c):
        return pl.BlockSpec((r, c), lambda i: (0, 0))

    return pl.pallas_call(
        body, name=name, grid=(S // tn,),
        in_specs=[col(LAT_ROWS), col(MLA_Q_RANK), col(MLA_KV_RANK), full(g_q), full(g_kv),
                  full(w_uq_p), full(w_ukv), col(QW), col(QW), col(QW), col(LANES), col(LANES), col(LANES),
                  col(QW), col(QW), col(MLA_HEADS * HEAD_DIM), col(GATE_ROWS)],
        out_specs=[col(LAT_ROWS), acc(QW, MLA_Q_RANK), acc(QW, MLA_KV_RANK), acc(MLA_Q_RANK, 1), acc(MLA_KV_RANK, 1)],
        out_shape=[jax.ShapeDtypeStruct((LAT_ROWS, S), BF16), jax.ShapeDtypeStruct((QW, MLA_Q_RANK), F32),
                   jax.ShapeDtypeStruct((QW, MLA_KV_RANK), F32), jax.ShapeDtypeStruct((MLA_Q_RANK, 1), F32),
                   jax.ShapeDtypeStruct((MLA_KV_RANK, 1), F32)],
        scratch_shapes=[pltpu.VMEM((LAT_ROWS, tn), F32)],
        compiler_params=_params("arbitrary"),
    )(lat, nq, nkv, g_q, g_kv, w_uq_p, w_ukv, *tq, *tmisc, dq, dk, dv, dflog)


def _dproj_cast(dqa, dkva, dqf, dkf, dvf, dlat, *, name):
    S = dqa.shape[1]
    tn = _tile(S, 512)
    parts = (dqa, dkva, dqf, dkf, dvf, dlat)

    def body(*refs):
        o_ref = refs[-1]
        r0 = 0
        for ref in refs[:-1]:
            n = ref.shape[0]
            o_ref[r0:r0 + n, :] = ref[...].astype(BF16)
            r0 += n

    return pl.pallas_call(
        body, name=name, grid=(S // tn,),
        in_specs=[pl.BlockSpec((p.shape[0], tn), lambda i: (0, i)) for p in parts],
        out_specs=pl.BlockSpec((IN_ROWS, tn), lambda i: (0, i)),
        out_shape=jax.ShapeDtypeStruct((IN_ROWS, S), BF16),
        compiler_params=_params("parallel"),
    )(*parts)


GELU_C = math.sqrt(2.0 / math.pi)
GELU_A = 0.044715


HALO = 16


def _shift_down(a, k, fill):
    r = pltpu.roll(a, k, 0)
    row = lax.broadcasted_iota(jnp.int32, (8, a.shape[1]), 0)
    head = r[0:8, :]
    for i in range(k):
        head = jnp.where(row == i, fill[len(fill) - k + i], head)
    return jnp.concatenate([head, r[8:, :]], axis=0)


def _shift_up(d, k, fill):
    n = d.shape[0]
    r = pltpu.roll(d, n - k, 0)
    row = lax.broadcasted_iota(jnp.int32, (8, d.shape[1]), 0)
    tail = r[n - 8:n, :]
    for i in range(k):
        tail = jnp.where(row == 8 - k + i, fill[i], tail)
    return jnp.concatenate([r[0:n - 8, :], tail], axis=0)


def _conv_taps(a, before, w_ref, b_ref):
    a1 = _shift_down(a, 1, before)
    a2 = _shift_down(a, 2, before)
    u = ((b_ref[...] + w_ref[0:1, :] * a2) + w_ref[1:2, :] * a1) + w_ref[2:3, :] * a
    return u, a1, a2


def _rows_before(halo_ref, first):
    h = halo_ref[HALO - 2:HALO, :].astype(F32)
    return jnp.where(first, 0.0, h[0:1, :]), jnp.where(first, 0.0, h[1:2, :])


def _conv_specs(S, tm, tc, nc):
    hb = tm // HALO
    main = lambda off: pl.BlockSpec((tm, tc), lambda j, i: (i, j + off))
    prev = lambda off: pl.BlockSpec((HALO, tc), lambda j, i: (jnp.maximum(i * hb - 1, 0), j + off))
    nxt = lambda off: pl.BlockSpec((HALO, tc), lambda j, i: (jnp.minimum((i + 1) * hb, S // HALO - 1), j + off))
    wspec = lambda off: pl.BlockSpec((3, tc), lambda j, i: (0, j + off))
    bspec = lambda off: pl.BlockSpec((1, tc), lambda j, i: (0, j + off))
    return main, prev, nxt, wspec, bspec


def _conv_geglu_fwd(a, conv_w, conv_b, *, name):
    S = a.shape[0]
    tm, tc = _tile(S, 512), _tile(D_FF, 1408)
    nc = D_FF // tc
    main, prev, _, wspec, bspec = _conv_specs(S, tm, tc, nc)

    def body(ag_ref, au_ref, hg_ref, hu_ref, wg_ref, wu_ref, bg_ref, bu_ref, z_ref):
        first = pl.program_id(1) == 0
        gate, _, _ = _conv_taps(ag_ref[...].astype(F32), _rows_before(hg_ref, first), wg_ref, bg_ref)
        up, _, _ = _conv_taps(au_ref[...].astype(F32), _rows_before(hu_ref, first), wu_ref, bu_ref)
        cdf = 0.5 * (1.0 + jnp.tanh(GELU_C * (gate + GELU_A * (gate * gate * gate))))
        z_ref[...] = (gate * cdf * up).astype(BF16)

    return pl.pallas_call(
        body, name=name, grid=(nc, S // tm),
        in_specs=[main(0), main(nc), prev(0), prev(nc), wspec(0), wspec(nc), bspec(0), bspec(nc)],
        out_specs=pl.BlockSpec((tm, tc), lambda j, i: (i, j)),
        out_shape=jax.ShapeDtypeStruct((S, D_FF), BF16),
        compiler_params=_params("parallel", "arbitrary"),
    )(a, a, a, a, conv_w, conv_w, conv_b, conv_b)


def _geglu_bwd(gate, up, dz):
    g2x = gate * gate
    th = jnp.tanh(GELU_C * (gate + GELU_A * (g2x * gate)))
    cdf = 0.5 * (1.0 + th)
    dgelu = cdf + gate * (0.5 * (1.0 - th * th) * (GELU_C * (1.0 + 3.0 * GELU_A * g2x)))
    return dz * up * dgelu, dz * (gate * cdf)


def _conv_geglu_bwd(a, conv_w, conv_b, dz, *, name):
    S = a.shape[0]
    tm, tc = _tile(S, 512), _tile(D_FF, 1408)
    nc = D_FF // tc
    nr = S // tm
    main, prev, nxt, wspec, bspec = _conv_specs(S, tm, tc, nc)

    def body(ag_ref, au_ref, pg_ref, pu_ref, ng_ref, nu_ref, wg_ref, wu_ref, bg_ref, bu_ref, dz_ref, dzn_ref,
             da_ref, dw_ref, db_ref):
        i = pl.program_id(1)
        first, last = i == 0, i == nr - 1

        @pl.when(first)
        def _():
            dw_ref[...] = jnp.zeros_like(dw_ref)
            db_ref[...] = jnp.zeros_like(db_ref)

        ag, au = ag_ref[...].astype(F32), au_ref[...].astype(F32)
        gate, g1, g2 = _conv_taps(ag, _rows_before(pg_ref, first), wg_ref, bg_ref)
        up, u1, u2 = _conv_taps(au, _rows_before(pu_ref, first), wu_ref, bu_ref)
        dug, duu = _geglu_bwd(gate, up, dz_ref[...])
        gate_n, _, _ = _conv_taps(ng_ref[...].astype(F32), (ag[tm - 2:tm - 1, :], ag[tm - 1:tm, :]), wg_ref, bg_ref)
        up_n, _, _ = _conv_taps(nu_ref[...].astype(F32), (au[tm - 2:tm - 1, :], au[tm - 1:tm, :]), wu_ref, bu_ref)
        dug_n, duu_n = _geglu_bwd(gate_n, up_n, dzn_ref[...])
        for half, du, du_n, w_ref, taps in ((0, dug, dug_n, wg_ref, (g2, g1, ag)), (1, duu, duu_n, wu_ref, (u2, u1, au))):
            after = (jnp.where(last, 0.0, du_n[0:1, :]), jnp.where(last, 0.0, du_n[1:2, :]))
            d1, d2 = _shift_up(du, 1, after), _shift_up(du, 2, after)
            da_ref[half] = (w_ref[2:3, :] * du + w_ref[1:2, :] * d1 + w_ref[0:1, :] * d2).astype(BF16)
            for tap in range(3):
                dw_ref[half, tap:tap + 1, :] += jnp.sum(du * taps[tap], axis=0, keepdims=True)
            db_ref[half] += jnp.sum(du, axis=0, keepdims=True)

    hb = tm // HALO
    return pl.pallas_call(
        body, name=name, grid=(nc, nr),
        in_specs=[main(0), main(nc), prev(0), prev(nc), nxt(0), nxt(nc), wspec(0), wspec(nc), bspec(0), bspec(nc),
                  pl.BlockSpec((tm, tc), lambda j, i: (i, j)),
                  pl.BlockSpec((HALO, tc), lambda j, i: (jnp.minimum((i + 1) * hb, S // HALO - 1), j))],
        out_specs=[pl.BlockSpec((2, tm, tc), lambda j, i: (0, i, j)), pl.BlockSpec((2, 3, tc), lambda j, i: (0, 0, j)),
                   pl.BlockSpec((2, 1, tc), lambda j, i: (0, 0, j))],
        out_shape=[jax.ShapeDtypeStruct((2, S, D_FF), BF16), jax.ShapeDtypeStruct((2, 3, D_FF), F32),
                   jax.ShapeDtypeStruct((2, 1, D_FF), F32)],
        compiler_params=_params("parallel", "arbitrary"),
    )(a, a, a, a, a, a, conv_w, conv_w, conv_b, conv_b, dz, dz)


def _adamw_update(w, g, m, v):
    m = ADAM_B1 * m + (1.0 - ADAM_B1) * g
    v = ADAM_B2 * v + (1.0 - ADAM_B2) * jnp.square(g)
    m_hat = m / (1.0 - ADAM_B1 ** ADAM_STEP)
    v_hat = v / (1.0 - ADAM_B2 ** ADAM_STEP)
    return -ADAM_LR * (m_hat / (jnp.sqrt(v_hat) + ADAM_EPS) + ADAM_WD * w), m, v


def _adamw(w, g, m, v, *, name):
    L, A, B = w.shape
    ta = _tile(A, ROW_TILE)

    def body(w_ref, g_ref, m_ref, v_ref, d_ref, mo_ref, vo_ref):
        d_ref[...], mo_ref[...], vo_ref[...] = _adamw_update(w_ref[...], g_ref[...], m_ref[...], v_ref[...])

    blk = pl.BlockSpec((None, ta, B), lambda l, i: (l, i, 0))
    shp = jax.ShapeDtypeStruct((L, A, B), F32)
    return pl.pallas_call(
        body, name=name, grid=(L, A // ta),
        in_specs=[blk] * 4, out_specs=[blk] * 3, out_shape=[shp] * 3,
        compiler_params=_params("parallel", "parallel"),
    )(w, g, m, v)


def _scalar(v):
    return jnp.reshape(v, (1,)).astype(jnp.int32)


def _adamw_halves(w, g_mine, g_other, m, v, *, name):
    L, A, B = w.shape
    ta = _tile(A // 2, ROW_TILE)
    nb = A // 2 // ta

    def body(c_ref, w_ref, gm_ref, go_ref, m_ref, v_ref, g_ref, d_ref, mo_ref, vo_ref):
        g = jnp.where(pl.program_id(1) // nb == c_ref[0], gm_ref[...], go_ref[...])
        g_ref[...] = g
        d_ref[...], mo_ref[...], vo_ref[...] = _adamw_update(w_ref[...], g, m_ref[...], v_ref[...])

    blk = pl.BlockSpec((None, ta, B), lambda l, i, c_ref: (l, i, 0))
    half = pl.BlockSpec((None, ta, B), lambda l, i, c_ref: (l, i % nb, 0))
    shp = jax.ShapeDtypeStruct((L, A, B), F32)
    return pl.pallas_call(
        body, name=name,
        grid_spec=pltpu.PrefetchScalarGridSpec(num_scalar_prefetch=1, grid=(L, A // ta),
                                               in_specs=[blk, half, half, blk, blk], out_specs=[blk] * 4),
        out_shape=[shp] * 4,
        compiler_params=_params("parallel", "parallel"),
    )(_scalar(lax.axis_index("c")), w, g_mine, g_other, m, v)


def _chip_index():
    return 2 * lax.axis_index("x") + lax.axis_index("y")


def _pair_sum(g, recv, *, name):
    n, A, B = g.shape
    ta = _tile(A // 2, ROW_TILE)
    nb = A // 2 // ta

    def body(c_ref, g_ref, r_ref, o_ref):
        o_ref[...] = g_ref[...] + r_ref[...]

    return pl.pallas_call(
        body, name=name,
        grid_spec=pltpu.PrefetchScalarGridSpec(
            num_scalar_prefetch=1, grid=(n, nb),
            in_specs=[pl.BlockSpec((None, ta, B), lambda s, r, c_ref: (s, c_ref[0] * nb + r, 0)),
                      pl.BlockSpec((None, ta, B), lambda s, r, c_ref: (s, r, 0))],
            out_specs=pl.BlockSpec((None, ta, B), lambda s, r, c_ref: (s, r, 0))),
        out_shape=jax.ShapeDtypeStruct((n, A // 2, B), F32),
        compiler_params=_params("parallel", "parallel"),
    )(_scalar(lax.axis_index("c")), g, recv)


def _chip_sum(landed, own, *, name):
    n, A2, B = landed.shape
    ta = _tile(A2, ROW_TILE)

    def body(me_ref, *refs):
        slots, own_ref, o_ref = refs[:n], refs[n], refs[n + 1]
        parts = [jnp.where(me_ref[0] == s, own_ref[...], slots[s][...]) for s in range(n)]
        o_ref[...] = ((parts[0] + parts[1]) + parts[2]) + parts[3]

    def slot(s):
        return pl.BlockSpec((None, ta, B), lambda r, me_ref: (jnp.where(me_ref[0] == s, (s + 1) % n, s), r, 0))

    return pl.pallas_call(
        body, name=name,
        grid_spec=pltpu.PrefetchScalarGridSpec(
            num_scalar_prefetch=1, grid=(A2 // ta,),
            in_specs=[slot(s) for s in range(n)] + [pl.BlockSpec((None, ta, B), lambda r, me_ref: (me_ref[0], r, 0))],
            out_specs=pl.BlockSpec((ta, B), lambda r, me_ref: (r, 0))),
        out_shape=jax.ShapeDtypeStruct((A2, B), F32),
        compiler_params=_params("parallel"),
    )(_scalar(_chip_index()), *([landed] * n), own)


HBM_SPEC = pl.BlockSpec(memory_space=pl.ANY)
COMM_PARAMS = pltpu.CompilerParams(has_side_effects=True)


def _mesh_pos():
    return lax.axis_index("x"), lax.axis_index("y"), lax.axis_index("c")


def _other_chips(x, y):
    return [(1 - x, y), (x, 1 - y), (1 - x, 1 - y)]


def _remote(src, dst, send_sems, recv_sems, k, to):
    return pltpu.make_async_remote_copy(src_ref=src, dst_ref=dst, send_sem=send_sems.at[k], recv_sem=recv_sems.at[k],
                                        device_id=to, device_id_type=MESH)


def _place_own(gathered, shard, *, name):
    A, B = shard.shape
    ta = _tile(A, ROW_TILE)

    def body(me_ref, s_ref, g_ref, o_ref):
        o_ref[...] = s_ref[...]

    return pl.pallas_call(
        body, name=name,
        grid_spec=pltpu.PrefetchScalarGridSpec(
            num_scalar_prefetch=1, grid=(A // ta,),
            in_specs=[pl.BlockSpec((ta, B), lambda r, me_ref: (r, 0)), HBM_SPEC],
            out_specs=pl.BlockSpec((None, ta, B), lambda r, me_ref: (me_ref[0], r, 0))),
        out_shape=jax.ShapeDtypeStruct(gathered.shape, gathered.dtype),
        input_output_aliases={2: 0},
        compiler_params=_params("parallel"),
    )(_scalar(_chip_index()), shard, gathered)


def _half_rows(rows, c, align=8):
    assert (rows // 2) % align == 0
    return pl.ds(pl.multiple_of(c * (rows // 2), align), rows // 2)


BF16_ROWS = 16


def _halved(rows):
    return rows % (2 * BF16_ROWS) == 0


def _gather_copies(srcs, lands, send_sems, recv_sems):
    x, y, c = _mesh_pos()
    me = 2 * x + y
    out = []
    for k in range(len(srcs)):
        a = srcs[k].shape[0]
        rows = _half_rows(a, c, BF16_ROWS) if _halved(a) else pl.ds(0, a)
        for j, (px, py) in enumerate(_other_chips(x, y)):
            send = _remote(srcs[k].at[rows], lands[k].at[me, rows], send_sems, recv_sems, 3 * k + j, (px, py, c))
            recv = _remote(srcs[k].at[rows], lands[k].at[2 * px + py, rows], send_sems, recv_sems, 3 * k + j, (px, py, c))
            out.append((send, recv))
    return out


def _gather_start(srcs):
    nl, n = len(srcs), len(srcs[0])
    lands = [[lax.empty((N_CHIPS,) + s.shape, s.dtype) for s in sl] for sl in srcs]
    flat = [a for l in range(nl) for a in srcs[l] + lands[l]]

    def body(*refs):
        bufs, sems, token = refs[:len(flat)], refs[len(flat):len(flat) + 2 * nl], refs[-1]
        for l in range(nl):
            mine = bufs[2 * n * l:2 * n * (l + 1)]
            for send, _ in _gather_copies(mine[:n], mine[n:], sems[2 * l], sems[2 * l + 1]):
                send.start()
        token[...] = jnp.zeros_like(token)

    res = pl.pallas_call(
        body, name="weight_gather_start",
        in_specs=[HBM_ONLY] * len(flat),
        out_specs=[SEM_SPEC] * (2 * nl) + [HBM_ONLY] * len(flat) + [pl.BlockSpec(memory_space=pltpu.VMEM)],
        out_shape=[pltpu.SemaphoreType.DMA((3 * n,))] * (2 * nl) + [pltpu.HBM(a.shape, a.dtype) for a in flat]
        + [jax.ShapeDtypeStruct((8, LANES), F32)],
        input_output_aliases={i: 2 * nl + i for i in range(len(flat))},
        compiler_params=SPLIT_PARAMS,
    )(*[pltpu.with_memory_space_constraint(a, pltpu.HBM) for a in flat])
    bufs = res[2 * nl:2 * nl + len(flat)]
    state = [(res[2 * l], res[2 * l + 1], list(bufs[2 * n * l:2 * n * l + n]), list(bufs[2 * n * l + n:2 * n * (l + 1)]))
             for l in range(nl)]
    return state, res[-1][0:1, 0:1]


def _gather_wait(state, after, *, name):
    send_sems, recv_sems, srcs, lands = state
    n = len(srcs)

    def body(*refs):
        for send, recv in _gather_copies(refs[:n], refs[n:2 * n], refs[2 * n], refs[2 * n + 1]):
            send.wait_send()
            recv.wait_recv()

    res = pl.pallas_call(
        body, name=name,
        in_specs=[HBM_ONLY] * (2 * n) + [SEM_SPEC, SEM_SPEC, HBM_SPEC],
        out_specs=[HBM_ONLY] * (2 * n),
        out_shape=[pltpu.HBM(a.shape, a.dtype) for a in srcs + lands],
        input_output_aliases={i: i for i in range(2 * n)},
        compiler_params=SPLIT_PARAMS,
    )(*srcs, *lands, send_sems, recv_sems, after)
    return list(res[:n]), list(res[n:])


def _gather_forward(lands, *, name):
    n = len(lands)

    def body(*refs):
        bufs, outs = refs[:n], refs[n:2 * n]
        send_sems, recv_sems = refs[2 * n:]
        x, y, c = _mesh_pos()
        copies, waits = [], []
        for k in range(n):
            a = lands[k].shape[1]
            if not _halved(a):
                continue
            for j, (px, py) in enumerate(_other_chips(x, y)):
                mine = 2 * px + py, _half_rows(a, c, BF16_ROWS)
                copies.append(_remote(bufs[k].at[mine], outs[k].at[mine], send_sems, recv_sems, 3 * k + j, (x, y, 1 - c)))
                lands_here = outs[k].at[2 * px + py, _half_rows(a, 1 - c, BF16_ROWS)]
                waits.append(_remote(lands_here, lands_here, send_sems, recv_sems, 3 * k + j, (x, y, 1 - c)))
        for cp in copies:
            cp.start()
        for cp in waits:
            cp.wait_recv()
        for cp in copies:
            cp.wait_send()

    return pl.pallas_call(
        body, name=name,
        in_specs=[HBM_SPEC] * n, out_specs=[HBM_SPEC] * n,
        out_shape=[jax.ShapeDtypeStruct(a.shape, a.dtype) for a in lands],
        scratch_shapes=[pltpu.SemaphoreType.DMA((3 * n,)), pltpu.SemaphoreType.DMA((3 * n,))],
        input_output_aliases={i: i for i in range(n)},
        compiler_params=COMM_PARAMS,
    )(*lands)


def _sibling_exchange(gs, *, name):
    n = len(gs)

    def body(*refs):
        ins, outs = refs[:n], refs[n:2 * n]
        send_sems, recv_sems = refs[2 * n:]
        x, y, c = _mesh_pos()
        copies = [_remote(ins[k].at[:, _half_rows(gs[k].shape[1], 1 - c)], outs[k], send_sems, recv_sems, k, (x, y, 1 - c))
                  for k in range(n)]
        for cp in copies:
            cp.start()
        for cp in copies:
            cp.wait()

    return pl.pallas_call(
        body, name=name,
        in_specs=[HBM_SPEC] * n, out_specs=[HBM_SPEC] * n,
        out_shape=[jax.ShapeDtypeStruct((g.shape[0], g.shape[1] // 2, g.shape[2]), g.dtype) for g in gs],
        scratch_shapes=[pltpu.SemaphoreType.DMA((n,)), pltpu.SemaphoreType.DMA((n,))],
        compiler_params=COMM_PARAMS,
    )(*gs)


HBM_ONLY = pl.BlockSpec(memory_space=pltpu.HBM)
SEM_SPEC = pl.BlockSpec(memory_space=pltpu.SEMAPHORE)
SPLIT_PARAMS = pltpu.CompilerParams(has_side_effects=pltpu.SideEffectType.DATAFLOW_SIDE_EFFECTING)


def _scatter_copies(srcs, lands, send_sems, recv_sems):
    x, y, c = _mesh_pos()
    me = 2 * x + y
    out = []
    for k in range(len(srcs)):
        for j, (px, py) in enumerate(_other_chips(x, y)):
            s = 2 * px + py
            send = _remote(srcs[k].at[s], lands[k].at[me], send_sems, recv_sems, 3 * k + j, (px, py, c))
            recv = _remote(srcs[k].at[s], lands[k].at[s], send_sems, recv_sems, 3 * k + j, (px, py, c))
            out.append((send, recv))
    return out


def _scatter_start(ps, *, name):
    n = len(ps)
    lands = [lax.empty(p.shape, p.dtype) for p in ps]

    def body(*refs):
        srcs, zones = refs[:n], refs[n:2 * n]
        send_sems, recv_sems, token = refs[2 * n], refs[2 * n + 1], refs[-1]
        for send, _ in _scatter_copies(srcs, zones, send_sems, recv_sems):
            send.start()
        token[...] = jnp.zeros_like(token)

    hbm = lambda a: pltpu.HBM(a.shape, a.dtype)
    res = pl.pallas_call(
        body, name=name,
        in_specs=[HBM_ONLY] * (2 * n),
        out_specs=[SEM_SPEC, SEM_SPEC] + [HBM_ONLY] * (2 * n) + [pl.BlockSpec(memory_space=pltpu.VMEM)],
        out_shape=[pltpu.SemaphoreType.DMA((3 * n,)), pltpu.SemaphoreType.DMA((3 * n,))] + [hbm(a) for a in ps + lands]
        + [jax.ShapeDtypeStruct((8, LANES), F32)],
        input_output_aliases={i: 2 + i for i in range(2 * n)},
        compiler_params=SPLIT_PARAMS,
    )(*[pltpu.with_memory_space_constraint(a, pltpu.HBM) for a in ps + lands])
    return (res[0], res[1], list(res[2:2 + n]), list(res[2 + n:2 + 2 * n])), res[-1][0:1, 0:1]


def _scatter_wait(started, after, *, name):
    ng = len(started)
    sizes = [len(st[2]) for st in started]
    offs = [2 * sum(sizes[:i]) for i in range(ng + 1)]
    flat = [a for (_, _, ps, lands) in started for a in ps + lands]

    def body(*refs):
        bufs, sems = refs[:len(flat)], refs[len(flat):len(flat) + 2 * ng]
        for i, n in enumerate(sizes):
            srcs, zones = bufs[offs[i]:offs[i] + n], bufs[offs[i] + n:offs[i + 1]]
            for send, recv in _scatter_copies(srcs, zones, sems[2 * i], sems[2 * i + 1]):
                send.wait_send()
                recv.wait_recv()

    res = pl.pallas_call(
        body, name=name,
        in_specs=[HBM_ONLY] * len(flat) + [SEM_SPEC] * (2 * ng) + [HBM_SPEC],
        out_specs=[HBM_ONLY] * len(flat),
        out_shape=[pltpu.HBM(a.shape, a.dtype) for a in flat],
        input_output_aliases={i: i for i in range(len(flat))},
        compiler_params=SPLIT_PARAMS,
    )(*flat, *[s for (ss, rs, _, _) in started for s in (ss, rs)], after)
    return [(list(res[offs[i]:offs[i] + n]), list(res[offs[i] + n:offs[i + 1]])) for i, n in enumerate(sizes)]


def _sibling_share(hs):
    n = len(hs)

    def body(*refs):
        ins, outs = refs[:n], refs[n:2 * n]
        send_sems, recv_sems = refs[2 * n:]
        x, y, c = _mesh_pos()
        copies = [_remote(ins[k], outs[k], send_sems, recv_sems, k, (x, y, 1 - c)) for k in range(n)]
        for cp in copies:
            cp.start()
        for cp in copies:
            cp.wait()

    return pl.pallas_call(
        body, name="grad_sibling_share",
        in_specs=[HBM_SPEC] * n, out_specs=[HBM_SPEC] * n,
        out_shape=[jax.ShapeDtypeStruct(h.shape, h.dtype) for h in hs],
        scratch_shapes=[pltpu.SemaphoreType.DMA((n,)), pltpu.SemaphoreType.DMA((n,))],
        compiler_params=COMM_PARAMS,
    )(*hs)


def _allreduce_small(part):
    rows, C = part.shape

    def body(p_ref, o_ref, slots, send_sems, recv_sems):
        x, y, c = _mesh_pos()
        me = 4 * x + 2 * y + c
        slots[me] = p_ref[...]
        copies = []
        for k in range(1, 8):
            kx, ky, kc = (k >> 2) & 1, (k >> 1) & 1, k & 1
            peer = (x ^ kx if kx else x, y ^ ky if ky else y, c ^ kc if kc else c)
            cp = _remote(p_ref, slots.at[me], send_sems, recv_sems, k - 1, peer)
            cp.start()
            copies.append((cp, peer))
        for k, (cp, peer) in enumerate(copies):
            src = 4 * peer[0] + 2 * peer[1] + peer[2]
            _remote(p_ref, slots.at[src], send_sems, recv_sems, k, peer).wait_recv()
        for cp, _ in copies:
            cp.wait_send()
        total = slots[0]
        for d in range(1, 8):
            total = total + slots[d]
        o_ref[...] = total

    return pl.pallas_call(
        body, name="small_grad_allreduce",
        in_specs=[pl.BlockSpec(memory_space=pltpu.VMEM)], out_specs=pl.BlockSpec(memory_space=pltpu.VMEM),
        out_shape=jax.ShapeDtypeStruct((rows, C), F32),
        scratch_shapes=[pltpu.VMEM((8, rows, C), F32), pltpu.SemaphoreType.DMA((7,)), pltpu.SemaphoreType.DMA((7,))],
        compiler_params=pltpu.CompilerParams(has_side_effects=True, vmem_limit_bytes=VMEM_LIMIT_BYTES),
    )(part)


def _pad_w_uq(w):
    lead = w.shape[:-1]
    w = w.reshape(lead + (MLA_HEADS, MLA_QK))
    w = jnp.concatenate([w, jnp.zeros(lead + (MLA_HEADS, MLA_PAD - MLA_QK), w.dtype)], axis=-1)
    return w.reshape(lead + (MLA_HEADS * MLA_PAD,))


def _unpad_w_uq(g):
    lead = g.shape[:-1]
    return g.reshape(lead + (MLA_HEADS, MLA_PAD))[..., :MLA_QK].reshape(lead + (MLA_HEADS * MLA_QK,))


def _t(a):
    return jnp.swapaxes(a, -1, -2)


def _shards_of_cols(w):
    A, NB = w.shape
    return w.reshape(A, N_CHIPS, NB // N_CHIPS).transpose(1, 0, 2)


BIG = ("w_in", "w_uq", "w_ukv", "w_out", "w_up", "w_down")
SMALL = ("attn_pre_norm", "forget_bias", "swa_sinks", "rel_bias", "q_latent_norm", "kv_latent_norm", "group_norm",
         "attn_post_norm", "ffn_pre_norm", "conv_b", "ffn_post_norm")
WEIGHTS = ("attn_pre_norm", "w_in", "forget_bias", "swa_sinks", "rel_bias", "q_latent_norm", "w_uq", "kv_latent_norm",
           "w_ukv", "group_norm", "w_out", "attn_post_norm", "ffn_pre_norm", "w_up", "conv_w", "conv_b", "w_down",
           "ffn_post_norm")


def _pack(arrs, cols, row_mult):
    flat = jnp.concatenate([a.reshape(-1) for a in arrs])
    n = flat.shape[0]
    per = cols * row_mult
    total = -(-n // per) * per
    return jnp.pad(flat, (0, total - n)).reshape(total // cols, cols)


def _unpack(packed, shapes):
    flat = packed.reshape(-1)
    out, off = [], 0
    for shp in shapes:
        n = int(np.prod(shp))
        out.append(flat[off:off + n].reshape(shp))
        off += n
    return out


LAYER_KEYS = ("w_qkv_t", "w_lat_t", "w_in_t", "w_uq_p", "w_uq_t", "w_ukv", "w_ukv_t", "w_out", "w_up", "w_down", "conv_w")


def _layer_weights(gathered):
    cols = lambda g: g.transpose(1, 0, 2).reshape(g.shape[1], N_CHIPS * g.shape[2])
    w_in_t = _t(gathered["w_in"]).reshape(IN_COLS, D_MODEL)
    w_in_t = jnp.pad(w_in_t, ((0, IN_ROWS - IN_COLS), (0, 0)))
    w_uq_p = _pad_w_uq(cols(gathered["w_uq"]))
    w_ukv = cols(gathered["w_ukv"])
    return dict(w_qkv_t=w_in_t[:QKV_ROWS], w_lat_t=w_in_t[QKV_ROWS:], w_in_t=w_in_t, w_uq_p=w_uq_p, w_uq_t=_t(w_uq_p),
                w_ukv=w_ukv, w_ukv_t=_t(w_ukv), w_out=gathered["w_out"].reshape(D_MODEL, D_MODEL), w_up=gathered["w_up"],
                w_down=gathered["w_down"].reshape(D_FF, D_MODEL), conv_w=cols(gathered["conv_w"]))


def _local_step(x, target, W, layer_weights, layer_done):
    W = dict(W, **{key: [None] * DEPTH for key in LAYER_KEYS})
    S = x.shape[0]
    tq_tabs, tm_tabs = _rope_tables(S)
    onehot_t = _rel_onehot_t()
    bias_t = _bias_table(W["rel_bias"].T, onehot_t).reshape(SWA_KV_HEADS, SWA_GROUP, 2 * WINDOW, WINDOW)
    bias_t = bias_t.transpose(0, 2, 1, 3).reshape(SWA_KV_HEADS, 2 * WINDOW, GW)
    row = lambda a: a.reshape(1, -1)
    col = lambda a: a.reshape(-1, 1)
    fox_rows = (FOX_ROW0, FOX_ROW0 + FOX_HEADS * HEAD_DIM, FOX_ROW0 + 2 * FOX_HEADS * HEAD_DIM, SWA_Q_HEADS)
    fox = dict(rows=fox_rows, H=FOX_HEADS, Dk=HEAD_DIM, Dv=HEAD_DIM, scale=HEAD_DIM ** -0.5)
    mla = dict(rows=(0, 0, 0, SWA_Q_HEADS + FOX_HEADS), H=MLA_HEADS, Dk=MLA_PAD, Dv=HEAD_DIM, scale=MLA_QK ** -0.5)

    saved = []
    h = _rms_fwd(x, row(W["attn_pre_norm"][0]), name="rms_in")
    for l in range(DEPTH):
        sv = {"x0": x, "h1": h}
        for key, val in layer_weights(l, h).items():
            W[key][l] = val
        qkv = _matmul(W["w_qkv_t"][l], h, tb=True, out_dtype=BF16, name="proj_qkv")
        lat = _matmul(W["w_lat_t"][l], h, tb=True, name="proj_lat")
        oa, lse_a = _swa_fwd(qkv, bias_t, W["swa_sinks"][l], name="swa_fwd")
        fb_col = jnp.pad(col(W["forget_bias"][l]), ((0, GATE_ROWS - FOX_HEADS), (0, 0)))
        f4 = _gate_fwd(lat, fb_col, name="fox_gate_fwd")[:FOX_HEADS]
        f2 = f4 * LOG2E
        f_row, f_col = f2[:, None, :], f2.T
        of, lse_f = _attn_fwd(qkv, qkv, qkv, f_row=f_row, f_col=f_col, name="fox_fwd", **fox)
        nq, nkv, qm, km, vm = _mla_prep_fwd(lat, col(W["q_latent_norm"][l]), col(W["kv_latent_norm"][l]), W["w_uq_t"][l],
                                            W["w_ukv_t"][l], tq_tabs, tm_tabs, name="mla_prep_fwd")
        oc, lse_c = _attn_fwd(qm, km, vm, name="mla_fwd", **mla)
        mixed = _group_norm_fwd(oa, of, oc, col(W["group_norm"][l]), name="group_norm_fwd")
        y = _matmul(mixed, W["w_out"][l], ta=True, name="proj_out")
        x1, h2 = _resid_rms(x, y, row(W["attn_post_norm"][l]), row(W["ffn_pre_norm"][l]), name="attn_resid")
        a = _matmul(h2, W["w_up"][l], b_shards=True, out_dtype=BF16, name="ffn_up")
        z = _conv_geglu_fwd(a, W["conv_w"][l], row(W["conv_b"][l]), name="conv_geglu_fwd")
        y2 = _matmul(z, W["w_down"][l], name="ffn_down")
        g_next = row(W["attn_pre_norm"][l + 1]) if l + 1 < DEPTH else None
        x2, h_next = _resid_rms(x1, y2, row(W["ffn_post_norm"][l]), g_next, name="ffn_resid")
        sv.update(qkv=qkv, lat=lat, oa=oa, lse_a=lse_a, fb_col=fb_col, f_row=f_row, f_col=f_col, of=of, lse_f=lse_f,
                  nq=nq, nkv=nkv, qm=qm, km=km, vm=vm, oc=oc, lse_c=lse_c, mixed=mixed, y=y, x1=x1, h2=h2, a=a, z=z, y2=y2)
        saved.append(sv)
        x, h = x2, h_next

    loss, dx = _loss_head(x, target)

    G = {k: [None] * DEPTH for k in WEIGHTS if k != "rel_bias" and k not in BIG}
    dbias_layers = [None] * DEPTH
    for l in reversed(range(DEPTH)):
        sv = saved[l]
        gb = {}
        dy2, dg = _rms_bwd(sv["y2"], row(W["ffn_post_norm"][l]), dx, out_dtype=BF16, name="ffn_post_bwd")
        G["ffn_post_norm"][l] = dg[0]
        dz = _matmul(dy2, W["w_down"][l], tb=True, name="ffn_down_dx")
        gb["w_down"] = _matmul(sv["z"], dy2, ta=True, name="ffn_down_dw").reshape(N_CHIPS, D_FF // N_CHIPS, D_MODEL)
        da, dcw, dcb = _conv_geglu_bwd(sv["a"], W["conv_w"][l], row(W["conv_b"][l]), dz, name="conv_geglu_bwd")
        G["conv_w"][l] = dcw.transpose(1, 0, 2).reshape(3, 2 * D_FF)
        G["conv_b"][l] = dcb.reshape(2 * D_FF)
        dh2 = _matmul(da, W["w_up"][l], tb=True, b_shards=True, a_halves=True, name="ffn_up_dx")
        gb["w_up"] = _matmul(sv["h2"], da, ta=True, out_shards=True, b_halves=True, name="ffn_up_dw")
        token = layer_done(l, gb)
        gb = {}
        dx1, dg = _rms_bwd(sv["x1"], row(W["ffn_pre_norm"][l]) + token, dh2, resid=dx, out_dtype=F32, name="ffn_pre_bwd")
        G["ffn_pre_norm"][l] = dg[0]
        dy, dg = _rms_bwd(sv["y"], row(W["attn_post_norm"][l]), dx1, out_dtype=BF16, name="attn_post_bwd")
        G["attn_post_norm"][l] = dg[0]
        dmixed = _matmul(W["w_out"][l], dy, tb=True, name="proj_out_dx")
        gb["w_out"] = _matmul(sv["mixed"], dy, name="proj_out_dw").reshape(N_CHIPS, D_MODEL // N_CHIPS, D_MODEL)
        doa, dof, doc, dg, delta = _group_norm_bwd(sv["oa"], sv["of"], sv["oc"], col(W["group_norm"][l]), dmixed,
                                                   name="group_norm_bwd")
        G["group_norm"][l] = dg[:, 0]
        dqa, dkva, dbias_l, dsink = _swa_bwd(sv["qkv"], bias_t, W["swa_sinks"][l], doa, sv["lse_a"],
                                             delta.reshape(-1, S), name="swa_bwd")
        dbias_layers[l] = (dbias_l.reshape(SWA_KV_HEADS, 2 * WINDOW, SWA_GROUP, WINDOW).transpose(0, 2, 1, 3)
                           .reshape(SWA_Q_HEADS, -1))
        G["swa_sinks"][l] = dsink[:, 0]
        dqf, dkf, dvf, dfk = _attn_bwd(sv["qkv"], sv["qkv"], sv["qkv"], do=dof, lse=sv["lse_f"], delta=delta,
                                       f_row=sv["f_row"], f_col=sv["f_col"], name="fox_bwd", **fox)
        dF = jnp.pad(dfk.T, ((0, GATE_ROWS - FOX_HEADS), (0, 0)))
        dflog, dfb = _gate_bwd(sv["lat"], sv["fb_col"], dF, name="fox_gate_bwd")
        G["forget_bias"][l] = dfb[:FOX_HEADS, 0]
        dqm, dkm, dvm = _attn_bwd(sv["qm"], sv["km"], sv["vm"], do=doc, lse=sv["lse_c"], delta=delta, name="mla_bwd", **mla)
        dlat, dwq_t, dwkv_t, dgq, dgkv = _mla_prep_bwd(
            sv["lat"], sv["nq"], sv["nkv"], col(W["q_latent_norm"][l]), col(W["kv_latent_norm"][l]), W["w_uq_p"][l],
            W["w_ukv"][l], tq_tabs, tm_tabs, dqm, dkm, dvm, dflog, name="mla_prep_bwd")
        gb["w_uq"], gb["w_ukv"] = _shards_of_cols(_unpad_w_uq(dwq_t.T)), _shards_of_cols(dwkv_t.T)
        G["q_latent_norm"][l], G["kv_latent_norm"][l] = dgq[:, 0], dgkv[:, 0]
        dproj = _dproj_cast(dqa, dkva, dqf, dkf, dvf, dlat, name="dproj_cast")
        dh1 = _matmul(dproj, W["w_in_t"][l], ta=True, name="proj_in_dx")
        dw_in_t = _matmul(dproj, sv["h1"], name="proj_in_dw")
        gb["w_in"] = _t(dw_in_t[:IN_COLS].reshape(N_CHIPS, IN_COLS // N_CHIPS, D_MODEL))
        token = layer_done(l, gb)
        dx, dg = _rms_bwd(sv["x0"], row(W["attn_pre_norm"][l]) + token, dh1, resid=dx1, out_dtype=F32, name="attn_pre_bwd")
        G["attn_pre_norm"][l] = dg[0]

    grads = {k: jnp.stack(v) for k, v in G.items()}
    grads["rel_bias"] = _bias_table_bwd(jnp.stack(dbias_layers), onehot_t).T
    return loss, dx, grads


def kernel(x, attn_pre_norm, w_in, forget_bias, swa_sinks, rel_bias, q_latent_norm, w_uq, kv_latent_norm, w_ukv, group_norm, w_out, attn_post_norm, ffn_pre_norm, w_up, conv_w, conv_b, w_down, ffn_post_norm, loss_target, m_attn_pre_norm, m_w_in, m_forget_bias, m_swa_sinks, m_rel_bias, m_q_latent_norm, m_w_uq, m_kv_latent_norm, m_w_ukv, m_group_norm, m_w_out, m_attn_post_norm, m_ffn_pre_norm, m_w_up, m_conv_w, m_conv_b, m_w_down, m_ffn_post_norm, v_attn_pre_norm, v_w_in, v_forget_bias, v_swa_sinks, v_rel_bias, v_q_latent_norm, v_w_uq, v_kv_latent_norm, v_w_ukv, v_group_norm, v_w_out, v_attn_post_norm, v_ffn_pre_norm, v_w_up, v_conv_w, v_conv_b, v_w_down, v_ffn_post_norm):
    args = dict(locals())
    w = {k: args[k] for k in WEIGHTS}
    m = {k: args["m_" + k] for k in WEIGHTS}
    v = {k: args["v_" + k] for k in WEIGHTS}

    sent = BIG + ("conv_w",)
    gather_state, token = _gather_start([[w[k][l] if k == "conv_w" else w[k][l].astype(BF16) for k in sent]
                                         for l in range(DEPTH)])
    W = {k: w[k] for k in SMALL}
    W["attn_pre_norm"] = W["attn_pre_norm"] + token

    def layer_weights(l, after):
        srcs, lands = _gather_wait(gather_state[l], after, name=f"weight_gather_wait_{l}")
        lands = _gather_forward(lands, name=f"weight_gather_forward_{l}")
        lands = [_place_own(g, s, name="place_own_shard") for g, s in zip(lands, srcs)]
        return _layer_weights(dict(zip(sent, lands)))

    started, groups = [], []

    def layer_done(l, gb):
        keys = [k for k in BIG if k in gb]
        gs = [gb[k] for k in keys]
        tag = f"{l}_{keys[0]}"
        recv = _sibling_exchange(gs, name="grad_sibling_exchange_" + tag)
        pair = [_pair_sum(gk, rk, name="grad_pair_sum") for gk, rk in zip(gs, recv)]
        state, token = _scatter_start(pair, name="grad_scatter_start_" + tag)
        started.append(state)
        groups.append((l, keys))
        return token

    loss_part, dx, g = _local_step(x[0], loss_target[0], W, layer_weights, layer_done)
    loss = lax.psum(loss_part, ("x", "y", "c"))

    reduced = {}
    for (l, keys), (pair, zones) in zip(groups, _scatter_wait(started, dx, name="grad_scatter_wait")):
        for k, p, z in zip(keys, pair, zones):
            reduced[k, l] = _chip_sum(z, p, name="grad_chip_sum")
    mine = [jnp.stack([reduced[k, l] for l in range(DEPTH)]) for k in BIG]
    other = _sibling_share(mine)
    out_g, out_d, out_m, out_v = {}, {}, {}, {}
    for k, g_mine, g_other in zip(BIG, mine, other):
        out_g[k], out_d[k], out_m[k], out_v[k] = _adamw_halves(w[k], g_mine, g_other, m[k], v[k], name="adamw_" + k)

    small_shapes = [w[k].shape for k in SMALL]
    reduced = _allreduce_small(_pack([g[k] for k in SMALL] + [g["conv_w"]], LANES, 8))
    *g_small, g_cw = _unpack(reduced, small_shapes + [g["conv_w"].shape])
    chip = 2 * lax.axis_index("x") + lax.axis_index("y")
    g_small.append(lax.dynamic_slice_in_dim(g_cw, chip * FF_SHARD, FF_SHARD, axis=2))
    names = SMALL + ("conv_w",)
    shapes = small_shapes + [w["conv_w"].shape]
    packed = lambda arrs: _pack(arrs, LANES, ROW_TILE)[None]
    d_s, m_s, v_s = _adamw(packed([w[k] for k in names]), packed(g_small), packed([m[k] for k in names]),
                           packed([v[k] for k in names]), name="adamw_small")
    out_g.update(zip(names, g_small))
    out_d.update(zip(names, _unpack(d_s, shapes)))
    out_m.update(zip(names, _unpack(m_s, shapes)))
    out_v.update(zip(names, _unpack(v_s, shapes)))

    return (loss, dx[None], *[out_g[k] for k in WEIGHTS], *[out_d[k] for k in WEIGHTS],
            *[out_m[k] for k in WEIGHTS], *[out_v[k] for k in WEIGHTS])
```

```python
import math

import numpy as np
import jax
import jax.numpy as jnp
from jax import lax
from jax.experimental import pallas as pl
from jax.experimental.pallas import tpu as pltpu

F32 = jnp.float32
BF16 = jnp.bfloat16

D_MODEL = 1024
DEPTH = 4
HEAD_DIM = 64
SWA_Q_HEADS = 8
SWA_KV_HEADS = 2
SWA_GROUP = SWA_Q_HEADS // SWA_KV_HEADS
WINDOW = 128
FOX_HEADS = 4
MLA_HEADS = 4
MLA_Q_RANK = 256
MLA_KV_RANK = 128
MLA_NOPE = 64
MLA_ROPE = 32
MLA_QK = MLA_NOPE + MLA_ROPE
ROPE_THETA = 10000.0
REL_BUCKETS = 32
REL_MAX_DIST = 128
D_FF = 2816
EPS = 1e-6
NEG_INF = -1e30
LANES = 128
N_CHIPS = 4

IN_COLS = 1956
IN_ROWS = 2048
QKV_ROWS = 1536
LAT_ROWS = IN_ROWS - QKV_ROWS
LAT_SHIFT = FOX_HEADS
FOX_ROW0 = 768
MLA_PAD = LANES
GATE_ROWS = 8

ADAM_LR = 0.001
ADAM_B1 = 0.9
ADAM_B2 = 0.999
ADAM_EPS = 1e-08
ADAM_WD = 0.01
ADAM_STEP = 10

VMEM_LIMIT_BYTES = 48 * 1024 * 1024
ATT_TILE = 512
LOG2E = math.log2(math.e)
ROW_TILE = 256
MESH = pl.DeviceIdType.MESH

NT = (((1,), (1,)), ((), ()))
TN = (((0,), (0,)), ((), ()))
NN = (((1,), (0,)), ((), ()))


def _params(*sem):
    return pltpu.CompilerParams(dimension_semantics=sem, vmem_limit_bytes=VMEM_LIMIT_BYTES)


def _tile(dim, cap):
    for t in (2816, 2048, 1408, 1024, 512, 256, 128, 64, 32, 16, 8):
        if t <= cap and dim % t == 0:
            return t
    return dim


def _dot(a, b, dims=NN):
    return lax.dot_general(a, b, dims, preferred_element_type=F32)


def _split3(a):
    a1 = a.astype(BF16)
    r1 = a - a1.astype(F32)
    a2 = r1.astype(BF16)
    a3 = (r1 - a2.astype(F32)).astype(BF16)
    return a1, a2, a3


FF_SHARD = 2 * D_FF // N_CHIPS
MATMUL_VMEM_BYTES = 40 * 1024 * 1024


def _matmul(a, b, *, ta=False, tb=False, out_dtype=F32, name, b_shards=False, out_shards=False, a_halves=False,
            b_halves=False):
    if a_halves:
        M, K = a.shape[1], 2 * a.shape[2]
    elif ta:
        K, M = a.shape
    else:
        M, K = a.shape
    if b_halves:
        K2, N = b.shape[1], 2 * b.shape[2]
    elif b_shards:
        K2, N = (2 * D_FF, D_MODEL) if tb else (D_MODEL, 2 * D_FF)
    elif tb:
        N, K2 = b.shape
    else:
        K2, N = b.shape
    assert K == K2, (a.shape, b.shape)
    tm, tn = (M if M <= 2048 else _tile(M, 1408)), _tile(N, 1408)
    tk = FF_SHARD if (b_shards and tb) else _tile(K, 2816)
    out_bytes = jnp.dtype(out_dtype).itemsize
    while 2 * 2 * tk * (tm + tn) + (4 + 2 * out_bytes) * tm * tn > MATMUL_VMEM_BYTES and tk % 256 == 0:
        tk //= 2
    nk = K // tk
    dims = (((0 if ta else 1,), (1 if tb else 0,)), ((), ()))

    def body(a_ref, b_ref, o_ref, acc_ref):
        k = pl.program_id(2)

        @pl.when(k == 0)
        def _():
            acc_ref[...] = jnp.zeros_like(acc_ref)

        acc_ref[...] += lax.dot_general(a_ref[...], b_ref[...], dims, preferred_element_type=F32)

        @pl.when(k == nk - 1)
        def _():
            o_ref[...] = acc_ref[...].astype(o_ref.dtype)

    if a_halves:
        nh = K // 2 // tk
        a_spec = pl.BlockSpec((None, tm, tk), lambda i, j, k: (k // nh, i, k % nh))
    else:
        a_spec = pl.BlockSpec((tk, tm), lambda i, j, k: (k, i)) if ta else pl.BlockSpec((tm, tk), lambda i, j, k: (i, k))
    if b_halves:
        nh = N // 2 // tn
        b_spec = pl.BlockSpec((None, tk, tn), lambda i, j, k: (j // nh, k, j % nh))
    elif b_shards and tb:
        assert tk == FF_SHARD
        b_spec = pl.BlockSpec((None, tn, tk), lambda i, j, k: (k, j, 0))
    elif b_shards:
        assert tn == FF_SHARD
        b_spec = pl.BlockSpec((None, tk, tn), lambda i, j, k: (j, k, 0))
    else:
        b_spec = pl.BlockSpec((tn, tk), lambda i, j, k: (j, k)) if tb else pl.BlockSpec((tk, tn), lambda i, j, k: (k, j))
    if out_shards:
        assert tn == FF_SHARD
        out_spec = pl.BlockSpec((None, tm, tn), lambda i, j, k: (j, i, 0))
        out_shape = jax.ShapeDtypeStruct((N // tn, M, tn), out_dtype)
    else:
        out_spec = pl.BlockSpec((tm, tn), lambda i, j, k: (i, j))
        out_shape = jax.ShapeDtypeStruct((M, N), out_dtype)
    return pl.pallas_call(
        body, name=name, grid=(M // tm, N // tn, nk),
        in_specs=[a_spec, b_spec], out_specs=out_spec, out_shape=out_shape,
        scratch_shapes=[pltpu.VMEM((tm, tn), F32)],
        compiler_params=_params("parallel", "parallel", "arbitrary"),
    )(a, b)


def _seg_rms(xs, g):
    r = lax.rsqrt(jnp.mean(xs * xs, axis=-1, keepdims=True) + EPS)
    return xs * r * g


def _seg_rms_bwd(xs, g, dy):
    r = lax.rsqrt(jnp.mean(xs * xs, axis=-1, keepdims=True) + EPS)
    gd = dy * g
    c = jnp.mean(gd * xs, axis=-1, keepdims=True)
    dx = r * gd - xs * (r * r * r * c)
    dg = jnp.sum(dy * (xs * r), axis=0, keepdims=True)
    return dx, dg


def _rms_fwd(x, g, *, name):
    S, W = x.shape
    tm = _tile(S, 512)

    def body(x_ref, g_ref, o_ref):
        o_ref[...] = _seg_rms(x_ref[...], g_ref[...]).astype(o_ref.dtype)

    return pl.pallas_call(
        body, name=name, grid=(S // tm,),
        in_specs=[pl.BlockSpec((tm, W), lambda i: (i, 0)), pl.BlockSpec((1, W), lambda i: (0, 0))],
        out_specs=pl.BlockSpec((tm, W), lambda i: (i, 0)),
        out_shape=jax.ShapeDtypeStruct((S, W), BF16),
        compiler_params=_params("parallel"),
    )(x, g)


def _rms_bwd(x, g, dy, *, resid=None, out_dtype, name):
    S, W = x.shape
    tm = _tile(S, 512)
    has_resid = resid is not None

    def body(*refs):
        if has_resid:
            x_ref, g_ref, dy_ref, r_ref, dx_ref, dg_ref = refs
        else:
            x_ref, g_ref, dy_ref, dx_ref, dg_ref = refs

        @pl.when(pl.program_id(0) == 0)
        def _():
            dg_ref[...] = jnp.zeros_like(dg_ref)

        dx, dg = _seg_rms_bwd(x_ref[...], g_ref[...], dy_ref[...])
        if has_resid:
            dx = dx + r_ref[...]
        dx_ref[...] = dx.astype(dx_ref.dtype)
        dg_ref[...] += dg

    row = pl.BlockSpec((tm, W), lambda i: (i, 0))
    vec = pl.BlockSpec((1, W), lambda i: (0, 0))
    ins = [x, g, dy] + ([resid] if has_resid else [])
    return pl.pallas_call(
        body, name=name, grid=(S // tm,),
        in_specs=[row, vec, row] + ([row] if has_resid else []),
        out_specs=[row, vec],
        out_shape=[jax.ShapeDtypeStruct((S, W), out_dtype), jax.ShapeDtypeStruct((1, W), F32)],
        compiler_params=_params("arbitrary"),
    )(*ins)


def _resid_rms(x, y, g_post, g_next, *, name):
    S, W = x.shape
    tm = _tile(S, 512)
    with_next = g_next is not None

    def body(*refs):
        if with_next:
            x_ref, y_ref, gp_ref, gn_ref, xo_ref, h_ref = refs
        else:
            x_ref, y_ref, gp_ref, xo_ref = refs
        xn = x_ref[...] + _seg_rms(y_ref[...], gp_ref[...])
        xo_ref[...] = xn
        if with_next:
            h_ref[...] = _seg_rms(xn, gn_ref[...]).astype(BF16)

    row = pl.BlockSpec((tm, W), lambda i: (i, 0))
    vec = pl.BlockSpec((1, W), lambda i: (0, 0))
    outs = [jax.ShapeDtypeStruct((S, W), F32)] + ([jax.ShapeDtypeStruct((S, W), BF16)] if with_next else [])
    res = pl.pallas_call(
        body, name=name, grid=(S // tm,),
        in_specs=[row, row, vec] + ([vec] if with_next else []),
        out_specs=[row] + ([row] if with_next else []),
        out_shape=outs,
        compiler_params=_params("parallel"),
    )(*([x, y, g_post] + ([g_next] if with_next else [])))
    return (res[0], res[1]) if with_next else (res[0], None)


def _col_rms(xs, g):
    r = lax.rsqrt(jnp.mean(xs * xs, axis=0, keepdims=True) + EPS)
    return xs * r * g


def _col_rms_bwd(xs, g, dy):
    r = lax.rsqrt(jnp.mean(xs * xs, axis=0, keepdims=True) + EPS)
    gd = dy * g
    c = jnp.mean(gd * xs, axis=0, keepdims=True)
    dx = r * gd - xs * (r * r * r * c)
    dg = jnp.sum(dy * (xs * r), axis=1, keepdims=True)
    return dx, dg


GROUP_ROWS = (SWA_Q_HEADS * HEAD_DIM, FOX_HEADS * HEAD_DIM, MLA_HEADS * HEAD_DIM)


def _group_specs(S, tn):
    outs = [pl.BlockSpec((n, tn), lambda i: (0, i)) for n in GROUP_ROWS]
    g = pl.BlockSpec((D_MODEL, 1), lambda i: (0, 0))
    mixed = pl.BlockSpec((D_MODEL, tn), lambda i: (0, i))
    return outs, g, mixed


def _group_norm_fwd(oa, of, oc, g, *, name):
    S = oa.shape[1]
    tn = _tile(S, 512)
    outs, gs, mixed = _group_specs(S, tn)

    def body(a_ref, f_ref, c_ref, g_ref, o_ref):
        r0 = 0
        for ref, n in zip((a_ref, f_ref, c_ref), GROUP_ROWS):
            o_ref[r0:r0 + n, :] = _col_rms(ref[...], g_ref[r0:r0 + n, :]).astype(BF16)
            r0 += n

    return pl.pallas_call(
        body, name=name, grid=(S // tn,),
        in_specs=outs + [gs], out_specs=mixed,
        out_shape=jax.ShapeDtypeStruct((D_MODEL, S), BF16),
        compiler_params=_params("parallel"),
    )(oa, of, oc, g)


def _group_norm_bwd(oa, of, oc, g, dmixed, *, name):
    S = oa.shape[1]
    tn = _tile(S, 512)
    outs, gs, mixed = _group_specs(S, tn)
    n_heads = D_MODEL // HEAD_DIM

    def body(a_ref, f_ref, c_ref, g_ref, dm_ref, da_ref, df_ref, dc_ref, dg_ref, dl_ref):
        @pl.when(pl.program_id(0) == 0)
        def _():
            dg_ref[...] = jnp.zeros_like(dg_ref)

        r0 = 0
        for ref, dref, n in zip((a_ref, f_ref, c_ref), (da_ref, df_ref, dc_ref), GROUP_ROWS):
            o = ref[...]
            dx, dg = _col_rms_bwd(o, g_ref[r0:r0 + n, :], dm_ref[r0:r0 + n, :])
            dxb = dx.astype(BF16)
            dref[...] = dxb
            dg_ref[r0:r0 + n, :] += dg
            od = o * dxb.astype(F32)
            for h in range(n // HEAD_DIM):
                dl_ref[r0 // HEAD_DIM + h] = jnp.sum(od[h * HEAD_DIM:(h + 1) * HEAD_DIM, :], axis=0, keepdims=True)
            r0 += n

    return pl.pallas_call(
        body, name=name, grid=(S // tn,),
        in_specs=outs + [gs, mixed], out_specs=outs + [gs, pl.BlockSpec((n_heads, 1, tn), lambda i: (0, 0, i))],
        out_shape=[jax.ShapeDtypeStruct((n, S), BF16) for n in GROUP_ROWS] + [jax.ShapeDtypeStruct((D_MODEL, 1), F32),
                                                                              jax.ShapeDtypeStruct((n_heads, 1, S), F32)],
        compiler_params=_params("arbitrary"),
    )(oa, of, oc, g, dmixed)


def _loss_head(y, target):
    S, W = y.shape
    tm = _tile(S, 512)

    def body(y_ref, t_ref, d_ref, l_ref):
        @pl.when(pl.program_id(0) == 0)
        def _():
            l_ref[...] = jnp.zeros_like(l_ref)

        err = y_ref[...] - t_ref[...]
        d_ref[...] = err * (1.0 / W)
        l_ref[...] += 0.5 * jnp.sum(jnp.mean(err * err, axis=-1, keepdims=True), axis=0, keepdims=True)

    row = pl.BlockSpec((tm, W), lambda i: (i, 0))
    d, l = pl.pallas_call(
        body, name="loss_head", grid=(S // tm,),
        in_specs=[row, row],
        out_specs=[row, pl.BlockSpec((1, 1), lambda i: (0, 0))],
        out_shape=[jax.ShapeDtypeStruct((S, W), F32), jax.ShapeDtypeStruct((1, 1), F32)],
        compiler_params=_params("arbitrary"),
    )(y, target)
    return l[0, 0], d


def _attn_fwd(q_src, k_src, v_src, rows, H, Dk, Dv, scale, f_row=None, f_col=None, *, name):
    S = q_src.shape[1]
    T = _tile(S, ATT_TILE)
    nq = S // T
    forget = f_row is not None
    qb, kb, vb = rows[0] // (H * Dk), rows[1] // (H * Dk), rows[2] // (H * Dv)
    hs = range(H)

    def body(*refs):
        if forget:
            q_ref, k_ref, v_ref, fq_ref, fk_ref, o_ref, lse_ref = refs
        else:
            q_ref, k_ref, v_ref, o_ref, lse_ref = refs
        i = pl.program_id(0)

        def tile(j, masked, state):
            off = pl.multiple_of(j * T, T)
            ss = [_dot(k_ref[h * Dk:(h + 1) * Dk, pl.ds(off, T)], q_ref[h * Dk:(h + 1) * Dk, :], TN) * (scale * LOG2E)
                  for h in hs]
            if forget:
                ss = [ss[h] + (fq_ref[h] - fk_ref[pl.ds(off, T), h:h + 1]) for h in hs]
            if masked:
                r = lax.broadcasted_iota(jnp.int32, (T, T), 0)
                c = lax.broadcasted_iota(jnp.int32, (T, T), 1)
                ss = [jnp.where(r <= c, s, NEG_INF) for s in ss]
            m_new = [jnp.maximum(state[h][0], jnp.max(ss[h], axis=0, keepdims=True)) for h in hs]
            alpha = [jnp.exp2(state[h][0] - m_new[h]) for h in hs]
            ps = [jnp.exp2(ss[h] - m_new[h]) for h in hs]
            l_new = [alpha[h] * state[h][1] + jnp.sum(ps[h], axis=0, keepdims=True) for h in hs]
            p_hi = [p.astype(BF16) for p in ps]
            vs = [v_ref[h * Dv:(h + 1) * Dv, pl.ds(off, T)] for h in hs]
            pv = [_dot(vs[h], p_hi[h]) for h in hs]
            if forget:
                pv = [pv[h] + _dot(vs[h], (ps[h] - p_hi[h].astype(F32)).astype(BF16)) for h in hs]
            return tuple((m_new[h], l_new[h], alpha[h] * state[h][2] + pv[h]) for h in hs)

        init = tuple((jnp.full((1, T), NEG_INF, F32), jnp.zeros((1, T), F32), jnp.zeros((Dv, T), F32)) for _ in hs)
        state = lax.fori_loop(0, i, lambda j, st: tile(j, False, st), init)
        state = tile(i, True, state)
        for h in hs:
            m, l, acc = state[h]
            o_ref[h * Dv:(h + 1) * Dv, :] = acc / l
            lse_ref[h] = m + jnp.log2(l)

    in_specs = [pl.BlockSpec((H * Dk, T), lambda i: (qb, i)),
                pl.BlockSpec((H * Dk, S), lambda i: (kb, 0)),
                pl.BlockSpec((H * Dv, S), lambda i: (vb, 0))]
    ins = [q_src, k_src, v_src]
    if forget:
        in_specs += [pl.BlockSpec((H, 1, T), lambda i: (0, 0, i)), pl.BlockSpec((S, H), lambda i: (0, 0))]
        ins += [f_row, f_col]
    return pl.pallas_call(
        body, name=name, grid=(nq,),
        in_specs=in_specs,
        out_specs=[pl.BlockSpec((H * Dv, T), lambda i: (0, i)), pl.BlockSpec((H, 1, T), lambda i: (0, 0, i))],
        out_shape=[jax.ShapeDtypeStruct((H * Dv, S), F32), jax.ShapeDtypeStruct((H, 1, S), F32)],
        compiler_params=_params("parallel"),
    )(*ins)


def _attn_bwd(q_src, k_src, v_src, rows, H, Dk, Dv, scale, do, lse, delta, f_row=None, f_col=None, *, name):
    S = q_src.shape[1]
    T = _tile(S, ATT_TILE)
    nq = S // T
    forget = f_row is not None
    qb, kb, vb, db = rows[0] // (H * Dk), rows[1] // (H * Dk), rows[2] // (H * Dv), rows[3] // H
    hs = range(H)

    def body(*refs):
        if forget:
            (q_ref, k_ref, v_ref, do_ref, lse_ref, dl_ref, fq_ref, fk_ref,
             dq_ref, dk_ref, dv_ref, df_ref, dk_s, dv_s, df_s) = refs
        else:
            q_ref, k_ref, v_ref, do_ref, lse_ref, dl_ref, dq_ref, dk_ref, dv_ref, dk_s, dv_s = refs
        j = pl.program_id(0)

        @pl.when(j == 0)
        def _():
            dq_ref[...] = jnp.zeros_like(dq_ref)

        dk_s[...] = jnp.zeros_like(dk_s)
        dv_s[...] = jnp.zeros_like(dv_s)
        if forget:
            df_s[...] = jnp.zeros_like(df_s)
        kt = [k_ref[h * Dk:(h + 1) * Dk, :] for h in hs]
        kj = [k.T for k in kt]
        vj = [v_ref[h * Dv:(h + 1) * Dv, :].T for h in hs]
        koff = pl.multiple_of(j * T, T)

        def tile(i, masked):
            cols = pl.ds(pl.multiple_of(i * T, T), T)
            qi = [q_ref[h * Dk:(h + 1) * Dk, cols] for h in hs]
            doi = [do_ref[h * Dv:(h + 1) * Dv, cols] for h in hs]
            st = [_dot(kj[h], qi[h]) * (scale * LOG2E) for h in hs]
            if forget:
                st = [st[h] + (fq_ref[h, :, cols] - fk_ref[pl.ds(koff, T), h:h + 1]) for h in hs]
            if masked:
                r = lax.broadcasted_iota(jnp.int32, (T, T), 0)
                c = lax.broadcasted_iota(jnp.int32, (T, T), 1)
                st = [jnp.where(r <= c, x, NEG_INF) for x in st]
            pt = [jnp.exp2(st[h] - lse_ref[h, :, cols]) for h in hs]
            dpt = [_dot(vj[h], doi[h]) for h in hs]
            dst = [pt[h] * (dpt[h] - dl_ref[h, :, cols]) for h in hs]
            ptb = [p.astype(BF16) for p in pt]
            dsb = [d.astype(BF16) for d in dst]
            for h in hs:
                dv_s[h * Dv:(h + 1) * Dv, :] += _dot(doi[h], ptb[h], NT)
            for h in hs:
                dk_s[h * Dk:(h + 1) * Dk, :] += _dot(qi[h], dsb[h], NT)
            for h in hs:
                dq_ref[h * Dk:(h + 1) * Dk, cols] += _dot(kt[h], dsb[h]) * scale
            if forget:
                for h in hs:
                    part = dst[h][:, 0:LANES]
                    for c0 in range(LANES, T, LANES):
                        part = part + dst[h][:, c0:c0 + LANES]
                    df_s[h] += part

        tile(j, True)

        def loop_body(i, carry):
            tile(i, False)
            return carry

        lax.fori_loop(j + 1, nq, loop_body, 0)
        dk_ref[...] = dk_s[...] * scale
        dv_ref[...] = dv_s[...]
        if forget:
            df_ref[...] = jnp.concatenate([-jnp.sum(df_s[h], axis=-1, keepdims=True) for h in hs], axis=1)

    res = lambda D, b0: pl.BlockSpec((H * D, S), lambda j: (b0, 0))
    blk = lambda D, b0: pl.BlockSpec((H * D, T), lambda j: (b0, j))
    row3 = lambda b0: pl.BlockSpec((H, 1, S), lambda j: (b0, 0, 0))
    in_specs = [res(Dk, qb), blk(Dk, kb), blk(Dv, vb), res(Dv, 0), row3(0), row3(db)]
    ins = [q_src, k_src, v_src, do, lse, delta]
    out_specs = [res(Dk, 0), blk(Dk, 0), blk(Dv, 0)]
    out_shape = [jax.ShapeDtypeStruct((H * Dk, S), F32), jax.ShapeDtypeStruct((H * Dk, S), F32),
                 jax.ShapeDtypeStruct((H * Dv, S), F32)]
    scratch = [pltpu.VMEM((H * Dk, T), F32), pltpu.VMEM((H * Dv, T), F32)]
    if forget:
        in_specs += [row3(0), pl.BlockSpec((S, H), lambda j: (0, 0))]
        ins += [f_row, f_col]
        out_specs.append(pl.BlockSpec((T, H), lambda j: (j, 0)))
        out_shape.append(jax.ShapeDtypeStruct((S, H), F32))
        scratch.append(pltpu.VMEM((H, T, min(T, LANES)), F32))
    return pl.pallas_call(
        body, name=name, grid=(nq,),
        in_specs=in_specs, out_specs=out_specs, out_shape=out_shape, scratch_shapes=scratch,
        compiler_params=_params("arbitrary"),
    )(*ins)


GW = SWA_GROUP * WINDOW


def _swa_masks(i):
    r = lax.broadcasted_iota(jnp.int32, (WINDOW, GW), 0)
    c = lax.broadcasted_iota(jnp.int32, (WINDOW, GW), 1) % WINDOW
    return (r > c) & (i > 0), r <= c


def _swa_specs():
    W = WINDOW
    kv_rows = SWA_KV_HEADS * HEAD_DIM
    q = pl.BlockSpec((SWA_Q_HEADS * HEAD_DIM, W), lambda i: (0, i))
    prev = lambda b: pl.BlockSpec((kv_rows, W), lambda i: (b, jnp.maximum(i - 1, 0)))
    cur = lambda b: pl.BlockSpec((kv_rows, W), lambda i: (b, i))
    bias = pl.BlockSpec((SWA_KV_HEADS, 2 * W, GW), lambda i: (0, 0, 0))
    stat = pl.BlockSpec((SWA_Q_HEADS, W), lambda i: (0, i))
    sink = pl.BlockSpec(memory_space=pltpu.SMEM)
    return q, prev(4), cur(4), prev(5), cur(5), bias, stat, sink


def _group_lanes(ref, g, rows_per_head):
    h0 = g * SWA_GROUP
    return jnp.concatenate([ref[(h0 + j) * rows_per_head:(h0 + j + 1) * rows_per_head, :] for j in range(SWA_GROUP)], axis=1)


def _swa_scores(g, q_ref, kp_ref, kc_ref, b_ref, masks):
    rows = slice(g * HEAD_DIM, (g + 1) * HEAD_DIM)
    qg = _group_lanes(q_ref, g, HEAD_DIM)
    scale = HEAD_DIM ** -0.5
    s_p = jnp.where(masks[0], _dot(kp_ref[rows, :], qg, TN) * scale + b_ref[g, 0:WINDOW, :], NEG_INF)
    s_c = jnp.where(masks[1], _dot(kc_ref[rows, :], qg, TN) * scale + b_ref[g, WINDOW:2 * WINDOW, :], NEG_INF)
    return qg, rows, s_p, s_c


def _sink_row(sink_ref, g):
    return jnp.concatenate([jnp.full((1, WINDOW), sink_ref[g * SWA_GROUP + j], F32) for j in range(SWA_GROUP)], axis=1)


def _swa_fwd(qkv, bias_g, sinks, *, name):
    S = qkv.shape[1]
    qs, kp, kc, vp, vc, bs, stat, sk = _swa_specs()
    gs = range(SWA_KV_HEADS)

    def body(sink_ref, q_ref, kp_ref, kc_ref, vp_ref, vc_ref, b_ref, o_ref, lse_ref):
        masks = _swa_masks(pl.program_id(0))
        sc = [_swa_scores(g, q_ref, kp_ref, kc_ref, b_ref, masks) for g in gs]
        sinks_g = [_sink_row(sink_ref, g) for g in gs]
        m = [jnp.maximum(jnp.maximum(jnp.max(sc[g][2], axis=0, keepdims=True), jnp.max(sc[g][3], axis=0, keepdims=True)),
                         sinks_g[g]) for g in gs]
        p_p = [jnp.exp(sc[g][2] - m[g]) for g in gs]
        p_c = [jnp.exp(sc[g][3] - m[g]) for g in gs]
        l = [jnp.sum(p_p[g], axis=0, keepdims=True) + jnp.sum(p_c[g], axis=0, keepdims=True) + jnp.exp(sinks_g[g] - m[g])
             for g in gs]
        o = [_dot(vp_ref[sc[g][1], :], p_p[g].astype(BF16)) + _dot(vc_ref[sc[g][1], :], p_c[g].astype(BF16)) for g in gs]
        for g in gs:
            og = o[g] / l[g]
            lse = m[g] + jnp.log(l[g])
            for j in range(SWA_GROUP):
                h = g * SWA_GROUP + j
                o_ref[h * HEAD_DIM:(h + 1) * HEAD_DIM, :] = og[:, j * WINDOW:(j + 1) * WINDOW]
                lse_ref[h:h + 1, :] = lse[:, j * WINDOW:(j + 1) * WINDOW]

    return pl.pallas_call(
        body, name=name, grid=(S // WINDOW,),
        in_specs=[sk, qs, kp, kc, vp, vc, bs],
        out_specs=[qs, stat],
        out_shape=[jax.ShapeDtypeStruct((SWA_Q_HEADS * HEAD_DIM, S), F32), jax.ShapeDtypeStruct((SWA_Q_HEADS, S), F32)],
        compiler_params=_params("parallel"),
    )(sinks, qkv, qkv, qkv, qkv, qkv, bias_g)


def _swa_bwd(qkv, bias_g, sinks, do, lse, delta, *, name):
    S = qkv.shape[1]
    W = WINDOW
    qs, kp, kc, vp, vc, bs, stat, sk = _swa_specs()
    scale = HEAD_DIM ** -0.5
    kv_rows = SWA_KV_HEADS * HEAD_DIM
    gs = range(SWA_KV_HEADS)

    def body(sink_ref, q_ref, kp_ref, kc_ref, vp_ref, vc_ref, b_ref, do_ref, lse_ref, dl_ref,
             dq_ref, dkv_ref, db_ref, dsk_ref):
        i = pl.program_id(0)

        @pl.when(i == 0)
        def _():
            dkv_ref[...] = jnp.zeros_like(dkv_ref)
            db_ref[...] = jnp.zeros_like(db_ref)
            dsk_ref[...] = jnp.zeros_like(dsk_ref)

        masks = _swa_masks(i)
        prev = pl.ds(pl.multiple_of(jnp.maximum(i - 1, 0) * W, W), W)
        cur = pl.ds(pl.multiple_of(i * W, W), W)
        sc = [_swa_scores(g, q_ref, kp_ref, kc_ref, b_ref, masks) for g in gs]
        dog = [_group_lanes(do_ref, g, HEAD_DIM) for g in gs]
        lse = [_group_lanes(lse_ref, g, 1) for g in gs]
        dl = [_group_lanes(dl_ref, g, 1) for g in gs]
        p_p = [jnp.exp(sc[g][2] - lse[g]) for g in gs]
        p_c = [jnp.exp(sc[g][3] - lse[g]) for g in gs]
        ds_p = [p_p[g] * (_dot(vp_ref[sc[g][1], :], dog[g], TN) - dl[g]) for g in gs]
        ds_c = [p_c[g] * (_dot(vc_ref[sc[g][1], :], dog[g], TN) - dl[g]) for g in gs]
        for g in gs:
            db_ref[g, 0:W, :] += ds_p[g]
            db_ref[g, W:2 * W, :] += ds_c[g]
            dsk = jnp.exp(_sink_row(sink_ref, g) - lse[g]) * dl[g]
            for j in range(SWA_GROUP):
                h = g * SWA_GROUP + j
                dsk_ref[h:h + 1, :] -= jnp.broadcast_to(jnp.sum(dsk[:, j * W:(j + 1) * W], axis=1, keepdims=True), (1, LANES))
        dsb_p = [d.astype(BF16) for d in ds_p]
        dsb_c = [d.astype(BF16) for d in ds_c]
        for g in gs:
            rows = sc[g][1]
            dq = (_dot(kp_ref[rows, :], dsb_p[g]) + _dot(kc_ref[rows, :], dsb_c[g])) * scale
            for j in range(SWA_GROUP):
                h = g * SWA_GROUP + j
                dq_ref[h * HEAD_DIM:(h + 1) * HEAD_DIM, :] = dq[:, j * W:(j + 1) * W]
        for g in gs:
            rows = sc[g][1]
            vrows = slice(kv_rows + rows.start, kv_rows + rows.stop)
            dkv_ref[rows, prev] += _dot(sc[g][0], dsb_p[g], NT) * scale
            dkv_ref[rows, cur] += _dot(sc[g][0], dsb_c[g], NT) * scale
            dkv_ref[vrows, prev] += _dot(dog[g], p_p[g].astype(BF16), NT)
            dkv_ref[vrows, cur] += _dot(dog[g], p_c[g].astype(BF16), NT)

    return pl.pallas_call(
        body, name=name, grid=(S // W,),
        in_specs=[sk, qs, kp, kc, vp, vc, bs, qs, stat, stat],
        out_specs=[qs, pl.BlockSpec((2 * kv_rows, S), lambda i: (0, 0)), bs, pl.BlockSpec((SWA_Q_HEADS, LANES), lambda i: (0, 0))],
        out_shape=[jax.ShapeDtypeStruct((SWA_Q_HEADS * HEAD_DIM, S), F32), jax.ShapeDtypeStruct((2 * kv_rows, S), F32),
                   jax.ShapeDtypeStruct((SWA_KV_HEADS, 2 * W, GW), F32), jax.ShapeDtypeStruct((SWA_Q_HEADS, LANES), F32)],
        compiler_params=_params("arbitrary"),
    )(sinks, qkv, qkv, qkv, qkv, qkv, bias_g, do, lse, delta)


def _rel_onehot_t():
    qi = jnp.arange(WINDOW, dtype=jnp.int32)[None, :] + WINDOW
    kj = jnp.arange(2 * WINDOW, dtype=jnp.int32)[:, None]
    dist = qi - kj
    max_exact = REL_BUCKETS // 2
    d = jnp.maximum(dist, 0)
    log_ratio = jnp.log(jnp.maximum(d, 1).astype(F32) / max_exact) / math.log(REL_MAX_DIST / max_exact)
    large = jnp.minimum(max_exact + (log_ratio * (REL_BUCKETS - max_exact)).astype(jnp.int32), REL_BUCKETS - 1)
    bucket = jnp.where(d < max_exact, d, large).reshape(-1)
    return (bucket[None, :] == jnp.arange(REL_BUCKETS, dtype=jnp.int32)[:, None]).astype(BF16)


def _bias_table(rel_bias_t, onehot_t):
    Hq, NB = rel_bias_t.shape
    N = onehot_t.shape[1]
    tn = _tile(N, 4096)

    def body(r_ref, oh_ref, o_ref):
        oh = oh_ref[...]
        a1, a2, a3 = _split3(r_ref[...])
        o_ref[...] = _dot(a1, oh) + _dot(a2, oh) + _dot(a3, oh)

    return pl.pallas_call(
        body, name="rel_bias_table", grid=(N // tn,),
        in_specs=[pl.BlockSpec((Hq, NB), lambda j: (0, 0)), pl.BlockSpec((NB, tn), lambda j: (0, j))],
        out_specs=pl.BlockSpec((Hq, tn), lambda j: (0, j)),
        out_shape=jax.ShapeDtypeStruct((Hq, N), F32),
        compiler_params=_params("parallel"),
    )(rel_bias_t, onehot_t)


def _bias_table_bwd(dbias, onehot_t):
    L, Hq, N = dbias.shape
    NB = onehot_t.shape[0]
    tn = _tile(N, 4096)

    def body(d_ref, oh_ref, o_ref):
        @pl.when(pl.program_id(0) == 0)
        def _():
            o_ref[...] = jnp.zeros_like(o_ref)

        d = d_ref[0]
        for l in range(1, L):
            d = d + d_ref[l]
        oh = oh_ref[...]
        a1, a2, a3 = _split3(d)
        o_ref[...] += _dot(a1, oh, NT) + _dot(a2, oh, NT) + _dot(a3, oh, NT)

    return pl.pallas_call(
        body, name="rel_bias_bwd", grid=(N // tn,),
        in_specs=[pl.BlockSpec((L, Hq, tn), lambda j: (0, 0, j)), pl.BlockSpec((NB, tn), lambda j: (0, j))],
        out_specs=pl.BlockSpec((Hq, NB), lambda j: (0, 0)),
        out_shape=jax.ShapeDtypeStruct((Hq, NB), F32),
        compiler_params=_params("arbitrary"),
    )(dbias, onehot_t)


def _gate_fwd(lat, fb_col, *, name):
    S = lat.shape[1]
    tn = _tile(S, 256)

    def body(z_ref, fb_ref, o_ref, carry):
        @pl.when(pl.program_id(0) == 0)
        def _():
            carry[...] = jnp.zeros_like(carry)

        z = z_ref[...] + fb_ref[...]
        lf = jnp.minimum(z, 0.0) - jnp.log1p(jnp.exp(-jnp.abs(z)))
        r = lax.broadcasted_iota(jnp.int32, (tn, tn), 0)
        c = lax.broadcasted_iota(jnp.int32, (tn, tn), 1)
        tri = (r <= c).astype(BF16)
        a1, a2, a3 = _split3(lf)
        cum = _dot(a1, tri) + _dot(a2, tri) + _dot(a3, tri) + carry[:, 0:1]
        o_ref[...] = cum
        carry[...] = jnp.broadcast_to(cum[:, tn - 1:tn], carry.shape)

    return pl.pallas_call(
        body, name=name, grid=(S // tn,),
        in_specs=[pl.BlockSpec((GATE_ROWS, tn), lambda i: (0, i)), pl.BlockSpec((GATE_ROWS, 1), lambda i: (0, 0))],
        out_specs=pl.BlockSpec((GATE_ROWS, tn), lambda i: (0, i)),
        out_shape=jax.ShapeDtypeStruct((GATE_ROWS, S), F32),
        scratch_shapes=[pltpu.VMEM((GATE_ROWS, LANES), F32)],
        compiler_params=_params("arbitrary"),
    )(lat, fb_col)


def _gate_bwd(lat, fb_col, dF, *, name):
    S = lat.shape[1]
    tn = _tile(S, 256)
    nt = S // tn

    def body(z_ref, fb_ref, df_ref, dz_ref, dfb_ref, carry):
        @pl.when(pl.program_id(0) == 0)
        def _():
            carry[...] = jnp.zeros_like(carry)
            dfb_ref[...] = jnp.zeros_like(dfb_ref)

        r = lax.broadcasted_iota(jnp.int32, (tn, tn), 0)
        c = lax.broadcasted_iota(jnp.int32, (tn, tn), 1)
        tri = (r >= c).astype(BF16)
        a1, a2, a3 = _split3(df_ref[...])
        dlf = _dot(a1, tri) + _dot(a2, tri) + _dot(a3, tri) + carry[:, 0:1]
        carry[...] = jnp.broadcast_to(dlf[:, 0:1], carry.shape)
        z = z_ref[...] + fb_ref[...]
        row = lax.broadcasted_iota(jnp.int32, (GATE_ROWS, tn), 0)
        dz = jnp.where(row < FOX_HEADS, dlf / (1.0 + jnp.exp(z)), 0.0)
        dz_ref[...] = dz
        dfb_ref[...] += jnp.sum(dz, axis=1, keepdims=True)

    blk = pl.BlockSpec((GATE_ROWS, tn), lambda i: (0, nt - 1 - i))
    vec = pl.BlockSpec((GATE_ROWS, 1), lambda i: (0, 0))
    return pl.pallas_call(
        body, name=name, grid=(nt,),
        in_specs=[blk, vec, blk], out_specs=[blk, vec],
        out_shape=[jax.ShapeDtypeStruct((GATE_ROWS, S), F32), jax.ShapeDtypeStruct((GATE_ROWS, 1), F32)],
        scratch_shapes=[pltpu.VMEM((GATE_ROWS, LANES), F32)],
        compiler_params=_params("arbitrary"),
    )(lat, fb_col, dF)


def _rope_tables(S):
    pos = jnp.arange(S, dtype=F32)
    inv_freq = ROPE_THETA ** (-(jnp.arange(MLA_ROPE // 2, dtype=F32) * 2.0 / MLA_ROPE))
    ang = pos[:, None] * inv_freq[None, :]
    cos, sin = jnp.cos(ang).T, jnp.sin(ang).T
    z16 = jnp.zeros_like(cos)

    def slab(lo, fill):
        def put(first, second, f):
            return jnp.concatenate([jnp.full((lo, S), f, F32), first, second, jnp.full((LANES - lo - MLA_ROPE, S), f, F32)], axis=0)
        return put(cos, cos, fill), put(-sin, z16, 0.0), put(z16, sin, 0.0)

    tq = tuple(jnp.tile(t, (MLA_HEADS, 1)) for t in slab(MLA_NOPE, 1.0))
    return tq, slab(0, 0.0)


def _rope(x, c, s1, s2):
    n = x.shape[0]
    half = MLA_ROPE // 2
    return x * c + pltpu.roll(x, n - half, 0) * s1 + pltpu.roll(x, half, 0) * s2


def _rope_t(dy, c, s1, s2):
    n = dy.shape[0]
    half = MLA_ROPE // 2
    return dy * c + pltpu.roll(dy * s1, half, 0) + pltpu.roll(dy * s2, n - half, 0)


KR_SLAB0 = MLA_Q_RANK + MLA_KV_RANK


def _mla_prep_fwd(lat, g_q, g_kv, w_uq_t, w_ukv_t, tq, tmisc, *, name):
    S = lat.shape[1]
    tn = _tile(S, 512)
    QW = MLA_HEADS * MLA_PAD

    def body(lat_ref, gq_ref, gkv_ref, wq_ref, wkv_ref, c_ref, s1_ref, s2_ref, cm_ref, s1m_ref, s2m_ref,
             nq_ref, nkv_ref, q_ref, k_ref, v_ref):
        x = pltpu.roll(lat_ref[...], LAT_ROWS - LAT_SHIFT, 0)
        nq = _col_rms(x[0:MLA_Q_RANK, :], gq_ref[...]).astype(BF16)
        nkv = _col_rms(x[MLA_Q_RANK:KR_SLAB0, :], gkv_ref[...]).astype(BF16)
        nq_ref[...] = nq
        nkv_ref[...] = nkv
        q_ref[...] = _rope(_dot(wq_ref[...], nq), c_ref[...], s1_ref[...], s2_ref[...]).astype(BF16)
        kv = _dot(wkv_ref[...], nkv).astype(BF16)
        kr = _rope(x[KR_SLAB0:LAT_ROWS, :], cm_ref[...], s1m_ref[...], s2m_ref[...]).astype(BF16)
        for h in range(MLA_HEADS):
            k_ref[h * MLA_PAD:h * MLA_PAD + MLA_NOPE, :] = kv[h * LANES:h * LANES + MLA_NOPE, :]
            k_ref[h * MLA_PAD + MLA_NOPE:(h + 1) * MLA_PAD, :] = kr[0:MLA_PAD - MLA_NOPE, :]
            v_ref[h * HEAD_DIM:(h + 1) * HEAD_DIM, :] = kv[h * LANES + MLA_NOPE:(h + 1) * LANES, :]

    def col(rows):
        return pl.BlockSpec((rows, tn), lambda i: (0, i))

    def full(a):
        return pl.BlockSpec(a.shape, lambda i: (0, 0))

    return pl.pallas_call(
        body, name=name, grid=(S // tn,),
        in_specs=[col(LAT_ROWS), full(g_q), full(g_kv), full(w_uq_t), full(w_ukv_t),
                  col(QW), col(QW), col(QW), col(LANES), col(LANES), col(LANES)],
        out_specs=[col(MLA_Q_RANK), col(MLA_KV_RANK), col(QW), col(QW), col(MLA_HEADS * HEAD_DIM)],
        out_shape=[jax.ShapeDtypeStruct((MLA_Q_RANK, S), BF16), jax.ShapeDtypeStruct((MLA_KV_RANK, S), BF16),
                   jax.ShapeDtypeStruct((QW, S), BF16), jax.ShapeDtypeStruct((QW, S), BF16),
                   jax.ShapeDtypeStruct((MLA_HEADS * HEAD_DIM, S), BF16)],
        compiler_params=_params("parallel"),
    )(lat, g_q, g_kv, w_uq_t, w_ukv_t, *tq, *tmisc)


def _mla_prep_bwd(lat, nq, nkv, g_q, g_kv, w_uq_p, w_ukv, tq, tmisc, dq, dk, dv, dflog, *, name):
    S = lat.shape[1]
    tn = _tile(S, 512)
    QW = MLA_HEADS * MLA_PAD

    def body(lat_ref, nq_ref, nkv_ref, gq_ref, gkv_ref, wq_ref, wkv_ref, c_ref, s1_ref, s2_ref,
             cm_ref, s1m_ref, s2m_ref, dq_ref, dk_ref, dv_ref, dfl_ref,
             dlat_ref, dwq_ref, dwkv_ref, dgq_ref, dgkv_ref, y_s):
        @pl.when(pl.program_id(0) == 0)
        def _():
            dwq_ref[...] = jnp.zeros_like(dwq_ref)
            dwkv_ref[...] = jnp.zeros_like(dwkv_ref)
            dgq_ref[...] = jnp.zeros_like(dgq_ref)
            dgkv_ref[...] = jnp.zeros_like(dgkv_ref)

        x = pltpu.roll(lat_ref[...], LAT_ROWS - LAT_SHIFT, 0)
        dqm = _rope_t(dq_ref[...], c_ref[...], s1_ref[...], s2_ref[...]).astype(BF16)
        dwq_ref[...] += _dot(dqm, nq_ref[...], NT)
        dx, dg = _col_rms_bwd(x[0:MLA_Q_RANK, :], gq_ref[...], _dot(wq_ref[...], dqm))
        y_s[0:MLA_Q_RANK, :] = dx
        dgq_ref[...] += dg
        dkv = jnp.concatenate(
            [part for h in range(MLA_HEADS)
             for part in (dk_ref[h * MLA_PAD:h * MLA_PAD + MLA_NOPE, :], dv_ref[h * HEAD_DIM:(h + 1) * HEAD_DIM, :])],
            axis=0).astype(BF16)
        dwkv_ref[...] += _dot(dkv, nkv_ref[...], NT)
        dx, dg = _col_rms_bwd(x[MLA_Q_RANK:KR_SLAB0, :], gkv_ref[...], _dot(wkv_ref[...], dkv))
        y_s[MLA_Q_RANK:KR_SLAB0, :] = dx
        dgkv_ref[...] += dg
        dkr = dk_ref[MLA_NOPE:MLA_PAD, :]
        for h in range(1, MLA_HEADS):
            dkr = dkr + dk_ref[h * MLA_PAD + MLA_NOPE:(h + 1) * MLA_PAD, :]
        dkr = jnp.concatenate([dkr, jnp.zeros((MLA_NOPE, tn), F32)], axis=0)
        y_s[KR_SLAB0:LAT_ROWS, :] = _rope_t(dkr, cm_ref[...], s1m_ref[...], s2m_ref[...])
        y = pltpu.roll(y_s[...], LAT_SHIFT, 0)
        row = lax.broadcasted_iota(jnp.int32, (LAT_ROWS, tn), 0)
        dfl = jnp.concatenate([dfl_ref[...], jnp.zeros((LAT_ROWS - GATE_ROWS, tn), F32)], axis=0)
        dlat_ref[...] = jnp.where(row < LAT_SHIFT, dfl, y).astype(BF16)

    def col(rows):
        return pl.BlockSpec((rows, tn), lambda i: (0, i))

    def full(a):
        return pl.BlockSpec(a.shape, lambda i: (0, 0))

    def acc(r, c):
        return pl.BlockSpec((r, c), lambda i: (0, 0))

    return pl.pallas_call(
        body, name=name, grid=(S // tn,),
        in_specs=[col(LAT_ROWS), col(MLA_Q_RANK), col(MLA_KV_RANK), full(g_q), full(g_kv),
                  full(w_uq_p), full(w_ukv), col(QW), col(QW), col(QW), col(LANES), col(LANES), col(LANES),
                  col(QW), col(QW), col(MLA_HEADS * HEAD_DIM), col(GATE_ROWS)],
        out_specs=[col(LAT_ROWS), acc(QW, MLA_Q_RANK), acc(QW, MLA_KV_RANK), acc(MLA_Q_RANK, 1), acc(MLA_KV_RANK, 1)],
        out_shape=[jax.ShapeDtypeStruct((LAT_ROWS, S), BF16), jax.ShapeDtypeStruct((QW, MLA_Q_RANK), F32),
                   jax.ShapeDtypeStruct((QW, MLA_KV_RANK), F32), jax.ShapeDtypeStruct((MLA_Q_RANK, 1), F32),
                   jax.ShapeDtypeStruct((MLA_KV_RANK, 1), F32)],
        scratch_shapes=[pltpu.VMEM((LAT_ROWS, tn), F32)],
        compiler_params=_params("arbitrary"),
    )(lat, nq, nkv, g_q, g_kv, w_uq_p, w_ukv, *tq, *tmisc, dq, dk, dv, dflog)


def _dproj_cast(dqa, dkva, dqf, dkf, dvf, dlat, *, name):
    S = dqa.shape[1]
    tn = _tile(S, 512)
    parts = (dqa, dkva, dqf, dkf, dvf, dlat)

    def body(*refs):
        o_ref = refs[-1]
        r0 = 0
        for ref in refs[:-1]:
            n = ref.shape[0]
            o_ref[r0:r0 + n, :] = ref[...].astype(BF16)
            r0 += n

    return pl.pallas_call(
        body, name=name, grid=(S // tn,),
        in_specs=[pl.BlockSpec((p.shape[0], tn), lambda i: (0, i)) for p in parts],
        out_specs=pl.BlockSpec((IN_ROWS, tn), lambda i: (0, i)),
        out_shape=jax.ShapeDtypeStruct((IN_ROWS, S), BF16),
        compiler_params=_params("parallel"),
    )(*parts)


GELU_C = math.sqrt(2.0 / math.pi)
GELU_A = 0.044715


HALO = 16


def _shift_down(a, k, fill):
    r = pltpu.roll(a, k, 0)
    row = lax.broadcasted_iota(jnp.int32, (8, a.shape[1]), 0)
    head = r[0:8, :]
    for i in range(k):
        head = jnp.where(row == i, fill[len(fill) - k + i], head)
    return jnp.concatenate([head, r[8:, :]], axis=0)


def _shift_up(d, k, fill):
    n = d.shape[0]
    r = pltpu.roll(d, n - k, 0)
    row = lax.broadcasted_iota(jnp.int32, (8, d.shape[1]), 0)
    tail = r[n - 8:n, :]
    for i in range(k):
        tail = jnp.where(row == 8 - k + i, fill[i], tail)
    return jnp.concatenate([r[0:n - 8, :], tail], axis=0)


def _conv_taps(a, before, w_ref, b_ref):
    a1 = _shift_down(a, 1, before)
    a2 = _shift_down(a, 2, before)
    return ((b_ref[...] + w_ref[0:1, :] * a2) + w_ref[1:2, :] * a1) + w_ref[2:3, :] * a


def _rows_before(halo_ref, first):
    h = halo_ref[HALO - 2:HALO, :].astype(F32)
    return jnp.where(first, 0.0, h[0:1, :]), jnp.where(first, 0.0, h[1:2, :])


def _conv_specs(S, tm, tc, nc):
    hb = tm // HALO
    main = lambda off: pl.BlockSpec((tm, tc), lambda j, i: (i, j + off))
    prev = lambda off: pl.BlockSpec((HALO, tc), lambda j, i: (jnp.maximum(i * hb - 1, 0), j + off))
    wspec = lambda off: pl.BlockSpec((3, tc), lambda j, i: (0, j + off))
    bspec = lambda off: pl.BlockSpec((1, tc), lambda j, i: (0, j + off))
    return main, prev, wspec, bspec


def _conv_geglu_fwd(a, conv_w, conv_b, *, name):
    S = a.shape[0]
    tm, tc = _tile(S, 512), _tile(D_FF, 1408)
    nc = D_FF // tc
    main, prev, wspec, bspec = _conv_specs(S, tm, tc, nc)

    def body(ag_ref, au_ref, hg_ref, hu_ref, wg_ref, wu_ref, bg_ref, bu_ref, u_ref, z_ref):
        first = pl.program_id(1) == 0
        gate = _conv_taps(ag_ref[...].astype(F32), _rows_before(hg_ref, first), wg_ref, bg_ref)
        up = _conv_taps(au_ref[...].astype(F32), _rows_before(hu_ref, first), wu_ref, bu_ref)
        u_ref[0] = gate
        u_ref[1] = up
        cdf = 0.5 * (1.0 + jnp.tanh(GELU_C * (gate + GELU_A * (gate * gate * gate))))
        z_ref[...] = (gate * cdf * up).astype(BF16)

    return pl.pallas_call(
        body, name=name, grid=(nc, S // tm),
        in_specs=[main(0), main(nc), prev(0), prev(nc), wspec(0), wspec(nc), bspec(0), bspec(nc)],
        out_specs=[pl.BlockSpec((2, tm, tc), lambda j, i: (0, i, j)), pl.BlockSpec((tm, tc), lambda j, i: (i, j))],
        out_shape=[jax.ShapeDtypeStruct((2, S, D_FF), F32), jax.ShapeDtypeStruct((S, D_FF), BF16)],
        compiler_params=_params("parallel", "arbitrary"),
    )(a, a, a, a, conv_w, conv_w, conv_b, conv_b)


def _geglu_bwd(gate, up, dz):
    g2x = gate * gate
    th = jnp.tanh(GELU_C * (gate + GELU_A * (g2x * gate)))
    cdf = 0.5 * (1.0 + th)
    dgelu = cdf + gate * (0.5 * (1.0 - th * th) * (GELU_C * (1.0 + 3.0 * GELU_A * g2x)))
    return dz * up * dgelu, dz * (gate * cdf)


def _conv_geglu_bwd(a, u, conv_w, dz, *, name):
    S = a.shape[0]
    tm, tc = _tile(S, 512), _tile(D_FF, 1408)
    nc = D_FF // tc
    nr = S // tm
    main, _, wspec, _ = _conv_specs(S, tm, tc, nc)
    hb = tm // 8

    def body(ag_ref, au_ref, u_ref, un_ref, wg_ref, wu_ref, dz_ref, dzn_ref, da_ref, dw_ref, db_ref):
        i = pl.program_id(1)
        last = i == nr - 1

        @pl.when(i == 0)
        def _():
            dw_ref[...] = jnp.zeros_like(dw_ref)
            db_ref[...] = jnp.zeros_like(db_ref)

        dus = _geglu_bwd(u_ref[0], u_ref[1], dz_ref[...])
        dus_n = _geglu_bwd(un_ref[0], un_ref[1], dzn_ref[...])
        for half, a_ref, w_ref in ((0, ag_ref, wg_ref), (1, au_ref, wu_ref)):
            du, du_n = dus[half], dus_n[half]
            after = (jnp.where(last, 0.0, du_n[0:1, :]), jnp.where(last, 0.0, du_n[1:2, :]))
            shifted = (_shift_up(du, 2, after), _shift_up(du, 1, after), du)
            da_ref[half] = (w_ref[2:3, :] * du + w_ref[1:2, :] * shifted[1] + w_ref[0:1, :] * shifted[0]).astype(BF16)
            af = a_ref[...].astype(F32)
            for tap in range(3):
                dw_ref[half, tap:tap + 1, :] += jnp.sum(shifted[tap] * af, axis=0, keepdims=True)
            db_ref[half] += jnp.sum(du, axis=0, keepdims=True)

    nxt8 = lambda j, i: (0, jnp.minimum((i + 1) * hb, S // 8 - 1), j)
    return pl.pallas_call(
        body, name=name, grid=(nc, nr),
        in_specs=[main(0), main(nc), pl.BlockSpec((2, tm, tc), lambda j, i: (0, i, j)), pl.BlockSpec((2, 8, tc), nxt8),
                  wspec(0), wspec(nc), pl.BlockSpec((tm, tc), lambda j, i: (i, j)),
                  pl.BlockSpec((8, tc), lambda j, i: (jnp.minimum((i + 1) * hb, S // 8 - 1), j))],
        out_specs=[pl.BlockSpec((2, tm, tc), lambda j, i: (0, i, j)), pl.BlockSpec((2, 3, tc), lambda j, i: (0, 0, j)),
                   pl.BlockSpec((2, 1, tc), lambda j, i: (0, 0, j))],
        out_shape=[jax.ShapeDtypeStruct((2, S, D_FF), BF16), jax.ShapeDtypeStruct((2, 3, D_FF), F32),
                   jax.ShapeDtypeStruct((2, 1, D_FF), F32)],
        compiler_params=_params("parallel", "arbitrary"),
    )(a, a, u, u, conv_w, conv_w, dz, dz)


def _adamw_update(w, g, m, v):
    m = ADAM_B1 * m + (1.0 - ADAM_B1) * g
    v = ADAM_B2 * v + (1.0 - ADAM_B2) * jnp.square(g)
    m_hat = m / (1.0 - ADAM_B1 ** ADAM_STEP)
    v_hat = v / (1.0 - ADAM_B2 ** ADAM_STEP)
    return -ADAM_LR * (m_hat / (jnp.sqrt(v_hat) + ADAM_EPS) + ADAM_WD * w), m, v


def _adamw(w, g, m, v, *, name):
    L, A, B = w.shape
    ta = _tile(A, ROW_TILE)

    def body(w_ref, g_ref, m_ref, v_ref, d_ref, mo_ref, vo_ref):
        d_ref[...], mo_ref[...], vo_ref[...] = _adamw_update(w_ref[...], g_ref[...], m_ref[...], v_ref[...])

    blk = pl.BlockSpec((None, ta, B), lambda l, i: (l, i, 0))
    shp = jax.ShapeDtypeStruct((L, A, B), F32)
    return pl.pallas_call(
        body, name=name, grid=(L, A // ta),
        in_specs=[blk] * 4, out_specs=[blk] * 3, out_shape=[shp] * 3,
        compiler_params=_params("parallel", "parallel"),
    )(w, g, m, v)


def _scalar(v):
    return jnp.reshape(v, (1,)).astype(jnp.int32)


def _adamw_halves(w, g_mine, g_other, m, v, *, name):
    L, A, B = w.shape
    ta = _tile(A // 2, ROW_TILE)
    nb = A // 2 // ta

    def body(c_ref, w_ref, gm_ref, go_ref, m_ref, v_ref, g_ref, d_ref, mo_ref, vo_ref):
        g = jnp.where(pl.program_id(1) // nb == c_ref[0], gm_ref[...], go_ref[...])
        g_ref[...] = g
        d_ref[...], mo_ref[...], vo_ref[...] = _adamw_update(w_ref[...], g, m_ref[...], v_ref[...])

    blk = pl.BlockSpec((None, ta, B), lambda l, i, c_ref: (l, i, 0))
    half = pl.BlockSpec((None, ta, B), lambda l, i, c_ref: (l, i % nb, 0))
    shp = jax.ShapeDtypeStruct((L, A, B), F32)
    return pl.pallas_call(
        body, name=name,
        grid_spec=pltpu.PrefetchScalarGridSpec(num_scalar_prefetch=1, grid=(L, A // ta),
                                               in_specs=[blk, half, half, blk, blk], out_specs=[blk] * 4),
        out_shape=[shp] * 4,
        compiler_params=_params("parallel", "parallel"),
    )(_scalar(lax.axis_index("c")), w, g_mine, g_other, m, v)


def _chip_index():
    return 2 * lax.axis_index("x") + lax.axis_index("y")


def _pair_sum(g, recv, *, name):
    n, A, B = g.shape
    ta = _tile(A // 2, ROW_TILE)
    nb = A // 2 // ta

    def body(c_ref, g_ref, r_ref, o_ref):
        o_ref[...] = g_ref[...] + r_ref[...]

    return pl.pallas_call(
        body, name=name,
        grid_spec=pltpu.PrefetchScalarGridSpec(
            num_scalar_prefetch=1, grid=(n, nb),
            in_specs=[pl.BlockSpec((None, ta, B), lambda s, r, c_ref: (s, c_ref[0] * nb + r, 0)),
                      pl.BlockSpec((None, ta, B), lambda s, r, c_ref: (s, r, 0))],
            out_specs=pl.BlockSpec((None, ta, B), lambda s, r, c_ref: (s, r, 0))),
        out_shape=jax.ShapeDtypeStruct((n, A // 2, B), F32),
        compiler_params=_params("parallel", "parallel"),
    )(_scalar(lax.axis_index("c")), g, recv)


def _chip_sum(landed, own, *, name):
    n, A2, B = landed.shape
    ta = _tile(A2, ROW_TILE)

    def body(me_ref, *refs):
        slots, own_ref, o_ref = refs[:n], refs[n], refs[n + 1]
        parts = [jnp.where(me_ref[0] == s, own_ref[...], slots[s][...]) for s in range(n)]
        o_ref[...] = ((parts[0] + parts[1]) + parts[2]) + parts[3]

    def slot(s):
        return pl.BlockSpec((None, ta, B), lambda r, me_ref: (jnp.where(me_ref[0] == s, (s + 1) % n, s), r, 0))

    return pl.pallas_call(
        body, name=name,
        grid_spec=pltpu.PrefetchScalarGridSpec(
            num_scalar_prefetch=1, grid=(A2 // ta,),
            in_specs=[slot(s) for s in range(n)] + [pl.BlockSpec((None, ta, B), lambda r, me_ref: (me_ref[0], r, 0))],
            out_specs=pl.BlockSpec((ta, B), lambda r, me_ref: (r, 0))),
        out_shape=jax.ShapeDtypeStruct((A2, B), F32),
        compiler_params=_params("parallel"),
    )(_scalar(_chip_index()), *([landed] * n), own)


HBM_SPEC = pl.BlockSpec(memory_space=pl.ANY)
COMM_PARAMS = pltpu.CompilerParams(has_side_effects=True)


def _mesh_pos():
    return lax.axis_index("x"), lax.axis_index("y"), lax.axis_index("c")


def _other_chips(x, y):
    return [(1 - x, y), (x, 1 - y), (1 - x, 1 - y)]


def _remote(src, dst, send_sems, recv_sems, k, to):
    return pltpu.make_async_remote_copy(src_ref=src, dst_ref=dst, send_sem=send_sems.at[k], recv_sem=recv_sems.at[k],
                                        device_id=to, device_id_type=MESH)


def _place_own(gathered, shard, *, name):
    A, B = shard.shape
    ta = _tile(A, ROW_TILE)

    def body(me_ref, s_ref, g_ref, o_ref):
        o_ref[...] = s_ref[...]

    return pl.pallas_call(
        body, name=name,
        grid_spec=pltpu.PrefetchScalarGridSpec(
            num_scalar_prefetch=1, grid=(A // ta,),
            in_specs=[pl.BlockSpec((ta, B), lambda r, me_ref: (r, 0)), HBM_SPEC],
            out_specs=pl.BlockSpec((None, ta, B), lambda r, me_ref: (me_ref[0], r, 0))),
        out_shape=jax.ShapeDtypeStruct(gathered.shape, gathered.dtype),
        input_output_aliases={2: 0},
        compiler_params=_params("parallel"),
    )(_scalar(_chip_index()), shard, gathered)


def _half_rows(rows, c, align=8):
    assert (rows // 2) % align == 0
    return pl.ds(pl.multiple_of(c * (rows // 2), align), rows // 2)


BF16_ROWS = 16


def _halved(rows):
    return rows % (2 * BF16_ROWS) == 0


def _gather_copies(srcs, lands, send_sems, recv_sems):
    x, y, c = _mesh_pos()
    me = 2 * x + y
    out = []
    for k in range(len(srcs)):
        a = srcs[k].shape[0]
        rows = _half_rows(a, c, BF16_ROWS) if _halved(a) else pl.ds(0, a)
        for j, (px, py) in enumerate(_other_chips(x, y)):
            send = _remote(srcs[k].at[rows], lands[k].at[me, rows], send_sems, recv_sems, 3 * k + j, (px, py, c))
            recv = _remote(srcs[k].at[rows], lands[k].at[2 * px + py, rows], send_sems, recv_sems, 3 * k + j, (px, py, c))
            out.append((send, recv))
    return out


def _gather_start(srcs):
    nl, n = len(srcs), len(srcs[0])
    lands = [[lax.empty((N_CHIPS,) + s.shape, s.dtype) for s in sl] for sl in srcs]
    flat = [a for l in range(nl) for a in srcs[l] + lands[l]]

    def body(*refs):
        bufs, sems, token = refs[:len(flat)], refs[len(flat):len(flat) + 2 * nl], refs[-1]
        for l in range(nl):
            mine = bufs[2 * n * l:2 * n * (l + 1)]
            for send, _ in _gather_copies(mine[:n], mine[n:], sems[2 * l], sems[2 * l + 1]):
                send.start()
        token[...] = jnp.zeros_like(token)

    res = pl.pallas_call(
        body, name="weight_gather_start",
        in_specs=[HBM_ONLY] * len(flat),
        out_specs=[SEM_SPEC] * (2 * nl) + [HBM_ONLY] * len(flat) + [pl.BlockSpec(memory_space=pltpu.VMEM)],
        out_shape=[pltpu.SemaphoreType.DMA((3 * n,))] * (2 * nl) + [pltpu.HBM(a.shape, a.dtype) for a in flat]
        + [jax.ShapeDtypeStruct((8, LANES), F32)],
        input_output_aliases={i: 2 * nl + i for i in range(len(flat))},
        compiler_params=SPLIT_PARAMS,
    )(*[pltpu.with_memory_space_constraint(a, pltpu.HBM) for a in flat])
    bufs = res[2 * nl:2 * nl + len(flat)]
    state = [(res[2 * l], res[2 * l + 1], list(bufs[2 * n * l:2 * n * l + n]), list(bufs[2 * n * l + n:2 * n * (l + 1)]))
             for l in range(nl)]
    return state, res[-1][0:1, 0:1]


def _gather_wait(state, after, *, name):
    send_sems, recv_sems, srcs, lands = state
    n = len(srcs)

    def body(*refs):
        for send, recv in _gather_copies(refs[:n], refs[n:2 * n], refs[2 * n], refs[2 * n + 1]):
            send.wait_send()
            recv.wait_recv()

    res = pl.pallas_call(
        body, name=name,
        in_specs=[HBM_ONLY] * (2 * n) + [SEM_SPEC, SEM_SPEC, HBM_SPEC],
        out_specs=[HBM_ONLY] * (2 * n),
        out_shape=[pltpu.HBM(a.shape, a.dtype) for a in srcs + lands],
        input_output_aliases={i: i for i in range(2 * n)},
        compiler_params=SPLIT_PARAMS,
    )(*srcs, *lands, send_sems, recv_sems, after)
    return list(res[:n]), list(res[n:])


def _gather_forward(lands, *, name):
    n = len(lands)

    def body(*refs):
        bufs, outs = refs[:n], refs[n:2 * n]
        send_sems, recv_sems = refs[2 * n:]
        x, y, c = _mesh_pos()
        copies, waits = [], []
        for k in range(n):
            a = lands[k].shape[1]
            if not _halved(a):
                continue
            for j, (px, py) in enumerate(_other_chips(x, y)):
                mine = 2 * px + py, _half_rows(a, c, BF16_ROWS)
                copies.append(_remote(bufs[k].at[mine], outs[k].at[mine], send_sems, recv_sems, 3 * k + j, (x, y, 1 - c)))
                lands_here = outs[k].at[2 * px + py, _half_rows(a, 1 - c, BF16_ROWS)]
                waits.append(_remote(lands_here, lands_here, send_sems, recv_sems, 3 * k + j, (x, y, 1 - c)))
        for cp in copies:
            cp.start()
        for cp in waits:
            cp.wait_recv()
        for cp in copies:
            cp.wait_send()

    return pl.pallas_call(
        body, name=name,
        in_specs=[HBM_SPEC] * n, out_specs=[HBM_SPEC] * n,
        out_shape=[jax.ShapeDtypeStruct(a.shape, a.dtype) for a in lands],
        scratch_shapes=[pltpu.SemaphoreType.DMA((3 * n,)), pltpu.SemaphoreType.DMA((3 * n,))],
        input_output_aliases={i: i for i in range(n)},
        compiler_params=COMM_PARAMS,
    )(*lands)


def _sibling_exchange(gs, *, name):
    n = len(gs)

    def body(*refs):
        ins, outs = refs[:n], refs[n:2 * n]
        send_sems, recv_sems = refs[2 * n:]
        x, y, c = _mesh_pos()
        copies = [_remote(ins[k].at[:, _half_rows(gs[k].shape[1], 1 - c)], outs[k], send_sems, recv_sems, k, (x, y, 1 - c))
                  for k in range(n)]
        for cp in copies:
            cp.start()
        for cp in copies:
            cp.wait()

    return pl.pallas_call(
        body, name=name,
        in_specs=[HBM_SPEC] * n, out_specs=[HBM_SPEC] * n,
        out_shape=[jax.ShapeDtypeStruct((g.shape[0], g.shape[1] // 2, g.shape[2]), g.dtype) for g in gs],
        scratch_shapes=[pltpu.SemaphoreType.DMA((n,)), pltpu.SemaphoreType.DMA((n,))],
        compiler_params=COMM_PARAMS,
    )(*gs)


HBM_ONLY = pl.BlockSpec(memory_space=pltpu.HBM)
SEM_SPEC = pl.BlockSpec(memory_space=pltpu.SEMAPHORE)
SPLIT_PARAMS = pltpu.CompilerParams(has_side_effects=pltpu.SideEffectType.DATAFLOW_SIDE_EFFECTING)


def _scatter_copies(srcs, lands, send_sems, recv_sems):
    x, y, c = _mesh_pos()
    me = 2 * x + y
    out = []
    for k in range(len(srcs)):
        for j, (px, py) in enumerate(_other_chips(x, y)):
            s = 2 * px + py
            send = _remote(srcs[k].at[s], lands[k].at[me], send_sems, recv_sems, 3 * k + j, (px, py, c))
            recv = _remote(srcs[k].at[s], lands[k].at[s], send_sems, recv_sems, 3 * k + j, (px, py, c))
            out.append((send, recv))
    return out


def _scatter_start(ps, *, name):
    n = len(ps)
    lands = [lax.empty(p.shape, p.dtype) for p in ps]

    def body(*refs):
        srcs, zones = refs[:n], refs[n:2 * n]
        send_sems, recv_sems, token = refs[2 * n], refs[2 * n + 1], refs[-1]
        for send, _ in _scatter_copies(srcs, zones, send_sems, recv_sems):
            send.start()
        token[...] = jnp.zeros_like(token)

    hbm = lambda a: pltpu.HBM(a.shape, a.dtype)
    res = pl.pallas_call(
        body, name=name,
        in_specs=[HBM_ONLY] * (2 * n),
        out_specs=[SEM_SPEC, SEM_SPEC] + [HBM_ONLY] * (2 * n) + [pl.BlockSpec(memory_space=pltpu.VMEM)],
        out_shape=[pltpu.SemaphoreType.DMA((3 * n,)), pltpu.SemaphoreType.DMA((3 * n,))] + [hbm(a) for a in ps + lands]
        + [jax.ShapeDtypeStruct((8, LANES), F32)],
        input_output_aliases={i: 2 + i for i in range(2 * n)},
        compiler_params=SPLIT_PARAMS,
    )(*[pltpu.with_memory_space_constraint(a, pltpu.HBM) for a in ps + lands])
    return (res[0], res[1], list(res[2:2 + n]), list(res[2 + n:2 + 2 * n])), res[-1][0:1, 0:1]


def _scatter_wait(started, after, *, name):
    ng = len(started)
    sizes = [len(st[2]) for st in started]
    offs = [2 * sum(sizes[:i]) for i in range(ng + 1)]
    flat = [a for (_, _, ps, lands) in started for a in ps + lands]

    def body(*refs):
        bufs, sems = refs[:len(flat)], refs[len(flat):len(flat) + 2 * ng]
        for i, n in enumerate(sizes):
            srcs, zones = bufs[offs[i]:offs[i] + n], bufs[offs[i] + n:offs[i + 1]]
            for send, recv in _scatter_copies(srcs, zones, sems[2 * i], sems[2 * i + 1]):
                send.wait_send()
                recv.wait_recv()

    res = pl.pallas_call(
        body, name=name,
        in_specs=[HBM_ONLY] * len(flat) + [SEM_SPEC] * (2 * ng) + [HBM_SPEC],
        out_specs=[HBM_ONLY] * len(flat),
        out_shape=[pltpu.HBM(a.shape, a.dtype) for a in flat],
        input_output_aliases={i: i for i in range(len(flat))},
        compiler_params=SPLIT_PARAMS,
    )(*flat, *[s for (ss, rs, _, _) in started for s in (ss, rs)], after)
    return [(list(res[offs[i]:offs[i] + n]), list(res[offs[i] + n:offs[i + 1]])) for i, n in enumerate(sizes)]


def _sibling_share(hs):
    n = len(hs)

    def body(*refs):
        ins, outs = refs[:n], refs[n:2 * n]
        send_sems, recv_sems = refs[2 * n:]
        x, y, c = _mesh_pos()
        copies = [_remote(ins[k], outs[k], send_sems, recv_sems, k, (x, y, 1 - c)) for k in range(n)]
        for cp in copies:
            cp.start()
        for cp in copies:
            cp.wait()

    return pl.pallas_call(
        body, name="grad_sibling_share",
        in_specs=[HBM_SPEC] * n, out_specs=[HBM_SPEC] * n,
        out_shape=[jax.ShapeDtypeStruct(h.shape, h.dtype) for h in hs],
        scratch_shapes=[pltpu.SemaphoreType.DMA((n,)), pltpu.SemaphoreType.DMA((n,))],
        compiler_params=COMM_PARAMS,
    )(*hs)


def _allreduce_small(part):
    rows, C = part.shape

    def body(p_ref, o_ref, slots, send_sems, recv_sems):
        x, y, c = _mesh_pos()
        me = 4 * x + 2 * y + c
        slots[me] = p_ref[...]
        copies = []
        for k in range(1, 8):
            kx, ky, kc = (k >> 2) & 1, (k >> 1) & 1, k & 1
            peer = (x ^ kx if kx else x, y ^ ky if ky else y, c ^ kc if kc else c)
            cp = _remote(p_ref, slots.at[me], send_sems, recv_sems, k - 1, peer)
            cp.start()
            copies.append((cp, peer))
        for k, (cp, peer) in enumerate(copies):
            src = 4 * peer[0] + 2 * peer[1] + peer[2]
            _remote(p_ref, slots.at[src], send_sems, recv_sems, k, peer).wait_recv()
        for cp, _ in copies:
            cp.wait_send()
        total = slots[0]
        for d in range(1, 8):
            total = total + slots[d]
        o_ref[...] = total

    return pl.pallas_call(
        body, name="small_grad_allreduce",
        in_specs=[pl.BlockSpec(memory_space=pltpu.VMEM)], out_specs=pl.BlockSpec(memory_space=pltpu.VMEM),
        out_shape=jax.ShapeDtypeStruct((rows, C), F32),
        scratch_shapes=[pltpu.VMEM((8, rows, C), F32), pltpu.SemaphoreType.DMA((7,)), pltpu.SemaphoreType.DMA((7,))],
        compiler_params=pltpu.CompilerParams(has_side_effects=True, vmem_limit_bytes=VMEM_LIMIT_BYTES),
    )(part)


def _pad_w_uq(w):
    lead = w.shape[:-1]
    w = w.reshape(lead + (MLA_HEADS, MLA_QK))
    w = jnp.concatenate([w, jnp.zeros(lead + (MLA_HEADS, MLA_PAD - MLA_QK), w.dtype)], axis=-1)
    return w.reshape(lead + (MLA_HEADS * MLA_PAD,))


def _unpad_w_uq(g):
    lead = g.shape[:-1]
    return g.reshape(lead + (MLA_HEADS, MLA_PAD))[..., :MLA_QK].reshape(lead + (MLA_HEADS * MLA_QK,))


def _t(a):
    return jnp.swapaxes(a, -1, -2)


def _shards_of_cols(w):
    A, NB = w.shape
    return w.reshape(A, N_CHIPS, NB // N_CHIPS).transpose(1, 0, 2)


BIG = ("w_in", "w_uq", "w_ukv", "w_out", "w_up", "w_down")
SMALL = ("attn_pre_norm", "forget_bias", "swa_sinks", "rel_bias", "q_latent_norm", "kv_latent_norm", "group_norm",
         "attn_post_norm", "ffn_pre_norm", "conv_b", "ffn_post_norm")
WEIGHTS = ("attn_pre_norm", "w_in", "forget_bias", "swa_sinks", "rel_bias", "q_latent_norm", "w_uq", "kv_latent_norm",
           "w_ukv", "group_norm", "w_out", "attn_post_norm", "ffn_pre_norm", "w_up", "conv_w", "conv_b", "w_down",
           "ffn_post_norm")


def _pack(arrs, cols, row_mult):
    flat = jnp.concatenate([a.reshape(-1) for a in arrs])
    n = flat.shape[0]
    per = cols * row_mult
    total = -(-n // per) * per
    return jnp.pad(flat, (0, total - n)).reshape(total // cols, cols)


def _unpack(packed, shapes):
    flat = packed.reshape(-1)
    out, off = [], 0
    for shp in shapes:
        n = int(np.prod(shp))
        out.append(flat[off:off + n].reshape(shp))
        off += n
    return out


LAYER_KEYS = ("w_qkv_t", "w_lat_t", "w_in_t", "w_uq_p", "w_uq_t", "w_ukv", "w_ukv_t", "w_out", "w_up", "w_down", "conv_w")


def _layer_weights(gathered):
    cols = lambda g: g.transpose(1, 0, 2).reshape(g.shape[1], N_CHIPS * g.shape[2])
    w_in_t = _t(gathered["w_in"]).reshape(IN_COLS, D_MODEL)
    w_in_t = jnp.pad(w_in_t, ((0, IN_ROWS - IN_COLS), (0, 0)))
    w_uq_p = _pad_w_uq(cols(gathered["w_uq"]))
    w_ukv = cols(gathered["w_ukv"])
    return dict(w_qkv_t=w_in_t[:QKV_ROWS], w_lat_t=w_in_t[QKV_ROWS:], w_in_t=w_in_t, w_uq_p=w_uq_p, w_uq_t=_t(w_uq_p),
                w_ukv=w_ukv, w_ukv_t=_t(w_ukv), w_out=gathered["w_out"].reshape(D_MODEL, D_MODEL), w_up=gathered["w_up"],
                w_down=gathered["w_down"].reshape(D_FF, D_MODEL), conv_w=cols(gathered["conv_w"]))


def _local_step(x, target, W, layer_weights, layer_done):
    W = dict(W, **{key: [None] * DEPTH for key in LAYER_KEYS})
    S = x.shape[0]
    tq_tabs, tm_tabs = _rope_tables(S)
    onehot_t = _rel_onehot_t()
    bias_t = _bias_table(W["rel_bias"].T, onehot_t).reshape(SWA_KV_HEADS, SWA_GROUP, 2 * WINDOW, WINDOW)
    bias_t = bias_t.transpose(0, 2, 1, 3).reshape(SWA_KV_HEADS, 2 * WINDOW, GW)
    row = lambda a: a.reshape(1, -1)
    col = lambda a: a.reshape(-1, 1)
    fox_rows = (FOX_ROW0, FOX_ROW0 + FOX_HEADS * HEAD_DIM, FOX_ROW0 + 2 * FOX_HEADS * HEAD_DIM, SWA_Q_HEADS)
    fox = dict(rows=fox_rows, H=FOX_HEADS, Dk=HEAD_DIM, Dv=HEAD_DIM, scale=HEAD_DIM ** -0.5)
    mla = dict(rows=(0, 0, 0, SWA_Q_HEADS + FOX_HEADS), H=MLA_HEADS, Dk=MLA_PAD, Dv=HEAD_DIM, scale=MLA_QK ** -0.5)

    saved = []
    h = _rms_fwd(x, row(W["attn_pre_norm"][0]), name="rms_in")
    for l in range(DEPTH):
        sv = {"x0": x, "h1": h}
        for key, val in layer_weights(l, h).items():
            W[key][l] = val
        qkv = _matmul(W["w_qkv_t"][l], h, tb=True, out_dtype=BF16, name="proj_qkv")
        lat = _matmul(W["w_lat_t"][l], h, tb=True, name="proj_lat")
        oa, lse_a = _swa_fwd(qkv, bias_t, W["swa_sinks"][l], name="swa_fwd")
        fb_col = jnp.pad(col(W["forget_bias"][l]), ((0, GATE_ROWS - FOX_HEADS), (0, 0)))
        f4 = _gate_fwd(lat, fb_col, name="fox_gate_fwd")[:FOX_HEADS]
        f2 = f4 * LOG2E
        f_row, f_col = f2[:, None, :], f2.T
        of, lse_f = _attn_fwd(qkv, qkv, qkv, f_row=f_row, f_col=f_col, name="fox_fwd", **fox)
        nq, nkv, qm, km, vm = _mla_prep_fwd(lat, col(W["q_latent_norm"][l]), col(W["kv_latent_norm"][l]), W["w_uq_t"][l],
                                            W["w_ukv_t"][l], tq_tabs, tm_tabs, name="mla_prep_fwd")
        oc, lse_c = _attn_fwd(qm, km, vm, name="mla_fwd", **mla)
        mixed = _group_norm_fwd(oa, of, oc, col(W["group_norm"][l]), name="group_norm_fwd")
        y = _matmul(mixed, W["w_out"][l], ta=True, name="proj_out")
        x1, h2 = _resid_rms(x, y, row(W["attn_post_norm"][l]), row(W["ffn_pre_norm"][l]), name="attn_resid")
        a = _matmul(h2, W["w_up"][l], b_shards=True, out_dtype=BF16, name="ffn_up")
        u, z = _conv_geglu_fwd(a, W["conv_w"][l], row(W["conv_b"][l]), name="conv_geglu_fwd")
        y2 = _matmul(z, W["w_down"][l], name="ffn_down")
        g_next = row(W["attn_pre_norm"][l + 1]) if l + 1 < DEPTH else None
        x2, h_next = _resid_rms(x1, y2, row(W["ffn_post_norm"][l]), g_next, name="ffn_resid")
        sv.update(qkv=qkv, lat=lat, oa=oa, lse_a=lse_a, fb_col=fb_col, f_row=f_row, f_col=f_col, of=of, lse_f=lse_f,
                  nq=nq, nkv=nkv, qm=qm, km=km, vm=vm, oc=oc, lse_c=lse_c, mixed=mixed, y=y, x1=x1, h2=h2, a=a, u=u, z=z, y2=y2)
        saved.append(sv)
        x, h = x2, h_next

    loss, dx = _loss_head(x, target)

    G = {k: [None] * DEPTH for k in WEIGHTS if k != "rel_bias" and k not in BIG}
    dbias_layers = [None] * DEPTH
    for l in reversed(range(DEPTH)):
        sv = saved[l]
        gb = {}
        dy2, dg = _rms_bwd(sv["y2"], row(W["ffn_post_norm"][l]), dx, out_dtype=BF16, name="ffn_post_bwd")
        G["ffn_post_norm"][l] = dg[0]
        dz = _matmul(dy2, W["w_down"][l], tb=True, name="ffn_down_dx")
        gb["w_down"] = _matmul(sv["z"], dy2, ta=True, name="ffn_down_dw").reshape(N_CHIPS, D_FF // N_CHIPS, D_MODEL)
        da, dcw, dcb = _conv_geglu_bwd(sv["a"], sv["u"], W["conv_w"][l], dz, name="conv_geglu_bwd")
        G["conv_w"][l] = dcw.transpose(1, 0, 2).reshape(3, 2 * D_FF)
        G["conv_b"][l] = dcb.reshape(2 * D_FF)
        dh2 = _matmul(da, W["w_up"][l], tb=True, b_shards=True, a_halves=True, name="ffn_up_dx")
        gb["w_up"] = _matmul(sv["h2"], da, ta=True, out_shards=True, b_halves=True, name="ffn_up_dw")
        token = layer_done(l, gb)
        gb = {}
        dx1, dg = _rms_bwd(sv["x1"], row(W["ffn_pre_norm"][l]) + token, dh2, resid=dx, out_dtype=F32, name="ffn_pre_bwd")
        G["ffn_pre_norm"][l] = dg[0]
        dy, dg = _rms_bwd(sv["y"], row(W["attn_post_norm"][l]), dx1, out_dtype=BF16, name="attn_post_bwd")
        G["attn_post_norm"][l] = dg[0]
        dmixed = _matmul(W["w_out"][l], dy, tb=True, name="proj_out_dx")
        gb["w_out"] = _matmul(sv["mixed"], dy, name="proj_out_dw").reshape(N_CHIPS, D_MODEL // N_CHIPS, D_MODEL)
        doa, dof, doc, dg, delta = _group_norm_bwd(sv["oa"], sv["of"], sv["oc"], col(W["group_norm"][l]), dmixed,
                                                   name="group_norm_bwd")
        G["group_norm"][l] = dg[:, 0]
        dqa, dkva, dbias_l, dsink = _swa_bwd(sv["qkv"], bias_t, W["swa_sinks"][l], doa, sv["lse_a"],
                                             delta.reshape(-1, S), name="swa_bwd")
        dbias_layers[l] = (dbias_l.reshape(SWA_KV_HEADS, 2 * WINDOW, SWA_GROUP, WINDOW).transpose(0, 2, 1, 3)
                           .reshape(SWA_Q_HEADS, -1))
        G["swa_sinks"][l] = dsink[:, 0]
        dqf, dkf, dvf, dfk = _attn_bwd(sv["qkv"], sv["qkv"], sv["qkv"], do=dof, lse=sv["lse_f"], delta=delta,
                                       f_row=sv["f_row"], f_col=sv["f_col"], name="fox_bwd", **fox)
        dF = jnp.pad(dfk.T, ((0, GATE_ROWS - FOX_HEADS), (0, 0)))
        dflog, dfb = _gate_bwd(sv["lat"], sv["fb_col"], dF, name="fox_gate_bwd")
        G["forget_bias"][l] = dfb[:FOX_HEADS, 0]
        dqm, dkm, dvm = _attn_bwd(sv["qm"], sv["km"], sv["vm"], do=doc, lse=sv["lse_c"], delta=delta, name="mla_bwd", **mla)
        dlat, dwq_t, dwkv_t, dgq, dgkv = _mla_prep_bwd(
            sv["lat"], sv["nq"], sv["nkv"], col(W["q_latent_norm"][l]), col(W["kv_latent_norm"][l]), W["w_uq_p"][l],
            W["w_ukv"][l], tq_tabs, tm_tabs, dqm, dkm, dvm, dflog, name="mla_prep_bwd")
        gb["w_uq"], gb["w_ukv"] = _shards_of_cols(_unpad_w_uq(dwq_t.T)), _shards_of_cols(dwkv_t.T)
        G["q_latent_norm"][l], G["kv_latent_norm"][l] = dgq[:, 0], dgkv[:, 0]
        dproj = _dproj_cast(dqa, dkva, dqf, dkf, dvf, dlat, name="dproj_cast")
        dh1 = _matmul(dproj, W["w_in_t"][l], ta=True, name="proj_in_dx")
        dw_in_t = _matmul(dproj, sv["h1"], name="proj_in_dw")
        gb["w_in"] = _t(dw_in_t[:IN_COLS].reshape(N_CHIPS, IN_COLS // N_CHIPS, D_MODEL))
        token = layer_done(l, gb)
        dx, dg = _rms_bwd(sv["x0"], row(W["attn_pre_norm"][l]) + token, dh1, resid=dx1, out_dtype=F32, name="attn_pre_bwd")
        G["attn_pre_norm"][l] = dg[0]

    grads = {k: jnp.stack(v) for k, v in G.items()}
    grads["rel_bias"] = _bias_table_bwd(jnp.stack(dbias_layers), onehot_t).T
    return loss, dx, grads


def kernel(x, attn_pre_norm, w_in, forget_bias, swa_sinks, rel_bias, q_latent_norm, w_uq, kv_latent_norm, w_ukv, group_norm, w_out, attn_post_norm, ffn_pre_norm, w_up, conv_w, conv_b, w_down, ffn_post_norm, loss_target, m_attn_pre_norm, m_w_in, m_forget_bias, m_swa_sinks, m_rel_bias, m_q_latent_norm, m_w_uq, m_kv_latent_norm, m_w_ukv, m_group_norm, m_w_out, m_attn_post_norm, m_ffn_pre_norm, m_w_up, m_conv_w, m_conv_b, m_w_down, m_ffn_post_norm, v_attn_pre_norm, v_w_in, v_forget_bias, v_swa_sinks, v_rel_bias, v_q_latent_norm, v_w_uq, v_kv_latent_norm, v_w_ukv, v_group_norm, v_w_out, v_attn_post_norm, v_ffn_pre_norm, v_w_up, v_conv_w, v_conv_b, v_w_down, v_ffn_post_norm):
    args = dict(locals())
    w = {k: args[k] for k in WEIGHTS}
    m = {k: args["m_" + k] for k in WEIGHTS}
    v = {k: args["v_" + k] for k in WEIGHTS}

    sent = BIG + ("conv_w",)
    gather_state, token = _gather_start([[w[k][l] if k == "conv_w" else w[k][l].astype(BF16) for k in sent]
                                         for l in range(DEPTH)])
    W = {k: w[k] for k in SMALL}
    W["attn_pre_norm"] = W["attn_pre_norm"] + token

    def layer_weights(l, after):
        srcs, lands = _gather_wait(gather_state[l], after, name=f"weight_gather_wait_{l}")
        lands = _gather_forward(lands, name=f"weight_gather_forward_{l}")
        lands = [_place_own(g, s, name="place_own_shard") for g, s in zip(lands, srcs)]
        return _layer_weights(dict(zip(sent, lands)))

    started, groups = [], []

    def layer_done(l, gb):
        keys = [k for k in BIG if k in gb]
        gs = [gb[k] for k in keys]
        tag = f"{l}_{keys[0]}"
        recv = _sibling_exchange(gs, name="grad_sibling_exchange_" + tag)
        pair = [_pair_sum(gk, rk, name="grad_pair_sum") for gk, rk in zip(gs, recv)]
        state, token = _scatter_start(pair, name="grad_scatter_start_" + tag)
        started.append(state)
        groups.append((l, keys))
        return token

    loss_part, dx, g = _local_step(x[0], loss_target[0], W, layer_weights, layer_done)
    loss = lax.psum(loss_part, ("x", "y", "c"))

    reduced = {}
    for (l, keys), (pair, zones) in zip(groups, _scatter_wait(started, dx, name="grad_scatter_wait")):
        for k, p, z in zip(keys, pair, zones):
            reduced[k, l] = _chip_sum(z, p, name="grad_chip_sum")
    mine = [jnp.stack([reduced[k, l] for l in range(DEPTH)]) for k in BIG]
    other = _sibling_share(mine)
    out_g, out_d, out_m, out_v = {}, {}, {}, {}
    for k, g_mine, g_other in zip(BIG, mine, other):
        out_g[k], out_d[k], out_m[k], out_v[k] = _adamw_halves(w[k], g_mine, g_other, m[k], v[k], name="adamw_" + k)

    small_shapes = [w[k].shape for k in SMALL]
    reduced = _allreduce_small(_pack([g[k] for k in SMALL] + [g["conv_w"]], LANES, 8))
    *g_small, g_cw = _unpack(reduced, small_shapes + [g["conv_w"].shape])
    chip = 2 * lax.axis_index("x") + lax.axis_index("y")
    g_small.append(lax.dynamic_slice_in_dim(g_cw, chip * FF_SHARD, FF_SHARD, axis=2))
    names = SMALL + ("conv_w",)
    shapes = small_shapes + [w["conv_w"].shape]
    packed = lambda arrs: _pack(arrs, LANES, ROW_TILE)[None]
    d_s, m_s, v_s = _adamw(packed([w[k] for k in names]), packed(g_small), packed([m[k] for k in names]),
                           packed([v[k] for k in names]), name="adamw_small")
    out_g.update(zip(names, g_small))
    out_d.update(zip(names, _unpack(d_s, shapes)))
    out_m.update(zip(names, _unpack(m_s, shapes)))
    out_v.update(zip(names, _unpack(v_s, shapes)))

    return (loss, dx[None], *[out_g[k] for k in WEIGHTS], *[out_d[k] for k in WEIGHTS],
            *[out_m[k] for k in WEIGHTS], *[out_v[k] for k in WEIGHTS])
```

```python
import math

import numpy as np
import jax
import jax.numpy as jnp
from jax import lax
from jax.experimental import pallas as pl
from jax.experimental.pallas import tpu as pltpu

F32 = jnp.float32
BF16 = jnp.bfloat16

D_MODEL = 1024
DEPTH = 4
HEAD_DIM = 64
SWA_Q_HEADS = 8
SWA_KV_HEADS = 2
SWA_GROUP = SWA_Q_HEADS // SWA_KV_HEADS
WINDOW = 128
FOX_HEADS = 4
MLA_HEADS = 4
MLA_Q_RANK = 256
MLA_KV_RANK = 128
MLA_NOPE = 64
MLA_ROPE = 32
MLA_QK = MLA_NOPE + MLA_ROPE
ROPE_THETA = 10000.0
REL_BUCKETS = 32
REL_MAX_DIST = 128
D_FF = 2816
EPS = 1e-6
NEG_INF = -1e30
LANES = 128
N_CHIPS = 4

IN_COLS = 1956
IN_ROWS = 2048
QKV_ROWS = 1536
LAT_ROWS = IN_ROWS - QKV_ROWS
LAT_SHIFT = FOX_HEADS
FOX_ROW0 = 768
MLA_PAD = LANES
GATE_ROWS = 8

ADAM_LR = 0.001
ADAM_B1 = 0.9
ADAM_B2 = 0.999
ADAM_EPS = 1e-08
ADAM_WD = 0.01
ADAM_STEP = 10

VMEM_LIMIT_BYTES = 48 * 1024 * 1024
ATT_TILE = 512
LOG2E = math.log2(math.e)
ROW_TILE = 256
MESH = pl.DeviceIdType.MESH

NT = (((1,), (1,)), ((), ()))
TN = (((0,), (0,)), ((), ()))
NN = (((1,), (0,)), ((), ()))


def _params(*sem):
    return pltpu.CompilerParams(dimension_semantics=sem, vmem_limit_bytes=VMEM_LIMIT_BYTES)


def _tile(dim, cap):
    for t in (2816, 2048, 1408, 1024, 512, 256, 128, 64, 32, 16, 8):
        if t <= cap and dim % t == 0:
            return t
    return dim


def _dot(a, b, dims=NN):
    return lax.dot_general(a, b, dims, preferred_element_type=F32)


def _split3(a):
    a1 = a.astype(BF16)
    r1 = a - a1.astype(F32)
    a2 = r1.astype(BF16)
    a3 = (r1 - a2.astype(F32)).astype(BF16)
    return a1, a2, a3


FF_SHARD = 2 * D_FF // N_CHIPS
MATMUL_VMEM_BYTES = 40 * 1024 * 1024


def _matmul(a, b, *, ta=False, tb=False, out_dtype=F32, name, b_shards=False, out_shards=False, a_halves=False,
            b_halves=False):
    if a_halves:
        M, K = a.shape[1], 2 * a.shape[2]
    elif ta:
        K, M = a.shape
    else:
        M, K = a.shape
    if b_halves:
        K2, N = b.shape[1], 2 * b.shape[2]
    elif b_shards:
        K2, N = (2 * D_FF, D_MODEL) if tb else (D_MODEL, 2 * D_FF)
    elif tb:
        N, K2 = b.shape
    else:
        K2, N = b.shape
    assert K == K2, (a.shape, b.shape)
    tm, tn = (M if M <= 2048 else _tile(M, 1408)), _tile(N, 1408)
    tk = FF_SHARD if (b_shards and tb) else _tile(K, 2816)
    out_bytes = jnp.dtype(out_dtype).itemsize
    while 2 * 2 * tk * (tm + tn) + (4 + 2 * out_bytes) * tm * tn > MATMUL_VMEM_BYTES and tk % 256 == 0:
        tk //= 2
    nk = K // tk
    dims = (((0 if ta else 1,), (1 if tb else 0,)), ((), ()))

    def body(a_ref, b_ref, o_ref, acc_ref):
        k = pl.program_id(2)

        @pl.when(k == 0)
        def _():
            acc_ref[...] = jnp.zeros_like(acc_ref)

        acc_ref[...] += lax.dot_general(a_ref[...], b_ref[...], dims, preferred_element_type=F32)

        @pl.when(k == nk - 1)
        def _():
            o_ref[...] = acc_ref[...].astype(o_ref.dtype)

    if a_halves:
        nh = K // 2 // tk
        a_spec = pl.BlockSpec((None, tm, tk), lambda i, j, k: (k // nh, i, k % nh))
    else:
        a_spec = pl.BlockSpec((tk, tm), lambda i, j, k: (k, i)) if ta else pl.BlockSpec((tm, tk), lambda i, j, k: (i, k))
    if b_halves:
        nh = N // 2 // tn
        b_spec = pl.BlockSpec((None, tk, tn), lambda i, j, k: (j // nh, k, j % nh))
    elif b_shards and tb:
        assert tk == FF_SHARD
        b_spec = pl.BlockSpec((None, tn, tk), lambda i, j, k: (k, j, 0))
    elif b_shards:
        assert tn == FF_SHARD
        b_spec = pl.BlockSpec((None, tk, tn), lambda i, j, k: (j, k, 0))
    else:
        b_spec = pl.BlockSpec((tn, tk), lambda i, j, k: (j, k)) if tb else pl.BlockSpec((tk, tn), lambda i, j, k: (k, j))
    if out_shards:
        assert tn == FF_SHARD
        out_spec = pl.BlockSpec((None, tm, tn), lambda i, j, k: (j, i, 0))
        out_shape = jax.ShapeDtypeStruct((N // tn, M, tn), out_dtype)
    else:
        out_spec = pl.BlockSpec((tm, tn), lambda i, j, k: (i, j))
        out_shape = jax.ShapeDtypeStruct((M, N), out_dtype)
    return pl.pallas_call(
        body, name=name, grid=(M // tm, N // tn, nk),
        in_specs=[a_spec, b_spec], out_specs=out_spec, out_shape=out_shape,
        scratch_shapes=[pltpu.VMEM((tm, tn), F32)],
        compiler_params=_params("parallel", "parallel", "arbitrary"),
    )(a, b)


def _seg_rms(xs, g):
    r = lax.rsqrt(jnp.mean(xs * xs, axis=-1, keepdims=True) + EPS)
    return xs * r * g


def _seg_rms_bwd(xs, g, dy):
    r = lax.rsqrt(jnp.mean(xs * xs, axis=-1, keepdims=True) + EPS)
    gd = dy * g
    c = jnp.mean(gd * xs, axis=-1, keepdims=True)
    dx = r * gd - xs * (r * r * r * c)
    dg = jnp.sum(dy * (xs * r), axis=0, keepdims=True)
    return dx, dg


def _rms_fwd(x, g, *, name):
    S, W = x.shape
    tm = _tile(S, 512)

    def body(x_ref, g_ref, o_ref):
        o_ref[...] = _seg_rms(x_ref[...], g_ref[...]).astype(o_ref.dtype)

    return pl.pallas_call(
        body, name=name, grid=(S // tm,),
        in_specs=[pl.BlockSpec((tm, W), lambda i: (i, 0)), pl.BlockSpec((1, W), lambda i: (0, 0))],
        out_specs=pl.BlockSpec((tm, W), lambda i: (i, 0)),
        out_shape=jax.ShapeDtypeStruct((S, W), BF16),
        compiler_params=_params("parallel"),
    )(x, g)


def _rms_bwd(x, g, dy, *, resid=None, out_dtype, name, then=None):
    S, W = x.shape
    tm = _tile(S, 512)
    has_resid = resid is not None
    chained = then is not None

    def body(*refs):
        refs = list(refs)
        x_ref, g_ref, dy_ref = refs[:3]
        r_ref = refs[3] if has_resid else None
        n_in = 3 + has_resid + 2 * chained
        x2_ref, g2_ref = (refs[n_in - 2], refs[n_in - 1]) if chained else (None, None)
        outs = refs[n_in:]
        dx_ref, dg_ref = outs[0], outs[1]

        @pl.when(pl.program_id(0) == 0)
        def _():
            dg_ref[...] = jnp.zeros_like(dg_ref)
            if chained:
                outs[3][...] = jnp.zeros_like(outs[3])

        dx, dg = _seg_rms_bwd(x_ref[...], g_ref[...], dy_ref[...])
        if has_resid:
            dx = dx + r_ref[...]
        dx_ref[...] = dx.astype(dx_ref.dtype)
        dg_ref[...] += dg
        if chained:
            dx2, dg2 = _seg_rms_bwd(x2_ref[...], g2_ref[...], dx)
            outs[2][...] = dx2.astype(BF16)
            outs[3][...] += dg2

    row = pl.BlockSpec((tm, W), lambda i: (i, 0))
    vec = pl.BlockSpec((1, W), lambda i: (0, 0))
    ins = [x, g, dy] + ([resid] if has_resid else []) + (list(then) if chained else [])
    return pl.pallas_call(
        body, name=name, grid=(S // tm,),
        in_specs=[row, vec, row] + ([row] if has_resid else []) + ([row, vec] if chained else []),
        out_specs=[row, vec] + ([row, vec] if chained else []),
        out_shape=[jax.ShapeDtypeStruct((S, W), out_dtype), jax.ShapeDtypeStruct((1, W), F32)]
        + ([jax.ShapeDtypeStruct((S, W), BF16), jax.ShapeDtypeStruct((1, W), F32)] if chained else []),
        compiler_params=_params("arbitrary"),
    )(*ins)


def _resid_rms(x, y, g_post, g_next, *, name):
    S, W = x.shape
    tm = _tile(S, 512)
    with_next = g_next is not None

    def body(*refs):
        if with_next:
            x_ref, y_ref, gp_ref, gn_ref, xo_ref, h_ref = refs
        else:
            x_ref, y_ref, gp_ref, xo_ref = refs
        xn = x_ref[...] + _seg_rms(y_ref[...], gp_ref[...])
        xo_ref[...] = xn
        if with_next:
            h_ref[...] = _seg_rms(xn, gn_ref[...]).astype(BF16)

    row = pl.BlockSpec((tm, W), lambda i: (i, 0))
    vec = pl.BlockSpec((1, W), lambda i: (0, 0))
    outs = [jax.ShapeDtypeStruct((S, W), F32)] + ([jax.ShapeDtypeStruct((S, W), BF16)] if with_next else [])
    res = pl.pallas_call(
        body, name=name, grid=(S // tm,),
        in_specs=[row, row, vec] + ([vec] if with_next else []),
        out_specs=[row] + ([row] if with_next else []),
        out_shape=outs,
        compiler_params=_params("parallel"),
    )(*([x, y, g_post] + ([g_next] if with_next else [])))
    return (res[0], res[1]) if with_next else (res[0], None)


def _col_rms(xs, g):
    r = lax.rsqrt(jnp.mean(xs * xs, axis=0, keepdims=True) + EPS)
    return xs * r * g


def _col_rms_bwd(xs, g, dy):
    r = lax.rsqrt(jnp.mean(xs * xs, axis=0, keepdims=True) + EPS)
    gd = dy * g
    c = jnp.mean(gd * xs, axis=0, keepdims=True)
    dx = r * gd - xs * (r * r * r * c)
    dg = jnp.sum(dy * (xs * r), axis=1, keepdims=True)
    return dx, dg


GROUP_ROWS = (SWA_Q_HEADS * HEAD_DIM, FOX_HEADS * HEAD_DIM, MLA_HEADS * HEAD_DIM)


def _group_specs(S, tn):
    outs = [pl.BlockSpec((n, tn), lambda i: (0, i)) for n in GROUP_ROWS]
    g = pl.BlockSpec((D_MODEL, 1), lambda i: (0, 0))
    mixed = pl.BlockSpec((D_MODEL, tn), lambda i: (0, i))
    return outs, g, mixed


def _group_norm_fwd(oa, of, oc, g, *, name):
    S = oa.shape[1]
    tn = _tile(S, 512)
    outs, gs, mixed = _group_specs(S, tn)

    def body(a_ref, f_ref, c_ref, g_ref, o_ref):
        r0 = 0
        for ref, n in zip((a_ref, f_ref, c_ref), GROUP_ROWS):
            o_ref[r0:r0 + n, :] = _col_rms(ref[...], g_ref[r0:r0 + n, :]).astype(BF16)
            r0 += n

    return pl.pallas_call(
        body, name=name, grid=(S // tn,),
        in_specs=outs + [gs], out_specs=mixed,
        out_shape=jax.ShapeDtypeStruct((D_MODEL, S), BF16),
        compiler_params=_params("parallel"),
    )(oa, of, oc, g)


def _group_norm_bwd(oa, of, oc, g, dmixed, *, name):
    S = oa.shape[1]
    tn = _tile(S, 512)
    outs, gs, mixed = _group_specs(S, tn)
    n_heads = D_MODEL // HEAD_DIM

    def body(a_ref, f_ref, c_ref, g_ref, dm_ref, da_ref, df_ref, dc_ref, dg_ref, dl_ref):
        @pl.when(pl.program_id(0) == 0)
        def _():
            dg_ref[...] = jnp.zeros_like(dg_ref)

        r0 = 0
        for ref, dref, n in zip((a_ref, f_ref, c_ref), (da_ref, df_ref, dc_ref), GROUP_ROWS):
            o = ref[...]
            dx, dg = _col_rms_bwd(o, g_ref[r0:r0 + n, :], dm_ref[r0:r0 + n, :])
            dxb = dx.astype(BF16)
            dref[...] = dxb
            dg_ref[r0:r0 + n, :] += dg
            od = o * dxb.astype(F32)
            for h in range(n // HEAD_DIM):
                dl_ref[r0 // HEAD_DIM + h] = jnp.sum(od[h * HEAD_DIM:(h + 1) * HEAD_DIM, :], axis=0, keepdims=True)
            r0 += n

    return pl.pallas_call(
        body, name=name, grid=(S // tn,),
        in_specs=outs + [gs, mixed], out_specs=outs + [gs, pl.BlockSpec((n_heads, 1, tn), lambda i: (0, 0, i))],
        out_shape=[jax.ShapeDtypeStruct((n, S), BF16) for n in GROUP_ROWS] + [jax.ShapeDtypeStruct((D_MODEL, 1), F32),
                                                                              jax.ShapeDtypeStruct((n_heads, 1, S), F32)],
        compiler_params=_params("arbitrary"),
    )(oa, of, oc, g, dmixed)


def _loss_head(y, target):
    S, W = y.shape
    tm = _tile(S, 512)

    def body(y_ref, t_ref, d_ref, l_ref):
        @pl.when(pl.program_id(0) == 0)
        def _():
            l_ref[...] = jnp.zeros_like(l_ref)

        err = y_ref[...] - t_ref[...]
        d_ref[...] = err * (1.0 / W)
        l_ref[...] += 0.5 * jnp.sum(jnp.mean(err * err, axis=-1, keepdims=True), axis=0, keepdims=True)

    row = pl.BlockSpec((tm, W), lambda i: (i, 0))
    d, l = pl.pallas_call(
        body, name="loss_head", grid=(S // tm,),
        in_specs=[row, row],
        out_specs=[row, pl.BlockSpec((1, 1), lambda i: (0, 0))],
        out_shape=[jax.ShapeDtypeStruct((S, W), F32), jax.ShapeDtypeStruct((1, 1), F32)],
        compiler_params=_params("arbitrary"),
    )(y, target)
    return l[0, 0], d


def _attn_fwd(q_src, k_src, v_src, rows, H, Dk, Dv, scale, f_row=None, f_col=None, *, name):
    S = q_src.shape[1]
    T = _tile(S, ATT_TILE)
    nq = S // T
    forget = f_row is not None
    qb, kb, vb = rows[0] // (H * Dk), rows[1] // (H * Dk), rows[2] // (H * Dv)
    hs = range(H)

    def body(*refs):
        if forget:
            q_ref, k_ref, v_ref, fq_ref, fk_ref, o_ref, lse_ref = refs
        else:
            q_ref, k_ref, v_ref, o_ref, lse_ref = refs
        i = pl.program_id(0)

        def tile(j, masked, state):
            off = pl.multiple_of(j * T, T)
            ss = [_dot(k_ref[h * Dk:(h + 1) * Dk, pl.ds(off, T)], q_ref[h * Dk:(h + 1) * Dk, :], TN) * (scale * LOG2E)
                  for h in hs]
            if forget:
                ss = [ss[h] + (fq_ref[h] - fk_ref[pl.ds(off, T), h:h + 1]) for h in hs]
            if masked:
                r = lax.broadcasted_iota(jnp.int32, (T, T), 0)
                c = lax.broadcasted_iota(jnp.int32, (T, T), 1)
                ss = [jnp.where(r <= c, s, NEG_INF) for s in ss]
            m_new = [jnp.maximum(state[h][0], jnp.max(ss[h], axis=0, keepdims=True)) for h in hs]
            alpha = [jnp.exp2(state[h][0] - m_new[h]) for h in hs]
            ps = [jnp.exp2(ss[h] - m_new[h]) for h in hs]
            l_new = [alpha[h] * state[h][1] + jnp.sum(ps[h], axis=0, keepdims=True) for h in hs]
            p_hi = [p.astype(BF16) for p in ps]
            vs = [v_ref[h * Dv:(h + 1) * Dv, pl.ds(off, T)] for h in hs]
            pv = [_dot(vs[h], p_hi[h]) for h in hs]
            if forget:
                pv = [pv[h] + _dot(vs[h], (ps[h] - p_hi[h].astype(F32)).astype(BF16)) for h in hs]
            return tuple((m_new[h], l_new[h], alpha[h] * state[h][2] + pv[h]) for h in hs)

        init = tuple((jnp.full((1, T), NEG_INF, F32), jnp.zeros((1, T), F32), jnp.zeros((Dv, T), F32)) for _ in hs)
        state = lax.fori_loop(0, i, lambda j, st: tile(j, False, st), init)
        state = tile(i, True, state)
        for h in hs:
            m, l, acc = state[h]
            o_ref[h * Dv:(h + 1) * Dv, :] = acc / l
            lse_ref[h] = m + jnp.log2(l)

    in_specs = [pl.BlockSpec((H * Dk, T), lambda i: (qb, i)),
                pl.BlockSpec((H * Dk, S), lambda i: (kb, 0)),
                pl.BlockSpec((H * Dv, S), lambda i: (vb, 0))]
    ins = [q_src, k_src, v_src]
    if forget:
        in_specs += [pl.BlockSpec((H, 1, T), lambda i: (0, 0, i)), pl.BlockSpec((S, H), lambda i: (0, 0))]
        ins += [f_row, f_col]
    return pl.pallas_call(
        body, name=name, grid=(nq,),
        in_specs=in_specs,
        out_specs=[pl.BlockSpec((H * Dv, T), lambda i: (0, i)), pl.BlockSpec((H, 1, T), lambda i: (0, 0, i))],
        out_shape=[jax.ShapeDtypeStruct((H * Dv, S), F32), jax.ShapeDtypeStruct((H, 1, S), F32)],
        compiler_params=_params("parallel"),
    )(*ins)


def _attn_bwd(q_src, k_src, v_src, rows, H, Dk, Dv, scale, do, lse, delta, f_row=None, f_col=None, *, name):
    S = q_src.shape[1]
    T = _tile(S, ATT_TILE)
    nq = S // T
    forget = f_row is not None
    qb, kb, vb, db = rows[0] // (H * Dk), rows[1] // (H * Dk), rows[2] // (H * Dv), rows[3] // H
    hs = range(H)

    def body(*refs):
        if forget:
            (q_ref, k_ref, v_ref, do_ref, lse_ref, dl_ref, fq_ref, fk_ref,
             dq_ref, dk_ref, dv_ref, df_ref, dk_s, dv_s, df_s) = refs
        else:
            q_ref, k_ref, v_ref, do_ref, lse_ref, dl_ref, dq_ref, dk_ref, dv_ref, dk_s, dv_s = refs
        j = pl.program_id(0)

        @pl.when(j == 0)
        def _():
            dq_ref[...] = jnp.zeros_like(dq_ref)

        dk_s[...] = jnp.zeros_like(dk_s)
        dv_s[...] = jnp.zeros_like(dv_s)
        if forget:
            df_s[...] = jnp.zeros_like(df_s)
        kt = [k_ref[h * Dk:(h + 1) * Dk, :] for h in hs]
        kj = [k.T for k in kt]
        vj = [v_ref[h * Dv:(h + 1) * Dv, :].T for h in hs]
        koff = pl.multiple_of(j * T, T)

        def tile(i, masked):
            cols = pl.ds(pl.multiple_of(i * T, T), T)
            qi = [q_ref[h * Dk:(h + 1) * Dk, cols] for h in hs]
            doi = [do_ref[h * Dv:(h + 1) * Dv, cols] for h in hs]
            st = [_dot(kj[h], qi[h]) * (scale * LOG2E) for h in hs]
            if forget:
                st = [st[h] + (fq_ref[h, :, cols] - fk_ref[pl.ds(koff, T), h:h + 1]) for h in hs]
            if masked:
                r = lax.broadcasted_iota(jnp.int32, (T, T), 0)
                c = lax.broadcasted_iota(jnp.int32, (T, T), 1)
                st = [jnp.where(r <= c, x, NEG_INF) for x in st]
            pt = [jnp.exp2(st[h] - lse_ref[h, :, cols]) for h in hs]
            dpt = [_dot(vj[h], doi[h]) for h in hs]
            dst = [pt[h] * (dpt[h] - dl_ref[h, :, cols]) for h in hs]
            ptb = [p.astype(BF16) for p in pt]
            dsb = [d.astype(BF16) for d in dst]
            for h in hs:
                dv_s[h * Dv:(h + 1) * Dv, :] += _dot(doi[h], ptb[h], NT)
            for h in hs:
                dk_s[h * Dk:(h + 1) * Dk, :] += _dot(qi[h], dsb[h], NT)
            for h in hs:
                dq_ref[h * Dk:(h + 1) * Dk, cols] += _dot(kt[h], dsb[h]) * scale
            if forget:
                for h in hs:
                    part = dst[h][:, 0:LANES]
                    for c0 in range(LANES, T, LANES):
                        part = part + dst[h][:, c0:c0 + LANES]
                    df_s[h] += part

        tile(j, True)

        def loop_body(i, carry):
            tile(i, False)
            return carry

        lax.fori_loop(j + 1, nq, loop_body, 0)
        dk_ref[...] = dk_s[...] * scale
        dv_ref[...] = dv_s[...]
        if forget:
            df_ref[...] = jnp.concatenate([-jnp.sum(df_s[h], axis=-1, keepdims=True) for h in hs], axis=1)

    res = lambda D, b0: pl.BlockSpec((H * D, S), lambda j: (b0, 0))
    blk = lambda D, b0: pl.BlockSpec((H * D, T), lambda j: (b0, j))
    row3 = lambda b0: pl.BlockSpec((H, 1, S), lambda j: (b0, 0, 0))
    in_specs = [res(Dk, qb), blk(Dk, kb), blk(Dv, vb), res(Dv, 0), row3(0), row3(db)]
    ins = [q_src, k_src, v_src, do, lse, delta]
    out_specs = [res(Dk, 0), blk(Dk, 0), blk(Dv, 0)]
    out_shape = [jax.ShapeDtypeStruct((H * Dk, S), F32), jax.ShapeDtypeStruct((H * Dk, S), F32),
                 jax.ShapeDtypeStruct((H * Dv, S), F32)]
    scratch = [pltpu.VMEM((H * Dk, T), F32), pltpu.VMEM((H * Dv, T), F32)]
    if forget:
        in_specs += [row3(0), pl.BlockSpec((S, H), lambda j: (0, 0))]
        ins += [f_row, f_col]
        out_specs.append(pl.BlockSpec((T, H), lambda j: (j, 0)))
        out_shape.append(jax.ShapeDtypeStruct((S, H), F32))
        scratch.append(pltpu.VMEM((H, T, min(T, LANES)), F32))
    return pl.pallas_call(
        body, name=name, grid=(nq,),
        in_specs=in_specs, out_specs=out_specs, out_shape=out_shape, scratch_shapes=scratch,
        compiler_params=_params("arbitrary"),
    )(*ins)


GW = SWA_GROUP * WINDOW


def _swa_masks(i):
    r = lax.broadcasted_iota(jnp.int32, (WINDOW, GW), 0)
    c = lax.broadcasted_iota(jnp.int32, (WINDOW, GW), 1) % WINDOW
    return (r > c) & (i > 0), r <= c


def _swa_specs():
    W = WINDOW
    kv_rows = SWA_KV_HEADS * HEAD_DIM
    q = pl.BlockSpec((SWA_Q_HEADS * HEAD_DIM, W), lambda i: (0, i))
    prev = lambda b: pl.BlockSpec((kv_rows, W), lambda i: (b, jnp.maximum(i - 1, 0)))
    cur = lambda b: pl.BlockSpec((kv_rows, W), lambda i: (b, i))
    bias = pl.BlockSpec((SWA_KV_HEADS, 2 * W, GW), lambda i: (0, 0, 0))
    stat = pl.BlockSpec((SWA_Q_HEADS, W), lambda i: (0, i))
    sink = pl.BlockSpec(memory_space=pltpu.SMEM)
    return q, prev(4), cur(4), prev(5), cur(5), bias, stat, sink


def _group_lanes(ref, g, rows_per_head):
    h0 = g * SWA_GROUP
    return jnp.concatenate([ref[(h0 + j) * rows_per_head:(h0 + j + 1) * rows_per_head, :] for j in range(SWA_GROUP)], axis=1)


def _swa_scores(g, q_ref, kp_ref, kc_ref, b_ref, masks):
    rows = slice(g * HEAD_DIM, (g + 1) * HEAD_DIM)
    qg = _group_lanes(q_ref, g, HEAD_DIM)
    scale = HEAD_DIM ** -0.5
    s_p = jnp.where(masks[0], _dot(kp_ref[rows, :], qg, TN) * scale + b_ref[g, 0:WINDOW, :], NEG_INF)
    s_c = jnp.where(masks[1], _dot(kc_ref[rows, :], qg, TN) * scale + b_ref[g, WINDOW:2 * WINDOW, :], NEG_INF)
    return qg, rows, s_p, s_c


def _sink_row(sink_ref, g):
    return jnp.concatenate([jnp.full((1, WINDOW), sink_ref[g * SWA_GROUP + j], F32) for j in range(SWA_GROUP)], axis=1)


def _swa_fwd(qkv, bias_g, sinks, *, name):
    S = qkv.shape[1]
    qs, kp, kc, vp, vc, bs, stat, sk = _swa_specs()
    gs = range(SWA_KV_HEADS)

    def body(sink_ref, q_ref, kp_ref, kc_ref, vp_ref, vc_ref, b_ref, o_ref, lse_ref):
        masks = _swa_masks(pl.program_id(0))
        sc = [_swa_scores(g, q_ref, kp_ref, kc_ref, b_ref, masks) for g in gs]
        sinks_g = [_sink_row(sink_ref, g) for g in gs]
        m = [jnp.maximum(jnp.maximum(jnp.max(sc[g][2], axis=0, keepdims=True), jnp.max(sc[g][3], axis=0, keepdims=True)),
                         sinks_g[g]) for g in gs]
        p_p = [jnp.exp(sc[g][2] - m[g]) for g in gs]
        p_c = [jnp.exp(sc[g][3] - m[g]) for g in gs]
        l = [jnp.sum(p_p[g], axis=0, keepdims=True) + jnp.sum(p_c[g], axis=0, keepdims=True) + jnp.exp(sinks_g[g] - m[g])
             for g in gs]
        o = [_dot(vp_ref[sc[g][1], :], p_p[g].astype(BF16)) + _dot(vc_ref[sc[g][1], :], p_c[g].astype(BF16)) for g in gs]
        for g in gs:
            og = o[g] / l[g]
            lse = m[g] + jnp.log(l[g])
            for j in range(SWA_GROUP):
                h = g * SWA_GROUP + j
                o_ref[h * HEAD_DIM:(h + 1) * HEAD_DIM, :] = og[:, j * WINDOW:(j + 1) * WINDOW]
                lse_ref[h:h + 1, :] = lse[:, j * WINDOW:(j + 1) * WINDOW]

    return pl.pallas_call(
        body, name=name, grid=(S // WINDOW,),
        in_specs=[sk, qs, kp, kc, vp, vc, bs],
        out_specs=[qs, stat],
        out_shape=[jax.ShapeDtypeStruct((SWA_Q_HEADS * HEAD_DIM, S), F32), jax.ShapeDtypeStruct((SWA_Q_HEADS, S), F32)],
        compiler_params=_params("parallel"),
    )(sinks, qkv, qkv, qkv, qkv, qkv, bias_g)


def _swa_bwd(qkv, bias_g, sinks, do, lse, delta, *, name):
    S = qkv.shape[1]
    W = WINDOW
    qs, kp, kc, vp, vc, bs, stat, sk = _swa_specs()
    scale = HEAD_DIM ** -0.5
    kv_rows = SWA_KV_HEADS * HEAD_DIM
    gs = range(SWA_KV_HEADS)

    def body(sink_ref, q_ref, kp_ref, kc_ref, vp_ref, vc_ref, b_ref, do_ref, lse_ref, dl_ref,
             dq_ref, dkv_ref, db_ref, dsk_ref):
        i = pl.program_id(0)

        @pl.when(i == 0)
        def _():
            dkv_ref[...] = jnp.zeros_like(dkv_ref)
            db_ref[...] = jnp.zeros_like(db_ref)
            dsk_ref[...] = jnp.zeros_like(dsk_ref)

        masks = _swa_masks(i)
        prev = pl.ds(pl.multiple_of(jnp.maximum(i - 1, 0) * W, W), W)
        cur = pl.ds(pl.multiple_of(i * W, W), W)
        sc = [_swa_scores(g, q_ref, kp_ref, kc_ref, b_ref, masks) for g in gs]
        dog = [_group_lanes(do_ref, g, HEAD_DIM) for g in gs]
        lse = [_group_lanes(lse_ref, g, 1) for g in gs]
        dl = [_group_lanes(dl_ref, g, 1) for g in gs]
        p_p = [jnp.exp(sc[g][2] - lse[g]) for g in gs]
        p_c = [jnp.exp(sc[g][3] - lse[g]) for g in gs]
        ds_p = [p_p[g] * (_dot(vp_ref[sc[g][1], :], dog[g], TN) - dl[g]) for g in gs]
        ds_c = [p_c[g] * (_dot(vc_ref[sc[g][1], :], dog[g], TN) - dl[g]) for g in gs]
        for g in gs:
            db_ref[g, 0:W, :] += ds_p[g]
            db_ref[g, W:2 * W, :] += ds_c[g]
            dsk = jnp.exp(_sink_row(sink_ref, g) - lse[g]) * dl[g]
            for j in range(SWA_GROUP):
                h = g * SWA_GROUP + j
                dsk_ref[h:h + 1, :] -= jnp.broadcast_to(jnp.sum(dsk[:, j * W:(j + 1) * W], axis=1, keepdims=True), (1, LANES))
        dsb_p = [d.astype(BF16) for d in ds_p]
        dsb_c = [d.astype(BF16) for d in ds_c]
        for g in gs:
            rows = sc[g][1]
            dq = (_dot(kp_ref[rows, :], dsb_p[g]) + _dot(kc_ref[rows, :], dsb_c[g])) * scale
            for j in range(SWA_GROUP):
                h = g * SWA_GROUP + j
                dq_ref[h * HEAD_DIM:(h + 1) * HEAD_DIM, :] = dq[:, j * W:(j + 1) * W]
        for g in gs:
            rows = sc[g][1]
            vrows = slice(kv_rows + rows.start, kv_rows + rows.stop)
            dkv_ref[rows, prev] += _dot(sc[g][0], dsb_p[g], NT) * scale
            dkv_ref[rows, cur] += _dot(sc[g][0], dsb_c[g], NT) * scale
            dkv_ref[vrows, prev] += _dot(dog[g], p_p[g].astype(BF16), NT)
            dkv_ref[vrows, cur] += _dot(dog[g], p_c[g].astype(BF16), NT)

    return pl.pallas_call(
        body, name=name, grid=(S // W,),
        in_specs=[sk, qs, kp, kc, vp, vc, bs, qs, stat, stat],
        out_specs=[qs, pl.BlockSpec((2 * kv_rows, S), lambda i: (0, 0)), bs, pl.BlockSpec((SWA_Q_HEADS, LANES), lambda i: (0, 0))],
        out_shape=[jax.ShapeDtypeStruct((SWA_Q_HEADS * HEAD_DIM, S), F32), jax.ShapeDtypeStruct((2 * kv_rows, S), F32),
                   jax.ShapeDtypeStruct((SWA_KV_HEADS, 2 * W, GW), F32), jax.ShapeDtypeStruct((SWA_Q_HEADS, LANES), F32)],
        compiler_params=_params("arbitrary"),
    )(sinks, qkv, qkv, qkv, qkv, qkv, bias_g, do, lse, delta)


def _rel_onehot_t():
    qi = jnp.arange(WINDOW, dtype=jnp.int32)[None, :] + WINDOW
    kj = jnp.arange(2 * WINDOW, dtype=jnp.int32)[:, None]
    dist = qi - kj
    max_exact = REL_BUCKETS // 2
    d = jnp.maximum(dist, 0)
    log_ratio = jnp.log(jnp.maximum(d, 1).astype(F32) / max_exact) / math.log(REL_MAX_DIST / max_exact)
    large = jnp.minimum(max_exact + (log_ratio * (REL_BUCKETS - max_exact)).astype(jnp.int32), REL_BUCKETS - 1)
    bucket = jnp.where(d < max_exact, d, large).reshape(-1)
    return (bucket[None, :] == jnp.arange(REL_BUCKETS, dtype=jnp.int32)[:, None]).astype(BF16)


def _bias_table(rel_bias_t, onehot_t):
    Hq, NB = rel_bias_t.shape
    N = onehot_t.shape[1]
    tn = _tile(N, 4096)

    def body(r_ref, oh_ref, o_ref):
        oh = oh_ref[...]
        a1, a2, a3 = _split3(r_ref[...])
        o_ref[...] = _dot(a1, oh) + _dot(a2, oh) + _dot(a3, oh)

    return pl.pallas_call(
        body, name="rel_bias_table", grid=(N // tn,),
        in_specs=[pl.BlockSpec((Hq, NB), lambda j: (0, 0)), pl.BlockSpec((NB, tn), lambda j: (0, j))],
        out_specs=pl.BlockSpec((Hq, tn), lambda j: (0, j)),
        out_shape=jax.ShapeDtypeStruct((Hq, N), F32),
        compiler_params=_params("parallel"),
    )(rel_bias_t, onehot_t)


def _bias_table_bwd(dbias, onehot_t):
    L, Hq, N = dbias.shape
    NB = onehot_t.shape[0]
    tn = _tile(N, 4096)

    def body(d_ref, oh_ref, o_ref):
        @pl.when(pl.program_id(0) == 0)
        def _():
            o_ref[...] = jnp.zeros_like(o_ref)

        d = d_ref[0]
        for l in range(1, L):
            d = d + d_ref[l]
        oh = oh_ref[...]
        a1, a2, a3 = _split3(d)
        o_ref[...] += _dot(a1, oh, NT) + _dot(a2, oh, NT) + _dot(a3, oh, NT)

    return pl.pallas_call(
        body, name="rel_bias_bwd", grid=(N // tn,),
        in_specs=[pl.BlockSpec((L, Hq, tn), lambda j: (0, 0, j)), pl.BlockSpec((NB, tn), lambda j: (0, j))],
        out_specs=pl.BlockSpec((Hq, NB), lambda j: (0, 0)),
        out_shape=jax.ShapeDtypeStruct((Hq, NB), F32),
        compiler_params=_params("arbitrary"),
    )(dbias, onehot_t)


def _gate_fwd(lat, fb_col, *, name):
    S = lat.shape[1]
    tn = _tile(S, 256)

    def body(z_ref, fb_ref, o_ref, carry):
        @pl.when(pl.program_id(0) == 0)
        def _():
            carry[...] = jnp.zeros_like(carry)

        z = z_ref[...] + fb_ref[...]
        lf = jnp.minimum(z, 0.0) - jnp.log1p(jnp.exp(-jnp.abs(z)))
        r = lax.broadcasted_iota(jnp.int32, (tn, tn), 0)
        c = lax.broadcasted_iota(jnp.int32, (tn, tn), 1)
        tri = (r <= c).astype(BF16)
        a1, a2, a3 = _split3(lf)
        cum = _dot(a1, tri) + _dot(a2, tri) + _dot(a3, tri) + carry[:, 0:1]
        o_ref[...] = cum
        carry[...] = jnp.broadcast_to(cum[:, tn - 1:tn], carry.shape)

    return pl.pallas_call(
        body, name=name, grid=(S // tn,),
        in_specs=[pl.BlockSpec((GATE_ROWS, tn), lambda i: (0, i)), pl.BlockSpec((GATE_ROWS, 1), lambda i: (0, 0))],
        out_specs=pl.BlockSpec((GATE_ROWS, tn), lambda i: (0, i)),
        out_shape=jax.ShapeDtypeStruct((GATE_ROWS, S), F32),
        scratch_shapes=[pltpu.VMEM((GATE_ROWS, LANES), F32)],
        compiler_params=_params("arbitrary"),
    )(lat, fb_col)


def _gate_bwd(lat, fb_col, dF, *, name):
    S = lat.shape[1]
    tn = _tile(S, 256)
    nt = S // tn

    def body(z_ref, fb_ref, df_ref, dz_ref, dfb_ref, carry):
        @pl.when(pl.program_id(0) == 0)
        def _():
            carry[...] = jnp.zeros_like(carry)
            dfb_ref[...] = jnp.zeros_like(dfb_ref)

        r = lax.broadcasted_iota(jnp.int32, (tn, tn), 0)
        c = lax.broadcasted_iota(jnp.int32, (tn, tn), 1)
        tri = (r >= c).astype(BF16)
        a1, a2, a3 = _split3(df_ref[...])
        dlf = _dot(a1, tri) + _dot(a2, tri) + _dot(a3, tri) + carry[:, 0:1]
        carry[...] = jnp.broadcast_to(dlf[:, 0:1], carry.shape)
        z = z_ref[...] + fb_ref[...]
        row = lax.broadcasted_iota(jnp.int32, (GATE_ROWS, tn), 0)
        dz = jnp.where(row < FOX_HEADS, dlf / (1.0 + jnp.exp(z)), 0.0)
        dz_ref[...] = dz
        dfb_ref[...] += jnp.sum(dz, axis=1, keepdims=True)

    blk = pl.BlockSpec((GATE_ROWS, tn), lambda i: (0, nt - 1 - i))
    vec = pl.BlockSpec((GATE_ROWS, 1), lambda i: (0, 0))
    return pl.pallas_call(
        body, name=name, grid=(nt,),
        in_specs=[blk, vec, blk], out_specs=[blk, vec],
        out_shape=[jax.ShapeDtypeStruct((GATE_ROWS, S), F32), jax.ShapeDtypeStruct((GATE_ROWS, 1), F32)],
        scratch_shapes=[pltpu.VMEM((GATE_ROWS, LANES), F32)],
        compiler_params=_params("arbitrary"),
    )(lat, fb_col, dF)


def _rope_tables(S):
    pos = jnp.arange(S, dtype=F32)
    inv_freq = ROPE_THETA ** (-(jnp.arange(MLA_ROPE // 2, dtype=F32) * 2.0 / MLA_ROPE))
    ang = pos[:, None] * inv_freq[None, :]
    cos, sin = jnp.cos(ang).T, jnp.sin(ang).T
    z16 = jnp.zeros_like(cos)

    def slab(lo, fill):
        def put(first, second, f):
            return jnp.concatenate([jnp.full((lo, S), f, F32), first, second, jnp.full((LANES - lo - MLA_ROPE, S), f, F32)], axis=0)
        return put(cos, cos, fill), put(-sin, z16, 0.0), put(z16, sin, 0.0)

    tq = tuple(jnp.tile(t, (MLA_HEADS, 1)) for t in slab(MLA_NOPE, 1.0))
    return tq, slab(0, 0.0)


def _rope(x, c, s1, s2):
    n = x.shape[0]
    half = MLA_ROPE // 2
    return x * c + pltpu.roll(x, n - half, 0) * s1 + pltpu.roll(x, half, 0) * s2


def _rope_t(dy, c, s1, s2):
    n = dy.shape[0]
    half = MLA_ROPE // 2
    return dy * c + pltpu.roll(dy * s1, half, 0) + pltpu.roll(dy * s2, n - half, 0)


KR_SLAB0 = MLA_Q_RANK + MLA_KV_RANK


def _mla_prep_fwd(lat, g_q, g_kv, w_uq_t, w_ukv_t, tq, tmisc, *, name):
    S = lat.shape[1]
    tn = _tile(S, 512)
    QW = MLA_HEADS * MLA_PAD

    def body(lat_ref, gq_ref, gkv_ref, wq_ref, wkv_ref, c_ref, s1_ref, s2_ref, cm_ref, s1m_ref, s2m_ref,
             nq_ref, nkv_ref, q_ref, k_ref, v_ref):
        x = pltpu.roll(lat_ref[...], LAT_ROWS - LAT_SHIFT, 0)
        nq = _col_rms(x[0:MLA_Q_RANK, :], gq_ref[...]).astype(BF16)
        nkv = _col_rms(x[MLA_Q_RANK:KR_SLAB0, :], gkv_ref[...]).astype(BF16)
        nq_ref[...] = nq
        nkv_ref[...] = nkv
        q_ref[...] = _rope(_dot(wq_ref[...], nq), c_ref[...], s1_ref[...], s2_ref[...]).astype(BF16)
        kv = _dot(wkv_ref[...], nkv).astype(BF16)
        kr = _rope(x[KR_SLAB0:LAT_ROWS, :], cm_ref[...], s1m_ref[...], s2m_ref[...]).astype(BF16)
        for h in range(MLA_HEADS):
            k_ref[h * MLA_PAD:h * MLA_PAD + MLA_NOPE, :] = kv[h * LANES:h * LANES + MLA_NOPE, :]
            k_ref[h * MLA_PAD + MLA_NOPE:(h + 1) * MLA_PAD, :] = kr[0:MLA_PAD - MLA_NOPE, :]
            v_ref[h * HEAD_DIM:(h + 1) * HEAD_DIM, :] = kv[h * LANES + MLA_NOPE:(h + 1) * LANES, :]

    def col(rows):
        return pl.BlockSpec((rows, tn), lambda i: (0, i))

    def full(a):
        return pl.BlockSpec(a.shape, lambda i: (0, 0))

    return pl.pallas_call(
        body, name=name, grid=(S // tn,),
        in_specs=[col(LAT_ROWS), full(g_q), full(g_kv), full(w_uq_t), full(w_ukv_t),
                  col(QW), col(QW), col(QW), col(LANES), col(LANES), col(LANES)],
        out_specs=[col(MLA_Q_RANK), col(MLA_KV_RANK), col(QW), col(QW), col(MLA_HEADS * HEAD_DIM)],
        out_shape=[jax.ShapeDtypeStruct((MLA_Q_RANK, S), BF16), jax.ShapeDtypeStruct((MLA_KV_RANK, S), BF16),
                   jax.ShapeDtypeStruct((QW, S), BF16), jax.ShapeDtypeStruct((QW, S), BF16),
                   jax.ShapeDtypeStruct((MLA_HEADS * HEAD_DIM, S), BF16)],
        compiler_params=_params("parallel"),
    )(lat, g_q, g_kv, w_uq_t, w_ukv_t, *tq, *tmisc)


def _mla_prep_bwd(lat, nq, nkv, g_q, g_kv, w_uq_p, w_ukv, tq, tmisc, dq, dk, dv, dflog, *, name):
    S = lat.shape[1]
    tn = _tile(S, 512)
    QW = MLA_HEADS * MLA_PAD

    def body(lat_ref, nq_ref, nkv_ref, gq_ref, gkv_ref, wq_ref, wkv_ref, c_ref, s1_ref, s2_ref,
             cm_ref, s1m_ref, s2m_ref, dq_ref, dk_ref, dv_ref, dfl_ref,
             dlat_ref, dwq_ref, dwkv_ref, dgq_ref, dgkv_ref, y_s):
        @pl.when(pl.program_id(0) == 0)
        def _():
            dwq_ref[...] = jnp.zeros_like(dwq_ref)
            dwkv_ref[...] = jnp.zeros_like(dwkv_ref)
            dgq_ref[...] = jnp.zeros_like(dgq_ref)
            dgkv_ref[...] = jnp.zeros_like(dgkv_ref)

        x = pltpu.roll(lat_ref[...], LAT_ROWS - LAT_SHIFT, 0)
        dqm = _rope_t(dq_ref[...], c_ref[...], s1_ref[...], s2_ref[...]).astype(BF16)
        dwq_ref[...] += _dot(dqm, nq_ref[...], NT)
        dx, dg = _col_rms_bwd(x[0:MLA_Q_RANK, :], gq_ref[...], _dot(wq_ref[...], dqm))
        y_s[0:MLA_Q_RANK, :] = dx
        dgq_ref[...] += dg
        dkv = jnp.concatenate(
            [part for h in range(MLA_HEADS)
             for part in (dk_ref[h * MLA_PAD:h * MLA_PAD + MLA_NOPE, :], dv_ref[h * HEAD_DIM:(h + 1) * HEAD_DIM, :])],
            axis=0).astype(BF16)
        dwkv_ref[...] += _dot(dkv, nkv_ref[...], NT)
        dx, dg = _col_rms_bwd(x[MLA_Q_RANK:KR_SLAB0, :], gkv_ref[...], _dot(wkv_ref[...], dkv))
        y_s[MLA_Q_RANK:KR_SLAB0, :] = dx
        dgkv_ref[...] += dg
        dkr = dk_ref[MLA_NOPE:MLA_PAD, :]
        for h in range(1, MLA_HEADS):
            dkr = dkr + dk_ref[h * MLA_PAD + MLA_NOPE:(h + 1) * MLA_PAD, :]
        dkr = jnp.concatenate([dkr, jnp.zeros((MLA_NOPE, tn), F32)], axis=0)
        y_s[KR_SLAB0:LAT_ROWS, :] = _rope_t(dkr, cm_ref[...], s1m_ref[...], s2m_ref[...])
        y = pltpu.roll(y_s[...], LAT_SHIFT, 0)
        row = lax.broadcasted_iota(jnp.int32, (LAT_ROWS, tn), 0)
        dfl = jnp.concatenate([dfl_ref[...], jnp.zeros((LAT_ROWS - GATE_ROWS, tn), F32)], axis=0)
        dlat_ref[...] = jnp.where(row < LAT_SHIFT, dfl, y).astype(BF16)

    def col(rows):
        return pl.BlockSpec((rows, tn), lambda i: (0, i))

    def full(a):
        return pl.BlockSpec(a.shape, lambda i: (0, 0))

    def acc(r, c):
        return pl.BlockSpec((r, c), lambda i: (0, 0))

    return pl.pallas_call(
        body, name=name, grid=(S // tn,),
        in_specs=[col(LAT_ROWS), col(MLA_Q_RANK), col(MLA_KV_RANK), full(g_q), full(g_kv),
                  full(w_uq_p), full(w_ukv), col(QW), col(QW), col(QW), col(LANES), col(LANES), col(LANES),
                  col(QW), col(QW), col(MLA_HEADS * HEAD_DIM), col(GATE_ROWS)],
        out_specs=[col(LAT_ROWS), acc(QW, MLA_Q_RANK), acc(QW, MLA_KV_RANK), acc(MLA_Q_RANK, 1), acc(MLA_KV_RANK, 1)],
        out_shape=[jax.ShapeDtypeStruct((LAT_ROWS, S), BF16), jax.ShapeDtypeStruct((QW, MLA_Q_RANK), F32),
                   jax.ShapeDtypeStruct((QW, MLA_KV_RANK), F32), jax.ShapeDtypeStruct((MLA_Q_RANK, 1), F32),
                   jax.ShapeDtypeStruct((MLA_KV_RANK, 1), F32)],
        scratch_shapes=[pltpu.VMEM((LAT_ROWS, tn), F32)],
        compiler_params=_params("arbitrary"),
    )(lat, nq, nkv, g_q, g_kv, w_uq_p, w_ukv, *tq, *tmisc, dq, dk, dv, dflog)


def _dproj_cast(dqa, dkva, dqf, dkf, dvf, dlat, *, name):
    S = dqa.shape[1]
    tn = _tile(S, 512)
    parts = (dqa, dkva, dqf, dkf, dvf, dlat)

    def body(*refs):
        o_ref = refs[-1]
        r0 = 0
        for ref in refs[:-1]:
            n = ref.shape[0]
            o_ref[r0:r0 + n, :] = ref[...].astype(BF16)
            r0 += n

    return pl.pallas_call(
        body, name=name, grid=(S // tn,),
        in_specs=[pl.BlockSpec((p.shape[0], tn), lambda i: (0, i)) for p in parts],
        out_specs=pl.BlockSpec((IN_ROWS, tn), lambda i: (0, i)),
        out_shape=jax.ShapeDtypeStruct((IN_ROWS, S), BF16),
        compiler_params=_params("parallel"),
    )(*parts)


GELU_C = math.sqrt(2.0 / math.pi)
GELU_A = 0.044715


HALO = 16


def _shift_down(a, k, fill):
    r = pltpu.roll(a, k, 0)
    row = lax.broadcasted_iota(jnp.int32, (8, a.shape[1]), 0)
    head = r[0:8, :]
    for i in range(k):
        head = jnp.where(row == i, fill[len(fill) - k + i], head)
    return jnp.concatenate([head, r[8:, :]], axis=0)


def _shift_up(d, k, fill):
    n = d.shape[0]
    r = pltpu.roll(d, n - k, 0)
    row = lax.broadcasted_iota(jnp.int32, (8, d.shape[1]), 0)
    tail = r[n - 8:n, :]
    for i in range(k):
        tail = jnp.where(row == 8 - k + i, fill[i], tail)
    return jnp.concatenate([r[0:n - 8, :], tail], axis=0)


def _conv_taps(a, before, w_ref, b_ref):
    a1 = _shift_down(a, 1, before)
    a2 = _shift_down(a, 2, before)
    return ((b_ref[...] + w_ref[0:1, :] * a2) + w_ref[1:2, :] * a1) + w_ref[2:3, :] * a


def _rows_before(halo_ref, first):
    h = halo_ref[HALO - 2:HALO, :].astype(F32)
    return jnp.where(first, 0.0, h[0:1, :]), jnp.where(first, 0.0, h[1:2, :])


def _conv_specs(S, tm, tc, nc):
    hb = tm // HALO
    main = lambda off: pl.BlockSpec((tm, tc), lambda j, i: (i, j + off))
    prev = lambda off: pl.BlockSpec((HALO, tc), lambda j, i: (jnp.maximum(i * hb - 1, 0), j + off))
    wspec = lambda off: pl.BlockSpec((3, tc), lambda j, i: (0, j + off))
    bspec = lambda off: pl.BlockSpec((1, tc), lambda j, i: (0, j + off))
    return main, prev, wspec, bspec


def _conv_geglu_fwd(a, conv_w, conv_b, *, name):
    S = a.shape[0]
    tm, tc = _tile(S, 512), _tile(D_FF, 1408)
    nc = D_FF // tc
    main, prev, wspec, bspec = _conv_specs(S, tm, tc, nc)

    def body(ag_ref, au_ref, hg_ref, hu_ref, wg_ref, wu_ref, bg_ref, bu_ref, u_ref, z_ref):
        first = pl.program_id(1) == 0
        gate = _conv_taps(ag_ref[...].astype(F32), _rows_before(hg_ref, first), wg_ref, bg_ref)
        up = _conv_taps(au_ref[...].astype(F32), _rows_before(hu_ref, first), wu_ref, bu_ref)
        u_ref[0] = gate
        u_ref[1] = up
        cdf = 0.5 * (1.0 + jnp.tanh(GELU_C * (gate + GELU_A * (gate * gate * gate))))
        z_ref[...] = (gate * cdf * up).astype(BF16)

    return pl.pallas_call(
        body, name=name, grid=(nc, S // tm),
        in_specs=[main(0), main(nc), prev(0), prev(nc), wspec(0), wspec(nc), bspec(0), bspec(nc)],
        out_specs=[pl.BlockSpec((2, tm, tc), lambda j, i: (0, i, j)), pl.BlockSpec((tm, tc), lambda j, i: (i, j))],
        out_shape=[jax.ShapeDtypeStruct((2, S, D_FF), F32), jax.ShapeDtypeStruct((S, D_FF), BF16)],
        compiler_params=_params("parallel", "arbitrary"),
    )(a, a, a, a, conv_w, conv_w, conv_b, conv_b)


def _geglu_bwd(gate, up, dz):
    g2x = gate * gate
    th = jnp.tanh(GELU_C * (gate + GELU_A * (g2x * gate)))
    cdf = 0.5 * (1.0 + th)
    dgelu = cdf + gate * (0.5 * (1.0 - th * th) * (GELU_C * (1.0 + 3.0 * GELU_A * g2x)))
    return dz * up * dgelu, dz * (gate * cdf)


def _conv_geglu_bwd(a, u, conv_w, dz, *, name):
    S = a.shape[0]
    tm, tc = _tile(S, 512), _tile(D_FF, 1408)
    nc = D_FF // tc
    nr = S // tm
    main, _, wspec, _ = _conv_specs(S, tm, tc, nc)
    hb = tm // 8

    def body(ag_ref, au_ref, u_ref, un_ref, wg_ref, wu_ref, dz_ref, dzn_ref, da_ref, dw_ref, db_ref):
        i = pl.program_id(1)
        last = i == nr - 1

        @pl.when(i == 0)
        def _():
            dw_ref[...] = jnp.zeros_like(dw_ref)
            db_ref[...] = jnp.zeros_like(db_ref)

        dus = _geglu_bwd(u_ref[0], u_ref[1], dz_ref[...])
        dus_n = _geglu_bwd(un_ref[0], un_ref[1], dzn_ref[...])
        for half, a_ref, w_ref in ((0, ag_ref, wg_ref), (1, au_ref, wu_ref)):
            du, du_n = dus[half], dus_n[half]
            after = (jnp.where(last, 0.0, du_n[0:1, :]), jnp.where(last, 0.0, du_n[1:2, :]))
            shifted = (_shift_up(du, 2, after), _shift_up(du, 1, after), du)
            da_ref[half] = (w_ref[2:3, :] * du + w_ref[1:2, :] * shifted[1] + w_ref[0:1, :] * shifted[0]).astype(BF16)
            af = a_ref[...].astype(F32)
            for tap in range(3):
                dw_ref[half, tap:tap + 1, :] += jnp.sum(shifted[tap] * af, axis=0, keepdims=True)
            db_ref[half] += jnp.sum(du, axis=0, keepdims=True)

    nxt8 = lambda j, i: (0, jnp.minimum((i + 1) * hb, S // 8 - 1), j)
    return pl.pallas_call(
        body, name=name, grid=(nc, nr),
        in_specs=[main(0), main(nc), pl.BlockSpec((2, tm, tc), lambda j, i: (0, i, j)), pl.BlockSpec((2, 8, tc), nxt8),
                  wspec(0), wspec(nc), pl.BlockSpec((tm, tc), lambda j, i: (i, j)),
                  pl.BlockSpec((8, tc), lambda j, i: (jnp.minimum((i + 1) * hb, S // 8 - 1), j))],
        out_specs=[pl.BlockSpec((2, tm, tc), lambda j, i: (0, i, j)), pl.BlockSpec((2, 3, tc), lambda j, i: (0, 0, j)),
                   pl.BlockSpec((2, 1, tc), lambda j, i: (0, 0, j))],
        out_shape=[jax.ShapeDtypeStruct((2, S, D_FF), BF16), jax.ShapeDtypeStruct((2, 3, D_FF), F32),
                   jax.ShapeDtypeStruct((2, 1, D_FF), F32)],
        compiler_params=_params("parallel", "arbitrary"),
    )(a, a, u, u, conv_w, conv_w, dz, dz)


def _adamw_update(w, g, m, v):
    m = ADAM_B1 * m + (1.0 - ADAM_B1) * g
    v = ADAM_B2 * v + (1.0 - ADAM_B2) * jnp.square(g)
    m_hat = m / (1.0 - ADAM_B1 ** ADAM_STEP)
    v_hat = v / (1.0 - ADAM_B2 ** ADAM_STEP)
    return -ADAM_LR * (m_hat / (jnp.sqrt(v_hat) + ADAM_EPS) + ADAM_WD * w), m, v


def _adamw(w, g, m, v, *, name):
    L, A, B = w.shape
    ta = _tile(A, ROW_TILE)

    def body(w_ref, g_ref, m_ref, v_ref, d_ref, mo_ref, vo_ref):
        d_ref[...], mo_ref[...], vo_ref[...] = _adamw_update(w_ref[...], g_ref[...], m_ref[...], v_ref[...])

    blk = pl.BlockSpec((None, ta, B), lambda l, i: (l, i, 0))
    shp = jax.ShapeDtypeStruct((L, A, B), F32)
    return pl.pallas_call(
        body, name=name, grid=(L, A // ta),
        in_specs=[blk] * 4, out_specs=[blk] * 3, out_shape=[shp] * 3,
        compiler_params=_params("parallel", "parallel"),
    )(w, g, m, v)


def _scalar(v):
    return jnp.reshape(v, (1,)).astype(jnp.int32)


def _adamw_halves(w, g_mine, g_other, m, v, *, name):
    L, A, B = w.shape
    ta = _tile(A // 2, ROW_TILE)
    nb = A // 2 // ta

    def body(c_ref, w_ref, gm_ref, go_ref, m_ref, v_ref, g_ref, d_ref, mo_ref, vo_ref):
        g = jnp.where(pl.program_id(1) // nb == c_ref[0], gm_ref[...], go_ref[...])
        g_ref[...] = g
        d_ref[...], mo_ref[...], vo_ref[...] = _adamw_update(w_ref[...], g, m_ref[...], v_ref[...])

    blk = pl.BlockSpec((None, ta, B), lambda l, i, c_ref: (l, i, 0))
    half = pl.BlockSpec((None, ta, B), lambda l, i, c_ref: (l, i % nb, 0))
    shp = jax.ShapeDtypeStruct((L, A, B), F32)
    return pl.pallas_call(
        body, name=name,
        grid_spec=pltpu.PrefetchScalarGridSpec(num_scalar_prefetch=1, grid=(L, A // ta),
                                               in_specs=[blk, half, half, blk, blk], out_specs=[blk] * 4),
        out_shape=[shp] * 4,
        compiler_params=_params("parallel", "parallel"),
    )(_scalar(lax.axis_index("c")), w, g_mine, g_other, m, v)


def _chip_index():
    return 2 * lax.axis_index("x") + lax.axis_index("y")


def _pair_sum(g, recv, *, name):
    n, A, B = g.shape
    ta = _tile(A // 2, ROW_TILE)
    nb = A // 2 // ta

    def body(c_ref, g_ref, r_ref, o_ref):
        o_ref[...] = g_ref[...] + r_ref[...]

    return pl.pallas_call(
        body, name=name,
        grid_spec=pltpu.PrefetchScalarGridSpec(
            num_scalar_prefetch=1, grid=(n, nb),
            in_specs=[pl.BlockSpec((None, ta, B), lambda s, r, c_ref: (s, c_ref[0] * nb + r, 0)),
                      pl.BlockSpec((None, ta, B), lambda s, r, c_ref: (s, r, 0))],
            out_specs=pl.BlockSpec((None, ta, B), lambda s, r, c_ref: (s, r, 0))),
        out_shape=jax.ShapeDtypeStruct((n, A // 2, B), F32),
        compiler_params=_params("parallel", "parallel"),
    )(_scalar(lax.axis_index("c")), g, recv)


def _chip_sum(landed, own, *, name):
    n, A2, B = landed.shape
    ta = _tile(A2, ROW_TILE)

    def body(me_ref, *refs):
        slots, own_ref, o_ref = refs[:n], refs[n], refs[n + 1]
        parts = [jnp.where(me_ref[0] == s, own_ref[...], slots[s][...]) for s in range(n)]
        o_ref[...] = ((parts[0] + parts[1]) + parts[2]) + parts[3]

    def slot(s):
        return pl.BlockSpec((None, ta, B), lambda r, me_ref: (jnp.where(me_ref[0] == s, (s + 1) % n, s), r, 0))

    return pl.pallas_call(
        body, name=name,
        grid_spec=pltpu.PrefetchScalarGridSpec(
            num_scalar_prefetch=1, grid=(A2 // ta,),
            in_specs=[slot(s) for s in range(n)] + [pl.BlockSpec((None, ta, B), lambda r, me_ref: (me_ref[0], r, 0))],
            out_specs=pl.BlockSpec((ta, B), lambda r, me_ref: (r, 0))),
        out_shape=jax.ShapeDtypeStruct((A2, B), F32),
        compiler_params=_params("parallel"),
    )(_scalar(_chip_index()), *([landed] * n), own)


HBM_SPEC = pl.BlockSpec(memory_space=pl.ANY)
COMM_PARAMS = pltpu.CompilerParams(has_side_effects=True)


def _mesh_pos():
    return lax.axis_index("x"), lax.axis_index("y"), lax.axis_index("c")


def _other_chips(x, y):
    return [(1 - x, y), (x, 1 - y), (1 - x, 1 - y)]


def _remote(src, dst, send_sems, recv_sems, k, to):
    return pltpu.make_async_remote_copy(src_ref=src, dst_ref=dst, send_sem=send_sems.at[k], recv_sem=recv_sems.at[k],
                                        device_id=to, device_id_type=MESH)


def _place_own(gathered, shard, *, name):
    A, B = shard.shape
    ta = _tile(A, ROW_TILE)

    def body(me_ref, s_ref, g_ref, o_ref):
        o_ref[...] = s_ref[...]

    return pl.pallas_call(
        body, name=name,
        grid_spec=pltpu.PrefetchScalarGridSpec(
            num_scalar_prefetch=1, grid=(A // ta,),
            in_specs=[pl.BlockSpec((ta, B), lambda r, me_ref: (r, 0)), HBM_SPEC],
            out_specs=pl.BlockSpec((None, ta, B), lambda r, me_ref: (me_ref[0], r, 0))),
        out_shape=jax.ShapeDtypeStruct(gathered.shape, gathered.dtype),
        input_output_aliases={2: 0},
        compiler_params=_params("parallel"),
    )(_scalar(_chip_index()), shard, gathered)


def _half_rows(rows, c, align=8):
    assert (rows // 2) % align == 0
    return pl.ds(pl.multiple_of(c * (rows // 2), align), rows // 2)


BF16_ROWS = 16


def _halved(rows):
    return rows % (2 * BF16_ROWS) == 0


def _gather_copies(srcs, lands, send_sems, recv_sems):
    x, y, c = _mesh_pos()
    me = 2 * x + y
    out = []
    for k in range(len(srcs)):
        a = srcs[k].shape[0]
        rows = _half_rows(a, c, BF16_ROWS) if _halved(a) else pl.ds(0, a)
        for j, (px, py) in enumerate(_other_chips(x, y)):
            send = _remote(srcs[k].at[rows], lands[k].at[me, rows], send_sems, recv_sems, 3 * k + j, (px, py, c))
            recv = _remote(srcs[k].at[rows], lands[k].at[2 * px + py, rows], send_sems, recv_sems, 3 * k + j, (px, py, c))
            out.append((send, recv))
    return out


def _gather_start(srcs):
    nl, n = len(srcs), len(srcs[0])
    lands = [[lax.empty((N_CHIPS,) + s.shape, s.dtype) for s in sl] for sl in srcs]
    flat = [a for l in range(nl) for a in srcs[l] + lands[l]]

    def body(*refs):
        bufs, sems, token = refs[:len(flat)], refs[len(flat):len(flat) + 2 * nl], refs[-1]
        for l in range(nl):
            mine = bufs[2 * n * l:2 * n * (l + 1)]
            for send, _ in _gather_copies(mine[:n], mine[n:], sems[2 * l], sems[2 * l + 1]):
                send.start()
        token[...] = jnp.zeros_like(token)

    res = pl.pallas_call(
        body, name="weight_gather_start",
        in_specs=[HBM_ONLY] * len(flat),
        out_specs=[SEM_SPEC] * (2 * nl) + [HBM_ONLY] * len(flat) + [pl.BlockSpec(memory_space=pltpu.VMEM)],
        out_shape=[pltpu.SemaphoreType.DMA((3 * n,))] * (2 * nl) + [pltpu.HBM(a.shape, a.dtype) for a in flat]
        + [jax.ShapeDtypeStruct((8, LANES), F32)],
        input_output_aliases={i: 2 * nl + i for i in range(len(flat))},
        compiler_params=SPLIT_PARAMS,
    )(*[pltpu.with_memory_space_constraint(a, pltpu.HBM) for a in flat])
    bufs = res[2 * nl:2 * nl + len(flat)]
    state = [(res[2 * l], res[2 * l + 1], list(bufs[2 * n * l:2 * n * l + n]), list(bufs[2 * n * l + n:2 * n * (l + 1)]))
             for l in range(nl)]
    return state, res[-1][0:1, 0:1]


def _gather_wait(state, after, *, name):
    send_sems, recv_sems, srcs, lands = state
    n = len(srcs)

    def body(*refs):
        for send, recv in _gather_copies(refs[:n], refs[n:2 * n], refs[2 * n], refs[2 * n + 1]):
            send.wait_send()
            recv.wait_recv()

    res = pl.pallas_call(
        body, name=name,
        in_specs=[HBM_ONLY] * (2 * n) + [SEM_SPEC, SEM_SPEC, HBM_SPEC],
        out_specs=[HBM_ONLY] * (2 * n),
        out_shape=[pltpu.HBM(a.shape, a.dtype) for a in srcs + lands],
        input_output_aliases={i: i for i in range(2 * n)},
        compiler_params=SPLIT_PARAMS,
    )(*srcs, *lands, send_sems, recv_sems, after)
    return list(res[:n]), list(res[n:])


def _gather_forward(lands, *, name):
    n = len(lands)

    def body(*refs):
        bufs, outs = refs[:n], refs[n:2 * n]
        send_sems, recv_sems = refs[2 * n:]
        x, y, c = _mesh_pos()
        copies, waits = [], []
        for k in range(n):
            a = lands[k].shape[1]
            if not _halved(a):
                continue
            for j, (px, py) in enumerate(_other_chips(x, y)):
                mine = 2 * px + py, _half_rows(a, c, BF16_ROWS)
                copies.append(_remote(bufs[k].at[mine], outs[k].at[mine], send_sems, recv_sems, 3 * k + j, (x, y, 1 - c)))
                lands_here = outs[k].at[2 * px + py, _half_rows(a, 1 - c, BF16_ROWS)]
                waits.append(_remote(lands_here, lands_here, send_sems, recv_sems, 3 * k + j, (x, y, 1 - c)))
        for cp in copies:
            cp.start()
        for cp in waits:
            cp.wait_recv()
        for cp in copies:
            cp.wait_send()

    return pl.pallas_call(
        body, name=name,
        in_specs=[HBM_SPEC] * n, out_specs=[HBM_SPEC] * n,
        out_shape=[jax.ShapeDtypeStruct(a.shape, a.dtype) for a in lands],
        scratch_shapes=[pltpu.SemaphoreType.DMA((3 * n,)), pltpu.SemaphoreType.DMA((3 * n,))],
        input_output_aliases={i: i for i in range(n)},
        compiler_params=COMM_PARAMS,
    )(*lands)


def _sibling_exchange(gs, *, name):
    n = len(gs)

    def body(*refs):
        ins, outs = refs[:n], refs[n:2 * n]
        send_sems, recv_sems = refs[2 * n:]
        x, y, c = _mesh_pos()
        copies = [_remote(ins[k].at[:, _half_rows(gs[k].shape[1], 1 - c)], outs[k], send_sems, recv_sems, k, (x, y, 1 - c))
                  for k in range(n)]
        for cp in copies:
            cp.start()
        for cp in copies:
            cp.wait()

    return pl.pallas_call(
        body, name=name,
        in_specs=[HBM_SPEC] * n, out_specs=[HBM_SPEC] * n,
        out_shape=[jax.ShapeDtypeStruct((g.shape[0], g.shape[1] // 2, g.shape[2]), g.dtype) for g in gs],
        scratch_shapes=[pltpu.SemaphoreType.DMA((n,)), pltpu.SemaphoreType.DMA((n,))],
        compiler_params=COMM_PARAMS,
    )(*gs)


HBM_ONLY = pl.BlockSpec(memory_space=pltpu.HBM)
SEM_SPEC = pl.BlockSpec(memory_space=pltpu.SEMAPHORE)
SPLIT_PARAMS = pltpu.CompilerParams(has_side_effects=pltpu.SideEffectType.DATAFLOW_SIDE_EFFECTING)


def _scatter_copies(srcs, lands, send_sems, recv_sems):
    x, y, c = _mesh_pos()
    me = 2 * x + y
    out = []
    for k in range(len(srcs)):
        for j, (px, py) in enumerate(_other_chips(x, y)):
            s = 2 * px + py
            send = _remote(srcs[k].at[s], lands[k].at[me], send_sems, recv_sems, 3 * k + j, (px, py, c))
            recv = _remote(srcs[k].at[s], lands[k].at[s], send_sems, recv_sems, 3 * k + j, (px, py, c))
            out.append((send, recv))
    return out


def _scatter_start(ps, *, name):
    n = len(ps)
    lands = [lax.empty(p.shape, p.dtype) for p in ps]

    def body(*refs):
        srcs, zones = refs[:n], refs[n:2 * n]
        send_sems, recv_sems, token = refs[2 * n], refs[2 * n + 1], refs[-1]
        for send, _ in _scatter_copies(srcs, zones, send_sems, recv_sems):
            send.start()
        token[...] = jnp.zeros_like(token)

    hbm = lambda a: pltpu.HBM(a.shape, a.dtype)
    res = pl.pallas_call(
        body, name=name,
        in_specs=[HBM_ONLY] * (2 * n),
        out_specs=[SEM_SPEC, SEM_SPEC] + [HBM_ONLY] * (2 * n) + [pl.BlockSpec(memory_space=pltpu.VMEM)],
        out_shape=[pltpu.SemaphoreType.DMA((3 * n,)), pltpu.SemaphoreType.DMA((3 * n,))] + [hbm(a) for a in ps + lands]
        + [jax.ShapeDtypeStruct((8, LANES), F32)],
        input_output_aliases={i: 2 + i for i in range(2 * n)},
        compiler_params=SPLIT_PARAMS,
    )(*[pltpu.with_memory_space_constraint(a, pltpu.HBM) for a in ps + lands])
    return (res[0], res[1], list(res[2:2 + n]), list(res[2 + n:2 + 2 * n])), res[-1][0:1, 0:1]


def _scatter_wait(started, after, *, name):
    ng = len(started)
    sizes = [len(st[2]) for st in started]
    offs = [2 * sum(sizes[:i]) for i in range(ng + 1)]
    flat = [a for (_, _, ps, lands) in started for a in ps + lands]

    def body(*refs):
        bufs, sems = refs[:len(flat)], refs[len(flat):len(flat) + 2 * ng]
        for i, n in enumerate(sizes):
            srcs, zones = bufs[offs[i]:offs[i] + n], bufs[offs[i] + n:offs[i + 1]]
            for send, recv in _scatter_copies(srcs, zones, sems[2 * i], sems[2 * i + 1]):
                send.wait_send()
                recv.wait_recv()

    res = pl.pallas_call(
        body, name=name,
        in_specs=[HBM_ONLY] * len(flat) + [SEM_SPEC] * (2 * ng) + [HBM_SPEC],
        out_specs=[HBM_ONLY] * len(flat),
        out_shape=[pltpu.HBM(a.shape, a.dtype) for a in flat],
        input_output_aliases={i: i for i in range(len(flat))},
        compiler_params=SPLIT_PARAMS,
    )(*flat, *[s for (ss, rs, _, _) in started for s in (ss, rs)], after)
    return [(list(res[offs[i]:offs[i] + n]), list(res[offs[i] + n:offs[i + 1]])) for i, n in enumerate(sizes)]


def _sibling_share(hs):
    n = len(hs)

    def body(*refs):
        ins, outs = refs[:n], refs[n:2 * n]
        send_sems, recv_sems = refs[2 * n:]
        x, y, c = _mesh_pos()
        copies = [_remote(ins[k], outs[k], send_sems, recv_sems, k, (x, y, 1 - c)) for k in range(n)]
        for cp in copies:
            cp.start()
        for cp in copies:
            cp.wait()

    return pl.pallas_call(
        body, name="grad_sibling_share",
        in_specs=[HBM_SPEC] * n, out_specs=[HBM_SPEC] * n,
        out_shape=[jax.ShapeDtypeStruct(h.shape, h.dtype) for h in hs],
        scratch_shapes=[pltpu.SemaphoreType.DMA((n,)), pltpu.SemaphoreType.DMA((n,))],
        compiler_params=COMM_PARAMS,
    )(*hs)


def _allreduce_small(part):
    rows, C = part.shape

    def body(p_ref, o_ref, slots, send_sems, recv_sems):
        x, y, c = _mesh_pos()
        me = 4 * x + 2 * y + c
        slots[me] = p_ref[...]
        copies = []
        for k in range(1, 8):
            kx, ky, kc = (k >> 2) & 1, (k >> 1) & 1, k & 1
            peer = (x ^ kx if kx else x, y ^ ky if ky else y, c ^ kc if kc else c)
            cp = _remote(p_ref, slots.at[me], send_sems, recv_sems, k - 1, peer)
            cp.start()
            copies.append((cp, peer))
        for k, (cp, peer) in enumerate(copies):
            src = 4 * peer[0] + 2 * peer[1] + peer[2]
            _remote(p_ref, slots.at[src], send_sems, recv_sems, k, peer).wait_recv()
        for cp, _ in copies:
            cp.wait_send()
        total = slots[0]
        for d in range(1, 8):
            total = total + slots[d]
        o_ref[...] = total

    return pl.pallas_call(
        body, name="small_grad_allreduce",
        in_specs=[pl.BlockSpec(memory_space=pltpu.VMEM)], out_specs=pl.BlockSpec(memory_space=pltpu.VMEM),
        out_shape=jax.ShapeDtypeStruct((rows, C), F32),
        scratch_shapes=[pltpu.VMEM((8, rows, C), F32), pltpu.SemaphoreType.DMA((7,)), pltpu.SemaphoreType.DMA((7,))],
        compiler_params=pltpu.CompilerParams(has_side_effects=True, vmem_limit_bytes=VMEM_LIMIT_BYTES),
    )(part)


def _pad_w_uq(w):
    lead = w.shape[:-1]
    w = w.reshape(lead + (MLA_HEADS, MLA_QK))
    w = jnp.concatenate([w, jnp.zeros(lead + (MLA_HEADS, MLA_PAD - MLA_QK), w.dtype)], axis=-1)
    return w.reshape(lead + (MLA_HEADS * MLA_PAD,))


def _unpad_w_uq(g):
    lead = g.shape[:-1]
    return g.reshape(lead + (MLA_HEADS, MLA_PAD))[..., :MLA_QK].reshape(lead + (MLA_HEADS * MLA_QK,))


def _t(a):
    return jnp.swapaxes(a, -1, -2)


def _shards_of_cols(w):
    A, NB = w.shape
    return w.reshape(A, N_CHIPS, NB // N_CHIPS).transpose(1, 0, 2)


BIG = ("w_in", "w_uq", "w_ukv", "w_out", "w_up", "w_down")
SMALL = ("attn_pre_norm", "forget_bias", "swa_sinks", "rel_bias", "q_latent_norm", "kv_latent_norm", "group_norm",
         "attn_post_norm", "ffn_pre_norm", "conv_b", "ffn_post_norm")
WEIGHTS = ("attn_pre_norm", "w_in", "forget_bias", "swa_sinks", "rel_bias", "q_latent_norm", "w_uq", "kv_latent_norm",
           "w_ukv", "group_norm", "w_out", "attn_post_norm", "ffn_pre_norm", "w_up", "conv_w", "conv_b", "w_down",
           "ffn_post_norm")


PACK_UNIT = 8 * LANES


def _pack_rows(shape):
    return -(-int(np.prod(shape)) // PACK_UNIT) * 8


def _pack(arrs, row_mult=8):
    parts = []
    for a in arrs:
        n = int(np.prod(a.shape))
        parts.append(jnp.pad(a.reshape(-1), (0, _pack_rows(a.shape) * LANES - n)).reshape(-1, LANES))
    rows = sum(p.shape[0] for p in parts)
    pad = -rows % row_mult
    if pad:
        parts.append(jnp.zeros((pad, LANES), parts[0].dtype))
    return jnp.concatenate(parts, axis=0)


def _unpack(packed, shapes):
    packed = packed.reshape(-1, LANES)
    out, off = [], 0
    for shp in shapes:
        r = _pack_rows(shp)
        out.append(packed[off:off + r].reshape(-1)[:int(np.prod(shp))].reshape(shp))
        off += r
    return out


LAYER_KEYS = ("w_qkv_t", "w_lat_t", "w_in_t", "w_uq_p", "w_uq_t", "w_ukv", "w_ukv_t", "w_out", "w_up", "w_down", "conv_w")


def _layer_weights(gathered):
    cols = lambda g: g.transpose(1, 0, 2).reshape(g.shape[1], N_CHIPS * g.shape[2])
    w_in_t = _t(gathered["w_in"]).reshape(IN_COLS, D_MODEL)
    w_in_t = jnp.pad(w_in_t, ((0, IN_ROWS - IN_COLS), (0, 0)))
    w_uq_p = _pad_w_uq(cols(gathered["w_uq"]))
    w_ukv = cols(gathered["w_ukv"])
    return dict(w_qkv_t=w_in_t[:QKV_ROWS], w_lat_t=w_in_t[QKV_ROWS:], w_in_t=w_in_t, w_uq_p=w_uq_p, w_uq_t=_t(w_uq_p),
                w_ukv=w_ukv, w_ukv_t=_t(w_ukv), w_out=gathered["w_out"].reshape(D_MODEL, D_MODEL), w_up=gathered["w_up"],
                w_down=gathered["w_down"].reshape(D_FF, D_MODEL), conv_w=cols(gathered["conv_w"]))


def _local_step(x, target, W, layer_weights, layer_done):
    W = dict(W, **{key: [None] * DEPTH for key in LAYER_KEYS})
    S = x.shape[0]
    tq_tabs, tm_tabs = _rope_tables(S)
    onehot_t = _rel_onehot_t()
    bias_t = _bias_table(W["rel_bias"].T, onehot_t).reshape(SWA_KV_HEADS, SWA_GROUP, 2 * WINDOW, WINDOW)
    bias_t = bias_t.transpose(0, 2, 1, 3).reshape(SWA_KV_HEADS, 2 * WINDOW, GW)
    row = lambda a: a.reshape(1, -1)
    col = lambda a: a.reshape(-1, 1)
    fox_rows = (FOX_ROW0, FOX_ROW0 + FOX_HEADS * HEAD_DIM, FOX_ROW0 + 2 * FOX_HEADS * HEAD_DIM, SWA_Q_HEADS)
    fox = dict(rows=fox_rows, H=FOX_HEADS, Dk=HEAD_DIM, Dv=HEAD_DIM, scale=HEAD_DIM ** -0.5)
    mla = dict(rows=(0, 0, 0, SWA_Q_HEADS + FOX_HEADS), H=MLA_HEADS, Dk=MLA_PAD, Dv=HEAD_DIM, scale=MLA_QK ** -0.5)

    saved = []
    h = _rms_fwd(x, row(W["attn_pre_norm"][0]), name="rms_in")
    for l in range(DEPTH):
        sv = {"x0": x, "h1": h}
        for key, val in layer_weights(l, h).items():
            W[key][l] = val
        qkv = _matmul(W["w_qkv_t"][l], h, tb=True, out_dtype=BF16, name="proj_qkv")
        lat = _matmul(W["w_lat_t"][l], h, tb=True, name="proj_lat")
        oa, lse_a = _swa_fwd(qkv, bias_t, W["swa_sinks"][l], name="swa_fwd")
        fb_col = jnp.pad(col(W["forget_bias"][l]), ((0, GATE_ROWS - FOX_HEADS), (0, 0)))
        f4 = _gate_fwd(lat, fb_col, name="fox_gate_fwd")[:FOX_HEADS]
        f2 = f4 * LOG2E
        f_row, f_col = f2[:, None, :], f2.T
        of, lse_f = _attn_fwd(qkv, qkv, qkv, f_row=f_row, f_col=f_col, name="fox_fwd", **fox)
        nq, nkv, qm, km, vm = _mla_prep_fwd(lat, col(W["q_latent_norm"][l]), col(W["kv_latent_norm"][l]), W["w_uq_t"][l],
                                            W["w_ukv_t"][l], tq_tabs, tm_tabs, name="mla_prep_fwd")
        oc, lse_c = _attn_fwd(qm, km, vm, name="mla_fwd", **mla)
        mixed = _group_norm_fwd(oa, of, oc, col(W["group_norm"][l]), name="group_norm_fwd")
        y = _matmul(mixed, W["w_out"][l], ta=True, name="proj_out")
        x1, h2 = _resid_rms(x, y, row(W["attn_post_norm"][l]), row(W["ffn_pre_norm"][l]), name="attn_resid")
        a = _matmul(h2, W["w_up"][l], b_shards=True, out_dtype=BF16, name="ffn_up")
        u, z = _conv_geglu_fwd(a, W["conv_w"][l], row(W["conv_b"][l]), name="conv_geglu_fwd")
        y2 = _matmul(z, W["w_down"][l], name="ffn_down")
        g_next = row(W["attn_pre_norm"][l + 1]) if l + 1 < DEPTH else None
        x2, h_next = _resid_rms(x1, y2, row(W["ffn_post_norm"][l]), g_next, name="ffn_resid")
        sv.update(qkv=qkv, lat=lat, oa=oa, lse_a=lse_a, fb_col=fb_col, f_row=f_row, f_col=f_col, of=of, lse_f=lse_f,
                  nq=nq, nkv=nkv, qm=qm, km=km, vm=vm, oc=oc, lse_c=lse_c, mixed=mixed, y=y, x1=x1, h2=h2, a=a, u=u, z=z, y2=y2)
        saved.append(sv)
        x, h = x2, h_next

    loss, dx = _loss_head(x, target)

    G = {k: [None] * DEPTH for k in WEIGHTS if k != "rel_bias" and k not in BIG}
    dbias_layers = [None] * DEPTH
    for l in reversed(range(DEPTH)):
        sv = saved[l]
        gb = {}
        if l == DEPTH - 1:
            dy2, dg = _rms_bwd(sv["y2"], row(W["ffn_post_norm"][l]), dx, out_dtype=BF16, name="ffn_post_bwd")
            G["ffn_post_norm"][l] = dg[0]
        dz = _matmul(dy2, W["w_down"][l], tb=True, name="ffn_down_dx")
        gb["w_down"] = _matmul(sv["z"], dy2, ta=True, name="ffn_down_dw").reshape(N_CHIPS, D_FF // N_CHIPS, D_MODEL)
        da, dcw, dcb = _conv_geglu_bwd(sv["a"], sv["u"], W["conv_w"][l], dz, name="conv_geglu_bwd")
        G["conv_w"][l] = dcw.transpose(1, 0, 2).reshape(3, 2 * D_FF)
        G["conv_b"][l] = dcb.reshape(2 * D_FF)
        dh2 = _matmul(da, W["w_up"][l], tb=True, b_shards=True, a_halves=True, name="ffn_up_dx")
        gb["w_up"] = _matmul(sv["h2"], da, ta=True, out_shards=True, b_halves=True, name="ffn_up_dw")
        token = layer_done(l, gb)
        gb = {}
        dx1, dg, dy, dg_post = _rms_bwd(sv["x1"], row(W["ffn_pre_norm"][l]) + token, dh2, resid=dx, out_dtype=F32,
                                        then=(sv["y"], row(W["attn_post_norm"][l])), name="ffn_pre_bwd")
        G["ffn_pre_norm"][l] = dg[0]
        G["attn_post_norm"][l] = dg_post[0]
        dmixed = _matmul(W["w_out"][l], dy, tb=True, name="proj_out_dx")
        gb["w_out"] = _matmul(sv["mixed"], dy, name="proj_out_dw").reshape(N_CHIPS, D_MODEL // N_CHIPS, D_MODEL)
        doa, dof, doc, dg, delta = _group_norm_bwd(sv["oa"], sv["of"], sv["oc"], col(W["group_norm"][l]), dmixed,
                                                   name="group_norm_bwd")
        G["group_norm"][l] = dg[:, 0]
        dqa, dkva, dbias_l, dsink = _swa_bwd(sv["qkv"], bias_t, W["swa_sinks"][l], doa, sv["lse_a"],
                                             delta.reshape(-1, S), name="swa_bwd")
        dbias_layers[l] = (dbias_l.reshape(SWA_KV_HEADS, 2 * WINDOW, SWA_GROUP, WINDOW).transpose(0, 2, 1, 3)
                           .reshape(SWA_Q_HEADS, -1))
        G["swa_sinks"][l] = dsink[:, 0]
        dqf, dkf, dvf, dfk = _attn_bwd(sv["qkv"], sv["qkv"], sv["qkv"], do=dof, lse=sv["lse_f"], delta=delta,
                                       f_row=sv["f_row"], f_col=sv["f_col"], name="fox_bwd", **fox)
        dF = jnp.pad(dfk.T, ((0, GATE_ROWS - FOX_HEADS), (0, 0)))
        dflog, dfb = _gate_bwd(sv["lat"], sv["fb_col"], dF, name="fox_gate_bwd")
        G["forget_bias"][l] = dfb[:FOX_HEADS, 0]
        dqm, dkm, dvm = _attn_bwd(sv["qm"], sv["km"], sv["vm"], do=doc, lse=sv["lse_c"], delta=delta, name="mla_bwd", **mla)
        dlat, dwq_t, dwkv_t, dgq, dgkv = _mla_prep_bwd(
            sv["lat"], sv["nq"], sv["nkv"], col(W["q_latent_norm"][l]), col(W["kv_latent_norm"][l]), W["w_uq_p"][l],
            W["w_ukv"][l], tq_tabs, tm_tabs, dqm, dkm, dvm, dflog, name="mla_prep_bwd")
        gb["w_uq"], gb["w_ukv"] = _shards_of_cols(_unpad_w_uq(dwq_t.T)), _shards_of_cols(dwkv_t.T)
        G["q_latent_norm"][l], G["kv_latent_norm"][l] = dgq[:, 0], dgkv[:, 0]
        dproj = _dproj_cast(dqa, dkva, dqf, dkf, dvf, dlat, name="dproj_cast")
        dh1 = _matmul(dproj, W["w_in_t"][l], ta=True, name="proj_in_dx")
        dw_in_t = _matmul(dproj, sv["h1"], name="proj_in_dw")
        gb["w_in"] = _t(dw_in_t[:IN_COLS].reshape(N_CHIPS, IN_COLS // N_CHIPS, D_MODEL))
        token = layer_done(l, gb)
        below = (saved[l - 1]["y2"], row(W["ffn_post_norm"][l - 1])) if l > 0 else None
        res = _rms_bwd(sv["x0"], row(W["attn_pre_norm"][l]) + token, dh1, resid=dx1, out_dtype=F32, then=below,
                       name="attn_pre_bwd")
        dx, G["attn_pre_norm"][l] = res[0], res[1][0]
        if l > 0:
            dy2, G["ffn_post_norm"][l - 1] = res[2], res[3][0]

    grads = {k: jnp.stack(v) for k, v in G.items()}
    grads["rel_bias"] = _bias_table_bwd(jnp.stack(dbias_layers), onehot_t).T
    return loss, dx, grads


def kernel(x, attn_pre_norm, w_in, forget_bias, swa_sinks, rel_bias, q_latent_norm, w_uq, kv_latent_norm, w_ukv, group_norm, w_out, attn_post_norm, ffn_pre_norm, w_up, conv_w, conv_b, w_down, ffn_post_norm, loss_target, m_attn_pre_norm, m_w_in, m_forget_bias, m_swa_sinks, m_rel_bias, m_q_latent_norm, m_w_uq, m_kv_latent_norm, m_w_ukv, m_group_norm, m_w_out, m_attn_post_norm, m_ffn_pre_norm, m_w_up, m_conv_w, m_conv_b, m_w_down, m_ffn_post_norm, v_attn_pre_norm, v_w_in, v_forget_bias, v_swa_sinks, v_rel_bias, v_q_latent_norm, v_w_uq, v_kv_latent_norm, v_w_ukv, v_group_norm, v_w_out, v_attn_post_norm, v_ffn_pre_norm, v_w_up, v_conv_w, v_conv_b, v_w_down, v_ffn_post_norm):
    args = dict(locals())
    w = {k: args[k] for k in WEIGHTS}
    m = {k: args["m_" + k] for k in WEIGHTS}
    v = {k: args["v_" + k] for k in WEIGHTS}

    sent = BIG + ("conv_w",)
    gather_state, token = _gather_start([[w[k][l] if k == "conv_w" else w[k][l].astype(BF16) for k in sent]
                                         for l in range(DEPTH)])
    W = {k: w[k] for k in SMALL}
    W["attn_pre_norm"] = W["attn_pre_norm"] + token

    def layer_weights(l, after):
        srcs, lands = _gather_wait(gather_state[l], after, name=f"weight_gather_wait_{l}")
        lands = _gather_forward(lands, name=f"weight_gather_forward_{l}")
        lands = [_place_own(g, s, name="place_own_shard") for g, s in zip(lands, srcs)]
        return _layer_weights(dict(zip(sent, lands)))

    started, groups = [], []

    def layer_done(l, gb):
        keys = [k for k in BIG if k in gb]
        gs = [gb[k] for k in keys]
        tag = f"{l}_{keys[0]}"
        recv = _sibling_exchange(gs, name="grad_sibling_exchange_" + tag)
        pair = [_pair_sum(gk, rk, name="grad_pair_sum") for gk, rk in zip(gs, recv)]
        state, token = _scatter_start(pair, name="grad_scatter_start_" + tag)
        started.append(state)
        groups.append((l, keys))
        return token

    loss_part, dx, g = _local_step(x[0], loss_target[0], W, layer_weights, layer_done)
    loss = lax.psum(loss_part, ("x", "y", "c"))

    reduced = {}
    for (l, keys), (pair, zones) in zip(groups, _scatter_wait(started, dx, name="grad_scatter_wait")):
        for k, p, z in zip(keys, pair, zones):
            reduced[k, l] = _chip_sum(z, p, name="grad_chip_sum")
    mine = [jnp.stack([reduced[k, l] for l in range(DEPTH)]) for k in BIG]
    other = _sibling_share(mine)
    out_g, out_d, out_m, out_v = {}, {}, {}, {}
    for k, g_mine, g_other in zip(BIG, mine, other):
        out_g[k], out_d[k], out_m[k], out_v[k] = _adamw_halves(w[k], g_mine, g_other, m[k], v[k], name="adamw_" + k)

    small_shapes = [w[k].shape for k in SMALL]
    reduced = _allreduce_small(_pack([g[k] for k in SMALL] + [g["conv_w"]]))
    *g_small, g_cw = _unpack(reduced, small_shapes + [g["conv_w"].shape])
    chip = 2 * lax.axis_index("x") + lax.axis_index("y")
    g_small.append(lax.dynamic_slice_in_dim(g_cw, chip * FF_SHARD, FF_SHARD, axis=2))
    names = SMALL + ("conv_w",)
    shapes = small_shapes + [w["conv_w"].shape]
    packed = lambda arrs: _pack(arrs, ROW_TILE)[None]
    d_s, m_s, v_s = _adamw(packed([w[k] for k in names]), packed(g_small), packed([m[k] for k in names]),
                           packed([v[k] for k in names]), name="adamw_small")
    out_g.update(zip(names, g_small))
    out_d.update(zip(names, _unpack(d_s, shapes)))
    out_m.update(zip(names, _unpack(m_s, shapes)))
    out_v.update(zip(names, _unpack(v_s, shapes)))

    return (loss, dx[None], *[out_g[k] for k in WEIGHTS], *[out_d[k] for k in WEIGHTS],
            *[out_m[k] for k in WEIGHTS], *[out_v[k] for k in WEIGHTS])
```

```python
import math

import numpy as np
import jax
import jax.numpy as jnp
from jax import lax
from jax.experimental import pallas as pl
from jax.experimental.pallas import tpu as pltpu

F32 = jnp.float32
BF16 = jnp.bfloat16

D_MODEL = 1024
DEPTH = 4
HEAD_DIM = 64
SWA_Q_HEADS = 8
SWA_KV_HEADS = 2
SWA_GROUP = SWA_Q_HEADS // SWA_KV_HEADS
WINDOW = 128
FOX_HEADS = 4
MLA_HEADS = 4
MLA_Q_RANK = 256
MLA_KV_RANK = 128
MLA_NOPE = 64
MLA_ROPE = 32
MLA_QK = MLA_NOPE + MLA_ROPE
ROPE_THETA = 10000.0
REL_BUCKETS = 32
REL_MAX_DIST = 128
D_FF = 2816
EPS = 1e-6
NEG_INF = -1e30
LANES = 128
N_CHIPS = 4

IN_COLS = 1956
IN_ROWS = 2048
QKV_ROWS = 1536
LAT_ROWS = IN_ROWS - QKV_ROWS
LAT_SHIFT = FOX_HEADS
FOX_ROW0 = 768
MLA_PAD = LANES
GATE_ROWS = 8

ADAM_LR = 0.001
ADAM_B1 = 0.9
ADAM_B2 = 0.999
ADAM_EPS = 1e-08
ADAM_WD = 0.01
ADAM_STEP = 10

VMEM_LIMIT_BYTES = 48 * 1024 * 1024
ATT_TILE = 512
LOG2E = math.log2(math.e)
ROW_TILE = 256
MESH = pl.DeviceIdType.MESH

NT = (((1,), (1,)), ((), ()))
TN = (((0,), (0,)), ((), ()))
NN = (((1,), (0,)), ((), ()))


def _params(*sem):
    return pltpu.CompilerParams(dimension_semantics=sem, vmem_limit_bytes=VMEM_LIMIT_BYTES)


def _tile(dim, cap):
    for t in (2816, 2048, 1408, 1024, 512, 256, 128, 64, 32, 16, 8):
        if t <= cap and dim % t == 0:
            return t
    return dim


def _dot(a, b, dims=NN):
    return lax.dot_general(a, b, dims, preferred_element_type=F32)


def _split3(a):
    a1 = a.astype(BF16)
    r1 = a - a1.astype(F32)
    a2 = r1.astype(BF16)
    a3 = (r1 - a2.astype(F32)).astype(BF16)
    return a1, a2, a3


FF_SHARD = 2 * D_FF // N_CHIPS
MATMUL_VMEM_BYTES = 40 * 1024 * 1024


def _matmul(a, b, *, ta=False, tb=False, out_dtype=F32, name, b_shards=False, out_shards=False, a_halves=False,
            b_halves=False):
    if a_halves:
        M, K = a.shape[1], 2 * a.shape[2]
    elif ta:
        K, M = a.shape
    else:
        M, K = a.shape
    if b_halves:
        K2, N = b.shape[1], 2 * b.shape[2]
    elif b_shards:
        K2, N = (2 * D_FF, D_MODEL) if tb else (D_MODEL, 2 * D_FF)
    elif tb:
        N, K2 = b.shape
    else:
        K2, N = b.shape
    assert K == K2, (a.shape, b.shape)
    tm, tn = (M if M <= 2048 else _tile(M, 1408)), _tile(N, 1408)
    tk = FF_SHARD if (b_shards and tb) else _tile(K, 2816)
    out_bytes = jnp.dtype(out_dtype).itemsize
    while 2 * 2 * tk * (tm + tn) + (4 + 2 * out_bytes) * tm * tn > MATMUL_VMEM_BYTES and tk % 256 == 0:
        tk //= 2
    nk = K // tk
    dims = (((0 if ta else 1,), (1 if tb else 0,)), ((), ()))

    def body(a_ref, b_ref, o_ref, acc_ref):
        k = pl.program_id(2)

        @pl.when(k == 0)
        def _():
            acc_ref[...] = jnp.zeros_like(acc_ref)

        acc_ref[...] += lax.dot_general(a_ref[...], b_ref[...], dims, preferred_element_type=F32)

        @pl.when(k == nk - 1)
        def _():
            o_ref[...] = acc_ref[...].astype(o_ref.dtype)

    if a_halves:
        nh = K // 2 // tk
        a_spec = pl.BlockSpec((None, tm, tk), lambda i, j, k: (k // nh, i, k % nh))
    else:
        a_spec = pl.BlockSpec((tk, tm), lambda i, j, k: (k, i)) if ta else pl.BlockSpec((tm, tk), lambda i, j, k: (i, k))
    if b_halves:
        nh = N // 2 // tn
        b_spec = pl.BlockSpec((None, tk, tn), lambda i, j, k: (j // nh, k, j % nh))
    elif b_shards and tb:
        assert tk == FF_SHARD
        b_spec = pl.BlockSpec((None, tn, tk), lambda i, j, k: (k, j, 0))
    elif b_shards:
        assert tn == FF_SHARD
        b_spec = pl.BlockSpec((None, tk, tn), lambda i, j, k: (j, k, 0))
    else:
        b_spec = pl.BlockSpec((tn, tk), lambda i, j, k: (j, k)) if tb else pl.BlockSpec((tk, tn), lambda i, j, k: (k, j))
    if out_shards:
        assert tn == FF_SHARD
        out_spec = pl.BlockSpec((None, tm, tn), lambda i, j, k: (j, i, 0))
        out_shape = jax.ShapeDtypeStruct((N // tn, M, tn), out_dtype)
    else:
        out_spec = pl.BlockSpec((tm, tn), lambda i, j, k: (i, j))
        out_shape = jax.ShapeDtypeStruct((M, N), out_dtype)
    return pl.pallas_call(
        body, name=name, grid=(M // tm, N // tn, nk),
        in_specs=[a_spec, b_spec], out_specs=out_spec, out_shape=out_shape,
        scratch_shapes=[pltpu.VMEM((tm, tn), F32)],
        compiler_params=_params("parallel", "parallel", "arbitrary"),
    )(a, b)


def _seg_rms(xs, g):
    r = lax.rsqrt(jnp.mean(xs * xs, axis=-1, keepdims=True) + EPS)
    return xs * r * g


def _seg_rms_bwd(xs, g, dy):
    r = lax.rsqrt(jnp.mean(xs * xs, axis=-1, keepdims=True) + EPS)
    gd = dy * g
    c = jnp.mean(gd * xs, axis=-1, keepdims=True)
    dx = r * gd - xs * (r * r * r * c)
    dg = jnp.sum(dy * (xs * r), axis=0, keepdims=True)
    return dx, dg


def _rms_fwd(x, g, *, name):
    S, W = x.shape
    tm = _tile(S, 512)

    def body(x_ref, g_ref, o_ref):
        o_ref[...] = _seg_rms(x_ref[...], g_ref[...]).astype(o_ref.dtype)

    return pl.pallas_call(
        body, name=name, grid=(S // tm,),
        in_specs=[pl.BlockSpec((tm, W), lambda i: (i, 0)), pl.BlockSpec((1, W), lambda i: (0, 0))],
        out_specs=pl.BlockSpec((tm, W), lambda i: (i, 0)),
        out_shape=jax.ShapeDtypeStruct((S, W), BF16),
        compiler_params=_params("parallel"),
    )(x, g)


def _rms_bwd(x, g, dy, *, resid=None, out_dtype, name, then=None):
    S, W = x.shape
    tm = _tile(S, 512)
    has_resid = resid is not None
    chained = then is not None

    def body(*refs):
        refs = list(refs)
        x_ref, g_ref, dy_ref = refs[:3]
        r_ref = refs[3] if has_resid else None
        n_in = 3 + has_resid + 2 * chained
        x2_ref, g2_ref = (refs[n_in - 2], refs[n_in - 1]) if chained else (None, None)
        outs = refs[n_in:]
        dx_ref, dg_ref = outs[0], outs[1]

        @pl.when(pl.program_id(0) == 0)
        def _():
            dg_ref[...] = jnp.zeros_like(dg_ref)
            if chained:
                outs[3][...] = jnp.zeros_like(outs[3])

        dx, dg = _seg_rms_bwd(x_ref[...], g_ref[...], dy_ref[...])
        if has_resid:
            dx = dx + r_ref[...]
        dx_ref[...] = dx.astype(dx_ref.dtype)
        dg_ref[...] += dg
        if chained:
            dx2, dg2 = _seg_rms_bwd(x2_ref[...], g2_ref[...], dx)
            outs[2][...] = dx2.astype(BF16)
            outs[3][...] += dg2

    row = pl.BlockSpec((tm, W), lambda i: (i, 0))
    vec = pl.BlockSpec((1, W), lambda i: (0, 0))
    ins = [x, g, dy] + ([resid] if has_resid else []) + (list(then) if chained else [])
    return pl.pallas_call(
        body, name=name, grid=(S // tm,),
        in_specs=[row, vec, row] + ([row] if has_resid else []) + ([row, vec] if chained else []),
        out_specs=[row, vec] + ([row, vec] if chained else []),
        out_shape=[jax.ShapeDtypeStruct((S, W), out_dtype), jax.ShapeDtypeStruct((1, W), F32)]
        + ([jax.ShapeDtypeStruct((S, W), BF16), jax.ShapeDtypeStruct((1, W), F32)] if chained else []),
        compiler_params=_params("arbitrary"),
    )(*ins)


def _resid_rms(x, y, g_post, g_next, *, name):
    S, W = x.shape
    tm = _tile(S, 512)
    with_next = g_next is not None

    def body(*refs):
        if with_next:
            x_ref, y_ref, gp_ref, gn_ref, xo_ref, h_ref = refs
        else:
            x_ref, y_ref, gp_ref, xo_ref = refs
        xn = x_ref[...] + _seg_rms(y_ref[...], gp_ref[...])
        xo_ref[...] = xn
        if with_next:
            h_ref[...] = _seg_rms(xn, gn_ref[...]).astype(BF16)

    row = pl.BlockSpec((tm, W), lambda i: (i, 0))
    vec = pl.BlockSpec((1, W), lambda i: (0, 0))
    outs = [jax.ShapeDtypeStruct((S, W), F32)] + ([jax.ShapeDtypeStruct((S, W), BF16)] if with_next else [])
    res = pl.pallas_call(
        body, name=name, grid=(S // tm,),
        in_specs=[row, row, vec] + ([vec] if with_next else []),
        out_specs=[row] + ([row] if with_next else []),
        out_shape=outs,
        compiler_params=_params("parallel"),
    )(*([x, y, g_post] + ([g_next] if with_next else [])))
    return (res[0], res[1]) if with_next else (res[0], None)


def _col_rms(xs, g):
    r = lax.rsqrt(jnp.mean(xs * xs, axis=0, keepdims=True) + EPS)
    return xs * r * g


def _col_rms_bwd(xs, g, dy):
    r = lax.rsqrt(jnp.mean(xs * xs, axis=0, keepdims=True) + EPS)
    gd = dy * g
    c = jnp.mean(gd * xs, axis=0, keepdims=True)
    dx = r * gd - xs * (r * r * r * c)
    dg = jnp.sum(dy * (xs * r), axis=1, keepdims=True)
    return dx, dg


GROUP_ROWS = (SWA_Q_HEADS * HEAD_DIM, FOX_HEADS * HEAD_DIM, MLA_HEADS * HEAD_DIM)


def _group_specs(S, tn):
    outs = [pl.BlockSpec((n, tn), lambda i: (0, i)) for n in GROUP_ROWS]
    g = pl.BlockSpec((D_MODEL, 1), lambda i: (0, 0))
    mixed = pl.BlockSpec((D_MODEL, tn), lambda i: (0, i))
    return outs, g, mixed


def _group_norm_fwd(oa, of, oc, g, *, name):
    S = oa.shape[1]
    tn = _tile(S, 512)
    outs, gs, mixed = _group_specs(S, tn)

    def body(a_ref, f_ref, c_ref, g_ref, o_ref):
        r0 = 0
        for ref, n in zip((a_ref, f_ref, c_ref), GROUP_ROWS):
            o_ref[r0:r0 + n, :] = _col_rms(ref[...], g_ref[r0:r0 + n, :]).astype(BF16)
            r0 += n

    return pl.pallas_call(
        body, name=name, grid=(S // tn,),
        in_specs=outs + [gs], out_specs=mixed,
        out_shape=jax.ShapeDtypeStruct((D_MODEL, S), BF16),
        compiler_params=_params("parallel"),
    )(oa, of, oc, g)


def _group_norm_bwd(oa, of, oc, g, dmixed, *, name):
    S = oa.shape[1]
    tn = _tile(S, 512)
    outs, gs, mixed = _group_specs(S, tn)
    n_heads = D_MODEL // HEAD_DIM

    def body(a_ref, f_ref, c_ref, g_ref, dm_ref, da_ref, df_ref, dc_ref, dg_ref, dl_ref):
        @pl.when(pl.program_id(0) == 0)
        def _():
            dg_ref[...] = jnp.zeros_like(dg_ref)

        r0 = 0
        for ref, dref, n in zip((a_ref, f_ref, c_ref), (da_ref, df_ref, dc_ref), GROUP_ROWS):
            o = ref[...]
            dx, dg = _col_rms_bwd(o, g_ref[r0:r0 + n, :], dm_ref[r0:r0 + n, :])
            dxb = dx.astype(BF16)
            dref[...] = dxb
            dg_ref[r0:r0 + n, :] += dg
            od = o * dxb.astype(F32)
            for h in range(n // HEAD_DIM):
                dl_ref[r0 // HEAD_DIM + h] = jnp.sum(od[h * HEAD_DIM:(h + 1) * HEAD_DIM, :], axis=0, keepdims=True)
            r0 += n

    return pl.pallas_call(
        body, name=name, grid=(S // tn,),
        in_specs=outs + [gs, mixed], out_specs=outs + [gs, pl.BlockSpec((n_heads, 1, tn), lambda i: (0, 0, i))],
        out_shape=[jax.ShapeDtypeStruct((n, S), BF16) for n in GROUP_ROWS] + [jax.ShapeDtypeStruct((D_MODEL, 1), F32),
                                                                              jax.ShapeDtypeStruct((n_heads, 1, S), F32)],
        compiler_params=_params("arbitrary"),
    )(oa, of, oc, g, dmixed)


def _loss_head(y, target):
    S, W = y.shape
    tm = _tile(S, 512)

    def body(y_ref, t_ref, d_ref, l_ref):
        @pl.when(pl.program_id(0) == 0)
        def _():
            l_ref[...] = jnp.zeros_like(l_ref)

        err = y_ref[...] - t_ref[...]
        d_ref[...] = err * (1.0 / W)
        l_ref[...] += 0.5 * jnp.sum(jnp.mean(err * err, axis=-1, keepdims=True), axis=0, keepdims=True)

    row = pl.BlockSpec((tm, W), lambda i: (i, 0))
    d, l = pl.pallas_call(
        body, name="loss_head", grid=(S // tm,),
        in_specs=[row, row],
        out_specs=[row, pl.BlockSpec((1, 1), lambda i: (0, 0))],
        out_shape=[jax.ShapeDtypeStruct((S, W), F32), jax.ShapeDtypeStruct((1, 1), F32)],
        compiler_params=_params("arbitrary"),
    )(y, target)
    return l[0, 0], d


def _attn_fwd(q_src, k_src, v_src, rows, H, Dk, Dv, scale, f_row=None, f_col=None, *, name):
    S = q_src.shape[1]
    T = _tile(S, ATT_TILE)
    nq = S // T
    forget = f_row is not None
    qb, kb, vb = rows[0] // (H * Dk), rows[1] // (H * Dk), rows[2] // (H * Dv)
    hs = range(H)

    def body(*refs):
        if forget:
            q_ref, k_ref, v_ref, fq_ref, fk_ref, o_ref, lse_ref = refs
        else:
            q_ref, k_ref, v_ref, o_ref, lse_ref = refs
        i = pl.program_id(0)

        def tile(j, masked, state):
            off = pl.multiple_of(j * T, T)
            ss = [_dot(k_ref[h * Dk:(h + 1) * Dk, pl.ds(off, T)], q_ref[h * Dk:(h + 1) * Dk, :], TN) * (scale * LOG2E)
                  for h in hs]
            if forget:
                ss = [ss[h] + (fq_ref[h] - fk_ref[pl.ds(off, T), h:h + 1]) for h in hs]
            if masked:
                r = lax.broadcasted_iota(jnp.int32, (T, T), 0)
                c = lax.broadcasted_iota(jnp.int32, (T, T), 1)
                ss = [jnp.where(r <= c, s, NEG_INF) for s in ss]
            m_new = [jnp.maximum(state[h][0], jnp.max(ss[h], axis=0, keepdims=True)) for h in hs]
            alpha = [jnp.exp2(state[h][0] - m_new[h]) for h in hs]
            ps = [jnp.exp2(ss[h] - m_new[h]) for h in hs]
            l_new = [alpha[h] * state[h][1] + jnp.sum(ps[h], axis=0, keepdims=True) for h in hs]
            p_hi = [p.astype(BF16) for p in ps]
            vs = [v_ref[h * Dv:(h + 1) * Dv, pl.ds(off, T)] for h in hs]
            pv = [_dot(vs[h], p_hi[h]) for h in hs]
            if forget:
                pv = [pv[h] + _dot(vs[h], (ps[h] - p_hi[h].astype(F32)).astype(BF16)) for h in hs]
            return tuple((m_new[h], l_new[h], alpha[h] * state[h][2] + pv[h]) for h in hs)

        init = tuple((jnp.full((1, T), NEG_INF, F32), jnp.zeros((1, T), F32), jnp.zeros((Dv, T), F32)) for _ in hs)
        state = lax.fori_loop(0, i, lambda j, st: tile(j, False, st), init)
        state = tile(i, True, state)
        for h in hs:
            m, l, acc = state[h]
            o_ref[h * Dv:(h + 1) * Dv, :] = acc / l
            lse_ref[h] = m + jnp.log2(l)

    in_specs = [pl.BlockSpec((H * Dk, T), lambda i: (qb, i)),
                pl.BlockSpec((H * Dk, S), lambda i: (kb, 0)),
                pl.BlockSpec((H * Dv, S), lambda i: (vb, 0))]
    ins = [q_src, k_src, v_src]
    if forget:
        in_specs += [pl.BlockSpec((H, 1, T), lambda i: (0, 0, i)), pl.BlockSpec((S, H), lambda i: (0, 0))]
        ins += [f_row, f_col]
    return pl.pallas_call(
        body, name=name, grid=(nq,),
        in_specs=in_specs,
        out_specs=[pl.BlockSpec((H * Dv, T), lambda i: (0, i)), pl.BlockSpec((H, 1, T), lambda i: (0, 0, i))],
        out_shape=[jax.ShapeDtypeStruct((H * Dv, S), F32), jax.ShapeDtypeStruct((H, 1, S), F32)],
        compiler_params=_params("parallel"),
    )(*ins)


def _attn_bwd(q_src, k_src, v_src, rows, H, Dk, Dv, scale, do, lse, delta, f_row=None, f_col=None, *, name):
    S = q_src.shape[1]
    T = _tile(S, ATT_TILE)
    nq = S // T
    forget = f_row is not None
    qb, kb, vb, db = rows[0] // (H * Dk), rows[1] // (H * Dk), rows[2] // (H * Dv), rows[3] // H
    hs = range(H)

    def body(*refs):
        if forget:
            (q_ref, k_ref, v_ref, do_ref, lse_ref, dl_ref, fq_ref, fk_ref,
             dq_ref, dk_ref, dv_ref, df_ref, dk_s, dv_s, df_s) = refs
        else:
            q_ref, k_ref, v_ref, do_ref, lse_ref, dl_ref, dq_ref, dk_ref, dv_ref, dk_s, dv_s = refs
        j = pl.program_id(0)

        @pl.when(j == 0)
        def _():
            dq_ref[...] = jnp.zeros_like(dq_ref)

        dk_s[...] = jnp.zeros_like(dk_s)
        dv_s[...] = jnp.zeros_like(dv_s)
        if forget:
            df_s[...] = jnp.zeros_like(df_s)
        kt = [k_ref[h * Dk:(h + 1) * Dk, :] for h in hs]
        kj = [k.T for k in kt]
        vj = [v_ref[h * Dv:(h + 1) * Dv, :].T for h in hs]
        koff = pl.multiple_of(j * T, T)

        def tile(i, masked):
            cols = pl.ds(pl.multiple_of(i * T, T), T)
            qi = [q_ref[h * Dk:(h + 1) * Dk, cols] for h in hs]
            doi = [do_ref[h * Dv:(h + 1) * Dv, cols] for h in hs]
            st = [_dot(kj[h], qi[h]) * (scale * LOG2E) for h in hs]
            if forget:
                st = [st[h] + (fq_ref[h, :, cols] - fk_ref[pl.ds(koff, T), h:h + 1]) for h in hs]
            if masked:
                r = lax.broadcasted_iota(jnp.int32, (T, T), 0)
                c = lax.broadcasted_iota(jnp.int32, (T, T), 1)
                st = [jnp.where(r <= c, x, NEG_INF) for x in st]
            pt = [jnp.exp2(st[h] - lse_ref[h, :, cols]) for h in hs]
            dpt = [_dot(vj[h], doi[h]) for h in hs]
            dst = [pt[h] * (dpt[h] - dl_ref[h, :, cols]) for h in hs]
            ptb = [p.astype(BF16) for p in pt]
            dsb = [d.astype(BF16) for d in dst]
            for h in hs:
                dv_s[h * Dv:(h + 1) * Dv, :] += _dot(doi[h], ptb[h], NT)
            for h in hs:
                dk_s[h * Dk:(h + 1) * Dk, :] += _dot(qi[h], dsb[h], NT)
            for h in hs:
                dq_ref[h * Dk:(h + 1) * Dk, cols] += _dot(kt[h], dsb[h]) * scale
            if forget:
                for h in hs:
                    part = dst[h][:, 0:LANES]
                    for c0 in range(LANES, T, LANES):
                        part = part + dst[h][:, c0:c0 + LANES]
                    df_s[h] += part

        tile(j, True)

        def loop_body(i, carry):
            tile(i, False)
            return carry

        lax.fori_loop(j + 1, nq, loop_body, 0)
        dk_ref[...] = dk_s[...] * scale
        dv_ref[...] = dv_s[...]
        if forget:
            df_ref[...] = jnp.concatenate([-jnp.sum(df_s[h], axis=-1, keepdims=True) for h in hs], axis=1)

    res = lambda D, b0: pl.BlockSpec((H * D, S), lambda j: (b0, 0))
    blk = lambda D, b0: pl.BlockSpec((H * D, T), lambda j: (b0, j))
    row3 = lambda b0: pl.BlockSpec((H, 1, S), lambda j: (b0, 0, 0))
    in_specs = [res(Dk, qb), blk(Dk, kb), blk(Dv, vb), res(Dv, 0), row3(0), row3(db)]
    ins = [q_src, k_src, v_src, do, lse, delta]
    out_specs = [res(Dk, 0), blk(Dk, 0), blk(Dv, 0)]
    out_shape = [jax.ShapeDtypeStruct((H * Dk, S), F32), jax.ShapeDtypeStruct((H * Dk, S), F32),
                 jax.ShapeDtypeStruct((H * Dv, S), F32)]
    scratch = [pltpu.VMEM((H * Dk, T), F32), pltpu.VMEM((H * Dv, T), F32)]
    if forget:
        in_specs += [row3(0), pl.BlockSpec((S, H), lambda j: (0, 0))]
        ins += [f_row, f_col]
        out_specs.append(pl.BlockSpec((T, H), lambda j: (j, 0)))
        out_shape.append(jax.ShapeDtypeStruct((S, H), F32))
        scratch.append(pltpu.VMEM((H, T, min(T, LANES)), F32))
    return pl.pallas_call(
        body, name=name, grid=(nq,),
        in_specs=in_specs, out_specs=out_specs, out_shape=out_shape, scratch_shapes=scratch,
        compiler_params=_params("arbitrary"),
    )(*ins)


GW = SWA_GROUP * WINDOW


def _swa_masks(i):
    r = lax.broadcasted_iota(jnp.int32, (WINDOW, GW), 0)
    c = lax.broadcasted_iota(jnp.int32, (WINDOW, GW), 1) % WINDOW
    return (r > c) & (i > 0), r <= c


def _swa_specs():
    W = WINDOW
    kv_rows = SWA_KV_HEADS * HEAD_DIM
    q = pl.BlockSpec((SWA_Q_HEADS * HEAD_DIM, W), lambda i: (0, i))
    prev = lambda b: pl.BlockSpec((kv_rows, W), lambda i: (b, jnp.maximum(i - 1, 0)))
    cur = lambda b: pl.BlockSpec((kv_rows, W), lambda i: (b, i))
    bias = pl.BlockSpec((SWA_KV_HEADS, 2 * W, GW), lambda i: (0, 0, 0))
    stat = pl.BlockSpec((SWA_Q_HEADS, W), lambda i: (0, i))
    sink = pl.BlockSpec(memory_space=pltpu.SMEM)
    return q, prev(4), cur(4), prev(5), cur(5), bias, stat, sink


def _group_lanes(ref, g, rows_per_head):
    h0 = g * SWA_GROUP
    return jnp.concatenate([ref[(h0 + j) * rows_per_head:(h0 + j + 1) * rows_per_head, :] for j in range(SWA_GROUP)], axis=1)


def _swa_scores(g, q_ref, kp_ref, kc_ref, b_ref, masks):
    rows = slice(g * HEAD_DIM, (g + 1) * HEAD_DIM)
    qg = _group_lanes(q_ref, g, HEAD_DIM)
    scale = HEAD_DIM ** -0.5
    s_p = jnp.where(masks[0], _dot(kp_ref[rows, :], qg, TN) * scale + b_ref[g, 0:WINDOW, :], NEG_INF)
    s_c = jnp.where(masks[1], _dot(kc_ref[rows, :], qg, TN) * scale + b_ref[g, WINDOW:2 * WINDOW, :], NEG_INF)
    return qg, rows, s_p, s_c


def _sink_row(sink_ref, g):
    return jnp.concatenate([jnp.full((1, WINDOW), sink_ref[g * SWA_GROUP + j], F32) for j in range(SWA_GROUP)], axis=1)


def _swa_fwd(qkv, bias_g, sinks, *, name):
    S = qkv.shape[1]
    qs, kp, kc, vp, vc, bs, stat, sk = _swa_specs()
    gs = range(SWA_KV_HEADS)

    def body(sink_ref, q_ref, kp_ref, kc_ref, vp_ref, vc_ref, b_ref, o_ref, lse_ref):
        masks = _swa_masks(pl.program_id(0))
        sc = [_swa_scores(g, q_ref, kp_ref, kc_ref, b_ref, masks) for g in gs]
        sinks_g = [_sink_row(sink_ref, g) for g in gs]
        m = [jnp.maximum(jnp.maximum(jnp.max(sc[g][2], axis=0, keepdims=True), jnp.max(sc[g][3], axis=0, keepdims=True)),
                         sinks_g[g]) for g in gs]
        p_p = [jnp.exp(sc[g][2] - m[g]) for g in gs]
        p_c = [jnp.exp(sc[g][3] - m[g]) for g in gs]
        l = [jnp.sum(p_p[g], axis=0, keepdims=True) + jnp.sum(p_c[g], axis=0, keepdims=True) + jnp.exp(sinks_g[g] - m[g])
             for g in gs]
        o = [_dot(vp_ref[sc[g][1], :], p_p[g].astype(BF16)) + _dot(vc_ref[sc[g][1], :], p_c[g].astype(BF16)) for g in gs]
        for g in gs:
            og = o[g] / l[g]
            lse = m[g] + jnp.log(l[g])
            for j in range(SWA_GROUP):
                h = g * SWA_GROUP + j
                o_ref[h * HEAD_DIM:(h + 1) * HEAD_DIM, :] = og[:, j * WINDOW:(j + 1) * WINDOW]
                lse_ref[h:h + 1, :] = lse[:, j * WINDOW:(j + 1) * WINDOW]

    return pl.pallas_call(
        body, name=name, grid=(S // WINDOW,),
        in_specs=[sk, qs, kp, kc, vp, vc, bs],
        out_specs=[qs, stat],
        out_shape=[jax.ShapeDtypeStruct((SWA_Q_HEADS * HEAD_DIM, S), F32), jax.ShapeDtypeStruct((SWA_Q_HEADS, S), F32)],
        compiler_params=_params("parallel"),
    )(sinks, qkv, qkv, qkv, qkv, qkv, bias_g)


def _swa_bwd(qkv, bias_g, sinks, do, lse, delta, *, name):
    S = qkv.shape[1]
    W = WINDOW
    qs, kp, kc, vp, vc, bs, stat, sk = _swa_specs()
    scale = HEAD_DIM ** -0.5
    kv_rows = SWA_KV_HEADS * HEAD_DIM
    gs = range(SWA_KV_HEADS)

    def body(sink_ref, q_ref, kp_ref, kc_ref, vp_ref, vc_ref, b_ref, do_ref, lse_ref, dl_ref,
             dq_ref, dkv_ref, db_ref, dsk_ref):
        i = pl.program_id(0)

        @pl.when(i == 0)
        def _():
            dkv_ref[...] = jnp.zeros_like(dkv_ref)
            db_ref[...] = jnp.zeros_like(db_ref)
            dsk_ref[...] = jnp.zeros_like(dsk_ref)

        masks = _swa_masks(i)
        prev = pl.ds(pl.multiple_of(jnp.maximum(i - 1, 0) * W, W), W)
        cur = pl.ds(pl.multiple_of(i * W, W), W)
        sc = [_swa_scores(g, q_ref, kp_ref, kc_ref, b_ref, masks) for g in gs]
        dog = [_group_lanes(do_ref, g, HEAD_DIM) for g in gs]
        lse = [_group_lanes(lse_ref, g, 1) for g in gs]
        dl = [_group_lanes(dl_ref, g, 1) for g in gs]
        p_p = [jnp.exp(sc[g][2] - lse[g]) for g in gs]
        p_c = [jnp.exp(sc[g][3] - lse[g]) for g in gs]
        ds_p = [p_p[g] * (_dot(vp_ref[sc[g][1], :], dog[g], TN) - dl[g]) for g in gs]
        ds_c = [p_c[g] * (_dot(vc_ref[sc[g][1], :], dog[g], TN) - dl[g]) for g in gs]
        for g in gs:
            db_ref[g, 0:W, :] += ds_p[g]
            db_ref[g, W:2 * W, :] += ds_c[g]
            dsk = jnp.exp(_sink_row(sink_ref, g) - lse[g]) * dl[g]
            for j in range(SWA_GROUP):
                h = g * SWA_GROUP + j
                dsk_ref[h:h + 1, :] -= jnp.broadcast_to(jnp.sum(dsk[:, j * W:(j + 1) * W], axis=1, keepdims=True), (1, LANES))
        dsb_p = [d.astype(BF16) for d in ds_p]
        dsb_c = [d.astype(BF16) for d in ds_c]
        for g in gs:
            rows = sc[g][1]
            dq = (_dot(kp_ref[rows, :], dsb_p[g]) + _dot(kc_ref[rows, :], dsb_c[g])) * scale
            for j in range(SWA_GROUP):
                h = g * SWA_GROUP + j
                dq_ref[h * HEAD_DIM:(h + 1) * HEAD_DIM, :] = dq[:, j * W:(j + 1) * W]
        for g in gs:
            rows = sc[g][1]
            vrows = slice(kv_rows + rows.start, kv_rows + rows.stop)
            dkv_ref[rows, prev] += _dot(sc[g][0], dsb_p[g], NT) * scale
            dkv_ref[rows, cur] += _dot(sc[g][0], dsb_c[g], NT) * scale
            dkv_ref[vrows, prev] += _dot(dog[g], p_p[g].astype(BF16), NT)
            dkv_ref[vrows, cur] += _dot(dog[g], p_c[g].astype(BF16), NT)

    return pl.pallas_call(
        body, name=name, grid=(S // W,),
        in_specs=[sk, qs, kp, kc, vp, vc, bs, qs, stat, stat],
        out_specs=[qs, pl.BlockSpec((2 * kv_rows, S), lambda i: (0, 0)), bs, pl.BlockSpec((SWA_Q_HEADS, LANES), lambda i: (0, 0))],
        out_shape=[jax.ShapeDtypeStruct((SWA_Q_HEADS * HEAD_DIM, S), F32), jax.ShapeDtypeStruct((2 * kv_rows, S), F32),
                   jax.ShapeDtypeStruct((SWA_KV_HEADS, 2 * W, GW), F32), jax.ShapeDtypeStruct((SWA_Q_HEADS, LANES), F32)],
        compiler_params=_params("arbitrary"),
    )(sinks, qkv, qkv, qkv, qkv, qkv, bias_g, do, lse, delta)


def _rel_onehot_t():
    qi = jnp.arange(WINDOW, dtype=jnp.int32)[None, :] + WINDOW
    kj = jnp.arange(2 * WINDOW, dtype=jnp.int32)[:, None]
    dist = qi - kj
    max_exact = REL_BUCKETS // 2
    d = jnp.maximum(dist, 0)
    log_ratio = jnp.log(jnp.maximum(d, 1).astype(F32) / max_exact) / math.log(REL_MAX_DIST / max_exact)
    large = jnp.minimum(max_exact + (log_ratio * (REL_BUCKETS - max_exact)).astype(jnp.int32), REL_BUCKETS - 1)
    bucket = jnp.where(d < max_exact, d, large).reshape(-1)
    return (bucket[None, :] == jnp.arange(REL_BUCKETS, dtype=jnp.int32)[:, None]).astype(BF16)


def _bias_table(rel_bias_t, onehot_t):
    Hq, NB = rel_bias_t.shape
    N = onehot_t.shape[1]
    tn = _tile(N, 4096)

    def body(r_ref, oh_ref, o_ref):
        oh = oh_ref[...]
        a1, a2, a3 = _split3(r_ref[...])
        o_ref[...] = _dot(a1, oh) + _dot(a2, oh) + _dot(a3, oh)

    return pl.pallas_call(
        body, name="rel_bias_table", grid=(N // tn,),
        in_specs=[pl.BlockSpec((Hq, NB), lambda j: (0, 0)), pl.BlockSpec((NB, tn), lambda j: (0, j))],
        out_specs=pl.BlockSpec((Hq, tn), lambda j: (0, j)),
        out_shape=jax.ShapeDtypeStruct((Hq, N), F32),
        compiler_params=_params("parallel"),
    )(rel_bias_t, onehot_t)


def _bias_table_bwd(dbias, onehot_t):
    L, Hq, N = dbias.shape
    NB = onehot_t.shape[0]
    tn = _tile(N, 4096)

    def body(d_ref, oh_ref, o_ref):
        @pl.when(pl.program_id(0) == 0)
        def _():
            o_ref[...] = jnp.zeros_like(o_ref)

        d = d_ref[0]
        for l in range(1, L):
            d = d + d_ref[l]
        oh = oh_ref[...]
        a1, a2, a3 = _split3(d)
        o_ref[...] += _dot(a1, oh, NT) + _dot(a2, oh, NT) + _dot(a3, oh, NT)

    return pl.pallas_call(
        body, name="rel_bias_bwd", grid=(N // tn,),
        in_specs=[pl.BlockSpec((L, Hq, tn), lambda j: (0, 0, j)), pl.BlockSpec((NB, tn), lambda j: (0, j))],
        out_specs=pl.BlockSpec((Hq, NB), lambda j: (0, 0)),
        out_shape=jax.ShapeDtypeStruct((Hq, NB), F32),
        compiler_params=_params("arbitrary"),
    )(dbias, onehot_t)


def _gate_fwd(lat, fb_col, *, name):
    S = lat.shape[1]
    tn = _tile(S, 256)

    def body(z_ref, fb_ref, o_ref, carry):
        @pl.when(pl.program_id(0) == 0)
        def _():
            carry[...] = jnp.zeros_like(carry)

        z = z_ref[...] + fb_ref[...]
        lf = jnp.minimum(z, 0.0) - jnp.log1p(jnp.exp(-jnp.abs(z)))
        r = lax.broadcasted_iota(jnp.int32, (tn, tn), 0)
        c = lax.broadcasted_iota(jnp.int32, (tn, tn), 1)
        tri = (r <= c).astype(BF16)
        a1, a2, a3 = _split3(lf)
        cum = _dot(a1, tri) + _dot(a2, tri) + _dot(a3, tri) + carry[:, 0:1]
        o_ref[...] = cum
        carry[...] = jnp.broadcast_to(cum[:, tn - 1:tn], carry.shape)

    return pl.pallas_call(
        body, name=name, grid=(S // tn,),
        in_specs=[pl.BlockSpec((GATE_ROWS, tn), lambda i: (0, i)), pl.BlockSpec((GATE_ROWS, 1), lambda i: (0, 0))],
        out_specs=pl.BlockSpec((GATE_ROWS, tn), lambda i: (0, i)),
        out_shape=jax.ShapeDtypeStruct((GATE_ROWS, S), F32),
        scratch_shapes=[pltpu.VMEM((GATE_ROWS, LANES), F32)],
        compiler_params=_params("arbitrary"),
    )(lat, fb_col)


def _gate_bwd(lat, fb_col, dF, *, name):
    S = lat.shape[1]
    tn = _tile(S, 256)
    nt = S // tn

    def body(z_ref, fb_ref, df_ref, dz_ref, dfb_ref, carry):
        @pl.when(pl.program_id(0) == 0)
        def _():
            carry[...] = jnp.zeros_like(carry)
            dfb_ref[...] = jnp.zeros_like(dfb_ref)

        r = lax.broadcasted_iota(jnp.int32, (tn, tn), 0)
        c = lax.broadcasted_iota(jnp.int32, (tn, tn), 1)
        tri = (r >= c).astype(BF16)
        a1, a2, a3 = _split3(df_ref[...])
        dlf = _dot(a1, tri) + _dot(a2, tri) + _dot(a3, tri) + carry[:, 0:1]
        carry[...] = jnp.broadcast_to(dlf[:, 0:1], carry.shape)
        z = z_ref[...] + fb_ref[...]
        row = lax.broadcasted_iota(jnp.int32, (GATE_ROWS, tn), 0)
        dz = jnp.where(row < FOX_HEADS, dlf / (1.0 + jnp.exp(z)), 0.0)
        dz_ref[...] = dz
        dfb_ref[...] += jnp.sum(dz, axis=1, keepdims=True)

    blk = pl.BlockSpec((GATE_ROWS, tn), lambda i: (0, nt - 1 - i))
    vec = pl.BlockSpec((GATE_ROWS, 1), lambda i: (0, 0))
    return pl.pallas_call(
        body, name=name, grid=(nt,),
        in_specs=[blk, vec, blk], out_specs=[blk, vec],
        out_shape=[jax.ShapeDtypeStruct((GATE_ROWS, S), F32), jax.ShapeDtypeStruct((GATE_ROWS, 1), F32)],
        scratch_shapes=[pltpu.VMEM((GATE_ROWS, LANES), F32)],
        compiler_params=_params("arbitrary"),
    )(lat, fb_col, dF)


def _rope_tables(S):
    pos = jnp.arange(S, dtype=F32)
    inv_freq = ROPE_THETA ** (-(jnp.arange(MLA_ROPE // 2, dtype=F32) * 2.0 / MLA_ROPE))
    ang = pos[:, None] * inv_freq[None, :]
    cos, sin = jnp.cos(ang).T, jnp.sin(ang).T
    z16 = jnp.zeros_like(cos)

    def slab(lo, fill):
        def put(first, second, f):
            return jnp.concatenate([jnp.full((lo, S), f, F32), first, second, jnp.full((LANES - lo - MLA_ROPE, S), f, F32)], axis=0)
        return put(cos, cos, fill), put(-sin, z16, 0.0), put(z16, sin, 0.0)

    tq = tuple(jnp.tile(t, (MLA_HEADS, 1)) for t in slab(MLA_NOPE, 1.0))
    return tq, slab(0, 0.0)


def _rope(x, c, s1, s2):
    n = x.shape[0]
    half = MLA_ROPE // 2
    return x * c + pltpu.roll(x, n - half, 0) * s1 + pltpu.roll(x, half, 0) * s2


def _rope_t(dy, c, s1, s2):
    n = dy.shape[0]
    half = MLA_ROPE // 2
    return dy * c + pltpu.roll(dy * s1, half, 0) + pltpu.roll(dy * s2, n - half, 0)


KR_SLAB0 = MLA_Q_RANK + MLA_KV_RANK


def _mla_prep_fwd(lat, g_q, g_kv, w_uq_t, w_ukv_t, tq, tmisc, *, name):
    S = lat.shape[1]
    tn = _tile(S, 512)
    QW = MLA_HEADS * MLA_PAD

    def body(lat_ref, gq_ref, gkv_ref, wq_ref, wkv_ref, c_ref, s1_ref, s2_ref, cm_ref, s1m_ref, s2m_ref,
             nq_ref, nkv_ref, q_ref, k_ref, v_ref):
        x = pltpu.roll(lat_ref[...], LAT_ROWS - LAT_SHIFT, 0)
        nq = _col_rms(x[0:MLA_Q_RANK, :], gq_ref[...]).astype(BF16)
        nkv = _col_rms(x[MLA_Q_RANK:KR_SLAB0, :], gkv_ref[...]).astype(BF16)
        nq_ref[...] = nq
        nkv_ref[...] = nkv
        q_ref[...] = _rope(_dot(wq_ref[...], nq), c_ref[...], s1_ref[...], s2_ref[...]).astype(BF16)
        kv = _dot(wkv_ref[...], nkv).astype(BF16)
        kr = _rope(x[KR_SLAB0:LAT_ROWS, :], cm_ref[...], s1m_ref[...], s2m_ref[...]).astype(BF16)
        for h in range(MLA_HEADS):
            k_ref[h * MLA_PAD:h * MLA_PAD + MLA_NOPE, :] = kv[h * LANES:h * LANES + MLA_NOPE, :]
            k_ref[h * MLA_PAD + MLA_NOPE:(h + 1) * MLA_PAD, :] = kr[0:MLA_PAD - MLA_NOPE, :]
            v_ref[h * HEAD_DIM:(h + 1) * HEAD_DIM, :] = kv[h * LANES + MLA_NOPE:(h + 1) * LANES, :]

    def col(rows):
        return pl.BlockSpec((rows, tn), lambda i: (0, i))

    def full(a):
        return pl.BlockSpec(a.shape, lambda i: (0, 0))

    return pl.pallas_call(
        body, name=name, grid=(S // tn,),
        in_specs=[col(LAT_ROWS), full(g_q), full(g_kv), full(w_uq_t), full(w_ukv_t),
                  col(QW), col(QW), col(QW), col(LANES), col(LANES), col(LANES)],
        out_specs=[col(MLA_Q_RANK), col(MLA_KV_RANK), col(QW), col(QW), col(MLA_HEADS * HEAD_DIM)],
        out_shape=[jax.ShapeDtypeStruct((MLA_Q_RANK, S), BF16), jax.ShapeDtypeStruct((MLA_KV_RANK, S), BF16),
                   jax.ShapeDtypeStruct((QW, S), BF16), jax.ShapeDtypeStruct((QW, S), BF16),
                   jax.ShapeDtypeStruct((MLA_HEADS * HEAD_DIM, S), BF16)],
        compiler_params=_params("parallel"),
    )(lat, g_q, g_kv, w_uq_t, w_ukv_t, *tq, *tmisc)


def _mla_prep_bwd(lat, nq, nkv, g_q, g_kv, w_uq_p, w_ukv, tq, tmisc, dq, dk, dv, dflog, *, name):
    S = lat.shape[1]
    tn = _tile(S, 512)
    QW = MLA_HEADS * MLA_PAD

    def body(lat_ref, nq_ref, nkv_ref, gq_ref, gkv_ref, wq_ref, wkv_ref, c_ref, s1_ref, s2_ref,
             cm_ref, s1m_ref, s2m_ref, dq_ref, dk_ref, dv_ref, dfl_ref,
             dlat_ref, dwq_ref, dwkv_ref, dgq_ref, dgkv_ref, y_s):
        @pl.when(pl.program_id(0) == 0)
        def _():
            dwq_ref[...] = jnp.zeros_like(dwq_ref)
            dwkv_ref[...] = jnp.zeros_like(dwkv_ref)
            dgq_ref[...] = jnp.zeros_like(dgq_ref)
            dgkv_ref[...] = jnp.zeros_like(dgkv_ref)

        x = pltpu.roll(lat_ref[...], LAT_ROWS - LAT_SHIFT, 0)
        dqm = _rope_t(dq_ref[...], c_ref[...], s1_ref[...], s2_ref[...]).astype(BF16)
        dwq_ref[...] += _dot(dqm, nq_ref[...], NT)
        dx, dg = _col_rms_bwd(x[0:MLA_Q_RANK, :], gq_ref[...], _dot(wq_ref[...], dqm))
        y_s[0:MLA_Q_RANK, :] = dx
        dgq_ref[...] += dg
        dkv = jnp.concatenate(
            [part for h in range(MLA_HEADS)
             for part in (dk_ref[h * MLA_PAD:h * MLA_PAD + MLA_NOPE, :], dv_ref[h * HEAD_DIM:(h + 1) * HEAD_DIM, :])],
            axis=0).astype(BF16)
        dwkv_ref[...] += _dot(dkv, nkv_ref[...], NT)
        dx, dg = _col_rms_bwd(x[MLA_Q_RANK:KR_SLAB0, :], gkv_ref[...], _dot(wkv_ref[...], dkv))
        y_s[MLA_Q_RANK:KR_SLAB0, :] = dx
        dgkv_ref[...] += dg
        dkr = dk_ref[MLA_NOPE:MLA_PAD, :]
        for h in range(1, MLA_HEADS):
            dkr = dkr + dk_ref[h * MLA_PAD + MLA_NOPE:(h + 1) * MLA_PAD, :]
        dkr = jnp.concatenate([dkr, jnp.zeros((MLA_NOPE, tn), F32)], axis=0)
        y_s[KR_SLAB0:LAT_ROWS, :] = _rope_t(dkr, cm_ref[...], s1m_ref[...], s2m_ref[...])
        y = pltpu.roll(y_s[...], LAT_SHIFT, 0)
        row = lax.broadcasted_iota(jnp.int32, (LAT_ROWS, tn), 0)
        dfl = jnp.concatenate([dfl_ref[...], jnp.zeros((LAT_ROWS - GATE_ROWS, tn), F32)], axis=0)
        dlat_ref[...] = jnp.where(row < LAT_SHIFT, dfl, y).astype(BF16)

    def col(rows):
        return pl.BlockSpec((rows, tn), lambda i: (0, i))

    def full(a):
        return pl.BlockSpec(a.shape, lambda i: (0, 0))

    def acc(r, c):
        return pl.BlockSpec((r, c), lambda i: (0, 0))

    return pl.pallas_call(
        body, name=name, grid=(S // tn,),
        in_specs=[col(LAT_ROWS), col(MLA_Q_RANK), col(MLA_KV_RANK), full(g_q), full(g_kv),
                  full(w_uq_p), full(w_ukv), col(QW), col(QW), col(QW), col(LANES), col(LANES), col(LANES),
                  col(QW), col(QW), col(MLA_HEADS * HEAD_DIM), col(GATE_ROWS)],
        out_specs=[col(LAT_ROWS), acc(QW, MLA_Q_RANK), acc(QW, MLA_KV_RANK), acc(MLA_Q_RANK, 1), acc(MLA_KV_RANK, 1)],
        out_shape=[jax.ShapeDtypeStruct((LAT_ROWS, S), BF16), jax.ShapeDtypeStruct((QW, MLA_Q_RANK), F32),
                   jax.ShapeDtypeStruct((QW, MLA_KV_RANK), F32), jax.ShapeDtypeStruct((MLA_Q_RANK, 1), F32),
                   jax.ShapeDtypeStruct((MLA_KV_RANK, 1), F32)],
        scratch_shapes=[pltpu.VMEM((LAT_ROWS, tn), F32)],
        compiler_params=_params("arbitrary"),
    )(lat, nq, nkv, g_q, g_kv, w_uq_p, w_ukv, *tq, *tmisc, dq, dk, dv, dflog)


def _dproj_cast(dqa, dkva, dqf, dkf, dvf, dlat, *, name):
    S = dqa.shape[1]
    tn = _tile(S, 512)
    parts = (dqa, dkva, dqf, dkf, dvf, dlat)

    def body(*refs):
        o_ref = refs[-1]
        r0 = 0
        for ref in refs[:-1]:
            n = ref.shape[0]
            o_ref[r0:r0 + n, :] = ref[...].astype(BF16)
            r0 += n

    return pl.pallas_call(
        body, name=name, grid=(S // tn,),
        in_specs=[pl.BlockSpec((p.shape[0], tn), lambda i: (0, i)) for p in parts],
        out_specs=pl.BlockSpec((IN_ROWS, tn), lambda i: (0, i)),
        out_shape=jax.ShapeDtypeStruct((IN_ROWS, S), BF16),
        compiler_params=_params("parallel"),
    )(*parts)


GELU_C = math.sqrt(2.0 / math.pi)
GELU_A = 0.044715


HALO = 16


def _shift_down(a, k, fill):
    r = pltpu.roll(a, k, 0)
    row = lax.broadcasted_iota(jnp.int32, (8, a.shape[1]), 0)
    head = r[0:8, :]
    for i in range(k):
        head = jnp.where(row == i, fill[len(fill) - k + i], head)
    return jnp.concatenate([head, r[8:, :]], axis=0)


def _shift_up(d, k, fill):
    n = d.shape[0]
    r = pltpu.roll(d, n - k, 0)
    row = lax.broadcasted_iota(jnp.int32, (8, d.shape[1]), 0)
    tail = r[n - 8:n, :]
    for i in range(k):
        tail = jnp.where(row == 8 - k + i, fill[i], tail)
    return jnp.concatenate([r[0:n - 8, :], tail], axis=0)


def _conv_taps(a, before, w_ref, b_ref):
    a1 = _shift_down(a, 1, before)
    a2 = _shift_down(a, 2, before)
    return ((b_ref[...] + w_ref[0:1, :] * a2) + w_ref[1:2, :] * a1) + w_ref[2:3, :] * a


def _rows_before(halo_ref, first):
    h = halo_ref[HALO - 2:HALO, :].astype(F32)
    return jnp.where(first, 0.0, h[0:1, :]), jnp.where(first, 0.0, h[1:2, :])


def _conv_specs(S, tm, tc, nc):
    hb = tm // HALO
    main = lambda off: pl.BlockSpec((tm, tc), lambda j, i: (i, j + off))
    prev = lambda off: pl.BlockSpec((HALO, tc), lambda j, i: (jnp.maximum(i * hb - 1, 0), j + off))
    wspec = lambda off: pl.BlockSpec((3, tc), lambda j, i: (0, j + off))
    bspec = lambda off: pl.BlockSpec((1, tc), lambda j, i: (0, j + off))
    return main, prev, wspec, bspec


def _conv_geglu_fwd(a, conv_w, conv_b, *, name):
    S = a.shape[0]
    tm, tc = _tile(S, 512), _tile(D_FF, 1408)
    nc = D_FF // tc
    main, prev, wspec, bspec = _conv_specs(S, tm, tc, nc)

    def body(ag_ref, au_ref, hg_ref, hu_ref, wg_ref, wu_ref, bg_ref, bu_ref, u_ref, z_ref):
        first = pl.program_id(1) == 0
        gate = _conv_taps(ag_ref[...].astype(F32), _rows_before(hg_ref, first), wg_ref, bg_ref)
        up = _conv_taps(au_ref[...].astype(F32), _rows_before(hu_ref, first), wu_ref, bu_ref)
        u_ref[0] = gate
        u_ref[1] = up
        cdf = 0.5 * (1.0 + jnp.tanh(GELU_C * (gate + GELU_A * (gate * gate * gate))))
        z_ref[...] = (gate * cdf * up).astype(BF16)

    return pl.pallas_call(
        body, name=name, grid=(nc, S // tm),
        in_specs=[main(0), main(nc), prev(0), prev(nc), wspec(0), wspec(nc), bspec(0), bspec(nc)],
        out_specs=[pl.BlockSpec((2, tm, tc), lambda j, i: (0, i, j)), pl.BlockSpec((tm, tc), lambda j, i: (i, j))],
        out_shape=[jax.ShapeDtypeStruct((2, S, D_FF), F32), jax.ShapeDtypeStruct((S, D_FF), BF16)],
        compiler_params=_params("parallel", "arbitrary"),
    )(a, a, a, a, conv_w, conv_w, conv_b, conv_b)


def _geglu_bwd(gate, up, dz):
    g2x = gate * gate
    th = jnp.tanh(GELU_C * (gate + GELU_A * (g2x * gate)))
    cdf = 0.5 * (1.0 + th)
    dgelu = cdf + gate * (0.5 * (1.0 - th * th) * (GELU_C * (1.0 + 3.0 * GELU_A * g2x)))
    return dz * up * dgelu, dz * (gate * cdf)


def _conv_geglu_bwd(a, u, conv_w, dz, *, name):
    S = a.shape[0]
    tm, tc = _tile(S, 512), _tile(D_FF, 1408)
    nc = D_FF // tc
    nr = S // tm
    main, _, wspec, _ = _conv_specs(S, tm, tc, nc)
    hb = tm // 8

    def body(ag_ref, au_ref, u_ref, un_ref, wg_ref, wu_ref, dz_ref, dzn_ref, da_ref, dw_ref, db_ref):
        i = pl.program_id(1)
        last = i == nr - 1

        @pl.when(i == 0)
        def _():
            dw_ref[...] = jnp.zeros_like(dw_ref)
            db_ref[...] = jnp.zeros_like(db_ref)

        dus = _geglu_bwd(u_ref[0], u_ref[1], dz_ref[...])
        dus_n = _geglu_bwd(un_ref[0], un_ref[1], dzn_ref[...])
        for half, a_ref, w_ref in ((0, ag_ref, wg_ref), (1, au_ref, wu_ref)):
            du, du_n = dus[half], dus_n[half]
            after = (jnp.where(last, 0.0, du_n[0:1, :]), jnp.where(last, 0.0, du_n[1:2, :]))
            shifted = (_shift_up(du, 2, after), _shift_up(du, 1, after), du)
            da_ref[half] = (w_ref[2:3, :] * du + w_ref[1:2, :] * shifted[1] + w_ref[0:1, :] * shifted[0]).astype(BF16)
            af = a_ref[...].astype(F32)
            for tap in range(3):
                dw_ref[half, tap:tap + 1, :] += jnp.sum(shifted[tap] * af, axis=0, keepdims=True)
            db_ref[half] += jnp.sum(du, axis=0, keepdims=True)

    nxt8 = lambda j, i: (0, jnp.minimum((i + 1) * hb, S // 8 - 1), j)
    return pl.pallas_call(
        body, name=name, grid=(nc, nr),
        in_specs=[main(0), main(nc), pl.BlockSpec((2, tm, tc), lambda j, i: (0, i, j)), pl.BlockSpec((2, 8, tc), nxt8),
                  wspec(0), wspec(nc), pl.BlockSpec((tm, tc), lambda j, i: (i, j)),
                  pl.BlockSpec((8, tc), lambda j, i: (jnp.minimum((i + 1) * hb, S // 8 - 1), j))],
        out_specs=[pl.BlockSpec((2, tm, tc), lambda j, i: (0, i, j)), pl.BlockSpec((2, 3, tc), lambda j, i: (0, 0, j)),
                   pl.BlockSpec((2, 1, tc), lambda j, i: (0, 0, j))],
        out_shape=[jax.ShapeDtypeStruct((2, S, D_FF), BF16), jax.ShapeDtypeStruct((2, 3, D_FF), F32),
                   jax.ShapeDtypeStruct((2, 1, D_FF), F32)],
        compiler_params=_params("parallel", "arbitrary"),
    )(a, a, u, u, conv_w, conv_w, dz, dz)


ROW_BLOCK_BYTES = 1536 * 1024


def _row_tile(rows, cols):
    return rows if rows * cols * 4 <= ROW_BLOCK_BYTES else _tile(rows, ROW_TILE)


def _adamw_update(w, g, m, v):
    m = ADAM_B1 * m + (1.0 - ADAM_B1) * g
    v = ADAM_B2 * v + (1.0 - ADAM_B2) * jnp.square(g)
    m_hat = m / (1.0 - ADAM_B1 ** ADAM_STEP)
    v_hat = v / (1.0 - ADAM_B2 ** ADAM_STEP)
    return -ADAM_LR * (m_hat / (jnp.sqrt(v_hat) + ADAM_EPS) + ADAM_WD * w), m, v


def _adamw(w, g, m, v, *, name):
    L, A, B = w.shape
    ta = _tile(A, ROW_TILE)

    def body(w_ref, g_ref, m_ref, v_ref, d_ref, mo_ref, vo_ref):
        d_ref[...], mo_ref[...], vo_ref[...] = _adamw_update(w_ref[...], g_ref[...], m_ref[...], v_ref[...])

    blk = pl.BlockSpec((None, ta, B), lambda l, i: (l, i, 0))
    shp = jax.ShapeDtypeStruct((L, A, B), F32)
    return pl.pallas_call(
        body, name=name, grid=(L, A // ta),
        in_specs=[blk] * 4, out_specs=[blk] * 3, out_shape=[shp] * 3,
        compiler_params=_params("parallel", "parallel"),
    )(w, g, m, v)


def _scalar(v):
    return jnp.reshape(v, (1,)).astype(jnp.int32)


def _adamw_halves(w, g_mine, g_other, m, v, *, name):
    L, A, B = w.shape
    ta = _row_tile(A // 2, B)
    nb = A // 2 // ta

    def body(c_ref, w_ref, gm_ref, go_ref, m_ref, v_ref, g_ref, d_ref, mo_ref, vo_ref):
        g = jnp.where(pl.program_id(1) // nb == c_ref[0], gm_ref[...], go_ref[...])
        g_ref[...] = g
        d_ref[...], mo_ref[...], vo_ref[...] = _adamw_update(w_ref[...], g, m_ref[...], v_ref[...])

    blk = pl.BlockSpec((None, ta, B), lambda l, i, c_ref: (l, i, 0))
    half = pl.BlockSpec((None, ta, B), lambda l, i, c_ref: (l, i % nb, 0))
    shp = jax.ShapeDtypeStruct((L, A, B), F32)
    return pl.pallas_call(
        body, name=name,
        grid_spec=pltpu.PrefetchScalarGridSpec(num_scalar_prefetch=1, grid=(L, A // ta),
                                               in_specs=[blk, half, half, blk, blk], out_specs=[blk] * 4),
        out_shape=[shp] * 4,
        compiler_params=_params("parallel", "parallel"),
    )(_scalar(lax.axis_index("c")), w, g_mine, g_other, m, v)


def _chip_index():
    return 2 * lax.axis_index("x") + lax.axis_index("y")


def _pair_sum(g, recv, *, name):
    n, A, B = g.shape
    ta = _row_tile(A // 2, B)
    nb = A // 2 // ta

    def body(c_ref, g_ref, r_ref, o_ref):
        o_ref[...] = g_ref[...] + r_ref[...]

    return pl.pallas_call(
        body, name=name,
        grid_spec=pltpu.PrefetchScalarGridSpec(
            num_scalar_prefetch=1, grid=(n, nb),
            in_specs=[pl.BlockSpec((None, ta, B), lambda s, r, c_ref: (s, c_ref[0] * nb + r, 0)),
                      pl.BlockSpec((None, ta, B), lambda s, r, c_ref: (s, r, 0))],
            out_specs=pl.BlockSpec((None, ta, B), lambda s, r, c_ref: (s, r, 0))),
        out_shape=jax.ShapeDtypeStruct((n, A // 2, B), F32),
        compiler_params=_params("parallel", "parallel"),
    )(_scalar(lax.axis_index("c")), g, recv)


def _chip_sum(landed, own, *, name):
    n, A2, B = landed.shape
    ta = _row_tile(A2, B)

    def body(me_ref, *refs):
        slots, own_ref, o_ref = refs[:n], refs[n], refs[n + 1]
        parts = [jnp.where(me_ref[0] == s, own_ref[...], slots[s][...]) for s in range(n)]
        o_ref[...] = ((parts[0] + parts[1]) + parts[2]) + parts[3]

    def slot(s):
        return pl.BlockSpec((None, ta, B), lambda r, me_ref: (jnp.where(me_ref[0] == s, (s + 1) % n, s), r, 0))

    return pl.pallas_call(
        body, name=name,
        grid_spec=pltpu.PrefetchScalarGridSpec(
            num_scalar_prefetch=1, grid=(A2 // ta,),
            in_specs=[slot(s) for s in range(n)] + [pl.BlockSpec((None, ta, B), lambda r, me_ref: (me_ref[0], r, 0))],
            out_specs=pl.BlockSpec((ta, B), lambda r, me_ref: (r, 0))),
        out_shape=jax.ShapeDtypeStruct((A2, B), F32),
        compiler_params=_params("parallel"),
    )(_scalar(_chip_index()), *([landed] * n), own)


HBM_SPEC = pl.BlockSpec(memory_space=pl.ANY)
COMM_PARAMS = pltpu.CompilerParams(has_side_effects=True)


def _mesh_pos():
    return lax.axis_index("x"), lax.axis_index("y"), lax.axis_index("c")


def _other_chips(x, y):
    return [(1 - x, y), (x, 1 - y), (1 - x, 1 - y)]


def _remote(src, dst, send_sems, recv_sems, k, to):
    return pltpu.make_async_remote_copy(src_ref=src, dst_ref=dst, send_sem=send_sems.at[k], recv_sem=recv_sems.at[k],
                                        device_id=to, device_id_type=MESH)


def _place_own(gathered, shards, *, name):
    n = len(shards)

    def body(me_ref, *refs):
        for s_ref, o_ref in zip(refs[:n], refs[2 * n:]):
            o_ref[...] = s_ref[...]

    return pl.pallas_call(
        body, name=name,
        grid_spec=pltpu.PrefetchScalarGridSpec(
            num_scalar_prefetch=1, grid=(1,),
            in_specs=[pl.BlockSpec(s.shape, lambda i, me_ref: (0, 0)) for s in shards] + [HBM_SPEC] * n,
            out_specs=[pl.BlockSpec((None,) + s.shape, lambda i, me_ref: (me_ref[0], 0, 0)) for s in shards]),
        out_shape=[jax.ShapeDtypeStruct(g.shape, g.dtype) for g in gathered],
        input_output_aliases={1 + n + k: k for k in range(n)},
        compiler_params=_params("arbitrary"),
    )(_scalar(_chip_index()), *shards, *gathered)


def _half_rows(rows, c, align=8):
    assert (rows // 2) % align == 0
    return pl.ds(pl.multiple_of(c * (rows // 2), align), rows // 2)


BF16_ROWS = 16


def _halved(rows):
    return rows % (2 * BF16_ROWS) == 0


def _gather_copies(srcs, lands, send_sems, recv_sems):
    x, y, c = _mesh_pos()
    me = 2 * x + y
    out = []
    for k in range(len(srcs)):
        a = srcs[k].shape[0]
        rows = _half_rows(a, c, BF16_ROWS) if _halved(a) else pl.ds(0, a)
        for j, (px, py) in enumerate(_other_chips(x, y)):
            send = _remote(srcs[k].at[rows], lands[k].at[me, rows], send_sems, recv_sems, 3 * k + j, (px, py, c))
            recv = _remote(srcs[k].at[rows], lands[k].at[2 * px + py, rows], send_sems, recv_sems, 3 * k + j, (px, py, c))
            out.append((send, recv))
    return out


def _gather_start(srcs):
    nl, n = len(srcs), len(srcs[0])
    lands = [[lax.empty((N_CHIPS,) + s.shape, s.dtype) for s in sl] for sl in srcs]
    flat = [a for l in range(nl) for a in srcs[l] + lands[l]]

    def body(*refs):
        bufs, sems, token = refs[:len(flat)], refs[len(flat):len(flat) + 2 * nl], refs[-1]
        for l in range(nl):
            mine = bufs[2 * n * l:2 * n * (l + 1)]
            for send, _ in _gather_copies(mine[:n], mine[n:], sems[2 * l], sems[2 * l + 1]):
                send.start()
        token[...] = jnp.zeros_like(token)

    res = pl.pallas_call(
        body, name="weight_gather_start",
        in_specs=[HBM_ONLY] * len(flat),
        out_specs=[SEM_SPEC] * (2 * nl) + [HBM_ONLY] * len(flat) + [pl.BlockSpec(memory_space=pltpu.VMEM)],
        out_shape=[pltpu.SemaphoreType.DMA((3 * n,))] * (2 * nl) + [pltpu.HBM(a.shape, a.dtype) for a in flat]
        + [jax.ShapeDtypeStruct((8, LANES), F32)],
        input_output_aliases={i: 2 * nl + i for i in range(len(flat))},
        compiler_params=SPLIT_PARAMS,
    )(*[pltpu.with_memory_space_constraint(a, pltpu.HBM) for a in flat])
    bufs = res[2 * nl:2 * nl + len(flat)]
    state = [(res[2 * l], res[2 * l + 1], list(bufs[2 * n * l:2 * n * l + n]), list(bufs[2 * n * l + n:2 * n * (l + 1)]))
             for l in range(nl)]
    return state, res[-1][0:1, 0:1]


def _gather_wait(state, after, *, name):
    send_sems, recv_sems, srcs, lands = state
    n = len(srcs)

    def body(*refs):
        for send, recv in _gather_copies(refs[:n], refs[n:2 * n], refs[2 * n], refs[2 * n + 1]):
            send.wait_send()
            recv.wait_recv()

    res = pl.pallas_call(
        body, name=name,
        in_specs=[HBM_ONLY] * (2 * n) + [SEM_SPEC, SEM_SPEC, HBM_SPEC],
        out_specs=[HBM_ONLY] * (2 * n),
        out_shape=[pltpu.HBM(a.shape, a.dtype) for a in srcs + lands],
        input_output_aliases={i: i for i in range(2 * n)},
        compiler_params=SPLIT_PARAMS,
    )(*srcs, *lands, send_sems, recv_sems, after)
    return list(res[:n]), list(res[n:])


def _gather_forward(lands, *, name):
    n = len(lands)

    def body(*refs):
        bufs, outs = refs[:n], refs[n:2 * n]
        send_sems, recv_sems = refs[2 * n:]
        x, y, c = _mesh_pos()
        copies, waits = [], []
        for k in range(n):
            a = lands[k].shape[1]
            if not _halved(a):
                continue
            for j, (px, py) in enumerate(_other_chips(x, y)):
                mine = 2 * px + py, _half_rows(a, c, BF16_ROWS)
                copies.append(_remote(bufs[k].at[mine], outs[k].at[mine], send_sems, recv_sems, 3 * k + j, (x, y, 1 - c)))
                lands_here = outs[k].at[2 * px + py, _half_rows(a, 1 - c, BF16_ROWS)]
                waits.append(_remote(lands_here, lands_here, send_sems, recv_sems, 3 * k + j, (x, y, 1 - c)))
        for cp in copies:
            cp.start()
        for cp in waits:
            cp.wait_recv()
        for cp in copies:
            cp.wait_send()

    return pl.pallas_call(
        body, name=name,
        in_specs=[HBM_SPEC] * n, out_specs=[HBM_SPEC] * n,
        out_shape=[jax.ShapeDtypeStruct(a.shape, a.dtype) for a in lands],
        scratch_shapes=[pltpu.SemaphoreType.DMA((3 * n,)), pltpu.SemaphoreType.DMA((3 * n,))],
        input_output_aliases={i: i for i in range(n)},
        compiler_params=COMM_PARAMS,
    )(*lands)


def _sibling_exchange(gs, *, name):
    n = len(gs)

    def body(*refs):
        ins, outs = refs[:n], refs[n:2 * n]
        send_sems, recv_sems = refs[2 * n:]
        x, y, c = _mesh_pos()
        copies = [_remote(ins[k].at[:, _half_rows(gs[k].shape[1], 1 - c)], outs[k], send_sems, recv_sems, k, (x, y, 1 - c))
                  for k in range(n)]
        for cp in copies:
            cp.start()
        for cp in copies:
            cp.wait()

    return pl.pallas_call(
        body, name=name,
        in_specs=[HBM_SPEC] * n, out_specs=[HBM_SPEC] * n,
        out_shape=[jax.ShapeDtypeStruct((g.shape[0], g.shape[1] // 2, g.shape[2]), g.dtype) for g in gs],
        scratch_shapes=[pltpu.SemaphoreType.DMA((n,)), pltpu.SemaphoreType.DMA((n,))],
        compiler_params=COMM_PARAMS,
    )(*gs)


HBM_ONLY = pl.BlockSpec(memory_space=pltpu.HBM)
SEM_SPEC = pl.BlockSpec(memory_space=pltpu.SEMAPHORE)
SPLIT_PARAMS = pltpu.CompilerParams(has_side_effects=pltpu.SideEffectType.DATAFLOW_SIDE_EFFECTING)


def _scatter_copies(srcs, lands, send_sems, recv_sems):
    x, y, c = _mesh_pos()
    me = 2 * x + y
    out = []
    for k in range(len(srcs)):
        for j, (px, py) in enumerate(_other_chips(x, y)):
            s = 2 * px + py
            send = _remote(srcs[k].at[s], lands[k].at[me], send_sems, recv_sems, 3 * k + j, (px, py, c))
            recv = _remote(srcs[k].at[s], lands[k].at[s], send_sems, recv_sems, 3 * k + j, (px, py, c))
            out.append((send, recv))
    return out


def _scatter_start(ps, *, name):
    n = len(ps)
    lands = [lax.empty(p.shape, p.dtype) for p in ps]

    def body(*refs):
        srcs, zones = refs[:n], refs[n:2 * n]
        send_sems, recv_sems, token = refs[2 * n], refs[2 * n + 1], refs[-1]
        for send, _ in _scatter_copies(srcs, zones, send_sems, recv_sems):
            send.start()
        token[...] = jnp.zeros_like(token)

    hbm = lambda a: pltpu.HBM(a.shape, a.dtype)
    res = pl.pallas_call(
        body, name=name,
        in_specs=[HBM_ONLY] * (2 * n),
        out_specs=[SEM_SPEC, SEM_SPEC] + [HBM_ONLY] * (2 * n) + [pl.BlockSpec(memory_space=pltpu.VMEM)],
        out_shape=[pltpu.SemaphoreType.DMA((3 * n,)), pltpu.SemaphoreType.DMA((3 * n,))] + [hbm(a) for a in ps + lands]
        + [jax.ShapeDtypeStruct((8, LANES), F32)],
        input_output_aliases={i: 2 + i for i in range(2 * n)},
        compiler_params=SPLIT_PARAMS,
    )(*[pltpu.with_memory_space_constraint(a, pltpu.HBM) for a in ps + lands])
    return (res[0], res[1], list(res[2:2 + n]), list(res[2 + n:2 + 2 * n])), res[-1][0:1, 0:1]


def _scatter_wait(started, after, *, name):
    ng = len(started)
    sizes = [len(st[2]) for st in started]
    offs = [2 * sum(sizes[:i]) for i in range(ng + 1)]
    flat = [a for (_, _, ps, lands) in started for a in ps + lands]

    def body(*refs):
        bufs, sems = refs[:len(flat)], refs[len(flat):len(flat) + 2 * ng]
        for i, n in enumerate(sizes):
            srcs, zones = bufs[offs[i]:offs[i] + n], bufs[offs[i] + n:offs[i + 1]]
            for send, recv in _scatter_copies(srcs, zones, sems[2 * i], sems[2 * i + 1]):
                send.wait_send()
                recv.wait_recv()

    res = pl.pallas_call(
        body, name=name,
        in_specs=[HBM_ONLY] * len(flat) + [SEM_SPEC] * (2 * ng) + [HBM_SPEC],
        out_specs=[HBM_ONLY] * len(flat),
        out_shape=[pltpu.HBM(a.shape, a.dtype) for a in flat],
        input_output_aliases={i: i for i in range(len(flat))},
        compiler_params=SPLIT_PARAMS,
    )(*flat, *[s for (ss, rs, _, _) in started for s in (ss, rs)], after)
    return [(list(res[offs[i]:offs[i] + n]), list(res[offs[i] + n:offs[i + 1]])) for i, n in enumerate(sizes)]


def _sibling_share(hs):
    n = len(hs)

    def body(*refs):
        ins, outs = refs[:n], refs[n:2 * n]
        send_sems, recv_sems = refs[2 * n:]
        x, y, c = _mesh_pos()
        copies = [_remote(ins[k], outs[k], send_sems, recv_sems, k, (x, y, 1 - c)) for k in range(n)]
        for cp in copies:
            cp.start()
        for cp in copies:
            cp.wait()

    return pl.pallas_call(
        body, name="grad_sibling_share",
        in_specs=[HBM_SPEC] * n, out_specs=[HBM_SPEC] * n,
        out_shape=[jax.ShapeDtypeStruct(h.shape, h.dtype) for h in hs],
        scratch_shapes=[pltpu.SemaphoreType.DMA((n,)), pltpu.SemaphoreType.DMA((n,))],
        compiler_params=COMM_PARAMS,
    )(*hs)


def _allreduce_small(part):
    rows, C = part.shape

    def body(p_ref, o_ref, slots, send_sems, recv_sems):
        x, y, c = _mesh_pos()
        me = 4 * x + 2 * y + c
        slots[me] = p_ref[...]
        copies = []
        for k in range(1, 8):
            kx, ky, kc = (k >> 2) & 1, (k >> 1) & 1, k & 1
            peer = (x ^ kx if kx else x, y ^ ky if ky else y, c ^ kc if kc else c)
            cp = _remote(p_ref, slots.at[me], send_sems, recv_sems, k - 1, peer)
            cp.start()
            copies.append((cp, peer))
        for k, (cp, peer) in enumerate(copies):
            src = 4 * peer[0] + 2 * peer[1] + peer[2]
            _remote(p_ref, slots.at[src], send_sems, recv_sems, k, peer).wait_recv()
        for cp, _ in copies:
            cp.wait_send()
        total = slots[0]
        for d in range(1, 8):
            total = total + slots[d]
        o_ref[...] = total

    return pl.pallas_call(
        body, name="small_grad_allreduce",
        in_specs=[pl.BlockSpec(memory_space=pltpu.VMEM)], out_specs=pl.BlockSpec(memory_space=pltpu.VMEM),
        out_shape=jax.ShapeDtypeStruct((rows, C), F32),
        scratch_shapes=[pltpu.VMEM((8, rows, C), F32), pltpu.SemaphoreType.DMA((7,)), pltpu.SemaphoreType.DMA((7,))],
        compiler_params=pltpu.CompilerParams(has_side_effects=True, vmem_limit_bytes=VMEM_LIMIT_BYTES),
    )(part)


def _pad_w_uq(w):
    lead = w.shape[:-1]
    w = w.reshape(lead + (MLA_HEADS, MLA_QK))
    w = jnp.concatenate([w, jnp.zeros(lead + (MLA_HEADS, MLA_PAD - MLA_QK), w.dtype)], axis=-1)
    return w.reshape(lead + (MLA_HEADS * MLA_PAD,))


def _unpad_w_uq(g):
    lead = g.shape[:-1]
    return g.reshape(lead + (MLA_HEADS, MLA_PAD))[..., :MLA_QK].reshape(lead + (MLA_HEADS * MLA_QK,))


def _t(a):
    return jnp.swapaxes(a, -1, -2)


def _shards_of_cols(w):
    A, NB = w.shape
    return w.reshape(A, N_CHIPS, NB // N_CHIPS).transpose(1, 0, 2)


BIG = ("w_in", "w_uq", "w_ukv", "w_out", "w_up", "w_down")
SMALL = ("attn_pre_norm", "forget_bias", "swa_sinks", "rel_bias", "q_latent_norm", "kv_latent_norm", "group_norm",
         "attn_post_norm", "ffn_pre_norm", "conv_b", "ffn_post_norm")
WEIGHTS = ("attn_pre_norm", "w_in", "forget_bias", "swa_sinks", "rel_bias", "q_latent_norm", "w_uq", "kv_latent_norm",
           "w_ukv", "group_norm", "w_out", "attn_post_norm", "ffn_pre_norm", "w_up", "conv_w", "conv_b", "w_down",
           "ffn_post_norm")


PACK_UNIT = 8 * LANES


def _pack_rows(shape):
    return -(-int(np.prod(shape)) // PACK_UNIT) * 8


def _pack(arrs, row_mult=8):
    parts = []
    for a in arrs:
        n = int(np.prod(a.shape))
        parts.append(jnp.pad(a.reshape(-1), (0, _pack_rows(a.shape) * LANES - n)).reshape(-1, LANES))
    rows = sum(p.shape[0] for p in parts)
    pad = -rows % row_mult
    if pad:
        parts.append(jnp.zeros((pad, LANES), parts[0].dtype))
    return jnp.concatenate(parts, axis=0)


def _unpack(packed, shapes):
    packed = packed.reshape(-1, LANES)
    out, off = [], 0
    for shp in shapes:
        r = _pack_rows(shp)
        out.append(packed[off:off + r].reshape(-1)[:int(np.prod(shp))].reshape(shp))
        off += r
    return out


LAYER_KEYS = ("w_qkv_t", "w_lat_t", "w_in_t", "w_uq_p", "w_uq_t", "w_ukv", "w_ukv_t", "w_out", "w_up", "w_down", "conv_w")


def _layer_weights(gathered):
    cols = lambda g: g.transpose(1, 0, 2).reshape(g.shape[1], N_CHIPS * g.shape[2])
    w_in_t = _t(gathered["w_in"]).reshape(IN_COLS, D_MODEL)
    w_in_t = jnp.pad(w_in_t, ((0, IN_ROWS - IN_COLS), (0, 0)))
    w_uq_p = _pad_w_uq(cols(gathered["w_uq"]))
    w_ukv = cols(gathered["w_ukv"])
    return dict(w_qkv_t=w_in_t[:QKV_ROWS], w_lat_t=w_in_t[QKV_ROWS:], w_in_t=w_in_t, w_uq_p=w_uq_p, w_uq_t=_t(w_uq_p),
                w_ukv=w_ukv, w_ukv_t=_t(w_ukv), w_out=gathered["w_out"].reshape(D_MODEL, D_MODEL), w_up=gathered["w_up"],
                w_down=gathered["w_down"].reshape(D_FF, D_MODEL), conv_w=cols(gathered["conv_w"]))


def _local_step(x, target, W, layer_weights, layer_done):
    W = dict(W, **{key: [None] * DEPTH for key in LAYER_KEYS})
    S = x.shape[0]
    tq_tabs, tm_tabs = _rope_tables(S)
    onehot_t = _rel_onehot_t()
    bias_t = _bias_table(W["rel_bias"].T, onehot_t).reshape(SWA_KV_HEADS, SWA_GROUP, 2 * WINDOW, WINDOW)
    bias_t = bias_t.transpose(0, 2, 1, 3).reshape(SWA_KV_HEADS, 2 * WINDOW, GW)
    row = lambda a: a.reshape(1, -1)
    col = lambda a: a.reshape(-1, 1)
    fox_rows = (FOX_ROW0, FOX_ROW0 + FOX_HEADS * HEAD_DIM, FOX_ROW0 + 2 * FOX_HEADS * HEAD_DIM, SWA_Q_HEADS)
    fox = dict(rows=fox_rows, H=FOX_HEADS, Dk=HEAD_DIM, Dv=HEAD_DIM, scale=HEAD_DIM ** -0.5)
    mla = dict(rows=(0, 0, 0, SWA_Q_HEADS + FOX_HEADS), H=MLA_HEADS, Dk=MLA_PAD, Dv=HEAD_DIM, scale=MLA_QK ** -0.5)

    saved = []
    h = _rms_fwd(x, row(W["attn_pre_norm"][0]), name="rms_in")
    for l in range(DEPTH):
        sv = {"x0": x, "h1": h}
        for key, val in layer_weights(l, h).items():
            W[key][l] = val
        qkv = _matmul(W["w_qkv_t"][l], h, tb=True, out_dtype=BF16, name="proj_qkv")
        lat = _matmul(W["w_lat_t"][l], h, tb=True, name="proj_lat")
        oa, lse_a = _swa_fwd(qkv, bias_t, W["swa_sinks"][l], name="swa_fwd")
        fb_col = jnp.pad(col(W["forget_bias"][l]), ((0, GATE_ROWS - FOX_HEADS), (0, 0)))
        f4 = _gate_fwd(lat, fb_col, name="fox_gate_fwd")[:FOX_HEADS]
        f2 = f4 * LOG2E
        f_row, f_col = f2[:, None, :], f2.T
        of, lse_f = _attn_fwd(qkv, qkv, qkv, f_row=f_row, f_col=f_col, name="fox_fwd", **fox)
        nq, nkv, qm, km, vm = _mla_prep_fwd(lat, col(W["q_latent_norm"][l]), col(W["kv_latent_norm"][l]), W["w_uq_t"][l],
                                            W["w_ukv_t"][l], tq_tabs, tm_tabs, name="mla_prep_fwd")
        oc, lse_c = _attn_fwd(qm, km, vm, name="mla_fwd", **mla)
        mixed = _group_norm_fwd(oa, of, oc, col(W["group_norm"][l]), name="group_norm_fwd")
        y = _matmul(mixed, W["w_out"][l], ta=True, name="proj_out")
        x1, h2 = _resid_rms(x, y, row(W["attn_post_norm"][l]), row(W["ffn_pre_norm"][l]), name="attn_resid")
        a = _matmul(h2, W["w_up"][l], b_shards=True, out_dtype=BF16, name="ffn_up")
        u, z = _conv_geglu_fwd(a, W["conv_w"][l], row(W["conv_b"][l]), name="conv_geglu_fwd")
        y2 = _matmul(z, W["w_down"][l], name="ffn_down")
        g_next = row(W["attn_pre_norm"][l + 1]) if l + 1 < DEPTH else None
        x2, h_next = _resid_rms(x1, y2, row(W["ffn_post_norm"][l]), g_next, name="ffn_resid")
        sv.update(qkv=qkv, lat=lat, oa=oa, lse_a=lse_a, fb_col=fb_col, f_row=f_row, f_col=f_col, of=of, lse_f=lse_f,
                  nq=nq, nkv=nkv, qm=qm, km=km, vm=vm, oc=oc, lse_c=lse_c, mixed=mixed, y=y, x1=x1, h2=h2, a=a, u=u, z=z, y2=y2)
        saved.append(sv)
        x, h = x2, h_next

    loss, dx = _loss_head(x, target)

    G = {k: [None] * DEPTH for k in WEIGHTS if k != "rel_bias" and k not in BIG}
    dbias_layers = [None] * DEPTH
    for l in reversed(range(DEPTH)):
        sv = saved[l]
        gb = {}
        if l == DEPTH - 1:
            dy2, dg = _rms_bwd(sv["y2"], row(W["ffn_post_norm"][l]), dx, out_dtype=BF16, name="ffn_post_bwd")
            G["ffn_post_norm"][l] = dg[0]
        dz = _matmul(dy2, W["w_down"][l], tb=True, name="ffn_down_dx")
        gb["w_down"] = _matmul(sv["z"], dy2, ta=True, name="ffn_down_dw").reshape(N_CHIPS, D_FF // N_CHIPS, D_MODEL)
        da, dcw, dcb = _conv_geglu_bwd(sv["a"], sv["u"], W["conv_w"][l], dz, name="conv_geglu_bwd")
        G["conv_w"][l] = dcw.transpose(1, 0, 2).reshape(3, 2 * D_FF)
        G["conv_b"][l] = dcb.reshape(2 * D_FF)
        dh2 = _matmul(da, W["w_up"][l], tb=True, b_shards=True, a_halves=True, name="ffn_up_dx")
        gb["w_up"] = _matmul(sv["h2"], da, ta=True, out_shards=True, b_halves=True, name="ffn_up_dw")
        token = layer_done(l, gb)
        gb = {}
        dx1, dg, dy, dg_post = _rms_bwd(sv["x1"], row(W["ffn_pre_norm"][l]) + token, dh2, resid=dx, out_dtype=F32,
                                        then=(sv["y"], row(W["attn_post_norm"][l])), name="ffn_pre_bwd")
        G["ffn_pre_norm"][l] = dg[0]
        G["attn_post_norm"][l] = dg_post[0]
        dmixed = _matmul(W["w_out"][l], dy, tb=True, name="proj_out_dx")
        gb["w_out"] = _matmul(sv["mixed"], dy, name="proj_out_dw").reshape(N_CHIPS, D_MODEL // N_CHIPS, D_MODEL)
        doa, dof, doc, dg, delta = _group_norm_bwd(sv["oa"], sv["of"], sv["oc"], col(W["group_norm"][l]), dmixed,
                                                   name="group_norm_bwd")
        G["group_norm"][l] = dg[:, 0]
        dqa, dkva, dbias_l, dsink = _swa_bwd(sv["qkv"], bias_t, W["swa_sinks"][l], doa, sv["lse_a"],
                                             delta.reshape(-1, S), name="swa_bwd")
        dbias_layers[l] = (dbias_l.reshape(SWA_KV_HEADS, 2 * WINDOW, SWA_GROUP, WINDOW).transpose(0, 2, 1, 3)
                           .reshape(SWA_Q_HEADS, -1))
        G["swa_sinks"][l] = dsink[:, 0]
        dqf, dkf, dvf, dfk = _attn_bwd(sv["qkv"], sv["qkv"], sv["qkv"], do=dof, lse=sv["lse_f"], delta=delta,
                                       f_row=sv["f_row"], f_col=sv["f_col"], name="fox_bwd", **fox)
        dF = jnp.pad(dfk.T, ((0, GATE_ROWS - FOX_HEADS), (0, 0)))
        dflog, dfb = _gate_bwd(sv["lat"], sv["fb_col"], dF, name="fox_gate_bwd")
        G["forget_bias"][l] = dfb[:FOX_HEADS, 0]
        dqm, dkm, dvm = _attn_bwd(sv["qm"], sv["km"], sv["vm"], do=doc, lse=sv["lse_c"], delta=delta, name="mla_bwd", **mla)
        dlat, dwq_t, dwkv_t, dgq, dgkv = _mla_prep_bwd(
            sv["lat"], sv["nq"], sv["nkv"], col(W["q_latent_norm"][l]), col(W["kv_latent_norm"][l]), W["w_uq_p"][l],
            W["w_ukv"][l], tq_tabs, tm_tabs, dqm, dkm, dvm, dflog, name="mla_prep_bwd")
        gb["w_uq"], gb["w_ukv"] = _shards_of_cols(_unpad_w_uq(dwq_t.T)), _shards_of_cols(dwkv_t.T)
        G["q_latent_norm"][l], G["kv_latent_norm"][l] = dgq[:, 0], dgkv[:, 0]
        dproj = _dproj_cast(dqa, dkva, dqf, dkf, dvf, dlat, name="dproj_cast")
        dh1 = _matmul(dproj, W["w_in_t"][l], ta=True, name="proj_in_dx")
        dw_in_t = _matmul(dproj, sv["h1"], name="proj_in_dw")
        gb["w_in"] = _t(dw_in_t[:IN_COLS].reshape(N_CHIPS, IN_COLS // N_CHIPS, D_MODEL))
        token = layer_done(l, gb)
        below = (saved[l - 1]["y2"], row(W["ffn_post_norm"][l - 1])) if l > 0 else None
        res = _rms_bwd(sv["x0"], row(W["attn_pre_norm"][l]) + token, dh1, resid=dx1, out_dtype=F32, then=below,
                       name="attn_pre_bwd")
        dx, G["attn_pre_norm"][l] = res[0], res[1][0]
        if l > 0:
            dy2, G["ffn_post_norm"][l - 1] = res[2], res[3][0]

    grads = {k: jnp.stack(v) for k, v in G.items()}
    grads["rel_bias"] = _bias_table_bwd(jnp.stack(dbias_layers), onehot_t).T
    return loss, dx, grads


def kernel(x, attn_pre_norm, w_in, forget_bias, swa_sinks, rel_bias, q_latent_norm, w_uq, kv_latent_norm, w_ukv, group_norm, w_out, attn_post_norm, ffn_pre_norm, w_up, conv_w, conv_b, w_down, ffn_post_norm, loss_target, m_attn_pre_norm, m_w_in, m_forget_bias, m_swa_sinks, m_rel_bias, m_q_latent_norm, m_w_uq, m_kv_latent_norm, m_w_ukv, m_group_norm, m_w_out, m_attn_post_norm, m_ffn_pre_norm, m_w_up, m_conv_w, m_conv_b, m_w_down, m_ffn_post_norm, v_attn_pre_norm, v_w_in, v_forget_bias, v_swa_sinks, v_rel_bias, v_q_latent_norm, v_w_uq, v_kv_latent_norm, v_w_ukv, v_group_norm, v_w_out, v_attn_post_norm, v_ffn_pre_norm, v_w_up, v_conv_w, v_conv_b, v_w_down, v_ffn_post_norm):
    args = dict(locals())
    w = {k: args[k] for k in WEIGHTS}
    m = {k: args["m_" + k] for k in WEIGHTS}
    v = {k: args["v_" + k] for k in WEIGHTS}

    sent = BIG + ("conv_w",)
    gather_state, token = _gather_start([[w[k][l] if k == "conv_w" else w[k][l].astype(BF16) for k in sent]
                                         for l in range(DEPTH)])
    W = {k: w[k] for k in SMALL}
    W["attn_pre_norm"] = W["attn_pre_norm"] + token

    def layer_weights(l, after):
        srcs, lands = _gather_wait(gather_state[l], after, name=f"weight_gather_wait_{l}")
        lands = _gather_forward(lands, name=f"weight_gather_forward_{l}")
        lands = _place_own(lands, srcs, name="place_own_shards")
        return _layer_weights(dict(zip(sent, lands)))

    started, groups = [], []

    def layer_done(l, gb):
        keys = [k for k in BIG if k in gb]
        gs = [gb[k] for k in keys]
        tag = f"{l}_{keys[0]}"
        recv = _sibling_exchange(gs, name="grad_sibling_exchange_" + tag)
        pair = [_pair_sum(gk, rk, name="grad_pair_sum") for gk, rk in zip(gs, recv)]
        state, token = _scatter_start(pair, name="grad_scatter_start_" + tag)
        started.append(state)
        groups.append((l, keys))
        return token

    loss_part, dx, g = _local_step(x[0], loss_target[0], W, layer_weights, layer_done)
    loss = lax.psum(loss_part, ("x", "y", "c"))

    reduced = {}
    for (l, keys), (pair, zones) in zip(groups, _scatter_wait(started, dx, name="grad_scatter_wait")):
        for k, p, z in zip(keys, pair, zones):
            reduced[k, l] = _chip_sum(z, p, name="grad_chip_sum")
    mine = [jnp.stack([reduced[k, l] for l in range(DEPTH)]) for k in BIG]
    other = _sibling_share(mine)
    out_g, out_d, out_m, out_v = {}, {}, {}, {}
    for k, g_mine, g_other in zip(BIG, mine, other):
        out_g[k], out_d[k], out_m[k], out_v[k] = _adamw_halves(w[k], g_mine, g_other, m[k], v[k], name="adamw_" + k)

    small_shapes = [w[k].shape for k in SMALL]
    reduced = _allreduce_small(_pack([g[k] for k in SMALL] + [g["conv_w"]]))
    *g_small, g_cw = _unpack(reduced, small_shapes + [g["conv_w"].shape])
    chip = 2 * lax.axis_index("x") + lax.axis_index("y")
    g_small.append(lax.dynamic_slice_in_dim(g_cw, chip * FF_SHARD, FF_SHARD, axis=2))
    names = SMALL + ("conv_w",)
    shapes = small_shapes + [w["conv_w"].shape]
    packed = lambda arrs: _pack(arrs, ROW_TILE)[None]
    d_s, m_s, v_s = _adamw(packed([w[k] for k in names]), packed(g_small), packed([m[k] for k in names]),
                           packed([v[k] for k in names]), name="adamw_small")
    out_g.update(zip(names, g_small))
    out_d.update(zip(names, _unpack(d_s, shapes)))
    out_m.update(zip(names, _unpack(m_s, shapes)))
    out_v.update(zip(names, _unpack(v_s, shapes)))

    return (loss, dx[None], *[out_g[k] for k in WEIGHTS], *[out_d[k] for k in WEIGHTS],
            *[out_m[k] for k in WEIGHTS], *[out_v[k] for k in WEIGHTS])
```

```python
import math

import numpy as np
import jax
import jax.numpy as jnp
from jax import lax
from jax.experimental import pallas as pl
from jax.experimental.pallas import tpu as pltpu

F32 = jnp.float32
BF16 = jnp.bfloat16

D_MODEL = 1024
DEPTH = 4
HEAD_DIM = 64
SWA_Q_HEADS = 8
SWA_KV_HEADS = 2
SWA_GROUP = SWA_Q_HEADS // SWA_KV_HEADS
WINDOW = 128
FOX_HEADS = 4
MLA_HEADS = 4
MLA_Q_RANK = 256
MLA_KV_RANK = 128
MLA_NOPE = 64
MLA_ROPE = 32
MLA_QK = MLA_NOPE + MLA_ROPE
ROPE_THETA = 10000.0
REL_BUCKETS = 32
REL_MAX_DIST = 128
D_FF = 2816
EPS = 1e-6
NEG_INF = -1e30
LANES = 128
N_CHIPS = 4

IN_COLS = 1956
IN_ROWS = 2048
QKV_ROWS = 1536
LAT_ROWS = IN_ROWS - QKV_ROWS
LAT_SHIFT = FOX_HEADS
FOX_ROW0 = 768
MLA_PAD = LANES
GATE_ROWS = 8

ADAM_LR = 0.001
ADAM_B1 = 0.9
ADAM_B2 = 0.999
ADAM_EPS = 1e-08
ADAM_WD = 0.01
ADAM_STEP = 10

VMEM_LIMIT_BYTES = 48 * 1024 * 1024
ATT_TILE = 512
LOG2E = math.log2(math.e)
ROW_TILE = 256
MESH = pl.DeviceIdType.MESH

NT = (((1,), (1,)), ((), ()))
TN = (((0,), (0,)), ((), ()))
NN = (((1,), (0,)), ((), ()))


def _params(*sem):
    return pltpu.CompilerParams(dimension_semantics=sem, vmem_limit_bytes=VMEM_LIMIT_BYTES)


def _tile(dim, cap):
    for t in (2816, 2048, 1408, 1024, 512, 256, 128, 64, 32, 16, 8):
        if t <= cap and dim % t == 0:
            return t
    return dim


def _dot(a, b, dims=NN):
    return lax.dot_general(a, b, dims, preferred_element_type=F32)


def _split3(a):
    a1 = a.astype(BF16)
    r1 = a - a1.astype(F32)
    a2 = r1.astype(BF16)
    a3 = (r1 - a2.astype(F32)).astype(BF16)
    return a1, a2, a3


FF_SHARD = 2 * D_FF // N_CHIPS
MATMUL_VMEM_BYTES = 40 * 1024 * 1024


def _matmul(a, b, *, ta=False, tb=False, out_dtype=F32, name, b_shards=False, out_shards=False, a_halves=False,
            b_halves=False):
    if a_halves:
        M, K = a.shape[1], 2 * a.shape[2]
    elif ta:
        K, M = a.shape
    else:
        M, K = a.shape
    if b_halves:
        K2, N = b.shape[1], 2 * b.shape[2]
    elif b_shards:
        K2, N = (2 * D_FF, D_MODEL) if tb else (D_MODEL, 2 * D_FF)
    elif tb:
        N, K2 = b.shape
    else:
        K2, N = b.shape
    assert K == K2, (a.shape, b.shape)
    tm, tn = (M if M <= 2048 else _tile(M, 1408)), _tile(N, 1408)
    tk = FF_SHARD if (b_shards and tb) else _tile(K, 2816)
    out_bytes = jnp.dtype(out_dtype).itemsize
    while 2 * 2 * tk * (tm + tn) + (4 + 2 * out_bytes) * tm * tn > MATMUL_VMEM_BYTES and tk % 256 == 0:
        tk //= 2
    nk = K // tk
    dims = (((0 if ta else 1,), (1 if tb else 0,)), ((), ()))

    def body(a_ref, b_ref, o_ref, acc_ref):
        k = pl.program_id(2)

        @pl.when(k == 0)
        def _():
            acc_ref[...] = jnp.zeros_like(acc_ref)

        acc_ref[...] += lax.dot_general(a_ref[...], b_ref[...], dims, preferred_element_type=F32)

        @pl.when(k == nk - 1)
        def _():
            o_ref[...] = acc_ref[...].astype(o_ref.dtype)

    if a_halves:
        nh = K // 2 // tk
        a_spec = pl.BlockSpec((None, tm, tk), lambda i, j, k: (k // nh, i, k % nh))
    else:
        a_spec = pl.BlockSpec((tk, tm), lambda i, j, k: (k, i)) if ta else pl.BlockSpec((tm, tk), lambda i, j, k: (i, k))
    if b_halves:
        nh = N // 2 // tn
        b_spec = pl.BlockSpec((None, tk, tn), lambda i, j, k: (j // nh, k, j % nh))
    elif b_shards and tb:
        assert tk == FF_SHARD
        b_spec = pl.BlockSpec((None, tn, tk), lambda i, j, k: (k, j, 0))
    elif b_shards:
        assert tn == FF_SHARD
        b_spec = pl.BlockSpec((None, tk, tn), lambda i, j, k: (j, k, 0))
    else:
        b_spec = pl.BlockSpec((tn, tk), lambda i, j, k: (j, k)) if tb else pl.BlockSpec((tk, tn), lambda i, j, k: (k, j))
    if out_shards:
        assert tn == FF_SHARD
        out_spec = pl.BlockSpec((None, tm, tn), lambda i, j, k: (j, i, 0))
        out_shape = jax.ShapeDtypeStruct((N // tn, M, tn), out_dtype)
    else:
        out_spec = pl.BlockSpec((tm, tn), lambda i, j, k: (i, j))
        out_shape = jax.ShapeDtypeStruct((M, N), out_dtype)
    return pl.pallas_call(
        body, name=name, grid=(M // tm, N // tn, nk),
        in_specs=[a_spec, b_spec], out_specs=out_spec, out_shape=out_shape,
        scratch_shapes=[pltpu.VMEM((tm, tn), F32)],
        compiler_params=_params("parallel", "parallel", "arbitrary"),
    )(a, b)


def _seg_rms(xs, g):
    r = lax.rsqrt(jnp.mean(xs * xs, axis=-1, keepdims=True) + EPS)
    return xs * r * g


def _seg_rms_bwd(xs, g, dy):
    r = lax.rsqrt(jnp.mean(xs * xs, axis=-1, keepdims=True) + EPS)
    gd = dy * g
    c = jnp.mean(gd * xs, axis=-1, keepdims=True)
    dx = r * gd - xs * (r * r * r * c)
    dg = jnp.sum(dy * (xs * r), axis=0, keepdims=True)
    return dx, dg


def _rms_fwd(x, g, *, name):
    S, W = x.shape
    tm = _tile(S, 512)

    def body(x_ref, g_ref, o_ref):
        o_ref[...] = _seg_rms(x_ref[...], g_ref[...]).astype(o_ref.dtype)

    return pl.pallas_call(
        body, name=name, grid=(S // tm,),
        in_specs=[pl.BlockSpec((tm, W), lambda i: (i, 0)), pl.BlockSpec((1, W), lambda i: (0, 0))],
        out_specs=pl.BlockSpec((tm, W), lambda i: (i, 0)),
        out_shape=jax.ShapeDtypeStruct((S, W), BF16),
        compiler_params=_params("parallel"),
    )(x, g)


def _rms_bwd(x, g, dy, *, resid=None, out_dtype, name, then=None):
    S, W = x.shape
    tm = _tile(S, 512)
    has_resid = resid is not None
    chained = then is not None

    def body(*refs):
        refs = list(refs)
        x_ref, g_ref, dy_ref = refs[:3]
        r_ref = refs[3] if has_resid else None
        n_in = 3 + has_resid + 2 * chained
        x2_ref, g2_ref = (refs[n_in - 2], refs[n_in - 1]) if chained else (None, None)
        outs = refs[n_in:]
        dx_ref, dg_ref = outs[0], outs[1]

        @pl.when(pl.program_id(0) == 0)
        def _():
            dg_ref[...] = jnp.zeros_like(dg_ref)
            if chained:
                outs[3][...] = jnp.zeros_like(outs[3])

        dx, dg = _seg_rms_bwd(x_ref[...], g_ref[...], dy_ref[...])
        if has_resid:
            dx = dx + r_ref[...]
        dx_ref[...] = dx.astype(dx_ref.dtype)
        dg_ref[...] += dg
        if chained:
            dx2, dg2 = _seg_rms_bwd(x2_ref[...], g2_ref[...], dx)
            outs[2][...] = dx2.astype(BF16)
            outs[3][...] += dg2

    row = pl.BlockSpec((tm, W), lambda i: (i, 0))
    vec = pl.BlockSpec((1, W), lambda i: (0, 0))
    ins = [x, g, dy] + ([resid] if has_resid else []) + (list(then) if chained else [])
    return pl.pallas_call(
        body, name=name, grid=(S // tm,),
        in_specs=[row, vec, row] + ([row] if has_resid else []) + ([row, vec] if chained else []),
        out_specs=[row, vec] + ([row, vec] if chained else []),
        out_shape=[jax.ShapeDtypeStruct((S, W), out_dtype), jax.ShapeDtypeStruct((1, W), F32)]
        + ([jax.ShapeDtypeStruct((S, W), BF16), jax.ShapeDtypeStruct((1, W), F32)] if chained else []),
        compiler_params=_params("arbitrary"),
    )(*ins)


def _resid_rms(x, y, g_post, g_next, *, name):
    S, W = x.shape
    tm = _tile(S, 512)
    with_next = g_next is not None

    def body(*refs):
        if with_next:
            x_ref, y_ref, gp_ref, gn_ref, xo_ref, h_ref = refs
        else:
            x_ref, y_ref, gp_ref, xo_ref = refs
        xn = x_ref[...] + _seg_rms(y_ref[...], gp_ref[...])
        xo_ref[...] = xn
        if with_next:
            h_ref[...] = _seg_rms(xn, gn_ref[...]).astype(BF16)

    row = pl.BlockSpec((tm, W), lambda i: (i, 0))
    vec = pl.BlockSpec((1, W), lambda i: (0, 0))
    outs = [jax.ShapeDtypeStruct((S, W), F32)] + ([jax.ShapeDtypeStruct((S, W), BF16)] if with_next else [])
    res = pl.pallas_call(
        body, name=name, grid=(S // tm,),
        in_specs=[row, row, vec] + ([vec] if with_next else []),
        out_specs=[row] + ([row] if with_next else []),
        out_shape=outs,
        compiler_params=_params("parallel"),
    )(*([x, y, g_post] + ([g_next] if with_next else [])))
    return (res[0], res[1]) if with_next else (res[0], None)


def _col_rms(xs, g):
    r = lax.rsqrt(jnp.mean(xs * xs, axis=0, keepdims=True) + EPS)
    return xs * r * g


def _col_rms_bwd(xs, g, dy):
    r = lax.rsqrt(jnp.mean(xs * xs, axis=0, keepdims=True) + EPS)
    gd = dy * g
    c = jnp.mean(gd * xs, axis=0, keepdims=True)
    dx = r * gd - xs * (r * r * r * c)
    dg = jnp.sum(dy * (xs * r), axis=1, keepdims=True)
    return dx, dg


GROUP_ROWS = (SWA_Q_HEADS * HEAD_DIM, FOX_HEADS * HEAD_DIM, MLA_HEADS * HEAD_DIM)


def _group_specs(S, tn):
    outs = [pl.BlockSpec((n, tn), lambda i: (0, i)) for n in GROUP_ROWS]
    g = pl.BlockSpec((D_MODEL, 1), lambda i: (0, 0))
    mixed = pl.BlockSpec((D_MODEL, tn), lambda i: (0, i))
    return outs, g, mixed


def _group_norm_fwd(oa, of, oc, g, *, name):
    S = oa.shape[1]
    tn = _tile(S, 512)
    outs, gs, mixed = _group_specs(S, tn)

    def body(a_ref, f_ref, c_ref, g_ref, o_ref):
        r0 = 0
        for ref, n in zip((a_ref, f_ref, c_ref), GROUP_ROWS):
            o_ref[r0:r0 + n, :] = _col_rms(ref[...], g_ref[r0:r0 + n, :]).astype(BF16)
            r0 += n

    return pl.pallas_call(
        body, name=name, grid=(S // tn,),
        in_specs=outs + [gs], out_specs=mixed,
        out_shape=jax.ShapeDtypeStruct((D_MODEL, S), BF16),
        compiler_params=_params("parallel"),
    )(oa, of, oc, g)


def _group_norm_bwd(oa, of, oc, g, dmixed, *, name):
    S = oa.shape[1]
    tn = _tile(S, 512)
    outs, gs, mixed = _group_specs(S, tn)
    n_heads = D_MODEL // HEAD_DIM

    def body(a_ref, f_ref, c_ref, g_ref, dm_ref, da_ref, df_ref, dc_ref, dg_ref, dl_ref):
        @pl.when(pl.program_id(0) == 0)
        def _():
            dg_ref[...] = jnp.zeros_like(dg_ref)

        r0 = 0
        for ref, dref, n in zip((a_ref, f_ref, c_ref), (da_ref, df_ref, dc_ref), GROUP_ROWS):
            o = ref[...]
            dx, dg = _col_rms_bwd(o, g_ref[r0:r0 + n, :], dm_ref[r0:r0 + n, :])
            dxb = dx.astype(BF16)
            dref[...] = dxb
            dg_ref[r0:r0 + n, :] += dg
            od = o * dxb.astype(F32)
            for h in range(n // HEAD_DIM):
                dl_ref[r0 // HEAD_DIM + h] = jnp.sum(od[h * HEAD_DIM:(h + 1) * HEAD_DIM, :], axis=0, keepdims=True)
            r0 += n

    return pl.pallas_call(
        body, name=name, grid=(S // tn,),
        in_specs=outs + [gs, mixed], out_specs=outs + [gs, pl.BlockSpec((n_heads, 1, tn), lambda i: (0, 0, i))],
        out_shape=[jax.ShapeDtypeStruct((n, S), BF16) for n in GROUP_ROWS] + [jax.ShapeDtypeStruct((D_MODEL, 1), F32),
                                                                              jax.ShapeDtypeStruct((n_heads, 1, S), F32)],
        compiler_params=_params("arbitrary"),
    )(oa, of, oc, g, dmixed)


def _loss_head(y, target):
    S, W = y.shape
    tm = _tile(S, 512)

    def body(y_ref, t_ref, d_ref, l_ref):
        @pl.when(pl.program_id(0) == 0)
        def _():
            l_ref[...] = jnp.zeros_like(l_ref)

        err = y_ref[...] - t_ref[...]
        d_ref[...] = err * (1.0 / W)
        l_ref[...] += 0.5 * jnp.sum(jnp.mean(err * err, axis=-1, keepdims=True), axis=0, keepdims=True)

    row = pl.BlockSpec((tm, W), lambda i: (i, 0))
    d, l = pl.pallas_call(
        body, name="loss_head", grid=(S // tm,),
        in_specs=[row, row],
        out_specs=[row, pl.BlockSpec((1, 1), lambda i: (0, 0))],
        out_shape=[jax.ShapeDtypeStruct((S, W), F32), jax.ShapeDtypeStruct((1, 1), F32)],
        compiler_params=_params("arbitrary"),
    )(y, target)
    return l[0, 0], d


def _attn_fwd(q_src, k_src, v_src, rows, H, Dk, Dv, scale, f_row=None, f_col=None, *, name):
    S = q_src.shape[1]
    T = _tile(S, ATT_TILE)
    nq = S // T
    forget = f_row is not None
    qb, kb, vb = rows[0] // (H * Dk), rows[1] // (H * Dk), rows[2] // (H * Dv)
    hs = range(H)

    def body(*refs):
        if forget:
            q_ref, k_ref, v_ref, fq_ref, fk_ref, o_ref, lse_ref = refs
        else:
            q_ref, k_ref, v_ref, o_ref, lse_ref = refs
        i = pl.program_id(0)

        def tile(j, masked, state):
            off = pl.multiple_of(j * T, T)
            ss = [_dot(k_ref[h * Dk:(h + 1) * Dk, pl.ds(off, T)], q_ref[h * Dk:(h + 1) * Dk, :], TN) * (scale * LOG2E)
                  for h in hs]
            if forget:
                ss = [ss[h] + (fq_ref[h] - fk_ref[pl.ds(off, T), h:h + 1]) for h in hs]
            if masked:
                r = lax.broadcasted_iota(jnp.int32, (T, T), 0)
                c = lax.broadcasted_iota(jnp.int32, (T, T), 1)
                ss = [jnp.where(r <= c, s, NEG_INF) for s in ss]
            m_new = [jnp.maximum(state[h][0], jnp.max(ss[h], axis=0, keepdims=True)) for h in hs]
            alpha = [jnp.exp2(state[h][0] - m_new[h]) for h in hs]
            ps = [jnp.exp2(ss[h] - m_new[h]) for h in hs]
            l_new = [alpha[h] * state[h][1] + jnp.sum(ps[h], axis=0, keepdims=True) for h in hs]
            p_hi = [p.astype(BF16) for p in ps]
            vs = [v_ref[h * Dv:(h + 1) * Dv, pl.ds(off, T)] for h in hs]
            pv = [_dot(vs[h], p_hi[h]) for h in hs]
            if forget:
                pv = [pv[h] + _dot(vs[h], (ps[h] - p_hi[h].astype(F32)).astype(BF16)) for h in hs]
            return tuple((m_new[h], l_new[h], alpha[h] * state[h][2] + pv[h]) for h in hs)

        init = tuple((jnp.full((1, T), NEG_INF, F32), jnp.zeros((1, T), F32), jnp.zeros((Dv, T), F32)) for _ in hs)
        state = lax.fori_loop(0, i, lambda j, st: tile(j, False, st), init)
        state = tile(i, True, state)
        for h in hs:
            m, l, acc = state[h]
            o_ref[h * Dv:(h + 1) * Dv, :] = acc / l
            lse_ref[h] = m + jnp.log2(l)

    in_specs = [pl.BlockSpec((H * Dk, T), lambda i: (qb, i)),
                pl.BlockSpec((H * Dk, S), lambda i: (kb, 0)),
                pl.BlockSpec((H * Dv, S), lambda i: (vb, 0))]
    ins = [q_src, k_src, v_src]
    if forget:
        in_specs += [pl.BlockSpec((H, 1, T), lambda i: (0, 0, i)), pl.BlockSpec((S, H), lambda i: (0, 0))]
        ins += [f_row, f_col]
    return pl.pallas_call(
        body, name=name, grid=(nq,),
        in_specs=in_specs,
        out_specs=[pl.BlockSpec((H * Dv, T), lambda i: (0, i)), pl.BlockSpec((H, 1, T), lambda i: (0, 0, i))],
        out_shape=[jax.ShapeDtypeStruct((H * Dv, S), F32), jax.ShapeDtypeStruct((H, 1, S), F32)],
        compiler_params=_params("parallel"),
    )(*ins)


def _attn_bwd(q_src, k_src, v_src, rows, H, Dk, Dv, scale, do, lse, delta, f_row=None, f_col=None, *, name):
    S = q_src.shape[1]
    T = _tile(S, ATT_TILE)
    nq = S // T
    forget = f_row is not None
    qb, kb, vb, db = rows[0] // (H * Dk), rows[1] // (H * Dk), rows[2] // (H * Dv), rows[3] // H
    hs = range(H)

    def body(*refs):
        if forget:
            (q_ref, k_ref, v_ref, do_ref, lse_ref, dl_ref, fq_ref, fk_ref,
             dq_ref, dk_ref, dv_ref, df_ref, dk_s, dv_s, df_s) = refs
        else:
            q_ref, k_ref, v_ref, do_ref, lse_ref, dl_ref, dq_ref, dk_ref, dv_ref, dk_s, dv_s = refs
        j = pl.program_id(0)

        @pl.when(j == 0)
        def _():
            dq_ref[...] = jnp.zeros_like(dq_ref)

        dk_s[...] = jnp.zeros_like(dk_s)
        dv_s[...] = jnp.zeros_like(dv_s)
        if forget:
            df_s[...] = jnp.zeros_like(df_s)
        kt = [k_ref[h * Dk:(h + 1) * Dk, :] for h in hs]
        kj = [k.T for k in kt]
        vj = [v_ref[h * Dv:(h + 1) * Dv, :].T for h in hs]
        koff = pl.multiple_of(j * T, T)

        def tile(i, masked):
            cols = pl.ds(pl.multiple_of(i * T, T), T)
            qi = [q_ref[h * Dk:(h + 1) * Dk, cols] for h in hs]
            doi = [do_ref[h * Dv:(h + 1) * Dv, cols] for h in hs]
            st = [_dot(kj[h], qi[h]) * (scale * LOG2E) for h in hs]
            if forget:
                st = [st[h] + (fq_ref[h, :, cols] - fk_ref[pl.ds(koff, T), h:h + 1]) for h in hs]
            if masked:
                r = lax.broadcasted_iota(jnp.int32, (T, T), 0)
                c = lax.broadcasted_iota(jnp.int32, (T, T), 1)
                st = [jnp.where(r <= c, x, NEG_INF) for x in st]
            pt = [jnp.exp2(st[h] - lse_ref[h, :, cols]) for h in hs]
            dpt = [_dot(vj[h], doi[h]) for h in hs]
            dst = [pt[h] * (dpt[h] - dl_ref[h, :, cols]) for h in hs]
            ptb = [p.astype(BF16) for p in pt]
            dsb = [d.astype(BF16) for d in dst]
            for h in hs:
                dv_s[h * Dv:(h + 1) * Dv, :] += _dot(doi[h], ptb[h], NT)
            for h in hs:
                dk_s[h * Dk:(h + 1) * Dk, :] += _dot(qi[h], dsb[h], NT)
            for h in hs:
                dq_ref[h * Dk:(h + 1) * Dk, cols] += _dot(kt[h], dsb[h]) * scale
            if forget:
                for h in hs:
                    part = dst[h][:, 0:LANES]
                    for c0 in range(LANES, T, LANES):
                        part = part + dst[h][:, c0:c0 + LANES]
                    df_s[h] += part

        tile(j, True)

        def loop_body(i, carry):
            tile(i, False)
            return carry

        lax.fori_loop(j + 1, nq, loop_body, 0)
        dk_ref[...] = dk_s[...] * scale
        dv_ref[...] = dv_s[...]
        if forget:
            df_ref[...] = jnp.concatenate([-jnp.sum(df_s[h], axis=-1, keepdims=True) for h in hs], axis=1)

    res = lambda D, b0: pl.BlockSpec((H * D, S), lambda j: (b0, 0))
    blk = lambda D, b0: pl.BlockSpec((H * D, T), lambda j: (b0, j))
    row3 = lambda b0: pl.BlockSpec((H, 1, S), lambda j: (b0, 0, 0))
    in_specs = [res(Dk, qb), blk(Dk, kb), blk(Dv, vb), res(Dv, 0), row3(0), row3(db)]
    ins = [q_src, k_src, v_src, do, lse, delta]
    out_specs = [res(Dk, 0), blk(Dk, 0), blk(Dv, 0)]
    out_shape = [jax.ShapeDtypeStruct((H * Dk, S), F32), jax.ShapeDtypeStruct((H * Dk, S), F32),
                 jax.ShapeDtypeStruct((H * Dv, S), F32)]
    scratch = [pltpu.VMEM((H * Dk, T), F32), pltpu.VMEM((H * Dv, T), F32)]
    if forget:
        in_specs += [row3(0), pl.BlockSpec((S, H), lambda j: (0, 0))]
        ins += [f_row, f_col]
        out_specs.append(pl.BlockSpec((T, H), lambda j: (j, 0)))
        out_shape.append(jax.ShapeDtypeStruct((S, H), F32))
        scratch.append(pltpu.VMEM((H, T, min(T, LANES)), F32))
    return pl.pallas_call(
        body, name=name, grid=(nq,),
        in_specs=in_specs, out_specs=out_specs, out_shape=out_shape, scratch_shapes=scratch,
        compiler_params=_params("arbitrary"),
    )(*ins)


GW = SWA_GROUP * WINDOW


def _swa_masks(i):
    r = lax.broadcasted_iota(jnp.int32, (WINDOW, GW), 0)
    c = lax.broadcasted_iota(jnp.int32, (WINDOW, GW), 1) % WINDOW
    return (r > c) & (i > 0), r <= c


def _swa_specs():
    W = WINDOW
    kv_rows = SWA_KV_HEADS * HEAD_DIM
    q = pl.BlockSpec((SWA_Q_HEADS * HEAD_DIM, W), lambda i: (0, i))
    prev = lambda b: pl.BlockSpec((kv_rows, W), lambda i: (b, jnp.maximum(i - 1, 0)))
    cur = lambda b: pl.BlockSpec((kv_rows, W), lambda i: (b, i))
    bias = pl.BlockSpec((SWA_KV_HEADS, 2 * W, GW), lambda i: (0, 0, 0))
    stat = pl.BlockSpec((SWA_Q_HEADS, W), lambda i: (0, i))
    sink = pl.BlockSpec(memory_space=pltpu.SMEM)
    return q, prev(4), cur(4), prev(5), cur(5), bias, stat, sink


def _group_lanes(ref, g, rows_per_head):
    h0 = g * SWA_GROUP
    return jnp.concatenate([ref[(h0 + j) * rows_per_head:(h0 + j + 1) * rows_per_head, :] for j in range(SWA_GROUP)], axis=1)


def _swa_scores(g, q_ref, kp_ref, kc_ref, b_ref, masks):
    rows = slice(g * HEAD_DIM, (g + 1) * HEAD_DIM)
    qg = _group_lanes(q_ref, g, HEAD_DIM)
    scale = HEAD_DIM ** -0.5
    s_p = jnp.where(masks[0], _dot(kp_ref[rows, :], qg, TN) * scale + b_ref[g, 0:WINDOW, :], NEG_INF)
    s_c = jnp.where(masks[1], _dot(kc_ref[rows, :], qg, TN) * scale + b_ref[g, WINDOW:2 * WINDOW, :], NEG_INF)
    return qg, rows, s_p, s_c


def _sink_row(sink_ref, g):
    return jnp.concatenate([jnp.full((1, WINDOW), sink_ref[g * SWA_GROUP + j], F32) for j in range(SWA_GROUP)], axis=1)


def _swa_fwd(qkv, bias_g, sinks, *, name):
    S = qkv.shape[1]
    qs, kp, kc, vp, vc, bs, stat, sk = _swa_specs()
    gs = range(SWA_KV_HEADS)

    def body(sink_ref, q_ref, kp_ref, kc_ref, vp_ref, vc_ref, b_ref, o_ref, lse_ref):
        masks = _swa_masks(pl.program_id(0))
        sc = [_swa_scores(g, q_ref, kp_ref, kc_ref, b_ref, masks) for g in gs]
        sinks_g = [_sink_row(sink_ref, g) for g in gs]
        m = [jnp.maximum(jnp.maximum(jnp.max(sc[g][2], axis=0, keepdims=True), jnp.max(sc[g][3], axis=0, keepdims=True)),
                         sinks_g[g]) for g in gs]
        p_p = [jnp.exp(sc[g][2] - m[g]) for g in gs]
        p_c = [jnp.exp(sc[g][3] - m[g]) for g in gs]
        l = [jnp.sum(p_p[g], axis=0, keepdims=True) + jnp.sum(p_c[g], axis=0, keepdims=True) + jnp.exp(sinks_g[g] - m[g])
             for g in gs]
        o = [_dot(vp_ref[sc[g][1], :], p_p[g].astype(BF16)) + _dot(vc_ref[sc[g][1], :], p_c[g].astype(BF16)) for g in gs]
        for g in gs:
            og = o[g] / l[g]
            lse = m[g] + jnp.log(l[g])
            for j in range(SWA_GROUP):
                h = g * SWA_GROUP + j
                o_ref[h * HEAD_DIM:(h + 1) * HEAD_DIM, :] = og[:, j * WINDOW:(j + 1) * WINDOW]
                lse_ref[h:h + 1, :] = lse[:, j * WINDOW:(j + 1) * WINDOW]

    return pl.pallas_call(
        body, name=name, grid=(S // WINDOW,),
        in_specs=[sk, qs, kp, kc, vp, vc, bs],
        out_specs=[qs, stat],
        out_shape=[jax.ShapeDtypeStruct((SWA_Q_HEADS * HEAD_DIM, S), F32), jax.ShapeDtypeStruct((SWA_Q_HEADS, S), F32)],
        compiler_params=_params("parallel"),
    )(sinks, qkv, qkv, qkv, qkv, qkv, bias_g)


def _swa_bwd(qkv, bias_g, sinks, do, lse, delta, *, name):
    S = qkv.shape[1]
    W = WINDOW
    qs, kp, kc, vp, vc, bs, stat, sk = _swa_specs()
    scale = HEAD_DIM ** -0.5
    kv_rows = SWA_KV_HEADS * HEAD_DIM
    gs = range(SWA_KV_HEADS)

    def body(sink_ref, q_ref, kp_ref, kc_ref, vp_ref, vc_ref, b_ref, do_ref, lse_ref, dl_ref,
             dq_ref, dkv_ref, db_ref, dsk_ref):
        i = pl.program_id(0)

        @pl.when(i == 0)
        def _():
            dkv_ref[...] = jnp.zeros_like(dkv_ref)
            db_ref[...] = jnp.zeros_like(db_ref)
            dsk_ref[...] = jnp.zeros_like(dsk_ref)

        masks = _swa_masks(i)
        prev = pl.ds(pl.multiple_of(jnp.maximum(i - 1, 0) * W, W), W)
        cur = pl.ds(pl.multiple_of(i * W, W), W)
        sc = [_swa_scores(g, q_ref, kp_ref, kc_ref, b_ref, masks) for g in gs]
        dog = [_group_lanes(do_ref, g, HEAD_DIM) for g in gs]
        lse = [_group_lanes(lse_ref, g, 1) for g in gs]
        dl = [_group_lanes(dl_ref, g, 1) for g in gs]
        p_p = [jnp.exp(sc[g][2] - lse[g]) for g in gs]
        p_c = [jnp.exp(sc[g][3] - lse[g]) for g in gs]
        ds_p = [p_p[g] * (_dot(vp_ref[sc[g][1], :], dog[g], TN) - dl[g]) for g in gs]
        ds_c = [p_c[g] * (_dot(vc_ref[sc[g][1], :], dog[g], TN) - dl[g]) for g in gs]
        for g in gs:
            db_ref[g, 0:W, :] += ds_p[g]
            db_ref[g, W:2 * W, :] += ds_c[g]
            dsk = jnp.exp(_sink_row(sink_ref, g) - lse[g]) * dl[g]
            for j in range(SWA_GROUP):
                h = g * SWA_GROUP + j
                dsk_ref[h:h + 1, :] -= jnp.broadcast_to(jnp.sum(dsk[:, j * W:(j + 1) * W], axis=1, keepdims=True), (1, LANES))
        dsb_p = [d.astype(BF16) for d in ds_p]
        dsb_c = [d.astype(BF16) for d in ds_c]
        for g in gs:
            rows = sc[g][1]
            dq = (_dot(kp_ref[rows, :], dsb_p[g]) + _dot(kc_ref[rows, :], dsb_c[g])) * scale
            for j in range(SWA_GROUP):
                h = g * SWA_GROUP + j
                dq_ref[h * HEAD_DIM:(h + 1) * HEAD_DIM, :] = dq[:, j * W:(j + 1) * W]
        for g in gs:
            rows = sc[g][1]
            vrows = slice(kv_rows + rows.start, kv_rows + rows.stop)
            dkv_ref[rows, prev] += _dot(sc[g][0], dsb_p[g], NT) * scale
            dkv_ref[rows, cur] += _dot(sc[g][0], dsb_c[g], NT) * scale
            dkv_ref[vrows, prev] += _dot(dog[g], p_p[g].astype(BF16), NT)
            dkv_ref[vrows, cur] += _dot(dog[g], p_c[g].astype(BF16), NT)

    return pl.pallas_call(
        body, name=name, grid=(S // W,),
        in_specs=[sk, qs, kp, kc, vp, vc, bs, qs, stat, stat],
        out_specs=[qs, pl.BlockSpec((2 * kv_rows, S), lambda i: (0, 0)), bs, pl.BlockSpec((SWA_Q_HEADS, LANES), lambda i: (0, 0))],
        out_shape=[jax.ShapeDtypeStruct((SWA_Q_HEADS * HEAD_DIM, S), F32), jax.ShapeDtypeStruct((2 * kv_rows, S), F32),
                   jax.ShapeDtypeStruct((SWA_KV_HEADS, 2 * W, GW), F32), jax.ShapeDtypeStruct((SWA_Q_HEADS, LANES), F32)],
        compiler_params=_params("arbitrary"),
    )(sinks, qkv, qkv, qkv, qkv, qkv, bias_g, do, lse, delta)


def _rel_onehot_t():
    qi = jnp.arange(WINDOW, dtype=jnp.int32)[None, :] + WINDOW
    kj = jnp.arange(2 * WINDOW, dtype=jnp.int32)[:, None]
    dist = qi - kj
    max_exact = REL_BUCKETS // 2
    d = jnp.maximum(dist, 0)
    log_ratio = jnp.log(jnp.maximum(d, 1).astype(F32) / max_exact) / math.log(REL_MAX_DIST / max_exact)
    large = jnp.minimum(max_exact + (log_ratio * (REL_BUCKETS - max_exact)).astype(jnp.int32), REL_BUCKETS - 1)
    bucket = jnp.where(d < max_exact, d, large).reshape(-1)
    return (bucket[None, :] == jnp.arange(REL_BUCKETS, dtype=jnp.int32)[:, None]).astype(BF16)


def _bias_table(rel_bias_t, onehot_t):
    Hq, NB = rel_bias_t.shape
    N = onehot_t.shape[1]
    tn = _tile(N, 4096)

    def body(r_ref, oh_ref, o_ref):
        oh = oh_ref[...]
        a1, a2, a3 = _split3(r_ref[...])
        o_ref[...] = _dot(a1, oh) + _dot(a2, oh) + _dot(a3, oh)

    return pl.pallas_call(
        body, name="rel_bias_table", grid=(N // tn,),
        in_specs=[pl.BlockSpec((Hq, NB), lambda j: (0, 0)), pl.BlockSpec((NB, tn), lambda j: (0, j))],
        out_specs=pl.BlockSpec((Hq, tn), lambda j: (0, j)),
        out_shape=jax.ShapeDtypeStruct((Hq, N), F32),
        compiler_params=_params("parallel"),
    )(rel_bias_t, onehot_t)


def _bias_table_bwd(dbias, onehot_t):
    L, Hq, N = dbias.shape
    NB = onehot_t.shape[0]
    tn = _tile(N, 4096)

    def body(d_ref, oh_ref, o_ref):
        @pl.when(pl.program_id(0) == 0)
        def _():
            o_ref[...] = jnp.zeros_like(o_ref)

        d = d_ref[0]
        for l in range(1, L):
            d = d + d_ref[l]
        oh = oh_ref[...]
        a1, a2, a3 = _split3(d)
        o_ref[...] += _dot(a1, oh, NT) + _dot(a2, oh, NT) + _dot(a3, oh, NT)

    return pl.pallas_call(
        body, name="rel_bias_bwd", grid=(N // tn,),
        in_specs=[pl.BlockSpec((L, Hq, tn), lambda j: (0, 0, j)), pl.BlockSpec((NB, tn), lambda j: (0, j))],
        out_specs=pl.BlockSpec((Hq, NB), lambda j: (0, 0)),
        out_shape=jax.ShapeDtypeStruct((Hq, NB), F32),
        compiler_params=_params("arbitrary"),
    )(dbias, onehot_t)


def _gate_fwd(lat, fb_col, *, name):
    S = lat.shape[1]
    tn = _tile(S, 256)

    def body(z_ref, fb_ref, o_ref, carry):
        @pl.when(pl.program_id(0) == 0)
        def _():
            carry[...] = jnp.zeros_like(carry)

        z = z_ref[...] + fb_ref[...]
        lf = jnp.minimum(z, 0.0) - jnp.log1p(jnp.exp(-jnp.abs(z)))
        r = lax.broadcasted_iota(jnp.int32, (tn, tn), 0)
        c = lax.broadcasted_iota(jnp.int32, (tn, tn), 1)
        tri = (r <= c).astype(BF16)
        a1, a2, a3 = _split3(lf)
        cum = _dot(a1, tri) + _dot(a2, tri) + _dot(a3, tri) + carry[:, 0:1]
        o_ref[...] = cum
        carry[...] = jnp.broadcast_to(cum[:, tn - 1:tn], carry.shape)

    return pl.pallas_call(
        body, name=name, grid=(S // tn,),
        in_specs=[pl.BlockSpec((GATE_ROWS, tn), lambda i: (0, i)), pl.BlockSpec((GATE_ROWS, 1), lambda i: (0, 0))],
        out_specs=pl.BlockSpec((GATE_ROWS, tn), lambda i: (0, i)),
        out_shape=jax.ShapeDtypeStruct((GATE_ROWS, S), F32),
        scratch_shapes=[pltpu.VMEM((GATE_ROWS, LANES), F32)],
        compiler_params=_params("arbitrary"),
    )(lat, fb_col)


def _gate_bwd(lat, fb_col, dF, *, name):
    S = lat.shape[1]
    tn = _tile(S, 256)
    nt = S // tn

    def body(z_ref, fb_ref, df_ref, dz_ref, dfb_ref, carry):
        @pl.when(pl.program_id(0) == 0)
        def _():
            carry[...] = jnp.zeros_like(carry)
            dfb_ref[...] = jnp.zeros_like(dfb_ref)

        r = lax.broadcasted_iota(jnp.int32, (tn, tn), 0)
        c = lax.broadcasted_iota(jnp.int32, (tn, tn), 1)
        tri = (r >= c).astype(BF16)
        a1, a2, a3 = _split3(df_ref[...])
        dlf = _dot(a1, tri) + _dot(a2, tri) + _dot(a3, tri) + carry[:, 0:1]
        carry[...] = jnp.broadcast_to(dlf[:, 0:1], carry.shape)
        z = z_ref[...] + fb_ref[...]
        row = lax.broadcasted_iota(jnp.int32, (GATE_ROWS, tn), 0)
        dz = jnp.where(row < FOX_HEADS, dlf / (1.0 + jnp.exp(z)), 0.0)
        dz_ref[...] = dz
        dfb_ref[...] += jnp.sum(dz, axis=1, keepdims=True)

    blk = pl.BlockSpec((GATE_ROWS, tn), lambda i: (0, nt - 1 - i))
    vec = pl.BlockSpec((GATE_ROWS, 1), lambda i: (0, 0))
    return pl.pallas_call(
        body, name=name, grid=(nt,),
        in_specs=[blk, vec, blk], out_specs=[blk, vec],
        out_shape=[jax.ShapeDtypeStruct((GATE_ROWS, S), F32), jax.ShapeDtypeStruct((GATE_ROWS, 1), F32)],
        scratch_shapes=[pltpu.VMEM((GATE_ROWS, LANES), F32)],
        compiler_params=_params("arbitrary"),
    )(lat, fb_col, dF)


def _rope_tables(S):
    pos = jnp.arange(S, dtype=F32)
    inv_freq = ROPE_THETA ** (-(jnp.arange(MLA_ROPE // 2, dtype=F32) * 2.0 / MLA_ROPE))
    ang = pos[:, None] * inv_freq[None, :]
    cos, sin = jnp.cos(ang).T, jnp.sin(ang).T
    z16 = jnp.zeros_like(cos)

    def slab(lo, fill):
        def put(first, second, f):
            return jnp.concatenate([jnp.full((lo, S), f, F32), first, second, jnp.full((LANES - lo - MLA_ROPE, S), f, F32)], axis=0)
        return put(cos, cos, fill), put(-sin, z16, 0.0), put(z16, sin, 0.0)

    tq = tuple(jnp.tile(t, (MLA_HEADS, 1)) for t in slab(MLA_NOPE, 1.0))
    return tq, slab(0, 0.0)


def _rope(x, c, s1, s2):
    n = x.shape[0]
    half = MLA_ROPE // 2
    return x * c + pltpu.roll(x, n - half, 0) * s1 + pltpu.roll(x, half, 0) * s2


def _rope_t(dy, c, s1, s2):
    n = dy.shape[0]
    half = MLA_ROPE // 2
    return dy * c + pltpu.roll(dy * s1, half, 0) + pltpu.roll(dy * s2, n - half, 0)


KR_SLAB0 = MLA_Q_RANK + MLA_KV_RANK


def _mla_prep_fwd(lat, g_q, g_kv, w_uq_t, w_ukv_t, tq, tmisc, *, name):
    S = lat.shape[1]
    tn = _tile(S, 512)
    QW = MLA_HEADS * MLA_PAD

    def body(lat_ref, gq_ref, gkv_ref, wq_ref, wkv_ref, c_ref, s1_ref, s2_ref, cm_ref, s1m_ref, s2m_ref,
             nq_ref, nkv_ref, q_ref, k_ref, v_ref):
        x = pltpu.roll(lat_ref[...], LAT_ROWS - LAT_SHIFT, 0)
        nq = _col_rms(x[0:MLA_Q_RANK, :], gq_ref[...]).astype(BF16)
        nkv = _col_rms(x[MLA_Q_RANK:KR_SLAB0, :], gkv_ref[...]).astype(BF16)
        nq_ref[...] = nq
        nkv_ref[...] = nkv
        q_ref[...] = _rope(_dot(wq_ref[...], nq), c_ref[...], s1_ref[...], s2_ref[...]).astype(BF16)
        kv = _dot(wkv_ref[...], nkv).astype(BF16)
        kr = _rope(x[KR_SLAB0:LAT_ROWS, :], cm_ref[...], s1m_ref[...], s2m_ref[...]).astype(BF16)
        for h in range(MLA_HEADS):
            k_ref[h * MLA_PAD:h * MLA_PAD + MLA_NOPE, :] = kv[h * LANES:h * LANES + MLA_NOPE, :]
            k_ref[h * MLA_PAD + MLA_NOPE:(h + 1) * MLA_PAD, :] = kr[0:MLA_PAD - MLA_NOPE, :]
            v_ref[h * HEAD_DIM:(h + 1) * HEAD_DIM, :] = kv[h * LANES + MLA_NOPE:(h + 1) * LANES, :]

    def col(rows):
        return pl.BlockSpec((rows, tn), lambda i: (0, i))

    def full(a):
        return pl.BlockSpec(a.shape, lambda i: (0, 0))

    return pl.pallas_call(
        body, name=name, grid=(S // tn,),
        in_specs=[col(LAT_ROWS), full(g_q), full(g_kv), full(w_uq_t), full(w_ukv_t),
                  col(QW), col(QW), col(QW), col(LANES), col(LANES), col(LANES)],
        out_specs=[col(MLA_Q_RANK), col(MLA_KV_RANK), col(QW), col(QW), col(MLA_HEADS * HEAD_DIM)],
        out_shape=[jax.ShapeDtypeStruct((MLA_Q_RANK, S), BF16), jax.ShapeDtypeStruct((MLA_KV_RANK, S), BF16),
                   jax.ShapeDtypeStruct((QW, S), BF16), jax.ShapeDtypeStruct((QW, S), BF16),
                   jax.ShapeDtypeStruct((MLA_HEADS * HEAD_DIM, S), BF16)],
        compiler_params=_params("parallel"),
    )(lat, g_q, g_kv, w_uq_t, w_ukv_t, *tq, *tmisc)


def _mla_prep_bwd(lat, nq, nkv, g_q, g_kv, w_uq_p, w_ukv, tq, tmisc, dq, dk, dv, dflog, *, name):
    S = lat.shape[1]
    tn = _tile(S, 512)
    QW = MLA_HEADS * MLA_PAD

    def body(lat_ref, nq_ref, nkv_ref, gq_ref, gkv_ref, wq_ref, wkv_ref, c_ref, s1_ref, s2_ref,
             cm_ref, s1m_ref, s2m_ref, dq_ref, dk_ref, dv_ref, dfl_ref,
             dlat_ref, dwq_ref, dwkv_ref, dgq_ref, dgkv_ref, y_s):
        @pl.when(pl.program_id(0) == 0)
        def _():
            dwq_ref[...] = jnp.zeros_like(dwq_ref)
            dwkv_ref[...] = jnp.zeros_like(dwkv_ref)
            dgq_ref[...] = jnp.zeros_like(dgq_ref)
            dgkv_ref[...] = jnp.zeros_like(dgkv_ref)

        x = pltpu.roll(lat_ref[...], LAT_ROWS - LAT_SHIFT, 0)
        dqm = _rope_t(dq_ref[...], c_ref[...], s1_ref[...], s2_ref[...]).astype(BF16)
        dwq_ref[...] += _dot(dqm, nq_ref[...], NT)
        dx, dg = _col_rms_bwd(x[0:MLA_Q_RANK, :], gq_ref[...], _dot(wq_ref[...], dqm))
        y_s[0:MLA_Q_RANK, :] = dx
        dgq_ref[...] += dg
        dkv = jnp.concatenate(
            [part for h in range(MLA_HEADS)
             for part in (dk_ref[h * MLA_PAD:h * MLA_PAD + MLA_NOPE, :], dv_ref[h * HEAD_DIM:(h + 1) * HEAD_DIM, :])],
            axis=0).astype(BF16)
        dwkv_ref[...] += _dot(dkv, nkv_ref[...], NT)
        dx, dg = _col_rms_bwd(x[MLA_Q_RANK:KR_SLAB0, :], gkv_ref[...], _dot(wkv_ref[...], dkv))
        y_s[MLA_Q_RANK:KR_SLAB0, :] = dx
        dgkv_ref[...] += dg
        dkr = dk_ref[MLA_NOPE:MLA_PAD, :]
        for h in range(1, MLA_HEADS):
            dkr = dkr + dk_ref[h * MLA_PAD + MLA_NOPE:(h + 1) * MLA_PAD, :]
        dkr = jnp.concatenate([dkr, jnp.zeros((MLA_NOPE, tn), F32)], axis=0)
        y_s[KR_SLAB0:LAT_ROWS, :] = _rope_t(dkr, cm_ref[...], s1m_ref[...], s2m_ref[...])
        y = pltpu.roll(y_s[...], LAT_SHIFT, 0)
        row = lax.broadcasted_iota(jnp.int32, (LAT_ROWS, tn), 0)
        dfl = jnp.concatenate([dfl_ref[...], jnp.zeros((LAT_ROWS - GATE_ROWS, tn), F32)], axis=0)
        dlat_ref[...] = jnp.where(row < LAT_SHIFT, dfl, y).astype(BF16)

    def col(rows):
        return pl.BlockSpec((rows, tn), lambda i: (0, i))

    def full(a):
        return pl.BlockSpec(a.shape, lambda i: (0, 0))

    def acc(r, c):
        return pl.BlockSpec((r, c), lambda i: (0, 0))

    return pl.pallas_call(
        body, name=name, grid=(S // tn,),
        in_specs=[col(LAT_ROWS), col(MLA_Q_RANK), col(MLA_KV_RANK), full(g_q), full(g_kv),
                  full(w_uq_p), full(w_ukv), col(QW), col(QW), col(QW), col(LANES), col(LANES), col(LANES),
                  col(QW), col(QW), col(MLA_HEADS * HEAD_DIM), col(GATE_ROWS)],
        out_specs=[col(LAT_ROWS), acc(QW, MLA_Q_RANK), acc(QW, MLA_KV_RANK), acc(MLA_Q_RANK, 1), acc(MLA_KV_RANK, 1)],
        out_shape=[jax.ShapeDtypeStruct((LAT_ROWS, S), BF16), jax.ShapeDtypeStruct((QW, MLA_Q_RANK), F32),
                   jax.ShapeDtypeStruct((QW, MLA_KV_RANK), F32), jax.ShapeDtypeStruct((MLA_Q_RANK, 1), F32),
                   jax.ShapeDtypeStruct((MLA_KV_RANK, 1), F32)],
        scratch_shapes=[pltpu.VMEM((LAT_ROWS, tn), F32)],
        compiler_params=_params("arbitrary"),
    )(lat, nq, nkv, g_q, g_kv, w_uq_p, w_ukv, *tq, *tmisc, dq, dk, dv, dflog)


def _dproj_cast(dqa, dkva, dqf, dkf, dvf, dlat, *, name):
    S = dqa.shape[1]
    tn = _tile(S, 512)
    parts = (dqa, dkva, dqf, dkf, dvf, dlat)

    def body(*refs):
        o_ref = refs[-1]
        r0 = 0
        for ref in refs[:-1]:
            n = ref.shape[0]
            o_ref[r0:r0 + n, :] = ref[...].astype(BF16)
            r0 += n

    return pl.pallas_call(
        body, name=name, grid=(S // tn,),
        in_specs=[pl.BlockSpec((p.shape[0], tn), lambda i: (0, i)) for p in parts],
        out_specs=pl.BlockSpec((IN_ROWS, tn), lambda i: (0, i)),
        out_shape=jax.ShapeDtypeStruct((IN_ROWS, S), BF16),
        compiler_params=_params("parallel"),
    )(*parts)


GELU_C = math.sqrt(2.0 / math.pi)
GELU_A = 0.044715


HALO = 16


def _shift_down(a, k, fill):
    r = pltpu.roll(a, k, 0)
    row = lax.broadcasted_iota(jnp.int32, (8, a.shape[1]), 0)
    head = r[0:8, :]
    for i in range(k):
        head = jnp.where(row == i, fill[len(fill) - k + i], head)
    return jnp.concatenate([head, r[8:, :]], axis=0)


def _shift_up(d, k, fill):
    n = d.shape[0]
    r = pltpu.roll(d, n - k, 0)
    row = lax.broadcasted_iota(jnp.int32, (8, d.shape[1]), 0)
    tail = r[n - 8:n, :]
    for i in range(k):
        tail = jnp.where(row == 8 - k + i, fill[i], tail)
    return jnp.concatenate([r[0:n - 8, :], tail], axis=0)


def _conv_taps(a, before, w_ref, b_ref):
    a1 = _shift_down(a, 1, before)
    a2 = _shift_down(a, 2, before)
    return ((b_ref[...] + w_ref[0:1, :] * a2) + w_ref[1:2, :] * a1) + w_ref[2:3, :] * a


def _rows_before(halo_ref, first):
    h = halo_ref[HALO - 2:HALO, :].astype(F32)
    return jnp.where(first, 0.0, h[0:1, :]), jnp.where(first, 0.0, h[1:2, :])


def _conv_specs(S, tm, tc, nc):
    hb = tm // HALO
    main = lambda off: pl.BlockSpec((tm, tc), lambda j, i: (i, j + off))
    prev = lambda off: pl.BlockSpec((HALO, tc), lambda j, i: (jnp.maximum(i * hb - 1, 0), j + off))
    wspec = lambda off: pl.BlockSpec((3, tc), lambda j, i: (0, j + off))
    bspec = lambda off: pl.BlockSpec((1, tc), lambda j, i: (0, j + off))
    return main, prev, wspec, bspec


def _conv_geglu_fwd(a, conv_w, conv_b, *, name):
    S = a.shape[0]
    tm, tc = _tile(S, 512), _tile(D_FF, 1408)
    nc = D_FF // tc
    main, prev, wspec, bspec = _conv_specs(S, tm, tc, nc)

    def body(ag_ref, au_ref, hg_ref, hu_ref, wg_ref, wu_ref, bg_ref, bu_ref, u_ref, z_ref):
        first = pl.program_id(1) == 0
        gate = _conv_taps(ag_ref[...].astype(F32), _rows_before(hg_ref, first), wg_ref, bg_ref)
        up = _conv_taps(au_ref[...].astype(F32), _rows_before(hu_ref, first), wu_ref, bu_ref)
        u_ref[0] = gate
        u_ref[1] = up
        cdf = 0.5 * (1.0 + jnp.tanh(GELU_C * (gate + GELU_A * (gate * gate * gate))))
        z_ref[...] = (gate * cdf * up).astype(BF16)

    return pl.pallas_call(
        body, name=name, grid=(nc, S // tm),
        in_specs=[main(0), main(nc), prev(0), prev(nc), wspec(0), wspec(nc), bspec(0), bspec(nc)],
        out_specs=[pl.BlockSpec((2, tm, tc), lambda j, i: (0, i, j)), pl.BlockSpec((tm, tc), lambda j, i: (i, j))],
        out_shape=[jax.ShapeDtypeStruct((2, S, D_FF), F32), jax.ShapeDtypeStruct((S, D_FF), BF16)],
        compiler_params=_params("parallel", "arbitrary"),
    )(a, a, a, a, conv_w, conv_w, conv_b, conv_b)


def _geglu_bwd(gate, up, dz):
    g2x = gate * gate
    th = jnp.tanh(GELU_C * (gate + GELU_A * (g2x * gate)))
    cdf = 0.5 * (1.0 + th)
    dgelu = cdf + gate * (0.5 * (1.0 - th * th) * (GELU_C * (1.0 + 3.0 * GELU_A * g2x)))
    return dz * up * dgelu, dz * (gate * cdf)


def _conv_geglu_bwd(a, u, conv_w, dz, *, name):
    S = a.shape[0]
    tm, tc = _tile(S, 512), _tile(D_FF, 1408)
    nc = D_FF // tc
    nr = S // tm
    main, _, wspec, _ = _conv_specs(S, tm, tc, nc)
    hb = tm // 8

    def body(ag_ref, au_ref, u_ref, un_ref, wg_ref, wu_ref, dz_ref, dzn_ref, da_ref, dw_ref, db_ref):
        i = pl.program_id(1)
        last = i == nr - 1

        @pl.when(i == 0)
        def _():
            dw_ref[...] = jnp.zeros_like(dw_ref)
            db_ref[...] = jnp.zeros_like(db_ref)

        dus = _geglu_bwd(u_ref[0], u_ref[1], dz_ref[...])
        dus_n = _geglu_bwd(un_ref[0], un_ref[1], dzn_ref[...])
        for half, a_ref, w_ref in ((0, ag_ref, wg_ref), (1, au_ref, wu_ref)):
            du, du_n = dus[half], dus_n[half]
            after = (jnp.where(last, 0.0, du_n[0:1, :]), jnp.where(last, 0.0, du_n[1:2, :]))
            shifted = (_shift_up(du, 2, after), _shift_up(du, 1, after), du)
            da_ref[half] = (w_ref[2:3, :] * du + w_ref[1:2, :] * shifted[1] + w_ref[0:1, :] * shifted[0]).astype(BF16)
            af = a_ref[...].astype(F32)
            for tap in range(3):
                dw_ref[half, tap:tap + 1, :] += jnp.sum(shifted[tap] * af, axis=0, keepdims=True)
            db_ref[half] += jnp.sum(du, axis=0, keepdims=True)

    nxt8 = lambda j, i: (0, jnp.minimum((i + 1) * hb, S // 8 - 1), j)
    return pl.pallas_call(
        body, name=name, grid=(nc, nr),
        in_specs=[main(0), main(nc), pl.BlockSpec((2, tm, tc), lambda j, i: (0, i, j)), pl.BlockSpec((2, 8, tc), nxt8),
                  wspec(0), wspec(nc), pl.BlockSpec((tm, tc), lambda j, i: (i, j)),
                  pl.BlockSpec((8, tc), lambda j, i: (jnp.minimum((i + 1) * hb, S // 8 - 1), j))],
        out_specs=[pl.BlockSpec((2, tm, tc), lambda j, i: (0, i, j)), pl.BlockSpec((2, 3, tc), lambda j, i: (0, 0, j)),
                   pl.BlockSpec((2, 1, tc), lambda j, i: (0, 0, j))],
        out_shape=[jax.ShapeDtypeStruct((2, S, D_FF), BF16), jax.ShapeDtypeStruct((2, 3, D_FF), F32),
                   jax.ShapeDtypeStruct((2, 1, D_FF), F32)],
        compiler_params=_params("parallel", "arbitrary"),
    )(a, a, u, u, conv_w, conv_w, dz, dz)


ROW_BLOCK_BYTES = 1536 * 1024


def _row_tile(rows, cols):
    return rows if rows * cols * 4 <= ROW_BLOCK_BYTES else _tile(rows, ROW_TILE)


def _adamw_update(w, g, m, v):
    m = ADAM_B1 * m + (1.0 - ADAM_B1) * g
    v = ADAM_B2 * v + (1.0 - ADAM_B2) * jnp.square(g)
    m_hat = m / (1.0 - ADAM_B1 ** ADAM_STEP)
    v_hat = v / (1.0 - ADAM_B2 ** ADAM_STEP)
    return -ADAM_LR * (m_hat / (jnp.sqrt(v_hat) + ADAM_EPS) + ADAM_WD * w), m, v


def _adamw(w, g, m, v, *, name):
    L, A, B = w.shape
    ta = _tile(A, ROW_TILE)

    def body(w_ref, g_ref, m_ref, v_ref, d_ref, mo_ref, vo_ref):
        d_ref[...], mo_ref[...], vo_ref[...] = _adamw_update(w_ref[...], g_ref[...], m_ref[...], v_ref[...])

    blk = pl.BlockSpec((None, ta, B), lambda l, i: (l, i, 0))
    shp = jax.ShapeDtypeStruct((L, A, B), F32)
    return pl.pallas_call(
        body, name=name, grid=(L, A // ta),
        in_specs=[blk] * 4, out_specs=[blk] * 3, out_shape=[shp] * 3,
        compiler_params=_params("parallel", "parallel"),
    )(w, g, m, v)


def _scalar(v):
    return jnp.reshape(v, (1,)).astype(jnp.int32)


def _adamw_halves(w, g_mine, g_other, m, v, *, name):
    L, A, B = w.shape
    ta = _row_tile(A // 2, B)
    nb = A // 2 // ta

    def body(c_ref, w_ref, gm_ref, go_ref, m_ref, v_ref, g_ref, d_ref, mo_ref, vo_ref):
        g = jnp.where(pl.program_id(1) // nb == c_ref[0], gm_ref[...], go_ref[...])
        g_ref[...] = g
        d_ref[...], mo_ref[...], vo_ref[...] = _adamw_update(w_ref[...], g, m_ref[...], v_ref[...])

    blk = pl.BlockSpec((None, ta, B), lambda l, i, c_ref: (l, i, 0))
    half = pl.BlockSpec((None, ta, B), lambda l, i, c_ref: (l, i % nb, 0))
    shp = jax.ShapeDtypeStruct((L, A, B), F32)
    return pl.pallas_call(
        body, name=name,
        grid_spec=pltpu.PrefetchScalarGridSpec(num_scalar_prefetch=1, grid=(L, A // ta),
                                               in_specs=[blk, half, half, blk, blk], out_specs=[blk] * 4),
        out_shape=[shp] * 4,
        compiler_params=_params("parallel", "parallel"),
    )(_scalar(lax.axis_index("c")), w, g_mine, g_other, m, v)


def _chip_index():
    return 2 * lax.axis_index("x") + lax.axis_index("y")


def _pair_sum(g, recv, *, name):
    n, A, B = g.shape
    ta = _row_tile(A // 2, B)
    nb = A // 2 // ta

    def body(c_ref, g_ref, r_ref, o_ref):
        o_ref[...] = g_ref[...] + r_ref[...]

    return pl.pallas_call(
        body, name=name,
        grid_spec=pltpu.PrefetchScalarGridSpec(
            num_scalar_prefetch=1, grid=(n, nb),
            in_specs=[pl.BlockSpec((None, ta, B), lambda s, r, c_ref: (s, c_ref[0] * nb + r, 0)),
                      pl.BlockSpec((None, ta, B), lambda s, r, c_ref: (s, r, 0))],
            out_specs=pl.BlockSpec((None, ta, B), lambda s, r, c_ref: (s, r, 0))),
        out_shape=jax.ShapeDtypeStruct((n, A // 2, B), F32),
        compiler_params=_params("parallel", "parallel"),
    )(_scalar(lax.axis_index("c")), g, recv)


def _chip_sum(landed, own, *, name):
    n, A2, B = landed.shape
    ta = _row_tile(A2, B)

    def body(me_ref, *refs):
        slots, own_ref, o_ref = refs[:n], refs[n], refs[n + 1]
        parts = [jnp.where(me_ref[0] == s, own_ref[...], slots[s][...]) for s in range(n)]
        o_ref[...] = ((parts[0] + parts[1]) + parts[2]) + parts[3]

    def slot(s):
        return pl.BlockSpec((None, ta, B), lambda r, me_ref: (jnp.where(me_ref[0] == s, (s + 1) % n, s), r, 0))

    return pl.pallas_call(
        body, name=name,
        grid_spec=pltpu.PrefetchScalarGridSpec(
            num_scalar_prefetch=1, grid=(A2 // ta,),
            in_specs=[slot(s) for s in range(n)] + [pl.BlockSpec((None, ta, B), lambda r, me_ref: (me_ref[0], r, 0))],
            out_specs=pl.BlockSpec((ta, B), lambda r, me_ref: (r, 0))),
        out_shape=jax.ShapeDtypeStruct((A2, B), F32),
        compiler_params=_params("parallel"),
    )(_scalar(_chip_index()), *([landed] * n), own)


HBM_SPEC = pl.BlockSpec(memory_space=pl.ANY)
COMM_PARAMS = pltpu.CompilerParams(has_side_effects=True)


def _mesh_pos():
    return lax.axis_index("x"), lax.axis_index("y"), lax.axis_index("c")


def _other_chips(x, y):
    return [(1 - x, y), (x, 1 - y), (1 - x, 1 - y)]


def _remote(src, dst, send_sems, recv_sems, k, to):
    return pltpu.make_async_remote_copy(src_ref=src, dst_ref=dst, send_sem=send_sems.at[k], recv_sem=recv_sems.at[k],
                                        device_id=to, device_id_type=MESH)


def _place_own(gathered, shards, *, name):
    n = len(shards)

    def body(me_ref, *refs):
        for s_ref, o_ref in zip(refs[:n], refs[2 * n:]):
            o_ref[...] = s_ref[...]

    return pl.pallas_call(
        body, name=name,
        grid_spec=pltpu.PrefetchScalarGridSpec(
            num_scalar_prefetch=1, grid=(1,),
            in_specs=[pl.BlockSpec(s.shape, lambda i, me_ref: (0, 0)) for s in shards] + [HBM_SPEC] * n,
            out_specs=[pl.BlockSpec((None,) + s.shape, lambda i, me_ref: (me_ref[0], 0, 0)) for s in shards]),
        out_shape=[jax.ShapeDtypeStruct(g.shape, g.dtype) for g in gathered],
        input_output_aliases={1 + n + k: k for k in range(n)},
        compiler_params=_params("arbitrary"),
    )(_scalar(_chip_index()), *shards, *gathered)


def _half_rows(rows, c, align=8):
    assert (rows // 2) % align == 0
    return pl.ds(pl.multiple_of(c * (rows // 2), align), rows // 2)


BF16_ROWS = 16


def _halved(rows):
    return rows % (2 * BF16_ROWS) == 0


def _gather_copies(srcs, lands, send_sems, recv_sems):
    x, y, c = _mesh_pos()
    me = 2 * x + y
    out = []
    for k in range(len(srcs)):
        a = srcs[k].shape[0]
        rows = _half_rows(a, c, BF16_ROWS) if _halved(a) else pl.ds(0, a)
        for j, (px, py) in enumerate(_other_chips(x, y)):
            send = _remote(srcs[k].at[rows], lands[k].at[me, rows], send_sems, recv_sems, 3 * k + j, (px, py, c))
            recv = _remote(srcs[k].at[rows], lands[k].at[2 * px + py, rows], send_sems, recv_sems, 3 * k + j, (px, py, c))
            out.append((send, recv))
    return out


def _gather_start(srcs):
    nl, n = len(srcs), len(srcs[0])
    lands = [[lax.empty((N_CHIPS,) + s.shape, s.dtype) for s in sl] for sl in srcs]
    flat = [a for l in range(nl) for a in srcs[l] + lands[l]]

    def body(*refs):
        bufs, sems, token = refs[:len(flat)], refs[len(flat):len(flat) + 2 * nl], refs[-1]
        for l in range(nl):
            mine = bufs[2 * n * l:2 * n * (l + 1)]
            for send, _ in _gather_copies(mine[:n], mine[n:], sems[2 * l], sems[2 * l + 1]):
                send.start()
        token[...] = jnp.zeros_like(token)

    res = pl.pallas_call(
        body, name="weight_gather_start",
        in_specs=[HBM_ONLY] * len(flat),
        out_specs=[SEM_SPEC] * (2 * nl) + [HBM_ONLY] * len(flat) + [pl.BlockSpec(memory_space=pltpu.VMEM)],
        out_shape=[pltpu.SemaphoreType.DMA((3 * n,))] * (2 * nl) + [pltpu.HBM(a.shape, a.dtype) for a in flat]
        + [jax.ShapeDtypeStruct((8, LANES), F32)],
        input_output_aliases={i: 2 * nl + i for i in range(len(flat))},
        compiler_params=SPLIT_PARAMS,
    )(*[pltpu.with_memory_space_constraint(a, pltpu.HBM) for a in flat])
    bufs = res[2 * nl:2 * nl + len(flat)]
    state = [(res[2 * l], res[2 * l + 1], list(bufs[2 * n * l:2 * n * l + n]), list(bufs[2 * n * l + n:2 * n * (l + 1)]))
             for l in range(nl)]
    return state, res[-1][0:1, 0:1]


def _gather_wait(state, after, *, name):
    send_sems, recv_sems, srcs, lands = state
    n = len(srcs)

    def body(*refs):
        for send, recv in _gather_copies(refs[:n], refs[n:2 * n], refs[2 * n], refs[2 * n + 1]):
            send.wait_send()
            recv.wait_recv()

    res = pl.pallas_call(
        body, name=name,
        in_specs=[HBM_ONLY] * (2 * n) + [SEM_SPEC, SEM_SPEC, HBM_SPEC],
        out_specs=[HBM_ONLY] * (2 * n),
        out_shape=[pltpu.HBM(a.shape, a.dtype) for a in srcs + lands],
        input_output_aliases={i: i for i in range(2 * n)},
        compiler_params=SPLIT_PARAMS,
    )(*srcs, *lands, send_sems, recv_sems, after)
    return list(res[:n]), list(res[n:])


def _gather_forward(lands, *, name):
    n = len(lands)

    def body(*refs):
        bufs, outs = refs[:n], refs[n:2 * n]
        send_sems, recv_sems = refs[2 * n:]
        x, y, c = _mesh_pos()
        copies, waits = [], []
        for k in range(n):
            a = lands[k].shape[1]
            if not _halved(a):
                continue
            for j, (px, py) in enumerate(_other_chips(x, y)):
                mine = 2 * px + py, _half_rows(a, c, BF16_ROWS)
                copies.append(_remote(bufs[k].at[mine], outs[k].at[mine], send_sems, recv_sems, 3 * k + j, (x, y, 1 - c)))
                lands_here = outs[k].at[2 * px + py, _half_rows(a, 1 - c, BF16_ROWS)]
                waits.append(_remote(lands_here, lands_here, send_sems, recv_sems, 3 * k + j, (x, y, 1 - c)))
        for cp in copies:
            cp.start()
        for cp in waits:
            cp.wait_recv()
        for cp in copies:
            cp.wait_send()

    return pl.pallas_call(
        body, name=name,
        in_specs=[HBM_SPEC] * n, out_specs=[HBM_SPEC] * n,
        out_shape=[jax.ShapeDtypeStruct(a.shape, a.dtype) for a in lands],
        scratch_shapes=[pltpu.SemaphoreType.DMA((3 * n,)), pltpu.SemaphoreType.DMA((3 * n,))],
        input_output_aliases={i: i for i in range(n)},
        compiler_params=COMM_PARAMS,
    )(*lands)


def _sibling_exchange(gs, *, name):
    n = len(gs)

    def body(*refs):
        ins, outs = refs[:n], refs[n:2 * n]
        send_sems, recv_sems = refs[2 * n:]
        x, y, c = _mesh_pos()
        copies = [_remote(ins[k].at[:, _half_rows(gs[k].shape[1], 1 - c)], outs[k], send_sems, recv_sems, k, (x, y, 1 - c))
                  for k in range(n)]
        for cp in copies:
            cp.start()
        for cp in copies:
            cp.wait()

    return pl.pallas_call(
        body, name=name,
        in_specs=[HBM_SPEC] * n, out_specs=[HBM_SPEC] * n,
        out_shape=[jax.ShapeDtypeStruct((g.shape[0], g.shape[1] // 2, g.shape[2]), g.dtype) for g in gs],
        scratch_shapes=[pltpu.SemaphoreType.DMA((n,)), pltpu.SemaphoreType.DMA((n,))],
        compiler_params=COMM_PARAMS,
    )(*gs)


HBM_ONLY = pl.BlockSpec(memory_space=pltpu.HBM)
SEM_SPEC = pl.BlockSpec(memory_space=pltpu.SEMAPHORE)
SPLIT_PARAMS = pltpu.CompilerParams(has_side_effects=pltpu.SideEffectType.DATAFLOW_SIDE_EFFECTING)


def _scatter_copies(srcs, lands, send_sems, recv_sems):
    x, y, c = _mesh_pos()
    me = 2 * x + y
    out = []
    for k in range(len(srcs)):
        for j, (px, py) in enumerate(_other_chips(x, y)):
            s = 2 * px + py
            send = _remote(srcs[k].at[s], lands[k].at[me], send_sems, recv_sems, 3 * k + j, (px, py, c))
            recv = _remote(srcs[k].at[s], lands[k].at[s], send_sems, recv_sems, 3 * k + j, (px, py, c))
            out.append((send, recv))
    return out


def _exchange_copies(srcs, lands, send_sems, recv_sems):
    x, y, c = _mesh_pos()
    out = []
    for k in range(len(srcs)):
        cp = _remote(srcs[k].at[:, _half_rows(srcs[k].shape[1], 1 - c)], lands[k], send_sems, recv_sems, k, (x, y, 1 - c))
        out.append((cp, cp))
    return out


def _split_start(srcs, land_shapes, copies, n_sems, *, name):
    n = len(srcs)
    lands = [lax.empty(shape, s.dtype) for shape, s in zip(land_shapes, srcs)]

    def body(*refs):
        ins, zones = refs[:n], refs[n:2 * n]
        send_sems, recv_sems, token = refs[2 * n], refs[2 * n + 1], refs[-1]
        for send, _ in copies(ins, zones, send_sems, recv_sems):
            send.start()
        token[...] = jnp.zeros_like(token)

    hbm = lambda a: pltpu.HBM(a.shape, a.dtype)
    res = pl.pallas_call(
        body, name=name,
        in_specs=[HBM_ONLY] * (2 * n),
        out_specs=[SEM_SPEC, SEM_SPEC] + [HBM_ONLY] * (2 * n) + [pl.BlockSpec(memory_space=pltpu.VMEM)],
        out_shape=[pltpu.SemaphoreType.DMA((n_sems,)), pltpu.SemaphoreType.DMA((n_sems,))] + [hbm(a) for a in srcs + lands]
        + [jax.ShapeDtypeStruct((8, LANES), F32)],
        input_output_aliases={i: 2 + i for i in range(2 * n)},
        compiler_params=SPLIT_PARAMS,
    )(*[pltpu.with_memory_space_constraint(a, pltpu.HBM) for a in srcs + lands])
    return (res[0], res[1], list(res[2:2 + n]), list(res[2 + n:2 + 2 * n])), res[-1][0:1, 0:1]


def _scatter_start(ps, *, name):
    return _split_start(ps, [p.shape for p in ps], _scatter_copies, 3 * len(ps), name=name)


def _exchange_start(gs, *, name):
    return _split_start(gs, [(g.shape[0], g.shape[1] // 2, g.shape[2]) for g in gs], _exchange_copies, len(gs), name=name)


def _split_wait(started, copies, after, *, name):
    ng = len(started)
    sizes = [len(st[2]) for st in started]
    offs = [2 * sum(sizes[:i]) for i in range(ng + 1)]
    flat = [a for (_, _, ps, lands) in started for a in ps + lands]

    def body(*refs):
        bufs, sems = refs[:len(flat)], refs[len(flat):len(flat) + 2 * ng]
        for i, n in enumerate(sizes):
            srcs, zones = bufs[offs[i]:offs[i] + n], bufs[offs[i] + n:offs[i + 1]]
            for send, recv in copies(srcs, zones, sems[2 * i], sems[2 * i + 1]):
                send.wait_send()
                recv.wait_recv()

    res = pl.pallas_call(
        body, name=name,
        in_specs=[HBM_ONLY] * len(flat) + [SEM_SPEC] * (2 * ng) + [HBM_SPEC],
        out_specs=[HBM_ONLY] * len(flat),
        out_shape=[pltpu.HBM(a.shape, a.dtype) for a in flat],
        input_output_aliases={i: i for i in range(len(flat))},
        compiler_params=SPLIT_PARAMS,
    )(*flat, *[s for (ss, rs, _, _) in started for s in (ss, rs)], after)
    return [(list(res[offs[i]:offs[i] + n]), list(res[offs[i] + n:offs[i + 1]])) for i, n in enumerate(sizes)]


def _sibling_share(hs):
    n = len(hs)

    def body(*refs):
        ins, outs = refs[:n], refs[n:2 * n]
        send_sems, recv_sems = refs[2 * n:]
        x, y, c = _mesh_pos()
        copies = [_remote(ins[k], outs[k], send_sems, recv_sems, k, (x, y, 1 - c)) for k in range(n)]
        for cp in copies:
            cp.start()
        for cp in copies:
            cp.wait()

    return pl.pallas_call(
        body, name="grad_sibling_share",
        in_specs=[HBM_SPEC] * n, out_specs=[HBM_SPEC] * n,
        out_shape=[jax.ShapeDtypeStruct(h.shape, h.dtype) for h in hs],
        scratch_shapes=[pltpu.SemaphoreType.DMA((n,)), pltpu.SemaphoreType.DMA((n,))],
        compiler_params=COMM_PARAMS,
    )(*hs)


def _allreduce_small(part):
    rows, C = part.shape

    def body(p_ref, o_ref, slots, send_sems, recv_sems):
        x, y, c = _mesh_pos()
        me = 4 * x + 2 * y + c
        slots[me] = p_ref[...]
        copies = []
        for k in range(1, 8):
            kx, ky, kc = (k >> 2) & 1, (k >> 1) & 1, k & 1
            peer = (x ^ kx if kx else x, y ^ ky if ky else y, c ^ kc if kc else c)
            cp = _remote(p_ref, slots.at[me], send_sems, recv_sems, k - 1, peer)
            cp.start()
            copies.append((cp, peer))
        for k, (cp, peer) in enumerate(copies):
            src = 4 * peer[0] + 2 * peer[1] + peer[2]
            _remote(p_ref, slots.at[src], send_sems, recv_sems, k, peer).wait_recv()
        for cp, _ in copies:
            cp.wait_send()
        total = slots[0]
        for d in range(1, 8):
            total = total + slots[d]
        o_ref[...] = total

    return pl.pallas_call(
        body, name="small_grad_allreduce",
        in_specs=[pl.BlockSpec(memory_space=pltpu.VMEM)], out_specs=pl.BlockSpec(memory_space=pltpu.VMEM),
        out_shape=jax.ShapeDtypeStruct((rows, C), F32),
        scratch_shapes=[pltpu.VMEM((8, rows, C), F32), pltpu.SemaphoreType.DMA((7,)), pltpu.SemaphoreType.DMA((7,))],
        compiler_params=pltpu.CompilerParams(has_side_effects=True, vmem_limit_bytes=VMEM_LIMIT_BYTES),
    )(part)


def _pad_w_uq(w):
    lead = w.shape[:-1]
    w = w.reshape(lead + (MLA_HEADS, MLA_QK))
    w = jnp.concatenate([w, jnp.zeros(lead + (MLA_HEADS, MLA_PAD - MLA_QK), w.dtype)], axis=-1)
    return w.reshape(lead + (MLA_HEADS * MLA_PAD,))


def _unpad_w_uq(g):
    lead = g.shape[:-1]
    return g.reshape(lead + (MLA_HEADS, MLA_PAD))[..., :MLA_QK].reshape(lead + (MLA_HEADS * MLA_QK,))


def _t(a):
    return jnp.swapaxes(a, -1, -2)


def _shards_of_cols(w):
    A, NB = w.shape
    return w.reshape(A, N_CHIPS, NB // N_CHIPS).transpose(1, 0, 2)


BIG = ("w_in", "w_uq", "w_ukv", "w_out", "w_up", "w_down")
SMALL = ("attn_pre_norm", "forget_bias", "swa_sinks", "rel_bias", "q_latent_norm", "kv_latent_norm", "group_norm",
         "attn_post_norm", "ffn_pre_norm", "conv_b", "ffn_post_norm")
WEIGHTS = ("attn_pre_norm", "w_in", "forget_bias", "swa_sinks", "rel_bias", "q_latent_norm", "w_uq", "kv_latent_norm",
           "w_ukv", "group_norm", "w_out", "attn_post_norm", "ffn_pre_norm", "w_up", "conv_w", "conv_b", "w_down",
           "ffn_post_norm")


PACK_UNIT = 8 * LANES


def _pack_rows(shape):
    return -(-int(np.prod(shape)) // PACK_UNIT) * 8


def _pack(arrs, row_mult=8):
    parts = []
    for a in arrs:
        n = int(np.prod(a.shape))
        parts.append(jnp.pad(a.reshape(-1), (0, _pack_rows(a.shape) * LANES - n)).reshape(-1, LANES))
    rows = sum(p.shape[0] for p in parts)
    pad = -rows % row_mult
    if pad:
        parts.append(jnp.zeros((pad, LANES), parts[0].dtype))
    return jnp.concatenate(parts, axis=0)


def _unpack(packed, shapes):
    packed = packed.reshape(-1, LANES)
    out, off = [], 0
    for shp in shapes:
        r = _pack_rows(shp)
        out.append(packed[off:off + r].reshape(-1)[:int(np.prod(shp))].reshape(shp))
        off += r
    return out


LAYER_KEYS = ("w_qkv_t", "w_lat_t", "w_in_t", "w_uq_p", "w_uq_t", "w_ukv", "w_ukv_t", "w_out", "w_up", "w_down", "conv_w")


def _layer_weights(gathered):
    cols = lambda g: g.transpose(1, 0, 2).reshape(g.shape[1], N_CHIPS * g.shape[2])
    w_in_t = _t(gathered["w_in"]).reshape(IN_COLS, D_MODEL)
    w_in_t = jnp.pad(w_in_t, ((0, IN_ROWS - IN_COLS), (0, 0)))
    w_uq_p = _pad_w_uq(cols(gathered["w_uq"]))
    w_ukv = cols(gathered["w_ukv"])
    return dict(w_qkv_t=w_in_t[:QKV_ROWS], w_lat_t=w_in_t[QKV_ROWS:], w_in_t=w_in_t, w_uq_p=w_uq_p, w_uq_t=_t(w_uq_p),
                w_ukv=w_ukv, w_ukv_t=_t(w_ukv), w_out=gathered["w_out"].reshape(D_MODEL, D_MODEL), w_up=gathered["w_up"],
                w_down=gathered["w_down"].reshape(D_FF, D_MODEL), conv_w=cols(gathered["conv_w"]))


def _local_step(x, target, W, layer_weights, layer_done):
    W = dict(W, **{key: [None] * DEPTH for key in LAYER_KEYS})
    S = x.shape[0]
    tq_tabs, tm_tabs = _rope_tables(S)
    onehot_t = _rel_onehot_t()
    bias_t = _bias_table(W["rel_bias"].T, onehot_t).reshape(SWA_KV_HEADS, SWA_GROUP, 2 * WINDOW, WINDOW)
    bias_t = bias_t.transpose(0, 2, 1, 3).reshape(SWA_KV_HEADS, 2 * WINDOW, GW)
    row = lambda a: a.reshape(1, -1)
    col = lambda a: a.reshape(-1, 1)
    fox_rows = (FOX_ROW0, FOX_ROW0 + FOX_HEADS * HEAD_DIM, FOX_ROW0 + 2 * FOX_HEADS * HEAD_DIM, SWA_Q_HEADS)
    fox = dict(rows=fox_rows, H=FOX_HEADS, Dk=HEAD_DIM, Dv=HEAD_DIM, scale=HEAD_DIM ** -0.5)
    mla = dict(rows=(0, 0, 0, SWA_Q_HEADS + FOX_HEADS), H=MLA_HEADS, Dk=MLA_PAD, Dv=HEAD_DIM, scale=MLA_QK ** -0.5)

    saved = []
    h = _rms_fwd(x, row(W["attn_pre_norm"][0]), name="rms_in")
    for l in range(DEPTH):
        sv = {"x0": x, "h1": h}
        for key, val in layer_weights(l, h).items():
            W[key][l] = val
        qkv = _matmul(W["w_qkv_t"][l], h, tb=True, out_dtype=BF16, name="proj_qkv")
        lat = _matmul(W["w_lat_t"][l], h, tb=True, name="proj_lat")
        oa, lse_a = _swa_fwd(qkv, bias_t, W["swa_sinks"][l], name="swa_fwd")
        fb_col = jnp.pad(col(W["forget_bias"][l]), ((0, GATE_ROWS - FOX_HEADS), (0, 0)))
        f4 = _gate_fwd(lat, fb_col, name="fox_gate_fwd")[:FOX_HEADS]
        f2 = f4 * LOG2E
        f_row, f_col = f2[:, None, :], f2.T
        of, lse_f = _attn_fwd(qkv, qkv, qkv, f_row=f_row, f_col=f_col, name="fox_fwd", **fox)
        nq, nkv, qm, km, vm = _mla_prep_fwd(lat, col(W["q_latent_norm"][l]), col(W["kv_latent_norm"][l]), W["w_uq_t"][l],
                                            W["w_ukv_t"][l], tq_tabs, tm_tabs, name="mla_prep_fwd")
        oc, lse_c = _attn_fwd(qm, km, vm, name="mla_fwd", **mla)
        mixed = _group_norm_fwd(oa, of, oc, col(W["group_norm"][l]), name="group_norm_fwd")
        y = _matmul(mixed, W["w_out"][l], ta=True, name="proj_out")
        x1, h2 = _resid_rms(x, y, row(W["attn_post_norm"][l]), row(W["ffn_pre_norm"][l]), name="attn_resid")
        a = _matmul(h2, W["w_up"][l], b_shards=True, out_dtype=BF16, name="ffn_up")
        u, z = _conv_geglu_fwd(a, W["conv_w"][l], row(W["conv_b"][l]), name="conv_geglu_fwd")
        y2 = _matmul(z, W["w_down"][l], name="ffn_down")
        g_next = row(W["attn_pre_norm"][l + 1]) if l + 1 < DEPTH else None
        x2, h_next = _resid_rms(x1, y2, row(W["ffn_post_norm"][l]), g_next, name="ffn_resid")
        sv.update(qkv=qkv, lat=lat, oa=oa, lse_a=lse_a, fb_col=fb_col, f_row=f_row, f_col=f_col, of=of, lse_f=lse_f,
                  nq=nq, nkv=nkv, qm=qm, km=km, vm=vm, oc=oc, lse_c=lse_c, mixed=mixed, y=y, x1=x1, h2=h2, a=a, u=u, z=z, y2=y2)
        saved.append(sv)
        x, h = x2, h_next

    loss, dx = _loss_head(x, target)

    G = {k: [None] * DEPTH for k in WEIGHTS if k != "rel_bias" and k not in BIG}
    dbias_layers = [None] * DEPTH
    for l in reversed(range(DEPTH)):
        sv = saved[l]
        gb = {}
        if l == DEPTH - 1:
            dy2, dg = _rms_bwd(sv["y2"], row(W["ffn_post_norm"][l]), dx, out_dtype=BF16, name="ffn_post_bwd")
            G["ffn_post_norm"][l] = dg[0]
        dz = _matmul(dy2, W["w_down"][l], tb=True, name="ffn_down_dx")
        gb["w_down"] = _matmul(sv["z"], dy2, ta=True, name="ffn_down_dw").reshape(N_CHIPS, D_FF // N_CHIPS, D_MODEL)
        da, dcw, dcb = _conv_geglu_bwd(sv["a"], sv["u"], W["conv_w"][l], dz, name="conv_geglu_bwd")
        G["conv_w"][l] = dcw.transpose(1, 0, 2).reshape(3, 2 * D_FF)
        G["conv_b"][l] = dcb.reshape(2 * D_FF)
        dh2 = _matmul(da, W["w_up"][l], tb=True, b_shards=True, a_halves=True, name="ffn_up_dx")
        gb["w_up"] = _matmul(sv["h2"], da, ta=True, out_shards=True, b_halves=True, name="ffn_up_dw")
        token = layer_done(l, gb)
        gb = {}
        dx1, dg, dy, dg_post = _rms_bwd(sv["x1"], row(W["ffn_pre_norm"][l]) + token, dh2, resid=dx, out_dtype=F32,
                                        then=(sv["y"], row(W["attn_post_norm"][l])), name="ffn_pre_bwd")
        G["ffn_pre_norm"][l] = dg[0]
        G["attn_post_norm"][l] = dg_post[0]
        dmixed = _matmul(W["w_out"][l], dy, tb=True, name="proj_out_dx")
        gb["w_out"] = _matmul(sv["mixed"], dy, name="proj_out_dw").reshape(N_CHIPS, D_MODEL // N_CHIPS, D_MODEL)
        doa, dof, doc, dg, delta = _group_norm_bwd(sv["oa"], sv["of"], sv["oc"], col(W["group_norm"][l]), dmixed,
                                                   name="group_norm_bwd")
        G["group_norm"][l] = dg[:, 0]
        dqa, dkva, dbias_l, dsink = _swa_bwd(sv["qkv"], bias_t, W["swa_sinks"][l], doa, sv["lse_a"],
                                             delta.reshape(-1, S), name="swa_bwd")
        dbias_layers[l] = (dbias_l.reshape(SWA_KV_HEADS, 2 * WINDOW, SWA_GROUP, WINDOW).transpose(0, 2, 1, 3)
                           .reshape(SWA_Q_HEADS, -1))
        G["swa_sinks"][l] = dsink[:, 0]
        dqf, dkf, dvf, dfk = _attn_bwd(sv["qkv"], sv["qkv"], sv["qkv"], do=dof, lse=sv["lse_f"], delta=delta,
                                       f_row=sv["f_row"], f_col=sv["f_col"], name="fox_bwd", **fox)
        dF = jnp.pad(dfk.T, ((0, GATE_ROWS - FOX_HEADS), (0, 0)))
        dflog, dfb = _gate_bwd(sv["lat"], sv["fb_col"], dF, name="fox_gate_bwd")
        G["forget_bias"][l] = dfb[:FOX_HEADS, 0]
        dqm, dkm, dvm = _attn_bwd(sv["qm"], sv["km"], sv["vm"], do=doc, lse=sv["lse_c"], delta=delta, name="mla_bwd", **mla)
        dlat, dwq_t, dwkv_t, dgq, dgkv = _mla_prep_bwd(
            sv["lat"], sv["nq"], sv["nkv"], col(W["q_latent_norm"][l]), col(W["kv_latent_norm"][l]), W["w_uq_p"][l],
            W["w_ukv"][l], tq_tabs, tm_tabs, dqm, dkm, dvm, dflog, name="mla_prep_bwd")
        gb["w_uq"], gb["w_ukv"] = _shards_of_cols(_unpad_w_uq(dwq_t.T)), _shards_of_cols(dwkv_t.T)
        G["q_latent_norm"][l], G["kv_latent_norm"][l] = dgq[:, 0], dgkv[:, 0]
        dproj = _dproj_cast(dqa, dkva, dqf, dkf, dvf, dlat, name="dproj_cast")
        dh1 = _matmul(dproj, W["w_in_t"][l], ta=True, name="proj_in_dx")
        dw_in_t = _matmul(dproj, sv["h1"], name="proj_in_dw")
        gb["w_in"] = _t(dw_in_t[:IN_COLS].reshape(N_CHIPS, IN_COLS // N_CHIPS, D_MODEL))
        token = layer_done(l, gb)
        below = (saved[l - 1]["y2"], row(W["ffn_post_norm"][l - 1])) if l > 0 else None
        res = _rms_bwd(sv["x0"], row(W["attn_pre_norm"][l]) + token, dh1, resid=dx1, out_dtype=F32, then=below,
                       name="attn_pre_bwd")
        dx, G["attn_pre_norm"][l] = res[0], res[1][0]
        if l > 0:
            dy2, G["ffn_post_norm"][l - 1] = res[2], res[3][0]

    grads = {k: jnp.stack(v) for k, v in G.items()}
    grads["rel_bias"] = _bias_table_bwd(jnp.stack(dbias_layers), onehot_t).T
    return loss, dx, grads


def kernel(x, attn_pre_norm, w_in, forget_bias, swa_sinks, rel_bias, q_latent_norm, w_uq, kv_latent_norm, w_ukv, group_norm, w_out, attn_post_norm, ffn_pre_norm, w_up, conv_w, conv_b, w_down, ffn_post_norm, loss_target, m_attn_pre_norm, m_w_in, m_forget_bias, m_swa_sinks, m_rel_bias, m_q_latent_norm, m_w_uq, m_kv_latent_norm, m_w_ukv, m_group_norm, m_w_out, m_attn_post_norm, m_ffn_pre_norm, m_w_up, m_conv_w, m_conv_b, m_w_down, m_ffn_post_norm, v_attn_pre_norm, v_w_in, v_forget_bias, v_swa_sinks, v_rel_bias, v_q_latent_norm, v_w_uq, v_kv_latent_norm, v_w_ukv, v_group_norm, v_w_out, v_attn_post_norm, v_ffn_pre_norm, v_w_up, v_conv_w, v_conv_b, v_w_down, v_ffn_post_norm):
    args = dict(locals())
    w = {k: args[k] for k in WEIGHTS}
    m = {k: args["m_" + k] for k in WEIGHTS}
    v = {k: args["v_" + k] for k in WEIGHTS}

    sent = BIG + ("conv_w",)
    gather_state, token = _gather_start([[w[k][l] if k == "conv_w" else w[k][l].astype(BF16) for k in sent]
                                         for l in range(DEPTH)])
    W = {k: w[k] for k in SMALL}
    W["attn_pre_norm"] = W["attn_pre_norm"] + token

    def layer_weights(l, after):
        srcs, lands = _gather_wait(gather_state[l], after, name=f"weight_gather_wait_{l}")
        lands = _gather_forward(lands, name=f"weight_gather_forward_{l}")
        lands = _place_own(lands, srcs, name="place_own_shards")
        return _layer_weights(dict(zip(sent, lands)))

    started, groups, pending = [], [], []

    def to_chips(l, keys, gs, recv, tag):
        pair = [_pair_sum(gk, rk, name="grad_pair_sum") for gk, rk in zip(gs, recv)]
        state, token = _scatter_start(pair, name="grad_scatter_start_" + tag)
        started.append(state)
        groups.append((l, keys))
        return token

    def finish_pending(after):
        l, keys, tag, state = pending.pop()
        gs, recv = _split_wait([state], _exchange_copies, after, name="grad_exchange_wait_" + tag)[0]
        return to_chips(l, keys, gs, recv, tag)

    def layer_done(l, gb):
        keys = [k for k in BIG if k in gb]
        gs = [gb[k] for k in keys]
        tag = f"{l}_{keys[0]}"
        token = finish_pending(gs[0]) if pending else 0.0
        if l == 0:
            return token + to_chips(l, keys, gs, _sibling_exchange(gs, name="grad_sibling_exchange_" + tag), tag)
        state, started_token = _exchange_start(gs, name="grad_exchange_start_" + tag)
        pending.append((l, keys, tag, state))
        return token + started_token

    loss_part, dx, g = _local_step(x[0], loss_target[0], W, layer_weights, layer_done)
    loss = lax.psum(loss_part, ("x", "y", "c"))

    reduced = {}
    for (l, keys), (pair, zones) in zip(groups, _split_wait(started, _scatter_copies, dx, name="grad_scatter_wait")):
        for k, p, z in zip(keys, pair, zones):
            reduced[k, l] = _chip_sum(z, p, name="grad_chip_sum")
    mine = [jnp.stack([reduced[k, l] for l in range(DEPTH)]) for k in BIG]
    other = _sibling_share(mine)
    out_g, out_d, out_m, out_v = {}, {}, {}, {}
    for k, g_mine, g_other in zip(BIG, mine, other):
        out_g[k], out_d[k], out_m[k], out_v[k] = _adamw_halves(w[k], g_mine, g_other, m[k], v[k], name="adamw_" + k)

    small_shapes = [w[k].shape for k in SMALL]
    reduced = _allreduce_small(_pack([g[k] for k in SMALL] + [g["conv_w"]]))
    *g_small, g_cw = _unpack(reduced, small_shapes + [g["conv_w"].shape])
    chip = 2 * lax.axis_index("x") + lax.axis_index("y")
    g_small.append(lax.dynamic_slice_in_dim(g_cw, chip * FF_SHARD, FF_SHARD, axis=2))
    names = SMALL + ("conv_w",)
    shapes = small_shapes + [w["conv_w"].shape]
    packed = lambda arrs: _pack(arrs, ROW_TILE)[None]
    d_s, m_s, v_s = _adamw(packed([w[k] for k in names]), packed(g_small), packed([m[k] for k in names]),
                           packed([v[k] for k in names]), name="adamw_small")
    out_g.update(zip(names, g_small))
    out_d.update(zip(names, _unpack(d_s, shapes)))
    out_m.update(zip(names, _unpack(m_s, shapes)))
    out_v.update(zip(names, _unpack(v_s, shapes)))

    return (loss, dx[None], *[out_g[k] for k in WEIGHTS], *[out_d[k] for k in WEIGHTS],
            *[out_m[k] for k in WEIGHTS], *[out_v[k] for k in WEIGHTS])
```

```python
import math

import numpy as np
import jax
import jax.numpy as jnp
from jax import lax
from jax.experimental import pallas as pl
from jax.experimental.pallas import tpu as pltpu

F32 = jnp.float32
BF16 = jnp.bfloat16

D_MODEL = 1024
DEPTH = 4
HEAD_DIM = 64
SWA_Q_HEADS = 8
SWA_KV_HEADS = 2
SWA_GROUP = SWA_Q_HEADS // SWA_KV_HEADS
WINDOW = 128
FOX_HEADS = 4
MLA_HEADS = 4
MLA_Q_RANK = 256
MLA_KV_RANK = 128
MLA_NOPE = 64
MLA_ROPE = 32
MLA_QK = MLA_NOPE + MLA_ROPE
ROPE_THETA = 10000.0
REL_BUCKETS = 32
REL_MAX_DIST = 128
D_FF = 2816
EPS = 1e-6
NEG_INF = -1e30
LANES = 128
N_CHIPS = 4

IN_COLS = 1956
IN_ROWS = 2048
QKV_ROWS = 1536
LAT_ROWS = IN_ROWS - QKV_ROWS
LAT_SHIFT = FOX_HEADS
FOX_ROW0 = 768
MLA_PAD = LANES
GATE_ROWS = 8

ADAM_LR = 0.001
ADAM_B1 = 0.9
ADAM_B2 = 0.999
ADAM_EPS = 1e-08
ADAM_WD = 0.01
ADAM_STEP = 10

VMEM_LIMIT_BYTES = 48 * 1024 * 1024
ATT_TILE = 512
LOG2E = math.log2(math.e)
MLA_SCALE = MLA_QK ** -0.5
ROW_TILE = 256
MESH = pl.DeviceIdType.MESH

NT = (((1,), (1,)), ((), ()))
TN = (((0,), (0,)), ((), ()))
NN = (((1,), (0,)), ((), ()))


def _params(*sem):
    return pltpu.CompilerParams(dimension_semantics=sem, vmem_limit_bytes=VMEM_LIMIT_BYTES)


def _tile(dim, cap):
    for t in (2816, 2048, 1408, 1024, 512, 256, 128, 64, 32, 16, 8):
        if t <= cap and dim % t == 0:
            return t
    return dim


def _dot(a, b, dims=NN):
    return lax.dot_general(a, b, dims, preferred_element_type=F32)


def _split3(a):
    a1 = a.astype(BF16)
    r1 = a - a1.astype(F32)
    a2 = r1.astype(BF16)
    a3 = (r1 - a2.astype(F32)).astype(BF16)
    return a1, a2, a3


FF_SHARD = 2 * D_FF // N_CHIPS
MATMUL_VMEM_BYTES = 40 * 1024 * 1024


def _matmul(a, b, *, ta=False, tb=False, out_dtype=F32, name, b_shards=False, out_shards=False, a_halves=False,
            b_halves=False):
    if a_halves:
        M, K = a.shape[1], 2 * a.shape[2]
    elif ta:
        K, M = a.shape
    else:
        M, K = a.shape
    if b_halves:
        K2, N = b.shape[1], 2 * b.shape[2]
    elif b_shards:
        K2, N = (2 * D_FF, D_MODEL) if tb else (D_MODEL, 2 * D_FF)
    elif tb:
        N, K2 = b.shape
    else:
        K2, N = b.shape
    assert K == K2, (a.shape, b.shape)
    tm, tn = (M if M <= 2048 else _tile(M, 1408)), _tile(N, 1408)
    tk = FF_SHARD if (b_shards and tb) else _tile(K, 2816)
    out_bytes = jnp.dtype(out_dtype).itemsize
    while 2 * 2 * tk * (tm + tn) + (4 + 2 * out_bytes) * tm * tn > MATMUL_VMEM_BYTES and tk % 256 == 0:
        tk //= 2
    nk = K // tk
    dims = (((0 if ta else 1,), (1 if tb else 0,)), ((), ()))

    def body(a_ref, b_ref, o_ref, acc_ref):
        k = pl.program_id(2)

        @pl.when(k == 0)
        def _():
            acc_ref[...] = jnp.zeros_like(acc_ref)

        acc_ref[...] += lax.dot_general(a_ref[...], b_ref[...], dims, preferred_element_type=F32)

        @pl.when(k == nk - 1)
        def _():
            o_ref[...] = acc_ref[...].astype(o_ref.dtype)

    if a_halves:
        nh = K // 2 // tk
        a_spec = pl.BlockSpec((None, tm, tk), lambda i, j, k: (k // nh, i, k % nh))
    else:
        a_spec = pl.BlockSpec((tk, tm), lambda i, j, k: (k, i)) if ta else pl.BlockSpec((tm, tk), lambda i, j, k: (i, k))
    if b_halves:
        nh = N // 2 // tn
        b_spec = pl.BlockSpec((None, tk, tn), lambda i, j, k: (j // nh, k, j % nh))
    elif b_shards and tb:
        assert tk == FF_SHARD
        b_spec = pl.BlockSpec((None, tn, tk), lambda i, j, k: (k, j, 0))
    elif b_shards:
        assert tn == FF_SHARD
        b_spec = pl.BlockSpec((None, tk, tn), lambda i, j, k: (j, k, 0))
    else:
        b_spec = pl.BlockSpec((tn, tk), lambda i, j, k: (j, k)) if tb else pl.BlockSpec((tk, tn), lambda i, j, k: (k, j))
    if out_shards:
        assert tn == FF_SHARD
        out_spec = pl.BlockSpec((None, tm, tn), lambda i, j, k: (j, i, 0))
        out_shape = jax.ShapeDtypeStruct((N // tn, M, tn), out_dtype)
    else:
        out_spec = pl.BlockSpec((tm, tn), lambda i, j, k: (i, j))
        out_shape = jax.ShapeDtypeStruct((M, N), out_dtype)
    return pl.pallas_call(
        body, name=name, grid=(M // tm, N // tn, nk),
        in_specs=[a_spec, b_spec], out_specs=out_spec, out_shape=out_shape,
        scratch_shapes=[pltpu.VMEM((tm, tn), F32)],
        compiler_params=_params("parallel", "parallel", "arbitrary"),
    )(a, b)


def _seg_rms(xs, g):
    r = lax.rsqrt(jnp.mean(xs * xs, axis=-1, keepdims=True) + EPS)
    return xs * r * g


def _seg_rms_bwd(xs, g, dy):
    r = lax.rsqrt(jnp.mean(xs * xs, axis=-1, keepdims=True) + EPS)
    gd = dy * g
    c = jnp.mean(gd * xs, axis=-1, keepdims=True)
    dx = r * gd - xs * (r * r * r * c)
    dg = jnp.sum(dy * (xs * r), axis=0, keepdims=True)
    return dx, dg


def _rms_fwd(x, g, *, name):
    S, W = x.shape
    tm = _tile(S, 512)

    def body(x_ref, g_ref, o_ref):
        o_ref[...] = _seg_rms(x_ref[...], g_ref[...]).astype(o_ref.dtype)

    return pl.pallas_call(
        body, name=name, grid=(S // tm,),
        in_specs=[pl.BlockSpec((tm, W), lambda i: (i, 0)), pl.BlockSpec((1, W), lambda i: (0, 0))],
        out_specs=pl.BlockSpec((tm, W), lambda i: (i, 0)),
        out_shape=jax.ShapeDtypeStruct((S, W), BF16),
        compiler_params=_params("parallel"),
    )(x, g)


def _rms_bwd(x, g, dy, *, resid=None, out_dtype, name, then=None):
    S, W = x.shape
    tm = _tile(S, 512)
    has_resid = resid is not None
    chained = then is not None

    def body(*refs):
        refs = list(refs)
        x_ref, g_ref, dy_ref = refs[:3]
        r_ref = refs[3] if has_resid else None
        n_in = 3 + has_resid + 2 * chained
        x2_ref, g2_ref = (refs[n_in - 2], refs[n_in - 1]) if chained else (None, None)
        outs = refs[n_in:]
        dx_ref, dg_ref = outs[0], outs[1]

        @pl.when(pl.program_id(0) == 0)
        def _():
            dg_ref[...] = jnp.zeros_like(dg_ref)
            if chained:
                outs[3][...] = jnp.zeros_like(outs[3])

        dx, dg = _seg_rms_bwd(x_ref[...], g_ref[...], dy_ref[...])
        if has_resid:
            dx = dx + r_ref[...]
        dx_ref[...] = dx.astype(dx_ref.dtype)
        dg_ref[...] += dg
        if chained:
            dx2, dg2 = _seg_rms_bwd(x2_ref[...], g2_ref[...], dx)
            outs[2][...] = dx2.astype(BF16)
            outs[3][...] += dg2

    row = pl.BlockSpec((tm, W), lambda i: (i, 0))
    vec = pl.BlockSpec((1, W), lambda i: (0, 0))
    ins = [x, g, dy] + ([resid] if has_resid else []) + (list(then) if chained else [])
    return pl.pallas_call(
        body, name=name, grid=(S // tm,),
        in_specs=[row, vec, row] + ([row] if has_resid else []) + ([row, vec] if chained else []),
        out_specs=[row, vec] + ([row, vec] if chained else []),
        out_shape=[jax.ShapeDtypeStruct((S, W), out_dtype), jax.ShapeDtypeStruct((1, W), F32)]
        + ([jax.ShapeDtypeStruct((S, W), BF16), jax.ShapeDtypeStruct((1, W), F32)] if chained else []),
        compiler_params=_params("arbitrary"),
    )(*ins)


def _resid_rms(x, y, g_post, g_next, *, name):
    S, W = x.shape
    tm = _tile(S, 512)
    with_next = g_next is not None

    def body(*refs):
        if with_next:
            x_ref, y_ref, gp_ref, gn_ref, xo_ref, h_ref = refs
        else:
            x_ref, y_ref, gp_ref, xo_ref = refs
        xn = x_ref[...] + _seg_rms(y_ref[...], gp_ref[...])
        xo_ref[...] = xn
        if with_next:
            h_ref[...] = _seg_rms(xn, gn_ref[...]).astype(BF16)

    row = pl.BlockSpec((tm, W), lambda i: (i, 0))
    vec = pl.BlockSpec((1, W), lambda i: (0, 0))
    outs = [jax.ShapeDtypeStruct((S, W), F32)] + ([jax.ShapeDtypeStruct((S, W), BF16)] if with_next else [])
    res = pl.pallas_call(
        body, name=name, grid=(S // tm,),
        in_specs=[row, row, vec] + ([vec] if with_next else []),
        out_specs=[row] + ([row] if with_next else []),
        out_shape=outs,
        compiler_params=_params("parallel"),
    )(*([x, y, g_post] + ([g_next] if with_next else [])))
    return (res[0], res[1]) if with_next else (res[0], None)


def _col_rms(xs, g):
    r = lax.rsqrt(jnp.mean(xs * xs, axis=0, keepdims=True) + EPS)
    return xs * r * g


def _col_rms_bwd(xs, g, dy):
    r = lax.rsqrt(jnp.mean(xs * xs, axis=0, keepdims=True) + EPS)
    gd = dy * g
    c = jnp.mean(gd * xs, axis=0, keepdims=True)
    dx = r * gd - xs * (r * r * r * c)
    dg = jnp.sum(dy * (xs * r), axis=1, keepdims=True)
    return dx, dg


GROUP_ROWS = (SWA_Q_HEADS * HEAD_DIM, FOX_HEADS * HEAD_DIM, MLA_HEADS * HEAD_DIM)


def _group_specs(S, tn):
    outs = [pl.BlockSpec((n, tn), lambda i: (0, i)) for n in GROUP_ROWS]
    g = pl.BlockSpec((D_MODEL, 1), lambda i: (0, 0))
    mixed = pl.BlockSpec((D_MODEL, tn), lambda i: (0, i))
    return outs, g, mixed


def _group_norm_fwd(oa, of, oc, g, *, name):
    S = oa.shape[1]
    tn = _tile(S, 512)
    outs, gs, mixed = _group_specs(S, tn)

    def body(a_ref, f_ref, c_ref, g_ref, o_ref):
        r0 = 0
        for ref, n in zip((a_ref, f_ref, c_ref), GROUP_ROWS):
            o_ref[r0:r0 + n, :] = _col_rms(ref[...], g_ref[r0:r0 + n, :]).astype(BF16)
            r0 += n

    return pl.pallas_call(
        body, name=name, grid=(S // tn,),
        in_specs=outs + [gs], out_specs=mixed,
        out_shape=jax.ShapeDtypeStruct((D_MODEL, S), BF16),
        compiler_params=_params("parallel"),
    )(oa, of, oc, g)


def _group_norm_bwd(oa, of, oc, g, dmixed, *, name):
    S = oa.shape[1]
    tn = _tile(S, 512)
    outs, gs, mixed = _group_specs(S, tn)
    n_heads = D_MODEL // HEAD_DIM

    def body(a_ref, f_ref, c_ref, g_ref, dm_ref, da_ref, df_ref, dc_ref, dg_ref, dl_ref):
        @pl.when(pl.program_id(0) == 0)
        def _():
            dg_ref[...] = jnp.zeros_like(dg_ref)

        r0 = 0
        for ref, dref, n in zip((a_ref, f_ref, c_ref), (da_ref, df_ref, dc_ref), GROUP_ROWS):
            o = ref[...]
            dx, dg = _col_rms_bwd(o, g_ref[r0:r0 + n, :], dm_ref[r0:r0 + n, :])
            dxb = dx.astype(BF16)
            dref[...] = dxb
            dg_ref[r0:r0 + n, :] += dg
            od = o * dxb.astype(F32)
            for h in range(n // HEAD_DIM):
                dl_ref[r0 // HEAD_DIM + h] = jnp.sum(od[h * HEAD_DIM:(h + 1) * HEAD_DIM, :], axis=0, keepdims=True)
            r0 += n

    return pl.pallas_call(
        body, name=name, grid=(S // tn,),
        in_specs=outs + [gs, mixed], out_specs=outs + [gs, pl.BlockSpec((n_heads, 1, tn), lambda i: (0, 0, i))],
        out_shape=[jax.ShapeDtypeStruct((n, S), BF16) for n in GROUP_ROWS] + [jax.ShapeDtypeStruct((D_MODEL, 1), F32),
                                                                              jax.ShapeDtypeStruct((n_heads, 1, S), F32)],
        compiler_params=_params("arbitrary"),
    )(oa, of, oc, g, dmixed)


def _loss_head(y, target):
    S, W = y.shape
    tm = _tile(S, 512)

    def body(y_ref, t_ref, d_ref, l_ref):
        @pl.when(pl.program_id(0) == 0)
        def _():
            l_ref[...] = jnp.zeros_like(l_ref)

        err = y_ref[...] - t_ref[...]
        d_ref[...] = err * (1.0 / W)
        l_ref[...] += 0.5 * jnp.sum(jnp.mean(err * err, axis=-1, keepdims=True), axis=0, keepdims=True)

    row = pl.BlockSpec((tm, W), lambda i: (i, 0))
    d, l = pl.pallas_call(
        body, name="loss_head", grid=(S // tm,),
        in_specs=[row, row],
        out_specs=[row, pl.BlockSpec((1, 1), lambda i: (0, 0))],
        out_shape=[jax.ShapeDtypeStruct((S, W), F32), jax.ShapeDtypeStruct((1, 1), F32)],
        compiler_params=_params("arbitrary"),
    )(y, target)
    return l[0, 0], d


def _attn_fwd(q_src, k_src, v_src, rows, H, Dk, Dv, scale, f_row=None, f_col=None, *, name, q_scaled=False):
    S = q_src.shape[1]
    T = _tile(S, ATT_TILE)
    nq = S // T
    forget = f_row is not None
    qb, kb, vb = rows[0] // (H * Dk), rows[1] // (H * Dk), rows[2] // (H * Dv)
    hs = range(H)

    def body(*refs):
        if forget:
            q_ref, k_ref, v_ref, fq_ref, fk_ref, o_ref, lse_ref = refs
        else:
            q_ref, k_ref, v_ref, o_ref, lse_ref = refs
        i = pl.program_id(0)

        def tile(j, masked, state):
            off = pl.multiple_of(j * T, T)
            ss = [_dot(k_ref[h * Dk:(h + 1) * Dk, pl.ds(off, T)], q_ref[h * Dk:(h + 1) * Dk, :], TN) for h in hs]
            if not q_scaled:
                ss = [s * (scale * LOG2E) for s in ss]
            if forget:
                ss = [ss[h] + (fq_ref[h] - fk_ref[pl.ds(off, T), h:h + 1]) for h in hs]
            if masked:
                r = lax.broadcasted_iota(jnp.int32, (T, T), 0)
                c = lax.broadcasted_iota(jnp.int32, (T, T), 1)
                ss = [jnp.where(r <= c, s, NEG_INF) for s in ss]
            m_new = [jnp.maximum(state[h][0], jnp.max(ss[h], axis=0, keepdims=True)) for h in hs]
            alpha = [jnp.exp2(state[h][0] - m_new[h]) for h in hs]
            ps = [jnp.exp2(ss[h] - m_new[h]) for h in hs]
            l_new = [alpha[h] * state[h][1] + jnp.sum(ps[h], axis=0, keepdims=True) for h in hs]
            p_hi = [p.astype(BF16) for p in ps]
            vs = [v_ref[h * Dv:(h + 1) * Dv, pl.ds(off, T)] for h in hs]
            pv = [_dot(vs[h], p_hi[h]) for h in hs]
            if forget:
                pv = [pv[h] + _dot(vs[h], (ps[h] - p_hi[h].astype(F32)).astype(BF16)) for h in hs]
            return tuple((m_new[h], l_new[h], alpha[h] * state[h][2] + pv[h]) for h in hs)

        init = tuple((jnp.full((1, T), NEG_INF, F32), jnp.zeros((1, T), F32), jnp.zeros((Dv, T), F32)) for _ in hs)
        state = lax.fori_loop(0, i, lambda j, st: tile(j, False, st), init)
        state = tile(i, True, state)
        for h in hs:
            m, l, acc = state[h]
            o_ref[h * Dv:(h + 1) * Dv, :] = acc / l
            lse_ref[h] = m + jnp.log2(l)

    in_specs = [pl.BlockSpec((H * Dk, T), lambda i: (qb, i)),
                pl.BlockSpec((H * Dk, S), lambda i: (kb, 0)),
                pl.BlockSpec((H * Dv, S), lambda i: (vb, 0))]
    ins = [q_src, k_src, v_src]
    if forget:
        in_specs += [pl.BlockSpec((H, 1, T), lambda i: (0, 0, i)), pl.BlockSpec((S, H), lambda i: (0, 0))]
        ins += [f_row, f_col]
    return pl.pallas_call(
        body, name=name, grid=(nq,),
        in_specs=in_specs,
        out_specs=[pl.BlockSpec((H * Dv, T), lambda i: (0, i)), pl.BlockSpec((H, 1, T), lambda i: (0, 0, i))],
        out_shape=[jax.ShapeDtypeStruct((H * Dv, S), F32), jax.ShapeDtypeStruct((H, 1, S), F32)],
        compiler_params=_params("parallel"),
    )(*ins)


def _attn_bwd(q_src, k_src, v_src, rows, H, Dk, Dv, scale, do, lse, delta, f_row=None, f_col=None, *, name, q_scaled=False):
    S = q_src.shape[1]
    T = _tile(S, ATT_TILE)
    nq = S // T
    forget = f_row is not None
    qb, kb, vb, db = rows[0] // (H * Dk), rows[1] // (H * Dk), rows[2] // (H * Dv), rows[3] // H
    hs = range(H)

    def body(*refs):
        if forget:
            (q_ref, k_ref, v_ref, do_ref, lse_ref, dl_ref, fq_ref, fk_ref,
             dq_ref, dk_ref, dv_ref, df_ref, dk_s, dv_s, df_s) = refs
        else:
            q_ref, k_ref, v_ref, do_ref, lse_ref, dl_ref, dq_ref, dk_ref, dv_ref, dk_s, dv_s = refs
        j = pl.program_id(0)

        @pl.when(j == 0)
        def _():
            dq_ref[...] = jnp.zeros_like(dq_ref)

        dk_s[...] = jnp.zeros_like(dk_s)
        dv_s[...] = jnp.zeros_like(dv_s)
        if forget:
            df_s[...] = jnp.zeros_like(df_s)
        kt = [k_ref[h * Dk:(h + 1) * Dk, :] for h in hs]
        kj = [k.T for k in kt]
        vj = [v_ref[h * Dv:(h + 1) * Dv, :].T for h in hs]
        koff = pl.multiple_of(j * T, T)

        def tile(i, masked):
            cols = pl.ds(pl.multiple_of(i * T, T), T)
            qi = [q_ref[h * Dk:(h + 1) * Dk, cols] for h in hs]
            doi = [do_ref[h * Dv:(h + 1) * Dv, cols] for h in hs]
            st = [_dot(kj[h], qi[h]) for h in hs]
            if not q_scaled:
                st = [x * (scale * LOG2E) for x in st]
            if forget:
                st = [st[h] + (fq_ref[h, :, cols] - fk_ref[pl.ds(koff, T), h:h + 1]) for h in hs]
            if masked:
                r = lax.broadcasted_iota(jnp.int32, (T, T), 0)
                c = lax.broadcasted_iota(jnp.int32, (T, T), 1)
                st = [jnp.where(r <= c, x, NEG_INF) for x in st]
            pt = [jnp.exp2(st[h] - lse_ref[h, :, cols]) for h in hs]
            dpt = [_dot(vj[h], doi[h]) for h in hs]
            dst = [pt[h] * (dpt[h] - dl_ref[h, :, cols]) for h in hs]
            ptb = [p.astype(BF16) for p in pt]
            dsb = [d.astype(BF16) for d in dst]
            for h in hs:
                dv_s[h * Dv:(h + 1) * Dv, :] += _dot(doi[h], ptb[h], NT)
            for h in hs:
                dk_s[h * Dk:(h + 1) * Dk, :] += _dot(qi[h], dsb[h], NT)
            for h in hs:
                dq_ref[h * Dk:(h + 1) * Dk, cols] += _dot(kt[h], dsb[h]) * scale
            if forget:
                for h in hs:
                    part = dst[h][:, 0:LANES]
                    for c0 in range(LANES, T, LANES):
                        part = part + dst[h][:, c0:c0 + LANES]
                    df_s[h] += part

        tile(j, True)

        def loop_body(i, carry):
            tile(i, False)
            return carry

        lax.fori_loop(j + 1, nq, loop_body, 0)
        dk_ref[...] = dk_s[...] * ((1.0 / LOG2E) if q_scaled else scale)
        dv_ref[...] = dv_s[...]
        if forget:
            df_ref[...] = jnp.concatenate([-jnp.sum(df_s[h], axis=-1, keepdims=True) for h in hs], axis=1)

    res = lambda D, b0: pl.BlockSpec((H * D, S), lambda j: (b0, 0))
    blk = lambda D, b0: pl.BlockSpec((H * D, T), lambda j: (b0, j))
    row3 = lambda b0: pl.BlockSpec((H, 1, S), lambda j: (b0, 0, 0))
    in_specs = [res(Dk, qb), blk(Dk, kb), blk(Dv, vb), res(Dv, 0), row3(0), row3(db)]
    ins = [q_src, k_src, v_src, do, lse, delta]
    out_specs = [res(Dk, 0), blk(Dk, 0), blk(Dv, 0)]
    out_shape = [jax.ShapeDtypeStruct((H * Dk, S), F32), jax.ShapeDtypeStruct((H * Dk, S), F32),
                 jax.ShapeDtypeStruct((H * Dv, S), F32)]
    scratch = [pltpu.VMEM((H * Dk, T), F32), pltpu.VMEM((H * Dv, T), F32)]
    if forget:
        in_specs += [row3(0), pl.BlockSpec((S, H), lambda j: (0, 0))]
        ins += [f_row, f_col]
        out_specs.append(pl.BlockSpec((T, H), lambda j: (j, 0)))
        out_shape.append(jax.ShapeDtypeStruct((S, H), F32))
        scratch.append(pltpu.VMEM((H, T, min(T, LANES)), F32))
    return pl.pallas_call(
        body, name=name, grid=(nq,),
        in_specs=in_specs, out_specs=out_specs, out_shape=out_shape, scratch_shapes=scratch,
        compiler_params=_params("arbitrary"),
    )(*ins)


GW = SWA_GROUP * WINDOW


def _swa_masks(i):
    r = lax.broadcasted_iota(jnp.int32, (WINDOW, GW), 0)
    c = lax.broadcasted_iota(jnp.int32, (WINDOW, GW), 1) % WINDOW
    return (r > c) & (i > 0), r <= c


def _swa_specs():
    W = WINDOW
    kv_rows = SWA_KV_HEADS * HEAD_DIM
    q = pl.BlockSpec((SWA_Q_HEADS * HEAD_DIM, W), lambda i: (0, i))
    prev = lambda b: pl.BlockSpec((kv_rows, W), lambda i: (b, jnp.maximum(i - 1, 0)))
    cur = lambda b: pl.BlockSpec((kv_rows, W), lambda i: (b, i))
    bias = pl.BlockSpec((SWA_KV_HEADS, 2 * W, GW), lambda i: (0, 0, 0))
    stat = pl.BlockSpec((SWA_Q_HEADS, W), lambda i: (0, i))
    sink = pl.BlockSpec(memory_space=pltpu.SMEM)
    return q, prev(4), cur(4), prev(5), cur(5), bias, stat, sink


def _group_lanes(ref, g, rows_per_head):
    h0 = g * SWA_GROUP
    return jnp.concatenate([ref[(h0 + j) * rows_per_head:(h0 + j + 1) * rows_per_head, :] for j in range(SWA_GROUP)], axis=1)


def _swa_scores(g, q_ref, kp_ref, kc_ref, b_ref, masks):
    rows = slice(g * HEAD_DIM, (g + 1) * HEAD_DIM)
    qg = _group_lanes(q_ref, g, HEAD_DIM)
    scale = HEAD_DIM ** -0.5
    s_p = jnp.where(masks[0], _dot(kp_ref[rows, :], qg, TN) * scale + b_ref[g, 0:WINDOW, :], NEG_INF)
    s_c = jnp.where(masks[1], _dot(kc_ref[rows, :], qg, TN) * scale + b_ref[g, WINDOW:2 * WINDOW, :], NEG_INF)
    return qg, rows, s_p, s_c


def _sink_row(sink_ref, g):
    return jnp.concatenate([jnp.full((1, WINDOW), sink_ref[g * SWA_GROUP + j], F32) for j in range(SWA_GROUP)], axis=1)


def _swa_fwd(qkv, bias_g, sinks, *, name):
    S = qkv.shape[1]
    qs, kp, kc, vp, vc, bs, stat, sk = _swa_specs()
    gs = range(SWA_KV_HEADS)

    def body(sink_ref, q_ref, kp_ref, kc_ref, vp_ref, vc_ref, b_ref, o_ref, lse_ref):
        masks = _swa_masks(pl.program_id(0))
        sc = [_swa_scores(g, q_ref, kp_ref, kc_ref, b_ref, masks) for g in gs]
        sinks_g = [_sink_row(sink_ref, g) for g in gs]
        m = [jnp.maximum(jnp.maximum(jnp.max(sc[g][2], axis=0, keepdims=True), jnp.max(sc[g][3], axis=0, keepdims=True)),
                         sinks_g[g]) for g in gs]
        p_p = [jnp.exp(sc[g][2] - m[g]) for g in gs]
        p_c = [jnp.exp(sc[g][3] - m[g]) for g in gs]
        l = [jnp.sum(p_p[g], axis=0, keepdims=True) + jnp.sum(p_c[g], axis=0, keepdims=True) + jnp.exp(sinks_g[g] - m[g])
             for g in gs]
        o = [_dot(vp_ref[sc[g][1], :], p_p[g].astype(BF16)) + _dot(vc_ref[sc[g][1], :], p_c[g].astype(BF16)) for g in gs]
        for g in gs:
            og = o[g] / l[g]
            lse = m[g] + jnp.log(l[g])
            for j in range(SWA_GROUP):
                h = g * SWA_GROUP + j
                o_ref[h * HEAD_DIM:(h + 1) * HEAD_DIM, :] = og[:, j * WINDOW:(j + 1) * WINDOW]
                lse_ref[h:h + 1, :] = lse[:, j * WINDOW:(j + 1) * WINDOW]

    return pl.pallas_call(
        body, name=name, grid=(S // WINDOW,),
        in_specs=[sk, qs, kp, kc, vp, vc, bs],
        out_specs=[qs, stat],
        out_shape=[jax.ShapeDtypeStruct((SWA_Q_HEADS * HEAD_DIM, S), F32), jax.ShapeDtypeStruct((SWA_Q_HEADS, S), F32)],
        compiler_params=_params("parallel"),
    )(sinks, qkv, qkv, qkv, qkv, qkv, bias_g)


def _swa_bwd(qkv, bias_g, sinks, do, lse, delta, *, name):
    S = qkv.shape[1]
    W = WINDOW
    qs, kp, kc, vp, vc, bs, stat, sk = _swa_specs()
    scale = HEAD_DIM ** -0.5
    kv_rows = SWA_KV_HEADS * HEAD_DIM
    gs = range(SWA_KV_HEADS)

    def body(sink_ref, q_ref, kp_ref, kc_ref, vp_ref, vc_ref, b_ref, do_ref, lse_ref, dl_ref,
             dq_ref, dkv_ref, db_ref, dsk_ref):
        i = pl.program_id(0)

        @pl.when(i == 0)
        def _():
            dkv_ref[...] = jnp.zeros_like(dkv_ref)
            db_ref[...] = jnp.zeros_like(db_ref)
            dsk_ref[...] = jnp.zeros_like(dsk_ref)

        masks = _swa_masks(i)
        prev = pl.ds(pl.multiple_of(jnp.maximum(i - 1, 0) * W, W), W)
        cur = pl.ds(pl.multiple_of(i * W, W), W)
        sc = [_swa_scores(g, q_ref, kp_ref, kc_ref, b_ref, masks) for g in gs]
        dog = [_group_lanes(do_ref, g, HEAD_DIM) for g in gs]
        lse = [_group_lanes(lse_ref, g, 1) for g in gs]
        dl = [_group_lanes(dl_ref, g, 1) for g in gs]
        p_p = [jnp.exp(sc[g][2] - lse[g]) for g in gs]
        p_c = [jnp.exp(sc[g][3] - lse[g]) for g in gs]
        ds_p = [p_p[g] * (_dot(vp_ref[sc[g][1], :], dog[g], TN) - dl[g]) for g in gs]
        ds_c = [p_c[g] * (_dot(vc_ref[sc[g][1], :], dog[g], TN) - dl[g]) for g in gs]
        for g in gs:
            db_ref[g, 0:W, :] += ds_p[g]
            db_ref[g, W:2 * W, :] += ds_c[g]
            dsk = jnp.exp(_sink_row(sink_ref, g) - lse[g]) * dl[g]
            for j in range(SWA_GROUP):
                h = g * SWA_GROUP + j
                dsk_ref[h:h + 1, :] -= jnp.broadcast_to(jnp.sum(dsk[:, j * W:(j + 1) * W], axis=1, keepdims=True), (1, LANES))
        dsb_p = [d.astype(BF16) for d in ds_p]
        dsb_c = [d.astype(BF16) for d in ds_c]
        for g in gs:
            rows = sc[g][1]
            dq = (_dot(kp_ref[rows, :], dsb_p[g]) + _dot(kc_ref[rows, :], dsb_c[g])) * scale
            for j in range(SWA_GROUP):
                h = g * SWA_GROUP + j
                dq_ref[h * HEAD_DIM:(h + 1) * HEAD_DIM, :] = dq[:, j * W:(j + 1) * W]
        for g in gs:
            rows = sc[g][1]
            vrows = slice(kv_rows + rows.start, kv_rows + rows.stop)
            dkv_ref[rows, prev] += _dot(sc[g][0], dsb_p[g], NT) * scale
            dkv_ref[rows, cur] += _dot(sc[g][0], dsb_c[g], NT) * scale
            dkv_ref[vrows, prev] += _dot(dog[g], p_p[g].astype(BF16), NT)
            dkv_ref[vrows, cur] += _dot(dog[g], p_c[g].astype(BF16), NT)

    return pl.pallas_call(
        body, name=name, grid=(S // W,),
        in_specs=[sk, qs, kp, kc, vp, vc, bs, qs, stat, stat],
        out_specs=[qs, pl.BlockSpec((2 * kv_rows, S), lambda i: (0, 0)), bs, pl.BlockSpec((SWA_Q_HEADS, LANES), lambda i: (0, 0))],
        out_shape=[jax.ShapeDtypeStruct((SWA_Q_HEADS * HEAD_DIM, S), F32), jax.ShapeDtypeStruct((2 * kv_rows, S), F32),
                   jax.ShapeDtypeStruct((SWA_KV_HEADS, 2 * W, GW), F32), jax.ShapeDtypeStruct((SWA_Q_HEADS, LANES), F32)],
        compiler_params=_params("arbitrary"),
    )(sinks, qkv, qkv, qkv, qkv, qkv, bias_g, do, lse, delta)


def _rel_onehot_t():
    qi = jnp.arange(WINDOW, dtype=jnp.int32)[None, :] + WINDOW
    kj = jnp.arange(2 * WINDOW, dtype=jnp.int32)[:, None]
    dist = qi - kj
    max_exact = REL_BUCKETS // 2
    d = jnp.maximum(dist, 0)
    log_ratio = jnp.log(jnp.maximum(d, 1).astype(F32) / max_exact) / math.log(REL_MAX_DIST / max_exact)
    large = jnp.minimum(max_exact + (log_ratio * (REL_BUCKETS - max_exact)).astype(jnp.int32), REL_BUCKETS - 1)
    bucket = jnp.where(d < max_exact, d, large).reshape(-1)
    return (bucket[None, :] == jnp.arange(REL_BUCKETS, dtype=jnp.int32)[:, None]).astype(BF16)


def _bias_table(rel_bias_t, onehot_t):
    Hq, NB = rel_bias_t.shape
    N = onehot_t.shape[1]
    tn = _tile(N, 4096)

    def body(r_ref, oh_ref, o_ref):
        oh = oh_ref[...]
        a1, a2, a3 = _split3(r_ref[...])
        o_ref[...] = _dot(a1, oh) + _dot(a2, oh) + _dot(a3, oh)

    return pl.pallas_call(
        body, name="rel_bias_table", grid=(N // tn,),
        in_specs=[pl.BlockSpec((Hq, NB), lambda j: (0, 0)), pl.BlockSpec((NB, tn), lambda j: (0, j))],
        out_specs=pl.BlockSpec((Hq, tn), lambda j: (0, j)),
        out_shape=jax.ShapeDtypeStruct((Hq, N), F32),
        compiler_params=_params("parallel"),
    )(rel_bias_t, onehot_t)


def _bias_table_bwd(dbias, onehot_t):
    L, Hq, N = dbias.shape
    NB = onehot_t.shape[0]
    tn = _tile(N, 4096)

    def body(d_ref, oh_ref, o_ref):
        @pl.when(pl.program_id(0) == 0)
        def _():
            o_ref[...] = jnp.zeros_like(o_ref)

        d = d_ref[0]
        for l in range(1, L):
            d = d + d_ref[l]
        oh = oh_ref[...]
        a1, a2, a3 = _split3(d)
        o_ref[...] += _dot(a1, oh, NT) + _dot(a2, oh, NT) + _dot(a3, oh, NT)

    return pl.pallas_call(
        body, name="rel_bias_bwd", grid=(N // tn,),
        in_specs=[pl.BlockSpec((L, Hq, tn), lambda j: (0, 0, j)), pl.BlockSpec((NB, tn), lambda j: (0, j))],
        out_specs=pl.BlockSpec((Hq, NB), lambda j: (0, 0)),
        out_shape=jax.ShapeDtypeStruct((Hq, NB), F32),
        compiler_params=_params("arbitrary"),
    )(dbias, onehot_t)


def _gate_fwd(lat, fb_col, *, name):
    S = lat.shape[1]
    tn = _tile(S, 256)

    def body(z_ref, fb_ref, o_ref, carry):
        @pl.when(pl.program_id(0) == 0)
        def _():
            carry[...] = jnp.zeros_like(carry)

        z = z_ref[...] + fb_ref[...]
        lf = jnp.minimum(z, 0.0) - jnp.log1p(jnp.exp(-jnp.abs(z)))
        r = lax.broadcasted_iota(jnp.int32, (tn, tn), 0)
        c = lax.broadcasted_iota(jnp.int32, (tn, tn), 1)
        tri = (r <= c).astype(BF16)
        a1, a2, a3 = _split3(lf)
        cum = _dot(a1, tri) + _dot(a2, tri) + _dot(a3, tri) + carry[:, 0:1]
        o_ref[...] = cum
        carry[...] = jnp.broadcast_to(cum[:, tn - 1:tn], carry.shape)

    return pl.pallas_call(
        body, name=name, grid=(S // tn,),
        in_specs=[pl.BlockSpec((GATE_ROWS, tn), lambda i: (0, i)), pl.BlockSpec((GATE_ROWS, 1), lambda i: (0, 0))],
        out_specs=pl.BlockSpec((GATE_ROWS, tn), lambda i: (0, i)),
        out_shape=jax.ShapeDtypeStruct((GATE_ROWS, S), F32),
        scratch_shapes=[pltpu.VMEM((GATE_ROWS, LANES), F32)],
        compiler_params=_params("arbitrary"),
    )(lat, fb_col)


def _gate_bwd(lat, fb_col, dF, *, name):
    S = lat.shape[1]
    tn = _tile(S, 256)
    nt = S // tn

    def body(z_ref, fb_ref, df_ref, dz_ref, dfb_ref, carry):
        @pl.when(pl.program_id(0) == 0)
        def _():
            carry[...] = jnp.zeros_like(carry)
            dfb_ref[...] = jnp.zeros_like(dfb_ref)

        r = lax.broadcasted_iota(jnp.int32, (tn, tn), 0)
        c = lax.broadcasted_iota(jnp.int32, (tn, tn), 1)
        tri = (r >= c).astype(BF16)
        a1, a2, a3 = _split3(df_ref[...])
        dlf = _dot(a1, tri) + _dot(a2, tri) + _dot(a3, tri) + carry[:, 0:1]
        carry[...] = jnp.broadcast_to(dlf[:, 0:1], carry.shape)
        z = z_ref[...] + fb_ref[...]
        row = lax.broadcasted_iota(jnp.int32, (GATE_ROWS, tn), 0)
        dz = jnp.where(row < FOX_HEADS, dlf / (1.0 + jnp.exp(z)), 0.0)
        dz_ref[...] = dz
        dfb_ref[...] += jnp.sum(dz, axis=1, keepdims=True)

    blk = pl.BlockSpec((GATE_ROWS, tn), lambda i: (0, nt - 1 - i))
    vec = pl.BlockSpec((GATE_ROWS, 1), lambda i: (0, 0))
    return pl.pallas_call(
        body, name=name, grid=(nt,),
        in_specs=[blk, vec, blk], out_specs=[blk, vec],
        out_shape=[jax.ShapeDtypeStruct((GATE_ROWS, S), F32), jax.ShapeDtypeStruct((GATE_ROWS, 1), F32)],
        scratch_shapes=[pltpu.VMEM((GATE_ROWS, LANES), F32)],
        compiler_params=_params("arbitrary"),
    )(lat, fb_col, dF)


def _rope_tables(S):
    pos = jnp.arange(S, dtype=F32)
    inv_freq = ROPE_THETA ** (-(jnp.arange(MLA_ROPE // 2, dtype=F32) * 2.0 / MLA_ROPE))
    ang = pos[:, None] * inv_freq[None, :]
    cos, sin = jnp.cos(ang).T, jnp.sin(ang).T
    z16 = jnp.zeros_like(cos)

    def slab(lo, fill):
        def put(first, second, f):
            return jnp.concatenate([jnp.full((lo, S), f, F32), first, second, jnp.full((LANES - lo - MLA_ROPE, S), f, F32)], axis=0)
        return put(cos, cos, fill), put(-sin, z16, 0.0), put(z16, sin, 0.0)

    tq = tuple(jnp.tile(t, (MLA_HEADS, 1)) for t in slab(MLA_NOPE, 1.0))
    return tq, slab(0, 0.0)


def _rope(x, c, s1, s2):
    n = x.shape[0]
    half = MLA_ROPE // 2
    return x * c + pltpu.roll(x, n - half, 0) * s1 + pltpu.roll(x, half, 0) * s2


def _rope_t(dy, c, s1, s2):
    n = dy.shape[0]
    half = MLA_ROPE // 2
    return dy * c + pltpu.roll(dy * s1, half, 0) + pltpu.roll(dy * s2, n - half, 0)


KR_SLAB0 = MLA_Q_RANK + MLA_KV_RANK


def _mla_prep_fwd(lat, g_q, g_kv, w_uq_t, w_ukv_t, tq, tmisc, *, name):
    S = lat.shape[1]
    tn = _tile(S, 512)
    QW = MLA_HEADS * MLA_PAD

    def body(lat_ref, gq_ref, gkv_ref, wq_ref, wkv_ref, c_ref, s1_ref, s2_ref, cm_ref, s1m_ref, s2m_ref,
             nq_ref, nkv_ref, q_ref, k_ref, v_ref):
        x = pltpu.roll(lat_ref[...], LAT_ROWS - LAT_SHIFT, 0)
        nq = _col_rms(x[0:MLA_Q_RANK, :], gq_ref[...]).astype(BF16)
        nkv = _col_rms(x[MLA_Q_RANK:KR_SLAB0, :], gkv_ref[...]).astype(BF16)
        nq_ref[...] = nq
        nkv_ref[...] = nkv
        q = _rope(_dot(wq_ref[...], nq), c_ref[...], s1_ref[...], s2_ref[...])
        q_ref[...] = (q * (MLA_SCALE * LOG2E)).astype(BF16)
        kv = _dot(wkv_ref[...], nkv).astype(BF16)
        kr = _rope(x[KR_SLAB0:LAT_ROWS, :], cm_ref[...], s1m_ref[...], s2m_ref[...]).astype(BF16)
        for h in range(MLA_HEADS):
            k_ref[h * MLA_PAD:h * MLA_PAD + MLA_NOPE, :] = kv[h * LANES:h * LANES + MLA_NOPE, :]
            k_ref[h * MLA_PAD + MLA_NOPE:(h + 1) * MLA_PAD, :] = kr[0:MLA_PAD - MLA_NOPE, :]
            v_ref[h * HEAD_DIM:(h + 1) * HEAD_DIM, :] = kv[h * LANES + MLA_NOPE:(h + 1) * LANES, :]

    def col(rows):
        return pl.BlockSpec((rows, tn), lambda i: (0, i))

    def full(a):
        return pl.BlockSpec(a.shape, lambda i: (0, 0))

    return pl.pallas_call(
        body, name=name, grid=(S // tn,),
        in_specs=[col(LAT_ROWS), full(g_q), full(g_kv), full(w_uq_t), full(w_ukv_t),
                  col(QW), col(QW), col(QW), col(LANES), col(LANES), col(LANES)],
        out_specs=[col(MLA_Q_RANK), col(MLA_KV_RANK), col(QW), col(QW), col(MLA_HEADS * HEAD_DIM)],
        out_shape=[jax.ShapeDtypeStruct((MLA_Q_RANK, S), BF16), jax.ShapeDtypeStruct((MLA_KV_RANK, S), BF16),
                   jax.ShapeDtypeStruct((QW, S), BF16), jax.ShapeDtypeStruct((QW, S), BF16),
                   jax.ShapeDtypeStruct((MLA_HEADS * HEAD_DIM, S), BF16)],
        compiler_params=_params("parallel"),
    )(lat, g_q, g_kv, w_uq_t, w_ukv_t, *tq, *tmisc)


def _mla_prep_bwd(lat, nq, nkv, g_q, g_kv, w_uq_p, w_ukv, tq, tmisc, dq, dk, dv, dflog, *, name):
    S = lat.shape[1]
    tn = _tile(S, 512)
    QW = MLA_HEADS * MLA_PAD

    def body(lat_ref, nq_ref, nkv_ref, gq_ref, gkv_ref, wq_ref, wkv_ref, c_ref, s1_ref, s2_ref,
             cm_ref, s1m_ref, s2m_ref, dq_ref, dk_ref, dv_ref, dfl_ref,
             dlat_ref, dwq_ref, dwkv_ref, dgq_ref, dgkv_ref, y_s):
        @pl.when(pl.program_id(0) == 0)
        def _():
            dwq_ref[...] = jnp.zeros_like(dwq_ref)
            dwkv_ref[...] = jnp.zeros_like(dwkv_ref)
            dgq_ref[...] = jnp.zeros_like(dgq_ref)
            dgkv_ref[...] = jnp.zeros_like(dgkv_ref)

        x = pltpu.roll(lat_ref[...], LAT_ROWS - LAT_SHIFT, 0)
        dqm = _rope_t(dq_ref[...], c_ref[...], s1_ref[...], s2_ref[...]).astype(BF16)
        dwq_ref[...] += _dot(dqm, nq_ref[...], NT)
        dx, dg = _col_rms_bwd(x[0:MLA_Q_RANK, :], gq_ref[...], _dot(wq_ref[...], dqm))
        y_s[0:MLA_Q_RANK, :] = dx
        dgq_ref[...] += dg
        dkv = jnp.concatenate(
            [part for h in range(MLA_HEADS)
             for part in (dk_ref[h * MLA_PAD:h * MLA_PAD + MLA_NOPE, :], dv_ref[h * HEAD_DIM:(h + 1) * HEAD_DIM, :])],
            axis=0).astype(BF16)
        dwkv_ref[...] += _dot(dkv, nkv_ref[...], NT)
        dx, dg = _col_rms_bwd(x[MLA_Q_RANK:KR_SLAB0, :], gkv_ref[...], _dot(wkv_ref[...], dkv))
        y_s[MLA_Q_RANK:KR_SLAB0, :] = dx
        dgkv_ref[...] += dg
        dkr = dk_ref[MLA_NOPE:MLA_PAD, :]
        for h in range(1, MLA_HEADS):
            dkr = dkr + dk_ref[h * MLA_PAD + MLA_NOPE:(h + 1) * MLA_PAD, :]
        dkr = jnp.concatenate([dkr, jnp.zeros((MLA_NOPE, tn), F32)], axis=0)
        y_s[KR_SLAB0:LAT_ROWS, :] = _rope_t(dkr, cm_ref[...], s1m_ref[...], s2m_ref[...])
        y = pltpu.roll(y_s[...], LAT_SHIFT, 0)
        row = lax.broadcasted_iota(jnp.int32, (LAT_ROWS, tn), 0)
        dfl = jnp.concatenate([dfl_ref[...], jnp.zeros((LAT_ROWS - GATE_ROWS, tn), F32)], axis=0)
        dlat_ref[...] = jnp.where(row < LAT_SHIFT, dfl, y).astype(BF16)

    def col(rows):
        return pl.BlockSpec((rows, tn), lambda i: (0, i))

    def full(a):
        return pl.BlockSpec(a.shape, lambda i: (0, 0))

    def acc(r, c):
        return pl.BlockSpec((r, c), lambda i: (0, 0))

    return pl.pallas_call(
        body, name=name, grid=(S // tn,),
        in_specs=[col(LAT_ROWS), col(MLA_Q_RANK), col(MLA_KV_RANK), full(g_q), full(g_kv),
                  full(w_uq_p), full(w_ukv), col(QW), col(QW), col(QW), col(LANES), col(LANES), col(LANES),
                  col(QW), col(QW), col(MLA_HEADS * HEAD_DIM), col(GATE_ROWS)],
        out_specs=[col(LAT_ROWS), acc(QW, MLA_Q_RANK), acc(QW, MLA_KV_RANK), acc(MLA_Q_RANK, 1), acc(MLA_KV_RANK, 1)],
        out_shape=[jax.ShapeDtypeStruct((LAT_ROWS, S), BF16), jax.ShapeDtypeStruct((QW, MLA_Q_RANK), F32),
                   jax.ShapeDtypeStruct((QW, MLA_KV_RANK), F32), jax.ShapeDtypeStruct((MLA_Q_RANK, 1), F32),
                   jax.ShapeDtypeStruct((MLA_KV_RANK, 1), F32)],
        scratch_shapes=[pltpu.VMEM((LAT_ROWS, tn), F32)],
        compiler_params=_params("arbitrary"),
    )(lat, nq, nkv, g_q, g_kv, w_uq_p, w_ukv, *tq, *tmisc, dq, dk, dv, dflog)


def _dproj_cast(dqa, dkva, dqf, dkf, dvf, dlat, *, name):
    S = dqa.shape[1]
    tn = _tile(S, 512)
    parts = (dqa, dkva, dqf, dkf, dvf, dlat)

    def body(*refs):
        o_ref = refs[-1]
        r0 = 0
        for ref in refs[:-1]:
            n = ref.shape[0]
            o_ref[r0:r0 + n, :] = ref[...].astype(BF16)
            r0 += n

    return pl.pallas_call(
        body, name=name, grid=(S // tn,),
        in_specs=[pl.BlockSpec((p.shape[0], tn), lambda i: (0, i)) for p in parts],
        out_specs=pl.BlockSpec((IN_ROWS, tn), lambda i: (0, i)),
        out_shape=jax.ShapeDtypeStruct((IN_ROWS, S), BF16),
        compiler_params=_params("parallel"),
    )(*parts)


GELU_C = math.sqrt(2.0 / math.pi)
GELU_A = 0.044715


HALO = 16


def _shift_down(a, k, fill):
    r = pltpu.roll(a, k, 0)
    row = lax.broadcasted_iota(jnp.int32, (8, a.shape[1]), 0)
    head = r[0:8, :]
    for i in range(k):
        head = jnp.where(row == i, fill[len(fill) - k + i], head)
    return jnp.concatenate([head, r[8:, :]], axis=0)


def _shift_up(d, k, fill):
    n = d.shape[0]
    r = pltpu.roll(d, n - k, 0)
    row = lax.broadcasted_iota(jnp.int32, (8, d.shape[1]), 0)
    tail = r[n - 8:n, :]
    for i in range(k):
        tail = jnp.where(row == 8 - k + i, fill[i], tail)
    return jnp.concatenate([r[0:n - 8, :], tail], axis=0)


def _conv_taps(a, before, w_ref, b_ref):
    a1 = _shift_down(a, 1, before)
    a2 = _shift_down(a, 2, before)
    return ((b_ref[...] + w_ref[0:1, :] * a2) + w_ref[1:2, :] * a1) + w_ref[2:3, :] * a


def _rows_before(halo_ref, first):
    h = halo_ref[HALO - 2:HALO, :].astype(F32)
    return jnp.where(first, 0.0, h[0:1, :]), jnp.where(first, 0.0, h[1:2, :])


def _conv_specs(S, tm, tc, nc):
    hb = tm // HALO
    main = lambda off: pl.BlockSpec((tm, tc), lambda j, i: (i, j + off))
    prev = lambda off: pl.BlockSpec((HALO, tc), lambda j, i: (jnp.maximum(i * hb - 1, 0), j + off))
    wspec = lambda off: pl.BlockSpec((3, tc), lambda j, i: (0, j + off))
    bspec = lambda off: pl.BlockSpec((1, tc), lambda j, i: (0, j + off))
    return main, prev, wspec, bspec


def _conv_geglu_fwd(a, conv_w, conv_b, *, name):
    S = a.shape[0]
    tm, tc = _tile(S, 512), _tile(D_FF, 1408)
    nc = D_FF // tc
    main, prev, wspec, bspec = _conv_specs(S, tm, tc, nc)

    def body(ag_ref, au_ref, hg_ref, hu_ref, wg_ref, wu_ref, bg_ref, bu_ref, u_ref, z_ref):
        first = pl.program_id(1) == 0
        gate = _conv_taps(ag_ref[...].astype(F32), _rows_before(hg_ref, first), wg_ref, bg_ref)
        up = _conv_taps(au_ref[...].astype(F32), _rows_before(hu_ref, first), wu_ref, bu_ref)
        u_ref[0] = gate
        u_ref[1] = up
        cdf = 0.5 * (1.0 + jnp.tanh(GELU_C * (gate + GELU_A * (gate * gate * gate))))
        z_ref[...] = (gate * cdf * up).astype(BF16)

    return pl.pallas_call(
        body, name=name, grid=(nc, S // tm),
        in_specs=[main(0), main(nc), prev(0), prev(nc), wspec(0), wspec(nc), bspec(0), bspec(nc)],
        out_specs=[pl.BlockSpec((2, tm, tc), lambda j, i: (0, i, j)), pl.BlockSpec((tm, tc), lambda j, i: (i, j))],
        out_shape=[jax.ShapeDtypeStruct((2, S, D_FF), F32), jax.ShapeDtypeStruct((S, D_FF), BF16)],
        compiler_params=_params("parallel", "arbitrary"),
    )(a, a, a, a, conv_w, conv_w, conv_b, conv_b)


def _geglu_bwd(gate, up, dz):
    g2x = gate * gate
    th = jnp.tanh(GELU_C * (gate + GELU_A * (g2x * gate)))
    cdf = 0.5 * (1.0 + th)
    dgelu = cdf + gate * (0.5 * (1.0 - th * th) * (GELU_C * (1.0 + 3.0 * GELU_A * g2x)))
    return dz * up * dgelu, dz * (gate * cdf)


def _conv_geglu_bwd(a, u, conv_w, dz, *, name):
    S = a.shape[0]
    tm, tc = _tile(S, 512), _tile(D_FF, 1408)
    nc = D_FF // tc
    nr = S // tm
    main, _, wspec, _ = _conv_specs(S, tm, tc, nc)
    hb = tm // 8

    def body(ag_ref, au_ref, u_ref, un_ref, wg_ref, wu_ref, dz_ref, dzn_ref, da_ref, dw_ref, db_ref):
        i = pl.program_id(1)
        last = i == nr - 1

        @pl.when(i == 0)
        def _():
            dw_ref[...] = jnp.zeros_like(dw_ref)
            db_ref[...] = jnp.zeros_like(db_ref)

        dus = _geglu_bwd(u_ref[0], u_ref[1], dz_ref[...])
        dus_n = _geglu_bwd(un_ref[0], un_ref[1], dzn_ref[...])
        for half, a_ref, w_ref in ((0, ag_ref, wg_ref), (1, au_ref, wu_ref)):
            du, du_n = dus[half], dus_n[half]
            after = (jnp.where(last, 0.0, du_n[0:1, :]), jnp.where(last, 0.0, du_n[1:2, :]))
            shifted = (_shift_up(du, 2, after), _shift_up(du, 1, after), du)
            da_ref[half] = (w_ref[2:3, :] * du + w_ref[1:2, :] * shifted[1] + w_ref[0:1, :] * shifted[0]).astype(BF16)
            af = a_ref[...].astype(F32)
            for tap in range(3):
                dw_ref[half, tap:tap + 1, :] += jnp.sum(shifted[tap] * af, axis=0, keepdims=True)
            db_ref[half] += jnp.sum(du, axis=0, keepdims=True)

    nxt8 = lambda j, i: (0, jnp.minimum((i + 1) * hb, S // 8 - 1), j)
    return pl.pallas_call(
        body, name=name, grid=(nc, nr),
        in_specs=[main(0), main(nc), pl.BlockSpec((2, tm, tc), lambda j, i: (0, i, j)), pl.BlockSpec((2, 8, tc), nxt8),
                  wspec(0), wspec(nc), pl.BlockSpec((tm, tc), lambda j, i: (i, j)),
                  pl.BlockSpec((8, tc), lambda j, i: (jnp.minimum((i + 1) * hb, S // 8 - 1), j))],
        out_specs=[pl.BlockSpec((2, tm, tc), lambda j, i: (0, i, j)), pl.BlockSpec((2, 3, tc), lambda j, i: (0, 0, j)),
                   pl.BlockSpec((2, 1, tc), lambda j, i: (0, 0, j))],
        out_shape=[jax.ShapeDtypeStruct((2, S, D_FF), BF16), jax.ShapeDtypeStruct((2, 3, D_FF), F32),
                   jax.ShapeDtypeStruct((2, 1, D_FF), F32)],
        compiler_params=_params("parallel", "arbitrary"),
    )(a, a, u, u, conv_w, conv_w, dz, dz)


ROW_BLOCK_BYTES = 1536 * 1024


def _row_tile(rows, cols):
    return rows if rows * cols * 4 <= ROW_BLOCK_BYTES else _tile(rows, ROW_TILE)


def _adamw_update(w, g, m, v):
    m = ADAM_B1 * m + (1.0 - ADAM_B1) * g
    v = ADAM_B2 * v + (1.0 - ADAM_B2) * jnp.square(g)
    m_hat = m / (1.0 - ADAM_B1 ** ADAM_STEP)
    v_hat = v / (1.0 - ADAM_B2 ** ADAM_STEP)
    return -ADAM_LR * (m_hat / (jnp.sqrt(v_hat) + ADAM_EPS) + ADAM_WD * w), m, v


def _adamw(w, g, m, v, *, name):
    L, A, B = w.shape
    ta = _tile(A, ROW_TILE)

    def body(w_ref, g_ref, m_ref, v_ref, d_ref, mo_ref, vo_ref):
        d_ref[...], mo_ref[...], vo_ref[...] = _adamw_update(w_ref[...], g_ref[...], m_ref[...], v_ref[...])

    blk = pl.BlockSpec((None, ta, B), lambda l, i: (l, i, 0))
    shp = jax.ShapeDtypeStruct((L, A, B), F32)
    return pl.pallas_call(
        body, name=name, grid=(L, A // ta),
        in_specs=[blk] * 4, out_specs=[blk] * 3, out_shape=[shp] * 3,
        compiler_params=_params("parallel", "parallel"),
    )(w, g, m, v)


def _scalar(v):
    return jnp.reshape(v, (1,)).astype(jnp.int32)


def _adamw_halves(w, g_mine, g_other, m, v, *, name):
    L, A, B = w.shape
    ta = _row_tile(A // 2, B)
    nb = A // 2 // ta

    def body(c_ref, w_ref, gm_ref, go_ref, m_ref, v_ref, g_ref, d_ref, mo_ref, vo_ref):
        g = jnp.where(pl.program_id(1) // nb == c_ref[0], gm_ref[...], go_ref[...])
        g_ref[...] = g
        d_ref[...], mo_ref[...], vo_ref[...] = _adamw_update(w_ref[...], g, m_ref[...], v_ref[...])

    blk = pl.BlockSpec((None, ta, B), lambda l, i, c_ref: (l, i, 0))
    half = pl.BlockSpec((None, ta, B), lambda l, i, c_ref: (l, i % nb, 0))
    shp = jax.ShapeDtypeStruct((L, A, B), F32)
    return pl.pallas_call(
        body, name=name,
        grid_spec=pltpu.PrefetchScalarGridSpec(num_scalar_prefetch=1, grid=(L, A // ta),
                                               in_specs=[blk, half, half, blk, blk], out_specs=[blk] * 4),
        out_shape=[shp] * 4,
        compiler_params=_params("parallel", "parallel"),
    )(_scalar(lax.axis_index("c")), w, g_mine, g_other, m, v)


def _chip_index():
    return 2 * lax.axis_index("x") + lax.axis_index("y")


def _pair_sum(g, recv, *, name):
    n, A, B = g.shape
    ta = _row_tile(A // 2, B)
    nb = A // 2 // ta

    def body(c_ref, g_ref, r_ref, o_ref):
        o_ref[...] = g_ref[...] + r_ref[...]

    return pl.pallas_call(
        body, name=name,
        grid_spec=pltpu.PrefetchScalarGridSpec(
            num_scalar_prefetch=1, grid=(n, nb),
            in_specs=[pl.BlockSpec((None, ta, B), lambda s, r, c_ref: (s, c_ref[0] * nb + r, 0)),
                      pl.BlockSpec((None, ta, B), lambda s, r, c_ref: (s, r, 0))],
            out_specs=pl.BlockSpec((None, ta, B), lambda s, r, c_ref: (s, r, 0))),
        out_shape=jax.ShapeDtypeStruct((n, A // 2, B), F32),
        compiler_params=_params("parallel", "parallel"),
    )(_scalar(lax.axis_index("c")), g, recv)


def _chip_sum(landed, own, *, name):
    n, A2, B = landed.shape
    ta = _row_tile(A2, B)

    def body(me_ref, *refs):
        slots, own_ref, o_ref = refs[:n], refs[n], refs[n + 1]
        parts = [jnp.where(me_ref[0] == s, own_ref[...], slots[s][...]) for s in range(n)]
        o_ref[...] = ((parts[0] + parts[1]) + parts[2]) + parts[3]

    def slot(s):
        return pl.BlockSpec((None, ta, B), lambda r, me_ref: (jnp.where(me_ref[0] == s, (s + 1) % n, s), r, 0))

    return pl.pallas_call(
        body, name=name,
        grid_spec=pltpu.PrefetchScalarGridSpec(
            num_scalar_prefetch=1, grid=(A2 // ta,),
            in_specs=[slot(s) for s in range(n)] + [pl.BlockSpec((None, ta, B), lambda r, me_ref: (me_ref[0], r, 0))],
            out_specs=pl.BlockSpec((ta, B), lambda r, me_ref: (r, 0))),
        out_shape=jax.ShapeDtypeStruct((A2, B), F32),
        compiler_params=_params("parallel"),
    )(_scalar(_chip_index()), *([landed] * n), own)


HBM_SPEC = pl.BlockSpec(memory_space=pl.ANY)
COMM_PARAMS = pltpu.CompilerParams(has_side_effects=True)


def _mesh_pos():
    return lax.axis_index("x"), lax.axis_index("y"), lax.axis_index("c")


def _other_chips(x, y):
    return [(1 - x, y), (x, 1 - y), (1 - x, 1 - y)]


def _remote(src, dst, send_sems, recv_sems, k, to):
    return pltpu.make_async_remote_copy(src_ref=src, dst_ref=dst, send_sem=send_sems.at[k], recv_sem=recv_sems.at[k],
                                        device_id=to, device_id_type=MESH)


def _place_own(gathered, shards, *, name):
    n = len(shards)

    def body(me_ref, *refs):
        for s_ref, o_ref in zip(refs[:n], refs[2 * n:]):
            o_ref[...] = s_ref[...]

    return pl.pallas_call(
        body, name=name,
        grid_spec=pltpu.PrefetchScalarGridSpec(
            num_scalar_prefetch=1, grid=(1,),
            in_specs=[pl.BlockSpec(s.shape, lambda i, me_ref: (0, 0)) for s in shards] + [HBM_SPEC] * n,
            out_specs=[pl.BlockSpec((None,) + s.shape, lambda i, me_ref: (me_ref[0], 0, 0)) for s in shards]),
        out_shape=[jax.ShapeDtypeStruct(g.shape, g.dtype) for g in gathered],
        input_output_aliases={1 + n + k: k for k in range(n)},
        compiler_params=_params("arbitrary"),
    )(_scalar(_chip_index()), *shards, *gathered)


def _half_rows(rows, c, align=8):
    assert (rows // 2) % align == 0
    return pl.ds(pl.multiple_of(c * (rows // 2), align), rows // 2)


BF16_ROWS = 16


def _halved(rows):
    return rows % (2 * BF16_ROWS) == 0


def _gather_copies(srcs, lands, send_sems, recv_sems):
    x, y, c = _mesh_pos()
    me = 2 * x + y
    out = []
    for k in range(len(srcs)):
        a = srcs[k].shape[0]
        rows = _half_rows(a, c, BF16_ROWS) if _halved(a) else pl.ds(0, a)
        for j, (px, py) in enumerate(_other_chips(x, y)):
            send = _remote(srcs[k].at[rows], lands[k].at[me, rows], send_sems, recv_sems, 3 * k + j, (px, py, c))
            recv = _remote(srcs[k].at[rows], lands[k].at[2 * px + py, rows], send_sems, recv_sems, 3 * k + j, (px, py, c))
            out.append((send, recv))
    return out


def _gather_start(srcs):
    nl, n = len(srcs), len(srcs[0])
    lands = [[lax.empty((N_CHIPS,) + s.shape, s.dtype) for s in sl] for sl in srcs]
    flat = [a for l in range(nl) for a in srcs[l] + lands[l]]

    def body(*refs):
        bufs, sems, token = refs[:len(flat)], refs[len(flat):len(flat) + 2 * nl], refs[-1]
        for l in range(nl):
            mine = bufs[2 * n * l:2 * n * (l + 1)]
            for send, _ in _gather_copies(mine[:n], mine[n:], sems[2 * l], sems[2 * l + 1]):
                send.start()
        token[...] = jnp.zeros_like(token)

    res = pl.pallas_call(
        body, name="weight_gather_start",
        in_specs=[HBM_ONLY] * len(flat),
        out_specs=[SEM_SPEC] * (2 * nl) + [HBM_ONLY] * len(flat) + [pl.BlockSpec(memory_space=pltpu.VMEM)],
        out_shape=[pltpu.SemaphoreType.DMA((3 * n,))] * (2 * nl) + [pltpu.HBM(a.shape, a.dtype) for a in flat]
        + [jax.ShapeDtypeStruct((8, LANES), F32)],
        input_output_aliases={i: 2 * nl + i for i in range(len(flat))},
        compiler_params=SPLIT_PARAMS,
    )(*[pltpu.with_memory_space_constraint(a, pltpu.HBM) for a in flat])
    bufs = res[2 * nl:2 * nl + len(flat)]
    state = [(res[2 * l], res[2 * l + 1], list(bufs[2 * n * l:2 * n * l + n]), list(bufs[2 * n * l + n:2 * n * (l + 1)]))
             for l in range(nl)]
    return state, res[-1][0:1, 0:1]


def _gather_wait(state, after, *, name):
    send_sems, recv_sems, srcs, lands = state
    n = len(srcs)

    def body(*refs):
        for send, recv in _gather_copies(refs[:n], refs[n:2 * n], refs[2 * n], refs[2 * n + 1]):
            send.wait_send()
            recv.wait_recv()

    res = pl.pallas_call(
        body, name=name,
        in_specs=[HBM_ONLY] * (2 * n) + [SEM_SPEC, SEM_SPEC, HBM_SPEC],
        out_specs=[HBM_ONLY] * (2 * n),
        out_shape=[pltpu.HBM(a.shape, a.dtype) for a in srcs + lands],
        input_output_aliases={i: i for i in range(2 * n)},
        compiler_params=SPLIT_PARAMS,
    )(*srcs, *lands, send_sems, recv_sems, after)
    return list(res[:n]), list(res[n:])


def _gather_forward(lands, *, name):
    n = len(lands)

    def body(*refs):
        bufs, outs = refs[:n], refs[n:2 * n]
        send_sems, recv_sems = refs[2 * n:]
        x, y, c = _mesh_pos()
        copies, waits = [], []
        for k in range(n):
            a = lands[k].shape[1]
            if not _halved(a):
                continue
            for j, (px, py) in enumerate(_other_chips(x, y)):
                mine = 2 * px + py, _half_rows(a, c, BF16_ROWS)
                copies.append(_remote(bufs[k].at[mine], outs[k].at[mine], send_sems, recv_sems, 3 * k + j, (x, y, 1 - c)))
                lands_here = outs[k].at[2 * px + py, _half_rows(a, 1 - c, BF16_ROWS)]
                waits.append(_remote(lands_here, lands_here, send_sems, recv_sems, 3 * k + j, (x, y, 1 - c)))
        for cp in copies:
            cp.start()
        for cp in waits:
            cp.wait_recv()
        for cp in copies:
            cp.wait_send()

    return pl.pallas_call(
        body, name=name,
        in_specs=[HBM_SPEC] * n, out_specs=[HBM_SPEC] * n,
        out_shape=[jax.ShapeDtypeStruct(a.shape, a.dtype) for a in lands],
        scratch_shapes=[pltpu.SemaphoreType.DMA((3 * n,)), pltpu.SemaphoreType.DMA((3 * n,))],
        input_output_aliases={i: i for i in range(n)},
        compiler_params=COMM_PARAMS,
    )(*lands)


def _sibling_exchange(gs, *, name):
    n = len(gs)

    def body(*refs):
        ins, outs = refs[:n], refs[n:2 * n]
        send_sems, recv_sems = refs[2 * n:]
        x, y, c = _mesh_pos()
        copies = [_remote(ins[k].at[:, _half_rows(gs[k].shape[1], 1 - c)], outs[k], send_sems, recv_sems, k, (x, y, 1 - c))
                  for k in range(n)]
        for cp in copies:
            cp.start()
        for cp in copies:
            cp.wait()

    return pl.pallas_call(
        body, name=name,
        in_specs=[HBM_SPEC] * n, out_specs=[HBM_SPEC] * n,
        out_shape=[jax.ShapeDtypeStruct((g.shape[0], g.shape[1] // 2, g.shape[2]), g.dtype) for g in gs],
        scratch_shapes=[pltpu.SemaphoreType.DMA((n,)), pltpu.SemaphoreType.DMA((n,))],
        compiler_params=COMM_PARAMS,
    )(*gs)


HBM_ONLY = pl.BlockSpec(memory_space=pltpu.HBM)
SEM_SPEC = pl.BlockSpec(memory_space=pltpu.SEMAPHORE)
SPLIT_PARAMS = pltpu.CompilerParams(has_side_effects=pltpu.SideEffectType.DATAFLOW_SIDE_EFFECTING)


def _scatter_copies(srcs, lands, send_sems, recv_sems):
    x, y, c = _mesh_pos()
    me = 2 * x + y
    out = []
    for k in range(len(srcs)):
        for j, (px, py) in enumerate(_other_chips(x, y)):
            s = 2 * px + py
            send = _remote(srcs[k].at[s], lands[k].at[me], send_sems, recv_sems, 3 * k + j, (px, py, c))
            recv = _remote(srcs[k].at[s], lands[k].at[s], send_sems, recv_sems, 3 * k + j, (px, py, c))
            out.append((send, recv))
    return out


def _exchange_copies(srcs, lands, send_sems, recv_sems):
    x, y, c = _mesh_pos()
    out = []
    for k in range(len(srcs)):
        cp = _remote(srcs[k].at[:, _half_rows(srcs[k].shape[1], 1 - c)], lands[k], send_sems, recv_sems, k, (x, y, 1 - c))
        out.append((cp, cp))
    return out


def _split_start(srcs, land_shapes, copies, n_sems, *, name):
    n = len(srcs)
    lands = [lax.empty(shape, s.dtype) for shape, s in zip(land_shapes, srcs)]

    def body(*refs):
        ins, zones = refs[:n], refs[n:2 * n]
        send_sems, recv_sems, token = refs[2 * n], refs[2 * n + 1], refs[-1]
        for send, _ in copies(ins, zones, send_sems, recv_sems):
            send.start()
        token[...] = jnp.zeros_like(token)

    hbm = lambda a: pltpu.HBM(a.shape, a.dtype)
    res = pl.pallas_call(
        body, name=name,
        in_specs=[HBM_ONLY] * (2 * n),
        out_specs=[SEM_SPEC, SEM_SPEC] + [HBM_ONLY] * (2 * n) + [pl.BlockSpec(memory_space=pltpu.VMEM)],
        out_shape=[pltpu.SemaphoreType.DMA((n_sems,)), pltpu.SemaphoreType.DMA((n_sems,))] + [hbm(a) for a in srcs + lands]
        + [jax.ShapeDtypeStruct((8, LANES), F32)],
        input_output_aliases={i: 2 + i for i in range(2 * n)},
        compiler_params=SPLIT_PARAMS,
    )(*[pltpu.with_memory_space_constraint(a, pltpu.HBM) for a in srcs + lands])
    return (res[0], res[1], list(res[2:2 + n]), list(res[2 + n:2 + 2 * n])), res[-1][0:1, 0:1]


def _scatter_start(ps, *, name):
    return _split_start(ps, [p.shape for p in ps], _scatter_copies, 3 * len(ps), name=name)


def _exchange_start(gs, *, name):
    return _split_start(gs, [(g.shape[0], g.shape[1] // 2, g.shape[2]) for g in gs], _exchange_copies, len(gs), name=name)


def _split_wait(started, copies, after, *, name):
    ng = len(started)
    sizes = [len(st[2]) for st in started]
    offs = [2 * sum(sizes[:i]) for i in range(ng + 1)]
    flat = [a for (_, _, ps, lands) in started for a in ps + lands]

    def body(*refs):
        bufs, sems = refs[:len(flat)], refs[len(flat):len(flat) + 2 * ng]
        for i, n in enumerate(sizes):
            srcs, zones = bufs[offs[i]:offs[i] + n], bufs[offs[i] + n:offs[i + 1]]
            for send, recv in copies(srcs, zones, sems[2 * i], sems[2 * i + 1]):
                send.wait_send()
                recv.wait_recv()

    res = pl.pallas_call(
        body, name=name,
        in_specs=[HBM_ONLY] * len(flat) + [SEM_SPEC] * (2 * ng) + [HBM_SPEC],
        out_specs=[HBM_ONLY] * len(flat),
        out_shape=[pltpu.HBM(a.shape, a.dtype) for a in flat],
        input_output_aliases={i: i for i in range(len(flat))},
        compiler_params=SPLIT_PARAMS,
    )(*flat, *[s for (ss, rs, _, _) in started for s in (ss, rs)], after)
    return [(list(res[offs[i]:offs[i] + n]), list(res[offs[i] + n:offs[i + 1]])) for i, n in enumerate(sizes)]


def _sibling_share(hs):
    n = len(hs)

    def body(*refs):
        ins, outs = refs[:n], refs[n:2 * n]
        send_sems, recv_sems = refs[2 * n:]
        x, y, c = _mesh_pos()
        copies = [_remote(ins[k], outs[k], send_sems, recv_sems, k, (x, y, 1 - c)) for k in range(n)]
        for cp in copies:
            cp.start()
        for cp in copies:
            cp.wait()

    return pl.pallas_call(
        body, name="grad_sibling_share",
        in_specs=[HBM_SPEC] * n, out_specs=[HBM_SPEC] * n,
        out_shape=[jax.ShapeDtypeStruct(h.shape, h.dtype) for h in hs],
        scratch_shapes=[pltpu.SemaphoreType.DMA((n,)), pltpu.SemaphoreType.DMA((n,))],
        compiler_params=COMM_PARAMS,
    )(*hs)


def _allreduce_small(part):
    rows, C = part.shape

    def body(p_ref, o_ref, slots, send_sems, recv_sems):
        x, y, c = _mesh_pos()
        me = 4 * x + 2 * y + c
        slots[me] = p_ref[...]
        copies = []
        for k in range(1, 8):
            kx, ky, kc = (k >> 2) & 1, (k >> 1) & 1, k & 1
            peer = (x ^ kx if kx else x, y ^ ky if ky else y, c ^ kc if kc else c)
            cp = _remote(p_ref, slots.at[me], send_sems, recv_sems, k - 1, peer)
            cp.start()
            copies.append((cp, peer))
        for k, (cp, peer) in enumerate(copies):
            src = 4 * peer[0] + 2 * peer[1] + peer[2]
            _remote(p_ref, slots.at[src], send_sems, recv_sems, k, peer).wait_recv()
        for cp, _ in copies:
            cp.wait_send()
        total = slots[0]
        for d in range(1, 8):
            total = total + slots[d]
        o_ref[...] = total

    return pl.pallas_call(
        body, name="small_grad_allreduce",
        in_specs=[pl.BlockSpec(memory_space=pltpu.VMEM)], out_specs=pl.BlockSpec(memory_space=pltpu.VMEM),
        out_shape=jax.ShapeDtypeStruct((rows, C), F32),
        scratch_shapes=[pltpu.VMEM((8, rows, C), F32), pltpu.SemaphoreType.DMA((7,)), pltpu.SemaphoreType.DMA((7,))],
        compiler_params=pltpu.CompilerParams(has_side_effects=True, vmem_limit_bytes=VMEM_LIMIT_BYTES),
    )(part)


def _pad_w_uq(w):
    lead = w.shape[:-1]
    w = w.reshape(lead + (MLA_HEADS, MLA_QK))
    w = jnp.concatenate([w, jnp.zeros(lead + (MLA_HEADS, MLA_PAD - MLA_QK), w.dtype)], axis=-1)
    return w.reshape(lead + (MLA_HEADS * MLA_PAD,))


def _unpad_w_uq(g):
    lead = g.shape[:-1]
    return g.reshape(lead + (MLA_HEADS, MLA_PAD))[..., :MLA_QK].reshape(lead + (MLA_HEADS * MLA_QK,))


def _t(a):
    return jnp.swapaxes(a, -1, -2)


def _shards_of_cols(w):
    A, NB = w.shape
    return w.reshape(A, N_CHIPS, NB // N_CHIPS).transpose(1, 0, 2)


BIG = ("w_in", "w_uq", "w_ukv", "w_out", "w_up", "w_down")
SMALL = ("attn_pre_norm", "forget_bias", "swa_sinks", "rel_bias", "q_latent_norm", "kv_latent_norm", "group_norm",
         "attn_post_norm", "ffn_pre_norm", "conv_b", "ffn_post_norm")
WEIGHTS = ("attn_pre_norm", "w_in", "forget_bias", "swa_sinks", "rel_bias", "q_latent_norm", "w_uq", "kv_latent_norm",
           "w_ukv", "group_norm", "w_out", "attn_post_norm", "ffn_pre_norm", "w_up", "conv_w", "conv_b", "w_down",
           "ffn_post_norm")


PACK_UNIT = 8 * LANES


def _pack_rows(shape):
    return -(-int(np.prod(shape)) // PACK_UNIT) * 8


def _pack(arrs, row_mult=8):
    parts = []
    for a in arrs:
        n = int(np.prod(a.shape))
        parts.append(jnp.pad(a.reshape(-1), (0, _pack_rows(a.shape) * LANES - n)).reshape(-1, LANES))
    rows = sum(p.shape[0] for p in parts)
    pad = -rows % row_mult
    if pad:
        parts.append(jnp.zeros((pad, LANES), parts[0].dtype))
    return jnp.concatenate(parts, axis=0)


def _unpack(packed, shapes):
    packed = packed.reshape(-1, LANES)
    out, off = [], 0
    for shp in shapes:
        r = _pack_rows(shp)
        out.append(packed[off:off + r].reshape(-1)[:int(np.prod(shp))].reshape(shp))
        off += r
    return out


LAYER_KEYS = ("w_qkv_t", "w_lat_t", "w_in_t", "w_uq_p", "w_uq_t", "w_ukv", "w_ukv_t", "w_out", "w_up", "w_down", "conv_w")


def _layer_weights(gathered):
    cols = lambda g: g.transpose(1, 0, 2).reshape(g.shape[1], N_CHIPS * g.shape[2])
    w_in_t = _t(gathered["w_in"]).reshape(IN_COLS, D_MODEL)
    w_in_t = jnp.pad(w_in_t, ((0, IN_ROWS - IN_COLS), (0, 0)))
    w_uq_p = _pad_w_uq(cols(gathered["w_uq"]))
    w_ukv = cols(gathered["w_ukv"])
    return dict(w_qkv_t=w_in_t[:QKV_ROWS], w_lat_t=w_in_t[QKV_ROWS:], w_in_t=w_in_t, w_uq_p=w_uq_p, w_uq_t=_t(w_uq_p),
                w_ukv=w_ukv, w_ukv_t=_t(w_ukv), w_out=gathered["w_out"].reshape(D_MODEL, D_MODEL), w_up=gathered["w_up"],
                w_down=gathered["w_down"].reshape(D_FF, D_MODEL), conv_w=cols(gathered["conv_w"]))


def _local_step(x, target, W, layer_weights, layer_done):
    W = dict(W, **{key: [None] * DEPTH for key in LAYER_KEYS})
    S = x.shape[0]
    tq_tabs, tm_tabs = _rope_tables(S)
    onehot_t = _rel_onehot_t()
    bias_t = _bias_table(W["rel_bias"].T, onehot_t).reshape(SWA_KV_HEADS, SWA_GROUP, 2 * WINDOW, WINDOW)
    bias_t = bias_t.transpose(0, 2, 1, 3).reshape(SWA_KV_HEADS, 2 * WINDOW, GW)
    row = lambda a: a.reshape(1, -1)
    col = lambda a: a.reshape(-1, 1)
    fox_rows = (FOX_ROW0, FOX_ROW0 + FOX_HEADS * HEAD_DIM, FOX_ROW0 + 2 * FOX_HEADS * HEAD_DIM, SWA_Q_HEADS)
    fox = dict(rows=fox_rows, H=FOX_HEADS, Dk=HEAD_DIM, Dv=HEAD_DIM, scale=HEAD_DIM ** -0.5)
    mla = dict(rows=(0, 0, 0, SWA_Q_HEADS + FOX_HEADS), H=MLA_HEADS, Dk=MLA_PAD, Dv=HEAD_DIM, scale=MLA_SCALE, q_scaled=True)

    saved = []
    h = _rms_fwd(x, row(W["attn_pre_norm"][0]), name="rms_in")
    for l in range(DEPTH):
        sv = {"x0": x, "h1": h}
        for key, val in layer_weights(l, h).items():
            W[key][l] = val
        qkv = _matmul(W["w_qkv_t"][l], h, tb=True, out_dtype=BF16, name="proj_qkv")
        lat = _matmul(W["w_lat_t"][l], h, tb=True, name="proj_lat")
        oa, lse_a = _swa_fwd(qkv, bias_t, W["swa_sinks"][l], name="swa_fwd")
        fb_col = jnp.pad(col(W["forget_bias"][l]), ((0, GATE_ROWS - FOX_HEADS), (0, 0)))
        f4 = _gate_fwd(lat, fb_col, name="fox_gate_fwd")[:FOX_HEADS]
        f2 = f4 * LOG2E
        f_row, f_col = f2[:, None, :], f2.T
        of, lse_f = _attn_fwd(qkv, qkv, qkv, f_row=f_row, f_col=f_col, name="fox_fwd", **fox)
        nq, nkv, qm, km, vm = _mla_prep_fwd(lat, col(W["q_latent_norm"][l]), col(W["kv_latent_norm"][l]), W["w_uq_t"][l],
                                            W["w_ukv_t"][l], tq_tabs, tm_tabs, name="mla_prep_fwd")
        oc, lse_c = _attn_fwd(qm, km, vm, name="mla_fwd", **mla)
        mixed = _group_norm_fwd(oa, of, oc, col(W["group_norm"][l]), name="group_norm_fwd")
        y = _matmul(mixed, W["w_out"][l], ta=True, name="proj_out")
        x1, h2 = _resid_rms(x, y, row(W["attn_post_norm"][l]), row(W["ffn_pre_norm"][l]), name="attn_resid")
        a = _matmul(h2, W["w_up"][l], b_shards=True, out_dtype=BF16, name="ffn_up")
        u, z = _conv_geglu_fwd(a, W["conv_w"][l], row(W["conv_b"][l]), name="conv_geglu_fwd")
        y2 = _matmul(z, W["w_down"][l], name="ffn_down")
        g_next = row(W["attn_pre_norm"][l + 1]) if l + 1 < DEPTH else None
        x2, h_next = _resid_rms(x1, y2, row(W["ffn_post_norm"][l]), g_next, name="ffn_resid")
        sv.update(qkv=qkv, lat=lat, oa=oa, lse_a=lse_a, fb_col=fb_col, f_row=f_row, f_col=f_col, of=of, lse_f=lse_f,
                  nq=nq, nkv=nkv, qm=qm, km=km, vm=vm, oc=oc, lse_c=lse_c, mixed=mixed, y=y, x1=x1, h2=h2, a=a, u=u, z=z, y2=y2)
        saved.append(sv)
        x, h = x2, h_next

    loss, dx = _loss_head(x, target)

    G = {k: [None] * DEPTH for k in WEIGHTS if k != "rel_bias" and k not in BIG}
    dbias_layers = [None] * DEPTH
    for l in reversed(range(DEPTH)):
        sv = saved[l]
        gb = {}
        if l == DEPTH - 1:
            dy2, dg = _rms_bwd(sv["y2"], row(W["ffn_post_norm"][l]), dx, out_dtype=BF16, name="ffn_post_bwd")
            G["ffn_post_norm"][l] = dg[0]
        dz = _matmul(dy2, W["w_down"][l], tb=True, name="ffn_down_dx")
        gb["w_down"] = _matmul(sv["z"], dy2, ta=True, name="ffn_down_dw").reshape(N_CHIPS, D_FF // N_CHIPS, D_MODEL)
        da, dcw, dcb = _conv_geglu_bwd(sv["a"], sv["u"], W["conv_w"][l], dz, name="conv_geglu_bwd")
        G["conv_w"][l] = dcw.transpose(1, 0, 2).reshape(3, 2 * D_FF)
        G["conv_b"][l] = dcb.reshape(2 * D_FF)
        dh2 = _matmul(da, W["w_up"][l], tb=True, b_shards=True, a_halves=True, name="ffn_up_dx")
        gb["w_up"] = _matmul(sv["h2"], da, ta=True, out_shards=True, b_halves=True, name="ffn_up_dw")
        token = layer_done(l, gb)
        gb = {}
        dx1, dg, dy, dg_post = _rms_bwd(sv["x1"], row(W["ffn_pre_norm"][l]) + token, dh2, resid=dx, out_dtype=F32,
                                        then=(sv["y"], row(W["attn_post_norm"][l])), name="ffn_pre_bwd")
        G["ffn_pre_norm"][l] = dg[0]
        G["attn_post_norm"][l] = dg_post[0]
        dmixed = _matmul(W["w_out"][l], dy, tb=True, name="proj_out_dx")
        gb["w_out"] = _matmul(sv["mixed"], dy, name="proj_out_dw").reshape(N_CHIPS, D_MODEL // N_CHIPS, D_MODEL)
        doa, dof, doc, dg, delta = _group_norm_bwd(sv["oa"], sv["of"], sv["oc"], col(W["group_norm"][l]), dmixed,
                                                   name="group_norm_bwd")
        G["group_norm"][l] = dg[:, 0]
        dqa, dkva, dbias_l, dsink = _swa_bwd(sv["qkv"], bias_t, W["swa_sinks"][l], doa, sv["lse_a"],
                                             delta.reshape(-1, S), name="swa_bwd")
        dbias_layers[l] = (dbias_l.reshape(SWA_KV_HEADS, 2 * WINDOW, SWA_GROUP, WINDOW).transpose(0, 2, 1, 3)
                           .reshape(SWA_Q_HEADS, -1))
        G["swa_sinks"][l] = dsink[:, 0]
        dqf, dkf, dvf, dfk = _attn_bwd(sv["qkv"], sv["qkv"], sv["qkv"], do=dof, lse=sv["lse_f"], delta=delta,
                                       f_row=sv["f_row"], f_col=sv["f_col"], name="fox_bwd", **fox)
        dF = jnp.pad(dfk.T, ((0, GATE_ROWS - FOX_HEADS), (0, 0)))
        dflog, dfb = _gate_bwd(sv["lat"], sv["fb_col"], dF, name="fox_gate_bwd")
        G["forget_bias"][l] = dfb[:FOX_HEADS, 0]
        dqm, dkm, dvm = _attn_bwd(sv["qm"], sv["km"], sv["vm"], do=doc, lse=sv["lse_c"], delta=delta, name="mla_bwd", **mla)
        dlat, dwq_t, dwkv_t, dgq, dgkv = _mla_prep_bwd(
            sv["lat"], sv["nq"], sv["nkv"], col(W["q_latent_norm"][l]), col(W["kv_latent_norm"][l]), W["w_uq_p"][l],
            W["w_ukv"][l], tq_tabs, tm_tabs, dqm, dkm, dvm, dflog, name="mla_prep_bwd")
        gb["w_uq"], gb["w_ukv"] = _shards_of_cols(_unpad_w_uq(dwq_t.T)), _shards_of_cols(dwkv_t.T)
        G["q_latent_norm"][l], G["kv_latent_norm"][l] = dgq[:, 0], dgkv[:, 0]
        dproj = _dproj_cast(dqa, dkva, dqf, dkf, dvf, dlat, name="dproj_cast")
        dh1 = _matmul(dproj, W["w_in_t"][l], ta=True, name="proj_in_dx")
        dw_in_t = _matmul(dproj, sv["h1"], name="proj_in_dw")
        gb["w_in"] = _t(dw_in_t[:IN_COLS].reshape(N_CHIPS, IN_COLS // N_CHIPS, D_MODEL))
        token = layer_done(l, gb)
        below = (saved[l - 1]["y2"], row(W["ffn_post_norm"][l - 1])) if l > 0 else None
        res = _rms_bwd(sv["x0"], row(W["attn_pre_norm"][l]) + token, dh1, resid=dx1, out_dtype=F32, then=below,
                       name="attn_pre_bwd")
        dx, G["attn_pre_norm"][l] = res[0], res[1][0]
        if l > 0:
            dy2, G["ffn_post_norm"][l - 1] = res[2], res[3][0]

    grads = {k: jnp.stack(v) for k, v in G.items()}
    grads["rel_bias"] = _bias_table_bwd(jnp.stack(dbias_layers), onehot_t).T
    return loss, dx, grads


def kernel(x, attn_pre_norm, w_in, forget_bias, swa_sinks, rel_bias, q_latent_norm, w_uq, kv_latent_norm, w_ukv, group_norm, w_out, attn_post_norm, ffn_pre_norm, w_up, conv_w, conv_b, w_down, ffn_post_norm, loss_target, m_attn_pre_norm, m_w_in, m_forget_bias, m_swa_sinks, m_rel_bias, m_q_latent_norm, m_w_uq, m_kv_latent_norm, m_w_ukv, m_group_norm, m_w_out, m_attn_post_norm, m_ffn_pre_norm, m_w_up, m_conv_w, m_conv_b, m_w_down, m_ffn_post_norm, v_attn_pre_norm, v_w_in, v_forget_bias, v_swa_sinks, v_rel_bias, v_q_latent_norm, v_w_uq, v_kv_latent_norm, v_w_ukv, v_group_norm, v_w_out, v_attn_post_norm, v_ffn_pre_norm, v_w_up, v_conv_w, v_conv_b, v_w_down, v_ffn_post_norm):
    args = dict(locals())
    w = {k: args[k] for k in WEIGHTS}
    m = {k: args["m_" + k] for k in WEIGHTS}
    v = {k: args["v_" + k] for k in WEIGHTS}

    sent = BIG + ("conv_w",)
    gather_state, token = _gather_start([[w[k][l] if k == "conv_w" else w[k][l].astype(BF16) for k in sent]
                                         for l in range(DEPTH)])
    W = {k: w[k] for k in SMALL}
    W["attn_pre_norm"] = W["attn_pre_norm"] + token

    def layer_weights(l, after):
        srcs, lands = _gather_wait(gather_state[l], after, name=f"weight_gather_wait_{l}")
        lands = _gather_forward(lands, name=f"weight_gather_forward_{l}")
        lands = _place_own(lands, srcs, name="place_own_shards")
        return _layer_weights(dict(zip(sent, lands)))

    started, groups, pending = [], [], []

    def to_chips(l, keys, gs, recv, tag):
        pair = [_pair_sum(gk, rk, name="grad_pair_sum") for gk, rk in zip(gs, recv)]
        state, token = _scatter_start(pair, name="grad_scatter_start_" + tag)
        started.append(state)
        groups.append((l, keys))
        return token

    def finish_pending(after):
        l, keys, tag, state = pending.pop()
        gs, recv = _split_wait([state], _exchange_copies, after, name="grad_exchange_wait_" + tag)[0]
        return to_chips(l, keys, gs, recv, tag)

    def layer_done(l, gb):
        keys = [k for k in BIG if k in gb]
        gs = [gb[k] for k in keys]
        tag = f"{l}_{keys[0]}"
        token = finish_pending(gs[0]) if pending else 0.0
        if l == 0:
            return token + to_chips(l, keys, gs, _sibling_exchange(gs, name="grad_sibling_exchange_" + tag), tag)
        state, started_token = _exchange_start(gs, name="grad_exchange_start_" + tag)
        pending.append((l, keys, tag, state))
        return token + started_token

    loss_part, dx, g = _local_step(x[0], loss_target[0], W, layer_weights, layer_done)
    loss = lax.psum(loss_part, ("x", "y", "c"))

    reduced = {}
    for (l, keys), (pair, zones) in zip(groups, _split_wait(started, _scatter_copies, dx, name="grad_scatter_wait")):
        for k, p, z in zip(keys, pair, zones):
            reduced[k, l] = _chip_sum(z, p, name="grad_chip_sum")
    mine = [jnp.stack([reduced[k, l] for l in range(DEPTH)]) for k in BIG]
    other = _sibling_share(mine)
    out_g, out_d, out_m, out_v = {}, {}, {}, {}
    for k, g_mine, g_other in zip(BIG, mine, other):
        out_g[k], out_d[k], out_m[k], out_v[k] = _adamw_halves(w[k], g_mine, g_other, m[k], v[k], name="adamw_" + k)

    small_shapes = [w[k].shape for k in SMALL]
    reduced = _allreduce_small(_pack([g[k] for k in SMALL] + [g["conv_w"]]))
    *g_small, g_cw = _unpack(reduced, small_shapes + [g["conv_w"].shape])
    chip = 2 * lax.axis_index("x") + lax.axis_index("y")
    g_small.append(lax.dynamic_slice_in_dim(g_cw, chip * FF_SHARD, FF_SHARD, axis=2))
    names = SMALL + ("conv_w",)
    shapes = small_shapes + [w["conv_w"].shape]
    packed = lambda arrs: _pack(arrs, ROW_TILE)[None]
    d_s, m_s, v_s = _adamw(packed([w[k] for k in names]), packed(g_small), packed([m[k] for k in names]),
                           packed([v[k] for k in names]), name="adamw_small")
    out_g.update(zip(names, g_small))
    out_d.update(zip(names, _unpack(d_s, shapes)))
    out_m.update(zip(names, _unpack(m_s, shapes)))
    out_v.update(zip(names, _unpack(v_s, shapes)))

    return (loss, dx[None], *[out_g[k] for k in WEIGHTS], *[out_d[k] for k in WEIGHTS],
            *[out_m[k] for k in WEIGHTS], *[out_v[k] for k in WEIGHTS])
```

```python
import math

import numpy as np
import jax
import jax.numpy as jnp
from jax import lax
from jax.experimental import pallas as pl
from jax.experimental.pallas import tpu as pltpu

F32 = jnp.float32
BF16 = jnp.bfloat16

D_MODEL = 1024
DEPTH = 4
HEAD_DIM = 64
SWA_Q_HEADS = 8
SWA_KV_HEADS = 2
SWA_GROUP = SWA_Q_HEADS // SWA_KV_HEADS
WINDOW = 128
FOX_HEADS = 4
MLA_HEADS = 4
MLA_Q_RANK = 256
MLA_KV_RANK = 128
MLA_NOPE = 64
MLA_ROPE = 32
MLA_QK = MLA_NOPE + MLA_ROPE
ROPE_THETA = 10000.0
REL_BUCKETS = 32
REL_MAX_DIST = 128
D_FF = 2816
EPS = 1e-6
NEG_INF = -1e30
LANES = 128
N_CHIPS = 4

IN_COLS = 1956
IN_ROWS = 2048
QKV_ROWS = 1536
LAT_ROWS = IN_ROWS - QKV_ROWS
LAT_SHIFT = FOX_HEADS
FOX_ROW0 = 768
MLA_PAD = LANES
GATE_ROWS = 8

ADAM_LR = 0.001
ADAM_B1 = 0.9
ADAM_B2 = 0.999
ADAM_EPS = 1e-08
ADAM_WD = 0.01
ADAM_STEP = 10

VMEM_LIMIT_BYTES = 48 * 1024 * 1024
ATT_TILE = 512
LOG2E = math.log2(math.e)
MLA_SCALE = MLA_QK ** -0.5
ROW_TILE = 256
MESH = pl.DeviceIdType.MESH

NT = (((1,), (1,)), ((), ()))
TN = (((0,), (0,)), ((), ()))
NN = (((1,), (0,)), ((), ()))


def _params(*sem):
    return pltpu.CompilerParams(dimension_semantics=sem, vmem_limit_bytes=VMEM_LIMIT_BYTES)


def _tile(dim, cap):
    for t in (2816, 2048, 1408, 1024, 512, 256, 128, 64, 32, 16, 8):
        if t <= cap and dim % t == 0:
            return t
    return dim


def _dot(a, b, dims=NN):
    return lax.dot_general(a, b, dims, preferred_element_type=F32)


def _split3(a):
    a1 = a.astype(BF16)
    r1 = a - a1.astype(F32)
    a2 = r1.astype(BF16)
    a3 = (r1 - a2.astype(F32)).astype(BF16)
    return a1, a2, a3


FF_SHARD = 2 * D_FF // N_CHIPS
MATMUL_VMEM_BYTES = 40 * 1024 * 1024


def _matmul(a, b, *, ta=False, tb=False, out_dtype=F32, name, b_shards=False, out_shards=False, a_halves=False,
            b_halves=False):
    if a_halves:
        M, K = a.shape[1], 2 * a.shape[2]
    elif ta:
        K, M = a.shape
    else:
        M, K = a.shape
    if b_halves:
        K2, N = b.shape[1], 2 * b.shape[2]
    elif b_shards:
        K2, N = (2 * D_FF, D_MODEL) if tb else (D_MODEL, 2 * D_FF)
    elif tb:
        N, K2 = b.shape
    else:
        K2, N = b.shape
    assert K == K2, (a.shape, b.shape)
    tm, tn = (M if M <= 2048 else _tile(M, 1408)), _tile(N, 1408)
    tk = FF_SHARD if (b_shards and tb) else _tile(K, 2816)
    out_bytes = jnp.dtype(out_dtype).itemsize
    while 2 * 2 * tk * (tm + tn) + (4 + 2 * out_bytes) * tm * tn > MATMUL_VMEM_BYTES and tk % 256 == 0:
        tk //= 2
    nk = K // tk
    dims = (((0 if ta else 1,), (1 if tb else 0,)), ((), ()))

    def body(a_ref, b_ref, o_ref, acc_ref):
        k = pl.program_id(2)

        @pl.when(k == 0)
        def _():
            acc_ref[...] = jnp.zeros_like(acc_ref)

        acc_ref[...] += lax.dot_general(a_ref[...], b_ref[...], dims, preferred_element_type=F32)

        @pl.when(k == nk - 1)
        def _():
            o_ref[...] = acc_ref[...].astype(o_ref.dtype)

    if a_halves:
        nh = K // 2 // tk
        a_spec = pl.BlockSpec((None, tm, tk), lambda i, j, k: (k // nh, i, k % nh))
    else:
        a_spec = pl.BlockSpec((tk, tm), lambda i, j, k: (k, i)) if ta else pl.BlockSpec((tm, tk), lambda i, j, k: (i, k))
    if b_halves:
        nh = N // 2 // tn
        b_spec = pl.BlockSpec((None, tk, tn), lambda i, j, k: (j // nh, k, j % nh))
    elif b_shards and tb:
        assert tk == FF_SHARD
        b_spec = pl.BlockSpec((None, tn, tk), lambda i, j, k: (k, j, 0))
    elif b_shards:
        assert tn == FF_SHARD
        b_spec = pl.BlockSpec((None, tk, tn), lambda i, j, k: (j, k, 0))
    else:
        b_spec = pl.BlockSpec((tn, tk), lambda i, j, k: (j, k)) if tb else pl.BlockSpec((tk, tn), lambda i, j, k: (k, j))
    if out_shards:
        assert tn == FF_SHARD
        out_spec = pl.BlockSpec((None, tm, tn), lambda i, j, k: (j, i, 0))
        out_shape = jax.ShapeDtypeStruct((N // tn, M, tn), out_dtype)
    else:
        out_spec = pl.BlockSpec((tm, tn), lambda i, j, k: (i, j))
        out_shape = jax.ShapeDtypeStruct((M, N), out_dtype)
    return pl.pallas_call(
        body, name=name, grid=(M // tm, N // tn, nk),
        in_specs=[a_spec, b_spec], out_specs=out_spec, out_shape=out_shape,
        scratch_shapes=[pltpu.VMEM((tm, tn), F32)],
        compiler_params=_params("parallel", "parallel", "arbitrary"),
    )(a, b)


def _seg_rms(xs, g):
    r = lax.rsqrt(jnp.mean(xs * xs, axis=-1, keepdims=True) + EPS)
    return xs * r * g


def _seg_rms_bwd(xs, g, dy):
    r = lax.rsqrt(jnp.mean(xs * xs, axis=-1, keepdims=True) + EPS)
    gd = dy * g
    c = jnp.mean(gd * xs, axis=-1, keepdims=True)
    dx = r * gd - xs * (r * r * r * c)
    dg = jnp.sum(dy * (xs * r), axis=0, keepdims=True)
    return dx, dg


def _rms_fwd(x, g, *, name):
    S, W = x.shape
    tm = _tile(S, 512)

    def body(x_ref, g_ref, o_ref):
        o_ref[...] = _seg_rms(x_ref[...], g_ref[...]).astype(o_ref.dtype)

    return pl.pallas_call(
        body, name=name, grid=(S // tm,),
        in_specs=[pl.BlockSpec((tm, W), lambda i: (i, 0)), pl.BlockSpec((1, W), lambda i: (0, 0))],
        out_specs=pl.BlockSpec((tm, W), lambda i: (i, 0)),
        out_shape=jax.ShapeDtypeStruct((S, W), BF16),
        compiler_params=_params("parallel"),
    )(x, g)


def _rms_bwd(x, g, dy, *, resid=None, out_dtype, name, then=None):
    S, W = x.shape
    tm = _tile(S, 512)
    has_resid = resid is not None
    chained = then is not None

    def body(*refs):
        refs = list(refs)
        x_ref, g_ref, dy_ref = refs[:3]
        r_ref = refs[3] if has_resid else None
        n_in = 3 + has_resid + 2 * chained
        x2_ref, g2_ref = (refs[n_in - 2], refs[n_in - 1]) if chained else (None, None)
        outs = refs[n_in:]
        dx_ref, dg_ref = outs[0], outs[1]

        @pl.when(pl.program_id(0) == 0)
        def _():
            dg_ref[...] = jnp.zeros_like(dg_ref)
            if chained:
                outs[3][...] = jnp.zeros_like(outs[3])

        dx, dg = _seg_rms_bwd(x_ref[...], g_ref[...], dy_ref[...])
        if has_resid:
            dx = dx + r_ref[...]
        dx_ref[...] = dx.astype(dx_ref.dtype)
        dg_ref[...] += dg
        if chained:
            dx2, dg2 = _seg_rms_bwd(x2_ref[...], g2_ref[...], dx)
            outs[2][...] = dx2.astype(BF16)
            outs[3][...] += dg2

    row = pl.BlockSpec((tm, W), lambda i: (i, 0))
    vec = pl.BlockSpec((1, W), lambda i: (0, 0))
    ins = [x, g, dy] + ([resid] if has_resid else []) + (list(then) if chained else [])
    return pl.pallas_call(
        body, name=name, grid=(S // tm,),
        in_specs=[row, vec, row] + ([row] if has_resid else []) + ([row, vec] if chained else []),
        out_specs=[row, vec] + ([row, vec] if chained else []),
        out_shape=[jax.ShapeDtypeStruct((S, W), out_dtype), jax.ShapeDtypeStruct((1, W), F32)]
        + ([jax.ShapeDtypeStruct((S, W), BF16), jax.ShapeDtypeStruct((1, W), F32)] if chained else []),
        compiler_params=_params("arbitrary"),
    )(*ins)


def _resid_rms(x, y, g_post, g_next, *, name):
    S, W = x.shape
    tm = _tile(S, 512)
    with_next = g_next is not None

    def body(*refs):
        if with_next:
            x_ref, y_ref, gp_ref, gn_ref, xo_ref, h_ref = refs
        else:
            x_ref, y_ref, gp_ref, xo_ref = refs
        xn = x_ref[...] + _seg_rms(y_ref[...], gp_ref[...])
        xo_ref[...] = xn
        if with_next:
            h_ref[...] = _seg_rms(xn, gn_ref[...]).astype(BF16)

    row = pl.BlockSpec((tm, W), lambda i: (i, 0))
    vec = pl.BlockSpec((1, W), lambda i: (0, 0))
    outs = [jax.ShapeDtypeStruct((S, W), F32)] + ([jax.ShapeDtypeStruct((S, W), BF16)] if with_next else [])
    res = pl.pallas_call(
        body, name=name, grid=(S // tm,),
        in_specs=[row, row, vec] + ([vec] if with_next else []),
        out_specs=[row] + ([row] if with_next else []),
        out_shape=outs,
        compiler_params=_params("parallel"),
    )(*([x, y, g_post] + ([g_next] if with_next else [])))
    return (res[0], res[1]) if with_next else (res[0], None)


def _col_rms(xs, g):
    r = lax.rsqrt(jnp.mean(xs * xs, axis=0, keepdims=True) + EPS)
    return xs * r * g


def _col_rms_bwd(xs, g, dy):
    r = lax.rsqrt(jnp.mean(xs * xs, axis=0, keepdims=True) + EPS)
    gd = dy * g
    c = jnp.mean(gd * xs, axis=0, keepdims=True)
    dx = r * gd - xs * (r * r * r * c)
    dg = jnp.sum(dy * (xs * r), axis=1, keepdims=True)
    return dx, dg


GROUP_ROWS = (SWA_Q_HEADS * HEAD_DIM, FOX_HEADS * HEAD_DIM, MLA_HEADS * HEAD_DIM)


def _group_specs(S, tn):
    outs = [pl.BlockSpec((n, tn), lambda i: (0, i)) for n in GROUP_ROWS]
    g = pl.BlockSpec((D_MODEL, 1), lambda i: (0, 0))
    mixed = pl.BlockSpec((D_MODEL, tn), lambda i: (0, i))
    return outs, g, mixed


def _group_norm_fwd(oa, of, oc, g, *, name):
    S = oa.shape[1]
    tn = _tile(S, 512)
    outs, gs, mixed = _group_specs(S, tn)

    def body(a_ref, f_ref, c_ref, g_ref, o_ref):
        r0 = 0
        for ref, n in zip((a_ref, f_ref, c_ref), GROUP_ROWS):
            o_ref[r0:r0 + n, :] = _col_rms(ref[...], g_ref[r0:r0 + n, :]).astype(BF16)
            r0 += n

    return pl.pallas_call(
        body, name=name, grid=(S // tn,),
        in_specs=outs + [gs], out_specs=mixed,
        out_shape=jax.ShapeDtypeStruct((D_MODEL, S), BF16),
        compiler_params=_params("parallel"),
    )(oa, of, oc, g)


def _group_norm_bwd(oa, of, oc, g, dmixed, *, name):
    S = oa.shape[1]
    tn = _tile(S, 512)
    outs, gs, mixed = _group_specs(S, tn)
    n_heads = D_MODEL // HEAD_DIM

    def body(a_ref, f_ref, c_ref, g_ref, dm_ref, da_ref, df_ref, dc_ref, dg_ref, dl_ref):
        @pl.when(pl.program_id(0) == 0)
        def _():
            dg_ref[...] = jnp.zeros_like(dg_ref)

        r0 = 0
        for ref, dref, n in zip((a_ref, f_ref, c_ref), (da_ref, df_ref, dc_ref), GROUP_ROWS):
            o = ref[...]
            dx, dg = _col_rms_bwd(o, g_ref[r0:r0 + n, :], dm_ref[r0:r0 + n, :])
            dxb = dx.astype(BF16)
            dref[...] = dxb
            dg_ref[r0:r0 + n, :] += dg
            od = o * dxb.astype(F32)
            for h in range(n // HEAD_DIM):
                dl_ref[r0 // HEAD_DIM + h] = jnp.sum(od[h * HEAD_DIM:(h + 1) * HEAD_DIM, :], axis=0, keepdims=True)
            r0 += n

    return pl.pallas_call(
        body, name=name, grid=(S // tn,),
        in_specs=outs + [gs, mixed], out_specs=outs + [gs, pl.BlockSpec((n_heads, 1, tn), lambda i: (0, 0, i))],
        out_shape=[jax.ShapeDtypeStruct((n, S), BF16) for n in GROUP_ROWS] + [jax.ShapeDtypeStruct((D_MODEL, 1), F32),
                                                                              jax.ShapeDtypeStruct((n_heads, 1, S), F32)],
        compiler_params=_params("arbitrary"),
    )(oa, of, oc, g, dmixed)


def _loss_head(y, target):
    S, W = y.shape
    tm = _tile(S, 512)

    def body(y_ref, t_ref, d_ref, l_ref):
        @pl.when(pl.program_id(0) == 0)
        def _():
            l_ref[...] = jnp.zeros_like(l_ref)

        err = y_ref[...] - t_ref[...]
        d_ref[...] = err * (1.0 / W)
        l_ref[...] += 0.5 * jnp.sum(jnp.mean(err * err, axis=-1, keepdims=True), axis=0, keepdims=True)

    row = pl.BlockSpec((tm, W), lambda i: (i, 0))
    d, l = pl.pallas_call(
        body, name="loss_head", grid=(S // tm,),
        in_specs=[row, row],
        out_specs=[row, pl.BlockSpec((1, 1), lambda i: (0, 0))],
        out_shape=[jax.ShapeDtypeStruct((S, W), F32), jax.ShapeDtypeStruct((1, 1), F32)],
        compiler_params=_params("arbitrary"),
    )(y, target)
    return l[0, 0], d


def _attn_fwd(q_src, k_src, v_src, rows, H, Dk, Dv, scale, f_row=None, f_col=None, *, name, q_scaled=False):
    S = q_src.shape[1]
    T = _tile(S, ATT_TILE)
    nq = S // T
    forget = f_row is not None
    qb, kb, vb = rows[0] // (H * Dk), rows[1] // (H * Dk), rows[2] // (H * Dv)
    hs = range(H)

    def body(*refs):
        if forget:
            q_ref, k_ref, v_ref, fq_ref, fk_ref, o_ref, lse_ref = refs
        else:
            q_ref, k_ref, v_ref, o_ref, lse_ref = refs
        i = pl.program_id(0)

        def tile(j, masked, state):
            off = pl.multiple_of(j * T, T)
            ss = [_dot(k_ref[h * Dk:(h + 1) * Dk, pl.ds(off, T)], q_ref[h * Dk:(h + 1) * Dk, :], TN) for h in hs]
            if not q_scaled:
                ss = [s * (scale * LOG2E) for s in ss]
            if forget:
                ss = [ss[h] + (fq_ref[h] - fk_ref[pl.ds(off, T), h:h + 1]) for h in hs]
            if masked:
                r = lax.broadcasted_iota(jnp.int32, (T, T), 0)
                c = lax.broadcasted_iota(jnp.int32, (T, T), 1)
                ss = [jnp.where(r <= c, s, NEG_INF) for s in ss]
            m_new = [jnp.maximum(state[h][0], jnp.max(ss[h], axis=0, keepdims=True)) for h in hs]
            alpha = [jnp.exp2(state[h][0] - m_new[h]) for h in hs]
            ps = [jnp.exp2(ss[h] - m_new[h]) for h in hs]
            l_new = [alpha[h] * state[h][1] + jnp.sum(ps[h], axis=0, keepdims=True) for h in hs]
            p_hi = [p.astype(BF16) for p in ps]
            vs = [v_ref[h * Dv:(h + 1) * Dv, pl.ds(off, T)] for h in hs]
            pv = [_dot(vs[h], p_hi[h]) for h in hs]
            if forget:
                pv = [pv[h] + _dot(vs[h], (ps[h] - p_hi[h].astype(F32)).astype(BF16)) for h in hs]
            return tuple((m_new[h], l_new[h], alpha[h] * state[h][2] + pv[h]) for h in hs)

        init = tuple((jnp.full((1, T), NEG_INF, F32), jnp.zeros((1, T), F32), jnp.zeros((Dv, T), F32)) for _ in hs)
        state = lax.fori_loop(0, i, lambda j, st: tile(j, False, st), init)
        state = tile(i, True, state)
        for h in hs:
            m, l, acc = state[h]
            o_ref[h * Dv:(h + 1) * Dv, :] = acc / l
            lse_ref[h] = m + jnp.log2(l)

    in_specs = [pl.BlockSpec((H * Dk, T), lambda i: (qb, i)),
                pl.BlockSpec((H * Dk, S), lambda i: (kb, 0)),
                pl.BlockSpec((H * Dv, S), lambda i: (vb, 0))]
    ins = [q_src, k_src, v_src]
    if forget:
        in_specs += [pl.BlockSpec((H, 1, T), lambda i: (0, 0, i)), pl.BlockSpec((S, H), lambda i: (0, 0))]
        ins += [f_row, f_col]
    return pl.pallas_call(
        body, name=name, grid=(nq,),
        in_specs=in_specs,
        out_specs=[pl.BlockSpec((H * Dv, T), lambda i: (0, i)), pl.BlockSpec((H, 1, T), lambda i: (0, 0, i))],
        out_shape=[jax.ShapeDtypeStruct((H * Dv, S), F32), jax.ShapeDtypeStruct((H, 1, S), F32)],
        compiler_params=_params("parallel"),
    )(*ins)


def _attn_bwd(q_src, k_src, v_src, rows, H, Dk, Dv, scale, do, lse, delta, f_row=None, f_col=None, *, name, q_scaled=False):
    S = q_src.shape[1]
    T = _tile(S, ATT_TILE)
    nq = S // T
    forget = f_row is not None
    qb, kb, vb, db = rows[0] // (H * Dk), rows[1] // (H * Dk), rows[2] // (H * Dv), rows[3] // H
    hs = range(H)

    def body(*refs):
        if forget:
            (q_ref, k_ref, v_ref, do_ref, lse_ref, dl_ref, fq_ref, fk_ref,
             dq_ref, dk_ref, dv_ref, df_ref, dk_s, dv_s, df_s) = refs
        else:
            q_ref, k_ref, v_ref, do_ref, lse_ref, dl_ref, dq_ref, dk_ref, dv_ref, dk_s, dv_s = refs
        j = pl.program_id(0)

        @pl.when(j == 0)
        def _():
            dq_ref[...] = jnp.zeros_like(dq_ref)

        dk_s[...] = jnp.zeros_like(dk_s)
        dv_s[...] = jnp.zeros_like(dv_s)
        if forget:
            df_s[...] = jnp.zeros_like(df_s)
        kt = [k_ref[h * Dk:(h + 1) * Dk, :] for h in hs]
        kj = [k.T for k in kt]
        vj = [v_ref[h * Dv:(h + 1) * Dv, :].T for h in hs]
        koff = pl.multiple_of(j * T, T)

        def tile(i, masked):
            cols = pl.ds(pl.multiple_of(i * T, T), T)
            qi = [q_ref[h * Dk:(h + 1) * Dk, cols] for h in hs]
            doi = [do_ref[h * Dv:(h + 1) * Dv, cols] for h in hs]
            st = [_dot(kj[h], qi[h]) for h in hs]
            if not q_scaled:
                st = [x * (scale * LOG2E) for x in st]
            if forget:
                st = [st[h] + (fq_ref[h, :, cols] - fk_ref[pl.ds(koff, T), h:h + 1]) for h in hs]
            if masked:
                r = lax.broadcasted_iota(jnp.int32, (T, T), 0)
                c = lax.broadcasted_iota(jnp.int32, (T, T), 1)
                st = [jnp.where(r <= c, x, NEG_INF) for x in st]
            pt = [jnp.exp2(st[h] - lse_ref[h, :, cols]) for h in hs]
            dpt = [_dot(vj[h], doi[h]) for h in hs]
            dst = [pt[h] * (dpt[h] - dl_ref[h, :, cols]) for h in hs]
            ptb = [p.astype(BF16) for p in pt]
            dsb = [d.astype(BF16) for d in dst]
            for h in hs:
                dv_s[h * Dv:(h + 1) * Dv, :] += _dot(doi[h], ptb[h], NT)
            for h in hs:
                dk_s[h * Dk:(h + 1) * Dk, :] += _dot(qi[h], dsb[h], NT)
            for h in hs:
                dq_ref[h * Dk:(h + 1) * Dk, cols] += _dot(kt[h], dsb[h]) * scale
            if forget:
                for h in hs:
                    part = dst[h][:, 0:LANES]
                    for c0 in range(LANES, T, LANES):
                        part = part + dst[h][:, c0:c0 + LANES]
                    df_s[h] += part

        tile(j, True)

        def loop_body(i, carry):
            tile(i, False)
            return carry

        lax.fori_loop(j + 1, nq, loop_body, 0)
        dk_ref[...] = dk_s[...] * ((1.0 / LOG2E) if q_scaled else scale)
        dv_ref[...] = dv_s[...]
        if forget:
            df_ref[...] = jnp.concatenate([-jnp.sum(df_s[h], axis=-1, keepdims=True) for h in hs], axis=1)

    res = lambda D, b0: pl.BlockSpec((H * D, S), lambda j: (b0, 0))
    blk = lambda D, b0: pl.BlockSpec((H * D, T), lambda j: (b0, j))
    row3 = lambda b0: pl.BlockSpec((H, 1, S), lambda j: (b0, 0, 0))
    in_specs = [res(Dk, qb), blk(Dk, kb), blk(Dv, vb), res(Dv, 0), row3(0), row3(db)]
    ins = [q_src, k_src, v_src, do, lse, delta]
    out_specs = [res(Dk, 0), blk(Dk, 0), blk(Dv, 0)]
    out_shape = [jax.ShapeDtypeStruct((H * Dk, S), F32), jax.ShapeDtypeStruct((H * Dk, S), F32),
                 jax.ShapeDtypeStruct((H * Dv, S), F32)]
    scratch = [pltpu.VMEM((H * Dk, T), F32), pltpu.VMEM((H * Dv, T), F32)]
    if forget:
        in_specs += [row3(0), pl.BlockSpec((S, H), lambda j: (0, 0))]
        ins += [f_row, f_col]
        out_specs.append(pl.BlockSpec((T, H), lambda j: (j, 0)))
        out_shape.append(jax.ShapeDtypeStruct((S, H), F32))
        scratch.append(pltpu.VMEM((H, T, min(T, LANES)), F32))
    return pl.pallas_call(
        body, name=name, grid=(nq,),
        in_specs=in_specs, out_specs=out_specs, out_shape=out_shape, scratch_shapes=scratch,
        compiler_params=_params("arbitrary"),
    )(*ins)


GW = SWA_GROUP * WINDOW


def _swa_masks(i):
    r = lax.broadcasted_iota(jnp.int32, (WINDOW, GW), 0)
    c = lax.broadcasted_iota(jnp.int32, (WINDOW, GW), 1) % WINDOW
    return (r > c) & (i > 0), r <= c


def _swa_specs():
    W = WINDOW
    kv_rows = SWA_KV_HEADS * HEAD_DIM
    q = pl.BlockSpec((SWA_Q_HEADS * HEAD_DIM, W), lambda i: (0, i))
    prev = lambda b: pl.BlockSpec((kv_rows, W), lambda i: (b, jnp.maximum(i - 1, 0)))
    cur = lambda b: pl.BlockSpec((kv_rows, W), lambda i: (b, i))
    bias = pl.BlockSpec((SWA_KV_HEADS, 2 * W, GW), lambda i: (0, 0, 0))
    stat = pl.BlockSpec((SWA_Q_HEADS, W), lambda i: (0, i))
    sink = pl.BlockSpec(memory_space=pltpu.SMEM)
    return q, prev(4), cur(4), prev(5), cur(5), bias, stat, sink


def _group_lanes(ref, g, rows_per_head):
    h0 = g * SWA_GROUP
    return jnp.concatenate([ref[(h0 + j) * rows_per_head:(h0 + j + 1) * rows_per_head, :] for j in range(SWA_GROUP)], axis=1)


def _swa_scores(g, q_ref, kp_ref, kc_ref, b_ref, masks):
    rows = slice(g * HEAD_DIM, (g + 1) * HEAD_DIM)
    qg = _group_lanes(q_ref, g, HEAD_DIM)
    scale = HEAD_DIM ** -0.5
    s_p = jnp.where(masks[0], _dot(kp_ref[rows, :], qg, TN) * scale + b_ref[g, 0:WINDOW, :], NEG_INF)
    s_c = jnp.where(masks[1], _dot(kc_ref[rows, :], qg, TN) * scale + b_ref[g, WINDOW:2 * WINDOW, :], NEG_INF)
    return qg, rows, s_p, s_c


def _sink_row(sink_ref, g):
    return jnp.concatenate([jnp.full((1, WINDOW), sink_ref[g * SWA_GROUP + j], F32) for j in range(SWA_GROUP)], axis=1)


def _swa_fwd(qkv, bias_g, sinks, *, name):
    S = qkv.shape[1]
    qs, kp, kc, vp, vc, bs, stat, sk = _swa_specs()
    gs = range(SWA_KV_HEADS)

    def body(sink_ref, q_ref, kp_ref, kc_ref, vp_ref, vc_ref, b_ref, o_ref, lse_ref):
        masks = _swa_masks(pl.program_id(0))
        sc = [_swa_scores(g, q_ref, kp_ref, kc_ref, b_ref, masks) for g in gs]
        sinks_g = [_sink_row(sink_ref, g) for g in gs]
        m = [jnp.maximum(jnp.maximum(jnp.max(sc[g][2], axis=0, keepdims=True), jnp.max(sc[g][3], axis=0, keepdims=True)),
                         sinks_g[g]) for g in gs]
        p_p = [jnp.exp(sc[g][2] - m[g]) for g in gs]
        p_c = [jnp.exp(sc[g][3] - m[g]) for g in gs]
        l = [jnp.sum(p_p[g], axis=0, keepdims=True) + jnp.sum(p_c[g], axis=0, keepdims=True) + jnp.exp(sinks_g[g] - m[g])
             for g in gs]
        o = [_dot(vp_ref[sc[g][1], :], p_p[g].astype(BF16)) + _dot(vc_ref[sc[g][1], :], p_c[g].astype(BF16)) for g in gs]
        for g in gs:
            og = o[g] / l[g]
            lse = m[g] + jnp.log(l[g])
            for j in range(SWA_GROUP):
                h = g * SWA_GROUP + j
                o_ref[h * HEAD_DIM:(h + 1) * HEAD_DIM, :] = og[:, j * WINDOW:(j + 1) * WINDOW]
                lse_ref[h:h + 1, :] = lse[:, j * WINDOW:(j + 1) * WINDOW]

    return pl.pallas_call(
        body, name=name, grid=(S // WINDOW,),
        in_specs=[sk, qs, kp, kc, vp, vc, bs],
        out_specs=[qs, stat],
        out_shape=[jax.ShapeDtypeStruct((SWA_Q_HEADS * HEAD_DIM, S), F32), jax.ShapeDtypeStruct((SWA_Q_HEADS, S), F32)],
        compiler_params=_params("parallel"),
    )(sinks, qkv, qkv, qkv, qkv, qkv, bias_g)


def _swa_bwd(qkv, bias_g, sinks, do, lse, delta, *, name):
    S = qkv.shape[1]
    W = WINDOW
    qs, kp, kc, vp, vc, bs, stat, sk = _swa_specs()
    scale = HEAD_DIM ** -0.5
    kv_rows = SWA_KV_HEADS * HEAD_DIM
    gs = range(SWA_KV_HEADS)

    def body(sink_ref, q_ref, kp_ref, kc_ref, vp_ref, vc_ref, b_ref, do_ref, lse_ref, dl_ref,
             dq_ref, dkv_ref, db_ref, dsk_ref):
        i = pl.program_id(0)

        @pl.when(i == 0)
        def _():
            dkv_ref[...] = jnp.zeros_like(dkv_ref)
            db_ref[...] = jnp.zeros_like(db_ref)
            dsk_ref[...] = jnp.zeros_like(dsk_ref)

        masks = _swa_masks(i)
        prev = pl.ds(pl.multiple_of(jnp.maximum(i - 1, 0) * W, W), W)
        cur = pl.ds(pl.multiple_of(i * W, W), W)
        sc = [_swa_scores(g, q_ref, kp_ref, kc_ref, b_ref, masks) for g in gs]
        dog = [_group_lanes(do_ref, g, HEAD_DIM) for g in gs]
        lse = [_group_lanes(lse_ref, g, 1) for g in gs]
        dl = [_group_lanes(dl_ref, g, 1) for g in gs]
        p_p = [jnp.exp(sc[g][2] - lse[g]) for g in gs]
        p_c = [jnp.exp(sc[g][3] - lse[g]) for g in gs]
        ds_p = [p_p[g] * (_dot(vp_ref[sc[g][1], :], dog[g], TN) - dl[g]) for g in gs]
        ds_c = [p_c[g] * (_dot(vc_ref[sc[g][1], :], dog[g], TN) - dl[g]) for g in gs]
        for g in gs:
            db_ref[g, 0:W, :] += ds_p[g]
            db_ref[g, W:2 * W, :] += ds_c[g]
            dsk = jnp.exp(_sink_row(sink_ref, g) - lse[g]) * dl[g]
            for j in range(SWA_GROUP):
                h = g * SWA_GROUP + j
                dsk_ref[h:h + 1, :] -= jnp.broadcast_to(jnp.sum(dsk[:, j * W:(j + 1) * W], axis=1, keepdims=True), (1, LANES))
        dsb_p = [d.astype(BF16) for d in ds_p]
        dsb_c = [d.astype(BF16) for d in ds_c]
        for g in gs:
            rows = sc[g][1]
            dq = (_dot(kp_ref[rows, :], dsb_p[g]) + _dot(kc_ref[rows, :], dsb_c[g])) * scale
            for j in range(SWA_GROUP):
                h = g * SWA_GROUP + j
                dq_ref[h * HEAD_DIM:(h + 1) * HEAD_DIM, :] = dq[:, j * W:(j + 1) * W]
        for g in gs:
            rows = sc[g][1]
            vrows = slice(kv_rows + rows.start, kv_rows + rows.stop)
            dkv_ref[rows, prev] += _dot(sc[g][0], dsb_p[g], NT) * scale
            dkv_ref[rows, cur] += _dot(sc[g][0], dsb_c[g], NT) * scale
            dkv_ref[vrows, prev] += _dot(dog[g], p_p[g].astype(BF16), NT)
            dkv_ref[vrows, cur] += _dot(dog[g], p_c[g].astype(BF16), NT)

    return pl.pallas_call(
        body, name=name, grid=(S // W,),
        in_specs=[sk, qs, kp, kc, vp, vc, bs, qs, stat, stat],
        out_specs=[qs, pl.BlockSpec((2 * kv_rows, S), lambda i: (0, 0)), bs, pl.BlockSpec((SWA_Q_HEADS, LANES), lambda i: (0, 0))],
        out_shape=[jax.ShapeDtypeStruct((SWA_Q_HEADS * HEAD_DIM, S), F32), jax.ShapeDtypeStruct((2 * kv_rows, S), F32),
                   jax.ShapeDtypeStruct((SWA_KV_HEADS, 2 * W, GW), F32), jax.ShapeDtypeStruct((SWA_Q_HEADS, LANES), F32)],
        compiler_params=_params("arbitrary"),
    )(sinks, qkv, qkv, qkv, qkv, qkv, bias_g, do, lse, delta)


def _rel_onehot_t():
    qi = jnp.arange(WINDOW, dtype=jnp.int32)[None, :] + WINDOW
    kj = jnp.arange(2 * WINDOW, dtype=jnp.int32)[:, None]
    dist = qi - kj
    max_exact = REL_BUCKETS // 2
    d = jnp.maximum(dist, 0)
    log_ratio = jnp.log(jnp.maximum(d, 1).astype(F32) / max_exact) / math.log(REL_MAX_DIST / max_exact)
    large = jnp.minimum(max_exact + (log_ratio * (REL_BUCKETS - max_exact)).astype(jnp.int32), REL_BUCKETS - 1)
    bucket = jnp.where(d < max_exact, d, large).reshape(-1)
    return (bucket[None, :] == jnp.arange(REL_BUCKETS, dtype=jnp.int32)[:, None]).astype(BF16)


def _bias_table(rel_bias_t, onehot_t):
    Hq, NB = rel_bias_t.shape
    N = onehot_t.shape[1]
    tn = _tile(N, 4096)

    def body(r_ref, oh_ref, o_ref):
        oh = oh_ref[...]
        a1, a2, a3 = _split3(r_ref[...])
        o_ref[...] = _dot(a1, oh) + _dot(a2, oh) + _dot(a3, oh)

    return pl.pallas_call(
        body, name="rel_bias_table", grid=(N // tn,),
        in_specs=[pl.BlockSpec((Hq, NB), lambda j: (0, 0)), pl.BlockSpec((NB, tn), lambda j: (0, j))],
        out_specs=pl.BlockSpec((Hq, tn), lambda j: (0, j)),
        out_shape=jax.ShapeDtypeStruct((Hq, N), F32),
        compiler_params=_params("parallel"),
    )(rel_bias_t, onehot_t)


def _bias_table_bwd(dbias, onehot_t):
    L, Hq, N = dbias.shape
    NB = onehot_t.shape[0]
    tn = _tile(N, 4096)

    def body(d_ref, oh_ref, o_ref):
        @pl.when(pl.program_id(0) == 0)
        def _():
            o_ref[...] = jnp.zeros_like(o_ref)

        d = d_ref[0]
        for l in range(1, L):
            d = d + d_ref[l]
        oh = oh_ref[...]
        a1, a2, a3 = _split3(d)
        o_ref[...] += _dot(a1, oh, NT) + _dot(a2, oh, NT) + _dot(a3, oh, NT)

    return pl.pallas_call(
        body, name="rel_bias_bwd", grid=(N // tn,),
        in_specs=[pl.BlockSpec((L, Hq, tn), lambda j: (0, 0, j)), pl.BlockSpec((NB, tn), lambda j: (0, j))],
        out_specs=pl.BlockSpec((Hq, NB), lambda j: (0, 0)),
        out_shape=jax.ShapeDtypeStruct((Hq, NB), F32),
        compiler_params=_params("arbitrary"),
    )(dbias, onehot_t)


def _gate_fwd(lat, fb_col, *, name):
    S = lat.shape[1]
    tn = _tile(S, 256)

    def body(z_ref, fb_ref, o_ref, carry):
        @pl.when(pl.program_id(0) == 0)
        def _():
            carry[...] = jnp.zeros_like(carry)

        z = z_ref[...] + fb_ref[...]
        lf = jnp.minimum(z, 0.0) - jnp.log1p(jnp.exp(-jnp.abs(z)))
        r = lax.broadcasted_iota(jnp.int32, (tn, tn), 0)
        c = lax.broadcasted_iota(jnp.int32, (tn, tn), 1)
        tri = (r <= c).astype(BF16)
        a1, a2, a3 = _split3(lf)
        cum = _dot(a1, tri) + _dot(a2, tri) + _dot(a3, tri) + carry[:, 0:1]
        o_ref[...] = cum
        carry[...] = jnp.broadcast_to(cum[:, tn - 1:tn], carry.shape)

    return pl.pallas_call(
        body, name=name, grid=(S // tn,),
        in_specs=[pl.BlockSpec((GATE_ROWS, tn), lambda i: (0, i)), pl.BlockSpec((GATE_ROWS, 1), lambda i: (0, 0))],
        out_specs=pl.BlockSpec((GATE_ROWS, tn), lambda i: (0, i)),
        out_shape=jax.ShapeDtypeStruct((GATE_ROWS, S), F32),
        scratch_shapes=[pltpu.VMEM((GATE_ROWS, LANES), F32)],
        compiler_params=_params("arbitrary"),
    )(lat, fb_col)


def _gate_bwd(lat, fb_col, dF, *, name):
    S = lat.shape[1]
    tn = _tile(S, 256)
    nt = S // tn

    def body(z_ref, fb_ref, df_ref, dz_ref, dfb_ref, carry):
        @pl.when(pl.program_id(0) == 0)
        def _():
            carry[...] = jnp.zeros_like(carry)
            dfb_ref[...] = jnp.zeros_like(dfb_ref)

        r = lax.broadcasted_iota(jnp.int32, (tn, tn), 0)
        c = lax.broadcasted_iota(jnp.int32, (tn, tn), 1)
        tri = (r >= c).astype(BF16)
        a1, a2, a3 = _split3(df_ref[...])
        dlf = _dot(a1, tri) + _dot(a2, tri) + _dot(a3, tri) + carry[:, 0:1]
        carry[...] = jnp.broadcast_to(dlf[:, 0:1], carry.shape)
        z = z_ref[...] + fb_ref[...]
        row = lax.broadcasted_iota(jnp.int32, (GATE_ROWS, tn), 0)
        dz = jnp.where(row < FOX_HEADS, dlf / (1.0 + jnp.exp(z)), 0.0)
        dz_ref[...] = dz
        dfb_ref[...] += jnp.sum(dz, axis=1, keepdims=True)

    blk = pl.BlockSpec((GATE_ROWS, tn), lambda i: (0, nt - 1 - i))
    vec = pl.BlockSpec((GATE_ROWS, 1), lambda i: (0, 0))
    return pl.pallas_call(
        body, name=name, grid=(nt,),
        in_specs=[blk, vec, blk], out_specs=[blk, vec],
        out_shape=[jax.ShapeDtypeStruct((GATE_ROWS, S), F32), jax.ShapeDtypeStruct((GATE_ROWS, 1), F32)],
        scratch_shapes=[pltpu.VMEM((GATE_ROWS, LANES), F32)],
        compiler_params=_params("arbitrary"),
    )(lat, fb_col, dF)


def _rope_tables(S):
    pos = jnp.arange(S, dtype=F32)
    inv_freq = ROPE_THETA ** (-(jnp.arange(MLA_ROPE // 2, dtype=F32) * 2.0 / MLA_ROPE))
    ang = pos[:, None] * inv_freq[None, :]
    cos, sin = jnp.cos(ang).T, jnp.sin(ang).T
    z16 = jnp.zeros_like(cos)

    def slab(lo, fill):
        def put(first, second, f):
            return jnp.concatenate([jnp.full((lo, S), f, F32), first, second, jnp.full((LANES - lo - MLA_ROPE, S), f, F32)], axis=0)
        return put(cos, cos, fill), put(-sin, z16, 0.0), put(z16, sin, 0.0)

    tq = tuple(jnp.tile(t, (MLA_HEADS, 1)) for t in slab(MLA_NOPE, 1.0))
    return tq, slab(0, 0.0)


def _rope(x, c, s1, s2):
    n = x.shape[0]
    half = MLA_ROPE // 2
    return x * c + pltpu.roll(x, n - half, 0) * s1 + pltpu.roll(x, half, 0) * s2


def _rope_t(dy, c, s1, s2):
    n = dy.shape[0]
    half = MLA_ROPE // 2
    return dy * c + pltpu.roll(dy * s1, half, 0) + pltpu.roll(dy * s2, n - half, 0)


KR_SLAB0 = MLA_Q_RANK + MLA_KV_RANK


def _mla_prep_fwd(lat, g_q, g_kv, w_uq_t, w_ukv_t, tq, tmisc, *, name):
    S = lat.shape[1]
    tn = _tile(S, 512)
    QW = MLA_HEADS * MLA_PAD

    def body(lat_ref, gq_ref, gkv_ref, wq_ref, wkv_ref, c_ref, s1_ref, s2_ref, cm_ref, s1m_ref, s2m_ref,
             nq_ref, nkv_ref, q_ref, k_ref, v_ref):
        x = pltpu.roll(lat_ref[...], LAT_ROWS - LAT_SHIFT, 0)
        nq = _col_rms(x[0:MLA_Q_RANK, :], gq_ref[...]).astype(BF16)
        nkv = _col_rms(x[MLA_Q_RANK:KR_SLAB0, :], gkv_ref[...]).astype(BF16)
        nq_ref[...] = nq
        nkv_ref[...] = nkv
        q = _rope(_dot(wq_ref[...], nq), c_ref[...], s1_ref[...], s2_ref[...])
        q_ref[...] = (q * (MLA_SCALE * LOG2E)).astype(BF16)
        kv = _dot(wkv_ref[...], nkv).astype(BF16)
        kr = _rope(x[KR_SLAB0:LAT_ROWS, :], cm_ref[...], s1m_ref[...], s2m_ref[...]).astype(BF16)
        for h in range(MLA_HEADS):
            k_ref[h * MLA_PAD:h * MLA_PAD + MLA_NOPE, :] = kv[h * LANES:h * LANES + MLA_NOPE, :]
            k_ref[h * MLA_PAD + MLA_NOPE:(h + 1) * MLA_PAD, :] = kr[0:MLA_PAD - MLA_NOPE, :]
            v_ref[h * HEAD_DIM:(h + 1) * HEAD_DIM, :] = kv[h * LANES + MLA_NOPE:(h + 1) * LANES, :]

    def col(rows):
        return pl.BlockSpec((rows, tn), lambda i: (0, i))

    def full(a):
        return pl.BlockSpec(a.shape, lambda i: (0, 0))

    return pl.pallas_call(
        body, name=name, grid=(S // tn,),
        in_specs=[col(LAT_ROWS), full(g_q), full(g_kv), full(w_uq_t), full(w_ukv_t),
                  col(QW), col(QW), col(QW), col(LANES), col(LANES), col(LANES)],
        out_specs=[col(MLA_Q_RANK), col(MLA_KV_RANK), col(QW), col(QW), col(MLA_HEADS * HEAD_DIM)],
        out_shape=[jax.ShapeDtypeStruct((MLA_Q_RANK, S), BF16), jax.ShapeDtypeStruct((MLA_KV_RANK, S), BF16),
                   jax.ShapeDtypeStruct((QW, S), BF16), jax.ShapeDtypeStruct((QW, S), BF16),
                   jax.ShapeDtypeStruct((MLA_HEADS * HEAD_DIM, S), BF16)],
        compiler_params=_params("parallel"),
    )(lat, g_q, g_kv, w_uq_t, w_ukv_t, *tq, *tmisc)


def _mla_prep_bwd(lat, nq, nkv, g_q, g_kv, w_uq_p, w_ukv, tq, tmisc, dq, dk, dv, dflog, *, name):
    S = lat.shape[1]
    tn = _tile(S, 512)
    QW = MLA_HEADS * MLA_PAD

    def body(lat_ref, nq_ref, nkv_ref, gq_ref, gkv_ref, wq_ref, wkv_ref, c_ref, s1_ref, s2_ref,
             cm_ref, s1m_ref, s2m_ref, dq_ref, dk_ref, dv_ref, dfl_ref,
             dlat_ref, dwq_ref, dwkv_ref, dgq_ref, dgkv_ref, y_s):
        @pl.when(pl.program_id(0) == 0)
        def _():
            dwq_ref[...] = jnp.zeros_like(dwq_ref)
            dwkv_ref[...] = jnp.zeros_like(dwkv_ref)
            dgq_ref[...] = jnp.zeros_like(dgq_ref)
            dgkv_ref[...] = jnp.zeros_like(dgkv_ref)

        x = pltpu.roll(lat_ref[...], LAT_ROWS - LAT_SHIFT, 0)
        dqm = _rope_t(dq_ref[...], c_ref[...], s1_ref[...], s2_ref[...]).astype(BF16)
        dwq_ref[...] += _dot(dqm, nq_ref[...], NT)
        dx, dg = _col_rms_bwd(x[0:MLA_Q_RANK, :], gq_ref[...], _dot(wq_ref[...], dqm))
        y_s[0:MLA_Q_RANK, :] = dx
        dgq_ref[...] += dg
        dkv = jnp.concatenate(
            [part for h in range(MLA_HEADS)
             for part in (dk_ref[h * MLA_PAD:h * MLA_PAD + MLA_NOPE, :], dv_ref[h * HEAD_DIM:(h + 1) * HEAD_DIM, :])],
            axis=0).astype(BF16)
        dwkv_ref[...] += _dot(dkv, nkv_ref[...], NT)
        dx, dg = _col_rms_bwd(x[MLA_Q_RANK:KR_SLAB0, :], gkv_ref[...], _dot(wkv_ref[...], dkv))
        y_s[MLA_Q_RANK:KR_SLAB0, :] = dx
        dgkv_ref[...] += dg
        dkr = dk_ref[MLA_NOPE:MLA_PAD, :]
        for h in range(1, MLA_HEADS):
            dkr = dkr + dk_ref[h * MLA_PAD + MLA_NOPE:(h + 1) * MLA_PAD, :]
        dkr = jnp.concatenate([dkr, jnp.zeros((MLA_NOPE, tn), F32)], axis=0)
        y_s[KR_SLAB0:LAT_ROWS, :] = _rope_t(dkr, cm_ref[...], s1m_ref[...], s2m_ref[...])
        y = pltpu.roll(y_s[...], LAT_SHIFT, 0)
        row = lax.broadcasted_iota(jnp.int32, (LAT_ROWS, tn), 0)
        dfl = jnp.concatenate([dfl_ref[...], jnp.zeros((LAT_ROWS - GATE_ROWS, tn), F32)], axis=0)
        dlat_ref[...] = jnp.where(row < LAT_SHIFT, dfl, y).astype(BF16)

    def col(rows):
        return pl.BlockSpec((rows, tn), lambda i: (0, i))

    def full(a):
        return pl.BlockSpec(a.shape, lambda i: (0, 0))

    def acc(r, c):
        return pl.BlockSpec((r, c), lambda i: (0, 0))

    return pl.pallas_call(
        body, name=name, grid=(S // tn,),
        in_specs=[col(LAT_ROWS), col(MLA_Q_RANK), col(MLA_KV_RANK), full(g_q), full(g_kv),
                  full(w_uq_p), full(w_ukv), col(QW), col(QW), col(QW), col(LANES), col(LANES), col(LANES),
                  col(QW), col(QW), col(MLA_HEADS * HEAD_DIM), col(GATE_ROWS)],
        out_specs=[col(LAT_ROWS), acc(QW, MLA_Q_RANK), acc(QW, MLA_KV_RANK), acc(MLA_Q_RANK, 1), acc(MLA_KV_RANK, 1)],
        out_shape=[jax.ShapeDtypeStruct((LAT_ROWS, S), BF16), jax.ShapeDtypeStruct((QW, MLA_Q_RANK), F32),
                   jax.ShapeDtypeStruct((QW, MLA_KV_RANK), F32), jax.ShapeDtypeStruct((MLA_Q_RANK, 1), F32),
                   jax.ShapeDtypeStruct((MLA_KV_RANK, 1), F32)],
        scratch_shapes=[pltpu.VMEM((LAT_ROWS, tn), F32)],
        compiler_params=_params("arbitrary"),
    )(lat, nq, nkv, g_q, g_kv, w_uq_p, w_ukv, *tq, *tmisc, dq, dk, dv, dflog)


def _dproj_cast(dqa, dkva, dqf, dkf, dvf, dlat, *, name):
    S = dqa.shape[1]
    tn = _tile(S, 512)
    parts = (dqa, dkva, dqf, dkf, dvf, dlat)

    def body(*refs):
        o_ref = refs[-1]
        r0 = 0
        for ref in refs[:-1]:
            n = ref.shape[0]
            o_ref[r0:r0 + n, :] = ref[...].astype(BF16)
            r0 += n

    return pl.pallas_call(
        body, name=name, grid=(S // tn,),
        in_specs=[pl.BlockSpec((p.shape[0], tn), lambda i: (0, i)) for p in parts],
        out_specs=pl.BlockSpec((IN_ROWS, tn), lambda i: (0, i)),
        out_shape=jax.ShapeDtypeStruct((IN_ROWS, S), BF16),
        compiler_params=_params("parallel"),
    )(*parts)


GELU_C = math.sqrt(2.0 / math.pi)
GELU_A = 0.044715


HALO = 16


def _shift_down(a, k, fill):
    r = pltpu.roll(a, k, 0)
    row = lax.broadcasted_iota(jnp.int32, (8, a.shape[1]), 0)
    head = r[0:8, :]
    for i in range(k):
        head = jnp.where(row == i, fill[len(fill) - k + i], head)
    return jnp.concatenate([head, r[8:, :]], axis=0)


def _shift_up(d, k, fill):
    n = d.shape[0]
    r = pltpu.roll(d, n - k, 0)
    row = lax.broadcasted_iota(jnp.int32, (8, d.shape[1]), 0)
    tail = r[n - 8:n, :]
    for i in range(k):
        tail = jnp.where(row == 8 - k + i, fill[i], tail)
    return jnp.concatenate([r[0:n - 8, :], tail], axis=0)


def _conv_taps(a, before, w_ref, b_ref):
    a1 = _shift_down(a, 1, before)
    a2 = _shift_down(a, 2, before)
    return ((b_ref[...] + w_ref[0:1, :] * a2) + w_ref[1:2, :] * a1) + w_ref[2:3, :] * a


def _rows_before(halo_ref, first):
    h = halo_ref[HALO - 2:HALO, :].astype(F32)
    return jnp.where(first, 0.0, h[0:1, :]), jnp.where(first, 0.0, h[1:2, :])


def _conv_specs(S, tm, tc, nc):
    hb = tm // HALO
    main = lambda off: pl.BlockSpec((tm, tc), lambda j, i: (i, j + off))
    prev = lambda off: pl.BlockSpec((HALO, tc), lambda j, i: (jnp.maximum(i * hb - 1, 0), j + off))
    wspec = lambda off: pl.BlockSpec((3, tc), lambda j, i: (0, j + off))
    bspec = lambda off: pl.BlockSpec((1, tc), lambda j, i: (0, j + off))
    return main, prev, wspec, bspec


def _conv_geglu_fwd(a, conv_w, conv_b, *, name):
    S = a.shape[0]
    tm, tc = _tile(S, 512), _tile(D_FF, 1408)
    nc = D_FF // tc
    main, prev, wspec, bspec = _conv_specs(S, tm, tc, nc)

    def body(ag_ref, au_ref, hg_ref, hu_ref, wg_ref, wu_ref, bg_ref, bu_ref, u_ref, z_ref):
        first = pl.program_id(1) == 0
        gate = _conv_taps(ag_ref[...].astype(F32), _rows_before(hg_ref, first), wg_ref, bg_ref)
        up = _conv_taps(au_ref[...].astype(F32), _rows_before(hu_ref, first), wu_ref, bu_ref)
        u_ref[0] = gate
        u_ref[1] = up
        cdf = 0.5 * (1.0 + jnp.tanh(GELU_C * (gate + GELU_A * (gate * gate * gate))))
        z_ref[...] = (gate * cdf * up).astype(BF16)

    return pl.pallas_call(
        body, name=name, grid=(nc, S // tm),
        in_specs=[main(0), main(nc), prev(0), prev(nc), wspec(0), wspec(nc), bspec(0), bspec(nc)],
        out_specs=[pl.BlockSpec((2, tm, tc), lambda j, i: (0, i, j)), pl.BlockSpec((tm, tc), lambda j, i: (i, j))],
        out_shape=[jax.ShapeDtypeStruct((2, S, D_FF), F32), jax.ShapeDtypeStruct((S, D_FF), BF16)],
        compiler_params=_params("parallel", "arbitrary"),
    )(a, a, a, a, conv_w, conv_w, conv_b, conv_b)


def _geglu_bwd(gate, up, dz):
    g2x = gate * gate
    th = jnp.tanh(GELU_C * (gate + GELU_A * (g2x * gate)))
    cdf = 0.5 * (1.0 + th)
    dgelu = cdf + gate * (0.5 * (1.0 - th * th) * (GELU_C * (1.0 + 3.0 * GELU_A * g2x)))
    return dz * up * dgelu, dz * (gate * cdf)


def _conv_geglu_bwd(a, u, conv_w, dz, *, name):
    S = a.shape[0]
    tm, tc = _tile(S, 512), _tile(D_FF, 1408)
    nc = D_FF // tc
    nr = S // tm
    main, _, wspec, _ = _conv_specs(S, tm, tc, nc)
    hb = tm // 8

    def body(ag_ref, au_ref, u_ref, un_ref, wg_ref, wu_ref, dz_ref, dzn_ref, da_ref, dw_ref, db_ref):
        i = pl.program_id(1)
        last = i == nr - 1

        @pl.when(i == 0)
        def _():
            dw_ref[...] = jnp.zeros_like(dw_ref)
            db_ref[...] = jnp.zeros_like(db_ref)

        dus = _geglu_bwd(u_ref[0], u_ref[1], dz_ref[...])
        dus_n = _geglu_bwd(un_ref[0], un_ref[1], dzn_ref[...])
        for half, a_ref, w_ref in ((0, ag_ref, wg_ref), (1, au_ref, wu_ref)):
            du, du_n = dus[half], dus_n[half]
            after = (jnp.where(last, 0.0, du_n[0:1, :]), jnp.where(last, 0.0, du_n[1:2, :]))
            shifted = (_shift_up(du, 2, after), _shift_up(du, 1, after), du)
            da_ref[half] = (w_ref[2:3, :] * du + w_ref[1:2, :] * shifted[1] + w_ref[0:1, :] * shifted[0]).astype(BF16)
            af = a_ref[...].astype(F32)
            for tap in range(3):
                dw_ref[half, tap:tap + 1, :] += jnp.sum(shifted[tap] * af, axis=0, keepdims=True)
            db_ref[half] += jnp.sum(du, axis=0, keepdims=True)

    nxt8 = lambda j, i: (0, jnp.minimum((i + 1) * hb, S // 8 - 1), j)
    return pl.pallas_call(
        body, name=name, grid=(nc, nr),
        in_specs=[main(0), main(nc), pl.BlockSpec((2, tm, tc), lambda j, i: (0, i, j)), pl.BlockSpec((2, 8, tc), nxt8),
                  wspec(0), wspec(nc), pl.BlockSpec((tm, tc), lambda j, i: (i, j)),
                  pl.BlockSpec((8, tc), lambda j, i: (jnp.minimum((i + 1) * hb, S // 8 - 1), j))],
        out_specs=[pl.BlockSpec((2, tm, tc), lambda j, i: (0, i, j)), pl.BlockSpec((2, 3, tc), lambda j, i: (0, 0, j)),
                   pl.BlockSpec((2, 1, tc), lambda j, i: (0, 0, j))],
        out_shape=[jax.ShapeDtypeStruct((2, S, D_FF), BF16), jax.ShapeDtypeStruct((2, 3, D_FF), F32),
                   jax.ShapeDtypeStruct((2, 1, D_FF), F32)],
        compiler_params=_params("parallel", "arbitrary"),
    )(a, a, u, u, conv_w, conv_w, dz, dz)


ROW_BLOCK_BYTES = 1536 * 1024


def _row_tile(rows, cols):
    return rows if rows * cols * 4 <= ROW_BLOCK_BYTES else _tile(rows, ROW_TILE)


def _adamw_update(w, g, m, v):
    m = ADAM_B1 * m + (1.0 - ADAM_B1) * g
    v = ADAM_B2 * v + (1.0 - ADAM_B2) * jnp.square(g)
    m_hat = m / (1.0 - ADAM_B1 ** ADAM_STEP)
    v_hat = v / (1.0 - ADAM_B2 ** ADAM_STEP)
    return -ADAM_LR * (m_hat / (jnp.sqrt(v_hat) + ADAM_EPS) + ADAM_WD * w), m, v


def _adamw(w, g, m, v, *, name):
    L, A, B = w.shape
    ta = _tile(A, ROW_TILE)

    def body(w_ref, g_ref, m_ref, v_ref, d_ref, mo_ref, vo_ref):
        d_ref[...], mo_ref[...], vo_ref[...] = _adamw_update(w_ref[...], g_ref[...], m_ref[...], v_ref[...])

    blk = pl.BlockSpec((None, ta, B), lambda l, i: (l, i, 0))
    shp = jax.ShapeDtypeStruct((L, A, B), F32)
    return pl.pallas_call(
        body, name=name, grid=(L, A // ta),
        in_specs=[blk] * 4, out_specs=[blk] * 3, out_shape=[shp] * 3,
        compiler_params=_params("parallel", "parallel"),
    )(w, g, m, v)


def _scalar(v):
    return jnp.reshape(v, (1,)).astype(jnp.int32)


def _adamw_halves(w, g_mine, g_other, m, v, *, name):
    L, A, B = w.shape
    ta = _row_tile(A // 2, B)
    nb = A // 2 // ta

    def body(c_ref, w_ref, gm_ref, go_ref, m_ref, v_ref, g_ref, d_ref, mo_ref, vo_ref):
        g = jnp.where(pl.program_id(1) // nb == c_ref[0], gm_ref[...], go_ref[...])
        g_ref[...] = g
        d_ref[...], mo_ref[...], vo_ref[...] = _adamw_update(w_ref[...], g, m_ref[...], v_ref[...])

    blk = pl.BlockSpec((None, ta, B), lambda l, i, c_ref: (l, i, 0))
    half = pl.BlockSpec((None, ta, B), lambda l, i, c_ref: (l, i % nb, 0))
    shp = jax.ShapeDtypeStruct((L, A, B), F32)
    return pl.pallas_call(
        body, name=name,
        grid_spec=pltpu.PrefetchScalarGridSpec(num_scalar_prefetch=1, grid=(L, A // ta),
                                               in_specs=[blk, half, half, blk, blk], out_specs=[blk] * 4),
        out_shape=[shp] * 4,
        compiler_params=_params("parallel", "parallel"),
    )(_scalar(lax.axis_index("c")), w, g_mine, g_other, m, v)


def _chip_index():
    return 2 * lax.axis_index("x") + lax.axis_index("y")


def _pair_sum(g, recv, *, name):
    n, A, B = g.shape
    ta = _row_tile(A // 2, B)
    nb = A // 2 // ta

    def body(c_ref, g_ref, r_ref, o_ref):
        o_ref[...] = g_ref[...] + r_ref[...]

    return pl.pallas_call(
        body, name=name,
        grid_spec=pltpu.PrefetchScalarGridSpec(
            num_scalar_prefetch=1, grid=(n, nb),
            in_specs=[pl.BlockSpec((None, ta, B), lambda s, r, c_ref: (s, c_ref[0] * nb + r, 0)),
                      pl.BlockSpec((None, ta, B), lambda s, r, c_ref: (s, r, 0))],
            out_specs=pl.BlockSpec((None, ta, B), lambda s, r, c_ref: (s, r, 0))),
        out_shape=jax.ShapeDtypeStruct((n, A // 2, B), F32),
        compiler_params=_params("parallel", "parallel"),
    )(_scalar(lax.axis_index("c")), g, recv)


def _chip_sum(landed, own, *, name):
    n, A2, B = landed.shape
    ta = _row_tile(A2, B)

    def body(me_ref, *refs):
        slots, own_ref, o_ref = refs[:n], refs[n], refs[n + 1]
        parts = [jnp.where(me_ref[0] == s, own_ref[...], slots[s][...]) for s in range(n)]
        o_ref[...] = ((parts[0] + parts[1]) + parts[2]) + parts[3]

    def slot(s):
        return pl.BlockSpec((None, ta, B), lambda r, me_ref: (jnp.where(me_ref[0] == s, (s + 1) % n, s), r, 0))

    return pl.pallas_call(
        body, name=name,
        grid_spec=pltpu.PrefetchScalarGridSpec(
            num_scalar_prefetch=1, grid=(A2 // ta,),
            in_specs=[slot(s) for s in range(n)] + [pl.BlockSpec((None, ta, B), lambda r, me_ref: (me_ref[0], r, 0))],
            out_specs=pl.BlockSpec((ta, B), lambda r, me_ref: (r, 0))),
        out_shape=jax.ShapeDtypeStruct((A2, B), F32),
        compiler_params=_params("parallel"),
    )(_scalar(_chip_index()), *([landed] * n), own)


HBM_SPEC = pl.BlockSpec(memory_space=pl.ANY)
COMM_PARAMS = pltpu.CompilerParams(has_side_effects=True)


def _mesh_pos():
    return lax.axis_index("x"), lax.axis_index("y"), lax.axis_index("c")


def _other_chips(x, y):
    return [(1 - x, y), (x, 1 - y), (1 - x, 1 - y)]


def _remote(src, dst, send_sems, recv_sems, k, to):
    return pltpu.make_async_remote_copy(src_ref=src, dst_ref=dst, send_sem=send_sems.at[k], recv_sem=recv_sems.at[k],
                                        device_id=to, device_id_type=MESH)


def _place_own(gathered, shards, *, name):
    n = len(shards)

    def body(me_ref, *refs):
        for s_ref, o_ref in zip(refs[:n], refs[2 * n:]):
            o_ref[...] = s_ref[...]

    return pl.pallas_call(
        body, name=name,
        grid_spec=pltpu.PrefetchScalarGridSpec(
            num_scalar_prefetch=1, grid=(1,),
            in_specs=[pl.BlockSpec(s.shape, lambda i, me_ref: (0, 0)) for s in shards] + [HBM_SPEC] * n,
            out_specs=[pl.BlockSpec((None,) + s.shape, lambda i, me_ref: (me_ref[0], 0, 0)) for s in shards]),
        out_shape=[jax.ShapeDtypeStruct(g.shape, g.dtype) for g in gathered],
        input_output_aliases={1 + n + k: k for k in range(n)},
        compiler_params=_params("arbitrary"),
    )(_scalar(_chip_index()), *shards, *gathered)


def _half_rows(rows, c, align=8):
    assert (rows // 2) % align == 0
    return pl.ds(pl.multiple_of(c * (rows // 2), align), rows // 2)


BF16_ROWS = 16


def _halved(rows):
    return rows % (2 * BF16_ROWS) == 0


def _gather_copies(srcs, lands, send_sems, recv_sems):
    x, y, c = _mesh_pos()
    me = 2 * x + y
    out = []
    for k in range(len(srcs)):
        a = srcs[k].shape[0]
        rows = _half_rows(a, c, BF16_ROWS) if _halved(a) else pl.ds(0, a)
        for j, (px, py) in enumerate(_other_chips(x, y)):
            send = _remote(srcs[k].at[rows], lands[k].at[me, rows], send_sems, recv_sems, 3 * k + j, (px, py, c))
            recv = _remote(srcs[k].at[rows], lands[k].at[2 * px + py, rows], send_sems, recv_sems, 3 * k + j, (px, py, c))
            out.append((send, recv))
    return out


def _gather_start(srcs):
    nu = len(srcs)
    sizes = [len(su) for su in srcs]
    offs = [2 * sum(sizes[:u]) for u in range(nu + 1)]
    lands = [[lax.empty((N_CHIPS,) + s.shape, s.dtype) for s in su] for su in srcs]
    flat = [a for u in range(nu) for a in srcs[u] + lands[u]]

    def body(*refs):
        bufs, sems, token = refs[:len(flat)], refs[len(flat):len(flat) + 2 * nu], refs[-1]
        for u, n in enumerate(sizes):
            mine = bufs[offs[u]:offs[u + 1]]
            for send, _ in _gather_copies(mine[:n], mine[n:], sems[2 * u], sems[2 * u + 1]):
                send.start()
        token[...] = jnp.zeros_like(token)

    res = pl.pallas_call(
        body, name="weight_gather_start",
        in_specs=[HBM_ONLY] * len(flat),
        out_specs=[SEM_SPEC] * (2 * nu) + [HBM_ONLY] * len(flat) + [pl.BlockSpec(memory_space=pltpu.VMEM)],
        out_shape=[pltpu.SemaphoreType.DMA((3 * n,)) for n in sizes for _ in (0, 1)] + [pltpu.HBM(a.shape, a.dtype) for a in flat]
        + [jax.ShapeDtypeStruct((8, LANES), F32)],
        input_output_aliases={i: 2 * nu + i for i in range(len(flat))},
        compiler_params=SPLIT_PARAMS,
    )(*[pltpu.with_memory_space_constraint(a, pltpu.HBM) for a in flat])
    bufs = res[2 * nu:2 * nu + len(flat)]
    state = [(res[2 * u], res[2 * u + 1], list(bufs[offs[u]:offs[u] + n]), list(bufs[offs[u] + n:offs[u + 1]]))
             for u, n in enumerate(sizes)]
    return state, res[-1][0:1, 0:1]


def _gather_wait(state, after, *, name):
    send_sems, recv_sems, srcs, lands = state
    n = len(srcs)

    def body(*refs):
        for send, recv in _gather_copies(refs[:n], refs[n:2 * n], refs[2 * n], refs[2 * n + 1]):
            send.wait_send()
            recv.wait_recv()

    res = pl.pallas_call(
        body, name=name,
        in_specs=[HBM_ONLY] * (2 * n) + [SEM_SPEC, SEM_SPEC, HBM_SPEC],
        out_specs=[HBM_ONLY] * (2 * n),
        out_shape=[pltpu.HBM(a.shape, a.dtype) for a in srcs + lands],
        input_output_aliases={i: i for i in range(2 * n)},
        compiler_params=SPLIT_PARAMS,
    )(*srcs, *lands, send_sems, recv_sems, after)
    return list(res[:n]), list(res[n:])


def _gather_forward(lands, *, name):
    n = len(lands)

    def body(*refs):
        bufs, outs = refs[:n], refs[n:2 * n]
        send_sems, recv_sems = refs[2 * n:]
        x, y, c = _mesh_pos()
        copies, waits = [], []
        for k in range(n):
            a = lands[k].shape[1]
            if not _halved(a):
                continue
            for j, (px, py) in enumerate(_other_chips(x, y)):
                mine = 2 * px + py, _half_rows(a, c, BF16_ROWS)
                copies.append(_remote(bufs[k].at[mine], outs[k].at[mine], send_sems, recv_sems, 3 * k + j, (x, y, 1 - c)))
                lands_here = outs[k].at[2 * px + py, _half_rows(a, 1 - c, BF16_ROWS)]
                waits.append(_remote(lands_here, lands_here, send_sems, recv_sems, 3 * k + j, (x, y, 1 - c)))
        for cp in copies:
            cp.start()
        for cp in waits:
            cp.wait_recv()
        for cp in copies:
            cp.wait_send()

    return pl.pallas_call(
        body, name=name,
        in_specs=[HBM_SPEC] * n, out_specs=[HBM_SPEC] * n,
        out_shape=[jax.ShapeDtypeStruct(a.shape, a.dtype) for a in lands],
        scratch_shapes=[pltpu.SemaphoreType.DMA((3 * n,)), pltpu.SemaphoreType.DMA((3 * n,))],
        input_output_aliases={i: i for i in range(n)},
        compiler_params=COMM_PARAMS,
    )(*lands)


def _sibling_exchange(gs, *, name):
    n = len(gs)

    def body(*refs):
        ins, outs = refs[:n], refs[n:2 * n]
        send_sems, recv_sems = refs[2 * n:]
        x, y, c = _mesh_pos()
        copies = [_remote(ins[k].at[:, _half_rows(gs[k].shape[1], 1 - c)], outs[k], send_sems, recv_sems, k, (x, y, 1 - c))
                  for k in range(n)]
        for cp in copies:
            cp.start()
        for cp in copies:
            cp.wait()

    return pl.pallas_call(
        body, name=name,
        in_specs=[HBM_SPEC] * n, out_specs=[HBM_SPEC] * n,
        out_shape=[jax.ShapeDtypeStruct((g.shape[0], g.shape[1] // 2, g.shape[2]), g.dtype) for g in gs],
        scratch_shapes=[pltpu.SemaphoreType.DMA((n,)), pltpu.SemaphoreType.DMA((n,))],
        compiler_params=COMM_PARAMS,
    )(*gs)


HBM_ONLY = pl.BlockSpec(memory_space=pltpu.HBM)
SEM_SPEC = pl.BlockSpec(memory_space=pltpu.SEMAPHORE)
SPLIT_PARAMS = pltpu.CompilerParams(has_side_effects=pltpu.SideEffectType.DATAFLOW_SIDE_EFFECTING)


def _scatter_copies(srcs, lands, send_sems, recv_sems):
    x, y, c = _mesh_pos()
    me = 2 * x + y
    out = []
    for k in range(len(srcs)):
        for j, (px, py) in enumerate(_other_chips(x, y)):
            s = 2 * px + py
            send = _remote(srcs[k].at[s], lands[k].at[me], send_sems, recv_sems, 3 * k + j, (px, py, c))
            recv = _remote(srcs[k].at[s], lands[k].at[s], send_sems, recv_sems, 3 * k + j, (px, py, c))
            out.append((send, recv))
    return out


def _exchange_copies(srcs, lands, send_sems, recv_sems):
    x, y, c = _mesh_pos()
    out = []
    for k in range(len(srcs)):
        cp = _remote(srcs[k].at[:, _half_rows(srcs[k].shape[1], 1 - c)], lands[k], send_sems, recv_sems, k, (x, y, 1 - c))
        out.append((cp, cp))
    return out


def _split_start(srcs, land_shapes, copies, n_sems, *, name):
    n = len(srcs)
    lands = [lax.empty(shape, s.dtype) for shape, s in zip(land_shapes, srcs)]

    def body(*refs):
        ins, zones = refs[:n], refs[n:2 * n]
        send_sems, recv_sems, token = refs[2 * n], refs[2 * n + 1], refs[-1]
        for send, _ in copies(ins, zones, send_sems, recv_sems):
            send.start()
        token[...] = jnp.zeros_like(token)

    hbm = lambda a: pltpu.HBM(a.shape, a.dtype)
    res = pl.pallas_call(
        body, name=name,
        in_specs=[HBM_ONLY] * (2 * n),
        out_specs=[SEM_SPEC, SEM_SPEC] + [HBM_ONLY] * (2 * n) + [pl.BlockSpec(memory_space=pltpu.VMEM)],
        out_shape=[pltpu.SemaphoreType.DMA((n_sems,)), pltpu.SemaphoreType.DMA((n_sems,))] + [hbm(a) for a in srcs + lands]
        + [jax.ShapeDtypeStruct((8, LANES), F32)],
        input_output_aliases={i: 2 + i for i in range(2 * n)},
        compiler_params=SPLIT_PARAMS,
    )(*[pltpu.with_memory_space_constraint(a, pltpu.HBM) for a in srcs + lands])
    return (res[0], res[1], list(res[2:2 + n]), list(res[2 + n:2 + 2 * n])), res[-1][0:1, 0:1]


def _scatter_start(ps, *, name):
    return _split_start(ps, [p.shape for p in ps], _scatter_copies, 3 * len(ps), name=name)


def _exchange_start(gs, *, name):
    return _split_start(gs, [(g.shape[0], g.shape[1] // 2, g.shape[2]) for g in gs], _exchange_copies, len(gs), name=name)


def _split_wait(started, copies, after, *, name):
    ng = len(started)
    sizes = [len(st[2]) for st in started]
    offs = [2 * sum(sizes[:i]) for i in range(ng + 1)]
    flat = [a for (_, _, ps, lands) in started for a in ps + lands]

    def body(*refs):
        bufs, sems = refs[:len(flat)], refs[len(flat):len(flat) + 2 * ng]
        for i, n in enumerate(sizes):
            srcs, zones = bufs[offs[i]:offs[i] + n], bufs[offs[i] + n:offs[i + 1]]
            for send, recv in copies(srcs, zones, sems[2 * i], sems[2 * i + 1]):
                send.wait_send()
                recv.wait_recv()

    res = pl.pallas_call(
        body, name=name,
        in_specs=[HBM_ONLY] * len(flat) + [SEM_SPEC] * (2 * ng) + [HBM_SPEC],
        out_specs=[HBM_ONLY] * len(flat),
        out_shape=[pltpu.HBM(a.shape, a.dtype) for a in flat],
        input_output_aliases={i: i for i in range(len(flat))},
        compiler_params=SPLIT_PARAMS,
    )(*flat, *[s for (ss, rs, _, _) in started for s in (ss, rs)], after)
    return [(list(res[offs[i]:offs[i] + n]), list(res[offs[i] + n:offs[i + 1]])) for i, n in enumerate(sizes)]


def _sibling_share(hs):
    n = len(hs)

    def body(*refs):
        ins, outs = refs[:n], refs[n:2 * n]
        send_sems, recv_sems = refs[2 * n:]
        x, y, c = _mesh_pos()
        copies = [_remote(ins[k], outs[k], send_sems, recv_sems, k, (x, y, 1 - c)) for k in range(n)]
        for cp in copies:
            cp.start()
        for cp in copies:
            cp.wait()

    return pl.pallas_call(
        body, name="grad_sibling_share",
        in_specs=[HBM_SPEC] * n, out_specs=[HBM_SPEC] * n,
        out_shape=[jax.ShapeDtypeStruct(h.shape, h.dtype) for h in hs],
        scratch_shapes=[pltpu.SemaphoreType.DMA((n,)), pltpu.SemaphoreType.DMA((n,))],
        compiler_params=COMM_PARAMS,
    )(*hs)


def _allreduce_small(part):
    rows, C = part.shape

    def body(p_ref, o_ref, slots, send_sems, recv_sems):
        x, y, c = _mesh_pos()
        me = 4 * x + 2 * y + c
        slots[me] = p_ref[...]
        copies = []
        for k in range(1, 8):
            kx, ky, kc = (k >> 2) & 1, (k >> 1) & 1, k & 1
            peer = (x ^ kx if kx else x, y ^ ky if ky else y, c ^ kc if kc else c)
            cp = _remote(p_ref, slots.at[me], send_sems, recv_sems, k - 1, peer)
            cp.start()
            copies.append((cp, peer))
        for k, (cp, peer) in enumerate(copies):
            src = 4 * peer[0] + 2 * peer[1] + peer[2]
            _remote(p_ref, slots.at[src], send_sems, recv_sems, k, peer).wait_recv()
        for cp, _ in copies:
            cp.wait_send()
        total = slots[0]
        for d in range(1, 8):
            total = total + slots[d]
        o_ref[...] = total

    return pl.pallas_call(
        body, name="small_grad_allreduce",
        in_specs=[pl.BlockSpec(memory_space=pltpu.VMEM)], out_specs=pl.BlockSpec(memory_space=pltpu.VMEM),
        out_shape=jax.ShapeDtypeStruct((rows, C), F32),
        scratch_shapes=[pltpu.VMEM((8, rows, C), F32), pltpu.SemaphoreType.DMA((7,)), pltpu.SemaphoreType.DMA((7,))],
        compiler_params=pltpu.CompilerParams(has_side_effects=True, vmem_limit_bytes=VMEM_LIMIT_BYTES),
    )(part)


def _pad_w_uq(w):
    lead = w.shape[:-1]
    w = w.reshape(lead + (MLA_HEADS, MLA_QK))
    w = jnp.concatenate([w, jnp.zeros(lead + (MLA_HEADS, MLA_PAD - MLA_QK), w.dtype)], axis=-1)
    return w.reshape(lead + (MLA_HEADS * MLA_PAD,))


def _unpad_w_uq(g):
    lead = g.shape[:-1]
    return g.reshape(lead + (MLA_HEADS, MLA_PAD))[..., :MLA_QK].reshape(lead + (MLA_HEADS * MLA_QK,))


def _t(a):
    return jnp.swapaxes(a, -1, -2)


def _shards_of_cols(w):
    A, NB = w.shape
    return w.reshape(A, N_CHIPS, NB // N_CHIPS).transpose(1, 0, 2)


BIG = ("w_in", "w_uq", "w_ukv", "w_out", "w_up", "w_down")
SMALL = ("attn_pre_norm", "forget_bias", "swa_sinks", "rel_bias", "q_latent_norm", "kv_latent_norm", "group_norm",
         "attn_post_norm", "ffn_pre_norm", "conv_b", "ffn_post_norm")
WEIGHTS = ("attn_pre_norm", "w_in", "forget_bias", "swa_sinks", "rel_bias", "q_latent_norm", "w_uq", "kv_latent_norm",
           "w_ukv", "group_norm", "w_out", "attn_post_norm", "ffn_pre_norm", "w_up", "conv_w", "conv_b", "w_down",
           "ffn_post_norm")


PACK_UNIT = 8 * LANES


def _pack_rows(shape):
    return -(-int(np.prod(shape)) // PACK_UNIT) * 8


def _pack(arrs, row_mult=8):
    parts = []
    for a in arrs:
        n = int(np.prod(a.shape))
        parts.append(jnp.pad(a.reshape(-1), (0, _pack_rows(a.shape) * LANES - n)).reshape(-1, LANES))
    rows = sum(p.shape[0] for p in parts)
    pad = -rows % row_mult
    if pad:
        parts.append(jnp.zeros((pad, LANES), parts[0].dtype))
    return jnp.concatenate(parts, axis=0)


def _unpack(packed, shapes):
    packed = packed.reshape(-1, LANES)
    out, off = [], 0
    for shp in shapes:
        r = _pack_rows(shp)
        out.append(packed[off:off + r].reshape(-1)[:int(np.prod(shp))].reshape(shp))
        off += r
    return out


LAYER_KEYS = ("w_qkv_t", "w_lat_t", "w_in_t", "w_uq_p", "w_uq_t", "w_ukv", "w_ukv_t", "w_out", "w_up", "w_down", "conv_w")


MIX_WEIGHTS = ("w_in", "w_uq", "w_ukv", "w_out")
FFN_WEIGHTS = ("w_up", "w_down", "conv_w")


def _layer_weights(gathered):
    cols = lambda g: g.transpose(1, 0, 2).reshape(g.shape[1], N_CHIPS * g.shape[2])
    out = {}
    if "w_in" in gathered:
        w_in_t = _t(gathered["w_in"]).reshape(IN_COLS, D_MODEL)
        w_in_t = jnp.pad(w_in_t, ((0, IN_ROWS - IN_COLS), (0, 0)))
        w_uq_p = _pad_w_uq(cols(gathered["w_uq"]))
        w_ukv = cols(gathered["w_ukv"])
        out.update(w_qkv_t=w_in_t[:QKV_ROWS], w_lat_t=w_in_t[QKV_ROWS:], w_in_t=w_in_t, w_uq_p=w_uq_p, w_uq_t=_t(w_uq_p),
                   w_ukv=w_ukv, w_ukv_t=_t(w_ukv), w_out=gathered["w_out"].reshape(D_MODEL, D_MODEL))
    if "w_up" in gathered:
        out.update(w_up=gathered["w_up"], w_down=gathered["w_down"].reshape(D_FF, D_MODEL), conv_w=cols(gathered["conv_w"]))
    return out


def _local_step(x, target, W, layer_weights, layer_done):
    W = dict(W, **{key: [None] * DEPTH for key in LAYER_KEYS})
    S = x.shape[0]
    tq_tabs, tm_tabs = _rope_tables(S)
    onehot_t = _rel_onehot_t()
    bias_t = _bias_table(W["rel_bias"].T, onehot_t).reshape(SWA_KV_HEADS, SWA_GROUP, 2 * WINDOW, WINDOW)
    bias_t = bias_t.transpose(0, 2, 1, 3).reshape(SWA_KV_HEADS, 2 * WINDOW, GW)
    row = lambda a: a.reshape(1, -1)
    col = lambda a: a.reshape(-1, 1)
    fox_rows = (FOX_ROW0, FOX_ROW0 + FOX_HEADS * HEAD_DIM, FOX_ROW0 + 2 * FOX_HEADS * HEAD_DIM, SWA_Q_HEADS)
    fox = dict(rows=fox_rows, H=FOX_HEADS, Dk=HEAD_DIM, Dv=HEAD_DIM, scale=HEAD_DIM ** -0.5)
    mla = dict(rows=(0, 0, 0, SWA_Q_HEADS + FOX_HEADS), H=MLA_HEADS, Dk=MLA_PAD, Dv=HEAD_DIM, scale=MLA_SCALE, q_scaled=True)

    saved = []
    h = _rms_fwd(x, row(W["attn_pre_norm"][0]), name="rms_in")
    for l in range(DEPTH):
        sv = {"x0": x, "h1": h}
        for key, val in layer_weights(l, h, False).items():
            W[key][l] = val
        qkv = _matmul(W["w_qkv_t"][l], h, tb=True, out_dtype=BF16, name="proj_qkv")
        lat = _matmul(W["w_lat_t"][l], h, tb=True, name="proj_lat")
        oa, lse_a = _swa_fwd(qkv, bias_t, W["swa_sinks"][l], name="swa_fwd")
        fb_col = jnp.pad(col(W["forget_bias"][l]), ((0, GATE_ROWS - FOX_HEADS), (0, 0)))
        f4 = _gate_fwd(lat, fb_col, name="fox_gate_fwd")[:FOX_HEADS]
        f2 = f4 * LOG2E
        f_row, f_col = f2[:, None, :], f2.T
        of, lse_f = _attn_fwd(qkv, qkv, qkv, f_row=f_row, f_col=f_col, name="fox_fwd", **fox)
        nq, nkv, qm, km, vm = _mla_prep_fwd(lat, col(W["q_latent_norm"][l]), col(W["kv_latent_norm"][l]), W["w_uq_t"][l],
                                            W["w_ukv_t"][l], tq_tabs, tm_tabs, name="mla_prep_fwd")
        oc, lse_c = _attn_fwd(qm, km, vm, name="mla_fwd", **mla)
        mixed = _group_norm_fwd(oa, of, oc, col(W["group_norm"][l]), name="group_norm_fwd")
        y = _matmul(mixed, W["w_out"][l], ta=True, name="proj_out")
        x1, h2 = _resid_rms(x, y, row(W["attn_post_norm"][l]), row(W["ffn_pre_norm"][l]), name="attn_resid")
        for key, val in layer_weights(l, h2, True).items():
            W[key][l] = val
        a = _matmul(h2, W["w_up"][l], b_shards=True, out_dtype=BF16, name="ffn_up")
        u, z = _conv_geglu_fwd(a, W["conv_w"][l], row(W["conv_b"][l]), name="conv_geglu_fwd")
        y2 = _matmul(z, W["w_down"][l], name="ffn_down")
        g_next = row(W["attn_pre_norm"][l + 1]) if l + 1 < DEPTH else None
        x2, h_next = _resid_rms(x1, y2, row(W["ffn_post_norm"][l]), g_next, name="ffn_resid")
        sv.update(qkv=qkv, lat=lat, oa=oa, lse_a=lse_a, fb_col=fb_col, f_row=f_row, f_col=f_col, of=of, lse_f=lse_f,
                  nq=nq, nkv=nkv, qm=qm, km=km, vm=vm, oc=oc, lse_c=lse_c, mixed=mixed, y=y, x1=x1, h2=h2, a=a, u=u, z=z, y2=y2)
        saved.append(sv)
        x, h = x2, h_next

    loss, dx = _loss_head(x, target)

    G = {k: [None] * DEPTH for k in WEIGHTS if k != "rel_bias" and k not in BIG}
    dbias_layers = [None] * DEPTH
    for l in reversed(range(DEPTH)):
        sv = saved[l]
        gb = {}
        if l == DEPTH - 1:
            dy2, dg = _rms_bwd(sv["y2"], row(W["ffn_post_norm"][l]), dx, out_dtype=BF16, name="ffn_post_bwd")
            G["ffn_post_norm"][l] = dg[0]
        dz = _matmul(dy2, W["w_down"][l], tb=True, name="ffn_down_dx")
        gb["w_down"] = _matmul(sv["z"], dy2, ta=True, name="ffn_down_dw").reshape(N_CHIPS, D_FF // N_CHIPS, D_MODEL)
        da, dcw, dcb = _conv_geglu_bwd(sv["a"], sv["u"], W["conv_w"][l], dz, name="conv_geglu_bwd")
        G["conv_w"][l] = dcw.transpose(1, 0, 2).reshape(3, 2 * D_FF)
        G["conv_b"][l] = dcb.reshape(2 * D_FF)
        dh2 = _matmul(da, W["w_up"][l], tb=True, b_shards=True, a_halves=True, name="ffn_up_dx")
        gb["w_up"] = _matmul(sv["h2"], da, ta=True, out_shards=True, b_halves=True, name="ffn_up_dw")
        token = layer_done(l, gb)
        gb = {}
        dx1, dg, dy, dg_post = _rms_bwd(sv["x1"], row(W["ffn_pre_norm"][l]) + token, dh2, resid=dx, out_dtype=F32,
                                        then=(sv["y"], row(W["attn_post_norm"][l])), name="ffn_pre_bwd")
        G["ffn_pre_norm"][l] = dg[0]
        G["attn_post_norm"][l] = dg_post[0]
        dmixed = _matmul(W["w_out"][l], dy, tb=True, name="proj_out_dx")
        gb["w_out"] = _matmul(sv["mixed"], dy, name="proj_out_dw").reshape(N_CHIPS, D_MODEL // N_CHIPS, D_MODEL)
        doa, dof, doc, dg, delta = _group_norm_bwd(sv["oa"], sv["of"], sv["oc"], col(W["group_norm"][l]), dmixed,
                                                   name="group_norm_bwd")
        G["group_norm"][l] = dg[:, 0]
        dqa, dkva, dbias_l, dsink = _swa_bwd(sv["qkv"], bias_t, W["swa_sinks"][l], doa, sv["lse_a"],
                                             delta.reshape(-1, S), name="swa_bwd")
        dbias_layers[l] = (dbias_l.reshape(SWA_KV_HEADS, 2 * WINDOW, SWA_GROUP, WINDOW).transpose(0, 2, 1, 3)
                           .reshape(SWA_Q_HEADS, -1))
        G["swa_sinks"][l] = dsink[:, 0]
        dqf, dkf, dvf, dfk = _attn_bwd(sv["qkv"], sv["qkv"], sv["qkv"], do=dof, lse=sv["lse_f"], delta=delta,
                                       f_row=sv["f_row"], f_col=sv["f_col"], name="fox_bwd", **fox)
        dF = jnp.pad(dfk.T, ((0, GATE_ROWS - FOX_HEADS), (0, 0)))
        dflog, dfb = _gate_bwd(sv["lat"], sv["fb_col"], dF, name="fox_gate_bwd")
        G["forget_bias"][l] = dfb[:FOX_HEADS, 0]
        dqm, dkm, dvm = _attn_bwd(sv["qm"], sv["km"], sv["vm"], do=doc, lse=sv["lse_c"], delta=delta, name="mla_bwd", **mla)
        dlat, dwq_t, dwkv_t, dgq, dgkv = _mla_prep_bwd(
            sv["lat"], sv["nq"], sv["nkv"], col(W["q_latent_norm"][l]), col(W["kv_latent_norm"][l]), W["w_uq_p"][l],
            W["w_ukv"][l], tq_tabs, tm_tabs, dqm, dkm, dvm, dflog, name="mla_prep_bwd")
        gb["w_uq"], gb["w_ukv"] = _shards_of_cols(_unpad_w_uq(dwq_t.T)), _shards_of_cols(dwkv_t.T)
        G["q_latent_norm"][l], G["kv_latent_norm"][l] = dgq[:, 0], dgkv[:, 0]
        dproj = _dproj_cast(dqa, dkva, dqf, dkf, dvf, dlat, name="dproj_cast")
        dh1 = _matmul(dproj, W["w_in_t"][l], ta=True, name="proj_in_dx")
        dw_in_t = _matmul(dproj, sv["h1"], name="proj_in_dw")
        gb["w_in"] = _t(dw_in_t[:IN_COLS].reshape(N_CHIPS, IN_COLS // N_CHIPS, D_MODEL))
        token = layer_done(l, gb)
        below = (saved[l - 1]["y2"], row(W["ffn_post_norm"][l - 1])) if l > 0 else None
        res = _rms_bwd(sv["x0"], row(W["attn_pre_norm"][l]) + token, dh1, resid=dx1, out_dtype=F32, then=below,
                       name="attn_pre_bwd")
        dx, G["attn_pre_norm"][l] = res[0], res[1][0]
        if l > 0:
            dy2, G["ffn_post_norm"][l - 1] = res[2], res[3][0]

    grads = {k: jnp.stack(v) for k, v in G.items()}
    grads["rel_bias"] = _bias_table_bwd(jnp.stack(dbias_layers), onehot_t).T
    return loss, dx, grads


def kernel(x, attn_pre_norm, w_in, forget_bias, swa_sinks, rel_bias, q_latent_norm, w_uq, kv_latent_norm, w_ukv, group_norm, w_out, attn_post_norm, ffn_pre_norm, w_up, conv_w, conv_b, w_down, ffn_post_norm, loss_target, m_attn_pre_norm, m_w_in, m_forget_bias, m_swa_sinks, m_rel_bias, m_q_latent_norm, m_w_uq, m_kv_latent_norm, m_w_ukv, m_group_norm, m_w_out, m_attn_post_norm, m_ffn_pre_norm, m_w_up, m_conv_w, m_conv_b, m_w_down, m_ffn_post_norm, v_attn_pre_norm, v_w_in, v_forget_bias, v_swa_sinks, v_rel_bias, v_q_latent_norm, v_w_uq, v_kv_latent_norm, v_w_ukv, v_group_norm, v_w_out, v_attn_post_norm, v_ffn_pre_norm, v_w_up, v_conv_w, v_conv_b, v_w_down, v_ffn_post_norm):
    args = dict(locals())
    w = {k: args[k] for k in WEIGHTS}
    m = {k: args["m_" + k] for k in WEIGHTS}
    v = {k: args["v_" + k] for k in WEIGHTS}

    block = lambda l, keys: [w[k][l] if k == "conv_w" else w[k][l].astype(BF16) for k in keys]
    units = [(0, MIX_WEIGHTS), (0, FFN_WEIGHTS)] + [(l, MIX_WEIGHTS + FFN_WEIGHTS) for l in range(1, DEPTH)]
    gather_state, token = _gather_start([block(l, keys) for l, keys in units])
    W = {k: w[k] for k in SMALL}
    W["attn_pre_norm"] = W["attn_pre_norm"] + token

    def layer_weights(l, after, for_ffn):
        if for_ffn and l > 0:
            return {}
        keys = FFN_WEIGHTS if for_ffn else (MIX_WEIGHTS if l == 0 else MIX_WEIGHTS + FFN_WEIGHTS)
        tag = f"{l}_{keys[0]}"
        srcs, lands = _gather_wait(gather_state[units.index((l, keys))], after, name="weight_gather_wait_" + tag)
        lands = _gather_forward(lands, name="weight_gather_forward_" + tag)
        lands = _place_own(lands, srcs, name="place_own_shards")
        return _layer_weights(dict(zip(keys, lands)))

    started, groups, pending = [], [], []

    def to_chips(l, keys, gs, recv, tag):
        pair = [_pair_sum(gk, rk, name="grad_pair_sum") for gk, rk in zip(gs, recv)]
        state, token = _scatter_start(pair, name="grad_scatter_start_" + tag)
        started.append(state)
        groups.append((l, keys))
        return token

    def finish_pending(after):
        l, keys, tag, state = pending.pop()
        gs, recv = _split_wait([state], _exchange_copies, after, name="grad_exchange_wait_" + tag)[0]
        return to_chips(l, keys, gs, recv, tag)

    def layer_done(l, gb):
        keys = [k for k in BIG if k in gb]
        gs = [gb[k] for k in keys]
        tag = f"{l}_{keys[0]}"
        token = finish_pending(gs[0]) if pending else 0.0
        if l == 0:
            return token + to_chips(l, keys, gs, _sibling_exchange(gs, name="grad_sibling_exchange_" + tag), tag)
        state, started_token = _exchange_start(gs, name="grad_exchange_start_" + tag)
        pending.append((l, keys, tag, state))
        return token + started_token

    loss_part, dx, g = _local_step(x[0], loss_target[0], W, layer_weights, layer_done)
    loss = lax.psum(loss_part, ("x", "y", "c"))

    reduced = {}
    for (l, keys), (pair, zones) in zip(groups, _split_wait(started, _scatter_copies, dx, name="grad_scatter_wait")):
        for k, p, z in zip(keys, pair, zones):
            reduced[k, l] = _chip_sum(z, p, name="grad_chip_sum")
    mine = [jnp.stack([reduced[k, l] for l in range(DEPTH)]) for k in BIG]
    other = _sibling_share(mine)
    out_g, out_d, out_m, out_v = {}, {}, {}, {}
    for k, g_mine, g_other in zip(BIG, mine, other):
        out_g[k], out_d[k], out_m[k], out_v[k] = _adamw_halves(w[k], g_mine, g_other, m[k], v[k], name="adamw_" + k)

    small_shapes = [w[k].shape for k in SMALL]
    reduced = _allreduce_small(_pack([g[k] for k in SMALL] + [g["conv_w"]]))
    *g_small, g_cw = _unpack(reduced, small_shapes + [g["conv_w"].shape])
    chip = 2 * lax.axis_index("x") + lax.axis_index("y")
    g_small.append(lax.dynamic_slice_in_dim(g_cw, chip * FF_SHARD, FF_SHARD, axis=2))
    names = SMALL + ("conv_w",)
    shapes = small_shapes + [w["conv_w"].shape]
    packed = lambda arrs: _pack(arrs, ROW_TILE)[None]
    d_s, m_s, v_s = _adamw(packed([w[k] for k in names]), packed(g_small), packed([m[k] for k in names]),
                           packed([v[k] for k in names]), name="adamw_small")
    out_g.update(zip(names, g_small))
    out_d.update(zip(names, _unpack(d_s, shapes)))
    out_m.update(zip(names, _unpack(m_s, shapes)))
    out_v.update(zip(names, _unpack(v_s, shapes)))

    return (loss, dx[None], *[out_g[k] for k in WEIGHTS], *[out_d[k] for k in WEIGHTS],
            *[out_m[k] for k in WEIGHTS], *[out_v[k] for k in WEIGHTS])
```

```python
import math

import numpy as np
import jax
import jax.numpy as jnp
from jax import lax
from jax.experimental import pallas as pl
from jax.experimental.pallas import tpu as pltpu

F32 = jnp.float32
BF16 = jnp.bfloat16

D_MODEL = 1024
DEPTH = 4
HEAD_DIM = 64
SWA_Q_HEADS = 8
SWA_KV_HEADS = 2
SWA_GROUP = SWA_Q_HEADS // SWA_KV_HEADS
WINDOW = 128
FOX_HEADS = 4
MLA_HEADS = 4
MLA_Q_RANK = 256
MLA_KV_RANK = 128
MLA_NOPE = 64
MLA_ROPE = 32
MLA_QK = MLA_NOPE + MLA_ROPE
ROPE_THETA = 10000.0
REL_BUCKETS = 32
REL_MAX_DIST = 128
D_FF = 2816
EPS = 1e-6
NEG_INF = -1e30
LANES = 128
N_CHIPS = 4

IN_COLS = 1956
IN_ROWS = 2048
QKV_ROWS = 1536
LAT_ROWS = IN_ROWS - QKV_ROWS
LAT_SHIFT = FOX_HEADS
FOX_ROW0 = 768
MLA_PAD = LANES
GATE_ROWS = 8

ADAM_LR = 0.001
ADAM_B1 = 0.9
ADAM_B2 = 0.999
ADAM_EPS = 1e-08
ADAM_WD = 0.01
ADAM_STEP = 10

VMEM_LIMIT_BYTES = 48 * 1024 * 1024
ATT_TILE = 512
LOG2E = math.log2(math.e)
MLA_SCALE = MLA_QK ** -0.5
ROW_TILE = 256
MESH = pl.DeviceIdType.MESH

NT = (((1,), (1,)), ((), ()))
TN = (((0,), (0,)), ((), ()))
NN = (((1,), (0,)), ((), ()))


def _params(*sem):
    return pltpu.CompilerParams(dimension_semantics=sem, vmem_limit_bytes=VMEM_LIMIT_BYTES)


def _tile(dim, cap):
    for t in (2816, 2048, 1408, 1024, 512, 256, 128, 64, 32, 16, 8):
        if t <= cap and dim % t == 0:
            return t
    return dim


def _dot(a, b, dims=NN):
    return lax.dot_general(a, b, dims, preferred_element_type=F32)


def _split3(a):
    a1 = a.astype(BF16)
    r1 = a - a1.astype(F32)
    a2 = r1.astype(BF16)
    a3 = (r1 - a2.astype(F32)).astype(BF16)
    return a1, a2, a3


FF_SHARD = 2 * D_FF // N_CHIPS
MATMUL_VMEM_BYTES = 40 * 1024 * 1024


def _matmul(a, b, *, ta=False, tb=False, out_dtype=F32, name, b_shards=False, out_shards=False, a_halves=False,
            b_halves=False):
    if a_halves:
        M, K = a.shape[1], 2 * a.shape[2]
    elif ta:
        K, M = a.shape
    else:
        M, K = a.shape
    if b_halves:
        K2, N = b.shape[1], 2 * b.shape[2]
    elif b_shards:
        K2, N = (2 * D_FF, D_MODEL) if tb else (D_MODEL, 2 * D_FF)
    elif tb:
        N, K2 = b.shape
    else:
        K2, N = b.shape
    assert K == K2, (a.shape, b.shape)
    tm, tn = (M if M <= 2048 else _tile(M, 1408)), _tile(N, 1408)
    tk = FF_SHARD if (b_shards and tb) else _tile(K, 2816)
    out_bytes = jnp.dtype(out_dtype).itemsize
    while 2 * 2 * tk * (tm + tn) + (4 + 2 * out_bytes) * tm * tn > MATMUL_VMEM_BYTES and tk % 256 == 0:
        tk //= 2
    nk = K // tk
    dims = (((0 if ta else 1,), (1 if tb else 0,)), ((), ()))

    def body(a_ref, b_ref, o_ref, acc_ref):
        k = pl.program_id(2)

        @pl.when(k == 0)
        def _():
            acc_ref[...] = jnp.zeros_like(acc_ref)

        acc_ref[...] += lax.dot_general(a_ref[...], b_ref[...], dims, preferred_element_type=F32)

        @pl.when(k == nk - 1)
        def _():
            o_ref[...] = acc_ref[...].astype(o_ref.dtype)

    if a_halves:
        nh = K // 2 // tk
        a_spec = pl.BlockSpec((None, tm, tk), lambda i, j, k: (k // nh, i, k % nh))
    else:
        a_spec = pl.BlockSpec((tk, tm), lambda i, j, k: (k, i)) if ta else pl.BlockSpec((tm, tk), lambda i, j, k: (i, k))
    if b_halves:
        nh = N // 2 // tn
        b_spec = pl.BlockSpec((None, tk, tn), lambda i, j, k: (j // nh, k, j % nh))
    elif b_shards and tb:
        assert tk == FF_SHARD
        b_spec = pl.BlockSpec((None, tn, tk), lambda i, j, k: (k, j, 0))
    elif b_shards:
        assert tn == FF_SHARD
        b_spec = pl.BlockSpec((None, tk, tn), lambda i, j, k: (j, k, 0))
    else:
        b_spec = pl.BlockSpec((tn, tk), lambda i, j, k: (j, k)) if tb else pl.BlockSpec((tk, tn), lambda i, j, k: (k, j))
    if out_shards:
        assert tn == FF_SHARD
        out_spec = pl.BlockSpec((None, tm, tn), lambda i, j, k: (j, i, 0))
        out_shape = jax.ShapeDtypeStruct((N // tn, M, tn), out_dtype)
    else:
        out_spec = pl.BlockSpec((tm, tn), lambda i, j, k: (i, j))
        out_shape = jax.ShapeDtypeStruct((M, N), out_dtype)
    return pl.pallas_call(
        body, name=name, grid=(M // tm, N // tn, nk),
        in_specs=[a_spec, b_spec], out_specs=out_spec, out_shape=out_shape,
        scratch_shapes=[pltpu.VMEM((tm, tn), F32)],
        compiler_params=_params("parallel", "parallel", "arbitrary"),
    )(a, b)


def _seg_rms(xs, g):
    r = lax.rsqrt(jnp.mean(xs * xs, axis=-1, keepdims=True) + EPS)
    return xs * r * g


def _seg_rms_bwd(xs, g, dy):
    r = lax.rsqrt(jnp.mean(xs * xs, axis=-1, keepdims=True) + EPS)
    gd = dy * g
    c = jnp.mean(gd * xs, axis=-1, keepdims=True)
    dx = r * gd - xs * (r * r * r * c)
    dg = jnp.sum(dy * (xs * r), axis=0, keepdims=True)
    return dx, dg


def _rms_fwd(x, g, *, name):
    S, W = x.shape
    tm = _tile(S, 512)

    def body(x_ref, g_ref, o_ref):
        o_ref[...] = _seg_rms(x_ref[...], g_ref[...]).astype(o_ref.dtype)

    return pl.pallas_call(
        body, name=name, grid=(S // tm,),
        in_specs=[pl.BlockSpec((tm, W), lambda i: (i, 0)), pl.BlockSpec((1, W), lambda i: (0, 0))],
        out_specs=pl.BlockSpec((tm, W), lambda i: (i, 0)),
        out_shape=jax.ShapeDtypeStruct((S, W), BF16),
        compiler_params=_params("parallel"),
    )(x, g)


def _rms_bwd(x, g, dy, *, resid=None, out_dtype, name, then=None):
    S, W = x.shape
    tm = _tile(S, 512)
    has_resid = resid is not None
    chained = then is not None

    def body(*refs):
        refs = list(refs)
        x_ref, g_ref, dy_ref = refs[:3]
        r_ref = refs[3] if has_resid else None
        n_in = 3 + has_resid + 2 * chained
        x2_ref, g2_ref = (refs[n_in - 2], refs[n_in - 1]) if chained else (None, None)
        outs = refs[n_in:]
        dx_ref, dg_ref = outs[0], outs[1]

        @pl.when(pl.program_id(0) == 0)
        def _():
            dg_ref[...] = jnp.zeros_like(dg_ref)
            if chained:
                outs[3][...] = jnp.zeros_like(outs[3])

        dx, dg = _seg_rms_bwd(x_ref[...], g_ref[...], dy_ref[...])
        if has_resid:
            dx = dx + r_ref[...]
        dx_ref[...] = dx.astype(dx_ref.dtype)
        dg_ref[...] += dg
        if chained:
            dx2, dg2 = _seg_rms_bwd(x2_ref[...], g2_ref[...], dx)
            outs[2][...] = dx2.astype(BF16)
            outs[3][...] += dg2

    row = pl.BlockSpec((tm, W), lambda i: (i, 0))
    vec = pl.BlockSpec((1, W), lambda i: (0, 0))
    ins = [x, g, dy] + ([resid] if has_resid else []) + (list(then) if chained else [])
    return pl.pallas_call(
        body, name=name, grid=(S // tm,),
        in_specs=[row, vec, row] + ([row] if has_resid else []) + ([row, vec] if chained else []),
        out_specs=[row, vec] + ([row, vec] if chained else []),
        out_shape=[jax.ShapeDtypeStruct((S, W), out_dtype), jax.ShapeDtypeStruct((1, W), F32)]
        + ([jax.ShapeDtypeStruct((S, W), BF16), jax.ShapeDtypeStruct((1, W), F32)] if chained else []),
        compiler_params=_params("arbitrary"),
    )(*ins)


def _resid_rms(x, y, g_post, g_next, *, name):
    S, W = x.shape
    tm = _tile(S, 512)
    with_next = g_next is not None

    def body(*refs):
        if with_next:
            x_ref, y_ref, gp_ref, gn_ref, xo_ref, h_ref = refs
        else:
            x_ref, y_ref, gp_ref, xo_ref = refs
        xn = x_ref[...] + _seg_rms(y_ref[...], gp_ref[...])
        xo_ref[...] = xn
        if with_next:
            h_ref[...] = _seg_rms(xn, gn_ref[...]).astype(BF16)

    row = pl.BlockSpec((tm, W), lambda i: (i, 0))
    vec = pl.BlockSpec((1, W), lambda i: (0, 0))
    outs = [jax.ShapeDtypeStruct((S, W), F32)] + ([jax.ShapeDtypeStruct((S, W), BF16)] if with_next else [])
    res = pl.pallas_call(
        body, name=name, grid=(S // tm,),
        in_specs=[row, row, vec] + ([vec] if with_next else []),
        out_specs=[row] + ([row] if with_next else []),
        out_shape=outs,
        compiler_params=_params("parallel"),
    )(*([x, y, g_post] + ([g_next] if with_next else [])))
    return (res[0], res[1]) if with_next else (res[0], None)


def _col_rms(xs, g):
    r = lax.rsqrt(jnp.mean(xs * xs, axis=0, keepdims=True) + EPS)
    return xs * r * g


def _col_rms_bwd(xs, g, dy):
    r = lax.rsqrt(jnp.mean(xs * xs, axis=0, keepdims=True) + EPS)
    gd = dy * g
    c = jnp.mean(gd * xs, axis=0, keepdims=True)
    dx = r * gd - xs * (r * r * r * c)
    dg = jnp.sum(dy * (xs * r), axis=1, keepdims=True)
    return dx, dg


GROUP_ROWS = (SWA_Q_HEADS * HEAD_DIM, FOX_HEADS * HEAD_DIM, MLA_HEADS * HEAD_DIM)


def _group_specs(S, tn):
    outs = [pl.BlockSpec((n, tn), lambda i: (0, i)) for n in GROUP_ROWS]
    g = pl.BlockSpec((D_MODEL, 1), lambda i: (0, 0))
    mixed = pl.BlockSpec((D_MODEL, tn), lambda i: (0, i))
    return outs, g, mixed


def _group_norm_fwd(oa, of, oc, g, *, name):
    S = oa.shape[1]
    tn = _tile(S, 512)
    outs, gs, mixed = _group_specs(S, tn)

    def body(a_ref, f_ref, c_ref, g_ref, o_ref):
        r0 = 0
        for ref, n in zip((a_ref, f_ref, c_ref), GROUP_ROWS):
            o_ref[r0:r0 + n, :] = _col_rms(ref[...], g_ref[r0:r0 + n, :]).astype(BF16)
            r0 += n

    return pl.pallas_call(
        body, name=name, grid=(S // tn,),
        in_specs=outs + [gs], out_specs=mixed,
        out_shape=jax.ShapeDtypeStruct((D_MODEL, S), BF16),
        compiler_params=_params("parallel"),
    )(oa, of, oc, g)


def _group_norm_bwd(oa, of, oc, g, dmixed, *, name):
    S = oa.shape[1]
    tn = _tile(S, 512)
    outs, gs, mixed = _group_specs(S, tn)
    n_heads = D_MODEL // HEAD_DIM

    def body(a_ref, f_ref, c_ref, g_ref, dm_ref, da_ref, df_ref, dc_ref, dg_ref, dl_ref):
        @pl.when(pl.program_id(0) == 0)
        def _():
            dg_ref[...] = jnp.zeros_like(dg_ref)

        r0 = 0
        for ref, dref, n in zip((a_ref, f_ref, c_ref), (da_ref, df_ref, dc_ref), GROUP_ROWS):
            o = ref[...]
            dx, dg = _col_rms_bwd(o, g_ref[r0:r0 + n, :], dm_ref[r0:r0 + n, :])
            dxb = dx.astype(BF16)
            dref[...] = dxb
            dg_ref[r0:r0 + n, :] += dg
            od = o * dxb.astype(F32)
            for h in range(n // HEAD_DIM):
                dl_ref[r0 // HEAD_DIM + h] = jnp.sum(od[h * HEAD_DIM:(h + 1) * HEAD_DIM, :], axis=0, keepdims=True)
            r0 += n

    return pl.pallas_call(
        body, name=name, grid=(S // tn,),
        in_specs=outs + [gs, mixed], out_specs=outs + [gs, pl.BlockSpec((n_heads, 1, tn), lambda i: (0, 0, i))],
        out_shape=[jax.ShapeDtypeStruct((n, S), BF16) for n in GROUP_ROWS] + [jax.ShapeDtypeStruct((D_MODEL, 1), F32),
                                                                              jax.ShapeDtypeStruct((n_heads, 1, S), F32)],
        compiler_params=_params("arbitrary"),
    )(oa, of, oc, g, dmixed)


def _loss_head(y, target):
    S, W = y.shape
    tm = _tile(S, 512)

    def body(y_ref, t_ref, d_ref, l_ref):
        @pl.when(pl.program_id(0) == 0)
        def _():
            l_ref[...] = jnp.zeros_like(l_ref)

        err = y_ref[...] - t_ref[...]
        d_ref[...] = err * (1.0 / W)
        l_ref[...] += 0.5 * jnp.sum(jnp.mean(err * err, axis=-1, keepdims=True), axis=0, keepdims=True)

    row = pl.BlockSpec((tm, W), lambda i: (i, 0))
    d, l = pl.pallas_call(
        body, name="loss_head", grid=(S // tm,),
        in_specs=[row, row],
        out_specs=[row, pl.BlockSpec((1, 1), lambda i: (0, 0))],
        out_shape=[jax.ShapeDtypeStruct((S, W), F32), jax.ShapeDtypeStruct((1, 1), F32)],
        compiler_params=_params("arbitrary"),
    )(y, target)
    return l[0, 0], d


def _attn_fwd(q_src, k_src, v_src, rows, H, Dk, Dv, scale, f_row=None, f_col=None, *, name, q_scaled=False):
    S = q_src.shape[1]
    T = _tile(S, ATT_TILE)
    nq = S // T
    forget = f_row is not None
    qb, kb, vb = rows[0] // (H * Dk), rows[1] // (H * Dk), rows[2] // (H * Dv)
    hs = range(H)

    def body(*refs):
        if forget:
            q_ref, k_ref, v_ref, fq_ref, fk_ref, o_ref, lse_ref = refs
        else:
            q_ref, k_ref, v_ref, o_ref, lse_ref = refs
        i = pl.program_id(0)

        def tile(j, masked, state):
            off = pl.multiple_of(j * T, T)
            ss = [_dot(k_ref[h * Dk:(h + 1) * Dk, pl.ds(off, T)], q_ref[h * Dk:(h + 1) * Dk, :], TN) for h in hs]
            if not q_scaled:
                ss = [s * (scale * LOG2E) for s in ss]
            if forget:
                ss = [ss[h] + (fq_ref[h] - fk_ref[pl.ds(off, T), h:h + 1]) for h in hs]
            if masked:
                r = lax.broadcasted_iota(jnp.int32, (T, T), 0)
                c = lax.broadcasted_iota(jnp.int32, (T, T), 1)
                ss = [jnp.where(r <= c, s, NEG_INF) for s in ss]
            m_new = [jnp.maximum(state[h][0], jnp.max(ss[h], axis=0, keepdims=True)) for h in hs]
            alpha = [jnp.exp2(state[h][0] - m_new[h]) for h in hs]
            ps = [jnp.exp2(ss[h] - m_new[h]) for h in hs]
            l_new = [alpha[h] * state[h][1] + jnp.sum(ps[h], axis=0, keepdims=True) for h in hs]
            p_hi = [p.astype(BF16) for p in ps]
            vs = [v_ref[h * Dv:(h + 1) * Dv, pl.ds(off, T)] for h in hs]
            pv = [_dot(vs[h], p_hi[h]) for h in hs]
            if forget:
                pv = [pv[h] + _dot(vs[h], (ps[h] - p_hi[h].astype(F32)).astype(BF16)) for h in hs]
            return tuple((m_new[h], l_new[h], alpha[h] * state[h][2] + pv[h]) for h in hs)

        init = tuple((jnp.full((1, T), NEG_INF, F32), jnp.zeros((1, T), F32), jnp.zeros((Dv, T), F32)) for _ in hs)
        state = lax.fori_loop(0, i, lambda j, st: tile(j, False, st), init)
        state = tile(i, True, state)
        for h in hs:
            m, l, acc = state[h]
            o_ref[h * Dv:(h + 1) * Dv, :] = acc / l
            lse_ref[h] = m + jnp.log2(l)

    in_specs = [pl.BlockSpec((H * Dk, T), lambda i: (qb, i)),
                pl.BlockSpec((H * Dk, S), lambda i: (kb, 0)),
                pl.BlockSpec((H * Dv, S), lambda i: (vb, 0))]
    ins = [q_src, k_src, v_src]
    if forget:
        in_specs += [pl.BlockSpec((H, 1, T), lambda i: (0, 0, i)), pl.BlockSpec((S, H), lambda i: (0, 0))]
        ins += [f_row, f_col]
    return pl.pallas_call(
        body, name=name, grid=(nq,),
        in_specs=in_specs,
        out_specs=[pl.BlockSpec((H * Dv, T), lambda i: (0, i)), pl.BlockSpec((H, 1, T), lambda i: (0, 0, i))],
        out_shape=[jax.ShapeDtypeStruct((H * Dv, S), F32), jax.ShapeDtypeStruct((H, 1, S), F32)],
        compiler_params=_params("parallel"),
    )(*ins)


def _attn_bwd(q_src, k_src, v_src, rows, H, Dk, Dv, scale, do, lse, delta, f_row=None, f_col=None, *, name, q_scaled=False):
    S = q_src.shape[1]
    T = _tile(S, ATT_TILE)
    nq = S // T
    forget = f_row is not None
    qb, kb, vb, db = rows[0] // (H * Dk), rows[1] // (H * Dk), rows[2] // (H * Dv), rows[3] // H
    hs = range(H)

    def body(*refs):
        if forget:
            (q_ref, k_ref, v_ref, do_ref, lse_ref, dl_ref, fq_ref, fk_ref,
             dq_ref, dk_ref, dv_ref, df_ref, dk_s, dv_s, df_s) = refs
        else:
            q_ref, k_ref, v_ref, do_ref, lse_ref, dl_ref, dq_ref, dk_ref, dv_ref, dk_s, dv_s = refs
        j = pl.program_id(0)

        @pl.when(j == 0)
        def _():
            dq_ref[...] = jnp.zeros_like(dq_ref)

        dk_s[...] = jnp.zeros_like(dk_s)
        dv_s[...] = jnp.zeros_like(dv_s)
        if forget:
            df_s[...] = jnp.zeros_like(df_s)
        kt = [k_ref[h * Dk:(h + 1) * Dk, :] for h in hs]
        kj = [k.T for k in kt]
        vj = [v_ref[h * Dv:(h + 1) * Dv, :].T for h in hs]
        koff = pl.multiple_of(j * T, T)

        def tile(i, masked):
            cols = pl.ds(pl.multiple_of(i * T, T), T)
            qi = [q_ref[h * Dk:(h + 1) * Dk, cols] for h in hs]
            doi = [do_ref[h * Dv:(h + 1) * Dv, cols] for h in hs]
            st = [_dot(kj[h], qi[h]) for h in hs]
            if not q_scaled:
                st = [x * (scale * LOG2E) for x in st]
            if forget:
                st = [st[h] + (fq_ref[h, :, cols] - fk_ref[pl.ds(koff, T), h:h + 1]) for h in hs]
            if masked:
                r = lax.broadcasted_iota(jnp.int32, (T, T), 0)
                c = lax.broadcasted_iota(jnp.int32, (T, T), 1)
                st = [jnp.where(r <= c, x, NEG_INF) for x in st]
            pt = [jnp.exp2(st[h] - lse_ref[h, :, cols]) for h in hs]
            dpt = [_dot(vj[h], doi[h]) for h in hs]
            dst = [pt[h] * (dpt[h] - dl_ref[h, :, cols]) for h in hs]
            ptb = [p.astype(BF16) for p in pt]
            dsb = [d.astype(BF16) for d in dst]
            for h in hs:
                dv_s[h * Dv:(h + 1) * Dv, :] += _dot(doi[h], ptb[h], NT)
            for h in hs:
                dk_s[h * Dk:(h + 1) * Dk, :] += _dot(qi[h], dsb[h], NT)
            for h in hs:
                dq_ref[h * Dk:(h + 1) * Dk, cols] += _dot(kt[h], dsb[h]) * scale
            if forget:
                for h in hs:
                    part = dst[h][:, 0:LANES]
                    for c0 in range(LANES, T, LANES):
                        part = part + dst[h][:, c0:c0 + LANES]
                    df_s[h] += part

        tile(j, True)

        def loop_body(i, carry):
            tile(i, False)
            return carry

        lax.fori_loop(j + 1, nq, loop_body, 0)
        dk_ref[...] = dk_s[...] * ((1.0 / LOG2E) if q_scaled else scale)
        dv_ref[...] = dv_s[...]
        if forget:
            df_ref[...] = jnp.concatenate([-jnp.sum(df_s[h], axis=-1, keepdims=True) for h in hs], axis=1)

    res = lambda D, b0: pl.BlockSpec((H * D, S), lambda j: (b0, 0))
    blk = lambda D, b0: pl.BlockSpec((H * D, T), lambda j: (b0, j))
    row3 = lambda b0: pl.BlockSpec((H, 1, S), lambda j: (b0, 0, 0))
    in_specs = [res(Dk, qb), blk(Dk, kb), blk(Dv, vb), res(Dv, 0), row3(0), row3(db)]
    ins = [q_src, k_src, v_src, do, lse, delta]
    out_specs = [res(Dk, 0), blk(Dk, 0), blk(Dv, 0)]
    out_shape = [jax.ShapeDtypeStruct((H * Dk, S), F32), jax.ShapeDtypeStruct((H * Dk, S), F32),
                 jax.ShapeDtypeStruct((H * Dv, S), F32)]
    scratch = [pltpu.VMEM((H * Dk, T), F32), pltpu.VMEM((H * Dv, T), F32)]
    if forget:
        in_specs += [row3(0), pl.BlockSpec((S, H), lambda j: (0, 0))]
        ins += [f_row, f_col]
        out_specs.append(pl.BlockSpec((T, H), lambda j: (j, 0)))
        out_shape.append(jax.ShapeDtypeStruct((S, H), F32))
        scratch.append(pltpu.VMEM((H, T, min(T, LANES)), F32))
    return pl.pallas_call(
        body, name=name, grid=(nq,),
        in_specs=in_specs, out_specs=out_specs, out_shape=out_shape, scratch_shapes=scratch,
        compiler_params=_params("arbitrary"),
    )(*ins)


GW = SWA_GROUP * WINDOW


def _swa_masks(i):
    r = lax.broadcasted_iota(jnp.int32, (WINDOW, GW), 0)
    c = lax.broadcasted_iota(jnp.int32, (WINDOW, GW), 1) % WINDOW
    return (r > c) & (i > 0), r <= c


def _swa_specs():
    W = WINDOW
    kv_rows = SWA_KV_HEADS * HEAD_DIM
    q = pl.BlockSpec((SWA_Q_HEADS * HEAD_DIM, W), lambda i: (0, i))
    prev = lambda b: pl.BlockSpec((kv_rows, W), lambda i: (b, jnp.maximum(i - 1, 0)))
    cur = lambda b: pl.BlockSpec((kv_rows, W), lambda i: (b, i))
    bias = pl.BlockSpec((SWA_KV_HEADS, 2 * W, GW), lambda i: (0, 0, 0))
    stat = pl.BlockSpec((SWA_Q_HEADS, W), lambda i: (0, i))
    sink = pl.BlockSpec(memory_space=pltpu.SMEM)
    return q, prev(4), cur(4), prev(5), cur(5), bias, stat, sink


def _group_lanes(ref, g, rows_per_head):
    h0 = g * SWA_GROUP
    return jnp.concatenate([ref[(h0 + j) * rows_per_head:(h0 + j + 1) * rows_per_head, :] for j in range(SWA_GROUP)], axis=1)


def _swa_scores(g, q_ref, kp_ref, kc_ref, b_ref, masks):
    rows = slice(g * HEAD_DIM, (g + 1) * HEAD_DIM)
    qg = _group_lanes(q_ref, g, HEAD_DIM)
    scale = HEAD_DIM ** -0.5
    s_p = jnp.where(masks[0], _dot(kp_ref[rows, :], qg, TN) * scale + b_ref[g, 0:WINDOW, :], NEG_INF)
    s_c = jnp.where(masks[1], _dot(kc_ref[rows, :], qg, TN) * scale + b_ref[g, WINDOW:2 * WINDOW, :], NEG_INF)
    return qg, rows, s_p, s_c


def _sink_row(sink_ref, g):
    return jnp.concatenate([jnp.full((1, WINDOW), sink_ref[g * SWA_GROUP + j], F32) for j in range(SWA_GROUP)], axis=1)


def _swa_fwd(qkv, bias_g, sinks, *, name):
    S = qkv.shape[1]
    qs, kp, kc, vp, vc, bs, stat, sk = _swa_specs()
    gs = range(SWA_KV_HEADS)

    def body(sink_ref, q_ref, kp_ref, kc_ref, vp_ref, vc_ref, b_ref, o_ref, lse_ref):
        masks = _swa_masks(pl.program_id(0))
        sc = [_swa_scores(g, q_ref, kp_ref, kc_ref, b_ref, masks) for g in gs]
        sinks_g = [_sink_row(sink_ref, g) for g in gs]
        m = [jnp.maximum(jnp.maximum(jnp.max(sc[g][2], axis=0, keepdims=True), jnp.max(sc[g][3], axis=0, keepdims=True)),
                         sinks_g[g]) for g in gs]
        p_p = [jnp.exp(sc[g][2] - m[g]) for g in gs]
        p_c = [jnp.exp(sc[g][3] - m[g]) for g in gs]
        l = [jnp.sum(p_p[g], axis=0, keepdims=True) + jnp.sum(p_c[g], axis=0, keepdims=True) + jnp.exp(sinks_g[g] - m[g])
             for g in gs]
        o = [_dot(vp_ref[sc[g][1], :], p_p[g].astype(BF16)) + _dot(vc_ref[sc[g][1], :], p_c[g].astype(BF16)) for g in gs]
        for g in gs:
            og = o[g] / l[g]
            lse = m[g] + jnp.log(l[g])
            for j in range(SWA_GROUP):
                h = g * SWA_GROUP + j
                o_ref[h * HEAD_DIM:(h + 1) * HEAD_DIM, :] = og[:, j * WINDOW:(j + 1) * WINDOW]
                lse_ref[h:h + 1, :] = lse[:, j * WINDOW:(j + 1) * WINDOW]

    return pl.pallas_call(
        body, name=name, grid=(S // WINDOW,),
        in_specs=[sk, qs, kp, kc, vp, vc, bs],
        out_specs=[qs, stat],
        out_shape=[jax.ShapeDtypeStruct((SWA_Q_HEADS * HEAD_DIM, S), F32), jax.ShapeDtypeStruct((SWA_Q_HEADS, S), F32)],
        compiler_params=_params("parallel"),
    )(sinks, qkv, qkv, qkv, qkv, qkv, bias_g)


def _swa_bwd(qkv, bias_g, sinks, do, lse, delta, *, name):
    S = qkv.shape[1]
    W = WINDOW
    qs, kp, kc, vp, vc, bs, stat, sk = _swa_specs()
    scale = HEAD_DIM ** -0.5
    kv_rows = SWA_KV_HEADS * HEAD_DIM
    gs = range(SWA_KV_HEADS)

    def body(sink_ref, q_ref, kp_ref, kc_ref, vp_ref, vc_ref, b_ref, do_ref, lse_ref, dl_ref,
             dq_ref, dkv_ref, db_ref, dsk_ref):
        i = pl.program_id(0)

        @pl.when(i == 0)
        def _():
            dkv_ref[...] = jnp.zeros_like(dkv_ref)
            db_ref[...] = jnp.zeros_like(db_ref)
            dsk_ref[...] = jnp.zeros_like(dsk_ref)

        masks = _swa_masks(i)
        prev = pl.ds(pl.multiple_of(jnp.maximum(i - 1, 0) * W, W), W)
        cur = pl.ds(pl.multiple_of(i * W, W), W)
        sc = [_swa_scores(g, q_ref, kp_ref, kc_ref, b_ref, masks) for g in gs]
        dog = [_group_lanes(do_ref, g, HEAD_DIM) for g in gs]
        lse = [_group_lanes(lse_ref, g, 1) for g in gs]
        dl = [_group_lanes(dl_ref, g, 1) for g in gs]
        p_p = [jnp.exp(sc[g][2] - lse[g]) for g in gs]
        p_c = [jnp.exp(sc[g][3] - lse[g]) for g in gs]
        ds_p = [p_p[g] * (_dot(vp_ref[sc[g][1], :], dog[g], TN) - dl[g]) for g in gs]
        ds_c = [p_c[g] * (_dot(vc_ref[sc[g][1], :], dog[g], TN) - dl[g]) for g in gs]
        for g in gs:
            db_ref[g, 0:W, :] += ds_p[g]
            db_ref[g, W:2 * W, :] += ds_c[g]
            dsk = jnp.exp(_sink_row(sink_ref, g) - lse[g]) * dl[g]
            for j in range(SWA_GROUP):
                h = g * SWA_GROUP + j
                dsk_ref[h:h + 1, :] -= jnp.broadcast_to(jnp.sum(dsk[:, j * W:(j + 1) * W], axis=1, keepdims=True), (1, LANES))
        dsb_p = [d.astype(BF16) for d in ds_p]
        dsb_c = [d.astype(BF16) for d in ds_c]
        for g in gs:
            rows = sc[g][1]
            dq = (_dot(kp_ref[rows, :], dsb_p[g]) + _dot(kc_ref[rows, :], dsb_c[g])) * scale
            for j in range(SWA_GROUP):
                h = g * SWA_GROUP + j
                dq_ref[h * HEAD_DIM:(h + 1) * HEAD_DIM, :] = dq[:, j * W:(j + 1) * W]
        for g in gs:
            rows = sc[g][1]
            vrows = slice(kv_rows + rows.start, kv_rows + rows.stop)
            dkv_ref[rows, prev] += _dot(sc[g][0], dsb_p[g], NT) * scale
            dkv_ref[rows, cur] += _dot(sc[g][0], dsb_c[g], NT) * scale
            dkv_ref[vrows, prev] += _dot(dog[g], p_p[g].astype(BF16), NT)
            dkv_ref[vrows, cur] += _dot(dog[g], p_c[g].astype(BF16), NT)

    return pl.pallas_call(
        body, name=name, grid=(S // W,),
        in_specs=[sk, qs, kp, kc, vp, vc, bs, qs, stat, stat],
        out_specs=[qs, pl.BlockSpec((2 * kv_rows, S), lambda i: (0, 0)), bs, pl.BlockSpec((SWA_Q_HEADS, LANES), lambda i: (0, 0))],
        out_shape=[jax.ShapeDtypeStruct((SWA_Q_HEADS * HEAD_DIM, S), F32), jax.ShapeDtypeStruct((2 * kv_rows, S), F32),
                   jax.ShapeDtypeStruct((SWA_KV_HEADS, 2 * W, GW), F32), jax.ShapeDtypeStruct((SWA_Q_HEADS, LANES), F32)],
        compiler_params=_params("arbitrary"),
    )(sinks, qkv, qkv, qkv, qkv, qkv, bias_g, do, lse, delta)


def _rel_onehot_t():
    qi = jnp.arange(WINDOW, dtype=jnp.int32)[None, :] + WINDOW
    kj = jnp.arange(2 * WINDOW, dtype=jnp.int32)[:, None]
    dist = qi - kj
    max_exact = REL_BUCKETS // 2
    d = jnp.maximum(dist, 0)
    log_ratio = jnp.log(jnp.maximum(d, 1).astype(F32) / max_exact) / math.log(REL_MAX_DIST / max_exact)
    large = jnp.minimum(max_exact + (log_ratio * (REL_BUCKETS - max_exact)).astype(jnp.int32), REL_BUCKETS - 1)
    bucket = jnp.where(d < max_exact, d, large).reshape(-1)
    return (bucket[None, :] == jnp.arange(REL_BUCKETS, dtype=jnp.int32)[:, None]).astype(BF16)


def _bias_table(rel_bias_t, onehot_t):
    Hq, NB = rel_bias_t.shape
    N = onehot_t.shape[1]
    tn = _tile(N, 4096)

    def body(r_ref, oh_ref, o_ref):
        oh = oh_ref[...]
        a1, a2, a3 = _split3(r_ref[...])
        o_ref[...] = _dot(a1, oh) + _dot(a2, oh) + _dot(a3, oh)

    return pl.pallas_call(
        body, name="rel_bias_table", grid=(N // tn,),
        in_specs=[pl.BlockSpec((Hq, NB), lambda j: (0, 0)), pl.BlockSpec((NB, tn), lambda j: (0, j))],
        out_specs=pl.BlockSpec((Hq, tn), lambda j: (0, j)),
        out_shape=jax.ShapeDtypeStruct((Hq, N), F32),
        compiler_params=_params("parallel"),
    )(rel_bias_t, onehot_t)


def _bias_table_bwd(dbias, onehot_t):
    L, Hq, N = dbias.shape
    NB = onehot_t.shape[0]
    tn = _tile(N, 4096)

    def body(d_ref, oh_ref, o_ref):
        @pl.when(pl.program_id(0) == 0)
        def _():
            o_ref[...] = jnp.zeros_like(o_ref)

        d = d_ref[0]
        for l in range(1, L):
            d = d + d_ref[l]
        oh = oh_ref[...]
        a1, a2, a3 = _split3(d)
        o_ref[...] += _dot(a1, oh, NT) + _dot(a2, oh, NT) + _dot(a3, oh, NT)

    return pl.pallas_call(
        body, name="rel_bias_bwd", grid=(N // tn,),
        in_specs=[pl.BlockSpec((L, Hq, tn), lambda j: (0, 0, j)), pl.BlockSpec((NB, tn), lambda j: (0, j))],
        out_specs=pl.BlockSpec((Hq, NB), lambda j: (0, 0)),
        out_shape=jax.ShapeDtypeStruct((Hq, NB), F32),
        compiler_params=_params("arbitrary"),
    )(dbias, onehot_t)


def _gate_fwd(lat, fb_col, *, name):
    S = lat.shape[1]
    tn = _tile(S, 256)

    def body(z_ref, fb_ref, o_ref, carry):
        @pl.when(pl.program_id(0) == 0)
        def _():
            carry[...] = jnp.zeros_like(carry)

        z = z_ref[...] + fb_ref[...]
        lf = jnp.minimum(z, 0.0) - jnp.log1p(jnp.exp(-jnp.abs(z)))
        r = lax.broadcasted_iota(jnp.int32, (tn, tn), 0)
        c = lax.broadcasted_iota(jnp.int32, (tn, tn), 1)
        tri = (r <= c).astype(BF16)
        a1, a2, a3 = _split3(lf)
        cum = _dot(a1, tri) + _dot(a2, tri) + _dot(a3, tri) + carry[:, 0:1]
        o_ref[...] = cum
        carry[...] = jnp.broadcast_to(cum[:, tn - 1:tn], carry.shape)

    return pl.pallas_call(
        body, name=name, grid=(S // tn,),
        in_specs=[pl.BlockSpec((GATE_ROWS, tn), lambda i: (0, i)), pl.BlockSpec((GATE_ROWS, 1), lambda i: (0, 0))],
        out_specs=pl.BlockSpec((GATE_ROWS, tn), lambda i: (0, i)),
        out_shape=jax.ShapeDtypeStruct((GATE_ROWS, S), F32),
        scratch_shapes=[pltpu.VMEM((GATE_ROWS, LANES), F32)],
        compiler_params=_params("arbitrary"),
    )(lat, fb_col)


def _gate_bwd(lat, fb_col, dF, *, name):
    S = lat.shape[1]
    tn = _tile(S, 256)
    nt = S // tn

    def body(z_ref, fb_ref, df_ref, dz_ref, dfb_ref, carry):
        @pl.when(pl.program_id(0) == 0)
        def _():
            carry[...] = jnp.zeros_like(carry)
            dfb_ref[...] = jnp.zeros_like(dfb_ref)

        r = lax.broadcasted_iota(jnp.int32, (tn, tn), 0)
        c = lax.broadcasted_iota(jnp.int32, (tn, tn), 1)
        tri = (r >= c).astype(BF16)
        a1, a2, a3 = _split3(df_ref[...])
        dlf = _dot(a1, tri) + _dot(a2, tri) + _dot(a3, tri) + carry[:, 0:1]
        carry[...] = jnp.broadcast_to(dlf[:, 0:1], carry.shape)
        z = z_ref[...] + fb_ref[...]
        row = lax.broadcasted_iota(jnp.int32, (GATE_ROWS, tn), 0)
        dz = jnp.where(row < FOX_HEADS, dlf / (1.0 + jnp.exp(z)), 0.0)
        dz_ref[...] = dz
        dfb_ref[...] += jnp.sum(dz, axis=1, keepdims=True)

    blk = pl.BlockSpec((GATE_ROWS, tn), lambda i: (0, nt - 1 - i))
    vec = pl.BlockSpec((GATE_ROWS, 1), lambda i: (0, 0))
    return pl.pallas_call(
        body, name=name, grid=(nt,),
        in_specs=[blk, vec, blk], out_specs=[blk, vec],
        out_shape=[jax.ShapeDtypeStruct((GATE_ROWS, S), F32), jax.ShapeDtypeStruct((GATE_ROWS, 1), F32)],
        scratch_shapes=[pltpu.VMEM((GATE_ROWS, LANES), F32)],
        compiler_params=_params("arbitrary"),
    )(lat, fb_col, dF)


def _rope_tables(S):
    pos = jnp.arange(S, dtype=F32)
    inv_freq = ROPE_THETA ** (-(jnp.arange(MLA_ROPE // 2, dtype=F32) * 2.0 / MLA_ROPE))
    ang = pos[:, None] * inv_freq[None, :]
    cos, sin = jnp.cos(ang).T, jnp.sin(ang).T
    z16 = jnp.zeros_like(cos)

    def slab(lo, fill):
        def put(first, second, f):
            return jnp.concatenate([jnp.full((lo, S), f, F32), first, second, jnp.full((LANES - lo - MLA_ROPE, S), f, F32)], axis=0)
        return put(cos, cos, fill), put(-sin, z16, 0.0), put(z16, sin, 0.0)

    tq = tuple(jnp.tile(t, (MLA_HEADS, 1)) for t in slab(MLA_NOPE, 1.0))
    return tq, slab(0, 0.0)


def _rope(x, c, s1, s2):
    n = x.shape[0]
    half = MLA_ROPE // 2
    return x * c + pltpu.roll(x, n - half, 0) * s1 + pltpu.roll(x, half, 0) * s2


def _rope_t(dy, c, s1, s2):
    n = dy.shape[0]
    half = MLA_ROPE // 2
    return dy * c + pltpu.roll(dy * s1, half, 0) + pltpu.roll(dy * s2, n - half, 0)


KR_SLAB0 = MLA_Q_RANK + MLA_KV_RANK


def _mla_prep_fwd(lat, g_q, g_kv, w_uq_t, w_ukv_t, tq, tmisc, *, name):
    S = lat.shape[1]
    tn = _tile(S, 512)
    QW = MLA_HEADS * MLA_PAD

    def body(lat_ref, gq_ref, gkv_ref, wq_ref, wkv_ref, c_ref, s1_ref, s2_ref, cm_ref, s1m_ref, s2m_ref,
             nq_ref, nkv_ref, q_ref, k_ref, v_ref):
        x = pltpu.roll(lat_ref[...], LAT_ROWS - LAT_SHIFT, 0)
        nq = _col_rms(x[0:MLA_Q_RANK, :], gq_ref[...]).astype(BF16)
        nkv = _col_rms(x[MLA_Q_RANK:KR_SLAB0, :], gkv_ref[...]).astype(BF16)
        nq_ref[...] = nq
        nkv_ref[...] = nkv
        q = _rope(_dot(wq_ref[...], nq), c_ref[...], s1_ref[...], s2_ref[...])
        q_ref[...] = (q * (MLA_SCALE * LOG2E)).astype(BF16)
        kv = _dot(wkv_ref[...], nkv).astype(BF16)
        kr = _rope(x[KR_SLAB0:LAT_ROWS, :], cm_ref[...], s1m_ref[...], s2m_ref[...]).astype(BF16)
        for h in range(MLA_HEADS):
            k_ref[h * MLA_PAD:h * MLA_PAD + MLA_NOPE, :] = kv[h * LANES:h * LANES + MLA_NOPE, :]
            k_ref[h * MLA_PAD + MLA_NOPE:(h + 1) * MLA_PAD, :] = kr[0:MLA_PAD - MLA_NOPE, :]
            v_ref[h * HEAD_DIM:(h + 1) * HEAD_DIM, :] = kv[h * LANES + MLA_NOPE:(h + 1) * LANES, :]

    def col(rows):
        return pl.BlockSpec((rows, tn), lambda i: (0, i))

    def full(a):
        return pl.BlockSpec(a.shape, lambda i: (0, 0))

    return pl.pallas_call(
        body, name=name, grid=(S // tn,),
        in_specs=[col(LAT_ROWS), full(g_q), full(g_kv), full(w_uq_t), full(w_ukv_t),
                  col(QW), col(QW), col(QW), col(LANES), col(LANES), col(LANES)],
        out_specs=[col(MLA_Q_RANK), col(MLA_KV_RANK), col(QW), col(QW), col(MLA_HEADS * HEAD_DIM)],
        out_shape=[jax.ShapeDtypeStruct((MLA_Q_RANK, S), BF16), jax.ShapeDtypeStruct((MLA_KV_RANK, S), BF16),
                   jax.ShapeDtypeStruct((QW, S), BF16), jax.ShapeDtypeStruct((QW, S), BF16),
                   jax.ShapeDtypeStruct((MLA_HEADS * HEAD_DIM, S), BF16)],
        compiler_params=_params("parallel"),
    )(lat, g_q, g_kv, w_uq_t, w_ukv_t, *tq, *tmisc)


def _mla_prep_bwd(lat, nq, nkv, g_q, g_kv, w_uq_p, w_ukv, tq, tmisc, dq, dk, dv, dflog, *, name):
    S = lat.shape[1]
    tn = _tile(S, 512)
    QW = MLA_HEADS * MLA_PAD

    def body(lat_ref, nq_ref, nkv_ref, gq_ref, gkv_ref, wq_ref, wkv_ref, c_ref, s1_ref, s2_ref,
             cm_ref, s1m_ref, s2m_ref, dq_ref, dk_ref, dv_ref, dfl_ref,
             dlat_ref, dwq_ref, dwkv_ref, dgq_ref, dgkv_ref, y_s):
        @pl.when(pl.program_id(0) == 0)
        def _():
            dwq_ref[...] = jnp.zeros_like(dwq_ref)
            dwkv_ref[...] = jnp.zeros_like(dwkv_ref)
            dgq_ref[...] = jnp.zeros_like(dgq_ref)
            dgkv_ref[...] = jnp.zeros_like(dgkv_ref)

        x = pltpu.roll(lat_ref[...], LAT_ROWS - LAT_SHIFT, 0)
        dqm = _rope_t(dq_ref[...], c_ref[...], s1_ref[...], s2_ref[...]).astype(BF16)
        dwq_ref[...] += _dot(dqm, nq_ref[...], NT)
        dx, dg = _col_rms_bwd(x[0:MLA_Q_RANK, :], gq_ref[...], _dot(wq_ref[...], dqm))
        y_s[0:MLA_Q_RANK, :] = dx
        dgq_ref[...] += dg
        dkv = jnp.concatenate(
            [part for h in range(MLA_HEADS)
             for part in (dk_ref[h * MLA_PAD:h * MLA_PAD + MLA_NOPE, :], dv_ref[h * HEAD_DIM:(h + 1) * HEAD_DIM, :])],
            axis=0).astype(BF16)
        dwkv_ref[...] += _dot(dkv, nkv_ref[...], NT)
        dx, dg = _col_rms_bwd(x[MLA_Q_RANK:KR_SLAB0, :], gkv_ref[...], _dot(wkv_ref[...], dkv))
        y_s[MLA_Q_RANK:KR_SLAB0, :] = dx
        dgkv_ref[...] += dg
        dkr = dk_ref[MLA_NOPE:MLA_PAD, :]
        for h in range(1, MLA_HEADS):
            dkr = dkr + dk_ref[h * MLA_PAD + MLA_NOPE:(h + 1) * MLA_PAD, :]
        dkr = jnp.concatenate([dkr, jnp.zeros((MLA_NOPE, tn), F32)], axis=0)
        y_s[KR_SLAB0:LAT_ROWS, :] = _rope_t(dkr, cm_ref[...], s1m_ref[...], s2m_ref[...])
        y = pltpu.roll(y_s[...], LAT_SHIFT, 0)
        row = lax.broadcasted_iota(jnp.int32, (LAT_ROWS, tn), 0)
        dfl = jnp.concatenate([dfl_ref[...], jnp.zeros((LAT_ROWS - GATE_ROWS, tn), F32)], axis=0)
        dlat_ref[...] = jnp.where(row < LAT_SHIFT, dfl, y).astype(BF16)

    def col(rows):
        return pl.BlockSpec((rows, tn), lambda i: (0, i))

    def full(a):
        return pl.BlockSpec(a.shape, lambda i: (0, 0))

    def acc(r, c):
        return pl.BlockSpec((r, c), lambda i: (0, 0))

    return pl.pallas_call(
        body, name=name, grid=(S // tn,),
        in_specs=[col(LAT_ROWS), col(MLA_Q_RANK), col(MLA_KV_RANK), full(g_q), full(g_kv),
                  full(w_uq_p), full(w_ukv), col(QW), col(QW), col(QW), col(LANES), col(LANES), col(LANES),
                  col(QW), col(QW), col(MLA_HEADS * HEAD_DIM), col(GATE_ROWS)],
        out_specs=[col(LAT_ROWS), acc(QW, MLA_Q_RANK), acc(QW, MLA_KV_RANK), acc(MLA_Q_RANK, 1), acc(MLA_KV_RANK, 1)],
        out_shape=[jax.ShapeDtypeStruct((LAT_ROWS, S), BF16), jax.ShapeDtypeStruct((QW, MLA_Q_RANK), F32),
                   jax.ShapeDtypeStruct((QW, MLA_KV_RANK), F32), jax.ShapeDtypeStruct((MLA_Q_RANK, 1), F32),
                   jax.ShapeDtypeStruct((MLA_KV_RANK, 1), F32)],
        scratch_shapes=[pltpu.VMEM((LAT_ROWS, tn), F32)],
        compiler_params=_params("arbitrary"),
    )(lat, nq, nkv, g_q, g_kv, w_uq_p, w_ukv, *tq, *tmisc, dq, dk, dv, dflog)


def _dproj_cast(dqa, dkva, dqf, dkf, dvf, dlat, *, name):
    S = dqa.shape[1]
    tn = _tile(S, 512)
    parts = (dqa, dkva, dqf, dkf, dvf, dlat)

    def body(*refs):
        o_ref = refs[-1]
        r0 = 0
        for ref in refs[:-1]:
            n = ref.shape[0]
            o_ref[r0:r0 + n, :] = ref[...].astype(BF16)
            r0 += n

    return pl.pallas_call(
        body, name=name, grid=(S // tn,),
        in_specs=[pl.BlockSpec((p.shape[0], tn), lambda i: (0, i)) for p in parts],
        out_specs=pl.BlockSpec((IN_ROWS, tn), lambda i: (0, i)),
        out_shape=jax.ShapeDtypeStruct((IN_ROWS, S), BF16),
        compiler_params=_params("parallel"),
    )(*parts)


GELU_C = math.sqrt(2.0 / math.pi)
GELU_A = 0.044715


HALO = 16


def _shift_down(a, k, fill):
    r = pltpu.roll(a, k, 0)
    row = lax.broadcasted_iota(jnp.int32, (8, a.shape[1]), 0)
    head = r[0:8, :]
    for i in range(k):
        head = jnp.where(row == i, fill[len(fill) - k + i], head)
    return jnp.concatenate([head, r[8:, :]], axis=0)


def _shift_up(d, k, fill):
    n = d.shape[0]
    r = pltpu.roll(d, n - k, 0)
    row = lax.broadcasted_iota(jnp.int32, (8, d.shape[1]), 0)
    tail = r[n - 8:n, :]
    for i in range(k):
        tail = jnp.where(row == 8 - k + i, fill[i], tail)
    return jnp.concatenate([r[0:n - 8, :], tail], axis=0)


def _conv_taps(a, before, w_ref, b_ref):
    a1 = _shift_down(a, 1, before)
    a2 = _shift_down(a, 2, before)
    return ((b_ref[...] + w_ref[0:1, :] * a2) + w_ref[1:2, :] * a1) + w_ref[2:3, :] * a


def _rows_before(halo_ref, first):
    h = halo_ref[HALO - 2:HALO, :].astype(F32)
    return jnp.where(first, 0.0, h[0:1, :]), jnp.where(first, 0.0, h[1:2, :])


def _conv_specs(S, tm, tc, nc):
    hb = tm // HALO
    main = lambda off: pl.BlockSpec((tm, tc), lambda j, i: (i, j + off))
    prev = lambda off: pl.BlockSpec((HALO, tc), lambda j, i: (jnp.maximum(i * hb - 1, 0), j + off))
    wspec = lambda off: pl.BlockSpec((3, tc), lambda j, i: (0, j + off))
    bspec = lambda off: pl.BlockSpec((1, tc), lambda j, i: (0, j + off))
    return main, prev, wspec, bspec


def _conv_geglu_fwd(a, conv_w, conv_b, *, name):
    S = a.shape[0]
    tm, tc = _tile(S, 512), _tile(D_FF, 1408)
    nc = D_FF // tc
    main, prev, wspec, bspec = _conv_specs(S, tm, tc, nc)

    def body(ag_ref, au_ref, hg_ref, hu_ref, wg_ref, wu_ref, bg_ref, bu_ref, u_ref, z_ref):
        first = pl.program_id(1) == 0
        gate = _conv_taps(ag_ref[...].astype(F32), _rows_before(hg_ref, first), wg_ref, bg_ref)
        up = _conv_taps(au_ref[...].astype(F32), _rows_before(hu_ref, first), wu_ref, bu_ref)
        u_ref[0] = gate
        u_ref[1] = up
        cdf = 0.5 * (1.0 + jnp.tanh(GELU_C * (gate + GELU_A * (gate * gate * gate))))
        z_ref[...] = (gate * cdf * up).astype(BF16)

    return pl.pallas_call(
        body, name=name, grid=(nc, S // tm),
        in_specs=[main(0), main(nc), prev(0), prev(nc), wspec(0), wspec(nc), bspec(0), bspec(nc)],
        out_specs=[pl.BlockSpec((2, tm, tc), lambda j, i: (0, i, j)), pl.BlockSpec((tm, tc), lambda j, i: (i, j))],
        out_shape=[jax.ShapeDtypeStruct((2, S, D_FF), F32), jax.ShapeDtypeStruct((S, D_FF), BF16)],
        compiler_params=_params("parallel", "arbitrary"),
    )(a, a, a, a, conv_w, conv_w, conv_b, conv_b)


def _geglu_bwd(gate, up, dz):
    g2x = gate * gate
    th = jnp.tanh(GELU_C * (gate + GELU_A * (g2x * gate)))
    cdf = 0.5 * (1.0 + th)
    dgelu = cdf + gate * (0.5 * (1.0 - th * th) * (GELU_C * (1.0 + 3.0 * GELU_A * g2x)))
    return dz * up * dgelu, dz * (gate * cdf)


def _conv_geglu_bwd(a, u, conv_w, dz, *, name):
    S = a.shape[0]
    tm, tc = _tile(S, 512), _tile(D_FF, 1408)
    nc = D_FF // tc
    nr = S // tm
    main, _, wspec, _ = _conv_specs(S, tm, tc, nc)
    hb = tm // 8

    def body(ag_ref, au_ref, u_ref, un_ref, wg_ref, wu_ref, dz_ref, dzn_ref, da_ref, dw_ref, db_ref):
        i = pl.program_id(1)
        last = i == nr - 1

        @pl.when(i == 0)
        def _():
            dw_ref[...] = jnp.zeros_like(dw_ref)
            db_ref[...] = jnp.zeros_like(db_ref)

        dus = _geglu_bwd(u_ref[0], u_ref[1], dz_ref[...])
        dus_n = _geglu_bwd(un_ref[0], un_ref[1], dzn_ref[...])
        for half, a_ref, w_ref in ((0, ag_ref, wg_ref), (1, au_ref, wu_ref)):
            du, du_n = dus[half], dus_n[half]
            after = (jnp.where(last, 0.0, du_n[0:1, :]), jnp.where(last, 0.0, du_n[1:2, :]))
            shifted = (_shift_up(du, 2, after), _shift_up(du, 1, after), du)
            da_ref[half] = (w_ref[2:3, :] * du + w_ref[1:2, :] * shifted[1] + w_ref[0:1, :] * shifted[0]).astype(BF16)
            af = a_ref[...].astype(F32)
            for tap in range(3):
                dw_ref[half, tap:tap + 1, :] += jnp.sum(shifted[tap] * af, axis=0, keepdims=True)
            db_ref[half] += jnp.sum(du, axis=0, keepdims=True)

    nxt8 = lambda j, i: (0, jnp.minimum((i + 1) * hb, S // 8 - 1), j)
    return pl.pallas_call(
        body, name=name, grid=(nc, nr),
        in_specs=[main(0), main(nc), pl.BlockSpec((2, tm, tc), lambda j, i: (0, i, j)), pl.BlockSpec((2, 8, tc), nxt8),
                  wspec(0), wspec(nc), pl.BlockSpec((tm, tc), lambda j, i: (i, j)),
                  pl.BlockSpec((8, tc), lambda j, i: (jnp.minimum((i + 1) * hb, S // 8 - 1), j))],
        out_specs=[pl.BlockSpec((2, tm, tc), lambda j, i: (0, i, j)), pl.BlockSpec((2, 3, tc), lambda j, i: (0, 0, j)),
                   pl.BlockSpec((2, 1, tc), lambda j, i: (0, 0, j))],
        out_shape=[jax.ShapeDtypeStruct((2, S, D_FF), BF16), jax.ShapeDtypeStruct((2, 3, D_FF), F32),
                   jax.ShapeDtypeStruct((2, 1, D_FF), F32)],
        compiler_params=_params("parallel", "arbitrary"),
    )(a, a, u, u, conv_w, conv_w, dz, dz)


ROW_BLOCK_BYTES = 1536 * 1024


def _row_tile(rows, cols):
    return rows if rows * cols * 4 <= ROW_BLOCK_BYTES else _tile(rows, ROW_TILE)


def _adamw_update(w, g, m, v):
    m = ADAM_B1 * m + (1.0 - ADAM_B1) * g
    v = ADAM_B2 * v + (1.0 - ADAM_B2) * jnp.square(g)
    m_hat = m / (1.0 - ADAM_B1 ** ADAM_STEP)
    v_hat = v / (1.0 - ADAM_B2 ** ADAM_STEP)
    return -ADAM_LR * (m_hat / (jnp.sqrt(v_hat) + ADAM_EPS) + ADAM_WD * w), m, v


def _adamw(w, g, m, v, *, name):
    L, A, B = w.shape
    ta = _tile(A, ROW_TILE)

    def body(w_ref, g_ref, m_ref, v_ref, d_ref, mo_ref, vo_ref):
        d_ref[...], mo_ref[...], vo_ref[...] = _adamw_update(w_ref[...], g_ref[...], m_ref[...], v_ref[...])

    blk = pl.BlockSpec((None, ta, B), lambda l, i: (l, i, 0))
    shp = jax.ShapeDtypeStruct((L, A, B), F32)
    return pl.pallas_call(
        body, name=name, grid=(L, A // ta),
        in_specs=[blk] * 4, out_specs=[blk] * 3, out_shape=[shp] * 3,
        compiler_params=_params("parallel", "parallel"),
    )(w, g, m, v)


def _scalar(v):
    return jnp.reshape(v, (1,)).astype(jnp.int32)


def _adamw_halves(w, g_mine, g_other, m, v, *, name):
    L, A, B = w.shape
    ta = _row_tile(A // 2, B)
    nb = A // 2 // ta

    def body(c_ref, w_ref, gm_ref, go_ref, m_ref, v_ref, g_ref, d_ref, mo_ref, vo_ref):
        g = jnp.where(pl.program_id(1) // nb == c_ref[0], gm_ref[...], go_ref[...])
        g_ref[...] = g
        d_ref[...], mo_ref[...], vo_ref[...] = _adamw_update(w_ref[...], g, m_ref[...], v_ref[...])

    blk = pl.BlockSpec((None, ta, B), lambda l, i, c_ref: (l, i, 0))
    half = pl.BlockSpec((None, ta, B), lambda l, i, c_ref: (l, i % nb, 0))
    shp = jax.ShapeDtypeStruct((L, A, B), F32)
    return pl.pallas_call(
        body, name=name,
        grid_spec=pltpu.PrefetchScalarGridSpec(num_scalar_prefetch=1, grid=(L, A // ta),
                                               in_specs=[blk, half, half, blk, blk], out_specs=[blk] * 4),
        out_shape=[shp] * 4,
        compiler_params=_params("parallel", "parallel"),
    )(_scalar(lax.axis_index("c")), w, g_mine, g_other, m, v)


def _chip_index():
    return 2 * lax.axis_index("x") + lax.axis_index("y")


def _pair_sum(g, recv, *, name):
    n, A, B = g.shape
    ta = _row_tile(A // 2, B)
    nb = A // 2 // ta

    def body(c_ref, g_ref, r_ref, o_ref):
        o_ref[...] = g_ref[...] + r_ref[...]

    return pl.pallas_call(
        body, name=name,
        grid_spec=pltpu.PrefetchScalarGridSpec(
            num_scalar_prefetch=1, grid=(n, nb),
            in_specs=[pl.BlockSpec((None, ta, B), lambda s, r, c_ref: (s, c_ref[0] * nb + r, 0)),
                      pl.BlockSpec((None, ta, B), lambda s, r, c_ref: (s, r, 0))],
            out_specs=pl.BlockSpec((None, ta, B), lambda s, r, c_ref: (s, r, 0))),
        out_shape=jax.ShapeDtypeStruct((n, A // 2, B), F32),
        compiler_params=_params("parallel", "parallel"),
    )(_scalar(lax.axis_index("c")), g, recv)


def _chip_sum(landed, own, *, name):
    n, A2, B = landed.shape
    ta = _row_tile(A2, B)

    def body(me_ref, *refs):
        slots, own_ref, o_ref = refs[:n], refs[n], refs[n + 1]
        parts = [jnp.where(me_ref[0] == s, own_ref[...], slots[s][...]) for s in range(n)]
        o_ref[...] = ((parts[0] + parts[1]) + parts[2]) + parts[3]

    def slot(s):
        return pl.BlockSpec((None, ta, B), lambda r, me_ref: (jnp.where(me_ref[0] == s, (s + 1) % n, s), r, 0))

    return pl.pallas_call(
        body, name=name,
        grid_spec=pltpu.PrefetchScalarGridSpec(
            num_scalar_prefetch=1, grid=(A2 // ta,),
            in_specs=[slot(s) for s in range(n)] + [pl.BlockSpec((None, ta, B), lambda r, me_ref: (me_ref[0], r, 0))],
            out_specs=pl.BlockSpec((ta, B), lambda r, me_ref: (r, 0))),
        out_shape=jax.ShapeDtypeStruct((A2, B), F32),
        compiler_params=_params("parallel"),
    )(_scalar(_chip_index()), *([landed] * n), own)


HBM_SPEC = pl.BlockSpec(memory_space=pl.ANY)
COMM_PARAMS = pltpu.CompilerParams(has_side_effects=True)


def _mesh_pos():
    return lax.axis_index("x"), lax.axis_index("y"), lax.axis_index("c")


def _other_chips(x, y):
    return [(1 - x, y), (x, 1 - y), (1 - x, 1 - y)]


def _remote(src, dst, send_sems, recv_sems, k, to):
    return pltpu.make_async_remote_copy(src_ref=src, dst_ref=dst, send_sem=send_sems.at[k], recv_sem=recv_sems.at[k],
                                        device_id=to, device_id_type=MESH)


def _place_own(gathered, shards, *, name):
    n = len(shards)

    def body(me_ref, *refs):
        for s_ref, o_ref in zip(refs[:n], refs[2 * n:]):
            o_ref[...] = s_ref[...]

    return pl.pallas_call(
        body, name=name,
        grid_spec=pltpu.PrefetchScalarGridSpec(
            num_scalar_prefetch=1, grid=(1,),
            in_specs=[pl.BlockSpec(s.shape, lambda i, me_ref: (0, 0)) for s in shards] + [HBM_SPEC] * n,
            out_specs=[pl.BlockSpec((None,) + s.shape, lambda i, me_ref: (me_ref[0], 0, 0)) for s in shards]),
        out_shape=[jax.ShapeDtypeStruct(g.shape, g.dtype) for g in gathered],
        input_output_aliases={1 + n + k: k for k in range(n)},
        compiler_params=_params("arbitrary"),
    )(_scalar(_chip_index()), *shards, *gathered)


def _half_rows(rows, c, align=8):
    assert (rows // 2) % align == 0
    return pl.ds(pl.multiple_of(c * (rows // 2), align), rows // 2)


BF16_ROWS = 16


def _halved(rows):
    return rows % (2 * BF16_ROWS) == 0


def _gather_copies(srcs, lands, send_sems, recv_sems):
    x, y, c = _mesh_pos()
    me = 2 * x + y
    out = []
    for k in range(len(srcs)):
        a = srcs[k].shape[0]
        rows = _half_rows(a, c, BF16_ROWS) if _halved(a) else pl.ds(0, a)
        for j, (px, py) in enumerate(_other_chips(x, y)):
            send = _remote(srcs[k].at[rows], lands[k].at[me, rows], send_sems, recv_sems, 3 * k + j, (px, py, c))
            recv = _remote(srcs[k].at[rows], lands[k].at[2 * px + py, rows], send_sems, recv_sems, 3 * k + j, (px, py, c))
            out.append((send, recv))
    return out


def _gather_start(srcs):
    nu = len(srcs)
    sizes = [len(su) for su in srcs]
    offs = [2 * sum(sizes[:u]) for u in range(nu + 1)]
    lands = [[lax.empty((N_CHIPS,) + s.shape, s.dtype) for s in su] for su in srcs]
    flat = [a for u in range(nu) for a in srcs[u] + lands[u]]

    def body(*refs):
        bufs, sems, token = refs[:len(flat)], refs[len(flat):len(flat) + 2 * nu], refs[-1]
        for u, n in enumerate(sizes):
            mine = bufs[offs[u]:offs[u + 1]]
            for send, _ in _gather_copies(mine[:n], mine[n:], sems[2 * u], sems[2 * u + 1]):
                send.start()
        token[...] = jnp.zeros_like(token)

    res = pl.pallas_call(
        body, name="weight_gather_start",
        in_specs=[HBM_ONLY] * len(flat),
        out_specs=[SEM_SPEC] * (2 * nu) + [HBM_ONLY] * len(flat) + [pl.BlockSpec(memory_space=pltpu.VMEM)],
        out_shape=[pltpu.SemaphoreType.DMA((3 * n,)) for n in sizes for _ in (0, 1)] + [pltpu.HBM(a.shape, a.dtype) for a in flat]
        + [jax.ShapeDtypeStruct((8, LANES), F32)],
        input_output_aliases={i: 2 * nu + i for i in range(len(flat))},
        compiler_params=SPLIT_PARAMS,
    )(*[pltpu.with_memory_space_constraint(a, pltpu.HBM) for a in flat])
    bufs = res[2 * nu:2 * nu + len(flat)]
    state = [(res[2 * u], res[2 * u + 1], list(bufs[offs[u]:offs[u] + n]), list(bufs[offs[u] + n:offs[u + 1]]))
             for u, n in enumerate(sizes)]
    return state, res[-1][0:1, 0:1]


def _gather_wait(state, after, *, name):
    send_sems, recv_sems, srcs, lands = state
    n = len(srcs)

    def body(*refs):
        for send, recv in _gather_copies(refs[:n], refs[n:2 * n], refs[2 * n], refs[2 * n + 1]):
            send.wait_send()
            recv.wait_recv()

    res = pl.pallas_call(
        body, name=name,
        in_specs=[HBM_ONLY] * (2 * n) + [SEM_SPEC, SEM_SPEC, HBM_SPEC],
        out_specs=[HBM_ONLY] * (2 * n),
        out_shape=[pltpu.HBM(a.shape, a.dtype) for a in srcs + lands],
        input_output_aliases={i: i for i in range(2 * n)},
        compiler_params=SPLIT_PARAMS,
    )(*srcs, *lands, send_sems, recv_sems, after)
    return list(res[:n]), list(res[n:])


def _gather_forward(lands, *, name):
    n = len(lands)

    def body(*refs):
        bufs, outs = refs[:n], refs[n:2 * n]
        send_sems, recv_sems = refs[2 * n:]
        x, y, c = _mesh_pos()
        copies, waits = [], []
        for k in range(n):
            a = lands[k].shape[1]
            if not _halved(a):
                continue
            for j, (px, py) in enumerate(_other_chips(x, y)):
                mine = 2 * px + py, _half_rows(a, c, BF16_ROWS)
                copies.append(_remote(bufs[k].at[mine], outs[k].at[mine], send_sems, recv_sems, 3 * k + j, (x, y, 1 - c)))
                lands_here = outs[k].at[2 * px + py, _half_rows(a, 1 - c, BF16_ROWS)]
                waits.append(_remote(lands_here, lands_here, send_sems, recv_sems, 3 * k + j, (x, y, 1 - c)))
        for cp in copies:
            cp.start()
        for cp in waits:
            cp.wait_recv()
        for cp in copies:
            cp.wait_send()

    return pl.pallas_call(
        body, name=name,
        in_specs=[HBM_SPEC] * n, out_specs=[HBM_SPEC] * n,
        out_shape=[jax.ShapeDtypeStruct(a.shape, a.dtype) for a in lands],
        scratch_shapes=[pltpu.SemaphoreType.DMA((3 * n,)), pltpu.SemaphoreType.DMA((3 * n,))],
        input_output_aliases={i: i for i in range(n)},
        compiler_params=COMM_PARAMS,
    )(*lands)


def _sibling_exchange(gs, *, name):
    n = len(gs)

    def body(*refs):
        ins, outs = refs[:n], refs[n:2 * n]
        send_sems, recv_sems = refs[2 * n:]
        x, y, c = _mesh_pos()
        copies = [_remote(ins[k].at[:, _half_rows(gs[k].shape[1], 1 - c)], outs[k], send_sems, recv_sems, k, (x, y, 1 - c))
                  for k in range(n)]
        for cp in copies:
            cp.start()
        for cp in copies:
            cp.wait()

    return pl.pallas_call(
        body, name=name,
        in_specs=[HBM_SPEC] * n, out_specs=[HBM_SPEC] * n,
        out_shape=[jax.ShapeDtypeStruct((g.shape[0], g.shape[1] // 2, g.shape[2]), g.dtype) for g in gs],
        scratch_shapes=[pltpu.SemaphoreType.DMA((n,)), pltpu.SemaphoreType.DMA((n,))],
        compiler_params=COMM_PARAMS,
    )(*gs)


HBM_ONLY = pl.BlockSpec(memory_space=pltpu.HBM)
SEM_SPEC = pl.BlockSpec(memory_space=pltpu.SEMAPHORE)
SPLIT_PARAMS = pltpu.CompilerParams(has_side_effects=pltpu.SideEffectType.DATAFLOW_SIDE_EFFECTING)


def _scatter_copies(srcs, lands, send_sems, recv_sems):
    x, y, c = _mesh_pos()
    me = 2 * x + y
    out = []
    for k in range(len(srcs)):
        for j, (px, py) in enumerate(_other_chips(x, y)):
            s = 2 * px + py
            send = _remote(srcs[k].at[s], lands[k].at[me], send_sems, recv_sems, 3 * k + j, (px, py, c))
            recv = _remote(srcs[k].at[s], lands[k].at[s], send_sems, recv_sems, 3 * k + j, (px, py, c))
            out.append((send, recv))
    return out


def _exchange_copies(srcs, lands, send_sems, recv_sems):
    x, y, c = _mesh_pos()
    out = []
    for k in range(len(srcs)):
        cp = _remote(srcs[k].at[:, _half_rows(srcs[k].shape[1], 1 - c)], lands[k], send_sems, recv_sems, k, (x, y, 1 - c))
        out.append((cp, cp))
    return out


def _split_start(srcs, land_shapes, copies, n_sems, *, name):
    n = len(srcs)
    lands = [lax.empty(shape, s.dtype) for shape, s in zip(land_shapes, srcs)]

    def body(*refs):
        ins, zones = refs[:n], refs[n:2 * n]
        send_sems, recv_sems, token = refs[2 * n], refs[2 * n + 1], refs[-1]
        for send, _ in copies(ins, zones, send_sems, recv_sems):
            send.start()
        token[...] = jnp.zeros_like(token)

    hbm = lambda a: pltpu.HBM(a.shape, a.dtype)
    res = pl.pallas_call(
        body, name=name,
        in_specs=[HBM_ONLY] * (2 * n),
        out_specs=[SEM_SPEC, SEM_SPEC] + [HBM_ONLY] * (2 * n) + [pl.BlockSpec(memory_space=pltpu.VMEM)],
        out_shape=[pltpu.SemaphoreType.DMA((n_sems,)), pltpu.SemaphoreType.DMA((n_sems,))] + [hbm(a) for a in srcs + lands]
        + [jax.ShapeDtypeStruct((8, LANES), F32)],
        input_output_aliases={i: 2 + i for i in range(2 * n)},
        compiler_params=SPLIT_PARAMS,
    )(*[pltpu.with_memory_space_constraint(a, pltpu.HBM) for a in srcs + lands])
    return (res[0], res[1], list(res[2:2 + n]), list(res[2 + n:2 + 2 * n])), res[-1][0:1, 0:1]


def _scatter_start(ps, *, name):
    return _split_start(ps, [p.shape for p in ps], _scatter_copies, 3 * len(ps), name=name)


def _exchange_start(gs, *, name):
    return _split_start(gs, [(g.shape[0], g.shape[1] // 2, g.shape[2]) for g in gs], _exchange_copies, len(gs), name=name)


def _split_wait(started, copies, after, *, name):
    ng = len(started)
    sizes = [len(st[2]) for st in started]
    offs = [2 * sum(sizes[:i]) for i in range(ng + 1)]
    flat = [a for (_, _, ps, lands) in started for a in ps + lands]

    def body(*refs):
        bufs, sems = refs[:len(flat)], refs[len(flat):len(flat) + 2 * ng]
        for i, n in enumerate(sizes):
            srcs, zones = bufs[offs[i]:offs[i] + n], bufs[offs[i] + n:offs[i + 1]]
            for send, recv in copies(srcs, zones, sems[2 * i], sems[2 * i + 1]):
                send.wait_send()
                recv.wait_recv()

    res = pl.pallas_call(
        body, name=name,
        in_specs=[HBM_ONLY] * len(flat) + [SEM_SPEC] * (2 * ng) + [HBM_SPEC],
        out_specs=[HBM_ONLY] * len(flat),
        out_shape=[pltpu.HBM(a.shape, a.dtype) for a in flat],
        input_output_aliases={i: i for i in range(len(flat))},
        compiler_params=SPLIT_PARAMS,
    )(*flat, *[s for (ss, rs, _, _) in started for s in (ss, rs)], after)
    return [(list(res[offs[i]:offs[i] + n]), list(res[offs[i] + n:offs[i + 1]])) for i, n in enumerate(sizes)]


def _sibling_share(hs):
    n = len(hs)

    def body(*refs):
        ins, outs = refs[:n], refs[n:2 * n]
        send_sems, recv_sems = refs[2 * n:]
        x, y, c = _mesh_pos()
        copies = [_remote(ins[k], outs[k], send_sems, recv_sems, k, (x, y, 1 - c)) for k in range(n)]
        for cp in copies:
            cp.start()
        for cp in copies:
            cp.wait()

    return pl.pallas_call(
        body, name="grad_sibling_share",
        in_specs=[HBM_SPEC] * n, out_specs=[HBM_SPEC] * n,
        out_shape=[jax.ShapeDtypeStruct(h.shape, h.dtype) for h in hs],
        scratch_shapes=[pltpu.SemaphoreType.DMA((n,)), pltpu.SemaphoreType.DMA((n,))],
        compiler_params=COMM_PARAMS,
    )(*hs)


def _allreduce_small(part, by_chip):
    rows, C = part.shape
    rows2 = by_chip.shape[1]

    def body(p_ref, q_ref, o_ref, o2_ref, slots, slots2, send_sems, recv_sems):
        x, y, c = _mesh_pos()
        me = 4 * x + 2 * y + c
        slots[me] = p_ref[...]
        slots2[me] = q_ref[2 * x + y]
        copies = []
        for k in range(1, 8):
            kx, ky, kc = (k >> 2) & 1, (k >> 1) & 1, k & 1
            peer = (x ^ kx if kx else x, y ^ ky if ky else y, c ^ kc if kc else c)
            src = 4 * peer[0] + 2 * peer[1] + peer[2]
            pair = []
            for j, (mine, zone, lands) in enumerate(((p_ref, slots.at[me], slots.at[src]),
                                                     (q_ref.at[2 * peer[0] + peer[1]], slots2.at[me], slots2.at[src]))):
                cp = _remote(mine, zone, send_sems, recv_sems, 2 * (k - 1) + j, peer)
                cp.start()
                pair.append((cp, _remote(mine, lands, send_sems, recv_sems, 2 * (k - 1) + j, peer)))
            copies += pair
        for _, landing in copies:
            landing.wait_recv()
        for cp, _ in copies:
            cp.wait_send()
        total, total2 = slots[0], slots2[0]
        for d in range(1, 8):
            total, total2 = total + slots[d], total2 + slots2[d]
        o_ref[...] = total
        o2_ref[...] = total2

    vmem = pl.BlockSpec(memory_space=pltpu.VMEM)
    return pl.pallas_call(
        body, name="small_grad_allreduce",
        in_specs=[vmem, vmem], out_specs=[vmem, vmem],
        out_shape=[jax.ShapeDtypeStruct((rows, C), F32), jax.ShapeDtypeStruct((rows2, C), F32)],
        scratch_shapes=[pltpu.VMEM((8, rows, C), F32), pltpu.VMEM((8, rows2, C), F32),
                        pltpu.SemaphoreType.DMA((14,)), pltpu.SemaphoreType.DMA((14,))],
        compiler_params=pltpu.CompilerParams(has_side_effects=True, vmem_limit_bytes=VMEM_LIMIT_BYTES),
    )(part, by_chip)


def _pad_w_uq(w):
    lead = w.shape[:-1]
    w = w.reshape(lead + (MLA_HEADS, MLA_QK))
    w = jnp.concatenate([w, jnp.zeros(lead + (MLA_HEADS, MLA_PAD - MLA_QK), w.dtype)], axis=-1)
    return w.reshape(lead + (MLA_HEADS * MLA_PAD,))


def _unpad_w_uq(g):
    lead = g.shape[:-1]
    return g.reshape(lead + (MLA_HEADS, MLA_PAD))[..., :MLA_QK].reshape(lead + (MLA_HEADS * MLA_QK,))


def _t(a):
    return jnp.swapaxes(a, -1, -2)


def _shards_of_cols(w):
    A, NB = w.shape
    return w.reshape(A, N_CHIPS, NB // N_CHIPS).transpose(1, 0, 2)


BIG = ("w_in", "w_uq", "w_ukv", "w_out", "w_up", "w_down")
SMALL = ("attn_pre_norm", "forget_bias", "swa_sinks", "rel_bias", "q_latent_norm", "kv_latent_norm", "group_norm",
         "attn_post_norm", "ffn_pre_norm", "conv_b", "ffn_post_norm")
WEIGHTS = ("attn_pre_norm", "w_in", "forget_bias", "swa_sinks", "rel_bias", "q_latent_norm", "w_uq", "kv_latent_norm",
           "w_ukv", "group_norm", "w_out", "attn_post_norm", "ffn_pre_norm", "w_up", "conv_w", "conv_b", "w_down",
           "ffn_post_norm")


PACK_UNIT = 8 * LANES


def _pack_rows(shape):
    return -(-int(np.prod(shape)) // PACK_UNIT) * 8


def _pack(arrs, row_mult=8):
    parts = []
    for a in arrs:
        n = int(np.prod(a.shape))
        parts.append(jnp.pad(a.reshape(-1), (0, _pack_rows(a.shape) * LANES - n)).reshape(-1, LANES))
    rows = sum(p.shape[0] for p in parts)
    pad = -rows % row_mult
    if pad:
        parts.append(jnp.zeros((pad, LANES), parts[0].dtype))
    return jnp.concatenate(parts, axis=0)


def _unpack(packed, shapes):
    packed = packed.reshape(-1, LANES)
    out, off = [], 0
    for shp in shapes:
        r = _pack_rows(shp)
        out.append(packed[off:off + r].reshape(-1)[:int(np.prod(shp))].reshape(shp))
        off += r
    return out


LAYER_KEYS = ("w_qkv_t", "w_lat_t", "w_in_t", "w_uq_p", "w_uq_t", "w_ukv", "w_ukv_t", "w_out", "w_up", "w_down", "conv_w")


MIX_WEIGHTS = ("w_in", "w_uq", "w_ukv", "w_out")
FFN_WEIGHTS = ("w_up", "w_down", "conv_w")


def _layer_weights(gathered):
    cols = lambda g: g.transpose(1, 0, 2).reshape(g.shape[1], N_CHIPS * g.shape[2])
    out = {}
    if "w_in" in gathered:
        w_in_t = _t(gathered["w_in"]).reshape(IN_COLS, D_MODEL)
        w_in_t = jnp.pad(w_in_t, ((0, IN_ROWS - IN_COLS), (0, 0)))
        w_uq_p = _pad_w_uq(cols(gathered["w_uq"]))
        w_ukv = cols(gathered["w_ukv"])
        out.update(w_qkv_t=w_in_t[:QKV_ROWS], w_lat_t=w_in_t[QKV_ROWS:], w_in_t=w_in_t, w_uq_p=w_uq_p, w_uq_t=_t(w_uq_p),
                   w_ukv=w_ukv, w_ukv_t=_t(w_ukv), w_out=gathered["w_out"].reshape(D_MODEL, D_MODEL))
    if "w_up" in gathered:
        out.update(w_up=gathered["w_up"], w_down=gathered["w_down"].reshape(D_FF, D_MODEL), conv_w=cols(gathered["conv_w"]))
    return out


def _local_step(x, target, W, layer_weights, layer_done):
    W = dict(W, **{key: [None] * DEPTH for key in LAYER_KEYS})
    S = x.shape[0]
    tq_tabs, tm_tabs = _rope_tables(S)
    onehot_t = _rel_onehot_t()
    bias_t = _bias_table(W["rel_bias"].T, onehot_t).reshape(SWA_KV_HEADS, SWA_GROUP, 2 * WINDOW, WINDOW)
    bias_t = bias_t.transpose(0, 2, 1, 3).reshape(SWA_KV_HEADS, 2 * WINDOW, GW)
    row = lambda a: a.reshape(1, -1)
    col = lambda a: a.reshape(-1, 1)
    fox_rows = (FOX_ROW0, FOX_ROW0 + FOX_HEADS * HEAD_DIM, FOX_ROW0 + 2 * FOX_HEADS * HEAD_DIM, SWA_Q_HEADS)
    fox = dict(rows=fox_rows, H=FOX_HEADS, Dk=HEAD_DIM, Dv=HEAD_DIM, scale=HEAD_DIM ** -0.5)
    mla = dict(rows=(0, 0, 0, SWA_Q_HEADS + FOX_HEADS), H=MLA_HEADS, Dk=MLA_PAD, Dv=HEAD_DIM, scale=MLA_SCALE, q_scaled=True)

    saved = []
    h = _rms_fwd(x, row(W["attn_pre_norm"][0]), name="rms_in")
    for l in range(DEPTH):
        sv = {"x0": x, "h1": h}
        for key, val in layer_weights(l, h, False).items():
            W[key][l] = val
        qkv = _matmul(W["w_qkv_t"][l], h, tb=True, out_dtype=BF16, name="proj_qkv")
        lat = _matmul(W["w_lat_t"][l], h, tb=True, name="proj_lat")
        oa, lse_a = _swa_fwd(qkv, bias_t, W["swa_sinks"][l], name="swa_fwd")
        fb_col = jnp.pad(col(W["forget_bias"][l]), ((0, GATE_ROWS - FOX_HEADS), (0, 0)))
        f4 = _gate_fwd(lat, fb_col, name="fox_gate_fwd")[:FOX_HEADS]
        f2 = f4 * LOG2E
        f_row, f_col = f2[:, None, :], f2.T
        of, lse_f = _attn_fwd(qkv, qkv, qkv, f_row=f_row, f_col=f_col, name="fox_fwd", **fox)
        nq, nkv, qm, km, vm = _mla_prep_fwd(lat, col(W["q_latent_norm"][l]), col(W["kv_latent_norm"][l]), W["w_uq_t"][l],
                                            W["w_ukv_t"][l], tq_tabs, tm_tabs, name="mla_prep_fwd")
        oc, lse_c = _attn_fwd(qm, km, vm, name="mla_fwd", **mla)
        mixed = _group_norm_fwd(oa, of, oc, col(W["group_norm"][l]), name="group_norm_fwd")
        y = _matmul(mixed, W["w_out"][l], ta=True, name="proj_out")
        x1, h2 = _resid_rms(x, y, row(W["attn_post_norm"][l]), row(W["ffn_pre_norm"][l]), name="attn_resid")
        for key, val in layer_weights(l, h2, True).items():
            W[key][l] = val
        a = _matmul(h2, W["w_up"][l], b_shards=True, out_dtype=BF16, name="ffn_up")
        u, z = _conv_geglu_fwd(a, W["conv_w"][l], row(W["conv_b"][l]), name="conv_geglu_fwd")
        y2 = _matmul(z, W["w_down"][l], name="ffn_down")
        g_next = row(W["attn_pre_norm"][l + 1]) if l + 1 < DEPTH else None
        x2, h_next = _resid_rms(x1, y2, row(W["ffn_post_norm"][l]), g_next, name="ffn_resid")
        sv.update(qkv=qkv, lat=lat, oa=oa, lse_a=lse_a, fb_col=fb_col, f_row=f_row, f_col=f_col, of=of, lse_f=lse_f,
                  nq=nq, nkv=nkv, qm=qm, km=km, vm=vm, oc=oc, lse_c=lse_c, mixed=mixed, y=y, x1=x1, h2=h2, a=a, u=u, z=z, y2=y2)
        saved.append(sv)
        x, h = x2, h_next

    loss, dx = _loss_head(x, target)

    G = {k: [None] * DEPTH for k in WEIGHTS if k != "rel_bias" and k not in BIG}
    dbias_layers = [None] * DEPTH
    for l in reversed(range(DEPTH)):
        sv = saved[l]
        gb = {}
        if l == DEPTH - 1:
            dy2, dg = _rms_bwd(sv["y2"], row(W["ffn_post_norm"][l]), dx, out_dtype=BF16, name="ffn_post_bwd")
            G["ffn_post_norm"][l] = dg[0]
        dz = _matmul(dy2, W["w_down"][l], tb=True, name="ffn_down_dx")
        gb["w_down"] = _matmul(sv["z"], dy2, ta=True, name="ffn_down_dw").reshape(N_CHIPS, D_FF // N_CHIPS, D_MODEL)
        da, dcw, dcb = _conv_geglu_bwd(sv["a"], sv["u"], W["conv_w"][l], dz, name="conv_geglu_bwd")
        G["conv_w"][l] = dcw.transpose(1, 0, 2).reshape(3, 2 * D_FF)
        G["conv_b"][l] = dcb.reshape(2 * D_FF)
        dh2 = _matmul(da, W["w_up"][l], tb=True, b_shards=True, a_halves=True, name="ffn_up_dx")
        gb["w_up"] = _matmul(sv["h2"], da, ta=True, out_shards=True, b_halves=True, name="ffn_up_dw")
        token = layer_done(l, gb)
        gb = {}
        dx1, dg, dy, dg_post = _rms_bwd(sv["x1"], row(W["ffn_pre_norm"][l]) + token, dh2, resid=dx, out_dtype=F32,
                                        then=(sv["y"], row(W["attn_post_norm"][l])), name="ffn_pre_bwd")
        G["ffn_pre_norm"][l] = dg[0]
        G["attn_post_norm"][l] = dg_post[0]
        dmixed = _matmul(W["w_out"][l], dy, tb=True, name="proj_out_dx")
        gb["w_out"] = _matmul(sv["mixed"], dy, name="proj_out_dw").reshape(N_CHIPS, D_MODEL // N_CHIPS, D_MODEL)
        doa, dof, doc, dg, delta = _group_norm_bwd(sv["oa"], sv["of"], sv["oc"], col(W["group_norm"][l]), dmixed,
                                                   name="group_norm_bwd")
        G["group_norm"][l] = dg[:, 0]
        dqa, dkva, dbias_l, dsink = _swa_bwd(sv["qkv"], bias_t, W["swa_sinks"][l], doa, sv["lse_a"],
                                             delta.reshape(-1, S), name="swa_bwd")
        dbias_layers[l] = (dbias_l.reshape(SWA_KV_HEADS, 2 * WINDOW, SWA_GROUP, WINDOW).transpose(0, 2, 1, 3)
                           .reshape(SWA_Q_HEADS, -1))
        G["swa_sinks"][l] = dsink[:, 0]
        dqf, dkf, dvf, dfk = _attn_bwd(sv["qkv"], sv["qkv"], sv["qkv"], do=dof, lse=sv["lse_f"], delta=delta,
                                       f_row=sv["f_row"], f_col=sv["f_col"], name="fox_bwd", **fox)
        dF = jnp.pad(dfk.T, ((0, GATE_ROWS - FOX_HEADS), (0, 0)))
        dflog, dfb = _gate_bwd(sv["lat"], sv["fb_col"], dF, name="fox_gate_bwd")
        G["forget_bias"][l] = dfb[:FOX_HEADS, 0]
        dqm, dkm, dvm = _attn_bwd(sv["qm"], sv["km"], sv["vm"], do=doc, lse=sv["lse_c"], delta=delta, name="mla_bwd", **mla)
        dlat, dwq_t, dwkv_t, dgq, dgkv = _mla_prep_bwd(
            sv["lat"], sv["nq"], sv["nkv"], col(W["q_latent_norm"][l]), col(W["kv_latent_norm"][l]), W["w_uq_p"][l],
            W["w_ukv"][l], tq_tabs, tm_tabs, dqm, dkm, dvm, dflog, name="mla_prep_bwd")
        gb["w_uq"], gb["w_ukv"] = _shards_of_cols(_unpad_w_uq(dwq_t.T)), _shards_of_cols(dwkv_t.T)
        G["q_latent_norm"][l], G["kv_latent_norm"][l] = dgq[:, 0], dgkv[:, 0]
        dproj = _dproj_cast(dqa, dkva, dqf, dkf, dvf, dlat, name="dproj_cast")
        dh1 = _matmul(dproj, W["w_in_t"][l], ta=True, name="proj_in_dx")
        dw_in_t = _matmul(dproj, sv["h1"], name="proj_in_dw")
        gb["w_in"] = _t(dw_in_t[:IN_COLS].reshape(N_CHIPS, IN_COLS // N_CHIPS, D_MODEL))
        token = layer_done(l, gb)
        below = (saved[l - 1]["y2"], row(W["ffn_post_norm"][l - 1])) if l > 0 else None
        res = _rms_bwd(sv["x0"], row(W["attn_pre_norm"][l]) + token, dh1, resid=dx1, out_dtype=F32, then=below,
                       name="attn_pre_bwd")
        dx, G["attn_pre_norm"][l] = res[0], res[1][0]
        if l > 0:
            dy2, G["ffn_post_norm"][l - 1] = res[2], res[3][0]

    grads = {k: jnp.stack(v) for k, v in G.items()}
    grads["rel_bias"] = _bias_table_bwd(jnp.stack(dbias_layers), onehot_t).T
    return loss, dx, grads


def kernel(x, attn_pre_norm, w_in, forget_bias, swa_sinks, rel_bias, q_latent_norm, w_uq, kv_latent_norm, w_ukv, group_norm, w_out, attn_post_norm, ffn_pre_norm, w_up, conv_w, conv_b, w_down, ffn_post_norm, loss_target, m_attn_pre_norm, m_w_in, m_forget_bias, m_swa_sinks, m_rel_bias, m_q_latent_norm, m_w_uq, m_kv_latent_norm, m_w_ukv, m_group_norm, m_w_out, m_attn_post_norm, m_ffn_pre_norm, m_w_up, m_conv_w, m_conv_b, m_w_down, m_ffn_post_norm, v_attn_pre_norm, v_w_in, v_forget_bias, v_swa_sinks, v_rel_bias, v_q_latent_norm, v_w_uq, v_kv_latent_norm, v_w_ukv, v_group_norm, v_w_out, v_attn_post_norm, v_ffn_pre_norm, v_w_up, v_conv_w, v_conv_b, v_w_down, v_ffn_post_norm):
    args = dict(locals())
    w = {k: args[k] for k in WEIGHTS}
    m = {k: args["m_" + k] for k in WEIGHTS}
    v = {k: args["v_" + k] for k in WEIGHTS}

    block = lambda l, keys: [w[k][l] if k == "conv_w" else w[k][l].astype(BF16) for k in keys]
    units = [(0, MIX_WEIGHTS), (0, FFN_WEIGHTS)] + [(l, MIX_WEIGHTS + FFN_WEIGHTS) for l in range(1, DEPTH)]
    gather_state, token = _gather_start([block(l, keys) for l, keys in units])
    W = {k: w[k] for k in SMALL}
    W["attn_pre_norm"] = W["attn_pre_norm"] + token

    def layer_weights(l, after, for_ffn):
        if for_ffn and l > 0:
            return {}
        keys = FFN_WEIGHTS if for_ffn else (MIX_WEIGHTS if l == 0 else MIX_WEIGHTS + FFN_WEIGHTS)
        tag = f"{l}_{keys[0]}"
        srcs, lands = _gather_wait(gather_state[units.index((l, keys))], after, name="weight_gather_wait_" + tag)
        lands = _gather_forward(lands, name="weight_gather_forward_" + tag)
        lands = _place_own(lands, srcs, name="place_own_shards")
        return _layer_weights(dict(zip(keys, lands)))

    started, groups, pending = [], [], []

    def to_chips(l, keys, gs, recv, tag):
        pair = [_pair_sum(gk, rk, name="grad_pair_sum") for gk, rk in zip(gs, recv)]
        state, token = _scatter_start(pair, name="grad_scatter_start_" + tag)
        started.append(state)
        groups.append((l, keys))
        return token

    def finish_pending(after):
        l, keys, tag, state = pending.pop()
        gs, recv = _split_wait([state], _exchange_copies, after, name="grad_exchange_wait_" + tag)[0]
        return to_chips(l, keys, gs, recv, tag)

    def layer_done(l, gb):
        keys = [k for k in BIG if k in gb]
        gs = [gb[k] for k in keys]
        tag = f"{l}_{keys[0]}"
        token = finish_pending(gs[0]) if pending else 0.0
        if l == 0:
            return token + to_chips(l, keys, gs, _sibling_exchange(gs, name="grad_sibling_exchange_" + tag), tag)
        state, started_token = _exchange_start(gs, name="grad_exchange_start_" + tag)
        pending.append((l, keys, tag, state))
        return token + started_token

    loss_part, dx, g = _local_step(x[0], loss_target[0], W, layer_weights, layer_done)
    loss = lax.psum(loss_part, ("x", "y", "c"))

    reduced = {}
    for (l, keys), (pair, zones) in zip(groups, _split_wait(started, _scatter_copies, dx, name="grad_scatter_wait")):
        for k, p, z in zip(keys, pair, zones):
            reduced[k, l] = _chip_sum(z, p, name="grad_chip_sum")
    mine = [jnp.stack([reduced[k, l] for l in range(DEPTH)]) for k in BIG]
    other = _sibling_share(mine)
    out_g, out_d, out_m, out_v = {}, {}, {}, {}
    for k, g_mine, g_other in zip(BIG, mine, other):
        out_g[k], out_d[k], out_m[k], out_v[k] = _adamw_halves(w[k], g_mine, g_other, m[k], v[k], name="adamw_" + k)

    small_shapes = [w[k].shape for k in SMALL]
    taps_by_chip = g["conv_w"].reshape(DEPTH, 3, N_CHIPS, FF_SHARD).transpose(2, 0, 1, 3)
    reduced, taps = _allreduce_small(_pack([g[k] for k in SMALL]), jnp.stack([_pack([t]) for t in taps_by_chip]))
    g_small = _unpack(reduced, small_shapes) + _unpack(taps, [w["conv_w"].shape])
    names = SMALL + ("conv_w",)
    shapes = small_shapes + [w["conv_w"].shape]
    packed = lambda arrs: _pack(arrs, ROW_TILE)[None]
    d_s, m_s, v_s = _adamw(packed([w[k] for k in names]), packed(g_small), packed([m[k] for k in names]),
                           packed([v[k] for k in names]), name="adamw_small")
    out_g.update(zip(names, g_small))
    out_d.update(zip(names, _unpack(d_s, shapes)))
    out_m.update(zip(names, _unpack(m_s, shapes)))
    out_v.update(zip(names, _unpack(v_s, shapes)))

    return (loss, dx[None], *[out_g[k] for k in WEIGHTS], *[out_d[k] for k in WEIGHTS],
            *[out_m[k] for k in WEIGHTS], *[out_v[k] for k in WEIGHTS])
```

```python
import math

import numpy as np
import jax
import jax.numpy as jnp
from jax import lax
from jax.experimental import pallas as pl
from jax.experimental.pallas import tpu as pltpu

F32 = jnp.float32
BF16 = jnp.bfloat16

D_MODEL = 1024
DEPTH = 4
HEAD_DIM = 64
SWA_Q_HEADS = 8
SWA_KV_HEADS = 2
SWA_GROUP = SWA_Q_HEADS // SWA_KV_HEADS
WINDOW = 128
FOX_HEADS = 4
MLA_HEADS = 4
MLA_Q_RANK = 256
MLA_KV_RANK = 128
MLA_NOPE = 64
MLA_ROPE = 32
MLA_QK = MLA_NOPE + MLA_ROPE
ROPE_THETA = 10000.0
REL_BUCKETS = 32
REL_MAX_DIST = 128
D_FF = 2816
EPS = 1e-6
NEG_INF = -1e30
LANES = 128
N_CHIPS = 4

IN_COLS = 1956
IN_ROWS = 2048
QKV_ROWS = 1536
LAT_ROWS = IN_ROWS - QKV_ROWS
LAT_SHIFT = FOX_HEADS
FOX_ROW0 = 768
MLA_PAD = LANES
GATE_ROWS = 8

ADAM_LR = 0.001
ADAM_B1 = 0.9
ADAM_B2 = 0.999
ADAM_EPS = 1e-08
ADAM_WD = 0.01
ADAM_STEP = 10

VMEM_LIMIT_BYTES = 48 * 1024 * 1024
ATT_TILE = 512
LOG2E = math.log2(math.e)
MLA_SCALE = MLA_QK ** -0.5
ROW_TILE = 256
MESH = pl.DeviceIdType.MESH

NT = (((1,), (1,)), ((), ()))
TN = (((0,), (0,)), ((), ()))
NN = (((1,), (0,)), ((), ()))


def _params(*sem):
    return pltpu.CompilerParams(dimension_semantics=sem, vmem_limit_bytes=VMEM_LIMIT_BYTES)


def _tile(dim, cap):
    for t in (2816, 2048, 1408, 1024, 512, 256, 128, 64, 32, 16, 8):
        if t <= cap and dim % t == 0:
            return t
    return dim


def _dot(a, b, dims=NN):
    return lax.dot_general(a, b, dims, preferred_element_type=F32)


def _split3(a):
    a1 = a.astype(BF16)
    r1 = a - a1.astype(F32)
    a2 = r1.astype(BF16)
    a3 = (r1 - a2.astype(F32)).astype(BF16)
    return a1, a2, a3


FF_SHARD = 2 * D_FF // N_CHIPS
MATMUL_VMEM_BYTES = 40 * 1024 * 1024


def _matmul(a, b, *, ta=False, tb=False, out_dtype=F32, name, b_shards=False, out_shards=False, a_halves=False,
            b_halves=False):
    if a_halves:
        M, K = a.shape[1], 2 * a.shape[2]
    elif ta:
        K, M = a.shape
    else:
        M, K = a.shape
    if b_halves:
        K2, N = b.shape[1], 2 * b.shape[2]
    elif b_shards:
        K2, N = (2 * D_FF, D_MODEL) if tb else (D_MODEL, 2 * D_FF)
    elif tb:
        N, K2 = b.shape
    else:
        K2, N = b.shape
    assert K == K2, (a.shape, b.shape)
    tn = _tile(N, 1408)
    tk = FF_SHARD if (b_shards and tb) else _tile(K, 2816)
    out_bytes = jnp.dtype(out_dtype).itemsize
    vmem = lambda tm, tk: 2 * 2 * tk * (tm + tn) + (4 + 2 * out_bytes) * tm * tn
    tm = M if M <= 2048 else _tile(M, 1408)
    if M > 2048 and M % 2048 == 0 and tk == K and vmem(2048, tk) <= MATMUL_VMEM_BYTES:
        tm = 2048
    while vmem(tm, tk) > MATMUL_VMEM_BYTES and tk % 256 == 0:
        tk //= 2
    nk = K // tk
    dims = (((0 if ta else 1,), (1 if tb else 0,)), ((), ()))

    def body(a_ref, b_ref, o_ref, acc_ref):
        k = pl.program_id(2)

        @pl.when(k == 0)
        def _():
            acc_ref[...] = jnp.zeros_like(acc_ref)

        acc_ref[...] += lax.dot_general(a_ref[...], b_ref[...], dims, preferred_element_type=F32)

        @pl.when(k == nk - 1)
        def _():
            o_ref[...] = acc_ref[...].astype(o_ref.dtype)

    if a_halves:
        nh = K // 2 // tk
        a_spec = pl.BlockSpec((None, tm, tk), lambda i, j, k: (k // nh, i, k % nh))
    else:
        a_spec = pl.BlockSpec((tk, tm), lambda i, j, k: (k, i)) if ta else pl.BlockSpec((tm, tk), lambda i, j, k: (i, k))
    if b_halves:
        nh = N // 2 // tn
        b_spec = pl.BlockSpec((None, tk, tn), lambda i, j, k: (j // nh, k, j % nh))
    elif b_shards and tb:
        assert tk == FF_SHARD
        b_spec = pl.BlockSpec((None, tn, tk), lambda i, j, k: (k, j, 0))
    elif b_shards:
        assert tn == FF_SHARD
        b_spec = pl.BlockSpec((None, tk, tn), lambda i, j, k: (j, k, 0))
    else:
        b_spec = pl.BlockSpec((tn, tk), lambda i, j, k: (j, k)) if tb else pl.BlockSpec((tk, tn), lambda i, j, k: (k, j))
    if out_shards:
        assert tn == FF_SHARD
        out_spec = pl.BlockSpec((None, tm, tn), lambda i, j, k: (j, i, 0))
        out_shape = jax.ShapeDtypeStruct((N // tn, M, tn), out_dtype)
    else:
        out_spec = pl.BlockSpec((tm, tn), lambda i, j, k: (i, j))
        out_shape = jax.ShapeDtypeStruct((M, N), out_dtype)
    return pl.pallas_call(
        body, name=name, grid=(M // tm, N // tn, nk),
        in_specs=[a_spec, b_spec], out_specs=out_spec, out_shape=out_shape,
        scratch_shapes=[pltpu.VMEM((tm, tn), F32)],
        compiler_params=_params("parallel", "parallel", "arbitrary"),
    )(a, b)


def _seg_rms(xs, g):
    r = lax.rsqrt(jnp.mean(xs * xs, axis=-1, keepdims=True) + EPS)
    return xs * r * g


def _seg_rms_bwd(xs, g, dy):
    r = lax.rsqrt(jnp.mean(xs * xs, axis=-1, keepdims=True) + EPS)
    gd = dy * g
    c = jnp.mean(gd * xs, axis=-1, keepdims=True)
    dx = r * gd - xs * (r * r * r * c)
    dg = jnp.sum(dy * (xs * r), axis=0, keepdims=True)
    return dx, dg


def _rms_fwd(x, g, *, name):
    S, W = x.shape
    tm = _tile(S, 512)

    def body(x_ref, g_ref, o_ref):
        o_ref[...] = _seg_rms(x_ref[...], g_ref[...]).astype(o_ref.dtype)

    return pl.pallas_call(
        body, name=name, grid=(S // tm,),
        in_specs=[pl.BlockSpec((tm, W), lambda i: (i, 0)), pl.BlockSpec((1, W), lambda i: (0, 0))],
        out_specs=pl.BlockSpec((tm, W), lambda i: (i, 0)),
        out_shape=jax.ShapeDtypeStruct((S, W), BF16),
        compiler_params=_params("parallel"),
    )(x, g)


def _rms_bwd(x, g, dy, *, resid=None, out_dtype, name, then=None):
    S, W = x.shape
    tm = _tile(S, 512)
    has_resid = resid is not None
    chained = then is not None

    def body(*refs):
        refs = list(refs)
        x_ref, g_ref, dy_ref = refs[:3]
        r_ref = refs[3] if has_resid else None
        n_in = 3 + has_resid + 2 * chained
        x2_ref, g2_ref = (refs[n_in - 2], refs[n_in - 1]) if chained else (None, None)
        outs = refs[n_in:]
        dx_ref, dg_ref = outs[0], outs[1]

        @pl.when(pl.program_id(0) == 0)
        def _():
            dg_ref[...] = jnp.zeros_like(dg_ref)
            if chained:
                outs[3][...] = jnp.zeros_like(outs[3])

        dx, dg = _seg_rms_bwd(x_ref[...], g_ref[...], dy_ref[...])
        if has_resid:
            dx = dx + r_ref[...]
        dx_ref[...] = dx.astype(dx_ref.dtype)
        dg_ref[...] += dg
        if chained:
            dx2, dg2 = _seg_rms_bwd(x2_ref[...], g2_ref[...], dx)
            outs[2][...] = dx2.astype(BF16)
            outs[3][...] += dg2

    row = pl.BlockSpec((tm, W), lambda i: (i, 0))
    vec = pl.BlockSpec((1, W), lambda i: (0, 0))
    ins = [x, g, dy] + ([resid] if has_resid else []) + (list(then) if chained else [])
    return pl.pallas_call(
        body, name=name, grid=(S // tm,),
        in_specs=[row, vec, row] + ([row] if has_resid else []) + ([row, vec] if chained else []),
        out_specs=[row, vec] + ([row, vec] if chained else []),
        out_shape=[jax.ShapeDtypeStruct((S, W), out_dtype), jax.ShapeDtypeStruct((1, W), F32)]
        + ([jax.ShapeDtypeStruct((S, W), BF16), jax.ShapeDtypeStruct((1, W), F32)] if chained else []),
        compiler_params=_params("arbitrary"),
    )(*ins)


def _resid_rms(x, y, g_post, g_next, *, name):
    S, W = x.shape
    tm = _tile(S, 512)
    with_next = g_next is not None

    def body(*refs):
        if with_next:
            x_ref, y_ref, gp_ref, gn_ref, xo_ref, h_ref = refs
        else:
            x_ref, y_ref, gp_ref, xo_ref = refs
        xn = x_ref[...] + _seg_rms(y_ref[...], gp_ref[...])
        xo_ref[...] = xn
        if with_next:
            h_ref[...] = _seg_rms(xn, gn_ref[...]).astype(BF16)

    row = pl.BlockSpec((tm, W), lambda i: (i, 0))
    vec = pl.BlockSpec((1, W), lambda i: (0, 0))
    outs = [jax.ShapeDtypeStruct((S, W), F32)] + ([jax.ShapeDtypeStruct((S, W), BF16)] if with_next else [])
    res = pl.pallas_call(
        body, name=name, grid=(S // tm,),
        in_specs=[row, row, vec] + ([vec] if with_next else []),
        out_specs=[row] + ([row] if with_next else []),
        out_shape=outs,
        compiler_params=_params("parallel"),
    )(*([x, y, g_post] + ([g_next] if with_next else [])))
    return (res[0], res[1]) if with_next else (res[0], None)


def _col_rms(xs, g):
    r = lax.rsqrt(jnp.mean(xs * xs, axis=0, keepdims=True) + EPS)
    return xs * r * g


def _col_rms_bwd(xs, g, dy):
    r = lax.rsqrt(jnp.mean(xs * xs, axis=0, keepdims=True) + EPS)
    gd = dy * g
    c = jnp.mean(gd * xs, axis=0, keepdims=True)
    dx = r * gd - xs * (r * r * r * c)
    dg = jnp.sum(dy * (xs * r), axis=1, keepdims=True)
    return dx, dg


GROUP_ROWS = (SWA_Q_HEADS * HEAD_DIM, FOX_HEADS * HEAD_DIM, MLA_HEADS * HEAD_DIM)


def _group_specs(S, tn):
    outs = [pl.BlockSpec((n, tn), lambda i: (0, i)) for n in GROUP_ROWS]
    g = pl.BlockSpec((D_MODEL, 1), lambda i: (0, 0))
    mixed = pl.BlockSpec((D_MODEL, tn), lambda i: (0, i))
    return outs, g, mixed


def _group_norm_fwd(oa, of, oc, g, *, name):
    S = oa.shape[1]
    tn = _tile(S, 512)
    outs, gs, mixed = _group_specs(S, tn)

    def body(a_ref, f_ref, c_ref, g_ref, o_ref):
        r0 = 0
        for ref, n in zip((a_ref, f_ref, c_ref), GROUP_ROWS):
            o_ref[r0:r0 + n, :] = _col_rms(ref[...], g_ref[r0:r0 + n, :]).astype(BF16)
            r0 += n

    return pl.pallas_call(
        body, name=name, grid=(S // tn,),
        in_specs=outs + [gs], out_specs=mixed,
        out_shape=jax.ShapeDtypeStruct((D_MODEL, S), BF16),
        compiler_params=_params("parallel"),
    )(oa, of, oc, g)


def _group_norm_bwd(oa, of, oc, g, dmixed, *, name):
    S = oa.shape[1]
    tn = _tile(S, 512)
    outs, gs, mixed = _group_specs(S, tn)
    n_heads = D_MODEL // HEAD_DIM

    def body(a_ref, f_ref, c_ref, g_ref, dm_ref, da_ref, df_ref, dc_ref, dg_ref, dl_ref):
        @pl.when(pl.program_id(0) == 0)
        def _():
            dg_ref[...] = jnp.zeros_like(dg_ref)

        r0 = 0
        for ref, dref, n in zip((a_ref, f_ref, c_ref), (da_ref, df_ref, dc_ref), GROUP_ROWS):
            o = ref[...]
            dx, dg = _col_rms_bwd(o, g_ref[r0:r0 + n, :], dm_ref[r0:r0 + n, :])
            dxb = dx.astype(BF16)
            dref[...] = dxb
            dg_ref[r0:r0 + n, :] += dg
            od = o * dxb.astype(F32)
            for h in range(n // HEAD_DIM):
                dl_ref[r0 // HEAD_DIM + h] = jnp.sum(od[h * HEAD_DIM:(h + 1) * HEAD_DIM, :], axis=0, keepdims=True)
            r0 += n

    return pl.pallas_call(
        body, name=name, grid=(S // tn,),
        in_specs=outs + [gs, mixed], out_specs=outs + [gs, pl.BlockSpec((n_heads, 1, tn), lambda i: (0, 0, i))],
        out_shape=[jax.ShapeDtypeStruct((n, S), BF16) for n in GROUP_ROWS] + [jax.ShapeDtypeStruct((D_MODEL, 1), F32),
                                                                              jax.ShapeDtypeStruct((n_heads, 1, S), F32)],
        compiler_params=_params("arbitrary"),
    )(oa, of, oc, g, dmixed)


def _loss_head(y, target):
    S, W = y.shape
    tm = _tile(S, 512)

    def body(y_ref, t_ref, d_ref, l_ref):
        @pl.when(pl.program_id(0) == 0)
        def _():
            l_ref[...] = jnp.zeros_like(l_ref)

        err = y_ref[...] - t_ref[...]
        d_ref[...] = err * (1.0 / W)
        l_ref[...] += 0.5 * jnp.sum(jnp.mean(err * err, axis=-1, keepdims=True), axis=0, keepdims=True)

    row = pl.BlockSpec((tm, W), lambda i: (i, 0))
    d, l = pl.pallas_call(
        body, name="loss_head", grid=(S // tm,),
        in_specs=[row, row],
        out_specs=[row, pl.BlockSpec((1, 1), lambda i: (0, 0))],
        out_shape=[jax.ShapeDtypeStruct((S, W), F32), jax.ShapeDtypeStruct((1, 1), F32)],
        compiler_params=_params("arbitrary"),
    )(y, target)
    return l[0, 0], d


def _attn_fwd(q_src, k_src, v_src, rows, H, Dk, Dv, scale, f_row=None, f_col=None, *, name, q_scaled=False):
    S = q_src.shape[1]
    T = _tile(S, ATT_TILE)
    nq = S // T
    forget = f_row is not None
    qb, kb, vb = rows[0] // (H * Dk), rows[1] // (H * Dk), rows[2] // (H * Dv)
    hs = range(H)

    def body(*refs):
        if forget:
            q_ref, k_ref, v_ref, fq_ref, fk_ref, o_ref, lse_ref = refs
        else:
            q_ref, k_ref, v_ref, o_ref, lse_ref = refs
        i = pl.program_id(0)

        def tile(j, masked, state):
            off = pl.multiple_of(j * T, T)
            ss = [_dot(k_ref[h * Dk:(h + 1) * Dk, pl.ds(off, T)], q_ref[h * Dk:(h + 1) * Dk, :], TN) for h in hs]
            if not q_scaled:
                ss = [s * (scale * LOG2E) for s in ss]
            if forget:
                ss = [ss[h] + (fq_ref[h] - fk_ref[pl.ds(off, T), h:h + 1]) for h in hs]
            if masked:
                r = lax.broadcasted_iota(jnp.int32, (T, T), 0)
                c = lax.broadcasted_iota(jnp.int32, (T, T), 1)
                ss = [jnp.where(r <= c, s, NEG_INF) for s in ss]
            m_new = [jnp.maximum(state[h][0], jnp.max(ss[h], axis=0, keepdims=True)) for h in hs]
            alpha = [jnp.exp2(state[h][0] - m_new[h]) for h in hs]
            ps = [jnp.exp2(ss[h] - m_new[h]) for h in hs]
            l_new = [alpha[h] * state[h][1] + jnp.sum(ps[h], axis=0, keepdims=True) for h in hs]
            p_hi = [p.astype(BF16) for p in ps]
            vs = [v_ref[h * Dv:(h + 1) * Dv, pl.ds(off, T)] for h in hs]
            pv = [_dot(vs[h], p_hi[h]) for h in hs]
            if forget:
                pv = [pv[h] + _dot(vs[h], (ps[h] - p_hi[h].astype(F32)).astype(BF16)) for h in hs]
            return tuple((m_new[h], l_new[h], alpha[h] * state[h][2] + pv[h]) for h in hs)

        init = tuple((jnp.full((1, T), NEG_INF, F32), jnp.zeros((1, T), F32), jnp.zeros((Dv, T), F32)) for _ in hs)
        state = lax.fori_loop(0, i, lambda j, st: tile(j, False, st), init)
        state = tile(i, True, state)
        for h in hs:
            m, l, acc = state[h]
            o_ref[h * Dv:(h + 1) * Dv, :] = acc / l
            lse_ref[h] = m + jnp.log2(l)

    in_specs = [pl.BlockSpec((H * Dk, T), lambda i: (qb, i)),
                pl.BlockSpec((H * Dk, S), lambda i: (kb, 0)),
                pl.BlockSpec((H * Dv, S), lambda i: (vb, 0))]
    ins = [q_src, k_src, v_src]
    if forget:
        in_specs += [pl.BlockSpec((H, 1, T), lambda i: (0, 0, i)), pl.BlockSpec((S, H), lambda i: (0, 0))]
        ins += [f_row, f_col]
    return pl.pallas_call(
        body, name=name, grid=(nq,),
        in_specs=in_specs,
        out_specs=[pl.BlockSpec((H * Dv, T), lambda i: (0, i)), pl.BlockSpec((H, 1, T), lambda i: (0, 0, i))],
        out_shape=[jax.ShapeDtypeStruct((H * Dv, S), F32), jax.ShapeDtypeStruct((H, 1, S), F32)],
        compiler_params=_params("parallel"),
    )(*ins)


def _attn_bwd(q_src, k_src, v_src, rows, H, Dk, Dv, scale, do, lse, delta, f_row=None, f_col=None, *, name, q_scaled=False):
    S = q_src.shape[1]
    T = _tile(S, ATT_TILE)
    nq = S // T
    forget = f_row is not None
    qb, kb, vb, db = rows[0] // (H * Dk), rows[1] // (H * Dk), rows[2] // (H * Dv), rows[3] // H
    hs = range(H)

    def body(*refs):
        if forget:
            (q_ref, k_ref, v_ref, do_ref, lse_ref, dl_ref, fq_ref, fk_ref,
             dq_ref, dk_ref, dv_ref, df_ref, dk_s, dv_s, df_s) = refs
        else:
            q_ref, k_ref, v_ref, do_ref, lse_ref, dl_ref, dq_ref, dk_ref, dv_ref, dk_s, dv_s = refs
        j = pl.program_id(0)

        @pl.when(j == 0)
        def _():
            dq_ref[...] = jnp.zeros_like(dq_ref)

        dk_s[...] = jnp.zeros_like(dk_s)
        dv_s[...] = jnp.zeros_like(dv_s)
        if forget:
            df_s[...] = jnp.zeros_like(df_s)
        kt = [k_ref[h * Dk:(h + 1) * Dk, :] for h in hs]
        kj = [k.T for k in kt]
        vj = [v_ref[h * Dv:(h + 1) * Dv, :].T for h in hs]
        koff = pl.multiple_of(j * T, T)

        def tile(i, masked):
            cols = pl.ds(pl.multiple_of(i * T, T), T)
            qi = [q_ref[h * Dk:(h + 1) * Dk, cols] for h in hs]
            doi = [do_ref[h * Dv:(h + 1) * Dv, cols] for h in hs]
            st = [_dot(kj[h], qi[h]) for h in hs]
            if not q_scaled:
                st = [x * (scale * LOG2E) for x in st]
            if forget:
                st = [st[h] + (fq_ref[h, :, cols] - fk_ref[pl.ds(koff, T), h:h + 1]) for h in hs]
            if masked:
                r = lax.broadcasted_iota(jnp.int32, (T, T), 0)
                c = lax.broadcasted_iota(jnp.int32, (T, T), 1)
                st = [jnp.where(r <= c, x, NEG_INF) for x in st]
            pt = [jnp.exp2(st[h] - lse_ref[h, :, cols]) for h in hs]
            dpt = [_dot(vj[h], doi[h]) for h in hs]
            dst = [pt[h] * (dpt[h] - dl_ref[h, :, cols]) for h in hs]
            ptb = [p.astype(BF16) for p in pt]
            dsb = [d.astype(BF16) for d in dst]
            for h in hs:
                dv_s[h * Dv:(h + 1) * Dv, :] += _dot(doi[h], ptb[h], NT)
            for h in hs:
                dk_s[h * Dk:(h + 1) * Dk, :] += _dot(qi[h], dsb[h], NT)
            for h in hs:
                dq_ref[h * Dk:(h + 1) * Dk, cols] += _dot(kt[h], dsb[h]) * scale
            if forget:
                for h in hs:
                    part = dst[h][:, 0:LANES]
                    for c0 in range(LANES, T, LANES):
                        part = part + dst[h][:, c0:c0 + LANES]
                    df_s[h] += part

        tile(j, True)

        def loop_body(i, carry):
            tile(i, False)
            return carry

        lax.fori_loop(j + 1, nq, loop_body, 0)
        dk_ref[...] = dk_s[...] * ((1.0 / LOG2E) if q_scaled else scale)
        dv_ref[...] = dv_s[...]
        if forget:
            df_ref[...] = jnp.concatenate([-jnp.sum(df_s[h], axis=-1, keepdims=True) for h in hs], axis=1)

    res = lambda D, b0: pl.BlockSpec((H * D, S), lambda j: (b0, 0))
    blk = lambda D, b0: pl.BlockSpec((H * D, T), lambda j: (b0, j))
    row3 = lambda b0: pl.BlockSpec((H, 1, S), lambda j: (b0, 0, 0))
    in_specs = [res(Dk, qb), blk(Dk, kb), blk(Dv, vb), res(Dv, 0), row3(0), row3(db)]
    ins = [q_src, k_src, v_src, do, lse, delta]
    out_specs = [res(Dk, 0), blk(Dk, 0), blk(Dv, 0)]
    out_shape = [jax.ShapeDtypeStruct((H * Dk, S), F32), jax.ShapeDtypeStruct((H * Dk, S), F32),
                 jax.ShapeDtypeStruct((H * Dv, S), F32)]
    scratch = [pltpu.VMEM((H * Dk, T), F32), pltpu.VMEM((H * Dv, T), F32)]
    if forget:
        in_specs += [row3(0), pl.BlockSpec((S, H), lambda j: (0, 0))]
        ins += [f_row, f_col]
        out_specs.append(pl.BlockSpec((T, H), lambda j: (j, 0)))
        out_shape.append(jax.ShapeDtypeStruct((S, H), F32))
        scratch.append(pltpu.VMEM((H, T, min(T, LANES)), F32))
    return pl.pallas_call(
        body, name=name, grid=(nq,),
        in_specs=in_specs, out_specs=out_specs, out_shape=out_shape, scratch_shapes=scratch,
        compiler_params=_params("arbitrary"),
    )(*ins)


GW = SWA_GROUP * WINDOW


def _swa_masks(i):
    r = lax.broadcasted_iota(jnp.int32, (WINDOW, GW), 0)
    c = lax.broadcasted_iota(jnp.int32, (WINDOW, GW), 1) % WINDOW
    return (r > c) & (i > 0), r <= c


def _swa_specs():
    W = WINDOW
    kv_rows = SWA_KV_HEADS * HEAD_DIM
    q = pl.BlockSpec((SWA_Q_HEADS * HEAD_DIM, W), lambda i: (0, i))
    prev = lambda b: pl.BlockSpec((kv_rows, W), lambda i: (b, jnp.maximum(i - 1, 0)))
    cur = lambda b: pl.BlockSpec((kv_rows, W), lambda i: (b, i))
    bias = pl.BlockSpec((SWA_KV_HEADS, 2 * W, GW), lambda i: (0, 0, 0))
    stat = pl.BlockSpec((SWA_Q_HEADS, W), lambda i: (0, i))
    sink = pl.BlockSpec(memory_space=pltpu.SMEM)
    return q, prev(4), cur(4), prev(5), cur(5), bias, stat, sink


def _group_lanes(ref, g, rows_per_head):
    h0 = g * SWA_GROUP
    return jnp.concatenate([ref[(h0 + j) * rows_per_head:(h0 + j + 1) * rows_per_head, :] for j in range(SWA_GROUP)], axis=1)


def _swa_scores(g, q_ref, kp_ref, kc_ref, b_ref, masks):
    rows = slice(g * HEAD_DIM, (g + 1) * HEAD_DIM)
    qg = _group_lanes(q_ref, g, HEAD_DIM)
    scale = HEAD_DIM ** -0.5
    s_p = jnp.where(masks[0], _dot(kp_ref[rows, :], qg, TN) * scale + b_ref[g, 0:WINDOW, :], NEG_INF)
    s_c = jnp.where(masks[1], _dot(kc_ref[rows, :], qg, TN) * scale + b_ref[g, WINDOW:2 * WINDOW, :], NEG_INF)
    return qg, rows, s_p, s_c


def _sink_row(sink_ref, g):
    return jnp.concatenate([jnp.full((1, WINDOW), sink_ref[g * SWA_GROUP + j], F32) for j in range(SWA_GROUP)], axis=1)


def _swa_fwd(qkv, bias_g, sinks, *, name):
    S = qkv.shape[1]
    qs, kp, kc, vp, vc, bs, stat, sk = _swa_specs()
    gs = range(SWA_KV_HEADS)

    def body(sink_ref, q_ref, kp_ref, kc_ref, vp_ref, vc_ref, b_ref, o_ref, lse_ref):
        masks = _swa_masks(pl.program_id(0))
        sc = [_swa_scores(g, q_ref, kp_ref, kc_ref, b_ref, masks) for g in gs]
        sinks_g = [_sink_row(sink_ref, g) for g in gs]
        m = [jnp.maximum(jnp.maximum(jnp.max(sc[g][2], axis=0, keepdims=True), jnp.max(sc[g][3], axis=0, keepdims=True)),
                         sinks_g[g]) for g in gs]
        p_p = [jnp.exp(sc[g][2] - m[g]) for g in gs]
        p_c = [jnp.exp(sc[g][3] - m[g]) for g in gs]
        l = [jnp.sum(p_p[g], axis=0, keepdims=True) + jnp.sum(p_c[g], axis=0, keepdims=True) + jnp.exp(sinks_g[g] - m[g])
             for g in gs]
        o = [_dot(vp_ref[sc[g][1], :], p_p[g].astype(BF16)) + _dot(vc_ref[sc[g][1], :], p_c[g].astype(BF16)) for g in gs]
        for g in gs:
            og = o[g] / l[g]
            lse = m[g] + jnp.log(l[g])
            for j in range(SWA_GROUP):
                h = g * SWA_GROUP + j
                o_ref[h * HEAD_DIM:(h + 1) * HEAD_DIM, :] = og[:, j * WINDOW:(j + 1) * WINDOW]
                lse_ref[h:h + 1, :] = lse[:, j * WINDOW:(j + 1) * WINDOW]

    return pl.pallas_call(
        body, name=name, grid=(S // WINDOW,),
        in_specs=[sk, qs, kp, kc, vp, vc, bs],
        out_specs=[qs, stat],
        out_shape=[jax.ShapeDtypeStruct((SWA_Q_HEADS * HEAD_DIM, S), F32), jax.ShapeDtypeStruct((SWA_Q_HEADS, S), F32)],
        compiler_params=_params("parallel"),
    )(sinks, qkv, qkv, qkv, qkv, qkv, bias_g)


def _swa_bwd(qkv, bias_g, sinks, do, lse, delta, *, name):
    S = qkv.shape[1]
    W = WINDOW
    qs, kp, kc, vp, vc, bs, stat, sk = _swa_specs()
    scale = HEAD_DIM ** -0.5
    kv_rows = SWA_KV_HEADS * HEAD_DIM
    gs = range(SWA_KV_HEADS)

    def body(sink_ref, q_ref, kp_ref, kc_ref, vp_ref, vc_ref, b_ref, do_ref, lse_ref, dl_ref,
             dq_ref, dkv_ref, db_ref, dsk_ref):
        i = pl.program_id(0)

        @pl.when(i == 0)
        def _():
            dkv_ref[...] = jnp.zeros_like(dkv_ref)
            db_ref[...] = jnp.zeros_like(db_ref)
            dsk_ref[...] = jnp.zeros_like(dsk_ref)

        masks = _swa_masks(i)
        prev = pl.ds(pl.multiple_of(jnp.maximum(i - 1, 0) * W, W), W)
        cur = pl.ds(pl.multiple_of(i * W, W), W)
        sc = [_swa_scores(g, q_ref, kp_ref, kc_ref, b_ref, masks) for g in gs]
        dog = [_group_lanes(do_ref, g, HEAD_DIM) for g in gs]
        lse = [_group_lanes(lse_ref, g, 1) for g in gs]
        dl = [_group_lanes(dl_ref, g, 1) for g in gs]
        p_p = [jnp.exp(sc[g][2] - lse[g]) for g in gs]
        p_c = [jnp.exp(sc[g][3] - lse[g]) for g in gs]
        ds_p = [p_p[g] * (_dot(vp_ref[sc[g][1], :], dog[g], TN) - dl[g]) for g in gs]
        ds_c = [p_c[g] * (_dot(vc_ref[sc[g][1], :], dog[g], TN) - dl[g]) for g in gs]
        for g in gs:
            db_ref[g, 0:W, :] += ds_p[g]
            db_ref[g, W:2 * W, :] += ds_c[g]
            dsk = jnp.exp(_sink_row(sink_ref, g) - lse[g]) * dl[g]
            for j in range(SWA_GROUP):
                h = g * SWA_GROUP + j
                dsk_ref[h:h + 1, :] -= jnp.broadcast_to(jnp.sum(dsk[:, j * W:(j + 1) * W], axis=1, keepdims=True), (1, LANES))
        dsb_p = [d.astype(BF16) for d in ds_p]
        dsb_c = [d.astype(BF16) for d in ds_c]
        for g in gs:
            rows = sc[g][1]
            dq = (_dot(kp_ref[rows, :], dsb_p[g]) + _dot(kc_ref[rows, :], dsb_c[g])) * scale
            for j in range(SWA_GROUP):
                h = g * SWA_GROUP + j
                dq_ref[h * HEAD_DIM:(h + 1) * HEAD_DIM, :] = dq[:, j * W:(j + 1) * W]
        for g in gs:
            rows = sc[g][1]
            vrows = slice(kv_rows + rows.start, kv_rows + rows.stop)
            dkv_ref[rows, prev] += _dot(sc[g][0], dsb_p[g], NT) * scale
            dkv_ref[rows, cur] += _dot(sc[g][0], dsb_c[g], NT) * scale
            dkv_ref[vrows, prev] += _dot(dog[g], p_p[g].astype(BF16), NT)
            dkv_ref[vrows, cur] += _dot(dog[g], p_c[g].astype(BF16), NT)

    return pl.pallas_call(
        body, name=name, grid=(S // W,),
        in_specs=[sk, qs, kp, kc, vp, vc, bs, qs, stat, stat],
        out_specs=[qs, pl.BlockSpec((2 * kv_rows, S), lambda i: (0, 0)), bs, pl.BlockSpec((SWA_Q_HEADS, LANES), lambda i: (0, 0))],
        out_shape=[jax.ShapeDtypeStruct((SWA_Q_HEADS * HEAD_DIM, S), F32), jax.ShapeDtypeStruct((2 * kv_rows, S), F32),
                   jax.ShapeDtypeStruct((SWA_KV_HEADS, 2 * W, GW), F32), jax.ShapeDtypeStruct((SWA_Q_HEADS, LANES), F32)],
        compiler_params=_params("arbitrary"),
    )(sinks, qkv, qkv, qkv, qkv, qkv, bias_g, do, lse, delta)


def _rel_onehot_t():
    qi = jnp.arange(WINDOW, dtype=jnp.int32)[None, :] + WINDOW
    kj = jnp.arange(2 * WINDOW, dtype=jnp.int32)[:, None]
    dist = qi - kj
    max_exact = REL_BUCKETS // 2
    d = jnp.maximum(dist, 0)
    log_ratio = jnp.log(jnp.maximum(d, 1).astype(F32) / max_exact) / math.log(REL_MAX_DIST / max_exact)
    large = jnp.minimum(max_exact + (log_ratio * (REL_BUCKETS - max_exact)).astype(jnp.int32), REL_BUCKETS - 1)
    bucket = jnp.where(d < max_exact, d, large).reshape(-1)
    return (bucket[None, :] == jnp.arange(REL_BUCKETS, dtype=jnp.int32)[:, None]).astype(BF16)


def _bias_table(rel_bias_t, onehot_t):
    Hq, NB = rel_bias_t.shape
    N = onehot_t.shape[1]
    tn = _tile(N, 4096)

    def body(r_ref, oh_ref, o_ref):
        oh = oh_ref[...]
        a1, a2, a3 = _split3(r_ref[...])
        o_ref[...] = _dot(a1, oh) + _dot(a2, oh) + _dot(a3, oh)

    return pl.pallas_call(
        body, name="rel_bias_table", grid=(N // tn,),
        in_specs=[pl.BlockSpec((Hq, NB), lambda j: (0, 0)), pl.BlockSpec((NB, tn), lambda j: (0, j))],
        out_specs=pl.BlockSpec((Hq, tn), lambda j: (0, j)),
        out_shape=jax.ShapeDtypeStruct((Hq, N), F32),
        compiler_params=_params("parallel"),
    )(rel_bias_t, onehot_t)


def _bias_table_bwd(dbias, onehot_t):
    L, Hq, N = dbias.shape
    NB = onehot_t.shape[0]
    tn = _tile(N, 4096)

    def body(d_ref, oh_ref, o_ref):
        @pl.when(pl.program_id(0) == 0)
        def _():
            o_ref[...] = jnp.zeros_like(o_ref)

        d = d_ref[0]
        for l in range(1, L):
            d = d + d_ref[l]
        oh = oh_ref[...]
        a1, a2, a3 = _split3(d)
        o_ref[...] += _dot(a1, oh, NT) + _dot(a2, oh, NT) + _dot(a3, oh, NT)

    return pl.pallas_call(
        body, name="rel_bias_bwd", grid=(N // tn,),
        in_specs=[pl.BlockSpec((L, Hq, tn), lambda j: (0, 0, j)), pl.BlockSpec((NB, tn), lambda j: (0, j))],
        out_specs=pl.BlockSpec((Hq, NB), lambda j: (0, 0)),
        out_shape=jax.ShapeDtypeStruct((Hq, NB), F32),
        compiler_params=_params("arbitrary"),
    )(dbias, onehot_t)


def _gate_fwd(lat, fb_col, *, name):
    S = lat.shape[1]
    tn = _tile(S, 256)

    def body(z_ref, fb_ref, o_ref, carry):
        @pl.when(pl.program_id(0) == 0)
        def _():
            carry[...] = jnp.zeros_like(carry)

        z = z_ref[...] + fb_ref[...]
        lf = jnp.minimum(z, 0.0) - jnp.log1p(jnp.exp(-jnp.abs(z)))
        r = lax.broadcasted_iota(jnp.int32, (tn, tn), 0)
        c = lax.broadcasted_iota(jnp.int32, (tn, tn), 1)
        tri = (r <= c).astype(BF16)
        a1, a2, a3 = _split3(lf)
        cum = _dot(a1, tri) + _dot(a2, tri) + _dot(a3, tri) + carry[:, 0:1]
        o_ref[...] = cum
        carry[...] = jnp.broadcast_to(cum[:, tn - 1:tn], carry.shape)

    return pl.pallas_call(
        body, name=name, grid=(S // tn,),
        in_specs=[pl.BlockSpec((GATE_ROWS, tn), lambda i: (0, i)), pl.BlockSpec((GATE_ROWS, 1), lambda i: (0, 0))],
        out_specs=pl.BlockSpec((GATE_ROWS, tn), lambda i: (0, i)),
        out_shape=jax.ShapeDtypeStruct((GATE_ROWS, S), F32),
        scratch_shapes=[pltpu.VMEM((GATE_ROWS, LANES), F32)],
        compiler_params=_params("arbitrary"),
    )(lat, fb_col)


def _gate_bwd(lat, fb_col, dF, *, name):
    S = lat.shape[1]
    tn = _tile(S, 256)
    nt = S // tn

    def body(z_ref, fb_ref, df_ref, dz_ref, dfb_ref, carry):
        @pl.when(pl.program_id(0) == 0)
        def _():
            carry[...] = jnp.zeros_like(carry)
            dfb_ref[...] = jnp.zeros_like(dfb_ref)

        r = lax.broadcasted_iota(jnp.int32, (tn, tn), 0)
        c = lax.broadcasted_iota(jnp.int32, (tn, tn), 1)
        tri = (r >= c).astype(BF16)
        a1, a2, a3 = _split3(df_ref[...])
        dlf = _dot(a1, tri) + _dot(a2, tri) + _dot(a3, tri) + carry[:, 0:1]
        carry[...] = jnp.broadcast_to(dlf[:, 0:1], carry.shape)
        z = z_ref[...] + fb_ref[...]
        row = lax.broadcasted_iota(jnp.int32, (GATE_ROWS, tn), 0)
        dz = jnp.where(row < FOX_HEADS, dlf / (1.0 + jnp.exp(z)), 0.0)
        dz_ref[...] = dz
        dfb_ref[...] += jnp.sum(dz, axis=1, keepdims=True)

    blk = pl.BlockSpec((GATE_ROWS, tn), lambda i: (0, nt - 1 - i))
    vec = pl.BlockSpec((GATE_ROWS, 1), lambda i: (0, 0))
    return pl.pallas_call(
        body, name=name, grid=(nt,),
        in_specs=[blk, vec, blk], out_specs=[blk, vec],
        out_shape=[jax.ShapeDtypeStruct((GATE_ROWS, S), F32), jax.ShapeDtypeStruct((GATE_ROWS, 1), F32)],
        scratch_shapes=[pltpu.VMEM((GATE_ROWS, LANES), F32)],
        compiler_params=_params("arbitrary"),
    )(lat, fb_col, dF)


def _rope_tables(S):
    pos = jnp.arange(S, dtype=F32)
    inv_freq = ROPE_THETA ** (-(jnp.arange(MLA_ROPE // 2, dtype=F32) * 2.0 / MLA_ROPE))
    ang = pos[:, None] * inv_freq[None, :]
    cos, sin = jnp.cos(ang).T, jnp.sin(ang).T
    z16 = jnp.zeros_like(cos)

    def slab(lo, fill):
        def put(first, second, f):
            return jnp.concatenate([jnp.full((lo, S), f, F32), first, second, jnp.full((LANES - lo - MLA_ROPE, S), f, F32)], axis=0)
        return put(cos, cos, fill), put(-sin, z16, 0.0), put(z16, sin, 0.0)

    tq = tuple(jnp.tile(t, (MLA_HEADS, 1)) for t in slab(MLA_NOPE, 1.0))
    return tq, slab(0, 0.0)


def _rope(x, c, s1, s2):
    n = x.shape[0]
    half = MLA_ROPE // 2
    return x * c + pltpu.roll(x, n - half, 0) * s1 + pltpu.roll(x, half, 0) * s2


def _rope_t(dy, c, s1, s2):
    n = dy.shape[0]
    half = MLA_ROPE // 2
    return dy * c + pltpu.roll(dy * s1, half, 0) + pltpu.roll(dy * s2, n - half, 0)


KR_SLAB0 = MLA_Q_RANK + MLA_KV_RANK


def _mla_prep_fwd(lat, g_q, g_kv, w_uq_t, w_ukv_t, tq, tmisc, *, name):
    S = lat.shape[1]
    tn = _tile(S, 512)
    QW = MLA_HEADS * MLA_PAD

    def body(lat_ref, gq_ref, gkv_ref, wq_ref, wkv_ref, c_ref, s1_ref, s2_ref, cm_ref, s1m_ref, s2m_ref,
             nq_ref, nkv_ref, q_ref, k_ref, v_ref):
        x = pltpu.roll(lat_ref[...], LAT_ROWS - LAT_SHIFT, 0)
        nq = _col_rms(x[0:MLA_Q_RANK, :], gq_ref[...]).astype(BF16)
        nkv = _col_rms(x[MLA_Q_RANK:KR_SLAB0, :], gkv_ref[...]).astype(BF16)
        nq_ref[...] = nq
        nkv_ref[...] = nkv
        q = _rope(_dot(wq_ref[...], nq), c_ref[...], s1_ref[...], s2_ref[...])
        q_ref[...] = (q * (MLA_SCALE * LOG2E)).astype(BF16)
        kv = _dot(wkv_ref[...], nkv).astype(BF16)
        kr = _rope(x[KR_SLAB0:LAT_ROWS, :], cm_ref[...], s1m_ref[...], s2m_ref[...]).astype(BF16)
        for h in range(MLA_HEADS):
            k_ref[h * MLA_PAD:h * MLA_PAD + MLA_NOPE, :] = kv[h * LANES:h * LANES + MLA_NOPE, :]
            k_ref[h * MLA_PAD + MLA_NOPE:(h + 1) * MLA_PAD, :] = kr[0:MLA_PAD - MLA_NOPE, :]
            v_ref[h * HEAD_DIM:(h + 1) * HEAD_DIM, :] = kv[h * LANES + MLA_NOPE:(h + 1) * LANES, :]

    def col(rows):
        return pl.BlockSpec((rows, tn), lambda i: (0, i))

    def full(a):
        return pl.BlockSpec(a.shape, lambda i: (0, 0))

    return pl.pallas_call(
        body, name=name, grid=(S // tn,),
        in_specs=[col(LAT_ROWS), full(g_q), full(g_kv), full(w_uq_t), full(w_ukv_t),
                  col(QW), col(QW), col(QW), col(LANES), col(LANES), col(LANES)],
        out_specs=[col(MLA_Q_RANK), col(MLA_KV_RANK), col(QW), col(QW), col(MLA_HEADS * HEAD_DIM)],
        out_shape=[jax.ShapeDtypeStruct((MLA_Q_RANK, S), BF16), jax.ShapeDtypeStruct((MLA_KV_RANK, S), BF16),
                   jax.ShapeDtypeStruct((QW, S), BF16), jax.ShapeDtypeStruct((QW, S), BF16),
                   jax.ShapeDtypeStruct((MLA_HEADS * HEAD_DIM, S), BF16)],
        compiler_params=_params("parallel"),
    )(lat, g_q, g_kv, w_uq_t, w_ukv_t, *tq, *tmisc)


def _mla_prep_bwd(lat, nq, nkv, g_q, g_kv, w_uq_p, w_ukv, tq, tmisc, dq, dk, dv, dflog, *, name):
    S = lat.shape[1]
    tn = _tile(S, 512)
    QW = MLA_HEADS * MLA_PAD

    def body(lat_ref, nq_ref, nkv_ref, gq_ref, gkv_ref, wq_ref, wkv_ref, c_ref, s1_ref, s2_ref,
             cm_ref, s1m_ref, s2m_ref, dq_ref, dk_ref, dv_ref, dfl_ref,
             dlat_ref, dwq_ref, dwkv_ref, dgq_ref, dgkv_ref, y_s):
        @pl.when(pl.program_id(0) == 0)
        def _():
            dwq_ref[...] = jnp.zeros_like(dwq_ref)
            dwkv_ref[...] = jnp.zeros_like(dwkv_ref)
            dgq_ref[...] = jnp.zeros_like(dgq_ref)
            dgkv_ref[...] = jnp.zeros_like(dgkv_ref)

        x = pltpu.roll(lat_ref[...], LAT_ROWS - LAT_SHIFT, 0)
        dqm = _rope_t(dq_ref[...], c_ref[...], s1_ref[...], s2_ref[...]).astype(BF16)
        dwq_ref[...] += _dot(dqm, nq_ref[...], NT)
        dx, dg = _col_rms_bwd(x[0:MLA_Q_RANK, :], gq_ref[...], _dot(wq_ref[...], dqm))
        y_s[0:MLA_Q_RANK, :] = dx
        dgq_ref[...] += dg
        dkv = jnp.concatenate(
            [part for h in range(MLA_HEADS)
             for part in (dk_ref[h * MLA_PAD:h * MLA_PAD + MLA_NOPE, :], dv_ref[h * HEAD_DIM:(h + 1) * HEAD_DIM, :])],
            axis=0).astype(BF16)
        dwkv_ref[...] += _dot(dkv, nkv_ref[...], NT)
        dx, dg = _col_rms_bwd(x[MLA_Q_RANK:KR_SLAB0, :], gkv_ref[...], _dot(wkv_ref[...], dkv))
        y_s[MLA_Q_RANK:KR_SLAB0, :] = dx
        dgkv_ref[...] += dg
        dkr = dk_ref[MLA_NOPE:MLA_PAD, :]
        for h in range(1, MLA_HEADS):
            dkr = dkr + dk_ref[h * MLA_PAD + MLA_NOPE:(h + 1) * MLA_PAD, :]
        dkr = jnp.concatenate([dkr, jnp.zeros((MLA_NOPE, tn), F32)], axis=0)
        y_s[KR_SLAB0:LAT_ROWS, :] = _rope_t(dkr, cm_ref[...], s1m_ref[...], s2m_ref[...])
        y = pltpu.roll(y_s[...], LAT_SHIFT, 0)
        row = lax.broadcasted_iota(jnp.int32, (LAT_ROWS, tn), 0)
        dfl = jnp.concatenate([dfl_ref[...], jnp.zeros((LAT_ROWS - GATE_ROWS, tn), F32)], axis=0)
        dlat_ref[...] = jnp.where(row < LAT_SHIFT, dfl, y).astype(BF16)

    def col(rows):
        return pl.BlockSpec((rows, tn), lambda i: (0, i))

    def full(a):
        return pl.BlockSpec(a.shape, lambda i: (0, 0))

    def acc(r, c):
        return pl.BlockSpec((r, c), lambda i: (0, 0))

    return pl.pallas_call(
        body, name=name, grid=(S // tn,),
        in_specs=[col(LAT_ROWS), col(MLA_Q_RANK), col(MLA_KV_RANK), full(g_q), full(g_kv),
                  full(w_uq_p), full(w_ukv), col(QW), col(QW), col(QW), col(LANES), col(LANES), col(LANES),
                  col(QW), col(QW), col(MLA_HEADS * HEAD_DIM), col(GATE_ROWS)],
        out_specs=[col(LAT_ROWS), acc(QW, MLA_Q_RANK), acc(QW, MLA_KV_RANK), acc(MLA_Q_RANK, 1), acc(MLA_KV_RANK, 1)],
        out_shape=[jax.ShapeDtypeStruct((LAT_ROWS, S), BF16), jax.ShapeDtypeStruct((QW, MLA_Q_RANK), F32),
                   jax.ShapeDtypeStruct((QW, MLA_KV_RANK), F32), jax.ShapeDtypeStruct((MLA_Q_RANK, 1), F32),
                   jax.ShapeDtypeStruct((MLA_KV_RANK, 1), F32)],
        scratch_shapes=[pltpu.VMEM((LAT_ROWS, tn), F32)],
        compiler_params=_params("arbitrary"),
    )(lat, nq, nkv, g_q, g_kv, w_uq_p, w_ukv, *tq, *tmisc, dq, dk, dv, dflog)


def _dproj_cast(dqa, dkva, dqf, dkf, dvf, dlat, *, name):
    S = dqa.shape[1]
    tn = _tile(S, 512)
    parts = (dqa, dkva, dqf, dkf, dvf, dlat)

    def body(*refs):
        o_ref = refs[-1]
        r0 = 0
        for ref in refs[:-1]:
            n = ref.shape[0]
            o_ref[r0:r0 + n, :] = ref[...].astype(BF16)
            r0 += n

    return pl.pallas_call(
        body, name=name, grid=(S // tn,),
        in_specs=[pl.BlockSpec((p.shape[0], tn), lambda i: (0, i)) for p in parts],
        out_specs=pl.BlockSpec((IN_ROWS, tn), lambda i: (0, i)),
        out_shape=jax.ShapeDtypeStruct((IN_ROWS, S), BF16),
        compiler_params=_params("parallel"),
    )(*parts)


GELU_C = math.sqrt(2.0 / math.pi)
GELU_A = 0.044715


HALO = 16


def _shift_down(a, k, fill):
    r = pltpu.roll(a, k, 0)
    row = lax.broadcasted_iota(jnp.int32, (8, a.shape[1]), 0)
    head = r[0:8, :]
    for i in range(k):
        head = jnp.where(row == i, fill[len(fill) - k + i], head)
    return jnp.concatenate([head, r[8:, :]], axis=0)


def _shift_up(d, k, fill):
    n = d.shape[0]
    r = pltpu.roll(d, n - k, 0)
    row = lax.broadcasted_iota(jnp.int32, (8, d.shape[1]), 0)
    tail = r[n - 8:n, :]
    for i in range(k):
        tail = jnp.where(row == 8 - k + i, fill[i], tail)
    return jnp.concatenate([r[0:n - 8, :], tail], axis=0)


def _conv_taps(a, before, w_ref, b_ref):
    a1 = _shift_down(a, 1, before)
    a2 = _shift_down(a, 2, before)
    return ((b_ref[...] + w_ref[0:1, :] * a2) + w_ref[1:2, :] * a1) + w_ref[2:3, :] * a


def _rows_before(halo_ref, first):
    h = halo_ref[HALO - 2:HALO, :].astype(F32)
    return jnp.where(first, 0.0, h[0:1, :]), jnp.where(first, 0.0, h[1:2, :])


def _conv_specs(S, tm, tc, nc):
    hb = tm // HALO
    main = lambda off: pl.BlockSpec((tm, tc), lambda j, i: (i, j + off))
    prev = lambda off: pl.BlockSpec((HALO, tc), lambda j, i: (jnp.maximum(i * hb - 1, 0), j + off))
    wspec = lambda off: pl.BlockSpec((3, tc), lambda j, i: (0, j + off))
    bspec = lambda off: pl.BlockSpec((1, tc), lambda j, i: (0, j + off))
    return main, prev, wspec, bspec


def _conv_geglu_fwd(a, conv_w, conv_b, *, name):
    S = a.shape[0]
    tm, tc = _tile(S, 512), _tile(D_FF, 1408)
    nc = D_FF // tc
    main, prev, wspec, bspec = _conv_specs(S, tm, tc, nc)

    def body(ag_ref, au_ref, hg_ref, hu_ref, wg_ref, wu_ref, bg_ref, bu_ref, u_ref, z_ref):
        first = pl.program_id(1) == 0
        gate = _conv_taps(ag_ref[...].astype(F32), _rows_before(hg_ref, first), wg_ref, bg_ref)
        up = _conv_taps(au_ref[...].astype(F32), _rows_before(hu_ref, first), wu_ref, bu_ref)
        u_ref[0] = gate
        u_ref[1] = up
        cdf = 0.5 * (1.0 + jnp.tanh(GELU_C * (gate + GELU_A * (gate * gate * gate))))
        z_ref[...] = (gate * cdf * up).astype(BF16)

    return pl.pallas_call(
        body, name=name, grid=(nc, S // tm),
        in_specs=[main(0), main(nc), prev(0), prev(nc), wspec(0), wspec(nc), bspec(0), bspec(nc)],
        out_specs=[pl.BlockSpec((2, tm, tc), lambda j, i: (0, i, j)), pl.BlockSpec((tm, tc), lambda j, i: (i, j))],
        out_shape=[jax.ShapeDtypeStruct((2, S, D_FF), F32), jax.ShapeDtypeStruct((S, D_FF), BF16)],
        compiler_params=_params("parallel", "arbitrary"),
    )(a, a, a, a, conv_w, conv_w, conv_b, conv_b)


def _geglu_bwd(gate, up, dz):
    g2x = gate * gate
    th = jnp.tanh(GELU_C * (gate + GELU_A * (g2x * gate)))
    cdf = 0.5 * (1.0 + th)
    dgelu = cdf + gate * (0.5 * (1.0 - th * th) * (GELU_C * (1.0 + 3.0 * GELU_A * g2x)))
    return dz * up * dgelu, dz * (gate * cdf)


def _conv_geglu_bwd(a, u, conv_w, dz, *, name):
    S = a.shape[0]
    tm, tc = _tile(S, 512), _tile(D_FF, 1408)
    nc = D_FF // tc
    nr = S // tm
    main, _, wspec, _ = _conv_specs(S, tm, tc, nc)
    hb = tm // 8

    def body(ag_ref, au_ref, u_ref, un_ref, wg_ref, wu_ref, dz_ref, dzn_ref, da_ref, dw_ref, db_ref):
        i = pl.program_id(1)
        last = i == nr - 1

        @pl.when(i == 0)
        def _():
            dw_ref[...] = jnp.zeros_like(dw_ref)
            db_ref[...] = jnp.zeros_like(db_ref)

        dus = _geglu_bwd(u_ref[0], u_ref[1], dz_ref[...])
        dus_n = _geglu_bwd(un_ref[0], un_ref[1], dzn_ref[...])
        for half, a_ref, w_ref in ((0, ag_ref, wg_ref), (1, au_ref, wu_ref)):
            du, du_n = dus[half], dus_n[half]
            after = (jnp.where(last, 0.0, du_n[0:1, :]), jnp.where(last, 0.0, du_n[1:2, :]))
            shifted = (_shift_up(du, 2, after), _shift_up(du, 1, after), du)
            da_ref[half] = (w_ref[2:3, :] * du + w_ref[1:2, :] * shifted[1] + w_ref[0:1, :] * shifted[0]).astype(BF16)
            af = a_ref[...].astype(F32)
            for tap in range(3):
                dw_ref[half, tap:tap + 1, :] += jnp.sum(shifted[tap] * af, axis=0, keepdims=True)
            db_ref[half] += jnp.sum(du, axis=0, keepdims=True)

    nxt8 = lambda j, i: (0, jnp.minimum((i + 1) * hb, S // 8 - 1), j)
    return pl.pallas_call(
        body, name=name, grid=(nc, nr),
        in_specs=[main(0), main(nc), pl.BlockSpec((2, tm, tc), lambda j, i: (0, i, j)), pl.BlockSpec((2, 8, tc), nxt8),
                  wspec(0), wspec(nc), pl.BlockSpec((tm, tc), lambda j, i: (i, j)),
                  pl.BlockSpec((8, tc), lambda j, i: (jnp.minimum((i + 1) * hb, S // 8 - 1), j))],
        out_specs=[pl.BlockSpec((2, tm, tc), lambda j, i: (0, i, j)), pl.BlockSpec((2, 3, tc), lambda j, i: (0, 0, j)),
                   pl.BlockSpec((2, 1, tc), lambda j, i: (0, 0, j))],
        out_shape=[jax.ShapeDtypeStruct((2, S, D_FF), BF16), jax.ShapeDtypeStruct((2, 3, D_FF), F32),
                   jax.ShapeDtypeStruct((2, 1, D_FF), F32)],
        compiler_params=_params("parallel", "arbitrary"),
    )(a, a, u, u, conv_w, conv_w, dz, dz)


ROW_BLOCK_BYTES = 1536 * 1024


def _row_tile(rows, cols):
    return rows if rows * cols * 4 <= ROW_BLOCK_BYTES else _tile(rows, ROW_TILE)


def _adamw_update(w, g, m, v):
    m = ADAM_B1 * m + (1.0 - ADAM_B1) * g
    v = ADAM_B2 * v + (1.0 - ADAM_B2) * jnp.square(g)
    m_hat = m / (1.0 - ADAM_B1 ** ADAM_STEP)
    v_hat = v / (1.0 - ADAM_B2 ** ADAM_STEP)
    return -ADAM_LR * (m_hat / (jnp.sqrt(v_hat) + ADAM_EPS) + ADAM_WD * w), m, v


def _adamw(w, g, m, v, *, name):
    L, A, B = w.shape
    ta = _tile(A, ROW_TILE)

    def body(w_ref, g_ref, m_ref, v_ref, d_ref, mo_ref, vo_ref):
        d_ref[...], mo_ref[...], vo_ref[...] = _adamw_update(w_ref[...], g_ref[...], m_ref[...], v_ref[...])

    blk = pl.BlockSpec((None, ta, B), lambda l, i: (l, i, 0))
    shp = jax.ShapeDtypeStruct((L, A, B), F32)
    return pl.pallas_call(
        body, name=name, grid=(L, A // ta),
        in_specs=[blk] * 4, out_specs=[blk] * 3, out_shape=[shp] * 3,
        compiler_params=_params("parallel", "parallel"),
    )(w, g, m, v)


def _scalar(v):
    return jnp.reshape(v, (1,)).astype(jnp.int32)


def _adamw_halves(w, g_mine, g_other, m, v, *, name):
    L, A, B = w.shape
    ta = _row_tile(A // 2, B)
    nb = A // 2 // ta

    def body(c_ref, w_ref, gm_ref, go_ref, m_ref, v_ref, g_ref, d_ref, mo_ref, vo_ref):
        g = jnp.where(pl.program_id(1) // nb == c_ref[0], gm_ref[...], go_ref[...])
        g_ref[...] = g
        d_ref[...], mo_ref[...], vo_ref[...] = _adamw_update(w_ref[...], g, m_ref[...], v_ref[...])

    blk = pl.BlockSpec((None, ta, B), lambda l, i, c_ref: (l, i, 0))
    half = pl.BlockSpec((None, ta, B), lambda l, i, c_ref: (l, i % nb, 0))
    shp = jax.ShapeDtypeStruct((L, A, B), F32)
    return pl.pallas_call(
        body, name=name,
        grid_spec=pltpu.PrefetchScalarGridSpec(num_scalar_prefetch=1, grid=(L, A // ta),
                                               in_specs=[blk, half, half, blk, blk], out_specs=[blk] * 4),
        out_shape=[shp] * 4,
        compiler_params=_params("parallel", "parallel"),
    )(_scalar(lax.axis_index("c")), w, g_mine, g_other, m, v)


def _chip_index():
    return 2 * lax.axis_index("x") + lax.axis_index("y")


def _pair_sum(g, recv, *, name):
    n, A, B = g.shape
    ta = _row_tile(A // 2, B)
    nb = A // 2 // ta

    def body(c_ref, g_ref, r_ref, o_ref):
        o_ref[...] = g_ref[...] + r_ref[...]

    return pl.pallas_call(
        body, name=name,
        grid_spec=pltpu.PrefetchScalarGridSpec(
            num_scalar_prefetch=1, grid=(n, nb),
            in_specs=[pl.BlockSpec((None, ta, B), lambda s, r, c_ref: (s, c_ref[0] * nb + r, 0)),
                      pl.BlockSpec((None, ta, B), lambda s, r, c_ref: (s, r, 0))],
            out_specs=pl.BlockSpec((None, ta, B), lambda s, r, c_ref: (s, r, 0))),
        out_shape=jax.ShapeDtypeStruct((n, A // 2, B), F32),
        compiler_params=_params("parallel", "parallel"),
    )(_scalar(lax.axis_index("c")), g, recv)


def _chip_sum(landed, own, *, name):
    n, A2, B = landed.shape
    ta = _row_tile(A2, B)

    def body(me_ref, *refs):
        slots, own_ref, o_ref = refs[:n], refs[n], refs[n + 1]
        parts = [jnp.where(me_ref[0] == s, own_ref[...], slots[s][...]) for s in range(n)]
        o_ref[...] = ((parts[0] + parts[1]) + parts[2]) + parts[3]

    def slot(s):
        return pl.BlockSpec((None, ta, B), lambda r, me_ref: (jnp.where(me_ref[0] == s, (s + 1) % n, s), r, 0))

    return pl.pallas_call(
        body, name=name,
        grid_spec=pltpu.PrefetchScalarGridSpec(
            num_scalar_prefetch=1, grid=(A2 // ta,),
            in_specs=[slot(s) for s in range(n)] + [pl.BlockSpec((None, ta, B), lambda r, me_ref: (me_ref[0], r, 0))],
            out_specs=pl.BlockSpec((ta, B), lambda r, me_ref: (r, 0))),
        out_shape=jax.ShapeDtypeStruct((A2, B), F32),
        compiler_params=_params("parallel"),
    )(_scalar(_chip_index()), *([landed] * n), own)


HBM_SPEC = pl.BlockSpec(memory_space=pl.ANY)
COMM_PARAMS = pltpu.CompilerParams(has_side_effects=True)


def _mesh_pos():
    return lax.axis_index("x"), lax.axis_index("y"), lax.axis_index("c")


def _other_chips(x, y):
    return [(1 - x, y), (x, 1 - y), (1 - x, 1 - y)]


def _remote(src, dst, send_sems, recv_sems, k, to):
    return pltpu.make_async_remote_copy(src_ref=src, dst_ref=dst, send_sem=send_sems.at[k], recv_sem=recv_sems.at[k],
                                        device_id=to, device_id_type=MESH)


def _place_own(gathered, shards, *, name):
    n = len(shards)

    def body(me_ref, *refs):
        for s_ref, o_ref in zip(refs[:n], refs[2 * n:]):
            o_ref[...] = s_ref[...]

    return pl.pallas_call(
        body, name=name,
        grid_spec=pltpu.PrefetchScalarGridSpec(
            num_scalar_prefetch=1, grid=(1,),
            in_specs=[pl.BlockSpec(s.shape, lambda i, me_ref: (0, 0)) for s in shards] + [HBM_SPEC] * n,
            out_specs=[pl.BlockSpec((None,) + s.shape, lambda i, me_ref: (me_ref[0], 0, 0)) for s in shards]),
        out_shape=[jax.ShapeDtypeStruct(g.shape, g.dtype) for g in gathered],
        input_output_aliases={1 + n + k: k for k in range(n)},
        compiler_params=_params("arbitrary"),
    )(_scalar(_chip_index()), *shards, *gathered)


def _half_rows(rows, c, align=8):
    assert (rows // 2) % align == 0
    return pl.ds(pl.multiple_of(c * (rows // 2), align), rows // 2)


BF16_ROWS = 16


def _halved(rows):
    return rows % (2 * BF16_ROWS) == 0


def _gather_copies(srcs, lands, send_sems, recv_sems):
    x, y, c = _mesh_pos()
    me = 2 * x + y
    out = []
    for k in range(len(srcs)):
        a = srcs[k].shape[0]
        rows = _half_rows(a, c, BF16_ROWS) if _halved(a) else pl.ds(0, a)
        for j, (px, py) in enumerate(_other_chips(x, y)):
            send = _remote(srcs[k].at[rows], lands[k].at[me, rows], send_sems, recv_sems, 3 * k + j, (px, py, c))
            recv = _remote(srcs[k].at[rows], lands[k].at[2 * px + py, rows], send_sems, recv_sems, 3 * k + j, (px, py, c))
            out.append((send, recv))
    return out


def _gather_start(srcs):
    nu = len(srcs)
    sizes = [len(su) for su in srcs]
    offs = [2 * sum(sizes[:u]) for u in range(nu + 1)]
    lands = [[lax.empty((N_CHIPS,) + s.shape, s.dtype) for s in su] for su in srcs]
    flat = [a for u in range(nu) for a in srcs[u] + lands[u]]

    def body(*refs):
        bufs, sems, token = refs[:len(flat)], refs[len(flat):len(flat) + 2 * nu], refs[-1]
        for u, n in enumerate(sizes):
            mine = bufs[offs[u]:offs[u + 1]]
            for send, _ in _gather_copies(mine[:n], mine[n:], sems[2 * u], sems[2 * u + 1]):
                send.start()
        token[...] = jnp.zeros_like(token)

    res = pl.pallas_call(
        body, name="weight_gather_start",
        in_specs=[HBM_ONLY] * len(flat),
        out_specs=[SEM_SPEC] * (2 * nu) + [HBM_ONLY] * len(flat) + [pl.BlockSpec(memory_space=pltpu.VMEM)],
        out_shape=[pltpu.SemaphoreType.DMA((3 * n,)) for n in sizes for _ in (0, 1)] + [pltpu.HBM(a.shape, a.dtype) for a in flat]
        + [jax.ShapeDtypeStruct((1, 1), F32)],
        input_output_aliases={i: 2 * nu + i for i in range(len(flat))},
        compiler_params=SPLIT_PARAMS,
    )(*[pltpu.with_memory_space_constraint(a, pltpu.HBM) for a in flat])
    bufs = res[2 * nu:2 * nu + len(flat)]
    state = [(res[2 * u], res[2 * u + 1], list(bufs[offs[u]:offs[u] + n]), list(bufs[offs[u] + n:offs[u + 1]]))
             for u, n in enumerate(sizes)]
    return state, res[-1]


def _gather_wait(state, after, *, name):
    send_sems, recv_sems, srcs, lands = state
    n = len(srcs)

    def body(*refs):
        for send, recv in _gather_copies(refs[:n], refs[n:2 * n], refs[2 * n], refs[2 * n + 1]):
            send.wait_send()
            recv.wait_recv()

    res = pl.pallas_call(
        body, name=name,
        in_specs=[HBM_ONLY] * (2 * n) + [SEM_SPEC, SEM_SPEC, HBM_SPEC],
        out_specs=[HBM_ONLY] * (2 * n),
        out_shape=[pltpu.HBM(a.shape, a.dtype) for a in srcs + lands],
        input_output_aliases={i: i for i in range(2 * n)},
        compiler_params=SPLIT_PARAMS,
    )(*srcs, *lands, send_sems, recv_sems, after)
    return list(res[:n]), list(res[n:])


def _gather_forward(lands, *, name):
    n = len(lands)

    def body(*refs):
        bufs, outs = refs[:n], refs[n:2 * n]
        send_sems, recv_sems = refs[2 * n:]
        x, y, c = _mesh_pos()
        copies, waits = [], []
        for k in range(n):
            a = lands[k].shape[1]
            if not _halved(a):
                continue
            for j, (px, py) in enumerate(_other_chips(x, y)):
                mine = 2 * px + py, _half_rows(a, c, BF16_ROWS)
                copies.append(_remote(bufs[k].at[mine], outs[k].at[mine], send_sems, recv_sems, 3 * k + j, (x, y, 1 - c)))
                lands_here = outs[k].at[2 * px + py, _half_rows(a, 1 - c, BF16_ROWS)]
                waits.append(_remote(lands_here, lands_here, send_sems, recv_sems, 3 * k + j, (x, y, 1 - c)))
        for cp in copies:
            cp.start()
        for cp in waits:
            cp.wait_recv()
        for cp in copies:
            cp.wait_send()

    return pl.pallas_call(
        body, name=name,
        in_specs=[HBM_SPEC] * n, out_specs=[HBM_SPEC] * n,
        out_shape=[jax.ShapeDtypeStruct(a.shape, a.dtype) for a in lands],
        scratch_shapes=[pltpu.SemaphoreType.DMA((3 * n,)), pltpu.SemaphoreType.DMA((3 * n,))],
        input_output_aliases={i: i for i in range(n)},
        compiler_params=COMM_PARAMS,
    )(*lands)


def _sibling_exchange(gs, *, name):
    n = len(gs)

    def body(*refs):
        ins, outs = refs[:n], refs[n:2 * n]
        send_sems, recv_sems = refs[2 * n:]
        x, y, c = _mesh_pos()
        copies = [_remote(ins[k].at[:, _half_rows(gs[k].shape[1], 1 - c)], outs[k], send_sems, recv_sems, k, (x, y, 1 - c))
                  for k in range(n)]
        for cp in copies:
            cp.start()
        for cp in copies:
            cp.wait()

    return pl.pallas_call(
        body, name=name,
        in_specs=[HBM_SPEC] * n, out_specs=[HBM_SPEC] * n,
        out_shape=[jax.ShapeDtypeStruct((g.shape[0], g.shape[1] // 2, g.shape[2]), g.dtype) for g in gs],
        scratch_shapes=[pltpu.SemaphoreType.DMA((n,)), pltpu.SemaphoreType.DMA((n,))],
        compiler_params=COMM_PARAMS,
    )(*gs)


HBM_ONLY = pl.BlockSpec(memory_space=pltpu.HBM)
SEM_SPEC = pl.BlockSpec(memory_space=pltpu.SEMAPHORE)
SPLIT_PARAMS = pltpu.CompilerParams(has_side_effects=pltpu.SideEffectType.DATAFLOW_SIDE_EFFECTING)


def _scatter_copies(srcs, lands, send_sems, recv_sems):
    x, y, c = _mesh_pos()
    me = 2 * x + y
    out = []
    for k in range(len(srcs)):
        for j, (px, py) in enumerate(_other_chips(x, y)):
            s = 2 * px + py
            send = _remote(srcs[k].at[s], lands[k].at[me], send_sems, recv_sems, 3 * k + j, (px, py, c))
            recv = _remote(srcs[k].at[s], lands[k].at[s], send_sems, recv_sems, 3 * k + j, (px, py, c))
            out.append((send, recv))
    return out


def _exchange_copies(srcs, lands, send_sems, recv_sems):
    x, y, c = _mesh_pos()
    out = []
    for k in range(len(srcs)):
        cp = _remote(srcs[k].at[:, _half_rows(srcs[k].shape[1], 1 - c)], lands[k], send_sems, recv_sems, k, (x, y, 1 - c))
        out.append((cp, cp))
    return out


def _split_start(srcs, land_shapes, copies, n_sems, *, name):
    n = len(srcs)
    lands = [lax.empty(shape, s.dtype) for shape, s in zip(land_shapes, srcs)]

    def body(*refs):
        ins, zones = refs[:n], refs[n:2 * n]
        send_sems, recv_sems, token = refs[2 * n], refs[2 * n + 1], refs[-1]
        for send, _ in copies(ins, zones, send_sems, recv_sems):
            send.start()
        token[...] = jnp.zeros_like(token)

    hbm = lambda a: pltpu.HBM(a.shape, a.dtype)
    res = pl.pallas_call(
        body, name=name,
        in_specs=[HBM_ONLY] * (2 * n),
        out_specs=[SEM_SPEC, SEM_SPEC] + [HBM_ONLY] * (2 * n) + [pl.BlockSpec(memory_space=pltpu.VMEM)],
        out_shape=[pltpu.SemaphoreType.DMA((n_sems,)), pltpu.SemaphoreType.DMA((n_sems,))] + [hbm(a) for a in srcs + lands]
        + [jax.ShapeDtypeStruct((1, 1), F32)],
        input_output_aliases={i: 2 + i for i in range(2 * n)},
        compiler_params=SPLIT_PARAMS,
    )(*[pltpu.with_memory_space_constraint(a, pltpu.HBM) for a in srcs + lands])
    return (res[0], res[1], list(res[2:2 + n]), list(res[2 + n:2 + 2 * n])), res[-1]


def _scatter_start(ps, *, name):
    return _split_start(ps, [p.shape for p in ps], _scatter_copies, 3 * len(ps), name=name)


def _exchange_start(gs, *, name):
    return _split_start(gs, [(g.shape[0], g.shape[1] // 2, g.shape[2]) for g in gs], _exchange_copies, len(gs), name=name)


def _split_wait(started, copies, after, *, name):
    ng = len(started)
    sizes = [len(st[2]) for st in started]
    offs = [2 * sum(sizes[:i]) for i in range(ng + 1)]
    flat = [a for (_, _, ps, lands) in started for a in ps + lands]

    def body(*refs):
        bufs, sems = refs[:len(flat)], refs[len(flat):len(flat) + 2 * ng]
        for i, n in enumerate(sizes):
            srcs, zones = bufs[offs[i]:offs[i] + n], bufs[offs[i] + n:offs[i + 1]]
            for send, recv in copies(srcs, zones, sems[2 * i], sems[2 * i + 1]):
                send.wait_send()
                recv.wait_recv()

    res = pl.pallas_call(
        body, name=name,
        in_specs=[HBM_ONLY] * len(flat) + [SEM_SPEC] * (2 * ng) + [HBM_SPEC],
        out_specs=[HBM_ONLY] * len(flat),
        out_shape=[pltpu.HBM(a.shape, a.dtype) for a in flat],
        input_output_aliases={i: i for i in range(len(flat))},
        compiler_params=SPLIT_PARAMS,
    )(*flat, *[s for (ss, rs, _, _) in started for s in (ss, rs)], after)
    return [(list(res[offs[i]:offs[i] + n]), list(res[offs[i] + n:offs[i + 1]])) for i, n in enumerate(sizes)]


def _sibling_share(hs):
    n = len(hs)

    def body(*refs):
        ins, outs = refs[:n], refs[n:2 * n]
        send_sems, recv_sems = refs[2 * n:]
        x, y, c = _mesh_pos()
        copies = [_remote(ins[k], outs[k], send_sems, recv_sems, k, (x, y, 1 - c)) for k in range(n)]
        for cp in copies:
            cp.start()
        for cp in copies:
            cp.wait()

    return pl.pallas_call(
        body, name="grad_sibling_share",
        in_specs=[HBM_SPEC] * n, out_specs=[HBM_SPEC] * n,
        out_shape=[jax.ShapeDtypeStruct(h.shape, h.dtype) for h in hs],
        scratch_shapes=[pltpu.SemaphoreType.DMA((n,)), pltpu.SemaphoreType.DMA((n,))],
        compiler_params=COMM_PARAMS,
    )(*hs)


def _allreduce_small(part, by_chip):
    rows, C = part.shape
    rows2 = by_chip.shape[1]

    def body(p_ref, q_ref, o_ref, o2_ref, slots, slots2, send_sems, recv_sems):
        x, y, c = _mesh_pos()
        me = 4 * x + 2 * y + c
        slots[me] = p_ref[...]
        slots2[me] = q_ref[2 * x + y]
        copies = []
        for k in range(1, 8):
            kx, ky, kc = (k >> 2) & 1, (k >> 1) & 1, k & 1
            peer = (x ^ kx if kx else x, y ^ ky if ky else y, c ^ kc if kc else c)
            src = 4 * peer[0] + 2 * peer[1] + peer[2]
            pair = []
            for j, (mine, zone, lands) in enumerate(((p_ref, slots.at[me], slots.at[src]),
                                                     (q_ref.at[2 * peer[0] + peer[1]], slots2.at[me], slots2.at[src]))):
                cp = _remote(mine, zone, send_sems, recv_sems, 2 * (k - 1) + j, peer)
                cp.start()
                pair.append((cp, _remote(mine, lands, send_sems, recv_sems, 2 * (k - 1) + j, peer)))
            copies += pair
        for _, landing in copies:
            landing.wait_recv()
        for cp, _ in copies:
            cp.wait_send()
        total, total2 = slots[0], slots2[0]
        for d in range(1, 8):
            total, total2 = total + slots[d], total2 + slots2[d]
        o_ref[...] = total
        o2_ref[...] = total2

    vmem = pl.BlockSpec(memory_space=pltpu.VMEM)
    return pl.pallas_call(
        body, name="small_grad_allreduce",
        in_specs=[vmem, vmem], out_specs=[vmem, vmem],
        out_shape=[jax.ShapeDtypeStruct((rows, C), F32), jax.ShapeDtypeStruct((rows2, C), F32)],
        scratch_shapes=[pltpu.VMEM((8, rows, C), F32), pltpu.VMEM((8, rows2, C), F32),
                        pltpu.SemaphoreType.DMA((14,)), pltpu.SemaphoreType.DMA((14,))],
        compiler_params=pltpu.CompilerParams(has_side_effects=True, vmem_limit_bytes=VMEM_LIMIT_BYTES),
    )(part, by_chip)


def _pad_w_uq(w):
    lead = w.shape[:-1]
    w = w.reshape(lead + (MLA_HEADS, MLA_QK))
    w = jnp.concatenate([w, jnp.zeros(lead + (MLA_HEADS, MLA_PAD - MLA_QK), w.dtype)], axis=-1)
    return w.reshape(lead + (MLA_HEADS * MLA_PAD,))


def _unpad_w_uq(g):
    lead = g.shape[:-1]
    return g.reshape(lead + (MLA_HEADS, MLA_PAD))[..., :MLA_QK].reshape(lead + (MLA_HEADS * MLA_QK,))


def _t(a):
    return jnp.swapaxes(a, -1, -2)


def _shards_of_cols(w):
    A, NB = w.shape
    return w.reshape(A, N_CHIPS, NB // N_CHIPS).transpose(1, 0, 2)


BIG = ("w_in", "w_uq", "w_ukv", "w_out", "w_up", "w_down")
SMALL = ("attn_pre_norm", "forget_bias", "swa_sinks", "rel_bias", "q_latent_norm", "kv_latent_norm", "group_norm",
         "attn_post_norm", "ffn_pre_norm", "conv_b", "ffn_post_norm")
WEIGHTS = ("attn_pre_norm", "w_in", "forget_bias", "swa_sinks", "rel_bias", "q_latent_norm", "w_uq", "kv_latent_norm",
           "w_ukv", "group_norm", "w_out", "attn_post_norm", "ffn_pre_norm", "w_up", "conv_w", "conv_b", "w_down",
           "ffn_post_norm")


PACK_UNIT = 8 * LANES


def _pack_rows(shape):
    return -(-int(np.prod(shape)) // PACK_UNIT) * 8


def _pack(arrs, row_mult=8):
    parts = []
    for a in arrs:
        n = int(np.prod(a.shape))
        parts.append(jnp.pad(a.reshape(-1), (0, _pack_rows(a.shape) * LANES - n)).reshape(-1, LANES))
    rows = sum(p.shape[0] for p in parts)
    pad = -rows % row_mult
    if pad:
        parts.append(jnp.zeros((pad, LANES), parts[0].dtype))
    return jnp.concatenate(parts, axis=0)


def _unpack(packed, shapes):
    packed = packed.reshape(-1, LANES)
    out, off = [], 0
    for shp in shapes:
        r = _pack_rows(shp)
        out.append(packed[off:off + r].reshape(-1)[:int(np.prod(shp))].reshape(shp))
        off += r
    return out


LAYER_KEYS = ("w_qkv_t", "w_lat_t", "w_in_t", "w_uq_p", "w_uq_t", "w_ukv", "w_ukv_t", "w_out", "w_up", "w_down", "conv_w")


MIX_WEIGHTS = ("w_in", "w_uq", "w_ukv", "w_out")
FFN_WEIGHTS = ("w_up", "w_down", "conv_w")


def _layer_weights(gathered):
    cols = lambda g: g.transpose(1, 0, 2).reshape(g.shape[1], N_CHIPS * g.shape[2])
    out = {}
    if "w_in" in gathered:
        w_in_t = _t(gathered["w_in"]).reshape(IN_COLS, D_MODEL)
        w_in_t = jnp.pad(w_in_t, ((0, IN_ROWS - IN_COLS), (0, 0)))
        w_uq_p = _pad_w_uq(cols(gathered["w_uq"]))
        w_ukv = cols(gathered["w_ukv"])
        out.update(w_qkv_t=w_in_t[:QKV_ROWS], w_lat_t=w_in_t[QKV_ROWS:], w_in_t=w_in_t, w_uq_p=w_uq_p, w_uq_t=_t(w_uq_p),
                   w_ukv=w_ukv, w_ukv_t=_t(w_ukv), w_out=gathered["w_out"].reshape(D_MODEL, D_MODEL))
    if "w_up" in gathered:
        out.update(w_up=gathered["w_up"], w_down=gathered["w_down"].reshape(D_FF, D_MODEL), conv_w=cols(gathered["conv_w"]))
    return out


def _local_step(x, target, W, layer_weights, layer_done):
    W = dict(W, **{key: [None] * DEPTH for key in LAYER_KEYS})
    S = x.shape[0]
    tq_tabs, tm_tabs = _rope_tables(S)
    onehot_t = _rel_onehot_t()
    bias_t = _bias_table(W["rel_bias"].T, onehot_t).reshape(SWA_KV_HEADS, SWA_GROUP, 2 * WINDOW, WINDOW)
    bias_t = bias_t.transpose(0, 2, 1, 3).reshape(SWA_KV_HEADS, 2 * WINDOW, GW)
    row = lambda a: a.reshape(1, -1)
    col = lambda a: a.reshape(-1, 1)
    fox_rows = (FOX_ROW0, FOX_ROW0 + FOX_HEADS * HEAD_DIM, FOX_ROW0 + 2 * FOX_HEADS * HEAD_DIM, SWA_Q_HEADS)
    fox = dict(rows=fox_rows, H=FOX_HEADS, Dk=HEAD_DIM, Dv=HEAD_DIM, scale=HEAD_DIM ** -0.5)
    mla = dict(rows=(0, 0, 0, SWA_Q_HEADS + FOX_HEADS), H=MLA_HEADS, Dk=MLA_PAD, Dv=HEAD_DIM, scale=MLA_SCALE, q_scaled=True)

    saved = []
    h = _rms_fwd(x, row(W["attn_pre_norm"][0]), name="rms_in")
    for l in range(DEPTH):
        sv = {"x0": x, "h1": h}
        for key, val in layer_weights(l, h, False).items():
            W[key][l] = val
        qkv = _matmul(W["w_qkv_t"][l], h, tb=True, out_dtype=BF16, name="proj_qkv")
        lat = _matmul(W["w_lat_t"][l], h, tb=True, name="proj_lat")
        oa, lse_a = _swa_fwd(qkv, bias_t, W["swa_sinks"][l], name="swa_fwd")
        fb_col = jnp.pad(col(W["forget_bias"][l]), ((0, GATE_ROWS - FOX_HEADS), (0, 0)))
        f4 = _gate_fwd(lat, fb_col, name="fox_gate_fwd")[:FOX_HEADS]
        f2 = f4 * LOG2E
        f_row, f_col = f2[:, None, :], f2.T
        of, lse_f = _attn_fwd(qkv, qkv, qkv, f_row=f_row, f_col=f_col, name="fox_fwd", **fox)
        nq, nkv, qm, km, vm = _mla_prep_fwd(lat, col(W["q_latent_norm"][l]), col(W["kv_latent_norm"][l]), W["w_uq_t"][l],
                                            W["w_ukv_t"][l], tq_tabs, tm_tabs, name="mla_prep_fwd")
        oc, lse_c = _attn_fwd(qm, km, vm, name="mla_fwd", **mla)
        mixed = _group_norm_fwd(oa, of, oc, col(W["group_norm"][l]), name="group_norm_fwd")
        y = _matmul(mixed, W["w_out"][l], ta=True, name="proj_out")
        x1, h2 = _resid_rms(x, y, row(W["attn_post_norm"][l]), row(W["ffn_pre_norm"][l]), name="attn_resid")
        for key, val in layer_weights(l, h2, True).items():
            W[key][l] = val
        a = _matmul(h2, W["w_up"][l], b_shards=True, out_dtype=BF16, name="ffn_up")
        u, z = _conv_geglu_fwd(a, W["conv_w"][l], row(W["conv_b"][l]), name="conv_geglu_fwd")
        y2 = _matmul(z, W["w_down"][l], name="ffn_down")
        g_next = row(W["attn_pre_norm"][l + 1]) if l + 1 < DEPTH else None
        x2, h_next = _resid_rms(x1, y2, row(W["ffn_post_norm"][l]), g_next, name="ffn_resid")
        sv.update(qkv=qkv, lat=lat, oa=oa, lse_a=lse_a, fb_col=fb_col, f_row=f_row, f_col=f_col, of=of, lse_f=lse_f,
                  nq=nq, nkv=nkv, qm=qm, km=km, vm=vm, oc=oc, lse_c=lse_c, mixed=mixed, y=y, x1=x1, h2=h2, a=a, u=u, z=z, y2=y2)
        saved.append(sv)
        x, h = x2, h_next

    loss, dx = _loss_head(x, target)

    G = {k: [None] * DEPTH for k in WEIGHTS if k != "rel_bias" and k not in BIG}
    dbias_layers = [None] * DEPTH
    for l in reversed(range(DEPTH)):
        sv = saved[l]
        gb = {}
        if l == DEPTH - 1:
            dy2, dg = _rms_bwd(sv["y2"], row(W["ffn_post_norm"][l]), dx, out_dtype=BF16, name="ffn_post_bwd")
            G["ffn_post_norm"][l] = dg[0]
        dz = _matmul(dy2, W["w_down"][l], tb=True, name="ffn_down_dx")
        gb["w_down"] = _matmul(sv["z"], dy2, ta=True, name="ffn_down_dw").reshape(N_CHIPS, D_FF // N_CHIPS, D_MODEL)
        da, dcw, dcb = _conv_geglu_bwd(sv["a"], sv["u"], W["conv_w"][l], dz, name="conv_geglu_bwd")
        G["conv_w"][l] = dcw.transpose(1, 0, 2).reshape(3, 2 * D_FF)
        G["conv_b"][l] = dcb.reshape(2 * D_FF)
        dh2 = _matmul(da, W["w_up"][l], tb=True, b_shards=True, a_halves=True, name="ffn_up_dx")
        gb["w_up"] = _matmul(sv["h2"], da, ta=True, out_shards=True, b_halves=True, name="ffn_up_dw")
        token = layer_done(l, gb)
        gb = {}
        dx1, dg, dy, dg_post = _rms_bwd(sv["x1"], row(W["ffn_pre_norm"][l]) + token, dh2, resid=dx, out_dtype=F32,
                                        then=(sv["y"], row(W["attn_post_norm"][l])), name="ffn_pre_bwd")
        G["ffn_pre_norm"][l] = dg[0]
        G["attn_post_norm"][l] = dg_post[0]
        dmixed = _matmul(W["w_out"][l], dy, tb=True, name="proj_out_dx")
        gb["w_out"] = _matmul(sv["mixed"], dy, name="proj_out_dw").reshape(N_CHIPS, D_MODEL // N_CHIPS, D_MODEL)
        doa, dof, doc, dg, delta = _group_norm_bwd(sv["oa"], sv["of"], sv["oc"], col(W["group_norm"][l]), dmixed,
                                                   name="group_norm_bwd")
        G["group_norm"][l] = dg[:, 0]
        dqa, dkva, dbias_l, dsink = _swa_bwd(sv["qkv"], bias_t, W["swa_sinks"][l], doa, sv["lse_a"],
                                             delta.reshape(-1, S), name="swa_bwd")
        dbias_layers[l] = (dbias_l.reshape(SWA_KV_HEADS, 2 * WINDOW, SWA_GROUP, WINDOW).transpose(0, 2, 1, 3)
                           .reshape(SWA_Q_HEADS, -1))
        G["swa_sinks"][l] = dsink[:, 0]
        dqf, dkf, dvf, dfk = _attn_bwd(sv["qkv"], sv["qkv"], sv["qkv"], do=dof, lse=sv["lse_f"], delta=delta,
                                       f_row=sv["f_row"], f_col=sv["f_col"], name="fox_bwd", **fox)
        dF = jnp.pad(dfk.T, ((0, GATE_ROWS - FOX_HEADS), (0, 0)))
        dflog, dfb = _gate_bwd(sv["lat"], sv["fb_col"], dF, name="fox_gate_bwd")
        G["forget_bias"][l] = dfb[:FOX_HEADS, 0]
        dqm, dkm, dvm = _attn_bwd(sv["qm"], sv["km"], sv["vm"], do=doc, lse=sv["lse_c"], delta=delta, name="mla_bwd", **mla)
        dlat, dwq_t, dwkv_t, dgq, dgkv = _mla_prep_bwd(
            sv["lat"], sv["nq"], sv["nkv"], col(W["q_latent_norm"][l]), col(W["kv_latent_norm"][l]), W["w_uq_p"][l],
            W["w_ukv"][l], tq_tabs, tm_tabs, dqm, dkm, dvm, dflog, name="mla_prep_bwd")
        gb["w_uq"], gb["w_ukv"] = _shards_of_cols(_unpad_w_uq(dwq_t.T)), _shards_of_cols(dwkv_t.T)
        G["q_latent_norm"][l], G["kv_latent_norm"][l] = dgq[:, 0], dgkv[:, 0]
        dproj = _dproj_cast(dqa, dkva, dqf, dkf, dvf, dlat, name="dproj_cast")
        dh1 = _matmul(dproj, W["w_in_t"][l], ta=True, name="proj_in_dx")
        dw_in_t = _matmul(dproj, sv["h1"], name="proj_in_dw")
        gb["w_in"] = _t(dw_in_t[:IN_COLS].reshape(N_CHIPS, IN_COLS // N_CHIPS, D_MODEL))
        token = layer_done(l, gb)
        below = (saved[l - 1]["y2"], row(W["ffn_post_norm"][l - 1])) if l > 0 else None
        res = _rms_bwd(sv["x0"], row(W["attn_pre_norm"][l]) + token, dh1, resid=dx1, out_dtype=F32, then=below,
                       name="attn_pre_bwd")
        dx, G["attn_pre_norm"][l] = res[0], res[1][0]
        if l > 0:
            dy2, G["ffn_post_norm"][l - 1] = res[2], res[3][0]

    grads = {k: jnp.stack(v) for k, v in G.items()}
    grads["rel_bias"] = _bias_table_bwd(jnp.stack(dbias_layers), onehot_t).T
    return loss, dx, grads


def kernel(x, attn_pre_norm, w_in, forget_bias, swa_sinks, rel_bias, q_latent_norm, w_uq, kv_latent_norm, w_ukv, group_norm, w_out, attn_post_norm, ffn_pre_norm, w_up, conv_w, conv_b, w_down, ffn_post_norm, loss_target, m_attn_pre_norm, m_w_in, m_forget_bias, m_swa_sinks, m_rel_bias, m_q_latent_norm, m_w_uq, m_kv_latent_norm, m_w_ukv, m_group_norm, m_w_out, m_attn_post_norm, m_ffn_pre_norm, m_w_up, m_conv_w, m_conv_b, m_w_down, m_ffn_post_norm, v_attn_pre_norm, v_w_in, v_forget_bias, v_swa_sinks, v_rel_bias, v_q_latent_norm, v_w_uq, v_kv_latent_norm, v_w_ukv, v_group_norm, v_w_out, v_attn_post_norm, v_ffn_pre_norm, v_w_up, v_conv_w, v_conv_b, v_w_down, v_ffn_post_norm):
    args = dict(locals())
    w = {k: args[k] for k in WEIGHTS}
    m = {k: args["m_" + k] for k in WEIGHTS}
    v = {k: args["v_" + k] for k in WEIGHTS}

    block = lambda l, keys: [w[k][l] if k == "conv_w" else w[k][l].astype(BF16) for k in keys]
    units = [(0, MIX_WEIGHTS), (0, FFN_WEIGHTS)] + [(l, MIX_WEIGHTS + FFN_WEIGHTS) for l in range(1, DEPTH)]
    gather_state, token = _gather_start([block(l, keys) for l, keys in units])
    W = {k: w[k] for k in SMALL}
    W["attn_pre_norm"] = W["attn_pre_norm"] + token

    def layer_weights(l, after, for_ffn):
        if for_ffn and l > 0:
            return {}
        keys = FFN_WEIGHTS if for_ffn else (MIX_WEIGHTS if l == 0 else MIX_WEIGHTS + FFN_WEIGHTS)
        tag = f"{l}_{keys[0]}"
        srcs, lands = _gather_wait(gather_state[units.index((l, keys))], after, name="weight_gather_wait_" + tag)
        lands = _gather_forward(lands, name="weight_gather_forward_" + tag)
        lands = _place_own(lands, srcs, name="place_own_shards")
        return _layer_weights(dict(zip(keys, lands)))

    started, groups, pending = [], [], []

    def to_chips(l, keys, gs, recv, tag):
        pair = [_pair_sum(gk, rk, name="grad_pair_sum") for gk, rk in zip(gs, recv)]
        state, token = _scatter_start(pair, name="grad_scatter_start_" + tag)
        started.append(state)
        groups.append((l, keys))
        return token

    def finish_pending(after):
        l, keys, tag, state = pending.pop()
        gs, recv = _split_wait([state], _exchange_copies, after, name="grad_exchange_wait_" + tag)[0]
        return to_chips(l, keys, gs, recv, tag)

    def layer_done(l, gb):
        keys = [k for k in BIG if k in gb]
        gs = [gb[k] for k in keys]
        tag = f"{l}_{keys[0]}"
        token = finish_pending(gs[0]) if pending else 0.0
        if l == 0:
            return token + to_chips(l, keys, gs, _sibling_exchange(gs, name="grad_sibling_exchange_" + tag), tag)
        state, started_token = _exchange_start(gs, name="grad_exchange_start_" + tag)
        pending.append((l, keys, tag, state))
        return token + started_token

    loss_part, dx, g = _local_step(x[0], loss_target[0], W, layer_weights, layer_done)
    loss = lax.psum(loss_part, ("x", "y", "c"))

    reduced = {}
    for (l, keys), (pair, zones) in zip(groups, _split_wait(started, _scatter_copies, dx, name="grad_scatter_wait")):
        for k, p, z in zip(keys, pair, zones):
            reduced[k, l] = _chip_sum(z, p, name="grad_chip_sum")
    mine = [jnp.stack([reduced[k, l] for l in range(DEPTH)]) for k in BIG]
    other = _sibling_share(mine)
    out_g, out_d, out_m, out_v = {}, {}, {}, {}
    for k, g_mine, g_other in zip(BIG, mine, other):
        out_g[k], out_d[k], out_m[k], out_v[k] = _adamw_halves(w[k], g_mine, g_other, m[k], v[k], name="adamw_" + k)

    small_shapes = [w[k].shape for k in SMALL]
    taps_by_chip = g["conv_w"].reshape(DEPTH, 3, N_CHIPS, FF_SHARD).transpose(2, 0, 1, 3)
    reduced, taps = _allreduce_small(_pack([g[k] for k in SMALL]), jnp.stack([_pack([t]) for t in taps_by_chip]))
    g_small = _unpack(reduced, small_shapes) + _unpack(taps, [w["conv_w"].shape])
    names = SMALL + ("conv_w",)
    shapes = small_shapes + [w["conv_w"].shape]
    packed = lambda arrs: _pack(arrs, ROW_TILE)[None]
    d_s, m_s, v_s = _adamw(packed([w[k] for k in names]), packed(g_small), packed([m[k] for k in names]),
                           packed([v[k] for k in names]), name="adamw_small")
    out_g.update(zip(names, g_small))
    out_d.update(zip(names, _unpack(d_s, shapes)))
    out_m.update(zip(names, _unpack(m_s, shapes)))
    out_v.update(zip(names, _unpack(v_s, shapes)))

    return (loss, dx[None], *[out_g[k] for k in WEIGHTS], *[out_d[k] for k in WEIGHTS],
            *[out_m[k] for k in WEIGHTS], *[out_v[k] for k in WEIGHTS])
```

```python
import math

import numpy as np
import jax
import jax.numpy as jnp
from jax import lax
from jax.experimental import pallas as pl
from jax.experimental.pallas import tpu as pltpu

F32 = jnp.float32
BF16 = jnp.bfloat16

D_MODEL = 1024
DEPTH = 4
HEAD_DIM = 64
SWA_Q_HEADS = 8
SWA_KV_HEADS = 2
SWA_GROUP = SWA_Q_HEADS // SWA_KV_HEADS
WINDOW = 128
FOX_HEADS = 4
MLA_HEADS = 4
MLA_Q_RANK = 256
MLA_KV_RANK = 128
MLA_NOPE = 64
MLA_ROPE = 32
MLA_QK = MLA_NOPE + MLA_ROPE
ROPE_THETA = 10000.0
REL_BUCKETS = 32
REL_MAX_DIST = 128
D_FF = 2816
EPS = 1e-6
NEG_INF = -1e30
LANES = 128
N_CHIPS = 4

IN_COLS = 1956
IN_ROWS = 2048
QKV_ROWS = 1536
LAT_ROWS = IN_ROWS - QKV_ROWS
LAT_SHIFT = FOX_HEADS
FOX_ROW0 = 768
MLA_PAD = LANES
GATE_ROWS = 8

ADAM_LR = 0.001
ADAM_B1 = 0.9
ADAM_B2 = 0.999
ADAM_EPS = 1e-08
ADAM_WD = 0.01
ADAM_STEP = 10

VMEM_LIMIT_BYTES = 48 * 1024 * 1024
ATT_TILE = 512
LOG2E = math.log2(math.e)
MLA_SCALE = MLA_QK ** -0.5
ROW_TILE = 256
MESH = pl.DeviceIdType.MESH

NT = (((1,), (1,)), ((), ()))
TN = (((0,), (0,)), ((), ()))
NN = (((1,), (0,)), ((), ()))


def _params(*sem):
    return pltpu.CompilerParams(dimension_semantics=sem, vmem_limit_bytes=VMEM_LIMIT_BYTES)


def _tile(dim, cap):
    for t in (2816, 2048, 1408, 1024, 512, 256, 128, 64, 32, 16, 8):
        if t <= cap and dim % t == 0:
            return t
    return dim


def _dot(a, b, dims=NN):
    return lax.dot_general(a, b, dims, preferred_element_type=F32)


def _split3(a):
    a1 = a.astype(BF16)
    r1 = a - a1.astype(F32)
    a2 = r1.astype(BF16)
    a3 = (r1 - a2.astype(F32)).astype(BF16)
    return a1, a2, a3


FF_SHARD = 2 * D_FF // N_CHIPS
MATMUL_VMEM_BYTES = 40 * 1024 * 1024
NORM_BWD_ROWS = 512


def _matmul(a, b, *, ta=False, tb=False, out_dtype=F32, name, b_shards=False, out_shards=False, a_halves=False,
            b_halves=False, norm_bwd=None):
    if a_halves:
        M, K = a.shape[1], 2 * a.shape[2]
    elif ta:
        K, M = a.shape
    else:
        M, K = a.shape
    if b_halves:
        K2, N = b.shape[1], 2 * b.shape[2]
    elif b_shards:
        K2, N = (2 * D_FF, D_MODEL) if tb else (D_MODEL, 2 * D_FF)
    elif tb:
        N, K2 = b.shape
    else:
        K2, N = b.shape
    assert K == K2, (a.shape, b.shape)
    tn = _tile(N, 1408)
    tk = FF_SHARD if (b_shards and tb) else _tile(K, 2816)
    out_bytes = jnp.dtype(out_dtype).itemsize
    tile_bytes = 4 + (2 * out_bytes if norm_bwd is None else 2 * (4 * 4 + 2))
    vmem = lambda tm, tk: 2 * 2 * tk * (tm + tn) + tile_bytes * tm * tn
    tm = M if M <= 2048 else _tile(M, 1408)
    if M > 2048 and M % 2048 == 0 and tk == K and vmem(2048, tk) <= MATMUL_VMEM_BYTES:
        tm = 2048
    if norm_bwd is not None:
        assert tn == N and not out_shards
        tm = NORM_BWD_ROWS
    while vmem(tm, tk) > MATMUL_VMEM_BYTES and tk % 256 == 0:
        tk //= 2
    nk = K // tk
    dims = (((0 if ta else 1,), (1 if tb else 0,)), ((), ()))
    if norm_bwd is not None:
        x, g, resid, then = norm_bwd
        chained = then is not None

        def fused(a_ref, b_ref, x_ref, g_ref, r_ref, *refs):
            x2_ref, g2_ref = refs[:2] if chained else (None, None)
            outs, acc_ref = refs[2 * chained:-1], refs[-1]
            i, k = pl.program_id(0), pl.program_id(2)

            @pl.when(k == 0)
            def _():
                acc_ref[...] = jnp.zeros_like(acc_ref)

            acc_ref[...] += lax.dot_general(a_ref[...], b_ref[...], dims, preferred_element_type=F32)

            @pl.when((k == nk - 1) & (i == 0))
            def _():
                for o in outs[1::2]:
                    o[...] = jnp.zeros_like(o)

            @pl.when(k == nk - 1)
            def _():
                dx, dg = _seg_rms_bwd(x_ref[...], g_ref[...], acc_ref[...])
                dx = dx + r_ref[...]
                outs[0][...] = dx
                outs[1][...] += dg
                if chained:
                    dx2, dg2 = _seg_rms_bwd(x2_ref[...], g2_ref[...], dx)
                    outs[2][...] = dx2.astype(BF16)
                    outs[3][...] += dg2

    def body(a_ref, b_ref, o_ref, acc_ref):
        k = pl.program_id(2)

        @pl.when(k == 0)
        def _():
            acc_ref[...] = jnp.zeros_like(acc_ref)

        acc_ref[...] += lax.dot_general(a_ref[...], b_ref[...], dims, preferred_element_type=F32)

        @pl.when(k == nk - 1)
        def _():
            o_ref[...] = acc_ref[...].astype(o_ref.dtype)

    if a_halves:
        nh = K // 2 // tk
        a_spec = pl.BlockSpec((None, tm, tk), lambda i, j, k: (k // nh, i, k % nh))
    else:
        a_spec = pl.BlockSpec((tk, tm), lambda i, j, k: (k, i)) if ta else pl.BlockSpec((tm, tk), lambda i, j, k: (i, k))
    if b_halves:
        nh = N // 2 // tn
        b_spec = pl.BlockSpec((None, tk, tn), lambda i, j, k: (j // nh, k, j % nh))
    elif b_shards and tb:
        assert tk == FF_SHARD
        b_spec = pl.BlockSpec((None, tn, tk), lambda i, j, k: (k, j, 0))
    elif b_shards:
        assert tn == FF_SHARD
        b_spec = pl.BlockSpec((None, tk, tn), lambda i, j, k: (j, k, 0))
    else:
        b_spec = pl.BlockSpec((tn, tk), lambda i, j, k: (j, k)) if tb else pl.BlockSpec((tk, tn), lambda i, j, k: (k, j))
    if out_shards:
        assert tn == FF_SHARD
        out_spec = pl.BlockSpec((None, tm, tn), lambda i, j, k: (j, i, 0))
        out_shape = jax.ShapeDtypeStruct((N // tn, M, tn), out_dtype)
    else:
        out_spec = pl.BlockSpec((tm, tn), lambda i, j, k: (i, j))
        out_shape = jax.ShapeDtypeStruct((M, N), out_dtype)
    if norm_bwd is not None:
        row = pl.BlockSpec((tm, N), lambda i, j, k: (i, 0))
        vec = pl.BlockSpec((1, N), lambda i, j, k: (0, 0))
        return pl.pallas_call(
            fused, name=name, grid=(M // tm, 1, nk),
            in_specs=[a_spec, b_spec, row, vec, row] + ([row, vec] if chained else []),
            out_specs=[row, vec] + ([row, vec] if chained else []),
            out_shape=[jax.ShapeDtypeStruct((M, N), F32), jax.ShapeDtypeStruct((1, N), F32)]
            + ([jax.ShapeDtypeStruct((M, N), BF16), jax.ShapeDtypeStruct((1, N), F32)] if chained else []),
            scratch_shapes=[pltpu.VMEM((tm, tn), F32)],
            compiler_params=_params("arbitrary", "arbitrary", "arbitrary"),
        )(a, b, x, g, resid, *(then if chained else ()))
    return pl.pallas_call(
        body, name=name, grid=(M // tm, N // tn, nk),
        in_specs=[a_spec, b_spec], out_specs=out_spec, out_shape=out_shape,
        scratch_shapes=[pltpu.VMEM((tm, tn), F32)],
        compiler_params=_params("parallel", "parallel", "arbitrary"),
    )(a, b)


def _seg_rms(xs, g):
    r = lax.rsqrt(jnp.mean(xs * xs, axis=-1, keepdims=True) + EPS)
    return xs * r * g


def _seg_rms_bwd(xs, g, dy):
    r = lax.rsqrt(jnp.mean(xs * xs, axis=-1, keepdims=True) + EPS)
    gd = dy * g
    c = jnp.mean(gd * xs, axis=-1, keepdims=True)
    dx = r * gd - xs * (r * r * r * c)
    dg = jnp.sum(dy * (xs * r), axis=0, keepdims=True)
    return dx, dg


def _rms_fwd(x, g, *, name):
    S, W = x.shape
    tm = _tile(S, 512)

    def body(x_ref, g_ref, o_ref):
        o_ref[...] = _seg_rms(x_ref[...], g_ref[...]).astype(o_ref.dtype)

    return pl.pallas_call(
        body, name=name, grid=(S // tm,),
        in_specs=[pl.BlockSpec((tm, W), lambda i: (i, 0)), pl.BlockSpec((1, W), lambda i: (0, 0))],
        out_specs=pl.BlockSpec((tm, W), lambda i: (i, 0)),
        out_shape=jax.ShapeDtypeStruct((S, W), BF16),
        compiler_params=_params("parallel"),
    )(x, g)


def _rms_bwd(x, g, dy, *, resid=None, out_dtype, name, then=None):
    S, W = x.shape
    tm = _tile(S, 512)
    has_resid = resid is not None
    chained = then is not None

    def body(*refs):
        refs = list(refs)
        x_ref, g_ref, dy_ref = refs[:3]
        r_ref = refs[3] if has_resid else None
        n_in = 3 + has_resid + 2 * chained
        x2_ref, g2_ref = (refs[n_in - 2], refs[n_in - 1]) if chained else (None, None)
        outs = refs[n_in:]
        dx_ref, dg_ref = outs[0], outs[1]

        @pl.when(pl.program_id(0) == 0)
        def _():
            dg_ref[...] = jnp.zeros_like(dg_ref)
            if chained:
                outs[3][...] = jnp.zeros_like(outs[3])

        dx, dg = _seg_rms_bwd(x_ref[...], g_ref[...], dy_ref[...])
        if has_resid:
            dx = dx + r_ref[...]
        dx_ref[...] = dx.astype(dx_ref.dtype)
        dg_ref[...] += dg
        if chained:
            dx2, dg2 = _seg_rms_bwd(x2_ref[...], g2_ref[...], dx)
            outs[2][...] = dx2.astype(BF16)
            outs[3][...] += dg2

    row = pl.BlockSpec((tm, W), lambda i: (i, 0))
    vec = pl.BlockSpec((1, W), lambda i: (0, 0))
    ins = [x, g, dy] + ([resid] if has_resid else []) + (list(then) if chained else [])
    return pl.pallas_call(
        body, name=name, grid=(S // tm,),
        in_specs=[row, vec, row] + ([row] if has_resid else []) + ([row, vec] if chained else []),
        out_specs=[row, vec] + ([row, vec] if chained else []),
        out_shape=[jax.ShapeDtypeStruct((S, W), out_dtype), jax.ShapeDtypeStruct((1, W), F32)]
        + ([jax.ShapeDtypeStruct((S, W), BF16), jax.ShapeDtypeStruct((1, W), F32)] if chained else []),
        compiler_params=_params("arbitrary"),
    )(*ins)


def _resid_rms(x, y, g_post, g_next, *, name):
    S, W = x.shape
    tm = _tile(S, 512)
    with_next = g_next is not None

    def body(*refs):
        if with_next:
            x_ref, y_ref, gp_ref, gn_ref, xo_ref, h_ref = refs
        else:
            x_ref, y_ref, gp_ref, xo_ref = refs
        xn = x_ref[...] + _seg_rms(y_ref[...], gp_ref[...])
        xo_ref[...] = xn
        if with_next:
            h_ref[...] = _seg_rms(xn, gn_ref[...]).astype(BF16)

    row = pl.BlockSpec((tm, W), lambda i: (i, 0))
    vec = pl.BlockSpec((1, W), lambda i: (0, 0))
    outs = [jax.ShapeDtypeStruct((S, W), F32)] + ([jax.ShapeDtypeStruct((S, W), BF16)] if with_next else [])
    res = pl.pallas_call(
        body, name=name, grid=(S // tm,),
        in_specs=[row, row, vec] + ([vec] if with_next else []),
        out_specs=[row] + ([row] if with_next else []),
        out_shape=outs,
        compiler_params=_params("parallel"),
    )(*([x, y, g_post] + ([g_next] if with_next else [])))
    return (res[0], res[1]) if with_next else (res[0], None)


def _col_rms(xs, g):
    r = lax.rsqrt(jnp.mean(xs * xs, axis=0, keepdims=True) + EPS)
    return xs * r * g


def _col_rms_bwd(xs, g, dy):
    r = lax.rsqrt(jnp.mean(xs * xs, axis=0, keepdims=True) + EPS)
    gd = dy * g
    c = jnp.mean(gd * xs, axis=0, keepdims=True)
    dx = r * gd - xs * (r * r * r * c)
    dg = jnp.sum(dy * (xs * r), axis=1, keepdims=True)
    return dx, dg


GROUP_ROWS = (SWA_Q_HEADS * HEAD_DIM, FOX_HEADS * HEAD_DIM, MLA_HEADS * HEAD_DIM)


def _group_specs(S, tn):
    outs = [pl.BlockSpec((n, tn), lambda i: (0, i)) for n in GROUP_ROWS]
    g = pl.BlockSpec((D_MODEL, 1), lambda i: (0, 0))
    mixed = pl.BlockSpec((D_MODEL, tn), lambda i: (0, i))
    return outs, g, mixed


def _group_norm_fwd(oa, of, oc, g, *, name):
    S = oa.shape[1]
    tn = _tile(S, 512)
    outs, gs, mixed = _group_specs(S, tn)

    def body(a_ref, f_ref, c_ref, g_ref, o_ref):
        r0 = 0
        for ref, n in zip((a_ref, f_ref, c_ref), GROUP_ROWS):
            o_ref[r0:r0 + n, :] = _col_rms(ref[...], g_ref[r0:r0 + n, :]).astype(BF16)
            r0 += n

    return pl.pallas_call(
        body, name=name, grid=(S // tn,),
        in_specs=outs + [gs], out_specs=mixed,
        out_shape=jax.ShapeDtypeStruct((D_MODEL, S), BF16),
        compiler_params=_params("parallel"),
    )(oa, of, oc, g)


def _group_norm_bwd(oa, of, oc, g, dmixed, *, name):
    S = oa.shape[1]
    tn = _tile(S, 512)
    outs, gs, mixed = _group_specs(S, tn)
    n_heads = D_MODEL // HEAD_DIM

    def body(a_ref, f_ref, c_ref, g_ref, dm_ref, da_ref, df_ref, dc_ref, dg_ref, dl_ref):
        @pl.when(pl.program_id(0) == 0)
        def _():
            dg_ref[...] = jnp.zeros_like(dg_ref)

        r0 = 0
        for ref, dref, n in zip((a_ref, f_ref, c_ref), (da_ref, df_ref, dc_ref), GROUP_ROWS):
            o = ref[...]
            dx, dg = _col_rms_bwd(o, g_ref[r0:r0 + n, :], dm_ref[r0:r0 + n, :])
            dxb = dx.astype(BF16)
            dref[...] = dxb
            dg_ref[r0:r0 + n, :] += dg
            od = o * dxb.astype(F32)
            for h in range(n // HEAD_DIM):
                dl_ref[r0 // HEAD_DIM + h] = jnp.sum(od[h * HEAD_DIM:(h + 1) * HEAD_DIM, :], axis=0, keepdims=True)
            r0 += n

    return pl.pallas_call(
        body, name=name, grid=(S // tn,),
        in_specs=outs + [gs, mixed], out_specs=outs + [gs, pl.BlockSpec((n_heads, 1, tn), lambda i: (0, 0, i))],
        out_shape=[jax.ShapeDtypeStruct((n, S), BF16) for n in GROUP_ROWS] + [jax.ShapeDtypeStruct((D_MODEL, 1), F32),
                                                                              jax.ShapeDtypeStruct((n_heads, 1, S), F32)],
        compiler_params=_params("arbitrary"),
    )(oa, of, oc, g, dmixed)


def _loss_head(y, target):
    S, W = y.shape
    tm = _tile(S, 512)

    def body(y_ref, t_ref, d_ref, l_ref):
        @pl.when(pl.program_id(0) == 0)
        def _():
            l_ref[...] = jnp.zeros_like(l_ref)

        err = y_ref[...] - t_ref[...]
        d_ref[...] = err * (1.0 / W)
        l_ref[...] += 0.5 * jnp.sum(jnp.mean(err * err, axis=-1, keepdims=True), axis=0, keepdims=True)

    row = pl.BlockSpec((tm, W), lambda i: (i, 0))
    d, l = pl.pallas_call(
        body, name="loss_head", grid=(S // tm,),
        in_specs=[row, row],
        out_specs=[row, pl.BlockSpec((1, 1), lambda i: (0, 0))],
        out_shape=[jax.ShapeDtypeStruct((S, W), F32), jax.ShapeDtypeStruct((1, 1), F32)],
        compiler_params=_params("arbitrary"),
    )(y, target)
    return l[0, 0], d


def _attn_fwd(q_src, k_src, v_src, rows, H, Dk, Dv, scale, f_row=None, f_col=None, *, name, q_scaled=False):
    S = q_src.shape[1]
    T = _tile(S, ATT_TILE)
    nq = S // T
    forget = f_row is not None
    qb, kb, vb = rows[0] // (H * Dk), rows[1] // (H * Dk), rows[2] // (H * Dv)
    hs = range(H)

    def body(*refs):
        if forget:
            q_ref, k_ref, v_ref, fq_ref, fk_ref, o_ref, lse_ref = refs
        else:
            q_ref, k_ref, v_ref, o_ref, lse_ref = refs
        i = pl.program_id(0)

        def tile(j, masked, state):
            off = pl.multiple_of(j * T, T)
            ss = [_dot(k_ref[h * Dk:(h + 1) * Dk, pl.ds(off, T)], q_ref[h * Dk:(h + 1) * Dk, :], TN) for h in hs]
            if not q_scaled:
                ss = [s * (scale * LOG2E) for s in ss]
            if forget:
                ss = [ss[h] + (fq_ref[h] - fk_ref[pl.ds(off, T), h:h + 1]) for h in hs]
            if masked:
                r = lax.broadcasted_iota(jnp.int32, (T, T), 0)
                c = lax.broadcasted_iota(jnp.int32, (T, T), 1)
                ss = [jnp.where(r <= c, s, NEG_INF) for s in ss]
            m_new = [jnp.maximum(state[h][0], jnp.max(ss[h], axis=0, keepdims=True)) for h in hs]
            alpha = [jnp.exp2(state[h][0] - m_new[h]) for h in hs]
            ps = [jnp.exp2(ss[h] - m_new[h]) for h in hs]
            l_new = [alpha[h] * state[h][1] + jnp.sum(ps[h], axis=0, keepdims=True) for h in hs]
            p_hi = [p.astype(BF16) for p in ps]
            vs = [v_ref[h * Dv:(h + 1) * Dv, pl.ds(off, T)] for h in hs]
            pv = [_dot(vs[h], p_hi[h]) for h in hs]
            if forget:
                pv = [pv[h] + _dot(vs[h], (ps[h] - p_hi[h].astype(F32)).astype(BF16)) for h in hs]
            return tuple((m_new[h], l_new[h], alpha[h] * state[h][2] + pv[h]) for h in hs)

        init = tuple((jnp.full((1, T), NEG_INF, F32), jnp.zeros((1, T), F32), jnp.zeros((Dv, T), F32)) for _ in hs)
        state = lax.fori_loop(0, i, lambda j, st: tile(j, False, st), init)
        state = tile(i, True, state)
        for h in hs:
            m, l, acc = state[h]
            o_ref[h * Dv:(h + 1) * Dv, :] = acc / l
            lse_ref[h] = m + jnp.log2(l)

    in_specs = [pl.BlockSpec((H * Dk, T), lambda i: (qb, i)),
                pl.BlockSpec((H * Dk, S), lambda i: (kb, 0)),
                pl.BlockSpec((H * Dv, S), lambda i: (vb, 0))]
    ins = [q_src, k_src, v_src]
    if forget:
        in_specs += [pl.BlockSpec((H, 1, T), lambda i: (0, 0, i)), pl.BlockSpec((S, H), lambda i: (0, 0))]
        ins += [f_row, f_col]
    return pl.pallas_call(
        body, name=name, grid=(nq,),
        in_specs=in_specs,
        out_specs=[pl.BlockSpec((H * Dv, T), lambda i: (0, i)), pl.BlockSpec((H, 1, T), lambda i: (0, 0, i))],
        out_shape=[jax.ShapeDtypeStruct((H * Dv, S), F32), jax.ShapeDtypeStruct((H, 1, S), F32)],
        compiler_params=_params("parallel"),
    )(*ins)


def _attn_bwd(q_src, k_src, v_src, rows, H, Dk, Dv, scale, do, lse, delta, f_row=None, f_col=None, *, name, q_scaled=False):
    S = q_src.shape[1]
    T = _tile(S, ATT_TILE)
    nq = S // T
    forget = f_row is not None
    qb, kb, vb, db = rows[0] // (H * Dk), rows[1] // (H * Dk), rows[2] // (H * Dv), rows[3] // H
    hs = range(H)

    def body(*refs):
        if forget:
            (q_ref, k_ref, v_ref, do_ref, lse_ref, dl_ref, fq_ref, fk_ref,
             dq_ref, dk_ref, dv_ref, df_ref, dk_s, dv_s, df_s) = refs
        else:
            q_ref, k_ref, v_ref, do_ref, lse_ref, dl_ref, dq_ref, dk_ref, dv_ref, dk_s, dv_s = refs
        j = pl.program_id(0)

        @pl.when(j == 0)
        def _():
            dq_ref[...] = jnp.zeros_like(dq_ref)

        dk_s[...] = jnp.zeros_like(dk_s)
        dv_s[...] = jnp.zeros_like(dv_s)
        if forget:
            df_s[...] = jnp.zeros_like(df_s)
        kt = [k_ref[h * Dk:(h + 1) * Dk, :] for h in hs]
        kj = [k.T for k in kt]
        vj = [v_ref[h * Dv:(h + 1) * Dv, :].T for h in hs]
        koff = pl.multiple_of(j * T, T)

        def tile(i, masked):
            cols = pl.ds(pl.multiple_of(i * T, T), T)
            qi = [q_ref[h * Dk:(h + 1) * Dk, cols] for h in hs]
            doi = [do_ref[h * Dv:(h + 1) * Dv, cols] for h in hs]
            st = [_dot(kj[h], qi[h]) for h in hs]
            if not q_scaled:
                st = [x * (scale * LOG2E) for x in st]
            if forget:
                st = [st[h] + (fq_ref[h, :, cols] - fk_ref[pl.ds(koff, T), h:h + 1]) for h in hs]
            if masked:
                r = lax.broadcasted_iota(jnp.int32, (T, T), 0)
                c = lax.broadcasted_iota(jnp.int32, (T, T), 1)
                st = [jnp.where(r <= c, x, NEG_INF) for x in st]
            pt = [jnp.exp2(st[h] - lse_ref[h, :, cols]) for h in hs]
            dpt = [_dot(vj[h], doi[h]) for h in hs]
            dst = [pt[h] * (dpt[h] - dl_ref[h, :, cols]) for h in hs]
            ptb = [p.astype(BF16) for p in pt]
            dsb = [d.astype(BF16) for d in dst]
            for h in hs:
                dv_s[h * Dv:(h + 1) * Dv, :] += _dot(doi[h], ptb[h], NT)
            for h in hs:
                dk_s[h * Dk:(h + 1) * Dk, :] += _dot(qi[h], dsb[h], NT)
            for h in hs:
                dq_ref[h * Dk:(h + 1) * Dk, cols] += _dot(kt[h], dsb[h]) * scale
            if forget:
                for h in hs:
                    part = dst[h][:, 0:LANES]
                    for c0 in range(LANES, T, LANES):
                        part = part + dst[h][:, c0:c0 + LANES]
                    df_s[h] += part

        tile(j, True)

        def loop_body(i, carry):
            tile(i, False)
            return carry

        lax.fori_loop(j + 1, nq, loop_body, 0)
        dk_ref[...] = dk_s[...] * ((1.0 / LOG2E) if q_scaled else scale)
        dv_ref[...] = dv_s[...]
        if forget:
            df_ref[...] = jnp.concatenate([-jnp.sum(df_s[h], axis=-1, keepdims=True) for h in hs], axis=1)

    res = lambda D, b0: pl.BlockSpec((H * D, S), lambda j: (b0, 0))
    blk = lambda D, b0: pl.BlockSpec((H * D, T), lambda j: (b0, j))
    row3 = lambda b0: pl.BlockSpec((H, 1, S), lambda j: (b0, 0, 0))
    in_specs = [res(Dk, qb), blk(Dk, kb), blk(Dv, vb), res(Dv, 0), row3(0), row3(db)]
    ins = [q_src, k_src, v_src, do, lse, delta]
    out_specs = [res(Dk, 0), blk(Dk, 0), blk(Dv, 0)]
    out_shape = [jax.ShapeDtypeStruct((H * Dk, S), F32), jax.ShapeDtypeStruct((H * Dk, S), F32),
                 jax.ShapeDtypeStruct((H * Dv, S), F32)]
    scratch = [pltpu.VMEM((H * Dk, T), F32), pltpu.VMEM((H * Dv, T), F32)]
    if forget:
        in_specs += [row3(0), pl.BlockSpec((S, H), lambda j: (0, 0))]
        ins += [f_row, f_col]
        out_specs.append(pl.BlockSpec((T, H), lambda j: (j, 0)))
        out_shape.append(jax.ShapeDtypeStruct((S, H), F32))
        scratch.append(pltpu.VMEM((H, T, min(T, LANES)), F32))
    return pl.pallas_call(
        body, name=name, grid=(nq,),
        in_specs=in_specs, out_specs=out_specs, out_shape=out_shape, scratch_shapes=scratch,
        compiler_params=_params("arbitrary"),
    )(*ins)


GW = SWA_GROUP * WINDOW


def _swa_masks(i):
    r = lax.broadcasted_iota(jnp.int32, (WINDOW, GW), 0)
    c = lax.broadcasted_iota(jnp.int32, (WINDOW, GW), 1) % WINDOW
    return (r > c) & (i > 0), r <= c


def _swa_specs():
    W = WINDOW
    kv_rows = SWA_KV_HEADS * HEAD_DIM
    q = pl.BlockSpec((SWA_Q_HEADS * HEAD_DIM, W), lambda i: (0, i))
    prev = lambda b: pl.BlockSpec((kv_rows, W), lambda i: (b, jnp.maximum(i - 1, 0)))
    cur = lambda b: pl.BlockSpec((kv_rows, W), lambda i: (b, i))
    bias = pl.BlockSpec((SWA_KV_HEADS, 2 * W, GW), lambda i: (0, 0, 0))
    stat = pl.BlockSpec((SWA_Q_HEADS, W), lambda i: (0, i))
    sink = pl.BlockSpec(memory_space=pltpu.SMEM)
    return q, prev(4), cur(4), prev(5), cur(5), bias, stat, sink


def _group_lanes(ref, g, rows_per_head):
    h0 = g * SWA_GROUP
    return jnp.concatenate([ref[(h0 + j) * rows_per_head:(h0 + j + 1) * rows_per_head, :] for j in range(SWA_GROUP)], axis=1)


def _swa_scores(g, q_ref, kp_ref, kc_ref, b_ref, masks):
    rows = slice(g * HEAD_DIM, (g + 1) * HEAD_DIM)
    qg = _group_lanes(q_ref, g, HEAD_DIM)
    scale = HEAD_DIM ** -0.5
    s_p = jnp.where(masks[0], _dot(kp_ref[rows, :], qg, TN) * scale + b_ref[g, 0:WINDOW, :], NEG_INF)
    s_c = jnp.where(masks[1], _dot(kc_ref[rows, :], qg, TN) * scale + b_ref[g, WINDOW:2 * WINDOW, :], NEG_INF)
    return qg, rows, s_p, s_c


def _sink_row(sink_ref, g):
    return jnp.concatenate([jnp.full((1, WINDOW), sink_ref[g * SWA_GROUP + j], F32) for j in range(SWA_GROUP)], axis=1)


def _swa_fwd(qkv, bias_g, sinks, *, name):
    S = qkv.shape[1]
    qs, kp, kc, vp, vc, bs, stat, sk = _swa_specs()
    gs = range(SWA_KV_HEADS)

    def body(sink_ref, q_ref, kp_ref, kc_ref, vp_ref, vc_ref, b_ref, o_ref, lse_ref):
        masks = _swa_masks(pl.program_id(0))
        sc = [_swa_scores(g, q_ref, kp_ref, kc_ref, b_ref, masks) for g in gs]
        sinks_g = [_sink_row(sink_ref, g) for g in gs]
        m = [jnp.maximum(jnp.maximum(jnp.max(sc[g][2], axis=0, keepdims=True), jnp.max(sc[g][3], axis=0, keepdims=True)),
                         sinks_g[g]) for g in gs]
        p_p = [jnp.exp(sc[g][2] - m[g]) for g in gs]
        p_c = [jnp.exp(sc[g][3] - m[g]) for g in gs]
        l = [jnp.sum(p_p[g], axis=0, keepdims=True) + jnp.sum(p_c[g], axis=0, keepdims=True) + jnp.exp(sinks_g[g] - m[g])
             for g in gs]
        o = [_dot(vp_ref[sc[g][1], :], p_p[g].astype(BF16)) + _dot(vc_ref[sc[g][1], :], p_c[g].astype(BF16)) for g in gs]
        for g in gs:
            og = o[g] / l[g]
            lse = m[g] + jnp.log(l[g])
            for j in range(SWA_GROUP):
                h = g * SWA_GROUP + j
                o_ref[h * HEAD_DIM:(h + 1) * HEAD_DIM, :] = og[:, j * WINDOW:(j + 1) * WINDOW]
                lse_ref[h:h + 1, :] = lse[:, j * WINDOW:(j + 1) * WINDOW]

    return pl.pallas_call(
        body, name=name, grid=(S // WINDOW,),
        in_specs=[sk, qs, kp, kc, vp, vc, bs],
        out_specs=[qs, stat],
        out_shape=[jax.ShapeDtypeStruct((SWA_Q_HEADS * HEAD_DIM, S), F32), jax.ShapeDtypeStruct((SWA_Q_HEADS, S), F32)],
        compiler_params=_params("parallel"),
    )(sinks, qkv, qkv, qkv, qkv, qkv, bias_g)


def _swa_bwd(qkv, bias_g, sinks, do, lse, delta, *, name):
    S = qkv.shape[1]
    W = WINDOW
    qs, kp, kc, vp, vc, bs, stat, sk = _swa_specs()
    scale = HEAD_DIM ** -0.5
    kv_rows = SWA_KV_HEADS * HEAD_DIM
    gs = range(SWA_KV_HEADS)

    def body(sink_ref, q_ref, kp_ref, kc_ref, vp_ref, vc_ref, b_ref, do_ref, lse_ref, dl_ref,
             dq_ref, dkv_ref, db_ref, dsk_ref):
        i = pl.program_id(0)

        @pl.when(i == 0)
        def _():
            dkv_ref[...] = jnp.zeros_like(dkv_ref)
            db_ref[...] = jnp.zeros_like(db_ref)
            dsk_ref[...] = jnp.zeros_like(dsk_ref)

        masks = _swa_masks(i)
        prev = pl.ds(pl.multiple_of(jnp.maximum(i - 1, 0) * W, W), W)
        cur = pl.ds(pl.multiple_of(i * W, W), W)
        sc = [_swa_scores(g, q_ref, kp_ref, kc_ref, b_ref, masks) for g in gs]
        dog = [_group_lanes(do_ref, g, HEAD_DIM) for g in gs]
        lse = [_group_lanes(lse_ref, g, 1) for g in gs]
        dl = [_group_lanes(dl_ref, g, 1) for g in gs]
        p_p = [jnp.exp(sc[g][2] - lse[g]) for g in gs]
        p_c = [jnp.exp(sc[g][3] - lse[g]) for g in gs]
        ds_p = [p_p[g] * (_dot(vp_ref[sc[g][1], :], dog[g], TN) - dl[g]) for g in gs]
        ds_c = [p_c[g] * (_dot(vc_ref[sc[g][1], :], dog[g], TN) - dl[g]) for g in gs]
        for g in gs:
            db_ref[g, 0:W, :] += ds_p[g]
            db_ref[g, W:2 * W, :] += ds_c[g]
            dsk = jnp.exp(_sink_row(sink_ref, g) - lse[g]) * dl[g]
            for j in range(SWA_GROUP):
                h = g * SWA_GROUP + j
                dsk_ref[h:h + 1, :] -= jnp.broadcast_to(jnp.sum(dsk[:, j * W:(j + 1) * W], axis=1, keepdims=True), (1, LANES))
        dsb_p = [d.astype(BF16) for d in ds_p]
        dsb_c = [d.astype(BF16) for d in ds_c]
        for g in gs:
            rows = sc[g][1]
            dq = (_dot(kp_ref[rows, :], dsb_p[g]) + _dot(kc_ref[rows, :], dsb_c[g])) * scale
            for j in range(SWA_GROUP):
                h = g * SWA_GROUP + j
                dq_ref[h * HEAD_DIM:(h + 1) * HEAD_DIM, :] = dq[:, j * W:(j + 1) * W]
        for g in gs:
            rows = sc[g][1]
            vrows = slice(kv_rows + rows.start, kv_rows + rows.stop)
            dkv_ref[rows, prev] += _dot(sc[g][0], dsb_p[g], NT) * scale
            dkv_ref[rows, cur] += _dot(sc[g][0], dsb_c[g], NT) * scale
            dkv_ref[vrows, prev] += _dot(dog[g], p_p[g].astype(BF16), NT)
            dkv_ref[vrows, cur] += _dot(dog[g], p_c[g].astype(BF16), NT)

    return pl.pallas_call(
        body, name=name, grid=(S // W,),
        in_specs=[sk, qs, kp, kc, vp, vc, bs, qs, stat, stat],
        out_specs=[qs, pl.BlockSpec((2 * kv_rows, S), lambda i: (0, 0)), bs, pl.BlockSpec((SWA_Q_HEADS, LANES), lambda i: (0, 0))],
        out_shape=[jax.ShapeDtypeStruct((SWA_Q_HEADS * HEAD_DIM, S), F32), jax.ShapeDtypeStruct((2 * kv_rows, S), F32),
                   jax.ShapeDtypeStruct((SWA_KV_HEADS, 2 * W, GW), F32), jax.ShapeDtypeStruct((SWA_Q_HEADS, LANES), F32)],
        compiler_params=_params("arbitrary"),
    )(sinks, qkv, qkv, qkv, qkv, qkv, bias_g, do, lse, delta)


def _rel_onehot_t():
    qi = jnp.arange(WINDOW, dtype=jnp.int32)[None, :] + WINDOW
    kj = jnp.arange(2 * WINDOW, dtype=jnp.int32)[:, None]
    dist = qi - kj
    max_exact = REL_BUCKETS // 2
    d = jnp.maximum(dist, 0)
    log_ratio = jnp.log(jnp.maximum(d, 1).astype(F32) / max_exact) / math.log(REL_MAX_DIST / max_exact)
    large = jnp.minimum(max_exact + (log_ratio * (REL_BUCKETS - max_exact)).astype(jnp.int32), REL_BUCKETS - 1)
    bucket = jnp.where(d < max_exact, d, large).reshape(-1)
    return (bucket[None, :] == jnp.arange(REL_BUCKETS, dtype=jnp.int32)[:, None]).astype(BF16)


def _bias_table(rel_bias_t, onehot_t):
    Hq, NB = rel_bias_t.shape
    N = onehot_t.shape[1]
    tn = _tile(N, 4096)

    def body(r_ref, oh_ref, o_ref):
        oh = oh_ref[...]
        a1, a2, a3 = _split3(r_ref[...])
        o_ref[...] = _dot(a1, oh) + _dot(a2, oh) + _dot(a3, oh)

    return pl.pallas_call(
        body, name="rel_bias_table", grid=(N // tn,),
        in_specs=[pl.BlockSpec((Hq, NB), lambda j: (0, 0)), pl.BlockSpec((NB, tn), lambda j: (0, j))],
        out_specs=pl.BlockSpec((Hq, tn), lambda j: (0, j)),
        out_shape=jax.ShapeDtypeStruct((Hq, N), F32),
        compiler_params=_params("parallel"),
    )(rel_bias_t, onehot_t)


def _bias_table_bwd(dbias, onehot_t):
    L, Hq, N = dbias.shape
    NB = onehot_t.shape[0]
    tn = _tile(N, 4096)

    def body(d_ref, oh_ref, o_ref):
        @pl.when(pl.program_id(0) == 0)
        def _():
            o_ref[...] = jnp.zeros_like(o_ref)

        d = d_ref[0]
        for l in range(1, L):
            d = d + d_ref[l]
        oh = oh_ref[...]
        a1, a2, a3 = _split3(d)
        o_ref[...] += _dot(a1, oh, NT) + _dot(a2, oh, NT) + _dot(a3, oh, NT)

    return pl.pallas_call(
        body, name="rel_bias_bwd", grid=(N // tn,),
        in_specs=[pl.BlockSpec((L, Hq, tn), lambda j: (0, 0, j)), pl.BlockSpec((NB, tn), lambda j: (0, j))],
        out_specs=pl.BlockSpec((Hq, NB), lambda j: (0, 0)),
        out_shape=jax.ShapeDtypeStruct((Hq, NB), F32),
        compiler_params=_params("arbitrary"),
    )(dbias, onehot_t)


def _gate_fwd(lat, fb_col, *, name):
    S = lat.shape[1]
    tn = _tile(S, 256)

    def body(z_ref, fb_ref, o_ref, carry):
        @pl.when(pl.program_id(0) == 0)
        def _():
            carry[...] = jnp.zeros_like(carry)

        z = z_ref[...] + fb_ref[...]
        lf = jnp.minimum(z, 0.0) - jnp.log1p(jnp.exp(-jnp.abs(z)))
        r = lax.broadcasted_iota(jnp.int32, (tn, tn), 0)
        c = lax.broadcasted_iota(jnp.int32, (tn, tn), 1)
        tri = (r <= c).astype(BF16)
        a1, a2, a3 = _split3(lf)
        cum = _dot(a1, tri) + _dot(a2, tri) + _dot(a3, tri) + carry[:, 0:1]
        o_ref[...] = cum
        carry[...] = jnp.broadcast_to(cum[:, tn - 1:tn], carry.shape)

    return pl.pallas_call(
        body, name=name, grid=(S // tn,),
        in_specs=[pl.BlockSpec((GATE_ROWS, tn), lambda i: (0, i)), pl.BlockSpec((GATE_ROWS, 1), lambda i: (0, 0))],
        out_specs=pl.BlockSpec((GATE_ROWS, tn), lambda i: (0, i)),
        out_shape=jax.ShapeDtypeStruct((GATE_ROWS, S), F32),
        scratch_shapes=[pltpu.VMEM((GATE_ROWS, LANES), F32)],
        compiler_params=_params("arbitrary"),
    )(lat, fb_col)


def _gate_bwd(lat, fb_col, dF, *, name):
    S = lat.shape[1]
    tn = _tile(S, 256)
    nt = S // tn

    def body(z_ref, fb_ref, df_ref, dz_ref, dfb_ref, carry):
        @pl.when(pl.program_id(0) == 0)
        def _():
            carry[...] = jnp.zeros_like(carry)
            dfb_ref[...] = jnp.zeros_like(dfb_ref)

        r = lax.broadcasted_iota(jnp.int32, (tn, tn), 0)
        c = lax.broadcasted_iota(jnp.int32, (tn, tn), 1)
        tri = (r >= c).astype(BF16)
        a1, a2, a3 = _split3(df_ref[...])
        dlf = _dot(a1, tri) + _dot(a2, tri) + _dot(a3, tri) + carry[:, 0:1]
        carry[...] = jnp.broadcast_to(dlf[:, 0:1], carry.shape)
        z = z_ref[...] + fb_ref[...]
        row = lax.broadcasted_iota(jnp.int32, (GATE_ROWS, tn), 0)
        dz = jnp.where(row < FOX_HEADS, dlf / (1.0 + jnp.exp(z)), 0.0)
        dz_ref[...] = dz
        dfb_ref[...] += jnp.sum(dz, axis=1, keepdims=True)

    blk = pl.BlockSpec((GATE_ROWS, tn), lambda i: (0, nt - 1 - i))
    vec = pl.BlockSpec((GATE_ROWS, 1), lambda i: (0, 0))
    return pl.pallas_call(
        body, name=name, grid=(nt,),
        in_specs=[blk, vec, blk], out_specs=[blk, vec],
        out_shape=[jax.ShapeDtypeStruct((GATE_ROWS, S), F32), jax.ShapeDtypeStruct((GATE_ROWS, 1), F32)],
        scratch_shapes=[pltpu.VMEM((GATE_ROWS, LANES), F32)],
        compiler_params=_params("arbitrary"),
    )(lat, fb_col, dF)


def _rope_tables(S):
    pos = jnp.arange(S, dtype=F32)
    inv_freq = ROPE_THETA ** (-(jnp.arange(MLA_ROPE // 2, dtype=F32) * 2.0 / MLA_ROPE))
    ang = pos[:, None] * inv_freq[None, :]
    cos, sin = jnp.cos(ang).T, jnp.sin(ang).T
    z16 = jnp.zeros_like(cos)

    def slab(lo, fill):
        def put(first, second, f):
            return jnp.concatenate([jnp.full((lo, S), f, F32), first, second, jnp.full((LANES - lo - MLA_ROPE, S), f, F32)], axis=0)
        return put(cos, cos, fill), put(-sin, z16, 0.0), put(z16, sin, 0.0)

    tq = tuple(jnp.tile(t, (MLA_HEADS, 1)) for t in slab(MLA_NOPE, 1.0))
    return tq, slab(0, 0.0)


def _rope(x, c, s1, s2):
    n = x.shape[0]
    half = MLA_ROPE // 2
    return x * c + pltpu.roll(x, n - half, 0) * s1 + pltpu.roll(x, half, 0) * s2


def _rope_t(dy, c, s1, s2):
    n = dy.shape[0]
    half = MLA_ROPE // 2
    return dy * c + pltpu.roll(dy * s1, half, 0) + pltpu.roll(dy * s2, n - half, 0)


KR_SLAB0 = MLA_Q_RANK + MLA_KV_RANK


def _mla_prep_fwd(lat, g_q, g_kv, w_uq_t, w_ukv_t, tq, tmisc, *, name):
    S = lat.shape[1]
    tn = _tile(S, 512)
    QW = MLA_HEADS * MLA_PAD

    def body(lat_ref, gq_ref, gkv_ref, wq_ref, wkv_ref, c_ref, s1_ref, s2_ref, cm_ref, s1m_ref, s2m_ref,
             nq_ref, nkv_ref, q_ref, k_ref, v_ref):
        x = pltpu.roll(lat_ref[...], LAT_ROWS - LAT_SHIFT, 0)
        nq = _col_rms(x[0:MLA_Q_RANK, :], gq_ref[...]).astype(BF16)
        nkv = _col_rms(x[MLA_Q_RANK:KR_SLAB0, :], gkv_ref[...]).astype(BF16)
        nq_ref[...] = nq
        nkv_ref[...] = nkv
        q = _rope(_dot(wq_ref[...], nq), c_ref[...], s1_ref[...], s2_ref[...])
        q_ref[...] = (q * (MLA_SCALE * LOG2E)).astype(BF16)
        kv = _dot(wkv_ref[...], nkv).astype(BF16)
        kr = _rope(x[KR_SLAB0:LAT_ROWS, :], cm_ref[...], s1m_ref[...], s2m_ref[...]).astype(BF16)
        for h in range(MLA_HEADS):
            k_ref[h * MLA_PAD:h * MLA_PAD + MLA_NOPE, :] = kv[h * LANES:h * LANES + MLA_NOPE, :]
            k_ref[h * MLA_PAD + MLA_NOPE:(h + 1) * MLA_PAD, :] = kr[0:MLA_PAD - MLA_NOPE, :]
            v_ref[h * HEAD_DIM:(h + 1) * HEAD_DIM, :] = kv[h * LANES + MLA_NOPE:(h + 1) * LANES, :]

    def col(rows):
        return pl.BlockSpec((rows, tn), lambda i: (0, i))

    def full(a):
        return pl.BlockSpec(a.shape, lambda i: (0, 0))

    return pl.pallas_call(
        body, name=name, grid=(S // tn,),
        in_specs=[col(LAT_ROWS), full(g_q), full(g_kv), full(w_uq_t), full(w_ukv_t),
                  col(QW), col(QW), col(QW), col(LANES), col(LANES), col(LANES)],
        out_specs=[col(MLA_Q_RANK), col(MLA_KV_RANK), col(QW), col(QW), col(MLA_HEADS * HEAD_DIM)],
        out_shape=[jax.ShapeDtypeStruct((MLA_Q_RANK, S), BF16), jax.ShapeDtypeStruct((MLA_KV_RANK, S), BF16),
                   jax.ShapeDtypeStruct((QW, S), BF16), jax.ShapeDtypeStruct((QW, S), BF16),
                   jax.ShapeDtypeStruct((MLA_HEADS * HEAD_DIM, S), BF16)],
        compiler_params=_params("parallel"),
    )(lat, g_q, g_kv, w_uq_t, w_ukv_t, *tq, *tmisc)


def _mla_prep_bwd(lat, nq, nkv, g_q, g_kv, w_uq_p, w_ukv, tq, tmisc, dq, dk, dv, dflog, *, name):
    S = lat.shape[1]
    tn = _tile(S, 512)
    QW = MLA_HEADS * MLA_PAD

    def body(lat_ref, nq_ref, nkv_ref, gq_ref, gkv_ref, wq_ref, wkv_ref, c_ref, s1_ref, s2_ref,
             cm_ref, s1m_ref, s2m_ref, dq_ref, dk_ref, dv_ref, dfl_ref,
             dlat_ref, dwq_ref, dwkv_ref, dgq_ref, dgkv_ref, y_s):
        @pl.when(pl.program_id(0) == 0)
        def _():
            dwq_ref[...] = jnp.zeros_like(dwq_ref)
            dwkv_ref[...] = jnp.zeros_like(dwkv_ref)
            dgq_ref[...] = jnp.zeros_like(dgq_ref)
            dgkv_ref[...] = jnp.zeros_like(dgkv_ref)

        x = pltpu.roll(lat_ref[...], LAT_ROWS - LAT_SHIFT, 0)
        dqm = _rope_t(dq_ref[...], c_ref[...], s1_ref[...], s2_ref[...]).astype(BF16)
        dwq_ref[...] += _dot(dqm, nq_ref[...], NT)
        dx, dg = _col_rms_bwd(x[0:MLA_Q_RANK, :], gq_ref[...], _dot(wq_ref[...], dqm))
        y_s[0:MLA_Q_RANK, :] = dx
        dgq_ref[...] += dg
        dkv = jnp.concatenate(
            [part for h in range(MLA_HEADS)
             for part in (dk_ref[h * MLA_PAD:h * MLA_PAD + MLA_NOPE, :], dv_ref[h * HEAD_DIM:(h + 1) * HEAD_DIM, :])],
            axis=0).astype(BF16)
        dwkv_ref[...] += _dot(dkv, nkv_ref[...], NT)
        dx, dg = _col_rms_bwd(x[MLA_Q_RANK:KR_SLAB0, :], gkv_ref[...], _dot(wkv_ref[...], dkv))
        y_s[MLA_Q_RANK:KR_SLAB0, :] = dx
        dgkv_ref[...] += dg
        dkr = dk_ref[MLA_NOPE:MLA_PAD, :]
        for h in range(1, MLA_HEADS):
            dkr = dkr + dk_ref[h * MLA_PAD + MLA_NOPE:(h + 1) * MLA_PAD, :]
        dkr = jnp.concatenate([dkr, jnp.zeros((MLA_NOPE, tn), F32)], axis=0)
        y_s[KR_SLAB0:LAT_ROWS, :] = _rope_t(dkr, cm_ref[...], s1m_ref[...], s2m_ref[...])
        y = pltpu.roll(y_s[...], LAT_SHIFT, 0)
        row = lax.broadcasted_iota(jnp.int32, (LAT_ROWS, tn), 0)
        dfl = jnp.concatenate([dfl_ref[...], jnp.zeros((LAT_ROWS - GATE_ROWS, tn), F32)], axis=0)
        dlat_ref[...] = jnp.where(row < LAT_SHIFT, dfl, y).astype(BF16)

    def col(rows):
        return pl.BlockSpec((rows, tn), lambda i: (0, i))

    def full(a):
        return pl.BlockSpec(a.shape, lambda i: (0, 0))

    def acc(r, c):
        return pl.BlockSpec((r, c), lambda i: (0, 0))

    return pl.pallas_call(
        body, name=name, grid=(S // tn,),
        in_specs=[col(LAT_ROWS), col(MLA_Q_RANK), col(MLA_KV_RANK), full(g_q), full(g_kv),
                  full(w_uq_p), full(w_ukv), col(QW), col(QW), col(QW), col(LANES), col(LANES), col(LANES),
                  col(QW), col(QW), col(MLA_HEADS * HEAD_DIM), col(GATE_ROWS)],
        out_specs=[col(LAT_ROWS), acc(QW, MLA_Q_RANK), acc(QW, MLA_KV_RANK), acc(MLA_Q_RANK, 1), acc(MLA_KV_RANK, 1)],
        out_shape=[jax.ShapeDtypeStruct((LAT_ROWS, S), BF16), jax.ShapeDtypeStruct((QW, MLA_Q_RANK), F32),
                   jax.ShapeDtypeStruct((QW, MLA_KV_RANK), F32), jax.ShapeDtypeStruct((MLA_Q_RANK, 1), F32),
                   jax.ShapeDtypeStruct((MLA_KV_RANK, 1), F32)],
        scratch_shapes=[pltpu.VMEM((LAT_ROWS, tn), F32)],
        compiler_params=_params("arbitrary"),
    )(lat, nq, nkv, g_q, g_kv, w_uq_p, w_ukv, *tq, *tmisc, dq, dk, dv, dflog)


def _dproj_cast(dqa, dkva, dqf, dkf, dvf, dlat, *, name):
    S = dqa.shape[1]
    tn = _tile(S, 512)
    parts = (dqa, dkva, dqf, dkf, dvf, dlat)

    def body(*refs):
        o_ref = refs[-1]
        r0 = 0
        for ref in refs[:-1]:
            n = ref.shape[0]
            o_ref[r0:r0 + n, :] = ref[...].astype(BF16)
            r0 += n

    return pl.pallas_call(
        body, name=name, grid=(S // tn,),
        in_specs=[pl.BlockSpec((p.shape[0], tn), lambda i: (0, i)) for p in parts],
        out_specs=pl.BlockSpec((IN_ROWS, tn), lambda i: (0, i)),
        out_shape=jax.ShapeDtypeStruct((IN_ROWS, S), BF16),
        compiler_params=_params("parallel"),
    )(*parts)


GELU_C = math.sqrt(2.0 / math.pi)
GELU_A = 0.044715


HALO = 16


def _shift_down(a, k, fill):
    r = pltpu.roll(a, k, 0)
    row = lax.broadcasted_iota(jnp.int32, (8, a.shape[1]), 0)
    head = r[0:8, :]
    for i in range(k):
        head = jnp.where(row == i, fill[len(fill) - k + i], head)
    return jnp.concatenate([head, r[8:, :]], axis=0)


def _shift_up(d, k, fill):
    n = d.shape[0]
    r = pltpu.roll(d, n - k, 0)
    row = lax.broadcasted_iota(jnp.int32, (8, d.shape[1]), 0)
    tail = r[n - 8:n, :]
    for i in range(k):
        tail = jnp.where(row == 8 - k + i, fill[i], tail)
    return jnp.concatenate([r[0:n - 8, :], tail], axis=0)


def _conv_taps(a, before, w_ref, b_ref):
    a1 = _shift_down(a, 1, before)
    a2 = _shift_down(a, 2, before)
    return ((b_ref[...] + w_ref[0:1, :] * a2) + w_ref[1:2, :] * a1) + w_ref[2:3, :] * a


def _rows_before(halo_ref, first):
    h = halo_ref[HALO - 2:HALO, :].astype(F32)
    return jnp.where(first, 0.0, h[0:1, :]), jnp.where(first, 0.0, h[1:2, :])


def _conv_specs(S, tm, tc, nc):
    hb = tm // HALO
    main = lambda off: pl.BlockSpec((tm, tc), lambda j, i: (i, j + off))
    prev = lambda off: pl.BlockSpec((HALO, tc), lambda j, i: (jnp.maximum(i * hb - 1, 0), j + off))
    wspec = lambda off: pl.BlockSpec((3, tc), lambda j, i: (0, j + off))
    bspec = lambda off: pl.BlockSpec((1, tc), lambda j, i: (0, j + off))
    return main, prev, wspec, bspec


def _conv_geglu_fwd(a, conv_w, conv_b, *, name):
    S = a.shape[0]
    tm, tc = _tile(S, 512), _tile(D_FF, 1408)
    nc = D_FF // tc
    main, prev, wspec, bspec = _conv_specs(S, tm, tc, nc)

    def body(ag_ref, au_ref, hg_ref, hu_ref, wg_ref, wu_ref, bg_ref, bu_ref, u_ref, z_ref):
        first = pl.program_id(1) == 0
        gate = _conv_taps(ag_ref[...].astype(F32), _rows_before(hg_ref, first), wg_ref, bg_ref)
        up = _conv_taps(au_ref[...].astype(F32), _rows_before(hu_ref, first), wu_ref, bu_ref)
        u_ref[0] = gate
        u_ref[1] = up
        cdf = 0.5 * (1.0 + jnp.tanh(GELU_C * (gate + GELU_A * (gate * gate * gate))))
        z_ref[...] = (gate * cdf * up).astype(BF16)

    return pl.pallas_call(
        body, name=name, grid=(nc, S // tm),
        in_specs=[main(0), main(nc), prev(0), prev(nc), wspec(0), wspec(nc), bspec(0), bspec(nc)],
        out_specs=[pl.BlockSpec((2, tm, tc), lambda j, i: (0, i, j)), pl.BlockSpec((tm, tc), lambda j, i: (i, j))],
        out_shape=[jax.ShapeDtypeStruct((2, S, D_FF), F32), jax.ShapeDtypeStruct((S, D_FF), BF16)],
        compiler_params=_params("parallel", "arbitrary"),
    )(a, a, a, a, conv_w, conv_w, conv_b, conv_b)


def _geglu_bwd(gate, up, dz):
    g2x = gate * gate
    th = jnp.tanh(GELU_C * (gate + GELU_A * (g2x * gate)))
    cdf = 0.5 * (1.0 + th)
    dgelu = cdf + gate * (0.5 * (1.0 - th * th) * (GELU_C * (1.0 + 3.0 * GELU_A * g2x)))
    return dz * up * dgelu, dz * (gate * cdf)


def _conv_geglu_bwd(a, u, conv_w, dz, *, name):
    S = a.shape[0]
    tm, tc = _tile(S, 512), _tile(D_FF, 1408)
    nc = D_FF // tc
    nr = S // tm
    main, _, wspec, _ = _conv_specs(S, tm, tc, nc)
    hb = tm // 8

    def body(ag_ref, au_ref, u_ref, un_ref, wg_ref, wu_ref, dz_ref, dzn_ref, da_ref, dw_ref, db_ref):
        i = pl.program_id(1)
        last = i == nr - 1

        @pl.when(i == 0)
        def _():
            dw_ref[...] = jnp.zeros_like(dw_ref)
            db_ref[...] = jnp.zeros_like(db_ref)

        dus = _geglu_bwd(u_ref[0], u_ref[1], dz_ref[...])
        dus_n = _geglu_bwd(un_ref[0], un_ref[1], dzn_ref[...])
        for half, a_ref, w_ref in ((0, ag_ref, wg_ref), (1, au_ref, wu_ref)):
            du, du_n = dus[half], dus_n[half]
            after = (jnp.where(last, 0.0, du_n[0:1, :]), jnp.where(last, 0.0, du_n[1:2, :]))
            shifted = (_shift_up(du, 2, after), _shift_up(du, 1, after), du)
            da_ref[half] = (w_ref[2:3, :] * du + w_ref[1:2, :] * shifted[1] + w_ref[0:1, :] * shifted[0]).astype(BF16)
            af = a_ref[...].astype(F32)
            for tap in range(3):
                dw_ref[half, tap:tap + 1, :] += jnp.sum(shifted[tap] * af, axis=0, keepdims=True)
            db_ref[half] += jnp.sum(du, axis=0, keepdims=True)

    nxt8 = lambda j, i: (0, jnp.minimum((i + 1) * hb, S // 8 - 1), j)
    return pl.pallas_call(
        body, name=name, grid=(nc, nr),
        in_specs=[main(0), main(nc), pl.BlockSpec((2, tm, tc), lambda j, i: (0, i, j)), pl.BlockSpec((2, 8, tc), nxt8),
                  wspec(0), wspec(nc), pl.BlockSpec((tm, tc), lambda j, i: (i, j)),
                  pl.BlockSpec((8, tc), lambda j, i: (jnp.minimum((i + 1) * hb, S // 8 - 1), j))],
        out_specs=[pl.BlockSpec((2, tm, tc), lambda j, i: (0, i, j)), pl.BlockSpec((2, 3, tc), lambda j, i: (0, 0, j)),
                   pl.BlockSpec((2, 1, tc), lambda j, i: (0, 0, j))],
        out_shape=[jax.ShapeDtypeStruct((2, S, D_FF), BF16), jax.ShapeDtypeStruct((2, 3, D_FF), F32),
                   jax.ShapeDtypeStruct((2, 1, D_FF), F32)],
        compiler_params=_params("parallel", "arbitrary"),
    )(a, a, u, u, conv_w, conv_w, dz, dz)


ROW_BLOCK_BYTES = 1536 * 1024


def _row_tile(rows, cols):
    return rows if rows * cols * 4 <= ROW_BLOCK_BYTES else _tile(rows, ROW_TILE)


def _adamw_update(w, g, m, v):
    m = ADAM_B1 * m + (1.0 - ADAM_B1) * g
    v = ADAM_B2 * v + (1.0 - ADAM_B2) * jnp.square(g)
    m_hat = m / (1.0 - ADAM_B1 ** ADAM_STEP)
    v_hat = v / (1.0 - ADAM_B2 ** ADAM_STEP)
    return -ADAM_LR * (m_hat / (jnp.sqrt(v_hat) + ADAM_EPS) + ADAM_WD * w), m, v


def _adamw(w, g, m, v, *, name):
    L, A, B = w.shape
    ta = _tile(A, ROW_TILE)

    def body(w_ref, g_ref, m_ref, v_ref, d_ref, mo_ref, vo_ref):
        d_ref[...], mo_ref[...], vo_ref[...] = _adamw_update(w_ref[...], g_ref[...], m_ref[...], v_ref[...])

    blk = pl.BlockSpec((None, ta, B), lambda l, i: (l, i, 0))
    shp = jax.ShapeDtypeStruct((L, A, B), F32)
    return pl.pallas_call(
        body, name=name, grid=(L, A // ta),
        in_specs=[blk] * 4, out_specs=[blk] * 3, out_shape=[shp] * 3,
        compiler_params=_params("parallel", "parallel"),
    )(w, g, m, v)


def _scalar(v):
    return jnp.reshape(v, (1,)).astype(jnp.int32)


def _adamw_halves(w, g_mine, g_other, m, v, *, name):
    L, A, B = w.shape
    ta = _row_tile(A // 2, B)
    nb = A // 2 // ta

    def body(c_ref, w_ref, gm_ref, go_ref, m_ref, v_ref, g_ref, d_ref, mo_ref, vo_ref):
        g = jnp.where(pl.program_id(1) // nb == c_ref[0], gm_ref[...], go_ref[...])
        g_ref[...] = g
        d_ref[...], mo_ref[...], vo_ref[...] = _adamw_update(w_ref[...], g, m_ref[...], v_ref[...])

    blk = pl.BlockSpec((None, ta, B), lambda l, i, c_ref: (l, i, 0))
    half = pl.BlockSpec((None, ta, B), lambda l, i, c_ref: (l, i % nb, 0))
    shp = jax.ShapeDtypeStruct((L, A, B), F32)
    return pl.pallas_call(
        body, name=name,
        grid_spec=pltpu.PrefetchScalarGridSpec(num_scalar_prefetch=1, grid=(L, A // ta),
                                               in_specs=[blk, half, half, blk, blk], out_specs=[blk] * 4),
        out_shape=[shp] * 4,
        compiler_params=_params("parallel", "parallel"),
    )(_scalar(lax.axis_index("c")), w, g_mine, g_other, m, v)


def _chip_index():
    return 2 * lax.axis_index("x") + lax.axis_index("y")


def _pair_sum(g, recv, *, name):
    n, A, B = g.shape
    ta = _row_tile(A // 2, B)
    nb = A // 2 // ta

    def body(c_ref, g_ref, r_ref, o_ref):
        o_ref[...] = g_ref[...] + r_ref[...]

    return pl.pallas_call(
        body, name=name,
        grid_spec=pltpu.PrefetchScalarGridSpec(
            num_scalar_prefetch=1, grid=(n, nb),
            in_specs=[pl.BlockSpec((None, ta, B), lambda s, r, c_ref: (s, c_ref[0] * nb + r, 0)),
                      pl.BlockSpec((None, ta, B), lambda s, r, c_ref: (s, r, 0))],
            out_specs=pl.BlockSpec((None, ta, B), lambda s, r, c_ref: (s, r, 0))),
        out_shape=jax.ShapeDtypeStruct((n, A // 2, B), F32),
        compiler_params=_params("parallel", "parallel"),
    )(_scalar(lax.axis_index("c")), g, recv)


def _chip_sum(landed, own, *, name):
    n, A2, B = landed.shape
    ta = _row_tile(A2, B)

    def body(me_ref, *refs):
        slots, own_ref, o_ref = refs[:n], refs[n], refs[n + 1]
        parts = [jnp.where(me_ref[0] == s, own_ref[...], slots[s][...]) for s in range(n)]
        o_ref[...] = ((parts[0] + parts[1]) + parts[2]) + parts[3]

    def slot(s):
        return pl.BlockSpec((None, ta, B), lambda r, me_ref: (jnp.where(me_ref[0] == s, (s + 1) % n, s), r, 0))

    return pl.pallas_call(
        body, name=name,
        grid_spec=pltpu.PrefetchScalarGridSpec(
            num_scalar_prefetch=1, grid=(A2 // ta,),
            in_specs=[slot(s) for s in range(n)] + [pl.BlockSpec((None, ta, B), lambda r, me_ref: (me_ref[0], r, 0))],
            out_specs=pl.BlockSpec((ta, B), lambda r, me_ref: (r, 0))),
        out_shape=jax.ShapeDtypeStruct((A2, B), F32),
        compiler_params=_params("parallel"),
    )(_scalar(_chip_index()), *([landed] * n), own)


HBM_SPEC = pl.BlockSpec(memory_space=pl.ANY)
COMM_PARAMS = pltpu.CompilerParams(has_side_effects=True)


def _mesh_pos():
    return lax.axis_index("x"), lax.axis_index("y"), lax.axis_index("c")


def _other_chips(x, y):
    return [(1 - x, y), (x, 1 - y), (1 - x, 1 - y)]


def _remote(src, dst, send_sems, recv_sems, k, to):
    return pltpu.make_async_remote_copy(src_ref=src, dst_ref=dst, send_sem=send_sems.at[k], recv_sem=recv_sems.at[k],
                                        device_id=to, device_id_type=MESH)


def _place_own(gathered, shards, *, name):
    n = len(shards)

    def body(me_ref, *refs):
        for s_ref, o_ref in zip(refs[:n], refs[2 * n:]):
            o_ref[...] = s_ref[...]

    return pl.pallas_call(
        body, name=name,
        grid_spec=pltpu.PrefetchScalarGridSpec(
            num_scalar_prefetch=1, grid=(1,),
            in_specs=[pl.BlockSpec(s.shape, lambda i, me_ref: (0, 0)) for s in shards] + [HBM_SPEC] * n,
            out_specs=[pl.BlockSpec((None,) + s.shape, lambda i, me_ref: (me_ref[0], 0, 0)) for s in shards]),
        out_shape=[jax.ShapeDtypeStruct(g.shape, g.dtype) for g in gathered],
        input_output_aliases={1 + n + k: k for k in range(n)},
        compiler_params=_params("arbitrary"),
    )(_scalar(_chip_index()), *shards, *gathered)


def _half_rows(rows, c, align=8):
    assert (rows // 2) % align == 0
    return pl.ds(pl.multiple_of(c * (rows // 2), align), rows // 2)


BF16_ROWS = 16


def _halved(rows):
    return rows % (2 * BF16_ROWS) == 0


def _gather_copies(srcs, lands, send_sems, recv_sems):
    x, y, c = _mesh_pos()
    me = 2 * x + y
    out = []
    for k in range(len(srcs)):
        a = srcs[k].shape[0]
        rows = _half_rows(a, c, BF16_ROWS) if _halved(a) else pl.ds(0, a)
        for j, (px, py) in enumerate(_other_chips(x, y)):
            send = _remote(srcs[k].at[rows], lands[k].at[me, rows], send_sems, recv_sems, 3 * k + j, (px, py, c))
            recv = _remote(srcs[k].at[rows], lands[k].at[2 * px + py, rows], send_sems, recv_sems, 3 * k + j, (px, py, c))
            out.append((send, recv))
    return out


def _gather_start(srcs):
    nu = len(srcs)
    sizes = [len(su) for su in srcs]
    offs = [2 * sum(sizes[:u]) for u in range(nu + 1)]
    lands = [[lax.empty((N_CHIPS,) + s.shape, s.dtype) for s in su] for su in srcs]
    flat = [a for u in range(nu) for a in srcs[u] + lands[u]]

    def body(*refs):
        bufs, sems, token = refs[:len(flat)], refs[len(flat):len(flat) + 2 * nu], refs[-1]
        for u, n in enumerate(sizes):
            mine = bufs[offs[u]:offs[u + 1]]
            for send, _ in _gather_copies(mine[:n], mine[n:], sems[2 * u], sems[2 * u + 1]):
                send.start()
        token[...] = jnp.zeros_like(token)

    res = pl.pallas_call(
        body, name="weight_gather_start",
        in_specs=[HBM_ONLY] * len(flat),
        out_specs=[SEM_SPEC] * (2 * nu) + [HBM_ONLY] * len(flat) + [pl.BlockSpec(memory_space=pltpu.VMEM)],
        out_shape=[pltpu.SemaphoreType.DMA((3 * n,)) for n in sizes for _ in (0, 1)] + [pltpu.HBM(a.shape, a.dtype) for a in flat]
        + [jax.ShapeDtypeStruct((1, 1), F32)],
        input_output_aliases={i: 2 * nu + i for i in range(len(flat))},
        compiler_params=SPLIT_PARAMS,
    )(*[pltpu.with_memory_space_constraint(a, pltpu.HBM) for a in flat])
    bufs = res[2 * nu:2 * nu + len(flat)]
    state = [(res[2 * u], res[2 * u + 1], list(bufs[offs[u]:offs[u] + n]), list(bufs[offs[u] + n:offs[u + 1]]))
             for u, n in enumerate(sizes)]
    return state, res[-1]


def _gather_wait(state, after, *, name):
    send_sems, recv_sems, srcs, lands = state
    n = len(srcs)

    def body(*refs):
        for send, recv in _gather_copies(refs[:n], refs[n:2 * n], refs[2 * n], refs[2 * n + 1]):
            send.wait_send()
            recv.wait_recv()

    res = pl.pallas_call(
        body, name=name,
        in_specs=[HBM_ONLY] * (2 * n) + [SEM_SPEC, SEM_SPEC, HBM_SPEC],
        out_specs=[HBM_ONLY] * (2 * n),
        out_shape=[pltpu.HBM(a.shape, a.dtype) for a in srcs + lands],
        input_output_aliases={i: i for i in range(2 * n)},
        compiler_params=SPLIT_PARAMS,
    )(*srcs, *lands, send_sems, recv_sems, after)
    return list(res[:n]), list(res[n:])


def _gather_forward(lands, *, name):
    n = len(lands)

    def body(*refs):
        bufs, outs = refs[:n], refs[n:2 * n]
        send_sems, recv_sems = refs[2 * n:]
        x, y, c = _mesh_pos()
        copies, waits = [], []
        for k in range(n):
            a = lands[k].shape[1]
            if not _halved(a):
                continue
            for j, (px, py) in enumerate(_other_chips(x, y)):
                mine = 2 * px + py, _half_rows(a, c, BF16_ROWS)
                copies.append(_remote(bufs[k].at[mine], outs[k].at[mine], send_sems, recv_sems, 3 * k + j, (x, y, 1 - c)))
                lands_here = outs[k].at[2 * px + py, _half_rows(a, 1 - c, BF16_ROWS)]
                waits.append(_remote(lands_here, lands_here, send_sems, recv_sems, 3 * k + j, (x, y, 1 - c)))
        for cp in copies:
            cp.start()
        for cp in waits:
            cp.wait_recv()
        for cp in copies:
            cp.wait_send()

    return pl.pallas_call(
        body, name=name,
        in_specs=[HBM_SPEC] * n, out_specs=[HBM_SPEC] * n,
        out_shape=[jax.ShapeDtypeStruct(a.shape, a.dtype) for a in lands],
        scratch_shapes=[pltpu.SemaphoreType.DMA((3 * n,)), pltpu.SemaphoreType.DMA((3 * n,))],
        input_output_aliases={i: i for i in range(n)},
        compiler_params=COMM_PARAMS,
    )(*lands)


def _sibling_exchange(gs, *, name):
    n = len(gs)

    def body(*refs):
        ins, outs = refs[:n], refs[n:2 * n]
        send_sems, recv_sems = refs[2 * n:]
        x, y, c = _mesh_pos()
        copies = [_remote(ins[k].at[:, _half_rows(gs[k].shape[1], 1 - c)], outs[k], send_sems, recv_sems, k, (x, y, 1 - c))
                  for k in range(n)]
        for cp in copies:
            cp.start()
        for cp in copies:
            cp.wait()

    return pl.pallas_call(
        body, name=name,
        in_specs=[HBM_SPEC] * n, out_specs=[HBM_SPEC] * n,
        out_shape=[jax.ShapeDtypeStruct((g.shape[0], g.shape[1] // 2, g.shape[2]), g.dtype) for g in gs],
        scratch_shapes=[pltpu.SemaphoreType.DMA((n,)), pltpu.SemaphoreType.DMA((n,))],
        compiler_params=COMM_PARAMS,
    )(*gs)


HBM_ONLY = pl.BlockSpec(memory_space=pltpu.HBM)
SEM_SPEC = pl.BlockSpec(memory_space=pltpu.SEMAPHORE)
SPLIT_PARAMS = pltpu.CompilerParams(has_side_effects=pltpu.SideEffectType.DATAFLOW_SIDE_EFFECTING)


def _scatter_copies(srcs, lands, send_sems, recv_sems):
    x, y, c = _mesh_pos()
    me = 2 * x + y
    out = []
    for k in range(len(srcs)):
        for j, (px, py) in enumerate(_other_chips(x, y)):
            s = 2 * px + py
            send = _remote(srcs[k].at[s], lands[k].at[me], send_sems, recv_sems, 3 * k + j, (px, py, c))
            recv = _remote(srcs[k].at[s], lands[k].at[s], send_sems, recv_sems, 3 * k + j, (px, py, c))
            out.append((send, recv))
    return out


def _exchange_copies(srcs, lands, send_sems, recv_sems):
    x, y, c = _mesh_pos()
    out = []
    for k in range(len(srcs)):
        cp = _remote(srcs[k].at[:, _half_rows(srcs[k].shape[1], 1 - c)], lands[k], send_sems, recv_sems, k, (x, y, 1 - c))
        out.append((cp, cp))
    return out


def _split_start(srcs, land_shapes, copies, n_sems, *, name):
    n = len(srcs)
    lands = [lax.empty(shape, s.dtype) for shape, s in zip(land_shapes, srcs)]

    def body(*refs):
        ins, zones = refs[:n], refs[n:2 * n]
        send_sems, recv_sems, token = refs[2 * n], refs[2 * n + 1], refs[-1]
        for send, _ in copies(ins, zones, send_sems, recv_sems):
            send.start()
        token[...] = jnp.zeros_like(token)

    hbm = lambda a: pltpu.HBM(a.shape, a.dtype)
    res = pl.pallas_call(
        body, name=name,
        in_specs=[HBM_ONLY] * (2 * n),
        out_specs=[SEM_SPEC, SEM_SPEC] + [HBM_ONLY] * (2 * n) + [pl.BlockSpec(memory_space=pltpu.VMEM)],
        out_shape=[pltpu.SemaphoreType.DMA((n_sems,)), pltpu.SemaphoreType.DMA((n_sems,))] + [hbm(a) for a in srcs + lands]
        + [jax.ShapeDtypeStruct((1, 1), F32)],
        input_output_aliases={i: 2 + i for i in range(2 * n)},
        compiler_params=SPLIT_PARAMS,
    )(*[pltpu.with_memory_space_constraint(a, pltpu.HBM) for a in srcs + lands])
    return (res[0], res[1], list(res[2:2 + n]), list(res[2 + n:2 + 2 * n])), res[-1]


def _scatter_start(ps, *, name):
    return _split_start(ps, [p.shape for p in ps], _scatter_copies, 3 * len(ps), name=name)


def _exchange_start(gs, *, name):
    return _split_start(gs, [(g.shape[0], g.shape[1] // 2, g.shape[2]) for g in gs], _exchange_copies, len(gs), name=name)


def _split_wait(started, copies, after, *, name):
    ng = len(started)
    sizes = [len(st[2]) for st in started]
    offs = [2 * sum(sizes[:i]) for i in range(ng + 1)]
    flat = [a for (_, _, ps, lands) in started for a in ps + lands]

    def body(*refs):
        bufs, sems = refs[:len(flat)], refs[len(flat):len(flat) + 2 * ng]
        for i, n in enumerate(sizes):
            srcs, zones = bufs[offs[i]:offs[i] + n], bufs[offs[i] + n:offs[i + 1]]
            for send, recv in copies(srcs, zones, sems[2 * i], sems[2 * i + 1]):
                send.wait_send()
                recv.wait_recv()

    res = pl.pallas_call(
        body, name=name,
        in_specs=[HBM_ONLY] * len(flat) + [SEM_SPEC] * (2 * ng) + [HBM_SPEC],
        out_specs=[HBM_ONLY] * len(flat),
        out_shape=[pltpu.HBM(a.shape, a.dtype) for a in flat],
        input_output_aliases={i: i for i in range(len(flat))},
        compiler_params=SPLIT_PARAMS,
    )(*flat, *[s for (ss, rs, _, _) in started for s in (ss, rs)], after)
    return [(list(res[offs[i]:offs[i] + n]), list(res[offs[i] + n:offs[i + 1]])) for i, n in enumerate(sizes)]


def _sibling_share(hs):
    n = len(hs)

    def body(*refs):
        ins, outs = refs[:n], refs[n:2 * n]
        send_sems, recv_sems = refs[2 * n:]
        x, y, c = _mesh_pos()
        copies = [_remote(ins[k], outs[k], send_sems, recv_sems, k, (x, y, 1 - c)) for k in range(n)]
        for cp in copies:
            cp.start()
        for cp in copies:
            cp.wait()

    return pl.pallas_call(
        body, name="grad_sibling_share",
        in_specs=[HBM_SPEC] * n, out_specs=[HBM_SPEC] * n,
        out_shape=[jax.ShapeDtypeStruct(h.shape, h.dtype) for h in hs],
        scratch_shapes=[pltpu.SemaphoreType.DMA((n,)), pltpu.SemaphoreType.DMA((n,))],
        compiler_params=COMM_PARAMS,
    )(*hs)


def _allreduce_small(part, by_chip):
    rows, C = part.shape
    rows2 = by_chip.shape[1]

    def body(p_ref, q_ref, o_ref, o2_ref, slots, slots2, send_sems, recv_sems):
        x, y, c = _mesh_pos()
        me = 4 * x + 2 * y + c
        slots[me] = p_ref[...]
        slots2[me] = q_ref[2 * x + y]
        copies = []
        for k in range(1, 8):
            kx, ky, kc = (k >> 2) & 1, (k >> 1) & 1, k & 1
            peer = (x ^ kx if kx else x, y ^ ky if ky else y, c ^ kc if kc else c)
            src = 4 * peer[0] + 2 * peer[1] + peer[2]
            pair = []
            for j, (mine, zone, lands) in enumerate(((p_ref, slots.at[me], slots.at[src]),
                                                     (q_ref.at[2 * peer[0] + peer[1]], slots2.at[me], slots2.at[src]))):
                cp = _remote(mine, zone, send_sems, recv_sems, 2 * (k - 1) + j, peer)
                cp.start()
                pair.append((cp, _remote(mine, lands, send_sems, recv_sems, 2 * (k - 1) + j, peer)))
            copies += pair
        for _, landing in copies:
            landing.wait_recv()
        for cp, _ in copies:
            cp.wait_send()
        total, total2 = slots[0], slots2[0]
        for d in range(1, 8):
            total, total2 = total + slots[d], total2 + slots2[d]
        o_ref[...] = total
        o2_ref[...] = total2

    vmem = pl.BlockSpec(memory_space=pltpu.VMEM)
    return pl.pallas_call(
        body, name="small_grad_allreduce",
        in_specs=[vmem, vmem], out_specs=[vmem, vmem],
        out_shape=[jax.ShapeDtypeStruct((rows, C), F32), jax.ShapeDtypeStruct((rows2, C), F32)],
        scratch_shapes=[pltpu.VMEM((8, rows, C), F32), pltpu.VMEM((8, rows2, C), F32),
                        pltpu.SemaphoreType.DMA((14,)), pltpu.SemaphoreType.DMA((14,))],
        compiler_params=pltpu.CompilerParams(has_side_effects=True, vmem_limit_bytes=VMEM_LIMIT_BYTES),
    )(part, by_chip)


def _pad_w_uq(w):
    lead = w.shape[:-1]
    w = w.reshape(lead + (MLA_HEADS, MLA_QK))
    w = jnp.concatenate([w, jnp.zeros(lead + (MLA_HEADS, MLA_PAD - MLA_QK), w.dtype)], axis=-1)
    return w.reshape(lead + (MLA_HEADS * MLA_PAD,))


def _unpad_w_uq(g):
    lead = g.shape[:-1]
    return g.reshape(lead + (MLA_HEADS, MLA_PAD))[..., :MLA_QK].reshape(lead + (MLA_HEADS * MLA_QK,))


def _t(a):
    return jnp.swapaxes(a, -1, -2)


def _shards_of_cols(w):
    A, NB = w.shape
    return w.reshape(A, N_CHIPS, NB // N_CHIPS).transpose(1, 0, 2)


BIG = ("w_in", "w_uq", "w_ukv", "w_out", "w_up", "w_down")
SMALL = ("attn_pre_norm", "forget_bias", "swa_sinks", "rel_bias", "q_latent_norm", "kv_latent_norm", "group_norm",
         "attn_post_norm", "ffn_pre_norm", "conv_b", "ffn_post_norm")
WEIGHTS = ("attn_pre_norm", "w_in", "forget_bias", "swa_sinks", "rel_bias", "q_latent_norm", "w_uq", "kv_latent_norm",
           "w_ukv", "group_norm", "w_out", "attn_post_norm", "ffn_pre_norm", "w_up", "conv_w", "conv_b", "w_down",
           "ffn_post_norm")


PACK_UNIT = 8 * LANES


def _pack_rows(shape):
    return -(-int(np.prod(shape)) // PACK_UNIT) * 8


def _pack(arrs, row_mult=8):
    parts = []
    for a in arrs:
        n = int(np.prod(a.shape))
        parts.append(jnp.pad(a.reshape(-1), (0, _pack_rows(a.shape) * LANES - n)).reshape(-1, LANES))
    rows = sum(p.shape[0] for p in parts)
    pad = -rows % row_mult
    if pad:
        parts.append(jnp.zeros((pad, LANES), parts[0].dtype))
    return jnp.concatenate(parts, axis=0)


def _unpack(packed, shapes):
    packed = packed.reshape(-1, LANES)
    out, off = [], 0
    for shp in shapes:
        r = _pack_rows(shp)
        out.append(packed[off:off + r].reshape(-1)[:int(np.prod(shp))].reshape(shp))
        off += r
    return out


LAYER_KEYS = ("w_qkv_t", "w_lat_t", "w_in_t", "w_uq_p", "w_uq_t", "w_ukv", "w_ukv_t", "w_out", "w_up", "w_down", "conv_w")


MIX_WEIGHTS = ("w_in", "w_uq", "w_ukv", "w_out")
FFN_WEIGHTS = ("w_up", "w_down", "conv_w")


def _layer_weights(gathered):
    cols = lambda g: g.transpose(1, 0, 2).reshape(g.shape[1], N_CHIPS * g.shape[2])
    out = {}
    if "w_in" in gathered:
        w_in_t = _t(gathered["w_in"]).reshape(IN_COLS, D_MODEL)
        w_in_t = jnp.pad(w_in_t, ((0, IN_ROWS - IN_COLS), (0, 0)))
        w_uq_p = _pad_w_uq(cols(gathered["w_uq"]))
        w_ukv = cols(gathered["w_ukv"])
        out.update(w_qkv_t=w_in_t[:QKV_ROWS], w_lat_t=w_in_t[QKV_ROWS:], w_in_t=w_in_t, w_uq_p=w_uq_p, w_uq_t=_t(w_uq_p),
                   w_ukv=w_ukv, w_ukv_t=_t(w_ukv), w_out=gathered["w_out"].reshape(D_MODEL, D_MODEL))
    if "w_up" in gathered:
        out.update(w_up=gathered["w_up"], w_down=gathered["w_down"].reshape(D_FF, D_MODEL), conv_w=cols(gathered["conv_w"]))
    return out


def _local_step(x, target, W, layer_weights, layer_done):
    W = dict(W, **{key: [None] * DEPTH for key in LAYER_KEYS})
    S = x.shape[0]
    tq_tabs, tm_tabs = _rope_tables(S)
    onehot_t = _rel_onehot_t()
    bias_t = _bias_table(W["rel_bias"].T, onehot_t).reshape(SWA_KV_HEADS, SWA_GROUP, 2 * WINDOW, WINDOW)
    bias_t = bias_t.transpose(0, 2, 1, 3).reshape(SWA_KV_HEADS, 2 * WINDOW, GW)
    row = lambda a: a.reshape(1, -1)
    col = lambda a: a.reshape(-1, 1)
    fox_rows = (FOX_ROW0, FOX_ROW0 + FOX_HEADS * HEAD_DIM, FOX_ROW0 + 2 * FOX_HEADS * HEAD_DIM, SWA_Q_HEADS)
    fox = dict(rows=fox_rows, H=FOX_HEADS, Dk=HEAD_DIM, Dv=HEAD_DIM, scale=HEAD_DIM ** -0.5)
    mla = dict(rows=(0, 0, 0, SWA_Q_HEADS + FOX_HEADS), H=MLA_HEADS, Dk=MLA_PAD, Dv=HEAD_DIM, scale=MLA_SCALE, q_scaled=True)

    saved = []
    h = _rms_fwd(x, row(W["attn_pre_norm"][0]), name="rms_in")
    for l in range(DEPTH):
        sv = {"x0": x, "h1": h}
        for key, val in layer_weights(l, h, False).items():
            W[key][l] = val
        qkv = _matmul(W["w_qkv_t"][l], h, tb=True, out_dtype=BF16, name="proj_qkv")
        lat = _matmul(W["w_lat_t"][l], h, tb=True, name="proj_lat")
        oa, lse_a = _swa_fwd(qkv, bias_t, W["swa_sinks"][l], name="swa_fwd")
        fb_col = jnp.pad(col(W["forget_bias"][l]), ((0, GATE_ROWS - FOX_HEADS), (0, 0)))
        f4 = _gate_fwd(lat, fb_col, name="fox_gate_fwd")[:FOX_HEADS]
        f2 = f4 * LOG2E
        f_row, f_col = f2[:, None, :], f2.T
        of, lse_f = _attn_fwd(qkv, qkv, qkv, f_row=f_row, f_col=f_col, name="fox_fwd", **fox)
        nq, nkv, qm, km, vm = _mla_prep_fwd(lat, col(W["q_latent_norm"][l]), col(W["kv_latent_norm"][l]), W["w_uq_t"][l],
                                            W["w_ukv_t"][l], tq_tabs, tm_tabs, name="mla_prep_fwd")
        oc, lse_c = _attn_fwd(qm, km, vm, name="mla_fwd", **mla)
        mixed = _group_norm_fwd(oa, of, oc, col(W["group_norm"][l]), name="group_norm_fwd")
        y = _matmul(mixed, W["w_out"][l], ta=True, name="proj_out")
        x1, h2 = _resid_rms(x, y, row(W["attn_post_norm"][l]), row(W["ffn_pre_norm"][l]), name="attn_resid")
        for key, val in layer_weights(l, h2, True).items():
            W[key][l] = val
        a = _matmul(h2, W["w_up"][l], b_shards=True, out_dtype=BF16, name="ffn_up")
        u, z = _conv_geglu_fwd(a, W["conv_w"][l], row(W["conv_b"][l]), name="conv_geglu_fwd")
        y2 = _matmul(z, W["w_down"][l], name="ffn_down")
        g_next = row(W["attn_pre_norm"][l + 1]) if l + 1 < DEPTH else None
        x2, h_next = _resid_rms(x1, y2, row(W["ffn_post_norm"][l]), g_next, name="ffn_resid")
        sv.update(qkv=qkv, lat=lat, oa=oa, lse_a=lse_a, fb_col=fb_col, f_row=f_row, f_col=f_col, of=of, lse_f=lse_f,
                  nq=nq, nkv=nkv, qm=qm, km=km, vm=vm, oc=oc, lse_c=lse_c, mixed=mixed, y=y, x1=x1, h2=h2, a=a, u=u, z=z, y2=y2)
        saved.append(sv)
        x, h = x2, h_next

    loss, dx = _loss_head(x, target)

    G = {k: [None] * DEPTH for k in WEIGHTS if k != "rel_bias" and k not in BIG}
    dbias_layers = [None] * DEPTH
    for l in reversed(range(DEPTH)):
        sv = saved[l]
        gb = {}
        if l == DEPTH - 1:
            dy2, dg = _rms_bwd(sv["y2"], row(W["ffn_post_norm"][l]), dx, out_dtype=BF16, name="ffn_post_bwd")
            G["ffn_post_norm"][l] = dg[0]
        dz = _matmul(dy2, W["w_down"][l], tb=True, name="ffn_down_dx")
        gb["w_down"] = _matmul(sv["z"], dy2, ta=True, name="ffn_down_dw").reshape(N_CHIPS, D_FF // N_CHIPS, D_MODEL)
        da, dcw, dcb = _conv_geglu_bwd(sv["a"], sv["u"], W["conv_w"][l], dz, name="conv_geglu_bwd")
        G["conv_w"][l] = dcw.transpose(1, 0, 2).reshape(3, 2 * D_FF)
        G["conv_b"][l] = dcb.reshape(2 * D_FF)
        gb["w_up"] = _matmul(sv["h2"], da, ta=True, out_shards=True, b_halves=True, name="ffn_up_dw")
        token = layer_done(l, gb)
        gb = {}
        dx1, dg, dy, dg_post = _matmul(
            da, W["w_up"][l], tb=True, b_shards=True, a_halves=True, name="ffn_up_dx",
            norm_bwd=(sv["x1"], row(W["ffn_pre_norm"][l]) + token, dx, (sv["y"], row(W["attn_post_norm"][l]))))
        G["ffn_pre_norm"][l] = dg[0]
        G["attn_post_norm"][l] = dg_post[0]
        dmixed = _matmul(W["w_out"][l], dy, tb=True, name="proj_out_dx")
        gb["w_out"] = _matmul(sv["mixed"], dy, name="proj_out_dw").reshape(N_CHIPS, D_MODEL // N_CHIPS, D_MODEL)
        doa, dof, doc, dg, delta = _group_norm_bwd(sv["oa"], sv["of"], sv["oc"], col(W["group_norm"][l]), dmixed,
                                                   name="group_norm_bwd")
        G["group_norm"][l] = dg[:, 0]
        dqa, dkva, dbias_l, dsink = _swa_bwd(sv["qkv"], bias_t, W["swa_sinks"][l], doa, sv["lse_a"],
                                             delta.reshape(-1, S), name="swa_bwd")
        dbias_layers[l] = (dbias_l.reshape(SWA_KV_HEADS, 2 * WINDOW, SWA_GROUP, WINDOW).transpose(0, 2, 1, 3)
                           .reshape(SWA_Q_HEADS, -1))
        G["swa_sinks"][l] = dsink[:, 0]
        dqf, dkf, dvf, dfk = _attn_bwd(sv["qkv"], sv["qkv"], sv["qkv"], do=dof, lse=sv["lse_f"], delta=delta,
                                       f_row=sv["f_row"], f_col=sv["f_col"], name="fox_bwd", **fox)
        dF = jnp.pad(dfk.T, ((0, GATE_ROWS - FOX_HEADS), (0, 0)))
        dflog, dfb = _gate_bwd(sv["lat"], sv["fb_col"], dF, name="fox_gate_bwd")
        G["forget_bias"][l] = dfb[:FOX_HEADS, 0]
        dqm, dkm, dvm = _attn_bwd(sv["qm"], sv["km"], sv["vm"], do=doc, lse=sv["lse_c"], delta=delta, name="mla_bwd", **mla)
        dlat, dwq_t, dwkv_t, dgq, dgkv = _mla_prep_bwd(
            sv["lat"], sv["nq"], sv["nkv"], col(W["q_latent_norm"][l]), col(W["kv_latent_norm"][l]), W["w_uq_p"][l],
            W["w_ukv"][l], tq_tabs, tm_tabs, dqm, dkm, dvm, dflog, name="mla_prep_bwd")
        gb["w_uq"], gb["w_ukv"] = _shards_of_cols(_unpad_w_uq(dwq_t.T)), _shards_of_cols(dwkv_t.T)
        G["q_latent_norm"][l], G["kv_latent_norm"][l] = dgq[:, 0], dgkv[:, 0]
        dproj = _dproj_cast(dqa, dkva, dqf, dkf, dvf, dlat, name="dproj_cast")
        dw_in_t = _matmul(dproj, sv["h1"], name="proj_in_dw")
        gb["w_in"] = _t(dw_in_t[:IN_COLS].reshape(N_CHIPS, IN_COLS // N_CHIPS, D_MODEL))
        token = layer_done(l, gb)
        below = (saved[l - 1]["y2"], row(W["ffn_post_norm"][l - 1])) if l > 0 else None
        res = _matmul(dproj, W["w_in_t"][l], ta=True, name="proj_in_dx",
                      norm_bwd=(sv["x0"], row(W["attn_pre_norm"][l]) + token, dx1, below))
        dx, G["attn_pre_norm"][l] = res[0], res[1][0]
        if l > 0:
            dy2, G["ffn_post_norm"][l - 1] = res[2], res[3][0]

    grads = {k: jnp.stack(v) for k, v in G.items()}
    grads["rel_bias"] = _bias_table_bwd(jnp.stack(dbias_layers), onehot_t).T
    return loss, dx, grads


def kernel(x, attn_pre_norm, w_in, forget_bias, swa_sinks, rel_bias, q_latent_norm, w_uq, kv_latent_norm, w_ukv, group_norm, w_out, attn_post_norm, ffn_pre_norm, w_up, conv_w, conv_b, w_down, ffn_post_norm, loss_target, m_attn_pre_norm, m_w_in, m_forget_bias, m_swa_sinks, m_rel_bias, m_q_latent_norm, m_w_uq, m_kv_latent_norm, m_w_ukv, m_group_norm, m_w_out, m_attn_post_norm, m_ffn_pre_norm, m_w_up, m_conv_w, m_conv_b, m_w_down, m_ffn_post_norm, v_attn_pre_norm, v_w_in, v_forget_bias, v_swa_sinks, v_rel_bias, v_q_latent_norm, v_w_uq, v_kv_latent_norm, v_w_ukv, v_group_norm, v_w_out, v_attn_post_norm, v_ffn_pre_norm, v_w_up, v_conv_w, v_conv_b, v_w_down, v_ffn_post_norm):
    args = dict(locals())
    w = {k: args[k] for k in WEIGHTS}
    m = {k: args["m_" + k] for k in WEIGHTS}
    v = {k: args["v_" + k] for k in WEIGHTS}

    block = lambda l, keys: [w[k][l] if k == "conv_w" else w[k][l].astype(BF16) for k in keys]
    units = [(0, MIX_WEIGHTS), (0, FFN_WEIGHTS)] + [(l, MIX_WEIGHTS + FFN_WEIGHTS) for l in range(1, DEPTH)]
    gather_state, token = _gather_start([block(l, keys) for l, keys in units])
    W = {k: w[k] for k in SMALL}
    W["attn_pre_norm"] = W["attn_pre_norm"] + token

    def layer_weights(l, after, for_ffn):
        if for_ffn and l > 0:
            return {}
        keys = FFN_WEIGHTS if for_ffn else (MIX_WEIGHTS if l == 0 else MIX_WEIGHTS + FFN_WEIGHTS)
        tag = f"{l}_{keys[0]}"
        srcs, lands = _gather_wait(gather_state[units.index((l, keys))], after, name="weight_gather_wait_" + tag)
        lands = _gather_forward(lands, name="weight_gather_forward_" + tag)
        lands = _place_own(lands, srcs, name="place_own_shards")
        return _layer_weights(dict(zip(keys, lands)))

    started, groups, pending = [], [], []

    def to_chips(l, keys, gs, recv, tag):
        pair = [_pair_sum(gk, rk, name="grad_pair_sum") for gk, rk in zip(gs, recv)]
        state, token = _scatter_start(pair, name="grad_scatter_start_" + tag)
        started.append(state)
        groups.append((l, keys))
        return token

    def finish_pending(after):
        l, keys, tag, state = pending.pop()
        gs, recv = _split_wait([state], _exchange_copies, after, name="grad_exchange_wait_" + tag)[0]
        return to_chips(l, keys, gs, recv, tag)

    def layer_done(l, gb):
        keys = [k for k in BIG if k in gb]
        gs = [gb[k] for k in keys]
        tag = f"{l}_{keys[0]}"
        token = finish_pending(gs[0]) if pending else 0.0
        if l == 0:
            return token + to_chips(l, keys, gs, _sibling_exchange(gs, name="grad_sibling_exchange_" + tag), tag)
        state, started_token = _exchange_start(gs, name="grad_exchange_start_" + tag)
        pending.append((l, keys, tag, state))
        return token + started_token

    loss_part, dx, g = _local_step(x[0], loss_target[0], W, layer_weights, layer_done)
    loss = lax.psum(loss_part, ("x", "y", "c"))

    reduced = {}
    for (l, keys), (pair, zones) in zip(groups, _split_wait(started, _scatter_copies, dx, name="grad_scatter_wait")):
        for k, p, z in zip(keys, pair, zones):
            reduced[k, l] = _chip_sum(z, p, name="grad_chip_sum")
    mine = [jnp.stack([reduced[k, l] for l in range(DEPTH)]) for k in BIG]
    other = _sibling_share(mine)
    out_g, out_d, out_m, out_v = {}, {}, {}, {}
    for k, g_mine, g_other in zip(BIG, mine, other):
        out_g[k], out_d[k], out_m[k], out_v[k] = _adamw_halves(w[k], g_mine, g_other, m[k], v[k], name="adamw_" + k)

    small_shapes = [w[k].shape for k in SMALL]
    taps_by_chip = g["conv_w"].reshape(DEPTH, 3, N_CHIPS, FF_SHARD).transpose(2, 0, 1, 3)
    reduced, taps = _allreduce_small(_pack([g[k] for k in SMALL]), jnp.stack([_pack([t]) for t in taps_by_chip]))
    g_small = _unpack(reduced, small_shapes) + _unpack(taps, [w["conv_w"].shape])
    names = SMALL + ("conv_w",)
    shapes = small_shapes + [w["conv_w"].shape]
    packed = lambda arrs: _pack(arrs, ROW_TILE)[None]
    d_s, m_s, v_s = _adamw(packed([w[k] for k in names]), packed(g_small), packed([m[k] for k in names]),
                           packed([v[k] for k in names]), name="adamw_small")
    out_g.update(zip(names, g_small))
    out_d.update(zip(names, _unpack(d_s, shapes)))
    out_m.update(zip(names, _unpack(m_s, shapes)))
    out_v.update(zip(names, _unpack(v_s, shapes)))

    return (loss, dx[None], *[out_g[k] for k in WEIGHTS], *[out_d[k] for k in WEIGHTS],
            *[out_m[k] for k in WEIGHTS], *[out_v[k] for k in WEIGHTS])
```

```python
import math

import numpy as np
import jax
import jax.numpy as jnp
from jax import lax
from jax.experimental import pallas as pl
from jax.experimental.pallas import tpu as pltpu

F32 = jnp.float32
BF16 = jnp.bfloat16

D_MODEL = 1024
DEPTH = 4
HEAD_DIM = 64
SWA_Q_HEADS = 8
SWA_KV_HEADS = 2
SWA_GROUP = SWA_Q_HEADS // SWA_KV_HEADS
WINDOW = 128
FOX_HEADS = 4
MLA_HEADS = 4
MLA_Q_RANK = 256
MLA_KV_RANK = 128
MLA_NOPE = 64
MLA_ROPE = 32
MLA_QK = MLA_NOPE + MLA_ROPE
ROPE_THETA = 10000.0
REL_BUCKETS = 32
REL_MAX_DIST = 128
D_FF = 2816
EPS = 1e-6
NEG_INF = -1e30
LANES = 128
N_CHIPS = 4

IN_COLS = 1956
IN_ROWS = 2048
QKV_ROWS = 1536
LAT_ROWS = IN_ROWS - QKV_ROWS
LAT_SHIFT = FOX_HEADS
FOX_ROW0 = 768
MLA_PAD = LANES
GATE_ROWS = 8

ADAM_LR = 0.001
ADAM_B1 = 0.9
ADAM_B2 = 0.999
ADAM_EPS = 1e-08
ADAM_WD = 0.01
ADAM_STEP = 10

VMEM_LIMIT_BYTES = 48 * 1024 * 1024
ATT_TILE = 512
LOG2E = math.log2(math.e)
MLA_SCALE = MLA_QK ** -0.5
ROW_TILE = 256
MESH = pl.DeviceIdType.MESH

NT = (((1,), (1,)), ((), ()))
TN = (((0,), (0,)), ((), ()))
NN = (((1,), (0,)), ((), ()))


def _params(*sem):
    return pltpu.CompilerParams(dimension_semantics=sem, vmem_limit_bytes=VMEM_LIMIT_BYTES)


def _tile(dim, cap):
    for t in (2816, 2048, 1408, 1024, 512, 256, 128, 64, 32, 16, 8):
        if t <= cap and dim % t == 0:
            return t
    return dim


def _dot(a, b, dims=NN):
    return lax.dot_general(a, b, dims, preferred_element_type=F32)


def _split3(a):
    a1 = a.astype(BF16)
    r1 = a - a1.astype(F32)
    a2 = r1.astype(BF16)
    a3 = (r1 - a2.astype(F32)).astype(BF16)
    return a1, a2, a3


FF_SHARD = 2 * D_FF // N_CHIPS
MATMUL_VMEM_BYTES = 40 * 1024 * 1024
NORM_BWD_ROWS = 512
NORM_BWD_VMEM_LIMIT_BYTES = 56 * 1024 * 1024


def _matmul(a, b, *, ta=False, tb=False, out_dtype=F32, name, b_shards=False, out_shards=False, a_halves=False,
            b_halves=False, norm_bwd=None):
    if a_halves:
        M, K = a.shape[1], 2 * a.shape[2]
    elif ta:
        K, M = a.shape
    else:
        M, K = a.shape
    if b_halves:
        K2, N = b.shape[1], 2 * b.shape[2]
    elif b_shards:
        K2, N = (2 * D_FF, D_MODEL) if tb else (D_MODEL, 2 * D_FF)
    elif tb:
        N, K2 = b.shape
    else:
        K2, N = b.shape
    assert K == K2, (a.shape, b.shape)
    tn = _tile(N, 1408)
    tk = FF_SHARD if (b_shards and tb) else _tile(K, 2816)
    out_bytes = jnp.dtype(out_dtype).itemsize
    tile_bytes = 4 + (2 * out_bytes if norm_bwd is None else 2 * (4 * 4 + 2))
    vmem = lambda tm, tk: 2 * 2 * tk * (tm + tn) + tile_bytes * tm * tn
    tm = M if M <= 2048 else _tile(M, 1408)
    if M > 2048 and M % 2048 == 0 and tk == K and vmem(2048, tk) <= MATMUL_VMEM_BYTES:
        tm = 2048
    if norm_bwd is not None:
        assert tn == N and not out_shards
        tm = NORM_BWD_ROWS
    while vmem(tm, tk) > MATMUL_VMEM_BYTES and tk % 256 == 0:
        tk //= 2
    nk = K // tk
    dims = (((0 if ta else 1,), (1 if tb else 0,)), ((), ()))
    if norm_bwd is not None:
        x, g, resid, then = norm_bwd
        chained = then is not None

        def fused(a_ref, b_ref, x_ref, g_ref, r_ref, *refs):
            x2_ref, g2_ref = refs[:2] if chained else (None, None)
            outs, acc = refs[2 * chained:-1], refs[-1]
            k, i = pl.program_id(0), pl.program_id(1)
            acc_ref = acc.at[pl.ds(pl.multiple_of(i * tm, tm), tm), :] if nk > 1 else acc

            @pl.when(k == 0)
            def _():
                acc_ref[...] = jnp.zeros((tm, N), F32)

            acc_ref[...] += lax.dot_general(a_ref[...], b_ref[...], dims, preferred_element_type=F32)

            @pl.when((k == nk - 1) & (i == 0))
            def _():
                for o in outs[1::2]:
                    o[...] = jnp.zeros_like(o)

            @pl.when(k == nk - 1)
            def _():
                dx, dg = _seg_rms_bwd(x_ref[...], g_ref[...], acc_ref[...])
                dx = dx + r_ref[...]
                outs[0][...] = dx
                outs[1][...] += dg
                if chained:
                    dx2, dg2 = _seg_rms_bwd(x2_ref[...], g2_ref[...], dx)
                    outs[2][...] = dx2.astype(BF16)
                    outs[3][...] += dg2

    def body(a_ref, b_ref, o_ref, acc_ref):
        k = pl.program_id(2)

        @pl.when(k == 0)
        def _():
            acc_ref[...] = jnp.zeros_like(acc_ref)

        acc_ref[...] += lax.dot_general(a_ref[...], b_ref[...], dims, preferred_element_type=F32)

        @pl.when(k == nk - 1)
        def _():
            o_ref[...] = acc_ref[...].astype(o_ref.dtype)

    if a_halves:
        nh = K // 2 // tk
        a_spec = pl.BlockSpec((None, tm, tk), lambda i, j, k: (k // nh, i, k % nh))
    else:
        a_spec = pl.BlockSpec((tk, tm), lambda i, j, k: (k, i)) if ta else pl.BlockSpec((tm, tk), lambda i, j, k: (i, k))
    if b_halves:
        nh = N // 2 // tn
        b_spec = pl.BlockSpec((None, tk, tn), lambda i, j, k: (j // nh, k, j % nh))
    elif b_shards and tb:
        assert tk == FF_SHARD
        b_spec = pl.BlockSpec((None, tn, tk), lambda i, j, k: (k, j, 0))
    elif b_shards:
        assert tn == FF_SHARD
        b_spec = pl.BlockSpec((None, tk, tn), lambda i, j, k: (j, k, 0))
    else:
        b_spec = pl.BlockSpec((tn, tk), lambda i, j, k: (j, k)) if tb else pl.BlockSpec((tk, tn), lambda i, j, k: (k, j))
    if out_shards:
        assert tn == FF_SHARD
        out_spec = pl.BlockSpec((None, tm, tn), lambda i, j, k: (j, i, 0))
        out_shape = jax.ShapeDtypeStruct((N // tn, M, tn), out_dtype)
    else:
        out_spec = pl.BlockSpec((tm, tn), lambda i, j, k: (i, j))
        out_shape = jax.ShapeDtypeStruct((M, N), out_dtype)
    if norm_bwd is not None:
        row = pl.BlockSpec((tm, N), lambda k, i: (jnp.where(k == nk - 1, i, 0), 0))
        vec = pl.BlockSpec((1, N), lambda k, i: (0, 0))
        a_map, b_map = a_spec.index_map, b_spec.index_map
        return pl.pallas_call(
            fused, name=name, grid=(nk, M // tm),
            in_specs=[pl.BlockSpec(a_spec.block_shape, lambda k, i: a_map(i, 0, k)),
                      pl.BlockSpec(b_spec.block_shape, lambda k, i: b_map(i, 0, k)), row, vec, row]
            + ([row, vec] if chained else []),
            out_specs=[row, vec] + ([row, vec] if chained else []),
            out_shape=[jax.ShapeDtypeStruct((M, N), F32), jax.ShapeDtypeStruct((1, N), F32)]
            + ([jax.ShapeDtypeStruct((M, N), BF16), jax.ShapeDtypeStruct((1, N), F32)] if chained else []),
            scratch_shapes=[pltpu.VMEM((M if nk > 1 else tm, N), F32)],
            compiler_params=pltpu.CompilerParams(dimension_semantics=("arbitrary", "arbitrary"),
                                                 vmem_limit_bytes=NORM_BWD_VMEM_LIMIT_BYTES),
        )(a, b, x, g, resid, *(then if chained else ()))
    return pl.pallas_call(
        body, name=name, grid=(M // tm, N // tn, nk),
        in_specs=[a_spec, b_spec], out_specs=out_spec, out_shape=out_shape,
        scratch_shapes=[pltpu.VMEM((tm, tn), F32)],
        compiler_params=_params("parallel", "parallel", "arbitrary"),
    )(a, b)


def _seg_rms(xs, g):
    r = lax.rsqrt(jnp.mean(xs * xs, axis=-1, keepdims=True) + EPS)
    return xs * r * g


def _seg_rms_bwd(xs, g, dy):
    r = lax.rsqrt(jnp.mean(xs * xs, axis=-1, keepdims=True) + EPS)
    gd = dy * g
    c = jnp.mean(gd * xs, axis=-1, keepdims=True)
    dx = r * gd - xs * (r * r * r * c)
    dg = jnp.sum(dy * (xs * r), axis=0, keepdims=True)
    return dx, dg


def _rms_fwd(x, g, *, name):
    S, W = x.shape
    tm = _tile(S, 512)

    def body(x_ref, g_ref, o_ref):
        o_ref[...] = _seg_rms(x_ref[...], g_ref[...]).astype(o_ref.dtype)

    return pl.pallas_call(
        body, name=name, grid=(S // tm,),
        in_specs=[pl.BlockSpec((tm, W), lambda i: (i, 0)), pl.BlockSpec((1, W), lambda i: (0, 0))],
        out_specs=pl.BlockSpec((tm, W), lambda i: (i, 0)),
        out_shape=jax.ShapeDtypeStruct((S, W), BF16),
        compiler_params=_params("parallel"),
    )(x, g)


def _rms_bwd(x, g, dy, *, resid=None, out_dtype, name, then=None):
    S, W = x.shape
    tm = _tile(S, 512)
    has_resid = resid is not None
    chained = then is not None

    def body(*refs):
        refs = list(refs)
        x_ref, g_ref, dy_ref = refs[:3]
        r_ref = refs[3] if has_resid else None
        n_in = 3 + has_resid + 2 * chained
        x2_ref, g2_ref = (refs[n_in - 2], refs[n_in - 1]) if chained else (None, None)
        outs = refs[n_in:]
        dx_ref, dg_ref = outs[0], outs[1]

        @pl.when(pl.program_id(0) == 0)
        def _():
            dg_ref[...] = jnp.zeros_like(dg_ref)
            if chained:
                outs[3][...] = jnp.zeros_like(outs[3])

        dx, dg = _seg_rms_bwd(x_ref[...], g_ref[...], dy_ref[...])
        if has_resid:
            dx = dx + r_ref[...]
        dx_ref[...] = dx.astype(dx_ref.dtype)
        dg_ref[...] += dg
        if chained:
            dx2, dg2 = _seg_rms_bwd(x2_ref[...], g2_ref[...], dx)
            outs[2][...] = dx2.astype(BF16)
            outs[3][...] += dg2

    row = pl.BlockSpec((tm, W), lambda i: (i, 0))
    vec = pl.BlockSpec((1, W), lambda i: (0, 0))
    ins = [x, g, dy] + ([resid] if has_resid else []) + (list(then) if chained else [])
    return pl.pallas_call(
        body, name=name, grid=(S // tm,),
        in_specs=[row, vec, row] + ([row] if has_resid else []) + ([row, vec] if chained else []),
        out_specs=[row, vec] + ([row, vec] if chained else []),
        out_shape=[jax.ShapeDtypeStruct((S, W), out_dtype), jax.ShapeDtypeStruct((1, W), F32)]
        + ([jax.ShapeDtypeStruct((S, W), BF16), jax.ShapeDtypeStruct((1, W), F32)] if chained else []),
        compiler_params=_params("arbitrary"),
    )(*ins)


def _resid_rms(x, y, g_post, g_next, *, name):
    S, W = x.shape
    tm = _tile(S, 512)
    with_next = g_next is not None

    def body(*refs):
        if with_next:
            x_ref, y_ref, gp_ref, gn_ref, xo_ref, h_ref = refs
        else:
            x_ref, y_ref, gp_ref, xo_ref = refs
        xn = x_ref[...] + _seg_rms(y_ref[...], gp_ref[...])
        xo_ref[...] = xn
        if with_next:
            h_ref[...] = _seg_rms(xn, gn_ref[...]).astype(BF16)

    row = pl.BlockSpec((tm, W), lambda i: (i, 0))
    vec = pl.BlockSpec((1, W), lambda i: (0, 0))
    outs = [jax.ShapeDtypeStruct((S, W), F32)] + ([jax.ShapeDtypeStruct((S, W), BF16)] if with_next else [])
    res = pl.pallas_call(
        body, name=name, grid=(S // tm,),
        in_specs=[row, row, vec] + ([vec] if with_next else []),
        out_specs=[row] + ([row] if with_next else []),
        out_shape=outs,
        compiler_params=_params("parallel"),
    )(*([x, y, g_post] + ([g_next] if with_next else [])))
    return (res[0], res[1]) if with_next else (res[0], None)


def _col_rms(xs, g):
    r = lax.rsqrt(jnp.mean(xs * xs, axis=0, keepdims=True) + EPS)
    return xs * r * g


def _col_rms_bwd(xs, g, dy):
    r = lax.rsqrt(jnp.mean(xs * xs, axis=0, keepdims=True) + EPS)
    gd = dy * g
    c = jnp.mean(gd * xs, axis=0, keepdims=True)
    dx = r * gd - xs * (r * r * r * c)
    dg = jnp.sum(dy * (xs * r), axis=1, keepdims=True)
    return dx, dg


GROUP_ROWS = (SWA_Q_HEADS * HEAD_DIM, FOX_HEADS * HEAD_DIM, MLA_HEADS * HEAD_DIM)


def _group_specs(S, tn):
    outs = [pl.BlockSpec((n, tn), lambda i: (0, i)) for n in GROUP_ROWS]
    g = pl.BlockSpec((D_MODEL, 1), lambda i: (0, 0))
    mixed = pl.BlockSpec((D_MODEL, tn), lambda i: (0, i))
    return outs, g, mixed


def _group_norm_fwd(oa, of, oc, g, *, name):
    S = oa.shape[1]
    tn = _tile(S, 512)
    outs, gs, mixed = _group_specs(S, tn)

    def body(a_ref, f_ref, c_ref, g_ref, o_ref):
        r0 = 0
        for ref, n in zip((a_ref, f_ref, c_ref), GROUP_ROWS):
            o_ref[r0:r0 + n, :] = _col_rms(ref[...], g_ref[r0:r0 + n, :]).astype(BF16)
            r0 += n

    return pl.pallas_call(
        body, name=name, grid=(S // tn,),
        in_specs=outs + [gs], out_specs=mixed,
        out_shape=jax.ShapeDtypeStruct((D_MODEL, S), BF16),
        compiler_params=_params("parallel"),
    )(oa, of, oc, g)


def _group_norm_bwd(oa, of, oc, g, dmixed, *, name):
    S = oa.shape[1]
    tn = _tile(S, 512)
    outs, gs, mixed = _group_specs(S, tn)
    n_heads = D_MODEL // HEAD_DIM

    def body(a_ref, f_ref, c_ref, g_ref, dm_ref, da_ref, df_ref, dc_ref, dg_ref, dl_ref):
        @pl.when(pl.program_id(0) == 0)
        def _():
            dg_ref[...] = jnp.zeros_like(dg_ref)

        r0 = 0
        for ref, dref, n in zip((a_ref, f_ref, c_ref), (da_ref, df_ref, dc_ref), GROUP_ROWS):
            o = ref[...]
            dx, dg = _col_rms_bwd(o, g_ref[r0:r0 + n, :], dm_ref[r0:r0 + n, :])
            dxb = dx.astype(BF16)
            dref[...] = dxb
            dg_ref[r0:r0 + n, :] += dg
            od = o * dxb.astype(F32)
            for h in range(n // HEAD_DIM):
                dl_ref[r0 // HEAD_DIM + h] = jnp.sum(od[h * HEAD_DIM:(h + 1) * HEAD_DIM, :], axis=0, keepdims=True)
            r0 += n

    return pl.pallas_call(
        body, name=name, grid=(S // tn,),
        in_specs=outs + [gs, mixed], out_specs=outs + [gs, pl.BlockSpec((n_heads, 1, tn), lambda i: (0, 0, i))],
        out_shape=[jax.ShapeDtypeStruct((n, S), BF16) for n in GROUP_ROWS] + [jax.ShapeDtypeStruct((D_MODEL, 1), F32),
                                                                              jax.ShapeDtypeStruct((n_heads, 1, S), F32)],
        compiler_params=_params("arbitrary"),
    )(oa, of, oc, g, dmixed)


def _loss_head(y, target):
    S, W = y.shape
    tm = _tile(S, 512)

    def body(y_ref, t_ref, d_ref, l_ref):
        @pl.when(pl.program_id(0) == 0)
        def _():
            l_ref[...] = jnp.zeros_like(l_ref)

        err = y_ref[...] - t_ref[...]
        d_ref[...] = err * (1.0 / W)
        l_ref[...] += 0.5 * jnp.sum(jnp.mean(err * err, axis=-1, keepdims=True), axis=0, keepdims=True)

    row = pl.BlockSpec((tm, W), lambda i: (i, 0))
    d, l = pl.pallas_call(
        body, name="loss_head", grid=(S // tm,),
        in_specs=[row, row],
        out_specs=[row, pl.BlockSpec((1, 1), lambda i: (0, 0))],
        out_shape=[jax.ShapeDtypeStruct((S, W), F32), jax.ShapeDtypeStruct((1, 1), F32)],
        compiler_params=_params("arbitrary"),
    )(y, target)
    return l[0, 0], d


def _attn_fwd(q_src, k_src, v_src, rows, H, Dk, Dv, scale, f_row=None, f_col=None, *, name, q_scaled=False):
    S = q_src.shape[1]
    T = _tile(S, ATT_TILE)
    nq = S // T
    forget = f_row is not None
    qb, kb, vb = rows[0] // (H * Dk), rows[1] // (H * Dk), rows[2] // (H * Dv)
    hs = range(H)

    def body(*refs):
        if forget:
            q_ref, k_ref, v_ref, fq_ref, fk_ref, o_ref, lse_ref = refs
        else:
            q_ref, k_ref, v_ref, o_ref, lse_ref = refs
        i = pl.program_id(0)

        def tile(j, masked, state):
            off = pl.multiple_of(j * T, T)
            ss = [_dot(k_ref[h * Dk:(h + 1) * Dk, pl.ds(off, T)], q_ref[h * Dk:(h + 1) * Dk, :], TN) for h in hs]
            if not q_scaled:
                ss = [s * (scale * LOG2E) for s in ss]
            if forget:
                ss = [ss[h] + (fq_ref[h] - fk_ref[pl.ds(off, T), h:h + 1]) for h in hs]
            if masked:
                r = lax.broadcasted_iota(jnp.int32, (T, T), 0)
                c = lax.broadcasted_iota(jnp.int32, (T, T), 1)
                ss = [jnp.where(r <= c, s, NEG_INF) for s in ss]
            m_new = [jnp.maximum(state[h][0], jnp.max(ss[h], axis=0, keepdims=True)) for h in hs]
            alpha = [jnp.exp2(state[h][0] - m_new[h]) for h in hs]
            ps = [jnp.exp2(ss[h] - m_new[h]) for h in hs]
            l_new = [alpha[h] * state[h][1] + jnp.sum(ps[h], axis=0, keepdims=True) for h in hs]
            p_hi = [p.astype(BF16) for p in ps]
            vs = [v_ref[h * Dv:(h + 1) * Dv, pl.ds(off, T)] for h in hs]
            pv = [_dot(vs[h], p_hi[h]) for h in hs]
            if forget:
                pv = [pv[h] + _dot(vs[h], (ps[h] - p_hi[h].astype(F32)).astype(BF16)) for h in hs]
            return tuple((m_new[h], l_new[h], alpha[h] * state[h][2] + pv[h]) for h in hs)

        init = tuple((jnp.full((1, T), NEG_INF, F32), jnp.zeros((1, T), F32), jnp.zeros((Dv, T), F32)) for _ in hs)
        state = lax.fori_loop(0, i, lambda j, st: tile(j, False, st), init)
        state = tile(i, True, state)
        for h in hs:
            m, l, acc = state[h]
            o_ref[h * Dv:(h + 1) * Dv, :] = acc / l
            lse_ref[h] = m + jnp.log2(l)

    in_specs = [pl.BlockSpec((H * Dk, T), lambda i: (qb, i)),
                pl.BlockSpec((H * Dk, S), lambda i: (kb, 0)),
                pl.BlockSpec((H * Dv, S), lambda i: (vb, 0))]
    ins = [q_src, k_src, v_src]
    if forget:
        in_specs += [pl.BlockSpec((H, 1, T), lambda i: (0, 0, i)), pl.BlockSpec((S, H), lambda i: (0, 0))]
        ins += [f_row, f_col]
    return pl.pallas_call(
        body, name=name, grid=(nq,),
        in_specs=in_specs,
        out_specs=[pl.BlockSpec((H * Dv, T), lambda i: (0, i)), pl.BlockSpec((H, 1, T), lambda i: (0, 0, i))],
        out_shape=[jax.ShapeDtypeStruct((H * Dv, S), F32), jax.ShapeDtypeStruct((H, 1, S), F32)],
        compiler_params=_params("parallel"),
    )(*ins)


def _attn_bwd(q_src, k_src, v_src, rows, H, Dk, Dv, scale, do, lse, delta, f_row=None, f_col=None, *, name, q_scaled=False):
    S = q_src.shape[1]
    T = _tile(S, ATT_TILE)
    nq = S // T
    forget = f_row is not None
    qb, kb, vb, db = rows[0] // (H * Dk), rows[1] // (H * Dk), rows[2] // (H * Dv), rows[3] // H
    hs = range(H)

    def body(*refs):
        if forget:
            (q_ref, k_ref, v_ref, do_ref, lse_ref, dl_ref, fq_ref, fk_ref,
             dq_ref, dk_ref, dv_ref, df_ref, dk_s, dv_s, df_s) = refs
        else:
            q_ref, k_ref, v_ref, do_ref, lse_ref, dl_ref, dq_ref, dk_ref, dv_ref, dk_s, dv_s = refs
        j = pl.program_id(0)

        @pl.when(j == 0)
        def _():
            dq_ref[...] = jnp.zeros_like(dq_ref)

        dk_s[...] = jnp.zeros_like(dk_s)
        dv_s[...] = jnp.zeros_like(dv_s)
        if forget:
            df_s[...] = jnp.zeros_like(df_s)
        kt = [k_ref[h * Dk:(h + 1) * Dk, :] for h in hs]
        kj = [k.T for k in kt]
        vj = [v_ref[h * Dv:(h + 1) * Dv, :].T for h in hs]
        koff = pl.multiple_of(j * T, T)

        def tile(i, masked):
            cols = pl.ds(pl.multiple_of(i * T, T), T)
            qi = [q_ref[h * Dk:(h + 1) * Dk, cols] for h in hs]
            doi = [do_ref[h * Dv:(h + 1) * Dv, cols] for h in hs]
            st = [_dot(kj[h], qi[h]) for h in hs]
            if not q_scaled:
                st = [x * (scale * LOG2E) for x in st]
            if forget:
                st = [st[h] + (fq_ref[h, :, cols] - fk_ref[pl.ds(koff, T), h:h + 1]) for h in hs]
            if masked:
                r = lax.broadcasted_iota(jnp.int32, (T, T), 0)
                c = lax.broadcasted_iota(jnp.int32, (T, T), 1)
                st = [jnp.where(r <= c, x, NEG_INF) for x in st]
            pt = [jnp.exp2(st[h] - lse_ref[h, :, cols]) for h in hs]
            dpt = [_dot(vj[h], doi[h]) for h in hs]
            dst = [pt[h] * (dpt[h] - dl_ref[h, :, cols]) for h in hs]
            ptb = [p.astype(BF16) for p in pt]
            dsb = [d.astype(BF16) for d in dst]
            for h in hs:
                dv_s[h * Dv:(h + 1) * Dv, :] += _dot(doi[h], ptb[h], NT)
            for h in hs:
                dk_s[h * Dk:(h + 1) * Dk, :] += _dot(qi[h], dsb[h], NT)
            for h in hs:
                dq_ref[h * Dk:(h + 1) * Dk, cols] += _dot(kt[h], dsb[h]) * scale
            if forget:
                for h in hs:
                    part = dst[h][:, 0:LANES]
                    for c0 in range(LANES, T, LANES):
                        part = part + dst[h][:, c0:c0 + LANES]
                    df_s[h] += part

        tile(j, True)

        def loop_body(i, carry):
            tile(i, False)
            return carry

        lax.fori_loop(j + 1, nq, loop_body, 0)
        dk_ref[...] = dk_s[...] * ((1.0 / LOG2E) if q_scaled else scale)
        dv_ref[...] = dv_s[...]
        if forget:
            df_ref[...] = jnp.concatenate([-jnp.sum(df_s[h], axis=-1, keepdims=True) for h in hs], axis=1)

    res = lambda D, b0: pl.BlockSpec((H * D, S), lambda j: (b0, 0))
    blk = lambda D, b0: pl.BlockSpec((H * D, T), lambda j: (b0, j))
    row3 = lambda b0: pl.BlockSpec((H, 1, S), lambda j: (b0, 0, 0))
    in_specs = [res(Dk, qb), blk(Dk, kb), blk(Dv, vb), res(Dv, 0), row3(0), row3(db)]
    ins = [q_src, k_src, v_src, do, lse, delta]
    out_specs = [res(Dk, 0), blk(Dk, 0), blk(Dv, 0)]
    out_shape = [jax.ShapeDtypeStruct((H * Dk, S), F32), jax.ShapeDtypeStruct((H * Dk, S), F32),
                 jax.ShapeDtypeStruct((H * Dv, S), F32)]
    scratch = [pltpu.VMEM((H * Dk, T), F32), pltpu.VMEM((H * Dv, T), F32)]
    if forget:
        in_specs += [row3(0), pl.BlockSpec((S, H), lambda j: (0, 0))]
        ins += [f_row, f_col]
        out_specs.append(pl.BlockSpec((T, H), lambda j: (j, 0)))
        out_shape.append(jax.ShapeDtypeStruct((S, H), F32))
        scratch.append(pltpu.VMEM((H, T, min(T, LANES)), F32))
    return pl.pallas_call(
        body, name=name, grid=(nq,),
        in_specs=in_specs, out_specs=out_specs, out_shape=out_shape, scratch_shapes=scratch,
        compiler_params=_params("arbitrary"),
    )(*ins)


GW = SWA_GROUP * WINDOW


def _swa_masks(i):
    r = lax.broadcasted_iota(jnp.int32, (WINDOW, GW), 0)
    c = lax.broadcasted_iota(jnp.int32, (WINDOW, GW), 1) % WINDOW
    return (r > c) & (i > 0), r <= c


def _swa_specs():
    W = WINDOW
    kv_rows = SWA_KV_HEADS * HEAD_DIM
    q = pl.BlockSpec((SWA_Q_HEADS * HEAD_DIM, W), lambda i: (0, i))
    prev = lambda b: pl.BlockSpec((kv_rows, W), lambda i: (b, jnp.maximum(i - 1, 0)))
    cur = lambda b: pl.BlockSpec((kv_rows, W), lambda i: (b, i))
    bias = pl.BlockSpec((SWA_KV_HEADS, 2 * W, GW), lambda i: (0, 0, 0))
    stat = pl.BlockSpec((SWA_Q_HEADS, W), lambda i: (0, i))
    sink = pl.BlockSpec(memory_space=pltpu.SMEM)
    return q, prev(4), cur(4), prev(5), cur(5), bias, stat, sink


def _group_lanes(ref, g, rows_per_head):
    h0 = g * SWA_GROUP
    return jnp.concatenate([ref[(h0 + j) * rows_per_head:(h0 + j + 1) * rows_per_head, :] for j in range(SWA_GROUP)], axis=1)


def _swa_scores(g, q_ref, kp_ref, kc_ref, b_ref, masks):
    rows = slice(g * HEAD_DIM, (g + 1) * HEAD_DIM)
    qg = _group_lanes(q_ref, g, HEAD_DIM)
    scale = HEAD_DIM ** -0.5
    s_p = jnp.where(masks[0], _dot(kp_ref[rows, :], qg, TN) * scale + b_ref[g, 0:WINDOW, :], NEG_INF)
    s_c = jnp.where(masks[1], _dot(kc_ref[rows, :], qg, TN) * scale + b_ref[g, WINDOW:2 * WINDOW, :], NEG_INF)
    return qg, rows, s_p, s_c


def _sink_row(sink_ref, g):
    return jnp.concatenate([jnp.full((1, WINDOW), sink_ref[g * SWA_GROUP + j], F32) for j in range(SWA_GROUP)], axis=1)


def _swa_fwd(qkv, bias_g, sinks, *, name):
    S = qkv.shape[1]
    qs, kp, kc, vp, vc, bs, stat, sk = _swa_specs()
    gs = range(SWA_KV_HEADS)

    def body(sink_ref, q_ref, kp_ref, kc_ref, vp_ref, vc_ref, b_ref, o_ref, lse_ref):
        masks = _swa_masks(pl.program_id(0))
        sc = [_swa_scores(g, q_ref, kp_ref, kc_ref, b_ref, masks) for g in gs]
        sinks_g = [_sink_row(sink_ref, g) for g in gs]
        m = [jnp.maximum(jnp.maximum(jnp.max(sc[g][2], axis=0, keepdims=True), jnp.max(sc[g][3], axis=0, keepdims=True)),
                         sinks_g[g]) for g in gs]
        p_p = [jnp.exp(sc[g][2] - m[g]) for g in gs]
        p_c = [jnp.exp(sc[g][3] - m[g]) for g in gs]
        l = [jnp.sum(p_p[g], axis=0, keepdims=True) + jnp.sum(p_c[g], axis=0, keepdims=True) + jnp.exp(sinks_g[g] - m[g])
             for g in gs]
        o = [_dot(vp_ref[sc[g][1], :], p_p[g].astype(BF16)) + _dot(vc_ref[sc[g][1], :], p_c[g].astype(BF16)) for g in gs]
        for g in gs:
            og = o[g] / l[g]
            lse = m[g] + jnp.log(l[g])
            for j in range(SWA_GROUP):
                h = g * SWA_GROUP + j
                o_ref[h * HEAD_DIM:(h + 1) * HEAD_DIM, :] = og[:, j * WINDOW:(j + 1) * WINDOW]
                lse_ref[h:h + 1, :] = lse[:, j * WINDOW:(j + 1) * WINDOW]

    return pl.pallas_call(
        body, name=name, grid=(S // WINDOW,),
        in_specs=[sk, qs, kp, kc, vp, vc, bs],
        out_specs=[qs, stat],
        out_shape=[jax.ShapeDtypeStruct((SWA_Q_HEADS * HEAD_DIM, S), F32), jax.ShapeDtypeStruct((SWA_Q_HEADS, S), F32)],
        compiler_params=_params("parallel"),
    )(sinks, qkv, qkv, qkv, qkv, qkv, bias_g)


def _swa_bwd(qkv, bias_g, sinks, do, lse, delta, *, name):
    S = qkv.shape[1]
    W = WINDOW
    qs, kp, kc, vp, vc, bs, stat, sk = _swa_specs()
    scale = HEAD_DIM ** -0.5
    kv_rows = SWA_KV_HEADS * HEAD_DIM
    gs = range(SWA_KV_HEADS)

    def body(sink_ref, q_ref, kp_ref, kc_ref, vp_ref, vc_ref, b_ref, do_ref, lse_ref, dl_ref,
             dq_ref, dkv_ref, db_ref, dsk_ref):
        i = pl.program_id(0)

        @pl.when(i == 0)
        def _():
            dkv_ref[...] = jnp.zeros_like(dkv_ref)
            db_ref[...] = jnp.zeros_like(db_ref)
            dsk_ref[...] = jnp.zeros_like(dsk_ref)

        masks = _swa_masks(i)
        prev = pl.ds(pl.multiple_of(jnp.maximum(i - 1, 0) * W, W), W)
        cur = pl.ds(pl.multiple_of(i * W, W), W)
        sc = [_swa_scores(g, q_ref, kp_ref, kc_ref, b_ref, masks) for g in gs]
        dog = [_group_lanes(do_ref, g, HEAD_DIM) for g in gs]
        lse = [_group_lanes(lse_ref, g, 1) for g in gs]
        dl = [_group_lanes(dl_ref, g, 1) for g in gs]
        p_p = [jnp.exp(sc[g][2] - lse[g]) for g in gs]
        p_c = [jnp.exp(sc[g][3] - lse[g]) for g in gs]
        ds_p = [p_p[g] * (_dot(vp_ref[sc[g][1], :], dog[g], TN) - dl[g]) for g in gs]
        ds_c = [p_c[g] * (_dot(vc_ref[sc[g][1], :], dog[g], TN) - dl[g]) for g in gs]
        for g in gs:
            db_ref[g, 0:W, :] += ds_p[g]
            db_ref[g, W:2 * W, :] += ds_c[g]
            dsk = jnp.exp(_sink_row(sink_ref, g) - lse[g]) * dl[g]
            for j in range(SWA_GROUP):
                h = g * SWA_GROUP + j
                dsk_ref[h:h + 1, :] -= jnp.broadcast_to(jnp.sum(dsk[:, j * W:(j + 1) * W], axis=1, keepdims=True), (1, LANES))
        dsb_p = [d.astype(BF16) for d in ds_p]
        dsb_c = [d.astype(BF16) for d in ds_c]
        for g in gs:
            rows = sc[g][1]
            dq = (_dot(kp_ref[rows, :], dsb_p[g]) + _dot(kc_ref[rows, :], dsb_c[g])) * scale
            for j in range(SWA_GROUP):
                h = g * SWA_GROUP + j
                dq_ref[h * HEAD_DIM:(h + 1) * HEAD_DIM, :] = dq[:, j * W:(j + 1) * W]
        for g in gs:
            rows = sc[g][1]
            vrows = slice(kv_rows + rows.start, kv_rows + rows.stop)
            dkv_ref[rows, prev] += _dot(sc[g][0], dsb_p[g], NT) * scale
            dkv_ref[rows, cur] += _dot(sc[g][0], dsb_c[g], NT) * scale
            dkv_ref[vrows, prev] += _dot(dog[g], p_p[g].astype(BF16), NT)
            dkv_ref[vrows, cur] += _dot(dog[g], p_c[g].astype(BF16), NT)

    return pl.pallas_call(
        body, name=name, grid=(S // W,),
        in_specs=[sk, qs, kp, kc, vp, vc, bs, qs, stat, stat],
        out_specs=[qs, pl.BlockSpec((2 * kv_rows, S), lambda i: (0, 0)), bs, pl.BlockSpec((SWA_Q_HEADS, LANES), lambda i: (0, 0))],
        out_shape=[jax.ShapeDtypeStruct((SWA_Q_HEADS * HEAD_DIM, S), F32), jax.ShapeDtypeStruct((2 * kv_rows, S), F32),
                   jax.ShapeDtypeStruct((SWA_KV_HEADS, 2 * W, GW), F32), jax.ShapeDtypeStruct((SWA_Q_HEADS, LANES), F32)],
        compiler_params=_params("arbitrary"),
    )(sinks, qkv, qkv, qkv, qkv, qkv, bias_g, do, lse, delta)


def _rel_onehot_t():
    qi = jnp.arange(WINDOW, dtype=jnp.int32)[None, :] + WINDOW
    kj = jnp.arange(2 * WINDOW, dtype=jnp.int32)[:, None]
    dist = qi - kj
    max_exact = REL_BUCKETS // 2
    d = jnp.maximum(dist, 0)
    log_ratio = jnp.log(jnp.maximum(d, 1).astype(F32) / max_exact) / math.log(REL_MAX_DIST / max_exact)
    large = jnp.minimum(max_exact + (log_ratio * (REL_BUCKETS - max_exact)).astype(jnp.int32), REL_BUCKETS - 1)
    bucket = jnp.where(d < max_exact, d, large).reshape(-1)
    return (bucket[None, :] == jnp.arange(REL_BUCKETS, dtype=jnp.int32)[:, None]).astype(BF16)


def _bias_table(rel_bias_t, onehot_t):
    Hq, NB = rel_bias_t.shape
    N = onehot_t.shape[1]
    tn = _tile(N, 4096)

    def body(r_ref, oh_ref, o_ref):
        oh = oh_ref[...]
        a1, a2, a3 = _split3(r_ref[...])
        o_ref[...] = _dot(a1, oh) + _dot(a2, oh) + _dot(a3, oh)

    return pl.pallas_call(
        body, name="rel_bias_table", grid=(N // tn,),
        in_specs=[pl.BlockSpec((Hq, NB), lambda j: (0, 0)), pl.BlockSpec((NB, tn), lambda j: (0, j))],
        out_specs=pl.BlockSpec((Hq, tn), lambda j: (0, j)),
        out_shape=jax.ShapeDtypeStruct((Hq, N), F32),
        compiler_params=_params("parallel"),
    )(rel_bias_t, onehot_t)


def _bias_table_bwd(dbias, onehot_t):
    L, Hq, N = dbias.shape
    NB = onehot_t.shape[0]
    tn = _tile(N, 4096)

    def body(d_ref, oh_ref, o_ref):
        @pl.when(pl.program_id(0) == 0)
        def _():
            o_ref[...] = jnp.zeros_like(o_ref)

        d = d_ref[0]
        for l in range(1, L):
            d = d + d_ref[l]
        oh = oh_ref[...]
        a1, a2, a3 = _split3(d)
        o_ref[...] += _dot(a1, oh, NT) + _dot(a2, oh, NT) + _dot(a3, oh, NT)

    return pl.pallas_call(
        body, name="rel_bias_bwd", grid=(N // tn,),
        in_specs=[pl.BlockSpec((L, Hq, tn), lambda j: (0, 0, j)), pl.BlockSpec((NB, tn), lambda j: (0, j))],
        out_specs=pl.BlockSpec((Hq, NB), lambda j: (0, 0)),
        out_shape=jax.ShapeDtypeStruct((Hq, NB), F32),
        compiler_params=_params("arbitrary"),
    )(dbias, onehot_t)


def _gate_fwd(lat, fb_col, *, name):
    S = lat.shape[1]
    tn = _tile(S, 256)

    def body(z_ref, fb_ref, o_ref, carry):
        @pl.when(pl.program_id(0) == 0)
        def _():
            carry[...] = jnp.zeros_like(carry)

        z = z_ref[...] + fb_ref[...]
        lf = jnp.minimum(z, 0.0) - jnp.log1p(jnp.exp(-jnp.abs(z)))
        r = lax.broadcasted_iota(jnp.int32, (tn, tn), 0)
        c = lax.broadcasted_iota(jnp.int32, (tn, tn), 1)
        tri = (r <= c).astype(BF16)
        a1, a2, a3 = _split3(lf)
        cum = _dot(a1, tri) + _dot(a2, tri) + _dot(a3, tri) + carry[:, 0:1]
        o_ref[...] = cum
        carry[...] = jnp.broadcast_to(cum[:, tn - 1:tn], carry.shape)

    return pl.pallas_call(
        body, name=name, grid=(S // tn,),
        in_specs=[pl.BlockSpec((GATE_ROWS, tn), lambda i: (0, i)), pl.BlockSpec((GATE_ROWS, 1), lambda i: (0, 0))],
        out_specs=pl.BlockSpec((GATE_ROWS, tn), lambda i: (0, i)),
        out_shape=jax.ShapeDtypeStruct((GATE_ROWS, S), F32),
        scratch_shapes=[pltpu.VMEM((GATE_ROWS, LANES), F32)],
        compiler_params=_params("arbitrary"),
    )(lat, fb_col)


def _gate_bwd(lat, fb_col, dF, *, name):
    S = lat.shape[1]
    tn = _tile(S, 256)
    nt = S // tn

    def body(z_ref, fb_ref, df_ref, dz_ref, dfb_ref, carry):
        @pl.when(pl.program_id(0) == 0)
        def _():
            carry[...] = jnp.zeros_like(carry)
            dfb_ref[...] = jnp.zeros_like(dfb_ref)

        r = lax.broadcasted_iota(jnp.int32, (tn, tn), 0)
        c = lax.broadcasted_iota(jnp.int32, (tn, tn), 1)
        tri = (r >= c).astype(BF16)
        a1, a2, a3 = _split3(df_ref[...])
        dlf = _dot(a1, tri) + _dot(a2, tri) + _dot(a3, tri) + carry[:, 0:1]
        carry[...] = jnp.broadcast_to(dlf[:, 0:1], carry.shape)
        z = z_ref[...] + fb_ref[...]
        row = lax.broadcasted_iota(jnp.int32, (GATE_ROWS, tn), 0)
        dz = jnp.where(row < FOX_HEADS, dlf / (1.0 + jnp.exp(z)), 0.0)
        dz_ref[...] = dz
        dfb_ref[...] += jnp.sum(dz, axis=1, keepdims=True)

    blk = pl.BlockSpec((GATE_ROWS, tn), lambda i: (0, nt - 1 - i))
    vec = pl.BlockSpec((GATE_ROWS, 1), lambda i: (0, 0))
    return pl.pallas_call(
        body, name=name, grid=(nt,),
        in_specs=[blk, vec, blk], out_specs=[blk, vec],
        out_shape=[jax.ShapeDtypeStruct((GATE_ROWS, S), F32), jax.ShapeDtypeStruct((GATE_ROWS, 1), F32)],
        scratch_shapes=[pltpu.VMEM((GATE_ROWS, LANES), F32)],
        compiler_params=_params("arbitrary"),
    )(lat, fb_col, dF)


def _rope_tables(S):
    pos = jnp.arange(S, dtype=F32)
    inv_freq = ROPE_THETA ** (-(jnp.arange(MLA_ROPE // 2, dtype=F32) * 2.0 / MLA_ROPE))
    ang = pos[:, None] * inv_freq[None, :]
    cos, sin = jnp.cos(ang).T, jnp.sin(ang).T
    z16 = jnp.zeros_like(cos)

    def slab(lo, fill):
        def put(first, second, f):
            return jnp.concatenate([jnp.full((lo, S), f, F32), first, second, jnp.full((LANES - lo - MLA_ROPE, S), f, F32)], axis=0)
        return put(cos, cos, fill), put(-sin, z16, 0.0), put(z16, sin, 0.0)

    tq = tuple(jnp.tile(t, (MLA_HEADS, 1)) for t in slab(MLA_NOPE, 1.0))
    return tq, slab(0, 0.0)


def _rope(x, c, s1, s2):
    n = x.shape[0]
    half = MLA_ROPE // 2
    return x * c + pltpu.roll(x, n - half, 0) * s1 + pltpu.roll(x, half, 0) * s2


def _rope_t(dy, c, s1, s2):
    n = dy.shape[0]
    half = MLA_ROPE // 2
    return dy * c + pltpu.roll(dy * s1, half, 0) + pltpu.roll(dy * s2, n - half, 0)


KR_SLAB0 = MLA_Q_RANK + MLA_KV_RANK


def _mla_prep_fwd(lat, g_q, g_kv, w_uq_t, w_ukv_t, tq, tmisc, *, name):
    S = lat.shape[1]
    tn = _tile(S, 512)
    QW = MLA_HEADS * MLA_PAD

    def body(lat_ref, gq_ref, gkv_ref, wq_ref, wkv_ref, c_ref, s1_ref, s2_ref, cm_ref, s1m_ref, s2m_ref,
             nq_ref, nkv_ref, q_ref, k_ref, v_ref):
        x = pltpu.roll(lat_ref[...], LAT_ROWS - LAT_SHIFT, 0)
        nq = _col_rms(x[0:MLA_Q_RANK, :], gq_ref[...]).astype(BF16)
        nkv = _col_rms(x[MLA_Q_RANK:KR_SLAB0, :], gkv_ref[...]).astype(BF16)
        nq_ref[...] = nq
        nkv_ref[...] = nkv
        q = _rope(_dot(wq_ref[...], nq), c_ref[...], s1_ref[...], s2_ref[...])
        q_ref[...] = (q * (MLA_SCALE * LOG2E)).astype(BF16)
        kv = _dot(wkv_ref[...], nkv).astype(BF16)
        kr = _rope(x[KR_SLAB0:LAT_ROWS, :], cm_ref[...], s1m_ref[...], s2m_ref[...]).astype(BF16)
        for h in range(MLA_HEADS):
            k_ref[h * MLA_PAD:h * MLA_PAD + MLA_NOPE, :] = kv[h * LANES:h * LANES + MLA_NOPE, :]
            k_ref[h * MLA_PAD + MLA_NOPE:(h + 1) * MLA_PAD, :] = kr[0:MLA_PAD - MLA_NOPE, :]
            v_ref[h * HEAD_DIM:(h + 1) * HEAD_DIM, :] = kv[h * LANES + MLA_NOPE:(h + 1) * LANES, :]

    def col(rows):
        return pl.BlockSpec((rows, tn), lambda i: (0, i))

    def full(a):
        return pl.BlockSpec(a.shape, lambda i: (0, 0))

    return pl.pallas_call(
        body, name=name, grid=(S // tn,),
        in_specs=[col(LAT_ROWS), full(g_q), full(g_kv), full(w_uq_t), full(w_ukv_t),
                  col(QW), col(QW), col(QW), col(LANES), col(LANES), col(LANES)],
        out_specs=[col(MLA_Q_RANK), col(MLA_KV_RANK), col(QW), col(QW), col(MLA_HEADS * HEAD_DIM)],
        out_shape=[jax.ShapeDtypeStruct((MLA_Q_RANK, S), BF16), jax.ShapeDtypeStruct((MLA_KV_RANK, S), BF16),
                   jax.ShapeDtypeStruct((QW, S), BF16), jax.ShapeDtypeStruct((QW, S), BF16),
                   jax.ShapeDtypeStruct((MLA_HEADS * HEAD_DIM, S), BF16)],
        compiler_params=_params("parallel"),
    )(lat, g_q, g_kv, w_uq_t, w_ukv_t, *tq, *tmisc)


def _mla_prep_bwd(lat, nq, nkv, g_q, g_kv, w_uq_p, w_ukv, tq, tmisc, dq, dk, dv, dflog, *, name):
    S = lat.shape[1]
    tn = _tile(S, 512)
    QW = MLA_HEADS * MLA_PAD

    def body(lat_ref, nq_ref, nkv_ref, gq_ref, gkv_ref, wq_ref, wkv_ref, c_ref, s1_ref, s2_ref,
             cm_ref, s1m_ref, s2m_ref, dq_ref, dk_ref, dv_ref, dfl_ref,
             dlat_ref, dwq_ref, dwkv_ref, dgq_ref, dgkv_ref, y_s):
        @pl.when(pl.program_id(0) == 0)
        def _():
            dwq_ref[...] = jnp.zeros_like(dwq_ref)
            dwkv_ref[...] = jnp.zeros_like(dwkv_ref)
            dgq_ref[...] = jnp.zeros_like(dgq_ref)
            dgkv_ref[...] = jnp.zeros_like(dgkv_ref)

        x = pltpu.roll(lat_ref[...], LAT_ROWS - LAT_SHIFT, 0)
        dqm = _rope_t(dq_ref[...], c_ref[...], s1_ref[...], s2_ref[...]).astype(BF16)
        dwq_ref[...] += _dot(dqm, nq_ref[...], NT)
        dx, dg = _col_rms_bwd(x[0:MLA_Q_RANK, :], gq_ref[...], _dot(wq_ref[...], dqm))
        y_s[0:MLA_Q_RANK, :] = dx
        dgq_ref[...] += dg
        dkv = jnp.concatenate(
            [part for h in range(MLA_HEADS)
             for part in (dk_ref[h * MLA_PAD:h * MLA_PAD + MLA_NOPE, :], dv_ref[h * HEAD_DIM:(h + 1) * HEAD_DIM, :])],
            axis=0).astype(BF16)
        dwkv_ref[...] += _dot(dkv, nkv_ref[...], NT)
        dx, dg = _col_rms_bwd(x[MLA_Q_RANK:KR_SLAB0, :], gkv_ref[...], _dot(wkv_ref[...], dkv))
        y_s[MLA_Q_RANK:KR_SLAB0, :] = dx
        dgkv_ref[...] += dg
        dkr = dk_ref[MLA_NOPE:MLA_PAD, :]
        for h in range(1, MLA_HEADS):
            dkr = dkr + dk_ref[h * MLA_PAD + MLA_NOPE:(h + 1) * MLA_PAD, :]
        dkr = jnp.concatenate([dkr, jnp.zeros((MLA_NOPE, tn), F32)], axis=0)
        y_s[KR_SLAB0:LAT_ROWS, :] = _rope_t(dkr, cm_ref[...], s1m_ref[...], s2m_ref[...])
        y = pltpu.roll(y_s[...], LAT_SHIFT, 0)
        row = lax.broadcasted_iota(jnp.int32, (LAT_ROWS, tn), 0)
        dfl = jnp.concatenate([dfl_ref[...], jnp.zeros((LAT_ROWS - GATE_ROWS, tn), F32)], axis=0)
        dlat_ref[...] = jnp.where(row < LAT_SHIFT, dfl, y).astype(BF16)

    def col(rows):
        return pl.BlockSpec((rows, tn), lambda i: (0, i))

    def full(a):
        return pl.BlockSpec(a.shape, lambda i: (0, 0))

    def acc(r, c):
        return pl.BlockSpec((r, c), lambda i: (0, 0))

    return pl.pallas_call(
        body, name=name, grid=(S // tn,),
        in_specs=[col(LAT_ROWS), col(MLA_Q_RANK), col(MLA_KV_RANK), full(g_q), full(g_kv),
                  full(w_uq_p), full(w_ukv), col(QW), col(QW), col(QW), col(LANES), col(LANES), col(LANES),
                  col(QW), col(QW), col(MLA_HEADS * HEAD_DIM), col(GATE_ROWS)],
        out_specs=[col(LAT_ROWS), acc(QW, MLA_Q_RANK), acc(QW, MLA_KV_RANK), acc(MLA_Q_RANK, 1), acc(MLA_KV_RANK, 1)],
        out_shape=[jax.ShapeDtypeStruct((LAT_ROWS, S), BF16), jax.ShapeDtypeStruct((QW, MLA_Q_RANK), F32),
                   jax.ShapeDtypeStruct((QW, MLA_KV_RANK), F32), jax.ShapeDtypeStruct((MLA_Q_RANK, 1), F32),
                   jax.ShapeDtypeStruct((MLA_KV_RANK, 1), F32)],
        scratch_shapes=[pltpu.VMEM((LAT_ROWS, tn), F32)],
        compiler_params=_params("arbitrary"),
    )(lat, nq, nkv, g_q, g_kv, w_uq_p, w_ukv, *tq, *tmisc, dq, dk, dv, dflog)


def _dproj_cast(dqa, dkva, dqf, dkf, dvf, dlat, *, name):
    S = dqa.shape[1]
    tn = _tile(S, 512)
    parts = (dqa, dkva, dqf, dkf, dvf, dlat)

    def body(*refs):
        o_ref = refs[-1]
        r0 = 0
        for ref in refs[:-1]:
            n = ref.shape[0]
            o_ref[r0:r0 + n, :] = ref[...].astype(BF16)
            r0 += n

    return pl.pallas_call(
        body, name=name, grid=(S // tn,),
        in_specs=[pl.BlockSpec((p.shape[0], tn), lambda i: (0, i)) for p in parts],
        out_specs=pl.BlockSpec((IN_ROWS, tn), lambda i: (0, i)),
        out_shape=jax.ShapeDtypeStruct((IN_ROWS, S), BF16),
        compiler_params=_params("parallel"),
    )(*parts)


GELU_C = math.sqrt(2.0 / math.pi)
GELU_A = 0.044715


HALO = 16


def _shift_down(a, k, fill):
    r = pltpu.roll(a, k, 0)
    row = lax.broadcasted_iota(jnp.int32, (8, a.shape[1]), 0)
    head = r[0:8, :]
    for i in range(k):
        head = jnp.where(row == i, fill[len(fill) - k + i], head)
    return jnp.concatenate([head, r[8:, :]], axis=0)


def _shift_up(d, k, fill):
    n = d.shape[0]
    r = pltpu.roll(d, n - k, 0)
    row = lax.broadcasted_iota(jnp.int32, (8, d.shape[1]), 0)
    tail = r[n - 8:n, :]
    for i in range(k):
        tail = jnp.where(row == 8 - k + i, fill[i], tail)
    return jnp.concatenate([r[0:n - 8, :], tail], axis=0)


def _conv_taps(a, before, w_ref, b_ref):
    a1 = _shift_down(a, 1, before)
    a2 = _shift_down(a, 2, before)
    return ((b_ref[...] + w_ref[0:1, :] * a2) + w_ref[1:2, :] * a1) + w_ref[2:3, :] * a


def _rows_before(halo_ref, first):
    h = halo_ref[HALO - 2:HALO, :].astype(F32)
    return jnp.where(first, 0.0, h[0:1, :]), jnp.where(first, 0.0, h[1:2, :])


def _conv_specs(S, tm, tc, nc):
    hb = tm // HALO
    main = lambda off: pl.BlockSpec((tm, tc), lambda j, i: (i, j + off))
    prev = lambda off: pl.BlockSpec((HALO, tc), lambda j, i: (jnp.maximum(i * hb - 1, 0), j + off))
    wspec = lambda off: pl.BlockSpec((3, tc), lambda j, i: (0, j + off))
    bspec = lambda off: pl.BlockSpec((1, tc), lambda j, i: (0, j + off))
    return main, prev, wspec, bspec


def _conv_geglu_fwd(a, conv_w, conv_b, *, name):
    S = a.shape[0]
    tm, tc = _tile(S, 512), _tile(D_FF, 1408)
    nc = D_FF // tc
    main, prev, wspec, bspec = _conv_specs(S, tm, tc, nc)

    def body(ag_ref, au_ref, hg_ref, hu_ref, wg_ref, wu_ref, bg_ref, bu_ref, u_ref, z_ref):
        first = pl.program_id(1) == 0
        gate = _conv_taps(ag_ref[...].astype(F32), _rows_before(hg_ref, first), wg_ref, bg_ref)
        up = _conv_taps(au_ref[...].astype(F32), _rows_before(hu_ref, first), wu_ref, bu_ref)
        u_ref[0] = gate
        u_ref[1] = up
        cdf = 0.5 * (1.0 + jnp.tanh(GELU_C * (gate + GELU_A * (gate * gate * gate))))
        z_ref[...] = (gate * cdf * up).astype(BF16)

    return pl.pallas_call(
        body, name=name, grid=(nc, S // tm),
        in_specs=[main(0), main(nc), prev(0), prev(nc), wspec(0), wspec(nc), bspec(0), bspec(nc)],
        out_specs=[pl.BlockSpec((2, tm, tc), lambda j, i: (0, i, j)), pl.BlockSpec((tm, tc), lambda j, i: (i, j))],
        out_shape=[jax.ShapeDtypeStruct((2, S, D_FF), F32), jax.ShapeDtypeStruct((S, D_FF), BF16)],
        compiler_params=_params("parallel", "arbitrary"),
    )(a, a, a, a, conv_w, conv_w, conv_b, conv_b)


def _geglu_bwd(gate, up, dz):
    g2x = gate * gate
    th = jnp.tanh(GELU_C * (gate + GELU_A * (g2x * gate)))
    cdf = 0.5 * (1.0 + th)
    dgelu = cdf + gate * (0.5 * (1.0 - th * th) * (GELU_C * (1.0 + 3.0 * GELU_A * g2x)))
    return dz * up * dgelu, dz * (gate * cdf)


def _conv_geglu_bwd(a, u, conv_w, dz, *, name):
    S = a.shape[0]
    tm, tc = _tile(S, 512), _tile(D_FF, 1408)
    nc = D_FF // tc
    nr = S // tm
    main, _, wspec, _ = _conv_specs(S, tm, tc, nc)
    hb = tm // 8

    def body(ag_ref, au_ref, u_ref, un_ref, wg_ref, wu_ref, dz_ref, dzn_ref, da_ref, dw_ref, db_ref):
        i = pl.program_id(1)
        last = i == nr - 1

        @pl.when(i == 0)
        def _():
            dw_ref[...] = jnp.zeros_like(dw_ref)
            db_ref[...] = jnp.zeros_like(db_ref)

        dus = _geglu_bwd(u_ref[0], u_ref[1], dz_ref[...])
        dus_n = _geglu_bwd(un_ref[0], un_ref[1], dzn_ref[...])
        for half, a_ref, w_ref in ((0, ag_ref, wg_ref), (1, au_ref, wu_ref)):
            du, du_n = dus[half], dus_n[half]
            after = (jnp.where(last, 0.0, du_n[0:1, :]), jnp.where(last, 0.0, du_n[1:2, :]))
            shifted = (_shift_up(du, 2, after), _shift_up(du, 1, after), du)
            da_ref[half] = (w_ref[2:3, :] * du + w_ref[1:2, :] * shifted[1] + w_ref[0:1, :] * shifted[0]).astype(BF16)
            af = a_ref[...].astype(F32)
            for tap in range(3):
                dw_ref[half, tap:tap + 1, :] += jnp.sum(shifted[tap] * af, axis=0, keepdims=True)
            db_ref[half] += jnp.sum(du, axis=0, keepdims=True)

    nxt8 = lambda j, i: (0, jnp.minimum((i + 1) * hb, S // 8 - 1), j)
    return pl.pallas_call(
        body, name=name, grid=(nc, nr),
        in_specs=[main(0), main(nc), pl.BlockSpec((2, tm, tc), lambda j, i: (0, i, j)), pl.BlockSpec((2, 8, tc), nxt8),
                  wspec(0), wspec(nc), pl.BlockSpec((tm, tc), lambda j, i: (i, j)),
                  pl.BlockSpec((8, tc), lambda j, i: (jnp.minimum((i + 1) * hb, S // 8 - 1), j))],
        out_specs=[pl.BlockSpec((2, tm, tc), lambda j, i: (0, i, j)), pl.BlockSpec((2, 3, tc), lambda j, i: (0, 0, j)),
                   pl.BlockSpec((2, 1, tc), lambda j, i: (0, 0, j))],
        out_shape=[jax.ShapeDtypeStruct((2, S, D_FF), BF16), jax.ShapeDtypeStruct((2, 3, D_FF), F32),
                   jax.ShapeDtypeStruct((2, 1, D_FF), F32)],
        compiler_params=_params("parallel", "arbitrary"),
    )(a, a, u, u, conv_w, conv_w, dz, dz)


ROW_BLOCK_BYTES = 1536 * 1024


def _row_tile(rows, cols):
    return rows if rows * cols * 4 <= ROW_BLOCK_BYTES else _tile(rows, ROW_TILE)


def _adamw_update(w, g, m, v):
    m = ADAM_B1 * m + (1.0 - ADAM_B1) * g
    v = ADAM_B2 * v + (1.0 - ADAM_B2) * jnp.square(g)
    m_hat = m / (1.0 - ADAM_B1 ** ADAM_STEP)
    v_hat = v / (1.0 - ADAM_B2 ** ADAM_STEP)
    return -ADAM_LR * (m_hat / (jnp.sqrt(v_hat) + ADAM_EPS) + ADAM_WD * w), m, v


def _adamw(w, g, m, v, *, name):
    L, A, B = w.shape
    ta = _tile(A, ROW_TILE)

    def body(w_ref, g_ref, m_ref, v_ref, d_ref, mo_ref, vo_ref):
        d_ref[...], mo_ref[...], vo_ref[...] = _adamw_update(w_ref[...], g_ref[...], m_ref[...], v_ref[...])

    blk = pl.BlockSpec((None, ta, B), lambda l, i: (l, i, 0))
    shp = jax.ShapeDtypeStruct((L, A, B), F32)
    return pl.pallas_call(
        body, name=name, grid=(L, A // ta),
        in_specs=[blk] * 4, out_specs=[blk] * 3, out_shape=[shp] * 3,
        compiler_params=_params("parallel", "parallel"),
    )(w, g, m, v)


def _scalar(v):
    return jnp.reshape(v, (1,)).astype(jnp.int32)


def _adamw_halves(w, g_mine, g_other, m, v, *, name):
    L, A, B = w.shape
    ta = _row_tile(A // 2, B)
    nb = A // 2 // ta

    def body(c_ref, w_ref, gm_ref, go_ref, m_ref, v_ref, g_ref, d_ref, mo_ref, vo_ref):
        g = jnp.where(pl.program_id(1) // nb == c_ref[0], gm_ref[...], go_ref[...])
        g_ref[...] = g
        d_ref[...], mo_ref[...], vo_ref[...] = _adamw_update(w_ref[...], g, m_ref[...], v_ref[...])

    blk = pl.BlockSpec((None, ta, B), lambda l, i, c_ref: (l, i, 0))
    half = pl.BlockSpec((None, ta, B), lambda l, i, c_ref: (l, i % nb, 0))
    shp = jax.ShapeDtypeStruct((L, A, B), F32)
    return pl.pallas_call(
        body, name=name,
        grid_spec=pltpu.PrefetchScalarGridSpec(num_scalar_prefetch=1, grid=(L, A // ta),
                                               in_specs=[blk, half, half, blk, blk], out_specs=[blk] * 4),
        out_shape=[shp] * 4,
        compiler_params=_params("parallel", "parallel"),
    )(_scalar(lax.axis_index("c")), w, g_mine, g_other, m, v)


def _chip_index():
    return 2 * lax.axis_index("x") + lax.axis_index("y")


def _pair_sum(g, recv, *, name):
    n, A, B = g.shape
    ta = _row_tile(A // 2, B)
    nb = A // 2 // ta

    def body(c_ref, g_ref, r_ref, o_ref):
        o_ref[...] = g_ref[...] + r_ref[...]

    return pl.pallas_call(
        body, name=name,
        grid_spec=pltpu.PrefetchScalarGridSpec(
            num_scalar_prefetch=1, grid=(n, nb),
            in_specs=[pl.BlockSpec((None, ta, B), lambda s, r, c_ref: (s, c_ref[0] * nb + r, 0)),
                      pl.BlockSpec((None, ta, B), lambda s, r, c_ref: (s, r, 0))],
            out_specs=pl.BlockSpec((None, ta, B), lambda s, r, c_ref: (s, r, 0))),
        out_shape=jax.ShapeDtypeStruct((n, A // 2, B), F32),
        compiler_params=_params("parallel", "parallel"),
    )(_scalar(lax.axis_index("c")), g, recv)


def _chip_sum(landed, own, *, name):
    n, A2, B = landed.shape
    ta = _row_tile(A2, B)

    def body(me_ref, *refs):
        slots, own_ref, o_ref = refs[:n], refs[n], refs[n + 1]
        parts = [jnp.where(me_ref[0] == s, own_ref[...], slots[s][...]) for s in range(n)]
        o_ref[...] = ((parts[0] + parts[1]) + parts[2]) + parts[3]

    def slot(s):
        return pl.BlockSpec((None, ta, B), lambda r, me_ref: (jnp.where(me_ref[0] == s, (s + 1) % n, s), r, 0))

    return pl.pallas_call(
        body, name=name,
        grid_spec=pltpu.PrefetchScalarGridSpec(
            num_scalar_prefetch=1, grid=(A2 // ta,),
            in_specs=[slot(s) for s in range(n)] + [pl.BlockSpec((None, ta, B), lambda r, me_ref: (me_ref[0], r, 0))],
            out_specs=pl.BlockSpec((ta, B), lambda r, me_ref: (r, 0))),
        out_shape=jax.ShapeDtypeStruct((A2, B), F32),
        compiler_params=_params("parallel"),
    )(_scalar(_chip_index()), *([landed] * n), own)


HBM_SPEC = pl.BlockSpec(memory_space=pl.ANY)
COMM_PARAMS = pltpu.CompilerParams(has_side_effects=True)


def _mesh_pos():
    return lax.axis_index("x"), lax.axis_index("y"), lax.axis_index("c")


def _other_chips(x, y):
    return [(1 - x, y), (x, 1 - y), (1 - x, 1 - y)]


def _remote(src, dst, send_sems, recv_sems, k, to):
    return pltpu.make_async_remote_copy(src_ref=src, dst_ref=dst, send_sem=send_sems.at[k], recv_sem=recv_sems.at[k],
                                        device_id=to, device_id_type=MESH)


def _place_own(gathered, shards, *, name):
    n = len(shards)

    def body(me_ref, *refs):
        for s_ref, o_ref in zip(refs[:n], refs[2 * n:]):
            o_ref[...] = s_ref[...]

    return pl.pallas_call(
        body, name=name,
        grid_spec=pltpu.PrefetchScalarGridSpec(
            num_scalar_prefetch=1, grid=(1,),
            in_specs=[pl.BlockSpec(s.shape, lambda i, me_ref: (0, 0)) for s in shards] + [HBM_SPEC] * n,
            out_specs=[pl.BlockSpec((None,) + s.shape, lambda i, me_ref: (me_ref[0], 0, 0)) for s in shards]),
        out_shape=[jax.ShapeDtypeStruct(g.shape, g.dtype) for g in gathered],
        input_output_aliases={1 + n + k: k for k in range(n)},
        compiler_params=_params("arbitrary"),
    )(_scalar(_chip_index()), *shards, *gathered)


def _half_rows(rows, c, align=8):
    assert (rows // 2) % align == 0
    return pl.ds(pl.multiple_of(c * (rows // 2), align), rows // 2)


BF16_ROWS = 16


def _halved(rows):
    return rows % (2 * BF16_ROWS) == 0


def _gather_copies(srcs, lands, send_sems, recv_sems):
    x, y, c = _mesh_pos()
    me = 2 * x + y
    out = []
    for k in range(len(srcs)):
        a = srcs[k].shape[0]
        rows = _half_rows(a, c, BF16_ROWS) if _halved(a) else pl.ds(0, a)
        for j, (px, py) in enumerate(_other_chips(x, y)):
            send = _remote(srcs[k].at[rows], lands[k].at[me, rows], send_sems, recv_sems, 3 * k + j, (px, py, c))
            recv = _remote(srcs[k].at[rows], lands[k].at[2 * px + py, rows], send_sems, recv_sems, 3 * k + j, (px, py, c))
            out.append((send, recv))
    return out


def _gather_start(srcs):
    nu = len(srcs)
    sizes = [len(su) for su in srcs]
    offs = [2 * sum(sizes[:u]) for u in range(nu + 1)]
    lands = [[lax.empty((N_CHIPS,) + s.shape, s.dtype) for s in su] for su in srcs]
    flat = [a for u in range(nu) for a in srcs[u] + lands[u]]

    def body(*refs):
        bufs, sems, token = refs[:len(flat)], refs[len(flat):len(flat) + 2 * nu], refs[-1]
        for u, n in enumerate(sizes):
            mine = bufs[offs[u]:offs[u + 1]]
            for send, _ in _gather_copies(mine[:n], mine[n:], sems[2 * u], sems[2 * u + 1]):
                send.start()
        token[...] = jnp.zeros_like(token)

    res = pl.pallas_call(
        body, name="weight_gather_start",
        in_specs=[HBM_ONLY] * len(flat),
        out_specs=[SEM_SPEC] * (2 * nu) + [HBM_ONLY] * len(flat) + [pl.BlockSpec(memory_space=pltpu.VMEM)],
        out_shape=[pltpu.SemaphoreType.DMA((3 * n,)) for n in sizes for _ in (0, 1)] + [pltpu.HBM(a.shape, a.dtype) for a in flat]
        + [jax.ShapeDtypeStruct((1, 1), F32)],
        input_output_aliases={i: 2 * nu + i for i in range(len(flat))},
        compiler_params=SPLIT_PARAMS,
    )(*[pltpu.with_memory_space_constraint(a, pltpu.HBM) for a in flat])
    bufs = res[2 * nu:2 * nu + len(flat)]
    state = [(res[2 * u], res[2 * u + 1], list(bufs[offs[u]:offs[u] + n]), list(bufs[offs[u] + n:offs[u + 1]]))
             for u, n in enumerate(sizes)]
    return state, res[-1]


def _gather_wait(state, after, *, name):
    send_sems, recv_sems, srcs, lands = state
    n = len(srcs)

    def body(*refs):
        for send, recv in _gather_copies(refs[:n], refs[n:2 * n], refs[2 * n], refs[2 * n + 1]):
            send.wait_send()
            recv.wait_recv()

    res = pl.pallas_call(
        body, name=name,
        in_specs=[HBM_ONLY] * (2 * n) + [SEM_SPEC, SEM_SPEC, HBM_SPEC],
        out_specs=[HBM_ONLY] * (2 * n),
        out_shape=[pltpu.HBM(a.shape, a.dtype) for a in srcs + lands],
        input_output_aliases={i: i for i in range(2 * n)},
        compiler_params=SPLIT_PARAMS,
    )(*srcs, *lands, send_sems, recv_sems, after)
    return list(res[:n]), list(res[n:])


def _gather_forward(lands, *, name):
    n = len(lands)

    def body(*refs):
        bufs, outs = refs[:n], refs[n:2 * n]
        send_sems, recv_sems = refs[2 * n:]
        x, y, c = _mesh_pos()
        copies, waits = [], []
        for k in range(n):
            a = lands[k].shape[1]
            if not _halved(a):
                continue
            for j, (px, py) in enumerate(_other_chips(x, y)):
                mine = 2 * px + py, _half_rows(a, c, BF16_ROWS)
                copies.append(_remote(bufs[k].at[mine], outs[k].at[mine], send_sems, recv_sems, 3 * k + j, (x, y, 1 - c)))
                lands_here = outs[k].at[2 * px + py, _half_rows(a, 1 - c, BF16_ROWS)]
                waits.append(_remote(lands_here, lands_here, send_sems, recv_sems, 3 * k + j, (x, y, 1 - c)))
        for cp in copies:
            cp.start()
        for cp in waits:
            cp.wait_recv()
        for cp in copies:
            cp.wait_send()

    return pl.pallas_call(
        body, name=name,
        in_specs=[HBM_SPEC] * n, out_specs=[HBM_SPEC] * n,
        out_shape=[jax.ShapeDtypeStruct(a.shape, a.dtype) for a in lands],
        scratch_shapes=[pltpu.SemaphoreType.DMA((3 * n,)), pltpu.SemaphoreType.DMA((3 * n,))],
        input_output_aliases={i: i for i in range(n)},
        compiler_params=COMM_PARAMS,
    )(*lands)


def _sibling_exchange(gs, *, name):
    n = len(gs)

    def body(*refs):
        ins, outs = refs[:n], refs[n:2 * n]
        send_sems, recv_sems = refs[2 * n:]
        x, y, c = _mesh_pos()
        copies = [_remote(ins[k].at[:, _half_rows(gs[k].shape[1], 1 - c)], outs[k], send_sems, recv_sems, k, (x, y, 1 - c))
                  for k in range(n)]
        for cp in copies:
            cp.start()
        for cp in copies:
            cp.wait()

    return pl.pallas_call(
        body, name=name,
        in_specs=[HBM_SPEC] * n, out_specs=[HBM_SPEC] * n,
        out_shape=[jax.ShapeDtypeStruct((g.shape[0], g.shape[1] // 2, g.shape[2]), g.dtype) for g in gs],
        scratch_shapes=[pltpu.SemaphoreType.DMA((n,)), pltpu.SemaphoreType.DMA((n,))],
        compiler_params=COMM_PARAMS,
    )(*gs)


HBM_ONLY = pl.BlockSpec(memory_space=pltpu.HBM)
SEM_SPEC = pl.BlockSpec(memory_space=pltpu.SEMAPHORE)
SPLIT_PARAMS = pltpu.CompilerParams(has_side_effects=pltpu.SideEffectType.DATAFLOW_SIDE_EFFECTING)


def _scatter_copies(srcs, lands, send_sems, recv_sems):
    x, y, c = _mesh_pos()
    me = 2 * x + y
    out = []
    for k in range(len(srcs)):
        for j, (px, py) in enumerate(_other_chips(x, y)):
            s = 2 * px + py
            send = _remote(srcs[k].at[s], lands[k].at[me], send_sems, recv_sems, 3 * k + j, (px, py, c))
            recv = _remote(srcs[k].at[s], lands[k].at[s], send_sems, recv_sems, 3 * k + j, (px, py, c))
            out.append((send, recv))
    return out


def _exchange_copies(srcs, lands, send_sems, recv_sems):
    x, y, c = _mesh_pos()
    out = []
    for k in range(len(srcs)):
        cp = _remote(srcs[k].at[:, _half_rows(srcs[k].shape[1], 1 - c)], lands[k], send_sems, recv_sems, k, (x, y, 1 - c))
        out.append((cp, cp))
    return out


def _split_start(srcs, land_shapes, copies, n_sems, *, name):
    n = len(srcs)
    lands = [lax.empty(shape, s.dtype) for shape, s in zip(land_shapes, srcs)]

    def body(*refs):
        ins, zones = refs[:n], refs[n:2 * n]
        send_sems, recv_sems, token = refs[2 * n], refs[2 * n + 1], refs[-1]
        for send, _ in copies(ins, zones, send_sems, recv_sems):
            send.start()
        token[...] = jnp.zeros_like(token)

    hbm = lambda a: pltpu.HBM(a.shape, a.dtype)
    res = pl.pallas_call(
        body, name=name,
        in_specs=[HBM_ONLY] * (2 * n),
        out_specs=[SEM_SPEC, SEM_SPEC] + [HBM_ONLY] * (2 * n) + [pl.BlockSpec(memory_space=pltpu.VMEM)],
        out_shape=[pltpu.SemaphoreType.DMA((n_sems,)), pltpu.SemaphoreType.DMA((n_sems,))] + [hbm(a) for a in srcs + lands]
        + [jax.ShapeDtypeStruct((1, 1), F32)],
        input_output_aliases={i: 2 + i for i in range(2 * n)},
        compiler_params=SPLIT_PARAMS,
    )(*[pltpu.with_memory_space_constraint(a, pltpu.HBM) for a in srcs + lands])
    return (res[0], res[1], list(res[2:2 + n]), list(res[2 + n:2 + 2 * n])), res[-1]


def _scatter_start(ps, *, name):
    return _split_start(ps, [p.shape for p in ps], _scatter_copies, 3 * len(ps), name=name)


def _exchange_start(gs, *, name):
    return _split_start(gs, [(g.shape[0], g.shape[1] // 2, g.shape[2]) for g in gs], _exchange_copies, len(gs), name=name)


def _split_wait(started, copies, after, *, name):
    ng = len(started)
    sizes = [len(st[2]) for st in started]
    offs = [2 * sum(sizes[:i]) for i in range(ng + 1)]
    flat = [a for (_, _, ps, lands) in started for a in ps + lands]

    def body(*refs):
        bufs, sems = refs[:len(flat)], refs[len(flat):len(flat) + 2 * ng]
        for i, n in enumerate(sizes):
            srcs, zones = bufs[offs[i]:offs[i] + n], bufs[offs[i] + n:offs[i + 1]]
            for send, recv in copies(srcs, zones, sems[2 * i], sems[2 * i + 1]):
                send.wait_send()
                recv.wait_recv()

    res = pl.pallas_call(
        body, name=name,
        in_specs=[HBM_ONLY] * len(flat) + [SEM_SPEC] * (2 * ng) + [HBM_SPEC],
        out_specs=[HBM_ONLY] * len(flat),
        out_shape=[pltpu.HBM(a.shape, a.dtype) for a in flat],
        input_output_aliases={i: i for i in range(len(flat))},
        compiler_params=SPLIT_PARAMS,
    )(*flat, *[s for (ss, rs, _, _) in started for s in (ss, rs)], after)
    return [(list(res[offs[i]:offs[i] + n]), list(res[offs[i] + n:offs[i + 1]])) for i, n in enumerate(sizes)]


def _sibling_share(hs):
    n = len(hs)

    def body(*refs):
        ins, outs = refs[:n], refs[n:2 * n]
        send_sems, recv_sems = refs[2 * n:]
        x, y, c = _mesh_pos()
        copies = [_remote(ins[k], outs[k], send_sems, recv_sems, k, (x, y, 1 - c)) for k in range(n)]
        for cp in copies:
            cp.start()
        for cp in copies:
            cp.wait()

    return pl.pallas_call(
        body, name="grad_sibling_share",
        in_specs=[HBM_SPEC] * n, out_specs=[HBM_SPEC] * n,
        out_shape=[jax.ShapeDtypeStruct(h.shape, h.dtype) for h in hs],
        scratch_shapes=[pltpu.SemaphoreType.DMA((n,)), pltpu.SemaphoreType.DMA((n,))],
        compiler_params=COMM_PARAMS,
    )(*hs)


def _allreduce_small(part, by_chip):
    rows, C = part.shape
    rows2 = by_chip.shape[1]

    def body(p_ref, q_ref, o_ref, o2_ref, slots, slots2, send_sems, recv_sems):
        x, y, c = _mesh_pos()
        me = 4 * x + 2 * y + c
        slots[me] = p_ref[...]
        slots2[me] = q_ref[2 * x + y]
        copies = []
        for k in range(1, 8):
            kx, ky, kc = (k >> 2) & 1, (k >> 1) & 1, k & 1
            peer = (x ^ kx if kx else x, y ^ ky if ky else y, c ^ kc if kc else c)
            src = 4 * peer[0] + 2 * peer[1] + peer[2]
            pair = []
            for j, (mine, zone, lands) in enumerate(((p_ref, slots.at[me], slots.at[src]),
                                                     (q_ref.at[2 * peer[0] + peer[1]], slots2.at[me], slots2.at[src]))):
                cp = _remote(mine, zone, send_sems, recv_sems, 2 * (k - 1) + j, peer)
                cp.start()
                pair.append((cp, _remote(mine, lands, send_sems, recv_sems, 2 * (k - 1) + j, peer)))
            copies += pair
        for _, landing in copies:
            landing.wait_recv()
        for cp, _ in copies:
            cp.wait_send()
        total, total2 = slots[0], slots2[0]
        for d in range(1, 8):
            total, total2 = total + slots[d], total2 + slots2[d]
        o_ref[...] = total
        o2_ref[...] = total2

    vmem = pl.BlockSpec(memory_space=pltpu.VMEM)
    return pl.pallas_call(
        body, name="small_grad_allreduce",
        in_specs=[vmem, vmem], out_specs=[vmem, vmem],
        out_shape=[jax.ShapeDtypeStruct((rows, C), F32), jax.ShapeDtypeStruct((rows2, C), F32)],
        scratch_shapes=[pltpu.VMEM((8, rows, C), F32), pltpu.VMEM((8, rows2, C), F32),
                        pltpu.SemaphoreType.DMA((14,)), pltpu.SemaphoreType.DMA((14,))],
        compiler_params=pltpu.CompilerParams(has_side_effects=True, vmem_limit_bytes=VMEM_LIMIT_BYTES),
    )(part, by_chip)


def _pad_w_uq(w):
    lead = w.shape[:-1]
    w = w.reshape(lead + (MLA_HEADS, MLA_QK))
    w = jnp.concatenate([w, jnp.zeros(lead + (MLA_HEADS, MLA_PAD - MLA_QK), w.dtype)], axis=-1)
    return w.reshape(lead + (MLA_HEADS * MLA_PAD,))


def _unpad_w_uq(g):
    lead = g.shape[:-1]
    return g.reshape(lead + (MLA_HEADS, MLA_PAD))[..., :MLA_QK].reshape(lead + (MLA_HEADS * MLA_QK,))


def _t(a):
    return jnp.swapaxes(a, -1, -2)


def _shards_of_cols(w):
    A, NB = w.shape
    return w.reshape(A, N_CHIPS, NB // N_CHIPS).transpose(1, 0, 2)


BIG = ("w_in", "w_uq", "w_ukv", "w_out", "w_up", "w_down")
SMALL = ("attn_pre_norm", "forget_bias", "swa_sinks", "rel_bias", "q_latent_norm", "kv_latent_norm", "group_norm",
         "attn_post_norm", "ffn_pre_norm", "conv_b", "ffn_post_norm")
WEIGHTS = ("attn_pre_norm", "w_in", "forget_bias", "swa_sinks", "rel_bias", "q_latent_norm", "w_uq", "kv_latent_norm",
           "w_ukv", "group_norm", "w_out", "attn_post_norm", "ffn_pre_norm", "w_up", "conv_w", "conv_b", "w_down",
           "ffn_post_norm")


PACK_UNIT = 8 * LANES


def _pack_rows(shape):
    return -(-int(np.prod(shape)) // PACK_UNIT) * 8


def _pack(arrs, row_mult=8):
    parts = []
    for a in arrs:
        n = int(np.prod(a.shape))
        parts.append(jnp.pad(a.reshape(-1), (0, _pack_rows(a.shape) * LANES - n)).reshape(-1, LANES))
    rows = sum(p.shape[0] for p in parts)
    pad = -rows % row_mult
    if pad:
        parts.append(jnp.zeros((pad, LANES), parts[0].dtype))
    return jnp.concatenate(parts, axis=0)


def _unpack(packed, shapes):
    packed = packed.reshape(-1, LANES)
    out, off = [], 0
    for shp in shapes:
        r = _pack_rows(shp)
        out.append(packed[off:off + r].reshape(-1)[:int(np.prod(shp))].reshape(shp))
        off += r
    return out


LAYER_KEYS = ("w_qkv_t", "w_lat_t", "w_in_t", "w_uq_p", "w_uq_t", "w_ukv", "w_ukv_t", "w_out", "w_up", "w_down", "conv_w")


MIX_WEIGHTS = ("w_in", "w_uq", "w_ukv", "w_out")
FFN_WEIGHTS = ("w_up", "w_down", "conv_w")


def _layer_weights(gathered):
    cols = lambda g: g.transpose(1, 0, 2).reshape(g.shape[1], N_CHIPS * g.shape[2])
    out = {}
    if "w_in" in gathered:
        w_in_t = _t(gathered["w_in"]).reshape(IN_COLS, D_MODEL)
        w_in_t = jnp.pad(w_in_t, ((0, IN_ROWS - IN_COLS), (0, 0)))
        w_uq_p = _pad_w_uq(cols(gathered["w_uq"]))
        w_ukv = cols(gathered["w_ukv"])
        out.update(w_qkv_t=w_in_t[:QKV_ROWS], w_lat_t=w_in_t[QKV_ROWS:], w_in_t=w_in_t, w_uq_p=w_uq_p, w_uq_t=_t(w_uq_p),
                   w_ukv=w_ukv, w_ukv_t=_t(w_ukv), w_out=gathered["w_out"].reshape(D_MODEL, D_MODEL))
    if "w_up" in gathered:
        out.update(w_up=gathered["w_up"], w_down=gathered["w_down"].reshape(D_FF, D_MODEL), conv_w=cols(gathered["conv_w"]))
    return out


def _local_step(x, target, W, layer_weights, layer_done):
    W = dict(W, **{key: [None] * DEPTH for key in LAYER_KEYS})
    S = x.shape[0]
    tq_tabs, tm_tabs = _rope_tables(S)
    onehot_t = _rel_onehot_t()
    bias_t = _bias_table(W["rel_bias"].T, onehot_t).reshape(SWA_KV_HEADS, SWA_GROUP, 2 * WINDOW, WINDOW)
    bias_t = bias_t.transpose(0, 2, 1, 3).reshape(SWA_KV_HEADS, 2 * WINDOW, GW)
    row = lambda a: a.reshape(1, -1)
    col = lambda a: a.reshape(-1, 1)
    fox_rows = (FOX_ROW0, FOX_ROW0 + FOX_HEADS * HEAD_DIM, FOX_ROW0 + 2 * FOX_HEADS * HEAD_DIM, SWA_Q_HEADS)
    fox = dict(rows=fox_rows, H=FOX_HEADS, Dk=HEAD_DIM, Dv=HEAD_DIM, scale=HEAD_DIM ** -0.5)
    mla = dict(rows=(0, 0, 0, SWA_Q_HEADS + FOX_HEADS), H=MLA_HEADS, Dk=MLA_PAD, Dv=HEAD_DIM, scale=MLA_SCALE, q_scaled=True)

    saved = []
    h = _rms_fwd(x, row(W["attn_pre_norm"][0]), name="rms_in")
    for l in range(DEPTH):
        sv = {"x0": x, "h1": h}
        for key, val in layer_weights(l, h, False).items():
            W[key][l] = val
        qkv = _matmul(W["w_qkv_t"][l], h, tb=True, out_dtype=BF16, name="proj_qkv")
        lat = _matmul(W["w_lat_t"][l], h, tb=True, name="proj_lat")
        oa, lse_a = _swa_fwd(qkv, bias_t, W["swa_sinks"][l], name="swa_fwd")
        fb_col = jnp.pad(col(W["forget_bias"][l]), ((0, GATE_ROWS - FOX_HEADS), (0, 0)))
        f4 = _gate_fwd(lat, fb_col, name="fox_gate_fwd")[:FOX_HEADS]
        f2 = f4 * LOG2E
        f_row, f_col = f2[:, None, :], f2.T
        of, lse_f = _attn_fwd(qkv, qkv, qkv, f_row=f_row, f_col=f_col, name="fox_fwd", **fox)
        nq, nkv, qm, km, vm = _mla_prep_fwd(lat, col(W["q_latent_norm"][l]), col(W["kv_latent_norm"][l]), W["w_uq_t"][l],
                                            W["w_ukv_t"][l], tq_tabs, tm_tabs, name="mla_prep_fwd")
        oc, lse_c = _attn_fwd(qm, km, vm, name="mla_fwd", **mla)
        mixed = _group_norm_fwd(oa, of, oc, col(W["group_norm"][l]), name="group_norm_fwd")
        y = _matmul(mixed, W["w_out"][l], ta=True, name="proj_out")
        x1, h2 = _resid_rms(x, y, row(W["attn_post_norm"][l]), row(W["ffn_pre_norm"][l]), name="attn_resid")
        for key, val in layer_weights(l, h2, True).items():
            W[key][l] = val
        a = _matmul(h2, W["w_up"][l], b_shards=True, out_dtype=BF16, name="ffn_up")
        u, z = _conv_geglu_fwd(a, W["conv_w"][l], row(W["conv_b"][l]), name="conv_geglu_fwd")
        y2 = _matmul(z, W["w_down"][l], name="ffn_down")
        g_next = row(W["attn_pre_norm"][l + 1]) if l + 1 < DEPTH else None
        x2, h_next = _resid_rms(x1, y2, row(W["ffn_post_norm"][l]), g_next, name="ffn_resid")
        sv.update(qkv=qkv, lat=lat, oa=oa, lse_a=lse_a, fb_col=fb_col, f_row=f_row, f_col=f_col, of=of, lse_f=lse_f,
                  nq=nq, nkv=nkv, qm=qm, km=km, vm=vm, oc=oc, lse_c=lse_c, mixed=mixed, y=y, x1=x1, h2=h2, a=a, u=u, z=z, y2=y2)
        saved.append(sv)
        x, h = x2, h_next

    loss, dx = _loss_head(x, target)

    G = {k: [None] * DEPTH for k in WEIGHTS if k != "rel_bias" and k not in BIG}
    dbias_layers = [None] * DEPTH
    for l in reversed(range(DEPTH)):
        sv = saved[l]
        gb = {}
        if l == DEPTH - 1:
            dy2, dg = _rms_bwd(sv["y2"], row(W["ffn_post_norm"][l]), dx, out_dtype=BF16, name="ffn_post_bwd")
            G["ffn_post_norm"][l] = dg[0]
        dz = _matmul(dy2, W["w_down"][l], tb=True, name="ffn_down_dx")
        gb["w_down"] = _matmul(sv["z"], dy2, ta=True, name="ffn_down_dw").reshape(N_CHIPS, D_FF // N_CHIPS, D_MODEL)
        da, dcw, dcb = _conv_geglu_bwd(sv["a"], sv["u"], W["conv_w"][l], dz, name="conv_geglu_bwd")
        G["conv_w"][l] = dcw.transpose(1, 0, 2).reshape(3, 2 * D_FF)
        G["conv_b"][l] = dcb.reshape(2 * D_FF)
        gb["w_up"] = _matmul(sv["h2"], da, ta=True, out_shards=True, b_halves=True, name="ffn_up_dw")
        token = layer_done(l, gb)
        gb = {}
        dx1, dg, dy, dg_post = _matmul(
            da, W["w_up"][l], tb=True, b_shards=True, a_halves=True, name="ffn_up_dx",
            norm_bwd=(sv["x1"], row(W["ffn_pre_norm"][l]) + token, dx, (sv["y"], row(W["attn_post_norm"][l]))))
        G["ffn_pre_norm"][l] = dg[0]
        G["attn_post_norm"][l] = dg_post[0]
        dmixed = _matmul(W["w_out"][l], dy, tb=True, name="proj_out_dx")
        gb["w_out"] = _matmul(sv["mixed"], dy, name="proj_out_dw").reshape(N_CHIPS, D_MODEL // N_CHIPS, D_MODEL)
        doa, dof, doc, dg, delta = _group_norm_bwd(sv["oa"], sv["of"], sv["oc"], col(W["group_norm"][l]), dmixed,
                                                   name="group_norm_bwd")
        G["group_norm"][l] = dg[:, 0]
        dqa, dkva, dbias_l, dsink = _swa_bwd(sv["qkv"], bias_t, W["swa_sinks"][l], doa, sv["lse_a"],
                                             delta.reshape(-1, S), name="swa_bwd")
        dbias_layers[l] = (dbias_l.reshape(SWA_KV_HEADS, 2 * WINDOW, SWA_GROUP, WINDOW).transpose(0, 2, 1, 3)
                           .reshape(SWA_Q_HEADS, -1))
        G["swa_sinks"][l] = dsink[:, 0]
        dqf, dkf, dvf, dfk = _attn_bwd(sv["qkv"], sv["qkv"], sv["qkv"], do=dof, lse=sv["lse_f"], delta=delta,
                                       f_row=sv["f_row"], f_col=sv["f_col"], name="fox_bwd", **fox)
        dF = jnp.pad(dfk.T, ((0, GATE_ROWS - FOX_HEADS), (0, 0)))
        dflog, dfb = _gate_bwd(sv["lat"], sv["fb_col"], dF, name="fox_gate_bwd")
        G["forget_bias"][l] = dfb[:FOX_HEADS, 0]
        dqm, dkm, dvm = _attn_bwd(sv["qm"], sv["km"], sv["vm"], do=doc, lse=sv["lse_c"], delta=delta, name="mla_bwd", **mla)
        dlat, dwq_t, dwkv_t, dgq, dgkv = _mla_prep_bwd(
            sv["lat"], sv["nq"], sv["nkv"], col(W["q_latent_norm"][l]), col(W["kv_latent_norm"][l]), W["w_uq_p"][l],
            W["w_ukv"][l], tq_tabs, tm_tabs, dqm, dkm, dvm, dflog, name="mla_prep_bwd")
        gb["w_uq"], gb["w_ukv"] = _shards_of_cols(_unpad_w_uq(dwq_t.T)), _shards_of_cols(dwkv_t.T)
        G["q_latent_norm"][l], G["kv_latent_norm"][l] = dgq[:, 0], dgkv[:, 0]
        dproj = _dproj_cast(dqa, dkva, dqf, dkf, dvf, dlat, name="dproj_cast")
        dw_in_t = _matmul(dproj, sv["h1"], name="proj_in_dw")
        gb["w_in"] = _t(dw_in_t[:IN_COLS].reshape(N_CHIPS, IN_COLS // N_CHIPS, D_MODEL))
        token = layer_done(l, gb)
        below = (saved[l - 1]["y2"], row(W["ffn_post_norm"][l - 1])) if l > 0 else None
        res = _matmul(dproj, W["w_in_t"][l], ta=True, name="proj_in_dx",
                      norm_bwd=(sv["x0"], row(W["attn_pre_norm"][l]) + token, dx1, below))
        dx, G["attn_pre_norm"][l] = res[0], res[1][0]
        if l > 0:
            dy2, G["ffn_post_norm"][l - 1] = res[2], res[3][0]

    grads = {k: jnp.stack(v) for k, v in G.items()}
    grads["rel_bias"] = _bias_table_bwd(jnp.stack(dbias_layers), onehot_t).T
    return loss, dx, grads


def kernel(x, attn_pre_norm, w_in, forget_bias, swa_sinks, rel_bias, q_latent_norm, w_uq, kv_latent_norm, w_ukv, group_norm, w_out, attn_post_norm, ffn_pre_norm, w_up, conv_w, conv_b, w_down, ffn_post_norm, loss_target, m_attn_pre_norm, m_w_in, m_forget_bias, m_swa_sinks, m_rel_bias, m_q_latent_norm, m_w_uq, m_kv_latent_norm, m_w_ukv, m_group_norm, m_w_out, m_attn_post_norm, m_ffn_pre_norm, m_w_up, m_conv_w, m_conv_b, m_w_down, m_ffn_post_norm, v_attn_pre_norm, v_w_in, v_forget_bias, v_swa_sinks, v_rel_bias, v_q_latent_norm, v_w_uq, v_kv_latent_norm, v_w_ukv, v_group_norm, v_w_out, v_attn_post_norm, v_ffn_pre_norm, v_w_up, v_conv_w, v_conv_b, v_w_down, v_ffn_post_norm):
    args = dict(locals())
    w = {k: args[k] for k in WEIGHTS}
    m = {k: args["m_" + k] for k in WEIGHTS}
    v = {k: args["v_" + k] for k in WEIGHTS}

    block = lambda l, keys: [w[k][l] if k == "conv_w" else w[k][l].astype(BF16) for k in keys]
    units = [(0, MIX_WEIGHTS), (0, FFN_WEIGHTS)] + [(l, MIX_WEIGHTS + FFN_WEIGHTS) for l in range(1, DEPTH)]
    gather_state, token = _gather_start([block(l, keys) for l, keys in units])
    W = {k: w[k] for k in SMALL}
    W["attn_pre_norm"] = W["attn_pre_norm"] + token

    def layer_weights(l, after, for_ffn):
        if for_ffn and l > 0:
            return {}
        keys = FFN_WEIGHTS if for_ffn else (MIX_WEIGHTS if l == 0 else MIX_WEIGHTS + FFN_WEIGHTS)
        tag = f"{l}_{keys[0]}"
        srcs, lands = _gather_wait(gather_state[units.index((l, keys))], after, name="weight_gather_wait_" + tag)
        lands = _gather_forward(lands, name="weight_gather_forward_" + tag)
        lands = _place_own(lands, srcs, name="place_own_shards")
        return _layer_weights(dict(zip(keys, lands)))

    started, groups, pending = [], [], []

    def to_chips(l, keys, gs, recv, tag):
        pair = [_pair_sum(gk, rk, name="grad_pair_sum") for gk, rk in zip(gs, recv)]
        state, token = _scatter_start(pair, name="grad_scatter_start_" + tag)
        started.append(state)
        groups.append((l, keys))
        return token

    def finish_pending(after):
        l, keys, tag, state = pending.pop()
        gs, recv = _split_wait([state], _exchange_copies, after, name="grad_exchange_wait_" + tag)[0]
        return to_chips(l, keys, gs, recv, tag)

    def layer_done(l, gb):
        keys = [k for k in BIG if k in gb]
        gs = [gb[k] for k in keys]
        tag = f"{l}_{keys[0]}"
        token = finish_pending(gs[0]) if pending else 0.0
        if l == 0:
            return token + to_chips(l, keys, gs, _sibling_exchange(gs, name="grad_sibling_exchange_" + tag), tag)
        state, started_token = _exchange_start(gs, name="grad_exchange_start_" + tag)
        pending.append((l, keys, tag, state))
        return token + started_token

    loss_part, dx, g = _local_step(x[0], loss_target[0], W, layer_weights, layer_done)
    loss = lax.psum(loss_part, ("x", "y", "c"))

    reduced = {}
    for (l, keys), (pair, zones) in zip(groups, _split_wait(started, _scatter_copies, dx, name="grad_scatter_wait")):
        for k, p, z in zip(keys, pair, zones):
            reduced[k, l] = _chip_sum(z, p, name="grad_chip_sum")
    mine = [jnp.stack([reduced[k, l] for l in range(DEPTH)]) for k in BIG]
    other = _sibling_share(mine)
    out_g, out_d, out_m, out_v = {}, {}, {}, {}
    for k, g_mine, g_other in zip(BIG, mine, other):
        out_g[k], out_d[k], out_m[k], out_v[k] = _adamw_halves(w[k], g_mine, g_other, m[k], v[k], name="adamw_" + k)

    small_shapes = [w[k].shape for k in SMALL]
    taps_by_chip = g["conv_w"].reshape(DEPTH, 3, N_CHIPS, FF_SHARD).transpose(2, 0, 1, 3)
    reduced, taps = _allreduce_small(_pack([g[k] for k in SMALL]), jnp.stack([_pack([t]) for t in taps_by_chip]))
    g_small = _unpack(reduced, small_shapes) + _unpack(taps, [w["conv_w"].shape])
    names = SMALL + ("conv_w",)
    shapes = small_shapes + [w["conv_w"].shape]
    packed = lambda arrs: _pack(arrs, ROW_TILE)[None]
    d_s, m_s, v_s = _adamw(packed([w[k] for k in names]), packed(g_small), packed([m[k] for k in names]),
                           packed([v[k] for k in names]), name="adamw_small")
    out_g.update(zip(names, g_small))
    out_d.update(zip(names, _unpack(d_s, shapes)))
    out_m.update(zip(names, _unpack(m_s, shapes)))
    out_v.update(zip(names, _unpack(v_s, shapes)))

    return (loss, dx[None], *[out_g[k] for k in WEIGHTS], *[out_d[k] for k in WEIGHTS],
            *[out_m[k] for k in WEIGHTS], *[out_v[k] for k in WEIGHTS])
```

```python
import math

import numpy as np
import jax
import jax.numpy as jnp
from jax import lax
from jax.experimental import pallas as pl
from jax.experimental.pallas import tpu as pltpu

F32 = jnp.float32
BF16 = jnp.bfloat16

D_MODEL = 1024
DEPTH = 4
HEAD_DIM = 64
SWA_Q_HEADS = 8
SWA_KV_HEADS = 2
SWA_GROUP = SWA_Q_HEADS // SWA_KV_HEADS
WINDOW = 128
FOX_HEADS = 4
MLA_HEADS = 4
MLA_Q_RANK = 256
MLA_KV_RANK = 128
MLA_NOPE = 64
MLA_ROPE = 32
MLA_QK = MLA_NOPE + MLA_ROPE
ROPE_THETA = 10000.0
REL_BUCKETS = 32
REL_MAX_DIST = 128
D_FF = 2816
EPS = 1e-6
NEG_INF = -1e30
LANES = 128
N_CHIPS = 4

IN_COLS = 1956
IN_ROWS = 2048
QKV_ROWS = 1536
LAT_ROWS = IN_ROWS - QKV_ROWS
LAT_SHIFT = FOX_HEADS
FOX_ROW0 = 768
MLA_PAD = LANES
GATE_ROWS = 8

ADAM_LR = 0.001
ADAM_B1 = 0.9
ADAM_B2 = 0.999
ADAM_EPS = 1e-08
ADAM_WD = 0.01
ADAM_STEP = 10

VMEM_LIMIT_BYTES = 48 * 1024 * 1024
ATT_TILE = 512
LOG2E = math.log2(math.e)
MLA_SCALE = MLA_QK ** -0.5
ROW_TILE = 256
MESH = pl.DeviceIdType.MESH

NT = (((1,), (1,)), ((), ()))
TN = (((0,), (0,)), ((), ()))
NN = (((1,), (0,)), ((), ()))


def _params(*sem):
    return pltpu.CompilerParams(dimension_semantics=sem, vmem_limit_bytes=VMEM_LIMIT_BYTES)


def _tile(dim, cap):
    for t in (2816, 2048, 1408, 1024, 512, 256, 128, 64, 32, 16, 8):
        if t <= cap and dim % t == 0:
            return t
    return dim


def _dot(a, b, dims=NN):
    return lax.dot_general(a, b, dims, preferred_element_type=F32)


def _split3(a):
    a1 = a.astype(BF16)
    r1 = a - a1.astype(F32)
    a2 = r1.astype(BF16)
    a3 = (r1 - a2.astype(F32)).astype(BF16)
    return a1, a2, a3


FF_SHARD = 2 * D_FF // N_CHIPS
MATMUL_VMEM_BYTES = 40 * 1024 * 1024
NORM_BWD_ROWS = 512
NORM_BWD_VMEM_LIMIT_BYTES = 56 * 1024 * 1024


def _matmul(a, b, *, ta=False, tb=False, out_dtype=F32, name, b_shards=False, out_shards=False, a_halves=False,
            b_halves=False, norm_bwd=None, resid_rms=None):
    if a_halves:
        M, K = a.shape[1], 2 * a.shape[2]
    elif ta:
        K, M = a.shape
    else:
        M, K = a.shape
    if b_halves:
        K2, N = b.shape[1], 2 * b.shape[2]
    elif b_shards:
        K2, N = (2 * D_FF, D_MODEL) if tb else (D_MODEL, 2 * D_FF)
    elif tb:
        N, K2 = b.shape
    else:
        K2, N = b.shape
    assert K == K2, (a.shape, b.shape)
    tn = _tile(N, 1408)
    tk = FF_SHARD if (b_shards and tb) else _tile(K, 2816)
    out_bytes = jnp.dtype(out_dtype).itemsize
    with_tail = norm_bwd is not None or resid_rms is not None
    tile_bytes = 4 + (2 * (4 * 4 + 2) if with_tail else 2 * out_bytes)
    vmem = lambda tm, tk: 2 * 2 * tk * (tm + tn) + tile_bytes * tm * tn
    tm = M if M <= 2048 else _tile(M, 1408)
    if M > 2048 and M % 2048 == 0 and tk == K and vmem(2048, tk) <= MATMUL_VMEM_BYTES:
        tm = 2048
    if with_tail:
        assert tn == N and not out_shards and (norm_bwd is None or resid_rms is None)
        tm = NORM_BWD_ROWS
    while vmem(tm, tk) > MATMUL_VMEM_BYTES and tk % 256 == 0:
        tk //= 2
    nk = K // tk
    dims = (((0 if ta else 1,), (1 if tb else 0,)), ((), ()))
    if with_tail:
        if norm_bwd is not None:
            x, g, resid, then = norm_bwd
            chained = then is not None
            tail_in, in_kinds = [x, g, resid] + (list(then) if chained else []), "rvr" + ("rv" if chained else "")
            out_kinds, out_dtypes = "rv" + ("rv" if chained else ""), [F32, F32] + ([BF16, F32] if chained else [])

            def tail(dy, ins, outs):
                dx, dg = _seg_rms_bwd(ins[0][...], ins[1][...], dy)
                dx = dx + ins[2][...]
                outs[0][...] = dx
                outs[1][...] += dg
                if chained:
                    dx2, dg2 = _seg_rms_bwd(ins[3][...], ins[4][...], dx)
                    outs[2][...] = dx2.astype(BF16)
                    outs[3][...] += dg2
        else:
            x, g_post, g_next = resid_rms
            with_next = g_next is not None
            tail_in, in_kinds = [x, g_post] + ([g_next] if with_next else []), "rv" + ("v" if with_next else "")
            out_kinds, out_dtypes = "rr" + ("r" if with_next else ""), [F32, F32] + ([BF16] if with_next else [])

            def tail(y, ins, outs):
                outs[0][...] = y
                xn = ins[0][...] + _seg_rms(y, ins[1][...])
                outs[1][...] = xn
                if with_next:
                    outs[2][...] = _seg_rms(xn, ins[2][...]).astype(BF16)

        def fused(a_ref, b_ref, *refs):
            ins, outs, acc = refs[:len(tail_in)], refs[len(tail_in):-1], refs[-1]
            k, i = pl.program_id(0), pl.program_id(1)
            acc_ref = acc.at[pl.ds(pl.multiple_of(i * tm, tm), tm), :] if nk > 1 else acc

            @pl.when(k == 0)
            def _():
                acc_ref[...] = jnp.zeros((tm, N), F32)

            acc_ref[...] += lax.dot_general(a_ref[...], b_ref[...], dims, preferred_element_type=F32)

            @pl.when((k == nk - 1) & (i == 0))
            def _():
                for o, kind in zip(outs, out_kinds):
                    if kind == "v":
                        o[...] = jnp.zeros_like(o)

            @pl.when(k == nk - 1)
            def _():
                tail(acc_ref[...], ins, outs)

    def body(a_ref, b_ref, o_ref, acc_ref):
        k = pl.program_id(2)

        @pl.when(k == 0)
        def _():
            acc_ref[...] = jnp.zeros_like(acc_ref)

        acc_ref[...] += lax.dot_general(a_ref[...], b_ref[...], dims, preferred_element_type=F32)

        @pl.when(k == nk - 1)
        def _():
            o_ref[...] = acc_ref[...].astype(o_ref.dtype)

    if a_halves:
        nh = K // 2 // tk
        a_spec = pl.BlockSpec((None, tm, tk), lambda i, j, k: (k // nh, i, k % nh))
    else:
        a_spec = pl.BlockSpec((tk, tm), lambda i, j, k: (k, i)) if ta else pl.BlockSpec((tm, tk), lambda i, j, k: (i, k))
    if b_halves:
        nh = N // 2 // tn
        b_spec = pl.BlockSpec((None, tk, tn), lambda i, j, k: (j // nh, k, j % nh))
    elif b_shards and tb:
        assert tk == FF_SHARD
        b_spec = pl.BlockSpec((None, tn, tk), lambda i, j, k: (k, j, 0))
    elif b_shards:
        assert tn == FF_SHARD
        b_spec = pl.BlockSpec((None, tk, tn), lambda i, j, k: (j, k, 0))
    else:
        b_spec = pl.BlockSpec((tn, tk), lambda i, j, k: (j, k)) if tb else pl.BlockSpec((tk, tn), lambda i, j, k: (k, j))
    if out_shards:
        assert tn == FF_SHARD
        out_spec = pl.BlockSpec((None, tm, tn), lambda i, j, k: (j, i, 0))
        out_shape = jax.ShapeDtypeStruct((N // tn, M, tn), out_dtype)
    else:
        out_spec = pl.BlockSpec((tm, tn), lambda i, j, k: (i, j))
        out_shape = jax.ShapeDtypeStruct((M, N), out_dtype)
    if with_tail:
        spec = {"r": pl.BlockSpec((tm, N), lambda k, i: (jnp.where(k == nk - 1, i, 0), 0)),
                "v": pl.BlockSpec((1, N), lambda k, i: (0, 0))}
        a_map, b_map = a_spec.index_map, b_spec.index_map
        return pl.pallas_call(
            fused, name=name, grid=(nk, M // tm),
            in_specs=[pl.BlockSpec(a_spec.block_shape, lambda k, i: a_map(i, 0, k)),
                      pl.BlockSpec(b_spec.block_shape, lambda k, i: b_map(i, 0, k))] + [spec[c] for c in in_kinds],
            out_specs=[spec[c] for c in out_kinds],
            out_shape=[jax.ShapeDtypeStruct((M if c == "r" else 1, N), d) for c, d in zip(out_kinds, out_dtypes)],
            scratch_shapes=[pltpu.VMEM((M if nk > 1 else tm, N), F32)],
            compiler_params=pltpu.CompilerParams(dimension_semantics=("arbitrary", "arbitrary"),
                                                 vmem_limit_bytes=NORM_BWD_VMEM_LIMIT_BYTES),
        )(a, b, *tail_in)
    return pl.pallas_call(
        body, name=name, grid=(M // tm, N // tn, nk),
        in_specs=[a_spec, b_spec], out_specs=out_spec, out_shape=out_shape,
        scratch_shapes=[pltpu.VMEM((tm, tn), F32)],
        compiler_params=_params("parallel", "parallel", "arbitrary"),
    )(a, b)


def _seg_rms(xs, g):
    r = lax.rsqrt(jnp.mean(xs * xs, axis=-1, keepdims=True) + EPS)
    return xs * r * g


def _seg_rms_bwd(xs, g, dy):
    r = lax.rsqrt(jnp.mean(xs * xs, axis=-1, keepdims=True) + EPS)
    gd = dy * g
    c = jnp.mean(gd * xs, axis=-1, keepdims=True)
    dx = r * gd - xs * (r * r * r * c)
    dg = jnp.sum(dy * (xs * r), axis=0, keepdims=True)
    return dx, dg


def _rms_fwd(x, g, *, name):
    S, W = x.shape
    tm = _tile(S, 512)

    def body(x_ref, g_ref, o_ref):
        o_ref[...] = _seg_rms(x_ref[...], g_ref[...]).astype(o_ref.dtype)

    return pl.pallas_call(
        body, name=name, grid=(S // tm,),
        in_specs=[pl.BlockSpec((tm, W), lambda i: (i, 0)), pl.BlockSpec((1, W), lambda i: (0, 0))],
        out_specs=pl.BlockSpec((tm, W), lambda i: (i, 0)),
        out_shape=jax.ShapeDtypeStruct((S, W), BF16),
        compiler_params=_params("parallel"),
    )(x, g)


def _rms_bwd(x, g, dy, *, resid=None, out_dtype, name, then=None):
    S, W = x.shape
    tm = _tile(S, 512)
    has_resid = resid is not None
    chained = then is not None

    def body(*refs):
        refs = list(refs)
        x_ref, g_ref, dy_ref = refs[:3]
        r_ref = refs[3] if has_resid else None
        n_in = 3 + has_resid + 2 * chained
        x2_ref, g2_ref = (refs[n_in - 2], refs[n_in - 1]) if chained else (None, None)
        outs = refs[n_in:]
        dx_ref, dg_ref = outs[0], outs[1]

        @pl.when(pl.program_id(0) == 0)
        def _():
            dg_ref[...] = jnp.zeros_like(dg_ref)
            if chained:
                outs[3][...] = jnp.zeros_like(outs[3])

        dx, dg = _seg_rms_bwd(x_ref[...], g_ref[...], dy_ref[...])
        if has_resid:
            dx = dx + r_ref[...]
        dx_ref[...] = dx.astype(dx_ref.dtype)
        dg_ref[...] += dg
        if chained:
            dx2, dg2 = _seg_rms_bwd(x2_ref[...], g2_ref[...], dx)
            outs[2][...] = dx2.astype(BF16)
            outs[3][...] += dg2

    row = pl.BlockSpec((tm, W), lambda i: (i, 0))
    vec = pl.BlockSpec((1, W), lambda i: (0, 0))
    ins = [x, g, dy] + ([resid] if has_resid else []) + (list(then) if chained else [])
    return pl.pallas_call(
        body, name=name, grid=(S // tm,),
        in_specs=[row, vec, row] + ([row] if has_resid else []) + ([row, vec] if chained else []),
        out_specs=[row, vec] + ([row, vec] if chained else []),
        out_shape=[jax.ShapeDtypeStruct((S, W), out_dtype), jax.ShapeDtypeStruct((1, W), F32)]
        + ([jax.ShapeDtypeStruct((S, W), BF16), jax.ShapeDtypeStruct((1, W), F32)] if chained else []),
        compiler_params=_params("arbitrary"),
    )(*ins)


def _resid_rms(x, y, g_post, g_next, *, name):
    S, W = x.shape
    tm = _tile(S, 512)
    with_next = g_next is not None

    def body(*refs):
        if with_next:
            x_ref, y_ref, gp_ref, gn_ref, xo_ref, h_ref = refs
        else:
            x_ref, y_ref, gp_ref, xo_ref = refs
        xn = x_ref[...] + _seg_rms(y_ref[...], gp_ref[...])
        xo_ref[...] = xn
        if with_next:
            h_ref[...] = _seg_rms(xn, gn_ref[...]).astype(BF16)

    row = pl.BlockSpec((tm, W), lambda i: (i, 0))
    vec = pl.BlockSpec((1, W), lambda i: (0, 0))
    outs = [jax.ShapeDtypeStruct((S, W), F32)] + ([jax.ShapeDtypeStruct((S, W), BF16)] if with_next else [])
    res = pl.pallas_call(
        body, name=name, grid=(S // tm,),
        in_specs=[row, row, vec] + ([vec] if with_next else []),
        out_specs=[row] + ([row] if with_next else []),
        out_shape=outs,
        compiler_params=_params("parallel"),
    )(*([x, y, g_post] + ([g_next] if with_next else [])))
    return (res[0], res[1]) if with_next else (res[0], None)


def _col_rms(xs, g):
    r = lax.rsqrt(jnp.mean(xs * xs, axis=0, keepdims=True) + EPS)
    return xs * r * g


def _col_rms_bwd(xs, g, dy):
    r = lax.rsqrt(jnp.mean(xs * xs, axis=0, keepdims=True) + EPS)
    gd = dy * g
    c = jnp.mean(gd * xs, axis=0, keepdims=True)
    dx = r * gd - xs * (r * r * r * c)
    dg = jnp.sum(dy * (xs * r), axis=1, keepdims=True)
    return dx, dg


GROUP_ROWS = (SWA_Q_HEADS * HEAD_DIM, FOX_HEADS * HEAD_DIM, MLA_HEADS * HEAD_DIM)


def _group_specs(S, tn):
    outs = [pl.BlockSpec((n, tn), lambda i: (0, i)) for n in GROUP_ROWS]
    g = pl.BlockSpec((D_MODEL, 1), lambda i: (0, 0))
    mixed = pl.BlockSpec((D_MODEL, tn), lambda i: (0, i))
    return outs, g, mixed


def _group_norm_fwd(oa, of, oc, g, *, name):
    S = oa.shape[1]
    tn = _tile(S, 512)
    outs, gs, mixed = _group_specs(S, tn)

    def body(a_ref, f_ref, c_ref, g_ref, o_ref):
        r0 = 0
        for ref, n in zip((a_ref, f_ref, c_ref), GROUP_ROWS):
            o_ref[r0:r0 + n, :] = _col_rms(ref[...], g_ref[r0:r0 + n, :]).astype(BF16)
            r0 += n

    return pl.pallas_call(
        body, name=name, grid=(S // tn,),
        in_specs=outs + [gs], out_specs=mixed,
        out_shape=jax.ShapeDtypeStruct((D_MODEL, S), BF16),
        compiler_params=_params("parallel"),
    )(oa, of, oc, g)


def _group_norm_bwd(oa, of, oc, g, dmixed, *, name):
    S = oa.shape[1]
    tn = _tile(S, 512)
    outs, gs, mixed = _group_specs(S, tn)
    n_heads = D_MODEL // HEAD_DIM

    def body(a_ref, f_ref, c_ref, g_ref, dm_ref, da_ref, df_ref, dc_ref, dg_ref, dl_ref):
        @pl.when(pl.program_id(0) == 0)
        def _():
            dg_ref[...] = jnp.zeros_like(dg_ref)

        r0 = 0
        for ref, dref, n in zip((a_ref, f_ref, c_ref), (da_ref, df_ref, dc_ref), GROUP_ROWS):
            o = ref[...]
            dx, dg = _col_rms_bwd(o, g_ref[r0:r0 + n, :], dm_ref[r0:r0 + n, :])
            dxb = dx.astype(BF16)
            dref[...] = dxb
            dg_ref[r0:r0 + n, :] += dg
            od = o * dxb.astype(F32)
            for h in range(n // HEAD_DIM):
                dl_ref[r0 // HEAD_DIM + h] = jnp.sum(od[h * HEAD_DIM:(h + 1) * HEAD_DIM, :], axis=0, keepdims=True)
            r0 += n

    return pl.pallas_call(
        body, name=name, grid=(S // tn,),
        in_specs=outs + [gs, mixed], out_specs=outs + [gs, pl.BlockSpec((n_heads, 1, tn), lambda i: (0, 0, i))],
        out_shape=[jax.ShapeDtypeStruct((n, S), BF16) for n in GROUP_ROWS] + [jax.ShapeDtypeStruct((D_MODEL, 1), F32),
                                                                              jax.ShapeDtypeStruct((n_heads, 1, S), F32)],
        compiler_params=_params("arbitrary"),
    )(oa, of, oc, g, dmixed)


def _loss_head(y, target):
    S, W = y.shape
    tm = _tile(S, 512)

    def body(y_ref, t_ref, d_ref, l_ref):
        @pl.when(pl.program_id(0) == 0)
        def _():
            l_ref[...] = jnp.zeros_like(l_ref)

        err = y_ref[...] - t_ref[...]
        d_ref[...] = err * (1.0 / W)
        l_ref[...] += 0.5 * jnp.sum(jnp.mean(err * err, axis=-1, keepdims=True), axis=0, keepdims=True)

    row = pl.BlockSpec((tm, W), lambda i: (i, 0))
    d, l = pl.pallas_call(
        body, name="loss_head", grid=(S // tm,),
        in_specs=[row, row],
        out_specs=[row, pl.BlockSpec((1, 1), lambda i: (0, 0))],
        out_shape=[jax.ShapeDtypeStruct((S, W), F32), jax.ShapeDtypeStruct((1, 1), F32)],
        compiler_params=_params("arbitrary"),
    )(y, target)
    return l[0, 0], d


def _attn_fwd(q_src, k_src, v_src, rows, H, Dk, Dv, scale, f_row=None, f_col=None, *, name, q_scaled=False):
    S = q_src.shape[1]
    T = _tile(S, ATT_TILE)
    nq = S // T
    forget = f_row is not None
    qb, kb, vb = rows[0] // (H * Dk), rows[1] // (H * Dk), rows[2] // (H * Dv)
    hs = range(H)

    def body(*refs):
        if forget:
            q_ref, k_ref, v_ref, fq_ref, fk_ref, o_ref, lse_ref = refs
        else:
            q_ref, k_ref, v_ref, o_ref, lse_ref = refs
        i = pl.program_id(0)

        def tile(j, masked, state):
            off = pl.multiple_of(j * T, T)
            ss = [_dot(k_ref[h * Dk:(h + 1) * Dk, pl.ds(off, T)], q_ref[h * Dk:(h + 1) * Dk, :], TN) for h in hs]
            if not q_scaled:
                ss = [s * (scale * LOG2E) for s in ss]
            if forget:
                ss = [ss[h] + (fq_ref[h] - fk_ref[pl.ds(off, T), h:h + 1]) for h in hs]
            if masked:
                r = lax.broadcasted_iota(jnp.int32, (T, T), 0)
                c = lax.broadcasted_iota(jnp.int32, (T, T), 1)
                ss = [jnp.where(r <= c, s, NEG_INF) for s in ss]
            m_new = [jnp.maximum(state[h][0], jnp.max(ss[h], axis=0, keepdims=True)) for h in hs]
            alpha = [jnp.exp2(state[h][0] - m_new[h]) for h in hs]
            ps = [jnp.exp2(ss[h] - m_new[h]) for h in hs]
            l_new = [alpha[h] * state[h][1] + jnp.sum(ps[h], axis=0, keepdims=True) for h in hs]
            p_hi = [p.astype(BF16) for p in ps]
            vs = [v_ref[h * Dv:(h + 1) * Dv, pl.ds(off, T)] for h in hs]
            pv = [_dot(vs[h], p_hi[h]) for h in hs]
            if forget:
                pv = [pv[h] + _dot(vs[h], (ps[h] - p_hi[h].astype(F32)).astype(BF16)) for h in hs]
            return tuple((m_new[h], l_new[h], alpha[h] * state[h][2] + pv[h]) for h in hs)

        init = tuple((jnp.full((1, T), NEG_INF, F32), jnp.zeros((1, T), F32), jnp.zeros((Dv, T), F32)) for _ in hs)
        state = lax.fori_loop(0, i, lambda j, st: tile(j, False, st), init)
        state = tile(i, True, state)
        for h in hs:
            m, l, acc = state[h]
            o_ref[h * Dv:(h + 1) * Dv, :] = acc / l
            lse_ref[h] = m + jnp.log2(l)

    in_specs = [pl.BlockSpec((H * Dk, T), lambda i: (qb, i)),
                pl.BlockSpec((H * Dk, S), lambda i: (kb, 0)),
                pl.BlockSpec((H * Dv, S), lambda i: (vb, 0))]
    ins = [q_src, k_src, v_src]
    if forget:
        in_specs += [pl.BlockSpec((H, 1, T), lambda i: (0, 0, i)), pl.BlockSpec((S, H), lambda i: (0, 0))]
        ins += [f_row, f_col]
    return pl.pallas_call(
        body, name=name, grid=(nq,),
        in_specs=in_specs,
        out_specs=[pl.BlockSpec((H * Dv, T), lambda i: (0, i)), pl.BlockSpec((H, 1, T), lambda i: (0, 0, i))],
        out_shape=[jax.ShapeDtypeStruct((H * Dv, S), F32), jax.ShapeDtypeStruct((H, 1, S), F32)],
        compiler_params=_params("parallel"),
    )(*ins)


def _attn_bwd(q_src, k_src, v_src, rows, H, Dk, Dv, scale, do, lse, delta, f_row=None, f_col=None, *, name, q_scaled=False):
    S = q_src.shape[1]
    T = _tile(S, ATT_TILE)
    nq = S // T
    forget = f_row is not None
    qb, kb, vb, db = rows[0] // (H * Dk), rows[1] // (H * Dk), rows[2] // (H * Dv), rows[3] // H
    hs = range(H)

    def body(*refs):
        if forget:
            (q_ref, k_ref, v_ref, do_ref, lse_ref, dl_ref, fq_ref, fk_ref,
             dq_ref, dk_ref, dv_ref, df_ref, dk_s, dv_s, df_s) = refs
        else:
            q_ref, k_ref, v_ref, do_ref, lse_ref, dl_ref, dq_ref, dk_ref, dv_ref, dk_s, dv_s = refs
        j = pl.program_id(0)

        @pl.when(j == 0)
        def _():
            dq_ref[...] = jnp.zeros_like(dq_ref)

        dk_s[...] = jnp.zeros_like(dk_s)
        dv_s[...] = jnp.zeros_like(dv_s)
        if forget:
            df_s[...] = jnp.zeros_like(df_s)
        kt = [k_ref[h * Dk:(h + 1) * Dk, :] for h in hs]
        kj = [k.T for k in kt]
        vj = [v_ref[h * Dv:(h + 1) * Dv, :].T for h in hs]
        koff = pl.multiple_of(j * T, T)

        def tile(i, masked):
            cols = pl.ds(pl.multiple_of(i * T, T), T)
            qi = [q_ref[h * Dk:(h + 1) * Dk, cols] for h in hs]
            doi = [do_ref[h * Dv:(h + 1) * Dv, cols] for h in hs]
            st = [_dot(kj[h], qi[h]) for h in hs]
            if not q_scaled:
                st = [x * (scale * LOG2E) for x in st]
            if forget:
                st = [st[h] + (fq_ref[h, :, cols] - fk_ref[pl.ds(koff, T), h:h + 1]) for h in hs]
            if masked:
                r = lax.broadcasted_iota(jnp.int32, (T, T), 0)
                c = lax.broadcasted_iota(jnp.int32, (T, T), 1)
                st = [jnp.where(r <= c, x, NEG_INF) for x in st]
            pt = [jnp.exp2(st[h] - lse_ref[h, :, cols]) for h in hs]
            dpt = [_dot(vj[h], doi[h]) for h in hs]
            dst = [pt[h] * (dpt[h] - dl_ref[h, :, cols]) for h in hs]
            ptb = [p.astype(BF16) for p in pt]
            dsb = [d.astype(BF16) for d in dst]
            for h in hs:
                dv_s[h * Dv:(h + 1) * Dv, :] += _dot(doi[h], ptb[h], NT)
            for h in hs:
                dk_s[h * Dk:(h + 1) * Dk, :] += _dot(qi[h], dsb[h], NT)
            for h in hs:
                dq_ref[h * Dk:(h + 1) * Dk, cols] += _dot(kt[h], dsb[h]) * scale
            if forget:
                for h in hs:
                    part = dst[h][:, 0:LANES]
                    for c0 in range(LANES, T, LANES):
                        part = part + dst[h][:, c0:c0 + LANES]
                    df_s[h] += part

        tile(j, True)

        def loop_body(i, carry):
            tile(i, False)
            return carry

        lax.fori_loop(j + 1, nq, loop_body, 0)
        dk_ref[...] = dk_s[...] * ((1.0 / LOG2E) if q_scaled else scale)
        dv_ref[...] = dv_s[...]
        if forget:
            df_ref[...] = jnp.concatenate([-jnp.sum(df_s[h], axis=-1, keepdims=True) for h in hs], axis=1)

    res = lambda D, b0: pl.BlockSpec((H * D, S), lambda j: (b0, 0))
    blk = lambda D, b0: pl.BlockSpec((H * D, T), lambda j: (b0, j))
    row3 = lambda b0: pl.BlockSpec((H, 1, S), lambda j: (b0, 0, 0))
    in_specs = [res(Dk, qb), blk(Dk, kb), blk(Dv, vb), res(Dv, 0), row3(0), row3(db)]
    ins = [q_src, k_src, v_src, do, lse, delta]
    out_specs = [res(Dk, 0), blk(Dk, 0), blk(Dv, 0)]
    out_shape = [jax.ShapeDtypeStruct((H * Dk, S), F32), jax.ShapeDtypeStruct((H * Dk, S), F32),
                 jax.ShapeDtypeStruct((H * Dv, S), F32)]
    scratch = [pltpu.VMEM((H * Dk, T), F32), pltpu.VMEM((H * Dv, T), F32)]
    if forget:
        in_specs += [row3(0), pl.BlockSpec((S, H), lambda j: (0, 0))]
        ins += [f_row, f_col]
        out_specs.append(pl.BlockSpec((T, H), lambda j: (j, 0)))
        out_shape.append(jax.ShapeDtypeStruct((S, H), F32))
        scratch.append(pltpu.VMEM((H, T, min(T, LANES)), F32))
    return pl.pallas_call(
        body, name=name, grid=(nq,),
        in_specs=in_specs, out_specs=out_specs, out_shape=out_shape, scratch_shapes=scratch,
        compiler_params=_params("arbitrary"),
    )(*ins)


GW = SWA_GROUP * WINDOW


def _swa_masks(i):
    r = lax.broadcasted_iota(jnp.int32, (WINDOW, GW), 0)
    c = lax.broadcasted_iota(jnp.int32, (WINDOW, GW), 1) % WINDOW
    return (r > c) & (i > 0), r <= c


def _swa_specs():
    W = WINDOW
    kv_rows = SWA_KV_HEADS * HEAD_DIM
    q = pl.BlockSpec((SWA_Q_HEADS * HEAD_DIM, W), lambda i: (0, i))
    prev = lambda b: pl.BlockSpec((kv_rows, W), lambda i: (b, jnp.maximum(i - 1, 0)))
    cur = lambda b: pl.BlockSpec((kv_rows, W), lambda i: (b, i))
    bias = pl.BlockSpec((SWA_KV_HEADS, 2 * W, GW), lambda i: (0, 0, 0))
    stat = pl.BlockSpec((SWA_Q_HEADS, W), lambda i: (0, i))
    sink = pl.BlockSpec(memory_space=pltpu.SMEM)
    return q, prev(4), cur(4), prev(5), cur(5), bias, stat, sink


def _group_lanes(ref, g, rows_per_head):
    h0 = g * SWA_GROUP
    return jnp.concatenate([ref[(h0 + j) * rows_per_head:(h0 + j + 1) * rows_per_head, :] for j in range(SWA_GROUP)], axis=1)


def _swa_scores(g, q_ref, kp_ref, kc_ref, b_ref, masks):
    rows = slice(g * HEAD_DIM, (g + 1) * HEAD_DIM)
    qg = _group_lanes(q_ref, g, HEAD_DIM)
    scale = HEAD_DIM ** -0.5
    s_p = jnp.where(masks[0], _dot(kp_ref[rows, :], qg, TN) * scale + b_ref[g, 0:WINDOW, :], NEG_INF)
    s_c = jnp.where(masks[1], _dot(kc_ref[rows, :], qg, TN) * scale + b_ref[g, WINDOW:2 * WINDOW, :], NEG_INF)
    return qg, rows, s_p, s_c


def _sink_row(sink_ref, g):
    return jnp.concatenate([jnp.full((1, WINDOW), sink_ref[g * SWA_GROUP + j], F32) for j in range(SWA_GROUP)], axis=1)


def _swa_fwd(qkv, bias_g, sinks, *, name):
    S = qkv.shape[1]
    qs, kp, kc, vp, vc, bs, stat, sk = _swa_specs()
    gs = range(SWA_KV_HEADS)

    def body(sink_ref, q_ref, kp_ref, kc_ref, vp_ref, vc_ref, b_ref, o_ref, lse_ref):
        masks = _swa_masks(pl.program_id(0))
        sc = [_swa_scores(g, q_ref, kp_ref, kc_ref, b_ref, masks) for g in gs]
        sinks_g = [_sink_row(sink_ref, g) for g in gs]
        m = [jnp.maximum(jnp.maximum(jnp.max(sc[g][2], axis=0, keepdims=True), jnp.max(sc[g][3], axis=0, keepdims=True)),
                         sinks_g[g]) for g in gs]
        p_p = [jnp.exp(sc[g][2] - m[g]) for g in gs]
        p_c = [jnp.exp(sc[g][3] - m[g]) for g in gs]
        l = [jnp.sum(p_p[g], axis=0, keepdims=True) + jnp.sum(p_c[g], axis=0, keepdims=True) + jnp.exp(sinks_g[g] - m[g])
             for g in gs]
        o = [_dot(vp_ref[sc[g][1], :], p_p[g].astype(BF16)) + _dot(vc_ref[sc[g][1], :], p_c[g].astype(BF16)) for g in gs]
        for g in gs:
            og = o[g] / l[g]
            lse = m[g] + jnp.log(l[g])
            for j in range(SWA_GROUP):
                h = g * SWA_GROUP + j
                o_ref[h * HEAD_DIM:(h + 1) * HEAD_DIM, :] = og[:, j * WINDOW:(j + 1) * WINDOW]
                lse_ref[h:h + 1, :] = lse[:, j * WINDOW:(j + 1) * WINDOW]

    return pl.pallas_call(
        body, name=name, grid=(S // WINDOW,),
        in_specs=[sk, qs, kp, kc, vp, vc, bs],
        out_specs=[qs, stat],
        out_shape=[jax.ShapeDtypeStruct((SWA_Q_HEADS * HEAD_DIM, S), F32), jax.ShapeDtypeStruct((SWA_Q_HEADS, S), F32)],
        compiler_params=_params("parallel"),
    )(sinks, qkv, qkv, qkv, qkv, qkv, bias_g)


def _swa_bwd(qkv, bias_g, sinks, do, lse, delta, *, name):
    S = qkv.shape[1]
    W = WINDOW
    qs, kp, kc, vp, vc, bs, stat, sk = _swa_specs()
    scale = HEAD_DIM ** -0.5
    kv_rows = SWA_KV_HEADS * HEAD_DIM
    gs = range(SWA_KV_HEADS)

    def body(sink_ref, q_ref, kp_ref, kc_ref, vp_ref, vc_ref, b_ref, do_ref, lse_ref, dl_ref,
             dq_ref, dkv_ref, db_ref, dsk_ref):
        i = pl.program_id(0)

        @pl.when(i == 0)
        def _():
            dkv_ref[...] = jnp.zeros_like(dkv_ref)
            db_ref[...] = jnp.zeros_like(db_ref)
            dsk_ref[...] = jnp.zeros_like(dsk_ref)

        masks = _swa_masks(i)
        prev = pl.ds(pl.multiple_of(jnp.maximum(i - 1, 0) * W, W), W)
        cur = pl.ds(pl.multiple_of(i * W, W), W)
        sc = [_swa_scores(g, q_ref, kp_ref, kc_ref, b_ref, masks) for g in gs]
        dog = [_group_lanes(do_ref, g, HEAD_DIM) for g in gs]
        lse = [_group_lanes(lse_ref, g, 1) for g in gs]
        dl = [_group_lanes(dl_ref, g, 1) for g in gs]
        p_p = [jnp.exp(sc[g][2] - lse[g]) for g in gs]
        p_c = [jnp.exp(sc[g][3] - lse[g]) for g in gs]
        ds_p = [p_p[g] * (_dot(vp_ref[sc[g][1], :], dog[g], TN) - dl[g]) for g in gs]
        ds_c = [p_c[g] * (_dot(vc_ref[sc[g][1], :], dog[g], TN) - dl[g]) for g in gs]
        for g in gs:
            db_ref[g, 0:W, :] += ds_p[g]
            db_ref[g, W:2 * W, :] += ds_c[g]
            dsk = jnp.exp(_sink_row(sink_ref, g) - lse[g]) * dl[g]
            for j in range(SWA_GROUP):
                h = g * SWA_GROUP + j
                dsk_ref[h:h + 1, :] -= jnp.broadcast_to(jnp.sum(dsk[:, j * W:(j + 1) * W], axis=1, keepdims=True), (1, LANES))
        dsb_p = [d.astype(BF16) for d in ds_p]
        dsb_c = [d.astype(BF16) for d in ds_c]
        for g in gs:
            rows = sc[g][1]
            dq = (_dot(kp_ref[rows, :], dsb_p[g]) + _dot(kc_ref[rows, :], dsb_c[g])) * scale
            for j in range(SWA_GROUP):
                h = g * SWA_GROUP + j
                dq_ref[h * HEAD_DIM:(h + 1) * HEAD_DIM, :] = dq[:, j * W:(j + 1) * W]
        for g in gs:
            rows = sc[g][1]
            vrows = slice(kv_rows + rows.start, kv_rows + rows.stop)
            dkv_ref[rows, prev] += _dot(sc[g][0], dsb_p[g], NT) * scale
            dkv_ref[rows, cur] += _dot(sc[g][0], dsb_c[g], NT) * scale
            dkv_ref[vrows, prev] += _dot(dog[g], p_p[g].astype(BF16), NT)
            dkv_ref[vrows, cur] += _dot(dog[g], p_c[g].astype(BF16), NT)

    return pl.pallas_call(
        body, name=name, grid=(S // W,),
        in_specs=[sk, qs, kp, kc, vp, vc, bs, qs, stat, stat],
        out_specs=[qs, pl.BlockSpec((2 * kv_rows, S), lambda i: (0, 0)), bs, pl.BlockSpec((SWA_Q_HEADS, LANES), lambda i: (0, 0))],
        out_shape=[jax.ShapeDtypeStruct((SWA_Q_HEADS * HEAD_DIM, S), F32), jax.ShapeDtypeStruct((2 * kv_rows, S), F32),
                   jax.ShapeDtypeStruct((SWA_KV_HEADS, 2 * W, GW), F32), jax.ShapeDtypeStruct((SWA_Q_HEADS, LANES), F32)],
        compiler_params=_params("arbitrary"),
    )(sinks, qkv, qkv, qkv, qkv, qkv, bias_g, do, lse, delta)


def _rel_onehot_t():
    qi = jnp.arange(WINDOW, dtype=jnp.int32)[None, :] + WINDOW
    kj = jnp.arange(2 * WINDOW, dtype=jnp.int32)[:, None]
    dist = qi - kj
    max_exact = REL_BUCKETS // 2
    d = jnp.maximum(dist, 0)
    log_ratio = jnp.log(jnp.maximum(d, 1).astype(F32) / max_exact) / math.log(REL_MAX_DIST / max_exact)
    large = jnp.minimum(max_exact + (log_ratio * (REL_BUCKETS - max_exact)).astype(jnp.int32), REL_BUCKETS - 1)
    bucket = jnp.where(d < max_exact, d, large).reshape(-1)
    return (bucket[None, :] == jnp.arange(REL_BUCKETS, dtype=jnp.int32)[:, None]).astype(BF16)


def _bias_table(rel_bias_t, onehot_t):
    Hq, NB = rel_bias_t.shape
    N = onehot_t.shape[1]
    tn = _tile(N, 4096)

    def body(r_ref, oh_ref, o_ref):
        oh = oh_ref[...]
        a1, a2, a3 = _split3(r_ref[...])
        o_ref[...] = _dot(a1, oh) + _dot(a2, oh) + _dot(a3, oh)

    return pl.pallas_call(
        body, name="rel_bias_table", grid=(N // tn,),
        in_specs=[pl.BlockSpec((Hq, NB), lambda j: (0, 0)), pl.BlockSpec((NB, tn), lambda j: (0, j))],
        out_specs=pl.BlockSpec((Hq, tn), lambda j: (0, j)),
        out_shape=jax.ShapeDtypeStruct((Hq, N), F32),
        compiler_params=_params("parallel"),
    )(rel_bias_t, onehot_t)


def _bias_table_bwd(dbias, onehot_t):
    L, Hq, N = dbias.shape
    NB = onehot_t.shape[0]
    tn = _tile(N, 4096)

    def body(d_ref, oh_ref, o_ref):
        @pl.when(pl.program_id(0) == 0)
        def _():
            o_ref[...] = jnp.zeros_like(o_ref)

        d = d_ref[0]
        for l in range(1, L):
            d = d + d_ref[l]
        oh = oh_ref[...]
        a1, a2, a3 = _split3(d)
        o_ref[...] += _dot(a1, oh, NT) + _dot(a2, oh, NT) + _dot(a3, oh, NT)

    return pl.pallas_call(
        body, name="rel_bias_bwd", grid=(N // tn,),
        in_specs=[pl.BlockSpec((L, Hq, tn), lambda j: (0, 0, j)), pl.BlockSpec((NB, tn), lambda j: (0, j))],
        out_specs=pl.BlockSpec((Hq, NB), lambda j: (0, 0)),
        out_shape=jax.ShapeDtypeStruct((Hq, NB), F32),
        compiler_params=_params("arbitrary"),
    )(dbias, onehot_t)


def _gate_fwd(lat, fb_col, *, name):
    S = lat.shape[1]
    tn = _tile(S, 256)

    def body(z_ref, fb_ref, o_ref, carry):
        @pl.when(pl.program_id(0) == 0)
        def _():
            carry[...] = jnp.zeros_like(carry)

        z = z_ref[...] + fb_ref[...]
        lf = jnp.minimum(z, 0.0) - jnp.log1p(jnp.exp(-jnp.abs(z)))
        r = lax.broadcasted_iota(jnp.int32, (tn, tn), 0)
        c = lax.broadcasted_iota(jnp.int32, (tn, tn), 1)
        tri = (r <= c).astype(BF16)
        a1, a2, a3 = _split3(lf)
        cum = _dot(a1, tri) + _dot(a2, tri) + _dot(a3, tri) + carry[:, 0:1]
        o_ref[...] = cum
        carry[...] = jnp.broadcast_to(cum[:, tn - 1:tn], carry.shape)

    return pl.pallas_call(
        body, name=name, grid=(S // tn,),
        in_specs=[pl.BlockSpec((GATE_ROWS, tn), lambda i: (0, i)), pl.BlockSpec((GATE_ROWS, 1), lambda i: (0, 0))],
        out_specs=pl.BlockSpec((GATE_ROWS, tn), lambda i: (0, i)),
        out_shape=jax.ShapeDtypeStruct((GATE_ROWS, S), F32),
        scratch_shapes=[pltpu.VMEM((GATE_ROWS, LANES), F32)],
        compiler_params=_params("arbitrary"),
    )(lat, fb_col)


def _gate_bwd(lat, fb_col, dF, *, name):
    S = lat.shape[1]
    tn = _tile(S, 256)
    nt = S // tn

    def body(z_ref, fb_ref, df_ref, dz_ref, dfb_ref, carry):
        @pl.when(pl.program_id(0) == 0)
        def _():
            carry[...] = jnp.zeros_like(carry)
            dfb_ref[...] = jnp.zeros_like(dfb_ref)

        r = lax.broadcasted_iota(jnp.int32, (tn, tn), 0)
        c = lax.broadcasted_iota(jnp.int32, (tn, tn), 1)
        tri = (r >= c).astype(BF16)
        a1, a2, a3 = _split3(df_ref[...])
        dlf = _dot(a1, tri) + _dot(a2, tri) + _dot(a3, tri) + carry[:, 0:1]
        carry[...] = jnp.broadcast_to(dlf[:, 0:1], carry.shape)
        z = z_ref[...] + fb_ref[...]
        row = lax.broadcasted_iota(jnp.int32, (GATE_ROWS, tn), 0)
        dz = jnp.where(row < FOX_HEADS, dlf / (1.0 + jnp.exp(z)), 0.0)
        dz_ref[...] = dz
        dfb_ref[...] += jnp.sum(dz, axis=1, keepdims=True)

    blk = pl.BlockSpec((GATE_ROWS, tn), lambda i: (0, nt - 1 - i))
    vec = pl.BlockSpec((GATE_ROWS, 1), lambda i: (0, 0))
    return pl.pallas_call(
        body, name=name, grid=(nt,),
        in_specs=[blk, vec, blk], out_specs=[blk, vec],
        out_shape=[jax.ShapeDtypeStruct((GATE_ROWS, S), F32), jax.ShapeDtypeStruct((GATE_ROWS, 1), F32)],
        scratch_shapes=[pltpu.VMEM((GATE_ROWS, LANES), F32)],
        compiler_params=_params("arbitrary"),
    )(lat, fb_col, dF)


def _rope_tables(S):
    pos = jnp.arange(S, dtype=F32)
    inv_freq = ROPE_THETA ** (-(jnp.arange(MLA_ROPE // 2, dtype=F32) * 2.0 / MLA_ROPE))
    ang = pos[:, None] * inv_freq[None, :]
    cos, sin = jnp.cos(ang).T, jnp.sin(ang).T
    z16 = jnp.zeros_like(cos)

    def slab(lo, fill):
        def put(first, second, f):
            return jnp.concatenate([jnp.full((lo, S), f, F32), first, second, jnp.full((LANES - lo - MLA_ROPE, S), f, F32)], axis=0)
        return put(cos, cos, fill), put(-sin, z16, 0.0), put(z16, sin, 0.0)

    tq = tuple(jnp.tile(t, (MLA_HEADS, 1)) for t in slab(MLA_NOPE, 1.0))
    return tq, slab(0, 0.0)


def _rope(x, c, s1, s2):
    n = x.shape[0]
    half = MLA_ROPE // 2
    return x * c + pltpu.roll(x, n - half, 0) * s1 + pltpu.roll(x, half, 0) * s2


def _rope_t(dy, c, s1, s2):
    n = dy.shape[0]
    half = MLA_ROPE // 2
    return dy * c + pltpu.roll(dy * s1, half, 0) + pltpu.roll(dy * s2, n - half, 0)


KR_SLAB0 = MLA_Q_RANK + MLA_KV_RANK


def _mla_prep_fwd(lat, g_q, g_kv, w_uq_t, w_ukv_t, tq, tmisc, *, name):
    S = lat.shape[1]
    tn = _tile(S, 512)
    QW = MLA_HEADS * MLA_PAD

    def body(lat_ref, gq_ref, gkv_ref, wq_ref, wkv_ref, c_ref, s1_ref, s2_ref, cm_ref, s1m_ref, s2m_ref,
             nq_ref, nkv_ref, q_ref, k_ref, v_ref):
        x = pltpu.roll(lat_ref[...], LAT_ROWS - LAT_SHIFT, 0)
        nq = _col_rms(x[0:MLA_Q_RANK, :], gq_ref[...]).astype(BF16)
        nkv = _col_rms(x[MLA_Q_RANK:KR_SLAB0, :], gkv_ref[...]).astype(BF16)
        nq_ref[...] = nq
        nkv_ref[...] = nkv
        q = _rope(_dot(wq_ref[...], nq), c_ref[...], s1_ref[...], s2_ref[...])
        q_ref[...] = (q * (MLA_SCALE * LOG2E)).astype(BF16)
        kv = _dot(wkv_ref[...], nkv).astype(BF16)
        kr = _rope(x[KR_SLAB0:LAT_ROWS, :], cm_ref[...], s1m_ref[...], s2m_ref[...]).astype(BF16)
        for h in range(MLA_HEADS):
            k_ref[h * MLA_PAD:h * MLA_PAD + MLA_NOPE, :] = kv[h * LANES:h * LANES + MLA_NOPE, :]
            k_ref[h * MLA_PAD + MLA_NOPE:(h + 1) * MLA_PAD, :] = kr[0:MLA_PAD - MLA_NOPE, :]
            v_ref[h * HEAD_DIM:(h + 1) * HEAD_DIM, :] = kv[h * LANES + MLA_NOPE:(h + 1) * LANES, :]

    def col(rows):
        return pl.BlockSpec((rows, tn), lambda i: (0, i))

    def full(a):
        return pl.BlockSpec(a.shape, lambda i: (0, 0))

    return pl.pallas_call(
        body, name=name, grid=(S // tn,),
        in_specs=[col(LAT_ROWS), full(g_q), full(g_kv), full(w_uq_t), full(w_ukv_t),
                  col(QW), col(QW), col(QW), col(LANES), col(LANES), col(LANES)],
        out_specs=[col(MLA_Q_RANK), col(MLA_KV_RANK), col(QW), col(QW), col(MLA_HEADS * HEAD_DIM)],
        out_shape=[jax.ShapeDtypeStruct((MLA_Q_RANK, S), BF16), jax.ShapeDtypeStruct((MLA_KV_RANK, S), BF16),
                   jax.ShapeDtypeStruct((QW, S), BF16), jax.ShapeDtypeStruct((QW, S), BF16),
                   jax.ShapeDtypeStruct((MLA_HEADS * HEAD_DIM, S), BF16)],
        compiler_params=_params("parallel"),
    )(lat, g_q, g_kv, w_uq_t, w_ukv_t, *tq, *tmisc)


def _mla_prep_bwd(lat, nq, nkv, g_q, g_kv, w_uq_p, w_ukv, tq, tmisc, dq, dk, dv, dflog, *, name):
    S = lat.shape[1]
    tn = _tile(S, 512)
    QW = MLA_HEADS * MLA_PAD

    def body(lat_ref, nq_ref, nkv_ref, gq_ref, gkv_ref, wq_ref, wkv_ref, c_ref, s1_ref, s2_ref,
             cm_ref, s1m_ref, s2m_ref, dq_ref, dk_ref, dv_ref, dfl_ref,
             dlat_ref, dwq_ref, dwkv_ref, dgq_ref, dgkv_ref, y_s):
        @pl.when(pl.program_id(0) == 0)
        def _():
            dwq_ref[...] = jnp.zeros_like(dwq_ref)
            dwkv_ref[...] = jnp.zeros_like(dwkv_ref)
            dgq_ref[...] = jnp.zeros_like(dgq_ref)
            dgkv_ref[...] = jnp.zeros_like(dgkv_ref)

        x = pltpu.roll(lat_ref[...], LAT_ROWS - LAT_SHIFT, 0)
        dqm = _rope_t(dq_ref[...], c_ref[...], s1_ref[...], s2_ref[...]).astype(BF16)
        dwq_ref[...] += _dot(dqm, nq_ref[...], NT)
        dx, dg = _col_rms_bwd(x[0:MLA_Q_RANK, :], gq_ref[...], _dot(wq_ref[...], dqm))
        y_s[0:MLA_Q_RANK, :] = dx
        dgq_ref[...] += dg
        dkv = jnp.concatenate(
            [part for h in range(MLA_HEADS)
             for part in (dk_ref[h * MLA_PAD:h * MLA_PAD + MLA_NOPE, :], dv_ref[h * HEAD_DIM:(h + 1) * HEAD_DIM, :])],
            axis=0).astype(BF16)
        dwkv_ref[...] += _dot(dkv, nkv_ref[...], NT)
        dx, dg = _col_rms_bwd(x[MLA_Q_RANK:KR_SLAB0, :], gkv_ref[...], _dot(wkv_ref[...], dkv))
        y_s[MLA_Q_RANK:KR_SLAB0, :] = dx
        dgkv_ref[...] += dg
        dkr = dk_ref[MLA_NOPE:MLA_PAD, :]
        for h in range(1, MLA_HEADS):
            dkr = dkr + dk_ref[h * MLA_PAD + MLA_NOPE:(h + 1) * MLA_PAD, :]
        dkr = jnp.concatenate([dkr, jnp.zeros((MLA_NOPE, tn), F32)], axis=0)
        y_s[KR_SLAB0:LAT_ROWS, :] = _rope_t(dkr, cm_ref[...], s1m_ref[...], s2m_ref[...])
        y = pltpu.roll(y_s[...], LAT_SHIFT, 0)
        row = lax.broadcasted_iota(jnp.int32, (LAT_ROWS, tn), 0)
        dfl = jnp.concatenate([dfl_ref[...], jnp.zeros((LAT_ROWS - GATE_ROWS, tn), F32)], axis=0)
        dlat_ref[...] = jnp.where(row < LAT_SHIFT, dfl, y).astype(BF16)

    def col(rows):
        return pl.BlockSpec((rows, tn), lambda i: (0, i))

    def full(a):
        return pl.BlockSpec(a.shape, lambda i: (0, 0))

    def acc(r, c):
        return pl.BlockSpec((r, c), lambda i: (0, 0))

    return pl.pallas_call(
        body, name=name, grid=(S // tn,),
        in_specs=[col(LAT_ROWS), col(MLA_Q_RANK), col(MLA_KV_RANK), full(g_q), full(g_kv),
                  full(w_uq_p), full(w_ukv), col(QW), col(QW), col(QW), col(LANES), col(LANES), col(LANES),
                  col(QW), col(QW), col(MLA_HEADS * HEAD_DIM), col(GATE_ROWS)],
        out_specs=[col(LAT_ROWS), acc(QW, MLA_Q_RANK), acc(QW, MLA_KV_RANK), acc(MLA_Q_RANK, 1), acc(MLA_KV_RANK, 1)],
        out_shape=[jax.ShapeDtypeStruct((LAT_ROWS, S), BF16), jax.ShapeDtypeStruct((QW, MLA_Q_RANK), F32),
                   jax.ShapeDtypeStruct((QW, MLA_KV_RANK), F32), jax.ShapeDtypeStruct((MLA_Q_RANK, 1), F32),
                   jax.ShapeDtypeStruct((MLA_KV_RANK, 1), F32)],
        scratch_shapes=[pltpu.VMEM((LAT_ROWS, tn), F32)],
        compiler_params=_params("arbitrary"),
    )(lat, nq, nkv, g_q, g_kv, w_uq_p, w_ukv, *tq, *tmisc, dq, dk, dv, dflog)


def _dproj_cast(dqa, dkva, dqf, dkf, dvf, dlat, *, name):
    S = dqa.shape[1]
    tn = _tile(S, 512)
    parts = (dqa, dkva, dqf, dkf, dvf, dlat)

    def body(*refs):
        o_ref = refs[-1]
        r0 = 0
        for ref in refs[:-1]:
            n = ref.shape[0]
            o_ref[r0:r0 + n, :] = ref[...].astype(BF16)
            r0 += n

    return pl.pallas_call(
        body, name=name, grid=(S // tn,),
        in_specs=[pl.BlockSpec((p.shape[0], tn), lambda i: (0, i)) for p in parts],
        out_specs=pl.BlockSpec((IN_ROWS, tn), lambda i: (0, i)),
        out_shape=jax.ShapeDtypeStruct((IN_ROWS, S), BF16),
        compiler_params=_params("parallel"),
    )(*parts)


GELU_C = math.sqrt(2.0 / math.pi)
GELU_A = 0.044715


HALO = 16


def _shift_down(a, k, fill):
    r = pltpu.roll(a, k, 0)
    row = lax.broadcasted_iota(jnp.int32, (8, a.shape[1]), 0)
    head = r[0:8, :]
    for i in range(k):
        head = jnp.where(row == i, fill[len(fill) - k + i], head)
    return jnp.concatenate([head, r[8:, :]], axis=0)


def _shift_up(d, k, fill):
    n = d.shape[0]
    r = pltpu.roll(d, n - k, 0)
    row = lax.broadcasted_iota(jnp.int32, (8, d.shape[1]), 0)
    tail = r[n - 8:n, :]
    for i in range(k):
        tail = jnp.where(row == 8 - k + i, fill[i], tail)
    return jnp.concatenate([r[0:n - 8, :], tail], axis=0)


def _conv_taps(a, before, w_ref, b_ref):
    a1 = _shift_down(a, 1, before)
    a2 = _shift_down(a, 2, before)
    return ((b_ref[...] + w_ref[0:1, :] * a2) + w_ref[1:2, :] * a1) + w_ref[2:3, :] * a


def _rows_before(halo_ref, first):
    h = halo_ref[HALO - 2:HALO, :].astype(F32)
    return jnp.where(first, 0.0, h[0:1, :]), jnp.where(first, 0.0, h[1:2, :])


def _conv_specs(S, tm, tc, nc):
    hb = tm // HALO
    main = lambda off: pl.BlockSpec((tm, tc), lambda j, i: (i, j + off))
    prev = lambda off: pl.BlockSpec((HALO, tc), lambda j, i: (jnp.maximum(i * hb - 1, 0), j + off))
    wspec = lambda off: pl.BlockSpec((3, tc), lambda j, i: (0, j + off))
    bspec = lambda off: pl.BlockSpec((1, tc), lambda j, i: (0, j + off))
    return main, prev, wspec, bspec


def _conv_geglu_fwd(a, conv_w, conv_b, *, name):
    S = a.shape[0]
    tm, tc = _tile(S, 512), _tile(D_FF, 1408)
    nc = D_FF // tc
    main, prev, wspec, bspec = _conv_specs(S, tm, tc, nc)

    def body(ag_ref, au_ref, hg_ref, hu_ref, wg_ref, wu_ref, bg_ref, bu_ref, u_ref, z_ref):
        first = pl.program_id(1) == 0
        gate = _conv_taps(ag_ref[...].astype(F32), _rows_before(hg_ref, first), wg_ref, bg_ref)
        up = _conv_taps(au_ref[...].astype(F32), _rows_before(hu_ref, first), wu_ref, bu_ref)
        u_ref[0] = gate
        u_ref[1] = up
        cdf = 0.5 * (1.0 + jnp.tanh(GELU_C * (gate + GELU_A * (gate * gate * gate))))
        z_ref[...] = (gate * cdf * up).astype(BF16)

    return pl.pallas_call(
        body, name=name, grid=(nc, S // tm),
        in_specs=[main(0), main(nc), prev(0), prev(nc), wspec(0), wspec(nc), bspec(0), bspec(nc)],
        out_specs=[pl.BlockSpec((2, tm, tc), lambda j, i: (0, i, j)), pl.BlockSpec((tm, tc), lambda j, i: (i, j))],
        out_shape=[jax.ShapeDtypeStruct((2, S, D_FF), F32), jax.ShapeDtypeStruct((S, D_FF), BF16)],
        compiler_params=_params("parallel", "arbitrary"),
    )(a, a, a, a, conv_w, conv_w, conv_b, conv_b)


def _geglu_bwd(gate, up, dz):
    g2x = gate * gate
    th = jnp.tanh(GELU_C * (gate + GELU_A * (g2x * gate)))
    cdf = 0.5 * (1.0 + th)
    dgelu = cdf + gate * (0.5 * (1.0 - th * th) * (GELU_C * (1.0 + 3.0 * GELU_A * g2x)))
    return dz * up * dgelu, dz * (gate * cdf)


def _conv_geglu_bwd(a, u, conv_w, dz, *, name):
    S = a.shape[0]
    tm, tc = _tile(S, 512), _tile(D_FF, 1408)
    nc = D_FF // tc
    nr = S // tm
    main, _, wspec, _ = _conv_specs(S, tm, tc, nc)
    hb = tm // 8

    def body(ag_ref, au_ref, u_ref, un_ref, wg_ref, wu_ref, dz_ref, dzn_ref, da_ref, dw_ref, db_ref):
        i = pl.program_id(1)
        last = i == nr - 1

        @pl.when(i == 0)
        def _():
            dw_ref[...] = jnp.zeros_like(dw_ref)
            db_ref[...] = jnp.zeros_like(db_ref)

        dus = _geglu_bwd(u_ref[0], u_ref[1], dz_ref[...])
        dus_n = _geglu_bwd(un_ref[0], un_ref[1], dzn_ref[...])
        for half, a_ref, w_ref in ((0, ag_ref, wg_ref), (1, au_ref, wu_ref)):
            du, du_n = dus[half], dus_n[half]
            after = (jnp.where(last, 0.0, du_n[0:1, :]), jnp.where(last, 0.0, du_n[1:2, :]))
            shifted = (_shift_up(du, 2, after), _shift_up(du, 1, after), du)
            da_ref[half] = (w_ref[2:3, :] * du + w_ref[1:2, :] * shifted[1] + w_ref[0:1, :] * shifted[0]).astype(BF16)
            af = a_ref[...].astype(F32)
            for tap in range(3):
                dw_ref[half, tap:tap + 1, :] += jnp.sum(shifted[tap] * af, axis=0, keepdims=True)
            db_ref[half] += jnp.sum(du, axis=0, keepdims=True)

    nxt8 = lambda j, i: (0, jnp.minimum((i + 1) * hb, S // 8 - 1), j)
    return pl.pallas_call(
        body, name=name, grid=(nc, nr),
        in_specs=[main(0), main(nc), pl.BlockSpec((2, tm, tc), lambda j, i: (0, i, j)), pl.BlockSpec((2, 8, tc), nxt8),
                  wspec(0), wspec(nc), pl.BlockSpec((tm, tc), lambda j, i: (i, j)),
                  pl.BlockSpec((8, tc), lambda j, i: (jnp.minimum((i + 1) * hb, S // 8 - 1), j))],
        out_specs=[pl.BlockSpec((2, tm, tc), lambda j, i: (0, i, j)), pl.BlockSpec((2, 3, tc), lambda j, i: (0, 0, j)),
                   pl.BlockSpec((2, 1, tc), lambda j, i: (0, 0, j))],
        out_shape=[jax.ShapeDtypeStruct((2, S, D_FF), BF16), jax.ShapeDtypeStruct((2, 3, D_FF), F32),
                   jax.ShapeDtypeStruct((2, 1, D_FF), F32)],
        compiler_params=_params("parallel", "arbitrary"),
    )(a, a, u, u, conv_w, conv_w, dz, dz)


ROW_BLOCK_BYTES = 1536 * 1024


def _row_tile(rows, cols):
    return rows if rows * cols * 4 <= ROW_BLOCK_BYTES else _tile(rows, ROW_TILE)


def _adamw_update(w, g, m, v):
    m = ADAM_B1 * m + (1.0 - ADAM_B1) * g
    v = ADAM_B2 * v + (1.0 - ADAM_B2) * jnp.square(g)
    m_hat = m / (1.0 - ADAM_B1 ** ADAM_STEP)
    v_hat = v / (1.0 - ADAM_B2 ** ADAM_STEP)
    return -ADAM_LR * (m_hat / (jnp.sqrt(v_hat) + ADAM_EPS) + ADAM_WD * w), m, v


def _adamw(w, g, m, v, *, name):
    L, A, B = w.shape
    ta = _tile(A, ROW_TILE)

    def body(w_ref, g_ref, m_ref, v_ref, d_ref, mo_ref, vo_ref):
        d_ref[...], mo_ref[...], vo_ref[...] = _adamw_update(w_ref[...], g_ref[...], m_ref[...], v_ref[...])

    blk = pl.BlockSpec((None, ta, B), lambda l, i: (l, i, 0))
    shp = jax.ShapeDtypeStruct((L, A, B), F32)
    return pl.pallas_call(
        body, name=name, grid=(L, A // ta),
        in_specs=[blk] * 4, out_specs=[blk] * 3, out_shape=[shp] * 3,
        compiler_params=_params("parallel", "parallel"),
    )(w, g, m, v)


def _scalar(v):
    return jnp.reshape(v, (1,)).astype(jnp.int32)


def _adamw_halves(w, g_mine, g_other, m, v, *, name):
    L, A, B = w.shape
    ta = _row_tile(A // 2, B)
    nb = A // 2 // ta

    def body(c_ref, w_ref, gm_ref, go_ref, m_ref, v_ref, g_ref, d_ref, mo_ref, vo_ref):
        g = jnp.where(pl.program_id(1) // nb == c_ref[0], gm_ref[...], go_ref[...])
        g_ref[...] = g
        d_ref[...], mo_ref[...], vo_ref[...] = _adamw_update(w_ref[...], g, m_ref[...], v_ref[...])

    blk = pl.BlockSpec((None, ta, B), lambda l, i, c_ref: (l, i, 0))
    half = pl.BlockSpec((None, ta, B), lambda l, i, c_ref: (l, i % nb, 0))
    shp = jax.ShapeDtypeStruct((L, A, B), F32)
    return pl.pallas_call(
        body, name=name,
        grid_spec=pltpu.PrefetchScalarGridSpec(num_scalar_prefetch=1, grid=(L, A // ta),
                                               in_specs=[blk, half, half, blk, blk], out_specs=[blk] * 4),
        out_shape=[shp] * 4,
        compiler_params=_params("parallel", "parallel"),
    )(_scalar(lax.axis_index("c")), w, g_mine, g_other, m, v)


def _chip_index():
    return 2 * lax.axis_index("x") + lax.axis_index("y")


def _pair_sum(g, recv, *, name):
    n, A, B = g.shape
    ta = _row_tile(A // 2, B)
    nb = A // 2 // ta

    def body(c_ref, g_ref, r_ref, o_ref):
        o_ref[...] = g_ref[...] + r_ref[...]

    return pl.pallas_call(
        body, name=name,
        grid_spec=pltpu.PrefetchScalarGridSpec(
            num_scalar_prefetch=1, grid=(n, nb),
            in_specs=[pl.BlockSpec((None, ta, B), lambda s, r, c_ref: (s, c_ref[0] * nb + r, 0)),
                      pl.BlockSpec((None, ta, B), lambda s, r, c_ref: (s, r, 0))],
            out_specs=pl.BlockSpec((None, ta, B), lambda s, r, c_ref: (s, r, 0))),
        out_shape=jax.ShapeDtypeStruct((n, A // 2, B), F32),
        compiler_params=_params("parallel", "parallel"),
    )(_scalar(lax.axis_index("c")), g, recv)


def _chip_sum(landed, own, *, name):
    n, A2, B = landed.shape
    ta = _row_tile(A2, B)

    def body(me_ref, *refs):
        slots, own_ref, o_ref = refs[:n], refs[n], refs[n + 1]
        parts = [jnp.where(me_ref[0] == s, own_ref[...], slots[s][...]) for s in range(n)]
        o_ref[...] = ((parts[0] + parts[1]) + parts[2]) + parts[3]

    def slot(s):
        return pl.BlockSpec((None, ta, B), lambda r, me_ref: (jnp.where(me_ref[0] == s, (s + 1) % n, s), r, 0))

    return pl.pallas_call(
        body, name=name,
        grid_spec=pltpu.PrefetchScalarGridSpec(
            num_scalar_prefetch=1, grid=(A2 // ta,),
            in_specs=[slot(s) for s in range(n)] + [pl.BlockSpec((None, ta, B), lambda r, me_ref: (me_ref[0], r, 0))],
            out_specs=pl.BlockSpec((ta, B), lambda r, me_ref: (r, 0))),
        out_shape=jax.ShapeDtypeStruct((A2, B), F32),
        compiler_params=_params("parallel"),
    )(_scalar(_chip_index()), *([landed] * n), own)


HBM_SPEC = pl.BlockSpec(memory_space=pl.ANY)
COMM_PARAMS = pltpu.CompilerParams(has_side_effects=True)


def _mesh_pos():
    return lax.axis_index("x"), lax.axis_index("y"), lax.axis_index("c")


def _other_chips(x, y):
    return [(1 - x, y), (x, 1 - y), (1 - x, 1 - y)]


def _remote(src, dst, send_sems, recv_sems, k, to):
    return pltpu.make_async_remote_copy(src_ref=src, dst_ref=dst, send_sem=send_sems.at[k], recv_sem=recv_sems.at[k],
                                        device_id=to, device_id_type=MESH)


def _place_own(gathered, shards, *, name):
    n = len(shards)

    def body(me_ref, *refs):
        for s_ref, o_ref in zip(refs[:n], refs[2 * n:]):
            o_ref[...] = s_ref[...]

    return pl.pallas_call(
        body, name=name,
        grid_spec=pltpu.PrefetchScalarGridSpec(
            num_scalar_prefetch=1, grid=(1,),
            in_specs=[pl.BlockSpec(s.shape, lambda i, me_ref: (0, 0)) for s in shards] + [HBM_SPEC] * n,
            out_specs=[pl.BlockSpec((None,) + s.shape, lambda i, me_ref: (me_ref[0], 0, 0)) for s in shards]),
        out_shape=[jax.ShapeDtypeStruct(g.shape, g.dtype) for g in gathered],
        input_output_aliases={1 + n + k: k for k in range(n)},
        compiler_params=_params("arbitrary"),
    )(_scalar(_chip_index()), *shards, *gathered)


def _half_rows(rows, c, align=8):
    assert (rows // 2) % align == 0
    return pl.ds(pl.multiple_of(c * (rows // 2), align), rows // 2)


BF16_ROWS = 16


def _halved(rows):
    return rows % (2 * BF16_ROWS) == 0


def _gather_copies(srcs, lands, send_sems, recv_sems):
    x, y, c = _mesh_pos()
    me = 2 * x + y
    out = []
    for k in range(len(srcs)):
        a = srcs[k].shape[0]
        rows = _half_rows(a, c, BF16_ROWS) if _halved(a) else pl.ds(0, a)
        for j, (px, py) in enumerate(_other_chips(x, y)):
            send = _remote(srcs[k].at[rows], lands[k].at[me, rows], send_sems, recv_sems, 3 * k + j, (px, py, c))
            recv = _remote(srcs[k].at[rows], lands[k].at[2 * px + py, rows], send_sems, recv_sems, 3 * k + j, (px, py, c))
            out.append((send, recv))
    return out


def _gather_start(srcs):
    nu = len(srcs)
    sizes = [len(su) for su in srcs]
    offs = [2 * sum(sizes[:u]) for u in range(nu + 1)]
    lands = [[lax.empty((N_CHIPS,) + s.shape, s.dtype) for s in su] for su in srcs]
    flat = [a for u in range(nu) for a in srcs[u] + lands[u]]

    def body(*refs):
        bufs, sems, token = refs[:len(flat)], refs[len(flat):len(flat) + 2 * nu], refs[-1]
        for u, n in enumerate(sizes):
            mine = bufs[offs[u]:offs[u + 1]]
            for send, _ in _gather_copies(mine[:n], mine[n:], sems[2 * u], sems[2 * u + 1]):
                send.start()
        token[...] = jnp.zeros_like(token)

    res = pl.pallas_call(
        body, name="weight_gather_start",
        in_specs=[HBM_ONLY] * len(flat),
        out_specs=[SEM_SPEC] * (2 * nu) + [HBM_ONLY] * len(flat) + [pl.BlockSpec(memory_space=pltpu.VMEM)],
        out_shape=[pltpu.SemaphoreType.DMA((3 * n,)) for n in sizes for _ in (0, 1)] + [pltpu.HBM(a.shape, a.dtype) for a in flat]
        + [jax.ShapeDtypeStruct((1, 1), F32)],
        input_output_aliases={i: 2 * nu + i for i in range(len(flat))},
        compiler_params=SPLIT_PARAMS,
    )(*[pltpu.with_memory_space_constraint(a, pltpu.HBM) for a in flat])
    bufs = res[2 * nu:2 * nu + len(flat)]
    state = [(res[2 * u], res[2 * u + 1], list(bufs[offs[u]:offs[u] + n]), list(bufs[offs[u] + n:offs[u + 1]]))
             for u, n in enumerate(sizes)]
    return state, res[-1]


def _gather_wait(state, after, *, name):
    send_sems, recv_sems, srcs, lands = state
    n = len(srcs)

    def body(*refs):
        for send, recv in _gather_copies(refs[:n], refs[n:2 * n], refs[2 * n], refs[2 * n + 1]):
            send.wait_send()
            recv.wait_recv()

    res = pl.pallas_call(
        body, name=name,
        in_specs=[HBM_ONLY] * (2 * n) + [SEM_SPEC, SEM_SPEC, HBM_SPEC],
        out_specs=[HBM_ONLY] * (2 * n),
        out_shape=[pltpu.HBM(a.shape, a.dtype) for a in srcs + lands],
        input_output_aliases={i: i for i in range(2 * n)},
        compiler_params=SPLIT_PARAMS,
    )(*srcs, *lands, send_sems, recv_sems, after)
    return list(res[:n]), list(res[n:])


def _gather_forward(lands, *, name):
    n = len(lands)

    def body(*refs):
        bufs, outs = refs[:n], refs[n:2 * n]
        send_sems, recv_sems = refs[2 * n:]
        x, y, c = _mesh_pos()
        copies, waits = [], []
        for k in range(n):
            a = lands[k].shape[1]
            if not _halved(a):
                continue
            for j, (px, py) in enumerate(_other_chips(x, y)):
                mine = 2 * px + py, _half_rows(a, c, BF16_ROWS)
                copies.append(_remote(bufs[k].at[mine], outs[k].at[mine], send_sems, recv_sems, 3 * k + j, (x, y, 1 - c)))
                lands_here = outs[k].at[2 * px + py, _half_rows(a, 1 - c, BF16_ROWS)]
                waits.append(_remote(lands_here, lands_here, send_sems, recv_sems, 3 * k + j, (x, y, 1 - c)))
        for cp in copies:
            cp.start()
        for cp in waits:
            cp.wait_recv()
        for cp in copies:
            cp.wait_send()

    return pl.pallas_call(
        body, name=name,
        in_specs=[HBM_SPEC] * n, out_specs=[HBM_SPEC] * n,
        out_shape=[jax.ShapeDtypeStruct(a.shape, a.dtype) for a in lands],
        scratch_shapes=[pltpu.SemaphoreType.DMA((3 * n,)), pltpu.SemaphoreType.DMA((3 * n,))],
        input_output_aliases={i: i for i in range(n)},
        compiler_params=COMM_PARAMS,
    )(*lands)


def _sibling_exchange(gs, *, name):
    n = len(gs)

    def body(*refs):
        ins, outs = refs[:n], refs[n:2 * n]
        send_sems, recv_sems = refs[2 * n:]
        x, y, c = _mesh_pos()
        copies = [_remote(ins[k].at[:, _half_rows(gs[k].shape[1], 1 - c)], outs[k], send_sems, recv_sems, k, (x, y, 1 - c))
                  for k in range(n)]
        for cp in copies:
            cp.start()
        for cp in copies:
            cp.wait()

    return pl.pallas_call(
        body, name=name,
        in_specs=[HBM_SPEC] * n, out_specs=[HBM_SPEC] * n,
        out_shape=[jax.ShapeDtypeStruct((g.shape[0], g.shape[1] // 2, g.shape[2]), g.dtype) for g in gs],
        scratch_shapes=[pltpu.SemaphoreType.DMA((n,)), pltpu.SemaphoreType.DMA((n,))],
        compiler_params=COMM_PARAMS,
    )(*gs)


HBM_ONLY = pl.BlockSpec(memory_space=pltpu.HBM)
SEM_SPEC = pl.BlockSpec(memory_space=pltpu.SEMAPHORE)
SPLIT_PARAMS = pltpu.CompilerParams(has_side_effects=pltpu.SideEffectType.DATAFLOW_SIDE_EFFECTING)


def _scatter_copies(srcs, lands, send_sems, recv_sems):
    x, y, c = _mesh_pos()
    me = 2 * x + y
    out = []
    for k in range(len(srcs)):
        for j, (px, py) in enumerate(_other_chips(x, y)):
            s = 2 * px + py
            send = _remote(srcs[k].at[s], lands[k].at[me], send_sems, recv_sems, 3 * k + j, (px, py, c))
            recv = _remote(srcs[k].at[s], lands[k].at[s], send_sems, recv_sems, 3 * k + j, (px, py, c))
            out.append((send, recv))
    return out


def _exchange_copies(srcs, lands, send_sems, recv_sems):
    x, y, c = _mesh_pos()
    out = []
    for k in range(len(srcs)):
        cp = _remote(srcs[k].at[:, _half_rows(srcs[k].shape[1], 1 - c)], lands[k], send_sems, recv_sems, k, (x, y, 1 - c))
        out.append((cp, cp))
    return out


def _split_start(srcs, land_shapes, copies, n_sems, *, name):
    n = len(srcs)
    lands = [lax.empty(shape, s.dtype) for shape, s in zip(land_shapes, srcs)]

    def body(*refs):
        ins, zones = refs[:n], refs[n:2 * n]
        send_sems, recv_sems, token = refs[2 * n], refs[2 * n + 1], refs[-1]
        for send, _ in copies(ins, zones, send_sems, recv_sems):
            send.start()
        token[...] = jnp.zeros_like(token)

    hbm = lambda a: pltpu.HBM(a.shape, a.dtype)
    res = pl.pallas_call(
        body, name=name,
        in_specs=[HBM_ONLY] * (2 * n),
        out_specs=[SEM_SPEC, SEM_SPEC] + [HBM_ONLY] * (2 * n) + [pl.BlockSpec(memory_space=pltpu.VMEM)],
        out_shape=[pltpu.SemaphoreType.DMA((n_sems,)), pltpu.SemaphoreType.DMA((n_sems,))] + [hbm(a) for a in srcs + lands]
        + [jax.ShapeDtypeStruct((1, 1), F32)],
        input_output_aliases={i: 2 + i for i in range(2 * n)},
        compiler_params=SPLIT_PARAMS,
    )(*[pltpu.with_memory_space_constraint(a, pltpu.HBM) for a in srcs + lands])
    return (res[0], res[1], list(res[2:2 + n]), list(res[2 + n:2 + 2 * n])), res[-1]


def _scatter_start(ps, *, name):
    return _split_start(ps, [p.shape for p in ps], _scatter_copies, 3 * len(ps), name=name)


def _exchange_start(gs, *, name):
    return _split_start(gs, [(g.shape[0], g.shape[1] // 2, g.shape[2]) for g in gs], _exchange_copies, len(gs), name=name)


def _split_wait(started, copies, after, *, name):
    ng = len(started)
    sizes = [len(st[2]) for st in started]
    offs = [2 * sum(sizes[:i]) for i in range(ng + 1)]
    flat = [a for (_, _, ps, lands) in started for a in ps + lands]

    def body(*refs):
        bufs, sems = refs[:len(flat)], refs[len(flat):len(flat) + 2 * ng]
        for i, n in enumerate(sizes):
            srcs, zones = bufs[offs[i]:offs[i] + n], bufs[offs[i] + n:offs[i + 1]]
            for send, recv in copies(srcs, zones, sems[2 * i], sems[2 * i + 1]):
                send.wait_send()
                recv.wait_recv()

    res = pl.pallas_call(
        body, name=name,
        in_specs=[HBM_ONLY] * len(flat) + [SEM_SPEC] * (2 * ng) + [HBM_SPEC],
        out_specs=[HBM_ONLY] * len(flat),
        out_shape=[pltpu.HBM(a.shape, a.dtype) for a in flat],
        input_output_aliases={i: i for i in range(len(flat))},
        compiler_params=SPLIT_PARAMS,
    )(*flat, *[s for (ss, rs, _, _) in started for s in (ss, rs)], after)
    return [(list(res[offs[i]:offs[i] + n]), list(res[offs[i] + n:offs[i + 1]])) for i, n in enumerate(sizes)]


def _sibling_share(hs):
    n = len(hs)

    def body(*refs):
        ins, outs = refs[:n], refs[n:2 * n]
        send_sems, recv_sems = refs[2 * n:]
        x, y, c = _mesh_pos()
        copies = [_remote(ins[k], outs[k], send_sems, recv_sems, k, (x, y, 1 - c)) for k in range(n)]
        for cp in copies:
            cp.start()
        for cp in copies:
            cp.wait()

    return pl.pallas_call(
        body, name="grad_sibling_share",
        in_specs=[HBM_SPEC] * n, out_specs=[HBM_SPEC] * n,
        out_shape=[jax.ShapeDtypeStruct(h.shape, h.dtype) for h in hs],
        scratch_shapes=[pltpu.SemaphoreType.DMA((n,)), pltpu.SemaphoreType.DMA((n,))],
        compiler_params=COMM_PARAMS,
    )(*hs)


def _allreduce_small(part, by_chip):
    rows, C = part.shape
    rows2 = by_chip.shape[1]

    def body(p_ref, q_ref, o_ref, o2_ref, slots, slots2, send_sems, recv_sems):
        x, y, c = _mesh_pos()
        me = 4 * x + 2 * y + c
        slots[me] = p_ref[...]
        slots2[me] = q_ref[2 * x + y]
        copies = []
        for k in range(1, 8):
            kx, ky, kc = (k >> 2) & 1, (k >> 1) & 1, k & 1
            peer = (x ^ kx if kx else x, y ^ ky if ky else y, c ^ kc if kc else c)
            src = 4 * peer[0] + 2 * peer[1] + peer[2]
            pair = []
            for j, (mine, zone, lands) in enumerate(((p_ref, slots.at[me], slots.at[src]),
                                                     (q_ref.at[2 * peer[0] + peer[1]], slots2.at[me], slots2.at[src]))):
                cp = _remote(mine, zone, send_sems, recv_sems, 2 * (k - 1) + j, peer)
                cp.start()
                pair.append((cp, _remote(mine, lands, send_sems, recv_sems, 2 * (k - 1) + j, peer)))
            copies += pair
        for _, landing in copies:
            landing.wait_recv()
        for cp, _ in copies:
            cp.wait_send()
        total, total2 = slots[0], slots2[0]
        for d in range(1, 8):
            total, total2 = total + slots[d], total2 + slots2[d]
        o_ref[...] = total
        o2_ref[...] = total2

    vmem = pl.BlockSpec(memory_space=pltpu.VMEM)
    return pl.pallas_call(
        body, name="small_grad_allreduce",
        in_specs=[vmem, vmem], out_specs=[vmem, vmem],
        out_shape=[jax.ShapeDtypeStruct((rows, C), F32), jax.ShapeDtypeStruct((rows2, C), F32)],
        scratch_shapes=[pltpu.VMEM((8, rows, C), F32), pltpu.VMEM((8, rows2, C), F32),
                        pltpu.SemaphoreType.DMA((14,)), pltpu.SemaphoreType.DMA((14,))],
        compiler_params=pltpu.CompilerParams(has_side_effects=True, vmem_limit_bytes=VMEM_LIMIT_BYTES),
    )(part, by_chip)


def _pad_w_uq(w):
    lead = w.shape[:-1]
    w = w.reshape(lead + (MLA_HEADS, MLA_QK))
    w = jnp.concatenate([w, jnp.zeros(lead + (MLA_HEADS, MLA_PAD - MLA_QK), w.dtype)], axis=-1)
    return w.reshape(lead + (MLA_HEADS * MLA_PAD,))


def _unpad_w_uq(g):
    lead = g.shape[:-1]
    return g.reshape(lead + (MLA_HEADS, MLA_PAD))[..., :MLA_QK].reshape(lead + (MLA_HEADS * MLA_QK,))


def _t(a):
    return jnp.swapaxes(a, -1, -2)


def _shards_of_cols(w):
    A, NB = w.shape
    return w.reshape(A, N_CHIPS, NB // N_CHIPS).transpose(1, 0, 2)


BIG = ("w_in", "w_uq", "w_ukv", "w_out", "w_up", "w_down")
SMALL = ("attn_pre_norm", "forget_bias", "swa_sinks", "rel_bias", "q_latent_norm", "kv_latent_norm", "group_norm",
         "attn_post_norm", "ffn_pre_norm", "conv_b", "ffn_post_norm")
WEIGHTS = ("attn_pre_norm", "w_in", "forget_bias", "swa_sinks", "rel_bias", "q_latent_norm", "w_uq", "kv_latent_norm",
           "w_ukv", "group_norm", "w_out", "attn_post_norm", "ffn_pre_norm", "w_up", "conv_w", "conv_b", "w_down",
           "ffn_post_norm")


PACK_UNIT = 8 * LANES


def _pack_rows(shape):
    return -(-int(np.prod(shape)) // PACK_UNIT) * 8


def _pack(arrs, row_mult=8):
    parts = []
    for a in arrs:
        n = int(np.prod(a.shape))
        parts.append(jnp.pad(a.reshape(-1), (0, _pack_rows(a.shape) * LANES - n)).reshape(-1, LANES))
    rows = sum(p.shape[0] for p in parts)
    pad = -rows % row_mult
    if pad:
        parts.append(jnp.zeros((pad, LANES), parts[0].dtype))
    return jnp.concatenate(parts, axis=0)


def _unpack(packed, shapes):
    packed = packed.reshape(-1, LANES)
    out, off = [], 0
    for shp in shapes:
        r = _pack_rows(shp)
        out.append(packed[off:off + r].reshape(-1)[:int(np.prod(shp))].reshape(shp))
        off += r
    return out


LAYER_KEYS = ("w_qkv_t", "w_lat_t", "w_in_t", "w_uq_p", "w_uq_t", "w_ukv", "w_ukv_t", "w_out", "w_up", "w_down", "conv_w")


MIX_WEIGHTS = ("w_in", "w_uq", "w_ukv", "w_out")
FFN_WEIGHTS = ("w_up", "w_down", "conv_w")


def _layer_weights(gathered):
    cols = lambda g: g.transpose(1, 0, 2).reshape(g.shape[1], N_CHIPS * g.shape[2])
    out = {}
    if "w_in" in gathered:
        w_in_t = _t(gathered["w_in"]).reshape(IN_COLS, D_MODEL)
        w_in_t = jnp.pad(w_in_t, ((0, IN_ROWS - IN_COLS), (0, 0)))
        w_uq_p = _pad_w_uq(cols(gathered["w_uq"]))
        w_ukv = cols(gathered["w_ukv"])
        out.update(w_qkv_t=w_in_t[:QKV_ROWS], w_lat_t=w_in_t[QKV_ROWS:], w_in_t=w_in_t, w_uq_p=w_uq_p, w_uq_t=_t(w_uq_p),
                   w_ukv=w_ukv, w_ukv_t=_t(w_ukv), w_out=gathered["w_out"].reshape(D_MODEL, D_MODEL))
    if "w_up" in gathered:
        out.update(w_up=gathered["w_up"], w_down=gathered["w_down"].reshape(D_FF, D_MODEL), conv_w=cols(gathered["conv_w"]))
    return out


def _local_step(x, target, W, layer_weights, layer_done):
    W = dict(W, **{key: [None] * DEPTH for key in LAYER_KEYS})
    S = x.shape[0]
    tq_tabs, tm_tabs = _rope_tables(S)
    onehot_t = _rel_onehot_t()
    bias_t = _bias_table(W["rel_bias"].T, onehot_t).reshape(SWA_KV_HEADS, SWA_GROUP, 2 * WINDOW, WINDOW)
    bias_t = bias_t.transpose(0, 2, 1, 3).reshape(SWA_KV_HEADS, 2 * WINDOW, GW)
    row = lambda a: a.reshape(1, -1)
    col = lambda a: a.reshape(-1, 1)
    fox_rows = (FOX_ROW0, FOX_ROW0 + FOX_HEADS * HEAD_DIM, FOX_ROW0 + 2 * FOX_HEADS * HEAD_DIM, SWA_Q_HEADS)
    fox = dict(rows=fox_rows, H=FOX_HEADS, Dk=HEAD_DIM, Dv=HEAD_DIM, scale=HEAD_DIM ** -0.5)
    mla = dict(rows=(0, 0, 0, SWA_Q_HEADS + FOX_HEADS), H=MLA_HEADS, Dk=MLA_PAD, Dv=HEAD_DIM, scale=MLA_SCALE, q_scaled=True)

    saved = []
    h = _rms_fwd(x, row(W["attn_pre_norm"][0]), name="rms_in")
    for l in range(DEPTH):
        sv = {"x0": x, "h1": h}
        for key, val in layer_weights(l, h, False).items():
            W[key][l] = val
        qkv = _matmul(W["w_qkv_t"][l], h, tb=True, out_dtype=BF16, name="proj_qkv")
        lat = _matmul(W["w_lat_t"][l], h, tb=True, name="proj_lat")
        oa, lse_a = _swa_fwd(qkv, bias_t, W["swa_sinks"][l], name="swa_fwd")
        fb_col = jnp.pad(col(W["forget_bias"][l]), ((0, GATE_ROWS - FOX_HEADS), (0, 0)))
        f4 = _gate_fwd(lat, fb_col, name="fox_gate_fwd")[:FOX_HEADS]
        f2 = f4 * LOG2E
        f_row, f_col = f2[:, None, :], f2.T
        of, lse_f = _attn_fwd(qkv, qkv, qkv, f_row=f_row, f_col=f_col, name="fox_fwd", **fox)
        nq, nkv, qm, km, vm = _mla_prep_fwd(lat, col(W["q_latent_norm"][l]), col(W["kv_latent_norm"][l]), W["w_uq_t"][l],
                                            W["w_ukv_t"][l], tq_tabs, tm_tabs, name="mla_prep_fwd")
        oc, lse_c = _attn_fwd(qm, km, vm, name="mla_fwd", **mla)
        mixed = _group_norm_fwd(oa, of, oc, col(W["group_norm"][l]), name="group_norm_fwd")
        y, x1, h2 = _matmul(mixed, W["w_out"][l], ta=True, name="proj_out",
                            resid_rms=(x, row(W["attn_post_norm"][l]), row(W["ffn_pre_norm"][l])))
        for key, val in layer_weights(l, h2, True).items():
            W[key][l] = val
        a = _matmul(h2, W["w_up"][l], b_shards=True, out_dtype=BF16, name="ffn_up")
        u, z = _conv_geglu_fwd(a, W["conv_w"][l], row(W["conv_b"][l]), name="conv_geglu_fwd")
        g_next = row(W["attn_pre_norm"][l + 1]) if l + 1 < DEPTH else None
        y2, x2, *h_next = _matmul(z, W["w_down"][l], name="ffn_down", resid_rms=(x1, row(W["ffn_post_norm"][l]), g_next))
        h_next = h_next[0] if h_next else None
        sv.update(qkv=qkv, lat=lat, oa=oa, lse_a=lse_a, fb_col=fb_col, f_row=f_row, f_col=f_col, of=of, lse_f=lse_f,
                  nq=nq, nkv=nkv, qm=qm, km=km, vm=vm, oc=oc, lse_c=lse_c, mixed=mixed, y=y, x1=x1, h2=h2, a=a, u=u, z=z, y2=y2)
        saved.append(sv)
        x, h = x2, h_next

    loss, dx = _loss_head(x, target)

    G = {k: [None] * DEPTH for k in WEIGHTS if k != "rel_bias" and k not in BIG}
    dbias_layers = [None] * DEPTH
    for l in reversed(range(DEPTH)):
        sv = saved[l]
        gb = {}
        if l == DEPTH - 1:
            dy2, dg = _rms_bwd(sv["y2"], row(W["ffn_post_norm"][l]), dx, out_dtype=BF16, name="ffn_post_bwd")
            G["ffn_post_norm"][l] = dg[0]
        dz = _matmul(dy2, W["w_down"][l], tb=True, name="ffn_down_dx")
        gb["w_down"] = _matmul(sv["z"], dy2, ta=True, name="ffn_down_dw").reshape(N_CHIPS, D_FF // N_CHIPS, D_MODEL)
        da, dcw, dcb = _conv_geglu_bwd(sv["a"], sv["u"], W["conv_w"][l], dz, name="conv_geglu_bwd")
        G["conv_w"][l] = dcw.transpose(1, 0, 2).reshape(3, 2 * D_FF)
        G["conv_b"][l] = dcb.reshape(2 * D_FF)
        gb["w_up"] = _matmul(sv["h2"], da, ta=True, out_shards=True, b_halves=True, name="ffn_up_dw")
        token = layer_done(l, gb)
        gb = {}
        dx1, dg, dy, dg_post = _matmul(
            da, W["w_up"][l], tb=True, b_shards=True, a_halves=True, name="ffn_up_dx",
            norm_bwd=(sv["x1"], row(W["ffn_pre_norm"][l]) + token, dx, (sv["y"], row(W["attn_post_norm"][l]))))
        G["ffn_pre_norm"][l] = dg[0]
        G["attn_post_norm"][l] = dg_post[0]
        dmixed = _matmul(W["w_out"][l], dy, tb=True, name="proj_out_dx")
        gb["w_out"] = _matmul(sv["mixed"], dy, name="proj_out_dw").reshape(N_CHIPS, D_MODEL // N_CHIPS, D_MODEL)
        doa, dof, doc, dg, delta = _group_norm_bwd(sv["oa"], sv["of"], sv["oc"], col(W["group_norm"][l]), dmixed,
                                                   name="group_norm_bwd")
        G["group_norm"][l] = dg[:, 0]
        dqa, dkva, dbias_l, dsink = _swa_bwd(sv["qkv"], bias_t, W["swa_sinks"][l], doa, sv["lse_a"],
                                             delta.reshape(-1, S), name="swa_bwd")
        dbias_layers[l] = (dbias_l.reshape(SWA_KV_HEADS, 2 * WINDOW, SWA_GROUP, WINDOW).transpose(0, 2, 1, 3)
                           .reshape(SWA_Q_HEADS, -1))
        G["swa_sinks"][l] = dsink[:, 0]
        dqf, dkf, dvf, dfk = _attn_bwd(sv["qkv"], sv["qkv"], sv["qkv"], do=dof, lse=sv["lse_f"], delta=delta,
                                       f_row=sv["f_row"], f_col=sv["f_col"], name="fox_bwd", **fox)
        dF = jnp.pad(dfk.T, ((0, GATE_ROWS - FOX_HEADS), (0, 0)))
        dflog, dfb = _gate_bwd(sv["lat"], sv["fb_col"], dF, name="fox_gate_bwd")
        G["forget_bias"][l] = dfb[:FOX_HEADS, 0]
        dqm, dkm, dvm = _attn_bwd(sv["qm"], sv["km"], sv["vm"], do=doc, lse=sv["lse_c"], delta=delta, name="mla_bwd", **mla)
        dlat, dwq_t, dwkv_t, dgq, dgkv = _mla_prep_bwd(
            sv["lat"], sv["nq"], sv["nkv"], col(W["q_latent_norm"][l]), col(W["kv_latent_norm"][l]), W["w_uq_p"][l],
            W["w_ukv"][l], tq_tabs, tm_tabs, dqm, dkm, dvm, dflog, name="mla_prep_bwd")
        gb["w_uq"], gb["w_ukv"] = _shards_of_cols(_unpad_w_uq(dwq_t.T)), _shards_of_cols(dwkv_t.T)
        G["q_latent_norm"][l], G["kv_latent_norm"][l] = dgq[:, 0], dgkv[:, 0]
        dproj = _dproj_cast(dqa, dkva, dqf, dkf, dvf, dlat, name="dproj_cast")
        dw_in_t = _matmul(dproj, sv["h1"], name="proj_in_dw")
        gb["w_in"] = _t(dw_in_t[:IN_COLS].reshape(N_CHIPS, IN_COLS // N_CHIPS, D_MODEL))
        token = layer_done(l, gb)
        below = (saved[l - 1]["y2"], row(W["ffn_post_norm"][l - 1])) if l > 0 else None
        res = _matmul(dproj, W["w_in_t"][l], ta=True, name="proj_in_dx",
                      norm_bwd=(sv["x0"], row(W["attn_pre_norm"][l]) + token, dx1, below))
        dx, G["attn_pre_norm"][l] = res[0], res[1][0]
        if l > 0:
            dy2, G["ffn_post_norm"][l - 1] = res[2], res[3][0]

    grads = {k: jnp.stack(v) for k, v in G.items()}
    grads["rel_bias"] = _bias_table_bwd(jnp.stack(dbias_layers), onehot_t).T
    return loss, dx, grads


def kernel(x, attn_pre_norm, w_in, forget_bias, swa_sinks, rel_bias, q_latent_norm, w_uq, kv_latent_norm, w_ukv, group_norm, w_out, attn_post_norm, ffn_pre_norm, w_up, conv_w, conv_b, w_down, ffn_post_norm, loss_target, m_attn_pre_norm, m_w_in, m_forget_bias, m_swa_sinks, m_rel_bias, m_q_latent_norm, m_w_uq, m_kv_latent_norm, m_w_ukv, m_group_norm, m_w_out, m_attn_post_norm, m_ffn_pre_norm, m_w_up, m_conv_w, m_conv_b, m_w_down, m_ffn_post_norm, v_attn_pre_norm, v_w_in, v_forget_bias, v_swa_sinks, v_rel_bias, v_q_latent_norm, v_w_uq, v_kv_latent_norm, v_w_ukv, v_group_norm, v_w_out, v_attn_post_norm, v_ffn_pre_norm, v_w_up, v_conv_w, v_conv_b, v_w_down, v_ffn_post_norm):
    args = dict(locals())
    w = {k: args[k] for k in WEIGHTS}
    m = {k: args["m_" + k] for k in WEIGHTS}
    v = {k: args["v_" + k] for k in WEIGHTS}

    block = lambda l, keys: [w[k][l] if k == "conv_w" else w[k][l].astype(BF16) for k in keys]
    units = [(0, MIX_WEIGHTS), (0, FFN_WEIGHTS)] + [(l, MIX_WEIGHTS + FFN_WEIGHTS) for l in range(1, DEPTH)]
    gather_state, token = _gather_start([block(l, keys) for l, keys in units])
    W = {k: w[k] for k in SMALL}
    W["attn_pre_norm"] = W["attn_pre_norm"] + token

    def layer_weights(l, after, for_ffn):
        if for_ffn and l > 0:
            return {}
        keys = FFN_WEIGHTS if for_ffn else (MIX_WEIGHTS if l == 0 else MIX_WEIGHTS + FFN_WEIGHTS)
        tag = f"{l}_{keys[0]}"
        srcs, lands = _gather_wait(gather_state[units.index((l, keys))], after, name="weight_gather_wait_" + tag)
        lands = _gather_forward(lands, name="weight_gather_forward_" + tag)
        lands = _place_own(lands, srcs, name="place_own_shards")
        return _layer_weights(dict(zip(keys, lands)))

    started, groups, pending = [], [], []

    def to_chips(l, keys, gs, recv, tag):
        pair = [_pair_sum(gk, rk, name="grad_pair_sum") for gk, rk in zip(gs, recv)]
        state, token = _scatter_start(pair, name="grad_scatter_start_" + tag)
        started.append(state)
        groups.append((l, keys))
        return token

    def finish_pending(after):
        l, keys, tag, state = pending.pop()
        gs, recv = _split_wait([state], _exchange_copies, after, name="grad_exchange_wait_" + tag)[0]
        return to_chips(l, keys, gs, recv, tag)

    def layer_done(l, gb):
        keys = [k for k in BIG if k in gb]
        gs = [gb[k] for k in keys]
        tag = f"{l}_{keys[0]}"
        token = finish_pending(gs[0]) if pending else 0.0
        if l == 0:
            return token + to_chips(l, keys, gs, _sibling_exchange(gs, name="grad_sibling_exchange_" + tag), tag)
        state, started_token = _exchange_start(gs, name="grad_exchange_start_" + tag)
        pending.append((l, keys, tag, state))
        return token + started_token

    loss_part, dx, g = _local_step(x[0], loss_target[0], W, layer_weights, layer_done)
    loss = lax.psum(loss_part, ("x", "y", "c"))

    reduced = {}
    for (l, keys), (pair, zones) in zip(groups, _split_wait(started, _scatter_copies, dx, name="grad_scatter_wait")):
        for k, p, z in zip(keys, pair, zones):
            reduced[k, l] = _chip_sum(z, p, name="grad_chip_sum")
    mine = [jnp.stack([reduced[k, l] for l in range(DEPTH)]) for k in BIG]
    other = _sibling_share(mine)
    out_g, out_d, out_m, out_v = {}, {}, {}, {}
    for k, g_mine, g_other in zip(BIG, mine, other):
        out_g[k], out_d[k], out_m[k], out_v[k] = _adamw_halves(w[k], g_mine, g_other, m[k], v[k], name="adamw_" + k)

    small_shapes = [w[k].shape for k in SMALL]
    taps_by_chip = g["conv_w"].reshape(DEPTH, 3, N_CHIPS, FF_SHARD).transpose(2, 0, 1, 3)
    reduced, taps = _allreduce_small(_pack([g[k] for k in SMALL]), jnp.stack([_pack([t]) for t in taps_by_chip]))
    g_small = _unpack(reduced, small_shapes) + _unpack(taps, [w["conv_w"].shape])
    names = SMALL + ("conv_w",)
    shapes = small_shapes + [w["conv_w"].shape]
    packed = lambda arrs: _pack(arrs, ROW_TILE)[None]
    d_s, m_s, v_s = _adamw(packed([w[k] for k in names]), packed(g_small), packed([m[k] for k in names]),
                           packed([v[k] for k in names]), name="adamw_small")
    out_g.update(zip(names, g_small))
    out_d.update(zip(names, _unpack(d_s, shapes)))
    out_m.update(zip(names, _unpack(m_s, shapes)))
    out_v.update(zip(names, _unpack(v_s, shapes)))

    return (loss, dx[None], *[out_g[k] for k in WEIGHTS], *[out_d[k] for k in WEIGHTS],
            *[out_m[k] for k in WEIGHTS], *[out_v[k] for k in WEIGHTS])
```

```python
import math

import numpy as np
import jax
import jax.numpy as jnp
from jax import lax
from jax.experimental import pallas as pl
from jax.experimental.pallas import tpu as pltpu

F32 = jnp.float32
BF16 = jnp.bfloat16

D_MODEL = 1024
DEPTH = 4
HEAD_DIM = 64
SWA_Q_HEADS = 8
SWA_KV_HEADS = 2
SWA_GROUP = SWA_Q_HEADS // SWA_KV_HEADS
WINDOW = 128
FOX_HEADS = 4
MLA_HEADS = 4
MLA_Q_RANK = 256
MLA_KV_RANK = 128
MLA_NOPE = 64
MLA_ROPE = 32
MLA_QK = MLA_NOPE + MLA_ROPE
ROPE_THETA = 10000.0
REL_BUCKETS = 32
REL_MAX_DIST = 128
D_FF = 2816
EPS = 1e-6
NEG_INF = -1e30
LANES = 128
N_CHIPS = 4

IN_COLS = 1956
IN_ROWS = 2048
QKV_ROWS = 1536
LAT_ROWS = IN_ROWS - QKV_ROWS
LAT_SHIFT = FOX_HEADS
FOX_ROW0 = 768
MLA_PAD = LANES
GATE_ROWS = 8

ADAM_LR = 0.001
ADAM_B1 = 0.9
ADAM_B2 = 0.999
ADAM_EPS = 1e-08
ADAM_WD = 0.01
ADAM_STEP = 10

VMEM_LIMIT_BYTES = 48 * 1024 * 1024
ATT_TILE = 512
LOG2E = math.log2(math.e)
MLA_SCALE = MLA_QK ** -0.5
ROW_TILE = 256
MESH = pl.DeviceIdType.MESH

NT = (((1,), (1,)), ((), ()))
TN = (((0,), (0,)), ((), ()))
NN = (((1,), (0,)), ((), ()))


def _params(*sem):
    return pltpu.CompilerParams(dimension_semantics=sem, vmem_limit_bytes=VMEM_LIMIT_BYTES)


def _tile(dim, cap):
    for t in (2816, 2048, 1408, 1024, 512, 256, 128, 64, 32, 16, 8):
        if t <= cap and dim % t == 0:
            return t
    return dim


def _dot(a, b, dims=NN):
    return lax.dot_general(a, b, dims, preferred_element_type=F32)


def _split3(a):
    a1 = a.astype(BF16)
    r1 = a - a1.astype(F32)
    a2 = r1.astype(BF16)
    a3 = (r1 - a2.astype(F32)).astype(BF16)
    return a1, a2, a3


FF_SHARD = 2 * D_FF // N_CHIPS
MATMUL_VMEM_BYTES = 40 * 1024 * 1024
TAIL_ROWS = 512
TAIL_VMEM_LIMIT_BYTES = 56 * 1024 * 1024


def _matmul(a, b, *, ta=False, tb=False, out_dtype=F32, name, b_shards=False, out_shards=False, a_halves=False,
            b_halves=False, norm_bwd=None, resid_rms=None):
    if a_halves:
        M, K = a.shape[1], 2 * a.shape[2]
    elif ta:
        K, M = a.shape
    else:
        M, K = a.shape
    if b_halves:
        K2, N = b.shape[1], 2 * b.shape[2]
    elif b_shards:
        K2, N = (2 * D_FF, D_MODEL) if tb else (D_MODEL, 2 * D_FF)
    elif tb:
        N, K2 = b.shape
    else:
        K2, N = b.shape
    assert K == K2, (a.shape, b.shape)
    tn = _tile(N, 1408)
    tk = FF_SHARD if (b_shards and tb) else _tile(K, 2816)
    out_bytes = jnp.dtype(out_dtype).itemsize
    with_tail = norm_bwd is not None or resid_rms is not None
    tile_bytes = 4 + (2 * (4 * 4 + 2) if with_tail else 2 * out_bytes)
    vmem = lambda tm, tk: 2 * 2 * tk * (tm + tn) + tile_bytes * tm * tn
    tm = M if M <= 2048 else _tile(M, 1408)
    if M > 2048 and M % 2048 == 0 and tk == K and vmem(2048, tk) <= MATMUL_VMEM_BYTES:
        tm = 2048
    if with_tail:
        assert tn == N and not out_shards and (norm_bwd is None or resid_rms is None)
        tm = TAIL_ROWS
    while vmem(tm, tk) > MATMUL_VMEM_BYTES and tk % 256 == 0:
        tk //= 2
    nk = K // tk
    dims = (((0 if ta else 1,), (1 if tb else 0,)), ((), ()))
    if with_tail:
        if norm_bwd is not None:
            x, g, resid, then = norm_bwd
            chained = then is not None
            tail_in, in_kinds = [x, g, resid] + (list(then) if chained else []), "rvr" + ("rv" if chained else "")
            out_kinds, out_dtypes = "rv" + ("rv" if chained else ""), [F32, F32] + ([BF16, F32] if chained else [])

            def tail(dy, ins, outs):
                dx, dg = _seg_rms_bwd(ins[0][...], ins[1][...], dy)
                dx = dx + ins[2][...]
                outs[0][...] = dx
                outs[1][...] += dg
                if chained:
                    dx2, dg2 = _seg_rms_bwd(ins[3][...], ins[4][...], dx)
                    outs[2][...] = dx2.astype(BF16)
                    outs[3][...] += dg2
        else:
            x, g_post, g_next = resid_rms
            with_next = g_next is not None
            tail_in, in_kinds = [x, g_post] + ([g_next] if with_next else []), "rv" + ("v" if with_next else "")
            out_kinds, out_dtypes = "rr" + ("r" if with_next else ""), [F32, F32] + ([BF16] if with_next else [])

            def tail(y, ins, outs):
                outs[0][...] = y
                xn = ins[0][...] + _seg_rms(y, ins[1][...])
                outs[1][...] = xn
                if with_next:
                    outs[2][...] = _seg_rms(xn, ins[2][...]).astype(BF16)

        def fused(a_ref, b_ref, *refs):
            ins, outs, acc = refs[:len(tail_in)], refs[len(tail_in):-1], refs[-1]
            k, i = pl.program_id(0), pl.program_id(1)
            acc_ref = acc.at[pl.ds(pl.multiple_of(i * tm, tm), tm), :] if nk > 1 else acc

            @pl.when(k == 0)
            def _():
                acc_ref[...] = jnp.zeros((tm, N), F32)

            acc_ref[...] += lax.dot_general(a_ref[...], b_ref[...], dims, preferred_element_type=F32)

            @pl.when((k == nk - 1) & (i == 0))
            def _():
                for o, kind in zip(outs, out_kinds):
                    if kind == "v":
                        o[...] = jnp.zeros_like(o)

            @pl.when(k == nk - 1)
            def _():
                tail(acc_ref[...], ins, outs)

    def body(a_ref, b_ref, o_ref, acc_ref):
        k = pl.program_id(2)

        @pl.when(k == 0)
        def _():
            acc_ref[...] = jnp.zeros_like(acc_ref)

        acc_ref[...] += lax.dot_general(a_ref[...], b_ref[...], dims, preferred_element_type=F32)

        @pl.when(k == nk - 1)
        def _():
            o_ref[...] = acc_ref[...].astype(o_ref.dtype)

    if a_halves:
        nh = K // 2 // tk
        a_spec = pl.BlockSpec((None, tm, tk), lambda i, j, k: (k // nh, i, k % nh))
    else:
        a_spec = pl.BlockSpec((tk, tm), lambda i, j, k: (k, i)) if ta else pl.BlockSpec((tm, tk), lambda i, j, k: (i, k))
    if b_halves:
        nh = N // 2 // tn
        b_spec = pl.BlockSpec((None, tk, tn), lambda i, j, k: (j // nh, k, j % nh))
    elif b_shards and tb:
        assert tk == FF_SHARD
        b_spec = pl.BlockSpec((None, tn, tk), lambda i, j, k: (k, j, 0))
    elif b_shards:
        assert tn == FF_SHARD
        b_spec = pl.BlockSpec((None, tk, tn), lambda i, j, k: (j, k, 0))
    else:
        b_spec = pl.BlockSpec((tn, tk), lambda i, j, k: (j, k)) if tb else pl.BlockSpec((tk, tn), lambda i, j, k: (k, j))
    if out_shards:
        assert tn == FF_SHARD
        out_spec = pl.BlockSpec((None, tm, tn), lambda i, j, k: (j, i, 0))
        out_shape = jax.ShapeDtypeStruct((N // tn, M, tn), out_dtype)
    else:
        out_spec = pl.BlockSpec((tm, tn), lambda i, j, k: (i, j))
        out_shape = jax.ShapeDtypeStruct((M, N), out_dtype)
    if with_tail:
        spec = {"r": pl.BlockSpec((tm, N), lambda k, i: (jnp.where(k == nk - 1, i, 0), 0)),
                "v": pl.BlockSpec((1, N), lambda k, i: (0, 0))}
        a_map, b_map = a_spec.index_map, b_spec.index_map
        return pl.pallas_call(
            fused, name=name, grid=(nk, M // tm),
            in_specs=[pl.BlockSpec(a_spec.block_shape, lambda k, i: a_map(i, 0, k)),
                      pl.BlockSpec(b_spec.block_shape, lambda k, i: b_map(i, 0, k))] + [spec[c] for c in in_kinds],
            out_specs=[spec[c] for c in out_kinds],
            out_shape=[jax.ShapeDtypeStruct((M if c == "r" else 1, N), d) for c, d in zip(out_kinds, out_dtypes)],
            scratch_shapes=[pltpu.VMEM((M if nk > 1 else tm, N), F32)],
            compiler_params=pltpu.CompilerParams(dimension_semantics=("arbitrary", "arbitrary"),
                                                 vmem_limit_bytes=TAIL_VMEM_LIMIT_BYTES if nk > 1 else VMEM_LIMIT_BYTES),
        )(a, b, *tail_in)
    return pl.pallas_call(
        body, name=name, grid=(M // tm, N // tn, nk),
        in_specs=[a_spec, b_spec], out_specs=out_spec, out_shape=out_shape,
        scratch_shapes=[pltpu.VMEM((tm, tn), F32)],
        compiler_params=_params("parallel", "parallel", "arbitrary"),
    )(a, b)


def _seg_rms(xs, g):
    r = lax.rsqrt(jnp.mean(xs * xs, axis=-1, keepdims=True) + EPS)
    return xs * r * g


def _seg_rms_bwd(xs, g, dy):
    r = lax.rsqrt(jnp.mean(xs * xs, axis=-1, keepdims=True) + EPS)
    gd = dy * g
    c = jnp.mean(gd * xs, axis=-1, keepdims=True)
    dx = r * gd - xs * (r * r * r * c)
    dg = jnp.sum(dy * (xs * r), axis=0, keepdims=True)
    return dx, dg


def _rms_fwd(x, g, *, name):
    S, W = x.shape
    tm = _tile(S, 512)

    def body(x_ref, g_ref, o_ref):
        o_ref[...] = _seg_rms(x_ref[...], g_ref[...]).astype(o_ref.dtype)

    return pl.pallas_call(
        body, name=name, grid=(S // tm,),
        in_specs=[pl.BlockSpec((tm, W), lambda i: (i, 0)), pl.BlockSpec((1, W), lambda i: (0, 0))],
        out_specs=pl.BlockSpec((tm, W), lambda i: (i, 0)),
        out_shape=jax.ShapeDtypeStruct((S, W), BF16),
        compiler_params=_params("parallel"),
    )(x, g)


def _rms_bwd(x, g, dy, *, out_dtype, name):
    S, W = x.shape
    tm = _tile(S, 512)

    def body(x_ref, g_ref, dy_ref, dx_ref, dg_ref):
        @pl.when(pl.program_id(0) == 0)
        def _():
            dg_ref[...] = jnp.zeros_like(dg_ref)

        dx, dg = _seg_rms_bwd(x_ref[...], g_ref[...], dy_ref[...])
        dx_ref[...] = dx.astype(dx_ref.dtype)
        dg_ref[...] += dg

    row = pl.BlockSpec((tm, W), lambda i: (i, 0))
    vec = pl.BlockSpec((1, W), lambda i: (0, 0))
    return pl.pallas_call(
        body, name=name, grid=(S // tm,),
        in_specs=[row, vec, row], out_specs=[row, vec],
        out_shape=[jax.ShapeDtypeStruct((S, W), out_dtype), jax.ShapeDtypeStruct((1, W), F32)],
        compiler_params=_params("arbitrary"),
    )(x, g, dy)


def _col_rms(xs, g):
    r = lax.rsqrt(jnp.mean(xs * xs, axis=0, keepdims=True) + EPS)
    return xs * r * g


def _col_rms_bwd(xs, g, dy):
    r = lax.rsqrt(jnp.mean(xs * xs, axis=0, keepdims=True) + EPS)
    gd = dy * g
    c = jnp.mean(gd * xs, axis=0, keepdims=True)
    dx = r * gd - xs * (r * r * r * c)
    dg = jnp.sum(dy * (xs * r), axis=1, keepdims=True)
    return dx, dg


GROUP_ROWS = (SWA_Q_HEADS * HEAD_DIM, FOX_HEADS * HEAD_DIM, MLA_HEADS * HEAD_DIM)


def _group_specs(S, tn):
    outs = [pl.BlockSpec((n, tn), lambda i: (0, i)) for n in GROUP_ROWS]
    g = pl.BlockSpec((D_MODEL, 1), lambda i: (0, 0))
    mixed = pl.BlockSpec((D_MODEL, tn), lambda i: (0, i))
    return outs, g, mixed


def _group_norm_fwd(oa, of, oc, g, *, name):
    S = oa.shape[1]
    tn = _tile(S, 512)
    outs, gs, mixed = _group_specs(S, tn)

    def body(a_ref, f_ref, c_ref, g_ref, o_ref):
        r0 = 0
        for ref, n in zip((a_ref, f_ref, c_ref), GROUP_ROWS):
            o_ref[r0:r0 + n, :] = _col_rms(ref[...], g_ref[r0:r0 + n, :]).astype(BF16)
            r0 += n

    return pl.pallas_call(
        body, name=name, grid=(S // tn,),
        in_specs=outs + [gs], out_specs=mixed,
        out_shape=jax.ShapeDtypeStruct((D_MODEL, S), BF16),
        compiler_params=_params("parallel"),
    )(oa, of, oc, g)


def _group_norm_bwd(oa, of, oc, g, dmixed, *, name):
    S = oa.shape[1]
    tn = _tile(S, 512)
    outs, gs, mixed = _group_specs(S, tn)
    n_heads = D_MODEL // HEAD_DIM

    def body(a_ref, f_ref, c_ref, g_ref, dm_ref, da_ref, df_ref, dc_ref, dg_ref, dl_ref):
        @pl.when(pl.program_id(0) == 0)
        def _():
            dg_ref[...] = jnp.zeros_like(dg_ref)

        r0 = 0
        for ref, dref, n in zip((a_ref, f_ref, c_ref), (da_ref, df_ref, dc_ref), GROUP_ROWS):
            o = ref[...]
            dx, dg = _col_rms_bwd(o, g_ref[r0:r0 + n, :], dm_ref[r0:r0 + n, :])
            dxb = dx.astype(BF16)
            dref[...] = dxb
            dg_ref[r0:r0 + n, :] += dg
            od = o * dxb.astype(F32)
            for h in range(n // HEAD_DIM):
                dl_ref[r0 // HEAD_DIM + h] = jnp.sum(od[h * HEAD_DIM:(h + 1) * HEAD_DIM, :], axis=0, keepdims=True)
            r0 += n

    return pl.pallas_call(
        body, name=name, grid=(S // tn,),
        in_specs=outs + [gs, mixed], out_specs=outs + [gs, pl.BlockSpec((n_heads, 1, tn), lambda i: (0, 0, i))],
        out_shape=[jax.ShapeDtypeStruct((n, S), BF16) for n in GROUP_ROWS] + [jax.ShapeDtypeStruct((D_MODEL, 1), F32),
                                                                              jax.ShapeDtypeStruct((n_heads, 1, S), F32)],
        compiler_params=_params("arbitrary"),
    )(oa, of, oc, g, dmixed)


def _loss_head(y, target):
    S, W = y.shape
    tm = _tile(S, 512)

    def body(y_ref, t_ref, d_ref, l_ref):
        @pl.when(pl.program_id(0) == 0)
        def _():
            l_ref[...] = jnp.zeros_like(l_ref)

        err = y_ref[...] - t_ref[...]
        d_ref[...] = err * (1.0 / W)
        l_ref[...] += 0.5 * jnp.sum(jnp.mean(err * err, axis=-1, keepdims=True), axis=0, keepdims=True)

    row = pl.BlockSpec((tm, W), lambda i: (i, 0))
    d, l = pl.pallas_call(
        body, name="loss_head", grid=(S // tm,),
        in_specs=[row, row],
        out_specs=[row, pl.BlockSpec((1, 1), lambda i: (0, 0))],
        out_shape=[jax.ShapeDtypeStruct((S, W), F32), jax.ShapeDtypeStruct((1, 1), F32)],
        compiler_params=_params("arbitrary"),
    )(y, target)
    return l[0, 0], d


def _attn_fwd(q_src, k_src, v_src, rows, H, Dk, Dv, scale, f_row=None, f_col=None, *, name, q_scaled=False):
    S = q_src.shape[1]
    T = _tile(S, ATT_TILE)
    nq = S // T
    forget = f_row is not None
    qb, kb, vb = rows[0] // (H * Dk), rows[1] // (H * Dk), rows[2] // (H * Dv)
    hs = range(H)

    def body(*refs):
        if forget:
            q_ref, k_ref, v_ref, fq_ref, fk_ref, o_ref, lse_ref = refs
        else:
            q_ref, k_ref, v_ref, o_ref, lse_ref = refs
        i = pl.program_id(0)

        def tile(j, masked, state):
            off = pl.multiple_of(j * T, T)
            ss = [_dot(k_ref[h * Dk:(h + 1) * Dk, pl.ds(off, T)], q_ref[h * Dk:(h + 1) * Dk, :], TN) for h in hs]
            if not q_scaled:
                ss = [s * (scale * LOG2E) for s in ss]
            if forget:
                ss = [ss[h] + (fq_ref[h] - fk_ref[pl.ds(off, T), h:h + 1]) for h in hs]
            if masked:
                r = lax.broadcasted_iota(jnp.int32, (T, T), 0)
                c = lax.broadcasted_iota(jnp.int32, (T, T), 1)
                ss = [jnp.where(r <= c, s, NEG_INF) for s in ss]
            m_new = [jnp.maximum(state[h][0], jnp.max(ss[h], axis=0, keepdims=True)) for h in hs]
            alpha = [jnp.exp2(state[h][0] - m_new[h]) for h in hs]
            ps = [jnp.exp2(ss[h] - m_new[h]) for h in hs]
            l_new = [alpha[h] * state[h][1] + jnp.sum(ps[h], axis=0, keepdims=True) for h in hs]
            p_hi = [p.astype(BF16) for p in ps]
            vs = [v_ref[h * Dv:(h + 1) * Dv, pl.ds(off, T)] for h in hs]
            pv = [_dot(vs[h], p_hi[h]) for h in hs]
            if forget:
                pv = [pv[h] + _dot(vs[h], (ps[h] - p_hi[h].astype(F32)).astype(BF16)) for h in hs]
            return tuple((m_new[h], l_new[h], alpha[h] * state[h][2] + pv[h]) for h in hs)

        init = tuple((jnp.full((1, T), NEG_INF, F32), jnp.zeros((1, T), F32), jnp.zeros((Dv, T), F32)) for _ in hs)
        state = lax.fori_loop(0, i, lambda j, st: tile(j, False, st), init)
        state = tile(i, True, state)
        for h in hs:
            m, l, acc = state[h]
            o_ref[h * Dv:(h + 1) * Dv, :] = acc / l
            lse_ref[h] = m + jnp.log2(l)

    in_specs = [pl.BlockSpec((H * Dk, T), lambda i: (qb, i)),
                pl.BlockSpec((H * Dk, S), lambda i: (kb, 0)),
                pl.BlockSpec((H * Dv, S), lambda i: (vb, 0))]
    ins = [q_src, k_src, v_src]
    if forget:
        in_specs += [pl.BlockSpec((H, 1, T), lambda i: (0, 0, i)), pl.BlockSpec((S, H), lambda i: (0, 0))]
        ins += [f_row, f_col]
    return pl.pallas_call(
        body, name=name, grid=(nq,),
        in_specs=in_specs,
        out_specs=[pl.BlockSpec((H * Dv, T), lambda i: (0, i)), pl.BlockSpec((H, 1, T), lambda i: (0, 0, i))],
        out_shape=[jax.ShapeDtypeStruct((H * Dv, S), F32), jax.ShapeDtypeStruct((H, 1, S), F32)],
        compiler_params=_params("parallel"),
    )(*ins)


def _attn_bwd(q_src, k_src, v_src, rows, H, Dk, Dv, scale, do, lse, delta, f_row=None, f_col=None, *, name, q_scaled=False):
    S = q_src.shape[1]
    T = _tile(S, ATT_TILE)
    nq = S // T
    forget = f_row is not None
    qb, kb, vb, db = rows[0] // (H * Dk), rows[1] // (H * Dk), rows[2] // (H * Dv), rows[3] // H
    hs = range(H)

    def body(*refs):
        if forget:
            (q_ref, k_ref, v_ref, do_ref, lse_ref, dl_ref, fq_ref, fk_ref,
             dq_ref, dk_ref, dv_ref, df_ref, dk_s, dv_s, df_s) = refs
        else:
            q_ref, k_ref, v_ref, do_ref, lse_ref, dl_ref, dq_ref, dk_ref, dv_ref, dk_s, dv_s = refs
        j = pl.program_id(0)

        @pl.when(j == 0)
        def _():
            dq_ref[...] = jnp.zeros_like(dq_ref)

        dk_s[...] = jnp.zeros_like(dk_s)
        dv_s[...] = jnp.zeros_like(dv_s)
        if forget:
            df_s[...] = jnp.zeros_like(df_s)
        kt = [k_ref[h * Dk:(h + 1) * Dk, :] for h in hs]
        kj = [k.T for k in kt]
        vj = [v_ref[h * Dv:(h + 1) * Dv, :].T for h in hs]
        koff = pl.multiple_of(j * T, T)

        def tile(i, masked):
            cols = pl.ds(pl.multiple_of(i * T, T), T)
            qi = [q_ref[h * Dk:(h + 1) * Dk, cols] for h in hs]
            doi = [do_ref[h * Dv:(h + 1) * Dv, cols] for h in hs]
            st = [_dot(kj[h], qi[h]) for h in hs]
            if not q_scaled:
                st = [x * (scale * LOG2E) for x in st]
            if forget:
                st = [st[h] + (fq_ref[h, :, cols] - fk_ref[pl.ds(koff, T), h:h + 1]) for h in hs]
            if masked:
                r = lax.broadcasted_iota(jnp.int32, (T, T), 0)
                c = lax.broadcasted_iota(jnp.int32, (T, T), 1)
                st = [jnp.where(r <= c, x, NEG_INF) for x in st]
            pt = [jnp.exp2(st[h] - lse_ref[h, :, cols]) for h in hs]
            dpt = [_dot(vj[h], doi[h]) for h in hs]
            dst = [pt[h] * (dpt[h] - dl_ref[h, :, cols]) for h in hs]
            ptb = [p.astype(BF16) for p in pt]
            dsb = [d.astype(BF16) for d in dst]
            for h in hs:
                dv_s[h * Dv:(h + 1) * Dv, :] += _dot(doi[h], ptb[h], NT)
            for h in hs:
                dk_s[h * Dk:(h + 1) * Dk, :] += _dot(qi[h], dsb[h], NT)
            for h in hs:
                dq_ref[h * Dk:(h + 1) * Dk, cols] += _dot(kt[h], dsb[h]) * scale
            if forget:
                for h in hs:
                    part = dst[h][:, 0:LANES]
                    for c0 in range(LANES, T, LANES):
                        part = part + dst[h][:, c0:c0 + LANES]
                    df_s[h] += part

        tile(j, True)

        def loop_body(i, carry):
            tile(i, False)
            return carry

        lax.fori_loop(j + 1, nq, loop_body, 0)
        dk_ref[...] = dk_s[...] * ((1.0 / LOG2E) if q_scaled else scale)
        dv_ref[...] = dv_s[...]
        if forget:
            df_ref[...] = jnp.concatenate([-jnp.sum(df_s[h], axis=-1, keepdims=True) for h in hs], axis=1)

    res = lambda D, b0: pl.BlockSpec((H * D, S), lambda j: (b0, 0))
    blk = lambda D, b0: pl.BlockSpec((H * D, T), lambda j: (b0, j))
    row3 = lambda b0: pl.BlockSpec((H, 1, S), lambda j: (b0, 0, 0))
    in_specs = [res(Dk, qb), blk(Dk, kb), blk(Dv, vb), res(Dv, 0), row3(0), row3(db)]
    ins = [q_src, k_src, v_src, do, lse, delta]
    out_specs = [res(Dk, 0), blk(Dk, 0), blk(Dv, 0)]
    out_shape = [jax.ShapeDtypeStruct((H * Dk, S), F32), jax.ShapeDtypeStruct((H * Dk, S), F32),
                 jax.ShapeDtypeStruct((H * Dv, S), F32)]
    scratch = [pltpu.VMEM((H * Dk, T), F32), pltpu.VMEM((H * Dv, T), F32)]
    if forget:
        in_specs += [row3(0), pl.BlockSpec((S, H), lambda j: (0, 0))]
        ins += [f_row, f_col]
        out_specs.append(pl.BlockSpec((T, H), lambda j: (j, 0)))
        out_shape.append(jax.ShapeDtypeStruct((S, H), F32))
        scratch.append(pltpu.VMEM((H, T, min(T, LANES)), F32))
    return pl.pallas_call(
        body, name=name, grid=(nq,),
        in_specs=in_specs, out_specs=out_specs, out_shape=out_shape, scratch_shapes=scratch,
        compiler_params=_params("arbitrary"),
    )(*ins)


GW = SWA_GROUP * WINDOW


def _swa_masks(i):
    r = lax.broadcasted_iota(jnp.int32, (WINDOW, GW), 0)
    c = lax.broadcasted_iota(jnp.int32, (WINDOW, GW), 1) % WINDOW
    return (r > c) & (i > 0), r <= c


def _swa_specs():
    W = WINDOW
    kv_rows = SWA_KV_HEADS * HEAD_DIM
    q = pl.BlockSpec((SWA_Q_HEADS * HEAD_DIM, W), lambda i: (0, i))
    prev = lambda b: pl.BlockSpec((kv_rows, W), lambda i: (b, jnp.maximum(i - 1, 0)))
    cur = lambda b: pl.BlockSpec((kv_rows, W), lambda i: (b, i))
    bias = pl.BlockSpec((SWA_KV_HEADS, 2 * W, GW), lambda i: (0, 0, 0))
    stat = pl.BlockSpec((SWA_Q_HEADS, W), lambda i: (0, i))
    sink = pl.BlockSpec(memory_space=pltpu.SMEM)
    return q, prev(4), cur(4), prev(5), cur(5), bias, stat, sink


def _group_lanes(ref, g, rows_per_head):
    h0 = g * SWA_GROUP
    return jnp.concatenate([ref[(h0 + j) * rows_per_head:(h0 + j + 1) * rows_per_head, :] for j in range(SWA_GROUP)], axis=1)


def _swa_scores(g, q_ref, kp_ref, kc_ref, b_ref, masks):
    rows = slice(g * HEAD_DIM, (g + 1) * HEAD_DIM)
    qg = _group_lanes(q_ref, g, HEAD_DIM)
    scale = HEAD_DIM ** -0.5
    s_p = jnp.where(masks[0], _dot(kp_ref[rows, :], qg, TN) * scale + b_ref[g, 0:WINDOW, :], NEG_INF)
    s_c = jnp.where(masks[1], _dot(kc_ref[rows, :], qg, TN) * scale + b_ref[g, WINDOW:2 * WINDOW, :], NEG_INF)
    return qg, rows, s_p, s_c


def _sink_row(sink_ref, g):
    return jnp.concatenate([jnp.full((1, WINDOW), sink_ref[g * SWA_GROUP + j], F32) for j in range(SWA_GROUP)], axis=1)


def _swa_fwd(qkv, bias_g, sinks, *, name):
    S = qkv.shape[1]
    qs, kp, kc, vp, vc, bs, stat, sk = _swa_specs()
    gs = range(SWA_KV_HEADS)

    def body(sink_ref, q_ref, kp_ref, kc_ref, vp_ref, vc_ref, b_ref, o_ref, lse_ref):
        masks = _swa_masks(pl.program_id(0))
        sc = [_swa_scores(g, q_ref, kp_ref, kc_ref, b_ref, masks) for g in gs]
        sinks_g = [_sink_row(sink_ref, g) for g in gs]
        m = [jnp.maximum(jnp.maximum(jnp.max(sc[g][2], axis=0, keepdims=True), jnp.max(sc[g][3], axis=0, keepdims=True)),
                         sinks_g[g]) for g in gs]
        p_p = [jnp.exp(sc[g][2] - m[g]) for g in gs]
        p_c = [jnp.exp(sc[g][3] - m[g]) for g in gs]
        l = [jnp.sum(p_p[g], axis=0, keepdims=True) + jnp.sum(p_c[g], axis=0, keepdims=True) + jnp.exp(sinks_g[g] - m[g])
             for g in gs]
        o = [_dot(vp_ref[sc[g][1], :], p_p[g].astype(BF16)) + _dot(vc_ref[sc[g][1], :], p_c[g].astype(BF16)) for g in gs]
        for g in gs:
            og = o[g] / l[g]
            lse = m[g] + jnp.log(l[g])
            for j in range(SWA_GROUP):
                h = g * SWA_GROUP + j
                o_ref[h * HEAD_DIM:(h + 1) * HEAD_DIM, :] = og[:, j * WINDOW:(j + 1) * WINDOW]
                lse_ref[h:h + 1, :] = lse[:, j * WINDOW:(j + 1) * WINDOW]

    return pl.pallas_call(
        body, name=name, grid=(S // WINDOW,),
        in_specs=[sk, qs, kp, kc, vp, vc, bs],
        out_specs=[qs, stat],
        out_shape=[jax.ShapeDtypeStruct((SWA_Q_HEADS * HEAD_DIM, S), F32), jax.ShapeDtypeStruct((SWA_Q_HEADS, S), F32)],
        compiler_params=_params("parallel"),
    )(sinks, qkv, qkv, qkv, qkv, qkv, bias_g)


def _swa_bwd(qkv, bias_g, sinks, do, lse, delta, *, name):
    S = qkv.shape[1]
    W = WINDOW
    qs, kp, kc, vp, vc, bs, stat, sk = _swa_specs()
    scale = HEAD_DIM ** -0.5
    kv_rows = SWA_KV_HEADS * HEAD_DIM
    gs = range(SWA_KV_HEADS)

    def body(sink_ref, q_ref, kp_ref, kc_ref, vp_ref, vc_ref, b_ref, do_ref, lse_ref, dl_ref,
             dq_ref, dkv_ref, db_ref, dsk_ref):
        i = pl.program_id(0)

        @pl.when(i == 0)
        def _():
            dkv_ref[...] = jnp.zeros_like(dkv_ref)
            db_ref[...] = jnp.zeros_like(db_ref)
            dsk_ref[...] = jnp.zeros_like(dsk_ref)

        masks = _swa_masks(i)
        prev = pl.ds(pl.multiple_of(jnp.maximum(i - 1, 0) * W, W), W)
        cur = pl.ds(pl.multiple_of(i * W, W), W)
        sc = [_swa_scores(g, q_ref, kp_ref, kc_ref, b_ref, masks) for g in gs]
        dog = [_group_lanes(do_ref, g, HEAD_DIM) for g in gs]
        lse = [_group_lanes(lse_ref, g, 1) for g in gs]
        dl = [_group_lanes(dl_ref, g, 1) for g in gs]
        p_p = [jnp.exp(sc[g][2] - lse[g]) for g in gs]
        p_c = [jnp.exp(sc[g][3] - lse[g]) for g in gs]
        ds_p = [p_p[g] * (_dot(vp_ref[sc[g][1], :], dog[g], TN) - dl[g]) for g in gs]
        ds_c = [p_c[g] * (_dot(vc_ref[sc[g][1], :], dog[g], TN) - dl[g]) for g in gs]
        for g in gs:
            db_ref[g, 0:W, :] += ds_p[g]
            db_ref[g, W:2 * W, :] += ds_c[g]
            dsk = jnp.exp(_sink_row(sink_ref, g) - lse[g]) * dl[g]
            for j in range(SWA_GROUP):
                h = g * SWA_GROUP + j
                dsk_ref[h:h + 1, :] -= jnp.broadcast_to(jnp.sum(dsk[:, j * W:(j + 1) * W], axis=1, keepdims=True), (1, LANES))
        dsb_p = [d.astype(BF16) for d in ds_p]
        dsb_c = [d.astype(BF16) for d in ds_c]
        for g in gs:
            rows = sc[g][1]
            dq = (_dot(kp_ref[rows, :], dsb_p[g]) + _dot(kc_ref[rows, :], dsb_c[g])) * scale
            for j in range(SWA_GROUP):
                h = g * SWA_GROUP + j
                dq_ref[h * HEAD_DIM:(h + 1) * HEAD_DIM, :] = dq[:, j * W:(j + 1) * W]
        for g in gs:
            rows = sc[g][1]
            vrows = slice(kv_rows + rows.start, kv_rows + rows.stop)
            dkv_ref[rows, prev] += _dot(sc[g][0], dsb_p[g], NT) * scale
            dkv_ref[rows, cur] += _dot(sc[g][0], dsb_c[g], NT) * scale
            dkv_ref[vrows, prev] += _dot(dog[g], p_p[g].astype(BF16), NT)
            dkv_ref[vrows, cur] += _dot(dog[g], p_c[g].astype(BF16), NT)

    return pl.pallas_call(
        body, name=name, grid=(S // W,),
        in_specs=[sk, qs, kp, kc, vp, vc, bs, qs, stat, stat],
        out_specs=[qs, pl.BlockSpec((2 * kv_rows, S), lambda i: (0, 0)), bs, pl.BlockSpec((SWA_Q_HEADS, LANES), lambda i: (0, 0))],
        out_shape=[jax.ShapeDtypeStruct((SWA_Q_HEADS * HEAD_DIM, S), F32), jax.ShapeDtypeStruct((2 * kv_rows, S), F32),
                   jax.ShapeDtypeStruct((SWA_KV_HEADS, 2 * W, GW), F32), jax.ShapeDtypeStruct((SWA_Q_HEADS, LANES), F32)],
        compiler_params=_params("arbitrary"),
    )(sinks, qkv, qkv, qkv, qkv, qkv, bias_g, do, lse, delta)


def _rel_onehot_t():
    qi = jnp.arange(WINDOW, dtype=jnp.int32)[None, :] + WINDOW
    kj = jnp.arange(2 * WINDOW, dtype=jnp.int32)[:, None]
    dist = qi - kj
    max_exact = REL_BUCKETS // 2
    d = jnp.maximum(dist, 0)
    log_ratio = jnp.log(jnp.maximum(d, 1).astype(F32) / max_exact) / math.log(REL_MAX_DIST / max_exact)
    large = jnp.minimum(max_exact + (log_ratio * (REL_BUCKETS - max_exact)).astype(jnp.int32), REL_BUCKETS - 1)
    bucket = jnp.where(d < max_exact, d, large).reshape(-1)
    return (bucket[None, :] == jnp.arange(REL_BUCKETS, dtype=jnp.int32)[:, None]).astype(BF16)


def _bias_table(rel_bias_t, onehot_t):
    Hq, NB = rel_bias_t.shape
    N = onehot_t.shape[1]
    tn = _tile(N, 4096)

    def body(r_ref, oh_ref, o_ref):
        oh = oh_ref[...]
        a1, a2, a3 = _split3(r_ref[...])
        o_ref[...] = _dot(a1, oh) + _dot(a2, oh) + _dot(a3, oh)

    return pl.pallas_call(
        body, name="rel_bias_table", grid=(N // tn,),
        in_specs=[pl.BlockSpec((Hq, NB), lambda j: (0, 0)), pl.BlockSpec((NB, tn), lambda j: (0, j))],
        out_specs=pl.BlockSpec((Hq, tn), lambda j: (0, j)),
        out_shape=jax.ShapeDtypeStruct((Hq, N), F32),
        compiler_params=_params("parallel"),
    )(rel_bias_t, onehot_t)


def _bias_table_bwd(dbias, onehot_t):
    L, Hq, N = dbias.shape
    NB = onehot_t.shape[0]
    tn = _tile(N, 4096)

    def body(d_ref, oh_ref, o_ref):
        @pl.when(pl.program_id(0) == 0)
        def _():
            o_ref[...] = jnp.zeros_like(o_ref)

        d = d_ref[0]
        for l in range(1, L):
            d = d + d_ref[l]
        oh = oh_ref[...]
        a1, a2, a3 = _split3(d)
        o_ref[...] += _dot(a1, oh, NT) + _dot(a2, oh, NT) + _dot(a3, oh, NT)

    return pl.pallas_call(
        body, name="rel_bias_bwd", grid=(N // tn,),
        in_specs=[pl.BlockSpec((L, Hq, tn), lambda j: (0, 0, j)), pl.BlockSpec((NB, tn), lambda j: (0, j))],
        out_specs=pl.BlockSpec((Hq, NB), lambda j: (0, 0)),
        out_shape=jax.ShapeDtypeStruct((Hq, NB), F32),
        compiler_params=_params("arbitrary"),
    )(dbias, onehot_t)


def _gate_fwd(lat, fb_col, *, name):
    S = lat.shape[1]
    tn = _tile(S, 256)

    def body(z_ref, fb_ref, o_ref, carry):
        @pl.when(pl.program_id(0) == 0)
        def _():
            carry[...] = jnp.zeros_like(carry)

        z = z_ref[...] + fb_ref[...]
        lf = jnp.minimum(z, 0.0) - jnp.log1p(jnp.exp(-jnp.abs(z)))
        r = lax.broadcasted_iota(jnp.int32, (tn, tn), 0)
        c = lax.broadcasted_iota(jnp.int32, (tn, tn), 1)
        tri = (r <= c).astype(BF16)
        a1, a2, a3 = _split3(lf)
        cum = _dot(a1, tri) + _dot(a2, tri) + _dot(a3, tri) + carry[:, 0:1]
        o_ref[...] = cum
        carry[...] = jnp.broadcast_to(cum[:, tn - 1:tn], carry.shape)

    return pl.pallas_call(
        body, name=name, grid=(S // tn,),
        in_specs=[pl.BlockSpec((GATE_ROWS, tn), lambda i: (0, i)), pl.BlockSpec((GATE_ROWS, 1), lambda i: (0, 0))],
        out_specs=pl.BlockSpec((GATE_ROWS, tn), lambda i: (0, i)),
        out_shape=jax.ShapeDtypeStruct((GATE_ROWS, S), F32),
        scratch_shapes=[pltpu.VMEM((GATE_ROWS, LANES), F32)],
        compiler_params=_params("arbitrary"),
    )(lat, fb_col)


def _gate_bwd(lat, fb_col, dF, *, name):
    S = lat.shape[1]
    tn = _tile(S, 256)
    nt = S // tn

    def body(z_ref, fb_ref, df_ref, dz_ref, dfb_ref, carry):
        @pl.when(pl.program_id(0) == 0)
        def _():
            carry[...] = jnp.zeros_like(carry)
            dfb_ref[...] = jnp.zeros_like(dfb_ref)

        r = lax.broadcasted_iota(jnp.int32, (tn, tn), 0)
        c = lax.broadcasted_iota(jnp.int32, (tn, tn), 1)
        tri = (r >= c).astype(BF16)
        a1, a2, a3 = _split3(df_ref[...])
        dlf = _dot(a1, tri) + _dot(a2, tri) + _dot(a3, tri) + carry[:, 0:1]
        carry[...] = jnp.broadcast_to(dlf[:, 0:1], carry.shape)
        z = z_ref[...] + fb_ref[...]
        row = lax.broadcasted_iota(jnp.int32, (GATE_ROWS, tn), 0)
        dz = jnp.where(row < FOX_HEADS, dlf / (1.0 + jnp.exp(z)), 0.0)
        dz_ref[...] = dz
        dfb_ref[...] += jnp.sum(dz, axis=1, keepdims=True)

    blk = pl.BlockSpec((GATE_ROWS, tn), lambda i: (0, nt - 1 - i))
    vec = pl.BlockSpec((GATE_ROWS, 1), lambda i: (0, 0))
    return pl.pallas_call(
        body, name=name, grid=(nt,),
        in_specs=[blk, vec, blk], out_specs=[blk, vec],
        out_shape=[jax.ShapeDtypeStruct((GATE_ROWS, S), F32), jax.ShapeDtypeStruct((GATE_ROWS, 1), F32)],
        scratch_shapes=[pltpu.VMEM((GATE_ROWS, LANES), F32)],
        compiler_params=_params("arbitrary"),
    )(lat, fb_col, dF)


def _rope_tables(S):
    pos = jnp.arange(S, dtype=F32)
    inv_freq = ROPE_THETA ** (-(jnp.arange(MLA_ROPE // 2, dtype=F32) * 2.0 / MLA_ROPE))
    ang = pos[:, None] * inv_freq[None, :]
    cos, sin = jnp.cos(ang).T, jnp.sin(ang).T
    z16 = jnp.zeros_like(cos)

    def slab(lo, fill):
        def put(first, second, f):
            return jnp.concatenate([jnp.full((lo, S), f, F32), first, second, jnp.full((LANES - lo - MLA_ROPE, S), f, F32)], axis=0)
        return put(cos, cos, fill), put(-sin, z16, 0.0), put(z16, sin, 0.0)

    tq = tuple(jnp.tile(t, (MLA_HEADS, 1)) for t in slab(MLA_NOPE, 1.0))
    return tq, slab(0, 0.0)


def _rope(x, c, s1, s2):
    n = x.shape[0]
    half = MLA_ROPE // 2
    return x * c + pltpu.roll(x, n - half, 0) * s1 + pltpu.roll(x, half, 0) * s2


def _rope_t(dy, c, s1, s2):
    n = dy.shape[0]
    half = MLA_ROPE // 2
    return dy * c + pltpu.roll(dy * s1, half, 0) + pltpu.roll(dy * s2, n - half, 0)


KR_SLAB0 = MLA_Q_RANK + MLA_KV_RANK


def _mla_prep_fwd(lat, g_q, g_kv, w_uq_t, w_ukv_t, tq, tmisc, *, name):
    S = lat.shape[1]
    tn = _tile(S, 512)
    QW = MLA_HEADS * MLA_PAD

    def body(lat_ref, gq_ref, gkv_ref, wq_ref, wkv_ref, c_ref, s1_ref, s2_ref, cm_ref, s1m_ref, s2m_ref,
             nq_ref, nkv_ref, q_ref, k_ref, v_ref):
        x = pltpu.roll(lat_ref[...], LAT_ROWS - LAT_SHIFT, 0)
        nq = _col_rms(x[0:MLA_Q_RANK, :], gq_ref[...]).astype(BF16)
        nkv = _col_rms(x[MLA_Q_RANK:KR_SLAB0, :], gkv_ref[...]).astype(BF16)
        nq_ref[...] = nq
        nkv_ref[...] = nkv
        q = _rope(_dot(wq_ref[...], nq), c_ref[...], s1_ref[...], s2_ref[...])
        q_ref[...] = (q * (MLA_SCALE * LOG2E)).astype(BF16)
        kv = _dot(wkv_ref[...], nkv).astype(BF16)
        kr = _rope(x[KR_SLAB0:LAT_ROWS, :], cm_ref[...], s1m_ref[...], s2m_ref[...]).astype(BF16)
        for h in range(MLA_HEADS):
            k_ref[h * MLA_PAD:h * MLA_PAD + MLA_NOPE, :] = kv[h * LANES:h * LANES + MLA_NOPE, :]
            k_ref[h * MLA_PAD + MLA_NOPE:(h + 1) * MLA_PAD, :] = kr[0:MLA_PAD - MLA_NOPE, :]
            v_ref[h * HEAD_DIM:(h + 1) * HEAD_DIM, :] = kv[h * LANES + MLA_NOPE:(h + 1) * LANES, :]

    def col(rows):
        return pl.BlockSpec((rows, tn), lambda i: (0, i))

    def full(a):
        return pl.BlockSpec(a.shape, lambda i: (0, 0))

    return pl.pallas_call(
        body, name=name, grid=(S // tn,),
        in_specs=[col(LAT_ROWS), full(g_q), full(g_kv), full(w_uq_t), full(w_ukv_t),
                  col(QW), col(QW), col(QW), col(LANES), col(LANES), col(LANES)],
        out_specs=[col(MLA_Q_RANK), col(MLA_KV_RANK), col(QW), col(QW), col(MLA_HEADS * HEAD_DIM)],
        out_shape=[jax.ShapeDtypeStruct((MLA_Q_RANK, S), BF16), jax.ShapeDtypeStruct((MLA_KV_RANK, S), BF16),
                   jax.ShapeDtypeStruct((QW, S), BF16), jax.ShapeDtypeStruct((QW, S), BF16),
                   jax.ShapeDtypeStruct((MLA_HEADS * HEAD_DIM, S), BF16)],
        compiler_params=_params("parallel"),
    )(lat, g_q, g_kv, w_uq_t, w_ukv_t, *tq, *tmisc)


def _mla_prep_bwd(lat, nq, nkv, g_q, g_kv, w_uq_p, w_ukv, tq, tmisc, dq, dk, dv, dflog, *, name):
    S = lat.shape[1]
    tn = _tile(S, 512)
    QW = MLA_HEADS * MLA_PAD

    def body(lat_ref, nq_ref, nkv_ref, gq_ref, gkv_ref, wq_ref, wkv_ref, c_ref, s1_ref, s2_ref,
             cm_ref, s1m_ref, s2m_ref, dq_ref, dk_ref, dv_ref, dfl_ref,
             dlat_ref, dwq_ref, dwkv_ref, dgq_ref, dgkv_ref, y_s):
        @pl.when(pl.program_id(0) == 0)
        def _():
            dwq_ref[...] = jnp.zeros_like(dwq_ref)
            dwkv_ref[...] = jnp.zeros_like(dwkv_ref)
            dgq_ref[...] = jnp.zeros_like(dgq_ref)
            dgkv_ref[...] = jnp.zeros_like(dgkv_ref)

        x = pltpu.roll(lat_ref[...], LAT_ROWS - LAT_SHIFT, 0)
        dqm = _rope_t(dq_ref[...], c_ref[...], s1_ref[...], s2_ref[...]).astype(BF16)
        dwq_ref[...] += _dot(dqm, nq_ref[...], NT)
        dx, dg = _col_rms_bwd(x[0:MLA_Q_RANK, :], gq_ref[...], _dot(wq_ref[...], dqm))
        y_s[0:MLA_Q_RANK, :] = dx
        dgq_ref[...] += dg
        dkv = jnp.concatenate(
            [part for h in range(MLA_HEADS)
             for part in (dk_ref[h * MLA_PAD:h * MLA_PAD + MLA_NOPE, :], dv_ref[h * HEAD_DIM:(h + 1) * HEAD_DIM, :])],
            axis=0).astype(BF16)
        dwkv_ref[...] += _dot(dkv, nkv_ref[...], NT)
        dx, dg = _col_rms_bwd(x[MLA_Q_RANK:KR_SLAB0, :], gkv_ref[...], _dot(wkv_ref[...], dkv))
        y_s[MLA_Q_RANK:KR_SLAB0, :] = dx
        dgkv_ref[...] += dg
        dkr = dk_ref[MLA_NOPE:MLA_PAD, :]
        for h in range(1, MLA_HEADS):
            dkr = dkr + dk_ref[h * MLA_PAD + MLA_NOPE:(h + 1) * MLA_PAD, :]
        dkr = jnp.concatenate([dkr, jnp.zeros((MLA_NOPE, tn), F32)], axis=0)
        y_s[KR_SLAB0:LAT_ROWS, :] = _rope_t(dkr, cm_ref[...], s1m_ref[...], s2m_ref[...])
        y = pltpu.roll(y_s[...], LAT_SHIFT, 0)
        row = lax.broadcasted_iota(jnp.int32, (LAT_ROWS, tn), 0)
        dfl = jnp.concatenate([dfl_ref[...], jnp.zeros((LAT_ROWS - GATE_ROWS, tn), F32)], axis=0)
        dlat_ref[...] = jnp.where(row < LAT_SHIFT, dfl, y).astype(BF16)

    def col(rows):
        return pl.BlockSpec((rows, tn), lambda i: (0, i))

    def full(a):
        return pl.BlockSpec(a.shape, lambda i: (0, 0))

    def acc(r, c):
        return pl.BlockSpec((r, c), lambda i: (0, 0))

    return pl.pallas_call(
        body, name=name, grid=(S // tn,),
        in_specs=[col(LAT_ROWS), col(MLA_Q_RANK), col(MLA_KV_RANK), full(g_q), full(g_kv),
                  full(w_uq_p), full(w_ukv), col(QW), col(QW), col(QW), col(LANES), col(LANES), col(LANES),
                  col(QW), col(QW), col(MLA_HEADS * HEAD_DIM), col(GATE_ROWS)],
        out_specs=[col(LAT_ROWS), acc(QW, MLA_Q_RANK), acc(QW, MLA_KV_RANK), acc(MLA_Q_RANK, 1), acc(MLA_KV_RANK, 1)],
        out_shape=[jax.ShapeDtypeStruct((LAT_ROWS, S), BF16), jax.ShapeDtypeStruct((QW, MLA_Q_RANK), F32),
                   jax.ShapeDtypeStruct((QW, MLA_KV_RANK), F32), jax.ShapeDtypeStruct((MLA_Q_RANK, 1), F32),
                   jax.ShapeDtypeStruct((MLA_KV_RANK, 1), F32)],
        scratch_shapes=[pltpu.VMEM((LAT_ROWS, tn), F32)],
        compiler_params=_params("arbitrary"),
    )(lat, nq, nkv, g_q, g_kv, w_uq_p, w_ukv, *tq, *tmisc, dq, dk, dv, dflog)


def _dproj_cast(dqa, dkva, dqf, dkf, dvf, dlat, *, name):
    S = dqa.shape[1]
    tn = _tile(S, 512)
    parts = (dqa, dkva, dqf, dkf, dvf, dlat)

    def body(*refs):
        o_ref = refs[-1]
        r0 = 0
        for ref in refs[:-1]:
            n = ref.shape[0]
            o_ref[r0:r0 + n, :] = ref[...].astype(BF16)
            r0 += n

    return pl.pallas_call(
        body, name=name, grid=(S // tn,),
        in_specs=[pl.BlockSpec((p.shape[0], tn), lambda i: (0, i)) for p in parts],
        out_specs=pl.BlockSpec((IN_ROWS, tn), lambda i: (0, i)),
        out_shape=jax.ShapeDtypeStruct((IN_ROWS, S), BF16),
        compiler_params=_params("parallel"),
    )(*parts)


GELU_C = math.sqrt(2.0 / math.pi)
GELU_A = 0.044715


HALO = 16


def _shift_down(a, k, fill):
    r = pltpu.roll(a, k, 0)
    row = lax.broadcasted_iota(jnp.int32, (8, a.shape[1]), 0)
    head = r[0:8, :]
    for i in range(k):
        head = jnp.where(row == i, fill[len(fill) - k + i], head)
    return jnp.concatenate([head, r[8:, :]], axis=0)


def _shift_up(d, k, fill):
    n = d.shape[0]
    r = pltpu.roll(d, n - k, 0)
    row = lax.broadcasted_iota(jnp.int32, (8, d.shape[1]), 0)
    tail = r[n - 8:n, :]
    for i in range(k):
        tail = jnp.where(row == 8 - k + i, fill[i], tail)
    return jnp.concatenate([r[0:n - 8, :], tail], axis=0)


def _conv_taps(a, before, w_ref, b_ref):
    a1 = _shift_down(a, 1, before)
    a2 = _shift_down(a, 2, before)
    return ((b_ref[...] + w_ref[0:1, :] * a2) + w_ref[1:2, :] * a1) + w_ref[2:3, :] * a


def _rows_before(halo_ref, first):
    h = halo_ref[HALO - 2:HALO, :].astype(F32)
    return jnp.where(first, 0.0, h[0:1, :]), jnp.where(first, 0.0, h[1:2, :])


def _conv_specs(S, tm, tc, nc):
    hb = tm // HALO
    main = lambda off: pl.BlockSpec((tm, tc), lambda j, i: (i, j + off))
    prev = lambda off: pl.BlockSpec((HALO, tc), lambda j, i: (jnp.maximum(i * hb - 1, 0), j + off))
    wspec = lambda off: pl.BlockSpec((3, tc), lambda j, i: (0, j + off))
    bspec = lambda off: pl.BlockSpec((1, tc), lambda j, i: (0, j + off))
    return main, prev, wspec, bspec


def _conv_geglu_fwd(a, conv_w, conv_b, *, name):
    S = a.shape[0]
    tm, tc = _tile(S, 512), _tile(D_FF, 1408)
    nc = D_FF // tc
    main, prev, wspec, bspec = _conv_specs(S, tm, tc, nc)

    def body(ag_ref, au_ref, hg_ref, hu_ref, wg_ref, wu_ref, bg_ref, bu_ref, u_ref, z_ref):
        first = pl.program_id(1) == 0
        gate = _conv_taps(ag_ref[...].astype(F32), _rows_before(hg_ref, first), wg_ref, bg_ref)
        up = _conv_taps(au_ref[...].astype(F32), _rows_before(hu_ref, first), wu_ref, bu_ref)
        u_ref[0] = gate
        u_ref[1] = up
        cdf = 0.5 * (1.0 + jnp.tanh(GELU_C * (gate + GELU_A * (gate * gate * gate))))
        z_ref[...] = (gate * cdf * up).astype(BF16)

    return pl.pallas_call(
        body, name=name, grid=(nc, S // tm),
        in_specs=[main(0), main(nc), prev(0), prev(nc), wspec(0), wspec(nc), bspec(0), bspec(nc)],
        out_specs=[pl.BlockSpec((2, tm, tc), lambda j, i: (0, i, j)), pl.BlockSpec((tm, tc), lambda j, i: (i, j))],
        out_shape=[jax.ShapeDtypeStruct((2, S, D_FF), F32), jax.ShapeDtypeStruct((S, D_FF), BF16)],
        compiler_params=_params("parallel", "arbitrary"),
    )(a, a, a, a, conv_w, conv_w, conv_b, conv_b)


def _geglu_bwd(gate, up, dz):
    g2x = gate * gate
    th = jnp.tanh(GELU_C * (gate + GELU_A * (g2x * gate)))
    cdf = 0.5 * (1.0 + th)
    dgelu = cdf + gate * (0.5 * (1.0 - th * th) * (GELU_C * (1.0 + 3.0 * GELU_A * g2x)))
    return dz * up * dgelu, dz * (gate * cdf)


def _conv_geglu_bwd(a, u, conv_w, dz, *, name):
    S = a.shape[0]
    tm, tc = _tile(S, 512), _tile(D_FF, 1408)
    nc = D_FF // tc
    nr = S // tm
    main, _, wspec, _ = _conv_specs(S, tm, tc, nc)
    hb = tm // 8

    def body(ag_ref, au_ref, u_ref, un_ref, wg_ref, wu_ref, dz_ref, dzn_ref, da_ref, dw_ref, db_ref):
        i = pl.program_id(1)
        last = i == nr - 1

        @pl.when(i == 0)
        def _():
            dw_ref[...] = jnp.zeros_like(dw_ref)
            db_ref[...] = jnp.zeros_like(db_ref)

        dus = _geglu_bwd(u_ref[0], u_ref[1], dz_ref[...])
        dus_n = _geglu_bwd(un_ref[0], un_ref[1], dzn_ref[...])
        for half, a_ref, w_ref in ((0, ag_ref, wg_ref), (1, au_ref, wu_ref)):
            du, du_n = dus[half], dus_n[half]
            after = (jnp.where(last, 0.0, du_n[0:1, :]), jnp.where(last, 0.0, du_n[1:2, :]))
            shifted = (_shift_up(du, 2, after), _shift_up(du, 1, after), du)
            da_ref[half] = (w_ref[2:3, :] * du + w_ref[1:2, :] * shifted[1] + w_ref[0:1, :] * shifted[0]).astype(BF16)
            af = a_ref[...].astype(F32)
            for tap in range(3):
                dw_ref[half, tap:tap + 1, :] += jnp.sum(shifted[tap] * af, axis=0, keepdims=True)
            db_ref[half] += jnp.sum(du, axis=0, keepdims=True)

    nxt8 = lambda j, i: (0, jnp.minimum((i + 1) * hb, S // 8 - 1), j)
    return pl.pallas_call(
        body, name=name, grid=(nc, nr),
        in_specs=[main(0), main(nc), pl.BlockSpec((2, tm, tc), lambda j, i: (0, i, j)), pl.BlockSpec((2, 8, tc), nxt8),
                  wspec(0), wspec(nc), pl.BlockSpec((tm, tc), lambda j, i: (i, j)),
                  pl.BlockSpec((8, tc), lambda j, i: (jnp.minimum((i + 1) * hb, S // 8 - 1), j))],
        out_specs=[pl.BlockSpec((2, tm, tc), lambda j, i: (0, i, j)), pl.BlockSpec((2, 3, tc), lambda j, i: (0, 0, j)),
                   pl.BlockSpec((2, 1, tc), lambda j, i: (0, 0, j))],
        out_shape=[jax.ShapeDtypeStruct((2, S, D_FF), BF16), jax.ShapeDtypeStruct((2, 3, D_FF), F32),
                   jax.ShapeDtypeStruct((2, 1, D_FF), F32)],
        compiler_params=_params("parallel", "arbitrary"),
    )(a, a, u, u, conv_w, conv_w, dz, dz)


ROW_BLOCK_BYTES = 1536 * 1024


def _row_tile(rows, cols):
    return rows if rows * cols * 4 <= ROW_BLOCK_BYTES else _tile(rows, ROW_TILE)


def _adamw_update(w, g, m, v):
    m = ADAM_B1 * m + (1.0 - ADAM_B1) * g
    v = ADAM_B2 * v + (1.0 - ADAM_B2) * jnp.square(g)
    m_hat = m / (1.0 - ADAM_B1 ** ADAM_STEP)
    v_hat = v / (1.0 - ADAM_B2 ** ADAM_STEP)
    return -ADAM_LR * (m_hat / (jnp.sqrt(v_hat) + ADAM_EPS) + ADAM_WD * w), m, v


def _adamw(w, g, m, v, *, name):
    L, A, B = w.shape
    ta = _tile(A, ROW_TILE)

    def body(w_ref, g_ref, m_ref, v_ref, d_ref, mo_ref, vo_ref):
        d_ref[...], mo_ref[...], vo_ref[...] = _adamw_update(w_ref[...], g_ref[...], m_ref[...], v_ref[...])

    blk = pl.BlockSpec((None, ta, B), lambda l, i: (l, i, 0))
    shp = jax.ShapeDtypeStruct((L, A, B), F32)
    return pl.pallas_call(
        body, name=name, grid=(L, A // ta),
        in_specs=[blk] * 4, out_specs=[blk] * 3, out_shape=[shp] * 3,
        compiler_params=_params("parallel", "parallel"),
    )(w, g, m, v)


def _scalar(v):
    return jnp.reshape(v, (1,)).astype(jnp.int32)


def _adamw_halves(w, g_mine, g_other, m, v, *, name):
    L, A, B = w.shape
    ta = _row_tile(A // 2, B)
    nb = A // 2 // ta

    def body(c_ref, w_ref, gm_ref, go_ref, m_ref, v_ref, g_ref, d_ref, mo_ref, vo_ref):
        g = jnp.where(pl.program_id(1) // nb == c_ref[0], gm_ref[...], go_ref[...])
        g_ref[...] = g
        d_ref[...], mo_ref[...], vo_ref[...] = _adamw_update(w_ref[...], g, m_ref[...], v_ref[...])

    blk = pl.BlockSpec((None, ta, B), lambda l, i, c_ref: (l, i, 0))
    half = pl.BlockSpec((None, ta, B), lambda l, i, c_ref: (l, i % nb, 0))
    shp = jax.ShapeDtypeStruct((L, A, B), F32)
    return pl.pallas_call(
        body, name=name,
        grid_spec=pltpu.PrefetchScalarGridSpec(num_scalar_prefetch=1, grid=(L, A // ta),
                                               in_specs=[blk, half, half, blk, blk], out_specs=[blk] * 4),
        out_shape=[shp] * 4,
        compiler_params=_params("parallel", "parallel"),
    )(_scalar(lax.axis_index("c")), w, g_mine, g_other, m, v)


def _chip_index():
    return 2 * lax.axis_index("x") + lax.axis_index("y")


def _pair_sum(g, recv, *, name):
    n, A, B = g.shape
    ta = _row_tile(A // 2, B)
    nb = A // 2 // ta

    def body(c_ref, g_ref, r_ref, o_ref):
        o_ref[...] = g_ref[...] + r_ref[...]

    return pl.pallas_call(
        body, name=name,
        grid_spec=pltpu.PrefetchScalarGridSpec(
            num_scalar_prefetch=1, grid=(n, nb),
            in_specs=[pl.BlockSpec((None, ta, B), lambda s, r, c_ref: (s, c_ref[0] * nb + r, 0)),
                      pl.BlockSpec((None, ta, B), lambda s, r, c_ref: (s, r, 0))],
            out_specs=pl.BlockSpec((None, ta, B), lambda s, r, c_ref: (s, r, 0))),
        out_shape=jax.ShapeDtypeStruct((n, A // 2, B), F32),
        compiler_params=_params("parallel", "parallel"),
    )(_scalar(lax.axis_index("c")), g, recv)


def _chip_sum(landed, own, *, name):
    n, A2, B = landed.shape
    ta = _row_tile(A2, B)

    def body(me_ref, *refs):
        slots, own_ref, o_ref = refs[:n], refs[n], refs[n + 1]
        parts = [jnp.where(me_ref[0] == s, own_ref[...], slots[s][...]) for s in range(n)]
        o_ref[...] = ((parts[0] + parts[1]) + parts[2]) + parts[3]

    def slot(s):
        return pl.BlockSpec((None, ta, B), lambda r, me_ref: (jnp.where(me_ref[0] == s, (s + 1) % n, s), r, 0))

    return pl.pallas_call(
        body, name=name,
        grid_spec=pltpu.PrefetchScalarGridSpec(
            num_scalar_prefetch=1, grid=(A2 // ta,),
            in_specs=[slot(s) for s in range(n)] + [pl.BlockSpec((None, ta, B), lambda r, me_ref: (me_ref[0], r, 0))],
            out_specs=pl.BlockSpec((ta, B), lambda r, me_ref: (r, 0))),
        out_shape=jax.ShapeDtypeStruct((A2, B), F32),
        compiler_params=_params("parallel"),
    )(_scalar(_chip_index()), *([landed] * n), own)


HBM_SPEC = pl.BlockSpec(memory_space=pl.ANY)
COMM_PARAMS = pltpu.CompilerParams(has_side_effects=True)


def _mesh_pos():
    return lax.axis_index("x"), lax.axis_index("y"), lax.axis_index("c")


def _other_chips(x, y):
    return [(1 - x, y), (x, 1 - y), (1 - x, 1 - y)]


def _remote(src, dst, send_sems, recv_sems, k, to):
    return pltpu.make_async_remote_copy(src_ref=src, dst_ref=dst, send_sem=send_sems.at[k], recv_sem=recv_sems.at[k],
                                        device_id=to, device_id_type=MESH)


def _place_own(gathered, shards, *, name):
    n = len(shards)

    def body(me_ref, *refs):
        for s_ref, o_ref in zip(refs[:n], refs[2 * n:]):
            o_ref[...] = s_ref[...]

    return pl.pallas_call(
        body, name=name,
        grid_spec=pltpu.PrefetchScalarGridSpec(
            num_scalar_prefetch=1, grid=(1,),
            in_specs=[pl.BlockSpec(s.shape, lambda i, me_ref: (0, 0)) for s in shards] + [HBM_SPEC] * n,
            out_specs=[pl.BlockSpec((None,) + s.shape, lambda i, me_ref: (me_ref[0], 0, 0)) for s in shards]),
        out_shape=[jax.ShapeDtypeStruct(g.shape, g.dtype) for g in gathered],
        input_output_aliases={1 + n + k: k for k in range(n)},
        compiler_params=_params("arbitrary"),
    )(_scalar(_chip_index()), *shards, *gathered)


def _half_rows(rows, c, align=8):
    assert (rows // 2) % align == 0
    return pl.ds(pl.multiple_of(c * (rows // 2), align), rows // 2)


BF16_ROWS = 16


def _halved(rows):
    return rows % (2 * BF16_ROWS) == 0


def _gather_copies(srcs, lands, send_sems, recv_sems):
    x, y, c = _mesh_pos()
    me = 2 * x + y
    out = []
    for k in range(len(srcs)):
        a = srcs[k].shape[0]
        rows = _half_rows(a, c, BF16_ROWS) if _halved(a) else pl.ds(0, a)
        for j, (px, py) in enumerate(_other_chips(x, y)):
            send = _remote(srcs[k].at[rows], lands[k].at[me, rows], send_sems, recv_sems, 3 * k + j, (px, py, c))
            recv = _remote(srcs[k].at[rows], lands[k].at[2 * px + py, rows], send_sems, recv_sems, 3 * k + j, (px, py, c))
            out.append((send, recv))
    return out


def _gather_start(srcs):
    nu = len(srcs)
    sizes = [len(su) for su in srcs]
    offs = [2 * sum(sizes[:u]) for u in range(nu + 1)]
    lands = [[lax.empty((N_CHIPS,) + s.shape, s.dtype) for s in su] for su in srcs]
    flat = [a for u in range(nu) for a in srcs[u] + lands[u]]

    def body(*refs):
        bufs, sems, token = refs[:len(flat)], refs[len(flat):len(flat) + 2 * nu], refs[-1]
        for u, n in enumerate(sizes):
            mine = bufs[offs[u]:offs[u + 1]]
            for send, _ in _gather_copies(mine[:n], mine[n:], sems[2 * u], sems[2 * u + 1]):
                send.start()
        token[...] = jnp.zeros_like(token)

    res = pl.pallas_call(
        body, name="weight_gather_start",
        in_specs=[HBM_ONLY] * len(flat),
        out_specs=[SEM_SPEC] * (2 * nu) + [HBM_ONLY] * len(flat) + [pl.BlockSpec(memory_space=pltpu.VMEM)],
        out_shape=[pltpu.SemaphoreType.DMA((3 * n,)) for n in sizes for _ in (0, 1)] + [pltpu.HBM(a.shape, a.dtype) for a in flat]
        + [jax.ShapeDtypeStruct((1, 1), F32)],
        input_output_aliases={i: 2 * nu + i for i in range(len(flat))},
        compiler_params=SPLIT_PARAMS,
    )(*[pltpu.with_memory_space_constraint(a, pltpu.HBM) for a in flat])
    bufs = res[2 * nu:2 * nu + len(flat)]
    state = [(res[2 * u], res[2 * u + 1], list(bufs[offs[u]:offs[u] + n]), list(bufs[offs[u] + n:offs[u + 1]]))
             for u, n in enumerate(sizes)]
    return state, res[-1]


def _gather_wait(state, after, *, name):
    send_sems, recv_sems, srcs, lands = state
    n = len(srcs)

    def body(*refs):
        for send, recv in _gather_copies(refs[:n], refs[n:2 * n], refs[2 * n], refs[2 * n + 1]):
            send.wait_send()
            recv.wait_recv()

    res = pl.pallas_call(
        body, name=name,
        in_specs=[HBM_ONLY] * (2 * n) + [SEM_SPEC, SEM_SPEC, HBM_SPEC],
        out_specs=[HBM_ONLY] * (2 * n),
        out_shape=[pltpu.HBM(a.shape, a.dtype) for a in srcs + lands],
        input_output_aliases={i: i for i in range(2 * n)},
        compiler_params=SPLIT_PARAMS,
    )(*srcs, *lands, send_sems, recv_sems, after)
    return list(res[:n]), list(res[n:])


def _gather_forward(lands, *, name):
    n = len(lands)

    def body(*refs):
        bufs, outs = refs[:n], refs[n:2 * n]
        send_sems, recv_sems = refs[2 * n:]
        x, y, c = _mesh_pos()
        copies, waits = [], []
        for k in range(n):
            a = lands[k].shape[1]
            if not _halved(a):
                continue
            for j, (px, py) in enumerate(_other_chips(x, y)):
                mine = 2 * px + py, _half_rows(a, c, BF16_ROWS)
                copies.append(_remote(bufs[k].at[mine], outs[k].at[mine], send_sems, recv_sems, 3 * k + j, (x, y, 1 - c)))
                lands_here = outs[k].at[2 * px + py, _half_rows(a, 1 - c, BF16_ROWS)]
                waits.append(_remote(lands_here, lands_here, send_sems, recv_sems, 3 * k + j, (x, y, 1 - c)))
        for cp in copies:
            cp.start()
        for cp in waits:
            cp.wait_recv()
        for cp in copies:
            cp.wait_send()

    return pl.pallas_call(
        body, name=name,
        in_specs=[HBM_SPEC] * n, out_specs=[HBM_SPEC] * n,
        out_shape=[jax.ShapeDtypeStruct(a.shape, a.dtype) for a in lands],
        scratch_shapes=[pltpu.SemaphoreType.DMA((3 * n,)), pltpu.SemaphoreType.DMA((3 * n,))],
        input_output_aliases={i: i for i in range(n)},
        compiler_params=COMM_PARAMS,
    )(*lands)


def _sibling_exchange(gs, *, name):
    n = len(gs)

    def body(*refs):
        ins, outs = refs[:n], refs[n:2 * n]
        send_sems, recv_sems = refs[2 * n:]
        x, y, c = _mesh_pos()
        copies = [_remote(ins[k].at[:, _half_rows(gs[k].shape[1], 1 - c)], outs[k], send_sems, recv_sems, k, (x, y, 1 - c))
                  for k in range(n)]
        for cp in copies:
            cp.start()
        for cp in copies:
            cp.wait()

    return pl.pallas_call(
        body, name=name,
        in_specs=[HBM_SPEC] * n, out_specs=[HBM_SPEC] * n,
        out_shape=[jax.ShapeDtypeStruct((g.shape[0], g.shape[1] // 2, g.shape[2]), g.dtype) for g in gs],
        scratch_shapes=[pltpu.SemaphoreType.DMA((n,)), pltpu.SemaphoreType.DMA((n,))],
        compiler_params=COMM_PARAMS,
    )(*gs)


HBM_ONLY = pl.BlockSpec(memory_space=pltpu.HBM)
SEM_SPEC = pl.BlockSpec(memory_space=pltpu.SEMAPHORE)
SPLIT_PARAMS = pltpu.CompilerParams(has_side_effects=pltpu.SideEffectType.DATAFLOW_SIDE_EFFECTING)


def _scatter_copies(srcs, lands, send_sems, recv_sems):
    x, y, c = _mesh_pos()
    me = 2 * x + y
    out = []
    for k in range(len(srcs)):
        for j, (px, py) in enumerate(_other_chips(x, y)):
            s = 2 * px + py
            send = _remote(srcs[k].at[s], lands[k].at[me], send_sems, recv_sems, 3 * k + j, (px, py, c))
            recv = _remote(srcs[k].at[s], lands[k].at[s], send_sems, recv_sems, 3 * k + j, (px, py, c))
            out.append((send, recv))
    return out


def _exchange_copies(srcs, lands, send_sems, recv_sems):
    x, y, c = _mesh_pos()
    out = []
    for k in range(len(srcs)):
        cp = _remote(srcs[k].at[:, _half_rows(srcs[k].shape[1], 1 - c)], lands[k], send_sems, recv_sems, k, (x, y, 1 - c))
        out.append((cp, cp))
    return out


def _split_start(srcs, land_shapes, copies, n_sems, *, name):
    n = len(srcs)
    lands = [lax.empty(shape, s.dtype) for shape, s in zip(land_shapes, srcs)]

    def body(*refs):
        ins, zones = refs[:n], refs[n:2 * n]
        send_sems, recv_sems, token = refs[2 * n], refs[2 * n + 1], refs[-1]
        for send, _ in copies(ins, zones, send_sems, recv_sems):
            send.start()
        token[...] = jnp.zeros_like(token)

    hbm = lambda a: pltpu.HBM(a.shape, a.dtype)
    res = pl.pallas_call(
        body, name=name,
        in_specs=[HBM_ONLY] * (2 * n),
        out_specs=[SEM_SPEC, SEM_SPEC] + [HBM_ONLY] * (2 * n) + [pl.BlockSpec(memory_space=pltpu.VMEM)],
        out_shape=[pltpu.SemaphoreType.DMA((n_sems,)), pltpu.SemaphoreType.DMA((n_sems,))] + [hbm(a) for a in srcs + lands]
        + [jax.ShapeDtypeStruct((1, 1), F32)],
        input_output_aliases={i: 2 + i for i in range(2 * n)},
        compiler_params=SPLIT_PARAMS,
    )(*[pltpu.with_memory_space_constraint(a, pltpu.HBM) for a in srcs + lands])
    return (res[0], res[1], list(res[2:2 + n]), list(res[2 + n:2 + 2 * n])), res[-1]


def _scatter_start(ps, *, name):
    return _split_start(ps, [p.shape for p in ps], _scatter_copies, 3 * len(ps), name=name)


def _exchange_start(gs, *, name):
    return _split_start(gs, [(g.shape[0], g.shape[1] // 2, g.shape[2]) for g in gs], _exchange_copies, len(gs), name=name)


def _split_wait(started, copies, after, *, name):
    ng = len(started)
    sizes = [len(st[2]) for st in started]
    offs = [2 * sum(sizes[:i]) for i in range(ng + 1)]
    flat = [a for (_, _, ps, lands) in started for a in ps + lands]

    def body(*refs):
        bufs, sems = refs[:len(flat)], refs[len(flat):len(flat) + 2 * ng]
        for i, n in enumerate(sizes):
            srcs, zones = bufs[offs[i]:offs[i] + n], bufs[offs[i] + n:offs[i + 1]]
            for send, recv in copies(srcs, zones, sems[2 * i], sems[2 * i + 1]):
                send.wait_send()
                recv.wait_recv()

    res = pl.pallas_call(
        body, name=name,
        in_specs=[HBM_ONLY] * len(flat) + [SEM_SPEC] * (2 * ng) + [HBM_SPEC],
        out_specs=[HBM_ONLY] * len(flat),
        out_shape=[pltpu.HBM(a.shape, a.dtype) for a in flat],
        input_output_aliases={i: i for i in range(len(flat))},
        compiler_params=SPLIT_PARAMS,
    )(*flat, *[s for (ss, rs, _, _) in started for s in (ss, rs)], after)
    return [(list(res[offs[i]:offs[i] + n]), list(res[offs[i] + n:offs[i + 1]])) for i, n in enumerate(sizes)]


def _sibling_share(hs):
    n = len(hs)

    def body(*refs):
        ins, outs = refs[:n], refs[n:2 * n]
        send_sems, recv_sems = refs[2 * n:]
        x, y, c = _mesh_pos()
        copies = [_remote(ins[k], outs[k], send_sems, recv_sems, k, (x, y, 1 - c)) for k in range(n)]
        for cp in copies:
            cp.start()
        for cp in copies:
            cp.wait()

    return pl.pallas_call(
        body, name="grad_sibling_share",
        in_specs=[HBM_SPEC] * n, out_specs=[HBM_SPEC] * n,
        out_shape=[jax.ShapeDtypeStruct(h.shape, h.dtype) for h in hs],
        scratch_shapes=[pltpu.SemaphoreType.DMA((n,)), pltpu.SemaphoreType.DMA((n,))],
        compiler_params=COMM_PARAMS,
    )(*hs)


def _allreduce_small(part, by_chip):
    rows, C = part.shape
    rows2 = by_chip.shape[1]

    def body(p_ref, q_ref, o_ref, o2_ref, slots, slots2, send_sems, recv_sems):
        x, y, c = _mesh_pos()
        me = 4 * x + 2 * y + c
        slots[me] = p_ref[...]
        slots2[me] = q_ref[2 * x + y]
        copies = []
        for k in range(1, 8):
            kx, ky, kc = (k >> 2) & 1, (k >> 1) & 1, k & 1
            peer = (x ^ kx if kx else x, y ^ ky if ky else y, c ^ kc if kc else c)
            src = 4 * peer[0] + 2 * peer[1] + peer[2]
            pair = []
            for j, (mine, zone, lands) in enumerate(((p_ref, slots.at[me], slots.at[src]),
                                                     (q_ref.at[2 * peer[0] + peer[1]], slots2.at[me], slots2.at[src]))):
                cp = _remote(mine, zone, send_sems, recv_sems, 2 * (k - 1) + j, peer)
                cp.start()
                pair.append((cp, _remote(mine, lands, send_sems, recv_sems, 2 * (k - 1) + j, peer)))
            copies += pair
        for _, landing in copies:
            landing.wait_recv()
        for cp, _ in copies:
            cp.wait_send()
        total, total2 = slots[0], slots2[0]
        for d in range(1, 8):
            total, total2 = total + slots[d], total2 + slots2[d]
        o_ref[...] = total
        o2_ref[...] = total2

    vmem = pl.BlockSpec(memory_space=pltpu.VMEM)
    return pl.pallas_call(
        body, name="small_grad_allreduce",
        in_specs=[vmem, vmem], out_specs=[vmem, vmem],
        out_shape=[jax.ShapeDtypeStruct((rows, C), F32), jax.ShapeDtypeStruct((rows2, C), F32)],
        scratch_shapes=[pltpu.VMEM((8, rows, C), F32), pltpu.VMEM((8, rows2, C), F32),
                        pltpu.SemaphoreType.DMA((14,)), pltpu.SemaphoreType.DMA((14,))],
        compiler_params=pltpu.CompilerParams(has_side_effects=True, vmem_limit_bytes=VMEM_LIMIT_BYTES),
    )(part, by_chip)


def _pad_w_uq(w):
    lead = w.shape[:-1]
    w = w.reshape(lead + (MLA_HEADS, MLA_QK))
    w = jnp.concatenate([w, jnp.zeros(lead + (MLA_HEADS, MLA_PAD - MLA_QK), w.dtype)], axis=-1)
    return w.reshape(lead + (MLA_HEADS * MLA_PAD,))


def _unpad_w_uq(g):
    lead = g.shape[:-1]
    return g.reshape(lead + (MLA_HEADS, MLA_PAD))[..., :MLA_QK].reshape(lead + (MLA_HEADS * MLA_QK,))


def _t(a):
    return jnp.swapaxes(a, -1, -2)


def _shards_of_cols(w):
    A, NB = w.shape
    return w.reshape(A, N_CHIPS, NB // N_CHIPS).transpose(1, 0, 2)


BIG = ("w_in", "w_uq", "w_ukv", "w_out", "w_up", "w_down")
SMALL = ("attn_pre_norm", "forget_bias", "swa_sinks", "rel_bias", "q_latent_norm", "kv_latent_norm", "group_norm",
         "attn_post_norm", "ffn_pre_norm", "conv_b", "ffn_post_norm")
WEIGHTS = ("attn_pre_norm", "w_in", "forget_bias", "swa_sinks", "rel_bias", "q_latent_norm", "w_uq", "kv_latent_norm",
           "w_ukv", "group_norm", "w_out", "attn_post_norm", "ffn_pre_norm", "w_up", "conv_w", "conv_b", "w_down",
           "ffn_post_norm")


PACK_UNIT = 8 * LANES


def _pack_rows(shape):
    return -(-int(np.prod(shape)) // PACK_UNIT) * 8


def _pack(arrs, row_mult=8):
    parts = []
    for a in arrs:
        n = int(np.prod(a.shape))
        parts.append(jnp.pad(a.reshape(-1), (0, _pack_rows(a.shape) * LANES - n)).reshape(-1, LANES))
    rows = sum(p.shape[0] for p in parts)
    pad = -rows % row_mult
    if pad:
        parts.append(jnp.zeros((pad, LANES), parts[0].dtype))
    return jnp.concatenate(parts, axis=0)


def _unpack(packed, shapes):
    packed = packed.reshape(-1, LANES)
    out, off = [], 0
    for shp in shapes:
        r = _pack_rows(shp)
        out.append(packed[off:off + r].reshape(-1)[:int(np.prod(shp))].reshape(shp))
        off += r
    return out


LAYER_KEYS = ("w_qkv_t", "w_lat_t", "w_in_t", "w_uq_p", "w_uq_t", "w_ukv", "w_ukv_t", "w_out", "w_up", "w_down", "conv_w")


MIX_WEIGHTS = ("w_in", "w_uq", "w_ukv", "w_out")
FFN_WEIGHTS = ("w_up", "w_down", "conv_w")


def _layer_weights(gathered):
    cols = lambda g: g.transpose(1, 0, 2).reshape(g.shape[1], N_CHIPS * g.shape[2])
    out = {}
    if "w_in" in gathered:
        w_in_t = _t(gathered["w_in"]).reshape(IN_COLS, D_MODEL)
        w_in_t = jnp.pad(w_in_t, ((0, IN_ROWS - IN_COLS), (0, 0)))
        w_uq_p = _pad_w_uq(cols(gathered["w_uq"]))
        w_ukv = cols(gathered["w_ukv"])
        out.update(w_qkv_t=w_in_t[:QKV_ROWS], w_lat_t=w_in_t[QKV_ROWS:], w_in_t=w_in_t, w_uq_p=w_uq_p, w_uq_t=_t(w_uq_p),
                   w_ukv=w_ukv, w_ukv_t=_t(w_ukv), w_out=gathered["w_out"].reshape(D_MODEL, D_MODEL))
    if "w_up" in gathered:
        out.update(w_up=gathered["w_up"], w_down=gathered["w_down"].reshape(D_FF, D_MODEL), conv_w=cols(gathered["conv_w"]))
    return out


def _local_step(x, target, W, layer_weights, layer_done):
    W = dict(W, **{key: [None] * DEPTH for key in LAYER_KEYS})
    S = x.shape[0]
    tq_tabs, tm_tabs = _rope_tables(S)
    onehot_t = _rel_onehot_t()
    bias_t = _bias_table(W["rel_bias"].T, onehot_t).reshape(SWA_KV_HEADS, SWA_GROUP, 2 * WINDOW, WINDOW)
    bias_t = bias_t.transpose(0, 2, 1, 3).reshape(SWA_KV_HEADS, 2 * WINDOW, GW)
    row = lambda a: a.reshape(1, -1)
    col = lambda a: a.reshape(-1, 1)
    fox_rows = (FOX_ROW0, FOX_ROW0 + FOX_HEADS * HEAD_DIM, FOX_ROW0 + 2 * FOX_HEADS * HEAD_DIM, SWA_Q_HEADS)
    fox = dict(rows=fox_rows, H=FOX_HEADS, Dk=HEAD_DIM, Dv=HEAD_DIM, scale=HEAD_DIM ** -0.5)
    mla = dict(rows=(0, 0, 0, SWA_Q_HEADS + FOX_HEADS), H=MLA_HEADS, Dk=MLA_PAD, Dv=HEAD_DIM, scale=MLA_SCALE, q_scaled=True)

    saved = []
    h = _rms_fwd(x, row(W["attn_pre_norm"][0]), name="rms_in")
    for l in range(DEPTH):
        sv = {"x0": x, "h1": h}
        for key, val in layer_weights(l, h, False).items():
            W[key][l] = val
        qkv = _matmul(W["w_qkv_t"][l], h, tb=True, out_dtype=BF16, name="proj_qkv")
        lat = _matmul(W["w_lat_t"][l], h, tb=True, name="proj_lat")
        oa, lse_a = _swa_fwd(qkv, bias_t, W["swa_sinks"][l], name="swa_fwd")
        fb_col = jnp.pad(col(W["forget_bias"][l]), ((0, GATE_ROWS - FOX_HEADS), (0, 0)))
        f4 = _gate_fwd(lat, fb_col, name="fox_gate_fwd")[:FOX_HEADS]
        f2 = f4 * LOG2E
        f_row, f_col = f2[:, None, :], f2.T
        of, lse_f = _attn_fwd(qkv, qkv, qkv, f_row=f_row, f_col=f_col, name="fox_fwd", **fox)
        nq, nkv, qm, km, vm = _mla_prep_fwd(lat, col(W["q_latent_norm"][l]), col(W["kv_latent_norm"][l]), W["w_uq_t"][l],
                                            W["w_ukv_t"][l], tq_tabs, tm_tabs, name="mla_prep_fwd")
        oc, lse_c = _attn_fwd(qm, km, vm, name="mla_fwd", **mla)
        mixed = _group_norm_fwd(oa, of, oc, col(W["group_norm"][l]), name="group_norm_fwd")
        y, x1, h2 = _matmul(mixed, W["w_out"][l], ta=True, name="proj_out",
                            resid_rms=(x, row(W["attn_post_norm"][l]), row(W["ffn_pre_norm"][l])))
        for key, val in layer_weights(l, h2, True).items():
            W[key][l] = val
        a = _matmul(h2, W["w_up"][l], b_shards=True, out_dtype=BF16, name="ffn_up")
        u, z = _conv_geglu_fwd(a, W["conv_w"][l], row(W["conv_b"][l]), name="conv_geglu_fwd")
        g_next = row(W["attn_pre_norm"][l + 1]) if l + 1 < DEPTH else None
        y2, x2, *h_next = _matmul(z, W["w_down"][l], name="ffn_down", resid_rms=(x1, row(W["ffn_post_norm"][l]), g_next))
        h_next = h_next[0] if h_next else None
        sv.update(qkv=qkv, lat=lat, oa=oa, lse_a=lse_a, fb_col=fb_col, f_row=f_row, f_col=f_col, of=of, lse_f=lse_f,
                  nq=nq, nkv=nkv, qm=qm, km=km, vm=vm, oc=oc, lse_c=lse_c, mixed=mixed, y=y, x1=x1, h2=h2, a=a, u=u, z=z, y2=y2)
        saved.append(sv)
        x, h = x2, h_next

    loss, dx = _loss_head(x, target)

    G = {k: [None] * DEPTH for k in WEIGHTS if k != "rel_bias" and k not in BIG}
    dbias_layers = [None] * DEPTH
    for l in reversed(range(DEPTH)):
        sv = saved[l]
        gb = {}
        if l == DEPTH - 1:
            dy2, dg = _rms_bwd(sv["y2"], row(W["ffn_post_norm"][l]), dx, out_dtype=BF16, name="ffn_post_bwd")
            G["ffn_post_norm"][l] = dg[0]
        dz = _matmul(dy2, W["w_down"][l], tb=True, name="ffn_down_dx")
        gb["w_down"] = _matmul(sv["z"], dy2, ta=True, name="ffn_down_dw").reshape(N_CHIPS, D_FF // N_CHIPS, D_MODEL)
        da, dcw, dcb = _conv_geglu_bwd(sv["a"], sv["u"], W["conv_w"][l], dz, name="conv_geglu_bwd")
        G["conv_w"][l] = dcw.transpose(1, 0, 2).reshape(3, 2 * D_FF)
        G["conv_b"][l] = dcb.reshape(2 * D_FF)
        gb["w_up"] = _matmul(sv["h2"], da, ta=True, out_shards=True, b_halves=True, name="ffn_up_dw")
        token = layer_done(l, gb)
        gb = {}
        dx1, dg, dy, dg_post = _matmul(
            da, W["w_up"][l], tb=True, b_shards=True, a_halves=True, name="ffn_up_dx",
            norm_bwd=(sv["x1"], row(W["ffn_pre_norm"][l]) + token, dx, (sv["y"], row(W["attn_post_norm"][l]))))
        G["ffn_pre_norm"][l] = dg[0]
        G["attn_post_norm"][l] = dg_post[0]
        dmixed = _matmul(W["w_out"][l], dy, tb=True, name="proj_out_dx")
        gb["w_out"] = _matmul(sv["mixed"], dy, name="proj_out_dw").reshape(N_CHIPS, D_MODEL // N_CHIPS, D_MODEL)
        doa, dof, doc, dg, delta = _group_norm_bwd(sv["oa"], sv["of"], sv["oc"], col(W["group_norm"][l]), dmixed,
                                                   name="group_norm_bwd")
        G["group_norm"][l] = dg[:, 0]
        dqa, dkva, dbias_l, dsink = _swa_bwd(sv["qkv"], bias_t, W["swa_sinks"][l], doa, sv["lse_a"],
                                             delta.reshape(-1, S), name="swa_bwd")
        dbias_layers[l] = (dbias_l.reshape(SWA_KV_HEADS, 2 * WINDOW, SWA_GROUP, WINDOW).transpose(0, 2, 1, 3)
                           .reshape(SWA_Q_HEADS, -1))
        G["swa_sinks"][l] = dsink[:, 0]
        dqf, dkf, dvf, dfk = _attn_bwd(sv["qkv"], sv["qkv"], sv["qkv"], do=dof, lse=sv["lse_f"], delta=delta,
                                       f_row=sv["f_row"], f_col=sv["f_col"], name="fox_bwd", **fox)
        dF = jnp.pad(dfk.T, ((0, GATE_ROWS - FOX_HEADS), (0, 0)))
        dflog, dfb = _gate_bwd(sv["lat"], sv["fb_col"], dF, name="fox_gate_bwd")
        G["forget_bias"][l] = dfb[:FOX_HEADS, 0]
        dqm, dkm, dvm = _attn_bwd(sv["qm"], sv["km"], sv["vm"], do=doc, lse=sv["lse_c"], delta=delta, name="mla_bwd", **mla)
        dlat, dwq_t, dwkv_t, dgq, dgkv = _mla_prep_bwd(
            sv["lat"], sv["nq"], sv["nkv"], col(W["q_latent_norm"][l]), col(W["kv_latent_norm"][l]), W["w_uq_p"][l],
            W["w_ukv"][l], tq_tabs, tm_tabs, dqm, dkm, dvm, dflog, name="mla_prep_bwd")
        gb["w_uq"], gb["w_ukv"] = _shards_of_cols(_unpad_w_uq(dwq_t.T)), _shards_of_cols(dwkv_t.T)
        G["q_latent_norm"][l], G["kv_latent_norm"][l] = dgq[:, 0], dgkv[:, 0]
        dproj = _dproj_cast(dqa, dkva, dqf, dkf, dvf, dlat, name="dproj_cast")
        dw_in_t = _matmul(dproj, sv["h1"], name="proj_in_dw")
        gb["w_in"] = _t(dw_in_t[:IN_COLS].reshape(N_CHIPS, IN_COLS // N_CHIPS, D_MODEL))
        token = layer_done(l, gb)
        below = (saved[l - 1]["y2"], row(W["ffn_post_norm"][l - 1])) if l > 0 else None
        res = _matmul(dproj, W["w_in_t"][l], ta=True, name="proj_in_dx",
                      norm_bwd=(sv["x0"], row(W["attn_pre_norm"][l]) + token, dx1, below))
        dx, G["attn_pre_norm"][l] = res[0], res[1][0]
        if l > 0:
            dy2, G["ffn_post_norm"][l - 1] = res[2], res[3][0]

    grads = {k: jnp.stack(v) for k, v in G.items()}
    grads["rel_bias"] = _bias_table_bwd(jnp.stack(dbias_layers), onehot_t).T
    return loss, dx, grads


def kernel(x, attn_pre_norm, w_in, forget_bias, swa_sinks, rel_bias, q_latent_norm, w_uq, kv_latent_norm, w_ukv, group_norm, w_out, attn_post_norm, ffn_pre_norm, w_up, conv_w, conv_b, w_down, ffn_post_norm, loss_target, m_attn_pre_norm, m_w_in, m_forget_bias, m_swa_sinks, m_rel_bias, m_q_latent_norm, m_w_uq, m_kv_latent_norm, m_w_ukv, m_group_norm, m_w_out, m_attn_post_norm, m_ffn_pre_norm, m_w_up, m_conv_w, m_conv_b, m_w_down, m_ffn_post_norm, v_attn_pre_norm, v_w_in, v_forget_bias, v_swa_sinks, v_rel_bias, v_q_latent_norm, v_w_uq, v_kv_latent_norm, v_w_ukv, v_group_norm, v_w_out, v_attn_post_norm, v_ffn_pre_norm, v_w_up, v_conv_w, v_conv_b, v_w_down, v_ffn_post_norm):
    args = dict(locals())
    w = {k: args[k] for k in WEIGHTS}
    m = {k: args["m_" + k] for k in WEIGHTS}
    v = {k: args["v_" + k] for k in WEIGHTS}

    block = lambda l, keys: [w[k][l] if k == "conv_w" else w[k][l].astype(BF16) for k in keys]
    units = [(0, MIX_WEIGHTS), (0, FFN_WEIGHTS)] + [(l, MIX_WEIGHTS + FFN_WEIGHTS) for l in range(1, DEPTH)]
    gather_state, token = _gather_start([block(l, keys) for l, keys in units])
    W = {k: w[k] for k in SMALL}
    W["attn_pre_norm"] = W["attn_pre_norm"] + token

    def layer_weights(l, after, for_ffn):
        if for_ffn and l > 0:
            return {}
        keys = FFN_WEIGHTS if for_ffn else (MIX_WEIGHTS if l == 0 else MIX_WEIGHTS + FFN_WEIGHTS)
        tag = f"{l}_{keys[0]}"
        srcs, lands = _gather_wait(gather_state[units.index((l, keys))], after, name="weight_gather_wait_" + tag)
        lands = _gather_forward(lands, name="weight_gather_forward_" + tag)
        lands = _place_own(lands, srcs, name="place_own_shards")
        return _layer_weights(dict(zip(keys, lands)))

    started, groups, pending = [], [], []

    def to_chips(l, keys, gs, recv, tag):
        pair = [_pair_sum(gk, rk, name="grad_pair_sum") for gk, rk in zip(gs, recv)]
        state, token = _scatter_start(pair, name="grad_scatter_start_" + tag)
        started.append(state)
        groups.append((l, keys))
        return token

    def finish_pending(after):
        l, keys, tag, state = pending.pop()
        gs, recv = _split_wait([state], _exchange_copies, after, name="grad_exchange_wait_" + tag)[0]
        return to_chips(l, keys, gs, recv, tag)

    def layer_done(l, gb):
        keys = [k for k in BIG if k in gb]
        gs = [gb[k] for k in keys]
        tag = f"{l}_{keys[0]}"
        token = finish_pending(gs[0]) if pending else 0.0
        if l == 0:
            return token + to_chips(l, keys, gs, _sibling_exchange(gs, name="grad_sibling_exchange_" + tag), tag)
        state, started_token = _exchange_start(gs, name="grad_exchange_start_" + tag)
        pending.append((l, keys, tag, state))
        return token + started_token

    loss_part, dx, g = _local_step(x[0], loss_target[0], W, layer_weights, layer_done)
    loss = lax.psum(loss_part, ("x", "y", "c"))

    reduced = {}
    for (l, keys), (pair, zones) in zip(groups, _split_wait(started, _scatter_copies, dx, name="grad_scatter_wait")):
        for k, p, z in zip(keys, pair, zones):
            reduced[k, l] = _chip_sum(z, p, name="grad_chip_sum")
    mine = [jnp.stack([reduced[k, l] for l in range(DEPTH)]) for k in BIG]
    other = _sibling_share(mine)
    out_g, out_d, out_m, out_v = {}, {}, {}, {}
    for k, g_mine, g_other in zip(BIG, mine, other):
        out_g[k], out_d[k], out_m[k], out_v[k] = _adamw_halves(w[k], g_mine, g_other, m[k], v[k], name="adamw_" + k)

    small_shapes = [w[k].shape for k in SMALL]
    taps_by_chip = g["conv_w"].reshape(DEPTH, 3, N_CHIPS, FF_SHARD).transpose(2, 0, 1, 3)
    reduced, taps = _allreduce_small(_pack([g[k] for k in SMALL]), jnp.stack([_pack([t]) for t in taps_by_chip]))
    g_small = _unpack(reduced, small_shapes) + _unpack(taps, [w["conv_w"].shape])
    names = SMALL + ("conv_w",)
    shapes = small_shapes + [w["conv_w"].shape]
    packed = lambda arrs: _pack(arrs, ROW_TILE)[None]
    d_s, m_s, v_s = _adamw(packed([w[k] for k in names]), packed(g_small), packed([m[k] for k in names]),
                           packed([v[k] for k in names]), name="adamw_small")
    out_g.update(zip(names, g_small))
    out_d.update(zip(names, _unpack(d_s, shapes)))
    out_m.update(zip(names, _unpack(m_s, shapes)))
    out_v.update(zip(names, _unpack(v_s, shapes)))

    return (loss, dx[None], *[out_g[k] for k in WEIGHTS], *[out_d[k] for k in WEIGHTS],
            *[out_m[k] for k in WEIGHTS], *[out_v[k] for k in WEIGHTS])
```

```python
import math

import numpy as np
import jax
import jax.numpy as jnp
from jax import lax
from jax.experimental import pallas as pl
from jax.experimental.pallas import tpu as pltpu

F32 = jnp.float32
BF16 = jnp.bfloat16

D_MODEL = 1024
DEPTH = 4
HEAD_DIM = 64
SWA_Q_HEADS = 8
SWA_KV_HEADS = 2
SWA_GROUP = SWA_Q_HEADS // SWA_KV_HEADS
WINDOW = 128
FOX_HEADS = 4
MLA_HEADS = 4
MLA_Q_RANK = 256
MLA_KV_RANK = 128
MLA_NOPE = 64
MLA_ROPE = 32
MLA_QK = MLA_NOPE + MLA_ROPE
ROPE_THETA = 10000.0
REL_BUCKETS = 32
REL_MAX_DIST = 128
D_FF = 2816
EPS = 1e-6
NEG_INF = -1e30
LANES = 128
N_CHIPS = 4

IN_COLS = 1956
IN_ROWS = 2048
QKV_ROWS = 1536
LAT_ROWS = IN_ROWS - QKV_ROWS
LAT_SHIFT = FOX_HEADS
FOX_ROW0 = 768
MLA_PAD = LANES
GATE_ROWS = 8

ADAM_LR = 0.001
ADAM_B1 = 0.9
ADAM_B2 = 0.999
ADAM_EPS = 1e-08
ADAM_WD = 0.01
ADAM_STEP = 10

VMEM_LIMIT_BYTES = 48 * 1024 * 1024
ATT_TILE = 512
LOG2E = math.log2(math.e)
MLA_SCALE = MLA_QK ** -0.5
ROW_TILE = 256
MESH = pl.DeviceIdType.MESH

NT = (((1,), (1,)), ((), ()))
TN = (((0,), (0,)), ((), ()))
NN = (((1,), (0,)), ((), ()))


def _params(*sem):
    return pltpu.CompilerParams(dimension_semantics=sem, vmem_limit_bytes=VMEM_LIMIT_BYTES)


def _tile(dim, cap):
    for t in (2816, 2048, 1408, 1024, 512, 256, 128, 64, 32, 16, 8):
        if t <= cap and dim % t == 0:
            return t
    return dim


def _dot(a, b, dims=NN):
    return lax.dot_general(a, b, dims, preferred_element_type=F32)


def _split3(a):
    a1 = a.astype(BF16)
    r1 = a - a1.astype(F32)
    a2 = r1.astype(BF16)
    a3 = (r1 - a2.astype(F32)).astype(BF16)
    return a1, a2, a3


FF_SHARD = 2 * D_FF // N_CHIPS
MATMUL_VMEM_BYTES = 40 * 1024 * 1024
TAIL_ROWS = 512
TAIL_VMEM_LIMIT_BYTES = 56 * 1024 * 1024


def _matmul(a, b, *, ta=False, tb=False, out_dtype=F32, name, b_shards=False, out_shards=False, a_halves=False,
            b_halves=False, norm_bwd=None, resid_rms=None):
    if a_halves:
        M, K = a.shape[1], 2 * a.shape[2]
    elif ta:
        K, M = a.shape
    else:
        M, K = a.shape
    if b_halves:
        K2, N = b.shape[1], 2 * b.shape[2]
    elif b_shards:
        K2, N = (2 * D_FF, D_MODEL) if tb else (D_MODEL, 2 * D_FF)
    elif tb:
        N, K2 = b.shape
    else:
        K2, N = b.shape
    assert K == K2, (a.shape, b.shape)
    tn = _tile(N, 1408)
    tk = FF_SHARD if (b_shards and tb) else _tile(K, 2816)
    out_bytes = jnp.dtype(out_dtype).itemsize
    with_tail = norm_bwd is not None or resid_rms is not None
    tile_bytes = 4 + (2 * (4 * 4 + 2) if with_tail else 2 * out_bytes)
    vmem = lambda tm, tk: 2 * 2 * tk * (tm + tn) + tile_bytes * tm * tn
    tm = M if M <= 2048 else _tile(M, 1408)
    if M > 2048 and M % 2048 == 0 and tk == K and vmem(2048, tk) <= MATMUL_VMEM_BYTES:
        tm = 2048
    if with_tail:
        assert tn == N and not out_shards and (norm_bwd is None or resid_rms is None)
        tm = TAIL_ROWS
    while vmem(tm, tk) > MATMUL_VMEM_BYTES and tk % 256 == 0:
        tk //= 2
    nk = K // tk
    dims = (((0 if ta else 1,), (1 if tb else 0,)), ((), ()))
    if with_tail:
        if norm_bwd is not None:
            x, g, resid, then = norm_bwd
            chained = then is not None
            tail_in, in_kinds = [x, g, resid] + (list(then) if chained else []), "rvr" + ("rv" if chained else "")
            out_kinds, out_dtypes = "rv" + ("rv" if chained else ""), [F32, F32] + ([BF16, F32] if chained else [])

            def tail(dy, ins, outs):
                dx, dg = _seg_rms_bwd(ins[0][...], ins[1][...], dy)
                dx = dx + ins[2][...]
                outs[0][...] = dx
                outs[1][...] += dg
                if chained:
                    dx2, dg2 = _seg_rms_bwd(ins[3][...], ins[4][...], dx)
                    outs[2][...] = dx2.astype(BF16)
                    outs[3][...] += dg2
        else:
            x, g_post, g_next = resid_rms
            with_next = g_next is not None
            tail_in, in_kinds = [x, g_post] + ([g_next] if with_next else []), "rv" + ("v" if with_next else "")
            out_kinds, out_dtypes = "rr" + ("r" if with_next else ""), [F32, F32] + ([BF16] if with_next else [])

            def tail(y, ins, outs):
                outs[0][...] = y
                xn = ins[0][...] + _seg_rms(y, ins[1][...])
                outs[1][...] = xn
                if with_next:
                    outs[2][...] = _seg_rms(xn, ins[2][...]).astype(BF16)

        n_tiles = M // tm
        skewed = nk == 1

        def zero_vectors(outs):
            for o, kind in zip(outs, out_kinds):
                if kind == "v":
                    o[...] = jnp.zeros_like(o)

        def fused_skewed(a_ref, b_ref, *refs):
            ins, outs, acc = refs[:len(tail_in)], refs[len(tail_in):-1], refs[-1]
            i = pl.program_id(1)
            prod = lambda: lax.dot_general(a_ref[...], b_ref[...], dims, preferred_element_type=F32)

            @pl.when(i == 0)
            def _():
                zero_vectors(outs)
                acc[0] = prod()

            for slot in range(2):
                @pl.when((i > 0) & (i < n_tiles) & (i % 2 == slot))
                def _(slot=slot):
                    acc[slot] = prod()
                    tail(acc[1 - slot], ins, outs)

            @pl.when(i == n_tiles)
            def _():
                tail(acc[(n_tiles - 1) % 2], ins, outs)

        def fused(a_ref, b_ref, *refs):
            ins, outs, acc = refs[:len(tail_in)], refs[len(tail_in):-1], refs[-1]
            k, i = pl.program_id(0), pl.program_id(1)
            acc_ref = acc.at[pl.ds(pl.multiple_of(i * tm, tm), tm), :]

            @pl.when(k == 0)
            def _():
                acc_ref[...] = jnp.zeros((tm, N), F32)

            acc_ref[...] += lax.dot_general(a_ref[...], b_ref[...], dims, preferred_element_type=F32)

            @pl.when((k == nk - 1) & (i == 0))
            def _():
                zero_vectors(outs)

            @pl.when(k == nk - 1)
            def _():
                tail(acc_ref[...], ins, outs)

    def body(a_ref, b_ref, o_ref, acc_ref):
        k = pl.program_id(2)

        @pl.when(k == 0)
        def _():
            acc_ref[...] = jnp.zeros_like(acc_ref)

        acc_ref[...] += lax.dot_general(a_ref[...], b_ref[...], dims, preferred_element_type=F32)

        @pl.when(k == nk - 1)
        def _():
            o_ref[...] = acc_ref[...].astype(o_ref.dtype)

    if a_halves:
        nh = K // 2 // tk
        a_spec = pl.BlockSpec((None, tm, tk), lambda i, j, k: (k // nh, i, k % nh))
    else:
        a_spec = pl.BlockSpec((tk, tm), lambda i, j, k: (k, i)) if ta else pl.BlockSpec((tm, tk), lambda i, j, k: (i, k))
    if b_halves:
        nh = N // 2 // tn
        b_spec = pl.BlockSpec((None, tk, tn), lambda i, j, k: (j // nh, k, j % nh))
    elif b_shards and tb:
        assert tk == FF_SHARD
        b_spec = pl.BlockSpec((None, tn, tk), lambda i, j, k: (k, j, 0))
    elif b_shards:
        assert tn == FF_SHARD
        b_spec = pl.BlockSpec((None, tk, tn), lambda i, j, k: (j, k, 0))
    else:
        b_spec = pl.BlockSpec((tn, tk), lambda i, j, k: (j, k)) if tb else pl.BlockSpec((tk, tn), lambda i, j, k: (k, j))
    if out_shards:
        assert tn == FF_SHARD
        out_spec = pl.BlockSpec((None, tm, tn), lambda i, j, k: (j, i, 0))
        out_shape = jax.ShapeDtypeStruct((N // tn, M, tn), out_dtype)
    else:
        out_spec = pl.BlockSpec((tm, tn), lambda i, j, k: (i, j))
        out_shape = jax.ShapeDtypeStruct((M, N), out_dtype)
    if with_tail:
        if skewed:
            row_map = lambda k, i: (jnp.maximum(i - 1, 0), 0)
            tile_of = lambda i: jnp.minimum(i, n_tiles - 1)
        else:
            row_map = lambda k, i: (jnp.where(k == nk - 1, i, 0), 0)
            tile_of = lambda i: i
        spec = {"r": pl.BlockSpec((tm, N), row_map), "v": pl.BlockSpec((1, N), lambda k, i: (0, 0))}
        a_map, b_map = a_spec.index_map, b_spec.index_map
        return pl.pallas_call(
            fused_skewed if skewed else fused, name=name, grid=(nk, n_tiles + skewed),
            in_specs=[pl.BlockSpec(a_spec.block_shape, lambda k, i: a_map(tile_of(i), 0, k)),
                      pl.BlockSpec(b_spec.block_shape, lambda k, i: b_map(tile_of(i), 0, k))] + [spec[c] for c in in_kinds],
            out_specs=[spec[c] for c in out_kinds],
            out_shape=[jax.ShapeDtypeStruct((M if c == "r" else 1, N), d) for c, d in zip(out_kinds, out_dtypes)],
            scratch_shapes=[pltpu.VMEM((2, tm, N) if skewed else (M, N), F32)],
            compiler_params=pltpu.CompilerParams(dimension_semantics=("arbitrary", "arbitrary"),
                                                 vmem_limit_bytes=TAIL_VMEM_LIMIT_BYTES if nk > 1 else VMEM_LIMIT_BYTES),
        )(a, b, *tail_in)
    return pl.pallas_call(
        body, name=name, grid=(M // tm, N // tn, nk),
        in_specs=[a_spec, b_spec], out_specs=out_spec, out_shape=out_shape,
        scratch_shapes=[pltpu.VMEM((tm, tn), F32)],
        compiler_params=_params("parallel", "parallel", "arbitrary"),
    )(a, b)


def _seg_rms(xs, g):
    r = lax.rsqrt(jnp.mean(xs * xs, axis=-1, keepdims=True) + EPS)
    return xs * r * g


def _seg_rms_bwd(xs, g, dy):
    r = lax.rsqrt(jnp.mean(xs * xs, axis=-1, keepdims=True) + EPS)
    gd = dy * g
    c = jnp.mean(gd * xs, axis=-1, keepdims=True)
    dx = r * gd - xs * (r * r * r * c)
    dg = jnp.sum(dy * (xs * r), axis=0, keepdims=True)
    return dx, dg


def _rms_fwd(x, g, *, name):
    S, W = x.shape
    tm = _tile(S, 512)

    def body(x_ref, g_ref, o_ref):
        o_ref[...] = _seg_rms(x_ref[...], g_ref[...]).astype(o_ref.dtype)

    return pl.pallas_call(
        body, name=name, grid=(S // tm,),
        in_specs=[pl.BlockSpec((tm, W), lambda i: (i, 0)), pl.BlockSpec((1, W), lambda i: (0, 0))],
        out_specs=pl.BlockSpec((tm, W), lambda i: (i, 0)),
        out_shape=jax.ShapeDtypeStruct((S, W), BF16),
        compiler_params=_params("parallel"),
    )(x, g)


def _rms_bwd(x, g, dy, *, out_dtype, name):
    S, W = x.shape
    tm = _tile(S, 512)

    def body(x_ref, g_ref, dy_ref, dx_ref, dg_ref):
        @pl.when(pl.program_id(0) == 0)
        def _():
            dg_ref[...] = jnp.zeros_like(dg_ref)

        dx, dg = _seg_rms_bwd(x_ref[...], g_ref[...], dy_ref[...])
        dx_ref[...] = dx.astype(dx_ref.dtype)
        dg_ref[...] += dg

    row = pl.BlockSpec((tm, W), lambda i: (i, 0))
    vec = pl.BlockSpec((1, W), lambda i: (0, 0))
    return pl.pallas_call(
        body, name=name, grid=(S // tm,),
        in_specs=[row, vec, row], out_specs=[row, vec],
        out_shape=[jax.ShapeDtypeStruct((S, W), out_dtype), jax.ShapeDtypeStruct((1, W), F32)],
        compiler_params=_params("arbitrary"),
    )(x, g, dy)


def _col_rms(xs, g):
    r = lax.rsqrt(jnp.mean(xs * xs, axis=0, keepdims=True) + EPS)
    return xs * r * g


def _col_rms_bwd(xs, g, dy):
    r = lax.rsqrt(jnp.mean(xs * xs, axis=0, keepdims=True) + EPS)
    gd = dy * g
    c = jnp.mean(gd * xs, axis=0, keepdims=True)
    dx = r * gd - xs * (r * r * r * c)
    dg = jnp.sum(dy * (xs * r), axis=1, keepdims=True)
    return dx, dg


GROUP_ROWS = (SWA_Q_HEADS * HEAD_DIM, FOX_HEADS * HEAD_DIM, MLA_HEADS * HEAD_DIM)


def _group_specs(S, tn):
    outs = [pl.BlockSpec((n, tn), lambda i: (0, i)) for n in GROUP_ROWS]
    g = pl.BlockSpec((D_MODEL, 1), lambda i: (0, 0))
    mixed = pl.BlockSpec((D_MODEL, tn), lambda i: (0, i))
    return outs, g, mixed


def _group_norm_fwd(oa, of, oc, g, *, name):
    S = oa.shape[1]
    tn = _tile(S, 512)
    outs, gs, mixed = _group_specs(S, tn)

    def body(a_ref, f_ref, c_ref, g_ref, o_ref):
        r0 = 0
        for ref, n in zip((a_ref, f_ref, c_ref), GROUP_ROWS):
            o_ref[r0:r0 + n, :] = _col_rms(ref[...], g_ref[r0:r0 + n, :]).astype(BF16)
            r0 += n

    return pl.pallas_call(
        body, name=name, grid=(S // tn,),
        in_specs=outs + [gs], out_specs=mixed,
        out_shape=jax.ShapeDtypeStruct((D_MODEL, S), BF16),
        compiler_params=_params("parallel"),
    )(oa, of, oc, g)


def _group_norm_bwd(oa, of, oc, g, dmixed, *, name):
    S = oa.shape[1]
    tn = _tile(S, 512)
    outs, gs, mixed = _group_specs(S, tn)
    n_heads = D_MODEL // HEAD_DIM

    def body(a_ref, f_ref, c_ref, g_ref, dm_ref, da_ref, df_ref, dc_ref, dg_ref, dl_ref):
        @pl.when(pl.program_id(0) == 0)
        def _():
            dg_ref[...] = jnp.zeros_like(dg_ref)

        r0 = 0
        for ref, dref, n in zip((a_ref, f_ref, c_ref), (da_ref, df_ref, dc_ref), GROUP_ROWS):
            o = ref[...]
            dx, dg = _col_rms_bwd(o, g_ref[r0:r0 + n, :], dm_ref[r0:r0 + n, :])
            dxb = dx.astype(BF16)
            dref[...] = dxb
            dg_ref[r0:r0 + n, :] += dg
            od = o * dxb.astype(F32)
            for h in range(n // HEAD_DIM):
                dl_ref[r0 // HEAD_DIM + h] = jnp.sum(od[h * HEAD_DIM:(h + 1) * HEAD_DIM, :], axis=0, keepdims=True)
            r0 += n

    return pl.pallas_call(
        body, name=name, grid=(S // tn,),
        in_specs=outs + [gs, mixed], out_specs=outs + [gs, pl.BlockSpec((n_heads, 1, tn), lambda i: (0, 0, i))],
        out_shape=[jax.ShapeDtypeStruct((n, S), BF16) for n in GROUP_ROWS] + [jax.ShapeDtypeStruct((D_MODEL, 1), F32),
                                                                              jax.ShapeDtypeStruct((n_heads, 1, S), F32)],
        compiler_params=_params("arbitrary"),
    )(oa, of, oc, g, dmixed)


def _loss_head(y, target):
    S, W = y.shape
    tm = _tile(S, 512)

    def body(y_ref, t_ref, d_ref, l_ref):
        @pl.when(pl.program_id(0) == 0)
        def _():
            l_ref[...] = jnp.zeros_like(l_ref)

        err = y_ref[...] - t_ref[...]
        d_ref[...] = err * (1.0 / W)
        l_ref[...] += 0.5 * jnp.sum(jnp.mean(err * err, axis=-1, keepdims=True), axis=0, keepdims=True)

    row = pl.BlockSpec((tm, W), lambda i: (i, 0))
    d, l = pl.pallas_call(
        body, name="loss_head", grid=(S // tm,),
        in_specs=[row, row],
        out_specs=[row, pl.BlockSpec((1, 1), lambda i: (0, 0))],
        out_shape=[jax.ShapeDtypeStruct((S, W), F32), jax.ShapeDtypeStruct((1, 1), F32)],
        compiler_params=_params("arbitrary"),
    )(y, target)
    return l[0, 0], d


def _attn_fwd(q_src, k_src, v_src, rows, H, Dk, Dv, scale, f_row=None, f_col=None, *, name, q_scaled=False):
    S = q_src.shape[1]
    T = _tile(S, ATT_TILE)
    nq = S // T
    forget = f_row is not None
    qb, kb, vb = rows[0] // (H * Dk), rows[1] // (H * Dk), rows[2] // (H * Dv)
    hs = range(H)

    def body(*refs):
        if forget:
            q_ref, k_ref, v_ref, fq_ref, fk_ref, o_ref, lse_ref = refs
        else:
            q_ref, k_ref, v_ref, o_ref, lse_ref = refs
        i = pl.program_id(0)

        def tile(j, masked, state):
            off = pl.multiple_of(j * T, T)
            ss = [_dot(k_ref[h * Dk:(h + 1) * Dk, pl.ds(off, T)], q_ref[h * Dk:(h + 1) * Dk, :], TN) for h in hs]
            if not q_scaled:
                ss = [s * (scale * LOG2E) for s in ss]
            if forget:
                ss = [ss[h] + (fq_ref[h] - fk_ref[pl.ds(off, T), h:h + 1]) for h in hs]
            if masked:
                r = lax.broadcasted_iota(jnp.int32, (T, T), 0)
                c = lax.broadcasted_iota(jnp.int32, (T, T), 1)
                ss = [jnp.where(r <= c, s, NEG_INF) for s in ss]
            m_new = [jnp.maximum(state[h][0], jnp.max(ss[h], axis=0, keepdims=True)) for h in hs]
            alpha = [jnp.exp2(state[h][0] - m_new[h]) for h in hs]
            ps = [jnp.exp2(ss[h] - m_new[h]) for h in hs]
            l_new = [alpha[h] * state[h][1] + jnp.sum(ps[h], axis=0, keepdims=True) for h in hs]
            p_hi = [p.astype(BF16) for p in ps]
            vs = [v_ref[h * Dv:(h + 1) * Dv, pl.ds(off, T)] for h in hs]
            pv = [_dot(vs[h], p_hi[h]) for h in hs]
            if forget:
                pv = [pv[h] + _dot(vs[h], (ps[h] - p_hi[h].astype(F32)).astype(BF16)) for h in hs]
            return tuple((m_new[h], l_new[h], alpha[h] * state[h][2] + pv[h]) for h in hs)

        init = tuple((jnp.full((1, T), NEG_INF, F32), jnp.zeros((1, T), F32), jnp.zeros((Dv, T), F32)) for _ in hs)
        state = lax.fori_loop(0, i, lambda j, st: tile(j, False, st), init)
        state = tile(i, True, state)
        for h in hs:
            m, l, acc = state[h]
            o_ref[h * Dv:(h + 1) * Dv, :] = acc / l
            lse_ref[h] = m + jnp.log2(l)

    in_specs = [pl.BlockSpec((H * Dk, T), lambda i: (qb, i)),
                pl.BlockSpec((H * Dk, S), lambda i: (kb, 0)),
                pl.BlockSpec((H * Dv, S), lambda i: (vb, 0))]
    ins = [q_src, k_src, v_src]
    if forget:
        in_specs += [pl.BlockSpec((H, 1, T), lambda i: (0, 0, i)), pl.BlockSpec((S, H), lambda i: (0, 0))]
        ins += [f_row, f_col]
    return pl.pallas_call(
        body, name=name, grid=(nq,),
        in_specs=in_specs,
        out_specs=[pl.BlockSpec((H * Dv, T), lambda i: (0, i)), pl.BlockSpec((H, 1, T), lambda i: (0, 0, i))],
        out_shape=[jax.ShapeDtypeStruct((H * Dv, S), F32), jax.ShapeDtypeStruct((H, 1, S), F32)],
        compiler_params=_params("parallel"),
    )(*ins)


def _attn_bwd(q_src, k_src, v_src, rows, H, Dk, Dv, scale, do, lse, delta, f_row=None, f_col=None, *, name, q_scaled=False):
    S = q_src.shape[1]
    T = _tile(S, ATT_TILE)
    nq = S // T
    forget = f_row is not None
    qb, kb, vb, db = rows[0] // (H * Dk), rows[1] // (H * Dk), rows[2] // (H * Dv), rows[3] // H
    hs = range(H)

    def body(*refs):
        if forget:
            (q_ref, k_ref, v_ref, do_ref, lse_ref, dl_ref, fq_ref, fk_ref,
             dq_ref, dk_ref, dv_ref, df_ref, dk_s, dv_s, df_s) = refs
        else:
            q_ref, k_ref, v_ref, do_ref, lse_ref, dl_ref, dq_ref, dk_ref, dv_ref, dk_s, dv_s = refs
        j = pl.program_id(0)

        @pl.when(j == 0)
        def _():
            dq_ref[...] = jnp.zeros_like(dq_ref)

        dk_s[...] = jnp.zeros_like(dk_s)
        dv_s[...] = jnp.zeros_like(dv_s)
        if forget:
            df_s[...] = jnp.zeros_like(df_s)
        kt = [k_ref[h * Dk:(h + 1) * Dk, :] for h in hs]
        kj = [k.T for k in kt]
        vj = [v_ref[h * Dv:(h + 1) * Dv, :].T for h in hs]
        koff = pl.multiple_of(j * T, T)

        def tile(i, masked):
            cols = pl.ds(pl.multiple_of(i * T, T), T)
            qi = [q_ref[h * Dk:(h + 1) * Dk, cols] for h in hs]
            doi = [do_ref[h * Dv:(h + 1) * Dv, cols] for h in hs]
            st = [_dot(kj[h], qi[h]) for h in hs]
            if not q_scaled:
                st = [x * (scale * LOG2E) for x in st]
            if forget:
                st = [st[h] + (fq_ref[h, :, cols] - fk_ref[pl.ds(koff, T), h:h + 1]) for h in hs]
            if masked:
                r = lax.broadcasted_iota(jnp.int32, (T, T), 0)
                c = lax.broadcasted_iota(jnp.int32, (T, T), 1)
                st = [jnp.where(r <= c, x, NEG_INF) for x in st]
            pt = [jnp.exp2(st[h] - lse_ref[h, :, cols]) for h in hs]
            dpt = [_dot(vj[h], doi[h]) for h in hs]
            dst = [pt[h] * (dpt[h] - dl_ref[h, :, cols]) for h in hs]
            ptb = [p.astype(BF16) for p in pt]
            dsb = [d.astype(BF16) for d in dst]
            for h in hs:
                dv_s[h * Dv:(h + 1) * Dv, :] += _dot(doi[h], ptb[h], NT)
            for h in hs:
                dk_s[h * Dk:(h + 1) * Dk, :] += _dot(qi[h], dsb[h], NT)
            for h in hs:
                dq_ref[h * Dk:(h + 1) * Dk, cols] += _dot(kt[h], dsb[h]) * scale
            if forget:
                for h in hs:
                    part = dst[h][:, 0:LANES]
                    for c0 in range(LANES, T, LANES):
                        part = part + dst[h][:, c0:c0 + LANES]
                    df_s[h] += part

        tile(j, True)

        def loop_body(i, carry):
            tile(i, False)
            return carry

        lax.fori_loop(j + 1, nq, loop_body, 0)
        dk_ref[...] = dk_s[...] * ((1.0 / LOG2E) if q_scaled else scale)
        dv_ref[...] = dv_s[...]
        if forget:
            df_ref[...] = jnp.concatenate([-jnp.sum(df_s[h], axis=-1, keepdims=True) for h in hs], axis=1)

    res = lambda D, b0: pl.BlockSpec((H * D, S), lambda j: (b0, 0))
    blk = lambda D, b0: pl.BlockSpec((H * D, T), lambda j: (b0, j))
    row3 = lambda b0: pl.BlockSpec((H, 1, S), lambda j: (b0, 0, 0))
    in_specs = [res(Dk, qb), blk(Dk, kb), blk(Dv, vb), res(Dv, 0), row3(0), row3(db)]
    ins = [q_src, k_src, v_src, do, lse, delta]
    out_specs = [res(Dk, 0), blk(Dk, 0), blk(Dv, 0)]
    out_shape = [jax.ShapeDtypeStruct((H * Dk, S), F32), jax.ShapeDtypeStruct((H * Dk, S), F32),
                 jax.ShapeDtypeStruct((H * Dv, S), F32)]
    scratch = [pltpu.VMEM((H * Dk, T), F32), pltpu.VMEM((H * Dv, T), F32)]
    if forget:
        in_specs += [row3(0), pl.BlockSpec((S, H), lambda j: (0, 0))]
        ins += [f_row, f_col]
        out_specs.append(pl.BlockSpec((T, H), lambda j: (j, 0)))
        out_shape.append(jax.ShapeDtypeStruct((S, H), F32))
        scratch.append(pltpu.VMEM((H, T, min(T, LANES)), F32))
    return pl.pallas_call(
        body, name=name, grid=(nq,),
        in_specs=in_specs, out_specs=out_specs, out_shape=out_shape, scratch_shapes=scratch,
        compiler_params=_params("arbitrary"),
    )(*ins)


GW = SWA_GROUP * WINDOW


def _swa_masks(i):
    r = lax.broadcasted_iota(jnp.int32, (WINDOW, GW), 0)
    c = lax.broadcasted_iota(jnp.int32, (WINDOW, GW), 1) % WINDOW
    return (r > c) & (i > 0), r <= c


def _swa_specs():
    W = WINDOW
    kv_rows = SWA_KV_HEADS * HEAD_DIM
    q = pl.BlockSpec((SWA_Q_HEADS * HEAD_DIM, W), lambda i: (0, i))
    prev = lambda b: pl.BlockSpec((kv_rows, W), lambda i: (b, jnp.maximum(i - 1, 0)))
    cur = lambda b: pl.BlockSpec((kv_rows, W), lambda i: (b, i))
    bias = pl.BlockSpec((SWA_KV_HEADS, 2 * W, GW), lambda i: (0, 0, 0))
    stat = pl.BlockSpec((SWA_Q_HEADS, W), lambda i: (0, i))
    sink = pl.BlockSpec(memory_space=pltpu.SMEM)
    return q, prev(4), cur(4), prev(5), cur(5), bias, stat, sink


def _group_lanes(ref, g, rows_per_head):
    h0 = g * SWA_GROUP
    return jnp.concatenate([ref[(h0 + j) * rows_per_head:(h0 + j + 1) * rows_per_head, :] for j in range(SWA_GROUP)], axis=1)


def _swa_scores(g, q_ref, kp_ref, kc_ref, b_ref, masks):
    rows = slice(g * HEAD_DIM, (g + 1) * HEAD_DIM)
    qg = _group_lanes(q_ref, g, HEAD_DIM)
    scale = HEAD_DIM ** -0.5
    s_p = jnp.where(masks[0], _dot(kp_ref[rows, :], qg, TN) * scale + b_ref[g, 0:WINDOW, :], NEG_INF)
    s_c = jnp.where(masks[1], _dot(kc_ref[rows, :], qg, TN) * scale + b_ref[g, WINDOW:2 * WINDOW, :], NEG_INF)
    return qg, rows, s_p, s_c


def _sink_row(sink_ref, g):
    return jnp.concatenate([jnp.full((1, WINDOW), sink_ref[g * SWA_GROUP + j], F32) for j in range(SWA_GROUP)], axis=1)


def _swa_fwd(qkv, bias_g, sinks, *, name):
    S = qkv.shape[1]
    qs, kp, kc, vp, vc, bs, stat, sk = _swa_specs()
    gs = range(SWA_KV_HEADS)

    def body(sink_ref, q_ref, kp_ref, kc_ref, vp_ref, vc_ref, b_ref, o_ref, lse_ref):
        masks = _swa_masks(pl.program_id(0))
        sc = [_swa_scores(g, q_ref, kp_ref, kc_ref, b_ref, masks) for g in gs]
        sinks_g = [_sink_row(sink_ref, g) for g in gs]
        m = [jnp.maximum(jnp.maximum(jnp.max(sc[g][2], axis=0, keepdims=True), jnp.max(sc[g][3], axis=0, keepdims=True)),
                         sinks_g[g]) for g in gs]
        p_p = [jnp.exp(sc[g][2] - m[g]) for g in gs]
        p_c = [jnp.exp(sc[g][3] - m[g]) for g in gs]
        l = [jnp.sum(p_p[g], axis=0, keepdims=True) + jnp.sum(p_c[g], axis=0, keepdims=True) + jnp.exp(sinks_g[g] - m[g])
             for g in gs]
        o = [_dot(vp_ref[sc[g][1], :], p_p[g].astype(BF16)) + _dot(vc_ref[sc[g][1], :], p_c[g].astype(BF16)) for g in gs]
        for g in gs:
            og = o[g] / l[g]
            lse = m[g] + jnp.log(l[g])
            for j in range(SWA_GROUP):
                h = g * SWA_GROUP + j
                o_ref[h * HEAD_DIM:(h + 1) * HEAD_DIM, :] = og[:, j * WINDOW:(j + 1) * WINDOW]
                lse_ref[h:h + 1, :] = lse[:, j * WINDOW:(j + 1) * WINDOW]

    return pl.pallas_call(
        body, name=name, grid=(S // WINDOW,),
        in_specs=[sk, qs, kp, kc, vp, vc, bs],
        out_specs=[qs, stat],
        out_shape=[jax.ShapeDtypeStruct((SWA_Q_HEADS * HEAD_DIM, S), F32), jax.ShapeDtypeStruct((SWA_Q_HEADS, S), F32)],
        compiler_params=_params("parallel"),
    )(sinks, qkv, qkv, qkv, qkv, qkv, bias_g)


def _swa_bwd(qkv, bias_g, sinks, do, lse, delta, *, name):
    S = qkv.shape[1]
    W = WINDOW
    qs, kp, kc, vp, vc, bs, stat, sk = _swa_specs()
    scale = HEAD_DIM ** -0.5
    kv_rows = SWA_KV_HEADS * HEAD_DIM
    gs = range(SWA_KV_HEADS)

    def body(sink_ref, q_ref, kp_ref, kc_ref, vp_ref, vc_ref, b_ref, do_ref, lse_ref, dl_ref,
             dq_ref, dkv_ref, db_ref, dsk_ref):
        i = pl.program_id(0)

        @pl.when(i == 0)
        def _():
            dkv_ref[...] = jnp.zeros_like(dkv_ref)
            db_ref[...] = jnp.zeros_like(db_ref)
            dsk_ref[...] = jnp.zeros_like(dsk_ref)

        masks = _swa_masks(i)
        prev = pl.ds(pl.multiple_of(jnp.maximum(i - 1, 0) * W, W), W)
        cur = pl.ds(pl.multiple_of(i * W, W), W)
        sc = [_swa_scores(g, q_ref, kp_ref, kc_ref, b_ref, masks) for g in gs]
        dog = [_group_lanes(do_ref, g, HEAD_DIM) for g in gs]
        lse = [_group_lanes(lse_ref, g, 1) for g in gs]
        dl = [_group_lanes(dl_ref, g, 1) for g in gs]
        p_p = [jnp.exp(sc[g][2] - lse[g]) for g in gs]
        p_c = [jnp.exp(sc[g][3] - lse[g]) for g in gs]
        ds_p = [p_p[g] * (_dot(vp_ref[sc[g][1], :], dog[g], TN) - dl[g]) for g in gs]
        ds_c = [p_c[g] * (_dot(vc_ref[sc[g][1], :], dog[g], TN) - dl[g]) for g in gs]
        for g in gs:
            db_ref[g, 0:W, :] += ds_p[g]
            db_ref[g, W:2 * W, :] += ds_c[g]
            dsk = jnp.exp(_sink_row(sink_ref, g) - lse[g]) * dl[g]
            for j in range(SWA_GROUP):
                h = g * SWA_GROUP + j
                dsk_ref[h:h + 1, :] -= jnp.broadcast_to(jnp.sum(dsk[:, j * W:(j + 1) * W], axis=1, keepdims=True), (1, LANES))
        dsb_p = [d.astype(BF16) for d in ds_p]
        dsb_c = [d.astype(BF16) for d in ds_c]
        for g in gs:
            rows = sc[g][1]
            dq = (_dot(kp_ref[rows, :], dsb_p[g]) + _dot(kc_ref[rows, :], dsb_c[g])) * scale
            for j in range(SWA_GROUP):
                h = g * SWA_GROUP + j
                dq_ref[h * HEAD_DIM:(h + 1) * HEAD_DIM, :] = dq[:, j * W:(j + 1) * W]
        for g in gs:
            rows = sc[g][1]
            vrows = slice(kv_rows + rows.start, kv_rows + rows.stop)
            dkv_ref[rows, prev] += _dot(sc[g][0], dsb_p[g], NT) * scale
            dkv_ref[rows, cur] += _dot(sc[g][0], dsb_c[g], NT) * scale
            dkv_ref[vrows, prev] += _dot(dog[g], p_p[g].astype(BF16), NT)
            dkv_ref[vrows, cur] += _dot(dog[g], p_c[g].astype(BF16), NT)

    return pl.pallas_call(
        body, name=name, grid=(S // W,),
        in_specs=[sk, qs, kp, kc, vp, vc, bs, qs, stat, stat],
        out_specs=[qs, pl.BlockSpec((2 * kv_rows, S), lambda i: (0, 0)), bs, pl.BlockSpec((SWA_Q_HEADS, LANES), lambda i: (0, 0))],
        out_shape=[jax.ShapeDtypeStruct((SWA_Q_HEADS * HEAD_DIM, S), F32), jax.ShapeDtypeStruct((2 * kv_rows, S), F32),
                   jax.ShapeDtypeStruct((SWA_KV_HEADS, 2 * W, GW), F32), jax.ShapeDtypeStruct((SWA_Q_HEADS, LANES), F32)],
        compiler_params=_params("arbitrary"),
    )(sinks, qkv, qkv, qkv, qkv, qkv, bias_g, do, lse, delta)


def _rel_onehot_t():
    qi = jnp.arange(WINDOW, dtype=jnp.int32)[None, :] + WINDOW
    kj = jnp.arange(2 * WINDOW, dtype=jnp.int32)[:, None]
    dist = qi - kj
    max_exact = REL_BUCKETS // 2
    d = jnp.maximum(dist, 0)
    log_ratio = jnp.log(jnp.maximum(d, 1).astype(F32) / max_exact) / math.log(REL_MAX_DIST / max_exact)
    large = jnp.minimum(max_exact + (log_ratio * (REL_BUCKETS - max_exact)).astype(jnp.int32), REL_BUCKETS - 1)
    bucket = jnp.where(d < max_exact, d, large).reshape(-1)
    return (bucket[None, :] == jnp.arange(REL_BUCKETS, dtype=jnp.int32)[:, None]).astype(BF16)


def _bias_table(rel_bias_t, onehot_t):
    Hq, NB = rel_bias_t.shape
    N = onehot_t.shape[1]
    tn = _tile(N, 4096)

    def body(r_ref, oh_ref, o_ref):
        oh = oh_ref[...]
        a1, a2, a3 = _split3(r_ref[...])
        o_ref[...] = _dot(a1, oh) + _dot(a2, oh) + _dot(a3, oh)

    return pl.pallas_call(
        body, name="rel_bias_table", grid=(N // tn,),
        in_specs=[pl.BlockSpec((Hq, NB), lambda j: (0, 0)), pl.BlockSpec((NB, tn), lambda j: (0, j))],
        out_specs=pl.BlockSpec((Hq, tn), lambda j: (0, j)),
        out_shape=jax.ShapeDtypeStruct((Hq, N), F32),
        compiler_params=_params("parallel"),
    )(rel_bias_t, onehot_t)


def _bias_table_bwd(dbias, onehot_t):
    L, Hq, N = dbias.shape
    NB = onehot_t.shape[0]
    tn = _tile(N, 4096)

    def body(d_ref, oh_ref, o_ref):
        @pl.when(pl.program_id(0) == 0)
        def _():
            o_ref[...] = jnp.zeros_like(o_ref)

        d = d_ref[0]
        for l in range(1, L):
            d = d + d_ref[l]
        oh = oh_ref[...]
        a1, a2, a3 = _split3(d)
        o_ref[...] += _dot(a1, oh, NT) + _dot(a2, oh, NT) + _dot(a3, oh, NT)

    return pl.pallas_call(
        body, name="rel_bias_bwd", grid=(N // tn,),
        in_specs=[pl.BlockSpec((L, Hq, tn), lambda j: (0, 0, j)), pl.BlockSpec((NB, tn), lambda j: (0, j))],
        out_specs=pl.BlockSpec((Hq, NB), lambda j: (0, 0)),
        out_shape=jax.ShapeDtypeStruct((Hq, NB), F32),
        compiler_params=_params("arbitrary"),
    )(dbias, onehot_t)


def _gate_fwd(lat, fb_col, *, name):
    S = lat.shape[1]
    tn = _tile(S, 256)

    def body(z_ref, fb_ref, o_ref, carry):
        @pl.when(pl.program_id(0) == 0)
        def _():
            carry[...] = jnp.zeros_like(carry)

        z = z_ref[...] + fb_ref[...]
        lf = jnp.minimum(z, 0.0) - jnp.log1p(jnp.exp(-jnp.abs(z)))
        r = lax.broadcasted_iota(jnp.int32, (tn, tn), 0)
        c = lax.broadcasted_iota(jnp.int32, (tn, tn), 1)
        tri = (r <= c).astype(BF16)
        a1, a2, a3 = _split3(lf)
        cum = _dot(a1, tri) + _dot(a2, tri) + _dot(a3, tri) + carry[:, 0:1]
        o_ref[...] = cum
        carry[...] = jnp.broadcast_to(cum[:, tn - 1:tn], carry.shape)

    return pl.pallas_call(
        body, name=name, grid=(S // tn,),
        in_specs=[pl.BlockSpec((GATE_ROWS, tn), lambda i: (0, i)), pl.BlockSpec((GATE_ROWS, 1), lambda i: (0, 0))],
        out_specs=pl.BlockSpec((GATE_ROWS, tn), lambda i: (0, i)),
        out_shape=jax.ShapeDtypeStruct((GATE_ROWS, S), F32),
        scratch_shapes=[pltpu.VMEM((GATE_ROWS, LANES), F32)],
        compiler_params=_params("arbitrary"),
    )(lat, fb_col)


def _gate_bwd(lat, fb_col, dF, *, name):
    S = lat.shape[1]
    tn = _tile(S, 256)
    nt = S // tn

    def body(z_ref, fb_ref, df_ref, dz_ref, dfb_ref, carry):
        @pl.when(pl.program_id(0) == 0)
        def _():
            carry[...] = jnp.zeros_like(carry)
            dfb_ref[...] = jnp.zeros_like(dfb_ref)

        r = lax.broadcasted_iota(jnp.int32, (tn, tn), 0)
        c = lax.broadcasted_iota(jnp.int32, (tn, tn), 1)
        tri = (r >= c).astype(BF16)
        a1, a2, a3 = _split3(df_ref[...])
        dlf = _dot(a1, tri) + _dot(a2, tri) + _dot(a3, tri) + carry[:, 0:1]
        carry[...] = jnp.broadcast_to(dlf[:, 0:1], carry.shape)
        z = z_ref[...] + fb_ref[...]
        row = lax.broadcasted_iota(jnp.int32, (GATE_ROWS, tn), 0)
        dz = jnp.where(row < FOX_HEADS, dlf / (1.0 + jnp.exp(z)), 0.0)
        dz_ref[...] = dz
        dfb_ref[...] += jnp.sum(dz, axis=1, keepdims=True)

    blk = pl.BlockSpec((GATE_ROWS, tn), lambda i: (0, nt - 1 - i))
    vec = pl.BlockSpec((GATE_ROWS, 1), lambda i: (0, 0))
    return pl.pallas_call(
        body, name=name, grid=(nt,),
        in_specs=[blk, vec, blk], out_specs=[blk, vec],
        out_shape=[jax.ShapeDtypeStruct((GATE_ROWS, S), F32), jax.ShapeDtypeStruct((GATE_ROWS, 1), F32)],
        scratch_shapes=[pltpu.VMEM((GATE_ROWS, LANES), F32)],
        compiler_params=_params("arbitrary"),
    )(lat, fb_col, dF)


def _rope_tables(S):
    pos = jnp.arange(S, dtype=F32)
    inv_freq = ROPE_THETA ** (-(jnp.arange(MLA_ROPE // 2, dtype=F32) * 2.0 / MLA_ROPE))
    ang = pos[:, None] * inv_freq[None, :]
    cos, sin = jnp.cos(ang).T, jnp.sin(ang).T
    z16 = jnp.zeros_like(cos)

    def slab(lo, fill):
        def put(first, second, f):
            return jnp.concatenate([jnp.full((lo, S), f, F32), first, second, jnp.full((LANES - lo - MLA_ROPE, S), f, F32)], axis=0)
        return put(cos, cos, fill), put(-sin, z16, 0.0), put(z16, sin, 0.0)

    tq = tuple(jnp.tile(t, (MLA_HEADS, 1)) for t in slab(MLA_NOPE, 1.0))
    return tq, slab(0, 0.0)


def _rope(x, c, s1, s2):
    n = x.shape[0]
    half = MLA_ROPE // 2
    return x * c + pltpu.roll(x, n - half, 0) * s1 + pltpu.roll(x, half, 0) * s2


def _rope_t(dy, c, s1, s2):
    n = dy.shape[0]
    half = MLA_ROPE // 2
    return dy * c + pltpu.roll(dy * s1, half, 0) + pltpu.roll(dy * s2, n - half, 0)


KR_SLAB0 = MLA_Q_RANK + MLA_KV_RANK


def _mla_prep_fwd(lat, g_q, g_kv, w_uq_t, w_ukv_t, tq, tmisc, *, name):
    S = lat.shape[1]
    tn = _tile(S, 512)
    QW = MLA_HEADS * MLA_PAD

    def body(lat_ref, gq_ref, gkv_ref, wq_ref, wkv_ref, c_ref, s1_ref, s2_ref, cm_ref, s1m_ref, s2m_ref,
             nq_ref, nkv_ref, q_ref, k_ref, v_ref):
        x = pltpu.roll(lat_ref[...], LAT_ROWS - LAT_SHIFT, 0)
        nq = _col_rms(x[0:MLA_Q_RANK, :], gq_ref[...]).astype(BF16)
        nkv = _col_rms(x[MLA_Q_RANK:KR_SLAB0, :], gkv_ref[...]).astype(BF16)
        nq_ref[...] = nq
        nkv_ref[...] = nkv
        q = _rope(_dot(wq_ref[...], nq), c_ref[...], s1_ref[...], s2_ref[...])
        q_ref[...] = (q * (MLA_SCALE * LOG2E)).astype(BF16)
        kv = _dot(wkv_ref[...], nkv).astype(BF16)
        kr = _rope(x[KR_SLAB0:LAT_ROWS, :], cm_ref[...], s1m_ref[...], s2m_ref[...]).astype(BF16)
        for h in range(MLA_HEADS):
            k_ref[h * MLA_PAD:h * MLA_PAD + MLA_NOPE, :] = kv[h * LANES:h * LANES + MLA_NOPE, :]
            k_ref[h * MLA_PAD + MLA_NOPE:(h + 1) * MLA_PAD, :] = kr[0:MLA_PAD - MLA_NOPE, :]
            v_ref[h * HEAD_DIM:(h + 1) * HEAD_DIM, :] = kv[h * LANES + MLA_NOPE:(h + 1) * LANES, :]

    def col(rows):
        return pl.BlockSpec((rows, tn), lambda i: (0, i))

    def full(a):
        return pl.BlockSpec(a.shape, lambda i: (0, 0))

    return pl.pallas_call(
        body, name=name, grid=(S // tn,),
        in_specs=[col(LAT_ROWS), full(g_q), full(g_kv), full(w_uq_t), full(w_ukv_t),
                  col(QW), col(QW), col(QW), col(LANES), col(LANES), col(LANES)],
        out_specs=[col(MLA_Q_RANK), col(MLA_KV_RANK), col(QW), col(QW), col(MLA_HEADS * HEAD_DIM)],
        out_shape=[jax.ShapeDtypeStruct((MLA_Q_RANK, S), BF16), jax.ShapeDtypeStruct((MLA_KV_RANK, S), BF16),
                   jax.ShapeDtypeStruct((QW, S), BF16), jax.ShapeDtypeStruct((QW, S), BF16),
                   jax.ShapeDtypeStruct((MLA_HEADS * HEAD_DIM, S), BF16)],
        compiler_params=_params("parallel"),
    )(lat, g_q, g_kv, w_uq_t, w_ukv_t, *tq, *tmisc)


def _mla_prep_bwd(lat, nq, nkv, g_q, g_kv, w_uq_p, w_ukv, tq, tmisc, dq, dk, dv, dflog, *, name):
    S = lat.shape[1]
    tn = _tile(S, 512)
    QW = MLA_HEADS * MLA_PAD

    def body(lat_ref, nq_ref, nkv_ref, gq_ref, gkv_ref, wq_ref, wkv_ref, c_ref, s1_ref, s2_ref,
             cm_ref, s1m_ref, s2m_ref, dq_ref, dk_ref, dv_ref, dfl_ref,
             dlat_ref, dwq_ref, dwkv_ref, dgq_ref, dgkv_ref, y_s):
        @pl.when(pl.program_id(0) == 0)
        def _():
            dwq_ref[...] = jnp.zeros_like(dwq_ref)
            dwkv_ref[...] = jnp.zeros_like(dwkv_ref)
            dgq_ref[...] = jnp.zeros_like(dgq_ref)
            dgkv_ref[...] = jnp.zeros_like(dgkv_ref)

        x = pltpu.roll(lat_ref[...], LAT_ROWS - LAT_SHIFT, 0)
        dqm = _rope_t(dq_ref[...], c_ref[...], s1_ref[...], s2_ref[...]).astype(BF16)
        dwq_ref[...] += _dot(dqm, nq_ref[...], NT)
        dx, dg = _col_rms_bwd(x[0:MLA_Q_RANK, :], gq_ref[...], _dot(wq_ref[...], dqm))
        y_s[0:MLA_Q_RANK, :] = dx
        dgq_ref[...] += dg
        dkv = jnp.concatenate(
            [part for h in range(MLA_HEADS)
             for part in (dk_ref[h * MLA_PAD:h * MLA_PAD + MLA_NOPE, :], dv_ref[h * HEAD_DIM:(h + 1) * HEAD_DIM, :])],
            axis=0).astype(BF16)
        dwkv_ref[...] += _dot(dkv, nkv_ref[...], NT)
        dx, dg = _col_rms_bwd(x[MLA_Q_RANK:KR_SLAB0, :], gkv_ref[...], _dot(wkv_ref[...], dkv))
        y_s[MLA_Q_RANK:KR_SLAB0, :] = dx
        dgkv_ref[...] += dg
        dkr = dk_ref[MLA_NOPE:MLA_PAD, :]
        for h in range(1, MLA_HEADS):
            dkr = dkr + dk_ref[h * MLA_PAD + MLA_NOPE:(h + 1) * MLA_PAD, :]
        dkr = jnp.concatenate([dkr, jnp.zeros((MLA_NOPE, tn), F32)], axis=0)
        y_s[KR_SLAB0:LAT_ROWS, :] = _rope_t(dkr, cm_ref[...], s1m_ref[...], s2m_ref[...])
        y = pltpu.roll(y_s[...], LAT_SHIFT, 0)
        row = lax.broadcasted_iota(jnp.int32, (LAT_ROWS, tn), 0)
        dfl = jnp.concatenate([dfl_ref[...], jnp.zeros((LAT_ROWS - GATE_ROWS, tn), F32)], axis=0)
        dlat_ref[...] = jnp.where(row < LAT_SHIFT, dfl, y).astype(BF16)

    def col(rows):
        return pl.BlockSpec((rows, tn), lambda i: (0, i))

    def full(a):
        return pl.BlockSpec(a.shape, lambda i: (0, 0))

    def acc(r, c):
        return pl.BlockSpec((r, c), lambda i: (0, 0))

    return pl.pallas_call(
        body, name=name, grid=(S // tn,),
        in_specs=[col(LAT_ROWS), col(MLA_Q_RANK), col(MLA_KV_RANK), full(g_q), full(g_kv),
                  full(w_uq_p), full(w_ukv), col(QW), col(QW), col(QW), col(LANES), col(LANES), col(LANES),
                  col(QW), col(QW), col(MLA_HEADS * HEAD_DIM), col(GATE_ROWS)],
        out_specs=[col(LAT_ROWS), acc(QW, MLA_Q_RANK), acc(QW, MLA_KV_RANK), acc(MLA_Q_RANK, 1), acc(MLA_KV_RANK, 1)],
        out_shape=[jax.ShapeDtypeStruct((LAT_ROWS, S), BF16), jax.ShapeDtypeStruct((QW, MLA_Q_RANK), F32),
                   jax.ShapeDtypeStruct((QW, MLA_KV_RANK), F32), jax.ShapeDtypeStruct((MLA_Q_RANK, 1), F32),
                   jax.ShapeDtypeStruct((MLA_KV_RANK, 1), F32)],
        scratch_shapes=[pltpu.VMEM((LAT_ROWS, tn), F32)],
        compiler_params=_params("arbitrary"),
    )(lat, nq, nkv, g_q, g_kv, w_uq_p, w_ukv, *tq, *tmisc, dq, dk, dv, dflog)


def _dproj_cast(dqa, dkva, dqf, dkf, dvf, dlat, *, name):
    S = dqa.shape[1]
    tn = _tile(S, 512)
    parts = (dqa, dkva, dqf, dkf, dvf, dlat)

    def body(*refs):
        o_ref = refs[-1]
        r0 = 0
        for ref in refs[:-1]:
            n = ref.shape[0]
            o_ref[r0:r0 + n, :] = ref[...].astype(BF16)
            r0 += n

    return pl.pallas_call(
        body, name=name, grid=(S // tn,),
        in_specs=[pl.BlockSpec((p.shape[0], tn), lambda i: (0, i)) for p in parts],
        out_specs=pl.BlockSpec((IN_ROWS, tn), lambda i: (0, i)),
        out_shape=jax.ShapeDtypeStruct((IN_ROWS, S), BF16),
        compiler_params=_params("parallel"),
    )(*parts)


GELU_C = math.sqrt(2.0 / math.pi)
GELU_A = 0.044715


HALO = 16


def _shift_down(a, k, fill):
    r = pltpu.roll(a, k, 0)
    row = lax.broadcasted_iota(jnp.int32, (8, a.shape[1]), 0)
    head = r[0:8, :]
    for i in range(k):
        head = jnp.where(row == i, fill[len(fill) - k + i], head)
    return jnp.concatenate([head, r[8:, :]], axis=0)


def _shift_up(d, k, fill):
    n = d.shape[0]
    r = pltpu.roll(d, n - k, 0)
    row = lax.broadcasted_iota(jnp.int32, (8, d.shape[1]), 0)
    tail = r[n - 8:n, :]
    for i in range(k):
        tail = jnp.where(row == 8 - k + i, fill[i], tail)
    return jnp.concatenate([r[0:n - 8, :], tail], axis=0)


def _conv_taps(a, before, w_ref, b_ref):
    a1 = _shift_down(a, 1, before)
    a2 = _shift_down(a, 2, before)
    return ((b_ref[...] + w_ref[0:1, :] * a2) + w_ref[1:2, :] * a1) + w_ref[2:3, :] * a


def _rows_before(halo_ref, first):
    h = halo_ref[HALO - 2:HALO, :].astype(F32)
    return jnp.where(first, 0.0, h[0:1, :]), jnp.where(first, 0.0, h[1:2, :])


def _conv_specs(S, tm, tc, nc):
    hb = tm // HALO
    main = lambda off: pl.BlockSpec((tm, tc), lambda j, i: (i, j + off))
    prev = lambda off: pl.BlockSpec((HALO, tc), lambda j, i: (jnp.maximum(i * hb - 1, 0), j + off))
    wspec = lambda off: pl.BlockSpec((3, tc), lambda j, i: (0, j + off))
    bspec = lambda off: pl.BlockSpec((1, tc), lambda j, i: (0, j + off))
    return main, prev, wspec, bspec


def _conv_geglu_fwd(a, conv_w, conv_b, *, name):
    S = a.shape[0]
    tm, tc = _tile(S, 512), _tile(D_FF, 1408)
    nc = D_FF // tc
    main, prev, wspec, bspec = _conv_specs(S, tm, tc, nc)

    def body(ag_ref, au_ref, hg_ref, hu_ref, wg_ref, wu_ref, bg_ref, bu_ref, u_ref, z_ref):
        first = pl.program_id(1) == 0
        gate = _conv_taps(ag_ref[...].astype(F32), _rows_before(hg_ref, first), wg_ref, bg_ref)
        up = _conv_taps(au_ref[...].astype(F32), _rows_before(hu_ref, first), wu_ref, bu_ref)
        u_ref[0] = gate
        u_ref[1] = up
        cdf = 0.5 * (1.0 + jnp.tanh(GELU_C * (gate + GELU_A * (gate * gate * gate))))
        z_ref[...] = (gate * cdf * up).astype(BF16)

    return pl.pallas_call(
        body, name=name, grid=(nc, S // tm),
        in_specs=[main(0), main(nc), prev(0), prev(nc), wspec(0), wspec(nc), bspec(0), bspec(nc)],
        out_specs=[pl.BlockSpec((2, tm, tc), lambda j, i: (0, i, j)), pl.BlockSpec((tm, tc), lambda j, i: (i, j))],
        out_shape=[jax.ShapeDtypeStruct((2, S, D_FF), F32), jax.ShapeDtypeStruct((S, D_FF), BF16)],
        compiler_params=_params("parallel", "arbitrary"),
    )(a, a, a, a, conv_w, conv_w, conv_b, conv_b)


def _geglu_bwd(gate, up, dz):
    g2x = gate * gate
    th = jnp.tanh(GELU_C * (gate + GELU_A * (g2x * gate)))
    cdf = 0.5 * (1.0 + th)
    dgelu = cdf + gate * (0.5 * (1.0 - th * th) * (GELU_C * (1.0 + 3.0 * GELU_A * g2x)))
    return dz * up * dgelu, dz * (gate * cdf)


def _conv_geglu_bwd(a, u, conv_w, dz, *, name):
    S = a.shape[0]
    tm, tc = _tile(S, 512), _tile(D_FF, 1408)
    nc = D_FF // tc
    nr = S // tm
    main, _, wspec, _ = _conv_specs(S, tm, tc, nc)
    hb = tm // 8

    def body(ag_ref, au_ref, u_ref, un_ref, wg_ref, wu_ref, dz_ref, dzn_ref, da_ref, dw_ref, db_ref):
        i = pl.program_id(1)
        last = i == nr - 1

        @pl.when(i == 0)
        def _():
            dw_ref[...] = jnp.zeros_like(dw_ref)
            db_ref[...] = jnp.zeros_like(db_ref)

        dus = _geglu_bwd(u_ref[0], u_ref[1], dz_ref[...])
        dus_n = _geglu_bwd(un_ref[0], un_ref[1], dzn_ref[...])
        for half, a_ref, w_ref in ((0, ag_ref, wg_ref), (1, au_ref, wu_ref)):
            du, du_n = dus[half], dus_n[half]
            after = (jnp.where(last, 0.0, du_n[0:1, :]), jnp.where(last, 0.0, du_n[1:2, :]))
            shifted = (_shift_up(du, 2, after), _shift_up(du, 1, after), du)
            da_ref[half] = (w_ref[2:3, :] * du + w_ref[1:2, :] * shifted[1] + w_ref[0:1, :] * shifted[0]).astype(BF16)
            af = a_ref[...].astype(F32)
            for tap in range(3):
                dw_ref[half, tap:tap + 1, :] += jnp.sum(shifted[tap] * af, axis=0, keepdims=True)
            db_ref[half] += jnp.sum(du, axis=0, keepdims=True)

    nxt8 = lambda j, i: (0, jnp.minimum((i + 1) * hb, S // 8 - 1), j)
    return pl.pallas_call(
        body, name=name, grid=(nc, nr),
        in_specs=[main(0), main(nc), pl.BlockSpec((2, tm, tc), lambda j, i: (0, i, j)), pl.BlockSpec((2, 8, tc), nxt8),
                  wspec(0), wspec(nc), pl.BlockSpec((tm, tc), lambda j, i: (i, j)),
                  pl.BlockSpec((8, tc), lambda j, i: (jnp.minimum((i + 1) * hb, S // 8 - 1), j))],
        out_specs=[pl.BlockSpec((2, tm, tc), lambda j, i: (0, i, j)), pl.BlockSpec((2, 3, tc), lambda j, i: (0, 0, j)),
                   pl.BlockSpec((2, 1, tc), lambda j, i: (0, 0, j))],
        out_shape=[jax.ShapeDtypeStruct((2, S, D_FF), BF16), jax.ShapeDtypeStruct((2, 3, D_FF), F32),
                   jax.ShapeDtypeStruct((2, 1, D_FF), F32)],
        compiler_params=_params("parallel", "arbitrary"),
    )(a, a, u, u, conv_w, conv_w, dz, dz)


ROW_BLOCK_BYTES = 1536 * 1024


def _row_tile(rows, cols):
    return rows if rows * cols * 4 <= ROW_BLOCK_BYTES else _tile(rows, ROW_TILE)


def _adamw_update(w, g, m, v):
    m = ADAM_B1 * m + (1.0 - ADAM_B1) * g
    v = ADAM_B2 * v + (1.0 - ADAM_B2) * jnp.square(g)
    m_hat = m / (1.0 - ADAM_B1 ** ADAM_STEP)
    v_hat = v / (1.0 - ADAM_B2 ** ADAM_STEP)
    return -ADAM_LR * (m_hat / (jnp.sqrt(v_hat) + ADAM_EPS) + ADAM_WD * w), m, v


def _adamw(w, g, m, v, *, name):
    L, A, B = w.shape
    ta = _tile(A, ROW_TILE)

    def body(w_ref, g_ref, m_ref, v_ref, d_ref, mo_ref, vo_ref):
        d_ref[...], mo_ref[...], vo_ref[...] = _adamw_update(w_ref[...], g_ref[...], m_ref[...], v_ref[...])

    blk = pl.BlockSpec((None, ta, B), lambda l, i: (l, i, 0))
    shp = jax.ShapeDtypeStruct((L, A, B), F32)
    return pl.pallas_call(
        body, name=name, grid=(L, A // ta),
        in_specs=[blk] * 4, out_specs=[blk] * 3, out_shape=[shp] * 3,
        compiler_params=_params("parallel", "parallel"),
    )(w, g, m, v)


def _scalar(v):
    return jnp.reshape(v, (1,)).astype(jnp.int32)


def _adamw_halves(w, g_mine, g_other, m, v, *, name):
    L, A, B = w.shape
    ta = _row_tile(A // 2, B)
    nb = A // 2 // ta

    def body(c_ref, w_ref, gm_ref, go_ref, m_ref, v_ref, g_ref, d_ref, mo_ref, vo_ref):
        g = jnp.where(pl.program_id(1) // nb == c_ref[0], gm_ref[...], go_ref[...])
        g_ref[...] = g
        d_ref[...], mo_ref[...], vo_ref[...] = _adamw_update(w_ref[...], g, m_ref[...], v_ref[...])

    blk = pl.BlockSpec((None, ta, B), lambda l, i, c_ref: (l, i, 0))
    half = pl.BlockSpec((None, ta, B), lambda l, i, c_ref: (l, i % nb, 0))
    shp = jax.ShapeDtypeStruct((L, A, B), F32)
    return pl.pallas_call(
        body, name=name,
        grid_spec=pltpu.PrefetchScalarGridSpec(num_scalar_prefetch=1, grid=(L, A // ta),
                                               in_specs=[blk, half, half, blk, blk], out_specs=[blk] * 4),
        out_shape=[shp] * 4,
        compiler_params=_params("parallel", "parallel"),
    )(_scalar(lax.axis_index("c")), w, g_mine, g_other, m, v)


def _chip_index():
    return 2 * lax.axis_index("x") + lax.axis_index("y")


def _pair_sum(g, recv, *, name):
    n, A, B = g.shape
    ta = _row_tile(A // 2, B)
    nb = A // 2 // ta

    def body(c_ref, g_ref, r_ref, o_ref):
        o_ref[...] = g_ref[...] + r_ref[...]

    return pl.pallas_call(
        body, name=name,
        grid_spec=pltpu.PrefetchScalarGridSpec(
            num_scalar_prefetch=1, grid=(n, nb),
            in_specs=[pl.BlockSpec((None, ta, B), lambda s, r, c_ref: (s, c_ref[0] * nb + r, 0)),
                      pl.BlockSpec((None, ta, B), lambda s, r, c_ref: (s, r, 0))],
            out_specs=pl.BlockSpec((None, ta, B), lambda s, r, c_ref: (s, r, 0))),
        out_shape=jax.ShapeDtypeStruct((n, A // 2, B), F32),
        compiler_params=_params("parallel", "parallel"),
    )(_scalar(lax.axis_index("c")), g, recv)


def _chip_sum(landed, own, *, name):
    n, A2, B = landed.shape
    ta = _row_tile(A2, B)

    def body(me_ref, *refs):
        slots, own_ref, o_ref = refs[:n], refs[n], refs[n + 1]
        parts = [jnp.where(me_ref[0] == s, own_ref[...], slots[s][...]) for s in range(n)]
        o_ref[...] = ((parts[0] + parts[1]) + parts[2]) + parts[3]

    def slot(s):
        return pl.BlockSpec((None, ta, B), lambda r, me_ref: (jnp.where(me_ref[0] == s, (s + 1) % n, s), r, 0))

    return pl.pallas_call(
        body, name=name,
        grid_spec=pltpu.PrefetchScalarGridSpec(
            num_scalar_prefetch=1, grid=(A2 // ta,),
            in_specs=[slot(s) for s in range(n)] + [pl.BlockSpec((None, ta, B), lambda r, me_ref: (me_ref[0], r, 0))],
            out_specs=pl.BlockSpec((ta, B), lambda r, me_ref: (r, 0))),
        out_shape=jax.ShapeDtypeStruct((A2, B), F32),
        compiler_params=_params("parallel"),
    )(_scalar(_chip_index()), *([landed] * n), own)


HBM_SPEC = pl.BlockSpec(memory_space=pl.ANY)
COMM_PARAMS = pltpu.CompilerParams(has_side_effects=True)


def _mesh_pos():
    return lax.axis_index("x"), lax.axis_index("y"), lax.axis_index("c")


def _other_chips(x, y):
    return [(1 - x, y), (x, 1 - y), (1 - x, 1 - y)]


def _remote(src, dst, send_sems, recv_sems, k, to):
    return pltpu.make_async_remote_copy(src_ref=src, dst_ref=dst, send_sem=send_sems.at[k], recv_sem=recv_sems.at[k],
                                        device_id=to, device_id_type=MESH)


def _place_own(gathered, shards, *, name):
    n = len(shards)

    def body(me_ref, *refs):
        for s_ref, o_ref in zip(refs[:n], refs[2 * n:]):
            o_ref[...] = s_ref[...]

    return pl.pallas_call(
        body, name=name,
        grid_spec=pltpu.PrefetchScalarGridSpec(
            num_scalar_prefetch=1, grid=(1,),
            in_specs=[pl.BlockSpec(s.shape, lambda i, me_ref: (0, 0)) for s in shards] + [HBM_SPEC] * n,
            out_specs=[pl.BlockSpec((None,) + s.shape, lambda i, me_ref: (me_ref[0], 0, 0)) for s in shards]),
        out_shape=[jax.ShapeDtypeStruct(g.shape, g.dtype) for g in gathered],
        input_output_aliases={1 + n + k: k for k in range(n)},
        compiler_params=_params("arbitrary"),
    )(_scalar(_chip_index()), *shards, *gathered)


def _half_rows(rows, c, align=8):
    assert (rows // 2) % align == 0
    return pl.ds(pl.multiple_of(c * (rows // 2), align), rows // 2)


BF16_ROWS = 16


def _halved(rows):
    return rows % (2 * BF16_ROWS) == 0


def _gather_copies(srcs, lands, send_sems, recv_sems):
    x, y, c = _mesh_pos()
    me = 2 * x + y
    out = []
    for k in range(len(srcs)):
        a = srcs[k].shape[0]
        rows = _half_rows(a, c, BF16_ROWS) if _halved(a) else pl.ds(0, a)
        for j, (px, py) in enumerate(_other_chips(x, y)):
            send = _remote(srcs[k].at[rows], lands[k].at[me, rows], send_sems, recv_sems, 3 * k + j, (px, py, c))
            recv = _remote(srcs[k].at[rows], lands[k].at[2 * px + py, rows], send_sems, recv_sems, 3 * k + j, (px, py, c))
            out.append((send, recv))
    return out


def _gather_start(srcs):
    nu = len(srcs)
    sizes = [len(su) for su in srcs]
    offs = [2 * sum(sizes[:u]) for u in range(nu + 1)]
    lands = [[lax.empty((N_CHIPS,) + s.shape, s.dtype) for s in su] for su in srcs]
    flat = [a for u in range(nu) for a in srcs[u] + lands[u]]

    def body(*refs):
        bufs, sems, token = refs[:len(flat)], refs[len(flat):len(flat) + 2 * nu], refs[-1]
        for u, n in enumerate(sizes):
            mine = bufs[offs[u]:offs[u + 1]]
            for send, _ in _gather_copies(mine[:n], mine[n:], sems[2 * u], sems[2 * u + 1]):
                send.start()
        token[...] = jnp.zeros_like(token)

    res = pl.pallas_call(
        body, name="weight_gather_start",
        in_specs=[HBM_ONLY] * len(flat),
        out_specs=[SEM_SPEC] * (2 * nu) + [HBM_ONLY] * len(flat) + [pl.BlockSpec(memory_space=pltpu.VMEM)],
        out_shape=[pltpu.SemaphoreType.DMA((3 * n,)) for n in sizes for _ in (0, 1)] + [pltpu.HBM(a.shape, a.dtype) for a in flat]
        + [jax.ShapeDtypeStruct((1, 1), F32)],
        input_output_aliases={i: 2 * nu + i for i in range(len(flat))},
        compiler_params=SPLIT_PARAMS,
    )(*[pltpu.with_memory_space_constraint(a, pltpu.HBM) for a in flat])
    bufs = res[2 * nu:2 * nu + len(flat)]
    state = [(res[2 * u], res[2 * u + 1], list(bufs[offs[u]:offs[u] + n]), list(bufs[offs[u] + n:offs[u + 1]]))
             for u, n in enumerate(sizes)]
    return state, res[-1]


def _gather_wait(state, after, *, name):
    send_sems, recv_sems, srcs, lands = state
    n = len(srcs)

    def body(*refs):
        for send, recv in _gather_copies(refs[:n], refs[n:2 * n], refs[2 * n], refs[2 * n + 1]):
            send.wait_send()
            recv.wait_recv()

    res = pl.pallas_call(
        body, name=name,
        in_specs=[HBM_ONLY] * (2 * n) + [SEM_SPEC, SEM_SPEC, HBM_SPEC],
        out_specs=[HBM_ONLY] * (2 * n),
        out_shape=[pltpu.HBM(a.shape, a.dtype) for a in srcs + lands],
        input_output_aliases={i: i for i in range(2 * n)},
        compiler_params=SPLIT_PARAMS,
    )(*srcs, *lands, send_sems, recv_sems, after)
    return list(res[:n]), list(res[n:])


def _gather_forward(lands, *, name):
    n = len(lands)

    def body(*refs):
        bufs, outs = refs[:n], refs[n:2 * n]
        send_sems, recv_sems = refs[2 * n:]
        x, y, c = _mesh_pos()
        copies, waits = [], []
        for k in range(n):
            a = lands[k].shape[1]
            if not _halved(a):
                continue
            for j, (px, py) in enumerate(_other_chips(x, y)):
                mine = 2 * px + py, _half_rows(a, c, BF16_ROWS)
                copies.append(_remote(bufs[k].at[mine], outs[k].at[mine], send_sems, recv_sems, 3 * k + j, (x, y, 1 - c)))
                lands_here = outs[k].at[2 * px + py, _half_rows(a, 1 - c, BF16_ROWS)]
                waits.append(_remote(lands_here, lands_here, send_sems, recv_sems, 3 * k + j, (x, y, 1 - c)))
        for cp in copies:
            cp.start()
        for cp in waits:
            cp.wait_recv()
        for cp in copies:
            cp.wait_send()

    return pl.pallas_call(
        body, name=name,
        in_specs=[HBM_SPEC] * n, out_specs=[HBM_SPEC] * n,
        out_shape=[jax.ShapeDtypeStruct(a.shape, a.dtype) for a in lands],
        scratch_shapes=[pltpu.SemaphoreType.DMA((3 * n,)), pltpu.SemaphoreType.DMA((3 * n,))],
        input_output_aliases={i: i for i in range(n)},
        compiler_params=COMM_PARAMS,
    )(*lands)


def _sibling_exchange(gs, *, name):
    n = len(gs)

    def body(*refs):
        ins, outs = refs[:n], refs[n:2 * n]
        send_sems, recv_sems = refs[2 * n:]
        x, y, c = _mesh_pos()
        copies = [_remote(ins[k].at[:, _half_rows(gs[k].shape[1], 1 - c)], outs[k], send_sems, recv_sems, k, (x, y, 1 - c))
                  for k in range(n)]
        for cp in copies:
            cp.start()
        for cp in copies:
            cp.wait()

    return pl.pallas_call(
        body, name=name,
        in_specs=[HBM_SPEC] * n, out_specs=[HBM_SPEC] * n,
        out_shape=[jax.ShapeDtypeStruct((g.shape[0], g.shape[1] // 2, g.shape[2]), g.dtype) for g in gs],
        scratch_shapes=[pltpu.SemaphoreType.DMA((n,)), pltpu.SemaphoreType.DMA((n,))],
        compiler_params=COMM_PARAMS,
    )(*gs)


HBM_ONLY = pl.BlockSpec(memory_space=pltpu.HBM)
SEM_SPEC = pl.BlockSpec(memory_space=pltpu.SEMAPHORE)
SPLIT_PARAMS = pltpu.CompilerParams(has_side_effects=pltpu.SideEffectType.DATAFLOW_SIDE_EFFECTING)


def _scatter_copies(srcs, lands, send_sems, recv_sems):
    x, y, c = _mesh_pos()
    me = 2 * x + y
    out = []
    for k in range(len(srcs)):
        for j, (px, py) in enumerate(_other_chips(x, y)):
            s = 2 * px + py
            send = _remote(srcs[k].at[s], lands[k].at[me], send_sems, recv_sems, 3 * k + j, (px, py, c))
            recv = _remote(srcs[k].at[s], lands[k].at[s], send_sems, recv_sems, 3 * k + j, (px, py, c))
            out.append((send, recv))
    return out


def _exchange_copies(srcs, lands, send_sems, recv_sems):
    x, y, c = _mesh_pos()
    out = []
    for k in range(len(srcs)):
        cp = _remote(srcs[k].at[:, _half_rows(srcs[k].shape[1], 1 - c)], lands[k], send_sems, recv_sems, k, (x, y, 1 - c))
        out.append((cp, cp))
    return out


def _split_start(srcs, land_shapes, copies, n_sems, *, name):
    n = len(srcs)
    lands = [lax.empty(shape, s.dtype) for shape, s in zip(land_shapes, srcs)]

    def body(*refs):
        ins, zones = refs[:n], refs[n:2 * n]
        send_sems, recv_sems, token = refs[2 * n], refs[2 * n + 1], refs[-1]
        for send, _ in copies(ins, zones, send_sems, recv_sems):
            send.start()
        token[...] = jnp.zeros_like(token)

    hbm = lambda a: pltpu.HBM(a.shape, a.dtype)
    res = pl.pallas_call(
        body, name=name,
        in_specs=[HBM_ONLY] * (2 * n),
        out_specs=[SEM_SPEC, SEM_SPEC] + [HBM_ONLY] * (2 * n) + [pl.BlockSpec(memory_space=pltpu.VMEM)],
        out_shape=[pltpu.SemaphoreType.DMA((n_sems,)), pltpu.SemaphoreType.DMA((n_sems,))] + [hbm(a) for a in srcs + lands]
        + [jax.ShapeDtypeStruct((1, 1), F32)],
        input_output_aliases={i: 2 + i for i in range(2 * n)},
        compiler_params=SPLIT_PARAMS,
    )(*[pltpu.with_memory_space_constraint(a, pltpu.HBM) for a in srcs + lands])
    return (res[0], res[1], list(res[2:2 + n]), list(res[2 + n:2 + 2 * n])), res[-1]


def _scatter_start(ps, *, name):
    return _split_start(ps, [p.shape for p in ps], _scatter_copies, 3 * len(ps), name=name)


def _exchange_start(gs, *, name):
    return _split_start(gs, [(g.shape[0], g.shape[1] // 2, g.shape[2]) for g in gs], _exchange_copies, len(gs), name=name)


def _split_wait(started, copies, after, *, name):
    ng = len(started)
    sizes = [len(st[2]) for st in started]
    offs = [2 * sum(sizes[:i]) for i in range(ng + 1)]
    flat = [a for (_, _, ps, lands) in started for a in ps + lands]

    def body(*refs):
        bufs, sems = refs[:len(flat)], refs[len(flat):len(flat) + 2 * ng]
        for i, n in enumerate(sizes):
            srcs, zones = bufs[offs[i]:offs[i] + n], bufs[offs[i] + n:offs[i + 1]]
            for send, recv in copies(srcs, zones, sems[2 * i], sems[2 * i + 1]):
                send.wait_send()
                recv.wait_recv()

    res = pl.pallas_call(
        body, name=name,
        in_specs=[HBM_ONLY] * len(flat) + [SEM_SPEC] * (2 * ng) + [HBM_SPEC],
        out_specs=[HBM_ONLY] * len(flat),
        out_shape=[pltpu.HBM(a.shape, a.dtype) for a in flat],
        input_output_aliases={i: i for i in range(len(flat))},
        compiler_params=SPLIT_PARAMS,
    )(*flat, *[s for (ss, rs, _, _) in started for s in (ss, rs)], after)
    return [(list(res[offs[i]:offs[i] + n]), list(res[offs[i] + n:offs[i + 1]])) for i, n in enumerate(sizes)]


def _sibling_share(hs):
    n = len(hs)

    def body(*refs):
        ins, outs = refs[:n], refs[n:2 * n]
        send_sems, recv_sems = refs[2 * n:]
        x, y, c = _mesh_pos()
        copies = [_remote(ins[k], outs[k], send_sems, recv_sems, k, (x, y, 1 - c)) for k in range(n)]
        for cp in copies:
            cp.start()
        for cp in copies:
            cp.wait()

    return pl.pallas_call(
        body, name="grad_sibling_share",
        in_specs=[HBM_SPEC] * n, out_specs=[HBM_SPEC] * n,
        out_shape=[jax.ShapeDtypeStruct(h.shape, h.dtype) for h in hs],
        scratch_shapes=[pltpu.SemaphoreType.DMA((n,)), pltpu.SemaphoreType.DMA((n,))],
        compiler_params=COMM_PARAMS,
    )(*hs)


def _allreduce_small(part, by_chip):
    rows, C = part.shape
    rows2 = by_chip.shape[1]

    def body(p_ref, q_ref, o_ref, o2_ref, slots, slots2, send_sems, recv_sems):
        x, y, c = _mesh_pos()
        me = 4 * x + 2 * y + c
        slots[me] = p_ref[...]
        slots2[me] = q_ref[2 * x + y]
        copies = []
        for k in range(1, 8):
            kx, ky, kc = (k >> 2) & 1, (k >> 1) & 1, k & 1
            peer = (x ^ kx if kx else x, y ^ ky if ky else y, c ^ kc if kc else c)
            src = 4 * peer[0] + 2 * peer[1] + peer[2]
            pair = []
            for j, (mine, zone, lands) in enumerate(((p_ref, slots.at[me], slots.at[src]),
                                                     (q_ref.at[2 * peer[0] + peer[1]], slots2.at[me], slots2.at[src]))):
                cp = _remote(mine, zone, send_sems, recv_sems, 2 * (k - 1) + j, peer)
                cp.start()
                pair.append((cp, _remote(mine, lands, send_sems, recv_sems, 2 * (k - 1) + j, peer)))
            copies += pair
        for _, landing in copies:
            landing.wait_recv()
        for cp, _ in copies:
            cp.wait_send()
        total, total2 = slots[0], slots2[0]
        for d in range(1, 8):
            total, total2 = total + slots[d], total2 + slots2[d]
        o_ref[...] = total
        o2_ref[...] = total2

    vmem = pl.BlockSpec(memory_space=pltpu.VMEM)
    return pl.pallas_call(
        body, name="small_grad_allreduce",
        in_specs=[vmem, vmem], out_specs=[vmem, vmem],
        out_shape=[jax.ShapeDtypeStruct((rows, C), F32), jax.ShapeDtypeStruct((rows2, C), F32)],
        scratch_shapes=[pltpu.VMEM((8, rows, C), F32), pltpu.VMEM((8, rows2, C), F32),
                        pltpu.SemaphoreType.DMA((14,)), pltpu.SemaphoreType.DMA((14,))],
        compiler_params=pltpu.CompilerParams(has_side_effects=True, vmem_limit_bytes=VMEM_LIMIT_BYTES),
    )(part, by_chip)


def _pad_w_uq(w):
    lead = w.shape[:-1]
    w = w.reshape(lead + (MLA_HEADS, MLA_QK))
    w = jnp.concatenate([w, jnp.zeros(lead + (MLA_HEADS, MLA_PAD - MLA_QK), w.dtype)], axis=-1)
    return w.reshape(lead + (MLA_HEADS * MLA_PAD,))


def _unpad_w_uq(g):
    lead = g.shape[:-1]
    return g.reshape(lead + (MLA_HEADS, MLA_PAD))[..., :MLA_QK].reshape(lead + (MLA_HEADS * MLA_QK,))


def _t(a):
    return jnp.swapaxes(a, -1, -2)


def _shards_of_cols(w):
    A, NB = w.shape
    return w.reshape(A, N_CHIPS, NB // N_CHIPS).transpose(1, 0, 2)


BIG = ("w_in", "w_uq", "w_ukv", "w_out", "w_up", "w_down")
SMALL = ("attn_pre_norm", "forget_bias", "swa_sinks", "rel_bias", "q_latent_norm", "kv_latent_norm", "group_norm",
         "attn_post_norm", "ffn_pre_norm", "conv_b", "ffn_post_norm")
WEIGHTS = ("attn_pre_norm", "w_in", "forget_bias", "swa_sinks", "rel_bias", "q_latent_norm", "w_uq", "kv_latent_norm",
           "w_ukv", "group_norm", "w_out", "attn_post_norm", "ffn_pre_norm", "w_up", "conv_w", "conv_b", "w_down",
           "ffn_post_norm")


PACK_UNIT = 8 * LANES


def _pack_rows(shape):
    return -(-int(np.prod(shape)) // PACK_UNIT) * 8


def _pack(arrs, row_mult=8):
    parts = []
    for a in arrs:
        n = int(np.prod(a.shape))
        parts.append(jnp.pad(a.reshape(-1), (0, _pack_rows(a.shape) * LANES - n)).reshape(-1, LANES))
    rows = sum(p.shape[0] for p in parts)
    pad = -rows % row_mult
    if pad:
        parts.append(jnp.zeros((pad, LANES), parts[0].dtype))
    return jnp.concatenate(parts, axis=0)


def _unpack(packed, shapes):
    packed = packed.reshape(-1, LANES)
    out, off = [], 0
    for shp in shapes:
        r = _pack_rows(shp)
        out.append(packed[off:off + r].reshape(-1)[:int(np.prod(shp))].reshape(shp))
        off += r
    return out


LAYER_KEYS = ("w_qkv_t", "w_lat_t", "w_in_t", "w_uq_p", "w_uq_t", "w_ukv", "w_ukv_t", "w_out", "w_up", "w_down", "conv_w")


MIX_WEIGHTS = ("w_in", "w_uq", "w_ukv", "w_out")
FFN_WEIGHTS = ("w_up", "w_down", "conv_w")


def _layer_weights(gathered):
    cols = lambda g: g.transpose(1, 0, 2).reshape(g.shape[1], N_CHIPS * g.shape[2])
    out = {}
    if "w_in" in gathered:
        w_in_t = _t(gathered["w_in"]).reshape(IN_COLS, D_MODEL)
        w_in_t = jnp.pad(w_in_t, ((0, IN_ROWS - IN_COLS), (0, 0)))
        w_uq_p = _pad_w_uq(cols(gathered["w_uq"]))
        w_ukv = cols(gathered["w_ukv"])
        out.update(w_qkv_t=w_in_t[:QKV_ROWS], w_lat_t=w_in_t[QKV_ROWS:], w_in_t=w_in_t, w_uq_p=w_uq_p, w_uq_t=_t(w_uq_p),
                   w_ukv=w_ukv, w_ukv_t=_t(w_ukv), w_out=gathered["w_out"].reshape(D_MODEL, D_MODEL))
    if "w_up" in gathered:
        out.update(w_up=gathered["w_up"], w_down=gathered["w_down"].reshape(D_FF, D_MODEL), conv_w=cols(gathered["conv_w"]))
    return out


def _local_step(x, target, W, layer_weights, layer_done):
    W = dict(W, **{key: [None] * DEPTH for key in LAYER_KEYS})
    S = x.shape[0]
    tq_tabs, tm_tabs = _rope_tables(S)
    onehot_t = _rel_onehot_t()
    bias_t = _bias_table(W["rel_bias"].T, onehot_t).reshape(SWA_KV_HEADS, SWA_GROUP, 2 * WINDOW, WINDOW)
    bias_t = bias_t.transpose(0, 2, 1, 3).reshape(SWA_KV_HEADS, 2 * WINDOW, GW)
    row = lambda a: a.reshape(1, -1)
    col = lambda a: a.reshape(-1, 1)
    fox_rows = (FOX_ROW0, FOX_ROW0 + FOX_HEADS * HEAD_DIM, FOX_ROW0 + 2 * FOX_HEADS * HEAD_DIM, SWA_Q_HEADS)
    fox = dict(rows=fox_rows, H=FOX_HEADS, Dk=HEAD_DIM, Dv=HEAD_DIM, scale=HEAD_DIM ** -0.5)
    mla = dict(rows=(0, 0, 0, SWA_Q_HEADS + FOX_HEADS), H=MLA_HEADS, Dk=MLA_PAD, Dv=HEAD_DIM, scale=MLA_SCALE, q_scaled=True)

    saved = []
    h = _rms_fwd(x, row(W["attn_pre_norm"][0]), name="rms_in")
    for l in range(DEPTH):
        sv = {"x0": x, "h1": h}
        for key, val in layer_weights(l, h, False).items():
            W[key][l] = val
        qkv = _matmul(W["w_qkv_t"][l], h, tb=True, out_dtype=BF16, name="proj_qkv")
        lat = _matmul(W["w_lat_t"][l], h, tb=True, name="proj_lat")
        oa, lse_a = _swa_fwd(qkv, bias_t, W["swa_sinks"][l], name="swa_fwd")
        fb_col = jnp.pad(col(W["forget_bias"][l]), ((0, GATE_ROWS - FOX_HEADS), (0, 0)))
        f4 = _gate_fwd(lat, fb_col, name="fox_gate_fwd")[:FOX_HEADS]
        f2 = f4 * LOG2E
        f_row, f_col = f2[:, None, :], f2.T
        of, lse_f = _attn_fwd(qkv, qkv, qkv, f_row=f_row, f_col=f_col, name="fox_fwd", **fox)
        nq, nkv, qm, km, vm = _mla_prep_fwd(lat, col(W["q_latent_norm"][l]), col(W["kv_latent_norm"][l]), W["w_uq_t"][l],
                                            W["w_ukv_t"][l], tq_tabs, tm_tabs, name="mla_prep_fwd")
        oc, lse_c = _attn_fwd(qm, km, vm, name="mla_fwd", **mla)
        mixed = _group_norm_fwd(oa, of, oc, col(W["group_norm"][l]), name="group_norm_fwd")
        y, x1, h2 = _matmul(mixed, W["w_out"][l], ta=True, name="proj_out",
                            resid_rms=(x, row(W["attn_post_norm"][l]), row(W["ffn_pre_norm"][l])))
        for key, val in layer_weights(l, h2, True).items():
            W[key][l] = val
        a = _matmul(h2, W["w_up"][l], b_shards=True, out_dtype=BF16, name="ffn_up")
        u, z = _conv_geglu_fwd(a, W["conv_w"][l], row(W["conv_b"][l]), name="conv_geglu_fwd")
        g_next = row(W["attn_pre_norm"][l + 1]) if l + 1 < DEPTH else None
        y2, x2, *h_next = _matmul(z, W["w_down"][l], name="ffn_down", resid_rms=(x1, row(W["ffn_post_norm"][l]), g_next))
        h_next = h_next[0] if h_next else None
        sv.update(qkv=qkv, lat=lat, oa=oa, lse_a=lse_a, fb_col=fb_col, f_row=f_row, f_col=f_col, of=of, lse_f=lse_f,
                  nq=nq, nkv=nkv, qm=qm, km=km, vm=vm, oc=oc, lse_c=lse_c, mixed=mixed, y=y, x1=x1, h2=h2, a=a, u=u, z=z, y2=y2)
        saved.append(sv)
        x, h = x2, h_next

    loss, dx = _loss_head(x, target)

    G = {k: [None] * DEPTH for k in WEIGHTS if k != "rel_bias" and k not in BIG}
    dbias_layers = [None] * DEPTH
    for l in reversed(range(DEPTH)):
        sv = saved[l]
        gb = {}
        if l == DEPTH - 1:
            dy2, dg = _rms_bwd(sv["y2"], row(W["ffn_post_norm"][l]), dx, out_dtype=BF16, name="ffn_post_bwd")
            G["ffn_post_norm"][l] = dg[0]
        dz = _matmul(dy2, W["w_down"][l], tb=True, name="ffn_down_dx")
        gb["w_down"] = _matmul(sv["z"], dy2, ta=True, name="ffn_down_dw").reshape(N_CHIPS, D_FF // N_CHIPS, D_MODEL)
        da, dcw, dcb = _conv_geglu_bwd(sv["a"], sv["u"], W["conv_w"][l], dz, name="conv_geglu_bwd")
        G["conv_w"][l] = dcw.transpose(1, 0, 2).reshape(3, 2 * D_FF)
        G["conv_b"][l] = dcb.reshape(2 * D_FF)
        gb["w_up"] = _matmul(sv["h2"], da, ta=True, out_shards=True, b_halves=True, name="ffn_up_dw")
        token = layer_done(l, gb)
        gb = {}
        dx1, dg, dy, dg_post = _matmul(
            da, W["w_up"][l], tb=True, b_shards=True, a_halves=True, name="ffn_up_dx",
            norm_bwd=(sv["x1"], row(W["ffn_pre_norm"][l]) + token, dx, (sv["y"], row(W["attn_post_norm"][l]))))
        G["ffn_pre_norm"][l] = dg[0]
        G["attn_post_norm"][l] = dg_post[0]
        dmixed = _matmul(W["w_out"][l], dy, tb=True, name="proj_out_dx")
        gb["w_out"] = _matmul(sv["mixed"], dy, name="proj_out_dw").reshape(N_CHIPS, D_MODEL // N_CHIPS, D_MODEL)
        doa, dof, doc, dg, delta = _group_norm_bwd(sv["oa"], sv["of"], sv["oc"], col(W["group_norm"][l]), dmixed,
                                                   name="group_norm_bwd")
        G["group_norm"][l] = dg[:, 0]
        dqa, dkva, dbias_l, dsink = _swa_bwd(sv["qkv"], bias_t, W["swa_sinks"][l], doa, sv["lse_a"],
                                             delta.reshape(-1, S), name="swa_bwd")
        dbias_layers[l] = (dbias_l.reshape(SWA_KV_HEADS, 2 * WINDOW, SWA_GROUP, WINDOW).transpose(0, 2, 1, 3)
                           .reshape(SWA_Q_HEADS, -1))
        G["swa_sinks"][l] = dsink[:, 0]
        dqf, dkf, dvf, dfk = _attn_bwd(sv["qkv"], sv["qkv"], sv["qkv"], do=dof, lse=sv["lse_f"], delta=delta,
                                       f_row=sv["f_row"], f_col=sv["f_col"], name="fox_bwd", **fox)
        dF = jnp.pad(dfk.T, ((0, GATE_ROWS - FOX_HEADS), (0, 0)))
        dflog, dfb = _gate_bwd(sv["lat"], sv["fb_col"], dF, name="fox_gate_bwd")
        G["forget_bias"][l] = dfb[:FOX_HEADS, 0]
        dqm, dkm, dvm = _attn_bwd(sv["qm"], sv["km"], sv["vm"], do=doc, lse=sv["lse_c"], delta=delta, name="mla_bwd", **mla)
        dlat, dwq_t, dwkv_t, dgq, dgkv = _mla_prep_bwd(
            sv["lat"], sv["nq"], sv["nkv"], col(W["q_latent_norm"][l]), col(W["kv_latent_norm"][l]), W["w_uq_p"][l],
            W["w_ukv"][l], tq_tabs, tm_tabs, dqm, dkm, dvm, dflog, name="mla_prep_bwd")
        gb["w_uq"], gb["w_ukv"] = _shards_of_cols(_unpad_w_uq(dwq_t.T)), _shards_of_cols(dwkv_t.T)
        G["q_latent_norm"][l], G["kv_latent_norm"][l] = dgq[:, 0], dgkv[:, 0]
        dproj = _dproj_cast(dqa, dkva, dqf, dkf, dvf, dlat, name="dproj_cast")
        dw_in_t = _matmul(dproj, sv["h1"], name="proj_in_dw")
        gb["w_in"] = _t(dw_in_t[:IN_COLS].reshape(N_CHIPS, IN_COLS // N_CHIPS, D_MODEL))
        token = layer_done(l, gb)
        below = (saved[l - 1]["y2"], row(W["ffn_post_norm"][l - 1])) if l > 0 else None
        res = _matmul(dproj, W["w_in_t"][l], ta=True, name="proj_in_dx",
                      norm_bwd=(sv["x0"], row(W["attn_pre_norm"][l]) + token, dx1, below))
        dx, G["attn_pre_norm"][l] = res[0], res[1][0]
        if l > 0:
            dy2, G["ffn_post_norm"][l - 1] = res[2], res[3][0]

    grads = {k: jnp.stack(v) for k, v in G.items()}
    grads["rel_bias"] = _bias_table_bwd(jnp.stack(dbias_layers), onehot_t).T
    return loss, dx, grads


def kernel(x, attn_pre_norm, w_in, forget_bias, swa_sinks, rel_bias, q_latent_norm, w_uq, kv_latent_norm, w_ukv, group_norm, w_out, attn_post_norm, ffn_pre_norm, w_up, conv_w, conv_b, w_down, ffn_post_norm, loss_target, m_attn_pre_norm, m_w_in, m_forget_bias, m_swa_sinks, m_rel_bias, m_q_latent_norm, m_w_uq, m_kv_latent_norm, m_w_ukv, m_group_norm, m_w_out, m_attn_post_norm, m_ffn_pre_norm, m_w_up, m_conv_w, m_conv_b, m_w_down, m_ffn_post_norm, v_attn_pre_norm, v_w_in, v_forget_bias, v_swa_sinks, v_rel_bias, v_q_latent_norm, v_w_uq, v_kv_latent_norm, v_w_ukv, v_group_norm, v_w_out, v_attn_post_norm, v_ffn_pre_norm, v_w_up, v_conv_w, v_conv_b, v_w_down, v_ffn_post_norm):
    args = dict(locals())
    w = {k: args[k] for k in WEIGHTS}
    m = {k: args["m_" + k] for k in WEIGHTS}
    v = {k: args["v_" + k] for k in WEIGHTS}

    block = lambda l, keys: [w[k][l] if k == "conv_w" else w[k][l].astype(BF16) for k in keys]
    units = [(0, MIX_WEIGHTS), (0, FFN_WEIGHTS)] + [(l, MIX_WEIGHTS + FFN_WEIGHTS) for l in range(1, DEPTH)]
    gather_state, token = _gather_start([block(l, keys) for l, keys in units])
    W = {k: w[k] for k in SMALL}
    W["attn_pre_norm"] = W["attn_pre_norm"] + token

    def layer_weights(l, after, for_ffn):
        if for_ffn and l > 0:
            return {}
        keys = FFN_WEIGHTS if for_ffn else (MIX_WEIGHTS if l == 0 else MIX_WEIGHTS + FFN_WEIGHTS)
        tag = f"{l}_{keys[0]}"
        srcs, lands = _gather_wait(gather_state[units.index((l, keys))], after, name="weight_gather_wait_" + tag)
        lands = _gather_forward(lands, name="weight_gather_forward_" + tag)
        lands = _place_own(lands, srcs, name="place_own_shards")
        return _layer_weights(dict(zip(keys, lands)))

    started, groups, pending = [], [], []

    def to_chips(l, keys, gs, recv, tag):
        pair = [_pair_sum(gk, rk, name="grad_pair_sum") for gk, rk in zip(gs, recv)]
        state, token = _scatter_start(pair, name="grad_scatter_start_" + tag)
        started.append(state)
        groups.append((l, keys))
        return token

    def finish_pending(after):
        l, keys, tag, state = pending.pop()
        gs, recv = _split_wait([state], _exchange_copies, after, name="grad_exchange_wait_" + tag)[0]
        return to_chips(l, keys, gs, recv, tag)

    def layer_done(l, gb):
        keys = [k for k in BIG if k in gb]
        gs = [gb[k] for k in keys]
        tag = f"{l}_{keys[0]}"
        token = finish_pending(gs[0]) if pending else 0.0
        if l == 0:
            return token + to_chips(l, keys, gs, _sibling_exchange(gs, name="grad_sibling_exchange_" + tag), tag)
        state, started_token = _exchange_start(gs, name="grad_exchange_start_" + tag)
        pending.append((l, keys, tag, state))
        return token + started_token

    loss_part, dx, g = _local_step(x[0], loss_target[0], W, layer_weights, layer_done)
    loss = lax.psum(loss_part, ("x", "y", "c"))

    reduced = {}
    for (l, keys), (pair, zones) in zip(groups, _split_wait(started, _scatter_copies, dx, name="grad_scatter_wait")):
        for k, p, z in zip(keys, pair, zones):
            reduced[k, l] = _chip_sum(z, p, name="grad_chip_sum")
    mine = [jnp.stack([reduced[k, l] for l in range(DEPTH)]) for k in BIG]
    other = _sibling_share(mine)
    out_g, out_d, out_m, out_v = {}, {}, {}, {}
    for k, g_mine, g_other in zip(BIG, mine, other):
        out_g[k], out_d[k], out_m[k], out_v[k] = _adamw_halves(w[k], g_mine, g_other, m[k], v[k], name="adamw_" + k)

    small_shapes = [w[k].shape for k in SMALL]
    taps_by_chip = g["conv_w"].reshape(DEPTH, 3, N_CHIPS, FF_SHARD).transpose(2, 0, 1, 3)
    reduced, taps = _allreduce_small(_pack([g[k] for k in SMALL]), jnp.stack([_pack([t]) for t in taps_by_chip]))
    g_small = _unpack(reduced, small_shapes) + _unpack(taps, [w["conv_w"].shape])
    names = SMALL + ("conv_w",)
    shapes = small_shapes + [w["conv_w"].shape]
    packed = lambda arrs: _pack(arrs, ROW_TILE)[None]
    d_s, m_s, v_s = _adamw(packed([w[k] for k in names]), packed(g_small), packed([m[k] for k in names]),
                           packed([v[k] for k in names]), name="adamw_small")
    out_g.update(zip(names, g_small))
    out_d.update(zip(names, _unpack(d_s, shapes)))
    out_m.update(zip(names, _unpack(m_s, shapes)))
    out_v.update(zip(names, _unpack(v_s, shapes)))

    return (loss, dx[None], *[out_g[k] for k in WEIGHTS], *[out_d[k] for k in WEIGHTS],
            *[out_m[k] for k in WEIGHTS], *[out_v[k] for k in WEIGHTS])
```

```python
import math

import numpy as np
import jax
import jax.numpy as jnp
from jax import lax
from jax.experimental import pallas as pl
from jax.experimental.pallas import tpu as pltpu

F32 = jnp.float32
BF16 = jnp.bfloat16

D_MODEL = 1024
DEPTH = 4
HEAD_DIM = 64
SWA_Q_HEADS = 8
SWA_KV_HEADS = 2
SWA_GROUP = SWA_Q_HEADS // SWA_KV_HEADS
WINDOW = 128
FOX_HEADS = 4
MLA_HEADS = 4
MLA_Q_RANK = 256
MLA_KV_RANK = 128
MLA_NOPE = 64
MLA_ROPE = 32
MLA_QK = MLA_NOPE + MLA_ROPE
ROPE_THETA = 10000.0
REL_BUCKETS = 32
REL_MAX_DIST = 128
D_FF = 2816
EPS = 1e-6
NEG_INF = -1e30
LANES = 128
N_CHIPS = 4

IN_COLS = 1956
IN_ROWS = 2048
QKV_ROWS = 1536
LAT_ROWS = IN_ROWS - QKV_ROWS
LAT_SHIFT = FOX_HEADS
FOX_ROW0 = 768
MLA_PAD = LANES
GATE_ROWS = 8

ADAM_LR = 0.001
ADAM_B1 = 0.9
ADAM_B2 = 0.999
ADAM_EPS = 1e-08
ADAM_WD = 0.01
ADAM_STEP = 10

VMEM_LIMIT_BYTES = 48 * 1024 * 1024
ATT_TILE = 512
LOG2E = math.log2(math.e)
MLA_SCALE = MLA_QK ** -0.5
ROW_TILE = 256
MESH = pl.DeviceIdType.MESH

NT = (((1,), (1,)), ((), ()))
TN = (((0,), (0,)), ((), ()))
NN = (((1,), (0,)), ((), ()))


SMALL_VMEM_LIMIT_BYTES = 32 * 1024 * 1024


def _params(*sem, vmem_bytes=SMALL_VMEM_LIMIT_BYTES):
    return pltpu.CompilerParams(dimension_semantics=sem, vmem_limit_bytes=vmem_bytes)


def _tile(dim, cap):
    for t in (2816, 2048, 1408, 1024, 512, 256, 128, 64, 32, 16, 8):
        if t <= cap and dim % t == 0:
            return t
    return dim


def _dot(a, b, dims=NN):
    return lax.dot_general(a, b, dims, preferred_element_type=F32)


def _split3(a):
    a1 = a.astype(BF16)
    r1 = a - a1.astype(F32)
    a2 = r1.astype(BF16)
    a3 = (r1 - a2.astype(F32)).astype(BF16)
    return a1, a2, a3


FF_SHARD = 2 * D_FF // N_CHIPS
MATMUL_VMEM_BYTES = 40 * 1024 * 1024
TAIL_ROWS = 512
TAIL_VMEM_LIMIT_BYTES = 56 * 1024 * 1024


def _matmul(a, b, *, ta=False, tb=False, out_dtype=F32, name, b_shards=False, out_shards=False, a_halves=False,
            b_halves=False, norm_bwd=None, resid_rms=None):
    if a_halves:
        M, K = a.shape[1], 2 * a.shape[2]
    elif ta:
        K, M = a.shape
    else:
        M, K = a.shape
    if b_halves:
        K2, N = b.shape[1], 2 * b.shape[2]
    elif b_shards:
        K2, N = (2 * D_FF, D_MODEL) if tb else (D_MODEL, 2 * D_FF)
    elif tb:
        N, K2 = b.shape
    else:
        K2, N = b.shape
    assert K == K2, (a.shape, b.shape)
    tn = _tile(N, 1408)
    tk = FF_SHARD if (b_shards and tb) else _tile(K, 2816)
    out_bytes = jnp.dtype(out_dtype).itemsize
    with_tail = norm_bwd is not None or resid_rms is not None
    tile_bytes = 4 + (2 * (4 * 4 + 2) if with_tail else 2 * out_bytes)
    vmem = lambda tm, tk: 2 * 2 * tk * (tm + tn) + tile_bytes * tm * tn
    tm = M if M <= 2048 else _tile(M, 1408)
    if M > 2048 and M % 2048 == 0 and tk == K and vmem(2048, tk) <= MATMUL_VMEM_BYTES:
        tm = 2048
    if with_tail:
        assert tn == N and not out_shards and (norm_bwd is None or resid_rms is None)
        tm = TAIL_ROWS
    while vmem(tm, tk) > MATMUL_VMEM_BYTES and tk % 256 == 0:
        tk //= 2
    nk = K // tk
    dims = (((0 if ta else 1,), (1 if tb else 0,)), ((), ()))
    if with_tail:
        if norm_bwd is not None:
            x, g, resid, then = norm_bwd
            chained = then is not None
            tail_in, in_kinds = [x, g, resid] + (list(then) if chained else []), "rvr" + ("rv" if chained else "")
            out_kinds, out_dtypes = "rv" + ("rv" if chained else ""), [F32, F32] + ([BF16, F32] if chained else [])

            def tail(dy, ins, outs):
                dx, dg = _seg_rms_bwd(ins[0][...], ins[1][...], dy)
                dx = dx + ins[2][...]
                outs[0][...] = dx
                outs[1][...] += dg
                if chained:
                    dx2, dg2 = _seg_rms_bwd(ins[3][...], ins[4][...], dx)
                    outs[2][...] = dx2.astype(BF16)
                    outs[3][...] += dg2
        else:
            x, g_post, g_next = resid_rms
            with_next = g_next is not None
            tail_in, in_kinds = [x, g_post] + ([g_next] if with_next else []), "rv" + ("v" if with_next else "")
            out_kinds, out_dtypes = "rr" + ("r" if with_next else ""), [F32, F32] + ([BF16] if with_next else [])

            def tail(y, ins, outs):
                outs[0][...] = y
                xn = ins[0][...] + _seg_rms(y, ins[1][...])
                outs[1][...] = xn
                if with_next:
                    outs[2][...] = _seg_rms(xn, ins[2][...]).astype(BF16)

        def fused(a_ref, b_ref, *refs):
            ins, outs, acc = refs[:len(tail_in)], refs[len(tail_in):-1], refs[-1]
            k, i = pl.program_id(0), pl.program_id(1)
            acc_ref = acc.at[pl.ds(pl.multiple_of(i * tm, tm), tm), :] if nk > 1 else acc

            @pl.when(k == 0)
            def _():
                acc_ref[...] = jnp.zeros((tm, N), F32)

            acc_ref[...] += lax.dot_general(a_ref[...], b_ref[...], dims, preferred_element_type=F32)

            @pl.when((k == nk - 1) & (i == 0))
            def _():
                for o, kind in zip(outs, out_kinds):
                    if kind == "v":
                        o[...] = jnp.zeros_like(o)

            @pl.when(k == nk - 1)
            def _():
                tail(acc_ref[...], ins, outs)

    def body(a_ref, b_ref, o_ref, acc_ref):
        k = pl.program_id(2)

        @pl.when(k == 0)
        def _():
            acc_ref[...] = jnp.zeros_like(acc_ref)

        acc_ref[...] += lax.dot_general(a_ref[...], b_ref[...], dims, preferred_element_type=F32)

        @pl.when(k == nk - 1)
        def _():
            o_ref[...] = acc_ref[...].astype(o_ref.dtype)

    if a_halves:
        nh = K // 2 // tk
        a_spec = pl.BlockSpec((None, tm, tk), lambda i, j, k: (k // nh, i, k % nh))
    else:
        a_spec = pl.BlockSpec((tk, tm), lambda i, j, k: (k, i)) if ta else pl.BlockSpec((tm, tk), lambda i, j, k: (i, k))
    if b_halves:
        nh = N // 2 // tn
        b_spec = pl.BlockSpec((None, tk, tn), lambda i, j, k: (j // nh, k, j % nh))
    elif b_shards and tb:
        assert tk == FF_SHARD
        b_spec = pl.BlockSpec((None, tn, tk), lambda i, j, k: (k, j, 0))
    elif b_shards:
        assert tn == FF_SHARD
        b_spec = pl.BlockSpec((None, tk, tn), lambda i, j, k: (j, k, 0))
    else:
        b_spec = pl.BlockSpec((tn, tk), lambda i, j, k: (j, k)) if tb else pl.BlockSpec((tk, tn), lambda i, j, k: (k, j))
    if out_shards:
        assert tn == FF_SHARD
        out_spec = pl.BlockSpec((None, tm, tn), lambda i, j, k: (j, i, 0))
        out_shape = jax.ShapeDtypeStruct((N // tn, M, tn), out_dtype)
    else:
        out_spec = pl.BlockSpec((tm, tn), lambda i, j, k: (i, j))
        out_shape = jax.ShapeDtypeStruct((M, N), out_dtype)
    if with_tail:
        spec = {"r": pl.BlockSpec((tm, N), lambda k, i: (jnp.where(k == nk - 1, i, 0), 0)),
                "v": pl.BlockSpec((1, N), lambda k, i: (0, 0))}
        a_map, b_map = a_spec.index_map, b_spec.index_map
        return pl.pallas_call(
            fused, name=name, grid=(nk, M // tm),
            in_specs=[pl.BlockSpec(a_spec.block_shape, lambda k, i: a_map(i, 0, k)),
                      pl.BlockSpec(b_spec.block_shape, lambda k, i: b_map(i, 0, k))] + [spec[c] for c in in_kinds],
            out_specs=[spec[c] for c in out_kinds],
            out_shape=[jax.ShapeDtypeStruct((M if c == "r" else 1, N), d) for c, d in zip(out_kinds, out_dtypes)],
            scratch_shapes=[pltpu.VMEM((M if nk > 1 else tm, N), F32)],
            compiler_params=pltpu.CompilerParams(dimension_semantics=("arbitrary", "arbitrary"),
                                                 vmem_limit_bytes=TAIL_VMEM_LIMIT_BYTES if nk > 1 else VMEM_LIMIT_BYTES),
        )(a, b, *tail_in)
    return pl.pallas_call(
        body, name=name, grid=(M // tm, N // tn, nk),
        in_specs=[a_spec, b_spec], out_specs=out_spec, out_shape=out_shape,
        scratch_shapes=[pltpu.VMEM((tm, tn), F32)],
        compiler_params=_params("parallel", "parallel", "arbitrary", vmem_bytes=VMEM_LIMIT_BYTES),
    )(a, b)


def _seg_rms(xs, g):
    r = lax.rsqrt(jnp.mean(xs * xs, axis=-1, keepdims=True) + EPS)
    return xs * r * g


def _seg_rms_bwd(xs, g, dy):
    r = lax.rsqrt(jnp.mean(xs * xs, axis=-1, keepdims=True) + EPS)
    gd = dy * g
    c = jnp.mean(gd * xs, axis=-1, keepdims=True)
    dx = r * gd - xs * (r * r * r * c)
    dg = jnp.sum(dy * (xs * r), axis=0, keepdims=True)
    return dx, dg


def _rms_fwd(x, g, *, name):
    S, W = x.shape
    tm = _tile(S, 512)

    def body(x_ref, g_ref, o_ref):
        o_ref[...] = _seg_rms(x_ref[...], g_ref[...]).astype(o_ref.dtype)

    return pl.pallas_call(
        body, name=name, grid=(S // tm,),
        in_specs=[pl.BlockSpec((tm, W), lambda i: (i, 0)), pl.BlockSpec((1, W), lambda i: (0, 0))],
        out_specs=pl.BlockSpec((tm, W), lambda i: (i, 0)),
        out_shape=jax.ShapeDtypeStruct((S, W), BF16),
        compiler_params=_params("parallel"),
    )(x, g)


def _rms_bwd(x, g, dy, *, out_dtype, name):
    S, W = x.shape
    tm = _tile(S, 512)

    def body(x_ref, g_ref, dy_ref, dx_ref, dg_ref):
        @pl.when(pl.program_id(0) == 0)
        def _():
            dg_ref[...] = jnp.zeros_like(dg_ref)

        dx, dg = _seg_rms_bwd(x_ref[...], g_ref[...], dy_ref[...])
        dx_ref[...] = dx.astype(dx_ref.dtype)
        dg_ref[...] += dg

    row = pl.BlockSpec((tm, W), lambda i: (i, 0))
    vec = pl.BlockSpec((1, W), lambda i: (0, 0))
    return pl.pallas_call(
        body, name=name, grid=(S // tm,),
        in_specs=[row, vec, row], out_specs=[row, vec],
        out_shape=[jax.ShapeDtypeStruct((S, W), out_dtype), jax.ShapeDtypeStruct((1, W), F32)],
        compiler_params=_params("arbitrary"),
    )(x, g, dy)


def _col_rms(xs, g):
    r = lax.rsqrt(jnp.mean(xs * xs, axis=0, keepdims=True) + EPS)
    return xs * r * g


def _col_rms_bwd(xs, g, dy):
    r = lax.rsqrt(jnp.mean(xs * xs, axis=0, keepdims=True) + EPS)
    gd = dy * g
    c = jnp.mean(gd * xs, axis=0, keepdims=True)
    dx = r * gd - xs * (r * r * r * c)
    dg = jnp.sum(dy * (xs * r), axis=1, keepdims=True)
    return dx, dg


GROUP_ROWS = (SWA_Q_HEADS * HEAD_DIM, FOX_HEADS * HEAD_DIM, MLA_HEADS * HEAD_DIM)


def _group_specs(S, tn):
    outs = [pl.BlockSpec((n, tn), lambda i: (0, i)) for n in GROUP_ROWS]
    g = pl.BlockSpec((D_MODEL, 1), lambda i: (0, 0))
    mixed = pl.BlockSpec((D_MODEL, tn), lambda i: (0, i))
    return outs, g, mixed


def _group_norm_fwd(oa, of, oc, g, *, name):
    S = oa.shape[1]
    tn = _tile(S, 512)
    outs, gs, mixed = _group_specs(S, tn)

    def body(a_ref, f_ref, c_ref, g_ref, o_ref):
        r0 = 0
        for ref, n in zip((a_ref, f_ref, c_ref), GROUP_ROWS):
            o_ref[r0:r0 + n, :] = _col_rms(ref[...], g_ref[r0:r0 + n, :]).astype(BF16)
            r0 += n

    return pl.pallas_call(
        body, name=name, grid=(S // tn,),
        in_specs=outs + [gs], out_specs=mixed,
        out_shape=jax.ShapeDtypeStruct((D_MODEL, S), BF16),
        compiler_params=_params("parallel"),
    )(oa, of, oc, g)


def _group_norm_bwd(oa, of, oc, g, dmixed, *, name):
    S = oa.shape[1]
    tn = _tile(S, 512)
    outs, gs, mixed = _group_specs(S, tn)
    n_heads = D_MODEL // HEAD_DIM

    def body(a_ref, f_ref, c_ref, g_ref, dm_ref, da_ref, df_ref, dc_ref, dg_ref, dl_ref):
        @pl.when(pl.program_id(0) == 0)
        def _():
            dg_ref[...] = jnp.zeros_like(dg_ref)

        r0 = 0
        for ref, dref, n in zip((a_ref, f_ref, c_ref), (da_ref, df_ref, dc_ref), GROUP_ROWS):
            o = ref[...]
            dx, dg = _col_rms_bwd(o, g_ref[r0:r0 + n, :], dm_ref[r0:r0 + n, :])
            dxb = dx.astype(BF16)
            dref[...] = dxb
            dg_ref[r0:r0 + n, :] += dg
            od = o * dxb.astype(F32)
            for h in range(n // HEAD_DIM):
                dl_ref[r0 // HEAD_DIM + h] = jnp.sum(od[h * HEAD_DIM:(h + 1) * HEAD_DIM, :], axis=0, keepdims=True)
            r0 += n

    return pl.pallas_call(
        body, name=name, grid=(S // tn,),
        in_specs=outs + [gs, mixed], out_specs=outs + [gs, pl.BlockSpec((n_heads, 1, tn), lambda i: (0, 0, i))],
        out_shape=[jax.ShapeDtypeStruct((n, S), BF16) for n in GROUP_ROWS] + [jax.ShapeDtypeStruct((D_MODEL, 1), F32),
                                                                              jax.ShapeDtypeStruct((n_heads, 1, S), F32)],
        compiler_params=_params("arbitrary"),
    )(oa, of, oc, g, dmixed)


def _loss_head(y, target):
    S, W = y.shape
    tm = _tile(S, 512)

    def body(y_ref, t_ref, d_ref, l_ref):
        @pl.when(pl.program_id(0) == 0)
        def _():
            l_ref[...] = jnp.zeros_like(l_ref)

        err = y_ref[...] - t_ref[...]
        d_ref[...] = err * (1.0 / W)
        l_ref[...] += 0.5 * jnp.sum(jnp.mean(err * err, axis=-1, keepdims=True), axis=0, keepdims=True)

    row = pl.BlockSpec((tm, W), lambda i: (i, 0))
    d, l = pl.pallas_call(
        body, name="loss_head", grid=(S // tm,),
        in_specs=[row, row],
        out_specs=[row, pl.BlockSpec((1, 1), lambda i: (0, 0))],
        out_shape=[jax.ShapeDtypeStruct((S, W), F32), jax.ShapeDtypeStruct((1, 1), F32)],
        compiler_params=_params("arbitrary"),
    )(y, target)
    return l[0, 0], d


def _attn_fwd(q_src, k_src, v_src, rows, H, Dk, Dv, scale, f_row=None, f_col=None, *, name, q_scaled=False):
    S = q_src.shape[1]
    T = _tile(S, ATT_TILE)
    nq = S // T
    forget = f_row is not None
    qb, kb, vb = rows[0] // (H * Dk), rows[1] // (H * Dk), rows[2] // (H * Dv)
    hs = range(H)

    def body(*refs):
        if forget:
            q_ref, k_ref, v_ref, fq_ref, fk_ref, o_ref, lse_ref = refs
        else:
            q_ref, k_ref, v_ref, o_ref, lse_ref = refs
        i = pl.program_id(0)

        def tile(j, masked, state):
            off = pl.multiple_of(j * T, T)
            ss = [_dot(k_ref[h * Dk:(h + 1) * Dk, pl.ds(off, T)], q_ref[h * Dk:(h + 1) * Dk, :], TN) for h in hs]
            if not q_scaled:
                ss = [s * (scale * LOG2E) for s in ss]
            if forget:
                ss = [ss[h] + (fq_ref[h] - fk_ref[pl.ds(off, T), h:h + 1]) for h in hs]
            if masked:
                r = lax.broadcasted_iota(jnp.int32, (T, T), 0)
                c = lax.broadcasted_iota(jnp.int32, (T, T), 1)
                ss = [jnp.where(r <= c, s, NEG_INF) for s in ss]
            m_new = [jnp.maximum(state[h][0], jnp.max(ss[h], axis=0, keepdims=True)) for h in hs]
            alpha = [jnp.exp2(state[h][0] - m_new[h]) for h in hs]
            ps = [jnp.exp2(ss[h] - m_new[h]) for h in hs]
            l_new = [alpha[h] * state[h][1] + jnp.sum(ps[h], axis=0, keepdims=True) for h in hs]
            p_hi = [p.astype(BF16) for p in ps]
            vs = [v_ref[h * Dv:(h + 1) * Dv, pl.ds(off, T)] for h in hs]
            pv = [_dot(vs[h], p_hi[h]) for h in hs]
            if forget:
                pv = [pv[h] + _dot(vs[h], (ps[h] - p_hi[h].astype(F32)).astype(BF16)) for h in hs]
            return tuple((m_new[h], l_new[h], alpha[h] * state[h][2] + pv[h]) for h in hs)

        init = tuple((jnp.full((1, T), NEG_INF, F32), jnp.zeros((1, T), F32), jnp.zeros((Dv, T), F32)) for _ in hs)
        state = lax.fori_loop(0, i, lambda j, st: tile(j, False, st), init)
        state = tile(i, True, state)
        for h in hs:
            m, l, acc = state[h]
            o_ref[h * Dv:(h + 1) * Dv, :] = acc / l
            lse_ref[h] = m + jnp.log2(l)

    in_specs = [pl.BlockSpec((H * Dk, T), lambda i: (qb, i)),
                pl.BlockSpec((H * Dk, S), lambda i: (kb, 0)),
                pl.BlockSpec((H * Dv, S), lambda i: (vb, 0))]
    ins = [q_src, k_src, v_src]
    if forget:
        in_specs += [pl.BlockSpec((H, 1, T), lambda i: (0, 0, i)), pl.BlockSpec((S, H), lambda i: (0, 0))]
        ins += [f_row, f_col]
    return pl.pallas_call(
        body, name=name, grid=(nq,),
        in_specs=in_specs,
        out_specs=[pl.BlockSpec((H * Dv, T), lambda i: (0, i)), pl.BlockSpec((H, 1, T), lambda i: (0, 0, i))],
        out_shape=[jax.ShapeDtypeStruct((H * Dv, S), F32), jax.ShapeDtypeStruct((H, 1, S), F32)],
        compiler_params=_params("parallel"),
    )(*ins)


def _attn_bwd(q_src, k_src, v_src, rows, H, Dk, Dv, scale, do, lse, delta, f_row=None, f_col=None, *, name, q_scaled=False):
    S = q_src.shape[1]
    T = _tile(S, ATT_TILE)
    nq = S // T
    forget = f_row is not None
    qb, kb, vb, db = rows[0] // (H * Dk), rows[1] // (H * Dk), rows[2] // (H * Dv), rows[3] // H
    hs = range(H)

    def body(*refs):
        if forget:
            (q_ref, k_ref, v_ref, do_ref, lse_ref, dl_ref, fq_ref, fk_ref,
             dq_ref, dk_ref, dv_ref, df_ref, dk_s, dv_s, df_s) = refs
        else:
            q_ref, k_ref, v_ref, do_ref, lse_ref, dl_ref, dq_ref, dk_ref, dv_ref, dk_s, dv_s = refs
        j = pl.program_id(0)

        @pl.when(j == 0)
        def _():
            dq_ref[...] = jnp.zeros_like(dq_ref)

        dk_s[...] = jnp.zeros_like(dk_s)
        dv_s[...] = jnp.zeros_like(dv_s)
        if forget:
            df_s[...] = jnp.zeros_like(df_s)
        kt = [k_ref[h * Dk:(h + 1) * Dk, :] for h in hs]
        kj = [k.T for k in kt]
        vj = [v_ref[h * Dv:(h + 1) * Dv, :].T for h in hs]
        koff = pl.multiple_of(j * T, T)

        def tile(i, masked):
            cols = pl.ds(pl.multiple_of(i * T, T), T)
            qi = [q_ref[h * Dk:(h + 1) * Dk, cols] for h in hs]
            doi = [do_ref[h * Dv:(h + 1) * Dv, cols] for h in hs]
            st = [_dot(kj[h], qi[h]) for h in hs]
            if not q_scaled:
                st = [x * (scale * LOG2E) for x in st]
            if forget:
                st = [st[h] + (fq_ref[h, :, cols] - fk_ref[pl.ds(koff, T), h:h + 1]) for h in hs]
            if masked:
                r = lax.broadcasted_iota(jnp.int32, (T, T), 0)
                c = lax.broadcasted_iota(jnp.int32, (T, T), 1)
                st = [jnp.where(r <= c, x, NEG_INF) for x in st]
            pt = [jnp.exp2(st[h] - lse_ref[h, :, cols]) for h in hs]
            dpt = [_dot(vj[h], doi[h]) for h in hs]
            dst = [pt[h] * (dpt[h] - dl_ref[h, :, cols]) for h in hs]
            ptb = [p.astype(BF16) for p in pt]
            dsb = [d.astype(BF16) for d in dst]
            for h in hs:
                dv_s[h * Dv:(h + 1) * Dv, :] += _dot(doi[h], ptb[h], NT)
            for h in hs:
                dk_s[h * Dk:(h + 1) * Dk, :] += _dot(qi[h], dsb[h], NT)
            for h in hs:
                dq_ref[h * Dk:(h + 1) * Dk, cols] += _dot(kt[h], dsb[h]) * scale
            if forget:
                for h in hs:
                    part = dst[h][:, 0:LANES]
                    for c0 in range(LANES, T, LANES):
                        part = part + dst[h][:, c0:c0 + LANES]
                    df_s[h] += part

        tile(j, True)

        def loop_body(i, carry):
            tile(i, False)
            return carry

        lax.fori_loop(j + 1, nq, loop_body, 0)
        dk_ref[...] = dk_s[...] * ((1.0 / LOG2E) if q_scaled else scale)
        dv_ref[...] = dv_s[...]
        if forget:
            df_ref[...] = jnp.concatenate([-jnp.sum(df_s[h], axis=-1, keepdims=True) for h in hs], axis=1)

    res = lambda D, b0: pl.BlockSpec((H * D, S), lambda j: (b0, 0))
    blk = lambda D, b0: pl.BlockSpec((H * D, T), lambda j: (b0, j))
    row3 = lambda b0: pl.BlockSpec((H, 1, S), lambda j: (b0, 0, 0))
    in_specs = [res(Dk, qb), blk(Dk, kb), blk(Dv, vb), res(Dv, 0), row3(0), row3(db)]
    ins = [q_src, k_src, v_src, do, lse, delta]
    out_specs = [res(Dk, 0), blk(Dk, 0), blk(Dv, 0)]
    out_shape = [jax.ShapeDtypeStruct((H * Dk, S), F32), jax.ShapeDtypeStruct((H * Dk, S), F32),
                 jax.ShapeDtypeStruct((H * Dv, S), F32)]
    scratch = [pltpu.VMEM((H * Dk, T), F32), pltpu.VMEM((H * Dv, T), F32)]
    if forget:
        in_specs += [row3(0), pl.BlockSpec((S, H), lambda j: (0, 0))]
        ins += [f_row, f_col]
        out_specs.append(pl.BlockSpec((T, H), lambda j: (j, 0)))
        out_shape.append(jax.ShapeDtypeStruct((S, H), F32))
        scratch.append(pltpu.VMEM((H, T, min(T, LANES)), F32))
    return pl.pallas_call(
        body, name=name, grid=(nq,),
        in_specs=in_specs, out_specs=out_specs, out_shape=out_shape, scratch_shapes=scratch,
        compiler_params=_params("arbitrary"),
    )(*ins)


GW = SWA_GROUP * WINDOW


def _swa_masks(i):
    r = lax.broadcasted_iota(jnp.int32, (WINDOW, GW), 0)
    c = lax.broadcasted_iota(jnp.int32, (WINDOW, GW), 1) % WINDOW
    return (r > c) & (i > 0), r <= c


def _swa_specs():
    W = WINDOW
    kv_rows = SWA_KV_HEADS * HEAD_DIM
    q = pl.BlockSpec((SWA_Q_HEADS * HEAD_DIM, W), lambda i: (0, i))
    prev = lambda b: pl.BlockSpec((kv_rows, W), lambda i: (b, jnp.maximum(i - 1, 0)))
    cur = lambda b: pl.BlockSpec((kv_rows, W), lambda i: (b, i))
    bias = pl.BlockSpec((SWA_KV_HEADS, 2 * W, GW), lambda i: (0, 0, 0))
    stat = pl.BlockSpec((SWA_Q_HEADS, W), lambda i: (0, i))
    sink = pl.BlockSpec(memory_space=pltpu.SMEM)
    return q, prev(4), cur(4), prev(5), cur(5), bias, stat, sink


def _group_lanes(ref, g, rows_per_head):
    h0 = g * SWA_GROUP
    return jnp.concatenate([ref[(h0 + j) * rows_per_head:(h0 + j + 1) * rows_per_head, :] for j in range(SWA_GROUP)], axis=1)


def _swa_scores(g, q_ref, kp_ref, kc_ref, b_ref, masks):
    rows = slice(g * HEAD_DIM, (g + 1) * HEAD_DIM)
    qg = _group_lanes(q_ref, g, HEAD_DIM)
    scale = HEAD_DIM ** -0.5
    s_p = jnp.where(masks[0], _dot(kp_ref[rows, :], qg, TN) * scale + b_ref[g, 0:WINDOW, :], NEG_INF)
    s_c = jnp.where(masks[1], _dot(kc_ref[rows, :], qg, TN) * scale + b_ref[g, WINDOW:2 * WINDOW, :], NEG_INF)
    return qg, rows, s_p, s_c


def _sink_row(sink_ref, g):
    return jnp.concatenate([jnp.full((1, WINDOW), sink_ref[g * SWA_GROUP + j], F32) for j in range(SWA_GROUP)], axis=1)


def _swa_fwd(qkv, bias_g, sinks, *, name):
    S = qkv.shape[1]
    qs, kp, kc, vp, vc, bs, stat, sk = _swa_specs()
    gs = range(SWA_KV_HEADS)

    def body(sink_ref, q_ref, kp_ref, kc_ref, vp_ref, vc_ref, b_ref, o_ref, lse_ref):
        masks = _swa_masks(pl.program_id(0))
        sc = [_swa_scores(g, q_ref, kp_ref, kc_ref, b_ref, masks) for g in gs]
        sinks_g = [_sink_row(sink_ref, g) for g in gs]
        m = [jnp.maximum(jnp.maximum(jnp.max(sc[g][2], axis=0, keepdims=True), jnp.max(sc[g][3], axis=0, keepdims=True)),
                         sinks_g[g]) for g in gs]
        p_p = [jnp.exp(sc[g][2] - m[g]) for g in gs]
        p_c = [jnp.exp(sc[g][3] - m[g]) for g in gs]
        l = [jnp.sum(p_p[g], axis=0, keepdims=True) + jnp.sum(p_c[g], axis=0, keepdims=True) + jnp.exp(sinks_g[g] - m[g])
             for g in gs]
        o = [_dot(vp_ref[sc[g][1], :], p_p[g].astype(BF16)) + _dot(vc_ref[sc[g][1], :], p_c[g].astype(BF16)) for g in gs]
        for g in gs:
            og = o[g] / l[g]
            lse = m[g] + jnp.log(l[g])
            for j in range(SWA_GROUP):
                h = g * SWA_GROUP + j
                o_ref[h * HEAD_DIM:(h + 1) * HEAD_DIM, :] = og[:, j * WINDOW:(j + 1) * WINDOW]
                lse_ref[h:h + 1, :] = lse[:, j * WINDOW:(j + 1) * WINDOW]

    return pl.pallas_call(
        body, name=name, grid=(S // WINDOW,),
        in_specs=[sk, qs, kp, kc, vp, vc, bs],
        out_specs=[qs, stat],
        out_shape=[jax.ShapeDtypeStruct((SWA_Q_HEADS * HEAD_DIM, S), F32), jax.ShapeDtypeStruct((SWA_Q_HEADS, S), F32)],
        compiler_params=_params("parallel"),
    )(sinks, qkv, qkv, qkv, qkv, qkv, bias_g)


def _swa_bwd(qkv, bias_g, sinks, do, lse, delta, *, name):
    S = qkv.shape[1]
    W = WINDOW
    qs, kp, kc, vp, vc, bs, stat, sk = _swa_specs()
    scale = HEAD_DIM ** -0.5
    kv_rows = SWA_KV_HEADS * HEAD_DIM
    gs = range(SWA_KV_HEADS)

    def body(sink_ref, q_ref, kp_ref, kc_ref, vp_ref, vc_ref, b_ref, do_ref, lse_ref, dl_ref,
             dq_ref, dkv_ref, db_ref, dsk_ref):
        i = pl.program_id(0)

        @pl.when(i == 0)
        def _():
            dkv_ref[...] = jnp.zeros_like(dkv_ref)
            db_ref[...] = jnp.zeros_like(db_ref)
            dsk_ref[...] = jnp.zeros_like(dsk_ref)

        masks = _swa_masks(i)
        prev = pl.ds(pl.multiple_of(jnp.maximum(i - 1, 0) * W, W), W)
        cur = pl.ds(pl.multiple_of(i * W, W), W)
        sc = [_swa_scores(g, q_ref, kp_ref, kc_ref, b_ref, masks) for g in gs]
        dog = [_group_lanes(do_ref, g, HEAD_DIM) for g in gs]
        lse = [_group_lanes(lse_ref, g, 1) for g in gs]
        dl = [_group_lanes(dl_ref, g, 1) for g in gs]
        p_p = [jnp.exp(sc[g][2] - lse[g]) for g in gs]
        p_c = [jnp.exp(sc[g][3] - lse[g]) for g in gs]
        ds_p = [p_p[g] * (_dot(vp_ref[sc[g][1], :], dog[g], TN) - dl[g]) for g in gs]
        ds_c = [p_c[g] * (_dot(vc_ref[sc[g][1], :], dog[g], TN) - dl[g]) for g in gs]
        for g in gs:
            db_ref[g, 0:W, :] += ds_p[g]
            db_ref[g, W:2 * W, :] += ds_c[g]
            dsk = jnp.exp(_sink_row(sink_ref, g) - lse[g]) * dl[g]
            for j in range(SWA_GROUP):
                h = g * SWA_GROUP + j
                dsk_ref[h:h + 1, :] -= jnp.broadcast_to(jnp.sum(dsk[:, j * W:(j + 1) * W], axis=1, keepdims=True), (1, LANES))
        dsb_p = [d.astype(BF16) for d in ds_p]
        dsb_c = [d.astype(BF16) for d in ds_c]
        for g in gs:
            rows = sc[g][1]
            dq = (_dot(kp_ref[rows, :], dsb_p[g]) + _dot(kc_ref[rows, :], dsb_c[g])) * scale
            for j in range(SWA_GROUP):
                h = g * SWA_GROUP + j
                dq_ref[h * HEAD_DIM:(h + 1) * HEAD_DIM, :] = dq[:, j * W:(j + 1) * W]
        for g in gs:
            rows = sc[g][1]
            vrows = slice(kv_rows + rows.start, kv_rows + rows.stop)
            dkv_ref[rows, prev] += _dot(sc[g][0], dsb_p[g], NT) * scale
            dkv_ref[rows, cur] += _dot(sc[g][0], dsb_c[g], NT) * scale
            dkv_ref[vrows, prev] += _dot(dog[g], p_p[g].astype(BF16), NT)
            dkv_ref[vrows, cur] += _dot(dog[g], p_c[g].astype(BF16), NT)

    return pl.pallas_call(
        body, name=name, grid=(S // W,),
        in_specs=[sk, qs, kp, kc, vp, vc, bs, qs, stat, stat],
        out_specs=[qs, pl.BlockSpec((2 * kv_rows, S), lambda i: (0, 0)), bs, pl.BlockSpec((SWA_Q_HEADS, LANES), lambda i: (0, 0))],
        out_shape=[jax.ShapeDtypeStruct((SWA_Q_HEADS * HEAD_DIM, S), F32), jax.ShapeDtypeStruct((2 * kv_rows, S), F32),
                   jax.ShapeDtypeStruct((SWA_KV_HEADS, 2 * W, GW), F32), jax.ShapeDtypeStruct((SWA_Q_HEADS, LANES), F32)],
        compiler_params=_params("arbitrary"),
    )(sinks, qkv, qkv, qkv, qkv, qkv, bias_g, do, lse, delta)


def _rel_onehot_t():
    qi = jnp.arange(WINDOW, dtype=jnp.int32)[None, :] + WINDOW
    kj = jnp.arange(2 * WINDOW, dtype=jnp.int32)[:, None]
    dist = qi - kj
    max_exact = REL_BUCKETS // 2
    d = jnp.maximum(dist, 0)
    log_ratio = jnp.log(jnp.maximum(d, 1).astype(F32) / max_exact) / math.log(REL_MAX_DIST / max_exact)
    large = jnp.minimum(max_exact + (log_ratio * (REL_BUCKETS - max_exact)).astype(jnp.int32), REL_BUCKETS - 1)
    bucket = jnp.where(d < max_exact, d, large).reshape(-1)
    return (bucket[None, :] == jnp.arange(REL_BUCKETS, dtype=jnp.int32)[:, None]).astype(BF16)


def _bias_table(rel_bias_t, onehot_t):
    Hq, NB = rel_bias_t.shape
    N = onehot_t.shape[1]
    tn = _tile(N, 4096)

    def body(r_ref, oh_ref, o_ref):
        oh = oh_ref[...]
        a1, a2, a3 = _split3(r_ref[...])
        o_ref[...] = _dot(a1, oh) + _dot(a2, oh) + _dot(a3, oh)

    return pl.pallas_call(
        body, name="rel_bias_table", grid=(N // tn,),
        in_specs=[pl.BlockSpec((Hq, NB), lambda j: (0, 0)), pl.BlockSpec((NB, tn), lambda j: (0, j))],
        out_specs=pl.BlockSpec((Hq, tn), lambda j: (0, j)),
        out_shape=jax.ShapeDtypeStruct((Hq, N), F32),
        compiler_params=_params("parallel"),
    )(rel_bias_t, onehot_t)


def _bias_table_bwd(dbias, onehot_t):
    L, Hq, N = dbias.shape
    NB = onehot_t.shape[0]
    tn = _tile(N, 4096)

    def body(d_ref, oh_ref, o_ref):
        @pl.when(pl.program_id(0) == 0)
        def _():
            o_ref[...] = jnp.zeros_like(o_ref)

        d = d_ref[0]
        for l in range(1, L):
            d = d + d_ref[l]
        oh = oh_ref[...]
        a1, a2, a3 = _split3(d)
        o_ref[...] += _dot(a1, oh, NT) + _dot(a2, oh, NT) + _dot(a3, oh, NT)

    return pl.pallas_call(
        body, name="rel_bias_bwd", grid=(N // tn,),
        in_specs=[pl.BlockSpec((L, Hq, tn), lambda j: (0, 0, j)), pl.BlockSpec((NB, tn), lambda j: (0, j))],
        out_specs=pl.BlockSpec((Hq, NB), lambda j: (0, 0)),
        out_shape=jax.ShapeDtypeStruct((Hq, NB), F32),
        compiler_params=_params("arbitrary"),
    )(dbias, onehot_t)


def _gate_fwd(lat, fb_col, *, name):
    S = lat.shape[1]
    tn = _tile(S, 256)

    def body(z_ref, fb_ref, o_ref, carry):
        @pl.when(pl.program_id(0) == 0)
        def _():
            carry[...] = jnp.zeros_like(carry)

        z = z_ref[...] + fb_ref[...]
        lf = jnp.minimum(z, 0.0) - jnp.log1p(jnp.exp(-jnp.abs(z)))
        r = lax.broadcasted_iota(jnp.int32, (tn, tn), 0)
        c = lax.broadcasted_iota(jnp.int32, (tn, tn), 1)
        tri = (r <= c).astype(BF16)
        a1, a2, a3 = _split3(lf)
        cum = _dot(a1, tri) + _dot(a2, tri) + _dot(a3, tri) + carry[:, 0:1]
        o_ref[...] = cum
        carry[...] = jnp.broadcast_to(cum[:, tn - 1:tn], carry.shape)

    return pl.pallas_call(
        body, name=name, grid=(S // tn,),
        in_specs=[pl.BlockSpec((GATE_ROWS, tn), lambda i: (0, i)), pl.BlockSpec((GATE_ROWS, 1), lambda i: (0, 0))],
        out_specs=pl.BlockSpec((GATE_ROWS, tn), lambda i: (0, i)),
        out_shape=jax.ShapeDtypeStruct((GATE_ROWS, S), F32),
        scratch_shapes=[pltpu.VMEM((GATE_ROWS, LANES), F32)],
        compiler_params=_params("arbitrary"),
    )(lat, fb_col)


def _gate_bwd(lat, fb_col, dF, *, name):
    S = lat.shape[1]
    tn = _tile(S, 256)
    nt = S // tn

    def body(z_ref, fb_ref, df_ref, dz_ref, dfb_ref, carry):
        @pl.when(pl.program_id(0) == 0)
        def _():
            carry[...] = jnp.zeros_like(carry)
            dfb_ref[...] = jnp.zeros_like(dfb_ref)

        r = lax.broadcasted_iota(jnp.int32, (tn, tn), 0)
        c = lax.broadcasted_iota(jnp.int32, (tn, tn), 1)
        tri = (r >= c).astype(BF16)
        a1, a2, a3 = _split3(df_ref[...])
        dlf = _dot(a1, tri) + _dot(a2, tri) + _dot(a3, tri) + carry[:, 0:1]
        carry[...] = jnp.broadcast_to(dlf[:, 0:1], carry.shape)
        z = z_ref[...] + fb_ref[...]
        row = lax.broadcasted_iota(jnp.int32, (GATE_ROWS, tn), 0)
        dz = jnp.where(row < FOX_HEADS, dlf / (1.0 + jnp.exp(z)), 0.0)
        dz_ref[...] = dz
        dfb_ref[...] += jnp.sum(dz, axis=1, keepdims=True)

    blk = pl.BlockSpec((GATE_ROWS, tn), lambda i: (0, nt - 1 - i))
    vec = pl.BlockSpec((GATE_ROWS, 1), lambda i: (0, 0))
    return pl.pallas_call(
        body, name=name, grid=(nt,),
        in_specs=[blk, vec, blk], out_specs=[blk, vec],
        out_shape=[jax.ShapeDtypeStruct((GATE_ROWS, S), F32), jax.ShapeDtypeStruct((GATE_ROWS, 1), F32)],
        scratch_shapes=[pltpu.VMEM((GATE_ROWS, LANES), F32)],
        compiler_params=_params("arbitrary"),
    )(lat, fb_col, dF)


def _rope_tables(S):
    pos = jnp.arange(S, dtype=F32)
    inv_freq = ROPE_THETA ** (-(jnp.arange(MLA_ROPE // 2, dtype=F32) * 2.0 / MLA_ROPE))
    ang = pos[:, None] * inv_freq[None, :]
    cos, sin = jnp.cos(ang).T, jnp.sin(ang).T
    z16 = jnp.zeros_like(cos)

    def slab(lo, fill):
        def put(first, second, f):
            return jnp.concatenate([jnp.full((lo, S), f, F32), first, second, jnp.full((LANES - lo - MLA_ROPE, S), f, F32)], axis=0)
        return put(cos, cos, fill), put(-sin, z16, 0.0), put(z16, sin, 0.0)

    tq = tuple(jnp.tile(t, (MLA_HEADS, 1)) for t in slab(MLA_NOPE, 1.0))
    return tq, slab(0, 0.0)


def _rope(x, c, s1, s2):
    n = x.shape[0]
    half = MLA_ROPE // 2
    return x * c + pltpu.roll(x, n - half, 0) * s1 + pltpu.roll(x, half, 0) * s2


def _rope_t(dy, c, s1, s2):
    n = dy.shape[0]
    half = MLA_ROPE // 2
    return dy * c + pltpu.roll(dy * s1, half, 0) + pltpu.roll(dy * s2, n - half, 0)


KR_SLAB0 = MLA_Q_RANK + MLA_KV_RANK


def _mla_prep_fwd(lat, g_q, g_kv, w_uq_t, w_ukv_t, tq, tmisc, *, name):
    S = lat.shape[1]
    tn = _tile(S, 512)
    QW = MLA_HEADS * MLA_PAD

    def body(lat_ref, gq_ref, gkv_ref, wq_ref, wkv_ref, c_ref, s1_ref, s2_ref, cm_ref, s1m_ref, s2m_ref,
             nq_ref, nkv_ref, q_ref, k_ref, v_ref):
        x = pltpu.roll(lat_ref[...], LAT_ROWS - LAT_SHIFT, 0)
        nq = _col_rms(x[0:MLA_Q_RANK, :], gq_ref[...]).astype(BF16)
        nkv = _col_rms(x[MLA_Q_RANK:KR_SLAB0, :], gkv_ref[...]).astype(BF16)
        nq_ref[...] = nq
        nkv_ref[...] = nkv
        q = _rope(_dot(wq_ref[...], nq), c_ref[...], s1_ref[...], s2_ref[...])
        q_ref[...] = (q * (MLA_SCALE * LOG2E)).astype(BF16)
        kv = _dot(wkv_ref[...], nkv).astype(BF16)
        kr = _rope(x[KR_SLAB0:LAT_ROWS, :], cm_ref[...], s1m_ref[...], s2m_ref[...]).astype(BF16)
        for h in range(MLA_HEADS):
            k_ref[h * MLA_PAD:h * MLA_PAD + MLA_NOPE, :] = kv[h * LANES:h * LANES + MLA_NOPE, :]
            k_ref[h * MLA_PAD + MLA_NOPE:(h + 1) * MLA_PAD, :] = kr[0:MLA_PAD - MLA_NOPE, :]
            v_ref[h * HEAD_DIM:(h + 1) * HEAD_DIM, :] = kv[h * LANES + MLA_NOPE:(h + 1) * LANES, :]

    def col(rows):
        return pl.BlockSpec((rows, tn), lambda i: (0, i))

    def full(a):
        return pl.BlockSpec(a.shape, lambda i: (0, 0))

    return pl.pallas_call(
        body, name=name, grid=(S // tn,),
        in_specs=[col(LAT_ROWS), full(g_q), full(g_kv), full(w_uq_t), full(w_ukv_t),
                  col(QW), col(QW), col(QW), col(LANES), col(LANES), col(LANES)],
        out_specs=[col(MLA_Q_RANK), col(MLA_KV_RANK), col(QW), col(QW), col(MLA_HEADS * HEAD_DIM)],
        out_shape=[jax.ShapeDtypeStruct((MLA_Q_RANK, S), BF16), jax.ShapeDtypeStruct((MLA_KV_RANK, S), BF16),
                   jax.ShapeDtypeStruct((QW, S), BF16), jax.ShapeDtypeStruct((QW, S), BF16),
                   jax.ShapeDtypeStruct((MLA_HEADS * HEAD_DIM, S), BF16)],
        compiler_params=_params("parallel"),
    )(lat, g_q, g_kv, w_uq_t, w_ukv_t, *tq, *tmisc)


def _mla_prep_bwd(lat, nq, nkv, g_q, g_kv, w_uq_p, w_ukv, tq, tmisc, dq, dk, dv, dflog, *, name):
    S = lat.shape[1]
    tn = _tile(S, 512)
    QW = MLA_HEADS * MLA_PAD

    def body(lat_ref, nq_ref, nkv_ref, gq_ref, gkv_ref, wq_ref, wkv_ref, c_ref, s1_ref, s2_ref,
             cm_ref, s1m_ref, s2m_ref, dq_ref, dk_ref, dv_ref, dfl_ref,
             dlat_ref, dwq_ref, dwkv_ref, dgq_ref, dgkv_ref, y_s):
        @pl.when(pl.program_id(0) == 0)
        def _():
            dwq_ref[...] = jnp.zeros_like(dwq_ref)
            dwkv_ref[...] = jnp.zeros_like(dwkv_ref)
            dgq_ref[...] = jnp.zeros_like(dgq_ref)
            dgkv_ref[...] = jnp.zeros_like(dgkv_ref)

        x = pltpu.roll(lat_ref[...], LAT_ROWS - LAT_SHIFT, 0)
        dqm = _rope_t(dq_ref[...], c_ref[...], s1_ref[...], s2_ref[...]).astype(BF16)
        dwq_ref[...] += _dot(dqm, nq_ref[...], NT)
        dx, dg = _col_rms_bwd(x[0:MLA_Q_RANK, :], gq_ref[...], _dot(wq_ref[...], dqm))
        y_s[0:MLA_Q_RANK, :] = dx
        dgq_ref[...] += dg
        dkv = jnp.concatenate(
            [part for h in range(MLA_HEADS)
             for part in (dk_ref[h * MLA_PAD:h * MLA_PAD + MLA_NOPE, :], dv_ref[h * HEAD_DIM:(h + 1) * HEAD_DIM, :])],
            axis=0).astype(BF16)
        dwkv_ref[...] += _dot(dkv, nkv_ref[...], NT)
        dx, dg = _col_rms_bwd(x[MLA_Q_RANK:KR_SLAB0, :], gkv_ref[...], _dot(wkv_ref[...], dkv))
        y_s[MLA_Q_RANK:KR_SLAB0, :] = dx
        dgkv_ref[...] += dg
        dkr = dk_ref[MLA_NOPE:MLA_PAD, :]
        for h in range(1, MLA_HEADS):
            dkr = dkr + dk_ref[h * MLA_PAD + MLA_NOPE:(h + 1) * MLA_PAD, :]
        dkr = jnp.concatenate([dkr, jnp.zeros((MLA_NOPE, tn), F32)], axis=0)
        y_s[KR_SLAB0:LAT_ROWS, :] = _rope_t(dkr, cm_ref[...], s1m_ref[...], s2m_ref[...])
        y = pltpu.roll(y_s[...], LAT_SHIFT, 0)
        row = lax.broadcasted_iota(jnp.int32, (LAT_ROWS, tn), 0)
        dfl = jnp.concatenate([dfl_ref[...], jnp.zeros((LAT_ROWS - GATE_ROWS, tn), F32)], axis=0)
        dlat_ref[...] = jnp.where(row < LAT_SHIFT, dfl, y).astype(BF16)

    def col(rows):
        return pl.BlockSpec((rows, tn), lambda i: (0, i))

    def full(a):
        return pl.BlockSpec(a.shape, lambda i: (0, 0))

    def acc(r, c):
        return pl.BlockSpec((r, c), lambda i: (0, 0))

    return pl.pallas_call(
        body, name=name, grid=(S // tn,),
        in_specs=[col(LAT_ROWS), col(MLA_Q_RANK), col(MLA_KV_RANK), full(g_q), full(g_kv),
                  full(w_uq_p), full(w_ukv), col(QW), col(QW), col(QW), col(LANES), col(LANES), col(LANES),
                  col(QW), col(QW), col(MLA_HEADS * HEAD_DIM), col(GATE_ROWS)],
        out_specs=[col(LAT_ROWS), acc(QW, MLA_Q_RANK), acc(QW, MLA_KV_RANK), acc(MLA_Q_RANK, 1), acc(MLA_KV_RANK, 1)],
        out_shape=[jax.ShapeDtypeStruct((LAT_ROWS, S), BF16), jax.ShapeDtypeStruct((QW, MLA_Q_RANK), F32),
                   jax.ShapeDtypeStruct((QW, MLA_KV_RANK), F32), jax.ShapeDtypeStruct((MLA_Q_RANK, 1), F32),
                   jax.ShapeDtypeStruct((MLA_KV_RANK, 1), F32)],
        scratch_shapes=[pltpu.VMEM((LAT_ROWS, tn), F32)],
        compiler_params=_params("arbitrary"),
    )(lat, nq, nkv, g_q, g_kv, w_uq_p, w_ukv, *tq, *tmisc, dq, dk, dv, dflog)


def _dproj_cast(dqa, dkva, dqf, dkf, dvf, dlat, *, name):
    S = dqa.shape[1]
    tn = _tile(S, 512)
    parts = (dqa, dkva, dqf, dkf, dvf, dlat)

    def body(*refs):
        o_ref = refs[-1]
        r0 = 0
        for ref in refs[:-1]:
            n = ref.shape[0]
            o_ref[r0:r0 + n, :] = ref[...].astype(BF16)
            r0 += n

    return pl.pallas_call(
        body, name=name, grid=(S // tn,),
        in_specs=[pl.BlockSpec((p.shape[0], tn), lambda i: (0, i)) for p in parts],
        out_specs=pl.BlockSpec((IN_ROWS, tn), lambda i: (0, i)),
        out_shape=jax.ShapeDtypeStruct((IN_ROWS, S), BF16),
        compiler_params=_params("parallel"),
    )(*parts)


GELU_C = math.sqrt(2.0 / math.pi)
GELU_A = 0.044715


HALO = 16


def _shift_down(a, k, fill):
    r = pltpu.roll(a, k, 0)
    row = lax.broadcasted_iota(jnp.int32, (8, a.shape[1]), 0)
    head = r[0:8, :]
    for i in range(k):
        head = jnp.where(row == i, fill[len(fill) - k + i], head)
    return jnp.concatenate([head, r[8:, :]], axis=0)


def _shift_up(d, k, fill):
    n = d.shape[0]
    r = pltpu.roll(d, n - k, 0)
    row = lax.broadcasted_iota(jnp.int32, (8, d.shape[1]), 0)
    tail = r[n - 8:n, :]
    for i in range(k):
        tail = jnp.where(row == 8 - k + i, fill[i], tail)
    return jnp.concatenate([r[0:n - 8, :], tail], axis=0)


def _conv_taps(a, before, w_ref, b_ref):
    a1 = _shift_down(a, 1, before)
    a2 = _shift_down(a, 2, before)
    return ((b_ref[...] + w_ref[0:1, :] * a2) + w_ref[1:2, :] * a1) + w_ref[2:3, :] * a


def _rows_before(halo_ref, first):
    h = halo_ref[HALO - 2:HALO, :].astype(F32)
    return jnp.where(first, 0.0, h[0:1, :]), jnp.where(first, 0.0, h[1:2, :])


def _conv_specs(S, tm, tc, nc):
    hb = tm // HALO
    main = lambda off: pl.BlockSpec((tm, tc), lambda j, i: (i, j + off))
    prev = lambda off: pl.BlockSpec((HALO, tc), lambda j, i: (jnp.maximum(i * hb - 1, 0), j + off))
    wspec = lambda off: pl.BlockSpec((3, tc), lambda j, i: (0, j + off))
    bspec = lambda off: pl.BlockSpec((1, tc), lambda j, i: (0, j + off))
    return main, prev, wspec, bspec


def _conv_geglu_fwd(a, conv_w, conv_b, *, name):
    S = a.shape[0]
    tm, tc = _tile(S, 512), _tile(D_FF, 1408)
    nc = D_FF // tc
    main, prev, wspec, bspec = _conv_specs(S, tm, tc, nc)

    def body(ag_ref, au_ref, hg_ref, hu_ref, wg_ref, wu_ref, bg_ref, bu_ref, u_ref, z_ref):
        first = pl.program_id(1) == 0
        gate = _conv_taps(ag_ref[...].astype(F32), _rows_before(hg_ref, first), wg_ref, bg_ref)
        up = _conv_taps(au_ref[...].astype(F32), _rows_before(hu_ref, first), wu_ref, bu_ref)
        u_ref[0] = gate
        u_ref[1] = up
        cdf = 0.5 * (1.0 + jnp.tanh(GELU_C * (gate + GELU_A * (gate * gate * gate))))
        z_ref[...] = (gate * cdf * up).astype(BF16)

    return pl.pallas_call(
        body, name=name, grid=(nc, S // tm),
        in_specs=[main(0), main(nc), prev(0), prev(nc), wspec(0), wspec(nc), bspec(0), bspec(nc)],
        out_specs=[pl.BlockSpec((2, tm, tc), lambda j, i: (0, i, j)), pl.BlockSpec((tm, tc), lambda j, i: (i, j))],
        out_shape=[jax.ShapeDtypeStruct((2, S, D_FF), F32), jax.ShapeDtypeStruct((S, D_FF), BF16)],
        compiler_params=_params("parallel", "arbitrary"),
    )(a, a, a, a, conv_w, conv_w, conv_b, conv_b)


def _geglu_bwd(gate, up, dz):
    g2x = gate * gate
    th = jnp.tanh(GELU_C * (gate + GELU_A * (g2x * gate)))
    cdf = 0.5 * (1.0 + th)
    dgelu = cdf + gate * (0.5 * (1.0 - th * th) * (GELU_C * (1.0 + 3.0 * GELU_A * g2x)))
    return dz * up * dgelu, dz * (gate * cdf)


def _conv_geglu_bwd(a, u, conv_w, dz, *, name):
    S = a.shape[0]
    tm, tc = _tile(S, 512), _tile(D_FF, 1408)
    nc = D_FF // tc
    nr = S // tm
    main, _, wspec, _ = _conv_specs(S, tm, tc, nc)
    hb = tm // 8

    def body(ag_ref, au_ref, u_ref, un_ref, wg_ref, wu_ref, dz_ref, dzn_ref, da_ref, dw_ref, db_ref):
        i = pl.program_id(1)
        last = i == nr - 1

        @pl.when(i == 0)
        def _():
            dw_ref[...] = jnp.zeros_like(dw_ref)
            db_ref[...] = jnp.zeros_like(db_ref)

        dus = _geglu_bwd(u_ref[0], u_ref[1], dz_ref[...])
        dus_n = _geglu_bwd(un_ref[0], un_ref[1], dzn_ref[...])
        for half, a_ref, w_ref in ((0, ag_ref, wg_ref), (1, au_ref, wu_ref)):
            du, du_n = dus[half], dus_n[half]
            after = (jnp.where(last, 0.0, du_n[0:1, :]), jnp.where(last, 0.0, du_n[1:2, :]))
            shifted = (_shift_up(du, 2, after), _shift_up(du, 1, after), du)
            da_ref[half] = (w_ref[2:3, :] * du + w_ref[1:2, :] * shifted[1] + w_ref[0:1, :] * shifted[0]).astype(BF16)
            af = a_ref[...].astype(F32)
            for tap in range(3):
                dw_ref[half, tap:tap + 1, :] += jnp.sum(shifted[tap] * af, axis=0, keepdims=True)
            db_ref[half] += jnp.sum(du, axis=0, keepdims=True)

    nxt8 = lambda j, i: (0, jnp.minimum((i + 1) * hb, S // 8 - 1), j)
    return pl.pallas_call(
        body, name=name, grid=(nc, nr),
        in_specs=[main(0), main(nc), pl.BlockSpec((2, tm, tc), lambda j, i: (0, i, j)), pl.BlockSpec((2, 8, tc), nxt8),
                  wspec(0), wspec(nc), pl.BlockSpec((tm, tc), lambda j, i: (i, j)),
                  pl.BlockSpec((8, tc), lambda j, i: (jnp.minimum((i + 1) * hb, S // 8 - 1), j))],
        out_specs=[pl.BlockSpec((2, tm, tc), lambda j, i: (0, i, j)), pl.BlockSpec((2, 3, tc), lambda j, i: (0, 0, j)),
                   pl.BlockSpec((2, 1, tc), lambda j, i: (0, 0, j))],
        out_shape=[jax.ShapeDtypeStruct((2, S, D_FF), BF16), jax.ShapeDtypeStruct((2, 3, D_FF), F32),
                   jax.ShapeDtypeStruct((2, 1, D_FF), F32)],
        compiler_params=_params("parallel", "arbitrary"),
    )(a, a, u, u, conv_w, conv_w, dz, dz)


ROW_BLOCK_BYTES = 1536 * 1024


def _row_tile(rows, cols):
    return rows if rows * cols * 4 <= ROW_BLOCK_BYTES else _tile(rows, ROW_TILE)


def _adamw_update(w, g, m, v):
    m = ADAM_B1 * m + (1.0 - ADAM_B1) * g
    v = ADAM_B2 * v + (1.0 - ADAM_B2) * jnp.square(g)
    m_hat = m / (1.0 - ADAM_B1 ** ADAM_STEP)
    v_hat = v / (1.0 - ADAM_B2 ** ADAM_STEP)
    return -ADAM_LR * (m_hat / (jnp.sqrt(v_hat) + ADAM_EPS) + ADAM_WD * w), m, v


def _adamw(w, g, m, v, *, name):
    L, A, B = w.shape
    ta = _tile(A, ROW_TILE)

    def body(w_ref, g_ref, m_ref, v_ref, d_ref, mo_ref, vo_ref):
        d_ref[...], mo_ref[...], vo_ref[...] = _adamw_update(w_ref[...], g_ref[...], m_ref[...], v_ref[...])

    blk = pl.BlockSpec((None, ta, B), lambda l, i: (l, i, 0))
    shp = jax.ShapeDtypeStruct((L, A, B), F32)
    return pl.pallas_call(
        body, name=name, grid=(L, A // ta),
        in_specs=[blk] * 4, out_specs=[blk] * 3, out_shape=[shp] * 3,
        compiler_params=_params("parallel", "parallel"),
    )(w, g, m, v)


def _scalar(v):
    return jnp.reshape(v, (1,)).astype(jnp.int32)


def _adamw_halves(w, g_mine, g_other, m, v, *, name):
    L, A, B = w.shape
    ta = _row_tile(A // 2, B)
    nb = A // 2 // ta

    def body(c_ref, w_ref, gm_ref, go_ref, m_ref, v_ref, g_ref, d_ref, mo_ref, vo_ref):
        g = jnp.where(pl.program_id(1) // nb == c_ref[0], gm_ref[...], go_ref[...])
        g_ref[...] = g
        d_ref[...], mo_ref[...], vo_ref[...] = _adamw_update(w_ref[...], g, m_ref[...], v_ref[...])

    blk = pl.BlockSpec((None, ta, B), lambda l, i, c_ref: (l, i, 0))
    half = pl.BlockSpec((None, ta, B), lambda l, i, c_ref: (l, i % nb, 0))
    shp = jax.ShapeDtypeStruct((L, A, B), F32)
    return pl.pallas_call(
        body, name=name,
        grid_spec=pltpu.PrefetchScalarGridSpec(num_scalar_prefetch=1, grid=(L, A // ta),
                                               in_specs=[blk, half, half, blk, blk], out_specs=[blk] * 4),
        out_shape=[shp] * 4,
        compiler_params=_params("parallel", "parallel"),
    )(_scalar(lax.axis_index("c")), w, g_mine, g_other, m, v)


def _chip_index():
    return 2 * lax.axis_index("x") + lax.axis_index("y")


def _pair_sum(g, recv, *, name):
    n, A, B = g.shape
    ta = _row_tile(A // 2, B)
    nb = A // 2 // ta

    def body(c_ref, g_ref, r_ref, o_ref):
        o_ref[...] = g_ref[...] + r_ref[...]

    return pl.pallas_call(
        body, name=name,
        grid_spec=pltpu.PrefetchScalarGridSpec(
            num_scalar_prefetch=1, grid=(n, nb),
            in_specs=[pl.BlockSpec((None, ta, B), lambda s, r, c_ref: (s, c_ref[0] * nb + r, 0)),
                      pl.BlockSpec((None, ta, B), lambda s, r, c_ref: (s, r, 0))],
            out_specs=pl.BlockSpec((None, ta, B), lambda s, r, c_ref: (s, r, 0))),
        out_shape=jax.ShapeDtypeStruct((n, A // 2, B), F32),
        compiler_params=_params("parallel", "parallel"),
    )(_scalar(lax.axis_index("c")), g, recv)


def _chip_sum(landed, own, *, name):
    n, A2, B = landed.shape
    ta = _row_tile(A2, B)

    def body(me_ref, *refs):
        slots, own_ref, o_ref = refs[:n], refs[n], refs[n + 1]
        parts = [jnp.where(me_ref[0] == s, own_ref[...], slots[s][...]) for s in range(n)]
        o_ref[...] = ((parts[0] + parts[1]) + parts[2]) + parts[3]

    def slot(s):
        return pl.BlockSpec((None, ta, B), lambda r, me_ref: (jnp.where(me_ref[0] == s, (s + 1) % n, s), r, 0))

    return pl.pallas_call(
        body, name=name,
        grid_spec=pltpu.PrefetchScalarGridSpec(
            num_scalar_prefetch=1, grid=(A2 // ta,),
            in_specs=[slot(s) for s in range(n)] + [pl.BlockSpec((None, ta, B), lambda r, me_ref: (me_ref[0], r, 0))],
            out_specs=pl.BlockSpec((ta, B), lambda r, me_ref: (r, 0))),
        out_shape=jax.ShapeDtypeStruct((A2, B), F32),
        compiler_params=_params("parallel"),
    )(_scalar(_chip_index()), *([landed] * n), own)


HBM_SPEC = pl.BlockSpec(memory_space=pl.ANY)
COMM_PARAMS = pltpu.CompilerParams(has_side_effects=True)


def _mesh_pos():
    return lax.axis_index("x"), lax.axis_index("y"), lax.axis_index("c")


def _other_chips(x, y):
    return [(1 - x, y), (x, 1 - y), (1 - x, 1 - y)]


def _remote(src, dst, send_sems, recv_sems, k, to):
    return pltpu.make_async_remote_copy(src_ref=src, dst_ref=dst, send_sem=send_sems.at[k], recv_sem=recv_sems.at[k],
                                        device_id=to, device_id_type=MESH)


def _place_own(gathered, shards, *, name):
    n = len(shards)

    def body(me_ref, *refs):
        for s_ref, o_ref in zip(refs[:n], refs[2 * n:]):
            o_ref[...] = s_ref[...]

    return pl.pallas_call(
        body, name=name,
        grid_spec=pltpu.PrefetchScalarGridSpec(
            num_scalar_prefetch=1, grid=(1,),
            in_specs=[pl.BlockSpec(s.shape, lambda i, me_ref: (0, 0)) for s in shards] + [HBM_SPEC] * n,
            out_specs=[pl.BlockSpec((None,) + s.shape, lambda i, me_ref: (me_ref[0], 0, 0)) for s in shards]),
        out_shape=[jax.ShapeDtypeStruct(g.shape, g.dtype) for g in gathered],
        input_output_aliases={1 + n + k: k for k in range(n)},
        compiler_params=_params("arbitrary"),
    )(_scalar(_chip_index()), *shards, *gathered)


def _half_rows(rows, c, align=8):
    assert (rows // 2) % align == 0
    return pl.ds(pl.multiple_of(c * (rows // 2), align), rows // 2)


BF16_ROWS = 16


def _halved(rows):
    return rows % (2 * BF16_ROWS) == 0


def _gather_copies(srcs, lands, send_sems, recv_sems):
    x, y, c = _mesh_pos()
    me = 2 * x + y
    out = []
    for k in range(len(srcs)):
        a = srcs[k].shape[0]
        rows = _half_rows(a, c, BF16_ROWS) if _halved(a) else pl.ds(0, a)
        for j, (px, py) in enumerate(_other_chips(x, y)):
            send = _remote(srcs[k].at[rows], lands[k].at[me, rows], send_sems, recv_sems, 3 * k + j, (px, py, c))
            recv = _remote(srcs[k].at[rows], lands[k].at[2 * px + py, rows], send_sems, recv_sems, 3 * k + j, (px, py, c))
            out.append((send, recv))
    return out


def _gather_start(srcs):
    nu = len(srcs)
    sizes = [len(su) for su in srcs]
    offs = [2 * sum(sizes[:u]) for u in range(nu + 1)]
    lands = [[lax.empty((N_CHIPS,) + s.shape, s.dtype) for s in su] for su in srcs]
    flat = [a for u in range(nu) for a in srcs[u] + lands[u]]

    def body(*refs):
        bufs, sems, token = refs[:len(flat)], refs[len(flat):len(flat) + 2 * nu], refs[-1]
        for u, n in enumerate(sizes):
            mine = bufs[offs[u]:offs[u + 1]]
            for send, _ in _gather_copies(mine[:n], mine[n:], sems[2 * u], sems[2 * u + 1]):
                send.start()
        token[...] = jnp.zeros_like(token)

    res = pl.pallas_call(
        body, name="weight_gather_start",
        in_specs=[HBM_ONLY] * len(flat),
        out_specs=[SEM_SPEC] * (2 * nu) + [HBM_ONLY] * len(flat) + [pl.BlockSpec(memory_space=pltpu.VMEM)],
        out_shape=[pltpu.SemaphoreType.DMA((3 * n,)) for n in sizes for _ in (0, 1)] + [pltpu.HBM(a.shape, a.dtype) for a in flat]
        + [jax.ShapeDtypeStruct((1, 1), F32)],
        input_output_aliases={i: 2 * nu + i for i in range(len(flat))},
        compiler_params=SPLIT_PARAMS,
    )(*[pltpu.with_memory_space_constraint(a, pltpu.HBM) for a in flat])
    bufs = res[2 * nu:2 * nu + len(flat)]
    state = [(res[2 * u], res[2 * u + 1], list(bufs[offs[u]:offs[u] + n]), list(bufs[offs[u] + n:offs[u + 1]]))
             for u, n in enumerate(sizes)]
    return state, res[-1]


def _gather_wait(state, after, *, name):
    send_sems, recv_sems, srcs, lands = state
    n = len(srcs)

    def body(*refs):
        for send, recv in _gather_copies(refs[:n], refs[n:2 * n], refs[2 * n], refs[2 * n + 1]):
            send.wait_send()
            recv.wait_recv()

    res = pl.pallas_call(
        body, name=name,
        in_specs=[HBM_ONLY] * (2 * n) + [SEM_SPEC, SEM_SPEC, HBM_SPEC],
        out_specs=[HBM_ONLY] * (2 * n),
        out_shape=[pltpu.HBM(a.shape, a.dtype) for a in srcs + lands],
        input_output_aliases={i: i for i in range(2 * n)},
        compiler_params=SPLIT_PARAMS,
    )(*srcs, *lands, send_sems, recv_sems, after)
    return list(res[:n]), list(res[n:])


def _gather_forward(lands, *, name):
    n = len(lands)

    def body(*refs):
        bufs, outs = refs[:n], refs[n:2 * n]
        send_sems, recv_sems = refs[2 * n:]
        x, y, c = _mesh_pos()
        copies, waits = [], []
        for k in range(n):
            a = lands[k].shape[1]
            if not _halved(a):
                continue
            for j, (px, py) in enumerate(_other_chips(x, y)):
                mine = 2 * px + py, _half_rows(a, c, BF16_ROWS)
                copies.append(_remote(bufs[k].at[mine], outs[k].at[mine], send_sems, recv_sems, 3 * k + j, (x, y, 1 - c)))
                lands_here = outs[k].at[2 * px + py, _half_rows(a, 1 - c, BF16_ROWS)]
                waits.append(_remote(lands_here, lands_here, send_sems, recv_sems, 3 * k + j, (x, y, 1 - c)))
        for cp in copies:
            cp.start()
        for cp in waits:
            cp.wait_recv()
        for cp in copies:
            cp.wait_send()

    return pl.pallas_call(
        body, name=name,
        in_specs=[HBM_SPEC] * n, out_specs=[HBM_SPEC] * n,
        out_shape=[jax.ShapeDtypeStruct(a.shape, a.dtype) for a in lands],
        scratch_shapes=[pltpu.SemaphoreType.DMA((3 * n,)), pltpu.SemaphoreType.DMA((3 * n,))],
        input_output_aliases={i: i for i in range(n)},
        compiler_params=COMM_PARAMS,
    )(*lands)


def _sibling_exchange(gs, *, name):
    n = len(gs)

    def body(*refs):
        ins, outs = refs[:n], refs[n:2 * n]
        send_sems, recv_sems = refs[2 * n:]
        x, y, c = _mesh_pos()
        copies = [_remote(ins[k].at[:, _half_rows(gs[k].shape[1], 1 - c)], outs[k], send_sems, recv_sems, k, (x, y, 1 - c))
                  for k in range(n)]
        for cp in copies:
            cp.start()
        for cp in copies:
            cp.wait()

    return pl.pallas_call(
        body, name=name,
        in_specs=[HBM_SPEC] * n, out_specs=[HBM_SPEC] * n,
        out_shape=[jax.ShapeDtypeStruct((g.shape[0], g.shape[1] // 2, g.shape[2]), g.dtype) for g in gs],
        scratch_shapes=[pltpu.SemaphoreType.DMA((n,)), pltpu.SemaphoreType.DMA((n,))],
        compiler_params=COMM_PARAMS,
    )(*gs)


HBM_ONLY = pl.BlockSpec(memory_space=pltpu.HBM)
SEM_SPEC = pl.BlockSpec(memory_space=pltpu.SEMAPHORE)
SPLIT_PARAMS = pltpu.CompilerParams(has_side_effects=pltpu.SideEffectType.DATAFLOW_SIDE_EFFECTING)


def _scatter_copies(srcs, lands, send_sems, recv_sems):
    x, y, c = _mesh_pos()
    me = 2 * x + y
    out = []
    for k in range(len(srcs)):
        for j, (px, py) in enumerate(_other_chips(x, y)):
            s = 2 * px + py
            send = _remote(srcs[k].at[s], lands[k].at[me], send_sems, recv_sems, 3 * k + j, (px, py, c))
            recv = _remote(srcs[k].at[s], lands[k].at[s], send_sems, recv_sems, 3 * k + j, (px, py, c))
            out.append((send, recv))
    return out


def _exchange_copies(srcs, lands, send_sems, recv_sems):
    x, y, c = _mesh_pos()
    out = []
    for k in range(len(srcs)):
        cp = _remote(srcs[k].at[:, _half_rows(srcs[k].shape[1], 1 - c)], lands[k], send_sems, recv_sems, k, (x, y, 1 - c))
        out.append((cp, cp))
    return out


def _split_start(srcs, land_shapes, copies, n_sems, *, name):
    n = len(srcs)
    lands = [lax.empty(shape, s.dtype) for shape, s in zip(land_shapes, srcs)]

    def body(*refs):
        ins, zones = refs[:n], refs[n:2 * n]
        send_sems, recv_sems, token = refs[2 * n], refs[2 * n + 1], refs[-1]
        for send, _ in copies(ins, zones, send_sems, recv_sems):
            send.start()
        token[...] = jnp.zeros_like(token)

    hbm = lambda a: pltpu.HBM(a.shape, a.dtype)
    res = pl.pallas_call(
        body, name=name,
        in_specs=[HBM_ONLY] * (2 * n),
        out_specs=[SEM_SPEC, SEM_SPEC] + [HBM_ONLY] * (2 * n) + [pl.BlockSpec(memory_space=pltpu.VMEM)],
        out_shape=[pltpu.SemaphoreType.DMA((n_sems,)), pltpu.SemaphoreType.DMA((n_sems,))] + [hbm(a) for a in srcs + lands]
        + [jax.ShapeDtypeStruct((1, 1), F32)],
        input_output_aliases={i: 2 + i for i in range(2 * n)},
        compiler_params=SPLIT_PARAMS,
    )(*[pltpu.with_memory_space_constraint(a, pltpu.HBM) for a in srcs + lands])
    return (res[0], res[1], list(res[2:2 + n]), list(res[2 + n:2 + 2 * n])), res[-1]


def _scatter_start(ps, *, name):
    return _split_start(ps, [p.shape for p in ps], _scatter_copies, 3 * len(ps), name=name)


def _exchange_start(gs, *, name):
    return _split_start(gs, [(g.shape[0], g.shape[1] // 2, g.shape[2]) for g in gs], _exchange_copies, len(gs), name=name)


def _split_wait(started, copies, after, *, name):
    ng = len(started)
    sizes = [len(st[2]) for st in started]
    offs = [2 * sum(sizes[:i]) for i in range(ng + 1)]
    flat = [a for (_, _, ps, lands) in started for a in ps + lands]

    def body(*refs):
        bufs, sems = refs[:len(flat)], refs[len(flat):len(flat) + 2 * ng]
        for i, n in enumerate(sizes):
            srcs, zones = bufs[offs[i]:offs[i] + n], bufs[offs[i] + n:offs[i + 1]]
            for send, recv in copies(srcs, zones, sems[2 * i], sems[2 * i + 1]):
                send.wait_send()
                recv.wait_recv()

    res = pl.pallas_call(
        body, name=name,
        in_specs=[HBM_ONLY] * len(flat) + [SEM_SPEC] * (2 * ng) + [HBM_SPEC],
        out_specs=[HBM_ONLY] * len(flat),
        out_shape=[pltpu.HBM(a.shape, a.dtype) for a in flat],
        input_output_aliases={i: i for i in range(len(flat))},
        compiler_params=SPLIT_PARAMS,
    )(*flat, *[s for (ss, rs, _, _) in started for s in (ss, rs)], after)
    return [(list(res[offs[i]:offs[i] + n]), list(res[offs[i] + n:offs[i + 1]])) for i, n in enumerate(sizes)]


def _sibling_share(hs):
    n = len(hs)

    def body(*refs):
        ins, outs = refs[:n], refs[n:2 * n]
        send_sems, recv_sems = refs[2 * n:]
        x, y, c = _mesh_pos()
        copies = [_remote(ins[k], outs[k], send_sems, recv_sems, k, (x, y, 1 - c)) for k in range(n)]
        for cp in copies:
            cp.start()
        for cp in copies:
            cp.wait()

    return pl.pallas_call(
        body, name="grad_sibling_share",
        in_specs=[HBM_SPEC] * n, out_specs=[HBM_SPEC] * n,
        out_shape=[jax.ShapeDtypeStruct(h.shape, h.dtype) for h in hs],
        scratch_shapes=[pltpu.SemaphoreType.DMA((n,)), pltpu.SemaphoreType.DMA((n,))],
        compiler_params=COMM_PARAMS,
    )(*hs)


def _allreduce_small(part, by_chip):
    rows, C = part.shape
    rows2 = by_chip.shape[1]

    def body(p_ref, q_ref, o_ref, o2_ref, slots, slots2, send_sems, recv_sems):
        x, y, c = _mesh_pos()
        me = 4 * x + 2 * y + c
        slots[me] = p_ref[...]
        slots2[me] = q_ref[2 * x + y]
        copies = []
        for k in range(1, 8):
            kx, ky, kc = (k >> 2) & 1, (k >> 1) & 1, k & 1
            peer = (x ^ kx if kx else x, y ^ ky if ky else y, c ^ kc if kc else c)
            src = 4 * peer[0] + 2 * peer[1] + peer[2]
            pair = []
            for j, (mine, zone, lands) in enumerate(((p_ref, slots.at[me], slots.at[src]),
                                                     (q_ref.at[2 * peer[0] + peer[1]], slots2.at[me], slots2.at[src]))):
                cp = _remote(mine, zone, send_sems, recv_sems, 2 * (k - 1) + j, peer)
                cp.start()
                pair.append((cp, _remote(mine, lands, send_sems, recv_sems, 2 * (k - 1) + j, peer)))
            copies += pair
        for _, landing in copies:
            landing.wait_recv()
        for cp, _ in copies:
            cp.wait_send()
        total, total2 = slots[0], slots2[0]
        for d in range(1, 8):
            total, total2 = total + slots[d], total2 + slots2[d]
        o_ref[...] = total
        o2_ref[...] = total2

    vmem = pl.BlockSpec(memory_space=pltpu.VMEM)
    return pl.pallas_call(
        body, name="small_grad_allreduce",
        in_specs=[vmem, vmem], out_specs=[vmem, vmem],
        out_shape=[jax.ShapeDtypeStruct((rows, C), F32), jax.ShapeDtypeStruct((rows2, C), F32)],
        scratch_shapes=[pltpu.VMEM((8, rows, C), F32), pltpu.VMEM((8, rows2, C), F32),
                        pltpu.SemaphoreType.DMA((14,)), pltpu.SemaphoreType.DMA((14,))],
        compiler_params=pltpu.CompilerParams(has_side_effects=True, vmem_limit_bytes=VMEM_LIMIT_BYTES),
    )(part, by_chip)


def _pad_w_uq(w):
    lead = w.shape[:-1]
    w = w.reshape(lead + (MLA_HEADS, MLA_QK))
    w = jnp.concatenate([w, jnp.zeros(lead + (MLA_HEADS, MLA_PAD - MLA_QK), w.dtype)], axis=-1)
    return w.reshape(lead + (MLA_HEADS * MLA_PAD,))


def _unpad_w_uq(g):
    lead = g.shape[:-1]
    return g.reshape(lead + (MLA_HEADS, MLA_PAD))[..., :MLA_QK].reshape(lead + (MLA_HEADS * MLA_QK,))


def _t(a):
    return jnp.swapaxes(a, -1, -2)


def _shards_of_cols(w):
    A, NB = w.shape
    return w.reshape(A, N_CHIPS, NB // N_CHIPS).transpose(1, 0, 2)


BIG = ("w_in", "w_uq", "w_ukv", "w_out", "w_up", "w_down")
SMALL = ("attn_pre_norm", "forget_bias", "swa_sinks", "rel_bias", "q_latent_norm", "kv_latent_norm", "group_norm",
         "attn_post_norm", "ffn_pre_norm", "conv_b", "ffn_post_norm")
WEIGHTS = ("attn_pre_norm", "w_in", "forget_bias", "swa_sinks", "rel_bias", "q_latent_norm", "w_uq", "kv_latent_norm",
           "w_ukv", "group_norm", "w_out", "attn_post_norm", "ffn_pre_norm", "w_up", "conv_w", "conv_b", "w_down",
           "ffn_post_norm")


PACK_UNIT = 8 * LANES


def _pack_rows(shape):
    return -(-int(np.prod(shape)) // PACK_UNIT) * 8


def _pack(arrs, row_mult=8):
    parts = []
    for a in arrs:
        n = int(np.prod(a.shape))
        parts.append(jnp.pad(a.reshape(-1), (0, _pack_rows(a.shape) * LANES - n)).reshape(-1, LANES))
    rows = sum(p.shape[0] for p in parts)
    pad = -rows % row_mult
    if pad:
        parts.append(jnp.zeros((pad, LANES), parts[0].dtype))
    return jnp.concatenate(parts, axis=0)


def _unpack(packed, shapes):
    packed = packed.reshape(-1, LANES)
    out, off = [], 0
    for shp in shapes:
        r = _pack_rows(shp)
        out.append(packed[off:off + r].reshape(-1)[:int(np.prod(shp))].reshape(shp))
        off += r
    return out


LAYER_KEYS = ("w_qkv_t", "w_lat_t", "w_in_t", "w_uq_p", "w_uq_t", "w_ukv", "w_ukv_t", "w_out", "w_up", "w_down", "conv_w")


MIX_WEIGHTS = ("w_in", "w_uq", "w_ukv", "w_out")
FFN_WEIGHTS = ("w_up", "w_down", "conv_w")


def _layer_weights(gathered):
    cols = lambda g: g.transpose(1, 0, 2).reshape(g.shape[1], N_CHIPS * g.shape[2])
    out = {}
    if "w_in" in gathered:
        w_in_t = _t(gathered["w_in"]).reshape(IN_COLS, D_MODEL)
        w_in_t = jnp.pad(w_in_t, ((0, IN_ROWS - IN_COLS), (0, 0)))
        w_uq_p = _pad_w_uq(cols(gathered["w_uq"]))
        w_ukv = cols(gathered["w_ukv"])
        out.update(w_qkv_t=w_in_t[:QKV_ROWS], w_lat_t=w_in_t[QKV_ROWS:], w_in_t=w_in_t, w_uq_p=w_uq_p, w_uq_t=_t(w_uq_p),
                   w_ukv=w_ukv, w_ukv_t=_t(w_ukv), w_out=gathered["w_out"].reshape(D_MODEL, D_MODEL))
    if "w_up" in gathered:
        out.update(w_up=gathered["w_up"], w_down=gathered["w_down"].reshape(D_FF, D_MODEL), conv_w=cols(gathered["conv_w"]))
    return out


def _local_step(x, target, W, layer_weights, layer_done):
    W = dict(W, **{key: [None] * DEPTH for key in LAYER_KEYS})
    S = x.shape[0]
    tq_tabs, tm_tabs = _rope_tables(S)
    onehot_t = _rel_onehot_t()
    bias_t = _bias_table(W["rel_bias"].T, onehot_t).reshape(SWA_KV_HEADS, SWA_GROUP, 2 * WINDOW, WINDOW)
    bias_t = bias_t.transpose(0, 2, 1, 3).reshape(SWA_KV_HEADS, 2 * WINDOW, GW)
    row = lambda a: a.reshape(1, -1)
    col = lambda a: a.reshape(-1, 1)
    fox_rows = (FOX_ROW0, FOX_ROW0 + FOX_HEADS * HEAD_DIM, FOX_ROW0 + 2 * FOX_HEADS * HEAD_DIM, SWA_Q_HEADS)
    fox = dict(rows=fox_rows, H=FOX_HEADS, Dk=HEAD_DIM, Dv=HEAD_DIM, scale=HEAD_DIM ** -0.5)
    mla = dict(rows=(0, 0, 0, SWA_Q_HEADS + FOX_HEADS), H=MLA_HEADS, Dk=MLA_PAD, Dv=HEAD_DIM, scale=MLA_SCALE, q_scaled=True)

    saved = []
    h = _rms_fwd(x, row(W["attn_pre_norm"][0]), name="rms_in")
    for l in range(DEPTH):
        sv = {"x0": x, "h1": h}
        for key, val in layer_weights(l, h, False).items():
            W[key][l] = val
        qkv = _matmul(W["w_qkv_t"][l], h, tb=True, out_dtype=BF16, name="proj_qkv")
        lat = _matmul(W["w_lat_t"][l], h, tb=True, name="proj_lat")
        oa, lse_a = _swa_fwd(qkv, bias_t, W["swa_sinks"][l], name="swa_fwd")
        fb_col = jnp.pad(col(W["forget_bias"][l]), ((0, GATE_ROWS - FOX_HEADS), (0, 0)))
        f4 = _gate_fwd(lat, fb_col, name="fox_gate_fwd")[:FOX_HEADS]
        f2 = f4 * LOG2E
        f_row, f_col = f2[:, None, :], f2.T
        of, lse_f = _attn_fwd(qkv, qkv, qkv, f_row=f_row, f_col=f_col, name="fox_fwd", **fox)
        nq, nkv, qm, km, vm = _mla_prep_fwd(lat, col(W["q_latent_norm"][l]), col(W["kv_latent_norm"][l]), W["w_uq_t"][l],
                                            W["w_ukv_t"][l], tq_tabs, tm_tabs, name="mla_prep_fwd")
        oc, lse_c = _attn_fwd(qm, km, vm, name="mla_fwd", **mla)
        mixed = _group_norm_fwd(oa, of, oc, col(W["group_norm"][l]), name="group_norm_fwd")
        y, x1, h2 = _matmul(mixed, W["w_out"][l], ta=True, name="proj_out",
                            resid_rms=(x, row(W["attn_post_norm"][l]), row(W["ffn_pre_norm"][l])))
        for key, val in layer_weights(l, h2, True).items():
            W[key][l] = val
        a = _matmul(h2, W["w_up"][l], b_shards=True, out_dtype=BF16, name="ffn_up")
        u, z = _conv_geglu_fwd(a, W["conv_w"][l], row(W["conv_b"][l]), name="conv_geglu_fwd")
        g_next = row(W["attn_pre_norm"][l + 1]) if l + 1 < DEPTH else None
        y2, x2, *h_next = _matmul(z, W["w_down"][l], name="ffn_down", resid_rms=(x1, row(W["ffn_post_norm"][l]), g_next))
        h_next = h_next[0] if h_next else None
        sv.update(qkv=qkv, lat=lat, oa=oa, lse_a=lse_a, fb_col=fb_col, f_row=f_row, f_col=f_col, of=of, lse_f=lse_f,
                  nq=nq, nkv=nkv, qm=qm, km=km, vm=vm, oc=oc, lse_c=lse_c, mixed=mixed, y=y, x1=x1, h2=h2, a=a, u=u, z=z, y2=y2)
        saved.append(sv)
        x, h = x2, h_next

    loss, dx = _loss_head(x, target)

    G = {k: [None] * DEPTH for k in WEIGHTS if k != "rel_bias" and k not in BIG}
    dbias_layers = [None] * DEPTH
    for l in reversed(range(DEPTH)):
        sv = saved[l]
        gb = {}
        if l == DEPTH - 1:
            dy2, dg = _rms_bwd(sv["y2"], row(W["ffn_post_norm"][l]), dx, out_dtype=BF16, name="ffn_post_bwd")
            G["ffn_post_norm"][l] = dg[0]
        dz = _matmul(dy2, W["w_down"][l], tb=True, name="ffn_down_dx")
        gb["w_down"] = _matmul(sv["z"], dy2, ta=True, name="ffn_down_dw").reshape(N_CHIPS, D_FF // N_CHIPS, D_MODEL)
        da, dcw, dcb = _conv_geglu_bwd(sv["a"], sv["u"], W["conv_w"][l], dz, name="conv_geglu_bwd")
        G["conv_w"][l] = dcw.transpose(1, 0, 2).reshape(3, 2 * D_FF)
        G["conv_b"][l] = dcb.reshape(2 * D_FF)
        gb["w_up"] = _matmul(sv["h2"], da, ta=True, out_shards=True, b_halves=True, name="ffn_up_dw")
        token = layer_done(l, gb)
        gb = {}
        dx1, dg, dy, dg_post = _matmul(
            da, W["w_up"][l], tb=True, b_shards=True, a_halves=True, name="ffn_up_dx",
            norm_bwd=(sv["x1"], row(W["ffn_pre_norm"][l]) + token, dx, (sv["y"], row(W["attn_post_norm"][l]))))
        G["ffn_pre_norm"][l] = dg[0]
        G["attn_post_norm"][l] = dg_post[0]
        dmixed = _matmul(W["w_out"][l], dy, tb=True, name="proj_out_dx")
        gb["w_out"] = _matmul(sv["mixed"], dy, name="proj_out_dw").reshape(N_CHIPS, D_MODEL // N_CHIPS, D_MODEL)
        doa, dof, doc, dg, delta = _group_norm_bwd(sv["oa"], sv["of"], sv["oc"], col(W["group_norm"][l]), dmixed,
                                                   name="group_norm_bwd")
        G["group_norm"][l] = dg[:, 0]
        dqa, dkva, dbias_l, dsink = _swa_bwd(sv["qkv"], bias_t, W["swa_sinks"][l], doa, sv["lse_a"],
                                             delta.reshape(-1, S), name="swa_bwd")
        dbias_layers[l] = (dbias_l.reshape(SWA_KV_HEADS, 2 * WINDOW, SWA_GROUP, WINDOW).transpose(0, 2, 1, 3)
                           .reshape(SWA_Q_HEADS, -1))
        G["swa_sinks"][l] = dsink[:, 0]
        dqf, dkf, dvf, dfk = _attn_bwd(sv["qkv"], sv["qkv"], sv["qkv"], do=dof, lse=sv["lse_f"], delta=delta,
                                       f_row=sv["f_row"], f_col=sv["f_col"], name="fox_bwd", **fox)
        dF = jnp.pad(dfk.T, ((0, GATE_ROWS - FOX_HEADS), (0, 0)))
        dflog, dfb = _gate_bwd(sv["lat"], sv["fb_col"], dF, name="fox_gate_bwd")
        G["forget_bias"][l] = dfb[:FOX_HEADS, 0]
        dqm, dkm, dvm = _attn_bwd(sv["qm"], sv["km"], sv["vm"], do=doc, lse=sv["lse_c"], delta=delta, name="mla_bwd", **mla)
        dlat, dwq_t, dwkv_t, dgq, dgkv = _mla_prep_bwd(
            sv["lat"], sv["nq"], sv["nkv"], col(W["q_latent_norm"][l]), col(W["kv_latent_norm"][l]), W["w_uq_p"][l],
            W["w_ukv"][l], tq_tabs, tm_tabs, dqm, dkm, dvm, dflog, name="mla_prep_bwd")
        gb["w_uq"], gb["w_ukv"] = _shards_of_cols(_unpad_w_uq(dwq_t.T)), _shards_of_cols(dwkv_t.T)
        G["q_latent_norm"][l], G["kv_latent_norm"][l] = dgq[:, 0], dgkv[:, 0]
        dproj = _dproj_cast(dqa, dkva, dqf, dkf, dvf, dlat, name="dproj_cast")
        dw_in_t = _matmul(dproj, sv["h1"], name="proj_in_dw")
        gb["w_in"] = _t(dw_in_t[:IN_COLS].reshape(N_CHIPS, IN_COLS // N_CHIPS, D_MODEL))
        token = layer_done(l, gb)
        below = (saved[l - 1]["y2"], row(W["ffn_post_norm"][l - 1])) if l > 0 else None
        res = _matmul(dproj, W["w_in_t"][l], ta=True, name="proj_in_dx",
                      norm_bwd=(sv["x0"], row(W["attn_pre_norm"][l]) + token, dx1, below))
        dx, G["attn_pre_norm"][l] = res[0], res[1][0]
        if l > 0:
            dy2, G["ffn_post_norm"][l - 1] = res[2], res[3][0]

    grads = {k: jnp.stack(v) for k, v in G.items()}
    grads["rel_bias"] = _bias_table_bwd(jnp.stack(dbias_layers), onehot_t).T
    return loss, dx, grads


def kernel(x, attn_pre_norm, w_in, forget_bias, swa_sinks, rel_bias, q_latent_norm, w_uq, kv_latent_norm, w_ukv, group_norm, w_out, attn_post_norm, ffn_pre_norm, w_up, conv_w, conv_b, w_down, ffn_post_norm, loss_target, m_attn_pre_norm, m_w_in, m_forget_bias, m_swa_sinks, m_rel_bias, m_q_latent_norm, m_w_uq, m_kv_latent_norm, m_w_ukv, m_group_norm, m_w_out, m_attn_post_norm, m_ffn_pre_norm, m_w_up, m_conv_w, m_conv_b, m_w_down, m_ffn_post_norm, v_attn_pre_norm, v_w_in, v_forget_bias, v_swa_sinks, v_rel_bias, v_q_latent_norm, v_w_uq, v_kv_latent_norm, v_w_ukv, v_group_norm, v_w_out, v_attn_post_norm, v_ffn_pre_norm, v_w_up, v_conv_w, v_conv_b, v_w_down, v_ffn_post_norm):
    args = dict(locals())
    w = {k: args[k] for k in WEIGHTS}
    m = {k: args["m_" + k] for k in WEIGHTS}
    v = {k: args["v_" + k] for k in WEIGHTS}

    block = lambda l, keys: [w[k][l] if k == "conv_w" else w[k][l].astype(BF16) for k in keys]
    units = [(0, MIX_WEIGHTS), (0, FFN_WEIGHTS)] + [(l, MIX_WEIGHTS + FFN_WEIGHTS) for l in range(1, DEPTH)]
    gather_state, token = _gather_start([block(l, keys) for l, keys in units])
    W = {k: w[k] for k in SMALL}
    W["attn_pre_norm"] = W["attn_pre_norm"] + token

    def layer_weights(l, after, for_ffn):
        if for_ffn and l > 0:
            return {}
        keys = FFN_WEIGHTS if for_ffn else (MIX_WEIGHTS if l == 0 else MIX_WEIGHTS + FFN_WEIGHTS)
        tag = f"{l}_{keys[0]}"
        srcs, lands = _gather_wait(gather_state[units.index((l, keys))], after, name="weight_gather_wait_" + tag)
        lands = _gather_forward(lands, name="weight_gather_forward_" + tag)
        lands = _place_own(lands, srcs, name="place_own_shards")
        return _layer_weights(dict(zip(keys, lands)))

    started, groups, pending = [], [], []

    def to_chips(l, keys, gs, recv, tag):
        pair = [_pair_sum(gk, rk, name="grad_pair_sum") for gk, rk in zip(gs, recv)]
        state, token = _scatter_start(pair, name="grad_scatter_start_" + tag)
        started.append(state)
        groups.append((l, keys))
        return token

    def finish_pending(after):
        l, keys, tag, state = pending.pop()
        gs, recv = _split_wait([state], _exchange_copies, after, name="grad_exchange_wait_" + tag)[0]
        return to_chips(l, keys, gs, recv, tag)

    def layer_done(l, gb):
        keys = [k for k in BIG if k in gb]
        gs = [gb[k] for k in keys]
        tag = f"{l}_{keys[0]}"
        token = finish_pending(gs[0]) if pending else 0.0
        if l == 0:
            return token + to_chips(l, keys, gs, _sibling_exchange(gs, name="grad_sibling_exchange_" + tag), tag)
        state, started_token = _exchange_start(gs, name="grad_exchange_start_" + tag)
        pending.append((l, keys, tag, state))
        return token + started_token

    loss_part, dx, g = _local_step(x[0], loss_target[0], W, layer_weights, layer_done)
    loss = lax.psum(loss_part, ("x", "y", "c"))

    reduced = {}
    for (l, keys), (pair, zones) in zip(groups, _split_wait(started, _scatter_copies, dx, name="grad_scatter_wait")):
        for k, p, z in zip(keys, pair, zones):
            reduced[k, l] = _chip_sum(z, p, name="grad_chip_sum")
    mine = [jnp.stack([reduced[k, l] for l in range(DEPTH)]) for k in BIG]
    other = _sibling_share(mine)
    out_g, out_d, out_m, out_v = {}, {}, {}, {}
    for k, g_mine, g_other in zip(BIG, mine, other):
        out_g[k], out_d[k], out_m[k], out_v[k] = _adamw_halves(w[k], g_mine, g_other, m[k], v[k], name="adamw_" + k)

    small_shapes = [w[k].shape for k in SMALL]
    taps_by_chip = g["conv_w"].reshape(DEPTH, 3, N_CHIPS, FF_SHARD).transpose(2, 0, 1, 3)
    reduced, taps = _allreduce_small(_pack([g[k] for k in SMALL]), jnp.stack([_pack([t]) for t in taps_by_chip]))
    g_small = _unpack(reduced, small_shapes) + _unpack(taps, [w["conv_w"].shape])
    names = SMALL + ("conv_w",)
    shapes = small_shapes + [w["conv_w"].shape]
    packed = lambda arrs: _pack(arrs, ROW_TILE)[None]
    d_s, m_s, v_s = _adamw(packed([w[k] for k in names]), packed(g_small), packed([m[k] for k in names]),
                           packed([v[k] for k in names]), name="adamw_small")
    out_g.update(zip(names, g_small))
    out_d.update(zip(names, _unpack(d_s, shapes)))
    out_m.update(zip(names, _unpack(m_s, shapes)))
    out_v.update(zip(names, _unpack(v_s, shapes)))

    return (loss, dx[None], *[out_g[k] for k in WEIGHTS], *[out_d[k] for k in WEIGHTS],
            *[out_m[k] for k in WEIGHTS], *[out_v[k] for k in WEIGHTS])
```

```python
import math

import numpy as np
import jax
import jax.numpy as jnp
from jax import lax
from jax.experimental import pallas as pl
from jax.experimental.pallas import tpu as pltpu

F32 = jnp.float32
BF16 = jnp.bfloat16

D_MODEL = 1024
DEPTH = 4
HEAD_DIM = 64
SWA_Q_HEADS = 8
SWA_KV_HEADS = 2
SWA_GROUP = SWA_Q_HEADS // SWA_KV_HEADS
WINDOW = 128
FOX_HEADS = 4
MLA_HEADS = 4
MLA_Q_RANK = 256
MLA_KV_RANK = 128
MLA_NOPE = 64
MLA_ROPE = 32
MLA_QK = MLA_NOPE + MLA_ROPE
ROPE_THETA = 10000.0
REL_BUCKETS = 32
REL_MAX_DIST = 128
D_FF = 2816
EPS = 1e-6
NEG_INF = -1e30
LANES = 128
N_CHIPS = 4

IN_COLS = 1956
IN_ROWS = 2048
QKV_ROWS = 1536
LAT_ROWS = IN_ROWS - QKV_ROWS
LAT_SHIFT = FOX_HEADS
FOX_ROW0 = 768
MLA_PAD = LANES
GATE_ROWS = 8

ADAM_LR = 0.001
ADAM_B1 = 0.9
ADAM_B2 = 0.999
ADAM_EPS = 1e-08
ADAM_WD = 0.01
ADAM_STEP = 10

VMEM_LIMIT_BYTES = 48 * 1024 * 1024
ATT_TILE = 512
LOG2E = math.log2(math.e)
MLA_SCALE = MLA_QK ** -0.5
ROW_TILE = 256
MESH = pl.DeviceIdType.MESH

NT = (((1,), (1,)), ((), ()))
TN = (((0,), (0,)), ((), ()))
NN = (((1,), (0,)), ((), ()))


SMALL_VMEM_LIMIT_BYTES = 60 * 1024 * 1024


def _params(*sem, vmem_bytes=SMALL_VMEM_LIMIT_BYTES):
    return pltpu.CompilerParams(dimension_semantics=sem, vmem_limit_bytes=vmem_bytes)


def _tile(dim, cap):
    for t in (2816, 2048, 1408, 1024, 512, 256, 128, 64, 32, 16, 8):
        if t <= cap and dim % t == 0:
            return t
    return dim


def _dot(a, b, dims=NN):
    return lax.dot_general(a, b, dims, preferred_element_type=F32)


def _split3(a):
    a1 = a.astype(BF16)
    r1 = a - a1.astype(F32)
    a2 = r1.astype(BF16)
    a3 = (r1 - a2.astype(F32)).astype(BF16)
    return a1, a2, a3


FF_SHARD = 2 * D_FF // N_CHIPS
MATMUL_VMEM_BYTES = 40 * 1024 * 1024
TAIL_ROWS = 512
TAIL_VMEM_LIMIT_BYTES = 56 * 1024 * 1024


def _matmul(a, b, *, ta=False, tb=False, out_dtype=F32, name, b_shards=False, out_shards=False, a_halves=False,
            b_halves=False, norm_bwd=None, resid_rms=None):
    if a_halves:
        M, K = a.shape[1], 2 * a.shape[2]
    elif ta:
        K, M = a.shape
    else:
        M, K = a.shape
    if b_halves:
        K2, N = b.shape[1], 2 * b.shape[2]
    elif b_shards:
        K2, N = (2 * D_FF, D_MODEL) if tb else (D_MODEL, 2 * D_FF)
    elif tb:
        N, K2 = b.shape
    else:
        K2, N = b.shape
    assert K == K2, (a.shape, b.shape)
    tn = _tile(N, 1408)
    tk = FF_SHARD if (b_shards and tb) else _tile(K, 2816)
    out_bytes = jnp.dtype(out_dtype).itemsize
    with_tail = norm_bwd is not None or resid_rms is not None
    tile_bytes = 4 + (2 * (4 * 4 + 2) if with_tail else 2 * out_bytes)
    vmem = lambda tm, tk: 2 * 2 * tk * (tm + tn) + tile_bytes * tm * tn
    tm = M if M <= 2048 else _tile(M, 1408)
    if M > 2048 and M % 2048 == 0 and tk == K and vmem(2048, tk) <= MATMUL_VMEM_BYTES:
        tm = 2048
    if with_tail:
        assert tn == N and not out_shards and (norm_bwd is None or resid_rms is None)
        tm = TAIL_ROWS
    while vmem(tm, tk) > MATMUL_VMEM_BYTES and tk % 256 == 0:
        tk //= 2
    nk = K // tk
    dims = (((0 if ta else 1,), (1 if tb else 0,)), ((), ()))
    if with_tail:
        if norm_bwd is not None:
            x, g, resid, then = norm_bwd
            chained = then is not None
            tail_in, in_kinds = [x, g, resid] + (list(then) if chained else []), "rvr" + ("rv" if chained else "")
            out_kinds, out_dtypes = "rv" + ("rv" if chained else ""), [F32, F32] + ([BF16, F32] if chained else [])

            def tail(dy, ins, outs):
                dx, dg = _seg_rms_bwd(ins[0][...], ins[1][...], dy)
                dx = dx + ins[2][...]
                outs[0][...] = dx
                outs[1][...] += dg
                if chained:
                    dx2, dg2 = _seg_rms_bwd(ins[3][...], ins[4][...], dx)
                    outs[2][...] = dx2.astype(BF16)
                    outs[3][...] += dg2
        else:
            x, g_post, g_next = resid_rms
            with_next = g_next is not None
            tail_in, in_kinds = [x, g_post] + ([g_next] if with_next else []), "rv" + ("v" if with_next else "")
            out_kinds, out_dtypes = "rr" + ("r" if with_next else ""), [F32, F32] + ([BF16] if with_next else [])

            def tail(y, ins, outs):
                outs[0][...] = y
                xn = ins[0][...] + _seg_rms(y, ins[1][...])
                outs[1][...] = xn
                if with_next:
                    outs[2][...] = _seg_rms(xn, ins[2][...]).astype(BF16)

        def fused(a_ref, b_ref, *refs):
            ins, outs, acc = refs[:len(tail_in)], refs[len(tail_in):-1], refs[-1]
            k, i = pl.program_id(0), pl.program_id(1)
            acc_ref = acc.at[pl.ds(pl.multiple_of(i * tm, tm), tm), :] if nk > 1 else acc

            @pl.when(k == 0)
            def _():
                acc_ref[...] = jnp.zeros((tm, N), F32)

            acc_ref[...] += lax.dot_general(a_ref[...], b_ref[...], dims, preferred_element_type=F32)

            @pl.when((k == nk - 1) & (i == 0))
            def _():
                for o, kind in zip(outs, out_kinds):
                    if kind == "v":
                        o[...] = jnp.zeros_like(o)

            @pl.when(k == nk - 1)
            def _():
                tail(acc_ref[...], ins, outs)

    def body(a_ref, b_ref, o_ref, acc_ref):
        k = pl.program_id(2)

        @pl.when(k == 0)
        def _():
            acc_ref[...] = jnp.zeros_like(acc_ref)

        acc_ref[...] += lax.dot_general(a_ref[...], b_ref[...], dims, preferred_element_type=F32)

        @pl.when(k == nk - 1)
        def _():
            o_ref[...] = acc_ref[...].astype(o_ref.dtype)

    if a_halves:
        nh = K // 2 // tk
        a_spec = pl.BlockSpec((None, tm, tk), lambda i, j, k: (k // nh, i, k % nh))
    else:
        a_spec = pl.BlockSpec((tk, tm), lambda i, j, k: (k, i)) if ta else pl.BlockSpec((tm, tk), lambda i, j, k: (i, k))
    if b_halves:
        nh = N // 2 // tn
        b_spec = pl.BlockSpec((None, tk, tn), lambda i, j, k: (j // nh, k, j % nh))
    elif b_shards and tb:
        assert tk == FF_SHARD
        b_spec = pl.BlockSpec((None, tn, tk), lambda i, j, k: (k, j, 0))
    elif b_shards:
        assert tn == FF_SHARD
        b_spec = pl.BlockSpec((None, tk, tn), lambda i, j, k: (j, k, 0))
    else:
        b_spec = pl.BlockSpec((tn, tk), lambda i, j, k: (j, k)) if tb else pl.BlockSpec((tk, tn), lambda i, j, k: (k, j))
    if out_shards:
        assert tn == FF_SHARD
        out_spec = pl.BlockSpec((None, tm, tn), lambda i, j, k: (j, i, 0))
        out_shape = jax.ShapeDtypeStruct((N // tn, M, tn), out_dtype)
    else:
        out_spec = pl.BlockSpec((tm, tn), lambda i, j, k: (i, j))
        out_shape = jax.ShapeDtypeStruct((M, N), out_dtype)
    if with_tail:
        spec = {"r": pl.BlockSpec((tm, N), lambda k, i: (jnp.where(k == nk - 1, i, 0), 0)),
                "v": pl.BlockSpec((1, N), lambda k, i: (0, 0))}
        a_map, b_map = a_spec.index_map, b_spec.index_map
        return pl.pallas_call(
            fused, name=name, grid=(nk, M // tm),
            in_specs=[pl.BlockSpec(a_spec.block_shape, lambda k, i: a_map(i, 0, k)),
                      pl.BlockSpec(b_spec.block_shape, lambda k, i: b_map(i, 0, k))] + [spec[c] for c in in_kinds],
            out_specs=[spec[c] for c in out_kinds],
            out_shape=[jax.ShapeDtypeStruct((M if c == "r" else 1, N), d) for c, d in zip(out_kinds, out_dtypes)],
            scratch_shapes=[pltpu.VMEM((M if nk > 1 else tm, N), F32)],
            compiler_params=pltpu.CompilerParams(dimension_semantics=("arbitrary", "arbitrary"),
                                                 vmem_limit_bytes=TAIL_VMEM_LIMIT_BYTES if nk > 1 else VMEM_LIMIT_BYTES),
        )(a, b, *tail_in)
    return pl.pallas_call(
        body, name=name, grid=(M // tm, N // tn, nk),
        in_specs=[a_spec, b_spec], out_specs=out_spec, out_shape=out_shape,
        scratch_shapes=[pltpu.VMEM((tm, tn), F32)],
        compiler_params=_params("parallel", "parallel", "arbitrary", vmem_bytes=VMEM_LIMIT_BYTES),
    )(a, b)


def _seg_rms(xs, g):
    r = lax.rsqrt(jnp.mean(xs * xs, axis=-1, keepdims=True) + EPS)
    return xs * r * g


def _seg_rms_bwd(xs, g, dy):
    r = lax.rsqrt(jnp.mean(xs * xs, axis=-1, keepdims=True) + EPS)
    gd = dy * g
    c = jnp.mean(gd * xs, axis=-1, keepdims=True)
    dx = r * gd - xs * (r * r * r * c)
    dg = jnp.sum(dy * (xs * r), axis=0, keepdims=True)
    return dx, dg


def _rms_fwd(x, g, *, name):
    S, W = x.shape
    tm = _tile(S, 512)

    def body(x_ref, g_ref, o_ref):
        o_ref[...] = _seg_rms(x_ref[...], g_ref[...]).astype(o_ref.dtype)

    return pl.pallas_call(
        body, name=name, grid=(S // tm,),
        in_specs=[pl.BlockSpec((tm, W), lambda i: (i, 0)), pl.BlockSpec((1, W), lambda i: (0, 0))],
        out_specs=pl.BlockSpec((tm, W), lambda i: (i, 0)),
        out_shape=jax.ShapeDtypeStruct((S, W), BF16),
        compiler_params=_params("parallel"),
    )(x, g)


def _rms_bwd(x, g, dy, *, out_dtype, name):
    S, W = x.shape
    tm = _tile(S, 512)

    def body(x_ref, g_ref, dy_ref, dx_ref, dg_ref):
        @pl.when(pl.program_id(0) == 0)
        def _():
            dg_ref[...] = jnp.zeros_like(dg_ref)

        dx, dg = _seg_rms_bwd(x_ref[...], g_ref[...], dy_ref[...])
        dx_ref[...] = dx.astype(dx_ref.dtype)
        dg_ref[...] += dg

    row = pl.BlockSpec((tm, W), lambda i: (i, 0))
    vec = pl.BlockSpec((1, W), lambda i: (0, 0))
    return pl.pallas_call(
        body, name=name, grid=(S // tm,),
        in_specs=[row, vec, row], out_specs=[row, vec],
        out_shape=[jax.ShapeDtypeStruct((S, W), out_dtype), jax.ShapeDtypeStruct((1, W), F32)],
        compiler_params=_params("arbitrary"),
    )(x, g, dy)


def _col_rms(xs, g):
    r = lax.rsqrt(jnp.mean(xs * xs, axis=0, keepdims=True) + EPS)
    return xs * r * g


def _col_rms_bwd(xs, g, dy):
    r = lax.rsqrt(jnp.mean(xs * xs, axis=0, keepdims=True) + EPS)
    gd = dy * g
    c = jnp.mean(gd * xs, axis=0, keepdims=True)
    dx = r * gd - xs * (r * r * r * c)
    dg = jnp.sum(dy * (xs * r), axis=1, keepdims=True)
    return dx, dg


GROUP_ROWS = (SWA_Q_HEADS * HEAD_DIM, FOX_HEADS * HEAD_DIM, MLA_HEADS * HEAD_DIM)


def _group_specs(S, tn):
    outs = [pl.BlockSpec((n, tn), lambda i: (0, i)) for n in GROUP_ROWS]
    g = pl.BlockSpec((D_MODEL, 1), lambda i: (0, 0))
    mixed = pl.BlockSpec((D_MODEL, tn), lambda i: (0, i))
    return outs, g, mixed


def _group_norm_fwd(oa, of, oc, g, *, name):
    S = oa.shape[1]
    tn = _tile(S, 512)
    outs, gs, mixed = _group_specs(S, tn)

    def body(a_ref, f_ref, c_ref, g_ref, o_ref):
        r0 = 0
        for ref, n in zip((a_ref, f_ref, c_ref), GROUP_ROWS):
            o_ref[r0:r0 + n, :] = _col_rms(ref[...], g_ref[r0:r0 + n, :]).astype(BF16)
            r0 += n

    return pl.pallas_call(
        body, name=name, grid=(S // tn,),
        in_specs=outs + [gs], out_specs=mixed,
        out_shape=jax.ShapeDtypeStruct((D_MODEL, S), BF16),
        compiler_params=_params("parallel"),
    )(oa, of, oc, g)


def _group_norm_bwd(oa, of, oc, g, dmixed, *, name):
    S = oa.shape[1]
    tn = _tile(S, 512)
    outs, gs, mixed = _group_specs(S, tn)
    n_heads = D_MODEL // HEAD_DIM

    def body(a_ref, f_ref, c_ref, g_ref, dm_ref, da_ref, df_ref, dc_ref, dg_ref, dl_ref):
        @pl.when(pl.program_id(0) == 0)
        def _():
            dg_ref[...] = jnp.zeros_like(dg_ref)

        r0 = 0
        for ref, dref, n in zip((a_ref, f_ref, c_ref), (da_ref, df_ref, dc_ref), GROUP_ROWS):
            o = ref[...]
            dx, dg = _col_rms_bwd(o, g_ref[r0:r0 + n, :], dm_ref[r0:r0 + n, :])
            dxb = dx.astype(BF16)
            dref[...] = dxb
            dg_ref[r0:r0 + n, :] += dg
            od = o * dxb.astype(F32)
            for h in range(n // HEAD_DIM):
                dl_ref[r0 // HEAD_DIM + h] = jnp.sum(od[h * HEAD_DIM:(h + 1) * HEAD_DIM, :], axis=0, keepdims=True)
            r0 += n

    return pl.pallas_call(
        body, name=name, grid=(S // tn,),
        in_specs=outs + [gs, mixed], out_specs=outs + [gs, pl.BlockSpec((n_heads, 1, tn), lambda i: (0, 0, i))],
        out_shape=[jax.ShapeDtypeStruct((n, S), BF16) for n in GROUP_ROWS] + [jax.ShapeDtypeStruct((D_MODEL, 1), F32),
                                                                              jax.ShapeDtypeStruct((n_heads, 1, S), F32)],
        compiler_params=_params("arbitrary"),
    )(oa, of, oc, g, dmixed)


def _loss_head(y, target):
    S, W = y.shape
    tm = _tile(S, 512)

    def body(y_ref, t_ref, d_ref, l_ref):
        @pl.when(pl.program_id(0) == 0)
        def _():
            l_ref[...] = jnp.zeros_like(l_ref)

        err = y_ref[...] - t_ref[...]
        d_ref[...] = err * (1.0 / W)
        l_ref[...] += 0.5 * jnp.sum(jnp.mean(err * err, axis=-1, keepdims=True), axis=0, keepdims=True)

    row = pl.BlockSpec((tm, W), lambda i: (i, 0))
    d, l = pl.pallas_call(
        body, name="loss_head", grid=(S // tm,),
        in_specs=[row, row],
        out_specs=[row, pl.BlockSpec((1, 1), lambda i: (0, 0))],
        out_shape=[jax.ShapeDtypeStruct((S, W), F32), jax.ShapeDtypeStruct((1, 1), F32)],
        compiler_params=_params("arbitrary"),
    )(y, target)
    return l[0, 0], d


def _attn_fwd(q_src, k_src, v_src, rows, H, Dk, Dv, scale, f_row=None, f_col=None, *, name, q_scaled=False):
    S = q_src.shape[1]
    T = _tile(S, ATT_TILE)
    nq = S // T
    forget = f_row is not None
    qb, kb, vb = rows[0] // (H * Dk), rows[1] // (H * Dk), rows[2] // (H * Dv)
    hs = range(H)

    def body(*refs):
        if forget:
            q_ref, k_ref, v_ref, fq_ref, fk_ref, o_ref, lse_ref = refs
        else:
            q_ref, k_ref, v_ref, o_ref, lse_ref = refs
        i = pl.program_id(0)

        def tile(j, masked, state):
            off = pl.multiple_of(j * T, T)
            ss = [_dot(k_ref[h * Dk:(h + 1) * Dk, pl.ds(off, T)], q_ref[h * Dk:(h + 1) * Dk, :], TN) for h in hs]
            if not q_scaled:
                ss = [s * (scale * LOG2E) for s in ss]
            if forget:
                ss = [ss[h] + (fq_ref[h] - fk_ref[pl.ds(off, T), h:h + 1]) for h in hs]
            if masked:
                r = lax.broadcasted_iota(jnp.int32, (T, T), 0)
                c = lax.broadcasted_iota(jnp.int32, (T, T), 1)
                ss = [jnp.where(r <= c, s, NEG_INF) for s in ss]
            m_new = [jnp.maximum(state[h][0], jnp.max(ss[h], axis=0, keepdims=True)) for h in hs]
            alpha = [jnp.exp2(state[h][0] - m_new[h]) for h in hs]
            ps = [jnp.exp2(ss[h] - m_new[h]) for h in hs]
            l_new = [alpha[h] * state[h][1] + jnp.sum(ps[h], axis=0, keepdims=True) for h in hs]
            p_hi = [p.astype(BF16) for p in ps]
            vs = [v_ref[h * Dv:(h + 1) * Dv, pl.ds(off, T)] for h in hs]
            pv = [_dot(vs[h], p_hi[h]) for h in hs]
            if forget:
                pv = [pv[h] + _dot(vs[h], (ps[h] - p_hi[h].astype(F32)).astype(BF16)) for h in hs]
            return tuple((m_new[h], l_new[h], alpha[h] * state[h][2] + pv[h]) for h in hs)

        init = tuple((jnp.full((1, T), NEG_INF, F32), jnp.zeros((1, T), F32), jnp.zeros((Dv, T), F32)) for _ in hs)
        state = lax.fori_loop(0, i, lambda j, st: tile(j, False, st), init)
        state = tile(i, True, state)
        for h in hs:
            m, l, acc = state[h]
            o_ref[h * Dv:(h + 1) * Dv, :] = acc / l
            lse_ref[h] = m + jnp.log2(l)

    in_specs = [pl.BlockSpec((H * Dk, T), lambda i: (qb, i)),
                pl.BlockSpec((H * Dk, S), lambda i: (kb, 0)),
                pl.BlockSpec((H * Dv, S), lambda i: (vb, 0))]
    ins = [q_src, k_src, v_src]
    if forget:
        in_specs += [pl.BlockSpec((H, 1, T), lambda i: (0, 0, i)), pl.BlockSpec((S, H), lambda i: (0, 0))]
        ins += [f_row, f_col]
    return pl.pallas_call(
        body, name=name, grid=(nq,),
        in_specs=in_specs,
        out_specs=[pl.BlockSpec((H * Dv, T), lambda i: (0, i)), pl.BlockSpec((H, 1, T), lambda i: (0, 0, i))],
        out_shape=[jax.ShapeDtypeStruct((H * Dv, S), F32), jax.ShapeDtypeStruct((H, 1, S), F32)],
        compiler_params=_params("parallel"),
    )(*ins)


def _attn_bwd(q_src, k_src, v_src, rows, H, Dk, Dv, scale, do, lse, delta, f_row=None, f_col=None, *, name, q_scaled=False):
    S = q_src.shape[1]
    T = _tile(S, ATT_TILE)
    nq = S // T
    forget = f_row is not None
    qb, kb, vb, db = rows[0] // (H * Dk), rows[1] // (H * Dk), rows[2] // (H * Dv), rows[3] // H
    hs = range(H)

    def body(*refs):
        if forget:
            (q_ref, k_ref, v_ref, do_ref, lse_ref, dl_ref, fq_ref, fk_ref,
             dq_ref, dk_ref, dv_ref, df_ref, dk_s, dv_s, df_s) = refs
        else:
            q_ref, k_ref, v_ref, do_ref, lse_ref, dl_ref, dq_ref, dk_ref, dv_ref, dk_s, dv_s = refs
        j = pl.program_id(0)

        @pl.when(j == 0)
        def _():
            dq_ref[...] = jnp.zeros_like(dq_ref)

        dk_s[...] = jnp.zeros_like(dk_s)
        dv_s[...] = jnp.zeros_like(dv_s)
        if forget:
            df_s[...] = jnp.zeros_like(df_s)
        kt = [k_ref[h * Dk:(h + 1) * Dk, :] for h in hs]
        kj = [k.T for k in kt]
        vj = [v_ref[h * Dv:(h + 1) * Dv, :].T for h in hs]
        koff = pl.multiple_of(j * T, T)

        def tile(i, masked):
            cols = pl.ds(pl.multiple_of(i * T, T), T)
            qi = [q_ref[h * Dk:(h + 1) * Dk, cols] for h in hs]
            doi = [do_ref[h * Dv:(h + 1) * Dv, cols] for h in hs]
            st = [_dot(kj[h], qi[h]) for h in hs]
            if not q_scaled:
                st = [x * (scale * LOG2E) for x in st]
            if forget:
                st = [st[h] + (fq_ref[h, :, cols] - fk_ref[pl.ds(koff, T), h:h + 1]) for h in hs]
            if masked:
                r = lax.broadcasted_iota(jnp.int32, (T, T), 0)
                c = lax.broadcasted_iota(jnp.int32, (T, T), 1)
                st = [jnp.where(r <= c, x, NEG_INF) for x in st]
            pt = [jnp.exp2(st[h] - lse_ref[h, :, cols]) for h in hs]
            dpt = [_dot(vj[h], doi[h]) for h in hs]
            dst = [pt[h] * (dpt[h] - dl_ref[h, :, cols]) for h in hs]
            ptb = [p.astype(BF16) for p in pt]
            dsb = [d.astype(BF16) for d in dst]
            for h in hs:
                dv_s[h * Dv:(h + 1) * Dv, :] += _dot(doi[h], ptb[h], NT)
            for h in hs:
                dk_s[h * Dk:(h + 1) * Dk, :] += _dot(qi[h], dsb[h], NT)
            for h in hs:
                dq_ref[h * Dk:(h + 1) * Dk, cols] += _dot(kt[h], dsb[h]) * scale
            if forget:
                for h in hs:
                    part = dst[h][:, 0:LANES]
                    for c0 in range(LANES, T, LANES):
                        part = part + dst[h][:, c0:c0 + LANES]
                    df_s[h] += part

        tile(j, True)

        def loop_body(i, carry):
            tile(i, False)
            return carry

        lax.fori_loop(j + 1, nq, loop_body, 0)
        dk_ref[...] = dk_s[...] * ((1.0 / LOG2E) if q_scaled else scale)
        dv_ref[...] = dv_s[...]
        if forget:
            df_ref[...] = jnp.concatenate([-jnp.sum(df_s[h], axis=-1, keepdims=True) for h in hs], axis=1)

    res = lambda D, b0: pl.BlockSpec((H * D, S), lambda j: (b0, 0))
    blk = lambda D, b0: pl.BlockSpec((H * D, T), lambda j: (b0, j))
    row3 = lambda b0: pl.BlockSpec((H, 1, S), lambda j: (b0, 0, 0))
    in_specs = [res(Dk, qb), blk(Dk, kb), blk(Dv, vb), res(Dv, 0), row3(0), row3(db)]
    ins = [q_src, k_src, v_src, do, lse, delta]
    out_specs = [res(Dk, 0), blk(Dk, 0), blk(Dv, 0)]
    out_shape = [jax.ShapeDtypeStruct((H * Dk, S), F32), jax.ShapeDtypeStruct((H * Dk, S), F32),
                 jax.ShapeDtypeStruct((H * Dv, S), F32)]
    scratch = [pltpu.VMEM((H * Dk, T), F32), pltpu.VMEM((H * Dv, T), F32)]
    if forget:
        in_specs += [row3(0), pl.BlockSpec((S, H), lambda j: (0, 0))]
        ins += [f_row, f_col]
        out_specs.append(pl.BlockSpec((T, H), lambda j: (j, 0)))
        out_shape.append(jax.ShapeDtypeStruct((S, H), F32))
        scratch.append(pltpu.VMEM((H, T, min(T, LANES)), F32))
    return pl.pallas_call(
        body, name=name, grid=(nq,),
        in_specs=in_specs, out_specs=out_specs, out_shape=out_shape, scratch_shapes=scratch,
        compiler_params=_params("arbitrary"),
    )(*ins)


GW = SWA_GROUP * WINDOW


def _swa_masks(i):
    r = lax.broadcasted_iota(jnp.int32, (WINDOW, GW), 0)
    c = lax.broadcasted_iota(jnp.int32, (WINDOW, GW), 1) % WINDOW
    return (r > c) & (i > 0), r <= c


def _swa_specs():
    W = WINDOW
    kv_rows = SWA_KV_HEADS * HEAD_DIM
    q = pl.BlockSpec((SWA_Q_HEADS * HEAD_DIM, W), lambda i: (0, i))
    prev = lambda b: pl.BlockSpec((kv_rows, W), lambda i: (b, jnp.maximum(i - 1, 0)))
    cur = lambda b: pl.BlockSpec((kv_rows, W), lambda i: (b, i))
    bias = pl.BlockSpec((SWA_KV_HEADS, 2 * W, GW), lambda i: (0, 0, 0))
    stat = pl.BlockSpec((SWA_Q_HEADS, W), lambda i: (0, i))
    sink = pl.BlockSpec(memory_space=pltpu.SMEM)
    return q, prev(4), cur(4), prev(5), cur(5), bias, stat, sink


def _group_lanes(ref, g, rows_per_head):
    h0 = g * SWA_GROUP
    return jnp.concatenate([ref[(h0 + j) * rows_per_head:(h0 + j + 1) * rows_per_head, :] for j in range(SWA_GROUP)], axis=1)


def _swa_scores(g, q_ref, kp_ref, kc_ref, b_ref, masks):
    rows = slice(g * HEAD_DIM, (g + 1) * HEAD_DIM)
    qg = _group_lanes(q_ref, g, HEAD_DIM)
    scale = HEAD_DIM ** -0.5
    s_p = jnp.where(masks[0], _dot(kp_ref[rows, :], qg, TN) * scale + b_ref[g, 0:WINDOW, :], NEG_INF)
    s_c = jnp.where(masks[1], _dot(kc_ref[rows, :], qg, TN) * scale + b_ref[g, WINDOW:2 * WINDOW, :], NEG_INF)
    return qg, rows, s_p, s_c


def _sink_row(sink_ref, g):
    return jnp.concatenate([jnp.full((1, WINDOW), sink_ref[g * SWA_GROUP + j], F32) for j in range(SWA_GROUP)], axis=1)


def _swa_fwd(qkv, bias_g, sinks, *, name):
    S = qkv.shape[1]
    qs, kp, kc, vp, vc, bs, stat, sk = _swa_specs()
    gs = range(SWA_KV_HEADS)

    def body(sink_ref, q_ref, kp_ref, kc_ref, vp_ref, vc_ref, b_ref, o_ref, lse_ref):
        masks = _swa_masks(pl.program_id(0))
        sc = [_swa_scores(g, q_ref, kp_ref, kc_ref, b_ref, masks) for g in gs]
        sinks_g = [_sink_row(sink_ref, g) for g in gs]
        m = [jnp.maximum(jnp.maximum(jnp.max(sc[g][2], axis=0, keepdims=True), jnp.max(sc[g][3], axis=0, keepdims=True)),
                         sinks_g[g]) for g in gs]
        p_p = [jnp.exp(sc[g][2] - m[g]) for g in gs]
        p_c = [jnp.exp(sc[g][3] - m[g]) for g in gs]
        l = [jnp.sum(p_p[g], axis=0, keepdims=True) + jnp.sum(p_c[g], axis=0, keepdims=True) + jnp.exp(sinks_g[g] - m[g])
             for g in gs]
        o = [_dot(vp_ref[sc[g][1], :], p_p[g].astype(BF16)) + _dot(vc_ref[sc[g][1], :], p_c[g].astype(BF16)) for g in gs]
        for g in gs:
            og = o[g] / l[g]
            lse = m[g] + jnp.log(l[g])
            for j in range(SWA_GROUP):
                h = g * SWA_GROUP + j
                o_ref[h * HEAD_DIM:(h + 1) * HEAD_DIM, :] = og[:, j * WINDOW:(j + 1) * WINDOW]
                lse_ref[h:h + 1, :] = lse[:, j * WINDOW:(j + 1) * WINDOW]

    return pl.pallas_call(
        body, name=name, grid=(S // WINDOW,),
        in_specs=[sk, qs, kp, kc, vp, vc, bs],
        out_specs=[qs, stat],
        out_shape=[jax.ShapeDtypeStruct((SWA_Q_HEADS * HEAD_DIM, S), F32), jax.ShapeDtypeStruct((SWA_Q_HEADS, S), F32)],
        compiler_params=_params("parallel"),
    )(sinks, qkv, qkv, qkv, qkv, qkv, bias_g)


def _swa_bwd(qkv, bias_g, sinks, do, lse, delta, *, name):
    S = qkv.shape[1]
    W = WINDOW
    qs, kp, kc, vp, vc, bs, stat, sk = _swa_specs()
    scale = HEAD_DIM ** -0.5
    kv_rows = SWA_KV_HEADS * HEAD_DIM
    gs = range(SWA_KV_HEADS)

    def body(sink_ref, q_ref, kp_ref, kc_ref, vp_ref, vc_ref, b_ref, do_ref, lse_ref, dl_ref,
             dq_ref, dkv_ref, db_ref, dsk_ref):
        i = pl.program_id(0)

        @pl.when(i == 0)
        def _():
            dkv_ref[...] = jnp.zeros_like(dkv_ref)
            db_ref[...] = jnp.zeros_like(db_ref)
            dsk_ref[...] = jnp.zeros_like(dsk_ref)

        masks = _swa_masks(i)
        prev = pl.ds(pl.multiple_of(jnp.maximum(i - 1, 0) * W, W), W)
        cur = pl.ds(pl.multiple_of(i * W, W), W)
        sc = [_swa_scores(g, q_ref, kp_ref, kc_ref, b_ref, masks) for g in gs]
        dog = [_group_lanes(do_ref, g, HEAD_DIM) for g in gs]
        lse = [_group_lanes(lse_ref, g, 1) for g in gs]
        dl = [_group_lanes(dl_ref, g, 1) for g in gs]
        p_p = [jnp.exp(sc[g][2] - lse[g]) for g in gs]
        p_c = [jnp.exp(sc[g][3] - lse[g]) for g in gs]
        ds_p = [p_p[g] * (_dot(vp_ref[sc[g][1], :], dog[g], TN) - dl[g]) for g in gs]
        ds_c = [p_c[g] * (_dot(vc_ref[sc[g][1], :], dog[g], TN) - dl[g]) for g in gs]
        for g in gs:
            db_ref[g, 0:W, :] += ds_p[g]
            db_ref[g, W:2 * W, :] += ds_c[g]
            dsk = jnp.exp(_sink_row(sink_ref, g) - lse[g]) * dl[g]
            for j in range(SWA_GROUP):
                h = g * SWA_GROUP + j
                dsk_ref[h:h + 1, :] -= jnp.broadcast_to(jnp.sum(dsk[:, j * W:(j + 1) * W], axis=1, keepdims=True), (1, LANES))
        dsb_p = [d.astype(BF16) for d in ds_p]
        dsb_c = [d.astype(BF16) for d in ds_c]
        for g in gs:
            rows = sc[g][1]
            dq = (_dot(kp_ref[rows, :], dsb_p[g]) + _dot(kc_ref[rows, :], dsb_c[g])) * scale
            for j in range(SWA_GROUP):
                h = g * SWA_GROUP + j
                dq_ref[h * HEAD_DIM:(h + 1) * HEAD_DIM, :] = dq[:, j * W:(j + 1) * W]
        for g in gs:
            rows = sc[g][1]
            vrows = slice(kv_rows + rows.start, kv_rows + rows.stop)
            dkv_ref[rows, prev] += _dot(sc[g][0], dsb_p[g], NT) * scale
            dkv_ref[rows, cur] += _dot(sc[g][0], dsb_c[g], NT) * scale
            dkv_ref[vrows, prev] += _dot(dog[g], p_p[g].astype(BF16), NT)
            dkv_ref[vrows, cur] += _dot(dog[g], p_c[g].astype(BF16), NT)

    return pl.pallas_call(
        body, name=name, grid=(S // W,),
        in_specs=[sk, qs, kp, kc, vp, vc, bs, qs, stat, stat],
        out_specs=[qs, pl.BlockSpec((2 * kv_rows, S), lambda i: (0, 0)), bs, pl.BlockSpec((SWA_Q_HEADS, LANES), lambda i: (0, 0))],
        out_shape=[jax.ShapeDtypeStruct((SWA_Q_HEADS * HEAD_DIM, S), F32), jax.ShapeDtypeStruct((2 * kv_rows, S), F32),
                   jax.ShapeDtypeStruct((SWA_KV_HEADS, 2 * W, GW), F32), jax.ShapeDtypeStruct((SWA_Q_HEADS, LANES), F32)],
        compiler_params=_params("arbitrary"),
    )(sinks, qkv, qkv, qkv, qkv, qkv, bias_g, do, lse, delta)


def _rel_onehot_t():
    qi = jnp.arange(WINDOW, dtype=jnp.int32)[None, :] + WINDOW
    kj = jnp.arange(2 * WINDOW, dtype=jnp.int32)[:, None]
    dist = qi - kj
    max_exact = REL_BUCKETS // 2
    d = jnp.maximum(dist, 0)
    log_ratio = jnp.log(jnp.maximum(d, 1).astype(F32) / max_exact) / math.log(REL_MAX_DIST / max_exact)
    large = jnp.minimum(max_exact + (log_ratio * (REL_BUCKETS - max_exact)).astype(jnp.int32), REL_BUCKETS - 1)
    bucket = jnp.where(d < max_exact, d, large).reshape(-1)
    return (bucket[None, :] == jnp.arange(REL_BUCKETS, dtype=jnp.int32)[:, None]).astype(BF16)


def _bias_table(rel_bias_t, onehot_t):
    Hq, NB = rel_bias_t.shape
    N = onehot_t.shape[1]
    tn = _tile(N, 4096)

    def body(r_ref, oh_ref, o_ref):
        oh = oh_ref[...]
        a1, a2, a3 = _split3(r_ref[...])
        o_ref[...] = _dot(a1, oh) + _dot(a2, oh) + _dot(a3, oh)

    return pl.pallas_call(
        body, name="rel_bias_table", grid=(N // tn,),
        in_specs=[pl.BlockSpec((Hq, NB), lambda j: (0, 0)), pl.BlockSpec((NB, tn), lambda j: (0, j))],
        out_specs=pl.BlockSpec((Hq, tn), lambda j: (0, j)),
        out_shape=jax.ShapeDtypeStruct((Hq, N), F32),
        compiler_params=_params("parallel"),
    )(rel_bias_t, onehot_t)


def _bias_table_bwd(dbias, onehot_t):
    L, Hq, N = dbias.shape
    NB = onehot_t.shape[0]
    tn = _tile(N, 4096)

    def body(d_ref, oh_ref, o_ref):
        @pl.when(pl.program_id(0) == 0)
        def _():
            o_ref[...] = jnp.zeros_like(o_ref)

        d = d_ref[0]
        for l in range(1, L):
            d = d + d_ref[l]
        oh = oh_ref[...]
        a1, a2, a3 = _split3(d)
        o_ref[...] += _dot(a1, oh, NT) + _dot(a2, oh, NT) + _dot(a3, oh, NT)

    return pl.pallas_call(
        body, name="rel_bias_bwd", grid=(N // tn,),
        in_specs=[pl.BlockSpec((L, Hq, tn), lambda j: (0, 0, j)), pl.BlockSpec((NB, tn), lambda j: (0, j))],
        out_specs=pl.BlockSpec((Hq, NB), lambda j: (0, 0)),
        out_shape=jax.ShapeDtypeStruct((Hq, NB), F32),
        compiler_params=_params("arbitrary"),
    )(dbias, onehot_t)


def _gate_fwd(lat, fb_col, *, name):
    S = lat.shape[1]
    tn = _tile(S, 256)

    def body(z_ref, fb_ref, o_ref, carry):
        @pl.when(pl.program_id(0) == 0)
        def _():
            carry[...] = jnp.zeros_like(carry)

        z = z_ref[...] + fb_ref[...]
        lf = jnp.minimum(z, 0.0) - jnp.log1p(jnp.exp(-jnp.abs(z)))
        r = lax.broadcasted_iota(jnp.int32, (tn, tn), 0)
        c = lax.broadcasted_iota(jnp.int32, (tn, tn), 1)
        tri = (r <= c).astype(BF16)
        a1, a2, a3 = _split3(lf)
        cum = _dot(a1, tri) + _dot(a2, tri) + _dot(a3, tri) + carry[:, 0:1]
        o_ref[...] = cum
        carry[...] = jnp.broadcast_to(cum[:, tn - 1:tn], carry.shape)

    return pl.pallas_call(
        body, name=name, grid=(S // tn,),
        in_specs=[pl.BlockSpec((GATE_ROWS, tn), lambda i: (0, i)), pl.BlockSpec((GATE_ROWS, 1), lambda i: (0, 0))],
        out_specs=pl.BlockSpec((GATE_ROWS, tn), lambda i: (0, i)),
        out_shape=jax.ShapeDtypeStruct((GATE_ROWS, S), F32),
        scratch_shapes=[pltpu.VMEM((GATE_ROWS, LANES), F32)],
        compiler_params=_params("arbitrary"),
    )(lat, fb_col)


def _gate_bwd(lat, fb_col, dF, *, name):
    S = lat.shape[1]
    tn = _tile(S, 256)
    nt = S // tn

    def body(z_ref, fb_ref, df_ref, dz_ref, dfb_ref, carry):
        @pl.when(pl.program_id(0) == 0)
        def _():
            carry[...] = jnp.zeros_like(carry)
            dfb_ref[...] = jnp.zeros_like(dfb_ref)

        r = lax.broadcasted_iota(jnp.int32, (tn, tn), 0)
        c = lax.broadcasted_iota(jnp.int32, (tn, tn), 1)
        tri = (r >= c).astype(BF16)
        a1, a2, a3 = _split3(df_ref[...])
        dlf = _dot(a1, tri) + _dot(a2, tri) + _dot(a3, tri) + carry[:, 0:1]
        carry[...] = jnp.broadcast_to(dlf[:, 0:1], carry.shape)
        z = z_ref[...] + fb_ref[...]
        row = lax.broadcasted_iota(jnp.int32, (GATE_ROWS, tn), 0)
        dz = jnp.where(row < FOX_HEADS, dlf / (1.0 + jnp.exp(z)), 0.0)
        dz_ref[...] = dz
        dfb_ref[...] += jnp.sum(dz, axis=1, keepdims=True)

    blk = pl.BlockSpec((GATE_ROWS, tn), lambda i: (0, nt - 1 - i))
    vec = pl.BlockSpec((GATE_ROWS, 1), lambda i: (0, 0))
    return pl.pallas_call(
        body, name=name, grid=(nt,),
        in_specs=[blk, vec, blk], out_specs=[blk, vec],
        out_shape=[jax.ShapeDtypeStruct((GATE_ROWS, S), F32), jax.ShapeDtypeStruct((GATE_ROWS, 1), F32)],
        scratch_shapes=[pltpu.VMEM((GATE_ROWS, LANES), F32)],
        compiler_params=_params("arbitrary"),
    )(lat, fb_col, dF)


def _rope_tables(S):
    pos = jnp.arange(S, dtype=F32)
    inv_freq = ROPE_THETA ** (-(jnp.arange(MLA_ROPE // 2, dtype=F32) * 2.0 / MLA_ROPE))
    ang = pos[:, None] * inv_freq[None, :]
    cos, sin = jnp.cos(ang).T, jnp.sin(ang).T
    z16 = jnp.zeros_like(cos)

    def slab(lo, fill):
        def put(first, second, f):
            return jnp.concatenate([jnp.full((lo, S), f, F32), first, second, jnp.full((LANES - lo - MLA_ROPE, S), f, F32)], axis=0)
        return put(cos, cos, fill), put(-sin, z16, 0.0), put(z16, sin, 0.0)

    tq = tuple(jnp.tile(t, (MLA_HEADS, 1)) for t in slab(MLA_NOPE, 1.0))
    return tq, slab(0, 0.0)


def _rope(x, c, s1, s2):
    n = x.shape[0]
    half = MLA_ROPE // 2
    return x * c + pltpu.roll(x, n - half, 0) * s1 + pltpu.roll(x, half, 0) * s2


def _rope_t(dy, c, s1, s2):
    n = dy.shape[0]
    half = MLA_ROPE // 2
    return dy * c + pltpu.roll(dy * s1, half, 0) + pltpu.roll(dy * s2, n - half, 0)


KR_SLAB0 = MLA_Q_RANK + MLA_KV_RANK


def _mla_prep_fwd(lat, g_q, g_kv, w_uq_t, w_ukv_t, tq, tmisc, *, name):
    S = lat.shape[1]
    tn = _tile(S, 512)
    QW = MLA_HEADS * MLA_PAD

    def body(lat_ref, gq_ref, gkv_ref, wq_ref, wkv_ref, c_ref, s1_ref, s2_ref, cm_ref, s1m_ref, s2m_ref,
             nq_ref, nkv_ref, q_ref, k_ref, v_ref):
        x = pltpu.roll(lat_ref[...], LAT_ROWS - LAT_SHIFT, 0)
        nq = _col_rms(x[0:MLA_Q_RANK, :], gq_ref[...]).astype(BF16)
        nkv = _col_rms(x[MLA_Q_RANK:KR_SLAB0, :], gkv_ref[...]).astype(BF16)
        nq_ref[...] = nq
        nkv_ref[...] = nkv
        q = _rope(_dot(wq_ref[...], nq), c_ref[...], s1_ref[...], s2_ref[...])
        q_ref[...] = (q * (MLA_SCALE * LOG2E)).astype(BF16)
        kv = _dot(wkv_ref[...], nkv).astype(BF16)
        kr = _rope(x[KR_SLAB0:LAT_ROWS, :], cm_ref[...], s1m_ref[...], s2m_ref[...]).astype(BF16)
        for h in range(MLA_HEADS):
            k_ref[h * MLA_PAD:h * MLA_PAD + MLA_NOPE, :] = kv[h * LANES:h * LANES + MLA_NOPE, :]
            k_ref[h * MLA_PAD + MLA_NOPE:(h + 1) * MLA_PAD, :] = kr[0:MLA_PAD - MLA_NOPE, :]
            v_ref[h * HEAD_DIM:(h + 1) * HEAD_DIM, :] = kv[h * LANES + MLA_NOPE:(h + 1) * LANES, :]

    def col(rows):
        return pl.BlockSpec((rows, tn), lambda i: (0, i))

    def full(a):
        return pl.BlockSpec(a.shape, lambda i: (0, 0))

    return pl.pallas_call(
        body, name=name, grid=(S // tn,),
        in_specs=[col(LAT_ROWS), full(g_q), full(g_kv), full(w_uq_t), full(w_ukv_t),
                  col(QW), col(QW), col(QW), col(LANES), col(LANES), col(LANES)],
        out_specs=[col(MLA_Q_RANK), col(MLA_KV_RANK), col(QW), col(QW), col(MLA_HEADS * HEAD_DIM)],
        out_shape=[jax.ShapeDtypeStruct((MLA_Q_RANK, S), BF16), jax.ShapeDtypeStruct((MLA_KV_RANK, S), BF16),
                   jax.ShapeDtypeStruct((QW, S), BF16), jax.ShapeDtypeStruct((QW, S), BF16),
                   jax.ShapeDtypeStruct((MLA_HEADS * HEAD_DIM, S), BF16)],
        compiler_params=_params("parallel"),
    )(lat, g_q, g_kv, w_uq_t, w_ukv_t, *tq, *tmisc)


def _mla_prep_bwd(lat, nq, nkv, g_q, g_kv, w_uq_p, w_ukv, tq, tmisc, dq, dk, dv, dflog, *, name):
    S = lat.shape[1]
    tn = _tile(S, 512)
    QW = MLA_HEADS * MLA_PAD

    def body(lat_ref, nq_ref, nkv_ref, gq_ref, gkv_ref, wq_ref, wkv_ref, c_ref, s1_ref, s2_ref,
             cm_ref, s1m_ref, s2m_ref, dq_ref, dk_ref, dv_ref, dfl_ref,
             dlat_ref, dwq_ref, dwkv_ref, dgq_ref, dgkv_ref, y_s):
        @pl.when(pl.program_id(0) == 0)
        def _():
            dwq_ref[...] = jnp.zeros_like(dwq_ref)
            dwkv_ref[...] = jnp.zeros_like(dwkv_ref)
            dgq_ref[...] = jnp.zeros_like(dgq_ref)
            dgkv_ref[...] = jnp.zeros_like(dgkv_ref)

        x = pltpu.roll(lat_ref[...], LAT_ROWS - LAT_SHIFT, 0)
        dqm = _rope_t(dq_ref[...], c_ref[...], s1_ref[...], s2_ref[...]).astype(BF16)
        dwq_ref[...] += _dot(dqm, nq_ref[...], NT)
        dx, dg = _col_rms_bwd(x[0:MLA_Q_RANK, :], gq_ref[...], _dot(wq_ref[...], dqm))
        y_s[0:MLA_Q_RANK, :] = dx
        dgq_ref[...] += dg
        dkv = jnp.concatenate(
            [part for h in range(MLA_HEADS)
             for part in (dk_ref[h * MLA_PAD:h * MLA_PAD + MLA_NOPE, :], dv_ref[h * HEAD_DIM:(h + 1) * HEAD_DIM, :])],
            axis=0).astype(BF16)
        dwkv_ref[...] += _dot(dkv, nkv_ref[...], NT)
        dx, dg = _col_rms_bwd(x[MLA_Q_RANK:KR_SLAB0, :], gkv_ref[...], _dot(wkv_ref[...], dkv))
        y_s[MLA_Q_RANK:KR_SLAB0, :] = dx
        dgkv_ref[...] += dg
        dkr = dk_ref[MLA_NOPE:MLA_PAD, :]
        for h in range(1, MLA_HEADS):
            dkr = dkr + dk_ref[h * MLA_PAD + MLA_NOPE:(h + 1) * MLA_PAD, :]
        dkr = jnp.concatenate([dkr, jnp.zeros((MLA_NOPE, tn), F32)], axis=0)
        y_s[KR_SLAB0:LAT_ROWS, :] = _rope_t(dkr, cm_ref[...], s1m_ref[...], s2m_ref[...])
        y = pltpu.roll(y_s[...], LAT_SHIFT, 0)
        row = lax.broadcasted_iota(jnp.int32, (LAT_ROWS, tn), 0)
        dfl = jnp.concatenate([dfl_ref[...], jnp.zeros((LAT_ROWS - GATE_ROWS, tn), F32)], axis=0)
        dlat_ref[...] = jnp.where(row < LAT_SHIFT, dfl, y).astype(BF16)

    def col(rows):
        return pl.BlockSpec((rows, tn), lambda i: (0, i))

    def full(a):
        return pl.BlockSpec(a.shape, lambda i: (0, 0))

    def acc(r, c):
        return pl.BlockSpec((r, c), lambda i: (0, 0))

    return pl.pallas_call(
        body, name=name, grid=(S // tn,),
        in_specs=[col(LAT_ROWS), col(MLA_Q_RANK), col(MLA_KV_RANK), full(g_q), full(g_kv),
                  full(w_uq_p), full(w_ukv), col(QW), col(QW), col(QW), col(LANES), col(LANES), col(LANES),
                  col(QW), col(QW), col(MLA_HEADS * HEAD_DIM), col(GATE_ROWS)],
        out_specs=[col(LAT_ROWS), acc(QW, MLA_Q_RANK), acc(QW, MLA_KV_RANK), acc(MLA_Q_RANK, 1), acc(MLA_KV_RANK, 1)],
        out_shape=[jax.ShapeDtypeStruct((LAT_ROWS, S), BF16), jax.ShapeDtypeStruct((QW, MLA_Q_RANK), F32),
                   jax.ShapeDtypeStruct((QW, MLA_KV_RANK), F32), jax.ShapeDtypeStruct((MLA_Q_RANK, 1), F32),
                   jax.ShapeDtypeStruct((MLA_KV_RANK, 1), F32)],
        scratch_shapes=[pltpu.VMEM((LAT_ROWS, tn), F32)],
        compiler_params=_params("arbitrary"),
    )(lat, nq, nkv, g_q, g_kv, w_uq_p, w_ukv, *tq, *tmisc, dq, dk, dv, dflog)


def _dproj_cast(dqa, dkva, dqf, dkf, dvf, dlat, *, name):
    S = dqa.shape[1]
    tn = _tile(S, 512)
    parts = (dqa, dkva, dqf, dkf, dvf, dlat)

    def body(*refs):
        o_ref = refs[-1]
        r0 = 0
        for ref in refs[:-1]:
            n = ref.shape[0]
            o_ref[r0:r0 + n, :] = ref[...].astype(BF16)
            r0 += n

    return pl.pallas_call(
        body, name=name, grid=(S // tn,),
        in_specs=[pl.BlockSpec((p.shape[0], tn), lambda i: (0, i)) for p in parts],
        out_specs=pl.BlockSpec((IN_ROWS, tn), lambda i: (0, i)),
        out_shape=jax.ShapeDtypeStruct((IN_ROWS, S), BF16),
        compiler_params=_params("parallel"),
    )(*parts)


GELU_C = math.sqrt(2.0 / math.pi)
GELU_A = 0.044715


HALO = 16


def _shift_down(a, k, fill):
    r = pltpu.roll(a, k, 0)
    row = lax.broadcasted_iota(jnp.int32, (8, a.shape[1]), 0)
    head = r[0:8, :]
    for i in range(k):
        head = jnp.where(row == i, fill[len(fill) - k + i], head)
    return jnp.concatenate([head, r[8:, :]], axis=0)


def _shift_up(d, k, fill):
    n = d.shape[0]
    r = pltpu.roll(d, n - k, 0)
    row = lax.broadcasted_iota(jnp.int32, (8, d.shape[1]), 0)
    tail = r[n - 8:n, :]
    for i in range(k):
        tail = jnp.where(row == 8 - k + i, fill[i], tail)
    return jnp.concatenate([r[0:n - 8, :], tail], axis=0)


def _conv_taps(a, before, w_ref, b_ref):
    a1 = _shift_down(a, 1, before)
    a2 = _shift_down(a, 2, before)
    return ((b_ref[...] + w_ref[0:1, :] * a2) + w_ref[1:2, :] * a1) + w_ref[2:3, :] * a


def _rows_before(halo_ref, first):
    h = halo_ref[HALO - 2:HALO, :].astype(F32)
    return jnp.where(first, 0.0, h[0:1, :]), jnp.where(first, 0.0, h[1:2, :])


def _conv_specs(S, tm, tc, nc):
    hb = tm // HALO
    main = lambda off: pl.BlockSpec((tm, tc), lambda j, i: (i, j + off))
    prev = lambda off: pl.BlockSpec((HALO, tc), lambda j, i: (jnp.maximum(i * hb - 1, 0), j + off))
    wspec = lambda off: pl.BlockSpec((3, tc), lambda j, i: (0, j + off))
    bspec = lambda off: pl.BlockSpec((1, tc), lambda j, i: (0, j + off))
    return main, prev, wspec, bspec


def _conv_geglu_fwd(a, conv_w, conv_b, *, name):
    S = a.shape[0]
    tm, tc = _tile(S, 512), _tile(D_FF, 1408)
    nc = D_FF // tc
    main, prev, wspec, bspec = _conv_specs(S, tm, tc, nc)

    def body(ag_ref, au_ref, hg_ref, hu_ref, wg_ref, wu_ref, bg_ref, bu_ref, u_ref, z_ref):
        first = pl.program_id(1) == 0
        gate = _conv_taps(ag_ref[...].astype(F32), _rows_before(hg_ref, first), wg_ref, bg_ref)
        up = _conv_taps(au_ref[...].astype(F32), _rows_before(hu_ref, first), wu_ref, bu_ref)
        u_ref[0] = gate
        u_ref[1] = up
        cdf = 0.5 * (1.0 + jnp.tanh(GELU_C * (gate + GELU_A * (gate * gate * gate))))
        z_ref[...] = (gate * cdf * up).astype(BF16)

    return pl.pallas_call(
        body, name=name, grid=(nc, S // tm),
        in_specs=[main(0), main(nc), prev(0), prev(nc), wspec(0), wspec(nc), bspec(0), bspec(nc)],
        out_specs=[pl.BlockSpec((2, tm, tc), lambda j, i: (0, i, j)), pl.BlockSpec((tm, tc), lambda j, i: (i, j))],
        out_shape=[jax.ShapeDtypeStruct((2, S, D_FF), F32), jax.ShapeDtypeStruct((S, D_FF), BF16)],
        compiler_params=_params("parallel", "arbitrary"),
    )(a, a, a, a, conv_w, conv_w, conv_b, conv_b)


def _geglu_bwd(gate, up, dz):
    g2x = gate * gate
    th = jnp.tanh(GELU_C * (gate + GELU_A * (g2x * gate)))
    cdf = 0.5 * (1.0 + th)
    dgelu = cdf + gate * (0.5 * (1.0 - th * th) * (GELU_C * (1.0 + 3.0 * GELU_A * g2x)))
    return dz * up * dgelu, dz * (gate * cdf)


def _conv_geglu_bwd(a, u, conv_w, dz, *, name):
    S = a.shape[0]
    tm, tc = _tile(S, 512), _tile(D_FF, 1408)
    nc = D_FF // tc
    nr = S // tm
    main, _, wspec, _ = _conv_specs(S, tm, tc, nc)
    hb = tm // 8

    def body(ag_ref, au_ref, u_ref, un_ref, wg_ref, wu_ref, dz_ref, dzn_ref, da_ref, dw_ref, db_ref):
        i = pl.program_id(1)
        last = i == nr - 1

        @pl.when(i == 0)
        def _():
            dw_ref[...] = jnp.zeros_like(dw_ref)
            db_ref[...] = jnp.zeros_like(db_ref)

        dus = _geglu_bwd(u_ref[0], u_ref[1], dz_ref[...])
        dus_n = _geglu_bwd(un_ref[0], un_ref[1], dzn_ref[...])
        for half, a_ref, w_ref in ((0, ag_ref, wg_ref), (1, au_ref, wu_ref)):
            du, du_n = dus[half], dus_n[half]
            after = (jnp.where(last, 0.0, du_n[0:1, :]), jnp.where(last, 0.0, du_n[1:2, :]))
            shifted = (_shift_up(du, 2, after), _shift_up(du, 1, after), du)
            da_ref[half] = (w_ref[2:3, :] * du + w_ref[1:2, :] * shifted[1] + w_ref[0:1, :] * shifted[0]).astype(BF16)
            af = a_ref[...].astype(F32)
            for tap in range(3):
                dw_ref[half, tap:tap + 1, :] += jnp.sum(shifted[tap] * af, axis=0, keepdims=True)
            db_ref[half] += jnp.sum(du, axis=0, keepdims=True)

    nxt8 = lambda j, i: (0, jnp.minimum((i + 1) * hb, S // 8 - 1), j)
    return pl.pallas_call(
        body, name=name, grid=(nc, nr),
        in_specs=[main(0), main(nc), pl.BlockSpec((2, tm, tc), lambda j, i: (0, i, j)), pl.BlockSpec((2, 8, tc), nxt8),
                  wspec(0), wspec(nc), pl.BlockSpec((tm, tc), lambda j, i: (i, j)),
                  pl.BlockSpec((8, tc), lambda j, i: (jnp.minimum((i + 1) * hb, S // 8 - 1), j))],
        out_specs=[pl.BlockSpec((2, tm, tc), lambda j, i: (0, i, j)), pl.BlockSpec((2, 3, tc), lambda j, i: (0, 0, j)),
                   pl.BlockSpec((2, 1, tc), lambda j, i: (0, 0, j))],
        out_shape=[jax.ShapeDtypeStruct((2, S, D_FF), BF16), jax.ShapeDtypeStruct((2, 3, D_FF), F32),
                   jax.ShapeDtypeStruct((2, 1, D_FF), F32)],
        compiler_params=_params("parallel", "arbitrary"),
    )(a, a, u, u, conv_w, conv_w, dz, dz)


ROW_BLOCK_BYTES = 1536 * 1024


def _row_tile(rows, cols):
    return rows if rows * cols * 4 <= ROW_BLOCK_BYTES else _tile(rows, ROW_TILE)


def _adamw_update(w, g, m, v):
    m = ADAM_B1 * m + (1.0 - ADAM_B1) * g
    v = ADAM_B2 * v + (1.0 - ADAM_B2) * jnp.square(g)
    m_hat = m / (1.0 - ADAM_B1 ** ADAM_STEP)
    v_hat = v / (1.0 - ADAM_B2 ** ADAM_STEP)
    return -ADAM_LR * (m_hat / (jnp.sqrt(v_hat) + ADAM_EPS) + ADAM_WD * w), m, v


def _adamw(w, g, m, v, *, name):
    L, A, B = w.shape
    ta = _tile(A, ROW_TILE)

    def body(w_ref, g_ref, m_ref, v_ref, d_ref, mo_ref, vo_ref):
        d_ref[...], mo_ref[...], vo_ref[...] = _adamw_update(w_ref[...], g_ref[...], m_ref[...], v_ref[...])

    blk = pl.BlockSpec((None, ta, B), lambda l, i: (l, i, 0))
    shp = jax.ShapeDtypeStruct((L, A, B), F32)
    return pl.pallas_call(
        body, name=name, grid=(L, A // ta),
        in_specs=[blk] * 4, out_specs=[blk] * 3, out_shape=[shp] * 3,
        compiler_params=_params("parallel", "parallel"),
    )(w, g, m, v)


def _scalar(v):
    return jnp.reshape(v, (1,)).astype(jnp.int32)


def _adamw_halves(w, g_mine, g_other, m, v, *, name):
    L, A, B = w.shape
    ta = _row_tile(A // 2, B)
    nb = A // 2 // ta

    def body(c_ref, w_ref, gm_ref, go_ref, m_ref, v_ref, g_ref, d_ref, mo_ref, vo_ref):
        g = jnp.where(pl.program_id(1) // nb == c_ref[0], gm_ref[...], go_ref[...])
        g_ref[...] = g
        d_ref[...], mo_ref[...], vo_ref[...] = _adamw_update(w_ref[...], g, m_ref[...], v_ref[...])

    blk = pl.BlockSpec((None, ta, B), lambda l, i, c_ref: (l, i, 0))
    half = pl.BlockSpec((None, ta, B), lambda l, i, c_ref: (l, i % nb, 0))
    shp = jax.ShapeDtypeStruct((L, A, B), F32)
    return pl.pallas_call(
        body, name=name,
        grid_spec=pltpu.PrefetchScalarGridSpec(num_scalar_prefetch=1, grid=(L, A // ta),
                                               in_specs=[blk, half, half, blk, blk], out_specs=[blk] * 4),
        out_shape=[shp] * 4,
        compiler_params=_params("parallel", "parallel"),
    )(_scalar(lax.axis_index("c")), w, g_mine, g_other, m, v)


def _chip_index():
    return 2 * lax.axis_index("x") + lax.axis_index("y")


def _pair_sum(g, recv, *, name):
    n, A, B = g.shape
    ta = _row_tile(A // 2, B)
    nb = A // 2 // ta

    def body(c_ref, g_ref, r_ref, o_ref):
        o_ref[...] = g_ref[...] + r_ref[...]

    return pl.pallas_call(
        body, name=name,
        grid_spec=pltpu.PrefetchScalarGridSpec(
            num_scalar_prefetch=1, grid=(n, nb),
            in_specs=[pl.BlockSpec((None, ta, B), lambda s, r, c_ref: (s, c_ref[0] * nb + r, 0)),
                      pl.BlockSpec((None, ta, B), lambda s, r, c_ref: (s, r, 0))],
            out_specs=pl.BlockSpec((None, ta, B), lambda s, r, c_ref: (s, r, 0))),
        out_shape=jax.ShapeDtypeStruct((n, A // 2, B), F32),
        compiler_params=_params("parallel", "parallel"),
    )(_scalar(lax.axis_index("c")), g, recv)


def _chip_sum(landed, own, *, name):
    n, A2, B = landed.shape
    ta = _row_tile(A2, B)

    def body(me_ref, *refs):
        slots, own_ref, o_ref = refs[:n], refs[n], refs[n + 1]
        parts = [jnp.where(me_ref[0] == s, own_ref[...], slots[s][...]) for s in range(n)]
        o_ref[...] = ((parts[0] + parts[1]) + parts[2]) + parts[3]

    def slot(s):
        return pl.BlockSpec((None, ta, B), lambda r, me_ref: (jnp.where(me_ref[0] == s, (s + 1) % n, s), r, 0))

    return pl.pallas_call(
        body, name=name,
        grid_spec=pltpu.PrefetchScalarGridSpec(
            num_scalar_prefetch=1, grid=(A2 // ta,),
            in_specs=[slot(s) for s in range(n)] + [pl.BlockSpec((None, ta, B), lambda r, me_ref: (me_ref[0], r, 0))],
            out_specs=pl.BlockSpec((ta, B), lambda r, me_ref: (r, 0))),
        out_shape=jax.ShapeDtypeStruct((A2, B), F32),
        compiler_params=_params("parallel"),
    )(_scalar(_chip_index()), *([landed] * n), own)


HBM_SPEC = pl.BlockSpec(memory_space=pl.ANY)
COMM_PARAMS = pltpu.CompilerParams(has_side_effects=True)


def _mesh_pos():
    return lax.axis_index("x"), lax.axis_index("y"), lax.axis_index("c")


def _other_chips(x, y):
    return [(1 - x, y), (x, 1 - y), (1 - x, 1 - y)]


def _remote(src, dst, send_sems, recv_sems, k, to):
    return pltpu.make_async_remote_copy(src_ref=src, dst_ref=dst, send_sem=send_sems.at[k], recv_sem=recv_sems.at[k],
                                        device_id=to, device_id_type=MESH)


def _place_own(gathered, shards, *, name):
    n = len(shards)

    def body(me_ref, *refs):
        for s_ref, o_ref in zip(refs[:n], refs[2 * n:]):
            o_ref[...] = s_ref[...]

    return pl.pallas_call(
        body, name=name,
        grid_spec=pltpu.PrefetchScalarGridSpec(
            num_scalar_prefetch=1, grid=(1,),
            in_specs=[pl.BlockSpec(s.shape, lambda i, me_ref: (0, 0)) for s in shards] + [HBM_SPEC] * n,
            out_specs=[pl.BlockSpec((None,) + s.shape, lambda i, me_ref: (me_ref[0], 0, 0)) for s in shards]),
        out_shape=[jax.ShapeDtypeStruct(g.shape, g.dtype) for g in gathered],
        input_output_aliases={1 + n + k: k for k in range(n)},
        compiler_params=_params("arbitrary"),
    )(_scalar(_chip_index()), *shards, *gathered)


def _half_rows(rows, c, align=8):
    assert (rows // 2) % align == 0
    return pl.ds(pl.multiple_of(c * (rows // 2), align), rows // 2)


BF16_ROWS = 16


def _halved(rows):
    return rows % (2 * BF16_ROWS) == 0


def _gather_copies(srcs, lands, send_sems, recv_sems):
    x, y, c = _mesh_pos()
    me = 2 * x + y
    out = []
    for k in range(len(srcs)):
        a = srcs[k].shape[0]
        rows = _half_rows(a, c, BF16_ROWS) if _halved(a) else pl.ds(0, a)
        for j, (px, py) in enumerate(_other_chips(x, y)):
            send = _remote(srcs[k].at[rows], lands[k].at[me, rows], send_sems, recv_sems, 3 * k + j, (px, py, c))
            recv = _remote(srcs[k].at[rows], lands[k].at[2 * px + py, rows], send_sems, recv_sems, 3 * k + j, (px, py, c))
            out.append((send, recv))
    return out


def _gather_start(srcs):
    nu = len(srcs)
    sizes = [len(su) for su in srcs]
    offs = [2 * sum(sizes[:u]) for u in range(nu + 1)]
    lands = [[lax.empty((N_CHIPS,) + s.shape, s.dtype) for s in su] for su in srcs]
    flat = [a for u in range(nu) for a in srcs[u] + lands[u]]

    def body(*refs):
        bufs, sems, token = refs[:len(flat)], refs[len(flat):len(flat) + 2 * nu], refs[-1]
        for u, n in enumerate(sizes):
            mine = bufs[offs[u]:offs[u + 1]]
            for send, _ in _gather_copies(mine[:n], mine[n:], sems[2 * u], sems[2 * u + 1]):
                send.start()
        token[...] = jnp.zeros_like(token)

    res = pl.pallas_call(
        body, name="weight_gather_start",
        in_specs=[HBM_ONLY] * len(flat),
        out_specs=[SEM_SPEC] * (2 * nu) + [HBM_ONLY] * len(flat) + [pl.BlockSpec(memory_space=pltpu.VMEM)],
        out_shape=[pltpu.SemaphoreType.DMA((3 * n,)) for n in sizes for _ in (0, 1)] + [pltpu.HBM(a.shape, a.dtype) for a in flat]
        + [jax.ShapeDtypeStruct((1, 1), F32)],
        input_output_aliases={i: 2 * nu + i for i in range(len(flat))},
        compiler_params=SPLIT_PARAMS,
    )(*[pltpu.with_memory_space_constraint(a, pltpu.HBM) for a in flat])
    bufs = res[2 * nu:2 * nu + len(flat)]
    state = [(res[2 * u], res[2 * u + 1], list(bufs[offs[u]:offs[u] + n]), list(bufs[offs[u] + n:offs[u + 1]]))
             for u, n in enumerate(sizes)]
    return state, res[-1]


def _gather_wait(state, after, *, name):
    send_sems, recv_sems, srcs, lands = state
    n = len(srcs)

    def body(*refs):
        for send, recv in _gather_copies(refs[:n], refs[n:2 * n], refs[2 * n], refs[2 * n + 1]):
            send.wait_send()
            recv.wait_recv()

    res = pl.pallas_call(
        body, name=name,
        in_specs=[HBM_ONLY] * (2 * n) + [SEM_SPEC, SEM_SPEC, HBM_SPEC],
        out_specs=[HBM_ONLY] * (2 * n),
        out_shape=[pltpu.HBM(a.shape, a.dtype) for a in srcs + lands],
        input_output_aliases={i: i for i in range(2 * n)},
        compiler_params=SPLIT_PARAMS,
    )(*srcs, *lands, send_sems, recv_sems, after)
    return list(res[:n]), list(res[n:])


def _gather_forward(lands, *, name):
    n = len(lands)

    def body(*refs):
        bufs, outs = refs[:n], refs[n:2 * n]
        send_sems, recv_sems = refs[2 * n:]
        x, y, c = _mesh_pos()
        copies, waits = [], []
        for k in range(n):
            a = lands[k].shape[1]
            if not _halved(a):
                continue
            for j, (px, py) in enumerate(_other_chips(x, y)):
                mine = 2 * px + py, _half_rows(a, c, BF16_ROWS)
                copies.append(_remote(bufs[k].at[mine], outs[k].at[mine], send_sems, recv_sems, 3 * k + j, (x, y, 1 - c)))
                lands_here = outs[k].at[2 * px + py, _half_rows(a, 1 - c, BF16_ROWS)]
                waits.append(_remote(lands_here, lands_here, send_sems, recv_sems, 3 * k + j, (x, y, 1 - c)))
        for cp in copies:
            cp.start()
        for cp in waits:
            cp.wait_recv()
        for cp in copies:
            cp.wait_send()

    return pl.pallas_call(
        body, name=name,
        in_specs=[HBM_SPEC] * n, out_specs=[HBM_SPEC] * n,
        out_shape=[jax.ShapeDtypeStruct(a.shape, a.dtype) for a in lands],
        scratch_shapes=[pltpu.SemaphoreType.DMA((3 * n,)), pltpu.SemaphoreType.DMA((3 * n,))],
        input_output_aliases={i: i for i in range(n)},
        compiler_params=COMM_PARAMS,
    )(*lands)


def _sibling_exchange(gs, *, name):
    n = len(gs)

    def body(*refs):
        ins, outs = refs[:n], refs[n:2 * n]
        send_sems, recv_sems = refs[2 * n:]
        x, y, c = _mesh_pos()
        copies = [_remote(ins[k].at[:, _half_rows(gs[k].shape[1], 1 - c)], outs[k], send_sems, recv_sems, k, (x, y, 1 - c))
                  for k in range(n)]
        for cp in copies:
            cp.start()
        for cp in copies:
            cp.wait()

    return pl.pallas_call(
        body, name=name,
        in_specs=[HBM_SPEC] * n, out_specs=[HBM_SPEC] * n,
        out_shape=[jax.ShapeDtypeStruct((g.shape[0], g.shape[1] // 2, g.shape[2]), g.dtype) for g in gs],
        scratch_shapes=[pltpu.SemaphoreType.DMA((n,)), pltpu.SemaphoreType.DMA((n,))],
        compiler_params=COMM_PARAMS,
    )(*gs)


HBM_ONLY = pl.BlockSpec(memory_space=pltpu.HBM)
SEM_SPEC = pl.BlockSpec(memory_space=pltpu.SEMAPHORE)
SPLIT_PARAMS = pltpu.CompilerParams(has_side_effects=pltpu.SideEffectType.DATAFLOW_SIDE_EFFECTING)


def _scatter_copies(srcs, lands, send_sems, recv_sems):
    x, y, c = _mesh_pos()
    me = 2 * x + y
    out = []
    for k in range(len(srcs)):
        for j, (px, py) in enumerate(_other_chips(x, y)):
            s = 2 * px + py
            send = _remote(srcs[k].at[s], lands[k].at[me], send_sems, recv_sems, 3 * k + j, (px, py, c))
            recv = _remote(srcs[k].at[s], lands[k].at[s], send_sems, recv_sems, 3 * k + j, (px, py, c))
            out.append((send, recv))
    return out


def _exchange_copies(srcs, lands, send_sems, recv_sems):
    x, y, c = _mesh_pos()
    out = []
    for k in range(len(srcs)):
        cp = _remote(srcs[k].at[:, _half_rows(srcs[k].shape[1], 1 - c)], lands[k], send_sems, recv_sems, k, (x, y, 1 - c))
        out.append((cp, cp))
    return out


def _split_start(srcs, land_shapes, copies, n_sems, *, name):
    n = len(srcs)
    lands = [lax.empty(shape, s.dtype) for shape, s in zip(land_shapes, srcs)]

    def body(*refs):
        ins, zones = refs[:n], refs[n:2 * n]
        send_sems, recv_sems, token = refs[2 * n], refs[2 * n + 1], refs[-1]
        for send, _ in copies(ins, zones, send_sems, recv_sems):
            send.start()
        token[...] = jnp.zeros_like(token)

    hbm = lambda a: pltpu.HBM(a.shape, a.dtype)
    res = pl.pallas_call(
        body, name=name,
        in_specs=[HBM_ONLY] * (2 * n),
        out_specs=[SEM_SPEC, SEM_SPEC] + [HBM_ONLY] * (2 * n) + [pl.BlockSpec(memory_space=pltpu.VMEM)],
        out_shape=[pltpu.SemaphoreType.DMA((n_sems,)), pltpu.SemaphoreType.DMA((n_sems,))] + [hbm(a) for a in srcs + lands]
        + [jax.ShapeDtypeStruct((1, 1), F32)],
        input_output_aliases={i: 2 + i for i in range(2 * n)},
        compiler_params=SPLIT_PARAMS,
    )(*[pltpu.with_memory_space_constraint(a, pltpu.HBM) for a in srcs + lands])
    return (res[0], res[1], list(res[2:2 + n]), list(res[2 + n:2 + 2 * n])), res[-1]


def _scatter_start(ps, *, name):
    return _split_start(ps, [p.shape for p in ps], _scatter_copies, 3 * len(ps), name=name)


def _exchange_start(gs, *, name):
    return _split_start(gs, [(g.shape[0], g.shape[1] // 2, g.shape[2]) for g in gs], _exchange_copies, len(gs), name=name)


def _split_wait(started, copies, after, *, name):
    ng = len(started)
    sizes = [len(st[2]) for st in started]
    offs = [2 * sum(sizes[:i]) for i in range(ng + 1)]
    flat = [a for (_, _, ps, lands) in started for a in ps + lands]

    def body(*refs):
        bufs, sems = refs[:len(flat)], refs[len(flat):len(flat) + 2 * ng]
        for i, n in enumerate(sizes):
            srcs, zones = bufs[offs[i]:offs[i] + n], bufs[offs[i] + n:offs[i + 1]]
            for send, recv in copies(srcs, zones, sems[2 * i], sems[2 * i + 1]):
                send.wait_send()
                recv.wait_recv()

    res = pl.pallas_call(
        body, name=name,
        in_specs=[HBM_ONLY] * len(flat) + [SEM_SPEC] * (2 * ng) + [HBM_SPEC],
        out_specs=[HBM_ONLY] * len(flat),
        out_shape=[pltpu.HBM(a.shape, a.dtype) for a in flat],
        input_output_aliases={i: i for i in range(len(flat))},
        compiler_params=SPLIT_PARAMS,
    )(*flat, *[s for (ss, rs, _, _) in started for s in (ss, rs)], after)
    return [(list(res[offs[i]:offs[i] + n]), list(res[offs[i] + n:offs[i + 1]])) for i, n in enumerate(sizes)]


def _sibling_share(hs):
    n = len(hs)

    def body(*refs):
        ins, outs = refs[:n], refs[n:2 * n]
        send_sems, recv_sems = refs[2 * n:]
        x, y, c = _mesh_pos()
        copies = [_remote(ins[k], outs[k], send_sems, recv_sems, k, (x, y, 1 - c)) for k in range(n)]
        for cp in copies:
            cp.start()
        for cp in copies:
            cp.wait()

    return pl.pallas_call(
        body, name="grad_sibling_share",
        in_specs=[HBM_SPEC] * n, out_specs=[HBM_SPEC] * n,
        out_shape=[jax.ShapeDtypeStruct(h.shape, h.dtype) for h in hs],
        scratch_shapes=[pltpu.SemaphoreType.DMA((n,)), pltpu.SemaphoreType.DMA((n,))],
        compiler_params=COMM_PARAMS,
    )(*hs)


def _allreduce_small(part, by_chip):
    rows, C = part.shape
    rows2 = by_chip.shape[1]

    def body(p_ref, q_ref, o_ref, o2_ref, slots, slots2, send_sems, recv_sems):
        x, y, c = _mesh_pos()
        me = 4 * x + 2 * y + c
        slots[me] = p_ref[...]
        slots2[me] = q_ref[2 * x + y]
        copies = []
        for k in range(1, 8):
            kx, ky, kc = (k >> 2) & 1, (k >> 1) & 1, k & 1
            peer = (x ^ kx if kx else x, y ^ ky if ky else y, c ^ kc if kc else c)
            src = 4 * peer[0] + 2 * peer[1] + peer[2]
            pair = []
            for j, (mine, zone, lands) in enumerate(((p_ref, slots.at[me], slots.at[src]),
                                                     (q_ref.at[2 * peer[0] + peer[1]], slots2.at[me], slots2.at[src]))):
                cp = _remote(mine, zone, send_sems, recv_sems, 2 * (k - 1) + j, peer)
                cp.start()
                pair.append((cp, _remote(mine, lands, send_sems, recv_sems, 2 * (k - 1) + j, peer)))
            copies += pair
        for _, landing in copies:
            landing.wait_recv()
        for cp, _ in copies:
            cp.wait_send()
        total, total2 = slots[0], slots2[0]
        for d in range(1, 8):
            total, total2 = total + slots[d], total2 + slots2[d]
        o_ref[...] = total
        o2_ref[...] = total2

    vmem = pl.BlockSpec(memory_space=pltpu.VMEM)
    return pl.pallas_call(
        body, name="small_grad_allreduce",
        in_specs=[vmem, vmem], out_specs=[vmem, vmem],
        out_shape=[jax.ShapeDtypeStruct((rows, C), F32), jax.ShapeDtypeStruct((rows2, C), F32)],
        scratch_shapes=[pltpu.VMEM((8, rows, C), F32), pltpu.VMEM((8, rows2, C), F32),
                        pltpu.SemaphoreType.DMA((14,)), pltpu.SemaphoreType.DMA((14,))],
        compiler_params=pltpu.CompilerParams(has_side_effects=True, vmem_limit_bytes=VMEM_LIMIT_BYTES),
    )(part, by_chip)


def _pad_w_uq(w):
    lead = w.shape[:-1]
    w = w.reshape(lead + (MLA_HEADS, MLA_QK))
    w = jnp.concatenate([w, jnp.zeros(lead + (MLA_HEADS, MLA_PAD - MLA_QK), w.dtype)], axis=-1)
    return w.reshape(lead + (MLA_HEADS * MLA_PAD,))


def _unpad_w_uq(g):
    lead = g.shape[:-1]
    return g.reshape(lead + (MLA_HEADS, MLA_PAD))[..., :MLA_QK].reshape(lead + (MLA_HEADS * MLA_QK,))


def _t(a):
    return jnp.swapaxes(a, -1, -2)


def _shards_of_cols(w):
    A, NB = w.shape
    return w.reshape(A, N_CHIPS, NB // N_CHIPS).transpose(1, 0, 2)


BIG = ("w_in", "w_uq", "w_ukv", "w_out", "w_up", "w_down")
SMALL = ("attn_pre_norm", "forget_bias", "swa_sinks", "rel_bias", "q_latent_norm", "kv_latent_norm", "group_norm",
         "attn_post_norm", "ffn_pre_norm", "conv_b", "ffn_post_norm")
WEIGHTS = ("attn_pre_norm", "w_in", "forget_bias", "swa_sinks", "rel_bias", "q_latent_norm", "w_uq", "kv_latent_norm",
           "w_ukv", "group_norm", "w_out", "attn_post_norm", "ffn_pre_norm", "w_up", "conv_w", "conv_b", "w_down",
           "ffn_post_norm")


PACK_UNIT = 8 * LANES


def _pack_rows(shape):
    return -(-int(np.prod(shape)) // PACK_UNIT) * 8


def _pack(arrs, row_mult=8):
    parts = []
    for a in arrs:
        n = int(np.prod(a.shape))
        parts.append(jnp.pad(a.reshape(-1), (0, _pack_rows(a.shape) * LANES - n)).reshape(-1, LANES))
    rows = sum(p.shape[0] for p in parts)
    pad = -rows % row_mult
    if pad:
        parts.append(jnp.zeros((pad, LANES), parts[0].dtype))
    return jnp.concatenate(parts, axis=0)


def _unpack(packed, shapes):
    packed = packed.reshape(-1, LANES)
    out, off = [], 0
    for shp in shapes:
        r = _pack_rows(shp)
        out.append(packed[off:off + r].reshape(-1)[:int(np.prod(shp))].reshape(shp))
        off += r
    return out


LAYER_KEYS = ("w_qkv_t", "w_lat_t", "w_in_t", "w_uq_p", "w_uq_t", "w_ukv", "w_ukv_t", "w_out", "w_up", "w_down", "conv_w")


MIX_WEIGHTS = ("w_in", "w_uq", "w_ukv", "w_out")
FFN_WEIGHTS = ("w_up", "w_down", "conv_w")


def _layer_weights(gathered):
    cols = lambda g: g.transpose(1, 0, 2).reshape(g.shape[1], N_CHIPS * g.shape[2])
    out = {}
    if "w_in" in gathered:
        w_in_t = _t(gathered["w_in"]).reshape(IN_COLS, D_MODEL)
        w_in_t = jnp.pad(w_in_t, ((0, IN_ROWS - IN_COLS), (0, 0)))
        w_uq_p = _pad_w_uq(cols(gathered["w_uq"]))
        w_ukv = cols(gathered["w_ukv"])
        out.update(w_qkv_t=w_in_t[:QKV_ROWS], w_lat_t=w_in_t[QKV_ROWS:], w_in_t=w_in_t, w_uq_p=w_uq_p, w_uq_t=_t(w_uq_p),
                   w_ukv=w_ukv, w_ukv_t=_t(w_ukv), w_out=gathered["w_out"].reshape(D_MODEL, D_MODEL))
    if "w_up" in gathered:
        out.update(w_up=gathered["w_up"], w_down=gathered["w_down"].reshape(D_FF, D_MODEL), conv_w=cols(gathered["conv_w"]))
    return out


def _local_step(x, target, W, layer_weights, layer_done):
    W = dict(W, **{key: [None] * DEPTH for key in LAYER_KEYS})
    S = x.shape[0]
    tq_tabs, tm_tabs = _rope_tables(S)
    onehot_t = _rel_onehot_t()
    bias_t = _bias_table(W["rel_bias"].T, onehot_t).reshape(SWA_KV_HEADS, SWA_GROUP, 2 * WINDOW, WINDOW)
    bias_t = bias_t.transpose(0, 2, 1, 3).reshape(SWA_KV_HEADS, 2 * WINDOW, GW)
    row = lambda a: a.reshape(1, -1)
    col = lambda a: a.reshape(-1, 1)
    fox_rows = (FOX_ROW0, FOX_ROW0 + FOX_HEADS * HEAD_DIM, FOX_ROW0 + 2 * FOX_HEADS * HEAD_DIM, SWA_Q_HEADS)
    fox = dict(rows=fox_rows, H=FOX_HEADS, Dk=HEAD_DIM, Dv=HEAD_DIM, scale=HEAD_DIM ** -0.5)
    mla = dict(rows=(0, 0, 0, SWA_Q_HEADS + FOX_HEADS), H=MLA_HEADS, Dk=MLA_PAD, Dv=HEAD_DIM, scale=MLA_SCALE, q_scaled=True)

    saved = []
    h = _rms_fwd(x, row(W["attn_pre_norm"][0]), name="rms_in")
    for l in range(DEPTH):
        sv = {"x0": x, "h1": h}
        for key, val in layer_weights(l, h, False).items():
            W[key][l] = val
        qkv = _matmul(W["w_qkv_t"][l], h, tb=True, out_dtype=BF16, name="proj_qkv")
        lat = _matmul(W["w_lat_t"][l], h, tb=True, name="proj_lat")
        oa, lse_a = _swa_fwd(qkv, bias_t, W["swa_sinks"][l], name="swa_fwd")
        fb_col = jnp.pad(col(W["forget_bias"][l]), ((0, GATE_ROWS - FOX_HEADS), (0, 0)))
        f4 = _gate_fwd(lat, fb_col, name="fox_gate_fwd")[:FOX_HEADS]
        f2 = f4 * LOG2E
        f_row, f_col = f2[:, None, :], f2.T
        of, lse_f = _attn_fwd(qkv, qkv, qkv, f_row=f_row, f_col=f_col, name="fox_fwd", **fox)
        nq, nkv, qm, km, vm = _mla_prep_fwd(lat, col(W["q_latent_norm"][l]), col(W["kv_latent_norm"][l]), W["w_uq_t"][l],
                                            W["w_ukv_t"][l], tq_tabs, tm_tabs, name="mla_prep_fwd")
        oc, lse_c = _attn_fwd(qm, km, vm, name="mla_fwd", **mla)
        mixed = _group_norm_fwd(oa, of, oc, col(W["group_norm"][l]), name="group_norm_fwd")
        y, x1, h2 = _matmul(mixed, W["w_out"][l], ta=True, name="proj_out",
                            resid_rms=(x, row(W["attn_post_norm"][l]), row(W["ffn_pre_norm"][l])))
        for key, val in layer_weights(l, h2, True).items():
            W[key][l] = val
        a = _matmul(h2, W["w_up"][l], b_shards=True, out_dtype=BF16, name="ffn_up")
        u, z = _conv_geglu_fwd(a, W["conv_w"][l], row(W["conv_b"][l]), name="conv_geglu_fwd")
        g_next = row(W["attn_pre_norm"][l + 1]) if l + 1 < DEPTH else None
        y2, x2, *h_next = _matmul(z, W["w_down"][l], name="ffn_down", resid_rms=(x1, row(W["ffn_post_norm"][l]), g_next))
        h_next = h_next[0] if h_next else None
        sv.update(qkv=qkv, lat=lat, oa=oa, lse_a=lse_a, fb_col=fb_col, f_row=f_row, f_col=f_col, of=of, lse_f=lse_f,
                  nq=nq, nkv=nkv, qm=qm, km=km, vm=vm, oc=oc, lse_c=lse_c, mixed=mixed, y=y, x1=x1, h2=h2, a=a, u=u, z=z, y2=y2)
        saved.append(sv)
        x, h = x2, h_next

    loss, dx = _loss_head(x, target)

    G = {k: [None] * DEPTH for k in WEIGHTS if k != "rel_bias" and k not in BIG}
    dbias_layers = [None] * DEPTH
    for l in reversed(range(DEPTH)):
        sv = saved[l]
        gb = {}
        if l == DEPTH - 1:
            dy2, dg = _rms_bwd(sv["y2"], row(W["ffn_post_norm"][l]), dx, out_dtype=BF16, name="ffn_post_bwd")
            G["ffn_post_norm"][l] = dg[0]
        dz = _matmul(dy2, W["w_down"][l], tb=True, name="ffn_down_dx")
        gb["w_down"] = _matmul(sv["z"], dy2, ta=True, name="ffn_down_dw").reshape(N_CHIPS, D_FF // N_CHIPS, D_MODEL)
        da, dcw, dcb = _conv_geglu_bwd(sv["a"], sv["u"], W["conv_w"][l], dz, name="conv_geglu_bwd")
        G["conv_w"][l] = dcw.transpose(1, 0, 2).reshape(3, 2 * D_FF)
        G["conv_b"][l] = dcb.reshape(2 * D_FF)
        gb["w_up"] = _matmul(sv["h2"], da, ta=True, out_shards=True, b_halves=True, name="ffn_up_dw")
        token = layer_done(l, gb)
        gb = {}
        dx1, dg, dy, dg_post = _matmul(
            da, W["w_up"][l], tb=True, b_shards=True, a_halves=True, name="ffn_up_dx",
            norm_bwd=(sv["x1"], row(W["ffn_pre_norm"][l]) + token, dx, (sv["y"], row(W["attn_post_norm"][l]))))
        G["ffn_pre_norm"][l] = dg[0]
        G["attn_post_norm"][l] = dg_post[0]
        dmixed = _matmul(W["w_out"][l], dy, tb=True, name="proj_out_dx")
        gb["w_out"] = _matmul(sv["mixed"], dy, name="proj_out_dw").reshape(N_CHIPS, D_MODEL // N_CHIPS, D_MODEL)
        doa, dof, doc, dg, delta = _group_norm_bwd(sv["oa"], sv["of"], sv["oc"], col(W["group_norm"][l]), dmixed,
                                                   name="group_norm_bwd")
        G["group_norm"][l] = dg[:, 0]
        dqa, dkva, dbias_l, dsink = _swa_bwd(sv["qkv"], bias_t, W["swa_sinks"][l], doa, sv["lse_a"],
                                             delta.reshape(-1, S), name="swa_bwd")
        dbias_layers[l] = (dbias_l.reshape(SWA_KV_HEADS, 2 * WINDOW, SWA_GROUP, WINDOW).transpose(0, 2, 1, 3)
                           .reshape(SWA_Q_HEADS, -1))
        G["swa_sinks"][l] = dsink[:, 0]
        dqf, dkf, dvf, dfk = _attn_bwd(sv["qkv"], sv["qkv"], sv["qkv"], do=dof, lse=sv["lse_f"], delta=delta,
                                       f_row=sv["f_row"], f_col=sv["f_col"], name="fox_bwd", **fox)
        dF = jnp.pad(dfk.T, ((0, GATE_ROWS - FOX_HEADS), (0, 0)))
        dflog, dfb = _gate_bwd(sv["lat"], sv["fb_col"], dF, name="fox_gate_bwd")
        G["forget_bias"][l] = dfb[:FOX_HEADS, 0]
        dqm, dkm, dvm = _attn_bwd(sv["qm"], sv["km"], sv["vm"], do=doc, lse=sv["lse_c"], delta=delta, name="mla_bwd", **mla)
        dlat, dwq_t, dwkv_t, dgq, dgkv = _mla_prep_bwd(
            sv["lat"], sv["nq"], sv["nkv"], col(W["q_latent_norm"][l]), col(W["kv_latent_norm"][l]), W["w_uq_p"][l],
            W["w_ukv"][l], tq_tabs, tm_tabs, dqm, dkm, dvm, dflog, name="mla_prep_bwd")
        gb["w_uq"], gb["w_ukv"] = _shards_of_cols(_unpad_w_uq(dwq_t.T)), _shards_of_cols(dwkv_t.T)
        G["q_latent_norm"][l], G["kv_latent_norm"][l] = dgq[:, 0], dgkv[:, 0]
        dproj = _dproj_cast(dqa, dkva, dqf, dkf, dvf, dlat, name="dproj_cast")
        dw_in_t = _matmul(dproj, sv["h1"], name="proj_in_dw")
        gb["w_in"] = _t(dw_in_t[:IN_COLS].reshape(N_CHIPS, IN_COLS // N_CHIPS, D_MODEL))
        token = layer_done(l, gb)
        below = (saved[l - 1]["y2"], row(W["ffn_post_norm"][l - 1])) if l > 0 else None
        res = _matmul(dproj, W["w_in_t"][l], ta=True, name="proj_in_dx",
                      norm_bwd=(sv["x0"], row(W["attn_pre_norm"][l]) + token, dx1, below))
        dx, G["attn_pre_norm"][l] = res[0], res[1][0]
        if l > 0:
            dy2, G["ffn_post_norm"][l - 1] = res[2], res[3][0]

    grads = {k: jnp.stack(v) for k, v in G.items()}
    grads["rel_bias"] = _bias_table_bwd(jnp.stack(dbias_layers), onehot_t).T
    return loss, dx, grads


def kernel(x, attn_pre_norm, w_in, forget_bias, swa_sinks, rel_bias, q_latent_norm, w_uq, kv_latent_norm, w_ukv, group_norm, w_out, attn_post_norm, ffn_pre_norm, w_up, conv_w, conv_b, w_down, ffn_post_norm, loss_target, m_attn_pre_norm, m_w_in, m_forget_bias, m_swa_sinks, m_rel_bias, m_q_latent_norm, m_w_uq, m_kv_latent_norm, m_w_ukv, m_group_norm, m_w_out, m_attn_post_norm, m_ffn_pre_norm, m_w_up, m_conv_w, m_conv_b, m_w_down, m_ffn_post_norm, v_attn_pre_norm, v_w_in, v_forget_bias, v_swa_sinks, v_rel_bias, v_q_latent_norm, v_w_uq, v_kv_latent_norm, v_w_ukv, v_group_norm, v_w_out, v_attn_post_norm, v_ffn_pre_norm, v_w_up, v_conv_w, v_conv_b, v_w_down, v_ffn_post_norm):
    args = dict(locals())
    w = {k: args[k] for k in WEIGHTS}
    m = {k: args["m_" + k] for k in WEIGHTS}
    v = {k: args["v_" + k] for k in WEIGHTS}

    block = lambda l, keys: [w[k][l] if k == "conv_w" else w[k][l].astype(BF16) for k in keys]
    units = [(0, MIX_WEIGHTS), (0, FFN_WEIGHTS)] + [(l, MIX_WEIGHTS + FFN_WEIGHTS) for l in range(1, DEPTH)]
    gather_state, token = _gather_start([block(l, keys) for l, keys in units])
    W = {k: w[k] for k in SMALL}
    W["attn_pre_norm"] = W["attn_pre_norm"] + token

    def layer_weights(l, after, for_ffn):
        if for_ffn and l > 0:
            return {}
        keys = FFN_WEIGHTS if for_ffn else (MIX_WEIGHTS if l == 0 else MIX_WEIGHTS + FFN_WEIGHTS)
        tag = f"{l}_{keys[0]}"
        srcs, lands = _gather_wait(gather_state[units.index((l, keys))], after, name="weight_gather_wait_" + tag)
        lands = _gather_forward(lands, name="weight_gather_forward_" + tag)
        lands = _place_own(lands, srcs, name="place_own_shards")
        return _layer_weights(dict(zip(keys, lands)))

    started, groups, pending = [], [], []

    def to_chips(l, keys, gs, recv, tag):
        pair = [_pair_sum(gk, rk, name="grad_pair_sum") for gk, rk in zip(gs, recv)]
        state, token = _scatter_start(pair, name="grad_scatter_start_" + tag)
        started.append(state)
        groups.append((l, keys))
        return token

    def finish_pending(after):
        l, keys, tag, state = pending.pop()
        gs, recv = _split_wait([state], _exchange_copies, after, name="grad_exchange_wait_" + tag)[0]
        return to_chips(l, keys, gs, recv, tag)

    def layer_done(l, gb):
        keys = [k for k in BIG if k in gb]
        gs = [gb[k] for k in keys]
        tag = f"{l}_{keys[0]}"
        token = finish_pending(gs[0]) if pending else 0.0
        if l == 0:
            return token + to_chips(l, keys, gs, _sibling_exchange(gs, name="grad_sibling_exchange_" + tag), tag)
        state, started_token = _exchange_start(gs, name="grad_exchange_start_" + tag)
        pending.append((l, keys, tag, state))
        return token + started_token

    loss_part, dx, g = _local_step(x[0], loss_target[0], W, layer_weights, layer_done)
    loss = lax.psum(loss_part, ("x", "y", "c"))

    reduced = {}
    for (l, keys), (pair, zones) in zip(groups, _split_wait(started, _scatter_copies, dx, name="grad_scatter_wait")):
        for k, p, z in zip(keys, pair, zones):
            reduced[k, l] = _chip_sum(z, p, name="grad_chip_sum")
    mine = [jnp.stack([reduced[k, l] for l in range(DEPTH)]) for k in BIG]
    other = _sibling_share(mine)
    out_g, out_d, out_m, out_v = {}, {}, {}, {}
    for k, g_mine, g_other in zip(BIG, mine, other):
        out_g[k], out_d[k], out_m[k], out_v[k] = _adamw_halves(w[k], g_mine, g_other, m[k], v[k], name="adamw_" + k)

    small_shapes = [w[k].shape for k in SMALL]
    taps_by_chip = g["conv_w"].reshape(DEPTH, 3, N_CHIPS, FF_SHARD).transpose(2, 0, 1, 3)
    reduced, taps = _allreduce_small(_pack([g[k] for k in SMALL]), jnp.stack([_pack([t]) for t in taps_by_chip]))
    g_small = _unpack(reduced, small_shapes) + _unpack(taps, [w["conv_w"].shape])
    names = SMALL + ("conv_w",)
    shapes = small_shapes + [w["conv_w"].shape]
    packed = lambda arrs: _pack(arrs, ROW_TILE)[None]
    d_s, m_s, v_s = _adamw(packed([w[k] for k in names]), packed(g_small), packed([m[k] for k in names]),
                           packed([v[k] for k in names]), name="adamw_small")
    out_g.update(zip(names, g_small))
    out_d.update(zip(names, _unpack(d_s, shapes)))
    out_m.update(zip(names, _unpack(m_s, shapes)))
    out_v.update(zip(names, _unpack(v_s, shapes)))

    return (loss, dx[None], *[out_g[k] for k in WEIGHTS], *[out_d[k] for k in WEIGHTS],
            *[out_m[k] for k in WEIGHTS], *[out_v[k] for k in WEIGHTS])
```

```python
import math

import numpy as np
import jax
import jax.numpy as jnp
from jax import lax
from jax.experimental import pallas as pl
from jax.experimental.pallas import tpu as pltpu

F32 = jnp.float32
BF16 = jnp.bfloat16

D_MODEL = 1024
DEPTH = 4
HEAD_DIM = 64
SWA_Q_HEADS = 8
SWA_KV_HEADS = 2
SWA_GROUP = SWA_Q_HEADS // SWA_KV_HEADS
WINDOW = 128
FOX_HEADS = 4
MLA_HEADS = 4
MLA_Q_RANK = 256
MLA_KV_RANK = 128
MLA_NOPE = 64
MLA_ROPE = 32
MLA_QK = MLA_NOPE + MLA_ROPE
ROPE_THETA = 10000.0
REL_BUCKETS = 32
REL_MAX_DIST = 128
D_FF = 2816
EPS = 1e-6
NEG_INF = -1e30
LANES = 128
N_CHIPS = 4

IN_COLS = 1956
IN_ROWS = 2048
QKV_ROWS = 1536
LAT_ROWS = IN_ROWS - QKV_ROWS
LAT_SHIFT = FOX_HEADS
FOX_ROW0 = 768
MLA_PAD = LANES
GATE_ROWS = 8

ADAM_LR = 0.001
ADAM_B1 = 0.9
ADAM_B2 = 0.999
ADAM_EPS = 1e-08
ADAM_WD = 0.01
ADAM_STEP = 10

VMEM_LIMIT_BYTES = 48 * 1024 * 1024
ATT_TILE = 512
LOG2E = math.log2(math.e)
MLA_SCALE = MLA_QK ** -0.5
ROW_TILE = 256
MESH = pl.DeviceIdType.MESH

NT = (((1,), (1,)), ((), ()))
TN = (((0,), (0,)), ((), ()))
NN = (((1,), (0,)), ((), ()))


WIDE_VMEM_LIMIT_BYTES = 60 * 1024 * 1024


def _params(*sem, vmem_bytes=VMEM_LIMIT_BYTES):
    return pltpu.CompilerParams(dimension_semantics=sem, vmem_limit_bytes=vmem_bytes)


def _tile(dim, cap):
    for t in (2816, 2048, 1408, 1024, 512, 256, 128, 64, 32, 16, 8):
        if t <= cap and dim % t == 0:
            return t
    return dim


def _dot(a, b, dims=NN):
    return lax.dot_general(a, b, dims, preferred_element_type=F32)


def _split3(a):
    a1 = a.astype(BF16)
    r1 = a - a1.astype(F32)
    a2 = r1.astype(BF16)
    a3 = (r1 - a2.astype(F32)).astype(BF16)
    return a1, a2, a3


FF_SHARD = 2 * D_FF // N_CHIPS
MATMUL_VMEM_BYTES = 40 * 1024 * 1024
TAIL_ROWS = 512
TAIL_VMEM_LIMIT_BYTES = 56 * 1024 * 1024


def _matmul(a, b, *, ta=False, tb=False, out_dtype=F32, name, b_shards=False, out_shards=False, a_halves=False,
            b_halves=False, norm_bwd=None, resid_rms=None):
    if a_halves:
        M, K = a.shape[1], 2 * a.shape[2]
    elif ta:
        K, M = a.shape
    else:
        M, K = a.shape
    if b_halves:
        K2, N = b.shape[1], 2 * b.shape[2]
    elif b_shards:
        K2, N = (2 * D_FF, D_MODEL) if tb else (D_MODEL, 2 * D_FF)
    elif tb:
        N, K2 = b.shape
    else:
        K2, N = b.shape
    assert K == K2, (a.shape, b.shape)
    tn = _tile(N, 1408)
    tk = FF_SHARD if (b_shards and tb) else _tile(K, 2816)
    out_bytes = jnp.dtype(out_dtype).itemsize
    with_tail = norm_bwd is not None or resid_rms is not None
    tile_bytes = 4 + (2 * (4 * 4 + 2) if with_tail else 2 * out_bytes)
    vmem = lambda tm, tk: 2 * 2 * tk * (tm + tn) + tile_bytes * tm * tn
    tm = M if M <= 2048 else _tile(M, 1408)
    if M > 2048 and M % 2048 == 0 and tk == K and vmem(2048, tk) <= MATMUL_VMEM_BYTES:
        tm = 2048
    if with_tail:
        assert tn == N and not out_shards and (norm_bwd is None or resid_rms is None)
        tm = TAIL_ROWS
    while vmem(tm, tk) > MATMUL_VMEM_BYTES and tk % 256 == 0:
        tk //= 2
    nk = K // tk
    dims = (((0 if ta else 1,), (1 if tb else 0,)), ((), ()))
    if with_tail:
        if norm_bwd is not None:
            x, g, resid, then = norm_bwd
            chained = then is not None
            tail_in, in_kinds = [x, g, resid] + (list(then) if chained else []), "rvr" + ("rv" if chained else "")
            out_kinds, out_dtypes = "rv" + ("rv" if chained else ""), [F32, F32] + ([BF16, F32] if chained else [])

            def tail(dy, ins, outs):
                dx, dg = _seg_rms_bwd(ins[0][...], ins[1][...], dy)
                dx = dx + ins[2][...]
                outs[0][...] = dx
                outs[1][...] += dg
                if chained:
                    dx2, dg2 = _seg_rms_bwd(ins[3][...], ins[4][...], dx)
                    outs[2][...] = dx2.astype(BF16)
                    outs[3][...] += dg2
        else:
            x, g_post, g_next = resid_rms
            with_next = g_next is not None
            tail_in, in_kinds = [x, g_post] + ([g_next] if with_next else []), "rv" + ("v" if with_next else "")
            out_kinds, out_dtypes = "rr" + ("r" if with_next else ""), [F32, F32] + ([BF16] if with_next else [])

            def tail(y, ins, outs):
                outs[0][...] = y
                xn = ins[0][...] + _seg_rms(y, ins[1][...])
                outs[1][...] = xn
                if with_next:
                    outs[2][...] = _seg_rms(xn, ins[2][...]).astype(BF16)

        def fused(a_ref, b_ref, *refs):
            ins, outs, acc = refs[:len(tail_in)], refs[len(tail_in):-1], refs[-1]
            k, i = pl.program_id(0), pl.program_id(1)
            acc_ref = acc.at[pl.ds(pl.multiple_of(i * tm, tm), tm), :] if nk > 1 else acc

            @pl.when(k == 0)
            def _():
                acc_ref[...] = jnp.zeros((tm, N), F32)

            acc_ref[...] += lax.dot_general(a_ref[...], b_ref[...], dims, preferred_element_type=F32)

            @pl.when((k == nk - 1) & (i == 0))
            def _():
                for o, kind in zip(outs, out_kinds):
                    if kind == "v":
                        o[...] = jnp.zeros_like(o)

            @pl.when(k == nk - 1)
            def _():
                tail(acc_ref[...], ins, outs)

    def body(a_ref, b_ref, o_ref, acc_ref):
        k = pl.program_id(2)

        @pl.when(k == 0)
        def _():
            acc_ref[...] = jnp.zeros_like(acc_ref)

        acc_ref[...] += lax.dot_general(a_ref[...], b_ref[...], dims, preferred_element_type=F32)

        @pl.when(k == nk - 1)
        def _():
            o_ref[...] = acc_ref[...].astype(o_ref.dtype)

    if a_halves:
        nh = K // 2 // tk
        a_spec = pl.BlockSpec((None, tm, tk), lambda i, j, k: (k // nh, i, k % nh))
    else:
        a_spec = pl.BlockSpec((tk, tm), lambda i, j, k: (k, i)) if ta else pl.BlockSpec((tm, tk), lambda i, j, k: (i, k))
    if b_halves:
        nh = N // 2 // tn
        b_spec = pl.BlockSpec((None, tk, tn), lambda i, j, k: (j // nh, k, j % nh))
    elif b_shards and tb:
        assert tk == FF_SHARD
        b_spec = pl.BlockSpec((None, tn, tk), lambda i, j, k: (k, j, 0))
    elif b_shards:
        assert tn == FF_SHARD
        b_spec = pl.BlockSpec((None, tk, tn), lambda i, j, k: (j, k, 0))
    else:
        b_spec = pl.BlockSpec((tn, tk), lambda i, j, k: (j, k)) if tb else pl.BlockSpec((tk, tn), lambda i, j, k: (k, j))
    if out_shards:
        assert tn == FF_SHARD
        out_spec = pl.BlockSpec((None, tm, tn), lambda i, j, k: (j, i, 0))
        out_shape = jax.ShapeDtypeStruct((N // tn, M, tn), out_dtype)
    else:
        out_spec = pl.BlockSpec((tm, tn), lambda i, j, k: (i, j))
        out_shape = jax.ShapeDtypeStruct((M, N), out_dtype)
    if with_tail:
        spec = {"r": pl.BlockSpec((tm, N), lambda k, i: (jnp.where(k == nk - 1, i, 0), 0)),
                "v": pl.BlockSpec((1, N), lambda k, i: (0, 0))}
        a_map, b_map = a_spec.index_map, b_spec.index_map
        return pl.pallas_call(
            fused, name=name, grid=(nk, M // tm),
            in_specs=[pl.BlockSpec(a_spec.block_shape, lambda k, i: a_map(i, 0, k)),
                      pl.BlockSpec(b_spec.block_shape, lambda k, i: b_map(i, 0, k))] + [spec[c] for c in in_kinds],
            out_specs=[spec[c] for c in out_kinds],
            out_shape=[jax.ShapeDtypeStruct((M if c == "r" else 1, N), d) for c, d in zip(out_kinds, out_dtypes)],
            scratch_shapes=[pltpu.VMEM((M if nk > 1 else tm, N), F32)],
            compiler_params=pltpu.CompilerParams(dimension_semantics=("arbitrary", "arbitrary"),
                                                 vmem_limit_bytes=TAIL_VMEM_LIMIT_BYTES if nk > 1 else VMEM_LIMIT_BYTES),
        )(a, b, *tail_in)
    return pl.pallas_call(
        body, name=name, grid=(M // tm, N // tn, nk),
        in_specs=[a_spec, b_spec], out_specs=out_spec, out_shape=out_shape,
        scratch_shapes=[pltpu.VMEM((tm, tn), F32)],
        compiler_params=_params("parallel", "parallel", "arbitrary", vmem_bytes=VMEM_LIMIT_BYTES),
    )(a, b)


def _seg_rms(xs, g):
    r = lax.rsqrt(jnp.mean(xs * xs, axis=-1, keepdims=True) + EPS)
    return xs * r * g


def _seg_rms_bwd(xs, g, dy):
    r = lax.rsqrt(jnp.mean(xs * xs, axis=-1, keepdims=True) + EPS)
    gd = dy * g
    c = jnp.mean(gd * xs, axis=-1, keepdims=True)
    dx = r * gd - xs * (r * r * r * c)
    dg = jnp.sum(dy * (xs * r), axis=0, keepdims=True)
    return dx, dg


def _rms_fwd(x, g, *, name):
    S, W = x.shape
    tm = _tile(S, 512)

    def body(x_ref, g_ref, o_ref):
        o_ref[...] = _seg_rms(x_ref[...], g_ref[...]).astype(o_ref.dtype)

    return pl.pallas_call(
        body, name=name, grid=(S // tm,),
        in_specs=[pl.BlockSpec((tm, W), lambda i: (i, 0)), pl.BlockSpec((1, W), lambda i: (0, 0))],
        out_specs=pl.BlockSpec((tm, W), lambda i: (i, 0)),
        out_shape=jax.ShapeDtypeStruct((S, W), BF16),
        compiler_params=_params("parallel"),
    )(x, g)


def _rms_bwd(x, g, dy, *, out_dtype, name):
    S, W = x.shape
    tm = _tile(S, 512)

    def body(x_ref, g_ref, dy_ref, dx_ref, dg_ref):
        @pl.when(pl.program_id(0) == 0)
        def _():
            dg_ref[...] = jnp.zeros_like(dg_ref)

        dx, dg = _seg_rms_bwd(x_ref[...], g_ref[...], dy_ref[...])
        dx_ref[...] = dx.astype(dx_ref.dtype)
        dg_ref[...] += dg

    row = pl.BlockSpec((tm, W), lambda i: (i, 0))
    vec = pl.BlockSpec((1, W), lambda i: (0, 0))
    return pl.pallas_call(
        body, name=name, grid=(S // tm,),
        in_specs=[row, vec, row], out_specs=[row, vec],
        out_shape=[jax.ShapeDtypeStruct((S, W), out_dtype), jax.ShapeDtypeStruct((1, W), F32)],
        compiler_params=_params("arbitrary"),
    )(x, g, dy)


def _col_rms(xs, g):
    r = lax.rsqrt(jnp.mean(xs * xs, axis=0, keepdims=True) + EPS)
    return xs * r * g


def _col_rms_bwd(xs, g, dy):
    r = lax.rsqrt(jnp.mean(xs * xs, axis=0, keepdims=True) + EPS)
    gd = dy * g
    c = jnp.mean(gd * xs, axis=0, keepdims=True)
    dx = r * gd - xs * (r * r * r * c)
    dg = jnp.sum(dy * (xs * r), axis=1, keepdims=True)
    return dx, dg


GROUP_ROWS = (SWA_Q_HEADS * HEAD_DIM, FOX_HEADS * HEAD_DIM, MLA_HEADS * HEAD_DIM)


def _group_specs(S, tn):
    outs = [pl.BlockSpec((n, tn), lambda i: (0, i)) for n in GROUP_ROWS]
    g = pl.BlockSpec((D_MODEL, 1), lambda i: (0, 0))
    mixed = pl.BlockSpec((D_MODEL, tn), lambda i: (0, i))
    return outs, g, mixed


def _group_norm_fwd(oa, of, oc, g, *, name):
    S = oa.shape[1]
    tn = _tile(S, 512)
    outs, gs, mixed = _group_specs(S, tn)

    def body(a_ref, f_ref, c_ref, g_ref, o_ref):
        r0 = 0
        for ref, n in zip((a_ref, f_ref, c_ref), GROUP_ROWS):
            o_ref[r0:r0 + n, :] = _col_rms(ref[...], g_ref[r0:r0 + n, :]).astype(BF16)
            r0 += n

    return pl.pallas_call(
        body, name=name, grid=(S // tn,),
        in_specs=outs + [gs], out_specs=mixed,
        out_shape=jax.ShapeDtypeStruct((D_MODEL, S), BF16),
        compiler_params=_params("parallel"),
    )(oa, of, oc, g)


def _group_norm_bwd(oa, of, oc, g, dmixed, *, name):
    S = oa.shape[1]
    tn = _tile(S, 512)
    outs, gs, mixed = _group_specs(S, tn)
    n_heads = D_MODEL // HEAD_DIM

    def body(a_ref, f_ref, c_ref, g_ref, dm_ref, da_ref, df_ref, dc_ref, dg_ref, dl_ref):
        @pl.when(pl.program_id(0) == 0)
        def _():
            dg_ref[...] = jnp.zeros_like(dg_ref)

        r0 = 0
        for ref, dref, n in zip((a_ref, f_ref, c_ref), (da_ref, df_ref, dc_ref), GROUP_ROWS):
            o = ref[...]
            dx, dg = _col_rms_bwd(o, g_ref[r0:r0 + n, :], dm_ref[r0:r0 + n, :])
            dxb = dx.astype(BF16)
            dref[...] = dxb
            dg_ref[r0:r0 + n, :] += dg
            od = o * dxb.astype(F32)
            for h in range(n // HEAD_DIM):
                dl_ref[r0 // HEAD_DIM + h] = jnp.sum(od[h * HEAD_DIM:(h + 1) * HEAD_DIM, :], axis=0, keepdims=True)
            r0 += n

    return pl.pallas_call(
        body, name=name, grid=(S // tn,),
        in_specs=outs + [gs, mixed], out_specs=outs + [gs, pl.BlockSpec((n_heads, 1, tn), lambda i: (0, 0, i))],
        out_shape=[jax.ShapeDtypeStruct((n, S), BF16) for n in GROUP_ROWS] + [jax.ShapeDtypeStruct((D_MODEL, 1), F32),
                                                                              jax.ShapeDtypeStruct((n_heads, 1, S), F32)],
        compiler_params=_params("arbitrary"),
    )(oa, of, oc, g, dmixed)


def _loss_head(y, target):
    S, W = y.shape
    tm = _tile(S, 512)

    def body(y_ref, t_ref, d_ref, l_ref):
        @pl.when(pl.program_id(0) == 0)
        def _():
            l_ref[...] = jnp.zeros_like(l_ref)

        err = y_ref[...] - t_ref[...]
        d_ref[...] = err * (1.0 / W)
        l_ref[...] += 0.5 * jnp.sum(jnp.mean(err * err, axis=-1, keepdims=True), axis=0, keepdims=True)

    row = pl.BlockSpec((tm, W), lambda i: (i, 0))
    d, l = pl.pallas_call(
        body, name="loss_head", grid=(S // tm,),
        in_specs=[row, row],
        out_specs=[row, pl.BlockSpec((1, 1), lambda i: (0, 0))],
        out_shape=[jax.ShapeDtypeStruct((S, W), F32), jax.ShapeDtypeStruct((1, 1), F32)],
        compiler_params=_params("arbitrary"),
    )(y, target)
    return l[0, 0], d


def _attn_fwd(q_src, k_src, v_src, rows, H, Dk, Dv, scale, f_row=None, f_col=None, *, name, q_scaled=False):
    S = q_src.shape[1]
    T = _tile(S, ATT_TILE)
    nq = S // T
    forget = f_row is not None
    qb, kb, vb = rows[0] // (H * Dk), rows[1] // (H * Dk), rows[2] // (H * Dv)
    hs = range(H)

    def body(*refs):
        if forget:
            q_ref, k_ref, v_ref, fq_ref, fk_ref, o_ref, lse_ref = refs
        else:
            q_ref, k_ref, v_ref, o_ref, lse_ref = refs
        i = pl.program_id(0)

        def tile(j, masked, state):
            off = pl.multiple_of(j * T, T)
            ss = [_dot(k_ref[h * Dk:(h + 1) * Dk, pl.ds(off, T)], q_ref[h * Dk:(h + 1) * Dk, :], TN) for h in hs]
            if not q_scaled:
                ss = [s * (scale * LOG2E) for s in ss]
            if forget:
                ss = [ss[h] + (fq_ref[h] - fk_ref[pl.ds(off, T), h:h + 1]) for h in hs]
            if masked:
                r = lax.broadcasted_iota(jnp.int32, (T, T), 0)
                c = lax.broadcasted_iota(jnp.int32, (T, T), 1)
                ss = [jnp.where(r <= c, s, NEG_INF) for s in ss]
            m_new = [jnp.maximum(state[h][0], jnp.max(ss[h], axis=0, keepdims=True)) for h in hs]
            alpha = [jnp.exp2(state[h][0] - m_new[h]) for h in hs]
            ps = [jnp.exp2(ss[h] - m_new[h]) for h in hs]
            l_new = [alpha[h] * state[h][1] + jnp.sum(ps[h], axis=0, keepdims=True) for h in hs]
            p_hi = [p.astype(BF16) for p in ps]
            vs = [v_ref[h * Dv:(h + 1) * Dv, pl.ds(off, T)] for h in hs]
            pv = [_dot(vs[h], p_hi[h]) for h in hs]
            if forget:
                pv = [pv[h] + _dot(vs[h], (ps[h] - p_hi[h].astype(F32)).astype(BF16)) for h in hs]
            return tuple((m_new[h], l_new[h], alpha[h] * state[h][2] + pv[h]) for h in hs)

        init = tuple((jnp.full((1, T), NEG_INF, F32), jnp.zeros((1, T), F32), jnp.zeros((Dv, T), F32)) for _ in hs)
        state = lax.fori_loop(0, i, lambda j, st: tile(j, False, st), init)
        state = tile(i, True, state)
        for h in hs:
            m, l, acc = state[h]
            o_ref[h * Dv:(h + 1) * Dv, :] = acc / l
            lse_ref[h] = m + jnp.log2(l)

    in_specs = [pl.BlockSpec((H * Dk, T), lambda i: (qb, i)),
                pl.BlockSpec((H * Dk, S), lambda i: (kb, 0)),
                pl.BlockSpec((H * Dv, S), lambda i: (vb, 0))]
    ins = [q_src, k_src, v_src]
    if forget:
        in_specs += [pl.BlockSpec((H, 1, T), lambda i: (0, 0, i)), pl.BlockSpec((S, H), lambda i: (0, 0))]
        ins += [f_row, f_col]
    return pl.pallas_call(
        body, name=name, grid=(nq,),
        in_specs=in_specs,
        out_specs=[pl.BlockSpec((H * Dv, T), lambda i: (0, i)), pl.BlockSpec((H, 1, T), lambda i: (0, 0, i))],
        out_shape=[jax.ShapeDtypeStruct((H * Dv, S), F32), jax.ShapeDtypeStruct((H, 1, S), F32)],
        compiler_params=_params("parallel", vmem_bytes=WIDE_VMEM_LIMIT_BYTES),
    )(*ins)


def _attn_bwd(q_src, k_src, v_src, rows, H, Dk, Dv, scale, do, lse, delta, f_row=None, f_col=None, *, name, q_scaled=False):
    S = q_src.shape[1]
    T = _tile(S, ATT_TILE)
    nq = S // T
    forget = f_row is not None
    qb, kb, vb, db = rows[0] // (H * Dk), rows[1] // (H * Dk), rows[2] // (H * Dv), rows[3] // H
    hs = range(H)

    def body(*refs):
        if forget:
            (q_ref, k_ref, v_ref, do_ref, lse_ref, dl_ref, fq_ref, fk_ref,
             dq_ref, dk_ref, dv_ref, df_ref, dk_s, dv_s, df_s) = refs
        else:
            q_ref, k_ref, v_ref, do_ref, lse_ref, dl_ref, dq_ref, dk_ref, dv_ref, dk_s, dv_s = refs
        j = pl.program_id(0)

        @pl.when(j == 0)
        def _():
            dq_ref[...] = jnp.zeros_like(dq_ref)

        dk_s[...] = jnp.zeros_like(dk_s)
        dv_s[...] = jnp.zeros_like(dv_s)
        if forget:
            df_s[...] = jnp.zeros_like(df_s)
        kt = [k_ref[h * Dk:(h + 1) * Dk, :] for h in hs]
        kj = [k.T for k in kt]
        vj = [v_ref[h * Dv:(h + 1) * Dv, :].T for h in hs]
        koff = pl.multiple_of(j * T, T)

        def tile(i, masked):
            cols = pl.ds(pl.multiple_of(i * T, T), T)
            qi = [q_ref[h * Dk:(h + 1) * Dk, cols] for h in hs]
            doi = [do_ref[h * Dv:(h + 1) * Dv, cols] for h in hs]
            st = [_dot(kj[h], qi[h]) for h in hs]
            if not q_scaled:
                st = [x * (scale * LOG2E) for x in st]
            if forget:
                st = [st[h] + (fq_ref[h, :, cols] - fk_ref[pl.ds(koff, T), h:h + 1]) for h in hs]
            if masked:
                r = lax.broadcasted_iota(jnp.int32, (T, T), 0)
                c = lax.broadcasted_iota(jnp.int32, (T, T), 1)
                st = [jnp.where(r <= c, x, NEG_INF) for x in st]
            pt = [jnp.exp2(st[h] - lse_ref[h, :, cols]) for h in hs]
            dpt = [_dot(vj[h], doi[h]) for h in hs]
            dst = [pt[h] * (dpt[h] - dl_ref[h, :, cols]) for h in hs]
            ptb = [p.astype(BF16) for p in pt]
            dsb = [d.astype(BF16) for d in dst]
            for h in hs:
                dv_s[h * Dv:(h + 1) * Dv, :] += _dot(doi[h], ptb[h], NT)
            for h in hs:
                dk_s[h * Dk:(h + 1) * Dk, :] += _dot(qi[h], dsb[h], NT)
            for h in hs:
                dq_ref[h * Dk:(h + 1) * Dk, cols] += _dot(kt[h], dsb[h]) * scale
            if forget:
                for h in hs:
                    part = dst[h][:, 0:LANES]
                    for c0 in range(LANES, T, LANES):
                        part = part + dst[h][:, c0:c0 + LANES]
                    df_s[h] += part

        tile(j, True)

        def loop_body(i, carry):
            tile(i, False)
            return carry

        lax.fori_loop(j + 1, nq, loop_body, 0)
        dk_ref[...] = dk_s[...] * ((1.0 / LOG2E) if q_scaled else scale)
        dv_ref[...] = dv_s[...]
        if forget:
            df_ref[...] = jnp.concatenate([-jnp.sum(df_s[h], axis=-1, keepdims=True) for h in hs], axis=1)

    res = lambda D, b0: pl.BlockSpec((H * D, S), lambda j: (b0, 0))
    blk = lambda D, b0: pl.BlockSpec((H * D, T), lambda j: (b0, j))
    row3 = lambda b0: pl.BlockSpec((H, 1, S), lambda j: (b0, 0, 0))
    in_specs = [res(Dk, qb), blk(Dk, kb), blk(Dv, vb), res(Dv, 0), row3(0), row3(db)]
    ins = [q_src, k_src, v_src, do, lse, delta]
    out_specs = [res(Dk, 0), blk(Dk, 0), blk(Dv, 0)]
    out_shape = [jax.ShapeDtypeStruct((H * Dk, S), F32), jax.ShapeDtypeStruct((H * Dk, S), F32),
                 jax.ShapeDtypeStruct((H * Dv, S), F32)]
    scratch = [pltpu.VMEM((H * Dk, T), F32), pltpu.VMEM((H * Dv, T), F32)]
    if forget:
        in_specs += [row3(0), pl.BlockSpec((S, H), lambda j: (0, 0))]
        ins += [f_row, f_col]
        out_specs.append(pl.BlockSpec((T, H), lambda j: (j, 0)))
        out_shape.append(jax.ShapeDtypeStruct((S, H), F32))
        scratch.append(pltpu.VMEM((H, T, min(T, LANES)), F32))
    return pl.pallas_call(
        body, name=name, grid=(nq,),
        in_specs=in_specs, out_specs=out_specs, out_shape=out_shape, scratch_shapes=scratch,
        compiler_params=_params("arbitrary", vmem_bytes=WIDE_VMEM_LIMIT_BYTES),
    )(*ins)


GW = SWA_GROUP * WINDOW


def _swa_masks(i):
    r = lax.broadcasted_iota(jnp.int32, (WINDOW, GW), 0)
    c = lax.broadcasted_iota(jnp.int32, (WINDOW, GW), 1) % WINDOW
    return (r > c) & (i > 0), r <= c


def _swa_specs():
    W = WINDOW
    kv_rows = SWA_KV_HEADS * HEAD_DIM
    q = pl.BlockSpec((SWA_Q_HEADS * HEAD_DIM, W), lambda i: (0, i))
    prev = lambda b: pl.BlockSpec((kv_rows, W), lambda i: (b, jnp.maximum(i - 1, 0)))
    cur = lambda b: pl.BlockSpec((kv_rows, W), lambda i: (b, i))
    bias = pl.BlockSpec((SWA_KV_HEADS, 2 * W, GW), lambda i: (0, 0, 0))
    stat = pl.BlockSpec((SWA_Q_HEADS, W), lambda i: (0, i))
    sink = pl.BlockSpec(memory_space=pltpu.SMEM)
    return q, prev(4), cur(4), prev(5), cur(5), bias, stat, sink


def _group_lanes(ref, g, rows_per_head):
    h0 = g * SWA_GROUP
    return jnp.concatenate([ref[(h0 + j) * rows_per_head:(h0 + j + 1) * rows_per_head, :] for j in range(SWA_GROUP)], axis=1)


def _swa_scores(g, q_ref, kp_ref, kc_ref, b_ref, masks):
    rows = slice(g * HEAD_DIM, (g + 1) * HEAD_DIM)
    qg = _group_lanes(q_ref, g, HEAD_DIM)
    scale = HEAD_DIM ** -0.5
    s_p = jnp.where(masks[0], _dot(kp_ref[rows, :], qg, TN) * scale + b_ref[g, 0:WINDOW, :], NEG_INF)
    s_c = jnp.where(masks[1], _dot(kc_ref[rows, :], qg, TN) * scale + b_ref[g, WINDOW:2 * WINDOW, :], NEG_INF)
    return qg, rows, s_p, s_c


def _sink_row(sink_ref, g):
    return jnp.concatenate([jnp.full((1, WINDOW), sink_ref[g * SWA_GROUP + j], F32) for j in range(SWA_GROUP)], axis=1)


def _swa_fwd(qkv, bias_g, sinks, *, name):
    S = qkv.shape[1]
    qs, kp, kc, vp, vc, bs, stat, sk = _swa_specs()
    gs = range(SWA_KV_HEADS)

    def body(sink_ref, q_ref, kp_ref, kc_ref, vp_ref, vc_ref, b_ref, o_ref, lse_ref):
        masks = _swa_masks(pl.program_id(0))
        sc = [_swa_scores(g, q_ref, kp_ref, kc_ref, b_ref, masks) for g in gs]
        sinks_g = [_sink_row(sink_ref, g) for g in gs]
        m = [jnp.maximum(jnp.maximum(jnp.max(sc[g][2], axis=0, keepdims=True), jnp.max(sc[g][3], axis=0, keepdims=True)),
                         sinks_g[g]) for g in gs]
        p_p = [jnp.exp(sc[g][2] - m[g]) for g in gs]
        p_c = [jnp.exp(sc[g][3] - m[g]) for g in gs]
        l = [jnp.sum(p_p[g], axis=0, keepdims=True) + jnp.sum(p_c[g], axis=0, keepdims=True) + jnp.exp(sinks_g[g] - m[g])
             for g in gs]
        o = [_dot(vp_ref[sc[g][1], :], p_p[g].astype(BF16)) + _dot(vc_ref[sc[g][1], :], p_c[g].astype(BF16)) for g in gs]
        for g in gs:
            og = o[g] / l[g]
            lse = m[g] + jnp.log(l[g])
            for j in range(SWA_GROUP):
                h = g * SWA_GROUP + j
                o_ref[h * HEAD_DIM:(h + 1) * HEAD_DIM, :] = og[:, j * WINDOW:(j + 1) * WINDOW]
                lse_ref[h:h + 1, :] = lse[:, j * WINDOW:(j + 1) * WINDOW]

    return pl.pallas_call(
        body, name=name, grid=(S // WINDOW,),
        in_specs=[sk, qs, kp, kc, vp, vc, bs],
        out_specs=[qs, stat],
        out_shape=[jax.ShapeDtypeStruct((SWA_Q_HEADS * HEAD_DIM, S), F32), jax.ShapeDtypeStruct((SWA_Q_HEADS, S), F32)],
        compiler_params=_params("parallel", vmem_bytes=WIDE_VMEM_LIMIT_BYTES),
    )(sinks, qkv, qkv, qkv, qkv, qkv, bias_g)


def _swa_bwd(qkv, bias_g, sinks, do, lse, delta, *, name):
    S = qkv.shape[1]
    W = WINDOW
    qs, kp, kc, vp, vc, bs, stat, sk = _swa_specs()
    scale = HEAD_DIM ** -0.5
    kv_rows = SWA_KV_HEADS * HEAD_DIM
    gs = range(SWA_KV_HEADS)

    def body(sink_ref, q_ref, kp_ref, kc_ref, vp_ref, vc_ref, b_ref, do_ref, lse_ref, dl_ref,
             dq_ref, dkv_ref, db_ref, dsk_ref):
        i = pl.program_id(0)

        @pl.when(i == 0)
        def _():
            dkv_ref[...] = jnp.zeros_like(dkv_ref)
            db_ref[...] = jnp.zeros_like(db_ref)
            dsk_ref[...] = jnp.zeros_like(dsk_ref)

        masks = _swa_masks(i)
        prev = pl.ds(pl.multiple_of(jnp.maximum(i - 1, 0) * W, W), W)
        cur = pl.ds(pl.multiple_of(i * W, W), W)
        sc = [_swa_scores(g, q_ref, kp_ref, kc_ref, b_ref, masks) for g in gs]
        dog = [_group_lanes(do_ref, g, HEAD_DIM) for g in gs]
        lse = [_group_lanes(lse_ref, g, 1) for g in gs]
        dl = [_group_lanes(dl_ref, g, 1) for g in gs]
        p_p = [jnp.exp(sc[g][2] - lse[g]) for g in gs]
        p_c = [jnp.exp(sc[g][3] - lse[g]) for g in gs]
        ds_p = [p_p[g] * (_dot(vp_ref[sc[g][1], :], dog[g], TN) - dl[g]) for g in gs]
        ds_c = [p_c[g] * (_dot(vc_ref[sc[g][1], :], dog[g], TN) - dl[g]) for g in gs]
        for g in gs:
            db_ref[g, 0:W, :] += ds_p[g]
            db_ref[g, W:2 * W, :] += ds_c[g]
            dsk = jnp.exp(_sink_row(sink_ref, g) - lse[g]) * dl[g]
            for j in range(SWA_GROUP):
                h = g * SWA_GROUP + j
                dsk_ref[h:h + 1, :] -= jnp.broadcast_to(jnp.sum(dsk[:, j * W:(j + 1) * W], axis=1, keepdims=True), (1, LANES))
        dsb_p = [d.astype(BF16) for d in ds_p]
        dsb_c = [d.astype(BF16) for d in ds_c]
        for g in gs:
            rows = sc[g][1]
            dq = (_dot(kp_ref[rows, :], dsb_p[g]) + _dot(kc_ref[rows, :], dsb_c[g])) * scale
            for j in range(SWA_GROUP):
                h = g * SWA_GROUP + j
                dq_ref[h * HEAD_DIM:(h + 1) * HEAD_DIM, :] = dq[:, j * W:(j + 1) * W]
        for g in gs:
            rows = sc[g][1]
            vrows = slice(kv_rows + rows.start, kv_rows + rows.stop)
            dkv_ref[rows, prev] += _dot(sc[g][0], dsb_p[g], NT) * scale
            dkv_ref[rows, cur] += _dot(sc[g][0], dsb_c[g], NT) * scale
            dkv_ref[vrows, prev] += _dot(dog[g], p_p[g].astype(BF16), NT)
            dkv_ref[vrows, cur] += _dot(dog[g], p_c[g].astype(BF16), NT)

    return pl.pallas_call(
        body, name=name, grid=(S // W,),
        in_specs=[sk, qs, kp, kc, vp, vc, bs, qs, stat, stat],
        out_specs=[qs, pl.BlockSpec((2 * kv_rows, S), lambda i: (0, 0)), bs, pl.BlockSpec((SWA_Q_HEADS, LANES), lambda i: (0, 0))],
        out_shape=[jax.ShapeDtypeStruct((SWA_Q_HEADS * HEAD_DIM, S), F32), jax.ShapeDtypeStruct((2 * kv_rows, S), F32),
                   jax.ShapeDtypeStruct((SWA_KV_HEADS, 2 * W, GW), F32), jax.ShapeDtypeStruct((SWA_Q_HEADS, LANES), F32)],
        compiler_params=_params("arbitrary", vmem_bytes=WIDE_VMEM_LIMIT_BYTES),
    )(sinks, qkv, qkv, qkv, qkv, qkv, bias_g, do, lse, delta)


def _rel_onehot_t():
    qi = jnp.arange(WINDOW, dtype=jnp.int32)[None, :] + WINDOW
    kj = jnp.arange(2 * WINDOW, dtype=jnp.int32)[:, None]
    dist = qi - kj
    max_exact = REL_BUCKETS // 2
    d = jnp.maximum(dist, 0)
    log_ratio = jnp.log(jnp.maximum(d, 1).astype(F32) / max_exact) / math.log(REL_MAX_DIST / max_exact)
    large = jnp.minimum(max_exact + (log_ratio * (REL_BUCKETS - max_exact)).astype(jnp.int32), REL_BUCKETS - 1)
    bucket = jnp.where(d < max_exact, d, large).reshape(-1)
    return (bucket[None, :] == jnp.arange(REL_BUCKETS, dtype=jnp.int32)[:, None]).astype(BF16)


def _bias_table(rel_bias_t, onehot_t):
    Hq, NB = rel_bias_t.shape
    N = onehot_t.shape[1]
    tn = _tile(N, 4096)

    def body(r_ref, oh_ref, o_ref):
        oh = oh_ref[...]
        a1, a2, a3 = _split3(r_ref[...])
        o_ref[...] = _dot(a1, oh) + _dot(a2, oh) + _dot(a3, oh)

    return pl.pallas_call(
        body, name="rel_bias_table", grid=(N // tn,),
        in_specs=[pl.BlockSpec((Hq, NB), lambda j: (0, 0)), pl.BlockSpec((NB, tn), lambda j: (0, j))],
        out_specs=pl.BlockSpec((Hq, tn), lambda j: (0, j)),
        out_shape=jax.ShapeDtypeStruct((Hq, N), F32),
        compiler_params=_params("parallel"),
    )(rel_bias_t, onehot_t)


def _bias_table_bwd(dbias, onehot_t):
    L, Hq, N = dbias.shape
    NB = onehot_t.shape[0]
    tn = _tile(N, 4096)

    def body(d_ref, oh_ref, o_ref):
        @pl.when(pl.program_id(0) == 0)
        def _():
            o_ref[...] = jnp.zeros_like(o_ref)

        d = d_ref[0]
        for l in range(1, L):
            d = d + d_ref[l]
        oh = oh_ref[...]
        a1, a2, a3 = _split3(d)
        o_ref[...] += _dot(a1, oh, NT) + _dot(a2, oh, NT) + _dot(a3, oh, NT)

    return pl.pallas_call(
        body, name="rel_bias_bwd", grid=(N // tn,),
        in_specs=[pl.BlockSpec((L, Hq, tn), lambda j: (0, 0, j)), pl.BlockSpec((NB, tn), lambda j: (0, j))],
        out_specs=pl.BlockSpec((Hq, NB), lambda j: (0, 0)),
        out_shape=jax.ShapeDtypeStruct((Hq, NB), F32),
        compiler_params=_params("arbitrary"),
    )(dbias, onehot_t)


def _gate_fwd(lat, fb_col, *, name):
    S = lat.shape[1]
    tn = _tile(S, 256)

    def body(z_ref, fb_ref, o_ref, carry):
        @pl.when(pl.program_id(0) == 0)
        def _():
            carry[...] = jnp.zeros_like(carry)

        z = z_ref[...] + fb_ref[...]
        lf = jnp.minimum(z, 0.0) - jnp.log1p(jnp.exp(-jnp.abs(z)))
        r = lax.broadcasted_iota(jnp.int32, (tn, tn), 0)
        c = lax.broadcasted_iota(jnp.int32, (tn, tn), 1)
        tri = (r <= c).astype(BF16)
        a1, a2, a3 = _split3(lf)
        cum = _dot(a1, tri) + _dot(a2, tri) + _dot(a3, tri) + carry[:, 0:1]
        o_ref[...] = cum
        carry[...] = jnp.broadcast_to(cum[:, tn - 1:tn], carry.shape)

    return pl.pallas_call(
        body, name=name, grid=(S // tn,),
        in_specs=[pl.BlockSpec((GATE_ROWS, tn), lambda i: (0, i)), pl.BlockSpec((GATE_ROWS, 1), lambda i: (0, 0))],
        out_specs=pl.BlockSpec((GATE_ROWS, tn), lambda i: (0, i)),
        out_shape=jax.ShapeDtypeStruct((GATE_ROWS, S), F32),
        scratch_shapes=[pltpu.VMEM((GATE_ROWS, LANES), F32)],
        compiler_params=_params("arbitrary"),
    )(lat, fb_col)


def _gate_bwd(lat, fb_col, dF, *, name):
    S = lat.shape[1]
    tn = _tile(S, 256)
    nt = S // tn

    def body(z_ref, fb_ref, df_ref, dz_ref, dfb_ref, carry):
        @pl.when(pl.program_id(0) == 0)
        def _():
            carry[...] = jnp.zeros_like(carry)
            dfb_ref[...] = jnp.zeros_like(dfb_ref)

        r = lax.broadcasted_iota(jnp.int32, (tn, tn), 0)
        c = lax.broadcasted_iota(jnp.int32, (tn, tn), 1)
        tri = (r >= c).astype(BF16)
        a1, a2, a3 = _split3(df_ref[...])
        dlf = _dot(a1, tri) + _dot(a2, tri) + _dot(a3, tri) + carry[:, 0:1]
        carry[...] = jnp.broadcast_to(dlf[:, 0:1], carry.shape)
        z = z_ref[...] + fb_ref[...]
        row = lax.broadcasted_iota(jnp.int32, (GATE_ROWS, tn), 0)
        dz = jnp.where(row < FOX_HEADS, dlf / (1.0 + jnp.exp(z)), 0.0)
        dz_ref[...] = dz
        dfb_ref[...] += jnp.sum(dz, axis=1, keepdims=True)

    blk = pl.BlockSpec((GATE_ROWS, tn), lambda i: (0, nt - 1 - i))
    vec = pl.BlockSpec((GATE_ROWS, 1), lambda i: (0, 0))
    return pl.pallas_call(
        body, name=name, grid=(nt,),
        in_specs=[blk, vec, blk], out_specs=[blk, vec],
        out_shape=[jax.ShapeDtypeStruct((GATE_ROWS, S), F32), jax.ShapeDtypeStruct((GATE_ROWS, 1), F32)],
        scratch_shapes=[pltpu.VMEM((GATE_ROWS, LANES), F32)],
        compiler_params=_params("arbitrary"),
    )(lat, fb_col, dF)


def _rope_tables(S):
    pos = jnp.arange(S, dtype=F32)
    inv_freq = ROPE_THETA ** (-(jnp.arange(MLA_ROPE // 2, dtype=F32) * 2.0 / MLA_ROPE))
    ang = pos[:, None] * inv_freq[None, :]
    cos, sin = jnp.cos(ang).T, jnp.sin(ang).T
    z16 = jnp.zeros_like(cos)

    def slab(lo, fill):
        def put(first, second, f):
            return jnp.concatenate([jnp.full((lo, S), f, F32), first, second, jnp.full((LANES - lo - MLA_ROPE, S), f, F32)], axis=0)
        return put(cos, cos, fill), put(-sin, z16, 0.0), put(z16, sin, 0.0)

    tq = tuple(jnp.tile(t, (MLA_HEADS, 1)) for t in slab(MLA_NOPE, 1.0))
    return tq, slab(0, 0.0)


def _rope(x, c, s1, s2):
    n = x.shape[0]
    half = MLA_ROPE // 2
    return x * c + pltpu.roll(x, n - half, 0) * s1 + pltpu.roll(x, half, 0) * s2


def _rope_t(dy, c, s1, s2):
    n = dy.shape[0]
    half = MLA_ROPE // 2
    return dy * c + pltpu.roll(dy * s1, half, 0) + pltpu.roll(dy * s2, n - half, 0)


KR_SLAB0 = MLA_Q_RANK + MLA_KV_RANK


def _mla_prep_fwd(lat, g_q, g_kv, w_uq_t, w_ukv_t, tq, tmisc, *, name):
    S = lat.shape[1]
    tn = _tile(S, 512)
    QW = MLA_HEADS * MLA_PAD

    def body(lat_ref, gq_ref, gkv_ref, wq_ref, wkv_ref, c_ref, s1_ref, s2_ref, cm_ref, s1m_ref, s2m_ref,
             nq_ref, nkv_ref, q_ref, k_ref, v_ref):
        x = pltpu.roll(lat_ref[...], LAT_ROWS - LAT_SHIFT, 0)
        nq = _col_rms(x[0:MLA_Q_RANK, :], gq_ref[...]).astype(BF16)
        nkv = _col_rms(x[MLA_Q_RANK:KR_SLAB0, :], gkv_ref[...]).astype(BF16)
        nq_ref[...] = nq
        nkv_ref[...] = nkv
        q = _rope(_dot(wq_ref[...], nq), c_ref[...], s1_ref[...], s2_ref[...])
        q_ref[...] = (q * (MLA_SCALE * LOG2E)).astype(BF16)
        kv = _dot(wkv_ref[...], nkv).astype(BF16)
        kr = _rope(x[KR_SLAB0:LAT_ROWS, :], cm_ref[...], s1m_ref[...], s2m_ref[...]).astype(BF16)
        for h in range(MLA_HEADS):
            k_ref[h * MLA_PAD:h * MLA_PAD + MLA_NOPE, :] = kv[h * LANES:h * LANES + MLA_NOPE, :]
            k_ref[h * MLA_PAD + MLA_NOPE:(h + 1) * MLA_PAD, :] = kr[0:MLA_PAD - MLA_NOPE, :]
            v_ref[h * HEAD_DIM:(h + 1) * HEAD_DIM, :] = kv[h * LANES + MLA_NOPE:(h + 1) * LANES, :]

    def col(rows):
        return pl.BlockSpec((rows, tn), lambda i: (0, i))

    def full(a):
        return pl.BlockSpec(a.shape, lambda i: (0, 0))

    return pl.pallas_call(
        body, name=name, grid=(S // tn,),
        in_specs=[col(LAT_ROWS), full(g_q), full(g_kv), full(w_uq_t), full(w_ukv_t),
                  col(QW), col(QW), col(QW), col(LANES), col(LANES), col(LANES)],
        out_specs=[col(MLA_Q_RANK), col(MLA_KV_RANK), col(QW), col(QW), col(MLA_HEADS * HEAD_DIM)],
        out_shape=[jax.ShapeDtypeStruct((MLA_Q_RANK, S), BF16), jax.ShapeDtypeStruct((MLA_KV_RANK, S), BF16),
                   jax.ShapeDtypeStruct((QW, S), BF16), jax.ShapeDtypeStruct((QW, S), BF16),
                   jax.ShapeDtypeStruct((MLA_HEADS * HEAD_DIM, S), BF16)],
        compiler_params=_params("parallel"),
    )(lat, g_q, g_kv, w_uq_t, w_ukv_t, *tq, *tmisc)


def _mla_prep_bwd(lat, nq, nkv, g_q, g_kv, w_uq_p, w_ukv, tq, tmisc, dq, dk, dv, dflog, *, name):
    S = lat.shape[1]
    tn = _tile(S, 512)
    QW = MLA_HEADS * MLA_PAD

    def body(lat_ref, nq_ref, nkv_ref, gq_ref, gkv_ref, wq_ref, wkv_ref, c_ref, s1_ref, s2_ref,
             cm_ref, s1m_ref, s2m_ref, dq_ref, dk_ref, dv_ref, dfl_ref,
             dlat_ref, dwq_ref, dwkv_ref, dgq_ref, dgkv_ref, y_s):
        @pl.when(pl.program_id(0) == 0)
        def _():
            dwq_ref[...] = jnp.zeros_like(dwq_ref)
            dwkv_ref[...] = jnp.zeros_like(dwkv_ref)
            dgq_ref[...] = jnp.zeros_like(dgq_ref)
            dgkv_ref[...] = jnp.zeros_like(dgkv_ref)

        x = pltpu.roll(lat_ref[...], LAT_ROWS - LAT_SHIFT, 0)
        dqm = _rope_t(dq_ref[...], c_ref[...], s1_ref[...], s2_ref[...]).astype(BF16)
        dwq_ref[...] += _dot(dqm, nq_ref[...], NT)
        dx, dg = _col_rms_bwd(x[0:MLA_Q_RANK, :], gq_ref[...], _dot(wq_ref[...], dqm))
        y_s[0:MLA_Q_RANK, :] = dx
        dgq_ref[...] += dg
        dkv = jnp.concatenate(
            [part for h in range(MLA_HEADS)
             for part in (dk_ref[h * MLA_PAD:h * MLA_PAD + MLA_NOPE, :], dv_ref[h * HEAD_DIM:(h + 1) * HEAD_DIM, :])],
            axis=0).astype(BF16)
        dwkv_ref[...] += _dot(dkv, nkv_ref[...], NT)
        dx, dg = _col_rms_bwd(x[MLA_Q_RANK:KR_SLAB0, :], gkv_ref[...], _dot(wkv_ref[...], dkv))
        y_s[MLA_Q_RANK:KR_SLAB0, :] = dx
        dgkv_ref[...] += dg
        dkr = dk_ref[MLA_NOPE:MLA_PAD, :]
        for h in range(1, MLA_HEADS):
            dkr = dkr + dk_ref[h * MLA_PAD + MLA_NOPE:(h + 1) * MLA_PAD, :]
        dkr = jnp.concatenate([dkr, jnp.zeros((MLA_NOPE, tn), F32)], axis=0)
        y_s[KR_SLAB0:LAT_ROWS, :] = _rope_t(dkr, cm_ref[...], s1m_ref[...], s2m_ref[...])
        y = pltpu.roll(y_s[...], LAT_SHIFT, 0)
        row = lax.broadcasted_iota(jnp.int32, (LAT_ROWS, tn), 0)
        dfl = jnp.concatenate([dfl_ref[...], jnp.zeros((LAT_ROWS - GATE_ROWS, tn), F32)], axis=0)
        dlat_ref[...] = jnp.where(row < LAT_SHIFT, dfl, y).astype(BF16)

    def col(rows):
        return pl.BlockSpec((rows, tn), lambda i: (0, i))

    def full(a):
        return pl.BlockSpec(a.shape, lambda i: (0, 0))

    def acc(r, c):
        return pl.BlockSpec((r, c), lambda i: (0, 0))

    return pl.pallas_call(
        body, name=name, grid=(S // tn,),
        in_specs=[col(LAT_ROWS), col(MLA_Q_RANK), col(MLA_KV_RANK), full(g_q), full(g_kv),
                  full(w_uq_p), full(w_ukv), col(QW), col(QW), col(QW), col(LANES), col(LANES), col(LANES),
                  col(QW), col(QW), col(MLA_HEADS * HEAD_DIM), col(GATE_ROWS)],
        out_specs=[col(LAT_ROWS), acc(QW, MLA_Q_RANK), acc(QW, MLA_KV_RANK), acc(MLA_Q_RANK, 1), acc(MLA_KV_RANK, 1)],
        out_shape=[jax.ShapeDtypeStruct((LAT_ROWS, S), BF16), jax.ShapeDtypeStruct((QW, MLA_Q_RANK), F32),
                   jax.ShapeDtypeStruct((QW, MLA_KV_RANK), F32), jax.ShapeDtypeStruct((MLA_Q_RANK, 1), F32),
                   jax.ShapeDtypeStruct((MLA_KV_RANK, 1), F32)],
        scratch_shapes=[pltpu.VMEM((LAT_ROWS, tn), F32)],
        compiler_params=_params("arbitrary"),
    )(lat, nq, nkv, g_q, g_kv, w_uq_p, w_ukv, *tq, *tmisc, dq, dk, dv, dflog)


def _dproj_cast(dqa, dkva, dqf, dkf, dvf, dlat, *, name):
    S = dqa.shape[1]
    tn = _tile(S, 512)
    parts = (dqa, dkva, dqf, dkf, dvf, dlat)

    def body(*refs):
        o_ref = refs[-1]
        r0 = 0
        for ref in refs[:-1]:
            n = ref.shape[0]
            o_ref[r0:r0 + n, :] = ref[...].astype(BF16)
            r0 += n

    return pl.pallas_call(
        body, name=name, grid=(S // tn,),
        in_specs=[pl.BlockSpec((p.shape[0], tn), lambda i: (0, i)) for p in parts],
        out_specs=pl.BlockSpec((IN_ROWS, tn), lambda i: (0, i)),
        out_shape=jax.ShapeDtypeStruct((IN_ROWS, S), BF16),
        compiler_params=_params("parallel"),
    )(*parts)


GELU_C = math.sqrt(2.0 / math.pi)
GELU_A = 0.044715


HALO = 16


def _shift_down(a, k, fill):
    r = pltpu.roll(a, k, 0)
    row = lax.broadcasted_iota(jnp.int32, (8, a.shape[1]), 0)
    head = r[0:8, :]
    for i in range(k):
        head = jnp.where(row == i, fill[len(fill) - k + i], head)
    return jnp.concatenate([head, r[8:, :]], axis=0)


def _shift_up(d, k, fill):
    n = d.shape[0]
    r = pltpu.roll(d, n - k, 0)
    row = lax.broadcasted_iota(jnp.int32, (8, d.shape[1]), 0)
    tail = r[n - 8:n, :]
    for i in range(k):
        tail = jnp.where(row == 8 - k + i, fill[i], tail)
    return jnp.concatenate([r[0:n - 8, :], tail], axis=0)


def _conv_taps(a, before, w_ref, b_ref):
    a1 = _shift_down(a, 1, before)
    a2 = _shift_down(a, 2, before)
    return ((b_ref[...] + w_ref[0:1, :] * a2) + w_ref[1:2, :] * a1) + w_ref[2:3, :] * a


def _rows_before(halo_ref, first):
    h = halo_ref[HALO - 2:HALO, :].astype(F32)
    return jnp.where(first, 0.0, h[0:1, :]), jnp.where(first, 0.0, h[1:2, :])


def _conv_specs(S, tm, tc, nc):
    hb = tm // HALO
    main = lambda off: pl.BlockSpec((tm, tc), lambda j, i: (i, j + off))
    prev = lambda off: pl.BlockSpec((HALO, tc), lambda j, i: (jnp.maximum(i * hb - 1, 0), j + off))
    wspec = lambda off: pl.BlockSpec((3, tc), lambda j, i: (0, j + off))
    bspec = lambda off: pl.BlockSpec((1, tc), lambda j, i: (0, j + off))
    return main, prev, wspec, bspec


def _conv_geglu_fwd(a, conv_w, conv_b, *, name):
    S = a.shape[0]
    tm, tc = _tile(S, 512), _tile(D_FF, 1408)
    nc = D_FF // tc
    main, prev, wspec, bspec = _conv_specs(S, tm, tc, nc)

    def body(ag_ref, au_ref, hg_ref, hu_ref, wg_ref, wu_ref, bg_ref, bu_ref, u_ref, z_ref):
        first = pl.program_id(1) == 0
        gate = _conv_taps(ag_ref[...].astype(F32), _rows_before(hg_ref, first), wg_ref, bg_ref)
        up = _conv_taps(au_ref[...].astype(F32), _rows_before(hu_ref, first), wu_ref, bu_ref)
        u_ref[0] = gate
        u_ref[1] = up
        cdf = 0.5 * (1.0 + jnp.tanh(GELU_C * (gate + GELU_A * (gate * gate * gate))))
        z_ref[...] = (gate * cdf * up).astype(BF16)

    return pl.pallas_call(
        body, name=name, grid=(nc, S // tm),
        in_specs=[main(0), main(nc), prev(0), prev(nc), wspec(0), wspec(nc), bspec(0), bspec(nc)],
        out_specs=[pl.BlockSpec((2, tm, tc), lambda j, i: (0, i, j)), pl.BlockSpec((tm, tc), lambda j, i: (i, j))],
        out_shape=[jax.ShapeDtypeStruct((2, S, D_FF), F32), jax.ShapeDtypeStruct((S, D_FF), BF16)],
        compiler_params=_params("parallel", "arbitrary", vmem_bytes=WIDE_VMEM_LIMIT_BYTES),
    )(a, a, a, a, conv_w, conv_w, conv_b, conv_b)


def _geglu_bwd(gate, up, dz):
    g2x = gate * gate
    th = jnp.tanh(GELU_C * (gate + GELU_A * (g2x * gate)))
    cdf = 0.5 * (1.0 + th)
    dgelu = cdf + gate * (0.5 * (1.0 - th * th) * (GELU_C * (1.0 + 3.0 * GELU_A * g2x)))
    return dz * up * dgelu, dz * (gate * cdf)


def _conv_geglu_bwd(a, u, conv_w, dz, *, name):
    S = a.shape[0]
    tm, tc = _tile(S, 512), _tile(D_FF, 1408)
    nc = D_FF // tc
    nr = S // tm
    main, _, wspec, _ = _conv_specs(S, tm, tc, nc)
    hb = tm // 8

    def body(ag_ref, au_ref, u_ref, un_ref, wg_ref, wu_ref, dz_ref, dzn_ref, da_ref, dw_ref, db_ref):
        i = pl.program_id(1)
        last = i == nr - 1

        @pl.when(i == 0)
        def _():
            dw_ref[...] = jnp.zeros_like(dw_ref)
            db_ref[...] = jnp.zeros_like(db_ref)

        dus = _geglu_bwd(u_ref[0], u_ref[1], dz_ref[...])
        dus_n = _geglu_bwd(un_ref[0], un_ref[1], dzn_ref[...])
        for half, a_ref, w_ref in ((0, ag_ref, wg_ref), (1, au_ref, wu_ref)):
            du, du_n = dus[half], dus_n[half]
            after = (jnp.where(last, 0.0, du_n[0:1, :]), jnp.where(last, 0.0, du_n[1:2, :]))
            shifted = (_shift_up(du, 2, after), _shift_up(du, 1, after), du)
            da_ref[half] = (w_ref[2:3, :] * du + w_ref[1:2, :] * shifted[1] + w_ref[0:1, :] * shifted[0]).astype(BF16)
            af = a_ref[...].astype(F32)
            for tap in range(3):
                dw_ref[half, tap:tap + 1, :] += jnp.sum(shifted[tap] * af, axis=0, keepdims=True)
            db_ref[half] += jnp.sum(du, axis=0, keepdims=True)

    nxt8 = lambda j, i: (0, jnp.minimum((i + 1) * hb, S // 8 - 1), j)
    return pl.pallas_call(
        body, name=name, grid=(nc, nr),
        in_specs=[main(0), main(nc), pl.BlockSpec((2, tm, tc), lambda j, i: (0, i, j)), pl.BlockSpec((2, 8, tc), nxt8),
                  wspec(0), wspec(nc), pl.BlockSpec((tm, tc), lambda j, i: (i, j)),
                  pl.BlockSpec((8, tc), lambda j, i: (jnp.minimum((i + 1) * hb, S // 8 - 1), j))],
        out_specs=[pl.BlockSpec((2, tm, tc), lambda j, i: (0, i, j)), pl.BlockSpec((2, 3, tc), lambda j, i: (0, 0, j)),
                   pl.BlockSpec((2, 1, tc), lambda j, i: (0, 0, j))],
        out_shape=[jax.ShapeDtypeStruct((2, S, D_FF), BF16), jax.ShapeDtypeStruct((2, 3, D_FF), F32),
                   jax.ShapeDtypeStruct((2, 1, D_FF), F32)],
        compiler_params=_params("parallel", "arbitrary", vmem_bytes=WIDE_VMEM_LIMIT_BYTES),
    )(a, a, u, u, conv_w, conv_w, dz, dz)


ROW_BLOCK_BYTES = 1536 * 1024


def _row_tile(rows, cols):
    return rows if rows * cols * 4 <= ROW_BLOCK_BYTES else _tile(rows, ROW_TILE)


def _adamw_update(w, g, m, v):
    m = ADAM_B1 * m + (1.0 - ADAM_B1) * g
    v = ADAM_B2 * v + (1.0 - ADAM_B2) * jnp.square(g)
    m_hat = m / (1.0 - ADAM_B1 ** ADAM_STEP)
    v_hat = v / (1.0 - ADAM_B2 ** ADAM_STEP)
    return -ADAM_LR * (m_hat / (jnp.sqrt(v_hat) + ADAM_EPS) + ADAM_WD * w), m, v


def _adamw(w, g, m, v, *, name):
    L, A, B = w.shape
    ta = _tile(A, ROW_TILE)

    def body(w_ref, g_ref, m_ref, v_ref, d_ref, mo_ref, vo_ref):
        d_ref[...], mo_ref[...], vo_ref[...] = _adamw_update(w_ref[...], g_ref[...], m_ref[...], v_ref[...])

    blk = pl.BlockSpec((None, ta, B), lambda l, i: (l, i, 0))
    shp = jax.ShapeDtypeStruct((L, A, B), F32)
    return pl.pallas_call(
        body, name=name, grid=(L, A // ta),
        in_specs=[blk] * 4, out_specs=[blk] * 3, out_shape=[shp] * 3,
        compiler_params=_params("parallel", "parallel"),
    )(w, g, m, v)


def _scalar(v):
    return jnp.reshape(v, (1,)).astype(jnp.int32)


def _adamw_halves(w, g_mine, g_other, m, v, *, name):
    L, A, B = w.shape
    ta = _row_tile(A // 2, B)
    nb = A // 2 // ta

    def body(c_ref, w_ref, gm_ref, go_ref, m_ref, v_ref, g_ref, d_ref, mo_ref, vo_ref):
        g = jnp.where(pl.program_id(1) // nb == c_ref[0], gm_ref[...], go_ref[...])
        g_ref[...] = g
        d_ref[...], mo_ref[...], vo_ref[...] = _adamw_update(w_ref[...], g, m_ref[...], v_ref[...])

    blk = pl.BlockSpec((None, ta, B), lambda l, i, c_ref: (l, i, 0))
    half = pl.BlockSpec((None, ta, B), lambda l, i, c_ref: (l, i % nb, 0))
    shp = jax.ShapeDtypeStruct((L, A, B), F32)
    return pl.pallas_call(
        body, name=name,
        grid_spec=pltpu.PrefetchScalarGridSpec(num_scalar_prefetch=1, grid=(L, A // ta),
                                               in_specs=[blk, half, half, blk, blk], out_specs=[blk] * 4),
        out_shape=[shp] * 4,
        compiler_params=_params("parallel", "parallel"),
    )(_scalar(lax.axis_index("c")), w, g_mine, g_other, m, v)


def _chip_index():
    return 2 * lax.axis_index("x") + lax.axis_index("y")


def _pair_sum(g, recv, *, name):
    n, A, B = g.shape
    ta = _row_tile(A // 2, B)
    nb = A // 2 // ta

    def body(c_ref, g_ref, r_ref, o_ref):
        o_ref[...] = g_ref[...] + r_ref[...]

    return pl.pallas_call(
        body, name=name,
        grid_spec=pltpu.PrefetchScalarGridSpec(
            num_scalar_prefetch=1, grid=(n, nb),
            in_specs=[pl.BlockSpec((None, ta, B), lambda s, r, c_ref: (s, c_ref[0] * nb + r, 0)),
                      pl.BlockSpec((None, ta, B), lambda s, r, c_ref: (s, r, 0))],
            out_specs=pl.BlockSpec((None, ta, B), lambda s, r, c_ref: (s, r, 0))),
        out_shape=jax.ShapeDtypeStruct((n, A // 2, B), F32),
        compiler_params=_params("parallel", "parallel"),
    )(_scalar(lax.axis_index("c")), g, recv)


def _chip_sum(landed, own, *, name):
    n, A2, B = landed.shape
    ta = _row_tile(A2, B)

    def body(me_ref, *refs):
        slots, own_ref, o_ref = refs[:n], refs[n], refs[n + 1]
        parts = [jnp.where(me_ref[0] == s, own_ref[...], slots[s][...]) for s in range(n)]
        o_ref[...] = ((parts[0] + parts[1]) + parts[2]) + parts[3]

    def slot(s):
        return pl.BlockSpec((None, ta, B), lambda r, me_ref: (jnp.where(me_ref[0] == s, (s + 1) % n, s), r, 0))

    return pl.pallas_call(
        body, name=name,
        grid_spec=pltpu.PrefetchScalarGridSpec(
            num_scalar_prefetch=1, grid=(A2 // ta,),
            in_specs=[slot(s) for s in range(n)] + [pl.BlockSpec((None, ta, B), lambda r, me_ref: (me_ref[0], r, 0))],
            out_specs=pl.BlockSpec((ta, B), lambda r, me_ref: (r, 0))),
        out_shape=jax.ShapeDtypeStruct((A2, B), F32),
        compiler_params=_params("parallel"),
    )(_scalar(_chip_index()), *([landed] * n), own)


HBM_SPEC = pl.BlockSpec(memory_space=pl.ANY)
COMM_PARAMS = pltpu.CompilerParams(has_side_effects=True)


def _mesh_pos():
    return lax.axis_index("x"), lax.axis_index("y"), lax.axis_index("c")


def _other_chips(x, y):
    return [(1 - x, y), (x, 1 - y), (1 - x, 1 - y)]


def _remote(src, dst, send_sems, recv_sems, k, to):
    return pltpu.make_async_remote_copy(src_ref=src, dst_ref=dst, send_sem=send_sems.at[k], recv_sem=recv_sems.at[k],
                                        device_id=to, device_id_type=MESH)


def _place_own(gathered, shards, *, name):
    n = len(shards)

    def body(me_ref, *refs):
        for s_ref, o_ref in zip(refs[:n], refs[2 * n:]):
            o_ref[...] = s_ref[...]

    return pl.pallas_call(
        body, name=name,
        grid_spec=pltpu.PrefetchScalarGridSpec(
            num_scalar_prefetch=1, grid=(1,),
            in_specs=[pl.BlockSpec(s.shape, lambda i, me_ref: (0, 0)) for s in shards] + [HBM_SPEC] * n,
            out_specs=[pl.BlockSpec((None,) + s.shape, lambda i, me_ref: (me_ref[0], 0, 0)) for s in shards]),
        out_shape=[jax.ShapeDtypeStruct(g.shape, g.dtype) for g in gathered],
        input_output_aliases={1 + n + k: k for k in range(n)},
        compiler_params=_params("arbitrary"),
    )(_scalar(_chip_index()), *shards, *gathered)


def _half_rows(rows, c, align=8):
    assert (rows // 2) % align == 0
    return pl.ds(pl.multiple_of(c * (rows // 2), align), rows // 2)


BF16_ROWS = 16


def _halved(rows):
    return rows % (2 * BF16_ROWS) == 0


def _gather_copies(srcs, lands, send_sems, recv_sems):
    x, y, c = _mesh_pos()
    me = 2 * x + y
    out = []
    for k in range(len(srcs)):
        a = srcs[k].shape[0]
        rows = _half_rows(a, c, BF16_ROWS) if _halved(a) else pl.ds(0, a)
        for j, (px, py) in enumerate(_other_chips(x, y)):
            send = _remote(srcs[k].at[rows], lands[k].at[me, rows], send_sems, recv_sems, 3 * k + j, (px, py, c))
            recv = _remote(srcs[k].at[rows], lands[k].at[2 * px + py, rows], send_sems, recv_sems, 3 * k + j, (px, py, c))
            out.append((send, recv))
    return out


def _gather_start(srcs):
    nu = len(srcs)
    sizes = [len(su) for su in srcs]
    offs = [2 * sum(sizes[:u]) for u in range(nu + 1)]
    lands = [[lax.empty((N_CHIPS,) + s.shape, s.dtype) for s in su] for su in srcs]
    flat = [a for u in range(nu) for a in srcs[u] + lands[u]]

    def body(*refs):
        bufs, sems, token = refs[:len(flat)], refs[len(flat):len(flat) + 2 * nu], refs[-1]
        for u, n in enumerate(sizes):
            mine = bufs[offs[u]:offs[u + 1]]
            for send, _ in _gather_copies(mine[:n], mine[n:], sems[2 * u], sems[2 * u + 1]):
                send.start()
        token[...] = jnp.zeros_like(token)

    res = pl.pallas_call(
        body, name="weight_gather_start",
        in_specs=[HBM_ONLY] * len(flat),
        out_specs=[SEM_SPEC] * (2 * nu) + [HBM_ONLY] * len(flat) + [pl.BlockSpec(memory_space=pltpu.VMEM)],
        out_shape=[pltpu.SemaphoreType.DMA((3 * n,)) for n in sizes for _ in (0, 1)] + [pltpu.HBM(a.shape, a.dtype) for a in flat]
        + [jax.ShapeDtypeStruct((1, 1), F32)],
        input_output_aliases={i: 2 * nu + i for i in range(len(flat))},
        compiler_params=SPLIT_PARAMS,
    )(*[pltpu.with_memory_space_constraint(a, pltpu.HBM) for a in flat])
    bufs = res[2 * nu:2 * nu + len(flat)]
    state = [(res[2 * u], res[2 * u + 1], list(bufs[offs[u]:offs[u] + n]), list(bufs[offs[u] + n:offs[u + 1]]))
             for u, n in enumerate(sizes)]
    return state, res[-1]


def _gather_wait(state, after, *, name):
    send_sems, recv_sems, srcs, lands = state
    n = len(srcs)

    def body(*refs):
        for send, recv in _gather_copies(refs[:n], refs[n:2 * n], refs[2 * n], refs[2 * n + 1]):
            send.wait_send()
            recv.wait_recv()

    res = pl.pallas_call(
        body, name=name,
        in_specs=[HBM_ONLY] * (2 * n) + [SEM_SPEC, SEM_SPEC, HBM_SPEC],
        out_specs=[HBM_ONLY] * (2 * n),
        out_shape=[pltpu.HBM(a.shape, a.dtype) for a in srcs + lands],
        input_output_aliases={i: i for i in range(2 * n)},
        compiler_params=SPLIT_PARAMS,
    )(*srcs, *lands, send_sems, recv_sems, after)
    return list(res[:n]), list(res[n:])


def _gather_forward(lands, *, name):
    n = len(lands)

    def body(*refs):
        bufs, outs = refs[:n], refs[n:2 * n]
        send_sems, recv_sems = refs[2 * n:]
        x, y, c = _mesh_pos()
        copies, waits = [], []
        for k in range(n):
            a = lands[k].shape[1]
            if not _halved(a):
                continue
            for j, (px, py) in enumerate(_other_chips(x, y)):
                mine = 2 * px + py, _half_rows(a, c, BF16_ROWS)
                copies.append(_remote(bufs[k].at[mine], outs[k].at[mine], send_sems, recv_sems, 3 * k + j, (x, y, 1 - c)))
                lands_here = outs[k].at[2 * px + py, _half_rows(a, 1 - c, BF16_ROWS)]
                waits.append(_remote(lands_here, lands_here, send_sems, recv_sems, 3 * k + j, (x, y, 1 - c)))
        for cp in copies:
            cp.start()
        for cp in waits:
            cp.wait_recv()
        for cp in copies:
            cp.wait_send()

    return pl.pallas_call(
        body, name=name,
        in_specs=[HBM_SPEC] * n, out_specs=[HBM_SPEC] * n,
        out_shape=[jax.ShapeDtypeStruct(a.shape, a.dtype) for a in lands],
        scratch_shapes=[pltpu.SemaphoreType.DMA((3 * n,)), pltpu.SemaphoreType.DMA((3 * n,))],
        input_output_aliases={i: i for i in range(n)},
        compiler_params=COMM_PARAMS,
    )(*lands)


def _sibling_exchange(gs, *, name):
    n = len(gs)

    def body(*refs):
        ins, outs = refs[:n], refs[n:2 * n]
        send_sems, recv_sems = refs[2 * n:]
        x, y, c = _mesh_pos()
        copies = [_remote(ins[k].at[:, _half_rows(gs[k].shape[1], 1 - c)], outs[k], send_sems, recv_sems, k, (x, y, 1 - c))
                  for k in range(n)]
        for cp in copies:
            cp.start()
        for cp in copies:
            cp.wait()

    return pl.pallas_call(
        body, name=name,
        in_specs=[HBM_SPEC] * n, out_specs=[HBM_SPEC] * n,
        out_shape=[jax.ShapeDtypeStruct((g.shape[0], g.shape[1] // 2, g.shape[2]), g.dtype) for g in gs],
        scratch_shapes=[pltpu.SemaphoreType.DMA((n,)), pltpu.SemaphoreType.DMA((n,))],
        compiler_params=COMM_PARAMS,
    )(*gs)


HBM_ONLY = pl.BlockSpec(memory_space=pltpu.HBM)
SEM_SPEC = pl.BlockSpec(memory_space=pltpu.SEMAPHORE)
SPLIT_PARAMS = pltpu.CompilerParams(has_side_effects=pltpu.SideEffectType.DATAFLOW_SIDE_EFFECTING)


def _scatter_copies(srcs, lands, send_sems, recv_sems):
    x, y, c = _mesh_pos()
    me = 2 * x + y
    out = []
    for k in range(len(srcs)):
        for j, (px, py) in enumerate(_other_chips(x, y)):
            s = 2 * px + py
            send = _remote(srcs[k].at[s], lands[k].at[me], send_sems, recv_sems, 3 * k + j, (px, py, c))
            recv = _remote(srcs[k].at[s], lands[k].at[s], send_sems, recv_sems, 3 * k + j, (px, py, c))
            out.append((send, recv))
    return out


def _exchange_copies(srcs, lands, send_sems, recv_sems):
    x, y, c = _mesh_pos()
    out = []
    for k in range(len(srcs)):
        cp = _remote(srcs[k].at[:, _half_rows(srcs[k].shape[1], 1 - c)], lands[k], send_sems, recv_sems, k, (x, y, 1 - c))
        out.append((cp, cp))
    return out


def _split_start(srcs, land_shapes, copies, n_sems, *, name):
    n = len(srcs)
    lands = [lax.empty(shape, s.dtype) for shape, s in zip(land_shapes, srcs)]

    def body(*refs):
        ins, zones = refs[:n], refs[n:2 * n]
        send_sems, recv_sems, token = refs[2 * n], refs[2 * n + 1], refs[-1]
        for send, _ in copies(ins, zones, send_sems, recv_sems):
            send.start()
        token[...] = jnp.zeros_like(token)

    hbm = lambda a: pltpu.HBM(a.shape, a.dtype)
    res = pl.pallas_call(
        body, name=name,
        in_specs=[HBM_ONLY] * (2 * n),
        out_specs=[SEM_SPEC, SEM_SPEC] + [HBM_ONLY] * (2 * n) + [pl.BlockSpec(memory_space=pltpu.VMEM)],
        out_shape=[pltpu.SemaphoreType.DMA((n_sems,)), pltpu.SemaphoreType.DMA((n_sems,))] + [hbm(a) for a in srcs + lands]
        + [jax.ShapeDtypeStruct((1, 1), F32)],
        input_output_aliases={i: 2 + i for i in range(2 * n)},
        compiler_params=SPLIT_PARAMS,
    )(*[pltpu.with_memory_space_constraint(a, pltpu.HBM) for a in srcs + lands])
    return (res[0], res[1], list(res[2:2 + n]), list(res[2 + n:2 + 2 * n])), res[-1]


def _scatter_start(ps, *, name):
    return _split_start(ps, [p.shape for p in ps], _scatter_copies, 3 * len(ps), name=name)


def _exchange_start(gs, *, name):
    return _split_start(gs, [(g.shape[0], g.shape[1] // 2, g.shape[2]) for g in gs], _exchange_copies, len(gs), name=name)


def _split_wait(started, copies, after, *, name):
    ng = len(started)
    sizes = [len(st[2]) for st in started]
    offs = [2 * sum(sizes[:i]) for i in range(ng + 1)]
    flat = [a for (_, _, ps, lands) in started for a in ps + lands]

    def body(*refs):
        bufs, sems = refs[:len(flat)], refs[len(flat):len(flat) + 2 * ng]
        for i, n in enumerate(sizes):
            srcs, zones = bufs[offs[i]:offs[i] + n], bufs[offs[i] + n:offs[i + 1]]
            for send, recv in copies(srcs, zones, sems[2 * i], sems[2 * i + 1]):
                send.wait_send()
                recv.wait_recv()

    res = pl.pallas_call(
        body, name=name,
        in_specs=[HBM_ONLY] * len(flat) + [SEM_SPEC] * (2 * ng) + [HBM_SPEC],
        out_specs=[HBM_ONLY] * len(flat),
        out_shape=[pltpu.HBM(a.shape, a.dtype) for a in flat],
        input_output_aliases={i: i for i in range(len(flat))},
        compiler_params=SPLIT_PARAMS,
    )(*flat, *[s for (ss, rs, _, _) in started for s in (ss, rs)], after)
    return [(list(res[offs[i]:offs[i] + n]), list(res[offs[i] + n:offs[i + 1]])) for i, n in enumerate(sizes)]


def _sibling_share(hs):
    n = len(hs)

    def body(*refs):
        ins, outs = refs[:n], refs[n:2 * n]
        send_sems, recv_sems = refs[2 * n:]
        x, y, c = _mesh_pos()
        copies = [_remote(ins[k], outs[k], send_sems, recv_sems, k, (x, y, 1 - c)) for k in range(n)]
        for cp in copies:
            cp.start()
        for cp in copies:
            cp.wait()

    return pl.pallas_call(
        body, name="grad_sibling_share",
        in_specs=[HBM_SPEC] * n, out_specs=[HBM_SPEC] * n,
        out_shape=[jax.ShapeDtypeStruct(h.shape, h.dtype) for h in hs],
        scratch_shapes=[pltpu.SemaphoreType.DMA((n,)), pltpu.SemaphoreType.DMA((n,))],
        compiler_params=COMM_PARAMS,
    )(*hs)


def _allreduce_small(part, by_chip):
    rows, C = part.shape
    rows2 = by_chip.shape[1]

    def body(p_ref, q_ref, o_ref, o2_ref, slots, slots2, send_sems, recv_sems):
        x, y, c = _mesh_pos()
        me = 4 * x + 2 * y + c
        slots[me] = p_ref[...]
        slots2[me] = q_ref[2 * x + y]
        copies = []
        for k in range(1, 8):
            kx, ky, kc = (k >> 2) & 1, (k >> 1) & 1, k & 1
            peer = (x ^ kx if kx else x, y ^ ky if ky else y, c ^ kc if kc else c)
            src = 4 * peer[0] + 2 * peer[1] + peer[2]
            pair = []
            for j, (mine, zone, lands) in enumerate(((p_ref, slots.at[me], slots.at[src]),
                                                     (q_ref.at[2 * peer[0] + peer[1]], slots2.at[me], slots2.at[src]))):
                cp = _remote(mine, zone, send_sems, recv_sems, 2 * (k - 1) + j, peer)
                cp.start()
                pair.append((cp, _remote(mine, lands, send_sems, recv_sems, 2 * (k - 1) + j, peer)))
            copies += pair
        for _, landing in copies:
            landing.wait_recv()
        for cp, _ in copies:
            cp.wait_send()
        total, total2 = slots[0], slots2[0]
        for d in range(1, 8):
            total, total2 = total + slots[d], total2 + slots2[d]
        o_ref[...] = total
        o2_ref[...] = total2

    vmem = pl.BlockSpec(memory_space=pltpu.VMEM)
    return pl.pallas_call(
        body, name="small_grad_allreduce",
        in_specs=[vmem, vmem], out_specs=[vmem, vmem],
        out_shape=[jax.ShapeDtypeStruct((rows, C), F32), jax.ShapeDtypeStruct((rows2, C), F32)],
        scratch_shapes=[pltpu.VMEM((8, rows, C), F32), pltpu.VMEM((8, rows2, C), F32),
                        pltpu.SemaphoreType.DMA((14,)), pltpu.SemaphoreType.DMA((14,))],
        compiler_params=pltpu.CompilerParams(has_side_effects=True, vmem_limit_bytes=VMEM_LIMIT_BYTES),
    )(part, by_chip)


def _pad_w_uq(w):
    lead = w.shape[:-1]
    w = w.reshape(lead + (MLA_HEADS, MLA_QK))
    w = jnp.concatenate([w, jnp.zeros(lead + (MLA_HEADS, MLA_PAD - MLA_QK), w.dtype)], axis=-1)
    return w.reshape(lead + (MLA_HEADS * MLA_PAD,))


def _unpad_w_uq(g):
    lead = g.shape[:-1]
    return g.reshape(lead + (MLA_HEADS, MLA_PAD))[..., :MLA_QK].reshape(lead + (MLA_HEADS * MLA_QK,))


def _t(a):
    return jnp.swapaxes(a, -1, -2)


def _shards_of_cols(w):
    A, NB = w.shape
    return w.reshape(A, N_CHIPS, NB // N_CHIPS).transpose(1, 0, 2)


BIG = ("w_in", "w_uq", "w_ukv", "w_out", "w_up", "w_down")
SMALL = ("attn_pre_norm", "forget_bias", "swa_sinks", "rel_bias", "q_latent_norm", "kv_latent_norm", "group_norm",
         "attn_post_norm", "ffn_pre_norm", "conv_b", "ffn_post_norm")
WEIGHTS = ("attn_pre_norm", "w_in", "forget_bias", "swa_sinks", "rel_bias", "q_latent_norm", "w_uq", "kv_latent_norm",
           "w_ukv", "group_norm", "w_out", "attn_post_norm", "ffn_pre_norm", "w_up", "conv_w", "conv_b", "w_down",
           "ffn_post_norm")


PACK_UNIT = 8 * LANES


def _pack_rows(shape):
    return -(-int(np.prod(shape)) // PACK_UNIT) * 8


def _pack(arrs, row_mult=8):
    parts = []
    for a in arrs:
        n = int(np.prod(a.shape))
        parts.append(jnp.pad(a.reshape(-1), (0, _pack_rows(a.shape) * LANES - n)).reshape(-1, LANES))
    rows = sum(p.shape[0] for p in parts)
    pad = -rows % row_mult
    if pad:
        parts.append(jnp.zeros((pad, LANES), parts[0].dtype))
    return jnp.concatenate(parts, axis=0)


def _unpack(packed, shapes):
    packed = packed.reshape(-1, LANES)
    out, off = [], 0
    for shp in shapes:
        r = _pack_rows(shp)
        out.append(packed[off:off + r].reshape(-1)[:int(np.prod(shp))].reshape(shp))
        off += r
    return out


LAYER_KEYS = ("w_qkv_t", "w_lat_t", "w_in_t", "w_uq_p", "w_uq_t", "w_ukv", "w_ukv_t", "w_out", "w_up", "w_down", "conv_w")


MIX_WEIGHTS = ("w_in", "w_uq", "w_ukv", "w_out")
FFN_WEIGHTS = ("w_up", "w_down", "conv_w")


def _layer_weights(gathered):
    cols = lambda g: g.transpose(1, 0, 2).reshape(g.shape[1], N_CHIPS * g.shape[2])
    out = {}
    if "w_in" in gathered:
        w_in_t = _t(gathered["w_in"]).reshape(IN_COLS, D_MODEL)
        w_in_t = jnp.pad(w_in_t, ((0, IN_ROWS - IN_COLS), (0, 0)))
        w_uq_p = _pad_w_uq(cols(gathered["w_uq"]))
        w_ukv = cols(gathered["w_ukv"])
        out.update(w_qkv_t=w_in_t[:QKV_ROWS], w_lat_t=w_in_t[QKV_ROWS:], w_in_t=w_in_t, w_uq_p=w_uq_p, w_uq_t=_t(w_uq_p),
                   w_ukv=w_ukv, w_ukv_t=_t(w_ukv), w_out=gathered["w_out"].reshape(D_MODEL, D_MODEL))
    if "w_up" in gathered:
        out.update(w_up=gathered["w_up"], w_down=gathered["w_down"].reshape(D_FF, D_MODEL), conv_w=cols(gathered["conv_w"]))
    return out


def _local_step(x, target, W, layer_weights, layer_done):
    W = dict(W, **{key: [None] * DEPTH for key in LAYER_KEYS})
    S = x.shape[0]
    tq_tabs, tm_tabs = _rope_tables(S)
    onehot_t = _rel_onehot_t()
    bias_t = _bias_table(W["rel_bias"].T, onehot_t).reshape(SWA_KV_HEADS, SWA_GROUP, 2 * WINDOW, WINDOW)
    bias_t = bias_t.transpose(0, 2, 1, 3).reshape(SWA_KV_HEADS, 2 * WINDOW, GW)
    row = lambda a: a.reshape(1, -1)
    col = lambda a: a.reshape(-1, 1)
    fox_rows = (FOX_ROW0, FOX_ROW0 + FOX_HEADS * HEAD_DIM, FOX_ROW0 + 2 * FOX_HEADS * HEAD_DIM, SWA_Q_HEADS)
    fox = dict(rows=fox_rows, H=FOX_HEADS, Dk=HEAD_DIM, Dv=HEAD_DIM, scale=HEAD_DIM ** -0.5)
    mla = dict(rows=(0, 0, 0, SWA_Q_HEADS + FOX_HEADS), H=MLA_HEADS, Dk=MLA_PAD, Dv=HEAD_DIM, scale=MLA_SCALE, q_scaled=True)

    saved = []
    h = _rms_fwd(x, row(W["attn_pre_norm"][0]), name="rms_in")
    for l in range(DEPTH):
        sv = {"x0": x, "h1": h}
        for key, val in layer_weights(l, h, False).items():
            W[key][l] = val
        qkv = _matmul(W["w_qkv_t"][l], h, tb=True, out_dtype=BF16, name="proj_qkv")
        lat = _matmul(W["w_lat_t"][l], h, tb=True, name="proj_lat")
        oa, lse_a = _swa_fwd(qkv, bias_t, W["swa_sinks"][l], name="swa_fwd")
        fb_col = jnp.pad(col(W["forget_bias"][l]), ((0, GATE_ROWS - FOX_HEADS), (0, 0)))
        f4 = _gate_fwd(lat, fb_col, name="fox_gate_fwd")[:FOX_HEADS]
        f2 = f4 * LOG2E
        f_row, f_col = f2[:, None, :], f2.T
        of, lse_f = _attn_fwd(qkv, qkv, qkv, f_row=f_row, f_col=f_col, name="fox_fwd", **fox)
        nq, nkv, qm, km, vm = _mla_prep_fwd(lat, col(W["q_latent_norm"][l]), col(W["kv_latent_norm"][l]), W["w_uq_t"][l],
                                            W["w_ukv_t"][l], tq_tabs, tm_tabs, name="mla_prep_fwd")
        oc, lse_c = _attn_fwd(qm, km, vm, name="mla_fwd", **mla)
        mixed = _group_norm_fwd(oa, of, oc, col(W["group_norm"][l]), name="group_norm_fwd")
        y, x1, h2 = _matmul(mixed, W["w_out"][l], ta=True, name="proj_out",
                            resid_rms=(x, row(W["attn_post_norm"][l]), row(W["ffn_pre_norm"][l])))
        for key, val in layer_weights(l, h2, True).items():
            W[key][l] = val
        a = _matmul(h2, W["w_up"][l], b_shards=True, out_dtype=BF16, name="ffn_up")
        u, z = _conv_geglu_fwd(a, W["conv_w"][l], row(W["conv_b"][l]), name="conv_geglu_fwd")
        g_next = row(W["attn_pre_norm"][l + 1]) if l + 1 < DEPTH else None
        y2, x2, *h_next = _matmul(z, W["w_down"][l], name="ffn_down", resid_rms=(x1, row(W["ffn_post_norm"][l]), g_next))
        h_next = h_next[0] if h_next else None
        sv.update(qkv=qkv, lat=lat, oa=oa, lse_a=lse_a, fb_col=fb_col, f_row=f_row, f_col=f_col, of=of, lse_f=lse_f,
                  nq=nq, nkv=nkv, qm=qm, km=km, vm=vm, oc=oc, lse_c=lse_c, mixed=mixed, y=y, x1=x1, h2=h2, a=a, u=u, z=z, y2=y2)
        saved.append(sv)
        x, h = x2, h_next

    loss, dx = _loss_head(x, target)

    G = {k: [None] * DEPTH for k in WEIGHTS if k != "rel_bias" and k not in BIG}
    dbias_layers = [None] * DEPTH
    for l in reversed(range(DEPTH)):
        sv = saved[l]
        gb = {}
        if l == DEPTH - 1:
            dy2, dg = _rms_bwd(sv["y2"], row(W["ffn_post_norm"][l]), dx, out_dtype=BF16, name="ffn_post_bwd")
            G["ffn_post_norm"][l] = dg[0]
        dz = _matmul(dy2, W["w_down"][l], tb=True, name="ffn_down_dx")
        gb["w_down"] = _matmul(sv["z"], dy2, ta=True, name="ffn_down_dw").reshape(N_CHIPS, D_FF // N_CHIPS, D_MODEL)
        da, dcw, dcb = _conv_geglu_bwd(sv["a"], sv["u"], W["conv_w"][l], dz, name="conv_geglu_bwd")
        G["conv_w"][l] = dcw.transpose(1, 0, 2).reshape(3, 2 * D_FF)
        G["conv_b"][l] = dcb.reshape(2 * D_FF)
        gb["w_up"] = _matmul(sv["h2"], da, ta=True, out_shards=True, b_halves=True, name="ffn_up_dw")
        token = layer_done(l, gb)
        gb = {}
        dx1, dg, dy, dg_post = _matmul(
            da, W["w_up"][l], tb=True, b_shards=True, a_halves=True, name="ffn_up_dx",
            norm_bwd=(sv["x1"], row(W["ffn_pre_norm"][l]) + token, dx, (sv["y"], row(W["attn_post_norm"][l]))))
        G["ffn_pre_norm"][l] = dg[0]
        G["attn_post_norm"][l] = dg_post[0]
        dmixed = _matmul(W["w_out"][l], dy, tb=True, name="proj_out_dx")
        gb["w_out"] = _matmul(sv["mixed"], dy, name="proj_out_dw").reshape(N_CHIPS, D_MODEL // N_CHIPS, D_MODEL)
        doa, dof, doc, dg, delta = _group_norm_bwd(sv["oa"], sv["of"], sv["oc"], col(W["group_norm"][l]), dmixed,
                                                   name="group_norm_bwd")
        G["group_norm"][l] = dg[:, 0]
        dqa, dkva, dbias_l, dsink = _swa_bwd(sv["qkv"], bias_t, W["swa_sinks"][l], doa, sv["lse_a"],
                                             delta.reshape(-1, S), name="swa_bwd")
        dbias_layers[l] = (dbias_l.reshape(SWA_KV_HEADS, 2 * WINDOW, SWA_GROUP, WINDOW).transpose(0, 2, 1, 3)
                           .reshape(SWA_Q_HEADS, -1))
        G["swa_sinks"][l] = dsink[:, 0]
        dqf, dkf, dvf, dfk = _attn_bwd(sv["qkv"], sv["qkv"], sv["qkv"], do=dof, lse=sv["lse_f"], delta=delta,
                                       f_row=sv["f_row"], f_col=sv["f_col"], name="fox_bwd", **fox)
        dF = jnp.pad(dfk.T, ((0, GATE_ROWS - FOX_HEADS), (0, 0)))
        dflog, dfb = _gate_bwd(sv["lat"], sv["fb_col"], dF, name="fox_gate_bwd")
        G["forget_bias"][l] = dfb[:FOX_HEADS, 0]
        dqm, dkm, dvm = _attn_bwd(sv["qm"], sv["km"], sv["vm"], do=doc, lse=sv["lse_c"], delta=delta, name="mla_bwd", **mla)
        dlat, dwq_t, dwkv_t, dgq, dgkv = _mla_prep_bwd(
            sv["lat"], sv["nq"], sv["nkv"], col(W["q_latent_norm"][l]), col(W["kv_latent_norm"][l]), W["w_uq_p"][l],
            W["w_ukv"][l], tq_tabs, tm_tabs, dqm, dkm, dvm, dflog, name="mla_prep_bwd")
        gb["w_uq"], gb["w_ukv"] = _shards_of_cols(_unpad_w_uq(dwq_t.T)), _shards_of_cols(dwkv_t.T)
        G["q_latent_norm"][l], G["kv_latent_norm"][l] = dgq[:, 0], dgkv[:, 0]
        dproj = _dproj_cast(dqa, dkva, dqf, dkf, dvf, dlat, name="dproj_cast")
        dw_in_t = _matmul(dproj, sv["h1"], name="proj_in_dw")
        gb["w_in"] = _t(dw_in_t[:IN_COLS].reshape(N_CHIPS, IN_COLS // N_CHIPS, D_MODEL))
        token = layer_done(l, gb)
        below = (saved[l - 1]["y2"], row(W["ffn_post_norm"][l - 1])) if l > 0 else None
        res = _matmul(dproj, W["w_in_t"][l], ta=True, name="proj_in_dx",
                      norm_bwd=(sv["x0"], row(W["attn_pre_norm"][l]) + token, dx1, below))
        dx, G["attn_pre_norm"][l] = res[0], res[1][0]
        if l > 0:
            dy2, G["ffn_post_norm"][l - 1] = res[2], res[3][0]

    grads = {k: jnp.stack(v) for k, v in G.items()}
    grads["rel_bias"] = _bias_table_bwd(jnp.stack(dbias_layers), onehot_t).T
    return loss, dx, grads


def kernel(x, attn_pre_norm, w_in, forget_bias, swa_sinks, rel_bias, q_latent_norm, w_uq, kv_latent_norm, w_ukv, group_norm, w_out, attn_post_norm, ffn_pre_norm, w_up, conv_w, conv_b, w_down, ffn_post_norm, loss_target, m_attn_pre_norm, m_w_in, m_forget_bias, m_swa_sinks, m_rel_bias, m_q_latent_norm, m_w_uq, m_kv_latent_norm, m_w_ukv, m_group_norm, m_w_out, m_attn_post_norm, m_ffn_pre_norm, m_w_up, m_conv_w, m_conv_b, m_w_down, m_ffn_post_norm, v_attn_pre_norm, v_w_in, v_forget_bias, v_swa_sinks, v_rel_bias, v_q_latent_norm, v_w_uq, v_kv_latent_norm, v_w_ukv, v_group_norm, v_w_out, v_attn_post_norm, v_ffn_pre_norm, v_w_up, v_conv_w, v_conv_b, v_w_down, v_ffn_post_norm):
    args = dict(locals())
    w = {k: args[k] for k in WEIGHTS}
    m = {k: args["m_" + k] for k in WEIGHTS}
    v = {k: args["v_" + k] for k in WEIGHTS}

    block = lambda l, keys: [w[k][l] if k == "conv_w" else w[k][l].astype(BF16) for k in keys]
    units = [(0, MIX_WEIGHTS), (0, FFN_WEIGHTS)] + [(l, MIX_WEIGHTS + FFN_WEIGHTS) for l in range(1, DEPTH)]
    gather_state, token = _gather_start([block(l, keys) for l, keys in units])
    W = {k: w[k] for k in SMALL}
    W["attn_pre_norm"] = W["attn_pre_norm"] + token

    def layer_weights(l, after, for_ffn):
        if for_ffn and l > 0:
            return {}
        keys = FFN_WEIGHTS if for_ffn else (MIX_WEIGHTS if l == 0 else MIX_WEIGHTS + FFN_WEIGHTS)
        tag = f"{l}_{keys[0]}"
        srcs, lands = _gather_wait(gather_state[units.index((l, keys))], after, name="weight_gather_wait_" + tag)
        lands = _gather_forward(lands, name="weight_gather_forward_" + tag)
        lands = _place_own(lands, srcs, name="place_own_shards")
        return _layer_weights(dict(zip(keys, lands)))

    started, groups, pending = [], [], []

    def to_chips(l, keys, gs, recv, tag):
        pair = [_pair_sum(gk, rk, name="grad_pair_sum") for gk, rk in zip(gs, recv)]
        state, token = _scatter_start(pair, name="grad_scatter_start_" + tag)
        started.append(state)
        groups.append((l, keys))
        return token

    def finish_pending(after):
        l, keys, tag, state = pending.pop()
        gs, recv = _split_wait([state], _exchange_copies, after, name="grad_exchange_wait_" + tag)[0]
        return to_chips(l, keys, gs, recv, tag)

    def layer_done(l, gb):
        keys = [k for k in BIG if k in gb]
        gs = [gb[k] for k in keys]
        tag = f"{l}_{keys[0]}"
        token = finish_pending(gs[0]) if pending else 0.0
        if l == 0:
            return token + to_chips(l, keys, gs, _sibling_exchange(gs, name="grad_sibling_exchange_" + tag), tag)
        state, started_token = _exchange_start(gs, name="grad_exchange_start_" + tag)
        pending.append((l, keys, tag, state))
        return token + started_token

    loss_part, dx, g = _local_step(x[0], loss_target[0], W, layer_weights, layer_done)
    loss = lax.psum(loss_part, ("x", "y", "c"))

    reduced = {}
    for (l, keys), (pair, zones) in zip(groups, _split_wait(started, _scatter_copies, dx, name="grad_scatter_wait")):
        for k, p, z in zip(keys, pair, zones):
            reduced[k, l] = _chip_sum(z, p, name="grad_chip_sum")
    mine = [jnp.stack([reduced[k, l] for l in range(DEPTH)]) for k in BIG]
    other = _sibling_share(mine)
    out_g, out_d, out_m, out_v = {}, {}, {}, {}
    for k, g_mine, g_other in zip(BIG, mine, other):
        out_g[k], out_d[k], out_m[k], out_v[k] = _adamw_halves(w[k], g_mine, g_other, m[k], v[k], name="adamw_" + k)

    small_shapes = [w[k].shape for k in SMALL]
    taps_by_chip = g["conv_w"].reshape(DEPTH, 3, N_CHIPS, FF_SHARD).transpose(2, 0, 1, 3)
    reduced, taps = _allreduce_small(_pack([g[k] for k in SMALL]), jnp.stack([_pack([t]) for t in taps_by_chip]))
    g_small = _unpack(reduced, small_shapes) + _unpack(taps, [w["conv_w"].shape])
    names = SMALL + ("conv_w",)
    shapes = small_shapes + [w["conv_w"].shape]
    packed = lambda arrs: _pack(arrs, ROW_TILE)[None]
    d_s, m_s, v_s = _adamw(packed([w[k] for k in names]), packed(g_small), packed([m[k] for k in names]),
                           packed([v[k] for k in names]), name="adamw_small")
    out_g.update(zip(names, g_small))
    out_d.update(zip(names, _unpack(d_s, shapes)))
    out_m.update(zip(names, _unpack(m_s, shapes)))
    out_v.update(zip(names, _unpack(v_s, shapes)))

    return (loss, dx[None], *[out_g[k] for k in WEIGHTS], *[out_d[k] for k in WEIGHTS],
            *[out_m[k] for k in WEIGHTS], *[out_v[k] for k in WEIGHTS])
```
